```python
import jax, jax.numpy as jnp
from jax import lax
import numpy as np

D_MODEL = 1024
BATCH = 8
SEQ = 4096
DEPTH = 2

GRID_W = 64
CTX_LEN = 256
EPS = 1e-6
NEG_INF = -1e30
N_MOD = 6
N_EVEN = (DEPTH + 1) // 2
N_ODD = DEPTH // 2
POOL_WINDOWS = (2, 4, 8, 16)
N_POOL_GROUPS = len(POOL_WINDOWS)
POOL_GROUP_DIM = D_MODEL // 8
POOL_WIDTH = N_POOL_GROUPS * POOL_GROUP_DIM
HEAD_DIM = 64
N_Q_HEADS = D_MODEL // 128
N_KV_HEADS = N_Q_HEADS // 4
GQA_GROUP = N_Q_HEADS // N_KV_HEADS
ATTN_WIDTH = N_Q_HEADS * HEAD_DIM
KV_WIDTH = N_KV_HEADS * HEAD_DIM
WINDOW = 128
ATTN_BLOCK = 128
ROPE_BASE = 10000.0
ROPE_FREQS = HEAD_DIM // 4
Q_END = POOL_WIDTH + ATTN_WIDTH
IN_EVEN = Q_END + 2 * KV_WIDTH
MIX_EVEN = POOL_WIDTH + ATTN_WIDTH
CHUNK = 128
N_SGU_GROUPS = 8
SGU_WIDTH = D_MODEL
SGU_GROUP_DIM = SGU_WIDTH // N_SGU_GROUPS
D_FF = 128 * ((8 * D_MODEL // 3 + 127) // 128)
CONV_W = 3

kernel_name = "hybrid_pool_swa_sgu_convffn_ctxprefix"

f32 = jnp.float32


def rmsnorm(x, g):
    xf = x.astype(f32)
    y = xf * lax.rsqrt(jnp.mean(xf * xf, axis=-1, keepdims=True) + EPS)
    return (y * g.astype(f32)).astype(x.dtype)


def layernorm(x, g, b):
    xf = x.astype(f32)
    mu = jnp.mean(xf, axis=-1, keepdims=True)
    var = jnp.mean(jnp.square(xf - mu), axis=-1, keepdims=True)
    y = (xf - mu) * lax.rsqrt(var + EPS)
    return (y * g.astype(f32) + b.astype(f32)).astype(x.dtype)


def adaln(cvec, w, b):
    m = jax.nn.silu(cvec) @ w + b
    return jnp.split(m[:, None, :], N_MOD, axis=-1)


def pre(x, g, shift, scale):
    return rmsnorm(x, g) * (1 + scale) + shift


def post(x, y, g, gate):
    return x + gate * rmsnorm(y, g)


def axial_rope_tables(L):
    rows = L // GRID_W
    row = jnp.repeat(jnp.arange(rows), GRID_W).astype(f32)
    col = jnp.tile(jnp.arange(GRID_W), rows).astype(f32)
    inv = ROPE_BASE ** (-jnp.arange(ROPE_FREQS, dtype=f32) / ROPE_FREQS)
    ang = jnp.concatenate([row[:, None] * inv, col[:, None] * inv], axis=-1)
    return jnp.cos(ang), jnp.sin(ang)


def apply_rope(x, cos, sin):
    B, L, H, _ = x.shape
    xr = x.astype(f32).reshape(B, L, H, 2, 2, ROPE_FREQS)
    x1, x2 = xr[..., 0, :], xr[..., 1, :]
    c = cos.reshape(L, 2, ROPE_FREQS)[None, :, None]
    s = sin.reshape(L, 2, ROPE_FREQS)[None, :, None]
    out = jnp.stack([x1 * c - x2 * s, x2 * c + x1 * s], axis=-2)
    return out.reshape(x.shape).astype(x.dtype)


def pool_mixer(u, w_pool, pool_scale):
    B, L, _ = u.shape
    ug = u.reshape(B, L, N_POOL_GROUPS, POOL_GROUP_DIM)
    cs = jnp.pad(jnp.cumsum(ug.astype(f32), axis=1), ((0, 0), (1, 0), (0, 0), (0, 0)))
    t = jnp.arange(L)[:, None]
    half = jnp.asarray(np.array([w // 2 for w in POOL_WINDOWS], dtype=np.int32))[None, :]
    start = jnp.clip(t - half, 0, L)
    end = jnp.clip(t + half, 0, L)
    g_idx = jnp.arange(N_POOL_GROUPS)[None, :]
    win_sum = cs[:, end, g_idx] - cs[:, start, g_idx]
    mean = win_sum / (end - start).astype(f32)[None, :, :, None]
    pooled = (mean - ug.astype(f32)).astype(u.dtype)
    y = jnp.einsum('blgc,gcd->blgd', pooled, w_pool)
    return y.reshape(B, L, POOL_WIDTH) * pool_scale


def band_mask(nb, L):
    i = jnp.arange(ATTN_BLOCK)[None, :, None]
    j = jnp.arange(3 * ATTN_BLOCK)[None, None, :]
    n = jnp.arange(nb)[:, None, None]
    q_pos = n * ATTN_BLOCK + i
    k_pos = (n - 1) * ATTN_BLOCK + j
    return (jnp.abs(k_pos - q_pos) <= WINDOW) & (k_pos >= 0) & (k_pos < L)


def windowed_gqa(q, k, v, kc, vc, sink):
    B, L = q.shape[:2]
    nb = L // ATTN_BLOCK
    scale = HEAD_DIM ** -0.5
    qb = q.reshape(B, nb, ATTN_BLOCK, N_KV_HEADS, GQA_GROUP, HEAD_DIM)

    def band(t):
        tb = jnp.pad(t.reshape(B, nb, ATTN_BLOCK, N_KV_HEADS, HEAD_DIM), ((0, 0), (1, 1), (0, 0), (0, 0), (0, 0)))
        return jnp.concatenate([tb[:, :-2], tb[:, 1:-1], tb[:, 2:]], axis=2)

    kb, vb = band(k), band(v)
    s_loc = jnp.einsum('bnqkgd,bnskd->bnkgqs', qb, kb, preferred_element_type=f32) * scale
    s_loc = jnp.where(band_mask(nb, L)[None, :, None, None], s_loc, NEG_INF)
    s_ctx = jnp.einsum('bnqkgd,bskd->bnkgqs', qb, kc, preferred_element_type=f32) * scale
    s_sink = jnp.broadcast_to(sink.astype(f32).reshape(N_KV_HEADS, GQA_GROUP, 1, 1), s_loc.shape[:-1] + (1,))
    p = jax.nn.softmax(jnp.concatenate([s_loc, s_ctx, s_sink], axis=-1), axis=-1)
    n_loc = 3 * ATTN_BLOCK
    p_loc = p[..., :n_loc].astype(v.dtype)
    p_ctx = p[..., n_loc:-1].astype(v.dtype)
    o = jnp.einsum('bnkgqs,bnskd->bnqkgd', p_loc, vb) + jnp.einsum('bnkgqs,bskd->bnqkgd', p_ctx, vc)
    return o.reshape(B, L, ATTN_WIDTH)


def context_gqa(qc, kc, vc, sink):
    B, C = qc.shape[:2]
    qg = qc.reshape(B, C, N_KV_HEADS, GQA_GROUP, HEAD_DIM)
    s = jnp.einsum('bqkgd,bskd->bkgqs', qg, kc, preferred_element_type=f32) * HEAD_DIM ** -0.5
    s_sink = jnp.broadcast_to(sink.astype(f32).reshape(N_KV_HEADS, GQA_GROUP, 1, 1), s.shape[:-1] + (1,))
    p = jax.nn.softmax(jnp.concatenate([s, s_sink], axis=-1), axis=-1)
    o = jnp.einsum('bkgqs,bskd->bqkgd', p[..., :-1].astype(vc.dtype), vc)
    return o.reshape(B, C, ATTN_WIDTH)


def even_mixer(h, hc, w_in, w_pool, pool_scale, sink, w_out, cos, sin, ctx_out):
    B, L, _ = h.shape
    C = hc.shape[1]
    z = h @ w_in
    u = z[..., :POOL_WIDTH]
    q = apply_rope(z[..., POOL_WIDTH:Q_END].reshape(B, L, N_Q_HEADS, HEAD_DIM), cos, sin)
    k = apply_rope(z[..., Q_END:Q_END + KV_WIDTH].reshape(B, L, N_KV_HEADS, HEAD_DIM), cos, sin)
    v = z[..., Q_END + KV_WIDTH:].reshape(B, L, N_KV_HEADS, HEAD_DIM)
    zc = hc @ w_in[:, Q_END:]
    kc = zc[..., :KV_WIDTH].reshape(B, C, N_KV_HEADS, HEAD_DIM)
    vc = zc[..., KV_WIDTH:].reshape(B, C, N_KV_HEADS, HEAD_DIM)
    y = jnp.concatenate([pool_mixer(u, w_pool, pool_scale), windowed_gqa(q, k, v, kc, vc, sink)], axis=-1) @ w_out
    yc = None
    if ctx_out:
        zq = hc @ w_in[:, :Q_END]
        qc = zq[..., POOL_WIDTH:].reshape(B, C, N_Q_HEADS, HEAD_DIM)
        yc = jnp.concatenate([pool_mixer(zq[..., :POOL_WIDTH], w_pool, pool_scale),
                              context_gqa(qc, kc, vc, sink)], axis=-1) @ w_out
    return y, yc


def sgu_mixer(h, w_in, ln_g, ln_b, w_s, b_s, w_out):
    B, L, _ = h.shape
    z = jax.nn.gelu(h @ w_in)
    u, v = jnp.split(z, 2, axis=-1)
    v = layernorm(v, ln_g, ln_b)
    vb = v.reshape(B, L // CHUNK, CHUNK, N_SGU_GROUPS, SGU_GROUP_DIM)
    s = jnp.einsum('gpq,bnqgc->bnpgc', w_s, vb) + b_s.T[:, :, None]
    return (u * s.reshape(B, L, SGU_WIDTH)) @ w_out


def conv_ffn(h, w_up, conv_w, conv_b, w_down):
    hu = h @ w_up
    hcv = lax.conv_general_dilated(hu, conv_w[:, None, :], window_strides=(1,), padding='SAME',
                                   dimension_numbers=('NWC', 'WIO', 'NWC'),
                                   feature_group_count=hu.shape[-1]) + conv_b
    gate, up = jnp.split(hcv, 2, axis=-1)
    return (jax.nn.silu(gate) * up) @ w_down


def _fwd_setup_inputs(seed: int = 0) -> dict:
    key = jax.random.key(seed)
    ks = jax.random.split(key, 32)
    nrm = jax.random.normal
    D = D_MODEL
    return {
        "x": nrm(ks[0], (BATCH, SEQ, D), f32),
        "c": nrm(ks[1], (BATCH, D), f32),
        "ctx": nrm(ks[2], (BATCH, CTX_LEN, D), f32),
        "c_ctx": nrm(ks[3], (D,), f32),
        "w_ada": nrm(ks[4], (DEPTH, D, N_MOD * D), f32) * (0.5 * D ** -0.5),
        "b_ada": nrm(ks[5], (DEPTH, N_MOD * D), f32) * 0.01,
        "g_mix_pre": 1.0 + 0.1 * nrm(ks[6], (DEPTH, D), f32),
        "g_mix_post": 1.0 + 0.1 * nrm(ks[7], (DEPTH, D), f32),
        "g_ffn_pre": 1.0 + 0.1 * nrm(ks[8], (DEPTH, D), f32),
        "g_ffn_post": 1.0 + 0.1 * nrm(ks[9], (DEPTH, D), f32),
        "w_in_even": nrm(ks[10], (N_EVEN, D, IN_EVEN), f32) * D ** -0.5,
        "w_pool": nrm(ks[11], (N_EVEN, N_POOL_GROUPS, POOL_GROUP_DIM, POOL_GROUP_DIM), f32) * POOL_GROUP_DIM ** -0.5,
        "pool_scale": 1.0 + 0.1 * nrm(ks[12], (N_EVEN, POOL_WIDTH), f32),
        "attn_sink": 0.5 * nrm(ks[13], (N_EVEN, N_Q_HEADS), f32),
        "w_out_even": nrm(ks[14], (N_EVEN, MIX_EVEN, D), f32) * MIX_EVEN ** -0.5,
        "w_in_odd": nrm(ks[15], (N_ODD, D, 2 * SGU_WIDTH), f32) * D ** -0.5,
        "sgu_ln_g": 1.0 + 0.1 * nrm(ks[16], (N_ODD, SGU_WIDTH), f32),
        "sgu_ln_b": 0.01 * nrm(ks[17], (N_ODD, SGU_WIDTH), f32),
        "sgu_w": nrm(ks[18], (N_ODD, N_SGU_GROUPS, CHUNK, CHUNK), f32) * CHUNK ** -0.5,
        "sgu_b": 1.0 + 0.1 * nrm(ks[19], (N_ODD, N_SGU_GROUPS, CHUNK), f32),
        "w_out_odd": nrm(ks[20], (N_ODD, SGU_WIDTH, D), f32) * SGU_WIDTH ** -0.5,
        "w_ffn_up": nrm(ks[21], (DEPTH, D, 2 * D_FF), f32) * D ** -0.5,
        "ffn_conv_w": nrm(ks[22], (DEPTH, CONV_W, 2 * D_FF), f32) * CONV_W ** -0.5,
        "ffn_conv_b": 0.01 * nrm(ks[23], (DEPTH, 2 * D_FF), f32),
        "w_ffn_down": nrm(ks[24], (DEPTH, D_FF, D), f32) * D_FF ** -0.5,
    }


def _fwd_reference(x, c, ctx, c_ctx, w_ada, b_ada, g_mix_pre, g_mix_post, g_ffn_pre, g_ffn_post,
              w_in_even, w_pool, pool_scale, attn_sink, w_out_even,
              w_in_odd, sgu_ln_g, sgu_ln_b, sgu_w, sgu_b, w_out_odd,
              w_ffn_up, ffn_conv_w, ffn_conv_b, w_ffn_down):
    L = x.shape[1]
    cos, sin = axial_rope_tables(L)
    xc = ctx
    for i in range(DEPTH):
        advance_ctx = any(j % 2 == 0 for j in range(i + 1, DEPTH))
        sh_m, sc_m, gt_m, sh_f, sc_f, gt_f = adaln(c, w_ada[i], b_ada[i])
        if i % 2 == 0 or advance_ctx:
            csh_m, csc_m, cgt_m, csh_f, csc_f, cgt_f = adaln(c_ctx[None, :], w_ada[i], b_ada[i])
        h = pre(x, g_mix_pre[i], sh_m, sc_m)
        if i % 2 == 0:
            e = i // 2
            hc = pre(xc, g_mix_pre[i], csh_m, csc_m)
            y, yc = even_mixer(h, hc, w_in_even[e], w_pool[e], pool_scale[e], attn_sink[e], w_out_even[e],
                               cos, sin, advance_ctx)
        else:
            o = i // 2
            y = sgu_mixer(h, w_in_odd[o], sgu_ln_g[o], sgu_ln_b[o], sgu_w[o], sgu_b[o], w_out_odd[o])
            yc = None
            if advance_ctx:
                yc = sgu_mixer(pre(xc, g_mix_pre[i], csh_m, csc_m), w_in_odd[o], sgu_ln_g[o], sgu_ln_b[o],
                               sgu_w[o], sgu_b[o], w_out_odd[o])
        x = post(x, y, g_mix_post[i], gt_m)
        x = post(x, conv_ffn(pre(x, g_ffn_pre[i], sh_f, sc_f), w_ffn_up[i], ffn_conv_w[i], ffn_conv_b[i],
                             w_ffn_down[i]), g_ffn_post[i], gt_f)
        if advance_ctx:
            xc = post(xc, yc, g_mix_post[i], cgt_m)
            xc = post(xc, conv_ffn(pre(xc, g_ffn_pre[i], csh_f, csc_f), w_ffn_up[i], ffn_conv_w[i],
                                   ffn_conv_b[i], w_ffn_down[i]), g_ffn_post[i], cgt_f)
    return x


import jax as _jax
import jax.numpy as _jnp

TWIN_FORMAT = 'train_step'
FWD_PARAMS = ['x', 'c', 'ctx', 'c_ctx', 'w_ada', 'b_ada', 'g_mix_pre', 'g_mix_post', 'g_ffn_pre', 'g_ffn_post', 'w_in_even', 'w_pool', 'pool_scale', 'attn_sink', 'w_out_even', 'w_in_odd', 'sgu_ln_g', 'sgu_ln_b', 'sgu_w', 'sgu_b', 'w_out_odd', 'w_ffn_up', 'ffn_conv_w', 'ffn_conv_b', 'w_ffn_down']
TWIN_WEIGHTS = ['c_ctx', 'w_ada', 'b_ada', 'g_mix_pre', 'g_mix_post', 'g_ffn_pre', 'g_ffn_post', 'w_in_even', 'w_pool', 'pool_scale', 'attn_sink', 'w_out_even', 'w_in_odd', 'sgu_ln_g', 'sgu_ln_b', 'sgu_w', 'sgu_b', 'w_out_odd', 'w_ffn_up', 'ffn_conv_w', 'ffn_conv_b', 'w_ffn_down']
TWIN_DIFF_INPUT = 'x'
TWIN_INPUTS = ['x', 'c', 'ctx', 'c_ctx', 'w_ada', 'b_ada', 'g_mix_pre', 'g_mix_post', 'g_ffn_pre', 'g_ffn_post', 'w_in_even', 'w_pool', 'pool_scale', 'attn_sink', 'w_out_even', 'w_in_odd', 'sgu_ln_g', 'sgu_ln_b', 'sgu_w', 'sgu_b', 'w_out_odd', 'w_ffn_up', 'ffn_conv_w', 'ffn_conv_b', 'w_ffn_down', 'loss_target', 'm_c_ctx', 'm_w_ada', 'm_b_ada', 'm_g_mix_pre', 'm_g_mix_post', 'm_g_ffn_pre', 'm_g_ffn_post', 'm_w_in_even', 'm_w_pool', 'm_pool_scale', 'm_attn_sink', 'm_w_out_even', 'm_w_in_odd', 'm_sgu_ln_g', 'm_sgu_ln_b', 'm_sgu_w', 'm_sgu_b', 'm_w_out_odd', 'm_w_ffn_up', 'm_ffn_conv_w', 'm_ffn_conv_b', 'm_w_ffn_down', 'v_c_ctx', 'v_w_ada', 'v_b_ada', 'v_g_mix_pre', 'v_g_mix_post', 'v_g_ffn_pre', 'v_g_ffn_post', 'v_w_in_even', 'v_w_pool', 'v_pool_scale', 'v_attn_sink', 'v_w_out_even', 'v_w_in_odd', 'v_sgu_ln_g', 'v_sgu_ln_b', 'v_sgu_w', 'v_sgu_b', 'v_w_out_odd', 'v_w_ffn_up', 'v_ffn_conv_w', 'v_ffn_conv_b', 'v_w_ffn_down']
TWIN_OUTPUTS = ['loss', 'grad_x', 'grad_c_ctx', 'grad_w_ada', 'grad_b_ada', 'grad_g_mix_pre', 'grad_g_mix_post', 'grad_g_ffn_pre', 'grad_g_ffn_post', 'grad_w_in_even', 'grad_w_pool', 'grad_pool_scale', 'grad_attn_sink', 'grad_w_out_even', 'grad_w_in_odd', 'grad_sgu_ln_g', 'grad_sgu_ln_b', 'grad_sgu_w', 'grad_sgu_b', 'grad_w_out_odd', 'grad_w_ffn_up', 'grad_ffn_conv_w', 'grad_ffn_conv_b', 'grad_w_ffn_down', 'delta_c_ctx', 'delta_w_ada', 'delta_b_ada', 'delta_g_mix_pre', 'delta_g_mix_post', 'delta_g_ffn_pre', 'delta_g_ffn_post', 'delta_w_in_even', 'delta_w_pool', 'delta_pool_scale', 'delta_attn_sink', 'delta_w_out_even', 'delta_w_in_odd', 'delta_sgu_ln_g', 'delta_sgu_ln_b', 'delta_sgu_w', 'delta_sgu_b', 'delta_w_out_odd', 'delta_w_ffn_up', 'delta_ffn_conv_w', 'delta_ffn_conv_b', 'delta_w_ffn_down', 'new_m_c_ctx', 'new_m_w_ada', 'new_m_b_ada', 'new_m_g_mix_pre', 'new_m_g_mix_post', 'new_m_g_ffn_pre', 'new_m_g_ffn_post', 'new_m_w_in_even', 'new_m_w_pool', 'new_m_pool_scale', 'new_m_attn_sink', 'new_m_w_out_even', 'new_m_w_in_odd', 'new_m_sgu_ln_g', 'new_m_sgu_ln_b', 'new_m_sgu_w', 'new_m_sgu_b', 'new_m_w_out_odd', 'new_m_w_ffn_up', 'new_m_ffn_conv_w', 'new_m_ffn_conv_b', 'new_m_w_ffn_down', 'new_v_c_ctx', 'new_v_w_ada', 'new_v_b_ada', 'new_v_g_mix_pre', 'new_v_g_mix_post', 'new_v_g_ffn_pre', 'new_v_g_ffn_post', 'new_v_w_in_even', 'new_v_w_pool', 'new_v_pool_scale', 'new_v_attn_sink', 'new_v_w_out_even', 'new_v_w_in_odd', 'new_v_sgu_ln_g', 'new_v_sgu_ln_b', 'new_v_sgu_w', 'new_v_sgu_b', 'new_v_w_out_odd', 'new_v_w_ffn_up', 'new_v_ffn_conv_w', 'new_v_ffn_conv_b', 'new_v_w_ffn_down']
TWIN_LEAF_KINDS = {'loss': 'loss', 'grad_x': 'grad_x', 'grad_c_ctx': 'grad_w', 'grad_w_ada': 'grad_w', 'grad_b_ada': 'grad_w', 'grad_g_mix_pre': 'grad_w', 'grad_g_mix_post': 'grad_w', 'grad_g_ffn_pre': 'grad_w', 'grad_g_ffn_post': 'grad_w', 'grad_w_in_even': 'grad_w', 'grad_w_pool': 'grad_w', 'grad_pool_scale': 'grad_w', 'grad_attn_sink': 'grad_w', 'grad_w_out_even': 'grad_w', 'grad_w_in_odd': 'grad_w', 'grad_sgu_ln_g': 'grad_w', 'grad_sgu_ln_b': 'grad_w', 'grad_sgu_w': 'grad_w', 'grad_sgu_b': 'grad_w', 'grad_w_out_odd': 'grad_w', 'grad_w_ffn_up': 'grad_w', 'grad_ffn_conv_w': 'grad_w', 'grad_ffn_conv_b': 'grad_w', 'grad_w_ffn_down': 'grad_w', 'delta_c_ctx': 'delta_w', 'delta_w_ada': 'delta_w', 'delta_b_ada': 'delta_w', 'delta_g_mix_pre': 'delta_w', 'delta_g_mix_post': 'delta_w', 'delta_g_ffn_pre': 'delta_w', 'delta_g_ffn_post': 'delta_w', 'delta_w_in_even': 'delta_w', 'delta_w_pool': 'delta_w', 'delta_pool_scale': 'delta_w', 'delta_attn_sink': 'delta_w', 'delta_w_out_even': 'delta_w', 'delta_w_in_odd': 'delta_w', 'delta_sgu_ln_g': 'delta_w', 'delta_sgu_ln_b': 'delta_w', 'delta_sgu_w': 'delta_w', 'delta_sgu_b': 'delta_w', 'delta_w_out_odd': 'delta_w', 'delta_w_ffn_up': 'delta_w', 'delta_ffn_conv_w': 'delta_w', 'delta_ffn_conv_b': 'delta_w', 'delta_w_ffn_down': 'delta_w', 'new_m_c_ctx': 'new_m', 'new_m_w_ada': 'new_m', 'new_m_b_ada': 'new_m', 'new_m_g_mix_pre': 'new_m', 'new_m_g_mix_post': 'new_m', 'new_m_g_ffn_pre': 'new_m', 'new_m_g_ffn_post': 'new_m', 'new_m_w_in_even': 'new_m', 'new_m_w_pool': 'new_m', 'new_m_pool_scale': 'new_m', 'new_m_attn_sink': 'new_m', 'new_m_w_out_even': 'new_m', 'new_m_w_in_odd': 'new_m', 'new_m_sgu_ln_g': 'new_m', 'new_m_sgu_ln_b': 'new_m', 'new_m_sgu_w': 'new_m', 'new_m_sgu_b': 'new_m', 'new_m_w_out_odd': 'new_m', 'new_m_w_ffn_up': 'new_m', 'new_m_ffn_conv_w': 'new_m', 'new_m_ffn_conv_b': 'new_m', 'new_m_w_ffn_down': 'new_m', 'new_v_c_ctx': 'new_v', 'new_v_w_ada': 'new_v', 'new_v_b_ada': 'new_v', 'new_v_g_mix_pre': 'new_v', 'new_v_g_mix_post': 'new_v', 'new_v_g_ffn_pre': 'new_v', 'new_v_g_ffn_post': 'new_v', 'new_v_w_in_even': 'new_v', 'new_v_w_pool': 'new_v', 'new_v_pool_scale': 'new_v', 'new_v_attn_sink': 'new_v', 'new_v_w_out_even': 'new_v', 'new_v_w_in_odd': 'new_v', 'new_v_sgu_ln_g': 'new_v', 'new_v_sgu_ln_b': 'new_v', 'new_v_sgu_w': 'new_v', 'new_v_sgu_b': 'new_v', 'new_v_w_out_odd': 'new_v', 'new_v_w_ffn_up': 'new_v', 'new_v_ffn_conv_w': 'new_v', 'new_v_ffn_conv_b': 'new_v', 'new_v_w_ffn_down': 'new_v'}


def _forward(args):
    return _fwd_reference(*[args[k] for k in FWD_PARAMS])


def _output_shape():
    def fwd():
        inp = _fwd_setup_inputs(0)
        return _fwd_reference(*[inp[k] for k in FWD_PARAMS])
    out = _jax.eval_shape(fwd)
    return out.shape, out.dtype

N_MICROBATCH = 1
ADAM_LR = 0.001
ADAM_B1 = 0.9
ADAM_B2 = 0.999
ADAM_EPS = 1e-08
ADAM_WD = 0.01
ADAM_STEP = 10
PER_EXAMPLE_BATCH_AXIS = {'x': 0, 'c': 0, 'ctx': 0, 'loss_target': 0}
SHARED_INPUTS = []
_WEIGHT_DTYPES = {'c_ctx': _jnp.float32, 'w_ada': _jnp.float32, 'b_ada': _jnp.float32, 'g_mix_pre': _jnp.float32, 'g_mix_post': _jnp.float32, 'g_ffn_pre': _jnp.float32, 'g_ffn_post': _jnp.float32, 'w_in_even': _jnp.float32, 'w_pool': _jnp.float32, 'pool_scale': _jnp.float32, 'attn_sink': _jnp.float32, 'w_out_even': _jnp.float32, 'w_in_odd': _jnp.float32, 'sgu_ln_g': _jnp.float32, 'sgu_ln_b': _jnp.float32, 'sgu_w': _jnp.float32, 'sgu_b': _jnp.float32, 'w_out_odd': _jnp.float32, 'w_ffn_up': _jnp.float32, 'ffn_conv_w': _jnp.float32, 'ffn_conv_b': _jnp.float32, 'w_ffn_down': _jnp.float32}
MOMENT_SCALE = {'c_ctx': 9.106134e-02, 'w_ada': 1.275873e+00, 'b_ada': 2.760855e+00, 'g_mix_pre': 1.413745e-01, 'g_mix_post': 3.282274e+00, 'g_ffn_pre': 1.310177e-01, 'g_ffn_post': 3.361825e+00, 'w_in_even': 1.571785e-01, 'w_pool': 2.118500e-01, 'pool_scale': 2.349936e-01, 'attn_sink': 8.724858e-04, 'w_out_even': 1.966267e-01, 'w_in_odd': 1.045070e-01, 'sgu_ln_g': 6.976840e-02, 'sgu_ln_b': 6.757945e-02, 'sgu_w': 6.468470e-02, 'sgu_b': 6.278048e-02, 'w_out_odd': 2.057644e-01, 'w_ffn_up': 6.270638e-02, 'ffn_conv_w': 6.771260e-02, 'ffn_conv_b': 1.028472e-01, 'w_ffn_down': 1.160880e-01}


def _to_microbatches(a, axis):
    t = _jnp.moveaxis(a, axis, 0)
    t = t.reshape((N_MICROBATCH, t.shape[0] // N_MICROBATCH) + t.shape[1:])
    return _jnp.moveaxis(t, 1, axis + 1)


def setup_inputs(seed: int = 0) -> dict:
    inp = _fwd_setup_inputs(seed)
    key = _jax.random.fold_in(_jax.random.key(seed), 7919)
    shape, _ = _output_shape()
    out = dict(inp)
    out["loss_target"] = _jax.random.normal(_jax.random.fold_in(key, 0), shape, _jnp.float32)
    for i, name in enumerate(TWIN_WEIGHTS):
        w = inp[name].astype(_jnp.float32)
        if MOMENT_SCALE is None:
            s = _jnp.sqrt(_jnp.mean(_jnp.square(w)) + 1e-30)
        else:
            s = MOMENT_SCALE[name]
        km, kv = _jax.random.split(_jax.random.fold_in(key, i + 1))
        out[name] = w
        out["m_" + name] = s * _jax.random.normal(km, w.shape, _jnp.float32)
        out["v_" + name] = (s * s) * _jax.random.uniform(kv, w.shape, _jnp.float32, 0.5, 1.5)
    if N_MICROBATCH > 1:
        for name, axis in PER_EXAMPLE_BATCH_AXIS.items():
            out[name] = _to_microbatches(out[name], axis)
    return {'x': out['x'], 'c': out['c'], 'ctx': out['ctx'], 'c_ctx': out['c_ctx'], 'w_ada': out['w_ada'], 'b_ada': out['b_ada'], 'g_mix_pre': out['g_mix_pre'], 'g_mix_post': out['g_mix_post'], 'g_ffn_pre': out['g_ffn_pre'], 'g_ffn_post': out['g_ffn_post'], 'w_in_even': out['w_in_even'], 'w_pool': out['w_pool'], 'pool_scale': out['pool_scale'], 'attn_sink': out['attn_sink'], 'w_out_even': out['w_out_even'], 'w_in_odd': out['w_in_odd'], 'sgu_ln_g': out['sgu_ln_g'], 'sgu_ln_b': out['sgu_ln_b'], 'sgu_w': out['sgu_w'], 'sgu_b': out['sgu_b'], 'w_out_odd': out['w_out_odd'], 'w_ffn_up': out['w_ffn_up'], 'ffn_conv_w': out['ffn_conv_w'], 'ffn_conv_b': out['ffn_conv_b'], 'w_ffn_down': out['w_ffn_down'], 'loss_target': out['loss_target'], 'm_c_ctx': out['m_c_ctx'], 'm_w_ada': out['m_w_ada'], 'm_b_ada': out['m_b_ada'], 'm_g_mix_pre': out['m_g_mix_pre'], 'm_g_mix_post': out['m_g_mix_post'], 'm_g_ffn_pre': out['m_g_ffn_pre'], 'm_g_ffn_post': out['m_g_ffn_post'], 'm_w_in_even': out['m_w_in_even'], 'm_w_pool': out['m_w_pool'], 'm_pool_scale': out['m_pool_scale'], 'm_attn_sink': out['m_attn_sink'], 'm_w_out_even': out['m_w_out_even'], 'm_w_in_odd': out['m_w_in_odd'], 'm_sgu_ln_g': out['m_sgu_ln_g'], 'm_sgu_ln_b': out['m_sgu_ln_b'], 'm_sgu_w': out['m_sgu_w'], 'm_sgu_b': out['m_sgu_b'], 'm_w_out_odd': out['m_w_out_odd'], 'm_w_ffn_up': out['m_w_ffn_up'], 'm_ffn_conv_w': out['m_ffn_conv_w'], 'm_ffn_conv_b': out['m_ffn_conv_b'], 'm_w_ffn_down': out['m_w_ffn_down'], 'v_c_ctx': out['v_c_ctx'], 'v_w_ada': out['v_w_ada'], 'v_b_ada': out['v_b_ada'], 'v_g_mix_pre': out['v_g_mix_pre'], 'v_g_mix_post': out['v_g_mix_post'], 'v_g_ffn_pre': out['v_g_ffn_pre'], 'v_g_ffn_post': out['v_g_ffn_post'], 'v_w_in_even': out['v_w_in_even'], 'v_w_pool': out['v_w_pool'], 'v_pool_scale': out['v_pool_scale'], 'v_attn_sink': out['v_attn_sink'], 'v_w_out_even': out['v_w_out_even'], 'v_w_in_odd': out['v_w_in_odd'], 'v_sgu_ln_g': out['v_sgu_ln_g'], 'v_sgu_ln_b': out['v_sgu_ln_b'], 'v_sgu_w': out['v_sgu_w'], 'v_sgu_b': out['v_sgu_b'], 'v_w_out_odd': out['v_w_out_odd'], 'v_w_ffn_up': out['v_w_ffn_up'], 'v_ffn_conv_w': out['v_ffn_conv_w'], 'v_ffn_conv_b': out['v_ffn_conv_b'], 'v_w_ffn_down': out['v_w_ffn_down']}


def _loss(weights, diff, rest, loss_target):
    with _jax.named_scope("forward"):
        args = {**rest, TWIN_DIFF_INPUT: diff, **{k: w.astype(_WEIGHT_DTYPES[k]) for k, w in weights.items()}}
        y = _forward(args)
    with _jax.named_scope("loss_head"):
        err = _jnp.square(y.astype(_jnp.float32) - loss_target)
        return 0.5 * _jnp.sum(_jnp.mean(err, axis=-1)) if err.ndim else 0.5 * err


def _adamw(w, g, m, v):
    m = ADAM_B1 * m + (1.0 - ADAM_B1) * g
    v = ADAM_B2 * v + (1.0 - ADAM_B2) * _jnp.square(g)
    m_hat = m / (1.0 - ADAM_B1 ** ADAM_STEP)
    v_hat = v / (1.0 - ADAM_B2 ** ADAM_STEP)
    delta = -ADAM_LR * (m_hat / (_jnp.sqrt(v_hat) + ADAM_EPS) + ADAM_WD * w)
    return delta, m, v


def reference(x, c, ctx, c_ctx, w_ada, b_ada, g_mix_pre, g_mix_post, g_ffn_pre, g_ffn_post, w_in_even, w_pool, pool_scale, attn_sink, w_out_even, w_in_odd, sgu_ln_g, sgu_ln_b, sgu_w, sgu_b, w_out_odd, w_ffn_up, ffn_conv_w, ffn_conv_b, w_ffn_down, loss_target, m_c_ctx, m_w_ada, m_b_ada, m_g_mix_pre, m_g_mix_post, m_g_ffn_pre, m_g_ffn_post, m_w_in_even, m_w_pool, m_pool_scale, m_attn_sink, m_w_out_even, m_w_in_odd, m_sgu_ln_g, m_sgu_ln_b, m_sgu_w, m_sgu_b, m_w_out_odd, m_w_ffn_up, m_ffn_conv_w, m_ffn_conv_b, m_w_ffn_down, v_c_ctx, v_w_ada, v_b_ada, v_g_mix_pre, v_g_mix_post, v_g_ffn_pre, v_g_ffn_post, v_w_in_even, v_w_pool, v_pool_scale, v_attn_sink, v_w_out_even, v_w_in_odd, v_sgu_ln_g, v_sgu_ln_b, v_sgu_w, v_sgu_b, v_w_out_odd, v_w_ffn_up, v_ffn_conv_w, v_ffn_conv_b, v_w_ffn_down):
    given = dict(x=x, c=c, ctx=ctx, c_ctx=c_ctx, w_ada=w_ada, b_ada=b_ada, g_mix_pre=g_mix_pre, g_mix_post=g_mix_post, g_ffn_pre=g_ffn_pre, g_ffn_post=g_ffn_post, w_in_even=w_in_even, w_pool=w_pool, pool_scale=pool_scale, attn_sink=attn_sink, w_out_even=w_out_even, w_in_odd=w_in_odd, sgu_ln_g=sgu_ln_g, sgu_ln_b=sgu_ln_b, sgu_w=sgu_w, sgu_b=sgu_b, w_out_odd=w_out_odd, w_ffn_up=w_ffn_up, ffn_conv_w=ffn_conv_w, ffn_conv_b=ffn_conv_b, w_ffn_down=w_ffn_down, loss_target=loss_target, m_c_ctx=m_c_ctx, m_w_ada=m_w_ada, m_b_ada=m_b_ada, m_g_mix_pre=m_g_mix_pre, m_g_mix_post=m_g_mix_post, m_g_ffn_pre=m_g_ffn_pre, m_g_ffn_post=m_g_ffn_post, m_w_in_even=m_w_in_even, m_w_pool=m_w_pool, m_pool_scale=m_pool_scale, m_attn_sink=m_attn_sink, m_w_out_even=m_w_out_even, m_w_in_odd=m_w_in_odd, m_sgu_ln_g=m_sgu_ln_g, m_sgu_ln_b=m_sgu_ln_b, m_sgu_w=m_sgu_w, m_sgu_b=m_sgu_b, m_w_out_odd=m_w_out_odd, m_w_ffn_up=m_w_ffn_up, m_ffn_conv_w=m_ffn_conv_w, m_ffn_conv_b=m_ffn_conv_b, m_w_ffn_down=m_w_ffn_down, v_c_ctx=v_c_ctx, v_w_ada=v_w_ada, v_b_ada=v_b_ada, v_g_mix_pre=v_g_mix_pre, v_g_mix_post=v_g_mix_post, v_g_ffn_pre=v_g_ffn_pre, v_g_ffn_post=v_g_ffn_post, v_w_in_even=v_w_in_even, v_w_pool=v_w_pool, v_pool_scale=v_pool_scale, v_attn_sink=v_attn_sink, v_w_out_even=v_w_out_even, v_w_in_odd=v_w_in_odd, v_sgu_ln_g=v_sgu_ln_g, v_sgu_ln_b=v_sgu_ln_b, v_sgu_w=v_sgu_w, v_sgu_b=v_sgu_b, v_w_out_odd=v_w_out_odd, v_w_ffn_up=v_w_ffn_up, v_ffn_conv_w=v_ffn_conv_w, v_ffn_conv_b=v_ffn_conv_b, v_w_ffn_down=v_w_ffn_down)
    weights = {n: given[n] for n in TWIN_WEIGHTS}
    shared = {n: given[n] for n in SHARED_INPUTS}
    per_example = {n: given[n] for n in ['x', 'c', 'ctx']}
    grad_fn = _jax.value_and_grad(_loss, argnums=(0, 1))

    def one_microbatch(ex, loss_target):
        ex = dict(ex)
        diff = ex.pop(TWIN_DIFF_INPUT)
        return grad_fn(weights, diff, {**shared, **ex}, loss_target)

    if N_MICROBATCH == 1:
        loss, (grad_w, grad_x) = one_microbatch(per_example, given["loss_target"])
    else:
        def body(carry, xs):
            loss_sum, grad_sum = carry
            l_k, (gw_k, gx_k) = one_microbatch(xs[0], xs[1])
            with _jax.named_scope("update"):
                return (loss_sum + l_k, _jax.tree.map(_jnp.add, grad_sum, gw_k)), gx_k

        init = (_jnp.zeros((), _jnp.float32), _jax.tree.map(_jnp.zeros_like, weights))
        (loss, grad_w), grad_x = _jax.lax.scan(body, init, (per_example, given["loss_target"]))
    with _jax.named_scope("update"):
        delta_w, new_m, new_v = {}, {}, {}
        for n in TWIN_WEIGHTS:
            delta_w[n], new_m[n], new_v[n] = _adamw(weights[n], grad_w[n], given["m_" + n], given["v_" + n])
    return (loss, grad_x, *[grad_w[n] for n in TWIN_WEIGHTS], *[delta_w[n] for n in TWIN_WEIGHTS],
            *[new_m[n] for n in TWIN_WEIGHTS], *[new_v[n] for n in TWIN_WEIGHTS])
```

```python
import functools
import math

import jax
import jax.numpy as jnp
from jax import lax
from jax.experimental import pallas as pl
from jax.experimental.pallas import tpu as pltpu

F32 = jnp.float32
BF16 = jnp.bfloat16
MESH = pl.DeviceIdType.MESH
N_DEV = 8
LANES = 128
VMEM_LIMIT = 48 * 1024 * 1024
EPS = 1e-6
NEG_INF = -1e30
GRID_W = 64
WINDOW = 128
BLK = 128
HEAD_DIM = 64
N_Q_HEADS = 8
N_KV_HEADS = 2
GQA = N_Q_HEADS // N_KV_HEADS
POOL_WINDOWS = (2, 4, 8, 16)
ROPE_BASE = 10000.0
ROPE_FREQS = HEAD_DIM // 4
PAD = 16
ADAM_LR, ADAM_B1, ADAM_B2, ADAM_EPS, ADAM_WD, ADAM_STEP = 0.001, 0.9, 0.999, 1e-08, 0.01, 10
BC1 = 1.0 - ADAM_B1 ** ADAM_STEP
BC2 = 1.0 - ADAM_B2 ** ADAM_STEP
SQRT_2_OVER_PI = math.sqrt(2.0 / math.pi)
GELU_C = 0.044715


def _cp(sem=None):
    return pltpu.CompilerParams(dimension_semantics=sem, vmem_limit_bytes=VMEM_LIMIT)


def _dot(a, b):
    return jnp.dot(a, b, preferred_element_type=F32)


def _dot_nt(a, b):
    return lax.dot_general(a, b, (((1,), (1,)), ((), ())), preferred_element_type=F32)


def _dot_tn(a, b):
    return lax.dot_general(a, b, (((0,), (0,)), ((), ())), preferred_element_type=F32)


def _rms(x):
    r = lax.rsqrt(jnp.mean(x * x, axis=-1, keepdims=True) + EPS)
    return x * r, r


def _rms_bwd(dn, n, r):
    return r * (dn - n * jnp.mean(dn * n, axis=-1, keepdims=True))


def _colsum(a):
    return jnp.sum(a, axis=0, keepdims=True)


def _rope(x, c, sa, sb):
    return x * c + pltpu.roll(x, LANES - ROPE_FREQS, 1) * sa + pltpu.roll(x, ROPE_FREQS, 1) * sb


def _full(shape):
    return pl.BlockSpec(shape, lambda *_: (0,) * len(shape))


def pre_mm(x, g, sh, sc, wt, *, tm, tn, pair=False, w_row_off=0, name):
    T, D = x.shape
    n_rows = wt.shape[0] - w_row_off
    half = tn // 2
    nj = n_rows // tn
    off = w_row_off // (half if pair else tn)

    def body(x_ref, g_ref, sh_ref, sc_ref, *rest):
        h_ref, z_ref = rest[-2:]

        @pl.when(pl.program_id(1) == 0)
        def _():
            n, _ = _rms(x_ref[...])
            h_ref[...] = (n * g_ref[...] * (1.0 + sc_ref[...]) + sh_ref[...]).astype(BF16)

        h = h_ref[...]
        if pair:
            z_ref[:, :half] = _dot_nt(h, rest[0][...]).astype(BF16)
            z_ref[:, half:] = _dot_nt(h, rest[1][...]).astype(BF16)
        else:
            z_ref[...] = _dot_nt(h, rest[0][...]).astype(BF16)

    vec = pl.BlockSpec((1, D), lambda i, j: (0, 0))
    if pair:
        w_specs = [pl.BlockSpec((half, D), lambda i, j: (j, 0)), pl.BlockSpec((half, D), lambda i, j: (j + nj, 0))]
        w_args = (wt, wt)
    else:
        w_specs = [pl.BlockSpec((tn, D), lambda i, j: (j + off, 0))]
        w_args = (wt,)
    return pl.pallas_call(
        body, name=name, grid=(T // tm, nj),
        in_specs=[pl.BlockSpec((tm, D), lambda i, j: (i, 0)), vec, vec, vec] + w_specs,
        out_specs=[pl.BlockSpec((tm, D), lambda i, j: (i, 0)), pl.BlockSpec((tm, tn), lambda i, j: (i, j))],
        out_shape=[jax.ShapeDtypeStruct((T, D), BF16), jax.ShapeDtypeStruct((T, n_rows), BF16)],
        compiler_params=_cp(("parallel", "arbitrary")),
    )(x, g, sh, sc, *w_args)


def inproj_even(x, g, sh, sc, wt, cos, sa, sb, *, tm, name):
    T, D = x.shape
    N = wt.shape[0]

    def body(x_ref, g_ref, sh_ref, sc_ref, w_ref, c_ref, sa_ref, sb_ref, h_ref, u_ref, q_ref, kv_ref):
        n, _ = _rms(x_ref[...])
        h = (n * g_ref[...] * (1.0 + sc_ref[...]) + sh_ref[...]).astype(BF16)
        h_ref[...] = h
        z = _dot_nt(h, w_ref[...])
        u_ref[...] = z[:, :4 * LANES]
        c, a, b = c_ref[...], sa_ref[...], sb_ref[...]
        for s in range(4):
            q_ref[:, s * LANES:(s + 1) * LANES] = _rope(z[:, (4 + s) * LANES:(5 + s) * LANES], c, a, b).astype(BF16)
        kv_ref[:, :LANES] = _rope(z[:, 8 * LANES:9 * LANES], c, a, b).astype(BF16)
        kv_ref[:, LANES:] = z[:, 9 * LANES:].astype(BF16)

    vec = pl.BlockSpec((1, D), lambda i: (0, 0))
    row = lambda w: pl.BlockSpec((tm, w), lambda i: (i, 0))
    return pl.pallas_call(
        body, name=name, grid=(T // tm,),
        in_specs=[row(D), vec, vec, vec, _full((N, D)), row(LANES), row(LANES), row(LANES)],
        out_specs=[row(D), row(4 * LANES), row(4 * LANES), row(2 * LANES)],
        out_shape=[jax.ShapeDtypeStruct((T, D), BF16), jax.ShapeDtypeStruct((T, 4 * LANES), F32),
                   jax.ShapeDtypeStruct((T, 4 * LANES), BF16), jax.ShapeDtypeStruct((T, 2 * LANES), BF16)],
        compiler_params=_cp(("parallel",)),
    )(x, g, sh, sc, wt, cos, sa, sb)


def mm_post(a, w, x, g, gt, *, tm, name):
    T, K = a.shape
    D = w.shape[1]

    def body(a_ref, w_ref, x_ref, g_ref, gt_ref, y_ref, xn_ref):
        y = _dot(a_ref[...], w_ref[...])
        n, _ = _rms(y)
        y_ref[...] = y
        xn_ref[...] = x_ref[...] + gt_ref[...] * (n * g_ref[...])

    vec = pl.BlockSpec((1, D), lambda i: (0, 0))
    row = lambda w_: pl.BlockSpec((tm, w_), lambda i: (i, 0))
    return pl.pallas_call(
        body, name=name, grid=(T // tm,),
        in_specs=[row(K), _full((K, D)), row(D), vec, vec],
        out_specs=[row(D), row(D)],
        out_shape=[jax.ShapeDtypeStruct((T, D), F32), jax.ShapeDtypeStruct((T, D), F32)],
        compiler_params=_cp(("parallel",)),
    )(a, w, x, g, gt)


def post_bwd_mm(dxn, y, g, gt, w, *, tm, name):
    T, D = y.shape
    K = w.shape[0]

    def body(dxn_ref, y_ref, g_ref, gt_ref, w_ref, dy_ref, da_ref, dg_ref, dgt_ref):
        @pl.when(pl.program_id(0) == 0)
        def _():
            dg_ref[...] = jnp.zeros_like(dg_ref)
            dgt_ref[...] = jnp.zeros_like(dgt_ref)

        d = dxn_ref[...]
        n, r = _rms(y_ref[...])
        g_, gt_ = g_ref[...], gt_ref[...]
        dg_ref[...] += _colsum(d * gt_ * n)
        dgt_ref[...] += _colsum(d * g_ * n)
        dy = _rms_bwd(d * (gt_ * g_), n, r).astype(BF16)
        dy_ref[...] = dy
        da_ref[...] = _dot_nt(dy, w_ref[...]).astype(BF16)

    vec = pl.BlockSpec((1, D), lambda i: (0, 0))
    row = lambda w_: pl.BlockSpec((tm, w_), lambda i: (i, 0))
    return pl.pallas_call(
        body, name=name, grid=(T // tm,),
        in_specs=[row(D), row(D), vec, vec, _full((K, D))],
        out_specs=[row(D), row(K), vec, vec],
        out_shape=[jax.ShapeDtypeStruct((T, D), BF16), jax.ShapeDtypeStruct((T, K), BF16),
                   jax.ShapeDtypeStruct((1, D), F32), jax.ShapeDtypeStruct((1, D), F32)],
        compiler_params=_cp(("arbitrary",)),
    )(dxn, y, g, gt, w)


def mm_pre_bwd(dz, wt, x, dres, g, sc, *, tm, tk, pair=False, w_row_off=0, name):
    T, N = dz.shape
    D = x.shape[1]
    nk = N // tk
    half = tk // 2
    off = w_row_off // tk
    has_res = dres is not None

    def body(*refs):
        dz_ref = refs[0]
        w_refs = refs[1:3] if pair else refs[1:2]
        rest = refs[len(w_refs) + 1:]
        x_ref = rest[0]
        dres_ref = rest[1] if has_res else None
        g_ref, sc_ref, dx_ref, dg_ref, dsh_ref, dsc_ref, acc = rest[1 + has_res:]
        i, k = pl.program_id(0), pl.program_id(1)

        @pl.when(jnp.logical_and(i == 0, k == 0))
        def _():
            dg_ref[...] = jnp.zeros_like(dg_ref)
            dsh_ref[...] = jnp.zeros_like(dsh_ref)
            dsc_ref[...] = jnp.zeros_like(dsc_ref)

        if pair:
            part = _dot(dz_ref[:, :half], w_refs[0][...]) + _dot(dz_ref[:, half:], w_refs[1][...])
        else:
            part = _dot(dz_ref[...], w_refs[0][...])

        @pl.when(k == 0)
        def _():
            acc[...] = part

        @pl.when(k > 0)
        def _():
            acc[...] += part

        @pl.when(k == nk - 1)
        def _():
            dh = acc[...]
            n, r = _rms(x_ref[...])
            g_, s1 = g_ref[...], 1.0 + sc_ref[...]
            dsh_ref[...] += _colsum(dh)
            dsc_ref[...] += _colsum(dh * n * g_)
            dg_ref[...] += _colsum(dh * s1 * n)
            dxp = _rms_bwd(dh * (g_ * s1), n, r)
            dx_ref[...] = dxp + dres_ref[...] if has_res else dxp

    vec = pl.BlockSpec((1, D), lambda i, k: (0, 0))
    row = pl.BlockSpec((tm, D), lambda i, k: (i, 0))
    if pair:
        w_specs = [pl.BlockSpec((half, D), lambda i, k: (k, 0)), pl.BlockSpec((half, D), lambda i, k: (k + nk, 0))]
        w_args = (wt, wt)
    else:
        w_specs = [pl.BlockSpec((tk, D), lambda i, k: (k + off, 0))]
        w_args = (wt,)
    res_specs, res_args = ([row], (dres,)) if has_res else ([], ())
    return pl.pallas_call(
        body, name=name, grid=(T // tm, nk),
        in_specs=[pl.BlockSpec((tm, tk), lambda i, k: (i, k))] + w_specs + [row] + res_specs + [vec, vec],
        out_specs=[row, vec, vec, vec],
        out_shape=[jax.ShapeDtypeStruct((T, D), F32)] + [jax.ShapeDtypeStruct((1, D), F32)] * 3,
        scratch_shapes=[pltpu.VMEM((tm, D), F32)],
        compiler_params=_cp(("arbitrary", "arbitrary")),
    )(dz, *w_args, x, *res_args, g, sc)


def wgrad(a, b, *, tr, a_index=None, extra=None, name):
    T, R = a.shape
    D = b.shape[1]
    a_index = a_index or (lambda r: r)

    def body(*refs):
        a_ref, b_ref = refs[:2]
        g_ref = refs[-1]
        acc = _dot_tn(a_ref[...], b_ref[...])
        if extra is not None:
            acc += _dot_tn(refs[2][...], refs[3][...])
        g_ref[...] = acc.astype(BF16)

    in_specs = [pl.BlockSpec((T, tr), lambda r: (0, a_index(r))), _full((T, D))]
    args = [a, b]
    if extra is not None:
        a2, b2 = extra
        in_specs += [pl.BlockSpec((a2.shape[0], tr), lambda r: (0, a_index(r))), _full(b2.shape)]
        args += [a2, b2]
    return pl.pallas_call(
        body, name=name, grid=(R // tr,),
        in_specs=in_specs, out_specs=pl.BlockSpec((tr, D), lambda r: (r, 0)),
        out_shape=jax.ShapeDtypeStruct((R, D), BF16),
        compiler_params=_cp(("parallel",)),
    )(*args)


def _conv_ext(ref, r0, rows, total):
    top = ref[pl.ds(pl.multiple_of(jnp.maximum(r0 - PAD, 0), PAD), PAD), :]
    mid = ref[pl.ds(r0, rows), :]
    bot = ref[pl.ds(pl.multiple_of(jnp.minimum(r0 + rows, total - PAD), PAD), PAD), :]
    ext = jnp.concatenate([top, mid, bot], axis=0).astype(F32)
    t = r0 - PAD + lax.broadcasted_iota(jnp.int32, (rows + 2 * PAD, 1), 0)
    return jnp.where(jnp.logical_and(t >= 0, t < total), ext, 0.0)


def _shift_rows(a, k):
    return pltpu.roll(a, k % a.shape[0], 0)


def conv_fwd(hu, cw, cb, *, rows, name):
    L, N2 = hu.shape
    wblk = 4 * LANES
    half = wblk // 2
    nchunk = L // rows

    def body(hu_ref, w_ref, b_ref, a_ref):
        w, b = w_ref[...], b_ref[...]

        def chunk(ci, carry):
            r0 = pl.multiple_of(ci * rows, rows)
            x = _conv_ext(hu_ref, r0, rows, L)
            hcv = w[0:1] * _shift_rows(x, 1) + w[1:2] * x + w[2:3] * _shift_rows(x, -1) + b
            hcv = hcv[PAD:PAD + rows]
            gate, up = hcv[:, :half], hcv[:, half:]
            a_ref[pl.ds(r0, rows), :] = (gate * jax.nn.sigmoid(gate) * up).astype(BF16)
            return carry

        lax.fori_loop(0, nchunk, chunk, 0)

    return pl.pallas_call(
        body, name=name, grid=(N2 // wblk,),
        in_specs=[pl.BlockSpec((L, wblk), lambda j: (0, j)), pl.BlockSpec((3, wblk), lambda j: (0, j)),
                  pl.BlockSpec((1, wblk), lambda j: (0, j))],
        out_specs=pl.BlockSpec((L, half), lambda j: (0, j)),
        out_shape=jax.ShapeDtypeStruct((L, N2 // 2), BF16),
        compiler_params=_cp(("parallel",)),
    )(hu, cw, cb)


def conv_bwd(da, hu, cw, cb, *, rows, name):
    L, N2 = hu.shape
    wblk = 4 * LANES
    half = wblk // 2
    nchunk = L // rows

    def body(da_ref, hu_ref, w_ref, b_ref, dhu_ref, dw_ref, db_ref):
        w, b = w_ref[...], b_ref[...]
        dw_ref[...] = jnp.zeros_like(dw_ref)
        db_ref[...] = jnp.zeros_like(db_ref)

        def chunk(ci, carry):
            r0 = pl.multiple_of(ci * rows, rows)
            x = _conv_ext(hu_ref, r0, rows, L)
            d = _conv_ext(da_ref, r0, rows, L)
            xm, xp = _shift_rows(x, 1), _shift_rows(x, -1)
            hcv = w[0:1] * xm + w[1:2] * x + w[2:3] * xp + b
            gate, up = hcv[:, :half], hcv[:, half:]
            sg = jax.nn.sigmoid(gate)
            dup = d * (gate * sg)
            dgate = d * up * (sg * (1.0 + gate * (1.0 - sg)))
            dh = jnp.concatenate([dgate, dup], axis=1)
            dhu = w[0:1] * _shift_rows(dh, -1) + w[1:2] * dh + w[2:3] * _shift_rows(dh, 1)
            dhu_ref[pl.ds(r0, rows), :] = dhu[PAD:PAD + rows].astype(BF16)
            mid = slice(PAD, PAD + rows)
            dhm = dh[mid]
            db_ref[...] += _colsum(dhm)
            dw_ref[0:1, :] += _colsum(dhm * xm[mid])
            dw_ref[1:2, :] += _colsum(dhm * x[mid])
            dw_ref[2:3, :] += _colsum(dhm * xp[mid])
            return carry

        lax.fori_loop(0, nchunk, chunk, 0)

    return pl.pallas_call(
        body, name=name, grid=(N2 // wblk,),
        in_specs=[pl.BlockSpec((L, half), lambda j: (0, j)), pl.BlockSpec((L, wblk), lambda j: (0, j)),
                  pl.BlockSpec((3, wblk), lambda j: (0, j)), pl.BlockSpec((1, wblk), lambda j: (0, j))],
        out_specs=[pl.BlockSpec((L, wblk), lambda j: (0, j)), pl.BlockSpec((3, wblk), lambda j: (0, j)),
                   pl.BlockSpec((1, wblk), lambda j: (0, j))],
        out_shape=[jax.ShapeDtypeStruct((L, N2), BF16), jax.ShapeDtypeStruct((3, N2), F32),
                   jax.ShapeDtypeStruct((1, N2), F32)],
        compiler_params=_cp(("parallel",)),
    )(da, hu, cw, cb)


def _window_sums(pad_ref, w, lead):
    a = pad_ref[...]
    k = 1
    while k < w:
        a = a + _shift_rows(a, -k)
        k *= 2
    return _shift_rows(a, lead) if lead else a


def _pool_counts(L, h):
    t = lax.broadcasted_iota(jnp.int32, (L, 1), 0)
    return (jnp.minimum(t + h, L) - jnp.maximum(t - h, 0)).astype(F32)


def _pooled(u_ref, pad_ref, L, w):
    h = w // 2
    pad_ref[pl.ds(PAD, L), :] = u_ref[...]
    win = _window_sums(pad_ref, w, h)[PAD:PAD + L]
    return win / _pool_counts(L, h) - u_ref[...]


def _zero_pad_edges(pad_ref, L):
    z = jnp.zeros((PAD, LANES), F32)
    pad_ref[pl.ds(0, PAD), :] = z
    pad_ref[pl.ds(PAD + L, PAD), :] = z


def pool_fwd(u, w_pool, pool_scale, *, name):
    L = u.shape[0]

    def body(u_ref, w_ref, ps_ref, p_ref, pad_ref):
        _zero_pad_edges(pad_ref, L)
        for gi, win in enumerate(POOL_WINDOWS):
            @pl.when(pl.program_id(0) == gi)
            def _():
                pooled = _pooled(u_ref, pad_ref, L, win)
                p_ref[...] = (_dot(pooled.astype(BF16), w_ref[...].astype(BF16)) * ps_ref[...]).astype(BF16)

    return pl.pallas_call(
        body, name=name, grid=(len(POOL_WINDOWS),),
        in_specs=[pl.BlockSpec((L, LANES), lambda gi: (0, gi)), pl.BlockSpec((None, LANES, LANES), lambda gi: (gi, 0, 0)),
                  pl.BlockSpec((1, LANES), lambda gi: (0, gi))],
        out_specs=pl.BlockSpec((L, LANES), lambda gi: (0, gi)),
        out_shape=jax.ShapeDtypeStruct((L, 4 * LANES), BF16),
        scratch_shapes=[pltpu.VMEM((L + 2 * PAD, LANES), F32)],
        compiler_params=_cp(("parallel",)),
    )(u, w_pool, pool_scale)


def pool_bwd(u, dpa, w_pool, pool_scale, *, name):
    L = u.shape[0]

    def body(u_ref, dp_ref, w_ref, ps_ref, du_ref, dw_ref, dps_ref, pad_ref):
        _zero_pad_edges(pad_ref, L)
        for gi, win in enumerate(POOL_WINDOWS):
            @pl.when(pl.program_id(0) == gi)
            def _():
                h = win // 2
                wb = w_ref[...].astype(BF16)
                pooled = _pooled(u_ref, pad_ref, L, win).astype(BF16)
                dp = dp_ref[...].astype(F32)
                dps_ref[...] = _colsum(dp * _dot(pooled, wb))
                dy = (dp * ps_ref[...]).astype(BF16)
                dw_ref[...] = _dot_tn(pooled, dy)
                dpooled = _dot_nt(dy, wb)
                pad_ref[pl.ds(PAD, L), :] = dpooled / _pool_counts(L, h)
                du_ref[...] = (_window_sums(pad_ref, win, h - 1)[PAD:PAD + L] - dpooled).astype(BF16)

    return pl.pallas_call(
        body, name=name, grid=(len(POOL_WINDOWS),),
        in_specs=[pl.BlockSpec((L, LANES), lambda gi: (0, gi)), pl.BlockSpec((L, LANES), lambda gi: (0, gi)),
                  pl.BlockSpec((None, LANES, LANES), lambda gi: (gi, 0, 0)), pl.BlockSpec((1, LANES), lambda gi: (0, gi))],
        out_specs=[pl.BlockSpec((L, LANES), lambda gi: (0, gi)), pl.BlockSpec((None, LANES, LANES), lambda gi: (gi, 0, 0)),
                   pl.BlockSpec((1, LANES), lambda gi: (0, gi))],
        out_shape=[jax.ShapeDtypeStruct((L, 4 * LANES), BF16), jax.ShapeDtypeStruct((4, LANES, LANES), F32),
                   jax.ShapeDtypeStruct((1, 4 * LANES), F32)],
        scratch_shapes=[pltpu.VMEM((L + 2 * PAD, LANES), F32)],
        compiler_params=_cp(("parallel",)),
    )(u, dpa, w_pool, pool_scale)


def _attn_probs(q4, band, kvc, sink_ref, kh, mask4):
    scale = HEAD_DIM ** -0.5
    ks = slice(kh * HEAD_DIM, (kh + 1) * HEAD_DIM)
    s_loc = jnp.where(mask4, _dot_nt(q4, band[:, ks]) * scale, NEG_INF)
    s_ctx = _dot_nt(q4, kvc[:, ks]) * scale
    sk = jnp.concatenate([jnp.full((BLK, 1), sink_ref[kh * GQA + hh], F32) for hh in range(GQA)], axis=0)
    m = jnp.maximum(jnp.maximum(jnp.max(s_loc, axis=-1, keepdims=True), jnp.max(s_ctx, axis=-1, keepdims=True)), sk)
    e_loc, e_ctx, e_s = jnp.exp(s_loc - m), jnp.exp(s_ctx - m), jnp.exp(sk - m)
    inv = 1.0 / (jnp.sum(e_loc, axis=-1, keepdims=True) + jnp.sum(e_ctx, axis=-1, keepdims=True) + e_s)
    return e_loc * inv, e_ctx * inv, e_s * inv


def _attn_block(n, L):
    start = pl.multiple_of(jnp.clip((n - 1) * BLK, 0, L - 3 * BLK), BLK)
    qpos = n * BLK + lax.broadcasted_iota(jnp.int32, (BLK, 3 * BLK), 0)
    kpos = start + lax.broadcasted_iota(jnp.int32, (BLK, 3 * BLK), 1)
    mask = jnp.abs(kpos - qpos) <= WINDOW
    return start, jnp.concatenate([mask] * GQA, axis=0)


def _stack_heads(ref, kh):
    return jnp.concatenate([ref[:, (kh * GQA + hh) * HEAD_DIM:(kh * GQA + hh + 1) * HEAD_DIM] for hh in range(GQA)], axis=0)


def attn_fwd(q, kv, kvc, sink, *, name):
    L = q.shape[0]
    C = kvc.shape[0]

    def body(q_ref, kv_ref, kvc_ref, sink_ref, o_ref, o_scr):
        start, mask4 = _attn_block(pl.program_id(0), L)
        band = kv_ref[pl.ds(start, 3 * BLK), :]
        kvc_ = kvc_ref[...]
        for kh in range(N_KV_HEADS):
            q4 = _stack_heads(q_ref, kh)
            p_loc, p_ctx, _ = _attn_probs(q4, band, kvc_, sink_ref, kh, mask4)
            vs = slice(2 * HEAD_DIM + kh * HEAD_DIM, 2 * HEAD_DIM + (kh + 1) * HEAD_DIM)
            o4 = _dot(p_loc.astype(BF16), band[:, vs]) + _dot(p_ctx.astype(BF16), kvc_[:, vs])
            for hh in range(GQA):
                h = kh * GQA + hh
                o_scr[:, h * HEAD_DIM:(h + 1) * HEAD_DIM] = o4[hh * BLK:(hh + 1) * BLK]
        o_ref[...] = o_scr[...].astype(BF16)

    return pl.pallas_call(
        body, name=name, grid=(L // BLK,),
        in_specs=[pl.BlockSpec((BLK, 4 * LANES), lambda n: (n, 0)), _full((L, 2 * LANES)), _full((C, 2 * LANES)),
                  pl.BlockSpec(memory_space=pltpu.SMEM)],
        out_specs=pl.BlockSpec((BLK, 4 * LANES), lambda n: (n, 0)),
        out_shape=jax.ShapeDtypeStruct((L, 4 * LANES), BF16),
        scratch_shapes=[pltpu.VMEM((BLK, 4 * LANES), F32)],
        compiler_params=_cp(("parallel",)),
    )(q, kv, kvc, sink)


def attn_bwd(q, kv, kvc, sink, dpa, cos, sa, sb, *, name):
    L = q.shape[0]
    C = kvc.shape[0]
    nb = L // BLK
    scale = HEAD_DIM ** -0.5

    def body(q_ref, kv_ref, kvc_ref, sink_ref, do_ref, c_ref, sa_ref, sb_ref, cq_ref, saq_ref, sbq_ref,
             dq_ref, dkv_ref, dkvc_ref, dsink_ref, dkv_acc, dkvc_acc, dq_scr, band_scr, ctx_scr):
        n = pl.program_id(0)

        @pl.when(n == 0)
        def _():
            dkv_acc[...] = jnp.zeros_like(dkv_acc)
            dkvc_acc[...] = jnp.zeros_like(dkvc_acc)
            dsink_ref[...] = jnp.zeros_like(dsink_ref)

        start, mask4 = _attn_block(n, L)
        band = kv_ref[pl.ds(start, 3 * BLK), :]
        kvc_ = kvc_ref[...]
        lane = lax.broadcasted_iota(jnp.int32, (1, LANES), 1)
        dsink = jnp.zeros((1, LANES), F32)
        for kh in range(N_KV_HEADS):
            ks = slice(kh * HEAD_DIM, (kh + 1) * HEAD_DIM)
            vs = slice(2 * HEAD_DIM + kh * HEAD_DIM, 2 * HEAD_DIM + (kh + 1) * HEAD_DIM)
            q4 = _stack_heads(q_ref, kh)
            do4 = _stack_heads(do_ref, kh)
            p_loc, p_ctx, p_s = _attn_probs(q4, band, kvc_, sink_ref, kh, mask4)
            dp_loc = _dot_nt(do4, band[:, vs])
            dp_ctx = _dot_nt(do4, kvc_[:, vs])
            delta = jnp.sum(p_loc * dp_loc, axis=-1, keepdims=True) + jnp.sum(p_ctx * dp_ctx, axis=-1, keepdims=True)
            ds_loc = (p_loc * (dp_loc - delta) * scale).astype(BF16)
            ds_ctx = (p_ctx * (dp_ctx - delta) * scale).astype(BF16)
            dsk = p_s * delta
            for hh in range(GQA):
                h = kh * GQA + hh
                dsink = dsink - jnp.where(lane == h, jnp.sum(dsk[hh * BLK:(hh + 1) * BLK], axis=0, keepdims=True), 0.0)
            dq4 = _dot(ds_loc, band[:, ks]) + _dot(ds_ctx, kvc_[:, ks])
            for hh in range(GQA):
                h = kh * GQA + hh
                dq_scr[:, h * HEAD_DIM:(h + 1) * HEAD_DIM] = dq4[hh * BLK:(hh + 1) * BLK]
            band_scr[:, ks] = _dot_tn(ds_loc, q4)
            band_scr[:, vs] = _dot_tn(p_loc.astype(BF16), do4)
            ctx_scr[:, ks] = _dot_tn(ds_ctx, q4)
            ctx_scr[:, vs] = _dot_tn(p_ctx.astype(BF16), do4)
        dsink_ref[...] += dsink
        dkv_acc[pl.ds(start, 3 * BLK), :] += band_scr[...]
        dkvc_acc[...] += ctx_scr[...]
        c, a, b = cq_ref[...], -saq_ref[...], -sbq_ref[...]
        for s in range(4):
            dq_ref[:, s * LANES:(s + 1) * LANES] = _rope(dq_scr[:, s * LANES:(s + 1) * LANES], c, a, b).astype(BF16)

        @pl.when(n == nb - 1)
        def _():
            dkv_ref[:, :LANES] = _rope(dkv_acc[:, :LANES], c_ref[...], -sa_ref[...], -sb_ref[...]).astype(BF16)
            dkv_ref[:, LANES:] = dkv_acc[:, LANES:].astype(BF16)
            dkvc_ref[...] = dkvc_acc[...].astype(BF16)

    blk = lambda w: pl.BlockSpec((BLK, w), lambda n: (n, 0))
    return pl.pallas_call(
        body, name=name, grid=(nb,),
        in_specs=[blk(4 * LANES), _full((L, 2 * LANES)), _full((C, 2 * LANES)), pl.BlockSpec(memory_space=pltpu.SMEM),
                  pl.BlockSpec((BLK, 4 * LANES), lambda n: (n, 1)),
                  _full((L, LANES)), _full((L, LANES)), _full((L, LANES)), blk(LANES), blk(LANES), blk(LANES)],
        out_specs=[blk(4 * LANES), _full((L, 2 * LANES)), _full((C, 2 * LANES)), _full((1, LANES))],
        out_shape=[jax.ShapeDtypeStruct((L, 4 * LANES), BF16), jax.ShapeDtypeStruct((L, 2 * LANES), BF16),
                   jax.ShapeDtypeStruct((C, 2 * LANES), BF16), jax.ShapeDtypeStruct((1, LANES), F32)],
        scratch_shapes=[pltpu.VMEM((L, 2 * LANES), F32), pltpu.VMEM((C, 2 * LANES), F32), pltpu.VMEM((BLK, 4 * LANES), F32),
                        pltpu.VMEM((3 * BLK, 2 * LANES), F32), pltpu.VMEM((C, 2 * LANES), F32)],
        compiler_params=_cp(("arbitrary",)),
    )(q, kv, kvc, sink, dpa, cos, sa, sb, cos, sa, sb)


def _gelu_parts(x):
    th = jnp.tanh(SQRT_2_OVER_PI * (x + GELU_C * x * x * x))
    return 0.5 * x * (1.0 + th), th


def _gelu_grad(x, th):
    return 0.5 * (1.0 + th) + 0.5 * x * (1.0 - th * th) * SQRT_2_OVER_PI * (1.0 + 3.0 * GELU_C * x * x)


def _layernorm(v):
    mu = jnp.mean(v, axis=-1, keepdims=True)
    vc = v - mu
    rstd = lax.rsqrt(jnp.mean(vc * vc, axis=-1, keepdims=True) + EPS)
    return vc * rstd, rstd


def sgu_fwd(z1, ln_g, ln_b, ws, bst, *, name):
    L, W2 = z1.shape
    W = W2 // 2
    ng = W // LANES

    def body(z_ref, g_ref, b_ref, ws_ref, bs_ref, o_ref):
        z, _ = _gelu_parts(z_ref[...].astype(F32))
        xhat, _ = _layernorm(z[:, W:])
        vln = (xhat * g_ref[...] + b_ref[...]).astype(BF16)
        for gi in range(ng):
            cs = slice(gi * LANES, (gi + 1) * LANES)
            s = _dot(ws_ref[gi].astype(BF16), vln[:, cs]) + bs_ref[:, gi:gi + 1]
            o_ref[:, cs] = (z[:, cs] * s).astype(BF16)

    vec = _full((1, W))
    return pl.pallas_call(
        body, name=name, grid=(L // BLK,),
        in_specs=[pl.BlockSpec((BLK, W2), lambda n: (n, 0)), vec, vec, _full((ng, LANES, LANES)), _full((BLK, ng))],
        out_specs=pl.BlockSpec((BLK, W), lambda n: (n, 0)),
        out_shape=jax.ShapeDtypeStruct((L, W), BF16),
        compiler_params=_cp(("parallel",)),
    )(z1, ln_g, ln_b, ws, bst)


def sgu_bwd(z1, dus, ln_g, ln_b, ws, bst, *, name):
    L, W2 = z1.shape
    W = W2 // 2
    ng = W // LANES

    def body(z_ref, d_ref, g_ref, b_ref, ws_ref, bs_ref, dz_ref, dws_ref, dbs_ref, dg_ref, db_ref, dv_scr):
        @pl.when(pl.program_id(0) == 0)
        def _():
            dws_ref[...] = jnp.zeros_like(dws_ref)
            dbs_ref[...] = jnp.zeros_like(dbs_ref)
            dg_ref[...] = jnp.zeros_like(dg_ref)
            db_ref[...] = jnp.zeros_like(db_ref)

        zp = z_ref[...].astype(F32)
        z, th = _gelu_parts(zp)
        xhat, rstd = _layernorm(z[:, W:])
        vln = (xhat * g_ref[...] + b_ref[...]).astype(BF16)
        d = d_ref[...].astype(F32)
        lane = lax.broadcasted_iota(jnp.int32, (1, LANES), 1)
        dbs = jnp.zeros((BLK, LANES), F32)
        dgel = _gelu_grad(zp, th)
        for gi in range(ng):
            cs = slice(gi * LANES, (gi + 1) * LANES)
            wb = ws_ref[gi].astype(BF16)
            s = _dot(wb, vln[:, cs]) + bs_ref[:, gi:gi + 1]
            dz_ref[:, cs] = (d[:, cs] * s * dgel[:, cs]).astype(BF16)
            ds = d[:, cs] * z[:, cs]
            dbs = dbs + jnp.where(lane == gi, jnp.sum(ds, axis=-1, keepdims=True), 0.0)
            dsb = ds.astype(BF16)
            dws_ref[gi] += _dot_nt(dsb, vln[:, cs])
            dv_scr[:, cs] = _dot_tn(wb, dsb)
        dbs_ref[...] += dbs
        dvln = dv_scr[...]
        dg_ref[...] += _colsum(dvln * xhat)
        db_ref[...] += _colsum(dvln)
        dxh = dvln * g_ref[...]
        dv = rstd * (dxh - jnp.mean(dxh, axis=-1, keepdims=True) - xhat * jnp.mean(dxh * xhat, axis=-1, keepdims=True))
        dz_ref[:, W:] = (dv * dgel[:, W:]).astype(BF16)

    vec = _full((1, W))
    return pl.pallas_call(
        body, name=name, grid=(L // BLK,),
        in_specs=[pl.BlockSpec((BLK, W2), lambda n: (n, 0)), pl.BlockSpec((BLK, W), lambda n: (n, 0)), vec, vec,
                  _full((ng, LANES, LANES)), _full((BLK, ng))],
        out_specs=[pl.BlockSpec((BLK, W2), lambda n: (n, 0)), _full((ng, LANES, LANES)), _full((BLK, LANES)), vec, vec],
        out_shape=[jax.ShapeDtypeStruct((L, W2), BF16), jax.ShapeDtypeStruct((ng, LANES, LANES), F32),
                   jax.ShapeDtypeStruct((BLK, LANES), F32), jax.ShapeDtypeStruct((1, W), F32), jax.ShapeDtypeStruct((1, W), F32)],
        scratch_shapes=[pltpu.VMEM((BLK, W), F32)],
        compiler_params=_cp(("arbitrary",)),
    )(z1, dus, ln_g, ln_b, ws, bst)


def loss_grad(xo, target, *, tm, name):
    T, D = xo.shape

    def body(x_ref, t_ref, l_ref, d_ref):
        @pl.when(pl.program_id(0) == 0)
        def _():
            l_ref[...] = jnp.zeros_like(l_ref)

        e = x_ref[...] - t_ref[...]
        l_ref[...] += 0.5 * jnp.sum(jnp.mean(e * e, axis=-1, keepdims=True), axis=0, keepdims=True)
        d_ref[...] = e * (1.0 / D)

    row = pl.BlockSpec((tm, D), lambda i: (i, 0))
    return pl.pallas_call(
        body, name=name, grid=(T // tm,), in_specs=[row, row], out_specs=[_full((1, 1)), row],
        out_shape=[jax.ShapeDtypeStruct((1, 1), F32), jax.ShapeDtypeStruct((T, D), F32)],
        compiler_params=_cp(("arbitrary",)),
    )(xo, target)


def adamw(w, m, v, gparts, *, tr, name):
    R, Wd = w.shape
    S = gparts.shape[0]

    def body(w_ref, m_ref, v_ref, gp_ref, g_ref, d_ref, nm_ref, nv_ref):
        g = gp_ref[0].astype(F32)
        for s in range(1, S):
            g = g + gp_ref[s].astype(F32)
        m_ = ADAM_B1 * m_ref[...] + (1.0 - ADAM_B1) * g
        v_ = ADAM_B2 * v_ref[...] + (1.0 - ADAM_B2) * (g * g)
        g_ref[...] = g
        nm_ref[...] = m_
        nv_ref[...] = v_
        d_ref[...] = -ADAM_LR * ((m_ / BC1) / (jnp.sqrt(v_ / BC2) + ADAM_EPS) + ADAM_WD * w_ref[...])

    row = pl.BlockSpec((tr, Wd), lambda i: (i, 0))
    return pl.pallas_call(
        body, name=name, grid=(R // tr,),
        in_specs=[row, row, row, pl.BlockSpec((S, tr, Wd), lambda i: (0, i, 0))],
        out_specs=[row] * 4, out_shape=[jax.ShapeDtypeStruct((R, Wd), F32)] * 4,
        compiler_params=_cp(("parallel",)),
    )(w, m, v, gparts)


def ada_fwd_mm(cs, w_ada, b_loc, *, name):
    R, D = cs.shape
    nl, _, n = w_ada.shape

    def body(c_ref, w_ref, b_ref, s_ref, m_ref):
        c = c_ref[...]
        s = c * jax.nn.sigmoid(c)
        s_ref[...] = s
        for i in range(nl):
            m_ref[i] = _dot(s.astype(BF16), w_ref[i].astype(BF16)) + b_ref[i:i + 1, :]

    return pl.pallas_call(
        body, name=name, in_specs=[_full((R, D)), _full((nl, D, n)), _full((nl, n))],
        out_specs=[_full((R, D)), _full((nl, R, n))], grid=(1,),
        out_shape=[jax.ShapeDtypeStruct((R, D), F32), jax.ShapeDtypeStruct((nl, R, n), F32)],
        compiler_params=_cp(("arbitrary",)),
    )(cs, w_ada, b_loc)


def ada_bwd_mm(s, c_ctx, dall, w_ada, *, name):
    R, D = s.shape
    nl, _, n = w_ada.shape

    def body(s_ref, cc_ref, d_ref, w_ref, gw_ref, dcc_ref):
        sb = s_ref[...].astype(BF16)
        row = lax.broadcasted_iota(jnp.int32, (R, 1), 0)
        dctx = d_ref[0, 1:2, :]
        for dv in range(1, N_DEV):
            dctx = dctx + d_ref[dv, 1:2, :]
        for i in range(nl):
            dm = jnp.zeros((R, n), F32)
            for dv in range(N_DEV):
                dm = dm + jnp.where(row == dv, d_ref[dv, 2 * i:2 * i + 1, :], 0.0)
            if i == 0:
                dm = dm + jnp.where(row == N_DEV, dctx, 0.0)
            gw_ref[i] = _dot_tn(sb, dm.astype(BF16))
        cc = cc_ref[...]
        sg = jax.nn.sigmoid(cc)
        ds = _dot_nt(jnp.broadcast_to(dctx, (8, n)).astype(BF16), w_ref[0].astype(BF16))
        dcc_ref[...] = ds * (sg * (1.0 + cc * (1.0 - sg)))

    return pl.pallas_call(
        body, name=name, grid=(1,),
        in_specs=[_full((R, D)), _full((1, D)), _full((N_DEV, 3, n)), _full((nl, D, n))],
        out_specs=[_full((nl, D, n)), _full((8, D))],
        out_shape=[jax.ShapeDtypeStruct((nl, D, n), F32), jax.ShapeDtypeStruct((8, D), F32)],
        compiler_params=_cp(("arbitrary",)),
    )(s, c_ctx, dall, w_ada)


def _place():
    x, y, c = lax.axis_index("x"), lax.axis_index("y"), lax.axis_index("c")
    return x, y, c


def _lin(p):
    return 4 * p[0] + 2 * p[1] + p[2]


def all_gather_small(xb, *, reduce=False, name):
    R, W = xb.shape

    def body(x_ref, *rest):
        out_ref = rest[0]
        send_sems, recv_sems, local_sem = rest[-3:]
        x, y, c = _place()
        me, sibling = (x, y, c), (x, y, 1 - c)
        chips = [(1 - x, y), (x, 1 - y), (1 - x, 1 - y)]

        def copy(k, block, to, src=None):
            dst = out_ref.at[_lin(block)]
            return pltpu.make_async_remote_copy(
                src_ref=dst if src is None else src, dst_ref=dst, send_sem=send_sems.at[k], recv_sem=recv_sems.at[k],
                device_id=to, device_id_type=MESH)

        mine = pltpu.make_async_copy(x_ref, out_ref.at[_lin(me)], local_sem)
        mine.start()
        first = [copy(0, me, sibling, src=x_ref)]
        first += [copy(1 + j, me, (*chip, c), src=x_ref) for j, chip in enumerate(chips)]
        for cp in first:
            cp.start()
        passed = [copy(4 + j, (*chip, c), sibling) for j, chip in enumerate(chips)]
        for j, chip in enumerate(chips):
            copy(1 + j, (*chip, c), me).wait_recv()
            passed[j].start()
        copy(0, sibling, me).wait_recv()
        for j, chip in enumerate(chips):
            copy(4 + j, (*chip, 1 - c), me).wait_recv()
        for cp in first + passed:
            cp.wait_send()
        mine.wait()
        if reduce:
            acc = out_ref[0]
            for dv in range(1, N_DEV):
                acc = acc + out_ref[dv]
            rest[1][...] = acc

    vm = pl.BlockSpec(memory_space=pltpu.VMEM)
    out_shape = [jax.ShapeDtypeStruct((N_DEV, R, W), xb.dtype)]
    if reduce:
        out_shape.append(jax.ShapeDtypeStruct((R, W), xb.dtype))
    res = pl.pallas_call(
        body, name=name, in_specs=[vm], out_specs=[vm] * len(out_shape), out_shape=out_shape,
        scratch_shapes=[pltpu.SemaphoreType.DMA((7,)), pltpu.SemaphoreType.DMA((7,)), pltpu.SemaphoreType.DMA],
        compiler_params=pltpu.CompilerParams(vmem_limit_bytes=VMEM_LIMIT),
    )(xb)
    return res if reduce else res[0]


def all_gather_weights(shards, *, name):
    nw = len(shards)

    def body(*refs):
        ins, outs = refs[:nw], refs[nw:2 * nw]
        send_sems, recv_sems, local_sems = refs[2 * nw:]
        x, y, c = _place()
        me, sibling = (x, y, c), (x, y, 1 - c)
        chips = [(1 - x, y), (x, 1 - y), (1 - x, 1 - y)]

        def rows(j, block):
            r = ins[j].shape[0]
            return outs[j].at[pl.ds(pl.multiple_of(_lin(block) * r, 16), r), :]

        def copy(j, k, block, to, src=None):
            dst = rows(j, block)
            return pltpu.make_async_remote_copy(
                src_ref=dst if src is None else src, dst_ref=dst, send_sem=send_sems.at[7 * j + k],
                recv_sem=recv_sems.at[7 * j + k], device_id=to, device_id_type=MESH)

        mine = [pltpu.make_async_copy(ins[j], rows(j, me), local_sems.at[j]) for j in range(nw)]
        first = []
        for j in range(nw):
            mine[j].start()
            first.append(copy(j, 0, me, sibling, src=ins[j]))
            first += [copy(j, 1 + i, me, (*chip, c), src=ins[j]) for i, chip in enumerate(chips)]
        for cp in first:
            cp.start()
        passed = []
        for j in range(nw):
            for i, chip in enumerate(chips):
                copy(j, 1 + i, (*chip, c), me).wait_recv()
                fw = copy(j, 4 + i, (*chip, c), sibling)
                fw.start()
                passed.append(fw)
        for j in range(nw):
            copy(j, 0, sibling, me).wait_recv()
            for i, chip in enumerate(chips):
                copy(j, 4 + i, (*chip, 1 - c), me).wait_recv()
        for cp in first + passed:
            cp.wait_send()
        for cp in mine:
            cp.wait()

    hbm = pl.BlockSpec(memory_space=pltpu.HBM)
    return pl.pallas_call(
        body, name=name, in_specs=[hbm] * nw, out_specs=[hbm] * nw,
        out_shape=[jax.ShapeDtypeStruct((N_DEV * s.shape[0], s.shape[1]), s.dtype) for s in shards],
        scratch_shapes=[pltpu.SemaphoreType.DMA((7 * nw,)), pltpu.SemaphoreType.DMA((7 * nw,)), pltpu.SemaphoreType.DMA((nw,))],
    )(*shards)


def reduce_scatter_send(grads, *, name):
    nw = len(grads)

    def body(*refs):
        ins, outs = refs[:nw], refs[nw:2 * nw]
        send_sems, recv_sems, local_sems = refs[2 * nw:]
        x, y, c = _place()
        me = _lin((x, y, c))
        peers = [(x ^ (k >> 2), y ^ ((k >> 1) & 1), c ^ (k & 1)) for k in range(1, N_DEV)]

        def copy(j, k, peer):
            r = outs[j].shape[1]
            return pltpu.make_async_remote_copy(
                src_ref=ins[j].at[pl.ds(pl.multiple_of(_lin(peer) * r, 16), r), :], dst_ref=outs[j].at[me],
                send_sem=send_sems.at[7 * j + k], recv_sem=recv_sems.at[7 * j + k], device_id=peer, device_id_type=MESH)

        mine, sends = [], []
        for j in range(nw):
            r = outs[j].shape[1]
            cp = pltpu.make_async_copy(ins[j].at[pl.ds(pl.multiple_of(me * r, 16), r), :], outs[j].at[me], local_sems.at[j])
            cp.start()
            mine.append(cp)
            for k, peer in enumerate(peers):
                cp = copy(j, k, peer)
                cp.start()
                sends.append(cp)
        for j in range(nw):
            for k, peer in enumerate(peers):
                r = outs[j].shape[1]
                pltpu.make_async_remote_copy(
                    src_ref=ins[j].at[pl.ds(0, r), :], dst_ref=outs[j].at[_lin(peer)], send_sem=send_sems.at[7 * j + k],
                    recv_sem=recv_sems.at[7 * j + k], device_id=peer, device_id_type=MESH).wait_recv()
        for cp in sends:
            cp.wait_send()
        for cp in mine:
            cp.wait()

    hbm = pl.BlockSpec(memory_space=pltpu.HBM)
    return pl.pallas_call(
        body, name=name, in_specs=[hbm] * nw, out_specs=[hbm] * nw,
        out_shape=[jax.ShapeDtypeStruct((N_DEV, g.shape[0] // N_DEV, g.shape[1]), g.dtype) for g in grads],
        scratch_shapes=[pltpu.SemaphoreType.DMA((7 * nw,)), pltpu.SemaphoreType.DMA((7 * nw,)), pltpu.SemaphoreType.DMA((nw,))],
    )(*grads)


def _rope_tables(L):
    t = jnp.arange(L)
    inv = ROPE_BASE ** (-jnp.arange(ROPE_FREQS, dtype=F32) / ROPE_FREQS)
    ar = (t // GRID_W).astype(F32)[:, None] * inv
    ac = (t % GRID_W).astype(F32)[:, None] * inv
    z = jnp.zeros_like(ar)
    cos = jnp.concatenate([jnp.cos(ar), jnp.cos(ar), jnp.cos(ac), jnp.cos(ac)], axis=1)
    sa = jnp.concatenate([-jnp.sin(ar), z, -jnp.sin(ac), z], axis=1)
    sb = jnp.concatenate([z, jnp.sin(ar), z, jnp.sin(ac)], axis=1)
    return tuple(jnp.tile(a, (1, LANES // HEAD_DIM)) for a in (cos, sa, sb))


def _rows128(a):
    f = a.reshape(-1)
    n = -(-f.shape[0] // LANES) * LANES
    return jnp.pad(f, (0, n - f.shape[0])).reshape(-1, LANES)


def _pad_rows(a, mult=8):
    n = -(-a.shape[0] // mult) * mult
    return jnp.pad(a, ((0, n - a.shape[0]), (0, 0)))


def _interleave(a, half_blocks, width):
    lead = a.shape[:-1]
    return a.reshape(*lead, 2, half_blocks, width).swapaxes(-3, -2).reshape(*lead, -1)


def _deinterleave(a, half_blocks, width):
    lead = a.shape[:-1]
    return a.reshape(*lead, half_blocks, 2, width).swapaxes(-3, -2).reshape(*lead, -1)


def kernel(x, c, ctx, c_ctx, w_ada, b_ada, g_mix_pre, g_mix_post, g_ffn_pre, g_ffn_post, w_in_even, w_pool, pool_scale, attn_sink, w_out_even, w_in_odd, sgu_ln_g, sgu_ln_b, sgu_w, sgu_b, w_out_odd, w_ffn_up, ffn_conv_w, ffn_conv_b, w_ffn_down, loss_target, m_c_ctx, m_w_ada, m_b_ada, m_g_mix_pre, m_g_mix_post, m_g_ffn_pre, m_g_ffn_post, m_w_in_even, m_w_pool, m_pool_scale, m_attn_sink, m_w_out_even, m_w_in_odd, m_sgu_ln_g, m_sgu_ln_b, m_sgu_w, m_sgu_b, m_w_out_odd, m_w_ffn_up, m_ffn_conv_w, m_ffn_conv_b, m_w_ffn_down, v_c_ctx, v_w_ada, v_b_ada, v_g_mix_pre, v_g_mix_post, v_g_ffn_pre, v_g_ffn_post, v_w_in_even, v_w_pool, v_pool_scale, v_attn_sink, v_w_out_even, v_w_in_odd, v_sgu_ln_g, v_sgu_ln_b, v_sgu_w, v_sgu_b, v_w_out_odd, v_w_ffn_up, v_ffn_conv_w, v_ffn_conv_b, v_w_ffn_down):
    P = dict(c_ctx=c_ctx, w_ada=w_ada, b_ada=b_ada, g_mix_pre=g_mix_pre, g_mix_post=g_mix_post, g_ffn_pre=g_ffn_pre,
             g_ffn_post=g_ffn_post, w_in_even=w_in_even, w_pool=w_pool, pool_scale=pool_scale, attn_sink=attn_sink,
             w_out_even=w_out_even, w_in_odd=w_in_odd, sgu_ln_g=sgu_ln_g, sgu_ln_b=sgu_ln_b, sgu_w=sgu_w, sgu_b=sgu_b,
             w_out_odd=w_out_odd, w_ffn_up=w_ffn_up, ffn_conv_w=ffn_conv_w, ffn_conv_b=ffn_conv_b, w_ffn_down=w_ffn_down)
    M = dict(c_ctx=m_c_ctx, w_ada=m_w_ada, b_ada=m_b_ada, g_mix_pre=m_g_mix_pre, g_mix_post=m_g_mix_post, g_ffn_pre=m_g_ffn_pre,
             g_ffn_post=m_g_ffn_post, w_in_even=m_w_in_even, w_pool=m_w_pool, pool_scale=m_pool_scale, attn_sink=m_attn_sink,
             w_out_even=m_w_out_even, w_in_odd=m_w_in_odd, sgu_ln_g=m_sgu_ln_g, sgu_ln_b=m_sgu_ln_b, sgu_w=m_sgu_w, sgu_b=m_sgu_b,
             w_out_odd=m_w_out_odd, w_ffn_up=m_w_ffn_up, ffn_conv_w=m_ffn_conv_w, ffn_conv_b=m_ffn_conv_b, w_ffn_down=m_w_ffn_down)
    V = dict(c_ctx=v_c_ctx, w_ada=v_w_ada, b_ada=v_b_ada, g_mix_pre=v_g_mix_pre, g_mix_post=v_g_mix_post, g_ffn_pre=v_g_ffn_pre,
             g_ffn_post=v_g_ffn_post, w_in_even=v_w_in_even, w_pool=v_w_pool, pool_scale=v_pool_scale, attn_sink=v_attn_sink,
             w_out_even=v_w_out_even, w_in_odd=v_w_in_odd, sgu_ln_g=v_sgu_ln_g, sgu_ln_b=v_sgu_ln_b, sgu_w=v_sgu_w, sgu_b=v_sgu_b,
             w_out_odd=v_w_out_odd, w_ffn_up=v_w_ffn_up, ffn_conv_w=v_ffn_conv_w, ffn_conv_b=v_ffn_conv_b, w_ffn_down=v_w_ffn_down)

    x = x[0]
    ctx = ctx[0]
    target = loss_target[0]
    L, D = x.shape
    C = ctx.shape[0]
    tm = min(512, L)
    conv_rows = min(256, L)
    me = 4 * lax.axis_index("x") + 2 * lax.axis_index("y") + lax.axis_index("c")
    n_ada = w_ada.shape[2]
    F = w_ffn_down.shape[1] * N_DEV
    hb = F // (2 * LANES)

    n_cw = ffn_conv_w.shape[2]
    small = jnp.concatenate([c.reshape(-1, LANES), sgu_ln_g, sgu_ln_b, _rows128(ffn_conv_w)], axis=0)
    n_small = small.shape[0]
    small_all = all_gather_small(_pad_rows(small), name="gather_small_inputs")[:, :n_small]
    c_all = small_all[:, :8].reshape(N_DEV, D)
    ln_g = small_all[:, 8].reshape(1, D)
    ln_b = small_all[:, 9].reshape(1, D)
    conv_w = small_all[:, 10:].reshape(N_DEV, -1)[:, :2 * 3 * n_cw].reshape(N_DEV, 2, 3, n_cw)
    conv_w = conv_w.transpose(1, 2, 0, 3).reshape(2, 3, 2 * F)

    cs = jnp.concatenate([c_all, c_ctx[None, :], jnp.zeros((7, D), F32)], axis=0)
    b_loc = lax.dynamic_slice(b_ada, (0, me * n_ada), (2, n_ada))
    silu_c, mods_loc = ada_fwd_mm(cs, w_ada, b_loc, name="ada_fwd")
    mods_all = all_gather_small(mods_loc.reshape(-1, LANES), name="gather_mods").reshape(N_DEV, 2, 16, n_ada)
    mods_all = mods_all.transpose(1, 2, 0, 3).reshape(2, 16, 6 * D)
    mod = lambda i, row: [m_[None, :] for m_ in jnp.split(lax.dynamic_index_in_dim(mods_all[i], row, 0, False), 6)]
    sh_m, sc_m, gt_m, sh_f, sc_f, gt_f = zip(mod(0, me), mod(1, me))
    csh_m, csc_m = mod(0, N_DEV)[:2]

    shards = [w_in_even[0].T, w_out_even[0], w_in_odd[0].T, w_out_odd[0], w_ffn_up[0].T, w_ffn_down[0],
              w_ffn_up[1].T, w_ffn_down[1]]
    win_e, wout_e, win_o, wout_o, wup0, wdn0, wup1, wdn1 = all_gather_weights(
        [s.astype(BF16) for s in shards], name="gather_weights")
    wup, wdn = (wup0, wup1), (wdn0, wdn1)
    cw = _interleave(conv_w, hb, 2 * LANES)
    cb = _interleave(ffn_conv_b, hb, 2 * LANES)
    row = lambda a, i: a[i][None, :]

    cos, sa, sb = _rope_tables(L)
    sink = attn_sink[0]
    bst = sgu_b[0].T

    def ffn_fwd(i, xin):
        h, hu = pre_mm(xin, row(g_ffn_pre, i), sh_f[i], sc_f[i], wup[i], tm=tm, tn=4 * LANES, pair=True, name=f"ffn_up_{i}")
        a = conv_fwd(hu, cw[i], cb[i][None, :], rows=conv_rows, name=f"ffn_conv_{i}")
        f, xo = mm_post(a, wdn[i], xin, row(g_ffn_post, i), gt_f[i], tm=tm, name=f"ffn_down_{i}")
        return h, hu, a, f, xo

    h0, u, q, kv = inproj_even(x, row(g_mix_pre, 0), sh_m[0], sc_m[0], win_e, cos, sa, sb, tm=tm, name="in_even")
    hc, kvc = pre_mm(ctx, row(g_mix_pre, 0), csh_m, csc_m, win_e, tm=C, tn=2 * LANES, w_row_off=8 * LANES, name="in_even_ctx")
    pa = jnp.concatenate([pool_fwd(u, w_pool[0], pool_scale, name="pool_fwd"),
                          attn_fwd(q, kv, kvc, sink, name="attn_fwd")], axis=1)
    y0, x1 = mm_post(pa, wout_e, x, row(g_mix_post, 0), gt_m[0], tm=tm, name="out_even")
    h1, hu0, a0, f0, x2 = ffn_fwd(0, x1)
    h2, z1 = pre_mm(x2, row(g_mix_pre, 1), sh_m[1], sc_m[1], win_o, tm=tm, tn=4 * LANES, name="in_odd")
    us = sgu_fwd(z1, ln_g, ln_b, sgu_w[0], bst, name="sgu_fwd")
    y1, x3 = mm_post(us, wout_o, x2, row(g_mix_post, 1), gt_m[1], tm=tm, name="out_odd")
    h3, hu1, a1, f1, x4 = ffn_fwd(1, x3)
    loss_part, dx4 = loss_grad(x4, target, tm=tm, name="loss")
    loss = lax.psum(loss_part[0, 0], ("x", "y", "c"))

    up_rows = lambda r: jnp.where(r < hb, 2 * r, 2 * (r - hb) + 1)

    def ffn_bwd(i, dxo, xin, h, hu, a, f):
        dyf, da, dg_post, dgt = post_bwd_mm(dxo, f, row(g_ffn_post, i), gt_f[i], wdn[i], tm=tm, name=f"ffn_down_bwd_{i}")
        dhu, dcw, dcb = conv_bwd(da, hu, cw[i], cb[i][None, :], rows=conv_rows, name=f"ffn_conv_bwd_{i}")
        dxin, dg_pre, dsh, dsc = mm_pre_bwd(dhu, wup[i], xin, dxo, row(g_ffn_pre, i), sc_f[i], tm=tm, tk=4 * LANES, pair=True,
                                            name=f"ffn_up_bwd_{i}")
        g_dn = wgrad(a, dyf, tr=2 * LANES, name=f"wgrad_down_{i}")
        g_up = wgrad(dhu, h, tr=2 * LANES, a_index=up_rows, name=f"wgrad_up_{i}")
        return dxin, g_up, g_dn, dict(g_ffn_post=dg_post, g_ffn_pre=dg_pre, gt_f=dgt, sh_f=dsh, sc_f=dsc,
                                      ffn_conv_w=_deinterleave(dcw, hb, 2 * LANES), ffn_conv_b=_deinterleave(dcb, hb, 2 * LANES)[0])

    dx3, g_up1, g_dn1, sf1 = ffn_bwd(1, dx4, x3, h3, hu1, a1, f1)
    dy1, dus, dg_mpost1, dgt_m1 = post_bwd_mm(dx3, y1, row(g_mix_post, 1), gt_m[1], wout_o, tm=tm, name="out_odd_bwd")
    dz1, dws, dbs, dlng, dlnb = sgu_bwd(z1, dus, ln_g, ln_b, sgu_w[0], bst, name="sgu_bwd")
    dx2, dg_mpre1, dsh_m1, dsc_m1 = mm_pre_bwd(dz1, win_o, x2, dx3, row(g_mix_pre, 1), sc_m[1], tm=tm, tk=4 * LANES, name="in_odd_bwd")
    g_out_o = wgrad(us, dy1, tr=2 * LANES, name="wgrad_out_odd")
    g_in_o = wgrad(dz1, h2, tr=2 * LANES, name="wgrad_in_odd")

    dx1, g_up0, g_dn0, sf0 = ffn_bwd(0, dx2, x1, h1, hu0, a0, f0)
    dy0, dpa, dg_mpost0, dgt_m0 = post_bwd_mm(dx1, y0, row(g_mix_post, 0), gt_m[0], wout_e, tm=tm, name="out_even_bwd")
    du, dwp, dps = pool_bwd(u, dpa, w_pool[0], pool_scale, name="pool_bwd")
    dq, dkv, dkvc, dsink = attn_bwd(q, kv, kvc, sink, dpa, cos, sa, sb, name="attn_bwd")
    dz0 = jnp.concatenate([du, dq, dkv], axis=1)
    grad_x, dg_mpre0, dsh_m0, dsc_m0 = mm_pre_bwd(dz0, win_e, x, dx1, row(g_mix_pre, 0), sc_m[0], tm=tm, tk=2 * LANES, name="in_even_bwd")
    _, dg_mpre0c, dcsh, dcsc = mm_pre_bwd(dkvc, win_e, ctx, None, row(g_mix_pre, 0), csc_m, tm=C, tk=2 * LANES,
                                          w_row_off=8 * LANES, name="in_even_ctx_bwd")
    g_out_e = wgrad(pa, dy0, tr=2 * LANES, name="wgrad_out_even")
    dzc = jnp.concatenate([jnp.zeros((C, 8 * LANES), BF16), dkvc], axis=1)
    g_in_e = wgrad(dz0, h0, tr=2 * LANES, extra=(dzc, hc), name="wgrad_in_even")

    slots = reduce_scatter_send([g_in_e, g_out_e, g_in_o, g_out_o, g_up0, g_dn0, g_up1, g_dn1], name="scatter_grads")
    out = {}

    def update(name, idx, gslots, transposed):
        w_, m_, v_ = (a[idx].T if transposed else a[idx] for a in (P[name], M[name], V[name]))
        r = w_.shape[0]
        tr = r // 4 if r % 64 == 0 and r > 256 else r
        res = adamw(w_, m_, v_, gslots, tr=tr, name=f"adamw_{name}_{idx}")
        for kind, val in zip(("grad", "delta", "new_m", "new_v"), res):
            out.setdefault((kind, name), []).append(val.T if transposed else val)

    update("w_in_even", 0, slots[0], True)
    update("w_out_even", 0, slots[1], False)
    update("w_in_odd", 0, slots[2], True)
    update("w_out_odd", 0, slots[3], False)
    update("w_ffn_up", 0, slots[4], True)
    update("w_ffn_down", 0, slots[5], False)
    update("w_ffn_up", 1, slots[6], True)
    update("w_ffn_down", 1, slots[7], False)

    zero = jnp.zeros((1, D), F32)
    dmod0 = jnp.concatenate([dsh_m0, dsc_m0, dgt_m0, sf0["sh_f"], sf0["sc_f"], sf0["gt_f"]], axis=1)
    dmodc = jnp.concatenate([dcsh, dcsc, zero, zero, zero, zero], axis=1)
    dmod1 = jnp.concatenate([dsh_m1, dsc_m1, dgt_m1, sf1["sh_f"], sf1["sc_f"], sf1["gt_f"]], axis=1)
    dmods = jnp.concatenate([dmod0, dmodc, dmod1], axis=0)
    dmods_all = all_gather_small(_pad_rows(dmods.reshape(-1, LANES)), name="gather_dmods")[:, :3 * 6 * D // LANES]
    dmods_all = dmods_all.reshape(N_DEV, 3, N_DEV, n_ada)
    dall = lax.dynamic_index_in_dim(dmods_all, me, 2, False)
    g_w_ada, dcc = ada_bwd_mm(silu_c, c_ctx[None, :], dall, w_ada, name="ada_bwd")
    nl = w_ada.shape[0]
    res = adamw(w_ada.reshape(nl * D, n_ada), m_w_ada.reshape(nl * D, n_ada), v_w_ada.reshape(nl * D, n_ada),
                g_w_ada.reshape(1, nl * D, n_ada), tr=nl * D // 8, name="adamw_w_ada")
    for kind, val in zip(("grad", "delta", "new_m", "new_v"), res):
        out[(kind, "w_ada")] = val.reshape(nl, D, n_ada)

    rep = dict(
        c_ctx=dcc[0],
        b_ada=jnp.stack([dmod0[0] + dmodc[0], dmod1[0]]),
        g_mix_pre=jnp.concatenate([dg_mpre0 + dg_mpre0c, dg_mpre1]),
        g_mix_post=jnp.concatenate([dg_mpost0, dg_mpost1]),
        g_ffn_pre=jnp.concatenate([sf0["g_ffn_pre"], sf1["g_ffn_pre"]]),
        g_ffn_post=jnp.concatenate([sf0["g_ffn_post"], sf1["g_ffn_post"]]),
        w_pool=dwp[None], pool_scale=dps, attn_sink=dsink[:, :N_Q_HEADS],
        sgu_w=dws[None], sgu_b=dbs[:, :sgu_b.shape[1]].T[None],
        ffn_conv_b=jnp.stack([sf0["ffn_conv_b"], sf1["ffn_conv_b"]]),
    )
    rep_names = list(rep)
    conv_g = jnp.stack([sf0["ffn_conv_w"], sf1["ffn_conv_w"]]).reshape(2, 3, N_DEV, n_cw).transpose(2, 0, 1, 3)
    shard_full = dict(sgu_ln_g=dlng.reshape(N_DEV, LANES), sgu_ln_b=dlnb.reshape(N_DEV, LANES),
                      ffn_conv_w=jnp.concatenate([_rows128(conv_g[d]) for d in range(N_DEV)], axis=0))
    pieces = [_rows128(rep[k]) for k in rep_names] + [shard_full[k] for k in shard_full]
    sizes = [p.shape[0] for p in pieces]
    gpack = _pad_rows(jnp.concatenate(pieces, axis=0))
    _, gsum = all_gather_small(gpack, reduce=True, name="allreduce_small_grads")
    offs = [sum(sizes[:i]) for i in range(len(sizes))]
    n_rep = len(rep_names)
    cw_rows = sizes[-1] // N_DEV
    g_own = [gsum[offs[i]:offs[i] + sizes[i]] for i in range(n_rep)]
    g_own.append(lax.dynamic_slice_in_dim(gsum, offs[n_rep] + me, 1, 0))
    g_own.append(lax.dynamic_slice_in_dim(gsum, offs[n_rep + 1] + me, 1, 0))
    g_own.append(lax.dynamic_slice_in_dim(gsum, offs[n_rep + 2] + me * cw_rows, cw_rows, 0))
    small_names = rep_names + list(shard_full)
    packs = [jnp.concatenate([_rows128(src[k]) for k in small_names], axis=0) for src in (P, M, V)]
    n_pack = packs[0].shape[0]
    gp = _pad_rows(jnp.concatenate(g_own, axis=0))[None]
    res = adamw(*[_pad_rows(p) for p in packs], gp, tr=gp.shape[1], name="adamw_small")
    o = 0
    for k in small_names:
        n = _rows128(P[k]).shape[0]
        for kind, val in zip(("grad", "delta", "new_m", "new_v"), res):
            out[(kind, k)] = val[o:o + n].reshape(-1)[:P[k].size].reshape(P[k].shape)
        o += n
    assert o == n_pack

    names = list(P)
    final = [loss, grad_x[None]]
    for kind in ("grad", "delta", "new_m", "new_v"):
        for k in names:
            val = out[(kind, k)]
            final.append(jnp.stack(val) if isinstance(val, list) else val)
    return tuple(final)
```

```python
import functools
import math

import jax
import jax.numpy as jnp
from jax import lax
from jax.experimental import pallas as pl
from jax.experimental.pallas import tpu as pltpu

F32 = jnp.float32
BF16 = jnp.bfloat16
MESH = pl.DeviceIdType.MESH
N_DEV = 8
LANES = 128
VMEM_LIMIT = 48 * 1024 * 1024
EPS = 1e-6
NEG_INF = -1e30
GRID_W = 64
WINDOW = 128
BLK = 128
HEAD_DIM = 64
N_Q_HEADS = 8
N_KV_HEADS = 2
GQA = N_Q_HEADS // N_KV_HEADS
POOL_WINDOWS = (2, 4, 8, 16)
ROPE_BASE = 10000.0
ROPE_FREQS = HEAD_DIM // 4
PAD = 16
ADAM_LR, ADAM_B1, ADAM_B2, ADAM_EPS, ADAM_WD, ADAM_STEP = 0.001, 0.9, 0.999, 1e-08, 0.01, 10
BC1 = 1.0 - ADAM_B1 ** ADAM_STEP
BC2 = 1.0 - ADAM_B2 ** ADAM_STEP
SQRT_2_OVER_PI = math.sqrt(2.0 / math.pi)
GELU_C = 0.044715


def _cp(sem=None):
    return pltpu.CompilerParams(dimension_semantics=sem, vmem_limit_bytes=VMEM_LIMIT)


def _dot(a, b):
    return jnp.dot(a, b, preferred_element_type=F32)


def _dot_nt(a, b):
    return lax.dot_general(a, b, (((1,), (1,)), ((), ())), preferred_element_type=F32)


def _dot_tn(a, b):
    return lax.dot_general(a, b, (((0,), (0,)), ((), ())), preferred_element_type=F32)


def _rms(x):
    r = lax.rsqrt(jnp.mean(x * x, axis=-1, keepdims=True) + EPS)
    return x * r, r


def _rms_bwd(dn, n, r):
    return r * (dn - n * jnp.mean(dn * n, axis=-1, keepdims=True))


def _colsum(a):
    return jnp.sum(a, axis=0, keepdims=True)


def _rope(x, c, sa, sb):
    return x * c + pltpu.roll(x, LANES - ROPE_FREQS, 1) * sa + pltpu.roll(x, ROPE_FREQS, 1) * sb


def _full(shape):
    return pl.BlockSpec(shape, lambda *_: (0,) * len(shape))


def pre_mm(x, g, sh, sc, wt, *, tm, tn, w_row_off=0, name):
    T, D = x.shape
    n_rows = wt.shape[0] - w_row_off
    off = w_row_off // tn

    def body(x_ref, g_ref, sh_ref, sc_ref, w_ref, h_ref, z_ref):
        @pl.when(pl.program_id(1) == 0)
        def _():
            n, _ = _rms(x_ref[...])
            h_ref[...] = (n * g_ref[...] * (1.0 + sc_ref[...]) + sh_ref[...]).astype(BF16)

        z_ref[...] = _dot_nt(h_ref[...], w_ref[...]).astype(BF16)

    vec = pl.BlockSpec((1, D), lambda i, j: (0, 0))
    return pl.pallas_call(
        body, name=name, grid=(T // tm, n_rows // tn),
        in_specs=[pl.BlockSpec((tm, D), lambda i, j: (i, 0)), vec, vec, vec, pl.BlockSpec((tn, D), lambda i, j: (j + off, 0))],
        out_specs=[pl.BlockSpec((tm, D), lambda i, j: (i, 0)), pl.BlockSpec((tm, tn), lambda i, j: (i, j))],
        out_shape=[jax.ShapeDtypeStruct((T, D), BF16), jax.ShapeDtypeStruct((T, n_rows), BF16)],
        compiler_params=_cp(("parallel", "arbitrary")),
    )(x, g, sh, sc, wt)


def inproj_even(x, g, sh, sc, wt, cos, sa, sb, *, tm, name):
    T, D = x.shape
    N = wt.shape[0]

    def body(x_ref, g_ref, sh_ref, sc_ref, w_ref, c_ref, sa_ref, sb_ref, h_ref, u_ref, q_ref, kv_ref):
        n, _ = _rms(x_ref[...])
        h = (n * g_ref[...] * (1.0 + sc_ref[...]) + sh_ref[...]).astype(BF16)
        h_ref[...] = h
        z = _dot_nt(h, w_ref[...])
        u_ref[...] = z[:, :4 * LANES]
        c, a, b = c_ref[...], sa_ref[...], sb_ref[...]
        for s in range(4):
            q_ref[:, s * LANES:(s + 1) * LANES] = _rope(z[:, (4 + s) * LANES:(5 + s) * LANES], c, a, b).astype(BF16)
        kv_ref[:, :LANES] = _rope(z[:, 8 * LANES:9 * LANES], c, a, b).astype(BF16)
        kv_ref[:, LANES:] = z[:, 9 * LANES:].astype(BF16)

    vec = pl.BlockSpec((1, D), lambda i: (0, 0))
    row = lambda w: pl.BlockSpec((tm, w), lambda i: (i, 0))
    return pl.pallas_call(
        body, name=name, grid=(T // tm,),
        in_specs=[row(D), vec, vec, vec, _full((N, D)), row(LANES), row(LANES), row(LANES)],
        out_specs=[row(D), row(4 * LANES), row(4 * LANES), row(2 * LANES)],
        out_shape=[jax.ShapeDtypeStruct((T, D), BF16), jax.ShapeDtypeStruct((T, 4 * LANES), F32),
                   jax.ShapeDtypeStruct((T, 4 * LANES), BF16), jax.ShapeDtypeStruct((T, 2 * LANES), BF16)],
        compiler_params=_cp(("parallel",)),
    )(x, g, sh, sc, wt, cos, sa, sb)


def mm_post(a, w, x, g, gt, *, tm, name):
    T, K = a.shape
    D = w.shape[1]

    def body(a_ref, w_ref, x_ref, g_ref, gt_ref, y_ref, xn_ref):
        y = _dot(a_ref[...], w_ref[...])
        n, _ = _rms(y)
        y_ref[...] = y
        xn_ref[...] = x_ref[...] + gt_ref[...] * (n * g_ref[...])

    vec = pl.BlockSpec((1, D), lambda i: (0, 0))
    row = lambda w_: pl.BlockSpec((tm, w_), lambda i: (i, 0))
    return pl.pallas_call(
        body, name=name, grid=(T // tm,),
        in_specs=[row(K), _full((K, D)), row(D), vec, vec],
        out_specs=[row(D), row(D)],
        out_shape=[jax.ShapeDtypeStruct((T, D), F32), jax.ShapeDtypeStruct((T, D), F32)],
        compiler_params=_cp(("parallel",)),
    )(a, w, x, g, gt)


def post_bwd_mm(dxn, y, g, gt, w, *, tm, name):
    T, D = y.shape
    K = w.shape[0]

    def body(dxn_ref, y_ref, g_ref, gt_ref, w_ref, dy_ref, da_ref, dg_ref, dgt_ref):
        @pl.when(pl.program_id(0) == 0)
        def _():
            dg_ref[...] = jnp.zeros_like(dg_ref)
            dgt_ref[...] = jnp.zeros_like(dgt_ref)

        d = dxn_ref[...]
        n, r = _rms(y_ref[...])
        g_, gt_ = g_ref[...], gt_ref[...]
        dg_ref[...] += _colsum(d * gt_ * n)
        dgt_ref[...] += _colsum(d * g_ * n)
        dy = _rms_bwd(d * (gt_ * g_), n, r).astype(BF16)
        dy_ref[...] = dy
        da_ref[...] = _dot_nt(dy, w_ref[...]).astype(BF16)

    vec = pl.BlockSpec((1, D), lambda i: (0, 0))
    row = lambda w_: pl.BlockSpec((tm, w_), lambda i: (i, 0))
    return pl.pallas_call(
        body, name=name, grid=(T // tm,),
        in_specs=[row(D), row(D), vec, vec, _full((K, D))],
        out_specs=[row(D), row(K), vec, vec],
        out_shape=[jax.ShapeDtypeStruct((T, D), BF16), jax.ShapeDtypeStruct((T, K), BF16),
                   jax.ShapeDtypeStruct((1, D), F32), jax.ShapeDtypeStruct((1, D), F32)],
        compiler_params=_cp(("arbitrary",)),
    )(dxn, y, g, gt, w)


def mm_pre_bwd(dzs, wt, x, dres, g, sc, *, tm, tk, w_row_off=0, name):
    T, N = dzs[0].shape
    D = x.shape[1]
    nk = N // tk
    npart = len(dzs)
    off = w_row_off // tk
    has_res = dres is not None

    def body(*refs):
        dz_refs = refs[:npart]
        w_refs = refs[npart:2 * npart]
        rest = refs[2 * npart:]
        x_ref = rest[0]
        dres_ref = rest[1] if has_res else None
        g_ref, sc_ref, dx_ref, dg_ref, dsh_ref, dsc_ref, acc = rest[1 + has_res:]
        i, k = pl.program_id(0), pl.program_id(1)

        @pl.when(jnp.logical_and(i == 0, k == 0))
        def _():
            dg_ref[...] = jnp.zeros_like(dg_ref)
            dsh_ref[...] = jnp.zeros_like(dsh_ref)
            dsc_ref[...] = jnp.zeros_like(dsc_ref)

        part = _dot(dz_refs[0][...], w_refs[0][...])
        for p in range(1, npart):
            part = part + _dot(dz_refs[p][...], w_refs[p][...])

        @pl.when(k == 0)
        def _():
            acc[...] = part

        @pl.when(k > 0)
        def _():
            acc[...] += part

        @pl.when(k == nk - 1)
        def _():
            dh = acc[...]
            n, r = _rms(x_ref[...])
            g_, s1 = g_ref[...], 1.0 + sc_ref[...]
            dsh_ref[...] += _colsum(dh)
            dsc_ref[...] += _colsum(dh * n * g_)
            dg_ref[...] += _colsum(dh * s1 * n)
            dxp = _rms_bwd(dh * (g_ * s1), n, r)
            dx_ref[...] = dxp + dres_ref[...] if has_res else dxp

    vec = pl.BlockSpec((1, D), lambda i, k: (0, 0))
    row = pl.BlockSpec((tm, D), lambda i, k: (i, 0))
    w_specs = [pl.BlockSpec((tk, D), (lambda i, k, p=p: (k + off + p * nk, 0))) for p in range(npart)]
    res_specs, res_args = ([row], (dres,)) if has_res else ([], ())
    return pl.pallas_call(
        body, name=name, grid=(T // tm, nk),
        in_specs=[pl.BlockSpec((tm, tk), lambda i, k: (i, k))] * npart + w_specs + [row] + res_specs + [vec, vec],
        out_specs=[row, vec, vec, vec],
        out_shape=[jax.ShapeDtypeStruct((T, D), F32)] + [jax.ShapeDtypeStruct((1, D), F32)] * 3,
        scratch_shapes=[pltpu.VMEM((tm, D), F32)],
        compiler_params=_cp(("arbitrary", "arbitrary")),
    )(*dzs, *([wt] * npart), x, *res_args, g, sc)


def wgrad(a_parts, b, *, tr, extra=None, name):
    T, R = a_parts[0].shape
    D = b.shape[1]
    npart = len(a_parts)
    nr = R // tr

    def body(*refs):
        a_refs, b_ref = refs[:npart], refs[npart]
        g_ref = refs[-1]
        for p in range(npart):
            @pl.when(pl.program_id(0) // nr == p)
            def _():
                acc = _dot_tn(a_refs[p][...], b_ref[...])
                if extra is not None:
                    acc += _dot_tn(refs[npart + 1][...], refs[npart + 2][...])
                g_ref[...] = acc.astype(BF16)

    in_specs = [pl.BlockSpec((T, tr), (lambda r, p=p: (0, jnp.clip(r - p * nr, 0, nr - 1)))) for p in range(npart)]
    in_specs.append(_full((T, D)))
    args = [*a_parts, b]
    if extra is not None:
        a2, b2 = extra
        in_specs += [pl.BlockSpec((a2.shape[0], tr), lambda r: (0, r)), _full(b2.shape)]
        args += [a2, b2]
    return pl.pallas_call(
        body, name=name, grid=(npart * nr,),
        in_specs=in_specs, out_specs=pl.BlockSpec((tr, D), lambda r: (r, 0)),
        out_shape=jax.ShapeDtypeStruct((npart * R, D), BF16),
        compiler_params=_cp(("parallel",)),
    )(*args)


def _conv_ext(ref, r0, rows, total):
    top = ref[pl.ds(pl.multiple_of(jnp.maximum(r0 - PAD, 0), PAD), PAD), :]
    mid = ref[pl.ds(r0, rows), :]
    bot = ref[pl.ds(pl.multiple_of(jnp.minimum(r0 + rows, total - PAD), PAD), PAD), :]
    ext = jnp.concatenate([top, mid, bot], axis=0).astype(F32)
    t = r0 - PAD + lax.broadcasted_iota(jnp.int32, (rows + 2 * PAD, 1), 0)
    return jnp.where(jnp.logical_and(t >= 0, t < total), ext, 0.0)


def _shift_rows(a, k):
    return pltpu.roll(a, k % a.shape[0], 0)


def _conv3(x, w, b):
    return w[0:1] * _shift_rows(x, 1) + w[1:2] * x + w[2:3] * _shift_rows(x, -1) + b


def _gate_up_specs(rows_, wblk, nb):
    return [pl.BlockSpec((rows_, wblk), lambda j: (0, j)), pl.BlockSpec((rows_, wblk), lambda j: (0, j + nb))]


def conv_fwd(hu, cw, cb, *, rows, wblk, name):
    L, N2 = hu.shape
    nb = N2 // 2 // wblk
    nchunk = L // rows

    def body(hg_ref, hu_ref, wg_ref, wu_ref, bg_ref, bu_ref, a_ref):
        def chunk(ci, carry):
            r0 = pl.multiple_of(ci * rows, rows)
            gate = _conv3(_conv_ext(hg_ref, r0, rows, L), wg_ref[...], bg_ref[...])[PAD:PAD + rows]
            up = _conv3(_conv_ext(hu_ref, r0, rows, L), wu_ref[...], bu_ref[...])[PAD:PAD + rows]
            a_ref[pl.ds(r0, rows), :] = (gate * jax.nn.sigmoid(gate) * up).astype(BF16)
            return carry

        lax.fori_loop(0, nchunk, chunk, 0)

    return pl.pallas_call(
        body, name=name, grid=(nb,),
        in_specs=_gate_up_specs(L, wblk, nb) + _gate_up_specs(3, wblk, nb) + _gate_up_specs(1, wblk, nb),
        out_specs=pl.BlockSpec((L, wblk), lambda j: (0, j)),
        out_shape=jax.ShapeDtypeStruct((L, N2 // 2), BF16),
        compiler_params=_cp(("parallel",)),
    )(hu, hu, cw, cw, cb, cb)


def conv_bwd(da, hu, cw, cb, *, rows, wblk, name):
    L, N2 = hu.shape
    F = N2 // 2
    nb = F // wblk
    nchunk = L // rows
    mid = slice(PAD, PAD + rows)

    def body(da_ref, hg_ref, hu_ref, wg_ref, wu_ref, bg_ref, bu_ref, dg_ref, du_ref, dwg_ref, dwu_ref, dbg_ref, dbu_ref):
        for ref in (dwg_ref, dwu_ref, dbg_ref, dbu_ref):
            ref[...] = jnp.zeros_like(ref)

        def half_bwd(x, dh, w_ref, dx_ref, dw_ref, db_ref, r0):
            w = w_ref[...]
            dx = w[0:1] * _shift_rows(dh, -1) + w[1:2] * dh + w[2:3] * _shift_rows(dh, 1)
            dx_ref[pl.ds(r0, rows), :] = dx[mid].astype(BF16)
            dhm = dh[mid]
            db_ref[...] += _colsum(dhm)
            dw_ref[0:1, :] += _colsum(dhm * _shift_rows(x, 1)[mid])
            dw_ref[1:2, :] += _colsum(dhm * x[mid])
            dw_ref[2:3, :] += _colsum(dhm * _shift_rows(x, -1)[mid])

        def chunk(ci, carry):
            r0 = pl.multiple_of(ci * rows, rows)
            xg = _conv_ext(hg_ref, r0, rows, L)
            xu = _conv_ext(hu_ref, r0, rows, L)
            d = _conv_ext(da_ref, r0, rows, L)
            gate = _conv3(xg, wg_ref[...], bg_ref[...])
            up = _conv3(xu, wu_ref[...], bu_ref[...])
            sg = jax.nn.sigmoid(gate)
            half_bwd(xu, d * (gate * sg), wu_ref, du_ref, dwu_ref, dbu_ref, r0)
            half_bwd(xg, d * up * (sg * (1.0 + gate * (1.0 - sg))), wg_ref, dg_ref, dwg_ref, dbg_ref, r0)
            return carry

        lax.fori_loop(0, nchunk, chunk, 0)

    blk = lambda r: pl.BlockSpec((r, wblk), lambda j: (0, j))
    return pl.pallas_call(
        body, name=name, grid=(nb,),
        in_specs=[blk(L)] + _gate_up_specs(L, wblk, nb) + _gate_up_specs(3, wblk, nb) + _gate_up_specs(1, wblk, nb),
        out_specs=[blk(L), blk(L), blk(3), blk(3), blk(1), blk(1)],
        out_shape=[jax.ShapeDtypeStruct((L, F), BF16)] * 2 + [jax.ShapeDtypeStruct((3, F), F32)] * 2
        + [jax.ShapeDtypeStruct((1, F), F32)] * 2,
        compiler_params=_cp(("parallel",)),
    )(da, hu, hu, cw, cw, cb, cb)


def _window_sums(pad_ref, w, lead):
    a = pad_ref[...]
    k = 1
    while k < w:
        a = a + _shift_rows(a, -k)
        k *= 2
    return _shift_rows(a, lead) if lead else a


def _pool_counts(L, h):
    t = lax.broadcasted_iota(jnp.int32, (L, 1), 0)
    return (jnp.minimum(t + h, L) - jnp.maximum(t - h, 0)).astype(F32)


def _pooled(u_ref, pad_ref, L, w):
    h = w // 2
    pad_ref[pl.ds(PAD, L), :] = u_ref[...]
    win = _window_sums(pad_ref, w, h)[PAD:PAD + L]
    return win / _pool_counts(L, h) - u_ref[...]


def _zero_pad_edges(pad_ref, L):
    z = jnp.zeros((PAD, LANES), F32)
    pad_ref[pl.ds(0, PAD), :] = z
    pad_ref[pl.ds(PAD + L, PAD), :] = z


def pool_fwd(u, w_pool, pool_scale, *, name):
    L = u.shape[0]

    def body(u_ref, w_ref, ps_ref, p_ref, pad_ref):
        _zero_pad_edges(pad_ref, L)
        for gi, win in enumerate(POOL_WINDOWS):
            @pl.when(pl.program_id(0) == gi)
            def _():
                pooled = _pooled(u_ref, pad_ref, L, win)
                p_ref[...] = (_dot(pooled.astype(BF16), w_ref[...].astype(BF16)) * ps_ref[...]).astype(BF16)

    return pl.pallas_call(
        body, name=name, grid=(len(POOL_WINDOWS),),
        in_specs=[pl.BlockSpec((L, LANES), lambda gi: (0, gi)), pl.BlockSpec((None, LANES, LANES), lambda gi: (gi, 0, 0)),
                  pl.BlockSpec((1, LANES), lambda gi: (0, gi))],
        out_specs=pl.BlockSpec((L, LANES), lambda gi: (0, gi)),
        out_shape=jax.ShapeDtypeStruct((L, 4 * LANES), BF16),
        scratch_shapes=[pltpu.VMEM((L + 2 * PAD, LANES), F32)],
        compiler_params=_cp(("parallel",)),
    )(u, w_pool, pool_scale)


def pool_bwd(u, dpa, w_pool, pool_scale, *, name):
    L = u.shape[0]

    def body(u_ref, dp_ref, w_ref, ps_ref, du_ref, dw_ref, dps_ref, pad_ref):
        _zero_pad_edges(pad_ref, L)
        for gi, win in enumerate(POOL_WINDOWS):
            @pl.when(pl.program_id(0) == gi)
            def _():
                h = win // 2
                wb = w_ref[...].astype(BF16)
                pooled = _pooled(u_ref, pad_ref, L, win).astype(BF16)
                dp = dp_ref[...].astype(F32)
                dps_ref[...] = _colsum(dp * _dot(pooled, wb))
                dy = (dp * ps_ref[...]).astype(BF16)
                dw_ref[...] = _dot_tn(pooled, dy)
                dpooled = _dot_nt(dy, wb)
                pad_ref[pl.ds(PAD, L), :] = dpooled / _pool_counts(L, h)
                du_ref[...] = (_window_sums(pad_ref, win, h - 1)[PAD:PAD + L] - dpooled).astype(BF16)

    return pl.pallas_call(
        body, name=name, grid=(len(POOL_WINDOWS),),
        in_specs=[pl.BlockSpec((L, LANES), lambda gi: (0, gi)), pl.BlockSpec((L, LANES), lambda gi: (0, gi)),
                  pl.BlockSpec((None, LANES, LANES), lambda gi: (gi, 0, 0)), pl.BlockSpec((1, LANES), lambda gi: (0, gi))],
        out_specs=[pl.BlockSpec((L, LANES), lambda gi: (0, gi)), pl.BlockSpec((None, LANES, LANES), lambda gi: (gi, 0, 0)),
                   pl.BlockSpec((1, LANES), lambda gi: (0, gi))],
        out_shape=[jax.ShapeDtypeStruct((L, 4 * LANES), BF16), jax.ShapeDtypeStruct((4, LANES, LANES), F32),
                   jax.ShapeDtypeStruct((1, 4 * LANES), F32)],
        scratch_shapes=[pltpu.VMEM((L + 2 * PAD, LANES), F32)],
        compiler_params=_cp(("parallel",)),
    )(u, dpa, w_pool, pool_scale)


def _attn_probs(q4, band, kvc, sink_ref, kh, mask4):
    scale = HEAD_DIM ** -0.5
    ks = slice(kh * HEAD_DIM, (kh + 1) * HEAD_DIM)
    s_loc = jnp.where(mask4, _dot_nt(q4, band[:, ks]) * scale, NEG_INF)
    s_ctx = _dot_nt(q4, kvc[:, ks]) * scale
    sk = jnp.concatenate([jnp.full((BLK, 1), sink_ref[kh * GQA + hh], F32) for hh in range(GQA)], axis=0)
    m = jnp.maximum(jnp.maximum(jnp.max(s_loc, axis=-1, keepdims=True), jnp.max(s_ctx, axis=-1, keepdims=True)), sk)
    e_loc, e_ctx, e_s = jnp.exp(s_loc - m), jnp.exp(s_ctx - m), jnp.exp(sk - m)
    inv = 1.0 / (jnp.sum(e_loc, axis=-1, keepdims=True) + jnp.sum(e_ctx, axis=-1, keepdims=True) + e_s)
    return e_loc * inv, e_ctx * inv, e_s * inv


def _attn_block(n, L):
    start = pl.multiple_of(jnp.clip((n - 1) * BLK, 0, L - 3 * BLK), BLK)
    qpos = n * BLK + lax.broadcasted_iota(jnp.int32, (BLK, 3 * BLK), 0)
    kpos = start + lax.broadcasted_iota(jnp.int32, (BLK, 3 * BLK), 1)
    mask = jnp.abs(kpos - qpos) <= WINDOW
    return start, jnp.concatenate([mask] * GQA, axis=0)


def _stack_heads(ref, kh):
    return jnp.concatenate([ref[:, (kh * GQA + hh) * HEAD_DIM:(kh * GQA + hh + 1) * HEAD_DIM] for hh in range(GQA)], axis=0)


def attn_fwd(q, kv, kvc, sink, *, name):
    L = q.shape[0]
    C = kvc.shape[0]

    def body(q_ref, kv_ref, kvc_ref, sink_ref, o_ref, o_scr):
        start, mask4 = _attn_block(pl.program_id(0), L)
        band = kv_ref[pl.ds(start, 3 * BLK), :]
        kvc_ = kvc_ref[...]
        for kh in range(N_KV_HEADS):
            q4 = _stack_heads(q_ref, kh)
            p_loc, p_ctx, _ = _attn_probs(q4, band, kvc_, sink_ref, kh, mask4)
            vs = slice(2 * HEAD_DIM + kh * HEAD_DIM, 2 * HEAD_DIM + (kh + 1) * HEAD_DIM)
            o4 = _dot(p_loc.astype(BF16), band[:, vs]) + _dot(p_ctx.astype(BF16), kvc_[:, vs])
            for hh in range(GQA):
                h = kh * GQA + hh
                o_scr[:, h * HEAD_DIM:(h + 1) * HEAD_DIM] = o4[hh * BLK:(hh + 1) * BLK]
        o_ref[...] = o_scr[...].astype(BF16)

    return pl.pallas_call(
        body, name=name, grid=(L // BLK,),
        in_specs=[pl.BlockSpec((BLK, 4 * LANES), lambda n: (n, 0)), _full((L, 2 * LANES)), _full((C, 2 * LANES)),
                  pl.BlockSpec(memory_space=pltpu.SMEM)],
        out_specs=pl.BlockSpec((BLK, 4 * LANES), lambda n: (n, 0)),
        out_shape=jax.ShapeDtypeStruct((L, 4 * LANES), BF16),
        scratch_shapes=[pltpu.VMEM((BLK, 4 * LANES), F32)],
        compiler_params=_cp(("parallel",)),
    )(q, kv, kvc, sink)


def attn_bwd(q, kv, kvc, sink, dpa, cos, sa, sb, *, name):
    L = q.shape[0]
    C = kvc.shape[0]
    nb = L // BLK
    scale = HEAD_DIM ** -0.5

    def body(q_ref, kv_ref, kvc_ref, sink_ref, do_ref, c_ref, sa_ref, sb_ref, cq_ref, saq_ref, sbq_ref,
             dq_ref, dkv_ref, dkvc_ref, dsink_ref, dkv_acc, dkvc_acc, dq_scr, band_scr, ctx_scr):
        n = pl.program_id(0)

        @pl.when(n == 0)
        def _():
            dkv_acc[...] = jnp.zeros_like(dkv_acc)
            dkvc_acc[...] = jnp.zeros_like(dkvc_acc)
            dsink_ref[...] = jnp.zeros_like(dsink_ref)

        start, mask4 = _attn_block(n, L)
        band = kv_ref[pl.ds(start, 3 * BLK), :]
        kvc_ = kvc_ref[...]
        lane = lax.broadcasted_iota(jnp.int32, (1, LANES), 1)
        dsink = jnp.zeros((1, LANES), F32)
        for kh in range(N_KV_HEADS):
            ks = slice(kh * HEAD_DIM, (kh + 1) * HEAD_DIM)
            vs = slice(2 * HEAD_DIM + kh * HEAD_DIM, 2 * HEAD_DIM + (kh + 1) * HEAD_DIM)
            q4 = _stack_heads(q_ref, kh)
            do4 = _stack_heads(do_ref, kh)
            p_loc, p_ctx, p_s = _attn_probs(q4, band, kvc_, sink_ref, kh, mask4)
            dp_loc = _dot_nt(do4, band[:, vs])
            dp_ctx = _dot_nt(do4, kvc_[:, vs])
            delta = jnp.sum(p_loc * dp_loc, axis=-1, keepdims=True) + jnp.sum(p_ctx * dp_ctx, axis=-1, keepdims=True)
            ds_loc = (p_loc * (dp_loc - delta) * scale).astype(BF16)
            ds_ctx = (p_ctx * (dp_ctx - delta) * scale).astype(BF16)
            dsk = p_s * delta
            for hh in range(GQA):
                h = kh * GQA + hh
                dsink = dsink - jnp.where(lane == h, jnp.sum(dsk[hh * BLK:(hh + 1) * BLK], axis=0, keepdims=True), 0.0)
            dq4 = _dot(ds_loc, band[:, ks]) + _dot(ds_ctx, kvc_[:, ks])
            for hh in range(GQA):
                h = kh * GQA + hh
                dq_scr[:, h * HEAD_DIM:(h + 1) * HEAD_DIM] = dq4[hh * BLK:(hh + 1) * BLK]
            band_scr[:, ks] = _dot_tn(ds_loc, q4)
            band_scr[:, vs] = _dot_tn(p_loc.astype(BF16), do4)
            ctx_scr[:, ks] = _dot_tn(ds_ctx, q4)
            ctx_scr[:, vs] = _dot_tn(p_ctx.astype(BF16), do4)
        dsink_ref[...] += dsink
        dkv_acc[pl.ds(start, 3 * BLK), :] += band_scr[...]
        dkvc_acc[...] += ctx_scr[...]
        c, a, b = cq_ref[...], -saq_ref[...], -sbq_ref[...]
        for s in range(4):
            dq_ref[:, s * LANES:(s + 1) * LANES] = _rope(dq_scr[:, s * LANES:(s + 1) * LANES], c, a, b).astype(BF16)

        @pl.when(n == nb - 1)
        def _():
            dkv_ref[:, :LANES] = _rope(dkv_acc[:, :LANES], c_ref[...], -sa_ref[...], -sb_ref[...]).astype(BF16)
            dkv_ref[:, LANES:] = dkv_acc[:, LANES:].astype(BF16)
            dkvc_ref[...] = dkvc_acc[...].astype(BF16)

    blk = lambda w: pl.BlockSpec((BLK, w), lambda n: (n, 0))
    return pl.pallas_call(
        body, name=name, grid=(nb,),
        in_specs=[blk(4 * LANES), _full((L, 2 * LANES)), _full((C, 2 * LANES)), pl.BlockSpec(memory_space=pltpu.SMEM),
                  pl.BlockSpec((BLK, 4 * LANES), lambda n: (n, 1)),
                  _full((L, LANES)), _full((L, LANES)), _full((L, LANES)), blk(LANES), blk(LANES), blk(LANES)],
        out_specs=[blk(4 * LANES), _full((L, 2 * LANES)), _full((C, 2 * LANES)), _full((1, LANES))],
        out_shape=[jax.ShapeDtypeStruct((L, 4 * LANES), BF16), jax.ShapeDtypeStruct((L, 2 * LANES), BF16),
                   jax.ShapeDtypeStruct((C, 2 * LANES), BF16), jax.ShapeDtypeStruct((1, LANES), F32)],
        scratch_shapes=[pltpu.VMEM((L, 2 * LANES), F32), pltpu.VMEM((C, 2 * LANES), F32), pltpu.VMEM((BLK, 4 * LANES), F32),
                        pltpu.VMEM((3 * BLK, 2 * LANES), F32), pltpu.VMEM((C, 2 * LANES), F32)],
        compiler_params=_cp(("arbitrary",)),
    )(q, kv, kvc, sink, dpa, cos, sa, sb, cos, sa, sb)


def _gelu_parts(x):
    th = jnp.tanh(SQRT_2_OVER_PI * (x + GELU_C * x * x * x))
    return 0.5 * x * (1.0 + th), th


def _gelu_grad(x, th):
    return 0.5 * (1.0 + th) + 0.5 * x * (1.0 - th * th) * SQRT_2_OVER_PI * (1.0 + 3.0 * GELU_C * x * x)


def _layernorm(v):
    mu = jnp.mean(v, axis=-1, keepdims=True)
    vc = v - mu
    rstd = lax.rsqrt(jnp.mean(vc * vc, axis=-1, keepdims=True) + EPS)
    return vc * rstd, rstd


def sgu_fwd(z1, ln_g, ln_b, ws, bst, *, name):
    L, W2 = z1.shape
    W = W2 // 2
    ng = W // LANES

    def body(z_ref, g_ref, b_ref, ws_ref, bs_ref, o_ref):
        z, _ = _gelu_parts(z_ref[...].astype(F32))
        xhat, _ = _layernorm(z[:, W:])
        vln = (xhat * g_ref[...] + b_ref[...]).astype(BF16)
        for gi in range(ng):
            cs = slice(gi * LANES, (gi + 1) * LANES)
            s = _dot(ws_ref[gi].astype(BF16), vln[:, cs]) + bs_ref[:, gi:gi + 1]
            o_ref[:, cs] = (z[:, cs] * s).astype(BF16)

    vec = _full((1, W))
    return pl.pallas_call(
        body, name=name, grid=(L // BLK,),
        in_specs=[pl.BlockSpec((BLK, W2), lambda n: (n, 0)), vec, vec, _full((ng, LANES, LANES)), _full((BLK, ng))],
        out_specs=pl.BlockSpec((BLK, W), lambda n: (n, 0)),
        out_shape=jax.ShapeDtypeStruct((L, W), BF16),
        compiler_params=_cp(("parallel",)),
    )(z1, ln_g, ln_b, ws, bst)


def sgu_bwd(z1, dus, ln_g, ln_b, ws, bst, *, name):
    L, W2 = z1.shape
    W = W2 // 2
    ng = W // LANES

    def body(z_ref, d_ref, g_ref, b_ref, ws_ref, bs_ref, dz_ref, dws_ref, dbs_ref, dg_ref, db_ref, dv_scr):
        @pl.when(pl.program_id(0) == 0)
        def _():
            dws_ref[...] = jnp.zeros_like(dws_ref)
            dbs_ref[...] = jnp.zeros_like(dbs_ref)
            dg_ref[...] = jnp.zeros_like(dg_ref)
            db_ref[...] = jnp.zeros_like(db_ref)

        zp = z_ref[...].astype(F32)
        z, th = _gelu_parts(zp)
        xhat, rstd = _layernorm(z[:, W:])
        vln = (xhat * g_ref[...] + b_ref[...]).astype(BF16)
        d = d_ref[...].astype(F32)
        lane = lax.broadcasted_iota(jnp.int32, (1, LANES), 1)
        dbs = jnp.zeros((BLK, LANES), F32)
        dgel = _gelu_grad(zp, th)
        for gi in range(ng):
            cs = slice(gi * LANES, (gi + 1) * LANES)
            wb = ws_ref[gi].astype(BF16)
            s = _dot(wb, vln[:, cs]) + bs_ref[:, gi:gi + 1]
            dz_ref[:, cs] = (d[:, cs] * s * dgel[:, cs]).astype(BF16)
            ds = d[:, cs] * z[:, cs]
            dbs = dbs + jnp.where(lane == gi, jnp.sum(ds, axis=-1, keepdims=True), 0.0)
            dsb = ds.astype(BF16)
            dws_ref[gi] += _dot_nt(dsb, vln[:, cs])
            dv_scr[:, cs] = _dot_tn(wb, dsb)
        dbs_ref[...] += dbs
        dvln = dv_scr[...]
        dg_ref[...] += _colsum(dvln * xhat)
        db_ref[...] += _colsum(dvln)
        dxh = dvln * g_ref[...]
        dv = rstd * (dxh - jnp.mean(dxh, axis=-1, keepdims=True) - xhat * jnp.mean(dxh * xhat, axis=-1, keepdims=True))
        dz_ref[:, W:] = (dv * dgel[:, W:]).astype(BF16)

    vec = _full((1, W))
    return pl.pallas_call(
        body, name=name, grid=(L // BLK,),
        in_specs=[pl.BlockSpec((BLK, W2), lambda n: (n, 0)), pl.BlockSpec((BLK, W), lambda n: (n, 0)), vec, vec,
                  _full((ng, LANES, LANES)), _full((BLK, ng))],
        out_specs=[pl.BlockSpec((BLK, W2), lambda n: (n, 0)), _full((ng, LANES, LANES)), _full((BLK, LANES)), vec, vec],
        out_shape=[jax.ShapeDtypeStruct((L, W2), BF16), jax.ShapeDtypeStruct((ng, LANES, LANES), F32),
                   jax.ShapeDtypeStruct((BLK, LANES), F32), jax.ShapeDtypeStruct((1, W), F32), jax.ShapeDtypeStruct((1, W), F32)],
        scratch_shapes=[pltpu.VMEM((BLK, W), F32)],
        compiler_params=_cp(("arbitrary",)),
    )(z1, dus, ln_g, ln_b, ws, bst)


def loss_grad(xo, target, *, tm, name):
    T, D = xo.shape

    def body(x_ref, t_ref, l_ref, d_ref):
        @pl.when(pl.program_id(0) == 0)
        def _():
            l_ref[...] = jnp.zeros_like(l_ref)

        e = x_ref[...] - t_ref[...]
        l_ref[...] += 0.5 * jnp.sum(jnp.mean(e * e, axis=-1, keepdims=True), axis=0, keepdims=True)
        d_ref[...] = e * (1.0 / D)

    row = pl.BlockSpec((tm, D), lambda i: (i, 0))
    return pl.pallas_call(
        body, name=name, grid=(T // tm,), in_specs=[row, row], out_specs=[_full((1, 1)), row],
        out_shape=[jax.ShapeDtypeStruct((1, 1), F32), jax.ShapeDtypeStruct((T, D), F32)],
        compiler_params=_cp(("arbitrary",)),
    )(xo, target)


def adamw(w, m, v, gparts, *, tr, name):
    R, Wd = w.shape
    S = gparts.shape[0]

    def body(w_ref, m_ref, v_ref, gp_ref, g_ref, d_ref, nm_ref, nv_ref):
        g = gp_ref[0].astype(F32)
        for s in range(1, S):
            g = g + gp_ref[s].astype(F32)
        m_ = ADAM_B1 * m_ref[...] + (1.0 - ADAM_B1) * g
        v_ = ADAM_B2 * v_ref[...] + (1.0 - ADAM_B2) * (g * g)
        g_ref[...] = g
        nm_ref[...] = m_
        nv_ref[...] = v_
        d_ref[...] = -ADAM_LR * ((m_ / BC1) / (jnp.sqrt(v_ / BC2) + ADAM_EPS) + ADAM_WD * w_ref[...])

    row = pl.BlockSpec((tr, Wd), lambda i: (i, 0))
    return pl.pallas_call(
        body, name=name, grid=(R // tr,),
        in_specs=[row, row, row, pl.BlockSpec((S, tr, Wd), lambda i: (0, i, 0))],
        out_specs=[row] * 4, out_shape=[jax.ShapeDtypeStruct((R, Wd), F32)] * 4,
        compiler_params=_cp(("parallel",)),
    )(w, m, v, gparts)


def ada_fwd_mm(cs, w_ada, b_loc, *, name):
    R, D = cs.shape
    nl, _, n = w_ada.shape

    def body(c_ref, w_ref, b_ref, s_ref, m_ref):
        c = c_ref[...]
        s = c * jax.nn.sigmoid(c)
        s_ref[...] = s
        for i in range(nl):
            m_ref[i] = _dot(s.astype(BF16), w_ref[i].astype(BF16)) + b_ref[i:i + 1, :]

    return pl.pallas_call(
        body, name=name, in_specs=[_full((R, D)), _full((nl, D, n)), _full((nl, n))],
        out_specs=[_full((R, D)), _full((nl, R, n))], grid=(1,),
        out_shape=[jax.ShapeDtypeStruct((R, D), F32), jax.ShapeDtypeStruct((nl, R, n), F32)],
        compiler_params=_cp(("arbitrary",)),
    )(cs, w_ada, b_loc)


def ada_bwd_mm(s, c_ctx, dall, w_ada, *, name):
    R, D = s.shape
    nl, _, n = w_ada.shape

    def body(s_ref, cc_ref, d_ref, w_ref, gw_ref, dcc_ref):
        sb = s_ref[...].astype(BF16)
        row = lax.broadcasted_iota(jnp.int32, (R, 1), 0)
        dctx = d_ref[0, 1:2, :]
        for dv in range(1, N_DEV):
            dctx = dctx + d_ref[dv, 1:2, :]
        for i in range(nl):
            dm = jnp.zeros((R, n), F32)
            for dv in range(N_DEV):
                dm = dm + jnp.where(row == dv, d_ref[dv, 2 * i:2 * i + 1, :], 0.0)
            if i == 0:
                dm = dm + jnp.where(row == N_DEV, dctx, 0.0)
            gw_ref[i] = _dot_tn(sb, dm.astype(BF16))
        cc = cc_ref[...]
        sg = jax.nn.sigmoid(cc)
        ds = _dot_nt(jnp.broadcast_to(dctx, (8, n)).astype(BF16), w_ref[0].astype(BF16))
        dcc_ref[...] = ds * (sg * (1.0 + cc * (1.0 - sg)))

    return pl.pallas_call(
        body, name=name, grid=(1,),
        in_specs=[_full((R, D)), _full((1, D)), _full((N_DEV, 3, n)), _full((nl, D, n))],
        out_specs=[_full((nl, D, n)), _full((8, D))],
        out_shape=[jax.ShapeDtypeStruct((nl, D, n), F32), jax.ShapeDtypeStruct((8, D), F32)],
        compiler_params=_cp(("arbitrary",)),
    )(s, c_ctx, dall, w_ada)


def _place():
    x, y, c = lax.axis_index("x"), lax.axis_index("y"), lax.axis_index("c")
    return x, y, c


def _lin(p):
    return 4 * p[0] + 2 * p[1] + p[2]


def all_gather_small(xb, *, reduce=False, name):
    R, W = xb.shape

    def body(x_ref, *rest):
        out_ref = rest[0]
        send_sems, recv_sems, local_sem = rest[-3:]
        x, y, c = _place()
        me, sibling = (x, y, c), (x, y, 1 - c)
        chips = [(1 - x, y), (x, 1 - y), (1 - x, 1 - y)]

        def copy(k, block, to, src=None):
            dst = out_ref.at[_lin(block)]
            return pltpu.make_async_remote_copy(
                src_ref=dst if src is None else src, dst_ref=dst, send_sem=send_sems.at[k], recv_sem=recv_sems.at[k],
                device_id=to, device_id_type=MESH)

        mine = pltpu.make_async_copy(x_ref, out_ref.at[_lin(me)], local_sem)
        mine.start()
        first = [copy(0, me, sibling, src=x_ref)]
        first += [copy(1 + j, me, (*chip, c), src=x_ref) for j, chip in enumerate(chips)]
        for cp in first:
            cp.start()
        passed = [copy(4 + j, (*chip, c), sibling) for j, chip in enumerate(chips)]
        for j, chip in enumerate(chips):
            copy(1 + j, (*chip, c), me).wait_recv()
            passed[j].start()
        copy(0, sibling, me).wait_recv()
        for j, chip in enumerate(chips):
            copy(4 + j, (*chip, 1 - c), me).wait_recv()
        for cp in first + passed:
            cp.wait_send()
        mine.wait()
        if reduce:
            acc = out_ref[0]
            for dv in range(1, N_DEV):
                acc = acc + out_ref[dv]
            rest[1][...] = acc

    vm = pl.BlockSpec(memory_space=pltpu.VMEM)
    out_shape = [jax.ShapeDtypeStruct((N_DEV, R, W), xb.dtype)]
    if reduce:
        out_shape.append(jax.ShapeDtypeStruct((R, W), xb.dtype))
    res = pl.pallas_call(
        body, name=name, in_specs=[vm], out_specs=[vm] * len(out_shape), out_shape=out_shape,
        scratch_shapes=[pltpu.SemaphoreType.DMA((7,)), pltpu.SemaphoreType.DMA((7,)), pltpu.SemaphoreType.DMA],
        compiler_params=pltpu.CompilerParams(vmem_limit_bytes=VMEM_LIMIT),
    )(xb)
    return res if reduce else res[0]


HBM_SPEC = pl.BlockSpec(memory_space=pltpu.HBM)
SEM_SPEC = pl.BlockSpec(memory_space=pltpu.SEMAPHORE)
ORDERED_EFFECT = pltpu.SideEffectType.DATAFLOW_SIDE_EFFECTING


def _exchange_copies(srcs, lands, sems, scatter):
    x, y, c = _place()
    me = _lin((x, y, c))
    for j in range(len(srcs)):
        r = lands[j].shape[0] // N_DEV
        block = lambda d, j=j, r=r: pl.ds(pl.multiple_of(d * r, 16), r)
        for k in range(1, N_DEV):
            peer = (x ^ (k >> 2), y ^ ((k >> 1) & 1), c ^ (k & 1))
            src = srcs[j].at[block(_lin(peer)), :] if scatter else srcs[j]
            mk = lambda dst, j=j, k=k, peer=peer, src=src: pltpu.make_async_remote_copy(
                src_ref=src, dst_ref=dst, send_sem=sems[2 * j].at[k - 1], recv_sem=sems[2 * j + 1].at[k - 1],
                device_id=peer, device_id_type=MESH)
            yield mk(lands[j].at[block(me), :]), mk(lands[j].at[block(_lin(peer)), :])


def exchange_start(srcs, lands, *, scatter, name):
    nw = len(srcs)

    def body(*refs):
        for start, _ in _exchange_copies(refs[:nw], refs[nw:2 * nw], refs[2 * nw:4 * nw], scatter):
            start.start()
        refs[-1][...] = jnp.zeros_like(refs[-1])

    thru = [pltpu.HBM(a.shape, a.dtype) for a in (*srcs, *lands)]
    res = pl.pallas_call(
        body, name=name, in_specs=[HBM_SPEC] * (2 * nw),
        out_specs=[SEM_SPEC] * (2 * nw) + [HBM_SPEC] * (2 * nw) + [pl.BlockSpec(memory_space=pltpu.VMEM)],
        out_shape=[pltpu.SemaphoreType.DMA((N_DEV - 1,))] * (2 * nw) + thru + [jax.ShapeDtypeStruct((8, LANES), F32)],
        input_output_aliases={i: 2 * nw + i for i in range(2 * nw)},
        compiler_params=pltpu.CompilerParams(has_side_effects=ORDERED_EFFECT),
    )(*[pltpu.with_memory_space_constraint(a, pltpu.HBM) for a in (*srcs, *lands)])
    return res[:2 * nw], res[2 * nw:3 * nw], res[3 * nw:4 * nw], res[-1]


def exchange_wait(srcs, lands, sems, after, *, scatter, name):
    nw = len(srcs)

    def body(*refs):
        for _, arrive in _exchange_copies(refs[:nw], refs[nw:2 * nw], refs[2 * nw:4 * nw], scatter):
            arrive.wait_send()
            arrive.wait_recv()

    res = pl.pallas_call(
        body, name=name, in_specs=[HBM_SPEC] * (2 * nw) + [SEM_SPEC] * (2 * nw) + [pl.BlockSpec(memory_space=pl.ANY)],
        out_specs=[HBM_SPEC] * (2 * nw), out_shape=[pltpu.HBM(a.shape, a.dtype) for a in (*srcs, *lands)],
        input_output_aliases={i: i for i in range(2 * nw)},
        compiler_params=pltpu.CompilerParams(has_side_effects=ORDERED_EFFECT),
    )(*srcs, *lands, *sems, after)
    return res[nw:]


def _rope_tables(L):
    t = jnp.arange(L)
    inv = ROPE_BASE ** (-jnp.arange(ROPE_FREQS, dtype=F32) / ROPE_FREQS)
    ar = (t // GRID_W).astype(F32)[:, None] * inv
    ac = (t % GRID_W).astype(F32)[:, None] * inv
    z = jnp.zeros_like(ar)
    cos = jnp.concatenate([jnp.cos(ar), jnp.cos(ar), jnp.cos(ac), jnp.cos(ac)], axis=1)
    sa = jnp.concatenate([-jnp.sin(ar), z, -jnp.sin(ac), z], axis=1)
    sb = jnp.concatenate([z, jnp.sin(ar), z, jnp.sin(ac)], axis=1)
    return tuple(jnp.tile(a, (1, LANES // HEAD_DIM)) for a in (cos, sa, sb))


def _rows128(a):
    f = a.reshape(-1)
    n = -(-f.shape[0] // (8 * LANES)) * 8 * LANES
    return jnp.pad(f, (0, n - f.shape[0])).reshape(-1, LANES)


def _own_rows(a, me):
    r = a.shape[0] // N_DEV
    return lax.dynamic_slice_in_dim(a, me * r, r, 0)


def _landing(own, me):
    r, d = own.shape
    return lax.dynamic_update_slice(lax.empty((N_DEV * r, d), own.dtype), own, (me * r, 0))


def kernel(x, c, ctx, c_ctx, w_ada, b_ada, g_mix_pre, g_mix_post, g_ffn_pre, g_ffn_post, w_in_even, w_pool, pool_scale, attn_sink, w_out_even, w_in_odd, sgu_ln_g, sgu_ln_b, sgu_w, sgu_b, w_out_odd, w_ffn_up, ffn_conv_w, ffn_conv_b, w_ffn_down, loss_target, m_c_ctx, m_w_ada, m_b_ada, m_g_mix_pre, m_g_mix_post, m_g_ffn_pre, m_g_ffn_post, m_w_in_even, m_w_pool, m_pool_scale, m_attn_sink, m_w_out_even, m_w_in_odd, m_sgu_ln_g, m_sgu_ln_b, m_sgu_w, m_sgu_b, m_w_out_odd, m_w_ffn_up, m_ffn_conv_w, m_ffn_conv_b, m_w_ffn_down, v_c_ctx, v_w_ada, v_b_ada, v_g_mix_pre, v_g_mix_post, v_g_ffn_pre, v_g_ffn_post, v_w_in_even, v_w_pool, v_pool_scale, v_attn_sink, v_w_out_even, v_w_in_odd, v_sgu_ln_g, v_sgu_ln_b, v_sgu_w, v_sgu_b, v_w_out_odd, v_w_ffn_up, v_ffn_conv_w, v_ffn_conv_b, v_w_ffn_down):
    P = dict(c_ctx=c_ctx, w_ada=w_ada, b_ada=b_ada, g_mix_pre=g_mix_pre, g_mix_post=g_mix_post, g_ffn_pre=g_ffn_pre,
             g_ffn_post=g_ffn_post, w_in_even=w_in_even, w_pool=w_pool, pool_scale=pool_scale, attn_sink=attn_sink,
             w_out_even=w_out_even, w_in_odd=w_in_odd, sgu_ln_g=sgu_ln_g, sgu_ln_b=sgu_ln_b, sgu_w=sgu_w, sgu_b=sgu_b,
             w_out_odd=w_out_odd, w_ffn_up=w_ffn_up, ffn_conv_w=ffn_conv_w, ffn_conv_b=ffn_conv_b, w_ffn_down=w_ffn_down)
    M = dict(c_ctx=m_c_ctx, w_ada=m_w_ada, b_ada=m_b_ada, g_mix_pre=m_g_mix_pre, g_mix_post=m_g_mix_post, g_ffn_pre=m_g_ffn_pre,
             g_ffn_post=m_g_ffn_post, w_in_even=m_w_in_even, w_pool=m_w_pool, pool_scale=m_pool_scale, attn_sink=m_attn_sink,
             w_out_even=m_w_out_even, w_in_odd=m_w_in_odd, sgu_ln_g=m_sgu_ln_g, sgu_ln_b=m_sgu_ln_b, sgu_w=m_sgu_w, sgu_b=m_sgu_b,
             w_out_odd=m_w_out_odd, w_ffn_up=m_w_ffn_up, ffn_conv_w=m_ffn_conv_w, ffn_conv_b=m_ffn_conv_b, w_ffn_down=m_w_ffn_down)
    V = dict(c_ctx=v_c_ctx, w_ada=v_w_ada, b_ada=v_b_ada, g_mix_pre=v_g_mix_pre, g_mix_post=v_g_mix_post, g_ffn_pre=v_g_ffn_pre,
             g_ffn_post=v_g_ffn_post, w_in_even=v_w_in_even, w_pool=v_w_pool, pool_scale=v_pool_scale, attn_sink=v_attn_sink,
             w_out_even=v_w_out_even, w_in_odd=v_w_in_odd, sgu_ln_g=v_sgu_ln_g, sgu_ln_b=v_sgu_ln_b, sgu_w=v_sgu_w, sgu_b=v_sgu_b,
             w_out_odd=v_w_out_odd, w_ffn_up=v_w_ffn_up, ffn_conv_w=v_ffn_conv_w, ffn_conv_b=v_ffn_conv_b, w_ffn_down=v_w_ffn_down)

    x = x[0]
    ctx = ctx[0]
    target = loss_target[0]
    L, D = x.shape
    C = ctx.shape[0]
    tm = min(512, L)
    tm_up = min(1024, L)
    conv_rows = min(256, L)
    me = 4 * lax.axis_index("x") + 2 * lax.axis_index("y") + lax.axis_index("c")
    n_ada = w_ada.shape[2]
    F = w_ffn_down.shape[1] * N_DEV
    half_f = F // 2

    shards = [s.astype(BF16) for s in (w_in_even[0].T, w_out_even[0], w_ffn_up[0].T, w_ffn_down[0],
                                       w_in_odd[0].T, w_out_odd[0], w_ffn_up[1].T, w_ffn_down[1])]
    w_sems, w_srcs, w_lands, token = exchange_start(shards, [_landing(s, me) for s in shards], scatter=False, name="gather_start")

    def weight(j, after):
        return exchange_wait([w_srcs[j]], [w_lands[j]], w_sems[2 * j:2 * j + 2], after, scatter=False, name=f"gather_wait_{j}")[0]

    n_cw = ffn_conv_w.shape[2]
    small = jnp.concatenate([_rows128(c + token[0, 0]), _rows128(sgu_ln_g), _rows128(sgu_ln_b), _rows128(ffn_conv_w)], axis=0)
    small_all = all_gather_small(small, name="gather_small_inputs")
    c_all = small_all[:, :8].reshape(N_DEV, D)
    ln_g = small_all[:, 8].reshape(1, D)
    ln_b = small_all[:, 16].reshape(1, D)
    conv_w = small_all[:, 24:].reshape(N_DEV, -1)[:, :2 * 3 * n_cw].reshape(N_DEV, 2, 3, n_cw)
    conv_w = conv_w.transpose(1, 2, 0, 3).reshape(2, 3, 2 * F)

    cs = jnp.concatenate([c_all, c_ctx[None, :], jnp.zeros((7, D), F32)], axis=0)
    b_loc = lax.dynamic_slice(b_ada, (0, me * n_ada), (2, n_ada))
    silu_c, mods_loc = ada_fwd_mm(cs, w_ada, b_loc, name="ada_fwd")
    mods_all = all_gather_small(mods_loc.reshape(-1, LANES), name="gather_mods").reshape(N_DEV, 2, 16, n_ada)
    mods_all = mods_all.transpose(1, 2, 0, 3).reshape(2, 16, 6 * D)
    mod = lambda i, row: [m_[None, :] for m_ in jnp.split(lax.dynamic_index_in_dim(mods_all[i], row, 0, False), 6)]
    sh_m, sc_m, gt_m, sh_f, sc_f, gt_f = zip(mod(0, me), mod(1, me))
    csh_m, csc_m = mod(0, N_DEV)[:2]

    row = lambda a, i: a[i][None, :]

    cos, sa, sb = _rope_tables(L)
    sink = attn_sink[0]
    bst = sgu_b[0].T
    wup, wdn = [None, None], [None, None]

    def ffn_fwd(i, xin):
        wup[i] = weight(2 + 4 * i, xin)
        h, hu = pre_mm(xin, row(g_ffn_pre, i), sh_f[i], sc_f[i], wup[i], tm=tm_up, tn=half_f, name=f"ffn_up_{i}")
        a = conv_fwd(hu, conv_w[i], ffn_conv_b[i][None, :], rows=conv_rows, wblk=2 * LANES, name=f"ffn_conv_{i}")
        wdn[i] = weight(3 + 4 * i, a)
        f, xo = mm_post(a, wdn[i], xin, row(g_ffn_post, i), gt_f[i], tm=tm, name=f"ffn_down_{i}")
        return h, hu, a, f, xo

    win_e = weight(0, sh_m[0])
    h0, u, q, kv = inproj_even(x, row(g_mix_pre, 0), sh_m[0], sc_m[0], win_e, cos, sa, sb, tm=tm, name="in_even")
    hc, kvc = pre_mm(ctx, row(g_mix_pre, 0), csh_m, csc_m, win_e, tm=C, tn=2 * LANES, w_row_off=8 * LANES, name="in_even_ctx")
    pa = jnp.concatenate([pool_fwd(u, w_pool[0], pool_scale, name="pool_fwd"),
                          attn_fwd(q, kv, kvc, sink, name="attn_fwd")], axis=1)
    wout_e = weight(1, pa)
    y0, x1 = mm_post(pa, wout_e, x, row(g_mix_post, 0), gt_m[0], tm=tm, name="out_even")
    h1, hu0, a0, f0, x2 = ffn_fwd(0, x1)
    win_o = weight(4, x2)
    h2, z1 = pre_mm(x2, row(g_mix_pre, 1), sh_m[1], sc_m[1], win_o, tm=tm_up, tn=D, name="in_odd")
    us = sgu_fwd(z1, ln_g, ln_b, sgu_w[0], bst, name="sgu_fwd")
    wout_o = weight(5, us)
    y1, x3 = mm_post(us, wout_o, x2, row(g_mix_post, 1), gt_m[1], tm=tm, name="out_odd")
    h3, hu1, a1, f1, x4 = ffn_fwd(1, x3)
    loss_part, dx4 = loss_grad(x4, target, tm=tm, name="loss")
    loss = lax.psum(loss_part[0, 0], ("x", "y", "c"))

    g_srcs, g_lands, g_sems = [], [], []

    def scatter(grads, nm):
        sems, srcs, lands, tok = exchange_start(grads, [_landing(_own_rows(g, me), me) for g in grads], scatter=True, name=nm)
        g_srcs.extend(srcs)
        g_lands.extend(lands)
        g_sems.extend(sems)
        return tok[0:1, 0:1]

    def ffn_bwd(i, dxo, xin, h, hu, a, f, g_post):
        dyf, da, dg_post, dgt = post_bwd_mm(dxo, f, g_post, gt_f[i], wdn[i], tm=tm, name=f"ffn_down_bwd_{i}")
        dhg, dhu, dcwg, dcwu, dcbg, dcbu = conv_bwd(da, hu, conv_w[i], ffn_conv_b[i][None, :], rows=conv_rows, wblk=2 * LANES,
                                                    name=f"ffn_conv_bwd_{i}")
        dxin, dg_pre, dsh, dsc = mm_pre_bwd([dhg, dhu], wup[i], xin, dxo, row(g_ffn_pre, i), sc_f[i], tm=tm, tk=half_f,
                                            name=f"ffn_up_bwd_{i}")
        g_dn = wgrad([a], dyf, tr=2 * LANES, name=f"wgrad_down_{i}")
        g_up = wgrad([dhg, dhu], h, tr=2 * LANES, name=f"wgrad_up_{i}")
        tok = scatter([g_dn, g_up], f"scatter_start_ffn_{i}")
        return dxin, tok, dict(g_ffn_post=dg_post, g_ffn_pre=dg_pre, gt_f=dgt, sh_f=dsh, sc_f=dsc,
                               ffn_conv_w=jnp.concatenate([dcwg, dcwu], axis=1), ffn_conv_b=jnp.concatenate([dcbg, dcbu], axis=1)[0])

    dx3, tok, sf1 = ffn_bwd(1, dx4, x3, h3, hu1, a1, f1, row(g_ffn_post, 1))
    dy1, dus, dg_mpost1, dgt_m1 = post_bwd_mm(dx3, y1, row(g_mix_post, 1) + tok, gt_m[1], wout_o, tm=tm, name="out_odd_bwd")
    dz1, dws, dbs, dlng, dlnb = sgu_bwd(z1, dus, ln_g, ln_b, sgu_w[0], bst, name="sgu_bwd")
    dx2, dg_mpre1, dsh_m1, dsc_m1 = mm_pre_bwd([dz1], win_o, x2, dx3, row(g_mix_pre, 1), sc_m[1], tm=tm, tk=D, name="in_odd_bwd")
    tok = scatter([wgrad([us], dy1, tr=2 * LANES, name="wgrad_out_odd"), wgrad([dz1], h2, tr=2 * LANES, name="wgrad_in_odd")],
                  "scatter_start_mix_1")

    dx1, tok, sf0 = ffn_bwd(0, dx2, x1, h1, hu0, a0, f0, row(g_ffn_post, 0) + tok)
    dy0, dpa, dg_mpost0, dgt_m0 = post_bwd_mm(dx1, y0, row(g_mix_post, 0) + tok, gt_m[0], wout_e, tm=tm, name="out_even_bwd")
    du, dwp, dps = pool_bwd(u, dpa, w_pool[0], pool_scale, name="pool_bwd")
    dq, dkv, dkvc, dsink = attn_bwd(q, kv, kvc, sink, dpa, cos, sa, sb, name="attn_bwd")
    dz0 = jnp.concatenate([du, dq, dkv], axis=1)
    dzc = jnp.concatenate([jnp.zeros((C, 8 * LANES), BF16), dkvc], axis=1)
    tok = scatter([wgrad([pa], dy0, tr=2 * LANES, name="wgrad_out_even"),
                   wgrad([dz0], h0, tr=2 * LANES, extra=(dzc, hc), name="wgrad_in_even")], "scatter_start_mix_0")
    grad_x, dg_mpre0, dsh_m0, dsc_m0 = mm_pre_bwd([dz0], win_e, x, dx1, row(g_mix_pre, 0) + tok, sc_m[0], tm=tm, tk=dz0.shape[1],
                                                  name="in_even_bwd")
    _, dg_mpre0c, dcsh, dcsc = mm_pre_bwd([dkvc], win_e, ctx, None, row(g_mix_pre, 0), csc_m, tm=C, tk=2 * LANES,
                                          w_row_off=8 * LANES, name="in_even_ctx_bwd")

    slots = exchange_wait(g_srcs, g_lands, g_sems, dcsh, scatter=True, name="scatter_wait")
    out = {}

    def update(name, idx, land, transposed):
        w_, m_, v_ = (a[idx].T if transposed else a[idx] for a in (P[name], M[name], V[name]))
        r = w_.shape[0]
        tr = r // 4 if r % 64 == 0 and r > 256 else r
        res = adamw(w_, m_, v_, land.reshape(N_DEV, r, land.shape[1]), tr=tr, name=f"adamw_{name}_{idx}")
        for kind, val in zip(("grad", "delta", "new_m", "new_v"), res):
            out.setdefault((kind, name), []).append(val.T if transposed else val)

    update("w_in_even", 0, slots[7], True)
    update("w_out_even", 0, slots[6], False)
    update("w_in_odd", 0, slots[3], True)
    update("w_out_odd", 0, slots[2], False)
    update("w_ffn_up", 0, slots[5], True)
    update("w_ffn_down", 0, slots[4], False)
    update("w_ffn_up", 1, slots[1], True)
    update("w_ffn_down", 1, slots[0], False)

    zero = jnp.zeros((1, D), F32)
    dmod0 = jnp.concatenate([dsh_m0, dsc_m0, dgt_m0, sf0["sh_f"], sf0["sc_f"], sf0["gt_f"]], axis=1)
    dmodc = jnp.concatenate([dcsh, dcsc, zero, zero, zero, zero], axis=1)
    dmod1 = jnp.concatenate([dsh_m1, dsc_m1, dgt_m1, sf1["sh_f"], sf1["sc_f"], sf1["gt_f"]], axis=1)
    dmods = jnp.concatenate([dmod0, dmodc, dmod1], axis=0)
    dmods_all = all_gather_small(dmods.reshape(-1, LANES), name="gather_dmods").reshape(N_DEV, 3, N_DEV, n_ada)
    dall = lax.dynamic_index_in_dim(dmods_all, me, 2, False)
    g_w_ada, dcc = ada_bwd_mm(silu_c, c_ctx[None, :], dall, w_ada, name="ada_bwd")
    nl = w_ada.shape[0]
    res = adamw(w_ada.reshape(nl * D, n_ada), m_w_ada.reshape(nl * D, n_ada), v_w_ada.reshape(nl * D, n_ada),
                g_w_ada.reshape(1, nl * D, n_ada), tr=nl * D // 8, name="adamw_w_ada")
    for kind, val in zip(("grad", "delta", "new_m", "new_v"), res):
        out[(kind, "w_ada")] = val.reshape(nl, D, n_ada)

    rep = dict(
        c_ctx=dcc[0],
        b_ada=jnp.stack([dmod0[0] + dmodc[0], dmod1[0]]),
        g_mix_pre=jnp.concatenate([dg_mpre0 + dg_mpre0c, dg_mpre1]),
        g_mix_post=jnp.concatenate([dg_mpost0, dg_mpost1]),
        g_ffn_pre=jnp.concatenate([sf0["g_ffn_pre"], sf1["g_ffn_pre"]]),
        g_ffn_post=jnp.concatenate([sf0["g_ffn_post"], sf1["g_ffn_post"]]),
        w_pool=dwp[None], pool_scale=dps, attn_sink=dsink[:, :N_Q_HEADS],
        sgu_w=dws[None], sgu_b=dbs[:, :sgu_b.shape[1]].T[None],
        ffn_conv_b=jnp.stack([sf0["ffn_conv_b"], sf1["ffn_conv_b"]]),
    )
    rep_names = list(rep)
    conv_g = jnp.stack([sf0["ffn_conv_w"], sf1["ffn_conv_w"]]).reshape(2, 3, N_DEV, n_cw).transpose(2, 0, 1, 3)
    shard_full = dict(sgu_ln_g=dlng.reshape(N_DEV, LANES), sgu_ln_b=dlnb.reshape(N_DEV, LANES),
                      ffn_conv_w=jnp.concatenate([_rows128(conv_g[d]) for d in range(N_DEV)], axis=0))
    pieces = [_rows128(rep[k]) for k in rep_names] + [shard_full[k] for k in shard_full]
    sizes = [p.shape[0] for p in pieces]
    _, gsum = all_gather_small(jnp.concatenate(pieces, axis=0), reduce=True, name="allreduce_small_grads")
    offs = [sum(sizes[:i]) for i in range(len(sizes))]
    n_rep = len(rep_names)
    cw_rows = sizes[-1] // N_DEV
    g_own = [gsum[offs[i]:offs[i] + sizes[i]] for i in range(n_rep)]
    g_own.append(_rows128(lax.dynamic_slice_in_dim(gsum, offs[n_rep] + me, 1, 0)))
    g_own.append(_rows128(lax.dynamic_slice_in_dim(gsum, offs[n_rep + 1] + me, 1, 0)))
    g_own.append(lax.dynamic_slice_in_dim(gsum, offs[n_rep + 2] + me * cw_rows, cw_rows, 0))
    small_names = rep_names + list(shard_full)
    packs = [jnp.concatenate([_rows128(src[k]) for k in small_names], axis=0) for src in (P, M, V)]
    n_pack = packs[0].shape[0]
    gp = jnp.concatenate(g_own, axis=0)[None]
    res = adamw(*packs, gp, tr=n_pack, name="adamw_small")
    o = 0
    for k in small_names:
        n = _rows128(P[k]).shape[0]
        for kind, val in zip(("grad", "delta", "new_m", "new_v"), res):
            out[(kind, k)] = val[o:o + n].reshape(-1)[:P[k].size].reshape(P[k].shape)
        o += n
    assert o == n_pack

    names = list(P)
    final = [loss, grad_x[None]]
    for kind in ("grad", "delta", "new_m", "new_v"):
        for k in names:
            val = out[(kind, k)]
            final.append(jnp.stack(val) if isinstance(val, list) else val)
    return tuple(final)
```

```python
import functools
import math

import jax
import jax.numpy as jnp
from jax import lax
from jax.experimental import pallas as pl
from jax.experimental.pallas import tpu as pltpu

F32 = jnp.float32
BF16 = jnp.bfloat16
MESH = pl.DeviceIdType.MESH
N_DEV = 8
LANES = 128
VMEM_LIMIT = 48 * 1024 * 1024
EPS = 1e-6
NEG_INF = -1e30
GRID_W = 64
WINDOW = 128
BLK = 128
HEAD_DIM = 64
N_Q_HEADS = 8
N_KV_HEADS = 2
GQA = N_Q_HEADS // N_KV_HEADS
POOL_WINDOWS = (2, 4, 8, 16)
ROPE_BASE = 10000.0
ROPE_FREQS = HEAD_DIM // 4
PAD = 16
ADAM_LR, ADAM_B1, ADAM_B2, ADAM_EPS, ADAM_WD, ADAM_STEP = 0.001, 0.9, 0.999, 1e-08, 0.01, 10
BC1 = 1.0 - ADAM_B1 ** ADAM_STEP
BC2 = 1.0 - ADAM_B2 ** ADAM_STEP
SQRT_2_OVER_PI = math.sqrt(2.0 / math.pi)
GELU_C = 0.044715


def _cp(sem=None):
    return pltpu.CompilerParams(dimension_semantics=sem, vmem_limit_bytes=VMEM_LIMIT)


def _dot(a, b):
    return jnp.dot(a, b, preferred_element_type=F32)


def _dot_nt(a, b):
    return lax.dot_general(a, b, (((1,), (1,)), ((), ())), preferred_element_type=F32)


def _dot_tn(a, b):
    return lax.dot_general(a, b, (((0,), (0,)), ((), ())), preferred_element_type=F32)


def _rms(x):
    r = lax.rsqrt(jnp.mean(x * x, axis=-1, keepdims=True) + EPS)
    return x * r, r


def _rms_bwd(dn, n, r):
    return r * (dn - n * jnp.mean(dn * n, axis=-1, keepdims=True))


def _colsum(a):
    return jnp.sum(a, axis=0, keepdims=True)


def _rope(x, c, sa, sb):
    return x * c + pltpu.roll(x, LANES - ROPE_FREQS, 1) * sa + pltpu.roll(x, ROPE_FREQS, 1) * sb


def _full(shape):
    return pl.BlockSpec(shape, lambda *_: (0,) * len(shape))


def pre_mm(x, g, sh, sc, wt, *, tm, tn, w_row_off=0, name):
    T, D = x.shape
    n_rows = wt.shape[0] - w_row_off
    off = w_row_off // tn

    def body(x_ref, g_ref, sh_ref, sc_ref, w_ref, h_ref, z_ref):
        @pl.when(pl.program_id(1) == 0)
        def _():
            n, _ = _rms(x_ref[...])
            h_ref[...] = (n * g_ref[...] * (1.0 + sc_ref[...]) + sh_ref[...]).astype(BF16)

        z_ref[...] = _dot_nt(h_ref[...], w_ref[...]).astype(BF16)

    vec = pl.BlockSpec((1, D), lambda i, j: (0, 0))
    return pl.pallas_call(
        body, name=name, grid=(T // tm, n_rows // tn),
        in_specs=[pl.BlockSpec((tm, D), lambda i, j: (i, 0)), vec, vec, vec, pl.BlockSpec((tn, D), lambda i, j: (j + off, 0))],
        out_specs=[pl.BlockSpec((tm, D), lambda i, j: (i, 0)), pl.BlockSpec((tm, tn), lambda i, j: (i, j))],
        out_shape=[jax.ShapeDtypeStruct((T, D), BF16), jax.ShapeDtypeStruct((T, n_rows), BF16)],
        compiler_params=_cp(("parallel", "arbitrary")),
    )(x, g, sh, sc, wt)


def inproj_even(x, g, sh, sc, wt, cos, sa, sb, *, tm, name):
    T, D = x.shape
    N = wt.shape[0]

    def body(x_ref, g_ref, sh_ref, sc_ref, w_ref, c_ref, sa_ref, sb_ref, h_ref, u_ref, q_ref, kv_ref):
        n, _ = _rms(x_ref[...])
        h = (n * g_ref[...] * (1.0 + sc_ref[...]) + sh_ref[...]).astype(BF16)
        h_ref[...] = h
        z = _dot_nt(h, w_ref[...])
        u_ref[...] = z[:, :4 * LANES]
        c, a, b = c_ref[...], sa_ref[...], sb_ref[...]
        for s in range(4):
            q_ref[:, s * LANES:(s + 1) * LANES] = _rope(z[:, (4 + s) * LANES:(5 + s) * LANES], c, a, b).astype(BF16)
        kv_ref[:, :LANES] = _rope(z[:, 8 * LANES:9 * LANES], c, a, b).astype(BF16)
        kv_ref[:, LANES:] = z[:, 9 * LANES:].astype(BF16)

    vec = pl.BlockSpec((1, D), lambda i: (0, 0))
    row = lambda w: pl.BlockSpec((tm, w), lambda i: (i, 0))
    return pl.pallas_call(
        body, name=name, grid=(T // tm,),
        in_specs=[row(D), vec, vec, vec, _full((N, D)), row(LANES), row(LANES), row(LANES)],
        out_specs=[row(D), row(4 * LANES), row(4 * LANES), row(2 * LANES)],
        out_shape=[jax.ShapeDtypeStruct((T, D), BF16), jax.ShapeDtypeStruct((T, 4 * LANES), F32),
                   jax.ShapeDtypeStruct((T, 4 * LANES), BF16), jax.ShapeDtypeStruct((T, 2 * LANES), BF16)],
        compiler_params=_cp(("parallel",)),
    )(x, g, sh, sc, wt, cos, sa, sb)


def mm_post(a, w, x, g, gt, *, tm, name):
    T, K = a.shape
    D = w.shape[1]

    def body(a_ref, w_ref, x_ref, g_ref, gt_ref, y_ref, xn_ref):
        y = _dot(a_ref[...], w_ref[...])
        n, _ = _rms(y)
        y_ref[...] = y
        xn_ref[...] = x_ref[...] + gt_ref[...] * (n * g_ref[...])

    vec = pl.BlockSpec((1, D), lambda i: (0, 0))
    row = lambda w_: pl.BlockSpec((tm, w_), lambda i: (i, 0))
    return pl.pallas_call(
        body, name=name, grid=(T // tm,),
        in_specs=[row(K), _full((K, D)), row(D), vec, vec],
        out_specs=[row(D), row(D)],
        out_shape=[jax.ShapeDtypeStruct((T, D), F32), jax.ShapeDtypeStruct((T, D), F32)],
        compiler_params=_cp(("parallel",)),
    )(a, w, x, g, gt)


def post_bwd_mm(dxn, y, g, gt, w, *, tm, name):
    T, D = y.shape
    K = w.shape[0]

    def body(dxn_ref, y_ref, g_ref, gt_ref, w_ref, dy_ref, da_ref, dg_ref, dgt_ref):
        @pl.when(pl.program_id(0) == 0)
        def _():
            dg_ref[...] = jnp.zeros_like(dg_ref)
            dgt_ref[...] = jnp.zeros_like(dgt_ref)

        d = dxn_ref[...]
        n, r = _rms(y_ref[...])
        g_, gt_ = g_ref[...], gt_ref[...]
        dg_ref[...] += _colsum(d * gt_ * n)
        dgt_ref[...] += _colsum(d * g_ * n)
        dy = _rms_bwd(d * (gt_ * g_), n, r).astype(BF16)
        dy_ref[...] = dy
        da_ref[...] = _dot_nt(dy, w_ref[...]).astype(BF16)

    vec = pl.BlockSpec((1, D), lambda i: (0, 0))
    row = lambda w_: pl.BlockSpec((tm, w_), lambda i: (i, 0))
    return pl.pallas_call(
        body, name=name, grid=(T // tm,),
        in_specs=[row(D), row(D), vec, vec, _full((K, D))],
        out_specs=[row(D), row(K), vec, vec],
        out_shape=[jax.ShapeDtypeStruct((T, D), BF16), jax.ShapeDtypeStruct((T, K), BF16),
                   jax.ShapeDtypeStruct((1, D), F32), jax.ShapeDtypeStruct((1, D), F32)],
        compiler_params=_cp(("arbitrary",)),
    )(dxn, y, g, gt, w)


def mm_pre_bwd(dzs, wt, x, dres, g, sc, *, tm, tk, w_row_off=0, name):
    T, N = dzs[0].shape
    D = x.shape[1]
    nk = N // tk
    npart = len(dzs)
    off = w_row_off // tk
    has_res = dres is not None

    def body(*refs):
        dz_refs = refs[:npart]
        w_refs = refs[npart:2 * npart]
        rest = refs[2 * npart:]
        x_ref = rest[0]
        dres_ref = rest[1] if has_res else None
        g_ref, sc_ref, dx_ref, dg_ref, dsh_ref, dsc_ref, acc = rest[1 + has_res:]
        i, k = pl.program_id(0), pl.program_id(1)

        @pl.when(jnp.logical_and(i == 0, k == 0))
        def _():
            dg_ref[...] = jnp.zeros_like(dg_ref)
            dsh_ref[...] = jnp.zeros_like(dsh_ref)
            dsc_ref[...] = jnp.zeros_like(dsc_ref)

        part = _dot(dz_refs[0][...], w_refs[0][...])
        for p in range(1, npart):
            part = part + _dot(dz_refs[p][...], w_refs[p][...])

        @pl.when(k == 0)
        def _():
            acc[...] = part

        @pl.when(k > 0)
        def _():
            acc[...] += part

        @pl.when(k == nk - 1)
        def _():
            dh = acc[...]
            n, r = _rms(x_ref[...])
            g_, s1 = g_ref[...], 1.0 + sc_ref[...]
            dsh_ref[...] += _colsum(dh)
            dsc_ref[...] += _colsum(dh * n * g_)
            dg_ref[...] += _colsum(dh * s1 * n)
            dxp = _rms_bwd(dh * (g_ * s1), n, r)
            dx_ref[...] = dxp + dres_ref[...] if has_res else dxp

    vec = pl.BlockSpec((1, D), lambda i, k: (0, 0))
    row = pl.BlockSpec((tm, D), lambda i, k: (i, 0))
    w_specs = [pl.BlockSpec((tk, D), (lambda i, k, p=p: (k + off + p * nk, 0))) for p in range(npart)]
    res_specs, res_args = ([row], (dres,)) if has_res else ([], ())
    return pl.pallas_call(
        body, name=name, grid=(T // tm, nk),
        in_specs=[pl.BlockSpec((tm, tk), lambda i, k: (i, k))] * npart + w_specs + [row] + res_specs + [vec, vec],
        out_specs=[row, vec, vec, vec],
        out_shape=[jax.ShapeDtypeStruct((T, D), F32)] + [jax.ShapeDtypeStruct((1, D), F32)] * 3,
        scratch_shapes=[pltpu.VMEM((tm, D), F32)],
        compiler_params=_cp(("arbitrary", "arbitrary")),
    )(*dzs, *([wt] * npart), x, *res_args, g, sc)


def wgrad(a_parts, b, *, tr, extra=None, name):
    T, R = a_parts[0].shape
    D = b.shape[1]
    npart = len(a_parts)
    nr = R // tr

    def body(*refs):
        a_refs, b_ref = refs[:npart], refs[npart]
        g_ref = refs[-1]
        for p in range(npart):
            @pl.when(pl.program_id(0) // nr == p)
            def _():
                acc = _dot_tn(a_refs[p][...], b_ref[...])
                if extra is not None:
                    acc += _dot_tn(refs[npart + 1][...], refs[npart + 2][...])
                g_ref[...] = acc.astype(BF16)

    in_specs = [pl.BlockSpec((T, tr), (lambda r, p=p: (0, jnp.clip(r - p * nr, 0, nr - 1)))) for p in range(npart)]
    in_specs.append(_full((T, D)))
    args = [*a_parts, b]
    if extra is not None:
        a2, b2 = extra
        in_specs += [pl.BlockSpec((a2.shape[0], tr), lambda r: (0, r)), _full(b2.shape)]
        args += [a2, b2]
    return pl.pallas_call(
        body, name=name, grid=(npart * nr,),
        in_specs=in_specs, out_specs=pl.BlockSpec((tr, D), lambda r: (r, 0)),
        out_shape=jax.ShapeDtypeStruct((npart * R, D), BF16),
        compiler_params=_cp(("parallel",)),
    )(*args)


def _conv_ext(ref, r0, rows, total):
    top = ref[pl.ds(pl.multiple_of(jnp.maximum(r0 - PAD, 0), PAD), PAD), :]
    mid = ref[pl.ds(r0, rows), :]
    bot = ref[pl.ds(pl.multiple_of(jnp.minimum(r0 + rows, total - PAD), PAD), PAD), :]
    ext = jnp.concatenate([top, mid, bot], axis=0).astype(F32)
    t = r0 - PAD + lax.broadcasted_iota(jnp.int32, (rows + 2 * PAD, 1), 0)
    return jnp.where(jnp.logical_and(t >= 0, t < total), ext, 0.0)


def _shift_rows(a, k):
    return pltpu.roll(a, k % a.shape[0], 0)


def _conv3(x, w, b):
    return w[0:1] * _shift_rows(x, 1) + w[1:2] * x + w[2:3] * _shift_rows(x, -1) + b


def _gate_up_specs(rows_, wblk, nb):
    return [pl.BlockSpec((rows_, wblk), lambda j: (0, j)), pl.BlockSpec((rows_, wblk), lambda j: (0, j + nb))]


def conv_fwd(hu, cw, cb, *, rows, wblk, name):
    L, N2 = hu.shape
    nb = N2 // 2 // wblk
    nchunk = L // rows

    def body(hg_ref, hu_ref, wg_ref, wu_ref, bg_ref, bu_ref, a_ref):
        def chunk(ci, carry):
            r0 = pl.multiple_of(ci * rows, rows)
            gate = _conv3(_conv_ext(hg_ref, r0, rows, L), wg_ref[...], bg_ref[...])[PAD:PAD + rows]
            up = _conv3(_conv_ext(hu_ref, r0, rows, L), wu_ref[...], bu_ref[...])[PAD:PAD + rows]
            a_ref[pl.ds(r0, rows), :] = (gate * jax.nn.sigmoid(gate) * up).astype(BF16)
            return carry

        lax.fori_loop(0, nchunk, chunk, 0)

    return pl.pallas_call(
        body, name=name, grid=(nb,),
        in_specs=_gate_up_specs(L, wblk, nb) + _gate_up_specs(3, wblk, nb) + _gate_up_specs(1, wblk, nb),
        out_specs=pl.BlockSpec((L, wblk), lambda j: (0, j)),
        out_shape=jax.ShapeDtypeStruct((L, N2 // 2), BF16),
        compiler_params=_cp(("parallel",)),
    )(hu, hu, cw, cw, cb, cb)


def conv_bwd(da, hu, cw, cb, *, rows, wblk, name):
    L, N2 = hu.shape
    F = N2 // 2
    nb = F // wblk
    nchunk = L // rows
    mid = slice(PAD, PAD + rows)

    def body(da_ref, hg_ref, hu_ref, wg_ref, wu_ref, bg_ref, bu_ref, dg_ref, du_ref, dwg_ref, dwu_ref, dbg_ref, dbu_ref):
        for ref in (dwg_ref, dwu_ref, dbg_ref, dbu_ref):
            ref[...] = jnp.zeros_like(ref)

        def half_bwd(x, dh, w_ref, dx_ref, dw_ref, db_ref, r0):
            w = w_ref[...]
            dx = w[0:1] * _shift_rows(dh, -1) + w[1:2] * dh + w[2:3] * _shift_rows(dh, 1)
            dx_ref[pl.ds(r0, rows), :] = dx[mid].astype(BF16)
            dhm = dh[mid]
            db_ref[...] += _colsum(dhm)
            dw_ref[0:1, :] += _colsum(dhm * _shift_rows(x, 1)[mid])
            dw_ref[1:2, :] += _colsum(dhm * x[mid])
            dw_ref[2:3, :] += _colsum(dhm * _shift_rows(x, -1)[mid])

        def chunk(ci, carry):
            r0 = pl.multiple_of(ci * rows, rows)
            xg = _conv_ext(hg_ref, r0, rows, L)
            xu = _conv_ext(hu_ref, r0, rows, L)
            d = _conv_ext(da_ref, r0, rows, L)
            gate = _conv3(xg, wg_ref[...], bg_ref[...])
            up = _conv3(xu, wu_ref[...], bu_ref[...])
            sg = jax.nn.sigmoid(gate)
            half_bwd(xu, d * (gate * sg), wu_ref, du_ref, dwu_ref, dbu_ref, r0)
            half_bwd(xg, d * up * (sg * (1.0 + gate * (1.0 - sg))), wg_ref, dg_ref, dwg_ref, dbg_ref, r0)
            return carry

        lax.fori_loop(0, nchunk, chunk, 0)

    blk = lambda r: pl.BlockSpec((r, wblk), lambda j: (0, j))
    return pl.pallas_call(
        body, name=name, grid=(nb,),
        in_specs=[blk(L)] + _gate_up_specs(L, wblk, nb) + _gate_up_specs(3, wblk, nb) + _gate_up_specs(1, wblk, nb),
        out_specs=[blk(L), blk(L), blk(3), blk(3), blk(1), blk(1)],
        out_shape=[jax.ShapeDtypeStruct((L, F), BF16)] * 2 + [jax.ShapeDtypeStruct((3, F), F32)] * 2
        + [jax.ShapeDtypeStruct((1, F), F32)] * 2,
        compiler_params=_cp(("parallel",)),
    )(da, hu, hu, cw, cw, cb, cb)


def _window_sums(pad_ref, w, lead):
    a = pad_ref[...]
    k = 1
    while k < w:
        a = a + _shift_rows(a, -k)
        k *= 2
    return _shift_rows(a, lead) if lead else a


def _pool_counts(L, h):
    t = lax.broadcasted_iota(jnp.int32, (L, 1), 0)
    return (jnp.minimum(t + h, L) - jnp.maximum(t - h, 0)).astype(F32)


def _pooled(u_ref, pad_ref, L, w):
    h = w // 2
    pad_ref[pl.ds(PAD, L), :] = u_ref[...]
    win = _window_sums(pad_ref, w, h)[PAD:PAD + L]
    return win / _pool_counts(L, h) - u_ref[...]


def _zero_pad_edges(pad_ref, L):
    z = jnp.zeros((PAD, LANES), F32)
    pad_ref[pl.ds(0, PAD), :] = z
    pad_ref[pl.ds(PAD + L, PAD), :] = z


def pool_fwd(u, w_pool, pool_scale, *, name):
    L = u.shape[0]

    def body(u_ref, w_ref, ps_ref, p_ref, pad_ref):
        _zero_pad_edges(pad_ref, L)
        for gi, win in enumerate(POOL_WINDOWS):
            @pl.when(pl.program_id(0) == gi)
            def _():
                pooled = _pooled(u_ref, pad_ref, L, win)
                p_ref[...] = (_dot(pooled.astype(BF16), w_ref[...].astype(BF16)) * ps_ref[...]).astype(BF16)

    return pl.pallas_call(
        body, name=name, grid=(len(POOL_WINDOWS),),
        in_specs=[pl.BlockSpec((L, LANES), lambda gi: (0, gi)), pl.BlockSpec((None, LANES, LANES), lambda gi: (gi, 0, 0)),
                  pl.BlockSpec((1, LANES), lambda gi: (0, gi))],
        out_specs=pl.BlockSpec((L, LANES), lambda gi: (0, gi)),
        out_shape=jax.ShapeDtypeStruct((L, 4 * LANES), BF16),
        scratch_shapes=[pltpu.VMEM((L + 2 * PAD, LANES), F32)],
        compiler_params=_cp(("parallel",)),
    )(u, w_pool, pool_scale)


def pool_bwd(u, dpa, w_pool, pool_scale, *, name):
    L = u.shape[0]

    def body(u_ref, dp_ref, w_ref, ps_ref, du_ref, dw_ref, dps_ref, pad_ref):
        _zero_pad_edges(pad_ref, L)
        for gi, win in enumerate(POOL_WINDOWS):
            @pl.when(pl.program_id(0) == gi)
            def _():
                h = win // 2
                wb = w_ref[...].astype(BF16)
                pooled = _pooled(u_ref, pad_ref, L, win).astype(BF16)
                dp = dp_ref[...].astype(F32)
                dps_ref[...] = _colsum(dp * _dot(pooled, wb))
                dy = (dp * ps_ref[...]).astype(BF16)
                dw_ref[...] = _dot_tn(pooled, dy)
                dpooled = _dot_nt(dy, wb)
                pad_ref[pl.ds(PAD, L), :] = dpooled / _pool_counts(L, h)
                du_ref[...] = (_window_sums(pad_ref, win, h - 1)[PAD:PAD + L] - dpooled).astype(BF16)

    return pl.pallas_call(
        body, name=name, grid=(len(POOL_WINDOWS),),
        in_specs=[pl.BlockSpec((L, LANES), lambda gi: (0, gi)), pl.BlockSpec((L, LANES), lambda gi: (0, gi)),
                  pl.BlockSpec((None, LANES, LANES), lambda gi: (gi, 0, 0)), pl.BlockSpec((1, LANES), lambda gi: (0, gi))],
        out_specs=[pl.BlockSpec((L, LANES), lambda gi: (0, gi)), pl.BlockSpec((None, LANES, LANES), lambda gi: (gi, 0, 0)),
                   pl.BlockSpec((1, LANES), lambda gi: (0, gi))],
        out_shape=[jax.ShapeDtypeStruct((L, 4 * LANES), BF16), jax.ShapeDtypeStruct((4, LANES, LANES), F32),
                   jax.ShapeDtypeStruct((1, 4 * LANES), F32)],
        scratch_shapes=[pltpu.VMEM((L + 2 * PAD, LANES), F32)],
        compiler_params=_cp(("parallel",)),
    )(u, dpa, w_pool, pool_scale)


def _attn_probs(q4, band, kvc, sink_ref, kh, mask4):
    scale = HEAD_DIM ** -0.5
    ks = slice(kh * HEAD_DIM, (kh + 1) * HEAD_DIM)
    s_loc = jnp.where(mask4, _dot_nt(q4, band[:, ks]) * scale, NEG_INF)
    s_ctx = _dot_nt(q4, kvc[:, ks]) * scale
    sk = jnp.concatenate([jnp.full((BLK, 1), sink_ref[kh * GQA + hh], F32) for hh in range(GQA)], axis=0)
    m = jnp.maximum(jnp.maximum(jnp.max(s_loc, axis=-1, keepdims=True), jnp.max(s_ctx, axis=-1, keepdims=True)), sk)
    e_loc, e_ctx, e_s = jnp.exp(s_loc - m), jnp.exp(s_ctx - m), jnp.exp(sk - m)
    inv = 1.0 / (jnp.sum(e_loc, axis=-1, keepdims=True) + jnp.sum(e_ctx, axis=-1, keepdims=True) + e_s)
    return e_loc * inv, e_ctx * inv, e_s * inv


def _attn_block(n, L):
    start = pl.multiple_of(jnp.clip((n - 1) * BLK, 0, L - 3 * BLK), BLK)
    qpos = n * BLK + lax.broadcasted_iota(jnp.int32, (BLK, 3 * BLK), 0)
    kpos = start + lax.broadcasted_iota(jnp.int32, (BLK, 3 * BLK), 1)
    mask = jnp.abs(kpos - qpos) <= WINDOW
    return start, jnp.concatenate([mask] * GQA, axis=0)


def _stack_heads(ref, kh):
    return jnp.concatenate([ref[:, (kh * GQA + hh) * HEAD_DIM:(kh * GQA + hh + 1) * HEAD_DIM] for hh in range(GQA)], axis=0)


def attn_fwd(q, kv, kvc, sink, *, name):
    L = q.shape[0]
    C = kvc.shape[0]

    def body(q_ref, kv_ref, kvc_ref, sink_ref, o_ref, o_scr):
        start, mask4 = _attn_block(pl.program_id(0), L)
        band = kv_ref[pl.ds(start, 3 * BLK), :]
        kvc_ = kvc_ref[...]
        for kh in range(N_KV_HEADS):
            q4 = _stack_heads(q_ref, kh)
            p_loc, p_ctx, _ = _attn_probs(q4, band, kvc_, sink_ref, kh, mask4)
            vs = slice(2 * HEAD_DIM + kh * HEAD_DIM, 2 * HEAD_DIM + (kh + 1) * HEAD_DIM)
            o4 = _dot(p_loc.astype(BF16), band[:, vs]) + _dot(p_ctx.astype(BF16), kvc_[:, vs])
            for hh in range(GQA):
                h = kh * GQA + hh
                o_scr[:, h * HEAD_DIM:(h + 1) * HEAD_DIM] = o4[hh * BLK:(hh + 1) * BLK]
        o_ref[...] = o_scr[...].astype(BF16)

    return pl.pallas_call(
        body, name=name, grid=(L // BLK,),
        in_specs=[pl.BlockSpec((BLK, 4 * LANES), lambda n: (n, 0)), _full((L, 2 * LANES)), _full((C, 2 * LANES)),
                  pl.BlockSpec(memory_space=pltpu.SMEM)],
        out_specs=pl.BlockSpec((BLK, 4 * LANES), lambda n: (n, 0)),
        out_shape=jax.ShapeDtypeStruct((L, 4 * LANES), BF16),
        scratch_shapes=[pltpu.VMEM((BLK, 4 * LANES), F32)],
        compiler_params=_cp(("parallel",)),
    )(q, kv, kvc, sink)


def attn_bwd(q, kv, kvc, sink, dpa, cos, sa, sb, *, name):
    L = q.shape[0]
    C = kvc.shape[0]
    nb = L // BLK
    scale = HEAD_DIM ** -0.5

    def body(q_ref, kv_ref, kvc_ref, sink_ref, do_ref, c_ref, sa_ref, sb_ref, cq_ref, saq_ref, sbq_ref,
             dq_ref, dkv_ref, dkvc_ref, dsink_ref, dkv_acc, dkvc_acc, dq_scr, band_scr, ctx_scr):
        n = pl.program_id(0)

        @pl.when(n == 0)
        def _():
            dkv_acc[...] = jnp.zeros_like(dkv_acc)
            dkvc_acc[...] = jnp.zeros_like(dkvc_acc)
            dsink_ref[...] = jnp.zeros_like(dsink_ref)

        start, mask4 = _attn_block(n, L)
        band = kv_ref[pl.ds(start, 3 * BLK), :]
        kvc_ = kvc_ref[...]
        lane = lax.broadcasted_iota(jnp.int32, (1, LANES), 1)
        dsink = jnp.zeros((1, LANES), F32)
        for kh in range(N_KV_HEADS):
            ks = slice(kh * HEAD_DIM, (kh + 1) * HEAD_DIM)
            vs = slice(2 * HEAD_DIM + kh * HEAD_DIM, 2 * HEAD_DIM + (kh + 1) * HEAD_DIM)
            q4 = _stack_heads(q_ref, kh)
            do4 = _stack_heads(do_ref, kh)
            p_loc, p_ctx, p_s = _attn_probs(q4, band, kvc_, sink_ref, kh, mask4)
            dp_loc = _dot_nt(do4, band[:, vs])
            dp_ctx = _dot_nt(do4, kvc_[:, vs])
            delta = jnp.sum(p_loc * dp_loc, axis=-1, keepdims=True) + jnp.sum(p_ctx * dp_ctx, axis=-1, keepdims=True)
            ds_loc = (p_loc * (dp_loc - delta) * scale).astype(BF16)
            ds_ctx = (p_ctx * (dp_ctx - delta) * scale).astype(BF16)
            dsk = p_s * delta
            for hh in range(GQA):
                h = kh * GQA + hh
                dsink = dsink - jnp.where(lane == h, jnp.sum(dsk[hh * BLK:(hh + 1) * BLK], axis=0, keepdims=True), 0.0)
            dq4 = _dot(ds_loc, band[:, ks]) + _dot(ds_ctx, kvc_[:, ks])
            for hh in range(GQA):
                h = kh * GQA + hh
                dq_scr[:, h * HEAD_DIM:(h + 1) * HEAD_DIM] = dq4[hh * BLK:(hh + 1) * BLK]
            band_scr[:, ks] = _dot_tn(ds_loc, q4)
            band_scr[:, vs] = _dot_tn(p_loc.astype(BF16), do4)
            ctx_scr[:, ks] = _dot_tn(ds_ctx, q4)
            ctx_scr[:, vs] = _dot_tn(p_ctx.astype(BF16), do4)
        dsink_ref[...] += dsink
        dkv_acc[pl.ds(start, 3 * BLK), :] += band_scr[...]
        dkvc_acc[...] += ctx_scr[...]
        c, a, b = cq_ref[...], -saq_ref[...], -sbq_ref[...]
        for s in range(4):
            dq_ref[:, s * LANES:(s + 1) * LANES] = _rope(dq_scr[:, s * LANES:(s + 1) * LANES], c, a, b).astype(BF16)

        @pl.when(n == nb - 1)
        def _():
            dkv_ref[:, :LANES] = _rope(dkv_acc[:, :LANES], c_ref[...], -sa_ref[...], -sb_ref[...]).astype(BF16)
            dkv_ref[:, LANES:] = dkv_acc[:, LANES:].astype(BF16)
            dkvc_ref[...] = dkvc_acc[...].astype(BF16)

    blk = lambda w: pl.BlockSpec((BLK, w), lambda n: (n, 0))
    return pl.pallas_call(
        body, name=name, grid=(nb,),
        in_specs=[blk(4 * LANES), _full((L, 2 * LANES)), _full((C, 2 * LANES)), pl.BlockSpec(memory_space=pltpu.SMEM),
                  pl.BlockSpec((BLK, 4 * LANES), lambda n: (n, 1)),
                  _full((L, LANES)), _full((L, LANES)), _full((L, LANES)), blk(LANES), blk(LANES), blk(LANES)],
        out_specs=[blk(4 * LANES), _full((L, 2 * LANES)), _full((C, 2 * LANES)), _full((1, LANES))],
        out_shape=[jax.ShapeDtypeStruct((L, 4 * LANES), BF16), jax.ShapeDtypeStruct((L, 2 * LANES), BF16),
                   jax.ShapeDtypeStruct((C, 2 * LANES), BF16), jax.ShapeDtypeStruct((1, LANES), F32)],
        scratch_shapes=[pltpu.VMEM((L, 2 * LANES), F32), pltpu.VMEM((C, 2 * LANES), F32), pltpu.VMEM((BLK, 4 * LANES), F32),
                        pltpu.VMEM((3 * BLK, 2 * LANES), F32), pltpu.VMEM((C, 2 * LANES), F32)],
        compiler_params=_cp(("arbitrary",)),
    )(q, kv, kvc, sink, dpa, cos, sa, sb, cos, sa, sb)


def _gelu_parts(x):
    th = jnp.tanh(SQRT_2_OVER_PI * (x + GELU_C * x * x * x))
    return 0.5 * x * (1.0 + th), th


def _gelu_grad(x, th):
    return 0.5 * (1.0 + th) + 0.5 * x * (1.0 - th * th) * SQRT_2_OVER_PI * (1.0 + 3.0 * GELU_C * x * x)


def _layernorm(v):
    mu = jnp.mean(v, axis=-1, keepdims=True)
    vc = v - mu
    rstd = lax.rsqrt(jnp.mean(vc * vc, axis=-1, keepdims=True) + EPS)
    return vc * rstd, rstd


def sgu_fwd(z1, ln_g, ln_b, ws, bst, *, name):
    L, W2 = z1.shape
    W = W2 // 2
    ng = W // LANES

    def body(z_ref, g_ref, b_ref, ws_ref, bs_ref, o_ref):
        z, _ = _gelu_parts(z_ref[...].astype(F32))
        xhat, _ = _layernorm(z[:, W:])
        vln = (xhat * g_ref[...] + b_ref[...]).astype(BF16)
        for gi in range(ng):
            cs = slice(gi * LANES, (gi + 1) * LANES)
            s = _dot(ws_ref[gi].astype(BF16), vln[:, cs]) + bs_ref[:, gi:gi + 1]
            o_ref[:, cs] = (z[:, cs] * s).astype(BF16)

    vec = _full((1, W))
    return pl.pallas_call(
        body, name=name, grid=(L // BLK,),
        in_specs=[pl.BlockSpec((BLK, W2), lambda n: (n, 0)), vec, vec, _full((ng, LANES, LANES)), _full((BLK, ng))],
        out_specs=pl.BlockSpec((BLK, W), lambda n: (n, 0)),
        out_shape=jax.ShapeDtypeStruct((L, W), BF16),
        compiler_params=_cp(("parallel",)),
    )(z1, ln_g, ln_b, ws, bst)


def sgu_bwd(z1, dus, ln_g, ln_b, ws, bst, *, name):
    L, W2 = z1.shape
    W = W2 // 2
    ng = W // LANES

    def body(z_ref, d_ref, g_ref, b_ref, ws_ref, bs_ref, dz_ref, dws_ref, dbs_ref, dg_ref, db_ref, dv_scr):
        @pl.when(pl.program_id(0) == 0)
        def _():
            dws_ref[...] = jnp.zeros_like(dws_ref)
            dbs_ref[...] = jnp.zeros_like(dbs_ref)
            dg_ref[...] = jnp.zeros_like(dg_ref)
            db_ref[...] = jnp.zeros_like(db_ref)

        zp = z_ref[...].astype(F32)
        z, th = _gelu_parts(zp)
        xhat, rstd = _layernorm(z[:, W:])
        vln = (xhat * g_ref[...] + b_ref[...]).astype(BF16)
        d = d_ref[...].astype(F32)
        lane = lax.broadcasted_iota(jnp.int32, (1, LANES), 1)
        dbs = jnp.zeros((BLK, LANES), F32)
        dgel = _gelu_grad(zp, th)
        for gi in range(ng):
            cs = slice(gi * LANES, (gi + 1) * LANES)
            wb = ws_ref[gi].astype(BF16)
            s = _dot(wb, vln[:, cs]) + bs_ref[:, gi:gi + 1]
            dz_ref[:, cs] = (d[:, cs] * s * dgel[:, cs]).astype(BF16)
            ds = d[:, cs] * z[:, cs]
            dbs = dbs + jnp.where(lane == gi, jnp.sum(ds, axis=-1, keepdims=True), 0.0)
            dsb = ds.astype(BF16)
            dws_ref[gi] += _dot_nt(dsb, vln[:, cs])
            dv_scr[:, cs] = _dot_tn(wb, dsb)
        dbs_ref[...] += dbs
        dvln = dv_scr[...]
        dg_ref[...] += _colsum(dvln * xhat)
        db_ref[...] += _colsum(dvln)
        dxh = dvln * g_ref[...]
        dv = rstd * (dxh - jnp.mean(dxh, axis=-1, keepdims=True) - xhat * jnp.mean(dxh * xhat, axis=-1, keepdims=True))
        dz_ref[:, W:] = (dv * dgel[:, W:]).astype(BF16)

    vec = _full((1, W))
    return pl.pallas_call(
        body, name=name, grid=(L // BLK,),
        in_specs=[pl.BlockSpec((BLK, W2), lambda n: (n, 0)), pl.BlockSpec((BLK, W), lambda n: (n, 0)), vec, vec,
                  _full((ng, LANES, LANES)), _full((BLK, ng))],
        out_specs=[pl.BlockSpec((BLK, W2), lambda n: (n, 0)), _full((ng, LANES, LANES)), _full((BLK, LANES)), vec, vec],
        out_shape=[jax.ShapeDtypeStruct((L, W2), BF16), jax.ShapeDtypeStruct((ng, LANES, LANES), F32),
                   jax.ShapeDtypeStruct((BLK, LANES), F32), jax.ShapeDtypeStruct((1, W), F32), jax.ShapeDtypeStruct((1, W), F32)],
        scratch_shapes=[pltpu.VMEM((BLK, W), F32)],
        compiler_params=_cp(("arbitrary",)),
    )(z1, dus, ln_g, ln_b, ws, bst)


def loss_grad(xo, target, *, tm, name):
    T, D = xo.shape

    def body(x_ref, t_ref, l_ref, d_ref):
        @pl.when(pl.program_id(0) == 0)
        def _():
            l_ref[...] = jnp.zeros_like(l_ref)

        e = x_ref[...] - t_ref[...]
        l_ref[...] += 0.5 * jnp.sum(jnp.mean(e * e, axis=-1, keepdims=True), axis=0, keepdims=True)
        d_ref[...] = e * (1.0 / D)

    row = pl.BlockSpec((tm, D), lambda i: (i, 0))
    return pl.pallas_call(
        body, name=name, grid=(T // tm,), in_specs=[row, row], out_specs=[_full((1, 1)), row],
        out_shape=[jax.ShapeDtypeStruct((1, 1), F32), jax.ShapeDtypeStruct((T, D), F32)],
        compiler_params=_cp(("arbitrary",)),
    )(xo, target)


def adamw(w, m, v, gparts, *, tr, name):
    R, Wd = w.shape
    S = gparts.shape[0]

    def body(w_ref, m_ref, v_ref, gp_ref, g_ref, d_ref, nm_ref, nv_ref):
        g = gp_ref[0].astype(F32)
        for s in range(1, S):
            g = g + gp_ref[s].astype(F32)
        m_ = ADAM_B1 * m_ref[...] + (1.0 - ADAM_B1) * g
        v_ = ADAM_B2 * v_ref[...] + (1.0 - ADAM_B2) * (g * g)
        g_ref[...] = g
        nm_ref[...] = m_
        nv_ref[...] = v_
        d_ref[...] = -ADAM_LR * ((m_ / BC1) / (jnp.sqrt(v_ / BC2) + ADAM_EPS) + ADAM_WD * w_ref[...])

    row = pl.BlockSpec((tr, Wd), lambda i: (i, 0))
    return pl.pallas_call(
        body, name=name, grid=(R // tr,),
        in_specs=[row, row, row, pl.BlockSpec((S, tr, Wd), lambda i: (0, i, 0))],
        out_specs=[row] * 4, out_shape=[jax.ShapeDtypeStruct((R, Wd), F32)] * 4,
        compiler_params=_cp(("parallel",)),
    )(w, m, v, gparts)


def ada_fwd_mm(cs, w_ada, b_loc, *, name):
    R, D = cs.shape
    nl, _, n = w_ada.shape

    def body(c_ref, w_ref, b_ref, s_ref, m_ref):
        c = c_ref[...]
        s = c * jax.nn.sigmoid(c)
        s_ref[...] = s
        for i in range(nl):
            m_ref[i] = _dot(s.astype(BF16), w_ref[i].astype(BF16)) + b_ref[i:i + 1, :]

    return pl.pallas_call(
        body, name=name, in_specs=[_full((R, D)), _full((nl, D, n)), _full((nl, n))],
        out_specs=[_full((R, D)), _full((nl, R, n))], grid=(1,),
        out_shape=[jax.ShapeDtypeStruct((R, D), F32), jax.ShapeDtypeStruct((nl, R, n), F32)],
        compiler_params=_cp(("arbitrary",)),
    )(cs, w_ada, b_loc)


def ada_bwd_mm(s, c_ctx, dall, w_ada, *, name):
    R, D = s.shape
    nl, _, n = w_ada.shape

    def body(s_ref, cc_ref, d_ref, w_ref, gw_ref, dcc_ref):
        sb = s_ref[...].astype(BF16)
        row = lax.broadcasted_iota(jnp.int32, (R, 1), 0)
        dctx = d_ref[0, 1:2, :]
        for dv in range(1, N_DEV):
            dctx = dctx + d_ref[dv, 1:2, :]
        for i in range(nl):
            dm = jnp.zeros((R, n), F32)
            for dv in range(N_DEV):
                dm = dm + jnp.where(row == dv, d_ref[dv, 2 * i:2 * i + 1, :], 0.0)
            if i == 0:
                dm = dm + jnp.where(row == N_DEV, dctx, 0.0)
            gw_ref[i] = _dot_tn(sb, dm.astype(BF16))
        cc = cc_ref[...]
        sg = jax.nn.sigmoid(cc)
        ds = _dot_nt(jnp.broadcast_to(dctx, (8, n)).astype(BF16), w_ref[0].astype(BF16))
        dcc_ref[...] = ds * (sg * (1.0 + cc * (1.0 - sg)))

    return pl.pallas_call(
        body, name=name, grid=(1,),
        in_specs=[_full((R, D)), _full((1, D)), _full((N_DEV, 3, n)), _full((nl, D, n))],
        out_specs=[_full((nl, D, n)), _full((8, D))],
        out_shape=[jax.ShapeDtypeStruct((nl, D, n), F32), jax.ShapeDtypeStruct((8, D), F32)],
        compiler_params=_cp(("arbitrary",)),
    )(s, c_ctx, dall, w_ada)


def _place():
    x, y, c = lax.axis_index("x"), lax.axis_index("y"), lax.axis_index("c")
    return x, y, c


def _lin(p):
    return 4 * p[0] + 2 * p[1] + p[2]


def all_gather_small(xb, *, reduce=False, name):
    R, W = xb.shape

    def body(x_ref, *rest):
        out_ref = rest[0]
        send_sems, recv_sems, local_sem = rest[-3:]
        x, y, c = _place()
        me, sibling = (x, y, c), (x, y, 1 - c)
        chips = [(1 - x, y), (x, 1 - y), (1 - x, 1 - y)]

        def copy(k, block, to, src=None):
            dst = out_ref.at[_lin(block)]
            return pltpu.make_async_remote_copy(
                src_ref=dst if src is None else src, dst_ref=dst, send_sem=send_sems.at[k], recv_sem=recv_sems.at[k],
                device_id=to, device_id_type=MESH)

        mine = pltpu.make_async_copy(x_ref, out_ref.at[_lin(me)], local_sem)
        mine.start()
        first = [copy(0, me, sibling, src=x_ref)]
        first += [copy(1 + j, me, (*chip, c), src=x_ref) for j, chip in enumerate(chips)]
        for cp in first:
            cp.start()
        passed = [copy(4 + j, (*chip, c), sibling) for j, chip in enumerate(chips)]
        for j, chip in enumerate(chips):
            copy(1 + j, (*chip, c), me).wait_recv()
            passed[j].start()
        copy(0, sibling, me).wait_recv()
        for j, chip in enumerate(chips):
            copy(4 + j, (*chip, 1 - c), me).wait_recv()
        for cp in first + passed:
            cp.wait_send()
        mine.wait()
        if reduce:
            acc = out_ref[0]
            for dv in range(1, N_DEV):
                acc = acc + out_ref[dv]
            rest[1][...] = acc

    vm = pl.BlockSpec(memory_space=pltpu.VMEM)
    out_shape = [jax.ShapeDtypeStruct((N_DEV, R, W), xb.dtype)]
    if reduce:
        out_shape.append(jax.ShapeDtypeStruct((R, W), xb.dtype))
    res = pl.pallas_call(
        body, name=name, in_specs=[vm], out_specs=[vm] * len(out_shape), out_shape=out_shape,
        scratch_shapes=[pltpu.SemaphoreType.DMA((7,)), pltpu.SemaphoreType.DMA((7,)), pltpu.SemaphoreType.DMA],
        compiler_params=pltpu.CompilerParams(vmem_limit_bytes=VMEM_LIMIT),
    )(xb)
    return res if reduce else res[0]


HBM_SPEC = pl.BlockSpec(memory_space=pltpu.HBM)
SEM_SPEC = pl.BlockSpec(memory_space=pltpu.SEMAPHORE)
ORDERED_EFFECT = pltpu.SideEffectType.DATAFLOW_SIDE_EFFECTING


def _exchange_copies(srcs, lands, sems, scatter):
    x, y, c = _place()
    me = _lin((x, y, c))
    for j in range(len(srcs)):
        r = lands[j].shape[0] // N_DEV
        block = lambda d, j=j, r=r: pl.ds(pl.multiple_of(d * r, 16), r)
        for k in range(1, N_DEV):
            peer = (x ^ (k >> 2), y ^ ((k >> 1) & 1), c ^ (k & 1))
            src = srcs[j].at[block(_lin(peer)), :] if scatter else srcs[j]
            mk = lambda dst, j=j, k=k, peer=peer, src=src: pltpu.make_async_remote_copy(
                src_ref=src, dst_ref=dst, send_sem=sems[2 * j].at[k - 1], recv_sem=sems[2 * j + 1].at[k - 1],
                device_id=peer, device_id_type=MESH)
            yield mk(lands[j].at[block(me), :]), mk(lands[j].at[block(_lin(peer)), :])


def exchange_start(srcs, lands, *, scatter, name):
    nw = len(srcs)

    def body(*refs):
        ins, lz, local_sems = refs[:nw], refs[nw:2 * nw], refs[-1]
        for start, _ in _exchange_copies(ins, lz, refs[2 * nw:4 * nw], scatter):
            start.start()
        me = _lin(_place())
        own = []
        for j in range(nw):
            r = lz[j].shape[0] // N_DEV
            rows = pl.ds(pl.multiple_of(me * r, 16), r)
            own.append(pltpu.make_async_copy(ins[j].at[rows, :] if scatter else ins[j], lz[j].at[rows, :], local_sems.at[j]))
            own[-1].start()
        for cp in own:
            cp.wait()
        refs[-2][...] = jnp.zeros_like(refs[-2])

    thru = [pltpu.HBM(a.shape, a.dtype) for a in (*srcs, *lands)]
    res = pl.pallas_call(
        body, name=name, in_specs=[HBM_SPEC] * (2 * nw),
        out_specs=[SEM_SPEC] * (2 * nw) + [HBM_SPEC] * (2 * nw) + [pl.BlockSpec(memory_space=pltpu.VMEM)],
        out_shape=[pltpu.SemaphoreType.DMA((N_DEV - 1,))] * (2 * nw) + thru + [jax.ShapeDtypeStruct((8, LANES), F32)],
        scratch_shapes=[pltpu.SemaphoreType.DMA((nw,))],
        input_output_aliases={i: 2 * nw + i for i in range(2 * nw)},
        compiler_params=pltpu.CompilerParams(has_side_effects=ORDERED_EFFECT),
    )(*[pltpu.with_memory_space_constraint(a, pltpu.HBM) for a in (*srcs, *lands)])
    return res[:2 * nw], res[2 * nw:3 * nw], res[3 * nw:4 * nw], res[-1]


def exchange_wait(srcs, lands, sems, after, *, scatter, name):
    nw = len(srcs)

    def body(*refs):
        for _, arrive in _exchange_copies(refs[:nw], refs[nw:2 * nw], refs[2 * nw:4 * nw], scatter):
            arrive.wait_send()
            arrive.wait_recv()

    res = pl.pallas_call(
        body, name=name, in_specs=[HBM_SPEC] * (2 * nw) + [SEM_SPEC] * (2 * nw) + [pl.BlockSpec(memory_space=pl.ANY)],
        out_specs=[HBM_SPEC] * (2 * nw), out_shape=[pltpu.HBM(a.shape, a.dtype) for a in (*srcs, *lands)],
        input_output_aliases={i: i for i in range(2 * nw)},
        compiler_params=pltpu.CompilerParams(has_side_effects=ORDERED_EFFECT),
    )(*srcs, *lands, *sems, after)
    return res[nw:]


def _rope_tables(L):
    t = jnp.arange(L)
    inv = ROPE_BASE ** (-jnp.arange(ROPE_FREQS, dtype=F32) / ROPE_FREQS)
    ar = (t // GRID_W).astype(F32)[:, None] * inv
    ac = (t % GRID_W).astype(F32)[:, None] * inv
    z = jnp.zeros_like(ar)
    cos = jnp.concatenate([jnp.cos(ar), jnp.cos(ar), jnp.cos(ac), jnp.cos(ac)], axis=1)
    sa = jnp.concatenate([-jnp.sin(ar), z, -jnp.sin(ac), z], axis=1)
    sb = jnp.concatenate([z, jnp.sin(ar), z, jnp.sin(ac)], axis=1)
    return tuple(jnp.tile(a, (1, LANES // HEAD_DIM)) for a in (cos, sa, sb))


def _rows128(a):
    f = a.reshape(-1)
    n = -(-f.shape[0] // (8 * LANES)) * 8 * LANES
    return jnp.pad(f, (0, n - f.shape[0])).reshape(-1, LANES)


def _landing(rows, like):
    return lax.empty((N_DEV * rows, like.shape[1]), like.dtype)


def kernel(x, c, ctx, c_ctx, w_ada, b_ada, g_mix_pre, g_mix_post, g_ffn_pre, g_ffn_post, w_in_even, w_pool, pool_scale, attn_sink, w_out_even, w_in_odd, sgu_ln_g, sgu_ln_b, sgu_w, sgu_b, w_out_odd, w_ffn_up, ffn_conv_w, ffn_conv_b, w_ffn_down, loss_target, m_c_ctx, m_w_ada, m_b_ada, m_g_mix_pre, m_g_mix_post, m_g_ffn_pre, m_g_ffn_post, m_w_in_even, m_w_pool, m_pool_scale, m_attn_sink, m_w_out_even, m_w_in_odd, m_sgu_ln_g, m_sgu_ln_b, m_sgu_w, m_sgu_b, m_w_out_odd, m_w_ffn_up, m_ffn_conv_w, m_ffn_conv_b, m_w_ffn_down, v_c_ctx, v_w_ada, v_b_ada, v_g_mix_pre, v_g_mix_post, v_g_ffn_pre, v_g_ffn_post, v_w_in_even, v_w_pool, v_pool_scale, v_attn_sink, v_w_out_even, v_w_in_odd, v_sgu_ln_g, v_sgu_ln_b, v_sgu_w, v_sgu_b, v_w_out_odd, v_w_ffn_up, v_ffn_conv_w, v_ffn_conv_b, v_w_ffn_down):
    P = dict(c_ctx=c_ctx, w_ada=w_ada, b_ada=b_ada, g_mix_pre=g_mix_pre, g_mix_post=g_mix_post, g_ffn_pre=g_ffn_pre,
             g_ffn_post=g_ffn_post, w_in_even=w_in_even, w_pool=w_pool, pool_scale=pool_scale, attn_sink=attn_sink,
             w_out_even=w_out_even, w_in_odd=w_in_odd, sgu_ln_g=sgu_ln_g, sgu_ln_b=sgu_ln_b, sgu_w=sgu_w, sgu_b=sgu_b,
             w_out_odd=w_out_odd, w_ffn_up=w_ffn_up, ffn_conv_w=ffn_conv_w, ffn_conv_b=ffn_conv_b, w_ffn_down=w_ffn_down)
    M = dict(c_ctx=m_c_ctx, w_ada=m_w_ada, b_ada=m_b_ada, g_mix_pre=m_g_mix_pre, g_mix_post=m_g_mix_post, g_ffn_pre=m_g_ffn_pre,
             g_ffn_post=m_g_ffn_post, w_in_even=m_w_in_even, w_pool=m_w_pool, pool_scale=m_pool_scale, attn_sink=m_attn_sink,
             w_out_even=m_w_out_even, w_in_odd=m_w_in_odd, sgu_ln_g=m_sgu_ln_g, sgu_ln_b=m_sgu_ln_b, sgu_w=m_sgu_w, sgu_b=m_sgu_b,
             w_out_odd=m_w_out_odd, w_ffn_up=m_w_ffn_up, ffn_conv_w=m_ffn_conv_w, ffn_conv_b=m_ffn_conv_b, w_ffn_down=m_w_ffn_down)
    V = dict(c_ctx=v_c_ctx, w_ada=v_w_ada, b_ada=v_b_ada, g_mix_pre=v_g_mix_pre, g_mix_post=v_g_mix_post, g_ffn_pre=v_g_ffn_pre,
             g_ffn_post=v_g_ffn_post, w_in_even=v_w_in_even, w_pool=v_w_pool, pool_scale=v_pool_scale, attn_sink=v_attn_sink,
             w_out_even=v_w_out_even, w_in_odd=v_w_in_odd, sgu_ln_g=v_sgu_ln_g, sgu_ln_b=v_sgu_ln_b, sgu_w=v_sgu_w, sgu_b=v_sgu_b,
             w_out_odd=v_w_out_odd, w_ffn_up=v_w_ffn_up, ffn_conv_w=v_ffn_conv_w, ffn_conv_b=v_ffn_conv_b, w_ffn_down=v_w_ffn_down)

    x = x[0]
    ctx = ctx[0]
    target = loss_target[0]
    L, D = x.shape
    C = ctx.shape[0]
    tm = min(512, L)
    tm_up = min(1024, L)
    conv_rows = min(256, L)
    me = 4 * lax.axis_index("x") + 2 * lax.axis_index("y") + lax.axis_index("c")
    n_ada = w_ada.shape[2]
    F = w_ffn_down.shape[1] * N_DEV
    half_f = F // 2

    n_cw = ffn_conv_w.shape[2]
    small = jnp.concatenate([_rows128(c), _rows128(sgu_ln_g), _rows128(sgu_ln_b), _rows128(ffn_conv_w)], axis=0)
    small_all = all_gather_small(small, name="gather_small_inputs")
    c_all = small_all[:, :8].reshape(N_DEV, D)
    ln_g = small_all[:, 8].reshape(1, D)
    ln_b = small_all[:, 16].reshape(1, D)
    conv_w = small_all[:, 24:].reshape(N_DEV, -1)[:, :2 * 3 * n_cw].reshape(N_DEV, 2, 3, n_cw)
    conv_w = conv_w.transpose(1, 2, 0, 3).reshape(2, 3, 2 * F)

    cs = jnp.concatenate([c_all, c_ctx[None, :], jnp.zeros((7, D), F32)], axis=0)
    b_loc = lax.dynamic_slice(b_ada, (0, me * n_ada), (2, n_ada))
    silu_c, mods_loc = ada_fwd_mm(cs, w_ada, b_loc, name="ada_fwd")
    mods_all = all_gather_small(mods_loc.reshape(-1, LANES), name="gather_mods")

    shards = [s.astype(BF16) for s in (w_in_even[0].T, w_out_even[0], w_ffn_up[0].T, w_ffn_down[0],
                                       w_in_odd[0].T, w_out_odd[0], w_ffn_up[1].T, w_ffn_down[1])]
    shards, mods_all = lax.optimization_barrier((shards, mods_all))
    w_sems, w_srcs, w_lands, _ = exchange_start(shards, [_landing(s.shape[0], s) for s in shards], scatter=False, name="gather_start")

    def weight(j, after):
        return exchange_wait([w_srcs[j]], [w_lands[j]], w_sems[2 * j:2 * j + 2], after, scatter=False, name=f"gather_wait_{j}")[0]

    mods_all = mods_all.reshape(N_DEV, 2, 16, n_ada).transpose(1, 2, 0, 3).reshape(2, 16, 6 * D)
    mod = lambda i, row: [m_[None, :] for m_ in jnp.split(lax.dynamic_index_in_dim(mods_all[i], row, 0, False), 6)]
    sh_m, sc_m, gt_m, sh_f, sc_f, gt_f = zip(mod(0, me), mod(1, me))
    csh_m, csc_m = mod(0, N_DEV)[:2]

    row = lambda a, i: a[i][None, :]

    cos, sa, sb = _rope_tables(L)
    sink = attn_sink[0]
    bst = sgu_b[0].T
    wup, wdn = [None, None], [None, None]

    def ffn_fwd(i, xin):
        wup[i] = weight(2 + 4 * i, xin)
        h, hu = pre_mm(xin, row(g_ffn_pre, i), sh_f[i], sc_f[i], wup[i], tm=tm_up, tn=half_f, name=f"ffn_up_{i}")
        a = conv_fwd(hu, conv_w[i], ffn_conv_b[i][None, :], rows=conv_rows, wblk=2 * LANES, name=f"ffn_conv_{i}")
        wdn[i] = weight(3 + 4 * i, a)
        f, xo = mm_post(a, wdn[i], xin, row(g_ffn_post, i), gt_f[i], tm=tm, name=f"ffn_down_{i}")
        return h, hu, a, f, xo

    win_e = weight(0, sh_m[0])
    h0, u, q, kv = inproj_even(x, row(g_mix_pre, 0), sh_m[0], sc_m[0], win_e, cos, sa, sb, tm=tm, name="in_even")
    hc, kvc = pre_mm(ctx, row(g_mix_pre, 0), csh_m, csc_m, win_e, tm=C, tn=2 * LANES, w_row_off=8 * LANES, name="in_even_ctx")
    pa = jnp.concatenate([pool_fwd(u, w_pool[0], pool_scale, name="pool_fwd"),
                          attn_fwd(q, kv, kvc, sink, name="attn_fwd")], axis=1)
    wout_e = weight(1, pa)
    y0, x1 = mm_post(pa, wout_e, x, row(g_mix_post, 0), gt_m[0], tm=tm, name="out_even")
    h1, hu0, a0, f0, x2 = ffn_fwd(0, x1)
    win_o = weight(4, x2)
    h2, z1 = pre_mm(x2, row(g_mix_pre, 1), sh_m[1], sc_m[1], win_o, tm=tm_up, tn=D, name="in_odd")
    us = sgu_fwd(z1, ln_g, ln_b, sgu_w[0], bst, name="sgu_fwd")
    wout_o = weight(5, us)
    y1, x3 = mm_post(us, wout_o, x2, row(g_mix_post, 1), gt_m[1], tm=tm, name="out_odd")
    h3, hu1, a1, f1, x4 = ffn_fwd(1, x3)
    loss_part, dx4 = loss_grad(x4, target, tm=tm, name="loss")
    loss = lax.psum(loss_part[0, 0], ("x", "y", "c"))

    g_srcs, g_lands, g_sems = [], [], []

    def scatter(grads, nm):
        sems, srcs, lands, tok = exchange_start(grads, [_landing(g.shape[0] // N_DEV, g) for g in grads], scatter=True, name=nm)
        g_srcs.extend(srcs)
        g_lands.extend(lands)
        g_sems.extend(sems)
        return tok[0:1, 0:1]

    def ffn_bwd(i, dxo, xin, h, hu, a, f, g_post):
        dyf, da, dg_post, dgt = post_bwd_mm(dxo, f, g_post, gt_f[i], wdn[i], tm=tm, name=f"ffn_down_bwd_{i}")
        dhg, dhu, dcwg, dcwu, dcbg, dcbu = conv_bwd(da, hu, conv_w[i], ffn_conv_b[i][None, :], rows=conv_rows, wblk=2 * LANES,
                                                    name=f"ffn_conv_bwd_{i}")
        dxin, dg_pre, dsh, dsc = mm_pre_bwd([dhg, dhu], wup[i], xin, dxo, row(g_ffn_pre, i), sc_f[i], tm=tm, tk=half_f,
                                            name=f"ffn_up_bwd_{i}")
        g_dn = wgrad([a], dyf, tr=2 * LANES, name=f"wgrad_down_{i}")
        g_up = wgrad([dhg, dhu], h, tr=2 * LANES, name=f"wgrad_up_{i}")
        tok = scatter([g_dn, g_up], f"scatter_start_ffn_{i}")
        return dxin, tok, dict(g_ffn_post=dg_post, g_ffn_pre=dg_pre, gt_f=dgt, sh_f=dsh, sc_f=dsc,
                               ffn_conv_w=jnp.concatenate([dcwg, dcwu], axis=1), ffn_conv_b=jnp.concatenate([dcbg, dcbu], axis=1)[0])

    dx3, tok, sf1 = ffn_bwd(1, dx4, x3, h3, hu1, a1, f1, row(g_ffn_post, 1))
    dy1, dus, dg_mpost1, dgt_m1 = post_bwd_mm(dx3, y1, row(g_mix_post, 1) + tok, gt_m[1], wout_o, tm=tm, name="out_odd_bwd")
    dz1, dws, dbs, dlng, dlnb = sgu_bwd(z1, dus, ln_g, ln_b, sgu_w[0], bst, name="sgu_bwd")
    dx2, dg_mpre1, dsh_m1, dsc_m1 = mm_pre_bwd([dz1], win_o, x2, dx3, row(g_mix_pre, 1), sc_m[1], tm=tm, tk=D, name="in_odd_bwd")
    tok = scatter([wgrad([us], dy1, tr=2 * LANES, name="wgrad_out_odd"), wgrad([dz1], h2, tr=2 * LANES, name="wgrad_in_odd")],
                  "scatter_start_mix_1")

    dx1, tok, sf0 = ffn_bwd(0, dx2, x1, h1, hu0, a0, f0, row(g_ffn_post, 0) + tok)
    dy0, dpa, dg_mpost0, dgt_m0 = post_bwd_mm(dx1, y0, row(g_mix_post, 0) + tok, gt_m[0], wout_e, tm=tm, name="out_even_bwd")
    du, dwp, dps = pool_bwd(u, dpa, w_pool[0], pool_scale, name="pool_bwd")
    dq, dkv, dkvc, dsink = attn_bwd(q, kv, kvc, sink, dpa, cos, sa, sb, name="attn_bwd")
    dz0 = jnp.concatenate([du, dq, dkv], axis=1)
    dzc = jnp.concatenate([jnp.zeros((C, 8 * LANES), BF16), dkvc], axis=1)
    tok = scatter([wgrad([pa], dy0, tr=2 * LANES, name="wgrad_out_even"),
                   wgrad([dz0], h0, tr=2 * LANES, extra=(dzc, hc), name="wgrad_in_even")], "scatter_start_mix_0")
    grad_x, dg_mpre0, dsh_m0, dsc_m0 = mm_pre_bwd([dz0], win_e, x, dx1, row(g_mix_pre, 0) + tok, sc_m[0], tm=tm, tk=dz0.shape[1],
                                                  name="in_even_bwd")
    _, dg_mpre0c, dcsh, dcsc = mm_pre_bwd([dkvc], win_e, ctx, None, row(g_mix_pre, 0), csc_m, tm=C, tk=2 * LANES,
                                          w_row_off=8 * LANES, name="in_even_ctx_bwd")

    slots = exchange_wait(g_srcs, g_lands, g_sems, dcsh, scatter=True, name="scatter_wait")
    out = {}

    def update(name, idx, land, transposed):
        w_, m_, v_ = (a[idx].T if transposed else a[idx] for a in (P[name], M[name], V[name]))
        r = w_.shape[0]
        tr = r // 4 if r % 64 == 0 and r > 256 else r
        res = adamw(w_, m_, v_, land.reshape(N_DEV, r, land.shape[1]), tr=tr, name=f"adamw_{name}_{idx}")
        for kind, val in zip(("grad", "delta", "new_m", "new_v"), res):
            out.setdefault((kind, name), []).append(val.T if transposed else val)

    update("w_in_even", 0, slots[7], True)
    update("w_out_even", 0, slots[6], False)
    update("w_in_odd", 0, slots[3], True)
    update("w_out_odd", 0, slots[2], False)
    update("w_ffn_up", 0, slots[5], True)
    update("w_ffn_down", 0, slots[4], False)
    update("w_ffn_up", 1, slots[1], True)
    update("w_ffn_down", 1, slots[0], False)

    zero = jnp.zeros((1, D), F32)
    dmod0 = jnp.concatenate([dsh_m0, dsc_m0, dgt_m0, sf0["sh_f"], sf0["sc_f"], sf0["gt_f"]], axis=1)
    dmodc = jnp.concatenate([dcsh, dcsc, zero, zero, zero, zero], axis=1)
    dmod1 = jnp.concatenate([dsh_m1, dsc_m1, dgt_m1, sf1["sh_f"], sf1["sc_f"], sf1["gt_f"]], axis=1)
    dmods = jnp.concatenate([dmod0, dmodc, dmod1], axis=0)
    dmods_all = all_gather_small(dmods.reshape(-1, LANES), name="gather_dmods").reshape(N_DEV, 3, N_DEV, n_ada)
    dall = lax.dynamic_index_in_dim(dmods_all, me, 2, False)
    g_w_ada, dcc = ada_bwd_mm(silu_c, c_ctx[None, :], dall, w_ada, name="ada_bwd")
    nl = w_ada.shape[0]
    res = adamw(w_ada.reshape(nl * D, n_ada), m_w_ada.reshape(nl * D, n_ada), v_w_ada.reshape(nl * D, n_ada),
                g_w_ada.reshape(1, nl * D, n_ada), tr=nl * D // 8, name="adamw_w_ada")
    for kind, val in zip(("grad", "delta", "new_m", "new_v"), res):
        out[(kind, "w_ada")] = val.reshape(nl, D, n_ada)

    rep = dict(
        c_ctx=dcc[0],
        b_ada=jnp.stack([dmod0[0] + dmodc[0], dmod1[0]]),
        g_mix_pre=jnp.concatenate([dg_mpre0 + dg_mpre0c, dg_mpre1]),
        g_mix_post=jnp.concatenate([dg_mpost0, dg_mpost1]),
        g_ffn_pre=jnp.concatenate([sf0["g_ffn_pre"], sf1["g_ffn_pre"]]),
        g_ffn_post=jnp.concatenate([sf0["g_ffn_post"], sf1["g_ffn_post"]]),
        w_pool=dwp[None], pool_scale=dps, attn_sink=dsink[:, :N_Q_HEADS],
        sgu_w=dws[None], sgu_b=dbs[:, :sgu_b.shape[1]].T[None],
        ffn_conv_b=jnp.stack([sf0["ffn_conv_b"], sf1["ffn_conv_b"]]),
    )
    rep_names = list(rep)
    conv_g = jnp.stack([sf0["ffn_conv_w"], sf1["ffn_conv_w"]]).reshape(2, 3, N_DEV, n_cw).transpose(2, 0, 1, 3)
    shard_full = dict(sgu_ln_g=dlng.reshape(N_DEV, LANES), sgu_ln_b=dlnb.reshape(N_DEV, LANES),
                      ffn_conv_w=jnp.concatenate([_rows128(conv_g[d]) for d in range(N_DEV)], axis=0))
    pieces = [_rows128(rep[k]) for k in rep_names] + [shard_full[k] for k in shard_full]
    sizes = [p.shape[0] for p in pieces]
    _, gsum = all_gather_small(jnp.concatenate(pieces, axis=0), reduce=True, name="allreduce_small_grads")
    offs = [sum(sizes[:i]) for i in range(len(sizes))]
    n_rep = len(rep_names)
    cw_rows = sizes[-1] // N_DEV
    g_own = [gsum[offs[i]:offs[i] + sizes[i]] for i in range(n_rep)]
    g_own.append(_rows128(lax.dynamic_slice_in_dim(gsum, offs[n_rep] + me, 1, 0)))
    g_own.append(_rows128(lax.dynamic_slice_in_dim(gsum, offs[n_rep + 1] + me, 1, 0)))
    g_own.append(lax.dynamic_slice_in_dim(gsum, offs[n_rep + 2] + me * cw_rows, cw_rows, 0))
    small_names = rep_names + list(shard_full)
    packs = [jnp.concatenate([_rows128(src[k]) for k in small_names], axis=0) for src in (P, M, V)]
    n_pack = packs[0].shape[0]
    gp = jnp.concatenate(g_own, axis=0)[None]
    res = adamw(*packs, gp, tr=n_pack, name="adamw_small")
    o = 0
    for k in small_names:
        n = _rows128(P[k]).shape[0]
        for kind, val in zip(("grad", "delta", "new_m", "new_v"), res):
            out[(kind, k)] = val[o:o + n].reshape(-1)[:P[k].size].reshape(P[k].shape)
        o += n
    assert o == n_pack

    names = list(P)
    final = [loss, grad_x[None]]
    for kind in ("grad", "delta", "new_m", "new_v"):
        for k in names:
            val = out[(kind, k)]
            final.append(jnp.stack(val) if isinstance(val, list) else val)
    return tuple(final)
```

```python
import functools
import math

import jax
import jax.numpy as jnp
from jax import lax
from jax.experimental import pallas as pl
from jax.experimental.pallas import tpu as pltpu

F32 = jnp.float32
BF16 = jnp.bfloat16
MESH = pl.DeviceIdType.MESH
N_DEV = 8
LANES = 128
VMEM_LIMIT = 48 * 1024 * 1024
EPS = 1e-6
NEG_INF = -1e30
GRID_W = 64
WINDOW = 128
BLK = 128
HEAD_DIM = 64
N_Q_HEADS = 8
N_KV_HEADS = 2
GQA = N_Q_HEADS // N_KV_HEADS
POOL_WINDOWS = (2, 4, 8, 16)
ROPE_BASE = 10000.0
ROPE_FREQS = HEAD_DIM // 4
PAD = 16
ADAM_LR, ADAM_B1, ADAM_B2, ADAM_EPS, ADAM_WD, ADAM_STEP = 0.001, 0.9, 0.999, 1e-08, 0.01, 10
BC1 = 1.0 - ADAM_B1 ** ADAM_STEP
BC2 = 1.0 - ADAM_B2 ** ADAM_STEP
SQRT_2_OVER_PI = math.sqrt(2.0 / math.pi)
GELU_C = 0.044715


def _cp(sem=None):
    return pltpu.CompilerParams(dimension_semantics=sem, vmem_limit_bytes=VMEM_LIMIT)


def _dot(a, b):
    return jnp.dot(a, b, preferred_element_type=F32)


def _dot_nt(a, b):
    return lax.dot_general(a, b, (((1,), (1,)), ((), ())), preferred_element_type=F32)


def _dot_tn(a, b):
    return lax.dot_general(a, b, (((0,), (0,)), ((), ())), preferred_element_type=F32)


def _rms(x):
    r = lax.rsqrt(jnp.mean(x * x, axis=-1, keepdims=True) + EPS)
    return x * r, r


def _rms_bwd(dn, n, r):
    return r * (dn - n * jnp.mean(dn * n, axis=-1, keepdims=True))


def _colsum(a):
    return jnp.sum(a, axis=0, keepdims=True)


def _rope(x, c, sa, sb):
    return x * c + pltpu.roll(x, LANES - ROPE_FREQS, 1) * sa + pltpu.roll(x, ROPE_FREQS, 1) * sb


def _full(shape):
    return pl.BlockSpec(shape, lambda *_: (0,) * len(shape))


def pre_mm(x, g, sh, sc, wt, *, tm, tn, w_row_off=0, name):
    T, D = x.shape
    n_rows = wt.shape[0] - w_row_off
    off = w_row_off // tn

    def body(x_ref, g_ref, sh_ref, sc_ref, w_ref, h_ref, z_ref):
        @pl.when(pl.program_id(1) == 0)
        def _():
            n, _ = _rms(x_ref[...])
            h_ref[...] = (n * g_ref[...] * (1.0 + sc_ref[...]) + sh_ref[...]).astype(BF16)

        z_ref[...] = _dot_nt(h_ref[...], w_ref[...]).astype(BF16)

    vec = pl.BlockSpec((1, D), lambda i, j: (0, 0))
    return pl.pallas_call(
        body, name=name, grid=(T // tm, n_rows // tn),
        in_specs=[pl.BlockSpec((tm, D), lambda i, j: (i, 0)), vec, vec, vec, pl.BlockSpec((tn, D), lambda i, j: (j + off, 0))],
        out_specs=[pl.BlockSpec((tm, D), lambda i, j: (i, 0)), pl.BlockSpec((tm, tn), lambda i, j: (i, j))],
        out_shape=[jax.ShapeDtypeStruct((T, D), BF16), jax.ShapeDtypeStruct((T, n_rows), BF16)],
        compiler_params=_cp(("parallel", "arbitrary")),
    )(x, g, sh, sc, wt)


def inproj_even(x, g, sh, sc, wt, cos, sa, sb, *, tm, name):
    T, D = x.shape
    N = wt.shape[0]

    def body(x_ref, g_ref, sh_ref, sc_ref, w_ref, c_ref, sa_ref, sb_ref, h_ref, u_ref, q_ref, kv_ref):
        n, _ = _rms(x_ref[...])
        h = (n * g_ref[...] * (1.0 + sc_ref[...]) + sh_ref[...]).astype(BF16)
        h_ref[...] = h
        z = _dot_nt(h, w_ref[...])
        u_ref[...] = z[:, :4 * LANES]
        c, a, b = c_ref[...], sa_ref[...], sb_ref[...]
        for s in range(4):
            q_ref[:, s * LANES:(s + 1) * LANES] = _rope(z[:, (4 + s) * LANES:(5 + s) * LANES], c, a, b).astype(BF16)
        kv_ref[:, :LANES] = _rope(z[:, 8 * LANES:9 * LANES], c, a, b).astype(BF16)
        kv_ref[:, LANES:] = z[:, 9 * LANES:].astype(BF16)

    vec = pl.BlockSpec((1, D), lambda i: (0, 0))
    row = lambda w: pl.BlockSpec((tm, w), lambda i: (i, 0))
    return pl.pallas_call(
        body, name=name, grid=(T // tm,),
        in_specs=[row(D), vec, vec, vec, _full((N, D)), row(LANES), row(LANES), row(LANES)],
        out_specs=[row(D), row(4 * LANES), row(4 * LANES), row(2 * LANES)],
        out_shape=[jax.ShapeDtypeStruct((T, D), BF16), jax.ShapeDtypeStruct((T, 4 * LANES), F32),
                   jax.ShapeDtypeStruct((T, 4 * LANES), BF16), jax.ShapeDtypeStruct((T, 2 * LANES), BF16)],
        compiler_params=_cp(("parallel",)),
    )(x, g, sh, sc, wt, cos, sa, sb)


def mm_post(a, w, x, g, gt, *, tm, name):
    T, K = a.shape
    D = w.shape[1]

    def body(a_ref, w_ref, x_ref, g_ref, gt_ref, y_ref, xn_ref):
        y = _dot(a_ref[...], w_ref[...])
        n, _ = _rms(y)
        y_ref[...] = y
        xn_ref[...] = x_ref[...] + gt_ref[...] * (n * g_ref[...])

    vec = pl.BlockSpec((1, D), lambda i: (0, 0))
    row = lambda w_: pl.BlockSpec((tm, w_), lambda i: (i, 0))
    return pl.pallas_call(
        body, name=name, grid=(T // tm,),
        in_specs=[row(K), _full((K, D)), row(D), vec, vec],
        out_specs=[row(D), row(D)],
        out_shape=[jax.ShapeDtypeStruct((T, D), F32), jax.ShapeDtypeStruct((T, D), F32)],
        compiler_params=_cp(("parallel",)),
    )(a, w, x, g, gt)


def post_bwd_mm(dxn, y, g, gt, w, *, tm, name):
    T, D = y.shape
    K = w.shape[0]

    def body(dxn_ref, y_ref, g_ref, gt_ref, w_ref, dy_ref, da_ref, dg_ref, dgt_ref):
        @pl.when(pl.program_id(0) == 0)
        def _():
            dg_ref[...] = jnp.zeros_like(dg_ref)
            dgt_ref[...] = jnp.zeros_like(dgt_ref)

        d = dxn_ref[...]
        n, r = _rms(y_ref[...])
        g_, gt_ = g_ref[...], gt_ref[...]
        dg_ref[...] += _colsum(d * gt_ * n)
        dgt_ref[...] += _colsum(d * g_ * n)
        dy = _rms_bwd(d * (gt_ * g_), n, r).astype(BF16)
        dy_ref[...] = dy
        da_ref[...] = _dot_nt(dy, w_ref[...]).astype(BF16)

    vec = pl.BlockSpec((1, D), lambda i: (0, 0))
    row = lambda w_: pl.BlockSpec((tm, w_), lambda i: (i, 0))
    return pl.pallas_call(
        body, name=name, grid=(T // tm,),
        in_specs=[row(D), row(D), vec, vec, _full((K, D))],
        out_specs=[row(D), row(K), vec, vec],
        out_shape=[jax.ShapeDtypeStruct((T, D), BF16), jax.ShapeDtypeStruct((T, K), BF16),
                   jax.ShapeDtypeStruct((1, D), F32), jax.ShapeDtypeStruct((1, D), F32)],
        compiler_params=_cp(("arbitrary",)),
    )(dxn, y, g, gt, w)


def mm_pre_bwd(dzs, wt, x, dres, g, sc, *, tm, tk, w_row_off=0, name):
    T, N = dzs[0].shape
    D = x.shape[1]
    nk = N // tk
    npart = len(dzs)
    off = w_row_off // tk
    has_res = dres is not None

    def body(*refs):
        dz_refs = refs[:npart]
        w_refs = refs[npart:2 * npart]
        rest = refs[2 * npart:]
        x_ref = rest[0]
        dres_ref = rest[1] if has_res else None
        g_ref, sc_ref, dx_ref, dg_ref, dsh_ref, dsc_ref, acc = rest[1 + has_res:]
        i, k = pl.program_id(0), pl.program_id(1)

        @pl.when(jnp.logical_and(i == 0, k == 0))
        def _():
            dg_ref[...] = jnp.zeros_like(dg_ref)
            dsh_ref[...] = jnp.zeros_like(dsh_ref)
            dsc_ref[...] = jnp.zeros_like(dsc_ref)

        part = _dot(dz_refs[0][...], w_refs[0][...])
        for p in range(1, npart):
            part = part + _dot(dz_refs[p][...], w_refs[p][...])

        @pl.when(k == 0)
        def _():
            acc[...] = part

        @pl.when(k > 0)
        def _():
            acc[...] += part

        @pl.when(k == nk - 1)
        def _():
            dh = acc[...]
            n, r = _rms(x_ref[...])
            g_, s1 = g_ref[...], 1.0 + sc_ref[...]
            dsh_ref[...] += _colsum(dh)
            dsc_ref[...] += _colsum(dh * n * g_)
            dg_ref[...] += _colsum(dh * s1 * n)
            dxp = _rms_bwd(dh * (g_ * s1), n, r)
            dx_ref[...] = dxp + dres_ref[...] if has_res else dxp

    vec = pl.BlockSpec((1, D), lambda i, k: (0, 0))
    row = pl.BlockSpec((tm, D), lambda i, k: (i, 0))
    w_specs = [pl.BlockSpec((tk, D), (lambda i, k, p=p: (k + off + p * nk, 0))) for p in range(npart)]
    res_specs, res_args = ([row], (dres,)) if has_res else ([], ())
    return pl.pallas_call(
        body, name=name, grid=(T // tm, nk),
        in_specs=[pl.BlockSpec((tm, tk), lambda i, k: (i, k))] * npart + w_specs + [row] + res_specs + [vec, vec],
        out_specs=[row, vec, vec, vec],
        out_shape=[jax.ShapeDtypeStruct((T, D), F32)] + [jax.ShapeDtypeStruct((1, D), F32)] * 3,
        scratch_shapes=[pltpu.VMEM((tm, D), F32)],
        compiler_params=_cp(("arbitrary", "arbitrary")),
    )(*dzs, *([wt] * npart), x, *res_args, g, sc)


def wgrad(a_parts, b, *, tr, extra=None, name):
    T, R = a_parts[0].shape
    D = b.shape[1]
    npart = len(a_parts)
    nr = R // tr

    def body(*refs):
        a_refs, b_ref = refs[:npart], refs[npart]
        g_ref = refs[-1]
        for p in range(npart):
            @pl.when(pl.program_id(0) // nr == p)
            def _():
                acc = _dot_tn(a_refs[p][...], b_ref[...])
                if extra is not None:
                    acc += _dot_tn(refs[npart + 1][...], refs[npart + 2][...])
                g_ref[...] = acc.astype(BF16)

    in_specs = [pl.BlockSpec((T, tr), (lambda r, p=p: (0, jnp.clip(r - p * nr, 0, nr - 1)))) for p in range(npart)]
    in_specs.append(_full((T, D)))
    args = [*a_parts, b]
    if extra is not None:
        a2, b2 = extra
        in_specs += [pl.BlockSpec((a2.shape[0], tr), lambda r: (0, r)), _full(b2.shape)]
        args += [a2, b2]
    return pl.pallas_call(
        body, name=name, grid=(npart * nr,),
        in_specs=in_specs, out_specs=pl.BlockSpec((tr, D), lambda r: (r, 0)),
        out_shape=jax.ShapeDtypeStruct((npart * R, D), BF16),
        compiler_params=_cp(("parallel",)),
    )(*args)


def _conv_ext(ref, r0, rows, total):
    top = ref[pl.ds(pl.multiple_of(jnp.maximum(r0 - PAD, 0), PAD), PAD), :]
    mid = ref[pl.ds(r0, rows), :]
    bot = ref[pl.ds(pl.multiple_of(jnp.minimum(r0 + rows, total - PAD), PAD), PAD), :]
    top = jnp.where(r0 > 0, top, jnp.zeros_like(top))
    bot = jnp.where(r0 + rows < total, bot, jnp.zeros_like(bot))
    return jnp.concatenate([top, mid, bot], axis=0).astype(F32)


def _shift_rows(a, k):
    return pltpu.roll(a, k % a.shape[0], 0)


def _conv3(x, w, b):
    return w[0:1] * _shift_rows(x, 1) + w[1:2] * x + w[2:3] * _shift_rows(x, -1) + b


def _gate_up_specs(rows_, wblk, nb):
    return [pl.BlockSpec((rows_, wblk), lambda j: (0, j)), pl.BlockSpec((rows_, wblk), lambda j: (0, j + nb))]


def conv_fwd(hu, cw, cb, *, rows, wblk, name):
    L, N2 = hu.shape
    nb = N2 // 2 // wblk
    nchunk = L // rows

    def body(hg_ref, hu_ref, wg_ref, wu_ref, bg_ref, bu_ref, a_ref):
        def chunk(ci, carry):
            r0 = pl.multiple_of(ci * rows, rows)
            gate = _conv3(_conv_ext(hg_ref, r0, rows, L), wg_ref[...], bg_ref[...])[PAD:PAD + rows]
            up = _conv3(_conv_ext(hu_ref, r0, rows, L), wu_ref[...], bu_ref[...])[PAD:PAD + rows]
            a_ref[pl.ds(r0, rows), :] = (gate * jax.nn.sigmoid(gate) * up).astype(BF16)
            return carry

        lax.fori_loop(0, nchunk, chunk, 0)

    return pl.pallas_call(
        body, name=name, grid=(nb,),
        in_specs=_gate_up_specs(L, wblk, nb) + _gate_up_specs(3, wblk, nb) + _gate_up_specs(1, wblk, nb),
        out_specs=pl.BlockSpec((L, wblk), lambda j: (0, j)),
        out_shape=jax.ShapeDtypeStruct((L, N2 // 2), BF16),
        compiler_params=_cp(("parallel",)),
    )(hu, hu, cw, cw, cb, cb)


def conv_bwd(da, hu, cw, cb, *, rows, wblk, name):
    L, N2 = hu.shape
    F = N2 // 2
    nb = F // wblk
    nchunk = L // rows
    mid = slice(PAD, PAD + rows)

    def body(da_ref, hg_ref, hu_ref, wg_ref, wu_ref, bg_ref, bu_ref, dg_ref, du_ref, dwg_ref, dwu_ref, dbg_ref, dbu_ref):
        for ref in (dwg_ref, dwu_ref, dbg_ref, dbu_ref):
            ref[...] = jnp.zeros_like(ref)

        def half_bwd(x, dh, w_ref, dx_ref, dw_ref, db_ref, r0):
            w = w_ref[...]
            nxt, prv = _shift_rows(dh, -1)[mid], _shift_rows(dh, 1)[mid]
            dhm, xm = dh[mid], x[mid]
            dx_ref[pl.ds(r0, rows), :] = (w[0:1] * nxt + w[1:2] * dhm + w[2:3] * prv).astype(BF16)
            db_ref[...] += _colsum(dhm)
            dw_ref[0:1, :] += _colsum(nxt * xm)
            dw_ref[1:2, :] += _colsum(dhm * xm)
            dw_ref[2:3, :] += _colsum(prv * xm)

        def chunk(ci, carry):
            r0 = pl.multiple_of(ci * rows, rows)
            xg = _conv_ext(hg_ref, r0, rows, L)
            xu = _conv_ext(hu_ref, r0, rows, L)
            d = _conv_ext(da_ref, r0, rows, L)
            gate = _conv3(xg, wg_ref[...], bg_ref[...])
            up = _conv3(xu, wu_ref[...], bu_ref[...])
            sg = jax.nn.sigmoid(gate)
            silu = gate * sg
            half_bwd(xu, d * silu, wu_ref, du_ref, dwu_ref, dbu_ref, r0)
            half_bwd(xg, d * up * (sg + silu * (1.0 - sg)), wg_ref, dg_ref, dwg_ref, dbg_ref, r0)
            return carry

        lax.fori_loop(0, nchunk, chunk, 0)

    blk = lambda r: pl.BlockSpec((r, wblk), lambda j: (0, j))
    return pl.pallas_call(
        body, name=name, grid=(nb,),
        in_specs=[blk(L)] + _gate_up_specs(L, wblk, nb) + _gate_up_specs(3, wblk, nb) + _gate_up_specs(1, wblk, nb),
        out_specs=[blk(L), blk(L), blk(3), blk(3), blk(1), blk(1)],
        out_shape=[jax.ShapeDtypeStruct((L, F), BF16)] * 2 + [jax.ShapeDtypeStruct((3, F), F32)] * 2
        + [jax.ShapeDtypeStruct((1, F), F32)] * 2,
        compiler_params=_cp(("parallel",)),
    )(da, hu, hu, cw, cw, cb, cb)


def _window_sums(pad_ref, w, lead):
    a = pad_ref[...]
    k = 1
    while k < w:
        a = a + _shift_rows(a, -k)
        k *= 2
    return _shift_rows(a, lead) if lead else a


def _pool_counts(L, h):
    t = lax.broadcasted_iota(jnp.int32, (L, 1), 0)
    return (jnp.minimum(t + h, L) - jnp.maximum(t - h, 0)).astype(F32)


def _pooled(u_ref, pad_ref, L, w):
    h = w // 2
    pad_ref[pl.ds(PAD, L), :] = u_ref[...]
    win = _window_sums(pad_ref, w, h)[PAD:PAD + L]
    return win / _pool_counts(L, h) - u_ref[...]


def _zero_pad_edges(pad_ref, L):
    z = jnp.zeros((PAD, LANES), F32)
    pad_ref[pl.ds(0, PAD), :] = z
    pad_ref[pl.ds(PAD + L, PAD), :] = z


def pool_fwd(u, w_pool, pool_scale, *, name):
    L = u.shape[0]

    def body(u_ref, w_ref, ps_ref, p_ref, pad_ref):
        _zero_pad_edges(pad_ref, L)
        for gi, win in enumerate(POOL_WINDOWS):
            @pl.when(pl.program_id(0) == gi)
            def _():
                pooled = _pooled(u_ref, pad_ref, L, win)
                p_ref[...] = (_dot(pooled.astype(BF16), w_ref[...].astype(BF16)) * ps_ref[...]).astype(BF16)

    return pl.pallas_call(
        body, name=name, grid=(len(POOL_WINDOWS),),
        in_specs=[pl.BlockSpec((L, LANES), lambda gi: (0, gi)), pl.BlockSpec((None, LANES, LANES), lambda gi: (gi, 0, 0)),
                  pl.BlockSpec((1, LANES), lambda gi: (0, gi))],
        out_specs=pl.BlockSpec((L, LANES), lambda gi: (0, gi)),
        out_shape=jax.ShapeDtypeStruct((L, 4 * LANES), BF16),
        scratch_shapes=[pltpu.VMEM((L + 2 * PAD, LANES), F32)],
        compiler_params=_cp(("parallel",)),
    )(u, w_pool, pool_scale)


def pool_bwd(u, dpa, w_pool, pool_scale, *, name):
    L = u.shape[0]

    def body(u_ref, dp_ref, w_ref, ps_ref, du_ref, dw_ref, dps_ref, pad_ref):
        _zero_pad_edges(pad_ref, L)
        for gi, win in enumerate(POOL_WINDOWS):
            @pl.when(pl.program_id(0) == gi)
            def _():
                h = win // 2
                wb = w_ref[...].astype(BF16)
                pooled = _pooled(u_ref, pad_ref, L, win).astype(BF16)
                dp = dp_ref[...].astype(F32)
                dps_ref[...] = _colsum(dp * _dot(pooled, wb))
                dy = (dp * ps_ref[...]).astype(BF16)
                dw_ref[...] = _dot_tn(pooled, dy)
                dpooled = _dot_nt(dy, wb)
                pad_ref[pl.ds(PAD, L), :] = dpooled / _pool_counts(L, h)
                du_ref[...] = (_window_sums(pad_ref, win, h - 1)[PAD:PAD + L] - dpooled).astype(BF16)

    return pl.pallas_call(
        body, name=name, grid=(len(POOL_WINDOWS),),
        in_specs=[pl.BlockSpec((L, LANES), lambda gi: (0, gi)), pl.BlockSpec((L, LANES), lambda gi: (0, gi)),
                  pl.BlockSpec((None, LANES, LANES), lambda gi: (gi, 0, 0)), pl.BlockSpec((1, LANES), lambda gi: (0, gi))],
        out_specs=[pl.BlockSpec((L, LANES), lambda gi: (0, gi)), pl.BlockSpec((None, LANES, LANES), lambda gi: (gi, 0, 0)),
                   pl.BlockSpec((1, LANES), lambda gi: (0, gi))],
        out_shape=[jax.ShapeDtypeStruct((L, 4 * LANES), BF16), jax.ShapeDtypeStruct((4, LANES, LANES), F32),
                   jax.ShapeDtypeStruct((1, 4 * LANES), F32)],
        scratch_shapes=[pltpu.VMEM((L + 2 * PAD, LANES), F32)],
        compiler_params=_cp(("parallel",)),
    )(u, dpa, w_pool, pool_scale)


def _attn_probs(q4, band, kvc, sink_ref, kh, mask4):
    scale = HEAD_DIM ** -0.5
    ks = slice(kh * HEAD_DIM, (kh + 1) * HEAD_DIM)
    s_loc = jnp.where(mask4, _dot_nt(q4, band[:, ks]) * scale, NEG_INF)
    s_ctx = _dot_nt(q4, kvc[:, ks]) * scale
    sk = jnp.concatenate([jnp.full((BLK, 1), sink_ref[kh * GQA + hh], F32) for hh in range(GQA)], axis=0)
    m = jnp.maximum(jnp.maximum(jnp.max(s_loc, axis=-1, keepdims=True), jnp.max(s_ctx, axis=-1, keepdims=True)), sk)
    e_loc, e_ctx, e_s = jnp.exp(s_loc - m), jnp.exp(s_ctx - m), jnp.exp(sk - m)
    inv = 1.0 / (jnp.sum(e_loc, axis=-1, keepdims=True) + jnp.sum(e_ctx, axis=-1, keepdims=True) + e_s)
    return e_loc * inv, e_ctx * inv, e_s * inv


def _attn_block(n, L):
    start = pl.multiple_of(jnp.clip((n - 1) * BLK, 0, L - 3 * BLK), BLK)
    qpos = n * BLK + lax.broadcasted_iota(jnp.int32, (BLK, 3 * BLK), 0)
    kpos = start + lax.broadcasted_iota(jnp.int32, (BLK, 3 * BLK), 1)
    mask = jnp.abs(kpos - qpos) <= WINDOW
    return start, jnp.concatenate([mask] * GQA, axis=0)


def _stack_heads(ref, kh):
    return jnp.concatenate([ref[:, (kh * GQA + hh) * HEAD_DIM:(kh * GQA + hh + 1) * HEAD_DIM] for hh in range(GQA)], axis=0)


def attn_fwd(q, kv, kvc, sink, *, name):
    L = q.shape[0]
    C = kvc.shape[0]

    def body(q_ref, kv_ref, kvc_ref, sink_ref, o_ref, o_scr):
        start, mask4 = _attn_block(pl.program_id(0), L)
        band = kv_ref[pl.ds(start, 3 * BLK), :]
        kvc_ = kvc_ref[...]
        for kh in range(N_KV_HEADS):
            q4 = _stack_heads(q_ref, kh)
            p_loc, p_ctx, _ = _attn_probs(q4, band, kvc_, sink_ref, kh, mask4)
            vs = slice(2 * HEAD_DIM + kh * HEAD_DIM, 2 * HEAD_DIM + (kh + 1) * HEAD_DIM)
            o4 = _dot(p_loc.astype(BF16), band[:, vs]) + _dot(p_ctx.astype(BF16), kvc_[:, vs])
            for hh in range(GQA):
                h = kh * GQA + hh
                o_scr[:, h * HEAD_DIM:(h + 1) * HEAD_DIM] = o4[hh * BLK:(hh + 1) * BLK]
        o_ref[...] = o_scr[...].astype(BF16)

    return pl.pallas_call(
        body, name=name, grid=(L // BLK,),
        in_specs=[pl.BlockSpec((BLK, 4 * LANES), lambda n: (n, 0)), _full((L, 2 * LANES)), _full((C, 2 * LANES)),
                  pl.BlockSpec(memory_space=pltpu.SMEM)],
        out_specs=pl.BlockSpec((BLK, 4 * LANES), lambda n: (n, 0)),
        out_shape=jax.ShapeDtypeStruct((L, 4 * LANES), BF16),
        scratch_shapes=[pltpu.VMEM((BLK, 4 * LANES), F32)],
        compiler_params=_cp(("parallel",)),
    )(q, kv, kvc, sink)


def attn_bwd(q, kv, kvc, sink, dpa, cos, sa, sb, *, name):
    L = q.shape[0]
    C = kvc.shape[0]
    nb = L // BLK
    scale = HEAD_DIM ** -0.5

    def body(q_ref, kv_ref, kvc_ref, sink_ref, do_ref, c_ref, sa_ref, sb_ref, cq_ref, saq_ref, sbq_ref,
             dq_ref, dkv_ref, dkvc_ref, dsink_ref, dkv_acc, dkvc_acc, dq_scr, band_scr, ctx_scr):
        n = pl.program_id(0)

        @pl.when(n == 0)
        def _():
            dkv_acc[...] = jnp.zeros_like(dkv_acc)
            dkvc_acc[...] = jnp.zeros_like(dkvc_acc)
            dsink_ref[...] = jnp.zeros_like(dsink_ref)

        start, mask4 = _attn_block(n, L)
        band = kv_ref[pl.ds(start, 3 * BLK), :]
        kvc_ = kvc_ref[...]
        lane = lax.broadcasted_iota(jnp.int32, (1, LANES), 1)
        dsink = jnp.zeros((1, LANES), F32)
        for kh in range(N_KV_HEADS):
            ks = slice(kh * HEAD_DIM, (kh + 1) * HEAD_DIM)
            vs = slice(2 * HEAD_DIM + kh * HEAD_DIM, 2 * HEAD_DIM + (kh + 1) * HEAD_DIM)
            q4 = _stack_heads(q_ref, kh)
            do4 = _stack_heads(do_ref, kh)
            p_loc, p_ctx, p_s = _attn_probs(q4, band, kvc_, sink_ref, kh, mask4)
            dp_loc = _dot_nt(do4, band[:, vs])
            dp_ctx = _dot_nt(do4, kvc_[:, vs])
            delta = jnp.sum(p_loc * dp_loc, axis=-1, keepdims=True) + jnp.sum(p_ctx * dp_ctx, axis=-1, keepdims=True)
            ds_loc = (p_loc * (dp_loc - delta) * scale).astype(BF16)
            ds_ctx = (p_ctx * (dp_ctx - delta) * scale).astype(BF16)
            dsk = p_s * delta
            for hh in range(GQA):
                h = kh * GQA + hh
                dsink = dsink - jnp.where(lane == h, jnp.sum(dsk[hh * BLK:(hh + 1) * BLK], axis=0, keepdims=True), 0.0)
            dq4 = _dot(ds_loc, band[:, ks]) + _dot(ds_ctx, kvc_[:, ks])
            for hh in range(GQA):
                h = kh * GQA + hh
                dq_scr[:, h * HEAD_DIM:(h + 1) * HEAD_DIM] = dq4[hh * BLK:(hh + 1) * BLK]
            band_scr[:, ks] = _dot_tn(ds_loc, q4)
            band_scr[:, vs] = _dot_tn(p_loc.astype(BF16), do4)
            ctx_scr[:, ks] = _dot_tn(ds_ctx, q4)
            ctx_scr[:, vs] = _dot_tn(p_ctx.astype(BF16), do4)
        dsink_ref[...] += dsink
        dkv_acc[pl.ds(start, 3 * BLK), :] += band_scr[...]
        dkvc_acc[...] += ctx_scr[...]
        c, a, b = cq_ref[...], -saq_ref[...], -sbq_ref[...]
        for s in range(4):
            dq_ref[:, s * LANES:(s + 1) * LANES] = _rope(dq_scr[:, s * LANES:(s + 1) * LANES], c, a, b).astype(BF16)

        @pl.when(n == nb - 1)
        def _():
            dkv_ref[:, :LANES] = _rope(dkv_acc[:, :LANES], c_ref[...], -sa_ref[...], -sb_ref[...]).astype(BF16)
            dkv_ref[:, LANES:] = dkv_acc[:, LANES:].astype(BF16)
            dkvc_ref[...] = dkvc_acc[...].astype(BF16)

    blk = lambda w: pl.BlockSpec((BLK, w), lambda n: (n, 0))
    return pl.pallas_call(
        body, name=name, grid=(nb,),
        in_specs=[blk(4 * LANES), _full((L, 2 * LANES)), _full((C, 2 * LANES)), pl.BlockSpec(memory_space=pltpu.SMEM),
                  pl.BlockSpec((BLK, 4 * LANES), lambda n: (n, 1)),
                  _full((L, LANES)), _full((L, LANES)), _full((L, LANES)), blk(LANES), blk(LANES), blk(LANES)],
        out_specs=[blk(4 * LANES), _full((L, 2 * LANES)), _full((C, 2 * LANES)), _full((1, LANES))],
        out_shape=[jax.ShapeDtypeStruct((L, 4 * LANES), BF16), jax.ShapeDtypeStruct((L, 2 * LANES), BF16),
                   jax.ShapeDtypeStruct((C, 2 * LANES), BF16), jax.ShapeDtypeStruct((1, LANES), F32)],
        scratch_shapes=[pltpu.VMEM((L, 2 * LANES), F32), pltpu.VMEM((C, 2 * LANES), F32), pltpu.VMEM((BLK, 4 * LANES), F32),
                        pltpu.VMEM((3 * BLK, 2 * LANES), F32), pltpu.VMEM((C, 2 * LANES), F32)],
        compiler_params=_cp(("arbitrary",)),
    )(q, kv, kvc, sink, dpa, cos, sa, sb, cos, sa, sb)


def _gelu_parts(x):
    th = jnp.tanh(SQRT_2_OVER_PI * (x + GELU_C * x * x * x))
    return 0.5 * x * (1.0 + th), th


def _gelu_grad(x, th):
    return 0.5 * (1.0 + th) + 0.5 * x * (1.0 - th * th) * SQRT_2_OVER_PI * (1.0 + 3.0 * GELU_C * x * x)


def _layernorm(v):
    mu = jnp.mean(v, axis=-1, keepdims=True)
    vc = v - mu
    rstd = lax.rsqrt(jnp.mean(vc * vc, axis=-1, keepdims=True) + EPS)
    return vc * rstd, rstd


def sgu_fwd(z1, ln_g, ln_b, ws, bst, *, name):
    L, W2 = z1.shape
    W = W2 // 2
    ng = W // LANES

    def body(z_ref, g_ref, b_ref, ws_ref, bs_ref, o_ref):
        z, _ = _gelu_parts(z_ref[...].astype(F32))
        xhat, _ = _layernorm(z[:, W:])
        vln = (xhat * g_ref[...] + b_ref[...]).astype(BF16)
        for gi in range(ng):
            cs = slice(gi * LANES, (gi + 1) * LANES)
            s = _dot(ws_ref[gi].astype(BF16), vln[:, cs]) + bs_ref[:, gi:gi + 1]
            o_ref[:, cs] = (z[:, cs] * s).astype(BF16)

    vec = _full((1, W))
    return pl.pallas_call(
        body, name=name, grid=(L // BLK,),
        in_specs=[pl.BlockSpec((BLK, W2), lambda n: (n, 0)), vec, vec, _full((ng, LANES, LANES)), _full((BLK, ng))],
        out_specs=pl.BlockSpec((BLK, W), lambda n: (n, 0)),
        out_shape=jax.ShapeDtypeStruct((L, W), BF16),
        compiler_params=_cp(("parallel",)),
    )(z1, ln_g, ln_b, ws, bst)


def sgu_bwd(z1, dus, ln_g, ln_b, ws, bst, *, name):
    L, W2 = z1.shape
    W = W2 // 2
    ng = W // LANES

    def body(z_ref, d_ref, g_ref, b_ref, ws_ref, bs_ref, dz_ref, dws_ref, dbs_ref, dg_ref, db_ref, dv_scr):
        @pl.when(pl.program_id(0) == 0)
        def _():
            dws_ref[...] = jnp.zeros_like(dws_ref)
            dbs_ref[...] = jnp.zeros_like(dbs_ref)
            dg_ref[...] = jnp.zeros_like(dg_ref)
            db_ref[...] = jnp.zeros_like(db_ref)

        zp = z_ref[...].astype(F32)
        z, th = _gelu_parts(zp)
        xhat, rstd = _layernorm(z[:, W:])
        vln = (xhat * g_ref[...] + b_ref[...]).astype(BF16)
        d = d_ref[...].astype(F32)
        lane = lax.broadcasted_iota(jnp.int32, (1, LANES), 1)
        dbs = jnp.zeros((BLK, LANES), F32)
        dgel = _gelu_grad(zp, th)
        for gi in range(ng):
            cs = slice(gi * LANES, (gi + 1) * LANES)
            wb = ws_ref[gi].astype(BF16)
            s = _dot(wb, vln[:, cs]) + bs_ref[:, gi:gi + 1]
            dz_ref[:, cs] = (d[:, cs] * s * dgel[:, cs]).astype(BF16)
            ds = d[:, cs] * z[:, cs]
            dbs = dbs + jnp.where(lane == gi, jnp.sum(ds, axis=-1, keepdims=True), 0.0)
            dsb = ds.astype(BF16)
            dws_ref[gi] += _dot_nt(dsb, vln[:, cs])
            dv_scr[:, cs] = _dot_tn(wb, dsb)
        dbs_ref[...] += dbs
        dvln = dv_scr[...]
        dg_ref[...] += _colsum(dvln * xhat)
        db_ref[...] += _colsum(dvln)
        dxh = dvln * g_ref[...]
        dv = rstd * (dxh - jnp.mean(dxh, axis=-1, keepdims=True) - xhat * jnp.mean(dxh * xhat, axis=-1, keepdims=True))
        dz_ref[:, W:] = (dv * dgel[:, W:]).astype(BF16)

    vec = _full((1, W))
    return pl.pallas_call(
        body, name=name, grid=(L // BLK,),
        in_specs=[pl.BlockSpec((BLK, W2), lambda n: (n, 0)), pl.BlockSpec((BLK, W), lambda n: (n, 0)), vec, vec,
                  _full((ng, LANES, LANES)), _full((BLK, ng))],
        out_specs=[pl.BlockSpec((BLK, W2), lambda n: (n, 0)), _full((ng, LANES, LANES)), _full((BLK, LANES)), vec, vec],
        out_shape=[jax.ShapeDtypeStruct((L, W2), BF16), jax.ShapeDtypeStruct((ng, LANES, LANES), F32),
                   jax.ShapeDtypeStruct((BLK, LANES), F32), jax.ShapeDtypeStruct((1, W), F32), jax.ShapeDtypeStruct((1, W), F32)],
        scratch_shapes=[pltpu.VMEM((BLK, W), F32)],
        compiler_params=_cp(("arbitrary",)),
    )(z1, dus, ln_g, ln_b, ws, bst)


def loss_grad(xo, target, *, tm, name):
    T, D = xo.shape

    def body(x_ref, t_ref, l_ref, d_ref):
        @pl.when(pl.program_id(0) == 0)
        def _():
            l_ref[...] = jnp.zeros_like(l_ref)

        e = x_ref[...] - t_ref[...]
        l_ref[...] += 0.5 * jnp.sum(jnp.mean(e * e, axis=-1, keepdims=True), axis=0, keepdims=True)
        d_ref[...] = e * (1.0 / D)

    row = pl.BlockSpec((tm, D), lambda i: (i, 0))
    return pl.pallas_call(
        body, name=name, grid=(T // tm,), in_specs=[row, row], out_specs=[_full((1, 1)), row],
        out_shape=[jax.ShapeDtypeStruct((1, 1), F32), jax.ShapeDtypeStruct((T, D), F32)],
        compiler_params=_cp(("arbitrary",)),
    )(xo, target)


def adamw(w, m, v, gparts, *, tr, name):
    R, Wd = w.shape
    S = gparts.shape[0]

    def body(w_ref, m_ref, v_ref, gp_ref, g_ref, d_ref, nm_ref, nv_ref):
        g = gp_ref[0].astype(F32)
        for s in range(1, S):
            g = g + gp_ref[s].astype(F32)
        m_ = ADAM_B1 * m_ref[...] + (1.0 - ADAM_B1) * g
        v_ = ADAM_B2 * v_ref[...] + (1.0 - ADAM_B2) * (g * g)
        g_ref[...] = g
        nm_ref[...] = m_
        nv_ref[...] = v_
        d_ref[...] = -ADAM_LR * ((m_ / BC1) / (jnp.sqrt(v_ / BC2) + ADAM_EPS) + ADAM_WD * w_ref[...])

    row = pl.BlockSpec((tr, Wd), lambda i: (i, 0))
    return pl.pallas_call(
        body, name=name, grid=(R // tr,),
        in_specs=[row, row, row, pl.BlockSpec((S, tr, Wd), lambda i: (0, i, 0))],
        out_specs=[row] * 4, out_shape=[jax.ShapeDtypeStruct((R, Wd), F32)] * 4,
        compiler_params=_cp(("parallel",)),
    )(w, m, v, gparts)


def ada_fwd_mm(cs, w_ada, b_loc, *, name):
    R, D = cs.shape
    nl, _, n = w_ada.shape

    def body(c_ref, w_ref, b_ref, s_ref, m_ref):
        c = c_ref[...]
        s = c * jax.nn.sigmoid(c)
        s_ref[...] = s
        for i in range(nl):
            m_ref[i] = _dot(s.astype(BF16), w_ref[i].astype(BF16)) + b_ref[i:i + 1, :]

    return pl.pallas_call(
        body, name=name, in_specs=[_full((R, D)), _full((nl, D, n)), _full((nl, n))],
        out_specs=[_full((R, D)), _full((nl, R, n))], grid=(1,),
        out_shape=[jax.ShapeDtypeStruct((R, D), F32), jax.ShapeDtypeStruct((nl, R, n), F32)],
        compiler_params=_cp(("arbitrary",)),
    )(cs, w_ada, b_loc)


def ada_bwd_mm(s, c_ctx, dall, w_ada, *, name):
    R, D = s.shape
    nl, _, n = w_ada.shape

    def body(s_ref, cc_ref, d_ref, w_ref, gw_ref, dcc_ref):
        sb = s_ref[...].astype(BF16)
        row = lax.broadcasted_iota(jnp.int32, (R, 1), 0)
        dctx = d_ref[0, 1:2, :]
        for dv in range(1, N_DEV):
            dctx = dctx + d_ref[dv, 1:2, :]
        for i in range(nl):
            dm = jnp.zeros((R, n), F32)
            for dv in range(N_DEV):
                dm = dm + jnp.where(row == dv, d_ref[dv, 2 * i:2 * i + 1, :], 0.0)
            if i == 0:
                dm = dm + jnp.where(row == N_DEV, dctx, 0.0)
            gw_ref[i] = _dot_tn(sb, dm.astype(BF16))
        cc = cc_ref[...]
        sg = jax.nn.sigmoid(cc)
        ds = _dot_nt(jnp.broadcast_to(dctx, (8, n)).astype(BF16), w_ref[0].astype(BF16))
        dcc_ref[...] = ds * (sg * (1.0 + cc * (1.0 - sg)))

    return pl.pallas_call(
        body, name=name, grid=(1,),
        in_specs=[_full((R, D)), _full((1, D)), _full((N_DEV, 3, n)), _full((nl, D, n))],
        out_specs=[_full((nl, D, n)), _full((8, D))],
        out_shape=[jax.ShapeDtypeStruct((nl, D, n), F32), jax.ShapeDtypeStruct((8, D), F32)],
        compiler_params=_cp(("arbitrary",)),
    )(s, c_ctx, dall, w_ada)


def _place():
    x, y, c = lax.axis_index("x"), lax.axis_index("y"), lax.axis_index("c")
    return x, y, c


def _lin(p):
    return 4 * p[0] + 2 * p[1] + p[2]


def all_gather_small(xb, *, reduce=False, name):
    R, W = xb.shape

    def body(x_ref, *rest):
        out_ref = rest[0]
        send_sems, recv_sems, local_sem = rest[-3:]
        x, y, c = _place()
        me, sibling = (x, y, c), (x, y, 1 - c)
        chips = [(1 - x, y), (x, 1 - y), (1 - x, 1 - y)]

        def copy(k, block, to, src=None):
            dst = out_ref.at[_lin(block)]
            return pltpu.make_async_remote_copy(
                src_ref=dst if src is None else src, dst_ref=dst, send_sem=send_sems.at[k], recv_sem=recv_sems.at[k],
                device_id=to, device_id_type=MESH)

        mine = pltpu.make_async_copy(x_ref, out_ref.at[_lin(me)], local_sem)
        mine.start()
        first = [copy(0, me, sibling, src=x_ref)]
        first += [copy(1 + j, me, (*chip, c), src=x_ref) for j, chip in enumerate(chips)]
        for cp in first:
            cp.start()
        passed = [copy(4 + j, (*chip, c), sibling) for j, chip in enumerate(chips)]
        for j, chip in enumerate(chips):
            copy(1 + j, (*chip, c), me).wait_recv()
            passed[j].start()
        copy(0, sibling, me).wait_recv()
        for j, chip in enumerate(chips):
            copy(4 + j, (*chip, 1 - c), me).wait_recv()
        for cp in first + passed:
            cp.wait_send()
        mine.wait()
        if reduce:
            acc = out_ref[0]
            for dv in range(1, N_DEV):
                acc = acc + out_ref[dv]
            rest[1][...] = acc

    vm = pl.BlockSpec(memory_space=pltpu.VMEM)
    out_shape = [jax.ShapeDtypeStruct((N_DEV, R, W), xb.dtype)]
    if reduce:
        out_shape.append(jax.ShapeDtypeStruct((R, W), xb.dtype))
    res = pl.pallas_call(
        body, name=name, in_specs=[vm], out_specs=[vm] * len(out_shape), out_shape=out_shape,
        scratch_shapes=[pltpu.SemaphoreType.DMA((7,)), pltpu.SemaphoreType.DMA((7,)), pltpu.SemaphoreType.DMA],
        compiler_params=pltpu.CompilerParams(vmem_limit_bytes=VMEM_LIMIT),
    )(xb)
    return res if reduce else res[0]


HBM_SPEC = pl.BlockSpec(memory_space=pltpu.HBM)
SEM_SPEC = pl.BlockSpec(memory_space=pltpu.SEMAPHORE)
ORDERED_EFFECT = pltpu.SideEffectType.DATAFLOW_SIDE_EFFECTING


def _exchange_copies(srcs, lands, sems, scatter):
    x, y, c = _place()
    me = _lin((x, y, c))
    for j in range(len(srcs)):
        r = lands[j].shape[0] // N_DEV
        block = lambda d, j=j, r=r: pl.ds(pl.multiple_of(d * r, 16), r)
        for k in range(1, N_DEV):
            peer = (x ^ (k >> 2), y ^ ((k >> 1) & 1), c ^ (k & 1))
            src = srcs[j].at[block(_lin(peer)), :] if scatter else srcs[j]
            mk = lambda dst, j=j, k=k, peer=peer, src=src: pltpu.make_async_remote_copy(
                src_ref=src, dst_ref=dst, send_sem=sems[2 * j].at[k - 1], recv_sem=sems[2 * j + 1].at[k - 1],
                device_id=peer, device_id_type=MESH)
            yield mk(lands[j].at[block(me), :]), mk(lands[j].at[block(_lin(peer)), :])


def exchange_start(srcs, lands, *, scatter, name):
    nw = len(srcs)

    def body(*refs):
        ins, lz, local_sems = refs[:nw], refs[nw:2 * nw], refs[-1]
        me = _lin(_place())
        own = []
        for j in range(nw):
            r = lz[j].shape[0] // N_DEV
            rows = pl.ds(pl.multiple_of(me * r, 16), r)
            own.append(pltpu.make_async_copy(ins[j].at[rows, :] if scatter else ins[j], lz[j].at[rows, :], local_sems.at[j]))
            own[-1].start()
        for cp in own:
            cp.wait()
        for start, _ in _exchange_copies(ins, lz, refs[2 * nw:4 * nw], scatter):
            start.start()
        refs[-2][...] = jnp.zeros_like(refs[-2])

    thru = [pltpu.HBM(a.shape, a.dtype) for a in (*srcs, *lands)]
    res = pl.pallas_call(
        body, name=name, in_specs=[HBM_SPEC] * (2 * nw),
        out_specs=[SEM_SPEC] * (2 * nw) + [HBM_SPEC] * (2 * nw) + [pl.BlockSpec(memory_space=pltpu.VMEM)],
        out_shape=[pltpu.SemaphoreType.DMA((N_DEV - 1,))] * (2 * nw) + thru + [jax.ShapeDtypeStruct((8, LANES), F32)],
        scratch_shapes=[pltpu.SemaphoreType.DMA((nw,))],
        input_output_aliases={i: 2 * nw + i for i in range(2 * nw)},
        compiler_params=pltpu.CompilerParams(has_side_effects=ORDERED_EFFECT),
    )(*[pltpu.with_memory_space_constraint(a, pltpu.HBM) for a in (*srcs, *lands)])
    return res[:2 * nw], res[2 * nw:3 * nw], res[3 * nw:4 * nw], res[-1]


def exchange_wait(srcs, lands, sems, after, *, scatter, name):
    nw = len(srcs)

    def body(*refs):
        for _, arrive in _exchange_copies(refs[:nw], refs[nw:2 * nw], refs[2 * nw:4 * nw], scatter):
            arrive.wait_send()
            arrive.wait_recv()

    res = pl.pallas_call(
        body, name=name, in_specs=[HBM_SPEC] * (2 * nw) + [SEM_SPEC] * (2 * nw) + [pl.BlockSpec(memory_space=pl.ANY)],
        out_specs=[HBM_SPEC] * (2 * nw), out_shape=[pltpu.HBM(a.shape, a.dtype) for a in (*srcs, *lands)],
        input_output_aliases={i: i for i in range(2 * nw)},
        compiler_params=pltpu.CompilerParams(has_side_effects=ORDERED_EFFECT),
    )(*srcs, *lands, *sems, after)
    return res[nw:]


def _rope_tables(L):
    t = jnp.arange(L)
    inv = ROPE_BASE ** (-jnp.arange(ROPE_FREQS, dtype=F32) / ROPE_FREQS)
    ar = (t // GRID_W).astype(F32)[:, None] * inv
    ac = (t % GRID_W).astype(F32)[:, None] * inv
    z = jnp.zeros_like(ar)
    cos = jnp.concatenate([jnp.cos(ar), jnp.cos(ar), jnp.cos(ac), jnp.cos(ac)], axis=1)
    sa = jnp.concatenate([-jnp.sin(ar), z, -jnp.sin(ac), z], axis=1)
    sb = jnp.concatenate([z, jnp.sin(ar), z, jnp.sin(ac)], axis=1)
    return tuple(jnp.tile(a, (1, LANES // HEAD_DIM)) for a in (cos, sa, sb))


def _rows128(a):
    f = a.reshape(-1)
    n = -(-f.shape[0] // (8 * LANES)) * 8 * LANES
    return jnp.pad(f, (0, n - f.shape[0])).reshape(-1, LANES)


def _landing(rows, like):
    return lax.empty((N_DEV * rows, like.shape[1]), like.dtype)


def kernel(x, c, ctx, c_ctx, w_ada, b_ada, g_mix_pre, g_mix_post, g_ffn_pre, g_ffn_post, w_in_even, w_pool, pool_scale, attn_sink, w_out_even, w_in_odd, sgu_ln_g, sgu_ln_b, sgu_w, sgu_b, w_out_odd, w_ffn_up, ffn_conv_w, ffn_conv_b, w_ffn_down, loss_target, m_c_ctx, m_w_ada, m_b_ada, m_g_mix_pre, m_g_mix_post, m_g_ffn_pre, m_g_ffn_post, m_w_in_even, m_w_pool, m_pool_scale, m_attn_sink, m_w_out_even, m_w_in_odd, m_sgu_ln_g, m_sgu_ln_b, m_sgu_w, m_sgu_b, m_w_out_odd, m_w_ffn_up, m_ffn_conv_w, m_ffn_conv_b, m_w_ffn_down, v_c_ctx, v_w_ada, v_b_ada, v_g_mix_pre, v_g_mix_post, v_g_ffn_pre, v_g_ffn_post, v_w_in_even, v_w_pool, v_pool_scale, v_attn_sink, v_w_out_even, v_w_in_odd, v_sgu_ln_g, v_sgu_ln_b, v_sgu_w, v_sgu_b, v_w_out_odd, v_w_ffn_up, v_ffn_conv_w, v_ffn_conv_b, v_w_ffn_down):
    P = dict(c_ctx=c_ctx, w_ada=w_ada, b_ada=b_ada, g_mix_pre=g_mix_pre, g_mix_post=g_mix_post, g_ffn_pre=g_ffn_pre,
             g_ffn_post=g_ffn_post, w_in_even=w_in_even, w_pool=w_pool, pool_scale=pool_scale, attn_sink=attn_sink,
             w_out_even=w_out_even, w_in_odd=w_in_odd, sgu_ln_g=sgu_ln_g, sgu_ln_b=sgu_ln_b, sgu_w=sgu_w, sgu_b=sgu_b,
             w_out_odd=w_out_odd, w_ffn_up=w_ffn_up, ffn_conv_w=ffn_conv_w, ffn_conv_b=ffn_conv_b, w_ffn_down=w_ffn_down)
    M = dict(c_ctx=m_c_ctx, w_ada=m_w_ada, b_ada=m_b_ada, g_mix_pre=m_g_mix_pre, g_mix_post=m_g_mix_post, g_ffn_pre=m_g_ffn_pre,
             g_ffn_post=m_g_ffn_post, w_in_even=m_w_in_even, w_pool=m_w_pool, pool_scale=m_pool_scale, attn_sink=m_attn_sink,
             w_out_even=m_w_out_even, w_in_odd=m_w_in_odd, sgu_ln_g=m_sgu_ln_g, sgu_ln_b=m_sgu_ln_b, sgu_w=m_sgu_w, sgu_b=m_sgu_b,
             w_out_odd=m_w_out_odd, w_ffn_up=m_w_ffn_up, ffn_conv_w=m_ffn_conv_w, ffn_conv_b=m_ffn_conv_b, w_ffn_down=m_w_ffn_down)
    V = dict(c_ctx=v_c_ctx, w_ada=v_w_ada, b_ada=v_b_ada, g_mix_pre=v_g_mix_pre, g_mix_post=v_g_mix_post, g_ffn_pre=v_g_ffn_pre,
             g_ffn_post=v_g_ffn_post, w_in_even=v_w_in_even, w_pool=v_w_pool, pool_scale=v_pool_scale, attn_sink=v_attn_sink,
             w_out_even=v_w_out_even, w_in_odd=v_w_in_odd, sgu_ln_g=v_sgu_ln_g, sgu_ln_b=v_sgu_ln_b, sgu_w=v_sgu_w, sgu_b=v_sgu_b,
             w_out_odd=v_w_out_odd, w_ffn_up=v_w_ffn_up, ffn_conv_w=v_ffn_conv_w, ffn_conv_b=v_ffn_conv_b, w_ffn_down=v_w_ffn_down)

    x = x[0]
    ctx = ctx[0]
    target = loss_target[0]
    L, D = x.shape
    C = ctx.shape[0]
    tm = min(512, L)
    tm_up = min(1024, L)
    conv_rows = min(256, L)
    me = 4 * lax.axis_index("x") + 2 * lax.axis_index("y") + lax.axis_index("c")
    n_ada = w_ada.shape[2]
    F = w_ffn_down.shape[1] * N_DEV
    half_f = F // 2

    n_cw = ffn_conv_w.shape[2]
    small = jnp.concatenate([_rows128(c), _rows128(sgu_ln_g), _rows128(sgu_ln_b), _rows128(ffn_conv_w)], axis=0)
    small_all = all_gather_small(small, name="gather_small_inputs")
    c_all = small_all[:, :8].reshape(N_DEV, D)
    ln_g = small_all[:, 8].reshape(1, D)
    ln_b = small_all[:, 16].reshape(1, D)
    conv_w = small_all[:, 24:].reshape(N_DEV, -1)[:, :2 * 3 * n_cw].reshape(N_DEV, 2, 3, n_cw)
    conv_w = conv_w.transpose(1, 2, 0, 3).reshape(2, 3, 2 * F)

    cs = jnp.concatenate([c_all, c_ctx[None, :], jnp.zeros((7, D), F32)], axis=0)
    b_loc = lax.dynamic_slice(b_ada, (0, me * n_ada), (2, n_ada))
    silu_c, mods_loc = ada_fwd_mm(cs, w_ada, b_loc, name="ada_fwd")
    mods_all = all_gather_small(mods_loc.reshape(-1, LANES), name="gather_mods")

    shards = [s.astype(BF16) for s in (w_in_even[0].T, w_out_even[0], w_ffn_up[0].T, w_ffn_down[0],
                                       w_in_odd[0].T, w_out_odd[0], w_ffn_up[1].T, w_ffn_down[1])]
    shards, mods_all = lax.optimization_barrier((shards, mods_all))
    w_sems, w_srcs, w_lands, _ = exchange_start(shards, [_landing(s.shape[0], s) for s in shards], scatter=False, name="gather_start")

    def weight(j, after):
        return exchange_wait([w_srcs[j]], [w_lands[j]], w_sems[2 * j:2 * j + 2], after, scatter=False, name=f"gather_wait_{j}")[0]

    mods_all = mods_all.reshape(N_DEV, 2, 16, n_ada).transpose(1, 2, 0, 3).reshape(2, 16, 6 * D)
    mod = lambda i, row: [m_[None, :] for m_ in jnp.split(lax.dynamic_index_in_dim(mods_all[i], row, 0, False), 6)]
    sh_m, sc_m, gt_m, sh_f, sc_f, gt_f = zip(mod(0, me), mod(1, me))
    csh_m, csc_m = mod(0, N_DEV)[:2]

    row = lambda a, i: a[i][None, :]

    cos, sa, sb = _rope_tables(L)
    sink = attn_sink[0]
    bst = sgu_b[0].T
    wup, wdn = [None, None], [None, None]

    def ffn_fwd(i, xin):
        wup[i] = weight(2 + 4 * i, xin)
        h, hu = pre_mm(xin, row(g_ffn_pre, i), sh_f[i], sc_f[i], wup[i], tm=tm_up, tn=half_f, name=f"ffn_up_{i}")
        a = conv_fwd(hu, conv_w[i], ffn_conv_b[i][None, :], rows=conv_rows, wblk=2 * LANES, name=f"ffn_conv_{i}")
        wdn[i] = weight(3 + 4 * i, a)
        f, xo = mm_post(a, wdn[i], xin, row(g_ffn_post, i), gt_f[i], tm=tm, name=f"ffn_down_{i}")
        return h, hu, a, f, xo

    win_e = weight(0, sh_m[0])
    h0, u, q, kv = inproj_even(x, row(g_mix_pre, 0), sh_m[0], sc_m[0], win_e, cos, sa, sb, tm=tm, name="in_even")
    hc, kvc = pre_mm(ctx, row(g_mix_pre, 0), csh_m, csc_m, win_e, tm=C, tn=2 * LANES, w_row_off=8 * LANES, name="in_even_ctx")
    pa = jnp.concatenate([pool_fwd(u, w_pool[0], pool_scale, name="pool_fwd"),
                          attn_fwd(q, kv, kvc, sink, name="attn_fwd")], axis=1)
    wout_e = weight(1, pa)
    y0, x1 = mm_post(pa, wout_e, x, row(g_mix_post, 0), gt_m[0], tm=tm, name="out_even")
    h1, hu0, a0, f0, x2 = ffn_fwd(0, x1)
    win_o = weight(4, x2)
    h2, z1 = pre_mm(x2, row(g_mix_pre, 1), sh_m[1], sc_m[1], win_o, tm=tm_up, tn=D, name="in_odd")
    us = sgu_fwd(z1, ln_g, ln_b, sgu_w[0], bst, name="sgu_fwd")
    wout_o = weight(5, us)
    y1, x3 = mm_post(us, wout_o, x2, row(g_mix_post, 1), gt_m[1], tm=tm, name="out_odd")
    h3, hu1, a1, f1, x4 = ffn_fwd(1, x3)
    loss_part, dx4 = loss_grad(x4, target, tm=tm, name="loss")
    loss = lax.psum(loss_part[0, 0], ("x", "y", "c"))

    g_srcs, g_lands, g_sems = [], [], []

    def scatter(grads, nm):
        sems, srcs, lands, tok = exchange_start(grads, [_landing(g.shape[0] // N_DEV, g) for g in grads], scatter=True, name=nm)
        g_srcs.extend(srcs)
        g_lands.extend(lands)
        g_sems.extend(sems)
        return tok[0:1, 0:1]

    def ffn_bwd(i, dxo, xin, h, hu, a, f, g_post):
        dyf, da, dg_post, dgt = post_bwd_mm(dxo, f, g_post, gt_f[i], wdn[i], tm=tm, name=f"ffn_down_bwd_{i}")
        dhg, dhu, dcwg, dcwu, dcbg, dcbu = conv_bwd(da, hu, conv_w[i], ffn_conv_b[i][None, :], rows=conv_rows, wblk=2 * LANES,
                                                    name=f"ffn_conv_bwd_{i}")
        dxin, dg_pre, dsh, dsc = mm_pre_bwd([dhg, dhu], wup[i], xin, dxo, row(g_ffn_pre, i), sc_f[i], tm=tm, tk=half_f,
                                            name=f"ffn_up_bwd_{i}")
        g_dn = wgrad([a], dyf, tr=2 * LANES, name=f"wgrad_down_{i}")
        g_up = wgrad([dhg, dhu], h, tr=2 * LANES, name=f"wgrad_up_{i}")
        tok = scatter([g_dn, g_up], f"scatter_start_ffn_{i}")
        return dxin, tok, dict(g_ffn_post=dg_post, g_ffn_pre=dg_pre, gt_f=dgt, sh_f=dsh, sc_f=dsc,
                               ffn_conv_w=jnp.concatenate([dcwg, dcwu], axis=1), ffn_conv_b=jnp.concatenate([dcbg, dcbu], axis=1)[0])

    dx3, tok, sf1 = ffn_bwd(1, dx4, x3, h3, hu1, a1, f1, row(g_ffn_post, 1))
    dy1, dus, dg_mpost1, dgt_m1 = post_bwd_mm(dx3, y1, row(g_mix_post, 1) + tok, gt_m[1], wout_o, tm=tm, name="out_odd_bwd")
    dz1, dws, dbs, dlng, dlnb = sgu_bwd(z1, dus, ln_g, ln_b, sgu_w[0], bst, name="sgu_bwd")
    dx2, dg_mpre1, dsh_m1, dsc_m1 = mm_pre_bwd([dz1], win_o, x2, dx3, row(g_mix_pre, 1), sc_m[1], tm=tm, tk=D, name="in_odd_bwd")
    tok = scatter([wgrad([us], dy1, tr=2 * LANES, name="wgrad_out_odd"), wgrad([dz1], h2, tr=2 * LANES, name="wgrad_in_odd")],
                  "scatter_start_mix_1")

    dx1, tok, sf0 = ffn_bwd(0, dx2, x1, h1, hu0, a0, f0, row(g_ffn_post, 0) + tok)
    dy0, dpa, dg_mpost0, dgt_m0 = post_bwd_mm(dx1, y0, row(g_mix_post, 0) + tok, gt_m[0], wout_e, tm=tm, name="out_even_bwd")
    du, dwp, dps = pool_bwd(u, dpa, w_pool[0], pool_scale, name="pool_bwd")
    dq, dkv, dkvc, dsink = attn_bwd(q, kv, kvc, sink, dpa, cos, sa, sb, name="attn_bwd")
    dz0 = jnp.concatenate([du, dq, dkv], axis=1)
    dzc = jnp.concatenate([jnp.zeros((C, 8 * LANES), BF16), dkvc], axis=1)
    tok = scatter([wgrad([pa], dy0, tr=2 * LANES, name="wgrad_out_even"),
                   wgrad([dz0], h0, tr=2 * LANES, extra=(dzc, hc), name="wgrad_in_even")], "scatter_start_mix_0")
    grad_x, dg_mpre0, dsh_m0, dsc_m0 = mm_pre_bwd([dz0], win_e, x, dx1, row(g_mix_pre, 0) + tok, sc_m[0], tm=tm, tk=dz0.shape[1],
                                                  name="in_even_bwd")
    _, dg_mpre0c, dcsh, dcsc = mm_pre_bwd([dkvc], win_e, ctx, None, row(g_mix_pre, 0), csc_m, tm=C, tk=2 * LANES,
                                          w_row_off=8 * LANES, name="in_even_ctx_bwd")

    slots = exchange_wait(g_srcs, g_lands, g_sems, dcsh, scatter=True, name="scatter_wait")
    out = {}

    def update(name, idx, land, transposed):
        w_, m_, v_ = (a[idx].T if transposed else a[idx] for a in (P[name], M[name], V[name]))
        r = w_.shape[0]
        tr = r // 4 if r % 64 == 0 and r > 256 else r
        res = adamw(w_, m_, v_, land.reshape(N_DEV, r, land.shape[1]), tr=tr, name=f"adamw_{name}_{idx}")
        for kind, val in zip(("grad", "delta", "new_m", "new_v"), res):
            out.setdefault((kind, name), []).append(val.T if transposed else val)

    update("w_in_even", 0, slots[7], True)
    update("w_out_even", 0, slots[6], False)
    update("w_in_odd", 0, slots[3], True)
    update("w_out_odd", 0, slots[2], False)
    update("w_ffn_up", 0, slots[5], True)
    update("w_ffn_down", 0, slots[4], False)
    update("w_ffn_up", 1, slots[1], True)
    update("w_ffn_down", 1, slots[0], False)

    zero = jnp.zeros((1, D), F32)
    dmod0 = jnp.concatenate([dsh_m0, dsc_m0, dgt_m0, sf0["sh_f"], sf0["sc_f"], sf0["gt_f"]], axis=1)
    dmodc = jnp.concatenate([dcsh, dcsc, zero, zero, zero, zero], axis=1)
    dmod1 = jnp.concatenate([dsh_m1, dsc_m1, dgt_m1, sf1["sh_f"], sf1["sc_f"], sf1["gt_f"]], axis=1)
    dmods = jnp.concatenate([dmod0, dmodc, dmod1], axis=0)
    dmods_all = all_gather_small(dmods.reshape(-1, LANES), name="gather_dmods").reshape(N_DEV, 3, N_DEV, n_ada)
    dall = lax.dynamic_index_in_dim(dmods_all, me, 2, False)
    g_w_ada, dcc = ada_bwd_mm(silu_c, c_ctx[None, :], dall, w_ada, name="ada_bwd")
    nl = w_ada.shape[0]
    res = adamw(w_ada.reshape(nl * D, n_ada), m_w_ada.reshape(nl * D, n_ada), v_w_ada.reshape(nl * D, n_ada),
                g_w_ada.reshape(1, nl * D, n_ada), tr=nl * D // 8, name="adamw_w_ada")
    for kind, val in zip(("grad", "delta", "new_m", "new_v"), res):
        out[(kind, "w_ada")] = val.reshape(nl, D, n_ada)

    rep = dict(
        c_ctx=dcc[0],
        b_ada=jnp.stack([dmod0[0] + dmodc[0], dmod1[0]]),
        g_mix_pre=jnp.concatenate([dg_mpre0 + dg_mpre0c, dg_mpre1]),
        g_mix_post=jnp.concatenate([dg_mpost0, dg_mpost1]),
        g_ffn_pre=jnp.concatenate([sf0["g_ffn_pre"], sf1["g_ffn_pre"]]),
        g_ffn_post=jnp.concatenate([sf0["g_ffn_post"], sf1["g_ffn_post"]]),
        w_pool=dwp[None], pool_scale=dps, attn_sink=dsink[:, :N_Q_HEADS],
        sgu_w=dws[None], sgu_b=dbs[:, :sgu_b.shape[1]].T[None],
        ffn_conv_b=jnp.stack([sf0["ffn_conv_b"], sf1["ffn_conv_b"]]),
    )
    rep_names = list(rep)
    conv_g = jnp.stack([sf0["ffn_conv_w"], sf1["ffn_conv_w"]]).reshape(2, 3, N_DEV, n_cw).transpose(2, 0, 1, 3)
    shard_full = dict(sgu_ln_g=dlng.reshape(N_DEV, LANES), sgu_ln_b=dlnb.reshape(N_DEV, LANES),
                      ffn_conv_w=jnp.concatenate([_rows128(conv_g[d]) for d in range(N_DEV)], axis=0))
    pieces = [_rows128(rep[k]) for k in rep_names] + [shard_full[k] for k in shard_full]
    sizes = [p.shape[0] for p in pieces]
    _, gsum = all_gather_small(jnp.concatenate(pieces, axis=0), reduce=True, name="allreduce_small_grads")
    offs = [sum(sizes[:i]) for i in range(len(sizes))]
    n_rep = len(rep_names)
    cw_rows = sizes[-1] // N_DEV
    g_own = [gsum[offs[i]:offs[i] + sizes[i]] for i in range(n_rep)]
    g_own.append(_rows128(lax.dynamic_slice_in_dim(gsum, offs[n_rep] + me, 1, 0)))
    g_own.append(_rows128(lax.dynamic_slice_in_dim(gsum, offs[n_rep + 1] + me, 1, 0)))
    g_own.append(lax.dynamic_slice_in_dim(gsum, offs[n_rep + 2] + me * cw_rows, cw_rows, 0))
    small_names = rep_names + list(shard_full)
    packs = [jnp.concatenate([_rows128(src[k]) for k in small_names], axis=0) for src in (P, M, V)]
    n_pack = packs[0].shape[0]
    gp = jnp.concatenate(g_own, axis=0)[None]
    res = adamw(*packs, gp, tr=n_pack, name="adamw_small")
    o = 0
    for k in small_names:
        n = _rows128(P[k]).shape[0]
        for kind, val in zip(("grad", "delta", "new_m", "new_v"), res):
            out[(kind, k)] = val[o:o + n].reshape(-1)[:P[k].size].reshape(P[k].shape)
        o += n
    assert o == n_pack

    names = list(P)
    final = [loss, grad_x[None]]
    for kind in ("grad", "delta", "new_m", "new_v"):
        for k in names:
            val = out[(kind, k)]
            final.append(jnp.stack(val) if isinstance(val, list) else val)
    return tuple(final)
```

```python
import functools
import math

import jax
import jax.numpy as jnp
from jax import lax
from jax.experimental import pallas as pl
from jax.experimental.pallas import tpu as pltpu

F32 = jnp.float32
BF16 = jnp.bfloat16
MESH = pl.DeviceIdType.MESH
N_DEV = 8
LANES = 128
VMEM_LIMIT = 48 * 1024 * 1024
EPS = 1e-6
NEG_INF = -1e30
GRID_W = 64
WINDOW = 128
BLK = 128
HEAD_DIM = 64
N_Q_HEADS = 8
N_KV_HEADS = 2
GQA = N_Q_HEADS // N_KV_HEADS
POOL_WINDOWS = (2, 4, 8, 16)
ROPE_BASE = 10000.0
ROPE_FREQS = HEAD_DIM // 4
PAD = 16
ADAM_LR, ADAM_B1, ADAM_B2, ADAM_EPS, ADAM_WD, ADAM_STEP = 0.001, 0.9, 0.999, 1e-08, 0.01, 10
BC1 = 1.0 - ADAM_B1 ** ADAM_STEP
BC2 = 1.0 - ADAM_B2 ** ADAM_STEP
SQRT_2_OVER_PI = math.sqrt(2.0 / math.pi)
GELU_C = 0.044715


def _cp(sem=None):
    return pltpu.CompilerParams(dimension_semantics=sem, vmem_limit_bytes=VMEM_LIMIT)


def _dot(a, b):
    return jnp.dot(a, b, preferred_element_type=F32)


def _dot_nt(a, b):
    return lax.dot_general(a, b, (((1,), (1,)), ((), ())), preferred_element_type=F32)


def _dot_tn(a, b):
    return lax.dot_general(a, b, (((0,), (0,)), ((), ())), preferred_element_type=F32)


def _rms(x):
    r = lax.rsqrt(jnp.mean(x * x, axis=-1, keepdims=True) + EPS)
    return x * r, r


def _rms_bwd(dn, n, r):
    return r * (dn - n * jnp.mean(dn * n, axis=-1, keepdims=True))


def _colsum(a):
    return jnp.sum(a, axis=0, keepdims=True)


def _rope(x, c, sa, sb):
    return x * c + pltpu.roll(x, LANES - ROPE_FREQS, 1) * sa + pltpu.roll(x, ROPE_FREQS, 1) * sb


def _full(shape):
    return pl.BlockSpec(shape, lambda *_: (0,) * len(shape))


def pre_mm(x, g, sh, sc, wt, *, tm, tn, w_row_off=0, name):
    T, D = x.shape
    n_rows = wt.shape[0] - w_row_off
    off = w_row_off // tn

    def body(x_ref, g_ref, sh_ref, sc_ref, w_ref, h_ref, z_ref):
        @pl.when(pl.program_id(1) == 0)
        def _():
            n, _ = _rms(x_ref[...])
            h_ref[...] = (n * g_ref[...] * (1.0 + sc_ref[...]) + sh_ref[...]).astype(BF16)

        z_ref[...] = _dot_nt(h_ref[...], w_ref[...]).astype(BF16)

    vec = pl.BlockSpec((1, D), lambda i, j: (0, 0))
    return pl.pallas_call(
        body, name=name, grid=(T // tm, n_rows // tn),
        in_specs=[pl.BlockSpec((tm, D), lambda i, j: (i, 0)), vec, vec, vec, pl.BlockSpec((tn, D), lambda i, j: (j + off, 0))],
        out_specs=[pl.BlockSpec((tm, D), lambda i, j: (i, 0)), pl.BlockSpec((tm, tn), lambda i, j: (i, j))],
        out_shape=[jax.ShapeDtypeStruct((T, D), BF16), jax.ShapeDtypeStruct((T, n_rows), BF16)],
        compiler_params=_cp(("parallel", "arbitrary")),
    )(x, g, sh, sc, wt)


def inproj_even(x, g, sh, sc, wt, cos, sa, sb, *, tm, name):
    T, D = x.shape
    N = wt.shape[0]

    def body(x_ref, g_ref, sh_ref, sc_ref, w_ref, c_ref, sa_ref, sb_ref, h_ref, u_ref, q_ref, kv_ref):
        n, _ = _rms(x_ref[...])
        h = (n * g_ref[...] * (1.0 + sc_ref[...]) + sh_ref[...]).astype(BF16)
        h_ref[...] = h
        z = _dot_nt(h, w_ref[...])
        u_ref[...] = z[:, :4 * LANES]
        c, a, b = c_ref[...], sa_ref[...], sb_ref[...]
        for s in range(4):
            q_ref[:, s * LANES:(s + 1) * LANES] = _rope(z[:, (4 + s) * LANES:(5 + s) * LANES], c, a, b).astype(BF16)
        kv_ref[:, :LANES] = _rope(z[:, 8 * LANES:9 * LANES], c, a, b).astype(BF16)
        kv_ref[:, LANES:] = z[:, 9 * LANES:].astype(BF16)

    vec = pl.BlockSpec((1, D), lambda i: (0, 0))
    row = lambda w: pl.BlockSpec((tm, w), lambda i: (i, 0))
    return pl.pallas_call(
        body, name=name, grid=(T // tm,),
        in_specs=[row(D), vec, vec, vec, _full((N, D)), row(LANES), row(LANES), row(LANES)],
        out_specs=[row(D), row(4 * LANES), row(4 * LANES), row(2 * LANES)],
        out_shape=[jax.ShapeDtypeStruct((T, D), BF16), jax.ShapeDtypeStruct((T, 4 * LANES), F32),
                   jax.ShapeDtypeStruct((T, 4 * LANES), BF16), jax.ShapeDtypeStruct((T, 2 * LANES), BF16)],
        compiler_params=_cp(("parallel",)),
    )(x, g, sh, sc, wt, cos, sa, sb)


def mm_post(a, w, x, g, gt, *, tm, name):
    T, K = a.shape
    D = w.shape[1]

    def body(a_ref, w_ref, x_ref, g_ref, gt_ref, y_ref, xn_ref):
        y = _dot(a_ref[...], w_ref[...])
        n, _ = _rms(y)
        y_ref[...] = y
        xn_ref[...] = x_ref[...] + gt_ref[...] * (n * g_ref[...])

    vec = pl.BlockSpec((1, D), lambda i: (0, 0))
    row = lambda w_: pl.BlockSpec((tm, w_), lambda i: (i, 0))
    return pl.pallas_call(
        body, name=name, grid=(T // tm,),
        in_specs=[row(K), _full((K, D)), row(D), vec, vec],
        out_specs=[row(D), row(D)],
        out_shape=[jax.ShapeDtypeStruct((T, D), F32), jax.ShapeDtypeStruct((T, D), F32)],
        compiler_params=_cp(("parallel",)),
    )(a, w, x, g, gt)


def post_bwd_mm(dxn, y, g, gt, w, *, tm, name):
    T, D = y.shape
    K = w.shape[0]

    def body(dxn_ref, y_ref, g_ref, gt_ref, w_ref, dy_ref, da_ref, dg_ref, dgt_ref):
        @pl.when(pl.program_id(0) == 0)
        def _():
            dg_ref[...] = jnp.zeros_like(dg_ref)
            dgt_ref[...] = jnp.zeros_like(dgt_ref)

        d = dxn_ref[...]
        n, r = _rms(y_ref[...])
        g_, gt_ = g_ref[...], gt_ref[...]
        dg_ref[...] += _colsum(d * gt_ * n)
        dgt_ref[...] += _colsum(d * g_ * n)
        dy = _rms_bwd(d * (gt_ * g_), n, r).astype(BF16)
        dy_ref[...] = dy
        da_ref[...] = _dot_nt(dy, w_ref[...]).astype(BF16)

    vec = pl.BlockSpec((1, D), lambda i: (0, 0))
    row = lambda w_: pl.BlockSpec((tm, w_), lambda i: (i, 0))
    return pl.pallas_call(
        body, name=name, grid=(T // tm,),
        in_specs=[row(D), row(D), vec, vec, _full((K, D))],
        out_specs=[row(D), row(K), vec, vec],
        out_shape=[jax.ShapeDtypeStruct((T, D), BF16), jax.ShapeDtypeStruct((T, K), BF16),
                   jax.ShapeDtypeStruct((1, D), F32), jax.ShapeDtypeStruct((1, D), F32)],
        compiler_params=_cp(("arbitrary",)),
    )(dxn, y, g, gt, w)


def mm_pre_bwd(dzs, wt, x, dres, g, sc, *, tm, tk, w_row_off=0, name):
    T, N = dzs[0].shape
    D = x.shape[1]
    nk = N // tk
    npart = len(dzs)
    off = w_row_off // tk
    has_res = dres is not None

    def body(*refs):
        dz_refs = refs[:npart]
        w_refs = refs[npart:2 * npart]
        rest = refs[2 * npart:]
        x_ref = rest[0]
        dres_ref = rest[1] if has_res else None
        g_ref, sc_ref, dx_ref, dg_ref, dsh_ref, dsc_ref, acc = rest[1 + has_res:]
        i, k = pl.program_id(0), pl.program_id(1)

        @pl.when(jnp.logical_and(i == 0, k == 0))
        def _():
            dg_ref[...] = jnp.zeros_like(dg_ref)
            dsh_ref[...] = jnp.zeros_like(dsh_ref)
            dsc_ref[...] = jnp.zeros_like(dsc_ref)

        part = _dot(dz_refs[0][...], w_refs[0][...])
        for p in range(1, npart):
            part = part + _dot(dz_refs[p][...], w_refs[p][...])

        @pl.when(k == 0)
        def _():
            acc[...] = part

        @pl.when(k > 0)
        def _():
            acc[...] += part

        @pl.when(k == nk - 1)
        def _():
            dh = acc[...]
            n, r = _rms(x_ref[...])
            g_, s1 = g_ref[...], 1.0 + sc_ref[...]
            dsh_ref[...] += _colsum(dh)
            dsc_ref[...] += _colsum(dh * n * g_)
            dg_ref[...] += _colsum(dh * s1 * n)
            dxp = _rms_bwd(dh * (g_ * s1), n, r)
            dx_ref[...] = dxp + dres_ref[...] if has_res else dxp

    vec = pl.BlockSpec((1, D), lambda i, k: (0, 0))
    row = pl.BlockSpec((tm, D), lambda i, k: (i, 0))
    w_specs = [pl.BlockSpec((tk, D), (lambda i, k, p=p: (k + off + p * nk, 0))) for p in range(npart)]
    res_specs, res_args = ([row], (dres,)) if has_res else ([], ())
    return pl.pallas_call(
        body, name=name, grid=(T // tm, nk),
        in_specs=[pl.BlockSpec((tm, tk), lambda i, k: (i, k))] * npart + w_specs + [row] + res_specs + [vec, vec],
        out_specs=[row, vec, vec, vec],
        out_shape=[jax.ShapeDtypeStruct((T, D), F32)] + [jax.ShapeDtypeStruct((1, D), F32)] * 3,
        scratch_shapes=[pltpu.VMEM((tm, D), F32)],
        compiler_params=_cp(("arbitrary", "arbitrary")),
    )(*dzs, *([wt] * npart), x, *res_args, g, sc)


def wgrad(a_parts, b, *, tr, extra=None, name):
    T, R = a_parts[0].shape
    D = b.shape[1]
    npart = len(a_parts)
    nr = R // tr

    def body(*refs):
        a_refs, b_ref = refs[:npart], refs[npart]
        g_ref = refs[-1]
        for p in range(npart):
            @pl.when(pl.program_id(0) // nr == p)
            def _():
                acc = _dot_tn(a_refs[p][...], b_ref[...])
                if extra is not None:
                    acc += _dot_tn(refs[npart + 1][...], refs[npart + 2][...])
                g_ref[...] = acc.astype(BF16)

    in_specs = [pl.BlockSpec((T, tr), (lambda r, p=p: (0, jnp.clip(r - p * nr, 0, nr - 1)))) for p in range(npart)]
    in_specs.append(_full((T, D)))
    args = [*a_parts, b]
    if extra is not None:
        a2, b2 = extra
        in_specs += [pl.BlockSpec((a2.shape[0], tr), lambda r: (0, r)), _full(b2.shape)]
        args += [a2, b2]
    return pl.pallas_call(
        body, name=name, grid=(npart * nr,),
        in_specs=in_specs, out_specs=pl.BlockSpec((tr, D), lambda r: (r, 0)),
        out_shape=jax.ShapeDtypeStruct((npart * R, D), BF16),
        compiler_params=_cp(("parallel",)),
    )(*args)


def _conv_ext(ref, r0, rows, total):
    top = ref[pl.ds(pl.multiple_of(jnp.maximum(r0 - PAD, 0), PAD), PAD), :]
    mid = ref[pl.ds(r0, rows), :]
    bot = ref[pl.ds(pl.multiple_of(jnp.minimum(r0 + rows, total - PAD), PAD), PAD), :]
    top = jnp.where(r0 > 0, top, jnp.zeros_like(top))
    bot = jnp.where(r0 + rows < total, bot, jnp.zeros_like(bot))
    return jnp.concatenate([top, mid, bot], axis=0).astype(F32)


def _shift_rows(a, k):
    return pltpu.roll(a, k % a.shape[0], 0)


def _conv3(x, w, b):
    return w[0:1] * _shift_rows(x, 1) + w[1:2] * x + w[2:3] * _shift_rows(x, -1) + b


def _gate_up_specs(rows_, wblk, nb):
    return [pl.BlockSpec((rows_, wblk), lambda j: (0, j)), pl.BlockSpec((rows_, wblk), lambda j: (0, j + nb))]


def conv_fwd(hu, cw, cb, *, rows, wblk, name):
    L, N2 = hu.shape
    nb = N2 // 2 // wblk
    nchunk = L // rows

    def body(hg_ref, hu_ref, wg_ref, wu_ref, bg_ref, bu_ref, a_ref):
        def chunk(ci, carry):
            r0 = pl.multiple_of(ci * rows, rows)
            gate = _conv3(_conv_ext(hg_ref, r0, rows, L), wg_ref[...], bg_ref[...])[PAD:PAD + rows]
            up = _conv3(_conv_ext(hu_ref, r0, rows, L), wu_ref[...], bu_ref[...])[PAD:PAD + rows]
            a_ref[pl.ds(r0, rows), :] = (gate * jax.nn.sigmoid(gate) * up).astype(BF16)
            return carry

        lax.fori_loop(0, nchunk, chunk, 0)

    return pl.pallas_call(
        body, name=name, grid=(nb,),
        in_specs=_gate_up_specs(L, wblk, nb) + _gate_up_specs(3, wblk, nb) + _gate_up_specs(1, wblk, nb),
        out_specs=pl.BlockSpec((L, wblk), lambda j: (0, j)),
        out_shape=jax.ShapeDtypeStruct((L, N2 // 2), BF16),
        compiler_params=_cp(("parallel",)),
    )(hu, hu, cw, cw, cb, cb)


def conv_bwd(da, hu, cw, cb, *, rows, wblk, name):
    L, N2 = hu.shape
    F = N2 // 2
    nb = F // wblk
    nchunk = L // rows
    mid = slice(PAD, PAD + rows)

    def body(da_ref, hg_ref, hu_ref, wg_ref, wu_ref, bg_ref, bu_ref, dg_ref, du_ref, dwg_ref, dwu_ref, dbg_ref, dbu_ref):
        for ref in (dwg_ref, dwu_ref, dbg_ref, dbu_ref):
            ref[...] = jnp.zeros_like(ref)

        def half_bwd(x, dh, w_ref, dx_ref, dw_ref, db_ref, r0):
            w = w_ref[...]
            nxt, prv = _shift_rows(dh, -1)[mid], _shift_rows(dh, 1)[mid]
            dhm, xm = dh[mid], x[mid]
            dx_ref[pl.ds(r0, rows), :] = (w[0:1] * nxt + w[1:2] * dhm + w[2:3] * prv).astype(BF16)
            db_ref[...] += _colsum(dhm)
            dw_ref[0:1, :] += _colsum(nxt * xm)
            dw_ref[1:2, :] += _colsum(dhm * xm)
            dw_ref[2:3, :] += _colsum(prv * xm)

        def chunk(ci, carry):
            r0 = pl.multiple_of(ci * rows, rows)
            xg = _conv_ext(hg_ref, r0, rows, L)
            xu = _conv_ext(hu_ref, r0, rows, L)
            d = _conv_ext(da_ref, r0, rows, L)
            gate = _conv3(xg, wg_ref[...], bg_ref[...])
            up = _conv3(xu, wu_ref[...], bu_ref[...])
            sg = jax.nn.sigmoid(gate)
            silu = gate * sg
            half_bwd(xu, d * silu, wu_ref, du_ref, dwu_ref, dbu_ref, r0)
            half_bwd(xg, d * up * (sg + silu * (1.0 - sg)), wg_ref, dg_ref, dwg_ref, dbg_ref, r0)
            return carry

        lax.fori_loop(0, nchunk, chunk, 0)

    blk = lambda r: pl.BlockSpec((r, wblk), lambda j: (0, j))
    return pl.pallas_call(
        body, name=name, grid=(nb,),
        in_specs=[blk(L)] + _gate_up_specs(L, wblk, nb) + _gate_up_specs(3, wblk, nb) + _gate_up_specs(1, wblk, nb),
        out_specs=[blk(L), blk(L), blk(3), blk(3), blk(1), blk(1)],
        out_shape=[jax.ShapeDtypeStruct((L, F), BF16)] * 2 + [jax.ShapeDtypeStruct((3, F), F32)] * 2
        + [jax.ShapeDtypeStruct((1, F), F32)] * 2,
        compiler_params=_cp(("parallel",)),
    )(da, hu, hu, cw, cw, cb, cb)


def _window_sums(pad_ref, w, lead):
    a = pad_ref[...]
    k = 1
    while k < w:
        a = a + _shift_rows(a, -k)
        k *= 2
    return _shift_rows(a, lead) if lead else a


def _pool_counts(L, h):
    t = lax.broadcasted_iota(jnp.int32, (L, 1), 0)
    return (jnp.minimum(t + h, L) - jnp.maximum(t - h, 0)).astype(F32)


def _pooled(u_ref, pad_ref, L, w):
    h = w // 2
    pad_ref[pl.ds(PAD, L), :] = u_ref[...]
    win = _window_sums(pad_ref, w, h)[PAD:PAD + L]
    return win / _pool_counts(L, h) - u_ref[...]


def _zero_pad_edges(pad_ref, L):
    z = jnp.zeros((PAD, LANES), F32)
    pad_ref[pl.ds(0, PAD), :] = z
    pad_ref[pl.ds(PAD + L, PAD), :] = z


def pool_fwd(u, w_pool, pool_scale, *, name):
    L = u.shape[0]

    def body(u_ref, w_ref, ps_ref, p_ref, pad_ref):
        _zero_pad_edges(pad_ref, L)
        for gi, win in enumerate(POOL_WINDOWS):
            @pl.when(pl.program_id(0) == gi)
            def _():
                pooled = _pooled(u_ref, pad_ref, L, win)
                p_ref[...] = (_dot(pooled.astype(BF16), w_ref[...].astype(BF16)) * ps_ref[...]).astype(BF16)

    return pl.pallas_call(
        body, name=name, grid=(len(POOL_WINDOWS),),
        in_specs=[pl.BlockSpec((L, LANES), lambda gi: (0, gi)), pl.BlockSpec((None, LANES, LANES), lambda gi: (gi, 0, 0)),
                  pl.BlockSpec((1, LANES), lambda gi: (0, gi))],
        out_specs=pl.BlockSpec((L, LANES), lambda gi: (0, gi)),
        out_shape=jax.ShapeDtypeStruct((L, 4 * LANES), BF16),
        scratch_shapes=[pltpu.VMEM((L + 2 * PAD, LANES), F32)],
        compiler_params=_cp(("parallel",)),
    )(u, w_pool, pool_scale)


def pool_bwd(u, dpa, w_pool, pool_scale, *, name):
    L = u.shape[0]

    def body(u_ref, dp_ref, w_ref, ps_ref, du_ref, dw_ref, dps_ref, pad_ref):
        _zero_pad_edges(pad_ref, L)
        for gi, win in enumerate(POOL_WINDOWS):
            @pl.when(pl.program_id(0) == gi)
            def _():
                h = win // 2
                wb = w_ref[...].astype(BF16)
                pooled = _pooled(u_ref, pad_ref, L, win).astype(BF16)
                dp = dp_ref[...].astype(F32)
                dps_ref[...] = _colsum(dp * _dot(pooled, wb))
                dy = (dp * ps_ref[...]).astype(BF16)
                dw_ref[...] = _dot_tn(pooled, dy)
                dpooled = _dot_nt(dy, wb)
                pad_ref[pl.ds(PAD, L), :] = dpooled / _pool_counts(L, h)
                du_ref[...] = (_window_sums(pad_ref, win, h - 1)[PAD:PAD + L] - dpooled).astype(BF16)

    return pl.pallas_call(
        body, name=name, grid=(len(POOL_WINDOWS),),
        in_specs=[pl.BlockSpec((L, LANES), lambda gi: (0, gi)), pl.BlockSpec((L, LANES), lambda gi: (0, gi)),
                  pl.BlockSpec((None, LANES, LANES), lambda gi: (gi, 0, 0)), pl.BlockSpec((1, LANES), lambda gi: (0, gi))],
        out_specs=[pl.BlockSpec((L, LANES), lambda gi: (0, gi)), pl.BlockSpec((None, LANES, LANES), lambda gi: (gi, 0, 0)),
                   pl.BlockSpec((1, LANES), lambda gi: (0, gi))],
        out_shape=[jax.ShapeDtypeStruct((L, 4 * LANES), BF16), jax.ShapeDtypeStruct((4, LANES, LANES), F32),
                   jax.ShapeDtypeStruct((1, 4 * LANES), F32)],
        scratch_shapes=[pltpu.VMEM((L + 2 * PAD, LANES), F32)],
        compiler_params=_cp(("parallel",)),
    )(u, dpa, w_pool, pool_scale)


def _attn_probs(q4, band, kvc, sink_ref, kh, mask4):
    scale = HEAD_DIM ** -0.5
    ks = slice(kh * HEAD_DIM, (kh + 1) * HEAD_DIM)
    s_loc = jnp.where(mask4, _dot_nt(q4, band[:, ks]) * scale, NEG_INF)
    s_ctx = _dot_nt(q4, kvc[:, ks]) * scale
    sk = jnp.concatenate([jnp.full((BLK, 1), sink_ref[kh * GQA + hh], F32) for hh in range(GQA)], axis=0)
    m = jnp.maximum(jnp.maximum(jnp.max(s_loc, axis=-1, keepdims=True), jnp.max(s_ctx, axis=-1, keepdims=True)), sk)
    e_loc, e_ctx, e_s = jnp.exp(s_loc - m), jnp.exp(s_ctx - m), jnp.exp(sk - m)
    inv = 1.0 / (jnp.sum(e_loc, axis=-1, keepdims=True) + jnp.sum(e_ctx, axis=-1, keepdims=True) + e_s)
    return e_loc * inv, e_ctx * inv, e_s * inv


def _attn_block(n, L):
    start = pl.multiple_of(jnp.clip((n - 1) * BLK, 0, L - 3 * BLK), BLK)
    qpos = n * BLK + lax.broadcasted_iota(jnp.int32, (BLK, 3 * BLK), 0)
    kpos = start + lax.broadcasted_iota(jnp.int32, (BLK, 3 * BLK), 1)
    mask = jnp.abs(kpos - qpos) <= WINDOW
    return start, jnp.concatenate([mask] * GQA, axis=0)


def _stack_heads(ref, kh):
    return jnp.concatenate([ref[:, (kh * GQA + hh) * HEAD_DIM:(kh * GQA + hh + 1) * HEAD_DIM] for hh in range(GQA)], axis=0)


def attn_fwd(q, kv, kvc, sink, *, name):
    L = q.shape[0]
    C = kvc.shape[0]

    def body(q_ref, kv_ref, kvc_ref, sink_ref, o_ref, o_scr):
        start, mask4 = _attn_block(pl.program_id(0), L)
        band = kv_ref[pl.ds(start, 3 * BLK), :]
        kvc_ = kvc_ref[...]
        for kh in range(N_KV_HEADS):
            q4 = _stack_heads(q_ref, kh)
            p_loc, p_ctx, _ = _attn_probs(q4, band, kvc_, sink_ref, kh, mask4)
            vs = slice(2 * HEAD_DIM + kh * HEAD_DIM, 2 * HEAD_DIM + (kh + 1) * HEAD_DIM)
            o4 = _dot(p_loc.astype(BF16), band[:, vs]) + _dot(p_ctx.astype(BF16), kvc_[:, vs])
            for hh in range(GQA):
                h = kh * GQA + hh
                o_scr[:, h * HEAD_DIM:(h + 1) * HEAD_DIM] = o4[hh * BLK:(hh + 1) * BLK]
        o_ref[...] = o_scr[...].astype(BF16)

    return pl.pallas_call(
        body, name=name, grid=(L // BLK,),
        in_specs=[pl.BlockSpec((BLK, 4 * LANES), lambda n: (n, 0)), _full((L, 2 * LANES)), _full((C, 2 * LANES)),
                  pl.BlockSpec(memory_space=pltpu.SMEM)],
        out_specs=pl.BlockSpec((BLK, 4 * LANES), lambda n: (n, 0)),
        out_shape=jax.ShapeDtypeStruct((L, 4 * LANES), BF16),
        scratch_shapes=[pltpu.VMEM((BLK, 4 * LANES), F32)],
        compiler_params=_cp(("parallel",)),
    )(q, kv, kvc, sink)


def attn_bwd(q, kv, kvc, sink, dpa, cos, sa, sb, *, name):
    L = q.shape[0]
    C = kvc.shape[0]
    nb = L // BLK
    scale = HEAD_DIM ** -0.5

    def body(q_ref, kv_ref, kvc_ref, sink_ref, do_ref, c_ref, sa_ref, sb_ref, cq_ref, saq_ref, sbq_ref,
             dq_ref, dkv_ref, dkvc_ref, dsink_ref, dkv_acc, dkvc_acc, dq_scr, band_scr, ctx_scr):
        n = pl.program_id(0)

        @pl.when(n == 0)
        def _():
            dkv_acc[...] = jnp.zeros_like(dkv_acc)
            dkvc_acc[...] = jnp.zeros_like(dkvc_acc)
            dsink_ref[...] = jnp.zeros_like(dsink_ref)

        start, mask4 = _attn_block(n, L)
        band = kv_ref[pl.ds(start, 3 * BLK), :]
        kvc_ = kvc_ref[...]
        lane = lax.broadcasted_iota(jnp.int32, (1, LANES), 1)
        dsink = jnp.zeros((1, LANES), F32)
        for kh in range(N_KV_HEADS):
            ks = slice(kh * HEAD_DIM, (kh + 1) * HEAD_DIM)
            vs = slice(2 * HEAD_DIM + kh * HEAD_DIM, 2 * HEAD_DIM + (kh + 1) * HEAD_DIM)
            q4 = _stack_heads(q_ref, kh)
            do4 = _stack_heads(do_ref, kh)
            p_loc, p_ctx, p_s = _attn_probs(q4, band, kvc_, sink_ref, kh, mask4)
            dp_loc = _dot_nt(do4, band[:, vs])
            dp_ctx = _dot_nt(do4, kvc_[:, vs])
            delta = jnp.sum(p_loc * dp_loc, axis=-1, keepdims=True) + jnp.sum(p_ctx * dp_ctx, axis=-1, keepdims=True)
            ds_loc = (p_loc * (dp_loc - delta) * scale).astype(BF16)
            ds_ctx = (p_ctx * (dp_ctx - delta) * scale).astype(BF16)
            dsk = p_s * delta
            for hh in range(GQA):
                h = kh * GQA + hh
                dsink = dsink - jnp.where(lane == h, jnp.sum(dsk[hh * BLK:(hh + 1) * BLK], axis=0, keepdims=True), 0.0)
            dq4 = _dot(ds_loc, band[:, ks]) + _dot(ds_ctx, kvc_[:, ks])
            for hh in range(GQA):
                h = kh * GQA + hh
                dq_scr[:, h * HEAD_DIM:(h + 1) * HEAD_DIM] = dq4[hh * BLK:(hh + 1) * BLK]
            band_scr[:, ks] = _dot_tn(ds_loc, q4)
            band_scr[:, vs] = _dot_tn(p_loc.astype(BF16), do4)
            ctx_scr[:, ks] = _dot_tn(ds_ctx, q4)
            ctx_scr[:, vs] = _dot_tn(p_ctx.astype(BF16), do4)
        dsink_ref[...] += dsink
        dkv_acc[pl.ds(start, 3 * BLK), :] += band_scr[...]
        dkvc_acc[...] += ctx_scr[...]
        c, a, b = cq_ref[...], -saq_ref[...], -sbq_ref[...]
        for s in range(4):
            dq_ref[:, s * LANES:(s + 1) * LANES] = _rope(dq_scr[:, s * LANES:(s + 1) * LANES], c, a, b).astype(BF16)

        @pl.when(n == nb - 1)
        def _():
            dkv_ref[:, :LANES] = _rope(dkv_acc[:, :LANES], c_ref[...], -sa_ref[...], -sb_ref[...]).astype(BF16)
            dkv_ref[:, LANES:] = dkv_acc[:, LANES:].astype(BF16)
            dkvc_ref[...] = dkvc_acc[...].astype(BF16)

    blk = lambda w: pl.BlockSpec((BLK, w), lambda n: (n, 0))
    return pl.pallas_call(
        body, name=name, grid=(nb,),
        in_specs=[blk(4 * LANES), _full((L, 2 * LANES)), _full((C, 2 * LANES)), pl.BlockSpec(memory_space=pltpu.SMEM),
                  pl.BlockSpec((BLK, 4 * LANES), lambda n: (n, 1)),
                  _full((L, LANES)), _full((L, LANES)), _full((L, LANES)), blk(LANES), blk(LANES), blk(LANES)],
        out_specs=[blk(4 * LANES), _full((L, 2 * LANES)), _full((C, 2 * LANES)), _full((1, LANES))],
        out_shape=[jax.ShapeDtypeStruct((L, 4 * LANES), BF16), jax.ShapeDtypeStruct((L, 2 * LANES), BF16),
                   jax.ShapeDtypeStruct((C, 2 * LANES), BF16), jax.ShapeDtypeStruct((1, LANES), F32)],
        scratch_shapes=[pltpu.VMEM((L, 2 * LANES), F32), pltpu.VMEM((C, 2 * LANES), F32), pltpu.VMEM((BLK, 4 * LANES), F32),
                        pltpu.VMEM((3 * BLK, 2 * LANES), F32), pltpu.VMEM((C, 2 * LANES), F32)],
        compiler_params=_cp(("arbitrary",)),
    )(q, kv, kvc, sink, dpa, cos, sa, sb, cos, sa, sb)


def _gelu_parts(x):
    th = jnp.tanh(SQRT_2_OVER_PI * (x + GELU_C * x * x * x))
    return 0.5 * x * (1.0 + th), th


def _gelu_grad(x, th):
    return 0.5 * (1.0 + th) + 0.5 * x * (1.0 - th * th) * SQRT_2_OVER_PI * (1.0 + 3.0 * GELU_C * x * x)


def _layernorm(v):
    mu = jnp.mean(v, axis=-1, keepdims=True)
    vc = v - mu
    rstd = lax.rsqrt(jnp.mean(vc * vc, axis=-1, keepdims=True) + EPS)
    return vc * rstd, rstd


def sgu_fwd(z1, ln_g, ln_b, ws, bst, *, name):
    L, W2 = z1.shape
    W = W2 // 2
    ng = W // LANES

    def body(z_ref, g_ref, b_ref, ws_ref, bs_ref, o_ref):
        z, _ = _gelu_parts(z_ref[...].astype(F32))
        xhat, _ = _layernorm(z[:, W:])
        vln = (xhat * g_ref[...] + b_ref[...]).astype(BF16)
        for gi in range(ng):
            cs = slice(gi * LANES, (gi + 1) * LANES)
            s = _dot(ws_ref[gi].astype(BF16), vln[:, cs]) + bs_ref[:, gi:gi + 1]
            o_ref[:, cs] = (z[:, cs] * s).astype(BF16)

    vec = _full((1, W))
    return pl.pallas_call(
        body, name=name, grid=(L // BLK,),
        in_specs=[pl.BlockSpec((BLK, W2), lambda n: (n, 0)), vec, vec, _full((ng, LANES, LANES)), _full((BLK, ng))],
        out_specs=pl.BlockSpec((BLK, W), lambda n: (n, 0)),
        out_shape=jax.ShapeDtypeStruct((L, W), BF16),
        compiler_params=_cp(("parallel",)),
    )(z1, ln_g, ln_b, ws, bst)


def sgu_bwd(z1, dus, ln_g, ln_b, ws, bst, *, name):
    L, W2 = z1.shape
    W = W2 // 2
    ng = W // LANES

    def body(z_ref, d_ref, g_ref, b_ref, ws_ref, bs_ref, dz_ref, dws_ref, dbs_ref, dg_ref, db_ref, dv_scr):
        @pl.when(pl.program_id(0) == 0)
        def _():
            dws_ref[...] = jnp.zeros_like(dws_ref)
            dbs_ref[...] = jnp.zeros_like(dbs_ref)
            dg_ref[...] = jnp.zeros_like(dg_ref)
            db_ref[...] = jnp.zeros_like(db_ref)

        zp = z_ref[...].astype(F32)
        z, th = _gelu_parts(zp)
        xhat, rstd = _layernorm(z[:, W:])
        vln = (xhat * g_ref[...] + b_ref[...]).astype(BF16)
        d = d_ref[...].astype(F32)
        lane = lax.broadcasted_iota(jnp.int32, (1, LANES), 1)
        dbs = jnp.zeros((BLK, LANES), F32)
        dgel = _gelu_grad(zp, th)
        for gi in range(ng):
            cs = slice(gi * LANES, (gi + 1) * LANES)
            wb = ws_ref[gi].astype(BF16)
            s = _dot(wb, vln[:, cs]) + bs_ref[:, gi:gi + 1]
            dz_ref[:, cs] = (d[:, cs] * s * dgel[:, cs]).astype(BF16)
            ds = d[:, cs] * z[:, cs]
            dbs = dbs + jnp.where(lane == gi, jnp.sum(ds, axis=-1, keepdims=True), 0.0)
            dsb = ds.astype(BF16)
            dws_ref[gi] += _dot_nt(dsb, vln[:, cs])
            dv_scr[:, cs] = _dot_tn(wb, dsb)
        dbs_ref[...] += dbs
        dvln = dv_scr[...]
        dg_ref[...] += _colsum(dvln * xhat)
        db_ref[...] += _colsum(dvln)
        dxh = dvln * g_ref[...]
        dv = rstd * (dxh - jnp.mean(dxh, axis=-1, keepdims=True) - xhat * jnp.mean(dxh * xhat, axis=-1, keepdims=True))
        dz_ref[:, W:] = (dv * dgel[:, W:]).astype(BF16)

    vec = _full((1, W))
    return pl.pallas_call(
        body, name=name, grid=(L // BLK,),
        in_specs=[pl.BlockSpec((BLK, W2), lambda n: (n, 0)), pl.BlockSpec((BLK, W), lambda n: (n, 0)), vec, vec,
                  _full((ng, LANES, LANES)), _full((BLK, ng))],
        out_specs=[pl.BlockSpec((BLK, W2), lambda n: (n, 0)), _full((ng, LANES, LANES)), _full((BLK, LANES)), vec, vec],
        out_shape=[jax.ShapeDtypeStruct((L, W2), BF16), jax.ShapeDtypeStruct((ng, LANES, LANES), F32),
                   jax.ShapeDtypeStruct((BLK, LANES), F32), jax.ShapeDtypeStruct((1, W), F32), jax.ShapeDtypeStruct((1, W), F32)],
        scratch_shapes=[pltpu.VMEM((BLK, W), F32)],
        compiler_params=_cp(("arbitrary",)),
    )(z1, dus, ln_g, ln_b, ws, bst)


def loss_grad(xo, target, *, tm, name):
    T, D = xo.shape

    def body(x_ref, t_ref, l_ref, d_ref):
        @pl.when(pl.program_id(0) == 0)
        def _():
            l_ref[...] = jnp.zeros_like(l_ref)

        e = x_ref[...] - t_ref[...]
        l_ref[...] += 0.5 * jnp.sum(jnp.mean(e * e, axis=-1, keepdims=True), axis=0, keepdims=True)
        d_ref[...] = e * (1.0 / D)

    row = pl.BlockSpec((tm, D), lambda i: (i, 0))
    return pl.pallas_call(
        body, name=name, grid=(T // tm,), in_specs=[row, row], out_specs=[_full((1, 1)), row],
        out_shape=[jax.ShapeDtypeStruct((1, 1), F32), jax.ShapeDtypeStruct((T, D), F32)],
        compiler_params=_cp(("arbitrary",)),
    )(xo, target)


def adamw(w, m, v, gparts, *, tr, name):
    R, Wd = w.shape
    S = gparts.shape[0]

    def body(w_ref, m_ref, v_ref, gp_ref, g_ref, d_ref, nm_ref, nv_ref):
        g = gp_ref[0].astype(F32)
        for s in range(1, S):
            g = g + gp_ref[s].astype(F32)
        m_ = ADAM_B1 * m_ref[...] + (1.0 - ADAM_B1) * g
        v_ = ADAM_B2 * v_ref[...] + (1.0 - ADAM_B2) * (g * g)
        g_ref[...] = g
        nm_ref[...] = m_
        nv_ref[...] = v_
        d_ref[...] = -ADAM_LR * ((m_ / BC1) / (jnp.sqrt(v_ / BC2) + ADAM_EPS) + ADAM_WD * w_ref[...])

    row = pl.BlockSpec((tr, Wd), lambda i: (i, 0))
    return pl.pallas_call(
        body, name=name, grid=(R // tr,),
        in_specs=[row, row, row, pl.BlockSpec((S, tr, Wd), lambda i: (0, i, 0))],
        out_specs=[row] * 4, out_shape=[jax.ShapeDtypeStruct((R, Wd), F32)] * 4,
        compiler_params=_cp(("parallel",)),
    )(w, m, v, gparts)


def ada_fwd_mm(cs, w_ada, b_loc, *, name):
    R, D = cs.shape
    nl, _, n = w_ada.shape

    def body(c_ref, w_ref, b_ref, s_ref, m_ref):
        c = c_ref[...]
        s = c * jax.nn.sigmoid(c)
        s_ref[...] = s
        for i in range(nl):
            m_ref[i] = _dot(s.astype(BF16), w_ref[i].astype(BF16)) + b_ref[i:i + 1, :]

    return pl.pallas_call(
        body, name=name, in_specs=[_full((R, D)), _full((nl, D, n)), _full((nl, n))],
        out_specs=[_full((R, D)), _full((nl, R, n))], grid=(1,),
        out_shape=[jax.ShapeDtypeStruct((R, D), F32), jax.ShapeDtypeStruct((nl, R, n), F32)],
        compiler_params=_cp(("arbitrary",)),
    )(cs, w_ada, b_loc)


def ada_bwd_mm(s, c_ctx, dall, w_ada, *, name):
    R, D = s.shape
    nl, _, n = w_ada.shape

    def body(s_ref, cc_ref, d_ref, w_ref, gw_ref, dcc_ref):
        sb = s_ref[...].astype(BF16)
        row = lax.broadcasted_iota(jnp.int32, (R, 1), 0)
        dctx = d_ref[0, 1:2, :]
        for dv in range(1, N_DEV):
            dctx = dctx + d_ref[dv, 1:2, :]
        for i in range(nl):
            dm = jnp.zeros((R, n), F32)
            for dv in range(N_DEV):
                dm = dm + jnp.where(row == dv, d_ref[dv, 2 * i:2 * i + 1, :], 0.0)
            if i == 0:
                dm = dm + jnp.where(row == N_DEV, dctx, 0.0)
            gw_ref[i] = _dot_tn(sb, dm.astype(BF16))
        cc = cc_ref[...]
        sg = jax.nn.sigmoid(cc)
        ds = _dot_nt(jnp.broadcast_to(dctx, (8, n)).astype(BF16), w_ref[0].astype(BF16))
        dcc_ref[...] = ds * (sg * (1.0 + cc * (1.0 - sg)))

    return pl.pallas_call(
        body, name=name, grid=(1,),
        in_specs=[_full((R, D)), _full((1, D)), _full((N_DEV, 3, n)), _full((nl, D, n))],
        out_specs=[_full((nl, D, n)), _full((8, D))],
        out_shape=[jax.ShapeDtypeStruct((nl, D, n), F32), jax.ShapeDtypeStruct((8, D), F32)],
        compiler_params=_cp(("arbitrary",)),
    )(s, c_ctx, dall, w_ada)


def _place():
    x, y, c = lax.axis_index("x"), lax.axis_index("y"), lax.axis_index("c")
    return x, y, c


def _lin(p):
    return 4 * p[0] + 2 * p[1] + p[2]


def all_gather_small(xb, *, reduce=False, name):
    R, W = xb.shape

    def body(x_ref, *rest):
        out_ref = rest[0]
        send_sems, recv_sems, local_sem = rest[-3:]
        x, y, c = _place()
        me, sibling = (x, y, c), (x, y, 1 - c)
        chips = [(1 - x, y), (x, 1 - y), (1 - x, 1 - y)]

        def copy(k, block, to, src=None):
            dst = out_ref.at[_lin(block)]
            return pltpu.make_async_remote_copy(
                src_ref=dst if src is None else src, dst_ref=dst, send_sem=send_sems.at[k], recv_sem=recv_sems.at[k],
                device_id=to, device_id_type=MESH)

        mine = pltpu.make_async_copy(x_ref, out_ref.at[_lin(me)], local_sem)
        mine.start()
        first = [copy(0, me, sibling, src=x_ref)]
        first += [copy(1 + j, me, (*chip, c), src=x_ref) for j, chip in enumerate(chips)]
        for cp in first:
            cp.start()
        passed = [copy(4 + j, (*chip, c), sibling) for j, chip in enumerate(chips)]
        for j, chip in enumerate(chips):
            copy(1 + j, (*chip, c), me).wait_recv()
            passed[j].start()
        copy(0, sibling, me).wait_recv()
        for j, chip in enumerate(chips):
            copy(4 + j, (*chip, 1 - c), me).wait_recv()
        for cp in first + passed:
            cp.wait_send()
        mine.wait()
        if reduce:
            acc = out_ref[0]
            for dv in range(1, N_DEV):
                acc = acc + out_ref[dv]
            rest[1][...] = acc

    vm = pl.BlockSpec(memory_space=pltpu.VMEM)
    out_shape = [jax.ShapeDtypeStruct((N_DEV, R, W), xb.dtype)]
    if reduce:
        out_shape.append(jax.ShapeDtypeStruct((R, W), xb.dtype))
    res = pl.pallas_call(
        body, name=name, in_specs=[vm], out_specs=[vm] * len(out_shape), out_shape=out_shape,
        scratch_shapes=[pltpu.SemaphoreType.DMA((7,)), pltpu.SemaphoreType.DMA((7,)), pltpu.SemaphoreType.DMA],
        compiler_params=pltpu.CompilerParams(vmem_limit_bytes=VMEM_LIMIT),
    )(xb)
    return res if reduce else res[0]


HBM_SPEC = pl.BlockSpec(memory_space=pltpu.HBM)
SEM_SPEC = pl.BlockSpec(memory_space=pltpu.SEMAPHORE)
ORDERED_EFFECT = pltpu.SideEffectType.DATAFLOW_SIDE_EFFECTING


def _exchange_copies(srcs, lands, sems, scatter):
    x, y, c = _place()
    me = _lin((x, y, c))
    for j in range(len(srcs)):
        r = lands[j].shape[0] // N_DEV
        block = lambda d, j=j, r=r: pl.ds(pl.multiple_of(d * r, 16), r)
        for k in range(1, N_DEV):
            peer = (x ^ (k >> 2), y ^ ((k >> 1) & 1), c ^ (k & 1))
            src = srcs[j].at[block(_lin(peer)), :] if scatter else srcs[j]
            mk = lambda dst, j=j, k=k, peer=peer, src=src: pltpu.make_async_remote_copy(
                src_ref=src, dst_ref=dst, send_sem=sems[2 * j].at[k - 1], recv_sem=sems[2 * j + 1].at[k - 1],
                device_id=peer, device_id_type=MESH)
            yield mk(lands[j].at[block(me), :]), mk(lands[j].at[block(_lin(peer)), :])


def exchange_start(srcs, lands, *, scatter, name):
    nw = len(srcs)

    def body(*refs):
        for start, _ in _exchange_copies(refs[:nw], refs[nw:2 * nw], refs[2 * nw:4 * nw], scatter):
            start.start()
        refs[-1][...] = jnp.zeros_like(refs[-1])

    thru = [pltpu.HBM(a.shape, a.dtype) for a in (*srcs, *lands)]
    res = pl.pallas_call(
        body, name=name, in_specs=[HBM_SPEC] * (2 * nw),
        out_specs=[SEM_SPEC] * (2 * nw) + [HBM_SPEC] * (2 * nw) + [pl.BlockSpec(memory_space=pltpu.VMEM)],
        out_shape=[pltpu.SemaphoreType.DMA((N_DEV - 1,))] * (2 * nw) + thru + [jax.ShapeDtypeStruct((8, LANES), F32)],
        input_output_aliases={i: 2 * nw + i for i in range(2 * nw)},
        compiler_params=pltpu.CompilerParams(has_side_effects=ORDERED_EFFECT),
    )(*[pltpu.with_memory_space_constraint(a, pltpu.HBM) for a in (*srcs, *lands)])
    return res[:2 * nw], res[2 * nw:3 * nw], res[3 * nw:4 * nw], res[-1]


def exchange_wait(srcs, lands, sems, after, *, scatter, name):
    nw = len(srcs)

    def body(*refs):
        for _, arrive in _exchange_copies(refs[:nw], refs[nw:2 * nw], refs[2 * nw:4 * nw], scatter):
            arrive.wait_send()
            arrive.wait_recv()

    res = pl.pallas_call(
        body, name=name, in_specs=[HBM_SPEC] * (2 * nw) + [SEM_SPEC] * (2 * nw) + [pl.BlockSpec(memory_space=pl.ANY)],
        out_specs=[HBM_SPEC] * (2 * nw), out_shape=[pltpu.HBM(a.shape, a.dtype) for a in (*srcs, *lands)],
        input_output_aliases={i: i for i in range(2 * nw)},
        compiler_params=pltpu.CompilerParams(has_side_effects=ORDERED_EFFECT),
    )(*srcs, *lands, *sems, after)
    me = _lin(_place())
    done = []
    for src, land in zip(res[:nw], res[nw:]):
        r = land.shape[0] // N_DEV
        own = lax.dynamic_slice_in_dim(src, me * r, r, 0) if scatter else src
        done.append(lax.dynamic_update_slice(land, own, (me * r, 0)))
    return done


def _rope_tables(L):
    t = jnp.arange(L)
    inv = ROPE_BASE ** (-jnp.arange(ROPE_FREQS, dtype=F32) / ROPE_FREQS)
    ar = (t // GRID_W).astype(F32)[:, None] * inv
    ac = (t % GRID_W).astype(F32)[:, None] * inv
    z = jnp.zeros_like(ar)
    cos = jnp.concatenate([jnp.cos(ar), jnp.cos(ar), jnp.cos(ac), jnp.cos(ac)], axis=1)
    sa = jnp.concatenate([-jnp.sin(ar), z, -jnp.sin(ac), z], axis=1)
    sb = jnp.concatenate([z, jnp.sin(ar), z, jnp.sin(ac)], axis=1)
    return tuple(jnp.tile(a, (1, LANES // HEAD_DIM)) for a in (cos, sa, sb))


def _rows128(a):
    f = a.reshape(-1)
    n = -(-f.shape[0] // (8 * LANES)) * 8 * LANES
    return jnp.pad(f, (0, n - f.shape[0])).reshape(-1, LANES)


def _landing(rows, like):
    return lax.empty((N_DEV * rows, like.shape[1]), like.dtype)


def kernel(x, c, ctx, c_ctx, w_ada, b_ada, g_mix_pre, g_mix_post, g_ffn_pre, g_ffn_post, w_in_even, w_pool, pool_scale, attn_sink, w_out_even, w_in_odd, sgu_ln_g, sgu_ln_b, sgu_w, sgu_b, w_out_odd, w_ffn_up, ffn_conv_w, ffn_conv_b, w_ffn_down, loss_target, m_c_ctx, m_w_ada, m_b_ada, m_g_mix_pre, m_g_mix_post, m_g_ffn_pre, m_g_ffn_post, m_w_in_even, m_w_pool, m_pool_scale, m_attn_sink, m_w_out_even, m_w_in_odd, m_sgu_ln_g, m_sgu_ln_b, m_sgu_w, m_sgu_b, m_w_out_odd, m_w_ffn_up, m_ffn_conv_w, m_ffn_conv_b, m_w_ffn_down, v_c_ctx, v_w_ada, v_b_ada, v_g_mix_pre, v_g_mix_post, v_g_ffn_pre, v_g_ffn_post, v_w_in_even, v_w_pool, v_pool_scale, v_attn_sink, v_w_out_even, v_w_in_odd, v_sgu_ln_g, v_sgu_ln_b, v_sgu_w, v_sgu_b, v_w_out_odd, v_w_ffn_up, v_ffn_conv_w, v_ffn_conv_b, v_w_ffn_down):
    P = dict(c_ctx=c_ctx, w_ada=w_ada, b_ada=b_ada, g_mix_pre=g_mix_pre, g_mix_post=g_mix_post, g_ffn_pre=g_ffn_pre,
             g_ffn_post=g_ffn_post, w_in_even=w_in_even, w_pool=w_pool, pool_scale=pool_scale, attn_sink=attn_sink,
             w_out_even=w_out_even, w_in_odd=w_in_odd, sgu_ln_g=sgu_ln_g, sgu_ln_b=sgu_ln_b, sgu_w=sgu_w, sgu_b=sgu_b,
             w_out_odd=w_out_odd, w_ffn_up=w_ffn_up, ffn_conv_w=ffn_conv_w, ffn_conv_b=ffn_conv_b, w_ffn_down=w_ffn_down)
    M = dict(c_ctx=m_c_ctx, w_ada=m_w_ada, b_ada=m_b_ada, g_mix_pre=m_g_mix_pre, g_mix_post=m_g_mix_post, g_ffn_pre=m_g_ffn_pre,
             g_ffn_post=m_g_ffn_post, w_in_even=m_w_in_even, w_pool=m_w_pool, pool_scale=m_pool_scale, attn_sink=m_attn_sink,
             w_out_even=m_w_out_even, w_in_odd=m_w_in_odd, sgu_ln_g=m_sgu_ln_g, sgu_ln_b=m_sgu_ln_b, sgu_w=m_sgu_w, sgu_b=m_sgu_b,
             w_out_odd=m_w_out_odd, w_ffn_up=m_w_ffn_up, ffn_conv_w=m_ffn_conv_w, ffn_conv_b=m_ffn_conv_b, w_ffn_down=m_w_ffn_down)
    V = dict(c_ctx=v_c_ctx, w_ada=v_w_ada, b_ada=v_b_ada, g_mix_pre=v_g_mix_pre, g_mix_post=v_g_mix_post, g_ffn_pre=v_g_ffn_pre,
             g_ffn_post=v_g_ffn_post, w_in_even=v_w_in_even, w_pool=v_w_pool, pool_scale=v_pool_scale, attn_sink=v_attn_sink,
             w_out_even=v_w_out_even, w_in_odd=v_w_in_odd, sgu_ln_g=v_sgu_ln_g, sgu_ln_b=v_sgu_ln_b, sgu_w=v_sgu_w, sgu_b=v_sgu_b,
             w_out_odd=v_w_out_odd, w_ffn_up=v_w_ffn_up, ffn_conv_w=v_ffn_conv_w, ffn_conv_b=v_ffn_conv_b, w_ffn_down=v_w_ffn_down)

    x = x[0]
    ctx = ctx[0]
    target = loss_target[0]
    L, D = x.shape
    C = ctx.shape[0]
    tm = min(512, L)
    tm_up = min(1024, L)
    conv_rows = min(256, L)
    me = 4 * lax.axis_index("x") + 2 * lax.axis_index("y") + lax.axis_index("c")
    n_ada = w_ada.shape[2]
    F = w_ffn_down.shape[1] * N_DEV
    half_f = F // 2

    n_cw = ffn_conv_w.shape[2]
    small = jnp.concatenate([_rows128(c), _rows128(sgu_ln_g), _rows128(sgu_ln_b), _rows128(ffn_conv_w)], axis=0)
    small_all = all_gather_small(small, name="gather_small_inputs")
    c_all = small_all[:, :8].reshape(N_DEV, D)
    ln_g = small_all[:, 8].reshape(1, D)
    ln_b = small_all[:, 16].reshape(1, D)
    conv_w = small_all[:, 24:].reshape(N_DEV, -1)[:, :2 * 3 * n_cw].reshape(N_DEV, 2, 3, n_cw)
    conv_w = conv_w.transpose(1, 2, 0, 3).reshape(2, 3, 2 * F)

    cs = jnp.concatenate([c_all, c_ctx[None, :], jnp.zeros((7, D), F32)], axis=0)
    b_loc = lax.dynamic_slice(b_ada, (0, me * n_ada), (2, n_ada))
    silu_c, mods_loc = ada_fwd_mm(cs, w_ada, b_loc, name="ada_fwd")
    mods_all = all_gather_small(mods_loc.reshape(-1, LANES), name="gather_mods")

    shards = [s.astype(BF16) for s in (w_in_even[0].T, w_out_even[0], w_ffn_up[0].T, w_ffn_down[0],
                                       w_in_odd[0].T, w_out_odd[0], w_ffn_up[1].T, w_ffn_down[1])]
    shards, mods_all = lax.optimization_barrier((shards, mods_all))
    w_sems, w_srcs, w_lands, _ = exchange_start(shards, [_landing(s.shape[0], s) for s in shards], scatter=False, name="gather_start")

    def weight(j, after):
        return exchange_wait([w_srcs[j]], [w_lands[j]], w_sems[2 * j:2 * j + 2], after, scatter=False, name=f"gather_wait_{j}")[0]

    mods_all = mods_all.reshape(N_DEV, 2, 16, n_ada).transpose(1, 2, 0, 3).reshape(2, 16, 6 * D)
    mod = lambda i, row: [m_[None, :] for m_ in jnp.split(lax.dynamic_index_in_dim(mods_all[i], row, 0, False), 6)]
    sh_m, sc_m, gt_m, sh_f, sc_f, gt_f = zip(mod(0, me), mod(1, me))
    csh_m, csc_m = mod(0, N_DEV)[:2]

    row = lambda a, i: a[i][None, :]

    cos, sa, sb = _rope_tables(L)
    sink = attn_sink[0]
    bst = sgu_b[0].T
    wup, wdn = [None, None], [None, None]

    def ffn_fwd(i, xin):
        wup[i] = weight(2 + 4 * i, xin)
        h, hu = pre_mm(xin, row(g_ffn_pre, i), sh_f[i], sc_f[i], wup[i], tm=tm_up, tn=half_f, name=f"ffn_up_{i}")
        a = conv_fwd(hu, conv_w[i], ffn_conv_b[i][None, :], rows=conv_rows, wblk=2 * LANES, name=f"ffn_conv_{i}")
        wdn[i] = weight(3 + 4 * i, a)
        f, xo = mm_post(a, wdn[i], xin, row(g_ffn_post, i), gt_f[i], tm=tm, name=f"ffn_down_{i}")
        return h, hu, a, f, xo

    win_e = weight(0, sh_m[0])
    h0, u, q, kv = inproj_even(x, row(g_mix_pre, 0), sh_m[0], sc_m[0], win_e, cos, sa, sb, tm=tm, name="in_even")
    hc, kvc = pre_mm(ctx, row(g_mix_pre, 0), csh_m, csc_m, win_e, tm=C, tn=2 * LANES, w_row_off=8 * LANES, name="in_even_ctx")
    pa = jnp.concatenate([pool_fwd(u, w_pool[0], pool_scale, name="pool_fwd"),
                          attn_fwd(q, kv, kvc, sink, name="attn_fwd")], axis=1)
    wout_e = weight(1, pa)
    y0, x1 = mm_post(pa, wout_e, x, row(g_mix_post, 0), gt_m[0], tm=tm, name="out_even")
    h1, hu0, a0, f0, x2 = ffn_fwd(0, x1)
    win_o = weight(4, x2)
    h2, z1 = pre_mm(x2, row(g_mix_pre, 1), sh_m[1], sc_m[1], win_o, tm=tm_up, tn=D, name="in_odd")
    us = sgu_fwd(z1, ln_g, ln_b, sgu_w[0], bst, name="sgu_fwd")
    wout_o = weight(5, us)
    y1, x3 = mm_post(us, wout_o, x2, row(g_mix_post, 1), gt_m[1], tm=tm, name="out_odd")
    h3, hu1, a1, f1, x4 = ffn_fwd(1, x3)
    loss_part, dx4 = loss_grad(x4, target, tm=tm, name="loss")
    loss = lax.psum(loss_part[0, 0], ("x", "y", "c"))

    g_srcs, g_lands, g_sems = [], [], []

    def scatter(grads, nm):
        sems, srcs, lands, tok = exchange_start(grads, [_landing(g.shape[0] // N_DEV, g) for g in grads], scatter=True, name=nm)
        g_srcs.extend(srcs)
        g_lands.extend(lands)
        g_sems.extend(sems)
        return tok[0:1, 0:1]

    def ffn_bwd(i, dxo, xin, h, hu, a, f, g_post):
        dyf, da, dg_post, dgt = post_bwd_mm(dxo, f, g_post, gt_f[i], wdn[i], tm=tm, name=f"ffn_down_bwd_{i}")
        dhg, dhu, dcwg, dcwu, dcbg, dcbu = conv_bwd(da, hu, conv_w[i], ffn_conv_b[i][None, :], rows=conv_rows, wblk=2 * LANES,
                                                    name=f"ffn_conv_bwd_{i}")
        dxin, dg_pre, dsh, dsc = mm_pre_bwd([dhg, dhu], wup[i], xin, dxo, row(g_ffn_pre, i), sc_f[i], tm=tm, tk=half_f,
                                            name=f"ffn_up_bwd_{i}")
        g_dn = wgrad([a], dyf, tr=2 * LANES, name=f"wgrad_down_{i}")
        g_up = wgrad([dhg, dhu], h, tr=2 * LANES, name=f"wgrad_up_{i}")
        tok = scatter([g_dn, g_up], f"scatter_start_ffn_{i}")
        return dxin, tok, dict(g_ffn_post=dg_post, g_ffn_pre=dg_pre, gt_f=dgt, sh_f=dsh, sc_f=dsc,
                               ffn_conv_w=jnp.concatenate([dcwg, dcwu], axis=1), ffn_conv_b=jnp.concatenate([dcbg, dcbu], axis=1)[0])

    dx3, tok, sf1 = ffn_bwd(1, dx4, x3, h3, hu1, a1, f1, row(g_ffn_post, 1))
    dy1, dus, dg_mpost1, dgt_m1 = post_bwd_mm(dx3, y1, row(g_mix_post, 1) + tok, gt_m[1], wout_o, tm=tm, name="out_odd_bwd")
    dz1, dws, dbs, dlng, dlnb = sgu_bwd(z1, dus, ln_g, ln_b, sgu_w[0], bst, name="sgu_bwd")
    dx2, dg_mpre1, dsh_m1, dsc_m1 = mm_pre_bwd([dz1], win_o, x2, dx3, row(g_mix_pre, 1), sc_m[1], tm=tm, tk=D, name="in_odd_bwd")
    tok = scatter([wgrad([us], dy1, tr=2 * LANES, name="wgrad_out_odd"), wgrad([dz1], h2, tr=2 * LANES, name="wgrad_in_odd")],
                  "scatter_start_mix_1")

    dx1, tok, sf0 = ffn_bwd(0, dx2, x1, h1, hu0, a0, f0, row(g_ffn_post, 0) + tok)
    dy0, dpa, dg_mpost0, dgt_m0 = post_bwd_mm(dx1, y0, row(g_mix_post, 0) + tok, gt_m[0], wout_e, tm=tm, name="out_even_bwd")
    du, dwp, dps = pool_bwd(u, dpa, w_pool[0], pool_scale, name="pool_bwd")
    dq, dkv, dkvc, dsink = attn_bwd(q, kv, kvc, sink, dpa, cos, sa, sb, name="attn_bwd")
    dz0 = jnp.concatenate([du, dq, dkv], axis=1)
    dzc = jnp.concatenate([jnp.zeros((C, 8 * LANES), BF16), dkvc], axis=1)
    tok = scatter([wgrad([pa], dy0, tr=2 * LANES, name="wgrad_out_even"),
                   wgrad([dz0], h0, tr=2 * LANES, extra=(dzc, hc), name="wgrad_in_even")], "scatter_start_mix_0")
    grad_x, dg_mpre0, dsh_m0, dsc_m0 = mm_pre_bwd([dz0], win_e, x, dx1, row(g_mix_pre, 0) + tok, sc_m[0], tm=tm, tk=dz0.shape[1],
                                                  name="in_even_bwd")
    _, dg_mpre0c, dcsh, dcsc = mm_pre_bwd([dkvc], win_e, ctx, None, row(g_mix_pre, 0), csc_m, tm=C, tk=2 * LANES,
                                          w_row_off=8 * LANES, name="in_even_ctx_bwd")

    slots = exchange_wait(g_srcs, g_lands, g_sems, dcsh, scatter=True, name="scatter_wait")
    out = {}

    def update(name, idx, land, transposed):
        w_, m_, v_ = (a[idx].T if transposed else a[idx] for a in (P[name], M[name], V[name]))
        r = w_.shape[0]
        tr = r // 4 if r % 64 == 0 and r > 256 else r
        res = adamw(w_, m_, v_, land.reshape(N_DEV, r, land.shape[1]), tr=tr, name=f"adamw_{name}_{idx}")
        for kind, val in zip(("grad", "delta", "new_m", "new_v"), res):
            out.setdefault((kind, name), []).append(val.T if transposed else val)

    update("w_in_even", 0, slots[7], True)
    update("w_out_even", 0, slots[6], False)
    update("w_in_odd", 0, slots[3], True)
    update("w_out_odd", 0, slots[2], False)
    update("w_ffn_up", 0, slots[5], True)
    update("w_ffn_down", 0, slots[4], False)
    update("w_ffn_up", 1, slots[1], True)
    update("w_ffn_down", 1, slots[0], False)

    zero = jnp.zeros((1, D), F32)
    dmod0 = jnp.concatenate([dsh_m0, dsc_m0, dgt_m0, sf0["sh_f"], sf0["sc_f"], sf0["gt_f"]], axis=1)
    dmodc = jnp.concatenate([dcsh, dcsc, zero, zero, zero, zero], axis=1)
    dmod1 = jnp.concatenate([dsh_m1, dsc_m1, dgt_m1, sf1["sh_f"], sf1["sc_f"], sf1["gt_f"]], axis=1)
    dmods = jnp.concatenate([dmod0, dmodc, dmod1], axis=0)
    dmods_all = all_gather_small(dmods.reshape(-1, LANES), name="gather_dmods").reshape(N_DEV, 3, N_DEV, n_ada)
    dall = lax.dynamic_index_in_dim(dmods_all, me, 2, False)
    g_w_ada, dcc = ada_bwd_mm(silu_c, c_ctx[None, :], dall, w_ada, name="ada_bwd")
    nl = w_ada.shape[0]
    res = adamw(w_ada.reshape(nl * D, n_ada), m_w_ada.reshape(nl * D, n_ada), v_w_ada.reshape(nl * D, n_ada),
                g_w_ada.reshape(1, nl * D, n_ada), tr=nl * D // 8, name="adamw_w_ada")
    for kind, val in zip(("grad", "delta", "new_m", "new_v"), res):
        out[(kind, "w_ada")] = val.reshape(nl, D, n_ada)

    rep = dict(
        c_ctx=dcc[0],
        b_ada=jnp.stack([dmod0[0] + dmodc[0], dmod1[0]]),
        g_mix_pre=jnp.concatenate([dg_mpre0 + dg_mpre0c, dg_mpre1]),
        g_mix_post=jnp.concatenate([dg_mpost0, dg_mpost1]),
        g_ffn_pre=jnp.concatenate([sf0["g_ffn_pre"], sf1["g_ffn_pre"]]),
        g_ffn_post=jnp.concatenate([sf0["g_ffn_post"], sf1["g_ffn_post"]]),
        w_pool=dwp[None], pool_scale=dps, attn_sink=dsink[:, :N_Q_HEADS],
        sgu_w=dws[None], sgu_b=dbs[:, :sgu_b.shape[1]].T[None],
        ffn_conv_b=jnp.stack([sf0["ffn_conv_b"], sf1["ffn_conv_b"]]),
    )
    rep_names = list(rep)
    conv_g = jnp.stack([sf0["ffn_conv_w"], sf1["ffn_conv_w"]]).reshape(2, 3, N_DEV, n_cw).transpose(2, 0, 1, 3)
    shard_full = dict(sgu_ln_g=dlng.reshape(N_DEV, LANES), sgu_ln_b=dlnb.reshape(N_DEV, LANES),
                      ffn_conv_w=jnp.concatenate([_rows128(conv_g[d]) for d in range(N_DEV)], axis=0))
    pieces = [_rows128(rep[k]) for k in rep_names] + [shard_full[k] for k in shard_full]
    sizes = [p.shape[0] for p in pieces]
    _, gsum = all_gather_small(jnp.concatenate(pieces, axis=0), reduce=True, name="allreduce_small_grads")
    offs = [sum(sizes[:i]) for i in range(len(sizes))]
    n_rep = len(rep_names)
    cw_rows = sizes[-1] // N_DEV
    g_own = [gsum[offs[i]:offs[i] + sizes[i]] for i in range(n_rep)]
    g_own.append(_rows128(lax.dynamic_slice_in_dim(gsum, offs[n_rep] + me, 1, 0)))
    g_own.append(_rows128(lax.dynamic_slice_in_dim(gsum, offs[n_rep + 1] + me, 1, 0)))
    g_own.append(lax.dynamic_slice_in_dim(gsum, offs[n_rep + 2] + me * cw_rows, cw_rows, 0))
    small_names = rep_names + list(shard_full)
    packs = [jnp.concatenate([_rows128(src[k]) for k in small_names], axis=0) for src in (P, M, V)]
    n_pack = packs[0].shape[0]
    gp = jnp.concatenate(g_own, axis=0)[None]
    res = adamw(*packs, gp, tr=n_pack, name="adamw_small")
    o = 0
    for k in small_names:
        n = _rows128(P[k]).shape[0]
        for kind, val in zip(("grad", "delta", "new_m", "new_v"), res):
            out[(kind, k)] = val[o:o + n].reshape(-1)[:P[k].size].reshape(P[k].shape)
        o += n
    assert o == n_pack

    names = list(P)
    final = [loss, grad_x[None]]
    for kind in ("grad", "delta", "new_m", "new_v"):
        for k in names:
            val = out[(kind, k)]
            final.append(jnp.stack(val) if isinstance(val, list) else val)
    return tuple(final)
```

```python
import functools
import math

import jax
import jax.numpy as jnp
from jax import lax
from jax.experimental import pallas as pl
from jax.experimental.pallas import tpu as pltpu

F32 = jnp.float32
BF16 = jnp.bfloat16
MESH = pl.DeviceIdType.MESH
N_DEV = 8
LANES = 128
VMEM_LIMIT = 48 * 1024 * 1024
EPS = 1e-6
NEG_INF = -1e30
GRID_W = 64
WINDOW = 128
BLK = 128
HEAD_DIM = 64
N_Q_HEADS = 8
N_KV_HEADS = 2
GQA = N_Q_HEADS // N_KV_HEADS
POOL_WINDOWS = (2, 4, 8, 16)
ROPE_BASE = 10000.0
ROPE_FREQS = HEAD_DIM // 4
PAD = 16
ADAM_LR, ADAM_B1, ADAM_B2, ADAM_EPS, ADAM_WD, ADAM_STEP = 0.001, 0.9, 0.999, 1e-08, 0.01, 10
BC1 = 1.0 - ADAM_B1 ** ADAM_STEP
BC2 = 1.0 - ADAM_B2 ** ADAM_STEP
SQRT_2_OVER_PI = math.sqrt(2.0 / math.pi)
GELU_C = 0.044715


def _cp(sem=None):
    return pltpu.CompilerParams(dimension_semantics=sem, vmem_limit_bytes=VMEM_LIMIT)


def _dot(a, b):
    return jnp.dot(a, b, preferred_element_type=F32)


def _dot_nt(a, b):
    return lax.dot_general(a, b, (((1,), (1,)), ((), ())), preferred_element_type=F32)


def _dot_tn(a, b):
    return lax.dot_general(a, b, (((0,), (0,)), ((), ())), preferred_element_type=F32)


def _rms(x):
    r = lax.rsqrt(jnp.mean(x * x, axis=-1, keepdims=True) + EPS)
    return x * r, r


def _rms_bwd(dn, n, r):
    return r * (dn - n * jnp.mean(dn * n, axis=-1, keepdims=True))


def _colsum(a):
    return jnp.sum(a, axis=0, keepdims=True)


def _rope(x, c, sa, sb):
    return x * c + pltpu.roll(x, LANES - ROPE_FREQS, 1) * sa + pltpu.roll(x, ROPE_FREQS, 1) * sb


def _full(shape):
    return pl.BlockSpec(shape, lambda *_: (0,) * len(shape))


def pre_mm(x, g, sh, sc, wt, *, tm, tn, w_row_off=0, name):
    T, D = x.shape
    n_rows = wt.shape[0] - w_row_off
    off = w_row_off // tn

    def body(x_ref, g_ref, sh_ref, sc_ref, w_ref, h_ref, z_ref):
        @pl.when(pl.program_id(1) == 0)
        def _():
            n, _ = _rms(x_ref[...])
            h_ref[...] = (n * g_ref[...] * (1.0 + sc_ref[...]) + sh_ref[...]).astype(BF16)

        z_ref[...] = _dot_nt(h_ref[...], w_ref[...]).astype(BF16)

    vec = pl.BlockSpec((1, D), lambda i, j: (0, 0))
    return pl.pallas_call(
        body, name=name, grid=(T // tm, n_rows // tn),
        in_specs=[pl.BlockSpec((tm, D), lambda i, j: (i, 0)), vec, vec, vec, pl.BlockSpec((tn, D), lambda i, j: (j + off, 0))],
        out_specs=[pl.BlockSpec((tm, D), lambda i, j: (i, 0)), pl.BlockSpec((tm, tn), lambda i, j: (i, j))],
        out_shape=[jax.ShapeDtypeStruct((T, D), BF16), jax.ShapeDtypeStruct((T, n_rows), BF16)],
        compiler_params=_cp(("parallel", "arbitrary")),
    )(x, g, sh, sc, wt)


def inproj_even(x, g, sh, sc, wt, cos, sa, sb, *, tm, name):
    T, D = x.shape
    N = wt.shape[0]

    def body(x_ref, g_ref, sh_ref, sc_ref, w_ref, c_ref, sa_ref, sb_ref, h_ref, u_ref, q_ref, kv_ref):
        n, _ = _rms(x_ref[...])
        h = (n * g_ref[...] * (1.0 + sc_ref[...]) + sh_ref[...]).astype(BF16)
        h_ref[...] = h
        z = _dot_nt(h, w_ref[...])
        u_ref[...] = z[:, :4 * LANES]
        c, a, b = c_ref[...], sa_ref[...], sb_ref[...]
        for s in range(4):
            q_ref[:, s * LANES:(s + 1) * LANES] = _rope(z[:, (4 + s) * LANES:(5 + s) * LANES], c, a, b).astype(BF16)
        kv_ref[:, :LANES] = _rope(z[:, 8 * LANES:9 * LANES], c, a, b).astype(BF16)
        kv_ref[:, LANES:] = z[:, 9 * LANES:].astype(BF16)

    vec = pl.BlockSpec((1, D), lambda i: (0, 0))
    row = lambda w: pl.BlockSpec((tm, w), lambda i: (i, 0))
    return pl.pallas_call(
        body, name=name, grid=(T // tm,),
        in_specs=[row(D), vec, vec, vec, _full((N, D)), row(LANES), row(LANES), row(LANES)],
        out_specs=[row(D), row(4 * LANES), row(4 * LANES), row(2 * LANES)],
        out_shape=[jax.ShapeDtypeStruct((T, D), BF16), jax.ShapeDtypeStruct((T, 4 * LANES), F32),
                   jax.ShapeDtypeStruct((T, 4 * LANES), BF16), jax.ShapeDtypeStruct((T, 2 * LANES), BF16)],
        compiler_params=_cp(("parallel",)),
    )(x, g, sh, sc, wt, cos, sa, sb)


def mm_post(a, w, x, g, gt, *, tm, name):
    T, K = a.shape
    D = w.shape[1]

    def body(a_ref, w_ref, x_ref, g_ref, gt_ref, y_ref, xn_ref):
        y = _dot(a_ref[...], w_ref[...])
        n, _ = _rms(y)
        y_ref[...] = y
        xn_ref[...] = x_ref[...] + gt_ref[...] * (n * g_ref[...])

    vec = pl.BlockSpec((1, D), lambda i: (0, 0))
    row = lambda w_: pl.BlockSpec((tm, w_), lambda i: (i, 0))
    return pl.pallas_call(
        body, name=name, grid=(T // tm,),
        in_specs=[row(K), _full((K, D)), row(D), vec, vec],
        out_specs=[row(D), row(D)],
        out_shape=[jax.ShapeDtypeStruct((T, D), F32), jax.ShapeDtypeStruct((T, D), F32)],
        compiler_params=_cp(("parallel",)),
    )(a, w, x, g, gt)


def post_bwd_mm(dxn, y, g, gt, w, *, tm, name):
    T, D = y.shape
    K = w.shape[0]

    def body(dxn_ref, y_ref, g_ref, gt_ref, w_ref, dy_ref, da_ref, dg_ref, dgt_ref):
        @pl.when(pl.program_id(0) == 0)
        def _():
            dg_ref[...] = jnp.zeros_like(dg_ref)
            dgt_ref[...] = jnp.zeros_like(dgt_ref)

        d = dxn_ref[...]
        n, r = _rms(y_ref[...])
        g_, gt_ = g_ref[...], gt_ref[...]
        dg_ref[...] += _colsum(d * gt_ * n)
        dgt_ref[...] += _colsum(d * g_ * n)
        dy = _rms_bwd(d * (gt_ * g_), n, r).astype(BF16)
        dy_ref[...] = dy
        da_ref[...] = _dot_nt(dy, w_ref[...]).astype(BF16)

    vec = pl.BlockSpec((1, D), lambda i: (0, 0))
    row = lambda w_: pl.BlockSpec((tm, w_), lambda i: (i, 0))
    return pl.pallas_call(
        body, name=name, grid=(T // tm,),
        in_specs=[row(D), row(D), vec, vec, _full((K, D))],
        out_specs=[row(D), row(K), vec, vec],
        out_shape=[jax.ShapeDtypeStruct((T, D), BF16), jax.ShapeDtypeStruct((T, K), BF16),
                   jax.ShapeDtypeStruct((1, D), F32), jax.ShapeDtypeStruct((1, D), F32)],
        compiler_params=_cp(("arbitrary",)),
    )(dxn, y, g, gt, w)


def mm_pre_bwd(dzs, wt, x, dres, g, sc, *, tm, tk, w_row_off=0, name):
    T, N = dzs[0].shape
    D = x.shape[1]
    nk = N // tk
    npart = len(dzs)
    off = w_row_off // tk
    has_res = dres is not None

    def body(*refs):
        dz_refs = refs[:npart]
        w_refs = refs[npart:2 * npart]
        rest = refs[2 * npart:]
        x_ref = rest[0]
        dres_ref = rest[1] if has_res else None
        g_ref, sc_ref, dx_ref, dg_ref, dsh_ref, dsc_ref, acc = rest[1 + has_res:]
        i, k = pl.program_id(0), pl.program_id(1)

        @pl.when(jnp.logical_and(i == 0, k == 0))
        def _():
            dg_ref[...] = jnp.zeros_like(dg_ref)
            dsh_ref[...] = jnp.zeros_like(dsh_ref)
            dsc_ref[...] = jnp.zeros_like(dsc_ref)

        part = _dot(dz_refs[0][...], w_refs[0][...])
        for p in range(1, npart):
            part = part + _dot(dz_refs[p][...], w_refs[p][...])

        @pl.when(k == 0)
        def _():
            acc[...] = part

        @pl.when(k > 0)
        def _():
            acc[...] += part

        @pl.when(k == nk - 1)
        def _():
            dh = acc[...]
            n, r = _rms(x_ref[...])
            g_, s1 = g_ref[...], 1.0 + sc_ref[...]
            dsh_ref[...] += _colsum(dh)
            dsc_ref[...] += _colsum(dh * n * g_)
            dg_ref[...] += _colsum(dh * s1 * n)
            dxp = _rms_bwd(dh * (g_ * s1), n, r)
            dx_ref[...] = dxp + dres_ref[...] if has_res else dxp

    vec = pl.BlockSpec((1, D), lambda i, k: (0, 0))
    row = pl.BlockSpec((tm, D), lambda i, k: (i, 0))
    w_specs = [pl.BlockSpec((tk, D), (lambda i, k, p=p: (k + off + p * nk, 0))) for p in range(npart)]
    res_specs, res_args = ([row], (dres,)) if has_res else ([], ())
    return pl.pallas_call(
        body, name=name, grid=(T // tm, nk),
        in_specs=[pl.BlockSpec((tm, tk), lambda i, k: (i, k))] * npart + w_specs + [row] + res_specs + [vec, vec],
        out_specs=[row, vec, vec, vec],
        out_shape=[jax.ShapeDtypeStruct((T, D), F32)] + [jax.ShapeDtypeStruct((1, D), F32)] * 3,
        scratch_shapes=[pltpu.VMEM((tm, D), F32)],
        compiler_params=_cp(("arbitrary", "arbitrary")),
    )(*dzs, *([wt] * npart), x, *res_args, g, sc)


def wgrad(a_parts, b, *, tr, extra=None, name):
    T, R = a_parts[0].shape
    D = b.shape[1]
    npart = len(a_parts)
    nr = R // tr

    def body(*refs):
        a_refs, b_ref = refs[:npart], refs[npart]
        g_ref = refs[-1]
        for p in range(npart):
            @pl.when(pl.program_id(0) // nr == p)
            def _():
                acc = _dot_tn(a_refs[p][...], b_ref[...])
                if extra is not None:
                    acc += _dot_tn(refs[npart + 1][...], refs[npart + 2][...])
                g_ref[...] = acc.astype(BF16)

    in_specs = [pl.BlockSpec((T, tr), (lambda r, p=p: (0, jnp.clip(r - p * nr, 0, nr - 1)))) for p in range(npart)]
    in_specs.append(_full((T, D)))
    args = [*a_parts, b]
    if extra is not None:
        a2, b2 = extra
        in_specs += [pl.BlockSpec((a2.shape[0], tr), lambda r: (0, r)), _full(b2.shape)]
        args += [a2, b2]
    return pl.pallas_call(
        body, name=name, grid=(npart * nr,),
        in_specs=in_specs, out_specs=pl.BlockSpec((tr, D), lambda r: (r, 0)),
        out_shape=jax.ShapeDtypeStruct((npart * R, D), BF16),
        compiler_params=_cp(("parallel",)),
    )(*args)


def _conv_ext(ref, r0, rows, total):
    top = ref[pl.ds(pl.multiple_of(jnp.maximum(r0 - PAD, 0), PAD), PAD), :]
    mid = ref[pl.ds(r0, rows), :]
    bot = ref[pl.ds(pl.multiple_of(jnp.minimum(r0 + rows, total - PAD), PAD), PAD), :]
    top = jnp.where(r0 > 0, top, jnp.zeros_like(top))
    bot = jnp.where(r0 + rows < total, bot, jnp.zeros_like(bot))
    return jnp.concatenate([top, mid, bot], axis=0).astype(F32)


def _shift_rows(a, k):
    return pltpu.roll(a, k % a.shape[0], 0)


def _conv3(x, w, b):
    return w[0:1] * _shift_rows(x, 1) + w[1:2] * x + w[2:3] * _shift_rows(x, -1) + b


def _gate_up_specs(rows_, wblk, nb):
    return [pl.BlockSpec((rows_, wblk), lambda j: (0, j)), pl.BlockSpec((rows_, wblk), lambda j: (0, j + nb))]


def conv_fwd(hu, cw, cb, *, rows, wblk, name):
    L, N2 = hu.shape
    nb = N2 // 2 // wblk
    nchunk = L // rows

    def body(hg_ref, hu_ref, wg_ref, wu_ref, bg_ref, bu_ref, a_ref, s1_ref, s2_ref):
        def chunk(ci, carry):
            r0 = pl.multiple_of(ci * rows, rows)
            gate = _conv3(_conv_ext(hg_ref, r0, rows, L), wg_ref[...], bg_ref[...])[PAD:PAD + rows]
            up = _conv3(_conv_ext(hu_ref, r0, rows, L), wu_ref[...], bu_ref[...])[PAD:PAD + rows]
            sg = jax.nn.sigmoid(gate)
            silu = gate * sg
            at = pl.ds(r0, rows)
            a_ref[at, :] = (silu * up).astype(BF16)
            s1_ref[at, :] = silu.astype(BF16)
            s2_ref[at, :] = (up * (sg + silu * (1.0 - sg))).astype(BF16)
            return carry

        lax.fori_loop(0, nchunk, chunk, 0)

    out = pl.BlockSpec((L, wblk), lambda j: (0, j))
    return pl.pallas_call(
        body, name=name, grid=(nb,),
        in_specs=_gate_up_specs(L, wblk, nb) + _gate_up_specs(3, wblk, nb) + _gate_up_specs(1, wblk, nb),
        out_specs=[out] * 3, out_shape=[jax.ShapeDtypeStruct((L, N2 // 2), BF16)] * 3,
        compiler_params=_cp(("parallel",)),
    )(hu, hu, cw, cw, cb, cb)


def conv_bwd(da, s1, s2, hu, cw, *, rows, wblk, name):
    L, N2 = hu.shape
    F = N2 // 2
    nb = F // wblk
    nchunk = L // rows
    mid = slice(PAD, PAD + rows)

    def body(da_ref, s1_ref, s2_ref, hg_ref, hu_ref, wg_ref, wu_ref, dg_ref, du_ref, dwg_ref, dwu_ref, dbg_ref, dbu_ref):
        for ref in (dwg_ref, dwu_ref, dbg_ref, dbu_ref):
            ref[...] = jnp.zeros_like(ref)

        def half_bwd(x_ref, dh, w_ref, dx_ref, dw_ref, db_ref, r0):
            w = w_ref[...]
            nxt, prv = _shift_rows(dh, -1)[mid], _shift_rows(dh, 1)[mid]
            dhm, xm = dh[mid], x_ref[pl.ds(r0, rows), :].astype(F32)
            dx_ref[pl.ds(r0, rows), :] = (w[0:1] * nxt + w[1:2] * dhm + w[2:3] * prv).astype(BF16)
            db_ref[...] += _colsum(dhm)
            dw_ref[0:1, :] += _colsum(nxt * xm)
            dw_ref[1:2, :] += _colsum(dhm * xm)
            dw_ref[2:3, :] += _colsum(prv * xm)

        def chunk(ci, carry):
            r0 = pl.multiple_of(ci * rows, rows)
            d = _conv_ext(da_ref, r0, rows, L)
            half_bwd(hu_ref, d * _conv_ext(s1_ref, r0, rows, L), wu_ref, du_ref, dwu_ref, dbu_ref, r0)
            half_bwd(hg_ref, d * _conv_ext(s2_ref, r0, rows, L), wg_ref, dg_ref, dwg_ref, dbg_ref, r0)
            return carry

        lax.fori_loop(0, nchunk, chunk, 0)

    blk = lambda r: pl.BlockSpec((r, wblk), lambda j: (0, j))
    return pl.pallas_call(
        body, name=name, grid=(nb,),
        in_specs=[blk(L)] * 3 + _gate_up_specs(L, wblk, nb) + _gate_up_specs(3, wblk, nb),
        out_specs=[blk(L), blk(L), blk(3), blk(3), blk(1), blk(1)],
        out_shape=[jax.ShapeDtypeStruct((L, F), BF16)] * 2 + [jax.ShapeDtypeStruct((3, F), F32)] * 2
        + [jax.ShapeDtypeStruct((1, F), F32)] * 2,
        compiler_params=_cp(("parallel",)),
    )(da, s1, s2, hu, hu, cw, cw)


def _window_sums(pad_ref, w, lead):
    a = pad_ref[...]
    k = 1
    while k < w:
        a = a + _shift_rows(a, -k)
        k *= 2
    return _shift_rows(a, lead) if lead else a


def _pool_counts(L, h):
    t = lax.broadcasted_iota(jnp.int32, (L, 1), 0)
    return (jnp.minimum(t + h, L) - jnp.maximum(t - h, 0)).astype(F32)


def _pooled(u_ref, pad_ref, L, w):
    h = w // 2
    pad_ref[pl.ds(PAD, L), :] = u_ref[...]
    win = _window_sums(pad_ref, w, h)[PAD:PAD + L]
    return win / _pool_counts(L, h) - u_ref[...]


def _zero_pad_edges(pad_ref, L):
    z = jnp.zeros((PAD, LANES), F32)
    pad_ref[pl.ds(0, PAD), :] = z
    pad_ref[pl.ds(PAD + L, PAD), :] = z


def pool_fwd(u, w_pool, pool_scale, *, name):
    L = u.shape[0]

    def body(u_ref, w_ref, ps_ref, p_ref, pad_ref):
        _zero_pad_edges(pad_ref, L)
        for gi, win in enumerate(POOL_WINDOWS):
            @pl.when(pl.program_id(0) == gi)
            def _():
                pooled = _pooled(u_ref, pad_ref, L, win)
                p_ref[...] = (_dot(pooled.astype(BF16), w_ref[...].astype(BF16)) * ps_ref[...]).astype(BF16)

    return pl.pallas_call(
        body, name=name, grid=(len(POOL_WINDOWS),),
        in_specs=[pl.BlockSpec((L, LANES), lambda gi: (0, gi)), pl.BlockSpec((None, LANES, LANES), lambda gi: (gi, 0, 0)),
                  pl.BlockSpec((1, LANES), lambda gi: (0, gi))],
        out_specs=pl.BlockSpec((L, LANES), lambda gi: (0, gi)),
        out_shape=jax.ShapeDtypeStruct((L, 4 * LANES), BF16),
        scratch_shapes=[pltpu.VMEM((L + 2 * PAD, LANES), F32)],
        compiler_params=_cp(("parallel",)),
    )(u, w_pool, pool_scale)


def pool_bwd(u, dpa, w_pool, pool_scale, *, name):
    L = u.shape[0]

    def body(u_ref, dp_ref, w_ref, ps_ref, du_ref, dw_ref, dps_ref, pad_ref):
        _zero_pad_edges(pad_ref, L)
        for gi, win in enumerate(POOL_WINDOWS):
            @pl.when(pl.program_id(0) == gi)
            def _():
                h = win // 2
                wb = w_ref[...].astype(BF16)
                pooled = _pooled(u_ref, pad_ref, L, win).astype(BF16)
                dp = dp_ref[...].astype(F32)
                dps_ref[...] = _colsum(dp * _dot(pooled, wb))
                dy = (dp * ps_ref[...]).astype(BF16)
                dw_ref[...] = _dot_tn(pooled, dy)
                dpooled = _dot_nt(dy, wb)
                pad_ref[pl.ds(PAD, L), :] = dpooled / _pool_counts(L, h)
                du_ref[...] = (_window_sums(pad_ref, win, h - 1)[PAD:PAD + L] - dpooled).astype(BF16)

    return pl.pallas_call(
        body, name=name, grid=(len(POOL_WINDOWS),),
        in_specs=[pl.BlockSpec((L, LANES), lambda gi: (0, gi)), pl.BlockSpec((L, LANES), lambda gi: (0, gi)),
                  pl.BlockSpec((None, LANES, LANES), lambda gi: (gi, 0, 0)), pl.BlockSpec((1, LANES), lambda gi: (0, gi))],
        out_specs=[pl.BlockSpec((L, LANES), lambda gi: (0, gi)), pl.BlockSpec((None, LANES, LANES), lambda gi: (gi, 0, 0)),
                   pl.BlockSpec((1, LANES), lambda gi: (0, gi))],
        out_shape=[jax.ShapeDtypeStruct((L, 4 * LANES), BF16), jax.ShapeDtypeStruct((4, LANES, LANES), F32),
                   jax.ShapeDtypeStruct((1, 4 * LANES), F32)],
        scratch_shapes=[pltpu.VMEM((L + 2 * PAD, LANES), F32)],
        compiler_params=_cp(("parallel",)),
    )(u, dpa, w_pool, pool_scale)


def _attn_probs(q4, band, kvc, sink_ref, kh, mask4):
    scale = HEAD_DIM ** -0.5
    ks = slice(kh * HEAD_DIM, (kh + 1) * HEAD_DIM)
    s_loc = jnp.where(mask4, _dot_nt(q4, band[:, ks]) * scale, NEG_INF)
    s_ctx = _dot_nt(q4, kvc[:, ks]) * scale
    sk = jnp.concatenate([jnp.full((BLK, 1), sink_ref[kh * GQA + hh], F32) for hh in range(GQA)], axis=0)
    m = jnp.maximum(jnp.maximum(jnp.max(s_loc, axis=-1, keepdims=True), jnp.max(s_ctx, axis=-1, keepdims=True)), sk)
    e_loc, e_ctx, e_s = jnp.exp(s_loc - m), jnp.exp(s_ctx - m), jnp.exp(sk - m)
    inv = 1.0 / (jnp.sum(e_loc, axis=-1, keepdims=True) + jnp.sum(e_ctx, axis=-1, keepdims=True) + e_s)
    return e_loc * inv, e_ctx * inv, e_s * inv


def _attn_block(n, L):
    start = pl.multiple_of(jnp.clip((n - 1) * BLK, 0, L - 3 * BLK), BLK)
    qpos = n * BLK + lax.broadcasted_iota(jnp.int32, (BLK, 3 * BLK), 0)
    kpos = start + lax.broadcasted_iota(jnp.int32, (BLK, 3 * BLK), 1)
    mask = jnp.abs(kpos - qpos) <= WINDOW
    return start, jnp.concatenate([mask] * GQA, axis=0)


def _stack_heads(ref, kh):
    return jnp.concatenate([ref[:, (kh * GQA + hh) * HEAD_DIM:(kh * GQA + hh + 1) * HEAD_DIM] for hh in range(GQA)], axis=0)


def attn_fwd(q, kv, kvc, sink, *, name):
    L = q.shape[0]
    C = kvc.shape[0]

    def body(q_ref, kv_ref, kvc_ref, sink_ref, o_ref, o_scr):
        start, mask4 = _attn_block(pl.program_id(0), L)
        band = kv_ref[pl.ds(start, 3 * BLK), :]
        kvc_ = kvc_ref[...]
        for kh in range(N_KV_HEADS):
            q4 = _stack_heads(q_ref, kh)
            p_loc, p_ctx, _ = _attn_probs(q4, band, kvc_, sink_ref, kh, mask4)
            vs = slice(2 * HEAD_DIM + kh * HEAD_DIM, 2 * HEAD_DIM + (kh + 1) * HEAD_DIM)
            o4 = _dot(p_loc.astype(BF16), band[:, vs]) + _dot(p_ctx.astype(BF16), kvc_[:, vs])
            for hh in range(GQA):
                h = kh * GQA + hh
                o_scr[:, h * HEAD_DIM:(h + 1) * HEAD_DIM] = o4[hh * BLK:(hh + 1) * BLK]
        o_ref[...] = o_scr[...].astype(BF16)

    return pl.pallas_call(
        body, name=name, grid=(L // BLK,),
        in_specs=[pl.BlockSpec((BLK, 4 * LANES), lambda n: (n, 0)), _full((L, 2 * LANES)), _full((C, 2 * LANES)),
                  pl.BlockSpec(memory_space=pltpu.SMEM)],
        out_specs=pl.BlockSpec((BLK, 4 * LANES), lambda n: (n, 0)),
        out_shape=jax.ShapeDtypeStruct((L, 4 * LANES), BF16),
        scratch_shapes=[pltpu.VMEM((BLK, 4 * LANES), F32)],
        compiler_params=_cp(("parallel",)),
    )(q, kv, kvc, sink)


def attn_bwd(q, kv, kvc, sink, dpa, cos, sa, sb, *, name):
    L = q.shape[0]
    C = kvc.shape[0]
    nb = L // BLK
    scale = HEAD_DIM ** -0.5

    def body(q_ref, kv_ref, kvc_ref, sink_ref, do_ref, c_ref, sa_ref, sb_ref, cq_ref, saq_ref, sbq_ref,
             dq_ref, dkv_ref, dkvc_ref, dsink_ref, dkv_acc, dkvc_acc, dq_scr, band_scr, ctx_scr):
        n = pl.program_id(0)

        @pl.when(n == 0)
        def _():
            dkv_acc[...] = jnp.zeros_like(dkv_acc)
            dkvc_acc[...] = jnp.zeros_like(dkvc_acc)
            dsink_ref[...] = jnp.zeros_like(dsink_ref)

        start, mask4 = _attn_block(n, L)
        band = kv_ref[pl.ds(start, 3 * BLK), :]
        kvc_ = kvc_ref[...]
        lane = lax.broadcasted_iota(jnp.int32, (1, LANES), 1)
        dsink = jnp.zeros((1, LANES), F32)
        for kh in range(N_KV_HEADS):
            ks = slice(kh * HEAD_DIM, (kh + 1) * HEAD_DIM)
            vs = slice(2 * HEAD_DIM + kh * HEAD_DIM, 2 * HEAD_DIM + (kh + 1) * HEAD_DIM)
            q4 = _stack_heads(q_ref, kh)
            do4 = _stack_heads(do_ref, kh)
            p_loc, p_ctx, p_s = _attn_probs(q4, band, kvc_, sink_ref, kh, mask4)
            dp_loc = _dot_nt(do4, band[:, vs])
            dp_ctx = _dot_nt(do4, kvc_[:, vs])
            delta = jnp.sum(p_loc * dp_loc, axis=-1, keepdims=True) + jnp.sum(p_ctx * dp_ctx, axis=-1, keepdims=True)
            ds_loc = (p_loc * (dp_loc - delta) * scale).astype(BF16)
            ds_ctx = (p_ctx * (dp_ctx - delta) * scale).astype(BF16)
            dsk = p_s * delta
            for hh in range(GQA):
                h = kh * GQA + hh
                dsink = dsink - jnp.where(lane == h, jnp.sum(dsk[hh * BLK:(hh + 1) * BLK], axis=0, keepdims=True), 0.0)
            dq4 = _dot(ds_loc, band[:, ks]) + _dot(ds_ctx, kvc_[:, ks])
            for hh in range(GQA):
                h = kh * GQA + hh
                dq_scr[:, h * HEAD_DIM:(h + 1) * HEAD_DIM] = dq4[hh * BLK:(hh + 1) * BLK]
            band_scr[:, ks] = _dot_tn(ds_loc, q4)
            band_scr[:, vs] = _dot_tn(p_loc.astype(BF16), do4)
            ctx_scr[:, ks] = _dot_tn(ds_ctx, q4)
            ctx_scr[:, vs] = _dot_tn(p_ctx.astype(BF16), do4)
        dsink_ref[...] += dsink
        dkv_acc[pl.ds(start, 3 * BLK), :] += band_scr[...]
        dkvc_acc[...] += ctx_scr[...]
        c, a, b = cq_ref[...], -saq_ref[...], -sbq_ref[...]
        for s in range(4):
            dq_ref[:, s * LANES:(s + 1) * LANES] = _rope(dq_scr[:, s * LANES:(s + 1) * LANES], c, a, b).astype(BF16)

        @pl.when(n == nb - 1)
        def _():
            dkv_ref[:, :LANES] = _rope(dkv_acc[:, :LANES], c_ref[...], -sa_ref[...], -sb_ref[...]).astype(BF16)
            dkv_ref[:, LANES:] = dkv_acc[:, LANES:].astype(BF16)
            dkvc_ref[...] = dkvc_acc[...].astype(BF16)

    blk = lambda w: pl.BlockSpec((BLK, w), lambda n: (n, 0))
    return pl.pallas_call(
        body, name=name, grid=(nb,),
        in_specs=[blk(4 * LANES), _full((L, 2 * LANES)), _full((C, 2 * LANES)), pl.BlockSpec(memory_space=pltpu.SMEM),
                  pl.BlockSpec((BLK, 4 * LANES), lambda n: (n, 1)),
                  _full((L, LANES)), _full((L, LANES)), _full((L, LANES)), blk(LANES), blk(LANES), blk(LANES)],
        out_specs=[blk(4 * LANES), _full((L, 2 * LANES)), _full((C, 2 * LANES)), _full((1, LANES))],
        out_shape=[jax.ShapeDtypeStruct((L, 4 * LANES), BF16), jax.ShapeDtypeStruct((L, 2 * LANES), BF16),
                   jax.ShapeDtypeStruct((C, 2 * LANES), BF16), jax.ShapeDtypeStruct((1, LANES), F32)],
        scratch_shapes=[pltpu.VMEM((L, 2 * LANES), F32), pltpu.VMEM((C, 2 * LANES), F32), pltpu.VMEM((BLK, 4 * LANES), F32),
                        pltpu.VMEM((3 * BLK, 2 * LANES), F32), pltpu.VMEM((C, 2 * LANES), F32)],
        compiler_params=_cp(("arbitrary",)),
    )(q, kv, kvc, sink, dpa, cos, sa, sb, cos, sa, sb)


def _gelu_parts(x):
    th = jnp.tanh(SQRT_2_OVER_PI * (x + GELU_C * x * x * x))
    return 0.5 * x * (1.0 + th), th


def _gelu_grad(x, th):
    return 0.5 * (1.0 + th) + 0.5 * x * (1.0 - th * th) * SQRT_2_OVER_PI * (1.0 + 3.0 * GELU_C * x * x)


def _layernorm(v):
    mu = jnp.mean(v, axis=-1, keepdims=True)
    vc = v - mu
    rstd = lax.rsqrt(jnp.mean(vc * vc, axis=-1, keepdims=True) + EPS)
    return vc * rstd, rstd


def sgu_fwd(z1, ln_g, ln_b, ws, bst, *, name):
    L, W2 = z1.shape
    W = W2 // 2
    ng = W // LANES

    def body(z_ref, g_ref, b_ref, ws_ref, bs_ref, o_ref):
        z, _ = _gelu_parts(z_ref[...].astype(F32))
        xhat, _ = _layernorm(z[:, W:])
        vln = (xhat * g_ref[...] + b_ref[...]).astype(BF16)
        for gi in range(ng):
            cs = slice(gi * LANES, (gi + 1) * LANES)
            s = _dot(ws_ref[gi].astype(BF16), vln[:, cs]) + bs_ref[:, gi:gi + 1]
            o_ref[:, cs] = (z[:, cs] * s).astype(BF16)

    vec = _full((1, W))
    return pl.pallas_call(
        body, name=name, grid=(L // BLK,),
        in_specs=[pl.BlockSpec((BLK, W2), lambda n: (n, 0)), vec, vec, _full((ng, LANES, LANES)), _full((BLK, ng))],
        out_specs=pl.BlockSpec((BLK, W), lambda n: (n, 0)),
        out_shape=jax.ShapeDtypeStruct((L, W), BF16),
        compiler_params=_cp(("parallel",)),
    )(z1, ln_g, ln_b, ws, bst)


def sgu_bwd(z1, dus, ln_g, ln_b, ws, bst, *, name):
    L, W2 = z1.shape
    W = W2 // 2
    ng = W // LANES

    def body(z_ref, d_ref, g_ref, b_ref, ws_ref, bs_ref, dz_ref, dws_ref, dbs_ref, dg_ref, db_ref, dv_scr):
        @pl.when(pl.program_id(0) == 0)
        def _():
            dws_ref[...] = jnp.zeros_like(dws_ref)
            dbs_ref[...] = jnp.zeros_like(dbs_ref)
            dg_ref[...] = jnp.zeros_like(dg_ref)
            db_ref[...] = jnp.zeros_like(db_ref)

        zp = z_ref[...].astype(F32)
        z, th = _gelu_parts(zp)
        xhat, rstd = _layernorm(z[:, W:])
        vln = (xhat * g_ref[...] + b_ref[...]).astype(BF16)
        d = d_ref[...].astype(F32)
        lane = lax.broadcasted_iota(jnp.int32, (1, LANES), 1)
        dbs = jnp.zeros((BLK, LANES), F32)
        dgel = _gelu_grad(zp, th)
        for gi in range(ng):
            cs = slice(gi * LANES, (gi + 1) * LANES)
            wb = ws_ref[gi].astype(BF16)
            s = _dot(wb, vln[:, cs]) + bs_ref[:, gi:gi + 1]
            dz_ref[:, cs] = (d[:, cs] * s * dgel[:, cs]).astype(BF16)
            ds = d[:, cs] * z[:, cs]
            dbs = dbs + jnp.where(lane == gi, jnp.sum(ds, axis=-1, keepdims=True), 0.0)
            dsb = ds.astype(BF16)
            dws_ref[gi] += _dot_nt(dsb, vln[:, cs])
            dv_scr[:, cs] = _dot_tn(wb, dsb)
        dbs_ref[...] += dbs
        dvln = dv_scr[...]
        dg_ref[...] += _colsum(dvln * xhat)
        db_ref[...] += _colsum(dvln)
        dxh = dvln * g_ref[...]
        dv = rstd * (dxh - jnp.mean(dxh, axis=-1, keepdims=True) - xhat * jnp.mean(dxh * xhat, axis=-1, keepdims=True))
        dz_ref[:, W:] = (dv * dgel[:, W:]).astype(BF16)

    vec = _full((1, W))
    return pl.pallas_call(
        body, name=name, grid=(L // BLK,),
        in_specs=[pl.BlockSpec((BLK, W2), lambda n: (n, 0)), pl.BlockSpec((BLK, W), lambda n: (n, 0)), vec, vec,
                  _full((ng, LANES, LANES)), _full((BLK, ng))],
        out_specs=[pl.BlockSpec((BLK, W2), lambda n: (n, 0)), _full((ng, LANES, LANES)), _full((BLK, LANES)), vec, vec],
        out_shape=[jax.ShapeDtypeStruct((L, W2), BF16), jax.ShapeDtypeStruct((ng, LANES, LANES), F32),
                   jax.ShapeDtypeStruct((BLK, LANES), F32), jax.ShapeDtypeStruct((1, W), F32), jax.ShapeDtypeStruct((1, W), F32)],
        scratch_shapes=[pltpu.VMEM((BLK, W), F32)],
        compiler_params=_cp(("arbitrary",)),
    )(z1, dus, ln_g, ln_b, ws, bst)


def loss_grad(xo, target, *, tm, name):
    T, D = xo.shape

    def body(x_ref, t_ref, l_ref, d_ref):
        @pl.when(pl.program_id(0) == 0)
        def _():
            l_ref[...] = jnp.zeros_like(l_ref)

        e = x_ref[...] - t_ref[...]
        l_ref[...] += 0.5 * jnp.sum(jnp.mean(e * e, axis=-1, keepdims=True), axis=0, keepdims=True)
        d_ref[...] = e * (1.0 / D)

    row = pl.BlockSpec((tm, D), lambda i: (i, 0))
    return pl.pallas_call(
        body, name=name, grid=(T // tm,), in_specs=[row, row], out_specs=[_full((1, 1)), row],
        out_shape=[jax.ShapeDtypeStruct((1, 1), F32), jax.ShapeDtypeStruct((T, D), F32)],
        compiler_params=_cp(("arbitrary",)),
    )(xo, target)


def adamw(w, m, v, gparts, *, tr, name):
    R, Wd = w.shape
    S = gparts.shape[0]

    def body(w_ref, m_ref, v_ref, gp_ref, g_ref, d_ref, nm_ref, nv_ref):
        g = gp_ref[0].astype(F32)
        for s in range(1, S):
            g = g + gp_ref[s].astype(F32)
        m_ = ADAM_B1 * m_ref[...] + (1.0 - ADAM_B1) * g
        v_ = ADAM_B2 * v_ref[...] + (1.0 - ADAM_B2) * (g * g)
        g_ref[...] = g
        nm_ref[...] = m_
        nv_ref[...] = v_
        d_ref[...] = -ADAM_LR * ((m_ / BC1) / (jnp.sqrt(v_ / BC2) + ADAM_EPS) + ADAM_WD * w_ref[...])

    row = pl.BlockSpec((tr, Wd), lambda i: (i, 0))
    return pl.pallas_call(
        body, name=name, grid=(R // tr,),
        in_specs=[row, row, row, pl.BlockSpec((S, tr, Wd), lambda i: (0, i, 0))],
        out_specs=[row] * 4, out_shape=[jax.ShapeDtypeStruct((R, Wd), F32)] * 4,
        compiler_params=_cp(("parallel",)),
    )(w, m, v, gparts)


def ada_fwd_mm(cs, w_ada, b_loc, *, name):
    R, D = cs.shape
    nl, _, n = w_ada.shape

    def body(c_ref, w_ref, b_ref, s_ref, m_ref):
        c = c_ref[...]
        s = c * jax.nn.sigmoid(c)
        s_ref[...] = s
        for i in range(nl):
            m_ref[i] = _dot(s.astype(BF16), w_ref[i].astype(BF16)) + b_ref[i:i + 1, :]

    return pl.pallas_call(
        body, name=name, in_specs=[_full((R, D)), _full((nl, D, n)), _full((nl, n))],
        out_specs=[_full((R, D)), _full((nl, R, n))], grid=(1,),
        out_shape=[jax.ShapeDtypeStruct((R, D), F32), jax.ShapeDtypeStruct((nl, R, n), F32)],
        compiler_params=_cp(("arbitrary",)),
    )(cs, w_ada, b_loc)


def ada_bwd_mm(s, c_ctx, dall, w_ada, *, name):
    R, D = s.shape
    nl, _, n = w_ada.shape

    def body(s_ref, cc_ref, d_ref, w_ref, gw_ref, dcc_ref):
        sb = s_ref[...].astype(BF16)
        row = lax.broadcasted_iota(jnp.int32, (R, 1), 0)
        dctx = d_ref[0, 1:2, :]
        for dv in range(1, N_DEV):
            dctx = dctx + d_ref[dv, 1:2, :]
        for i in range(nl):
            dm = jnp.zeros((R, n), F32)
            for dv in range(N_DEV):
                dm = dm + jnp.where(row == dv, d_ref[dv, 2 * i:2 * i + 1, :], 0.0)
            if i == 0:
                dm = dm + jnp.where(row == N_DEV, dctx, 0.0)
            gw_ref[i] = _dot_tn(sb, dm.astype(BF16))
        cc = cc_ref[...]
        sg = jax.nn.sigmoid(cc)
        ds = _dot_nt(jnp.broadcast_to(dctx, (8, n)).astype(BF16), w_ref[0].astype(BF16))
        dcc_ref[...] = ds * (sg * (1.0 + cc * (1.0 - sg)))

    return pl.pallas_call(
        body, name=name, grid=(1,),
        in_specs=[_full((R, D)), _full((1, D)), _full((N_DEV, 3, n)), _full((nl, D, n))],
        out_specs=[_full((nl, D, n)), _full((8, D))],
        out_shape=[jax.ShapeDtypeStruct((nl, D, n), F32), jax.ShapeDtypeStruct((8, D), F32)],
        compiler_params=_cp(("arbitrary",)),
    )(s, c_ctx, dall, w_ada)


def _place():
    x, y, c = lax.axis_index("x"), lax.axis_index("y"), lax.axis_index("c")
    return x, y, c


def _lin(p):
    return 4 * p[0] + 2 * p[1] + p[2]


def all_gather_small(xb, *, reduce=False, name):
    R, W = xb.shape

    def body(x_ref, *rest):
        out_ref = rest[0]
        send_sems, recv_sems, local_sem = rest[-3:]
        x, y, c = _place()
        me, sibling = (x, y, c), (x, y, 1 - c)
        chips = [(1 - x, y), (x, 1 - y), (1 - x, 1 - y)]

        def copy(k, block, to, src=None):
            dst = out_ref.at[_lin(block)]
            return pltpu.make_async_remote_copy(
                src_ref=dst if src is None else src, dst_ref=dst, send_sem=send_sems.at[k], recv_sem=recv_sems.at[k],
                device_id=to, device_id_type=MESH)

        mine = pltpu.make_async_copy(x_ref, out_ref.at[_lin(me)], local_sem)
        mine.start()
        first = [copy(0, me, sibling, src=x_ref)]
        first += [copy(1 + j, me, (*chip, c), src=x_ref) for j, chip in enumerate(chips)]
        for cp in first:
            cp.start()
        passed = [copy(4 + j, (*chip, c), sibling) for j, chip in enumerate(chips)]
        for j, chip in enumerate(chips):
            copy(1 + j, (*chip, c), me).wait_recv()
            passed[j].start()
        copy(0, sibling, me).wait_recv()
        for j, chip in enumerate(chips):
            copy(4 + j, (*chip, 1 - c), me).wait_recv()
        for cp in first + passed:
            cp.wait_send()
        mine.wait()
        if reduce:
            acc = out_ref[0]
            for dv in range(1, N_DEV):
                acc = acc + out_ref[dv]
            rest[1][...] = acc

    vm = pl.BlockSpec(memory_space=pltpu.VMEM)
    out_shape = [jax.ShapeDtypeStruct((N_DEV, R, W), xb.dtype)]
    if reduce:
        out_shape.append(jax.ShapeDtypeStruct((R, W), xb.dtype))
    res = pl.pallas_call(
        body, name=name, in_specs=[vm], out_specs=[vm] * len(out_shape), out_shape=out_shape,
        scratch_shapes=[pltpu.SemaphoreType.DMA((7,)), pltpu.SemaphoreType.DMA((7,)), pltpu.SemaphoreType.DMA],
        compiler_params=pltpu.CompilerParams(vmem_limit_bytes=VMEM_LIMIT),
    )(xb)
    return res if reduce else res[0]


HBM_SPEC = pl.BlockSpec(memory_space=pltpu.HBM)
SEM_SPEC = pl.BlockSpec(memory_space=pltpu.SEMAPHORE)
ORDERED_EFFECT = pltpu.SideEffectType.DATAFLOW_SIDE_EFFECTING


def _exchange_copies(srcs, lands, sems, scatter):
    x, y, c = _place()
    me = _lin((x, y, c))
    for j in range(len(srcs)):
        r = lands[j].shape[0] // N_DEV
        block = lambda d, j=j, r=r: pl.ds(pl.multiple_of(d * r, 16), r)
        for k in range(1, N_DEV):
            peer = (x ^ (k >> 2), y ^ ((k >> 1) & 1), c ^ (k & 1))
            src = srcs[j].at[block(_lin(peer)), :] if scatter else srcs[j]
            mk = lambda dst, j=j, k=k, peer=peer, src=src: pltpu.make_async_remote_copy(
                src_ref=src, dst_ref=dst, send_sem=sems[2 * j].at[k - 1], recv_sem=sems[2 * j + 1].at[k - 1],
                device_id=peer, device_id_type=MESH)
            yield mk(lands[j].at[block(me), :]), mk(lands[j].at[block(_lin(peer)), :])


def exchange_start(srcs, lands, *, scatter, name):
    nw = len(srcs)

    def body(*refs):
        for start, _ in _exchange_copies(refs[:nw], refs[nw:2 * nw], refs[2 * nw:4 * nw], scatter):
            start.start()
        refs[-1][...] = jnp.zeros_like(refs[-1])

    thru = [pltpu.HBM(a.shape, a.dtype) for a in (*srcs, *lands)]
    res = pl.pallas_call(
        body, name=name, in_specs=[HBM_SPEC] * (2 * nw),
        out_specs=[SEM_SPEC] * (2 * nw) + [HBM_SPEC] * (2 * nw) + [pl.BlockSpec(memory_space=pltpu.VMEM)],
        out_shape=[pltpu.SemaphoreType.DMA((N_DEV - 1,))] * (2 * nw) + thru + [jax.ShapeDtypeStruct((8, LANES), F32)],
        input_output_aliases={i: 2 * nw + i for i in range(2 * nw)},
        compiler_params=pltpu.CompilerParams(has_side_effects=ORDERED_EFFECT),
    )(*[pltpu.with_memory_space_constraint(a, pltpu.HBM) for a in (*srcs, *lands)])
    return res[:2 * nw], res[2 * nw:3 * nw], res[3 * nw:4 * nw], res[-1]


def exchange_wait(srcs, lands, sems, after, *, scatter, name):
    nw = len(srcs)

    def body(*refs):
        for _, arrive in _exchange_copies(refs[:nw], refs[nw:2 * nw], refs[2 * nw:4 * nw], scatter):
            arrive.wait_send()
            arrive.wait_recv()

    res = pl.pallas_call(
        body, name=name, in_specs=[HBM_SPEC] * (2 * nw) + [SEM_SPEC] * (2 * nw) + [pl.BlockSpec(memory_space=pl.ANY)],
        out_specs=[HBM_SPEC] * (2 * nw), out_shape=[pltpu.HBM(a.shape, a.dtype) for a in (*srcs, *lands)],
        input_output_aliases={i: i for i in range(2 * nw)},
        compiler_params=pltpu.CompilerParams(has_side_effects=ORDERED_EFFECT),
    )(*srcs, *lands, *sems, after)
    me = _lin(_place())
    done = []
    for src, land in zip(res[:nw], res[nw:]):
        r = land.shape[0] // N_DEV
        own = lax.dynamic_slice_in_dim(src, me * r, r, 0) if scatter else src
        done.append(lax.dynamic_update_slice(land, own, (me * r, 0)))
    return done


def _rope_tables(L):
    t = jnp.arange(L)
    inv = ROPE_BASE ** (-jnp.arange(ROPE_FREQS, dtype=F32) / ROPE_FREQS)
    ar = (t // GRID_W).astype(F32)[:, None] * inv
    ac = (t % GRID_W).astype(F32)[:, None] * inv
    z = jnp.zeros_like(ar)
    cos = jnp.concatenate([jnp.cos(ar), jnp.cos(ar), jnp.cos(ac), jnp.cos(ac)], axis=1)
    sa = jnp.concatenate([-jnp.sin(ar), z, -jnp.sin(ac), z], axis=1)
    sb = jnp.concatenate([z, jnp.sin(ar), z, jnp.sin(ac)], axis=1)
    return tuple(jnp.tile(a, (1, LANES // HEAD_DIM)) for a in (cos, sa, sb))


def _rows128(a):
    f = a.reshape(-1)
    n = -(-f.shape[0] // (8 * LANES)) * 8 * LANES
    return jnp.pad(f, (0, n - f.shape[0])).reshape(-1, LANES)


def _landing(rows, like):
    return lax.empty((N_DEV * rows, like.shape[1]), like.dtype)


def kernel(x, c, ctx, c_ctx, w_ada, b_ada, g_mix_pre, g_mix_post, g_ffn_pre, g_ffn_post, w_in_even, w_pool, pool_scale, attn_sink, w_out_even, w_in_odd, sgu_ln_g, sgu_ln_b, sgu_w, sgu_b, w_out_odd, w_ffn_up, ffn_conv_w, ffn_conv_b, w_ffn_down, loss_target, m_c_ctx, m_w_ada, m_b_ada, m_g_mix_pre, m_g_mix_post, m_g_ffn_pre, m_g_ffn_post, m_w_in_even, m_w_pool, m_pool_scale, m_attn_sink, m_w_out_even, m_w_in_odd, m_sgu_ln_g, m_sgu_ln_b, m_sgu_w, m_sgu_b, m_w_out_odd, m_w_ffn_up, m_ffn_conv_w, m_ffn_conv_b, m_w_ffn_down, v_c_ctx, v_w_ada, v_b_ada, v_g_mix_pre, v_g_mix_post, v_g_ffn_pre, v_g_ffn_post, v_w_in_even, v_w_pool, v_pool_scale, v_attn_sink, v_w_out_even, v_w_in_odd, v_sgu_ln_g, v_sgu_ln_b, v_sgu_w, v_sgu_b, v_w_out_odd, v_w_ffn_up, v_ffn_conv_w, v_ffn_conv_b, v_w_ffn_down):
    P = dict(c_ctx=c_ctx, w_ada=w_ada, b_ada=b_ada, g_mix_pre=g_mix_pre, g_mix_post=g_mix_post, g_ffn_pre=g_ffn_pre,
             g_ffn_post=g_ffn_post, w_in_even=w_in_even, w_pool=w_pool, pool_scale=pool_scale, attn_sink=attn_sink,
             w_out_even=w_out_even, w_in_odd=w_in_odd, sgu_ln_g=sgu_ln_g, sgu_ln_b=sgu_ln_b, sgu_w=sgu_w, sgu_b=sgu_b,
             w_out_odd=w_out_odd, w_ffn_up=w_ffn_up, ffn_conv_w=ffn_conv_w, ffn_conv_b=ffn_conv_b, w_ffn_down=w_ffn_down)
    M = dict(c_ctx=m_c_ctx, w_ada=m_w_ada, b_ada=m_b_ada, g_mix_pre=m_g_mix_pre, g_mix_post=m_g_mix_post, g_ffn_pre=m_g_ffn_pre,
             g_ffn_post=m_g_ffn_post, w_in_even=m_w_in_even, w_pool=m_w_pool, pool_scale=m_pool_scale, attn_sink=m_attn_sink,
             w_out_even=m_w_out_even, w_in_odd=m_w_in_odd, sgu_ln_g=m_sgu_ln_g, sgu_ln_b=m_sgu_ln_b, sgu_w=m_sgu_w, sgu_b=m_sgu_b,
             w_out_odd=m_w_out_odd, w_ffn_up=m_w_ffn_up, ffn_conv_w=m_ffn_conv_w, ffn_conv_b=m_ffn_conv_b, w_ffn_down=m_w_ffn_down)
    V = dict(c_ctx=v_c_ctx, w_ada=v_w_ada, b_ada=v_b_ada, g_mix_pre=v_g_mix_pre, g_mix_post=v_g_mix_post, g_ffn_pre=v_g_ffn_pre,
             g_ffn_post=v_g_ffn_post, w_in_even=v_w_in_even, w_pool=v_w_pool, pool_scale=v_pool_scale, attn_sink=v_attn_sink,
             w_out_even=v_w_out_even, w_in_odd=v_w_in_odd, sgu_ln_g=v_sgu_ln_g, sgu_ln_b=v_sgu_ln_b, sgu_w=v_sgu_w, sgu_b=v_sgu_b,
             w_out_odd=v_w_out_odd, w_ffn_up=v_w_ffn_up, ffn_conv_w=v_ffn_conv_w, ffn_conv_b=v_ffn_conv_b, w_ffn_down=v_w_ffn_down)

    x = x[0]
    ctx = ctx[0]
    target = loss_target[0]
    L, D = x.shape
    C = ctx.shape[0]
    tm = min(512, L)
    tm_up = min(1024, L)
    conv_rows = min(256, L)
    me = 4 * lax.axis_index("x") + 2 * lax.axis_index("y") + lax.axis_index("c")
    n_ada = w_ada.shape[2]
    F = w_ffn_down.shape[1] * N_DEV
    half_f = F // 2

    n_cw = ffn_conv_w.shape[2]
    small = jnp.concatenate([_rows128(c), _rows128(sgu_ln_g), _rows128(sgu_ln_b), _rows128(ffn_conv_w)], axis=0)
    small_all = all_gather_small(small, name="gather_small_inputs")
    c_all = small_all[:, :8].reshape(N_DEV, D)
    ln_g = small_all[:, 8].reshape(1, D)
    ln_b = small_all[:, 16].reshape(1, D)
    conv_w = small_all[:, 24:].reshape(N_DEV, -1)[:, :2 * 3 * n_cw].reshape(N_DEV, 2, 3, n_cw)
    conv_w = conv_w.transpose(1, 2, 0, 3).reshape(2, 3, 2 * F)

    cs = jnp.concatenate([c_all, c_ctx[None, :], jnp.zeros((7, D), F32)], axis=0)
    b_loc = lax.dynamic_slice(b_ada, (0, me * n_ada), (2, n_ada))
    silu_c, mods_loc = ada_fwd_mm(cs, w_ada, b_loc, name="ada_fwd")
    mods_all = all_gather_small(mods_loc.reshape(-1, LANES), name="gather_mods")

    shards = [s.astype(BF16) for s in (w_in_even[0].T, w_out_even[0], w_ffn_up[0].T, w_ffn_down[0],
                                       w_in_odd[0].T, w_out_odd[0], w_ffn_up[1].T, w_ffn_down[1])]
    shards, mods_all = lax.optimization_barrier((shards, mods_all))
    w_sems, w_srcs, w_lands, _ = exchange_start(shards, [_landing(s.shape[0], s) for s in shards], scatter=False, name="gather_start")

    def weight(j, after):
        return exchange_wait([w_srcs[j]], [w_lands[j]], w_sems[2 * j:2 * j + 2], after, scatter=False, name=f"gather_wait_{j}")[0]

    mods_all = mods_all.reshape(N_DEV, 2, 16, n_ada).transpose(1, 2, 0, 3).reshape(2, 16, 6 * D)
    mod = lambda i, row: [m_[None, :] for m_ in jnp.split(lax.dynamic_index_in_dim(mods_all[i], row, 0, False), 6)]
    sh_m, sc_m, gt_m, sh_f, sc_f, gt_f = zip(mod(0, me), mod(1, me))
    csh_m, csc_m = mod(0, N_DEV)[:2]

    row = lambda a, i: a[i][None, :]

    cos, sa, sb = _rope_tables(L)
    sink = attn_sink[0]
    bst = sgu_b[0].T
    wup, wdn = [None, None], [None, None]

    def ffn_fwd(i, xin):
        wup[i] = weight(2 + 4 * i, xin)
        h, hu = pre_mm(xin, row(g_ffn_pre, i), sh_f[i], sc_f[i], wup[i], tm=tm_up, tn=half_f, name=f"ffn_up_{i}")
        a, s1, s2 = conv_fwd(hu, conv_w[i], ffn_conv_b[i][None, :], rows=conv_rows, wblk=2 * LANES, name=f"ffn_conv_{i}")
        wdn[i] = weight(3 + 4 * i, a)
        f, xo = mm_post(a, wdn[i], xin, row(g_ffn_post, i), gt_f[i], tm=tm, name=f"ffn_down_{i}")
        return h, (hu, s1, s2), a, f, xo

    win_e = weight(0, sh_m[0])
    h0, u, q, kv = inproj_even(x, row(g_mix_pre, 0), sh_m[0], sc_m[0], win_e, cos, sa, sb, tm=tm, name="in_even")
    hc, kvc = pre_mm(ctx, row(g_mix_pre, 0), csh_m, csc_m, win_e, tm=C, tn=2 * LANES, w_row_off=8 * LANES, name="in_even_ctx")
    pa = jnp.concatenate([pool_fwd(u, w_pool[0], pool_scale, name="pool_fwd"),
                          attn_fwd(q, kv, kvc, sink, name="attn_fwd")], axis=1)
    wout_e = weight(1, pa)
    y0, x1 = mm_post(pa, wout_e, x, row(g_mix_post, 0), gt_m[0], tm=tm, name="out_even")
    h1, hu0, a0, f0, x2 = ffn_fwd(0, x1)
    win_o = weight(4, x2)
    h2, z1 = pre_mm(x2, row(g_mix_pre, 1), sh_m[1], sc_m[1], win_o, tm=tm_up, tn=D, name="in_odd")
    us = sgu_fwd(z1, ln_g, ln_b, sgu_w[0], bst, name="sgu_fwd")
    wout_o = weight(5, us)
    y1, x3 = mm_post(us, wout_o, x2, row(g_mix_post, 1), gt_m[1], tm=tm, name="out_odd")
    h3, hu1, a1, f1, x4 = ffn_fwd(1, x3)
    loss_part, dx4 = loss_grad(x4, target, tm=tm, name="loss")
    loss = lax.psum(loss_part[0, 0], ("x", "y", "c"))

    g_srcs, g_lands, g_sems = [], [], []

    def scatter(grads, nm):
        sems, srcs, lands, tok = exchange_start(grads, [_landing(g.shape[0] // N_DEV, g) for g in grads], scatter=True, name=nm)
        g_srcs.extend(srcs)
        g_lands.extend(lands)
        g_sems.extend(sems)
        return tok[0:1, 0:1]

    def ffn_bwd(i, dxo, xin, h, hu, a, f, g_post):
        dyf, da, dg_post, dgt = post_bwd_mm(dxo, f, g_post, gt_f[i], wdn[i], tm=tm, name=f"ffn_down_bwd_{i}")
        dhg, dhu, dcwg, dcwu, dcbg, dcbu = conv_bwd(da, hu[1], hu[2], hu[0], conv_w[i], rows=conv_rows, wblk=2 * LANES,
                                                    name=f"ffn_conv_bwd_{i}")
        dxin, dg_pre, dsh, dsc = mm_pre_bwd([dhg, dhu], wup[i], xin, dxo, row(g_ffn_pre, i), sc_f[i], tm=tm, tk=half_f,
                                            name=f"ffn_up_bwd_{i}")
        g_dn = wgrad([a], dyf, tr=2 * LANES, name=f"wgrad_down_{i}")
        g_up = wgrad([dhg, dhu], h, tr=2 * LANES, name=f"wgrad_up_{i}")
        tok = scatter([g_dn, g_up], f"scatter_start_ffn_{i}")
        return dxin, tok, dict(g_ffn_post=dg_post, g_ffn_pre=dg_pre, gt_f=dgt, sh_f=dsh, sc_f=dsc,
                               ffn_conv_w=jnp.concatenate([dcwg, dcwu], axis=1), ffn_conv_b=jnp.concatenate([dcbg, dcbu], axis=1)[0])

    dx3, tok, sf1 = ffn_bwd(1, dx4, x3, h3, hu1, a1, f1, row(g_ffn_post, 1))
    dy1, dus, dg_mpost1, dgt_m1 = post_bwd_mm(dx3, y1, row(g_mix_post, 1) + tok, gt_m[1], wout_o, tm=tm, name="out_odd_bwd")
    dz1, dws, dbs, dlng, dlnb = sgu_bwd(z1, dus, ln_g, ln_b, sgu_w[0], bst, name="sgu_bwd")
    dx2, dg_mpre1, dsh_m1, dsc_m1 = mm_pre_bwd([dz1], win_o, x2, dx3, row(g_mix_pre, 1), sc_m[1], tm=tm, tk=D, name="in_odd_bwd")
    tok = scatter([wgrad([us], dy1, tr=2 * LANES, name="wgrad_out_odd"), wgrad([dz1], h2, tr=2 * LANES, name="wgrad_in_odd")],
                  "scatter_start_mix_1")

    dx1, tok, sf0 = ffn_bwd(0, dx2, x1, h1, hu0, a0, f0, row(g_ffn_post, 0) + tok)
    dy0, dpa, dg_mpost0, dgt_m0 = post_bwd_mm(dx1, y0, row(g_mix_post, 0) + tok, gt_m[0], wout_e, tm=tm, name="out_even_bwd")
    du, dwp, dps = pool_bwd(u, dpa, w_pool[0], pool_scale, name="pool_bwd")
    dq, dkv, dkvc, dsink = attn_bwd(q, kv, kvc, sink, dpa, cos, sa, sb, name="attn_bwd")
    dz0 = jnp.concatenate([du, dq, dkv], axis=1)
    dzc = jnp.concatenate([jnp.zeros((C, 8 * LANES), BF16), dkvc], axis=1)
    tok = scatter([wgrad([pa], dy0, tr=2 * LANES, name="wgrad_out_even"),
                   wgrad([dz0], h0, tr=2 * LANES, extra=(dzc, hc), name="wgrad_in_even")], "scatter_start_mix_0")
    grad_x, dg_mpre0, dsh_m0, dsc_m0 = mm_pre_bwd([dz0], win_e, x, dx1, row(g_mix_pre, 0) + tok, sc_m[0], tm=tm, tk=dz0.shape[1],
                                                  name="in_even_bwd")
    _, dg_mpre0c, dcsh, dcsc = mm_pre_bwd([dkvc], win_e, ctx, None, row(g_mix_pre, 0), csc_m, tm=C, tk=2 * LANES,
                                          w_row_off=8 * LANES, name="in_even_ctx_bwd")

    slots = exchange_wait(g_srcs, g_lands, g_sems, dcsh, scatter=True, name="scatter_wait")
    out = {}

    def update(name, idx, land, transposed):
        w_, m_, v_ = (a[idx].T if transposed else a[idx] for a in (P[name], M[name], V[name]))
        r = w_.shape[0]
        tr = r // 4 if r % 64 == 0 and r > 256 else r
        res = adamw(w_, m_, v_, land.reshape(N_DEV, r, land.shape[1]), tr=tr, name=f"adamw_{name}_{idx}")
        for kind, val in zip(("grad", "delta", "new_m", "new_v"), res):
            out.setdefault((kind, name), []).append(val.T if transposed else val)

    update("w_in_even", 0, slots[7], True)
    update("w_out_even", 0, slots[6], False)
    update("w_in_odd", 0, slots[3], True)
    update("w_out_odd", 0, slots[2], False)
    update("w_ffn_up", 0, slots[5], True)
    update("w_ffn_down", 0, slots[4], False)
    update("w_ffn_up", 1, slots[1], True)
    update("w_ffn_down", 1, slots[0], False)

    zero = jnp.zeros((1, D), F32)
    dmod0 = jnp.concatenate([dsh_m0, dsc_m0, dgt_m0, sf0["sh_f"], sf0["sc_f"], sf0["gt_f"]], axis=1)
    dmodc = jnp.concatenate([dcsh, dcsc, zero, zero, zero, zero], axis=1)
    dmod1 = jnp.concatenate([dsh_m1, dsc_m1, dgt_m1, sf1["sh_f"], sf1["sc_f"], sf1["gt_f"]], axis=1)
    dmods = jnp.concatenate([dmod0, dmodc, dmod1], axis=0)
    dmods_all = all_gather_small(dmods.reshape(-1, LANES), name="gather_dmods").reshape(N_DEV, 3, N_DEV, n_ada)
    dall = lax.dynamic_index_in_dim(dmods_all, me, 2, False)
    g_w_ada, dcc = ada_bwd_mm(silu_c, c_ctx[None, :], dall, w_ada, name="ada_bwd")
    nl = w_ada.shape[0]
    res = adamw(w_ada.reshape(nl * D, n_ada), m_w_ada.reshape(nl * D, n_ada), v_w_ada.reshape(nl * D, n_ada),
                g_w_ada.reshape(1, nl * D, n_ada), tr=nl * D // 8, name="adamw_w_ada")
    for kind, val in zip(("grad", "delta", "new_m", "new_v"), res):
        out[(kind, "w_ada")] = val.reshape(nl, D, n_ada)

    rep = dict(
        c_ctx=dcc[0],
        b_ada=jnp.stack([dmod0[0] + dmodc[0], dmod1[0]]),
        g_mix_pre=jnp.concatenate([dg_mpre0 + dg_mpre0c, dg_mpre1]),
        g_mix_post=jnp.concatenate([dg_mpost0, dg_mpost1]),
        g_ffn_pre=jnp.concatenate([sf0["g_ffn_pre"], sf1["g_ffn_pre"]]),
        g_ffn_post=jnp.concatenate([sf0["g_ffn_post"], sf1["g_ffn_post"]]),
        w_pool=dwp[None], pool_scale=dps, attn_sink=dsink[:, :N_Q_HEADS],
        sgu_w=dws[None], sgu_b=dbs[:, :sgu_b.shape[1]].T[None],
        ffn_conv_b=jnp.stack([sf0["ffn_conv_b"], sf1["ffn_conv_b"]]),
    )
    rep_names = list(rep)
    conv_g = jnp.stack([sf0["ffn_conv_w"], sf1["ffn_conv_w"]]).reshape(2, 3, N_DEV, n_cw).transpose(2, 0, 1, 3)
    shard_full = dict(sgu_ln_g=dlng.reshape(N_DEV, LANES), sgu_ln_b=dlnb.reshape(N_DEV, LANES),
                      ffn_conv_w=jnp.concatenate([_rows128(conv_g[d]) for d in range(N_DEV)], axis=0))
    pieces = [_rows128(rep[k]) for k in rep_names] + [shard_full[k] for k in shard_full]
    sizes = [p.shape[0] for p in pieces]
    _, gsum = all_gather_small(jnp.concatenate(pieces, axis=0), reduce=True, name="allreduce_small_grads")
    offs = [sum(sizes[:i]) for i in range(len(sizes))]
    n_rep = len(rep_names)
    cw_rows = sizes[-1] // N_DEV
    g_own = [gsum[offs[i]:offs[i] + sizes[i]] for i in range(n_rep)]
    g_own.append(_rows128(lax.dynamic_slice_in_dim(gsum, offs[n_rep] + me, 1, 0)))
    g_own.append(_rows128(lax.dynamic_slice_in_dim(gsum, offs[n_rep + 1] + me, 1, 0)))
    g_own.append(lax.dynamic_slice_in_dim(gsum, offs[n_rep + 2] + me * cw_rows, cw_rows, 0))
    small_names = rep_names + list(shard_full)
    packs = [jnp.concatenate([_rows128(src[k]) for k in small_names], axis=0) for src in (P, M, V)]
    n_pack = packs[0].shape[0]
    gp = jnp.concatenate(g_own, axis=0)[None]
    res = adamw(*packs, gp, tr=n_pack, name="adamw_small")
    o = 0
    for k in small_names:
        n = _rows128(P[k]).shape[0]
        for kind, val in zip(("grad", "delta", "new_m", "new_v"), res):
            out[(kind, k)] = val[o:o + n].reshape(-1)[:P[k].size].reshape(P[k].shape)
        o += n
    assert o == n_pack

    names = list(P)
    final = [loss, grad_x[None]]
    for kind in ("grad", "delta", "new_m", "new_v"):
        for k in names:
            val = out[(kind, k)]
            final.append(jnp.stack(val) if isinstance(val, list) else val)
    return tuple(final)
```

```python
import functools
import math

import jax
import jax.numpy as jnp
from jax import lax
from jax.experimental import pallas as pl
from jax.experimental.pallas import tpu as pltpu

F32 = jnp.float32
BF16 = jnp.bfloat16
MESH = pl.DeviceIdType.MESH
N_DEV = 8
LANES = 128
VMEM_LIMIT = 48 * 1024 * 1024
EPS = 1e-6
NEG_INF = -1e30
GRID_W = 64
WINDOW = 128
BLK = 128
HEAD_DIM = 64
N_Q_HEADS = 8
N_KV_HEADS = 2
GQA = N_Q_HEADS // N_KV_HEADS
POOL_WINDOWS = (2, 4, 8, 16)
ROPE_BASE = 10000.0
ROPE_FREQS = HEAD_DIM // 4
PAD = 16
ADAM_LR, ADAM_B1, ADAM_B2, ADAM_EPS, ADAM_WD, ADAM_STEP = 0.001, 0.9, 0.999, 1e-08, 0.01, 10
BC1 = 1.0 - ADAM_B1 ** ADAM_STEP
BC2 = 1.0 - ADAM_B2 ** ADAM_STEP
SQRT_2_OVER_PI = math.sqrt(2.0 / math.pi)
GELU_C = 0.044715


def _cp(sem=None):
    return pltpu.CompilerParams(dimension_semantics=sem, vmem_limit_bytes=VMEM_LIMIT)


def _dot(a, b):
    return jnp.dot(a, b, preferred_element_type=F32)


def _dot_nt(a, b):
    return lax.dot_general(a, b, (((1,), (1,)), ((), ())), preferred_element_type=F32)


def _dot_tn(a, b):
    return lax.dot_general(a, b, (((0,), (0,)), ((), ())), preferred_element_type=F32)


def _rms(x):
    r = lax.rsqrt(jnp.mean(x * x, axis=-1, keepdims=True) + EPS)
    return x * r, r


def _rms_bwd(dn, n, r):
    return r * (dn - n * jnp.mean(dn * n, axis=-1, keepdims=True))


def _colsum(a):
    return jnp.sum(a, axis=0, keepdims=True)


def _rope(x, c, sa, sb):
    return x * c + pltpu.roll(x, LANES - ROPE_FREQS, 1) * sa + pltpu.roll(x, ROPE_FREQS, 1) * sb


def _full(shape):
    return pl.BlockSpec(shape, lambda *_: (0,) * len(shape))


def pre_mm(x, g, sh, sc, wt, *, tm, tn, w_row_off=0, name):
    T, D = x.shape
    n_rows = wt.shape[0] - w_row_off
    off = w_row_off // tn

    def body(x_ref, g_ref, sh_ref, sc_ref, w_ref, h_ref, z_ref):
        @pl.when(pl.program_id(1) == 0)
        def _():
            n, _ = _rms(x_ref[...])
            h_ref[...] = (n * g_ref[...] * (1.0 + sc_ref[...]) + sh_ref[...]).astype(BF16)

        z_ref[...] = _dot_nt(h_ref[...], w_ref[...]).astype(BF16)

    vec = pl.BlockSpec((1, D), lambda i, j: (0, 0))
    return pl.pallas_call(
        body, name=name, grid=(T // tm, n_rows // tn),
        in_specs=[pl.BlockSpec((tm, D), lambda i, j: (i, 0)), vec, vec, vec, pl.BlockSpec((tn, D), lambda i, j: (j + off, 0))],
        out_specs=[pl.BlockSpec((tm, D), lambda i, j: (i, 0)), pl.BlockSpec((tm, tn), lambda i, j: (i, j))],
        out_shape=[jax.ShapeDtypeStruct((T, D), BF16), jax.ShapeDtypeStruct((T, n_rows), BF16)],
        compiler_params=_cp(("parallel", "arbitrary")),
    )(x, g, sh, sc, wt)


def inproj_even(x, g, sh, sc, wt, cos, sa, sb, *, tm, name):
    T, D = x.shape
    N = wt.shape[0]

    def body(x_ref, g_ref, sh_ref, sc_ref, w_ref, c_ref, sa_ref, sb_ref, h_ref, u_ref, q_ref, kv_ref):
        n, _ = _rms(x_ref[...])
        h = (n * g_ref[...] * (1.0 + sc_ref[...]) + sh_ref[...]).astype(BF16)
        h_ref[...] = h
        z = _dot_nt(h, w_ref[...])
        u_ref[...] = z[:, :4 * LANES]
        c, a, b = c_ref[...], sa_ref[...], sb_ref[...]
        for s in range(4):
            q_ref[:, s * LANES:(s + 1) * LANES] = _rope(z[:, (4 + s) * LANES:(5 + s) * LANES], c, a, b).astype(BF16)
        kv_ref[:, :LANES] = _rope(z[:, 8 * LANES:9 * LANES], c, a, b).astype(BF16)
        kv_ref[:, LANES:] = z[:, 9 * LANES:].astype(BF16)

    vec = pl.BlockSpec((1, D), lambda i: (0, 0))
    row = lambda w: pl.BlockSpec((tm, w), lambda i: (i, 0))
    return pl.pallas_call(
        body, name=name, grid=(T // tm,),
        in_specs=[row(D), vec, vec, vec, _full((N, D)), row(LANES), row(LANES), row(LANES)],
        out_specs=[row(D), row(4 * LANES), row(4 * LANES), row(2 * LANES)],
        out_shape=[jax.ShapeDtypeStruct((T, D), BF16), jax.ShapeDtypeStruct((T, 4 * LANES), F32),
                   jax.ShapeDtypeStruct((T, 4 * LANES), BF16), jax.ShapeDtypeStruct((T, 2 * LANES), BF16)],
        compiler_params=_cp(("parallel",)),
    )(x, g, sh, sc, wt, cos, sa, sb)


def mm_post(a, w, x, g, gt, *, tm, name):
    T, K = a.shape
    D = w.shape[1]

    def body(a_ref, w_ref, x_ref, g_ref, gt_ref, y_ref, xn_ref):
        y = _dot(a_ref[...], w_ref[...])
        n, _ = _rms(y)
        y_ref[...] = y
        xn_ref[...] = x_ref[...] + gt_ref[...] * (n * g_ref[...])

    vec = pl.BlockSpec((1, D), lambda i: (0, 0))
    row = lambda w_: pl.BlockSpec((tm, w_), lambda i: (i, 0))
    return pl.pallas_call(
        body, name=name, grid=(T // tm,),
        in_specs=[row(K), _full((K, D)), row(D), vec, vec],
        out_specs=[row(D), row(D)],
        out_shape=[jax.ShapeDtypeStruct((T, D), F32), jax.ShapeDtypeStruct((T, D), F32)],
        compiler_params=_cp(("parallel",)),
    )(a, w, x, g, gt)


def post_bwd_mm(dxn, y, g, gt, w, *, tm, name):
    T, D = y.shape
    K = w.shape[0]

    def body(dxn_ref, y_ref, g_ref, gt_ref, w_ref, dy_ref, da_ref, dg_ref, dgt_ref):
        @pl.when(pl.program_id(0) == 0)
        def _():
            dg_ref[...] = jnp.zeros_like(dg_ref)
            dgt_ref[...] = jnp.zeros_like(dgt_ref)

        d = dxn_ref[...]
        n, r = _rms(y_ref[...])
        g_, gt_ = g_ref[...], gt_ref[...]
        dg_ref[...] += _colsum(d * gt_ * n)
        dgt_ref[...] += _colsum(d * g_ * n)
        dy = _rms_bwd(d * (gt_ * g_), n, r).astype(BF16)
        dy_ref[...] = dy
        da_ref[...] = _dot_nt(dy, w_ref[...]).astype(BF16)

    vec = pl.BlockSpec((1, D), lambda i: (0, 0))
    row = lambda w_: pl.BlockSpec((tm, w_), lambda i: (i, 0))
    return pl.pallas_call(
        body, name=name, grid=(T // tm,),
        in_specs=[row(D), row(D), vec, vec, _full((K, D))],
        out_specs=[row(D), row(K), vec, vec],
        out_shape=[jax.ShapeDtypeStruct((T, D), BF16), jax.ShapeDtypeStruct((T, K), BF16),
                   jax.ShapeDtypeStruct((1, D), F32), jax.ShapeDtypeStruct((1, D), F32)],
        compiler_params=_cp(("arbitrary",)),
    )(dxn, y, g, gt, w)


def mm_pre_bwd(dzs, wt, x, dres, g, sc, *, tm, tk, w_row_off=0, name):
    T, N = dzs[0].shape
    D = x.shape[1]
    nk = N // tk
    npart = len(dzs)
    off = w_row_off // tk
    has_res = dres is not None

    def body(*refs):
        dz_refs = refs[:npart]
        w_refs = refs[npart:2 * npart]
        rest = refs[2 * npart:]
        x_ref = rest[0]
        dres_ref = rest[1] if has_res else None
        g_ref, sc_ref, dx_ref, dg_ref, dsh_ref, dsc_ref, acc = rest[1 + has_res:]
        i, k = pl.program_id(0), pl.program_id(1)

        @pl.when(jnp.logical_and(i == 0, k == 0))
        def _():
            dg_ref[...] = jnp.zeros_like(dg_ref)
            dsh_ref[...] = jnp.zeros_like(dsh_ref)
            dsc_ref[...] = jnp.zeros_like(dsc_ref)

        part = _dot(dz_refs[0][...], w_refs[0][...])
        for p in range(1, npart):
            part = part + _dot(dz_refs[p][...], w_refs[p][...])

        @pl.when(k == 0)
        def _():
            acc[...] = part

        @pl.when(k > 0)
        def _():
            acc[...] += part

        @pl.when(k == nk - 1)
        def _():
            dh = acc[...]
            n, r = _rms(x_ref[...])
            g_, s1 = g_ref[...], 1.0 + sc_ref[...]
            dsh_ref[...] += _colsum(dh)
            dsc_ref[...] += _colsum(dh * n * g_)
            dg_ref[...] += _colsum(dh * s1 * n)
            dxp = _rms_bwd(dh * (g_ * s1), n, r)
            dx_ref[...] = dxp + dres_ref[...] if has_res else dxp

    vec = pl.BlockSpec((1, D), lambda i, k: (0, 0))
    row = pl.BlockSpec((tm, D), lambda i, k: (i, 0))
    w_specs = [pl.BlockSpec((tk, D), (lambda i, k, p=p: (k + off + p * nk, 0))) for p in range(npart)]
    res_specs, res_args = ([row], (dres,)) if has_res else ([], ())
    return pl.pallas_call(
        body, name=name, grid=(T // tm, nk),
        in_specs=[pl.BlockSpec((tm, tk), lambda i, k: (i, k))] * npart + w_specs + [row] + res_specs + [vec, vec],
        out_specs=[row, vec, vec, vec],
        out_shape=[jax.ShapeDtypeStruct((T, D), F32)] + [jax.ShapeDtypeStruct((1, D), F32)] * 3,
        scratch_shapes=[pltpu.VMEM((tm, D), F32)],
        compiler_params=_cp(("arbitrary", "arbitrary")),
    )(*dzs, *([wt] * npart), x, *res_args, g, sc)


def wgrad(a_parts, b, *, tr, extra=None, name):
    T, R = a_parts[0].shape
    D = b.shape[1]
    npart = len(a_parts)
    nr = R // tr

    def body(*refs):
        a_refs, b_ref = refs[:npart], refs[npart]
        g_ref = refs[-1]
        for p in range(npart):
            @pl.when(pl.program_id(0) // nr == p)
            def _():
                acc = _dot_tn(a_refs[p][...], b_ref[...])
                if extra is not None:
                    acc += _dot_tn(refs[npart + 1][...], refs[npart + 2][...])
                g_ref[...] = acc.astype(BF16)

    in_specs = [pl.BlockSpec((T, tr), (lambda r, p=p: (0, jnp.clip(r - p * nr, 0, nr - 1)))) for p in range(npart)]
    in_specs.append(_full((T, D)))
    args = [*a_parts, b]
    if extra is not None:
        a2, b2 = extra
        in_specs += [pl.BlockSpec((a2.shape[0], tr), lambda r: (0, r)), _full(b2.shape)]
        args += [a2, b2]
    return pl.pallas_call(
        body, name=name, grid=(npart * nr,),
        in_specs=in_specs, out_specs=pl.BlockSpec((tr, D), lambda r: (r, 0)),
        out_shape=jax.ShapeDtypeStruct((npart * R, D), BF16),
        compiler_params=_cp(("parallel",)),
    )(*args)


def _conv_ext(ref, r0, rows, total):
    top = ref[pl.ds(pl.multiple_of(jnp.maximum(r0 - PAD, 0), PAD), PAD), :]
    mid = ref[pl.ds(r0, rows), :]
    bot = ref[pl.ds(pl.multiple_of(jnp.minimum(r0 + rows, total - PAD), PAD), PAD), :]
    top = jnp.where(r0 > 0, top, jnp.zeros_like(top))
    bot = jnp.where(r0 + rows < total, bot, jnp.zeros_like(bot))
    return jnp.concatenate([top, mid, bot], axis=0).astype(F32)


def _shift_rows(a, k):
    return pltpu.roll(a, k % a.shape[0], 0)


def _conv3(x, w, b):
    return w[0:1] * _shift_rows(x, 1) + w[1:2] * x + w[2:3] * _shift_rows(x, -1) + b


def _gate_up_specs(rows_, wblk, nb):
    return [pl.BlockSpec((rows_, wblk), lambda j: (0, j)), pl.BlockSpec((rows_, wblk), lambda j: (0, j + nb))]


def conv_fwd(hu, cw, cb, *, rows, wblk, name):
    L, N2 = hu.shape
    nb = N2 // 2 // wblk
    nchunk = L // rows

    def body(hg_ref, hu_ref, wg_ref, wu_ref, bg_ref, bu_ref, a_ref, s1_ref, s2_ref):
        def chunk(ci, carry):
            r0 = pl.multiple_of(ci * rows, rows)
            gate = _conv3(_conv_ext(hg_ref, r0, rows, L), wg_ref[...], bg_ref[...])[PAD:PAD + rows]
            up = _conv3(_conv_ext(hu_ref, r0, rows, L), wu_ref[...], bu_ref[...])[PAD:PAD + rows]
            sg = jax.nn.sigmoid(gate)
            silu = gate * sg
            at = pl.ds(r0, rows)
            a_ref[at, :] = (silu * up).astype(BF16)
            s1_ref[at, :] = silu.astype(BF16)
            s2_ref[at, :] = (up * (sg + silu * (1.0 - sg))).astype(BF16)
            return carry

        lax.fori_loop(0, nchunk, chunk, 0)

    out = pl.BlockSpec((L, wblk), lambda j: (0, j))
    return pl.pallas_call(
        body, name=name, grid=(nb,),
        in_specs=_gate_up_specs(L, wblk, nb) + _gate_up_specs(3, wblk, nb) + _gate_up_specs(1, wblk, nb),
        out_specs=[out] * 3, out_shape=[jax.ShapeDtypeStruct((L, N2 // 2), BF16)] * 3,
        compiler_params=_cp(("parallel",)),
    )(hu, hu, cw, cw, cb, cb)


def conv_bwd(da, s1, s2, hu, cw, *, rows, wblk, name):
    L, N2 = hu.shape
    F = N2 // 2
    nb = F // wblk
    nchunk = L // rows
    mid = slice(PAD, PAD + rows)

    def body(da_ref, s1_ref, s2_ref, hg_ref, hu_ref, wg_ref, wu_ref, dg_ref, du_ref, dwg_ref, dwu_ref, dbg_ref, dbu_ref):
        for ref in (dwg_ref, dwu_ref, dbg_ref, dbu_ref):
            ref[...] = jnp.zeros_like(ref)

        def half_bwd(x_ref, dh, w_ref, dx_ref, dw_ref, db_ref, r0):
            w = w_ref[...]
            nxt, prv = _shift_rows(dh, -1)[mid], _shift_rows(dh, 1)[mid]
            dhm, xm = dh[mid], x_ref[pl.ds(r0, rows), :].astype(F32)
            dx_ref[pl.ds(r0, rows), :] = (w[0:1] * nxt + w[1:2] * dhm + w[2:3] * prv).astype(BF16)
            db_ref[...] += _colsum(dhm)
            dw_ref[0:1, :] += _colsum(nxt * xm)
            dw_ref[1:2, :] += _colsum(dhm * xm)
            dw_ref[2:3, :] += _colsum(prv * xm)

        def chunk(ci, carry):
            r0 = pl.multiple_of(ci * rows, rows)
            d = _conv_ext(da_ref, r0, rows, L)
            half_bwd(hu_ref, d * _conv_ext(s1_ref, r0, rows, L), wu_ref, du_ref, dwu_ref, dbu_ref, r0)
            half_bwd(hg_ref, d * _conv_ext(s2_ref, r0, rows, L), wg_ref, dg_ref, dwg_ref, dbg_ref, r0)
            return carry

        lax.fori_loop(0, nchunk, chunk, 0)

    blk = lambda r: pl.BlockSpec((r, wblk), lambda j: (0, j))
    return pl.pallas_call(
        body, name=name, grid=(nb,),
        in_specs=[blk(L)] * 3 + _gate_up_specs(L, wblk, nb) + _gate_up_specs(3, wblk, nb),
        out_specs=[blk(L), blk(L), blk(3), blk(3), blk(1), blk(1)],
        out_shape=[jax.ShapeDtypeStruct((L, F), BF16)] * 2 + [jax.ShapeDtypeStruct((3, F), F32)] * 2
        + [jax.ShapeDtypeStruct((1, F), F32)] * 2,
        compiler_params=_cp(("parallel",)),
    )(da, s1, s2, hu, hu, cw, cw)


def _window_sums(pad_ref, w, lead):
    a = pad_ref[...]
    k = 1
    while k < w:
        a = a + _shift_rows(a, -k)
        k *= 2
    return _shift_rows(a, lead) if lead else a


def _pool_counts(L, h):
    t = lax.broadcasted_iota(jnp.int32, (L, 1), 0)
    return (jnp.minimum(t + h, L) - jnp.maximum(t - h, 0)).astype(F32)


def _pooled(u_ref, pad_ref, L, w):
    h = w // 2
    pad_ref[pl.ds(PAD, L), :] = u_ref[...]
    win = _window_sums(pad_ref, w, h)[PAD:PAD + L]
    return win / _pool_counts(L, h) - u_ref[...]


def _zero_pad_edges(pad_ref, L):
    z = jnp.zeros((PAD, LANES), F32)
    pad_ref[pl.ds(0, PAD), :] = z
    pad_ref[pl.ds(PAD + L, PAD), :] = z


def pool_fwd(u, w_pool, pool_scale, *, name):
    L = u.shape[0]

    def body(u_ref, w_ref, ps_ref, p_ref, pad_ref):
        _zero_pad_edges(pad_ref, L)
        for gi, win in enumerate(POOL_WINDOWS):
            @pl.when(pl.program_id(0) == gi)
            def _():
                pooled = _pooled(u_ref, pad_ref, L, win)
                p_ref[...] = (_dot(pooled.astype(BF16), w_ref[...].astype(BF16)) * ps_ref[...]).astype(BF16)

    return pl.pallas_call(
        body, name=name, grid=(len(POOL_WINDOWS),),
        in_specs=[pl.BlockSpec((L, LANES), lambda gi: (0, gi)), pl.BlockSpec((None, LANES, LANES), lambda gi: (gi, 0, 0)),
                  pl.BlockSpec((1, LANES), lambda gi: (0, gi))],
        out_specs=pl.BlockSpec((L, LANES), lambda gi: (0, gi)),
        out_shape=jax.ShapeDtypeStruct((L, 4 * LANES), BF16),
        scratch_shapes=[pltpu.VMEM((L + 2 * PAD, LANES), F32)],
        compiler_params=_cp(("parallel",)),
    )(u, w_pool, pool_scale)


def pool_bwd(u, dpa, w_pool, pool_scale, *, name):
    L = u.shape[0]

    def body(u_ref, dp_ref, w_ref, ps_ref, du_ref, dw_ref, dps_ref, pad_ref):
        _zero_pad_edges(pad_ref, L)
        for gi, win in enumerate(POOL_WINDOWS):
            @pl.when(pl.program_id(0) == gi)
            def _():
                h = win // 2
                wb = w_ref[...].astype(BF16)
                pooled = _pooled(u_ref, pad_ref, L, win).astype(BF16)
                dp = dp_ref[...].astype(F32)
                dps_ref[...] = _colsum(dp * _dot(pooled, wb))
                dy = (dp * ps_ref[...]).astype(BF16)
                dw_ref[...] = _dot_tn(pooled, dy)
                dpooled = _dot_nt(dy, wb)
                pad_ref[pl.ds(PAD, L), :] = dpooled / _pool_counts(L, h)
                du_ref[...] = (_window_sums(pad_ref, win, h - 1)[PAD:PAD + L] - dpooled).astype(BF16)

    return pl.pallas_call(
        body, name=name, grid=(len(POOL_WINDOWS),),
        in_specs=[pl.BlockSpec((L, LANES), lambda gi: (0, gi)), pl.BlockSpec((L, LANES), lambda gi: (0, gi)),
                  pl.BlockSpec((None, LANES, LANES), lambda gi: (gi, 0, 0)), pl.BlockSpec((1, LANES), lambda gi: (0, gi))],
        out_specs=[pl.BlockSpec((L, LANES), lambda gi: (0, gi)), pl.BlockSpec((None, LANES, LANES), lambda gi: (gi, 0, 0)),
                   pl.BlockSpec((1, LANES), lambda gi: (0, gi))],
        out_shape=[jax.ShapeDtypeStruct((L, 4 * LANES), BF16), jax.ShapeDtypeStruct((4, LANES, LANES), F32),
                   jax.ShapeDtypeStruct((1, 4 * LANES), F32)],
        scratch_shapes=[pltpu.VMEM((L + 2 * PAD, LANES), F32)],
        compiler_params=_cp(("parallel",)),
    )(u, dpa, w_pool, pool_scale)


def _attn_probs(qk, band_k, ctx_k, sink_ref, kh, mask4):
    s_loc = jnp.where(mask4, _dot_nt(qk, band_k), NEG_INF)
    s_ctx = _dot_nt(qk, ctx_k)
    sk = jnp.concatenate([jnp.full((BLK, 1), sink_ref[kh * GQA + hh], F32) for hh in range(GQA)], axis=0)
    m = jnp.maximum(jnp.maximum(jnp.max(s_loc, axis=-1, keepdims=True), jnp.max(s_ctx, axis=-1, keepdims=True)), sk)
    e_loc, e_ctx, e_s = jnp.exp(s_loc - m), jnp.exp(s_ctx - m), jnp.exp(sk - m)
    inv = 1.0 / (jnp.sum(e_loc, axis=-1, keepdims=True) + jnp.sum(e_ctx, axis=-1, keepdims=True) + e_s)
    return e_loc * inv, e_ctx * inv, e_s * inv


def _attn_block(n, L):
    start = pl.multiple_of(jnp.clip((n - 1) * BLK, 0, L - 3 * BLK), BLK)
    qpos = n * BLK + lax.broadcasted_iota(jnp.int32, (BLK, 3 * BLK), 0)
    kpos = start + lax.broadcasted_iota(jnp.int32, (BLK, 3 * BLK), 1)
    mask = jnp.abs(kpos - qpos) <= WINDOW
    return start, jnp.concatenate([mask] * GQA, axis=0)


def _stack_slabs(ref):
    return jnp.concatenate([ref[:, s * LANES:(s + 1) * LANES] for s in range(GQA)], axis=0)


def _kv_head_lanes(kh):
    return (lax.broadcasted_iota(jnp.int32, (1, LANES), 1) // HEAD_DIM) == kh


def permute_heads(w, inverse=False):
    lo, hi = 4 * LANES, 8 * LANES
    mid = w[lo:hi].reshape(*((GQA, N_KV_HEADS) if inverse else (N_KV_HEADS, GQA)), HEAD_DIM, w.shape[1])
    return jnp.concatenate([w[:lo], mid.swapaxes(0, 1).reshape(hi - lo, w.shape[1]), w[hi:]], axis=0)


def attn_fwd(q, kv, kvc, sink, *, name):
    L = q.shape[0]
    C = kvc.shape[0]
    scale = HEAD_DIM ** -0.5

    def body(q_ref, kv_ref, kvc_ref, sink_ref, o_ref):
        start, mask4 = _attn_block(pl.program_id(0), L)
        band = kv_ref[pl.ds(start, 3 * BLK), :]
        kvc_ = kvc_ref[...]
        qs = _stack_slabs(q_ref) * scale
        o = jnp.zeros((GQA * BLK, LANES), F32)
        for kh in range(N_KV_HEADS):
            grp = _kv_head_lanes(kh)
            qk = jnp.where(grp, qs, jnp.zeros_like(qs))
            p_loc, p_ctx, _ = _attn_probs(qk, band[:, :LANES], kvc_[:, :LANES], sink_ref, kh, mask4)
            o = o + jnp.where(grp, _dot(p_loc.astype(BF16), band[:, LANES:]) + _dot(p_ctx.astype(BF16), kvc_[:, LANES:]), 0.0)
        for s in range(GQA):
            o_ref[:, s * LANES:(s + 1) * LANES] = o[s * BLK:(s + 1) * BLK].astype(BF16)

    return pl.pallas_call(
        body, name=name, grid=(L // BLK,),
        in_specs=[pl.BlockSpec((BLK, 4 * LANES), lambda n: (n, 0)), _full((L, 2 * LANES)), _full((C, 2 * LANES)),
                  pl.BlockSpec(memory_space=pltpu.SMEM)],
        out_specs=pl.BlockSpec((BLK, 4 * LANES), lambda n: (n, 0)),
        out_shape=jax.ShapeDtypeStruct((L, 4 * LANES), BF16),
        compiler_params=_cp(("parallel",)),
    )(q, kv, kvc, sink)


def attn_bwd(q, kv, kvc, sink, dpa, cos, sa, sb, *, name):
    L = q.shape[0]
    C = kvc.shape[0]
    nb = L // BLK
    scale = HEAD_DIM ** -0.5

    def body(q_ref, kv_ref, kvc_ref, sink_ref, do_ref, c_ref, sa_ref, sb_ref, cq_ref, saq_ref, sbq_ref,
             dq_ref, dkv_ref, dkvc_ref, dsink_ref, dkv_acc, dkvc_acc):
        n = pl.program_id(0)

        @pl.when(n == 0)
        def _():
            dkv_acc[...] = jnp.zeros_like(dkv_acc)
            dkvc_acc[...] = jnp.zeros_like(dkvc_acc)
            dsink_ref[...] = jnp.zeros_like(dsink_ref)

        start, mask4 = _attn_block(n, L)
        band = kv_ref[pl.ds(start, 3 * BLK), :]
        kvc_ = kvc_ref[...]
        band_k, band_v, ctx_k, ctx_v = band[:, :LANES], band[:, LANES:], kvc_[:, :LANES], kvc_[:, LANES:]
        qs = _stack_slabs(q_ref) * scale
        dos = _stack_slabs(do_ref)
        lane = lax.broadcasted_iota(jnp.int32, (1, LANES), 1)
        dsink = jnp.zeros((1, LANES), F32)
        dq = jnp.zeros((GQA * BLK, LANES), F32)
        dk = jnp.zeros((3 * BLK, LANES), F32)
        dv = jnp.zeros((3 * BLK, LANES), F32)
        dkc = jnp.zeros((C, LANES), F32)
        dvc = jnp.zeros((C, LANES), F32)
        for kh in range(N_KV_HEADS):
            grp = _kv_head_lanes(kh)
            qk = jnp.where(grp, qs, jnp.zeros_like(qs))
            dok = jnp.where(grp, dos, jnp.zeros_like(dos))
            p_loc, p_ctx, p_s = _attn_probs(qk, band_k, ctx_k, sink_ref, kh, mask4)
            dp_loc = _dot_nt(dok, band_v)
            dp_ctx = _dot_nt(dok, ctx_v)
            delta = jnp.sum(p_loc * dp_loc, axis=-1, keepdims=True) + jnp.sum(p_ctx * dp_ctx, axis=-1, keepdims=True)
            ds_loc = (p_loc * (dp_loc - delta)).astype(BF16)
            ds_ctx = (p_ctx * (dp_ctx - delta)).astype(BF16)
            dsk = p_s * delta
            for hh in range(GQA):
                dsink = dsink - jnp.where(lane == kh * GQA + hh, jnp.sum(dsk[hh * BLK:(hh + 1) * BLK], axis=0, keepdims=True), 0.0)
            dq = dq + jnp.where(grp, _dot(ds_loc, band_k) + _dot(ds_ctx, ctx_k), 0.0)
            dk = dk + _dot_tn(ds_loc, qk)
            dv = dv + _dot_tn(p_loc.astype(BF16), dok)
            dkc = dkc + _dot_tn(ds_ctx, qk)
            dvc = dvc + _dot_tn(p_ctx.astype(BF16), dok)
        dsink_ref[...] += dsink
        dkv_acc[pl.ds(start, 3 * BLK), :LANES] += dk
        dkv_acc[pl.ds(start, 3 * BLK), LANES:] += dv
        dkvc_acc[:, :LANES] += dkc
        dkvc_acc[:, LANES:] += dvc
        c, a, b = cq_ref[...], -saq_ref[...], -sbq_ref[...]
        for s in range(GQA):
            dq_ref[:, s * LANES:(s + 1) * LANES] = _rope(dq[s * BLK:(s + 1) * BLK] * scale, c, a, b).astype(BF16)

        @pl.when(n == nb - 1)
        def _():
            dkv_ref[:, :LANES] = _rope(dkv_acc[:, :LANES], c_ref[...], -sa_ref[...], -sb_ref[...]).astype(BF16)
            dkv_ref[:, LANES:] = dkv_acc[:, LANES:].astype(BF16)
            dkvc_ref[...] = dkvc_acc[...].astype(BF16)

    blk = lambda w: pl.BlockSpec((BLK, w), lambda n: (n, 0))
    return pl.pallas_call(
        body, name=name, grid=(nb,),
        in_specs=[blk(4 * LANES), _full((L, 2 * LANES)), _full((C, 2 * LANES)), pl.BlockSpec(memory_space=pltpu.SMEM),
                  pl.BlockSpec((BLK, 4 * LANES), lambda n: (n, 1)),
                  _full((L, LANES)), _full((L, LANES)), _full((L, LANES)), blk(LANES), blk(LANES), blk(LANES)],
        out_specs=[blk(4 * LANES), _full((L, 2 * LANES)), _full((C, 2 * LANES)), _full((1, LANES))],
        out_shape=[jax.ShapeDtypeStruct((L, 4 * LANES), BF16), jax.ShapeDtypeStruct((L, 2 * LANES), BF16),
                   jax.ShapeDtypeStruct((C, 2 * LANES), BF16), jax.ShapeDtypeStruct((1, LANES), F32)],
        scratch_shapes=[pltpu.VMEM((L, 2 * LANES), F32), pltpu.VMEM((C, 2 * LANES), F32)],
        compiler_params=_cp(("arbitrary",)),
    )(q, kv, kvc, sink, dpa, cos, sa, sb, cos, sa, sb)


def _gelu_parts(x):
    th = jnp.tanh(SQRT_2_OVER_PI * (x + GELU_C * x * x * x))
    return 0.5 * x * (1.0 + th), th


def _gelu_grad(x, th):
    return 0.5 * (1.0 + th) + 0.5 * x * (1.0 - th * th) * SQRT_2_OVER_PI * (1.0 + 3.0 * GELU_C * x * x)


def _layernorm(v):
    mu = jnp.mean(v, axis=-1, keepdims=True)
    vc = v - mu
    rstd = lax.rsqrt(jnp.mean(vc * vc, axis=-1, keepdims=True) + EPS)
    return vc * rstd, rstd


def sgu_fwd(z1, ln_g, ln_b, ws, bst, *, name):
    L, W2 = z1.shape
    W = W2 // 2
    ng = W // LANES

    def body(z_ref, g_ref, b_ref, ws_ref, bs_ref, o_ref):
        z, _ = _gelu_parts(z_ref[...].astype(F32))
        xhat, _ = _layernorm(z[:, W:])
        vln = (xhat * g_ref[...] + b_ref[...]).astype(BF16)
        for gi in range(ng):
            cs = slice(gi * LANES, (gi + 1) * LANES)
            s = _dot(ws_ref[gi].astype(BF16), vln[:, cs]) + bs_ref[:, gi:gi + 1]
            o_ref[:, cs] = (z[:, cs] * s).astype(BF16)

    vec = _full((1, W))
    return pl.pallas_call(
        body, name=name, grid=(L // BLK,),
        in_specs=[pl.BlockSpec((BLK, W2), lambda n: (n, 0)), vec, vec, _full((ng, LANES, LANES)), _full((BLK, ng))],
        out_specs=pl.BlockSpec((BLK, W), lambda n: (n, 0)),
        out_shape=jax.ShapeDtypeStruct((L, W), BF16),
        compiler_params=_cp(("parallel",)),
    )(z1, ln_g, ln_b, ws, bst)


def sgu_bwd(z1, dus, ln_g, ln_b, ws, bst, *, name):
    L, W2 = z1.shape
    W = W2 // 2
    ng = W // LANES

    def body(z_ref, d_ref, g_ref, b_ref, ws_ref, bs_ref, dz_ref, dws_ref, dbs_ref, dg_ref, db_ref, dv_scr):
        @pl.when(pl.program_id(0) == 0)
        def _():
            dws_ref[...] = jnp.zeros_like(dws_ref)
            dbs_ref[...] = jnp.zeros_like(dbs_ref)
            dg_ref[...] = jnp.zeros_like(dg_ref)
            db_ref[...] = jnp.zeros_like(db_ref)

        zp = z_ref[...].astype(F32)
        z, th = _gelu_parts(zp)
        xhat, rstd = _layernorm(z[:, W:])
        vln = (xhat * g_ref[...] + b_ref[...]).astype(BF16)
        d = d_ref[...].astype(F32)
        lane = lax.broadcasted_iota(jnp.int32, (1, LANES), 1)
        dbs = jnp.zeros((BLK, LANES), F32)
        dgel = _gelu_grad(zp, th)
        for gi in range(ng):
            cs = slice(gi * LANES, (gi + 1) * LANES)
            wb = ws_ref[gi].astype(BF16)
            s = _dot(wb, vln[:, cs]) + bs_ref[:, gi:gi + 1]
            dz_ref[:, cs] = (d[:, cs] * s * dgel[:, cs]).astype(BF16)
            ds = d[:, cs] * z[:, cs]
            dbs = dbs + jnp.where(lane == gi, jnp.sum(ds, axis=-1, keepdims=True), 0.0)
            dsb = ds.astype(BF16)
            dws_ref[gi] += _dot_nt(dsb, vln[:, cs])
            dv_scr[:, cs] = _dot_tn(wb, dsb)
        dbs_ref[...] += dbs
        dvln = dv_scr[...]
        dg_ref[...] += _colsum(dvln * xhat)
        db_ref[...] += _colsum(dvln)
        dxh = dvln * g_ref[...]
        dv = rstd * (dxh - jnp.mean(dxh, axis=-1, keepdims=True) - xhat * jnp.mean(dxh * xhat, axis=-1, keepdims=True))
        dz_ref[:, W:] = (dv * dgel[:, W:]).astype(BF16)

    vec = _full((1, W))
    return pl.pallas_call(
        body, name=name, grid=(L // BLK,),
        in_specs=[pl.BlockSpec((BLK, W2), lambda n: (n, 0)), pl.BlockSpec((BLK, W), lambda n: (n, 0)), vec, vec,
                  _full((ng, LANES, LANES)), _full((BLK, ng))],
        out_specs=[pl.BlockSpec((BLK, W2), lambda n: (n, 0)), _full((ng, LANES, LANES)), _full((BLK, LANES)), vec, vec],
        out_shape=[jax.ShapeDtypeStruct((L, W2), BF16), jax.ShapeDtypeStruct((ng, LANES, LANES), F32),
                   jax.ShapeDtypeStruct((BLK, LANES), F32), jax.ShapeDtypeStruct((1, W), F32), jax.ShapeDtypeStruct((1, W), F32)],
        scratch_shapes=[pltpu.VMEM((BLK, W), F32)],
        compiler_params=_cp(("arbitrary",)),
    )(z1, dus, ln_g, ln_b, ws, bst)


def loss_grad(xo, target, *, tm, name):
    T, D = xo.shape

    def body(x_ref, t_ref, l_ref, d_ref):
        @pl.when(pl.program_id(0) == 0)
        def _():
            l_ref[...] = jnp.zeros_like(l_ref)

        e = x_ref[...] - t_ref[...]
        l_ref[...] += 0.5 * jnp.sum(jnp.mean(e * e, axis=-1, keepdims=True), axis=0, keepdims=True)
        d_ref[...] = e * (1.0 / D)

    row = pl.BlockSpec((tm, D), lambda i: (i, 0))
    return pl.pallas_call(
        body, name=name, grid=(T // tm,), in_specs=[row, row], out_specs=[_full((1, 1)), row],
        out_shape=[jax.ShapeDtypeStruct((1, 1), F32), jax.ShapeDtypeStruct((T, D), F32)],
        compiler_params=_cp(("arbitrary",)),
    )(xo, target)


def adamw(w, m, v, gparts, *, tr, name):
    R, Wd = w.shape
    S = gparts.shape[0]

    def body(w_ref, m_ref, v_ref, gp_ref, g_ref, d_ref, nm_ref, nv_ref):
        g = gp_ref[0].astype(F32)
        for s in range(1, S):
            g = g + gp_ref[s].astype(F32)
        m_ = ADAM_B1 * m_ref[...] + (1.0 - ADAM_B1) * g
        v_ = ADAM_B2 * v_ref[...] + (1.0 - ADAM_B2) * (g * g)
        g_ref[...] = g
        nm_ref[...] = m_
        nv_ref[...] = v_
        d_ref[...] = -ADAM_LR * ((m_ / BC1) / (jnp.sqrt(v_ / BC2) + ADAM_EPS) + ADAM_WD * w_ref[...])

    row = pl.BlockSpec((tr, Wd), lambda i: (i, 0))
    return pl.pallas_call(
        body, name=name, grid=(R // tr,),
        in_specs=[row, row, row, pl.BlockSpec((S, tr, Wd), lambda i: (0, i, 0))],
        out_specs=[row] * 4, out_shape=[jax.ShapeDtypeStruct((R, Wd), F32)] * 4,
        compiler_params=_cp(("parallel",)),
    )(w, m, v, gparts)


def ada_fwd_mm(cs, w_ada, b_loc, *, name):
    R, D = cs.shape
    nl, _, n = w_ada.shape

    def body(c_ref, w_ref, b_ref, s_ref, m_ref):
        c = c_ref[...]
        s = c * jax.nn.sigmoid(c)
        s_ref[...] = s
        for i in range(nl):
            m_ref[i] = _dot(s.astype(BF16), w_ref[i].astype(BF16)) + b_ref[i:i + 1, :]

    return pl.pallas_call(
        body, name=name, in_specs=[_full((R, D)), _full((nl, D, n)), _full((nl, n))],
        out_specs=[_full((R, D)), _full((nl, R, n))], grid=(1,),
        out_shape=[jax.ShapeDtypeStruct((R, D), F32), jax.ShapeDtypeStruct((nl, R, n), F32)],
        compiler_params=_cp(("arbitrary",)),
    )(cs, w_ada, b_loc)


def ada_bwd_mm(s, c_ctx, dall, w_ada, *, name):
    R, D = s.shape
    nl, _, n = w_ada.shape

    def body(s_ref, cc_ref, d_ref, w_ref, gw_ref, dcc_ref):
        sb = s_ref[...].astype(BF16)
        row = lax.broadcasted_iota(jnp.int32, (R, 1), 0)
        dctx = d_ref[0, 1:2, :]
        for dv in range(1, N_DEV):
            dctx = dctx + d_ref[dv, 1:2, :]
        for i in range(nl):
            dm = jnp.zeros((R, n), F32)
            for dv in range(N_DEV):
                dm = dm + jnp.where(row == dv, d_ref[dv, 2 * i:2 * i + 1, :], 0.0)
            if i == 0:
                dm = dm + jnp.where(row == N_DEV, dctx, 0.0)
            gw_ref[i] = _dot_tn(sb, dm.astype(BF16))
        cc = cc_ref[...]
        sg = jax.nn.sigmoid(cc)
        ds = _dot_nt(jnp.broadcast_to(dctx, (8, n)).astype(BF16), w_ref[0].astype(BF16))
        dcc_ref[...] = ds * (sg * (1.0 + cc * (1.0 - sg)))

    return pl.pallas_call(
        body, name=name, grid=(1,),
        in_specs=[_full((R, D)), _full((1, D)), _full((N_DEV, 3, n)), _full((nl, D, n))],
        out_specs=[_full((nl, D, n)), _full((8, D))],
        out_shape=[jax.ShapeDtypeStruct((nl, D, n), F32), jax.ShapeDtypeStruct((8, D), F32)],
        compiler_params=_cp(("arbitrary",)),
    )(s, c_ctx, dall, w_ada)


def _place():
    x, y, c = lax.axis_index("x"), lax.axis_index("y"), lax.axis_index("c")
    return x, y, c


def _lin(p):
    return 4 * p[0] + 2 * p[1] + p[2]


def all_gather_small(xb, *, reduce=False, name):
    R, W = xb.shape

    def body(x_ref, *rest):
        out_ref = rest[0]
        send_sems, recv_sems, local_sem = rest[-3:]
        x, y, c = _place()
        me, sibling = (x, y, c), (x, y, 1 - c)
        chips = [(1 - x, y), (x, 1 - y), (1 - x, 1 - y)]

        def copy(k, block, to, src=None):
            dst = out_ref.at[_lin(block)]
            return pltpu.make_async_remote_copy(
                src_ref=dst if src is None else src, dst_ref=dst, send_sem=send_sems.at[k], recv_sem=recv_sems.at[k],
                device_id=to, device_id_type=MESH)

        mine = pltpu.make_async_copy(x_ref, out_ref.at[_lin(me)], local_sem)
        mine.start()
        first = [copy(0, me, sibling, src=x_ref)]
        first += [copy(1 + j, me, (*chip, c), src=x_ref) for j, chip in enumerate(chips)]
        for cp in first:
            cp.start()
        passed = [copy(4 + j, (*chip, c), sibling) for j, chip in enumerate(chips)]
        for j, chip in enumerate(chips):
            copy(1 + j, (*chip, c), me).wait_recv()
            passed[j].start()
        copy(0, sibling, me).wait_recv()
        for j, chip in enumerate(chips):
            copy(4 + j, (*chip, 1 - c), me).wait_recv()
        for cp in first + passed:
            cp.wait_send()
        mine.wait()
        if reduce:
            acc = out_ref[0]
            for dv in range(1, N_DEV):
                acc = acc + out_ref[dv]
            rest[1][...] = acc

    vm = pl.BlockSpec(memory_space=pltpu.VMEM)
    out_shape = [jax.ShapeDtypeStruct((N_DEV, R, W), xb.dtype)]
    if reduce:
        out_shape.append(jax.ShapeDtypeStruct((R, W), xb.dtype))
    res = pl.pallas_call(
        body, name=name, in_specs=[vm], out_specs=[vm] * len(out_shape), out_shape=out_shape,
        scratch_shapes=[pltpu.SemaphoreType.DMA((7,)), pltpu.SemaphoreType.DMA((7,)), pltpu.SemaphoreType.DMA],
        compiler_params=pltpu.CompilerParams(vmem_limit_bytes=VMEM_LIMIT),
    )(xb)
    return res if reduce else res[0]


HBM_SPEC = pl.BlockSpec(memory_space=pltpu.HBM)
SEM_SPEC = pl.BlockSpec(memory_space=pltpu.SEMAPHORE)
ORDERED_EFFECT = pltpu.SideEffectType.DATAFLOW_SIDE_EFFECTING


def _exchange_copies(srcs, lands, sems, scatter):
    x, y, c = _place()
    me = _lin((x, y, c))
    for j in range(len(srcs)):
        r = lands[j].shape[0] // N_DEV
        block = lambda d, j=j, r=r: pl.ds(pl.multiple_of(d * r, 16), r)
        for k in range(1, N_DEV):
            peer = (x ^ (k >> 2), y ^ ((k >> 1) & 1), c ^ (k & 1))
            src = srcs[j].at[block(_lin(peer)), :] if scatter else srcs[j]
            mk = lambda dst, j=j, k=k, peer=peer, src=src: pltpu.make_async_remote_copy(
                src_ref=src, dst_ref=dst, send_sem=sems[2 * j].at[k - 1], recv_sem=sems[2 * j + 1].at[k - 1],
                device_id=peer, device_id_type=MESH)
            yield mk(lands[j].at[block(me), :]), mk(lands[j].at[block(_lin(peer)), :])


def exchange_start(srcs, lands, *, scatter, name):
    nw = len(srcs)

    def body(*refs):
        for start, _ in _exchange_copies(refs[:nw], refs[nw:2 * nw], refs[2 * nw:4 * nw], scatter):
            start.start()
        refs[-1][...] = jnp.zeros_like(refs[-1])

    thru = [pltpu.HBM(a.shape, a.dtype) for a in (*srcs, *lands)]
    res = pl.pallas_call(
        body, name=name, in_specs=[HBM_SPEC] * (2 * nw),
        out_specs=[SEM_SPEC] * (2 * nw) + [HBM_SPEC] * (2 * nw) + [pl.BlockSpec(memory_space=pltpu.VMEM)],
        out_shape=[pltpu.SemaphoreType.DMA((N_DEV - 1,))] * (2 * nw) + thru + [jax.ShapeDtypeStruct((8, LANES), F32)],
        input_output_aliases={i: 2 * nw + i for i in range(2 * nw)},
        compiler_params=pltpu.CompilerParams(has_side_effects=ORDERED_EFFECT),
    )(*[pltpu.with_memory_space_constraint(a, pltpu.HBM) for a in (*srcs, *lands)])
    return res[:2 * nw], res[2 * nw:3 * nw], res[3 * nw:4 * nw], res[-1]


def exchange_wait(srcs, lands, sems, after, *, scatter, name):
    nw = len(srcs)

    def body(*refs):
        for _, arrive in _exchange_copies(refs[:nw], refs[nw:2 * nw], refs[2 * nw:4 * nw], scatter):
            arrive.wait_send()
            arrive.wait_recv()

    res = pl.pallas_call(
        body, name=name, in_specs=[HBM_SPEC] * (2 * nw) + [SEM_SPEC] * (2 * nw) + [pl.BlockSpec(memory_space=pl.ANY)],
        out_specs=[HBM_SPEC] * (2 * nw), out_shape=[pltpu.HBM(a.shape, a.dtype) for a in (*srcs, *lands)],
        input_output_aliases={i: i for i in range(2 * nw)},
        compiler_params=pltpu.CompilerParams(has_side_effects=ORDERED_EFFECT),
    )(*srcs, *lands, *sems, after)
    me = _lin(_place())
    done = []
    for src, land in zip(res[:nw], res[nw:]):
        r = land.shape[0] // N_DEV
        own = lax.dynamic_slice_in_dim(src, me * r, r, 0) if scatter else src
        done.append(lax.dynamic_update_slice(land, own, (me * r, 0)))
    return done


def _rope_tables(L):
    t = jnp.arange(L)
    inv = ROPE_BASE ** (-jnp.arange(ROPE_FREQS, dtype=F32) / ROPE_FREQS)
    ar = (t // GRID_W).astype(F32)[:, None] * inv
    ac = (t % GRID_W).astype(F32)[:, None] * inv
    z = jnp.zeros_like(ar)
    cos = jnp.concatenate([jnp.cos(ar), jnp.cos(ar), jnp.cos(ac), jnp.cos(ac)], axis=1)
    sa = jnp.concatenate([-jnp.sin(ar), z, -jnp.sin(ac), z], axis=1)
    sb = jnp.concatenate([z, jnp.sin(ar), z, jnp.sin(ac)], axis=1)
    return tuple(jnp.tile(a, (1, LANES // HEAD_DIM)) for a in (cos, sa, sb))


def _rows128(a):
    f = a.reshape(-1)
    n = -(-f.shape[0] // (8 * LANES)) * 8 * LANES
    return jnp.pad(f, (0, n - f.shape[0])).reshape(-1, LANES)


def _landing(rows, like):
    return lax.empty((N_DEV * rows, like.shape[1]), like.dtype)


def kernel(x, c, ctx, c_ctx, w_ada, b_ada, g_mix_pre, g_mix_post, g_ffn_pre, g_ffn_post, w_in_even, w_pool, pool_scale, attn_sink, w_out_even, w_in_odd, sgu_ln_g, sgu_ln_b, sgu_w, sgu_b, w_out_odd, w_ffn_up, ffn_conv_w, ffn_conv_b, w_ffn_down, loss_target, m_c_ctx, m_w_ada, m_b_ada, m_g_mix_pre, m_g_mix_post, m_g_ffn_pre, m_g_ffn_post, m_w_in_even, m_w_pool, m_pool_scale, m_attn_sink, m_w_out_even, m_w_in_odd, m_sgu_ln_g, m_sgu_ln_b, m_sgu_w, m_sgu_b, m_w_out_odd, m_w_ffn_up, m_ffn_conv_w, m_ffn_conv_b, m_w_ffn_down, v_c_ctx, v_w_ada, v_b_ada, v_g_mix_pre, v_g_mix_post, v_g_ffn_pre, v_g_ffn_post, v_w_in_even, v_w_pool, v_pool_scale, v_attn_sink, v_w_out_even, v_w_in_odd, v_sgu_ln_g, v_sgu_ln_b, v_sgu_w, v_sgu_b, v_w_out_odd, v_w_ffn_up, v_ffn_conv_w, v_ffn_conv_b, v_w_ffn_down):
    P = dict(c_ctx=c_ctx, w_ada=w_ada, b_ada=b_ada, g_mix_pre=g_mix_pre, g_mix_post=g_mix_post, g_ffn_pre=g_ffn_pre,
             g_ffn_post=g_ffn_post, w_in_even=w_in_even, w_pool=w_pool, pool_scale=pool_scale, attn_sink=attn_sink,
             w_out_even=w_out_even, w_in_odd=w_in_odd, sgu_ln_g=sgu_ln_g, sgu_ln_b=sgu_ln_b, sgu_w=sgu_w, sgu_b=sgu_b,
             w_out_odd=w_out_odd, w_ffn_up=w_ffn_up, ffn_conv_w=ffn_conv_w, ffn_conv_b=ffn_conv_b, w_ffn_down=w_ffn_down)
    M = dict(c_ctx=m_c_ctx, w_ada=m_w_ada, b_ada=m_b_ada, g_mix_pre=m_g_mix_pre, g_mix_post=m_g_mix_post, g_ffn_pre=m_g_ffn_pre,
             g_ffn_post=m_g_ffn_post, w_in_even=m_w_in_even, w_pool=m_w_pool, pool_scale=m_pool_scale, attn_sink=m_attn_sink,
             w_out_even=m_w_out_even, w_in_odd=m_w_in_odd, sgu_ln_g=m_sgu_ln_g, sgu_ln_b=m_sgu_ln_b, sgu_w=m_sgu_w, sgu_b=m_sgu_b,
             w_out_odd=m_w_out_odd, w_ffn_up=m_w_ffn_up, ffn_conv_w=m_ffn_conv_w, ffn_conv_b=m_ffn_conv_b, w_ffn_down=m_w_ffn_down)
    V = dict(c_ctx=v_c_ctx, w_ada=v_w_ada, b_ada=v_b_ada, g_mix_pre=v_g_mix_pre, g_mix_post=v_g_mix_post, g_ffn_pre=v_g_ffn_pre,
             g_ffn_post=v_g_ffn_post, w_in_even=v_w_in_even, w_pool=v_w_pool, pool_scale=v_pool_scale, attn_sink=v_attn_sink,
             w_out_even=v_w_out_even, w_in_odd=v_w_in_odd, sgu_ln_g=v_sgu_ln_g, sgu_ln_b=v_sgu_ln_b, sgu_w=v_sgu_w, sgu_b=v_sgu_b,
             w_out_odd=v_w_out_odd, w_ffn_up=v_w_ffn_up, ffn_conv_w=v_ffn_conv_w, ffn_conv_b=v_ffn_conv_b, w_ffn_down=v_w_ffn_down)

    x = x[0]
    ctx = ctx[0]
    target = loss_target[0]
    L, D = x.shape
    C = ctx.shape[0]
    tm = min(512, L)
    tm_up = min(1024, L)
    conv_rows = min(256, L)
    me = 4 * lax.axis_index("x") + 2 * lax.axis_index("y") + lax.axis_index("c")
    n_ada = w_ada.shape[2]
    F = w_ffn_down.shape[1] * N_DEV
    half_f = F // 2

    n_cw = ffn_conv_w.shape[2]
    small = jnp.concatenate([_rows128(c), _rows128(sgu_ln_g), _rows128(sgu_ln_b), _rows128(ffn_conv_w)], axis=0)
    small_all = all_gather_small(small, name="gather_small_inputs")
    c_all = small_all[:, :8].reshape(N_DEV, D)
    ln_g = small_all[:, 8].reshape(1, D)
    ln_b = small_all[:, 16].reshape(1, D)
    conv_w = small_all[:, 24:].reshape(N_DEV, -1)[:, :2 * 3 * n_cw].reshape(N_DEV, 2, 3, n_cw)
    conv_w = conv_w.transpose(1, 2, 0, 3).reshape(2, 3, 2 * F)

    cs = jnp.concatenate([c_all, c_ctx[None, :], jnp.zeros((7, D), F32)], axis=0)
    b_loc = lax.dynamic_slice(b_ada, (0, me * n_ada), (2, n_ada))
    silu_c, mods_loc = ada_fwd_mm(cs, w_ada, b_loc, name="ada_fwd")
    mods_all = all_gather_small(mods_loc.reshape(-1, LANES), name="gather_mods")

    shards = [s.astype(BF16) for s in (w_in_even[0].T, w_out_even[0], w_ffn_up[0].T, w_ffn_down[0],
                                       w_in_odd[0].T, w_out_odd[0], w_ffn_up[1].T, w_ffn_down[1])]
    shards, mods_all = lax.optimization_barrier((shards, mods_all))
    w_sems, w_srcs, w_lands, _ = exchange_start(shards, [_landing(s.shape[0], s) for s in shards], scatter=False, name="gather_start")

    def weight(j, after):
        return exchange_wait([w_srcs[j]], [w_lands[j]], w_sems[2 * j:2 * j + 2], after, scatter=False, name=f"gather_wait_{j}")[0]

    mods_all = mods_all.reshape(N_DEV, 2, 16, n_ada).transpose(1, 2, 0, 3).reshape(2, 16, 6 * D)
    mod = lambda i, row: [m_[None, :] for m_ in jnp.split(lax.dynamic_index_in_dim(mods_all[i], row, 0, False), 6)]
    sh_m, sc_m, gt_m, sh_f, sc_f, gt_f = zip(mod(0, me), mod(1, me))
    csh_m, csc_m = mod(0, N_DEV)[:2]

    row = lambda a, i: a[i][None, :]

    cos, sa, sb = _rope_tables(L)
    sink = attn_sink[0]
    bst = sgu_b[0].T
    wup, wdn = [None, None], [None, None]

    def ffn_fwd(i, xin):
        wup[i] = weight(2 + 4 * i, xin)
        h, hu = pre_mm(xin, row(g_ffn_pre, i), sh_f[i], sc_f[i], wup[i], tm=tm_up, tn=half_f, name=f"ffn_up_{i}")
        a, s1, s2 = conv_fwd(hu, conv_w[i], ffn_conv_b[i][None, :], rows=conv_rows, wblk=2 * LANES, name=f"ffn_conv_{i}")
        wdn[i] = weight(3 + 4 * i, a)
        f, xo = mm_post(a, wdn[i], xin, row(g_ffn_post, i), gt_f[i], tm=tm, name=f"ffn_down_{i}")
        return h, (hu, s1, s2), a, f, xo

    win_e = permute_heads(weight(0, sh_m[0]))
    h0, u, q, kv = inproj_even(x, row(g_mix_pre, 0), sh_m[0], sc_m[0], win_e, cos, sa, sb, tm=tm, name="in_even")
    hc, kvc = pre_mm(ctx, row(g_mix_pre, 0), csh_m, csc_m, win_e, tm=C, tn=2 * LANES, w_row_off=8 * LANES, name="in_even_ctx")
    pa = jnp.concatenate([pool_fwd(u, w_pool[0], pool_scale, name="pool_fwd"),
                          attn_fwd(q, kv, kvc, sink, name="attn_fwd")], axis=1)
    wout_e = permute_heads(weight(1, pa))
    y0, x1 = mm_post(pa, wout_e, x, row(g_mix_post, 0), gt_m[0], tm=tm, name="out_even")
    h1, hu0, a0, f0, x2 = ffn_fwd(0, x1)
    win_o = weight(4, x2)
    h2, z1 = pre_mm(x2, row(g_mix_pre, 1), sh_m[1], sc_m[1], win_o, tm=tm_up, tn=D, name="in_odd")
    us = sgu_fwd(z1, ln_g, ln_b, sgu_w[0], bst, name="sgu_fwd")
    wout_o = weight(5, us)
    y1, x3 = mm_post(us, wout_o, x2, row(g_mix_post, 1), gt_m[1], tm=tm, name="out_odd")
    h3, hu1, a1, f1, x4 = ffn_fwd(1, x3)
    loss_part, dx4 = loss_grad(x4, target, tm=tm, name="loss")
    loss = lax.psum(loss_part[0, 0], ("x", "y", "c"))

    g_srcs, g_lands, g_sems = [], [], []

    def scatter(grads, nm):
        sems, srcs, lands, tok = exchange_start(grads, [_landing(g.shape[0] // N_DEV, g) for g in grads], scatter=True, name=nm)
        g_srcs.extend(srcs)
        g_lands.extend(lands)
        g_sems.extend(sems)
        return tok[0:1, 0:1]

    def ffn_bwd(i, dxo, xin, h, hu, a, f, g_post):
        dyf, da, dg_post, dgt = post_bwd_mm(dxo, f, g_post, gt_f[i], wdn[i], tm=tm, name=f"ffn_down_bwd_{i}")
        dhg, dhu, dcwg, dcwu, dcbg, dcbu = conv_bwd(da, hu[1], hu[2], hu[0], conv_w[i], rows=conv_rows, wblk=2 * LANES,
                                                    name=f"ffn_conv_bwd_{i}")
        dxin, dg_pre, dsh, dsc = mm_pre_bwd([dhg, dhu], wup[i], xin, dxo, row(g_ffn_pre, i), sc_f[i], tm=tm, tk=half_f,
                                            name=f"ffn_up_bwd_{i}")
        g_dn = wgrad([a], dyf, tr=2 * LANES, name=f"wgrad_down_{i}")
        g_up = wgrad([dhg, dhu], h, tr=2 * LANES, name=f"wgrad_up_{i}")
        tok = scatter([g_dn, g_up], f"scatter_start_ffn_{i}")
        return dxin, tok, dict(g_ffn_post=dg_post, g_ffn_pre=dg_pre, gt_f=dgt, sh_f=dsh, sc_f=dsc,
                               ffn_conv_w=jnp.concatenate([dcwg, dcwu], axis=1), ffn_conv_b=jnp.concatenate([dcbg, dcbu], axis=1)[0])

    dx3, tok, sf1 = ffn_bwd(1, dx4, x3, h3, hu1, a1, f1, row(g_ffn_post, 1))
    dy1, dus, dg_mpost1, dgt_m1 = post_bwd_mm(dx3, y1, row(g_mix_post, 1) + tok, gt_m[1], wout_o, tm=tm, name="out_odd_bwd")
    dz1, dws, dbs, dlng, dlnb = sgu_bwd(z1, dus, ln_g, ln_b, sgu_w[0], bst, name="sgu_bwd")
    dx2, dg_mpre1, dsh_m1, dsc_m1 = mm_pre_bwd([dz1], win_o, x2, dx3, row(g_mix_pre, 1), sc_m[1], tm=tm, tk=D, name="in_odd_bwd")
    tok = scatter([wgrad([us], dy1, tr=2 * LANES, name="wgrad_out_odd"), wgrad([dz1], h2, tr=2 * LANES, name="wgrad_in_odd")],
                  "scatter_start_mix_1")

    dx1, tok, sf0 = ffn_bwd(0, dx2, x1, h1, hu0, a0, f0, row(g_ffn_post, 0) + tok)
    dy0, dpa, dg_mpost0, dgt_m0 = post_bwd_mm(dx1, y0, row(g_mix_post, 0) + tok, gt_m[0], wout_e, tm=tm, name="out_even_bwd")
    du, dwp, dps = pool_bwd(u, dpa, w_pool[0], pool_scale, name="pool_bwd")
    dq, dkv, dkvc, dsink = attn_bwd(q, kv, kvc, sink, dpa, cos, sa, sb, name="attn_bwd")
    dz0 = jnp.concatenate([du, dq, dkv], axis=1)
    dzc = jnp.concatenate([jnp.zeros((C, 8 * LANES), BF16), dkvc], axis=1)
    tok = scatter([permute_heads(wgrad([pa], dy0, tr=2 * LANES, name="wgrad_out_even"), inverse=True),
                   permute_heads(wgrad([dz0], h0, tr=2 * LANES, extra=(dzc, hc), name="wgrad_in_even"), inverse=True)],
                  "scatter_start_mix_0")
    grad_x, dg_mpre0, dsh_m0, dsc_m0 = mm_pre_bwd([dz0], win_e, x, dx1, row(g_mix_pre, 0) + tok, sc_m[0], tm=tm, tk=dz0.shape[1],
                                                  name="in_even_bwd")
    _, dg_mpre0c, dcsh, dcsc = mm_pre_bwd([dkvc], win_e, ctx, None, row(g_mix_pre, 0), csc_m, tm=C, tk=2 * LANES,
                                          w_row_off=8 * LANES, name="in_even_ctx_bwd")

    slots = exchange_wait(g_srcs, g_lands, g_sems, dcsh, scatter=True, name="scatter_wait")
    out = {}

    def update(name, idx, land, transposed):
        w_, m_, v_ = (a[idx].T if transposed else a[idx] for a in (P[name], M[name], V[name]))
        r = w_.shape[0]
        tr = r // 4 if r % 64 == 0 and r > 256 else r
        res = adamw(w_, m_, v_, land.reshape(N_DEV, r, land.shape[1]), tr=tr, name=f"adamw_{name}_{idx}")
        for kind, val in zip(("grad", "delta", "new_m", "new_v"), res):
            out.setdefault((kind, name), []).append(val.T if transposed else val)

    update("w_in_even", 0, slots[7], True)
    update("w_out_even", 0, slots[6], False)
    update("w_in_odd", 0, slots[3], True)
    update("w_out_odd", 0, slots[2], False)
    update("w_ffn_up", 0, slots[5], True)
    update("w_ffn_down", 0, slots[4], False)
    update("w_ffn_up", 1, slots[1], True)
    update("w_ffn_down", 1, slots[0], False)

    zero = jnp.zeros((1, D), F32)
    dmod0 = jnp.concatenate([dsh_m0, dsc_m0, dgt_m0, sf0["sh_f"], sf0["sc_f"], sf0["gt_f"]], axis=1)
    dmodc = jnp.concatenate([dcsh, dcsc, zero, zero, zero, zero], axis=1)
    dmod1 = jnp.concatenate([dsh_m1, dsc_m1, dgt_m1, sf1["sh_f"], sf1["sc_f"], sf1["gt_f"]], axis=1)
    dmods = jnp.concatenate([dmod0, dmodc, dmod1], axis=0)
    dmods_all = all_gather_small(dmods.reshape(-1, LANES), name="gather_dmods").reshape(N_DEV, 3, N_DEV, n_ada)
    dall = lax.dynamic_index_in_dim(dmods_all, me, 2, False)
    g_w_ada, dcc = ada_bwd_mm(silu_c, c_ctx[None, :], dall, w_ada, name="ada_bwd")
    nl = w_ada.shape[0]
    res = adamw(w_ada.reshape(nl * D, n_ada), m_w_ada.reshape(nl * D, n_ada), v_w_ada.reshape(nl * D, n_ada),
                g_w_ada.reshape(1, nl * D, n_ada), tr=nl * D // 8, name="adamw_w_ada")
    for kind, val in zip(("grad", "delta", "new_m", "new_v"), res):
        out[(kind, "w_ada")] = val.reshape(nl, D, n_ada)

    rep = dict(
        c_ctx=dcc[0],
        b_ada=jnp.stack([dmod0[0] + dmodc[0], dmod1[0]]),
        g_mix_pre=jnp.concatenate([dg_mpre0 + dg_mpre0c, dg_mpre1]),
        g_mix_post=jnp.concatenate([dg_mpost0, dg_mpost1]),
        g_ffn_pre=jnp.concatenate([sf0["g_ffn_pre"], sf1["g_ffn_pre"]]),
        g_ffn_post=jnp.concatenate([sf0["g_ffn_post"], sf1["g_ffn_post"]]),
        w_pool=dwp[None], pool_scale=dps, attn_sink=dsink[:, :N_Q_HEADS],
        sgu_w=dws[None], sgu_b=dbs[:, :sgu_b.shape[1]].T[None],
        ffn_conv_b=jnp.stack([sf0["ffn_conv_b"], sf1["ffn_conv_b"]]),
    )
    rep_names = list(rep)
    conv_g = jnp.stack([sf0["ffn_conv_w"], sf1["ffn_conv_w"]]).reshape(2, 3, N_DEV, n_cw).transpose(2, 0, 1, 3)
    shard_full = dict(sgu_ln_g=dlng.reshape(N_DEV, LANES), sgu_ln_b=dlnb.reshape(N_DEV, LANES),
                      ffn_conv_w=jnp.concatenate([_rows128(conv_g[d]) for d in range(N_DEV)], axis=0))
    pieces = [_rows128(rep[k]) for k in rep_names] + [shard_full[k] for k in shard_full]
    sizes = [p.shape[0] for p in pieces]
    _, gsum = all_gather_small(jnp.concatenate(pieces, axis=0), reduce=True, name="allreduce_small_grads")
    offs = [sum(sizes[:i]) for i in range(len(sizes))]
    n_rep = len(rep_names)
    cw_rows = sizes[-1] // N_DEV
    g_own = [gsum[offs[i]:offs[i] + sizes[i]] for i in range(n_rep)]
    g_own.append(_rows128(lax.dynamic_slice_in_dim(gsum, offs[n_rep] + me, 1, 0)))
    g_own.append(_rows128(lax.dynamic_slice_in_dim(gsum, offs[n_rep + 1] + me, 1, 0)))
    g_own.append(lax.dynamic_slice_in_dim(gsum, offs[n_rep + 2] + me * cw_rows, cw_rows, 0))
    small_names = rep_names + list(shard_full)
    packs = [jnp.concatenate([_rows128(src[k]) for k in small_names], axis=0) for src in (P, M, V)]
    n_pack = packs[0].shape[0]
    gp = jnp.concatenate(g_own, axis=0)[None]
    res = adamw(*packs, gp, tr=n_pack, name="adamw_small")
    o = 0
    for k in small_names:
        n = _rows128(P[k]).shape[0]
        for kind, val in zip(("grad", "delta", "new_m", "new_v"), res):
            out[(kind, k)] = val[o:o + n].reshape(-1)[:P[k].size].reshape(P[k].shape)
        o += n
    assert o == n_pack

    names = list(P)
    final = [loss, grad_x[None]]
    for kind in ("grad", "delta", "new_m", "new_v"):
        for k in names:
            val = out[(kind, k)]
            final.append(jnp.stack(val) if isinstance(val, list) else val)
    return tuple(final)
```

```python
import functools
import math

import jax
import jax.numpy as jnp
from jax import lax
from jax.experimental import pallas as pl
from jax.experimental.pallas import tpu as pltpu

F32 = jnp.float32
BF16 = jnp.bfloat16
MESH = pl.DeviceIdType.MESH
N_DEV = 8
LANES = 128
VMEM_LIMIT = 48 * 1024 * 1024
EPS = 1e-6
NEG_INF = -1e30
GRID_W = 64
WINDOW = 128
BLK = 128
HEAD_DIM = 64
N_Q_HEADS = 8
N_KV_HEADS = 2
GQA = N_Q_HEADS // N_KV_HEADS
POOL_WINDOWS = (2, 4, 8, 16)
ROPE_BASE = 10000.0
ROPE_FREQS = HEAD_DIM // 4
PAD = 16
ADAM_LR, ADAM_B1, ADAM_B2, ADAM_EPS, ADAM_WD, ADAM_STEP = 0.001, 0.9, 0.999, 1e-08, 0.01, 10
BC1 = 1.0 - ADAM_B1 ** ADAM_STEP
BC2 = 1.0 - ADAM_B2 ** ADAM_STEP
SQRT_2_OVER_PI = math.sqrt(2.0 / math.pi)
GELU_C = 0.044715


def _cp(sem=None):
    return pltpu.CompilerParams(dimension_semantics=sem, vmem_limit_bytes=VMEM_LIMIT)


def _dot(a, b):
    return jnp.dot(a, b, preferred_element_type=F32)


def _dot_nt(a, b):
    return lax.dot_general(a, b, (((1,), (1,)), ((), ())), preferred_element_type=F32)


def _dot_tn(a, b):
    return lax.dot_general(a, b, (((0,), (0,)), ((), ())), preferred_element_type=F32)


def _rms(x):
    r = lax.rsqrt(jnp.mean(x * x, axis=-1, keepdims=True) + EPS)
    return x * r, r


def _rms_bwd(dn, n, r):
    return r * (dn - n * jnp.mean(dn * n, axis=-1, keepdims=True))


def _colsum(a):
    return jnp.sum(a, axis=0, keepdims=True)


def _rope(x, c, sa, sb):
    return x * c + pltpu.roll(x, LANES - ROPE_FREQS, 1) * sa + pltpu.roll(x, ROPE_FREQS, 1) * sb


def _full(shape):
    return pl.BlockSpec(shape, lambda *_: (0,) * len(shape))


def pre_mm(x, g, sh, sc, wt, *, tm, tn, w_row_off=0, name):
    T, D = x.shape
    n_rows = wt.shape[0] - w_row_off
    off = w_row_off // tn

    def body(x_ref, g_ref, sh_ref, sc_ref, w_ref, h_ref, z_ref):
        @pl.when(pl.program_id(1) == 0)
        def _():
            n, _ = _rms(x_ref[...])
            h_ref[...] = (n * g_ref[...] * (1.0 + sc_ref[...]) + sh_ref[...]).astype(BF16)

        z_ref[...] = _dot_nt(h_ref[...], w_ref[...]).astype(BF16)

    vec = pl.BlockSpec((1, D), lambda i, j: (0, 0))
    return pl.pallas_call(
        body, name=name, grid=(T // tm, n_rows // tn),
        in_specs=[pl.BlockSpec((tm, D), lambda i, j: (i, 0)), vec, vec, vec, pl.BlockSpec((tn, D), lambda i, j: (j + off, 0))],
        out_specs=[pl.BlockSpec((tm, D), lambda i, j: (i, 0)), pl.BlockSpec((tm, tn), lambda i, j: (i, j))],
        out_shape=[jax.ShapeDtypeStruct((T, D), BF16), jax.ShapeDtypeStruct((T, n_rows), BF16)],
        compiler_params=_cp(("parallel", "arbitrary")),
    )(x, g, sh, sc, wt)


def inproj_even(x, g, sh, sc, wt, cos, sa, sb, *, tm, name):
    T, D = x.shape
    N = wt.shape[0]

    def body(x_ref, g_ref, sh_ref, sc_ref, w_ref, c_ref, sa_ref, sb_ref, h_ref, u_ref, q_ref, kv_ref):
        n, _ = _rms(x_ref[...])
        h = (n * g_ref[...] * (1.0 + sc_ref[...]) + sh_ref[...]).astype(BF16)
        h_ref[...] = h
        z = _dot_nt(h, w_ref[...])
        u_ref[...] = z[:, :4 * LANES]
        c, a, b = c_ref[...], sa_ref[...], sb_ref[...]
        for s in range(4):
            q_ref[:, s * LANES:(s + 1) * LANES] = _rope(z[:, (4 + s) * LANES:(5 + s) * LANES], c, a, b).astype(BF16)
        kv_ref[:, :LANES] = _rope(z[:, 8 * LANES:9 * LANES], c, a, b).astype(BF16)
        kv_ref[:, LANES:] = z[:, 9 * LANES:].astype(BF16)

    vec = pl.BlockSpec((1, D), lambda i: (0, 0))
    row = lambda w: pl.BlockSpec((tm, w), lambda i: (i, 0))
    return pl.pallas_call(
        body, name=name, grid=(T // tm,),
        in_specs=[row(D), vec, vec, vec, _full((N, D)), row(LANES), row(LANES), row(LANES)],
        out_specs=[row(D), row(4 * LANES), row(4 * LANES), row(2 * LANES)],
        out_shape=[jax.ShapeDtypeStruct((T, D), BF16), jax.ShapeDtypeStruct((T, 4 * LANES), F32),
                   jax.ShapeDtypeStruct((T, 4 * LANES), BF16), jax.ShapeDtypeStruct((T, 2 * LANES), BF16)],
        compiler_params=_cp(("parallel",)),
    )(x, g, sh, sc, wt, cos, sa, sb)


def mm_post(a, w, x, g, gt, *, tm, name):
    T, K = a.shape
    D = w.shape[1]

    def body(a_ref, w_ref, x_ref, g_ref, gt_ref, y_ref, xn_ref):
        y = _dot(a_ref[...], w_ref[...])
        n, _ = _rms(y)
        y_ref[...] = y
        xn_ref[...] = x_ref[...] + gt_ref[...] * (n * g_ref[...])

    vec = pl.BlockSpec((1, D), lambda i: (0, 0))
    row = lambda w_: pl.BlockSpec((tm, w_), lambda i: (i, 0))
    return pl.pallas_call(
        body, name=name, grid=(T // tm,),
        in_specs=[row(K), _full((K, D)), row(D), vec, vec],
        out_specs=[row(D), row(D)],
        out_shape=[jax.ShapeDtypeStruct((T, D), F32), jax.ShapeDtypeStruct((T, D), F32)],
        compiler_params=_cp(("parallel",)),
    )(a, w, x, g, gt)


def post_bwd_mm(dxn, y, g, gt, w, *, tm, name):
    T, D = y.shape
    K = w.shape[0]

    def body(dxn_ref, y_ref, g_ref, gt_ref, w_ref, dy_ref, da_ref, dg_ref, dgt_ref):
        @pl.when(pl.program_id(0) == 0)
        def _():
            dg_ref[...] = jnp.zeros_like(dg_ref)
            dgt_ref[...] = jnp.zeros_like(dgt_ref)

        d = dxn_ref[...]
        n, r = _rms(y_ref[...])
        g_, gt_ = g_ref[...], gt_ref[...]
        dg_ref[...] += _colsum(d * gt_ * n)
        dgt_ref[...] += _colsum(d * g_ * n)
        dy = _rms_bwd(d * (gt_ * g_), n, r).astype(BF16)
        dy_ref[...] = dy
        da_ref[...] = _dot_nt(dy, w_ref[...]).astype(BF16)

    vec = pl.BlockSpec((1, D), lambda i: (0, 0))
    row = lambda w_: pl.BlockSpec((tm, w_), lambda i: (i, 0))
    return pl.pallas_call(
        body, name=name, grid=(T // tm,),
        in_specs=[row(D), row(D), vec, vec, _full((K, D))],
        out_specs=[row(D), row(K), vec, vec],
        out_shape=[jax.ShapeDtypeStruct((T, D), BF16), jax.ShapeDtypeStruct((T, K), BF16),
                   jax.ShapeDtypeStruct((1, D), F32), jax.ShapeDtypeStruct((1, D), F32)],
        compiler_params=_cp(("arbitrary",)),
    )(dxn, y, g, gt, w)


def mm_pre_bwd(dzs, wt, x, dres, g, sc, *, tm, tk, w_row_off=0, name):
    T, N = dzs[0].shape
    D = x.shape[1]
    nk = N // tk
    npart = len(dzs)
    off = w_row_off // tk
    has_res = dres is not None

    def body(*refs):
        dz_refs = refs[:npart]
        w_refs = refs[npart:2 * npart]
        rest = refs[2 * npart:]
        x_ref = rest[0]
        dres_ref = rest[1] if has_res else None
        g_ref, sc_ref, dx_ref, dg_ref, dsh_ref, dsc_ref, acc = rest[1 + has_res:]
        i, k = pl.program_id(0), pl.program_id(1)

        @pl.when(jnp.logical_and(i == 0, k == 0))
        def _():
            dg_ref[...] = jnp.zeros_like(dg_ref)
            dsh_ref[...] = jnp.zeros_like(dsh_ref)
            dsc_ref[...] = jnp.zeros_like(dsc_ref)

        part = _dot(dz_refs[0][...], w_refs[0][...])
        for p in range(1, npart):
            part = part + _dot(dz_refs[p][...], w_refs[p][...])

        @pl.when(k == 0)
        def _():
            acc[...] = part

        @pl.when(k > 0)
        def _():
            acc[...] += part

        @pl.when(k == nk - 1)
        def _():
            dh = acc[...]
            n, r = _rms(x_ref[...])
            g_, s1 = g_ref[...], 1.0 + sc_ref[...]
            dsh_ref[...] += _colsum(dh)
            dsc_ref[...] += _colsum(dh * n * g_)
            dg_ref[...] += _colsum(dh * s1 * n)
            dxp = _rms_bwd(dh * (g_ * s1), n, r)
            dx_ref[...] = dxp + dres_ref[...] if has_res else dxp

    vec = pl.BlockSpec((1, D), lambda i, k: (0, 0))
    row = pl.BlockSpec((tm, D), lambda i, k: (i, 0))
    w_specs = [pl.BlockSpec((tk, D), (lambda i, k, p=p: (k + off + p * nk, 0))) for p in range(npart)]
    res_specs, res_args = ([row], (dres,)) if has_res else ([], ())
    return pl.pallas_call(
        body, name=name, grid=(T // tm, nk),
        in_specs=[pl.BlockSpec((tm, tk), lambda i, k: (i, k))] * npart + w_specs + [row] + res_specs + [vec, vec],
        out_specs=[row, vec, vec, vec],
        out_shape=[jax.ShapeDtypeStruct((T, D), F32)] + [jax.ShapeDtypeStruct((1, D), F32)] * 3,
        scratch_shapes=[pltpu.VMEM((tm, D), F32)],
        compiler_params=_cp(("arbitrary", "arbitrary")),
    )(*dzs, *([wt] * npart), x, *res_args, g, sc)


def wgrad(a_parts, b, *, tr, extra=None, name):
    T, R = a_parts[0].shape
    D = b.shape[1]
    npart = len(a_parts)
    nr = R // tr

    def body(*refs):
        a_refs, b_ref = refs[:npart], refs[npart]
        g_ref = refs[-1]
        for p in range(npart):
            @pl.when(pl.program_id(0) // nr == p)
            def _():
                acc = _dot_tn(a_refs[p][...], b_ref[...])
                if extra is not None:
                    acc += _dot_tn(refs[npart + 1][...], refs[npart + 2][...])
                g_ref[...] = acc.astype(BF16)

    in_specs = [pl.BlockSpec((T, tr), (lambda r, p=p: (0, jnp.clip(r - p * nr, 0, nr - 1)))) for p in range(npart)]
    in_specs.append(_full((T, D)))
    args = [*a_parts, b]
    if extra is not None:
        a2, b2 = extra
        in_specs += [pl.BlockSpec((a2.shape[0], tr), lambda r: (0, r)), _full(b2.shape)]
        args += [a2, b2]
    return pl.pallas_call(
        body, name=name, grid=(npart * nr,),
        in_specs=in_specs, out_specs=pl.BlockSpec((tr, D), lambda r: (r, 0)),
        out_shape=jax.ShapeDtypeStruct((npart * R, D), BF16),
        compiler_params=_cp(("parallel",)),
    )(*args)


def _conv_ext(ref, r0, rows, total):
    top = ref[pl.ds(pl.multiple_of(jnp.maximum(r0 - PAD, 0), PAD), PAD), :]
    mid = ref[pl.ds(r0, rows), :]
    bot = ref[pl.ds(pl.multiple_of(jnp.minimum(r0 + rows, total - PAD), PAD), PAD), :]
    top = jnp.where(r0 > 0, top, jnp.zeros_like(top))
    bot = jnp.where(r0 + rows < total, bot, jnp.zeros_like(bot))
    return jnp.concatenate([top, mid, bot], axis=0).astype(F32)


def _shift_rows(a, k):
    return pltpu.roll(a, k % a.shape[0], 0)


def _conv3(x, w, b):
    return w[0:1] * _shift_rows(x, 1) + w[1:2] * x + w[2:3] * _shift_rows(x, -1) + b


def _gate_up_specs(rows_, wblk, nb):
    return [pl.BlockSpec((rows_, wblk), lambda j: (0, j)), pl.BlockSpec((rows_, wblk), lambda j: (0, j + nb))]


def conv_fwd(hu, cw, cb, *, rows, wblk, name):
    L, N2 = hu.shape
    nb = N2 // 2 // wblk
    nchunk = L // rows

    def body(hg_ref, hu_ref, wg_ref, wu_ref, bg_ref, bu_ref, a_ref, s1_ref, s2_ref):
        def chunk(ci, carry):
            r0 = pl.multiple_of(ci * rows, rows)
            gate = _conv3(_conv_ext(hg_ref, r0, rows, L), wg_ref[...], bg_ref[...])[PAD:PAD + rows]
            up = _conv3(_conv_ext(hu_ref, r0, rows, L), wu_ref[...], bu_ref[...])[PAD:PAD + rows]
            sg = jax.nn.sigmoid(gate)
            silu = gate * sg
            at = pl.ds(r0, rows)
            a_ref[at, :] = (silu * up).astype(BF16)
            s1_ref[at, :] = silu.astype(BF16)
            s2_ref[at, :] = (up * (sg + silu * (1.0 - sg))).astype(BF16)
            return carry

        lax.fori_loop(0, nchunk, chunk, 0)

    out = pl.BlockSpec((L, wblk), lambda j: (0, j))
    return pl.pallas_call(
        body, name=name, grid=(nb,),
        in_specs=_gate_up_specs(L, wblk, nb) + _gate_up_specs(3, wblk, nb) + _gate_up_specs(1, wblk, nb),
        out_specs=[out] * 3, out_shape=[jax.ShapeDtypeStruct((L, N2 // 2), BF16)] * 3,
        compiler_params=_cp(("parallel",)),
    )(hu, hu, cw, cw, cb, cb)


def conv_bwd(da, s1, s2, hu, cw, *, rows, wblk, name):
    L, N2 = hu.shape
    F = N2 // 2
    nb = F // wblk
    nchunk = L // rows
    mid = slice(PAD, PAD + rows)

    def body(da_ref, s1_ref, s2_ref, hg_ref, hu_ref, wg_ref, wu_ref, dg_ref, du_ref, dwg_ref, dwu_ref, dbg_ref, dbu_ref):
        for ref in (dwg_ref, dwu_ref, dbg_ref, dbu_ref):
            ref[...] = jnp.zeros_like(ref)

        def half_bwd(x_ref, dh, w_ref, dx_ref, dw_ref, db_ref, r0):
            w = w_ref[...]
            nxt, prv = _shift_rows(dh, -1)[mid], _shift_rows(dh, 1)[mid]
            dhm, xm = dh[mid], x_ref[pl.ds(r0, rows), :].astype(F32)
            dx_ref[pl.ds(r0, rows), :] = (w[0:1] * nxt + w[1:2] * dhm + w[2:3] * prv).astype(BF16)
            db_ref[...] += _colsum(dhm)
            dw_ref[0:1, :] += _colsum(nxt * xm)
            dw_ref[1:2, :] += _colsum(dhm * xm)
            dw_ref[2:3, :] += _colsum(prv * xm)

        def chunk(ci, carry):
            r0 = pl.multiple_of(ci * rows, rows)
            d = _conv_ext(da_ref, r0, rows, L)
            half_bwd(hu_ref, d * _conv_ext(s1_ref, r0, rows, L), wu_ref, du_ref, dwu_ref, dbu_ref, r0)
            half_bwd(hg_ref, d * _conv_ext(s2_ref, r0, rows, L), wg_ref, dg_ref, dwg_ref, dbg_ref, r0)
            return carry

        lax.fori_loop(0, nchunk, chunk, 0)

    blk = lambda r: pl.BlockSpec((r, wblk), lambda j: (0, j))
    return pl.pallas_call(
        body, name=name, grid=(nb,),
        in_specs=[blk(L)] * 3 + _gate_up_specs(L, wblk, nb) + _gate_up_specs(3, wblk, nb),
        out_specs=[blk(L), blk(L), blk(3), blk(3), blk(1), blk(1)],
        out_shape=[jax.ShapeDtypeStruct((L, F), BF16)] * 2 + [jax.ShapeDtypeStruct((3, F), F32)] * 2
        + [jax.ShapeDtypeStruct((1, F), F32)] * 2,
        compiler_params=_cp(("parallel",)),
    )(da, s1, s2, hu, hu, cw, cw)


def _window_sums(pad_ref, w, lead):
    a = pad_ref[...]
    k = 1
    while k < w:
        a = a + _shift_rows(a, -k)
        k *= 2
    return _shift_rows(a, lead) if lead else a


def _pool_counts(L, h):
    t = lax.broadcasted_iota(jnp.int32, (L, 1), 0)
    return (jnp.minimum(t + h, L) - jnp.maximum(t - h, 0)).astype(F32)


def _pooled(u_ref, pad_ref, L, w):
    h = w // 2
    pad_ref[pl.ds(PAD, L), :] = u_ref[...]
    win = _window_sums(pad_ref, w, h)[PAD:PAD + L]
    return win / _pool_counts(L, h) - u_ref[...]


def _zero_pad_edges(pad_ref, L):
    z = jnp.zeros((PAD, LANES), F32)
    pad_ref[pl.ds(0, PAD), :] = z
    pad_ref[pl.ds(PAD + L, PAD), :] = z


def pool_fwd(u, w_pool, pool_scale, *, name):
    L = u.shape[0]

    def body(u_ref, w_ref, ps_ref, p_ref, pad_ref):
        _zero_pad_edges(pad_ref, L)
        for gi, win in enumerate(POOL_WINDOWS):
            @pl.when(pl.program_id(0) == gi)
            def _():
                pooled = _pooled(u_ref, pad_ref, L, win)
                p_ref[...] = (_dot(pooled.astype(BF16), w_ref[...].astype(BF16)) * ps_ref[...]).astype(BF16)

    return pl.pallas_call(
        body, name=name, grid=(len(POOL_WINDOWS),),
        in_specs=[pl.BlockSpec((L, LANES), lambda gi: (0, gi)), pl.BlockSpec((None, LANES, LANES), lambda gi: (gi, 0, 0)),
                  pl.BlockSpec((1, LANES), lambda gi: (0, gi))],
        out_specs=pl.BlockSpec((L, LANES), lambda gi: (0, gi)),
        out_shape=jax.ShapeDtypeStruct((L, 4 * LANES), BF16),
        scratch_shapes=[pltpu.VMEM((L + 2 * PAD, LANES), F32)],
        compiler_params=_cp(("parallel",)),
    )(u, w_pool, pool_scale)


def pool_bwd(u, dpa, w_pool, pool_scale, *, name):
    L = u.shape[0]

    def body(u_ref, dp_ref, w_ref, ps_ref, du_ref, dw_ref, dps_ref, pad_ref):
        _zero_pad_edges(pad_ref, L)
        for gi, win in enumerate(POOL_WINDOWS):
            @pl.when(pl.program_id(0) == gi)
            def _():
                h = win // 2
                wb = w_ref[...].astype(BF16)
                pooled = _pooled(u_ref, pad_ref, L, win).astype(BF16)
                dp = dp_ref[...].astype(F32)
                dps_ref[...] = _colsum(dp * _dot(pooled, wb))
                dy = (dp * ps_ref[...]).astype(BF16)
                dw_ref[...] = _dot_tn(pooled, dy)
                dpooled = _dot_nt(dy, wb)
                pad_ref[pl.ds(PAD, L), :] = dpooled / _pool_counts(L, h)
                du_ref[...] = (_window_sums(pad_ref, win, h - 1)[PAD:PAD + L] - dpooled).astype(BF16)

    return pl.pallas_call(
        body, name=name, grid=(len(POOL_WINDOWS),),
        in_specs=[pl.BlockSpec((L, LANES), lambda gi: (0, gi)), pl.BlockSpec((L, LANES), lambda gi: (0, gi)),
                  pl.BlockSpec((None, LANES, LANES), lambda gi: (gi, 0, 0)), pl.BlockSpec((1, LANES), lambda gi: (0, gi))],
        out_specs=[pl.BlockSpec((L, LANES), lambda gi: (0, gi)), pl.BlockSpec((None, LANES, LANES), lambda gi: (gi, 0, 0)),
                   pl.BlockSpec((1, LANES), lambda gi: (0, gi))],
        out_shape=[jax.ShapeDtypeStruct((L, 4 * LANES), BF16), jax.ShapeDtypeStruct((4, LANES, LANES), F32),
                   jax.ShapeDtypeStruct((1, 4 * LANES), F32)],
        scratch_shapes=[pltpu.VMEM((L + 2 * PAD, LANES), F32)],
        compiler_params=_cp(("parallel",)),
    )(u, dpa, w_pool, pool_scale)


def _attn_probs(qk, band_k, ctx_k, sink_ref, kh, mask4):
    s_loc = jnp.where(mask4, _dot_nt(qk, band_k), NEG_INF)
    s_ctx = _dot_nt(qk, ctx_k)
    sk = jnp.concatenate([jnp.full((BLK, 1), sink_ref[kh * GQA + hh], F32) for hh in range(GQA)], axis=0)
    m = jnp.maximum(jnp.maximum(jnp.max(s_loc, axis=-1, keepdims=True), jnp.max(s_ctx, axis=-1, keepdims=True)), sk)
    e_loc, e_ctx, e_s = jnp.exp(s_loc - m), jnp.exp(s_ctx - m), jnp.exp(sk - m)
    inv = 1.0 / (jnp.sum(e_loc, axis=-1, keepdims=True) + jnp.sum(e_ctx, axis=-1, keepdims=True) + e_s)
    return e_loc * inv, e_ctx * inv, e_s * inv


def _attn_block(n, L):
    start = pl.multiple_of(jnp.clip((n - 1) * BLK, 0, L - 3 * BLK), BLK)
    qpos = n * BLK + lax.broadcasted_iota(jnp.int32, (BLK, 3 * BLK), 0)
    kpos = start + lax.broadcasted_iota(jnp.int32, (BLK, 3 * BLK), 1)
    mask = jnp.abs(kpos - qpos) <= WINDOW
    return start, jnp.concatenate([mask] * GQA, axis=0)


def _stack_slabs(ref):
    return jnp.concatenate([ref[:, s * LANES:(s + 1) * LANES] for s in range(GQA)], axis=0)


def _kv_head_lanes(kh):
    return (lax.broadcasted_iota(jnp.int32, (1, LANES), 1) // HEAD_DIM) == kh


def permute_heads(w, inverse=False):
    lo, hi = 4 * LANES, 8 * LANES
    mid = w[lo:hi].reshape(*((GQA, N_KV_HEADS) if inverse else (N_KV_HEADS, GQA)), HEAD_DIM, w.shape[1])
    return jnp.concatenate([w[:lo], mid.swapaxes(0, 1).reshape(hi - lo, w.shape[1]), w[hi:]], axis=0)


def attn_fwd(q, kv, kvc, sink, *, name):
    L = q.shape[0]
    C = kvc.shape[0]
    scale = HEAD_DIM ** -0.5

    def body(q_ref, kv_ref, kvc_ref, sink_ref, o_ref):
        start, mask4 = _attn_block(pl.program_id(0), L)
        band = kv_ref[pl.ds(start, 3 * BLK), :]
        kvc_ = kvc_ref[...]
        qs = _stack_slabs(q_ref) * scale
        o = jnp.zeros((GQA * BLK, LANES), F32)
        for kh in range(N_KV_HEADS):
            grp = _kv_head_lanes(kh)
            qk = jnp.where(grp, qs, jnp.zeros_like(qs))
            p_loc, p_ctx, _ = _attn_probs(qk, band[:, :LANES], kvc_[:, :LANES], sink_ref, kh, mask4)
            o = o + jnp.where(grp, _dot(p_loc.astype(BF16), band[:, LANES:]) + _dot(p_ctx.astype(BF16), kvc_[:, LANES:]), 0.0)
        for s in range(GQA):
            o_ref[:, s * LANES:(s + 1) * LANES] = o[s * BLK:(s + 1) * BLK].astype(BF16)

    return pl.pallas_call(
        body, name=name, grid=(L // BLK,),
        in_specs=[pl.BlockSpec((BLK, 4 * LANES), lambda n: (n, 0)), _full((L, 2 * LANES)), _full((C, 2 * LANES)),
                  pl.BlockSpec(memory_space=pltpu.SMEM)],
        out_specs=pl.BlockSpec((BLK, 4 * LANES), lambda n: (n, 0)),
        out_shape=jax.ShapeDtypeStruct((L, 4 * LANES), BF16),
        compiler_params=_cp(("parallel",)),
    )(q, kv, kvc, sink)


def attn_bwd(q, kv, kvc, sink, dpa, cos, sa, sb, *, name):
    L = q.shape[0]
    C = kvc.shape[0]
    nb = L // BLK
    scale = HEAD_DIM ** -0.5

    def body(q_ref, kv_ref, kvc_ref, sink_ref, do_ref, c_ref, sa_ref, sb_ref, cq_ref, saq_ref, sbq_ref,
             dq_ref, dkv_ref, dkvc_ref, dsink_ref, dkv_acc, dkvc_acc):
        n = pl.program_id(0)

        @pl.when(n == 0)
        def _():
            dkv_acc[...] = jnp.zeros_like(dkv_acc)
            dkvc_acc[...] = jnp.zeros_like(dkvc_acc)
            dsink_ref[...] = jnp.zeros_like(dsink_ref)

        start, mask4 = _attn_block(n, L)
        band = kv_ref[pl.ds(start, 3 * BLK), :]
        kvc_ = kvc_ref[...]
        band_k, band_v, ctx_k, ctx_v = band[:, :LANES], band[:, LANES:], kvc_[:, :LANES], kvc_[:, LANES:]
        qs = _stack_slabs(q_ref) * scale
        dos = _stack_slabs(do_ref)
        lane = lax.broadcasted_iota(jnp.int32, (1, LANES), 1)
        dsink = jnp.zeros((1, LANES), F32)
        dq = jnp.zeros((GQA * BLK, LANES), F32)
        dk = jnp.zeros((3 * BLK, LANES), F32)
        dv = jnp.zeros((3 * BLK, LANES), F32)
        dkc = jnp.zeros((C, LANES), F32)
        dvc = jnp.zeros((C, LANES), F32)
        for kh in range(N_KV_HEADS):
            grp = _kv_head_lanes(kh)
            qk = jnp.where(grp, qs, jnp.zeros_like(qs))
            dok = jnp.where(grp, dos, jnp.zeros_like(dos))
            p_loc, p_ctx, p_s = _attn_probs(qk, band_k, ctx_k, sink_ref, kh, mask4)
            dp_loc = _dot_nt(dok, band_v)
            dp_ctx = _dot_nt(dok, ctx_v)
            delta = jnp.sum(p_loc * dp_loc, axis=-1, keepdims=True) + jnp.sum(p_ctx * dp_ctx, axis=-1, keepdims=True)
            ds_loc = (p_loc * (dp_loc - delta)).astype(BF16)
            ds_ctx = (p_ctx * (dp_ctx - delta)).astype(BF16)
            dsk = p_s * delta
            for hh in range(GQA):
                dsink = dsink - jnp.where(lane == kh * GQA + hh, jnp.sum(dsk[hh * BLK:(hh + 1) * BLK], axis=0, keepdims=True), 0.0)
            dq = dq + jnp.where(grp, _dot(ds_loc, band_k) + _dot(ds_ctx, ctx_k), 0.0)
            dk = dk + _dot_tn(ds_loc, qk)
            dv = dv + _dot_tn(p_loc.astype(BF16), dok)
            dkc = dkc + _dot_tn(ds_ctx, qk)
            dvc = dvc + _dot_tn(p_ctx.astype(BF16), dok)
        dsink_ref[...] += dsink
        dkv_acc[pl.ds(start, 3 * BLK), :LANES] += dk
        dkv_acc[pl.ds(start, 3 * BLK), LANES:] += dv
        dkvc_acc[:, :LANES] += dkc
        dkvc_acc[:, LANES:] += dvc
        c, a, b = cq_ref[...], -saq_ref[...], -sbq_ref[...]
        for s in range(GQA):
            dq_ref[:, s * LANES:(s + 1) * LANES] = _rope(dq[s * BLK:(s + 1) * BLK] * scale, c, a, b).astype(BF16)

        @pl.when(n == nb - 1)
        def _():
            dkv_ref[:, :LANES] = _rope(dkv_acc[:, :LANES], c_ref[...], -sa_ref[...], -sb_ref[...]).astype(BF16)
            dkv_ref[:, LANES:] = dkv_acc[:, LANES:].astype(BF16)
            dkvc_ref[...] = dkvc_acc[...].astype(BF16)

    blk = lambda w: pl.BlockSpec((BLK, w), lambda n: (n, 0))
    return pl.pallas_call(
        body, name=name, grid=(nb,),
        in_specs=[blk(4 * LANES), _full((L, 2 * LANES)), _full((C, 2 * LANES)), pl.BlockSpec(memory_space=pltpu.SMEM),
                  pl.BlockSpec((BLK, 4 * LANES), lambda n: (n, 1)),
                  _full((L, LANES)), _full((L, LANES)), _full((L, LANES)), blk(LANES), blk(LANES), blk(LANES)],
        out_specs=[blk(4 * LANES), _full((L, 2 * LANES)), _full((C, 2 * LANES)), _full((1, LANES))],
        out_shape=[jax.ShapeDtypeStruct((L, 4 * LANES), BF16), jax.ShapeDtypeStruct((L, 2 * LANES), BF16),
                   jax.ShapeDtypeStruct((C, 2 * LANES), BF16), jax.ShapeDtypeStruct((1, LANES), F32)],
        scratch_shapes=[pltpu.VMEM((L, 2 * LANES), F32), pltpu.VMEM((C, 2 * LANES), F32)],
        compiler_params=_cp(("arbitrary",)),
    )(q, kv, kvc, sink, dpa, cos, sa, sb, cos, sa, sb)


def _gelu_parts(x):
    th = jnp.tanh(SQRT_2_OVER_PI * (x + GELU_C * x * x * x))
    return 0.5 * x * (1.0 + th), th


def _gelu_grad(x, th):
    return 0.5 * (1.0 + th) + 0.5 * x * (1.0 - th * th) * SQRT_2_OVER_PI * (1.0 + 3.0 * GELU_C * x * x)


def _layernorm(v):
    mu = jnp.mean(v, axis=-1, keepdims=True)
    vc = v - mu
    rstd = lax.rsqrt(jnp.mean(vc * vc, axis=-1, keepdims=True) + EPS)
    return vc * rstd, rstd


def sgu_fwd(z1, ln_g, ln_b, ws, bst, *, name):
    L, W2 = z1.shape
    W = W2 // 2
    ng = W // LANES

    def body(z_ref, g_ref, b_ref, ws_ref, bs_ref, o_ref):
        z, _ = _gelu_parts(z_ref[...].astype(F32))
        xhat, _ = _layernorm(z[:, W:])
        vln = (xhat * g_ref[...] + b_ref[...]).astype(BF16)
        for gi in range(ng):
            cs = slice(gi * LANES, (gi + 1) * LANES)
            s = _dot(ws_ref[gi].astype(BF16), vln[:, cs]) + bs_ref[:, gi:gi + 1]
            o_ref[:, cs] = (z[:, cs] * s).astype(BF16)

    vec = _full((1, W))
    return pl.pallas_call(
        body, name=name, grid=(L // BLK,),
        in_specs=[pl.BlockSpec((BLK, W2), lambda n: (n, 0)), vec, vec, _full((ng, LANES, LANES)), _full((BLK, ng))],
        out_specs=pl.BlockSpec((BLK, W), lambda n: (n, 0)),
        out_shape=jax.ShapeDtypeStruct((L, W), BF16),
        compiler_params=_cp(("parallel",)),
    )(z1, ln_g, ln_b, ws, bst)


def sgu_bwd(z1, dus, ln_g, ln_b, ws, bst, *, name):
    L, W2 = z1.shape
    W = W2 // 2
    ng = W // LANES

    def body(z_ref, d_ref, g_ref, b_ref, ws_ref, bs_ref, dz_ref, dws_ref, dbs_ref, dg_ref, db_ref, dv_scr):
        @pl.when(pl.program_id(0) == 0)
        def _():
            dws_ref[...] = jnp.zeros_like(dws_ref)
            dbs_ref[...] = jnp.zeros_like(dbs_ref)
            dg_ref[...] = jnp.zeros_like(dg_ref)
            db_ref[...] = jnp.zeros_like(db_ref)

        zp = z_ref[...].astype(F32)
        z, th = _gelu_parts(zp)
        xhat, rstd = _layernorm(z[:, W:])
        vln = (xhat * g_ref[...] + b_ref[...]).astype(BF16)
        d = d_ref[...].astype(F32)
        lane = lax.broadcasted_iota(jnp.int32, (1, LANES), 1)
        dbs = jnp.zeros((BLK, LANES), F32)
        dgel = _gelu_grad(zp, th)
        for gi in range(ng):
            cs = slice(gi * LANES, (gi + 1) * LANES)
            wb = ws_ref[gi].astype(BF16)
            s = _dot(wb, vln[:, cs]) + bs_ref[:, gi:gi + 1]
            dz_ref[:, cs] = (d[:, cs] * s * dgel[:, cs]).astype(BF16)
            ds = d[:, cs] * z[:, cs]
            dbs = dbs + jnp.where(lane == gi, jnp.sum(ds, axis=-1, keepdims=True), 0.0)
            dsb = ds.astype(BF16)
            dws_ref[gi] += _dot_nt(dsb, vln[:, cs])
            dv_scr[:, cs] = _dot_tn(wb, dsb)
        dbs_ref[...] += dbs
        dvln = dv_scr[...]
        dg_ref[...] += _colsum(dvln * xhat)
        db_ref[...] += _colsum(dvln)
        dxh = dvln * g_ref[...]
        dv = rstd * (dxh - jnp.mean(dxh, axis=-1, keepdims=True) - xhat * jnp.mean(dxh * xhat, axis=-1, keepdims=True))
        dz_ref[:, W:] = (dv * dgel[:, W:]).astype(BF16)

    vec = _full((1, W))
    return pl.pallas_call(
        body, name=name, grid=(L // BLK,),
        in_specs=[pl.BlockSpec((BLK, W2), lambda n: (n, 0)), pl.BlockSpec((BLK, W), lambda n: (n, 0)), vec, vec,
                  _full((ng, LANES, LANES)), _full((BLK, ng))],
        out_specs=[pl.BlockSpec((BLK, W2), lambda n: (n, 0)), _full((ng, LANES, LANES)), _full((BLK, LANES)), vec, vec],
        out_shape=[jax.ShapeDtypeStruct((L, W2), BF16), jax.ShapeDtypeStruct((ng, LANES, LANES), F32),
                   jax.ShapeDtypeStruct((BLK, LANES), F32), jax.ShapeDtypeStruct((1, W), F32), jax.ShapeDtypeStruct((1, W), F32)],
        scratch_shapes=[pltpu.VMEM((BLK, W), F32)],
        compiler_params=_cp(("arbitrary",)),
    )(z1, dus, ln_g, ln_b, ws, bst)


def loss_grad(xo, target, *, tm, name):
    T, D = xo.shape

    def body(x_ref, t_ref, l_ref, d_ref):
        @pl.when(pl.program_id(0) == 0)
        def _():
            l_ref[...] = jnp.zeros_like(l_ref)

        e = x_ref[...] - t_ref[...]
        l_ref[...] += 0.5 * jnp.sum(jnp.mean(e * e, axis=-1, keepdims=True), axis=0, keepdims=True)
        d_ref[...] = e * (1.0 / D)

    row = pl.BlockSpec((tm, D), lambda i: (i, 0))
    return pl.pallas_call(
        body, name=name, grid=(T // tm,), in_specs=[row, row], out_specs=[_full((1, 1)), row],
        out_shape=[jax.ShapeDtypeStruct((1, 1), F32), jax.ShapeDtypeStruct((T, D), F32)],
        compiler_params=_cp(("arbitrary",)),
    )(xo, target)


def adamw(w, m, v, gparts, *, tr, name):
    NL, R, Wd = w.shape
    nr = R // tr

    def body(w_ref, m_ref, v_ref, *rest):
        gp_refs, (g_ref, d_ref, nm_ref, nv_ref) = rest[:NL], rest[NL:]
        for l in range(NL):
            @pl.when(pl.program_id(0) == l)
            def _():
                g = gp_refs[l][0].astype(F32)
                for s in range(1, gp_refs[l].shape[0]):
                    g = g + gp_refs[l][s].astype(F32)
                m_ = ADAM_B1 * m_ref[...] + (1.0 - ADAM_B1) * g
                v_ = ADAM_B2 * v_ref[...] + (1.0 - ADAM_B2) * (g * g)
                g_ref[...] = g
                nm_ref[...] = m_
                nv_ref[...] = v_
                d_ref[...] = -ADAM_LR * ((m_ / BC1) / (jnp.sqrt(v_ / BC2) + ADAM_EPS) + ADAM_WD * w_ref[...])

    row = pl.BlockSpec((None, tr, Wd), lambda l, i: (l, i, 0))
    gspecs = [pl.BlockSpec((gparts[l].shape[0], tr, Wd), (lambda l_, i, l=l: (0, jnp.clip(i + (l_ - l) * nr, 0, nr - 1), 0)))
              for l in range(NL)]
    return pl.pallas_call(
        body, name=name, grid=(NL, nr),
        in_specs=[row, row, row] + gspecs, out_specs=[row] * 4, out_shape=[jax.ShapeDtypeStruct((NL, R, Wd), F32)] * 4,
        compiler_params=_cp(("arbitrary", "arbitrary")),
    )(w, m, v, *gparts)


def ada_fwd_mm(cs, w_ada, b_loc, *, name):
    R, D = cs.shape
    nl, _, n = w_ada.shape

    def body(c_ref, w_ref, b_ref, s_ref, m_ref):
        c = c_ref[...]
        s = c * jax.nn.sigmoid(c)
        s_ref[...] = s
        for i in range(nl):
            m_ref[i] = _dot(s.astype(BF16), w_ref[i].astype(BF16)) + b_ref[i:i + 1, :]

    return pl.pallas_call(
        body, name=name, in_specs=[_full((R, D)), _full((nl, D, n)), _full((nl, n))],
        out_specs=[_full((R, D)), _full((nl, R, n))], grid=(1,),
        out_shape=[jax.ShapeDtypeStruct((R, D), F32), jax.ShapeDtypeStruct((nl, R, n), F32)],
        compiler_params=_cp(("arbitrary",)),
    )(cs, w_ada, b_loc)


def ada_bwd_mm(s, c_ctx, dall, w_ada, *, name):
    R, D = s.shape
    nl, _, n = w_ada.shape

    def body(s_ref, cc_ref, d_ref, w_ref, gw_ref, dcc_ref):
        sb = s_ref[...].astype(BF16)
        row = lax.broadcasted_iota(jnp.int32, (R, 1), 0)
        dctx = d_ref[0, 1:2, :]
        for dv in range(1, N_DEV):
            dctx = dctx + d_ref[dv, 1:2, :]
        for i in range(nl):
            dm = jnp.zeros((R, n), F32)
            for dv in range(N_DEV):
                dm = dm + jnp.where(row == dv, d_ref[dv, 2 * i:2 * i + 1, :], 0.0)
            if i == 0:
                dm = dm + jnp.where(row == N_DEV, dctx, 0.0)
            gw_ref[i] = _dot_tn(sb, dm.astype(BF16))
        cc = cc_ref[...]
        sg = jax.nn.sigmoid(cc)
        ds = _dot_nt(jnp.broadcast_to(dctx, (8, n)).astype(BF16), w_ref[0].astype(BF16))
        dcc_ref[...] = ds * (sg * (1.0 + cc * (1.0 - sg)))

    return pl.pallas_call(
        body, name=name, grid=(1,),
        in_specs=[_full((R, D)), _full((1, D)), _full((N_DEV, 3, n)), _full((nl, D, n))],
        out_specs=[_full((nl, D, n)), _full((8, D))],
        out_shape=[jax.ShapeDtypeStruct((nl, D, n), F32), jax.ShapeDtypeStruct((8, D), F32)],
        compiler_params=_cp(("arbitrary",)),
    )(s, c_ctx, dall, w_ada)


def _place():
    x, y, c = lax.axis_index("x"), lax.axis_index("y"), lax.axis_index("c")
    return x, y, c


def _lin(p):
    return 4 * p[0] + 2 * p[1] + p[2]


def all_gather_small(xb, *, reduce=False, name):
    R, W = xb.shape

    def body(x_ref, *rest):
        out_ref = rest[0]
        send_sems, recv_sems, local_sem = rest[-3:]
        x, y, c = _place()
        me, sibling = (x, y, c), (x, y, 1 - c)
        chips = [(1 - x, y), (x, 1 - y), (1 - x, 1 - y)]

        def copy(k, block, to, src=None):
            dst = out_ref.at[_lin(block)]
            return pltpu.make_async_remote_copy(
                src_ref=dst if src is None else src, dst_ref=dst, send_sem=send_sems.at[k], recv_sem=recv_sems.at[k],
                device_id=to, device_id_type=MESH)

        mine = pltpu.make_async_copy(x_ref, out_ref.at[_lin(me)], local_sem)
        mine.start()
        first = [copy(0, me, sibling, src=x_ref)]
        first += [copy(1 + j, me, (*chip, c), src=x_ref) for j, chip in enumerate(chips)]
        for cp in first:
            cp.start()
        passed = [copy(4 + j, (*chip, c), sibling) for j, chip in enumerate(chips)]
        for j, chip in enumerate(chips):
            copy(1 + j, (*chip, c), me).wait_recv()
            passed[j].start()
        copy(0, sibling, me).wait_recv()
        for j, chip in enumerate(chips):
            copy(4 + j, (*chip, 1 - c), me).wait_recv()
        for cp in first + passed:
            cp.wait_send()
        mine.wait()
        if reduce:
            acc = out_ref[0].astype(F32)
            for dv in range(1, N_DEV):
                acc = acc + out_ref[dv].astype(F32)
            rest[1][...] = acc

    vm = pl.BlockSpec(memory_space=pltpu.VMEM)
    out_shape = [jax.ShapeDtypeStruct((N_DEV, R, W), xb.dtype)]
    if reduce:
        out_shape.append(jax.ShapeDtypeStruct((R, W), F32))
    res = pl.pallas_call(
        body, name=name, in_specs=[vm], out_specs=[vm] * len(out_shape), out_shape=out_shape,
        scratch_shapes=[pltpu.SemaphoreType.DMA((7,)), pltpu.SemaphoreType.DMA((7,)), pltpu.SemaphoreType.DMA],
        compiler_params=pltpu.CompilerParams(vmem_limit_bytes=VMEM_LIMIT),
    )(xb)
    return res if reduce else res[0]


HBM_SPEC = pl.BlockSpec(memory_space=pltpu.HBM)
SEM_SPEC = pl.BlockSpec(memory_space=pltpu.SEMAPHORE)
ORDERED_EFFECT = pltpu.SideEffectType.DATAFLOW_SIDE_EFFECTING


def _exchange_copies(srcs, lands, sems, scatter):
    x, y, c = _place()
    me = _lin((x, y, c))
    for j in range(len(srcs)):
        r = lands[j].shape[0] // N_DEV
        block = lambda d, j=j, r=r: pl.ds(pl.multiple_of(d * r, 16), r)
        for k in range(1, N_DEV):
            peer = (x ^ (k >> 2), y ^ ((k >> 1) & 1), c ^ (k & 1))
            src = srcs[j].at[block(_lin(peer)), :] if scatter else srcs[j]
            mk = lambda dst, j=j, k=k, peer=peer, src=src: pltpu.make_async_remote_copy(
                src_ref=src, dst_ref=dst, send_sem=sems[2 * j].at[k - 1], recv_sem=sems[2 * j + 1].at[k - 1],
                device_id=peer, device_id_type=MESH)
            yield mk(lands[j].at[block(me), :]), mk(lands[j].at[block(_lin(peer)), :])


def exchange_start(srcs, lands, *, scatter, name):
    nw = len(srcs)

    def body(*refs):
        for start, _ in _exchange_copies(refs[:nw], refs[nw:2 * nw], refs[2 * nw:4 * nw], scatter):
            start.start()
        refs[-1][...] = jnp.zeros_like(refs[-1])

    thru = [pltpu.HBM(a.shape, a.dtype) for a in (*srcs, *lands)]
    res = pl.pallas_call(
        body, name=name, in_specs=[HBM_SPEC] * (2 * nw),
        out_specs=[SEM_SPEC] * (2 * nw) + [HBM_SPEC] * (2 * nw) + [pl.BlockSpec(memory_space=pltpu.VMEM)],
        out_shape=[pltpu.SemaphoreType.DMA((N_DEV - 1,))] * (2 * nw) + thru + [jax.ShapeDtypeStruct((8, LANES), F32)],
        input_output_aliases={i: 2 * nw + i for i in range(2 * nw)},
        compiler_params=pltpu.CompilerParams(has_side_effects=ORDERED_EFFECT),
    )(*[pltpu.with_memory_space_constraint(a, pltpu.HBM) for a in (*srcs, *lands)])
    return res[:2 * nw], res[2 * nw:3 * nw], res[3 * nw:4 * nw], res[-1]


def exchange_wait(srcs, lands, sems, after, *, scatter, name):
    nw = len(srcs)

    def body(*refs):
        for _, arrive in _exchange_copies(refs[:nw], refs[nw:2 * nw], refs[2 * nw:4 * nw], scatter):
            arrive.wait_send()
            arrive.wait_recv()

    res = pl.pallas_call(
        body, name=name, in_specs=[HBM_SPEC] * (2 * nw) + [SEM_SPEC] * (2 * nw) + [pl.BlockSpec(memory_space=pl.ANY)],
        out_specs=[HBM_SPEC] * (2 * nw), out_shape=[pltpu.HBM(a.shape, a.dtype) for a in (*srcs, *lands)],
        input_output_aliases={i: i for i in range(2 * nw)},
        compiler_params=pltpu.CompilerParams(has_side_effects=ORDERED_EFFECT),
    )(*srcs, *lands, *sems, after)
    return res[nw:]


def place_own(srcs, rows, me, *, scatter, name):
    nw = len(srcs)
    lands = [lax.empty((N_DEV * r, s_.shape[1]), s_.dtype) for r, s_ in zip(rows, srcs)]

    def body(me_ref, *refs):
        for j in range(nw):
            refs[2 * nw + j][...] = refs[j][...]

    mine = lambda i, me_ref: (me_ref[0], 0)
    src_at = mine if scatter else (lambda i, me_ref: (0, 0))
    blocks = [(r, s_.shape[1]) for r, s_ in zip(rows, srcs)]
    return pl.pallas_call(
        body, name=name,
        grid_spec=pltpu.PrefetchScalarGridSpec(
            num_scalar_prefetch=1, grid=(1,),
            in_specs=[pl.BlockSpec(b_, src_at) for b_ in blocks] + [pl.BlockSpec(memory_space=pl.ANY)] * nw,
            out_specs=[pl.BlockSpec(b_, mine) for b_ in blocks]),
        out_shape=[jax.ShapeDtypeStruct(l_.shape, l_.dtype) for l_ in lands],
        input_output_aliases={1 + nw + j: j for j in range(nw)},
        compiler_params=_cp(("arbitrary",)),
    )(jnp.reshape(me, (1,)).astype(jnp.int32), *srcs, *lands)


def _rope_tables(L):
    t = jnp.arange(L)
    inv = ROPE_BASE ** (-jnp.arange(ROPE_FREQS, dtype=F32) / ROPE_FREQS)
    ar = (t // GRID_W).astype(F32)[:, None] * inv
    ac = (t % GRID_W).astype(F32)[:, None] * inv
    z = jnp.zeros_like(ar)
    cos = jnp.concatenate([jnp.cos(ar), jnp.cos(ar), jnp.cos(ac), jnp.cos(ac)], axis=1)
    sa = jnp.concatenate([-jnp.sin(ar), z, -jnp.sin(ac), z], axis=1)
    sb = jnp.concatenate([z, jnp.sin(ar), z, jnp.sin(ac)], axis=1)
    return tuple(jnp.tile(a, (1, LANES // HEAD_DIM)) for a in (cos, sa, sb))


def _rows128(a):
    f = a.reshape(-1)
    n = -(-f.shape[0] // (8 * LANES)) * 8 * LANES
    return jnp.pad(f, (0, n - f.shape[0])).reshape(-1, LANES)


def kernel(x, c, ctx, c_ctx, w_ada, b_ada, g_mix_pre, g_mix_post, g_ffn_pre, g_ffn_post, w_in_even, w_pool, pool_scale, attn_sink, w_out_even, w_in_odd, sgu_ln_g, sgu_ln_b, sgu_w, sgu_b, w_out_odd, w_ffn_up, ffn_conv_w, ffn_conv_b, w_ffn_down, loss_target, m_c_ctx, m_w_ada, m_b_ada, m_g_mix_pre, m_g_mix_post, m_g_ffn_pre, m_g_ffn_post, m_w_in_even, m_w_pool, m_pool_scale, m_attn_sink, m_w_out_even, m_w_in_odd, m_sgu_ln_g, m_sgu_ln_b, m_sgu_w, m_sgu_b, m_w_out_odd, m_w_ffn_up, m_ffn_conv_w, m_ffn_conv_b, m_w_ffn_down, v_c_ctx, v_w_ada, v_b_ada, v_g_mix_pre, v_g_mix_post, v_g_ffn_pre, v_g_ffn_post, v_w_in_even, v_w_pool, v_pool_scale, v_attn_sink, v_w_out_even, v_w_in_odd, v_sgu_ln_g, v_sgu_ln_b, v_sgu_w, v_sgu_b, v_w_out_odd, v_w_ffn_up, v_ffn_conv_w, v_ffn_conv_b, v_w_ffn_down):
    P = dict(c_ctx=c_ctx, w_ada=w_ada, b_ada=b_ada, g_mix_pre=g_mix_pre, g_mix_post=g_mix_post, g_ffn_pre=g_ffn_pre,
             g_ffn_post=g_ffn_post, w_in_even=w_in_even, w_pool=w_pool, pool_scale=pool_scale, attn_sink=attn_sink,
             w_out_even=w_out_even, w_in_odd=w_in_odd, sgu_ln_g=sgu_ln_g, sgu_ln_b=sgu_ln_b, sgu_w=sgu_w, sgu_b=sgu_b,
             w_out_odd=w_out_odd, w_ffn_up=w_ffn_up, ffn_conv_w=ffn_conv_w, ffn_conv_b=ffn_conv_b, w_ffn_down=w_ffn_down)
    M = dict(c_ctx=m_c_ctx, w_ada=m_w_ada, b_ada=m_b_ada, g_mix_pre=m_g_mix_pre, g_mix_post=m_g_mix_post, g_ffn_pre=m_g_ffn_pre,
             g_ffn_post=m_g_ffn_post, w_in_even=m_w_in_even, w_pool=m_w_pool, pool_scale=m_pool_scale, attn_sink=m_attn_sink,
             w_out_even=m_w_out_even, w_in_odd=m_w_in_odd, sgu_ln_g=m_sgu_ln_g, sgu_ln_b=m_sgu_ln_b, sgu_w=m_sgu_w, sgu_b=m_sgu_b,
             w_out_odd=m_w_out_odd, w_ffn_up=m_w_ffn_up, ffn_conv_w=m_ffn_conv_w, ffn_conv_b=m_ffn_conv_b, w_ffn_down=m_w_ffn_down)
    V = dict(c_ctx=v_c_ctx, w_ada=v_w_ada, b_ada=v_b_ada, g_mix_pre=v_g_mix_pre, g_mix_post=v_g_mix_post, g_ffn_pre=v_g_ffn_pre,
             g_ffn_post=v_g_ffn_post, w_in_even=v_w_in_even, w_pool=v_w_pool, pool_scale=v_pool_scale, attn_sink=v_attn_sink,
             w_out_even=v_w_out_even, w_in_odd=v_w_in_odd, sgu_ln_g=v_sgu_ln_g, sgu_ln_b=v_sgu_ln_b, sgu_w=v_sgu_w, sgu_b=v_sgu_b,
             w_out_odd=v_w_out_odd, w_ffn_up=v_w_ffn_up, ffn_conv_w=v_ffn_conv_w, ffn_conv_b=v_ffn_conv_b, w_ffn_down=v_w_ffn_down)

    x = x[0]
    ctx = ctx[0]
    target = loss_target[0]
    L, D = x.shape
    C = ctx.shape[0]
    tm = min(512, L)
    tm_up = min(1024, L)
    conv_rows = min(256, L)
    me = 4 * lax.axis_index("x") + 2 * lax.axis_index("y") + lax.axis_index("c")
    n_ada = w_ada.shape[2]
    F = w_ffn_down.shape[1] * N_DEV
    half_f = F // 2

    n_cw = ffn_conv_w.shape[2]
    small = jnp.concatenate([_rows128(c), _rows128(sgu_ln_g), _rows128(sgu_ln_b), _rows128(ffn_conv_w)], axis=0)
    small_all = all_gather_small(small, name="gather_small_inputs")
    c_all = small_all[:, :8].reshape(N_DEV, D)
    ln_g = small_all[:, 8].reshape(1, D)
    ln_b = small_all[:, 16].reshape(1, D)
    conv_w = small_all[:, 24:].reshape(N_DEV, -1)[:, :2 * 3 * n_cw].reshape(N_DEV, 2, 3, n_cw)
    conv_w = conv_w.transpose(1, 2, 0, 3).reshape(2, 3, 2 * F)

    cs = jnp.concatenate([c_all, c_ctx[None, :], jnp.zeros((7, D), F32)], axis=0)
    b_loc = lax.dynamic_slice(b_ada, (0, me * n_ada), (2, n_ada))
    silu_c, mods_loc = ada_fwd_mm(cs, w_ada, b_loc, name="ada_fwd")
    mods_all = all_gather_small(mods_loc.reshape(-1, LANES), name="gather_mods")

    shards = [s.astype(BF16) for s in (w_in_even[0].T, w_out_even[0], w_ffn_up[0].T, w_ffn_down[0],
                                       w_in_odd[0].T, w_out_odd[0], w_ffn_up[1].T, w_ffn_down[1])]
    shards, mods_all = lax.optimization_barrier((shards, mods_all))
    w_sems, w_srcs, w_lands, _ = exchange_start(shards, place_own(shards, [s.shape[0] for s in shards], me, scatter=False, name="gather_own"),
                                              scatter=False, name="gather_start")

    def weight(j, after):
        return exchange_wait([w_srcs[j]], [w_lands[j]], w_sems[2 * j:2 * j + 2], after, scatter=False, name=f"gather_wait_{j}")[0]

    mods_all = mods_all.reshape(N_DEV, 2, 16, n_ada).transpose(1, 2, 0, 3).reshape(2, 16, 6 * D)
    mod = lambda i, row: [m_[None, :] for m_ in jnp.split(lax.dynamic_index_in_dim(mods_all[i], row, 0, False), 6)]
    sh_m, sc_m, gt_m, sh_f, sc_f, gt_f = zip(mod(0, me), mod(1, me))
    csh_m, csc_m = mod(0, N_DEV)[:2]

    row = lambda a, i: a[i][None, :]

    cos, sa, sb = _rope_tables(L)
    sink = attn_sink[0]
    bst = sgu_b[0].T
    wup, wdn = [None, None], [None, None]

    def ffn_fwd(i, xin):
        wup[i] = weight(2 + 4 * i, xin)
        h, hu = pre_mm(xin, row(g_ffn_pre, i), sh_f[i], sc_f[i], wup[i], tm=tm_up, tn=half_f, name=f"ffn_up_{i}")
        a, s1, s2 = conv_fwd(hu, conv_w[i], ffn_conv_b[i][None, :], rows=conv_rows, wblk=2 * LANES, name=f"ffn_conv_{i}")
        wdn[i] = weight(3 + 4 * i, a)
        f, xo = mm_post(a, wdn[i], xin, row(g_ffn_post, i), gt_f[i], tm=tm, name=f"ffn_down_{i}")
        return h, (hu, s1, s2), a, f, xo

    win_e = permute_heads(weight(0, sh_m[0]))
    h0, u, q, kv = inproj_even(x, row(g_mix_pre, 0), sh_m[0], sc_m[0], win_e, cos, sa, sb, tm=tm, name="in_even")
    hc, kvc = pre_mm(ctx, row(g_mix_pre, 0), csh_m, csc_m, win_e, tm=C, tn=2 * LANES, w_row_off=8 * LANES, name="in_even_ctx")
    pa = jnp.concatenate([pool_fwd(u, w_pool[0], pool_scale, name="pool_fwd"),
                          attn_fwd(q, kv, kvc, sink, name="attn_fwd")], axis=1)
    wout_e = permute_heads(weight(1, pa))
    y0, x1 = mm_post(pa, wout_e, x, row(g_mix_post, 0), gt_m[0], tm=tm, name="out_even")
    h1, hu0, a0, f0, x2 = ffn_fwd(0, x1)
    win_o = weight(4, x2)
    h2, z1 = pre_mm(x2, row(g_mix_pre, 1), sh_m[1], sc_m[1], win_o, tm=tm_up, tn=D, name="in_odd")
    us = sgu_fwd(z1, ln_g, ln_b, sgu_w[0], bst, name="sgu_fwd")
    wout_o = weight(5, us)
    y1, x3 = mm_post(us, wout_o, x2, row(g_mix_post, 1), gt_m[1], tm=tm, name="out_odd")
    h3, hu1, a1, f1, x4 = ffn_fwd(1, x3)
    loss_part, dx4 = loss_grad(x4, target, tm=tm, name="loss")
    loss = lax.psum(loss_part[0, 0], ("x", "y", "c"))

    g_srcs, g_lands, g_sems = [], [], []

    def scatter(grads, nm):
        own = place_own(grads, [g.shape[0] // N_DEV for g in grads], me, scatter=True, name=nm.replace("start", "own"))
        sems, srcs, lands, tok = exchange_start(grads, own, scatter=True, name=nm)
        g_srcs.extend(srcs)
        g_lands.extend(lands)
        g_sems.extend(sems)
        return tok[0:1, 0:1]

    def ffn_bwd(i, dxo, xin, h, hu, a, f, g_post):
        dyf, da, dg_post, dgt = post_bwd_mm(dxo, f, g_post, gt_f[i], wdn[i], tm=tm, name=f"ffn_down_bwd_{i}")
        dhg, dhu, dcwg, dcwu, dcbg, dcbu = conv_bwd(da, hu[1], hu[2], hu[0], conv_w[i], rows=conv_rows, wblk=2 * LANES,
                                                    name=f"ffn_conv_bwd_{i}")
        dxin, dg_pre, dsh, dsc = mm_pre_bwd([dhg, dhu], wup[i], xin, dxo, row(g_ffn_pre, i), sc_f[i], tm=tm, tk=half_f,
                                            name=f"ffn_up_bwd_{i}")
        g_dn = wgrad([a], dyf, tr=2 * LANES, name=f"wgrad_down_{i}")
        g_up = wgrad([dhg, dhu], h, tr=2 * LANES, name=f"wgrad_up_{i}")
        tok = scatter([g_dn, g_up], f"scatter_start_ffn_{i}")
        return dxin, tok, dict(g_ffn_post=dg_post, g_ffn_pre=dg_pre, gt_f=dgt, sh_f=dsh, sc_f=dsc,
                               ffn_conv_w=jnp.concatenate([dcwg, dcwu], axis=1), ffn_conv_b=jnp.concatenate([dcbg, dcbu], axis=1)[0])

    dx3, tok, sf1 = ffn_bwd(1, dx4, x3, h3, hu1, a1, f1, row(g_ffn_post, 1))
    dy1, dus, dg_mpost1, dgt_m1 = post_bwd_mm(dx3, y1, row(g_mix_post, 1) + tok, gt_m[1], wout_o, tm=tm, name="out_odd_bwd")
    dz1, dws, dbs, dlng, dlnb = sgu_bwd(z1, dus, ln_g, ln_b, sgu_w[0], bst, name="sgu_bwd")
    dx2, dg_mpre1, dsh_m1, dsc_m1 = mm_pre_bwd([dz1], win_o, x2, dx3, row(g_mix_pre, 1), sc_m[1], tm=tm, tk=D, name="in_odd_bwd")
    tok = scatter([wgrad([us], dy1, tr=2 * LANES, name="wgrad_out_odd"), wgrad([dz1], h2, tr=2 * LANES, name="wgrad_in_odd")],
                  "scatter_start_mix_1")

    dx1, tok, sf0 = ffn_bwd(0, dx2, x1, h1, hu0, a0, f0, row(g_ffn_post, 0) + tok)
    dy0, dpa, dg_mpost0, dgt_m0 = post_bwd_mm(dx1, y0, row(g_mix_post, 0) + tok, gt_m[0], wout_e, tm=tm, name="out_even_bwd")
    du, dwp, dps = pool_bwd(u, dpa, w_pool[0], pool_scale, name="pool_bwd")
    dq, dkv, dkvc, dsink = attn_bwd(q, kv, kvc, sink, dpa, cos, sa, sb, name="attn_bwd")
    dz0 = jnp.concatenate([du, dq, dkv], axis=1)
    dzc = jnp.concatenate([jnp.zeros((C, 8 * LANES), BF16), dkvc], axis=1)
    tok = scatter([permute_heads(wgrad([pa], dy0, tr=2 * LANES, name="wgrad_out_even"), inverse=True),
                   permute_heads(wgrad([dz0], h0, tr=2 * LANES, extra=(dzc, hc), name="wgrad_in_even"), inverse=True)],
                  "scatter_start_mix_0")
    grad_x, dg_mpre0, dsh_m0, dsc_m0 = mm_pre_bwd([dz0], win_e, x, dx1, row(g_mix_pre, 0) + tok, sc_m[0], tm=tm, tk=dz0.shape[1],
                                                  name="in_even_bwd")
    _, dg_mpre0c, dcsh, dcsc = mm_pre_bwd([dkvc], win_e, ctx, None, row(g_mix_pre, 0), csc_m, tm=C, tk=2 * LANES,
                                          w_row_off=8 * LANES, name="in_even_ctx_bwd")

    slots = exchange_wait(g_srcs, g_lands, g_sems, dcsh, scatter=True, name="scatter_wait")
    out = {}

    def update(name, lands, transposed):
        w_, m_, v_ = (a.transpose(0, 2, 1) if transposed else a for a in (P[name], M[name], V[name]))
        r = w_.shape[1]
        tr = r // 4 if r % 64 == 0 and r > 256 else r
        res = adamw(w_, m_, v_, [l_.reshape(N_DEV, r, l_.shape[1]) for l_ in lands], tr=tr, name=f"adamw_{name}")
        for kind, val in zip(("grad", "delta", "new_m", "new_v"), res):
            out[(kind, name)] = val.transpose(0, 2, 1) if transposed else val

    update("w_in_even", [slots[7]], True)
    update("w_out_even", [slots[6]], False)
    update("w_in_odd", [slots[3]], True)
    update("w_out_odd", [slots[2]], False)
    update("w_ffn_up", [slots[5], slots[1]], True)
    update("w_ffn_down", [slots[4], slots[0]], False)

    zero = jnp.zeros((1, D), F32)
    dmod0 = jnp.concatenate([dsh_m0, dsc_m0, dgt_m0, sf0["sh_f"], sf0["sc_f"], sf0["gt_f"]], axis=1)
    dmodc = jnp.concatenate([dcsh, dcsc, zero, zero, zero, zero], axis=1)
    dmod1 = jnp.concatenate([dsh_m1, dsc_m1, dgt_m1, sf1["sh_f"], sf1["sc_f"], sf1["gt_f"]], axis=1)
    dmods = jnp.concatenate([dmod0, dmodc, dmod1], axis=0)
    dmods_all = all_gather_small(dmods.reshape(-1, LANES), name="gather_dmods").reshape(N_DEV, 3, N_DEV, n_ada)
    dall = lax.dynamic_index_in_dim(dmods_all, me, 2, False)
    g_w_ada, dcc = ada_bwd_mm(silu_c, c_ctx[None, :], dall, w_ada, name="ada_bwd")
    nl = w_ada.shape[0]
    res = adamw(w_ada, m_w_ada, v_w_ada, [g_w_ada[l][None] for l in range(nl)], tr=D // 4, name="adamw_w_ada")
    for kind, val in zip(("grad", "delta", "new_m", "new_v"), res):
        out[(kind, "w_ada")] = val

    rep = dict(
        c_ctx=dcc[0],
        b_ada=jnp.stack([dmod0[0] + dmodc[0], dmod1[0]]),
        g_mix_pre=jnp.concatenate([dg_mpre0 + dg_mpre0c, dg_mpre1]),
        g_mix_post=jnp.concatenate([dg_mpost0, dg_mpost1]),
        g_ffn_pre=jnp.concatenate([sf0["g_ffn_pre"], sf1["g_ffn_pre"]]),
        g_ffn_post=jnp.concatenate([sf0["g_ffn_post"], sf1["g_ffn_post"]]),
        w_pool=dwp[None], pool_scale=dps, attn_sink=dsink[:, :N_Q_HEADS],
        sgu_w=dws[None], sgu_b=dbs[:, :sgu_b.shape[1]].T[None],
        ffn_conv_b=jnp.stack([sf0["ffn_conv_b"], sf1["ffn_conv_b"]]),
    )
    rep_names = list(rep)
    conv_g = jnp.stack([sf0["ffn_conv_w"], sf1["ffn_conv_w"]]).reshape(2, 3, N_DEV, n_cw).transpose(2, 0, 1, 3)
    shard_full = dict(sgu_ln_g=dlng.reshape(N_DEV, LANES), sgu_ln_b=dlnb.reshape(N_DEV, LANES),
                      ffn_conv_w=jnp.concatenate([_rows128(conv_g[d]) for d in range(N_DEV)], axis=0))
    pieces = [_rows128(rep[k]) for k in rep_names] + [shard_full[k] for k in shard_full]
    sizes = [p.shape[0] for p in pieces]
    pieces.append(jnp.zeros((-sum(sizes) % 16, LANES), F32))
    _, gsum = all_gather_small(jnp.concatenate(pieces, axis=0).astype(BF16), reduce=True, name="allreduce_small_grads")
    offs = [sum(sizes[:i]) for i in range(len(sizes))]
    n_rep = len(rep_names)
    cw_rows = sizes[-1] // N_DEV
    g_own = [gsum[offs[i]:offs[i] + sizes[i]] for i in range(n_rep)]
    g_own.append(_rows128(lax.dynamic_slice_in_dim(gsum, offs[n_rep] + me, 1, 0)))
    g_own.append(_rows128(lax.dynamic_slice_in_dim(gsum, offs[n_rep + 1] + me, 1, 0)))
    g_own.append(lax.dynamic_slice_in_dim(gsum, offs[n_rep + 2] + me * cw_rows, cw_rows, 0))
    small_names = rep_names + list(shard_full)
    packs = [jnp.concatenate([_rows128(src[k]) for k in small_names], axis=0) for src in (P, M, V)]
    n_pack = packs[0].shape[0]
    res = adamw(*[p_[None] for p_ in packs], [jnp.concatenate(g_own, axis=0)[None]], tr=n_pack, name="adamw_small")
    o = 0
    for k in small_names:
        n = _rows128(P[k]).shape[0]
        for kind, val in zip(("grad", "delta", "new_m", "new_v"), res):
            out[(kind, k)] = val[0, o:o + n].reshape(-1)[:P[k].size].reshape(P[k].shape)
        o += n
    assert o == n_pack

    names = list(P)
    final = [loss, grad_x[None]]
    for kind in ("grad", "delta", "new_m", "new_v"):
        for k in names:
            val = out[(kind, k)]
            final.append(val)
    return tuple(final)
```

```python
import functools
import math

import jax
import jax.numpy as jnp
from jax import lax
from jax.experimental import pallas as pl
from jax.experimental.pallas import tpu as pltpu

F32 = jnp.float32
BF16 = jnp.bfloat16
MESH = pl.DeviceIdType.MESH
N_DEV = 8
LANES = 128
VMEM_LIMIT = 48 * 1024 * 1024
EPS = 1e-6
NEG_INF = -1e30
GRID_W = 64
WINDOW = 128
BLK = 128
HEAD_DIM = 64
N_Q_HEADS = 8
N_KV_HEADS = 2
GQA = N_Q_HEADS // N_KV_HEADS
POOL_WINDOWS = (2, 4, 8, 16)
ROPE_BASE = 10000.0
ROPE_FREQS = HEAD_DIM // 4
PAD = 16
ADAM_LR, ADAM_B1, ADAM_B2, ADAM_EPS, ADAM_WD, ADAM_STEP = 0.001, 0.9, 0.999, 1e-08, 0.01, 10
BC1 = 1.0 - ADAM_B1 ** ADAM_STEP
BC2 = 1.0 - ADAM_B2 ** ADAM_STEP
SQRT_2_OVER_PI = math.sqrt(2.0 / math.pi)
GELU_C = 0.044715


def _cp(sem=None):
    return pltpu.CompilerParams(dimension_semantics=sem, vmem_limit_bytes=VMEM_LIMIT)


def _dot(a, b):
    return jnp.dot(a, b, preferred_element_type=F32)


def _dot_nt(a, b):
    return lax.dot_general(a, b, (((1,), (1,)), ((), ())), preferred_element_type=F32)


def _dot_tn(a, b):
    return lax.dot_general(a, b, (((0,), (0,)), ((), ())), preferred_element_type=F32)


def _rms(x):
    r = lax.rsqrt(jnp.mean(x * x, axis=-1, keepdims=True) + EPS)
    return x * r, r


def _rms_bwd(dn, n, r):
    return r * (dn - n * jnp.mean(dn * n, axis=-1, keepdims=True))


def _colsum(a):
    return jnp.sum(a, axis=0, keepdims=True)


def _rope(x, c, sa, sb):
    return x * c + pltpu.roll(x, LANES - ROPE_FREQS, 1) * sa + pltpu.roll(x, ROPE_FREQS, 1) * sb


def _full(shape):
    return pl.BlockSpec(shape, lambda *_: (0,) * len(shape))


def pre_mm(x, g, sh, sc, wt, *, tm, tn, w_row_off=0, name):
    T, D = x.shape
    n_rows = wt.shape[0] - w_row_off
    off = w_row_off // tn

    def body(x_ref, g_ref, sh_ref, sc_ref, w_ref, h_ref, z_ref):
        @pl.when(pl.program_id(1) == 0)
        def _():
            n, _ = _rms(x_ref[...])
            h_ref[...] = (n * g_ref[...] * (1.0 + sc_ref[...]) + sh_ref[...]).astype(BF16)

        z_ref[...] = _dot_nt(h_ref[...], w_ref[...]).astype(BF16)

    vec = pl.BlockSpec((1, D), lambda i, j: (0, 0))
    return pl.pallas_call(
        body, name=name, grid=(T // tm, n_rows // tn),
        in_specs=[pl.BlockSpec((tm, D), lambda i, j: (i, 0)), vec, vec, vec, pl.BlockSpec((tn, D), lambda i, j: (j + off, 0))],
        out_specs=[pl.BlockSpec((tm, D), lambda i, j: (i, 0)), pl.BlockSpec((tm, tn), lambda i, j: (i, j))],
        out_shape=[jax.ShapeDtypeStruct((T, D), BF16), jax.ShapeDtypeStruct((T, n_rows), BF16)],
        compiler_params=_cp(("parallel", "arbitrary")),
    )(x, g, sh, sc, wt)


def inproj_even(x, g, sh, sc, wt, cos, sa, sb, *, tm, name):
    T, D = x.shape
    N = wt.shape[0]

    def body(x_ref, g_ref, sh_ref, sc_ref, w_ref, c_ref, sa_ref, sb_ref, h_ref, u_ref, q_ref, kv_ref):
        n, _ = _rms(x_ref[...])
        h = (n * g_ref[...] * (1.0 + sc_ref[...]) + sh_ref[...]).astype(BF16)
        h_ref[...] = h
        z = _dot_nt(h, w_ref[...])
        u_ref[...] = z[:, :4 * LANES]
        c, a, b = c_ref[...], sa_ref[...], sb_ref[...]
        for s in range(4):
            q_ref[:, s * LANES:(s + 1) * LANES] = _rope(z[:, (4 + s) * LANES:(5 + s) * LANES], c, a, b).astype(BF16)
        kv_ref[:, :LANES] = _rope(z[:, 8 * LANES:9 * LANES], c, a, b).astype(BF16)
        kv_ref[:, LANES:] = z[:, 9 * LANES:].astype(BF16)

    vec = pl.BlockSpec((1, D), lambda i: (0, 0))
    row = lambda w: pl.BlockSpec((tm, w), lambda i: (i, 0))
    return pl.pallas_call(
        body, name=name, grid=(T // tm,),
        in_specs=[row(D), vec, vec, vec, _full((N, D)), row(LANES), row(LANES), row(LANES)],
        out_specs=[row(D), row(4 * LANES), row(4 * LANES), row(2 * LANES)],
        out_shape=[jax.ShapeDtypeStruct((T, D), BF16), jax.ShapeDtypeStruct((T, 4 * LANES), F32),
                   jax.ShapeDtypeStruct((T, 4 * LANES), BF16), jax.ShapeDtypeStruct((T, 2 * LANES), BF16)],
        compiler_params=_cp(("parallel",)),
    )(x, g, sh, sc, wt, cos, sa, sb)


def mm_post(a, w, x, g, gt, *, tm, name):
    T, K = a.shape
    D = w.shape[1]

    def body(a_ref, w_ref, x_ref, g_ref, gt_ref, y_ref, xn_ref):
        y = _dot(a_ref[...], w_ref[...])
        n, _ = _rms(y)
        y_ref[...] = y
        xn_ref[...] = x_ref[...] + gt_ref[...] * (n * g_ref[...])

    vec = pl.BlockSpec((1, D), lambda i: (0, 0))
    row = lambda w_: pl.BlockSpec((tm, w_), lambda i: (i, 0))
    return pl.pallas_call(
        body, name=name, grid=(T // tm,),
        in_specs=[row(K), _full((K, D)), row(D), vec, vec],
        out_specs=[row(D), row(D)],
        out_shape=[jax.ShapeDtypeStruct((T, D), F32), jax.ShapeDtypeStruct((T, D), F32)],
        compiler_params=_cp(("parallel",)),
    )(a, w, x, g, gt)


def post_bwd_mm(dxn, y, g, gt, w, *, tm, name):
    T, D = y.shape
    K = w.shape[0]

    def body(dxn_ref, y_ref, g_ref, gt_ref, w_ref, dy_ref, da_ref, dg_ref, dgt_ref):
        @pl.when(pl.program_id(0) == 0)
        def _():
            dg_ref[...] = jnp.zeros_like(dg_ref)
            dgt_ref[...] = jnp.zeros_like(dgt_ref)

        d = dxn_ref[...]
        n, r = _rms(y_ref[...])
        g_, gt_ = g_ref[...], gt_ref[...]
        dg_ref[...] += _colsum(d * gt_ * n)
        dgt_ref[...] += _colsum(d * g_ * n)
        dy = _rms_bwd(d * (gt_ * g_), n, r).astype(BF16)
        dy_ref[...] = dy
        da_ref[...] = _dot_nt(dy, w_ref[...]).astype(BF16)

    vec = pl.BlockSpec((1, D), lambda i: (0, 0))
    row = lambda w_: pl.BlockSpec((tm, w_), lambda i: (i, 0))
    return pl.pallas_call(
        body, name=name, grid=(T // tm,),
        in_specs=[row(D), row(D), vec, vec, _full((K, D))],
        out_specs=[row(D), row(K), vec, vec],
        out_shape=[jax.ShapeDtypeStruct((T, D), BF16), jax.ShapeDtypeStruct((T, K), BF16),
                   jax.ShapeDtypeStruct((1, D), F32), jax.ShapeDtypeStruct((1, D), F32)],
        compiler_params=_cp(("arbitrary",)),
    )(dxn, y, g, gt, w)


def mm_pre_bwd(dzs, wt, x, dres, g, sc, *, tm, tk, w_row_off=0, name):
    T, N = dzs[0].shape
    D = x.shape[1]
    nk = N // tk
    npart = len(dzs)
    off = w_row_off // tk
    has_res = dres is not None

    def body(*refs):
        dz_refs = refs[:npart]
        w_refs = refs[npart:2 * npart]
        rest = refs[2 * npart:]
        x_ref = rest[0]
        dres_ref = rest[1] if has_res else None
        g_ref, sc_ref, dx_ref, dg_ref, dsh_ref, dsc_ref, acc = rest[1 + has_res:]
        i, k = pl.program_id(0), pl.program_id(1)

        @pl.when(jnp.logical_and(i == 0, k == 0))
        def _():
            dg_ref[...] = jnp.zeros_like(dg_ref)
            dsh_ref[...] = jnp.zeros_like(dsh_ref)
            dsc_ref[...] = jnp.zeros_like(dsc_ref)

        part = _dot(dz_refs[0][...], w_refs[0][...])
        for p in range(1, npart):
            part = part + _dot(dz_refs[p][...], w_refs[p][...])

        @pl.when(k == 0)
        def _():
            acc[...] = part

        @pl.when(k > 0)
        def _():
            acc[...] += part

        @pl.when(k == nk - 1)
        def _():
            dh = acc[...]
            n, r = _rms(x_ref[...])
            g_, s1 = g_ref[...], 1.0 + sc_ref[...]
            dsh_ref[...] += _colsum(dh)
            dsc_ref[...] += _colsum(dh * n * g_)
            dg_ref[...] += _colsum(dh * s1 * n)
            dxp = _rms_bwd(dh * (g_ * s1), n, r)
            dx_ref[...] = dxp + dres_ref[...] if has_res else dxp

    vec = pl.BlockSpec((1, D), lambda i, k: (0, 0))
    row = pl.BlockSpec((tm, D), lambda i, k: (i, 0))
    w_specs = [pl.BlockSpec((tk, D), (lambda i, k, p=p: (k + off + p * nk, 0))) for p in range(npart)]
    res_specs, res_args = ([row], (dres,)) if has_res else ([], ())
    return pl.pallas_call(
        body, name=name, grid=(T // tm, nk),
        in_specs=[pl.BlockSpec((tm, tk), lambda i, k: (i, k))] * npart + w_specs + [row] + res_specs + [vec, vec],
        out_specs=[row, vec, vec, vec],
        out_shape=[jax.ShapeDtypeStruct((T, D), F32)] + [jax.ShapeDtypeStruct((1, D), F32)] * 3,
        scratch_shapes=[pltpu.VMEM((tm, D), F32)],
        compiler_params=_cp(("arbitrary", "arbitrary")),
    )(*dzs, *([wt] * npart), x, *res_args, g, sc)


def wgrad(a_parts, b, *, tr, extra=None, name):
    T, R = a_parts[0].shape
    D = b.shape[1]
    npart = len(a_parts)
    nr = R // tr

    def body(*refs):
        a_refs, b_ref = refs[:npart], refs[npart]
        g_ref = refs[-1]
        for p in range(npart):
            @pl.when(pl.program_id(0) // nr == p)
            def _():
                acc = _dot_tn(a_refs[p][...], b_ref[...])
                if extra is not None:
                    acc += _dot_tn(refs[npart + 1][...], refs[npart + 2][...])
                g_ref[...] = acc.astype(BF16)

    in_specs = [pl.BlockSpec((T, tr), (lambda r, p=p: (0, jnp.clip(r - p * nr, 0, nr - 1)))) for p in range(npart)]
    in_specs.append(_full((T, D)))
    args = [*a_parts, b]
    if extra is not None:
        a2, b2 = extra
        in_specs += [pl.BlockSpec((a2.shape[0], tr), lambda r: (0, r)), _full(b2.shape)]
        args += [a2, b2]
    return pl.pallas_call(
        body, name=name, grid=(npart * nr,),
        in_specs=in_specs, out_specs=pl.BlockSpec((tr, D), lambda r: (r, 0)),
        out_shape=jax.ShapeDtypeStruct((npart * R, D), BF16),
        compiler_params=_cp(("parallel",)),
    )(*args)


def _conv_ext(ref, r0, rows, total):
    top = ref[pl.ds(pl.multiple_of(jnp.maximum(r0 - PAD, 0), PAD), PAD), :]
    mid = ref[pl.ds(r0, rows), :]
    bot = ref[pl.ds(pl.multiple_of(jnp.minimum(r0 + rows, total - PAD), PAD), PAD), :]
    top = jnp.where(r0 > 0, top, jnp.zeros_like(top))
    bot = jnp.where(r0 + rows < total, bot, jnp.zeros_like(bot))
    return jnp.concatenate([top, mid, bot], axis=0).astype(F32)


def _shift_rows(a, k):
    return pltpu.roll(a, k % a.shape[0], 0)


def _conv3(x, w, b):
    return w[0:1] * _shift_rows(x, 1) + w[1:2] * x + w[2:3] * _shift_rows(x, -1) + b


def _gate_up_specs(rows_, wblk, nb):
    return [pl.BlockSpec((rows_, wblk), lambda j: (0, j)), pl.BlockSpec((rows_, wblk), lambda j: (0, j + nb))]


def conv_fwd(hu, cw, cb, *, rows, wblk, name):
    L, N2 = hu.shape
    nb = N2 // 2 // wblk
    nchunk = L // rows

    def body(hg_ref, hu_ref, wg_ref, wu_ref, bg_ref, bu_ref, a_ref, s1_ref, s2_ref):
        def chunk(ci, carry):
            r0 = pl.multiple_of(ci * rows, rows)
            gate = _conv3(_conv_ext(hg_ref, r0, rows, L), wg_ref[...], bg_ref[...])[PAD:PAD + rows]
            up = _conv3(_conv_ext(hu_ref, r0, rows, L), wu_ref[...], bu_ref[...])[PAD:PAD + rows]
            sg = jax.nn.sigmoid(gate)
            silu = gate * sg
            at = pl.ds(r0, rows)
            a_ref[at, :] = (silu * up).astype(BF16)
            s1_ref[at, :] = silu.astype(BF16)
            s2_ref[at, :] = (up * (sg + silu * (1.0 - sg))).astype(BF16)
            return carry

        lax.fori_loop(0, nchunk, chunk, 0)

    out = pl.BlockSpec((L, wblk), lambda j: (0, j))
    return pl.pallas_call(
        body, name=name, grid=(nb,),
        in_specs=_gate_up_specs(L, wblk, nb) + _gate_up_specs(3, wblk, nb) + _gate_up_specs(1, wblk, nb),
        out_specs=[out] * 3, out_shape=[jax.ShapeDtypeStruct((L, N2 // 2), BF16)] * 3,
        compiler_params=_cp(("parallel",)),
    )(hu, hu, cw, cw, cb, cb)


def conv_bwd(da, s1, s2, hu, cw, *, rows, wblk, name):
    L, N2 = hu.shape
    F = N2 // 2
    nb = F // wblk
    nchunk = L // rows
    mid = slice(PAD, PAD + rows)

    def body(da_ref, s1_ref, s2_ref, hg_ref, hu_ref, wg_ref, wu_ref, dg_ref, du_ref, dwg_ref, dwu_ref, dbg_ref, dbu_ref):
        for ref in (dwg_ref, dwu_ref, dbg_ref, dbu_ref):
            ref[...] = jnp.zeros_like(ref)

        def half_bwd(x_ref, dh, w_ref, dx_ref, dw_ref, db_ref, r0):
            w = w_ref[...]
            nxt, prv = _shift_rows(dh, -1)[mid], _shift_rows(dh, 1)[mid]
            dhm, xm = dh[mid], x_ref[pl.ds(r0, rows), :].astype(F32)
            dx_ref[pl.ds(r0, rows), :] = (w[0:1] * nxt + w[1:2] * dhm + w[2:3] * prv).astype(BF16)
            db_ref[...] += _colsum(dhm)
            dw_ref[0:1, :] += _colsum(nxt * xm)
            dw_ref[1:2, :] += _colsum(dhm * xm)
            dw_ref[2:3, :] += _colsum(prv * xm)

        def chunk(ci, carry):
            r0 = pl.multiple_of(ci * rows, rows)
            d = _conv_ext(da_ref, r0, rows, L)
            half_bwd(hu_ref, d * _conv_ext(s1_ref, r0, rows, L), wu_ref, du_ref, dwu_ref, dbu_ref, r0)
            half_bwd(hg_ref, d * _conv_ext(s2_ref, r0, rows, L), wg_ref, dg_ref, dwg_ref, dbg_ref, r0)
            return carry

        lax.fori_loop(0, nchunk, chunk, 0)

    blk = lambda r: pl.BlockSpec((r, wblk), lambda j: (0, j))
    return pl.pallas_call(
        body, name=name, grid=(nb,),
        in_specs=[blk(L)] * 3 + _gate_up_specs(L, wblk, nb) + _gate_up_specs(3, wblk, nb),
        out_specs=[blk(L), blk(L), blk(3), blk(3), blk(1), blk(1)],
        out_shape=[jax.ShapeDtypeStruct((L, F), BF16)] * 2 + [jax.ShapeDtypeStruct((3, F), F32)] * 2
        + [jax.ShapeDtypeStruct((1, F), F32)] * 2,
        compiler_params=_cp(("parallel",)),
    )(da, s1, s2, hu, hu, cw, cw)


def _window_sums(pad_ref, w, lead):
    a = pad_ref[...]
    k = 1
    while k < w:
        a = a + _shift_rows(a, -k)
        k *= 2
    return _shift_rows(a, lead) if lead else a


def _pool_counts(L, h):
    t = lax.broadcasted_iota(jnp.int32, (L, 1), 0)
    return (jnp.minimum(t + h, L) - jnp.maximum(t - h, 0)).astype(F32)


def _pooled(u_ref, pad_ref, L, w):
    h = w // 2
    pad_ref[pl.ds(PAD, L), :] = u_ref[...]
    win = _window_sums(pad_ref, w, h)[PAD:PAD + L]
    return win / _pool_counts(L, h) - u_ref[...]


def _zero_pad_edges(pad_ref, L):
    z = jnp.zeros((PAD, LANES), F32)
    pad_ref[pl.ds(0, PAD), :] = z
    pad_ref[pl.ds(PAD + L, PAD), :] = z


def pool_fwd(u, w_pool, pool_scale, *, name):
    L = u.shape[0]

    def body(u_ref, w_ref, ps_ref, p_ref, pad_ref):
        _zero_pad_edges(pad_ref, L)
        for gi, win in enumerate(POOL_WINDOWS):
            @pl.when(pl.program_id(0) == gi)
            def _():
                pooled = _pooled(u_ref, pad_ref, L, win)
                p_ref[...] = (_dot(pooled.astype(BF16), w_ref[...].astype(BF16)) * ps_ref[...]).astype(BF16)

    return pl.pallas_call(
        body, name=name, grid=(len(POOL_WINDOWS),),
        in_specs=[pl.BlockSpec((L, LANES), lambda gi: (0, gi)), pl.BlockSpec((None, LANES, LANES), lambda gi: (gi, 0, 0)),
                  pl.BlockSpec((1, LANES), lambda gi: (0, gi))],
        out_specs=pl.BlockSpec((L, LANES), lambda gi: (0, gi)),
        out_shape=jax.ShapeDtypeStruct((L, 4 * LANES), BF16),
        scratch_shapes=[pltpu.VMEM((L + 2 * PAD, LANES), F32)],
        compiler_params=_cp(("parallel",)),
    )(u, w_pool, pool_scale)


def pool_bwd(u, dpa, w_pool, pool_scale, *, name):
    L = u.shape[0]

    def body(u_ref, dp_ref, w_ref, ps_ref, du_ref, dw_ref, dps_ref, pad_ref):
        _zero_pad_edges(pad_ref, L)
        for gi, win in enumerate(POOL_WINDOWS):
            @pl.when(pl.program_id(0) == gi)
            def _():
                h = win // 2
                wb = w_ref[...].astype(BF16)
                pooled = _pooled(u_ref, pad_ref, L, win).astype(BF16)
                dp = dp_ref[...].astype(F32)
                dps_ref[...] = _colsum(dp * _dot(pooled, wb))
                dy = (dp * ps_ref[...]).astype(BF16)
                dw_ref[...] = _dot_tn(pooled, dy)
                dpooled = _dot_nt(dy, wb)
                pad_ref[pl.ds(PAD, L), :] = dpooled / _pool_counts(L, h)
                du_ref[...] = (_window_sums(pad_ref, win, h - 1)[PAD:PAD + L] - dpooled).astype(BF16)

    return pl.pallas_call(
        body, name=name, grid=(len(POOL_WINDOWS),),
        in_specs=[pl.BlockSpec((L, LANES), lambda gi: (0, gi)), pl.BlockSpec((L, LANES), lambda gi: (0, gi)),
                  pl.BlockSpec((None, LANES, LANES), lambda gi: (gi, 0, 0)), pl.BlockSpec((1, LANES), lambda gi: (0, gi))],
        out_specs=[pl.BlockSpec((L, LANES), lambda gi: (0, gi)), pl.BlockSpec((None, LANES, LANES), lambda gi: (gi, 0, 0)),
                   pl.BlockSpec((1, LANES), lambda gi: (0, gi))],
        out_shape=[jax.ShapeDtypeStruct((L, 4 * LANES), BF16), jax.ShapeDtypeStruct((4, LANES, LANES), F32),
                   jax.ShapeDtypeStruct((1, 4 * LANES), F32)],
        scratch_shapes=[pltpu.VMEM((L + 2 * PAD, LANES), F32)],
        compiler_params=_cp(("parallel",)),
    )(u, dpa, w_pool, pool_scale)


def _attn_probs(qk, band_k, ctx_k, sink_ref, kh, mask4):
    s_loc = jnp.where(mask4, _dot_nt(qk, band_k), NEG_INF)
    s_ctx = _dot_nt(qk, ctx_k)
    sk = jnp.concatenate([jnp.full((BLK, 1), sink_ref[kh * GQA + hh], F32) for hh in range(GQA)], axis=0)
    m = jnp.maximum(jnp.maximum(jnp.max(s_loc, axis=-1, keepdims=True), jnp.max(s_ctx, axis=-1, keepdims=True)), sk)
    e_loc, e_ctx, e_s = jnp.exp(s_loc - m), jnp.exp(s_ctx - m), jnp.exp(sk - m)
    inv = 1.0 / (jnp.sum(e_loc, axis=-1, keepdims=True) + jnp.sum(e_ctx, axis=-1, keepdims=True) + e_s)
    return e_loc * inv, e_ctx * inv, e_s * inv


def _attn_block(n, L):
    start = pl.multiple_of(jnp.clip((n - 1) * BLK, 0, L - 3 * BLK), BLK)
    qpos = n * BLK + lax.broadcasted_iota(jnp.int32, (BLK, 3 * BLK), 0)
    kpos = start + lax.broadcasted_iota(jnp.int32, (BLK, 3 * BLK), 1)
    mask = jnp.abs(kpos - qpos) <= WINDOW
    return start, jnp.concatenate([mask] * GQA, axis=0)


def _stack_slabs(ref):
    return jnp.concatenate([ref[:, s * LANES:(s + 1) * LANES] for s in range(GQA)], axis=0)


def _kv_head_lanes(kh):
    return (lax.broadcasted_iota(jnp.int32, (1, LANES), 1) // HEAD_DIM) == kh


def permute_heads(w, inverse=False):
    lo, hi = 4 * LANES, 8 * LANES
    mid = w[lo:hi].reshape(*((GQA, N_KV_HEADS) if inverse else (N_KV_HEADS, GQA)), HEAD_DIM, w.shape[1])
    return jnp.concatenate([w[:lo], mid.swapaxes(0, 1).reshape(hi - lo, w.shape[1]), w[hi:]], axis=0)


def attn_fwd(q, kv, kvc, sink, *, name):
    L = q.shape[0]
    C = kvc.shape[0]
    scale = HEAD_DIM ** -0.5

    def body(q_ref, kv_ref, kvc_ref, sink_ref, o_ref):
        start, mask4 = _attn_block(pl.program_id(0), L)
        band = kv_ref[pl.ds(start, 3 * BLK), :]
        kvc_ = kvc_ref[...]
        qs = _stack_slabs(q_ref) * scale
        o = jnp.zeros((GQA * BLK, LANES), F32)
        for kh in range(N_KV_HEADS):
            grp = _kv_head_lanes(kh)
            qk = jnp.where(grp, qs, jnp.zeros_like(qs))
            p_loc, p_ctx, _ = _attn_probs(qk, band[:, :LANES], kvc_[:, :LANES], sink_ref, kh, mask4)
            o = o + jnp.where(grp, _dot(p_loc.astype(BF16), band[:, LANES:]) + _dot(p_ctx.astype(BF16), kvc_[:, LANES:]), 0.0)
        for s in range(GQA):
            o_ref[:, s * LANES:(s + 1) * LANES] = o[s * BLK:(s + 1) * BLK].astype(BF16)

    return pl.pallas_call(
        body, name=name, grid=(L // BLK,),
        in_specs=[pl.BlockSpec((BLK, 4 * LANES), lambda n: (n, 0)), _full((L, 2 * LANES)), _full((C, 2 * LANES)),
                  pl.BlockSpec(memory_space=pltpu.SMEM)],
        out_specs=pl.BlockSpec((BLK, 4 * LANES), lambda n: (n, 0)),
        out_shape=jax.ShapeDtypeStruct((L, 4 * LANES), BF16),
        compiler_params=_cp(("parallel",)),
    )(q, kv, kvc, sink)


def attn_bwd(q, kv, kvc, sink, dpa, cos, sa, sb, *, name):
    L = q.shape[0]
    C = kvc.shape[0]
    nb = L // BLK
    scale = HEAD_DIM ** -0.5

    def body(q_ref, kv_ref, kvc_ref, sink_ref, do_ref, c_ref, sa_ref, sb_ref, cq_ref, saq_ref, sbq_ref,
             dq_ref, dkv_ref, dkvc_ref, dsink_ref, dkv_acc, dkvc_acc):
        n = pl.program_id(0)

        @pl.when(n == 0)
        def _():
            dkv_acc[...] = jnp.zeros_like(dkv_acc)
            dkvc_acc[...] = jnp.zeros_like(dkvc_acc)
            dsink_ref[...] = jnp.zeros_like(dsink_ref)

        start, mask4 = _attn_block(n, L)
        band = kv_ref[pl.ds(start, 3 * BLK), :]
        kvc_ = kvc_ref[...]
        band_k, band_v, ctx_k, ctx_v = band[:, :LANES], band[:, LANES:], kvc_[:, :LANES], kvc_[:, LANES:]
        qs = _stack_slabs(q_ref) * scale
        dos = _stack_slabs(do_ref)
        lane = lax.broadcasted_iota(jnp.int32, (1, LANES), 1)
        dsink = jnp.zeros((1, LANES), F32)
        dq = jnp.zeros((GQA * BLK, LANES), F32)
        dk = jnp.zeros((3 * BLK, LANES), F32)
        dv = jnp.zeros((3 * BLK, LANES), F32)
        dkc = jnp.zeros((C, LANES), F32)
        dvc = jnp.zeros((C, LANES), F32)
        for kh in range(N_KV_HEADS):
            grp = _kv_head_lanes(kh)
            qk = jnp.where(grp, qs, jnp.zeros_like(qs))
            dok = jnp.where(grp, dos, jnp.zeros_like(dos))
            p_loc, p_ctx, p_s = _attn_probs(qk, band_k, ctx_k, sink_ref, kh, mask4)
            dp_loc = _dot_nt(dok, band_v)
            dp_ctx = _dot_nt(dok, ctx_v)
            delta = jnp.sum(p_loc * dp_loc, axis=-1, keepdims=True) + jnp.sum(p_ctx * dp_ctx, axis=-1, keepdims=True)
            ds_loc = (p_loc * (dp_loc - delta)).astype(BF16)
            ds_ctx = (p_ctx * (dp_ctx - delta)).astype(BF16)
            dsk = p_s * delta
            for hh in range(GQA):
                dsink = dsink - jnp.where(lane == kh * GQA + hh, jnp.sum(dsk[hh * BLK:(hh + 1) * BLK], axis=0, keepdims=True), 0.0)
            dq = dq + jnp.where(grp, _dot(ds_loc, band_k) + _dot(ds_ctx, ctx_k), 0.0)
            dk = dk + _dot_tn(ds_loc, qk)
            dv = dv + _dot_tn(p_loc.astype(BF16), dok)
            dkc = dkc + _dot_tn(ds_ctx, qk)
            dvc = dvc + _dot_tn(p_ctx.astype(BF16), dok)
        dsink_ref[...] += dsink
        dkv_acc[pl.ds(start, 3 * BLK), :LANES] += dk
        dkv_acc[pl.ds(start, 3 * BLK), LANES:] += dv
        dkvc_acc[:, :LANES] += dkc
        dkvc_acc[:, LANES:] += dvc
        c, a, b = cq_ref[...], -saq_ref[...], -sbq_ref[...]
        for s in range(GQA):
            dq_ref[:, s * LANES:(s + 1) * LANES] = _rope(dq[s * BLK:(s + 1) * BLK] * scale, c, a, b).astype(BF16)

        @pl.when(n == nb - 1)
        def _():
            dkv_ref[:, :LANES] = _rope(dkv_acc[:, :LANES], c_ref[...], -sa_ref[...], -sb_ref[...]).astype(BF16)
            dkv_ref[:, LANES:] = dkv_acc[:, LANES:].astype(BF16)
            dkvc_ref[...] = dkvc_acc[...].astype(BF16)

    blk = lambda w: pl.BlockSpec((BLK, w), lambda n: (n, 0))
    return pl.pallas_call(
        body, name=name, grid=(nb,),
        in_specs=[blk(4 * LANES), _full((L, 2 * LANES)), _full((C, 2 * LANES)), pl.BlockSpec(memory_space=pltpu.SMEM),
                  pl.BlockSpec((BLK, 4 * LANES), lambda n: (n, 1)),
                  _full((L, LANES)), _full((L, LANES)), _full((L, LANES)), blk(LANES), blk(LANES), blk(LANES)],
        out_specs=[blk(4 * LANES), _full((L, 2 * LANES)), _full((C, 2 * LANES)), _full((1, LANES))],
        out_shape=[jax.ShapeDtypeStruct((L, 4 * LANES), BF16), jax.ShapeDtypeStruct((L, 2 * LANES), BF16),
                   jax.ShapeDtypeStruct((C, 2 * LANES), BF16), jax.ShapeDtypeStruct((1, LANES), F32)],
        scratch_shapes=[pltpu.VMEM((L, 2 * LANES), F32), pltpu.VMEM((C, 2 * LANES), F32)],
        compiler_params=_cp(("arbitrary",)),
    )(q, kv, kvc, sink, dpa, cos, sa, sb, cos, sa, sb)


def _gelu_parts(x):
    th = jnp.tanh(SQRT_2_OVER_PI * (x + GELU_C * x * x * x))
    return 0.5 * x * (1.0 + th), th


def _gelu_grad(x, th):
    return 0.5 * (1.0 + th) + 0.5 * x * (1.0 - th * th) * SQRT_2_OVER_PI * (1.0 + 3.0 * GELU_C * x * x)


def _layernorm(v):
    mu = jnp.mean(v, axis=-1, keepdims=True)
    vc = v - mu
    rstd = lax.rsqrt(jnp.mean(vc * vc, axis=-1, keepdims=True) + EPS)
    return vc * rstd, rstd


def sgu_fwd(z1, ln_g, ln_b, ws, bst, *, name):
    L, W2 = z1.shape
    W = W2 // 2
    ng = W // LANES

    def body(z_ref, g_ref, b_ref, ws_ref, bs_ref, o_ref):
        z, _ = _gelu_parts(z_ref[...].astype(F32))
        xhat, _ = _layernorm(z[:, W:])
        vln = (xhat * g_ref[...] + b_ref[...]).astype(BF16)
        for gi in range(ng):
            cs = slice(gi * LANES, (gi + 1) * LANES)
            s = _dot(ws_ref[gi].astype(BF16), vln[:, cs]) + bs_ref[:, gi:gi + 1]
            o_ref[:, cs] = (z[:, cs] * s).astype(BF16)

    vec = _full((1, W))
    return pl.pallas_call(
        body, name=name, grid=(L // BLK,),
        in_specs=[pl.BlockSpec((BLK, W2), lambda n: (n, 0)), vec, vec, _full((ng, LANES, LANES)), _full((BLK, ng))],
        out_specs=pl.BlockSpec((BLK, W), lambda n: (n, 0)),
        out_shape=jax.ShapeDtypeStruct((L, W), BF16),
        compiler_params=_cp(("parallel",)),
    )(z1, ln_g, ln_b, ws, bst)


def sgu_bwd(z1, dus, ln_g, ln_b, ws, bst, *, name):
    L, W2 = z1.shape
    W = W2 // 2
    ng = W // LANES

    def body(z_ref, d_ref, g_ref, b_ref, ws_ref, bs_ref, dz_ref, dws_ref, dbs_ref, dg_ref, db_ref, dv_scr):
        @pl.when(pl.program_id(0) == 0)
        def _():
            dws_ref[...] = jnp.zeros_like(dws_ref)
            dbs_ref[...] = jnp.zeros_like(dbs_ref)
            dg_ref[...] = jnp.zeros_like(dg_ref)
            db_ref[...] = jnp.zeros_like(db_ref)

        zp = z_ref[...].astype(F32)
        z, th = _gelu_parts(zp)
        xhat, rstd = _layernorm(z[:, W:])
        vln = (xhat * g_ref[...] + b_ref[...]).astype(BF16)
        d = d_ref[...].astype(F32)
        lane = lax.broadcasted_iota(jnp.int32, (1, LANES), 1)
        dbs = jnp.zeros((BLK, LANES), F32)
        dgel = _gelu_grad(zp, th)
        for gi in range(ng):
            cs = slice(gi * LANES, (gi + 1) * LANES)
            wb = ws_ref[gi].astype(BF16)
            s = _dot(wb, vln[:, cs]) + bs_ref[:, gi:gi + 1]
            dz_ref[:, cs] = (d[:, cs] * s * dgel[:, cs]).astype(BF16)
            ds = d[:, cs] * z[:, cs]
            dbs = dbs + jnp.where(lane == gi, jnp.sum(ds, axis=-1, keepdims=True), 0.0)
            dsb = ds.astype(BF16)
            dws_ref[gi] += _dot_nt(dsb, vln[:, cs])
            dv_scr[:, cs] = _dot_tn(wb, dsb)
        dbs_ref[...] += dbs
        dvln = dv_scr[...]
        dg_ref[...] += _colsum(dvln * xhat)
        db_ref[...] += _colsum(dvln)
        dxh = dvln * g_ref[...]
        dv = rstd * (dxh - jnp.mean(dxh, axis=-1, keepdims=True) - xhat * jnp.mean(dxh * xhat, axis=-1, keepdims=True))
        dz_ref[:, W:] = (dv * dgel[:, W:]).astype(BF16)

    vec = _full((1, W))
    return pl.pallas_call(
        body, name=name, grid=(L // BLK,),
        in_specs=[pl.BlockSpec((BLK, W2), lambda n: (n, 0)), pl.BlockSpec((BLK, W), lambda n: (n, 0)), vec, vec,
                  _full((ng, LANES, LANES)), _full((BLK, ng))],
        out_specs=[pl.BlockSpec((BLK, W2), lambda n: (n, 0)), _full((ng, LANES, LANES)), _full((BLK, LANES)), vec, vec],
        out_shape=[jax.ShapeDtypeStruct((L, W2), BF16), jax.ShapeDtypeStruct((ng, LANES, LANES), F32),
                   jax.ShapeDtypeStruct((BLK, LANES), F32), jax.ShapeDtypeStruct((1, W), F32), jax.ShapeDtypeStruct((1, W), F32)],
        scratch_shapes=[pltpu.VMEM((BLK, W), F32)],
        compiler_params=_cp(("arbitrary",)),
    )(z1, dus, ln_g, ln_b, ws, bst)


def loss_grad(xo, target, *, tm, name):
    T, D = xo.shape

    def body(x_ref, t_ref, l_ref, d_ref):
        @pl.when(pl.program_id(0) == 0)
        def _():
            l_ref[...] = jnp.zeros_like(l_ref)

        e = x_ref[...] - t_ref[...]
        l_ref[...] += 0.5 * jnp.sum(jnp.mean(e * e, axis=-1, keepdims=True), axis=0, keepdims=True)
        d_ref[...] = e * (1.0 / D)

    row = pl.BlockSpec((tm, D), lambda i: (i, 0))
    return pl.pallas_call(
        body, name=name, grid=(T // tm,), in_specs=[row, row], out_specs=[_full((1, 1)), row],
        out_shape=[jax.ShapeDtypeStruct((1, 1), F32), jax.ShapeDtypeStruct((T, D), F32)],
        compiler_params=_cp(("arbitrary",)),
    )(xo, target)


def _adamw_math(w, m, v, g):
    m_ = ADAM_B1 * m + (1.0 - ADAM_B1) * g
    v_ = ADAM_B2 * v + (1.0 - ADAM_B2) * (g * g)
    return -ADAM_LR * ((m_ / BC1) / (jnp.sqrt(v_ / BC2) + ADAM_EPS) + ADAM_WD * w), m_, v_


def adamw(w, m, v, gparts, *, tr, name):
    NL, R, Wd = w.shape
    nr = R // tr

    def body(w_ref, m_ref, v_ref, *rest):
        gp_refs, (g_ref, d_ref, nm_ref, nv_ref) = rest[:NL], rest[NL:]
        for l in range(NL):
            @pl.when(pl.program_id(0) == l)
            def _():
                g = gp_refs[l][0].astype(F32)
                for s in range(1, gp_refs[l].shape[0]):
                    g = g + gp_refs[l][s].astype(F32)
                g_ref[...] = g
                d_ref[...], nm_ref[...], nv_ref[...] = _adamw_math(w_ref[...], m_ref[...], v_ref[...], g)

    row = pl.BlockSpec((None, tr, Wd), lambda l, i: (l, i, 0))
    gspecs = [pl.BlockSpec((gparts[l].shape[0], tr, Wd), (lambda l_, i, l=l: (0, jnp.clip(i + (l_ - l) * nr, 0, nr - 1), 0)))
              for l in range(NL)]
    return pl.pallas_call(
        body, name=name, grid=(NL, nr),
        in_specs=[row, row, row] + gspecs, out_specs=[row] * 4, out_shape=[jax.ShapeDtypeStruct((NL, R, Wd), F32)] * 4,
        compiler_params=_cp(("arbitrary", "arbitrary")),
    )(w, m, v, *gparts)


def small_update(gsum, me, params, loss_row, *, name):
    n = len(params)

    def body(me_ref, gs_ref, *refs):
        ins, outs = refs[:3 * n], refs[3 * n:]
        for p, (w, _, _, off, per_dev) in enumerate(params):
            w_ref, m_ref, v_ref = ins[3 * p:3 * p + 3]
            g_ref, d_ref, nm_ref, nv_ref = outs[4 * p:4 * p + 4]
            rows, cols = w.shape
            if cols == LANES and rows % 8 == 0 and not per_dev:
                g = gs_ref[off:off + rows, :]
                g_ref[...] = g
                d_ref[...], nm_ref[...], nv_ref[...] = _adamw_math(w_ref[...], m_ref[...], v_ref[...], g)
                continue
            chunks = -(-cols // LANES)
            base = off + me_ref[0] * per_dev if per_dev else off
            for i in range(rows):
                for j in range(chunks):
                    wd = min(LANES, cols - j * LANES)
                    at = (slice(i, i + 1), slice(j * LANES, j * LANES + wd))
                    g = gs_ref[pl.ds(base + i * chunks + j, 1), 0:wd]
                    g_ref[at] = g
                    d_ref[at], nm_ref[at], nv_ref[at] = _adamw_math(w_ref[at], m_ref[at], v_ref[at], g)
        outs[-1][...] = jnp.sum(gs_ref[loss_row:loss_row + 1, :], axis=1, keepdims=True)

    vm = pl.BlockSpec(memory_space=pltpu.VMEM)
    flat = [a for w, m, v, _, _ in params for a in (w, m, v)]
    out_shape = [jax.ShapeDtypeStruct(w.shape, F32) for w, _, _, _, _ in params for _ in range(4)] + [jax.ShapeDtypeStruct((1, 1), F32)]
    return pl.pallas_call(
        body, name=name, in_specs=[pl.BlockSpec(memory_space=pltpu.SMEM)] + [vm] * (1 + len(flat)),
        out_specs=[vm] * len(out_shape), out_shape=out_shape,
        compiler_params=pltpu.CompilerParams(vmem_limit_bytes=VMEM_LIMIT),
    )(me, gsum, *flat)


def ada_fwd_mm(cs, w_ada, b_loc, *, name):
    R, D = cs.shape
    nl, _, n = w_ada.shape

    def body(c_ref, w_ref, b_ref, s_ref, m_ref):
        c = c_ref[...]
        s = c * jax.nn.sigmoid(c)
        s_ref[...] = s
        for i in range(nl):
            m_ref[i] = _dot(s.astype(BF16), w_ref[i].astype(BF16)) + b_ref[i:i + 1, :]

    return pl.pallas_call(
        body, name=name, in_specs=[_full((R, D)), _full((nl, D, n)), _full((nl, n))],
        out_specs=[_full((R, D)), _full((nl, R, n))], grid=(1,),
        out_shape=[jax.ShapeDtypeStruct((R, D), F32), jax.ShapeDtypeStruct((nl, R, n), F32)],
        compiler_params=_cp(("arbitrary",)),
    )(cs, w_ada, b_loc)


def ada_bwd_mm(s, c_ctx, dall, w_ada, *, name):
    R, D = s.shape
    nl, _, n = w_ada.shape

    def body(s_ref, cc_ref, d_ref, w_ref, gw_ref, dcc_ref):
        sb = s_ref[...].astype(BF16)
        row = lax.broadcasted_iota(jnp.int32, (R, 1), 0)
        dctx = d_ref[0, 1:2, :]
        for dv in range(1, N_DEV):
            dctx = dctx + d_ref[dv, 1:2, :]
        for i in range(nl):
            dm = jnp.zeros((R, n), F32)
            for dv in range(N_DEV):
                dm = dm + jnp.where(row == dv, d_ref[dv, 2 * i:2 * i + 1, :], 0.0)
            if i == 0:
                dm = dm + jnp.where(row == N_DEV, dctx, 0.0)
            gw_ref[i] = _dot_tn(sb, dm.astype(BF16))
        cc = cc_ref[...]
        sg = jax.nn.sigmoid(cc)
        ds = _dot_nt(jnp.broadcast_to(dctx, (8, n)).astype(BF16), w_ref[0].astype(BF16))
        dcc_ref[...] = ds * (sg * (1.0 + cc * (1.0 - sg)))

    return pl.pallas_call(
        body, name=name, grid=(1,),
        in_specs=[_full((R, D)), _full((1, D)), _full((N_DEV, 3, n)), _full((nl, D, n))],
        out_specs=[_full((nl, D, n)), _full((8, D))],
        out_shape=[jax.ShapeDtypeStruct((nl, D, n), F32), jax.ShapeDtypeStruct((8, D), F32)],
        compiler_params=_cp(("arbitrary",)),
    )(s, c_ctx, dall, w_ada)


def _place():
    x, y, c = lax.axis_index("x"), lax.axis_index("y"), lax.axis_index("c")
    return x, y, c


def _lin(p):
    return 4 * p[0] + 2 * p[1] + p[2]


def all_gather_small(xb, *, reduce=False, name):
    R, W = xb.shape

    def body(x_ref, *rest):
        out_ref = rest[0]
        send_sems, recv_sems, local_sem = rest[-3:]
        x, y, c = _place()
        me, sibling = (x, y, c), (x, y, 1 - c)
        chips = [(1 - x, y), (x, 1 - y), (1 - x, 1 - y)]

        def copy(k, block, to, src=None):
            dst = out_ref.at[_lin(block)]
            return pltpu.make_async_remote_copy(
                src_ref=dst if src is None else src, dst_ref=dst, send_sem=send_sems.at[k], recv_sem=recv_sems.at[k],
                device_id=to, device_id_type=MESH)

        mine = pltpu.make_async_copy(x_ref, out_ref.at[_lin(me)], local_sem)
        mine.start()
        first = [copy(0, me, sibling, src=x_ref)]
        first += [copy(1 + j, me, (*chip, c), src=x_ref) for j, chip in enumerate(chips)]
        for cp in first:
            cp.start()
        passed = [copy(4 + j, (*chip, c), sibling) for j, chip in enumerate(chips)]
        for j, chip in enumerate(chips):
            copy(1 + j, (*chip, c), me).wait_recv()
            passed[j].start()
        copy(0, sibling, me).wait_recv()
        for j, chip in enumerate(chips):
            copy(4 + j, (*chip, 1 - c), me).wait_recv()
        for cp in first + passed:
            cp.wait_send()
        mine.wait()
        if reduce:
            acc = out_ref[0].astype(F32)
            for dv in range(1, N_DEV):
                acc = acc + out_ref[dv].astype(F32)
            rest[1][...] = acc

    vm = pl.BlockSpec(memory_space=pltpu.VMEM)
    out_shape = [jax.ShapeDtypeStruct((N_DEV, R, W), xb.dtype)]
    if reduce:
        out_shape.append(jax.ShapeDtypeStruct((R, W), F32))
    res = pl.pallas_call(
        body, name=name, in_specs=[vm], out_specs=[vm] * len(out_shape), out_shape=out_shape,
        scratch_shapes=[pltpu.SemaphoreType.DMA((7,)), pltpu.SemaphoreType.DMA((7,)), pltpu.SemaphoreType.DMA],
        compiler_params=pltpu.CompilerParams(vmem_limit_bytes=VMEM_LIMIT),
    )(xb)
    return res if reduce else res[0]


HBM_SPEC = pl.BlockSpec(memory_space=pltpu.HBM)
SEM_SPEC = pl.BlockSpec(memory_space=pltpu.SEMAPHORE)
ORDERED_EFFECT = pltpu.SideEffectType.DATAFLOW_SIDE_EFFECTING


def _exchange_copies(srcs, lands, sems, scatter):
    x, y, c = _place()
    me = _lin((x, y, c))
    for j in range(len(srcs)):
        r = lands[j].shape[0] // N_DEV
        block = lambda d, j=j, r=r: pl.ds(pl.multiple_of(d * r, 16), r)
        for k in range(1, N_DEV):
            peer = (x ^ (k >> 2), y ^ ((k >> 1) & 1), c ^ (k & 1))
            src = srcs[j].at[block(_lin(peer)), :] if scatter else srcs[j]
            mk = lambda dst, j=j, k=k, peer=peer, src=src: pltpu.make_async_remote_copy(
                src_ref=src, dst_ref=dst, send_sem=sems[2 * j].at[k - 1], recv_sem=sems[2 * j + 1].at[k - 1],
                device_id=peer, device_id_type=MESH)
            yield mk(lands[j].at[block(me), :]), mk(lands[j].at[block(_lin(peer)), :])


def exchange_start(srcs, lands, *, scatter, name):
    nw = len(srcs)

    def body(*refs):
        for start, _ in _exchange_copies(refs[:nw], refs[nw:2 * nw], refs[2 * nw:4 * nw], scatter):
            start.start()
        refs[-1][...] = jnp.zeros_like(refs[-1])

    thru = [pltpu.HBM(a.shape, a.dtype) for a in (*srcs, *lands)]
    res = pl.pallas_call(
        body, name=name, in_specs=[HBM_SPEC] * (2 * nw),
        out_specs=[SEM_SPEC] * (2 * nw) + [HBM_SPEC] * (2 * nw) + [pl.BlockSpec(memory_space=pltpu.VMEM)],
        out_shape=[pltpu.SemaphoreType.DMA((N_DEV - 1,))] * (2 * nw) + thru + [jax.ShapeDtypeStruct((8, LANES), F32)],
        input_output_aliases={i: 2 * nw + i for i in range(2 * nw)},
        compiler_params=pltpu.CompilerParams(has_side_effects=ORDERED_EFFECT),
    )(*[pltpu.with_memory_space_constraint(a, pltpu.HBM) for a in (*srcs, *lands)])
    return res[:2 * nw], res[2 * nw:3 * nw], res[3 * nw:4 * nw], res[-1]


def exchange_wait(srcs, lands, sems, after, *, scatter, name):
    nw = len(srcs)

    def body(*refs):
        for _, arrive in _exchange_copies(refs[:nw], refs[nw:2 * nw], refs[2 * nw:4 * nw], scatter):
            arrive.wait_send()
            arrive.wait_recv()

    res = pl.pallas_call(
        body, name=name, in_specs=[HBM_SPEC] * (2 * nw) + [SEM_SPEC] * (2 * nw) + [pl.BlockSpec(memory_space=pl.ANY)],
        out_specs=[HBM_SPEC] * (2 * nw), out_shape=[pltpu.HBM(a.shape, a.dtype) for a in (*srcs, *lands)],
        input_output_aliases={i: i for i in range(2 * nw)},
        compiler_params=pltpu.CompilerParams(has_side_effects=ORDERED_EFFECT),
    )(*srcs, *lands, *sems, after)
    return res[nw:]


def place_own(srcs, rows, me, *, scatter, name):
    nw = len(srcs)
    lands = [lax.empty((N_DEV * r, s_.shape[1]), s_.dtype) for r, s_ in zip(rows, srcs)]

    def body(me_ref, *refs):
        for j in range(nw):
            refs[2 * nw + j][...] = refs[j][...]

    mine = lambda i, me_ref: (me_ref[0], 0)
    src_at = mine if scatter else (lambda i, me_ref: (0, 0))
    blocks = [(r, s_.shape[1]) for r, s_ in zip(rows, srcs)]
    return pl.pallas_call(
        body, name=name,
        grid_spec=pltpu.PrefetchScalarGridSpec(
            num_scalar_prefetch=1, grid=(1,),
            in_specs=[pl.BlockSpec(b_, src_at) for b_ in blocks] + [pl.BlockSpec(memory_space=pl.ANY)] * nw,
            out_specs=[pl.BlockSpec(b_, mine) for b_ in blocks]),
        out_shape=[jax.ShapeDtypeStruct(l_.shape, l_.dtype) for l_ in lands],
        input_output_aliases={1 + nw + j: j for j in range(nw)},
        compiler_params=_cp(("arbitrary",)),
    )(jnp.reshape(me, (1,)).astype(jnp.int32), *srcs, *lands)


def _rope_tables(L):
    t = jnp.arange(L)
    inv = ROPE_BASE ** (-jnp.arange(ROPE_FREQS, dtype=F32) / ROPE_FREQS)
    ar = (t // GRID_W).astype(F32)[:, None] * inv
    ac = (t % GRID_W).astype(F32)[:, None] * inv
    z = jnp.zeros_like(ar)
    cos = jnp.concatenate([jnp.cos(ar), jnp.cos(ar), jnp.cos(ac), jnp.cos(ac)], axis=1)
    sa = jnp.concatenate([-jnp.sin(ar), z, -jnp.sin(ac), z], axis=1)
    sb = jnp.concatenate([z, jnp.sin(ar), z, jnp.sin(ac)], axis=1)
    return tuple(jnp.tile(a, (1, LANES // HEAD_DIM)) for a in (cos, sa, sb))


def _nat2d(a):
    return a.reshape(1, -1) if a.ndim == 1 else a.reshape(-1, a.shape[-1])


def _pack_rows(a):
    rows, cols = a.shape
    chunks = -(-cols // LANES)
    f = jnp.pad(a, ((0, 0), (0, chunks * LANES - cols))).reshape(rows * chunks, LANES)
    return jnp.pad(f, ((0, -f.shape[0] % 8), (0, 0)))


def _rows128(a):
    f = a.reshape(-1)
    n = -(-f.shape[0] // (8 * LANES)) * 8 * LANES
    return jnp.pad(f, (0, n - f.shape[0])).reshape(-1, LANES)


def kernel(x, c, ctx, c_ctx, w_ada, b_ada, g_mix_pre, g_mix_post, g_ffn_pre, g_ffn_post, w_in_even, w_pool, pool_scale, attn_sink, w_out_even, w_in_odd, sgu_ln_g, sgu_ln_b, sgu_w, sgu_b, w_out_odd, w_ffn_up, ffn_conv_w, ffn_conv_b, w_ffn_down, loss_target, m_c_ctx, m_w_ada, m_b_ada, m_g_mix_pre, m_g_mix_post, m_g_ffn_pre, m_g_ffn_post, m_w_in_even, m_w_pool, m_pool_scale, m_attn_sink, m_w_out_even, m_w_in_odd, m_sgu_ln_g, m_sgu_ln_b, m_sgu_w, m_sgu_b, m_w_out_odd, m_w_ffn_up, m_ffn_conv_w, m_ffn_conv_b, m_w_ffn_down, v_c_ctx, v_w_ada, v_b_ada, v_g_mix_pre, v_g_mix_post, v_g_ffn_pre, v_g_ffn_post, v_w_in_even, v_w_pool, v_pool_scale, v_attn_sink, v_w_out_even, v_w_in_odd, v_sgu_ln_g, v_sgu_ln_b, v_sgu_w, v_sgu_b, v_w_out_odd, v_w_ffn_up, v_ffn_conv_w, v_ffn_conv_b, v_w_ffn_down):
    P = dict(c_ctx=c_ctx, w_ada=w_ada, b_ada=b_ada, g_mix_pre=g_mix_pre, g_mix_post=g_mix_post, g_ffn_pre=g_ffn_pre,
             g_ffn_post=g_ffn_post, w_in_even=w_in_even, w_pool=w_pool, pool_scale=pool_scale, attn_sink=attn_sink,
             w_out_even=w_out_even, w_in_odd=w_in_odd, sgu_ln_g=sgu_ln_g, sgu_ln_b=sgu_ln_b, sgu_w=sgu_w, sgu_b=sgu_b,
             w_out_odd=w_out_odd, w_ffn_up=w_ffn_up, ffn_conv_w=ffn_conv_w, ffn_conv_b=ffn_conv_b, w_ffn_down=w_ffn_down)
    M = dict(c_ctx=m_c_ctx, w_ada=m_w_ada, b_ada=m_b_ada, g_mix_pre=m_g_mix_pre, g_mix_post=m_g_mix_post, g_ffn_pre=m_g_ffn_pre,
             g_ffn_post=m_g_ffn_post, w_in_even=m_w_in_even, w_pool=m_w_pool, pool_scale=m_pool_scale, attn_sink=m_attn_sink,
             w_out_even=m_w_out_even, w_in_odd=m_w_in_odd, sgu_ln_g=m_sgu_ln_g, sgu_ln_b=m_sgu_ln_b, sgu_w=m_sgu_w, sgu_b=m_sgu_b,
             w_out_odd=m_w_out_odd, w_ffn_up=m_w_ffn_up, ffn_conv_w=m_ffn_conv_w, ffn_conv_b=m_ffn_conv_b, w_ffn_down=m_w_ffn_down)
    V = dict(c_ctx=v_c_ctx, w_ada=v_w_ada, b_ada=v_b_ada, g_mix_pre=v_g_mix_pre, g_mix_post=v_g_mix_post, g_ffn_pre=v_g_ffn_pre,
             g_ffn_post=v_g_ffn_post, w_in_even=v_w_in_even, w_pool=v_w_pool, pool_scale=v_pool_scale, attn_sink=v_attn_sink,
             w_out_even=v_w_out_even, w_in_odd=v_w_in_odd, sgu_ln_g=v_sgu_ln_g, sgu_ln_b=v_sgu_ln_b, sgu_w=v_sgu_w, sgu_b=v_sgu_b,
             w_out_odd=v_w_out_odd, w_ffn_up=v_w_ffn_up, ffn_conv_w=v_ffn_conv_w, ffn_conv_b=v_ffn_conv_b, w_ffn_down=v_w_ffn_down)

    x = x[0]
    ctx = ctx[0]
    target = loss_target[0]
    L, D = x.shape
    C = ctx.shape[0]
    tm = min(512, L)
    tm_up = min(1024, L)
    conv_rows = min(256, L)
    me = 4 * lax.axis_index("x") + 2 * lax.axis_index("y") + lax.axis_index("c")
    n_ada = w_ada.shape[2]
    F = w_ffn_down.shape[1] * N_DEV
    half_f = F // 2

    n_cw = ffn_conv_w.shape[2]
    small = jnp.concatenate([_rows128(c), _rows128(sgu_ln_g), _rows128(sgu_ln_b), _rows128(ffn_conv_w)], axis=0)
    small_all = all_gather_small(small, name="gather_small_inputs")
    c_all = small_all[:, :8].reshape(N_DEV, D)
    ln_g = small_all[:, 8].reshape(1, D)
    ln_b = small_all[:, 16].reshape(1, D)
    conv_w = small_all[:, 24:].reshape(N_DEV, -1)[:, :2 * 3 * n_cw].reshape(N_DEV, 2, 3, n_cw)
    conv_w = conv_w.transpose(1, 2, 0, 3).reshape(2, 3, 2 * F)

    cs = jnp.concatenate([c_all, c_ctx[None, :], jnp.zeros((7, D), F32)], axis=0)
    b_loc = lax.dynamic_slice(b_ada, (0, me * n_ada), (2, n_ada))
    silu_c, mods_loc = ada_fwd_mm(cs, w_ada, b_loc, name="ada_fwd")
    mods_all = all_gather_small(mods_loc.reshape(-1, LANES), name="gather_mods")

    shards = [s.astype(BF16) for s in (w_in_even[0].T, w_out_even[0], w_ffn_up[0].T, w_ffn_down[0],
                                       w_in_odd[0].T, w_out_odd[0], w_ffn_up[1].T, w_ffn_down[1])]
    shards, mods_all = lax.optimization_barrier((shards, mods_all))
    w_sems, w_srcs, w_lands, _ = exchange_start(shards, place_own(shards, [s.shape[0] for s in shards], me, scatter=False, name="gather_own"),
                                              scatter=False, name="gather_start")

    def weight(j, after):
        return exchange_wait([w_srcs[j]], [w_lands[j]], w_sems[2 * j:2 * j + 2], after, scatter=False, name=f"gather_wait_{j}")[0]

    mods_all = mods_all.reshape(N_DEV, 2, 16, n_ada).transpose(1, 2, 0, 3).reshape(2, 16, 6 * D)
    mod = lambda i, row: [m_[None, :] for m_ in jnp.split(lax.dynamic_index_in_dim(mods_all[i], row, 0, False), 6)]
    sh_m, sc_m, gt_m, sh_f, sc_f, gt_f = zip(mod(0, me), mod(1, me))
    csh_m, csc_m = mod(0, N_DEV)[:2]

    row = lambda a, i: a[i][None, :]

    cos, sa, sb = _rope_tables(L)
    sink = attn_sink[0]
    bst = sgu_b[0].T
    wup, wdn = [None, None], [None, None]

    def ffn_fwd(i, xin):
        wup[i] = weight(2 + 4 * i, xin)
        h, hu = pre_mm(xin, row(g_ffn_pre, i), sh_f[i], sc_f[i], wup[i], tm=tm_up, tn=half_f, name=f"ffn_up_{i}")
        a, s1, s2 = conv_fwd(hu, conv_w[i], ffn_conv_b[i][None, :], rows=conv_rows, wblk=2 * LANES, name=f"ffn_conv_{i}")
        wdn[i] = weight(3 + 4 * i, a)
        f, xo = mm_post(a, wdn[i], xin, row(g_ffn_post, i), gt_f[i], tm=tm, name=f"ffn_down_{i}")
        return h, (hu, s1, s2), a, f, xo

    win_e = permute_heads(weight(0, sh_m[0]))
    h0, u, q, kv = inproj_even(x, row(g_mix_pre, 0), sh_m[0], sc_m[0], win_e, cos, sa, sb, tm=tm, name="in_even")
    hc, kvc = pre_mm(ctx, row(g_mix_pre, 0), csh_m, csc_m, win_e, tm=C, tn=2 * LANES, w_row_off=8 * LANES, name="in_even_ctx")
    pa = jnp.concatenate([pool_fwd(u, w_pool[0], pool_scale, name="pool_fwd"),
                          attn_fwd(q, kv, kvc, sink, name="attn_fwd")], axis=1)
    wout_e = permute_heads(weight(1, pa))
    y0, x1 = mm_post(pa, wout_e, x, row(g_mix_post, 0), gt_m[0], tm=tm, name="out_even")
    h1, hu0, a0, f0, x2 = ffn_fwd(0, x1)
    win_o = weight(4, x2)
    h2, z1 = pre_mm(x2, row(g_mix_pre, 1), sh_m[1], sc_m[1], win_o, tm=tm_up, tn=D, name="in_odd")
    us = sgu_fwd(z1, ln_g, ln_b, sgu_w[0], bst, name="sgu_fwd")
    wout_o = weight(5, us)
    y1, x3 = mm_post(us, wout_o, x2, row(g_mix_post, 1), gt_m[1], tm=tm, name="out_odd")
    h3, hu1, a1, f1, x4 = ffn_fwd(1, x3)
    loss_part, dx4 = loss_grad(x4, target, tm=tm, name="loss")

    g_srcs, g_lands, g_sems = [], [], []

    def scatter(grads, nm):
        own = place_own(grads, [g.shape[0] // N_DEV for g in grads], me, scatter=True, name=nm.replace("start", "own"))
        sems, srcs, lands, tok = exchange_start(grads, own, scatter=True, name=nm)
        g_srcs.extend(srcs)
        g_lands.extend(lands)
        g_sems.extend(sems)
        return tok[0:1, 0:1]

    def ffn_bwd(i, dxo, xin, h, hu, a, f, g_post):
        dyf, da, dg_post, dgt = post_bwd_mm(dxo, f, g_post, gt_f[i], wdn[i], tm=tm, name=f"ffn_down_bwd_{i}")
        dhg, dhu, dcwg, dcwu, dcbg, dcbu = conv_bwd(da, hu[1], hu[2], hu[0], conv_w[i], rows=conv_rows, wblk=2 * LANES,
                                                    name=f"ffn_conv_bwd_{i}")
        dxin, dg_pre, dsh, dsc = mm_pre_bwd([dhg, dhu], wup[i], xin, dxo, row(g_ffn_pre, i), sc_f[i], tm=tm, tk=half_f,
                                            name=f"ffn_up_bwd_{i}")
        g_dn = wgrad([a], dyf, tr=2 * LANES, name=f"wgrad_down_{i}")
        g_up = wgrad([dhg, dhu], h, tr=2 * LANES, name=f"wgrad_up_{i}")
        tok = scatter([g_dn, g_up], f"scatter_start_ffn_{i}")
        return dxin, tok, dict(g_ffn_post=dg_post, g_ffn_pre=dg_pre, gt_f=dgt, sh_f=dsh, sc_f=dsc,
                               ffn_conv_w=jnp.concatenate([dcwg, dcwu], axis=1), ffn_conv_b=jnp.concatenate([dcbg, dcbu], axis=1)[0])

    dx3, tok, sf1 = ffn_bwd(1, dx4, x3, h3, hu1, a1, f1, row(g_ffn_post, 1))
    dy1, dus, dg_mpost1, dgt_m1 = post_bwd_mm(dx3, y1, row(g_mix_post, 1) + tok, gt_m[1], wout_o, tm=tm, name="out_odd_bwd")
    dz1, dws, dbs, dlng, dlnb = sgu_bwd(z1, dus, ln_g, ln_b, sgu_w[0], bst, name="sgu_bwd")
    dx2, dg_mpre1, dsh_m1, dsc_m1 = mm_pre_bwd([dz1], win_o, x2, dx3, row(g_mix_pre, 1), sc_m[1], tm=tm, tk=D, name="in_odd_bwd")
    tok = scatter([wgrad([us], dy1, tr=2 * LANES, name="wgrad_out_odd"), wgrad([dz1], h2, tr=2 * LANES, name="wgrad_in_odd")],
                  "scatter_start_mix_1")

    dx1, tok, sf0 = ffn_bwd(0, dx2, x1, h1, hu0, a0, f0, row(g_ffn_post, 0) + tok)
    dy0, dpa, dg_mpost0, dgt_m0 = post_bwd_mm(dx1, y0, row(g_mix_post, 0) + tok, gt_m[0], wout_e, tm=tm, name="out_even_bwd")
    du, dwp, dps = pool_bwd(u, dpa, w_pool[0], pool_scale, name="pool_bwd")
    dq, dkv, dkvc, dsink = attn_bwd(q, kv, kvc, sink, dpa, cos, sa, sb, name="attn_bwd")
    dz0 = jnp.concatenate([du, dq, dkv], axis=1)
    dzc = jnp.concatenate([jnp.zeros((C, 8 * LANES), BF16), dkvc], axis=1)
    tok = scatter([permute_heads(wgrad([pa], dy0, tr=2 * LANES, name="wgrad_out_even"), inverse=True),
                   permute_heads(wgrad([dz0], h0, tr=2 * LANES, extra=(dzc, hc), name="wgrad_in_even"), inverse=True)],
                  "scatter_start_mix_0")
    grad_x, dg_mpre0, dsh_m0, dsc_m0 = mm_pre_bwd([dz0], win_e, x, dx1, row(g_mix_pre, 0) + tok, sc_m[0], tm=tm, tk=dz0.shape[1],
                                                  name="in_even_bwd")
    _, dg_mpre0c, dcsh, dcsc = mm_pre_bwd([dkvc], win_e, ctx, None, row(g_mix_pre, 0), csc_m, tm=C, tk=2 * LANES,
                                          w_row_off=8 * LANES, name="in_even_ctx_bwd")

    slots = exchange_wait(g_srcs, g_lands, g_sems, dcsh, scatter=True, name="scatter_wait")
    out = {}

    def update(name, lands, transposed):
        w_, m_, v_ = (a.transpose(0, 2, 1) if transposed else a for a in (P[name], M[name], V[name]))
        r = w_.shape[1]
        tr = r // 4 if r % 64 == 0 and r > 256 else r
        res = adamw(w_, m_, v_, [l_.reshape(N_DEV, r, l_.shape[1]) for l_ in lands], tr=tr, name=f"adamw_{name}")
        for kind, val in zip(("grad", "delta", "new_m", "new_v"), res):
            out[(kind, name)] = val.transpose(0, 2, 1) if transposed else val

    update("w_in_even", [slots[7]], True)
    update("w_out_even", [slots[6]], False)
    update("w_in_odd", [slots[3]], True)
    update("w_out_odd", [slots[2]], False)
    update("w_ffn_up", [slots[5], slots[1]], True)
    update("w_ffn_down", [slots[4], slots[0]], False)

    zero = jnp.zeros((1, D), F32)
    dmod0 = jnp.concatenate([dsh_m0, dsc_m0, dgt_m0, sf0["sh_f"], sf0["sc_f"], sf0["gt_f"]], axis=1)
    dmodc = jnp.concatenate([dcsh, dcsc, zero, zero, zero, zero], axis=1)
    dmod1 = jnp.concatenate([dsh_m1, dsc_m1, dgt_m1, sf1["sh_f"], sf1["sc_f"], sf1["gt_f"]], axis=1)
    dmods = jnp.concatenate([dmod0, dmodc, dmod1], axis=0)
    dmods_all = all_gather_small(dmods.reshape(-1, LANES), name="gather_dmods").reshape(N_DEV, 3, N_DEV, n_ada)
    dall = lax.dynamic_index_in_dim(dmods_all, me, 2, False)
    g_w_ada, dcc = ada_bwd_mm(silu_c, c_ctx[None, :], dall, w_ada, name="ada_bwd")
    nl = w_ada.shape[0]
    res = adamw(w_ada, m_w_ada, v_w_ada, [g_w_ada[l][None] for l in range(nl)], tr=D // 4, name="adamw_w_ada")
    for kind, val in zip(("grad", "delta", "new_m", "new_v"), res):
        out[(kind, "w_ada")] = val

    rep = dict(
        c_ctx=dcc[0:1],
        b_ada=jnp.concatenate([dmod0 + dmodc, dmod1]),
        g_mix_pre=jnp.concatenate([dg_mpre0 + dg_mpre0c, dg_mpre1]),
        g_mix_post=jnp.concatenate([dg_mpost0, dg_mpost1]),
        g_ffn_pre=jnp.concatenate([sf0["g_ffn_pre"], sf1["g_ffn_pre"]]),
        g_ffn_post=jnp.concatenate([sf0["g_ffn_post"], sf1["g_ffn_post"]]),
        w_pool=_nat2d(dwp), pool_scale=dps, attn_sink=dsink[:, :N_Q_HEADS],
        sgu_w=_nat2d(dws), sgu_b=dbs[:, :sgu_b.shape[1]].T,
        ffn_conv_b=jnp.stack([sf0["ffn_conv_b"], sf1["ffn_conv_b"]]),
    )
    hi = loss_part.astype(BF16).astype(F32)
    mid = (loss_part - hi).astype(BF16).astype(F32)
    loss_piece = jnp.pad(jnp.concatenate([hi, mid, loss_part - hi - mid], axis=1), ((0, 7), (0, LANES - 3)))
    conv_g = jnp.stack([sf0["ffn_conv_w"], sf1["ffn_conv_w"]]).reshape(2 * 3, N_DEV, n_cw).swapaxes(0, 1)
    shard_full = dict(sgu_ln_g=dlng.reshape(N_DEV, LANES), sgu_ln_b=dlnb.reshape(N_DEV, LANES),
                      ffn_conv_w=jnp.concatenate([_pack_rows(conv_g[d]) for d in range(N_DEV)], axis=0))
    small_names = list(rep) + list(shard_full)
    pieces = [_pack_rows(rep[k]) for k in rep] + list(shard_full.values()) + [loss_piece]
    sizes = [p.shape[0] for p in pieces]
    offs = [sum(sizes[:i]) for i in range(len(sizes))]
    pieces.append(jnp.zeros((-sum(sizes) % 16, LANES), F32))
    _, gsum = all_gather_small(jnp.concatenate(pieces, axis=0).astype(BF16), reduce=True, name="allreduce_small_grads")
    per_dev = {k: shard_full[k].shape[0] // N_DEV for k in shard_full}
    params = [(_nat2d(P[k]), _nat2d(M[k]), _nat2d(V[k]), offs[i], per_dev.get(k, 0)) for i, k in enumerate(small_names)]
    res = small_update(gsum, jnp.reshape(me, (1,)).astype(jnp.int32), params, offs[-1], name="adamw_small")
    for i, k in enumerate(small_names):
        for kind, val in zip(("grad", "delta", "new_m", "new_v"), res[4 * i:4 * i + 4]):
            out[(kind, k)] = val.reshape(P[k].shape)
    loss = res[-1][0, 0]

    names = list(P)
    final = [loss, grad_x[None]]
    for kind in ("grad", "delta", "new_m", "new_v"):
        for k in names:
            val = out[(kind, k)]
            final.append(val)
    return tuple(final)
```

```python
import functools
import math

import jax
import jax.numpy as jnp
from jax import lax
from jax.experimental import pallas as pl
from jax.experimental.pallas import tpu as pltpu

F32 = jnp.float32
BF16 = jnp.bfloat16
MESH = pl.DeviceIdType.MESH
N_DEV = 8
LANES = 128
VMEM_LIMIT = 48 * 1024 * 1024
EPS = 1e-6
NEG_INF = -1e30
GRID_W = 64
WINDOW = 128
BLK = 128
HEAD_DIM = 64
N_Q_HEADS = 8
N_KV_HEADS = 2
GQA = N_Q_HEADS // N_KV_HEADS
POOL_WINDOWS = (2, 4, 8, 16)
ROPE_BASE = 10000.0
ROPE_FREQS = HEAD_DIM // 4
PAD = 16
ADAM_LR, ADAM_B1, ADAM_B2, ADAM_EPS, ADAM_WD, ADAM_STEP = 0.001, 0.9, 0.999, 1e-08, 0.01, 10
BC1 = 1.0 - ADAM_B1 ** ADAM_STEP
BC2 = 1.0 - ADAM_B2 ** ADAM_STEP
SQRT_2_OVER_PI = math.sqrt(2.0 / math.pi)
GELU_C = 0.044715


def _cp(sem=None):
    return pltpu.CompilerParams(dimension_semantics=sem, vmem_limit_bytes=VMEM_LIMIT)


def _dot(a, b):
    return jnp.dot(a, b, preferred_element_type=F32)


def _dot_nt(a, b):
    return lax.dot_general(a, b, (((1,), (1,)), ((), ())), preferred_element_type=F32)


def _dot_tn(a, b):
    return lax.dot_general(a, b, (((0,), (0,)), ((), ())), preferred_element_type=F32)


def _rms(x):
    r = lax.rsqrt(jnp.mean(x * x, axis=-1, keepdims=True) + EPS)
    return x * r, r


def _rms_bwd(dn, n, r):
    return r * (dn - n * jnp.mean(dn * n, axis=-1, keepdims=True))


def _colsum(a):
    return jnp.sum(a, axis=0, keepdims=True)


def _rope(x, c, sa, sb):
    return x * c + pltpu.roll(x, LANES - ROPE_FREQS, 1) * sa + pltpu.roll(x, ROPE_FREQS, 1) * sb


def _full(shape):
    return pl.BlockSpec(shape, lambda *_: (0,) * len(shape))


def pre_mm(x, g, sh, sc, wt, *, tm, tn, w_row_off=0, name):
    T, D = x.shape
    n_rows = wt.shape[0] - w_row_off
    off = w_row_off // tn

    def body(x_ref, g_ref, sh_ref, sc_ref, w_ref, h_ref, z_ref):
        @pl.when(pl.program_id(1) == 0)
        def _():
            n, _ = _rms(x_ref[...])
            h_ref[...] = (n * g_ref[...] * (1.0 + sc_ref[...]) + sh_ref[...]).astype(BF16)

        z_ref[...] = _dot_nt(h_ref[...], w_ref[...]).astype(BF16)

    vec = pl.BlockSpec((1, D), lambda i, j: (0, 0))
    return pl.pallas_call(
        body, name=name, grid=(T // tm, n_rows // tn),
        in_specs=[pl.BlockSpec((tm, D), lambda i, j: (i, 0)), vec, vec, vec, pl.BlockSpec((tn, D), lambda i, j: (j + off, 0))],
        out_specs=[pl.BlockSpec((tm, D), lambda i, j: (i, 0)), pl.BlockSpec((tm, tn), lambda i, j: (i, j))],
        out_shape=[jax.ShapeDtypeStruct((T, D), BF16), jax.ShapeDtypeStruct((T, n_rows), BF16)],
        compiler_params=_cp(("parallel", "arbitrary")),
    )(x, g, sh, sc, wt)


def inproj_even(x, g, sh, sc, wt, cos, sa, sb, *, tm, name):
    T, D = x.shape
    N = wt.shape[0]

    def body(x_ref, g_ref, sh_ref, sc_ref, w_ref, c_ref, sa_ref, sb_ref, h_ref, u_ref, q_ref, kv_ref):
        n, _ = _rms(x_ref[...])
        h = (n * g_ref[...] * (1.0 + sc_ref[...]) + sh_ref[...]).astype(BF16)
        h_ref[...] = h
        z = _dot_nt(h, w_ref[...])
        u_ref[...] = z[:, :4 * LANES]
        c, a, b = c_ref[...], sa_ref[...], sb_ref[...]
        for s in range(4):
            q_ref[:, s * LANES:(s + 1) * LANES] = _rope(z[:, (4 + s) * LANES:(5 + s) * LANES], c, a, b).astype(BF16)
        kv_ref[:, :LANES] = _rope(z[:, 8 * LANES:9 * LANES], c, a, b).astype(BF16)
        kv_ref[:, LANES:] = z[:, 9 * LANES:].astype(BF16)

    vec = pl.BlockSpec((1, D), lambda i: (0, 0))
    row = lambda w: pl.BlockSpec((tm, w), lambda i: (i, 0))
    return pl.pallas_call(
        body, name=name, grid=(T // tm,),
        in_specs=[row(D), vec, vec, vec, _full((N, D)), row(LANES), row(LANES), row(LANES)],
        out_specs=[row(D), row(4 * LANES), row(4 * LANES), row(2 * LANES)],
        out_shape=[jax.ShapeDtypeStruct((T, D), BF16), jax.ShapeDtypeStruct((T, 4 * LANES), F32),
                   jax.ShapeDtypeStruct((T, 4 * LANES), BF16), jax.ShapeDtypeStruct((T, 2 * LANES), BF16)],
        compiler_params=_cp(("parallel",)),
    )(x, g, sh, sc, wt, cos, sa, sb)


def mm_post(a, w, x, g, gt, *, tm, name):
    T, K = a.shape
    D = w.shape[1]

    def body(a_ref, w_ref, x_ref, g_ref, gt_ref, y_ref, xn_ref):
        y = _dot(a_ref[...], w_ref[...])
        n, _ = _rms(y)
        y_ref[...] = y.astype(BF16)
        xn_ref[...] = x_ref[...] + gt_ref[...] * (n * g_ref[...])

    vec = pl.BlockSpec((1, D), lambda i: (0, 0))
    row = lambda w_: pl.BlockSpec((tm, w_), lambda i: (i, 0))
    return pl.pallas_call(
        body, name=name, grid=(T // tm,),
        in_specs=[row(K), _full((K, D)), row(D), vec, vec],
        out_specs=[row(D), row(D)],
        out_shape=[jax.ShapeDtypeStruct((T, D), BF16), jax.ShapeDtypeStruct((T, D), F32)],
        compiler_params=_cp(("parallel",)),
    )(a, w, x, g, gt)


def post_bwd_mm(dxn, y, g, gt, w, *, tm, name):
    T, D = y.shape
    K = w.shape[0]

    def body(dxn_ref, y_ref, g_ref, gt_ref, w_ref, dy_ref, da_ref, dg_ref, dgt_ref):
        @pl.when(pl.program_id(0) == 0)
        def _():
            dg_ref[...] = jnp.zeros_like(dg_ref)
            dgt_ref[...] = jnp.zeros_like(dgt_ref)

        d = dxn_ref[...]
        n, r = _rms(y_ref[...].astype(F32))
        g_, gt_ = g_ref[...], gt_ref[...]
        dg_ref[...] += _colsum(d * gt_ * n)
        dgt_ref[...] += _colsum(d * g_ * n)
        dy = _rms_bwd(d * (gt_ * g_), n, r).astype(BF16)
        dy_ref[...] = dy
        da_ref[...] = _dot_nt(dy, w_ref[...]).astype(BF16)

    vec = pl.BlockSpec((1, D), lambda i: (0, 0))
    row = lambda w_: pl.BlockSpec((tm, w_), lambda i: (i, 0))
    return pl.pallas_call(
        body, name=name, grid=(T // tm,),
        in_specs=[row(D), row(D), vec, vec, _full((K, D))],
        out_specs=[row(D), row(K), vec, vec],
        out_shape=[jax.ShapeDtypeStruct((T, D), BF16), jax.ShapeDtypeStruct((T, K), BF16),
                   jax.ShapeDtypeStruct((1, D), F32), jax.ShapeDtypeStruct((1, D), F32)],
        compiler_params=_cp(("arbitrary",)),
    )(dxn, y, g, gt, w)


def mm_pre_bwd(dzs, wt, x, dres, g, sc, *, tm, tk, w_row_off=0, name):
    T, N = dzs[0].shape
    D = x.shape[1]
    nk = N // tk
    npart = len(dzs)
    off = w_row_off // tk
    has_res = dres is not None

    def body(*refs):
        dz_refs = refs[:npart]
        w_refs = refs[npart:2 * npart]
        rest = refs[2 * npart:]
        x_ref = rest[0]
        dres_ref = rest[1] if has_res else None
        g_ref, sc_ref, dx_ref, dg_ref, dsh_ref, dsc_ref, acc = rest[1 + has_res:]
        i, k = pl.program_id(0), pl.program_id(1)

        @pl.when(jnp.logical_and(i == 0, k == 0))
        def _():
            dg_ref[...] = jnp.zeros_like(dg_ref)
            dsh_ref[...] = jnp.zeros_like(dsh_ref)
            dsc_ref[...] = jnp.zeros_like(dsc_ref)

        part = _dot(dz_refs[0][...], w_refs[0][...])
        for p in range(1, npart):
            part = part + _dot(dz_refs[p][...], w_refs[p][...])

        @pl.when(k == 0)
        def _():
            acc[...] = part

        @pl.when(k > 0)
        def _():
            acc[...] += part

        @pl.when(k == nk - 1)
        def _():
            dh = acc[...]
            n, r = _rms(x_ref[...])
            g_, s1 = g_ref[...], 1.0 + sc_ref[...]
            dsh_ref[...] += _colsum(dh)
            dsc_ref[...] += _colsum(dh * n * g_)
            dg_ref[...] += _colsum(dh * s1 * n)
            dxp = _rms_bwd(dh * (g_ * s1), n, r)
            dx_ref[...] = dxp + dres_ref[...] if has_res else dxp

    vec = pl.BlockSpec((1, D), lambda i, k: (0, 0))
    row = pl.BlockSpec((tm, D), lambda i, k: (i, 0))
    w_specs = [pl.BlockSpec((tk, D), (lambda i, k, p=p: (k + off + p * nk, 0))) for p in range(npart)]
    res_specs, res_args = ([row], (dres,)) if has_res else ([], ())
    return pl.pallas_call(
        body, name=name, grid=(T // tm, nk),
        in_specs=[pl.BlockSpec((tm, tk), lambda i, k: (i, k))] * npart + w_specs + [row] + res_specs + [vec, vec],
        out_specs=[row, vec, vec, vec],
        out_shape=[jax.ShapeDtypeStruct((T, D), F32)] + [jax.ShapeDtypeStruct((1, D), F32)] * 3,
        scratch_shapes=[pltpu.VMEM((tm, D), F32)],
        compiler_params=_cp(("arbitrary", "arbitrary")),
    )(*dzs, *([wt] * npart), x, *res_args, g, sc)


def wgrad(a_parts, b, *, tr, extra=None, name):
    T, R = a_parts[0].shape
    D = b.shape[1]
    npart = len(a_parts)
    nr = R // tr

    def body(*refs):
        a_refs, b_ref = refs[:npart], refs[npart]
        g_ref = refs[-1]
        for p in range(npart):
            @pl.when(pl.program_id(0) // nr == p)
            def _():
                acc = _dot_tn(a_refs[p][...], b_ref[...])
                if extra is not None:
                    acc += _dot_tn(refs[npart + 1][...], refs[npart + 2][...])
                g_ref[...] = acc.astype(BF16)

    in_specs = [pl.BlockSpec((T, tr), (lambda r, p=p: (0, jnp.clip(r - p * nr, 0, nr - 1)))) for p in range(npart)]
    in_specs.append(_full((T, D)))
    args = [*a_parts, b]
    if extra is not None:
        a2, b2 = extra
        in_specs += [pl.BlockSpec((a2.shape[0], tr), lambda r: (0, r)), _full(b2.shape)]
        args += [a2, b2]
    return pl.pallas_call(
        body, name=name, grid=(npart * nr,),
        in_specs=in_specs, out_specs=pl.BlockSpec((tr, D), lambda r: (r, 0)),
        out_shape=jax.ShapeDtypeStruct((npart * R, D), BF16),
        compiler_params=_cp(("parallel",)),
    )(*args)


def _conv_ext(ref, r0, rows, total):
    top = ref[pl.ds(pl.multiple_of(jnp.maximum(r0 - PAD, 0), PAD), PAD), :]
    mid = ref[pl.ds(r0, rows), :]
    bot = ref[pl.ds(pl.multiple_of(jnp.minimum(r0 + rows, total - PAD), PAD), PAD), :]
    top = jnp.where(r0 > 0, top, jnp.zeros_like(top))
    bot = jnp.where(r0 + rows < total, bot, jnp.zeros_like(bot))
    return jnp.concatenate([top, mid, bot], axis=0).astype(F32)


def _shift_rows(a, k):
    return pltpu.roll(a, k % a.shape[0], 0)


def _conv3(x, w, b):
    return w[0:1] * _shift_rows(x, 1) + w[1:2] * x + w[2:3] * _shift_rows(x, -1) + b


def _gate_up_specs(rows_, wblk, nb):
    return [pl.BlockSpec((rows_, wblk), lambda j: (0, j)), pl.BlockSpec((rows_, wblk), lambda j: (0, j + nb))]


def conv_fwd(hu, cw, cb, *, rows, wblk, name):
    L, N2 = hu.shape
    nb = N2 // 2 // wblk
    nchunk = L // rows

    def body(hg_ref, hu_ref, wg_ref, wu_ref, bg_ref, bu_ref, a_ref, s1_ref, s2_ref):
        def chunk(ci, carry):
            r0 = pl.multiple_of(ci * rows, rows)
            gate = _conv3(_conv_ext(hg_ref, r0, rows, L), wg_ref[...], bg_ref[...])[PAD:PAD + rows]
            up = _conv3(_conv_ext(hu_ref, r0, rows, L), wu_ref[...], bu_ref[...])[PAD:PAD + rows]
            sg = jax.nn.sigmoid(gate)
            silu = gate * sg
            at = pl.ds(r0, rows)
            a_ref[at, :] = (silu * up).astype(BF16)
            s1_ref[at, :] = silu.astype(BF16)
            s2_ref[at, :] = (up * (sg + silu * (1.0 - sg))).astype(BF16)
            return carry

        lax.fori_loop(0, nchunk, chunk, 0)

    out = pl.BlockSpec((L, wblk), lambda j: (0, j))
    return pl.pallas_call(
        body, name=name, grid=(nb,),
        in_specs=_gate_up_specs(L, wblk, nb) + _gate_up_specs(3, wblk, nb) + _gate_up_specs(1, wblk, nb),
        out_specs=[out] * 3, out_shape=[jax.ShapeDtypeStruct((L, N2 // 2), BF16)] * 3,
        compiler_params=_cp(("parallel",)),
    )(hu, hu, cw, cw, cb, cb)


def conv_bwd(da, s1, s2, hu, cw, *, rows, wblk, name):
    L, N2 = hu.shape
    F = N2 // 2
    nb = F // wblk
    nchunk = L // rows
    mid = slice(PAD, PAD + rows)

    def body(da_ref, s1_ref, s2_ref, hg_ref, hu_ref, wg_ref, wu_ref, dg_ref, du_ref, dwg_ref, dwu_ref, dbg_ref, dbu_ref):
        for ref in (dwg_ref, dwu_ref, dbg_ref, dbu_ref):
            ref[...] = jnp.zeros_like(ref)

        def half_bwd(x_ref, dh, w_ref, dx_ref, dw_ref, db_ref, r0):
            w = w_ref[...]
            nxt, prv = _shift_rows(dh, -1)[mid], _shift_rows(dh, 1)[mid]
            dhm, xm = dh[mid], x_ref[pl.ds(r0, rows), :].astype(F32)
            dx_ref[pl.ds(r0, rows), :] = (w[0:1] * nxt + w[1:2] * dhm + w[2:3] * prv).astype(BF16)
            db_ref[...] += _colsum(dhm)
            dw_ref[0:1, :] += _colsum(nxt * xm)
            dw_ref[1:2, :] += _colsum(dhm * xm)
            dw_ref[2:3, :] += _colsum(prv * xm)

        def chunk(ci, carry):
            r0 = pl.multiple_of(ci * rows, rows)
            d = _conv_ext(da_ref, r0, rows, L)
            half_bwd(hu_ref, d * _conv_ext(s1_ref, r0, rows, L), wu_ref, du_ref, dwu_ref, dbu_ref, r0)
            half_bwd(hg_ref, d * _conv_ext(s2_ref, r0, rows, L), wg_ref, dg_ref, dwg_ref, dbg_ref, r0)
            return carry

        lax.fori_loop(0, nchunk, chunk, 0)

    blk = lambda r: pl.BlockSpec((r, wblk), lambda j: (0, j))
    return pl.pallas_call(
        body, name=name, grid=(nb,),
        in_specs=[blk(L)] * 3 + _gate_up_specs(L, wblk, nb) + _gate_up_specs(3, wblk, nb),
        out_specs=[blk(L), blk(L), blk(3), blk(3), blk(1), blk(1)],
        out_shape=[jax.ShapeDtypeStruct((L, F), BF16)] * 2 + [jax.ShapeDtypeStruct((3, F), F32)] * 2
        + [jax.ShapeDtypeStruct((1, F), F32)] * 2,
        compiler_params=_cp(("parallel",)),
    )(da, s1, s2, hu, hu, cw, cw)


def _window_sums(pad_ref, w, lead):
    a = pad_ref[...]
    k = 1
    while k < w:
        a = a + _shift_rows(a, -k)
        k *= 2
    return _shift_rows(a, lead) if lead else a


def _pool_counts(L, h):
    t = lax.broadcasted_iota(jnp.int32, (L, 1), 0)
    return (jnp.minimum(t + h, L) - jnp.maximum(t - h, 0)).astype(F32)


def _pooled(u_ref, pad_ref, L, w):
    h = w // 2
    pad_ref[pl.ds(PAD, L), :] = u_ref[...]
    win = _window_sums(pad_ref, w, h)[PAD:PAD + L]
    return win / _pool_counts(L, h) - u_ref[...]


def _zero_pad_edges(pad_ref, L):
    z = jnp.zeros((PAD, LANES), F32)
    pad_ref[pl.ds(0, PAD), :] = z
    pad_ref[pl.ds(PAD + L, PAD), :] = z


def pool_fwd(u, w_pool, pool_scale, *, name):
    L = u.shape[0]

    def body(u_ref, w_ref, ps_ref, p_ref, pad_ref):
        _zero_pad_edges(pad_ref, L)
        for gi, win in enumerate(POOL_WINDOWS):
            @pl.when(pl.program_id(0) == gi)
            def _():
                pooled = _pooled(u_ref, pad_ref, L, win)
                p_ref[...] = (_dot(pooled.astype(BF16), w_ref[...].astype(BF16)) * ps_ref[...]).astype(BF16)

    return pl.pallas_call(
        body, name=name, grid=(len(POOL_WINDOWS),),
        in_specs=[pl.BlockSpec((L, LANES), lambda gi: (0, gi)), pl.BlockSpec((None, LANES, LANES), lambda gi: (gi, 0, 0)),
                  pl.BlockSpec((1, LANES), lambda gi: (0, gi))],
        out_specs=pl.BlockSpec((L, LANES), lambda gi: (0, gi)),
        out_shape=jax.ShapeDtypeStruct((L, 4 * LANES), BF16),
        scratch_shapes=[pltpu.VMEM((L + 2 * PAD, LANES), F32)],
        compiler_params=_cp(("parallel",)),
    )(u, w_pool, pool_scale)


def pool_bwd(u, dpa, w_pool, pool_scale, *, name):
    L = u.shape[0]

    def body(u_ref, dp_ref, w_ref, ps_ref, du_ref, dw_ref, dps_ref, pad_ref):
        _zero_pad_edges(pad_ref, L)
        for gi, win in enumerate(POOL_WINDOWS):
            @pl.when(pl.program_id(0) == gi)
            def _():
                h = win // 2
                wb = w_ref[...].astype(BF16)
                pooled = _pooled(u_ref, pad_ref, L, win).astype(BF16)
                dp = dp_ref[...].astype(F32)
                dps_ref[...] = _colsum(dp * _dot(pooled, wb))
                dy = (dp * ps_ref[...]).astype(BF16)
                dw_ref[...] = _dot_tn(pooled, dy)
                dpooled = _dot_nt(dy, wb)
                pad_ref[pl.ds(PAD, L), :] = dpooled / _pool_counts(L, h)
                du_ref[...] = (_window_sums(pad_ref, win, h - 1)[PAD:PAD + L] - dpooled).astype(BF16)

    return pl.pallas_call(
        body, name=name, grid=(len(POOL_WINDOWS),),
        in_specs=[pl.BlockSpec((L, LANES), lambda gi: (0, gi)), pl.BlockSpec((L, LANES), lambda gi: (0, gi)),
                  pl.BlockSpec((None, LANES, LANES), lambda gi: (gi, 0, 0)), pl.BlockSpec((1, LANES), lambda gi: (0, gi))],
        out_specs=[pl.BlockSpec((L, LANES), lambda gi: (0, gi)), pl.BlockSpec((None, LANES, LANES), lambda gi: (gi, 0, 0)),
                   pl.BlockSpec((1, LANES), lambda gi: (0, gi))],
        out_shape=[jax.ShapeDtypeStruct((L, 4 * LANES), BF16), jax.ShapeDtypeStruct((4, LANES, LANES), F32),
                   jax.ShapeDtypeStruct((1, 4 * LANES), F32)],
        scratch_shapes=[pltpu.VMEM((L + 2 * PAD, LANES), F32)],
        compiler_params=_cp(("parallel",)),
    )(u, dpa, w_pool, pool_scale)


def _attn_probs(qk, band_k, ctx_k, sink_ref, kh, mask4):
    s_loc = jnp.where(mask4, _dot_nt(qk, band_k), NEG_INF)
    s_ctx = _dot_nt(qk, ctx_k)
    sk = jnp.concatenate([jnp.full((BLK, 1), sink_ref[kh * GQA + hh], F32) for hh in range(GQA)], axis=0)
    m = jnp.maximum(jnp.maximum(jnp.max(s_loc, axis=-1, keepdims=True), jnp.max(s_ctx, axis=-1, keepdims=True)), sk)
    e_loc, e_ctx, e_s = jnp.exp(s_loc - m), jnp.exp(s_ctx - m), jnp.exp(sk - m)
    inv = 1.0 / (jnp.sum(e_loc, axis=-1, keepdims=True) + jnp.sum(e_ctx, axis=-1, keepdims=True) + e_s)
    return e_loc * inv, e_ctx * inv, e_s * inv


def _attn_block(n, L):
    start = pl.multiple_of(jnp.clip((n - 1) * BLK, 0, L - 3 * BLK), BLK)
    qpos = n * BLK + lax.broadcasted_iota(jnp.int32, (BLK, 3 * BLK), 0)
    kpos = start + lax.broadcasted_iota(jnp.int32, (BLK, 3 * BLK), 1)
    mask = jnp.abs(kpos - qpos) <= WINDOW
    return start, jnp.concatenate([mask] * GQA, axis=0)


def _stack_slabs(ref):
    return jnp.concatenate([ref[:, s * LANES:(s + 1) * LANES] for s in range(GQA)], axis=0)


def _kv_head_lanes(kh):
    return (lax.broadcasted_iota(jnp.int32, (1, LANES), 1) // HEAD_DIM) == kh


def permute_heads(w, inverse=False):
    lo, hi = 4 * LANES, 8 * LANES
    mid = w[lo:hi].reshape(*((GQA, N_KV_HEADS) if inverse else (N_KV_HEADS, GQA)), HEAD_DIM, w.shape[1])
    return jnp.concatenate([w[:lo], mid.swapaxes(0, 1).reshape(hi - lo, w.shape[1]), w[hi:]], axis=0)


def attn_fwd(q, kv, kvc, sink, *, name):
    L = q.shape[0]
    C = kvc.shape[0]
    scale = HEAD_DIM ** -0.5

    def body(q_ref, kv_ref, kvc_ref, sink_ref, o_ref):
        start, mask4 = _attn_block(pl.program_id(0), L)
        band = kv_ref[pl.ds(start, 3 * BLK), :]
        kvc_ = kvc_ref[...]
        qs = _stack_slabs(q_ref) * scale
        o = jnp.zeros((GQA * BLK, LANES), F32)
        for kh in range(N_KV_HEADS):
            grp = _kv_head_lanes(kh)
            qk = jnp.where(grp, qs, jnp.zeros_like(qs))
            p_loc, p_ctx, _ = _attn_probs(qk, band[:, :LANES], kvc_[:, :LANES], sink_ref, kh, mask4)
            o = o + jnp.where(grp, _dot(p_loc.astype(BF16), band[:, LANES:]) + _dot(p_ctx.astype(BF16), kvc_[:, LANES:]), 0.0)
        for s in range(GQA):
            o_ref[:, s * LANES:(s + 1) * LANES] = o[s * BLK:(s + 1) * BLK].astype(BF16)

    return pl.pallas_call(
        body, name=name, grid=(L // BLK,),
        in_specs=[pl.BlockSpec((BLK, 4 * LANES), lambda n: (n, 0)), _full((L, 2 * LANES)), _full((C, 2 * LANES)),
                  pl.BlockSpec(memory_space=pltpu.SMEM)],
        out_specs=pl.BlockSpec((BLK, 4 * LANES), lambda n: (n, 0)),
        out_shape=jax.ShapeDtypeStruct((L, 4 * LANES), BF16),
        compiler_params=_cp(("parallel",)),
    )(q, kv, kvc, sink)


def attn_bwd(q, kv, kvc, sink, dpa, cos, sa, sb, *, name):
    L = q.shape[0]
    C = kvc.shape[0]
    nb = L // BLK
    scale = HEAD_DIM ** -0.5

    def body(q_ref, kv_ref, kvc_ref, sink_ref, do_ref, c_ref, sa_ref, sb_ref, cq_ref, saq_ref, sbq_ref,
             dq_ref, dkv_ref, dkvc_ref, dsink_ref, dkv_acc, dkvc_acc):
        n = pl.program_id(0)

        @pl.when(n == 0)
        def _():
            dkv_acc[...] = jnp.zeros_like(dkv_acc)
            dkvc_acc[...] = jnp.zeros_like(dkvc_acc)
            dsink_ref[...] = jnp.zeros_like(dsink_ref)

        start, mask4 = _attn_block(n, L)
        band = kv_ref[pl.ds(start, 3 * BLK), :]
        kvc_ = kvc_ref[...]
        band_k, band_v, ctx_k, ctx_v = band[:, :LANES], band[:, LANES:], kvc_[:, :LANES], kvc_[:, LANES:]
        qs = _stack_slabs(q_ref) * scale
        dos = _stack_slabs(do_ref)
        lane = lax.broadcasted_iota(jnp.int32, (1, LANES), 1)
        dsink = jnp.zeros((1, LANES), F32)
        dq = jnp.zeros((GQA * BLK, LANES), F32)
        dk = jnp.zeros((3 * BLK, LANES), F32)
        dv = jnp.zeros((3 * BLK, LANES), F32)
        dkc = jnp.zeros((C, LANES), F32)
        dvc = jnp.zeros((C, LANES), F32)
        for kh in range(N_KV_HEADS):
            grp = _kv_head_lanes(kh)
            qk = jnp.where(grp, qs, jnp.zeros_like(qs))
            dok = jnp.where(grp, dos, jnp.zeros_like(dos))
            p_loc, p_ctx, p_s = _attn_probs(qk, band_k, ctx_k, sink_ref, kh, mask4)
            dp_loc = _dot_nt(dok, band_v)
            dp_ctx = _dot_nt(dok, ctx_v)
            delta = jnp.sum(p_loc * dp_loc, axis=-1, keepdims=True) + jnp.sum(p_ctx * dp_ctx, axis=-1, keepdims=True)
            ds_loc = (p_loc * (dp_loc - delta)).astype(BF16)
            ds_ctx = (p_ctx * (dp_ctx - delta)).astype(BF16)
            dsk = p_s * delta
            for hh in range(GQA):
                dsink = dsink - jnp.where(lane == kh * GQA + hh, jnp.sum(dsk[hh * BLK:(hh + 1) * BLK], axis=0, keepdims=True), 0.0)
            dq = dq + jnp.where(grp, _dot(ds_loc, band_k) + _dot(ds_ctx, ctx_k), 0.0)
            dk = dk + _dot_tn(ds_loc, qk)
            dv = dv + _dot_tn(p_loc.astype(BF16), dok)
            dkc = dkc + _dot_tn(ds_ctx, qk)
            dvc = dvc + _dot_tn(p_ctx.astype(BF16), dok)
        dsink_ref[...] += dsink
        dkv_acc[pl.ds(start, 3 * BLK), :LANES] += dk
        dkv_acc[pl.ds(start, 3 * BLK), LANES:] += dv
        dkvc_acc[:, :LANES] += dkc
        dkvc_acc[:, LANES:] += dvc
        c, a, b = cq_ref[...], -saq_ref[...], -sbq_ref[...]
        for s in range(GQA):
            dq_ref[:, s * LANES:(s + 1) * LANES] = _rope(dq[s * BLK:(s + 1) * BLK] * scale, c, a, b).astype(BF16)

        @pl.when(n == nb - 1)
        def _():
            dkv_ref[:, :LANES] = _rope(dkv_acc[:, :LANES], c_ref[...], -sa_ref[...], -sb_ref[...]).astype(BF16)
            dkv_ref[:, LANES:] = dkv_acc[:, LANES:].astype(BF16)
            dkvc_ref[...] = dkvc_acc[...].astype(BF16)

    blk = lambda w: pl.BlockSpec((BLK, w), lambda n: (n, 0))
    return pl.pallas_call(
        body, name=name, grid=(nb,),
        in_specs=[blk(4 * LANES), _full((L, 2 * LANES)), _full((C, 2 * LANES)), pl.BlockSpec(memory_space=pltpu.SMEM),
                  pl.BlockSpec((BLK, 4 * LANES), lambda n: (n, 1)),
                  _full((L, LANES)), _full((L, LANES)), _full((L, LANES)), blk(LANES), blk(LANES), blk(LANES)],
        out_specs=[blk(4 * LANES), _full((L, 2 * LANES)), _full((C, 2 * LANES)), _full((1, LANES))],
        out_shape=[jax.ShapeDtypeStruct((L, 4 * LANES), BF16), jax.ShapeDtypeStruct((L, 2 * LANES), BF16),
                   jax.ShapeDtypeStruct((C, 2 * LANES), BF16), jax.ShapeDtypeStruct((1, LANES), F32)],
        scratch_shapes=[pltpu.VMEM((L, 2 * LANES), F32), pltpu.VMEM((C, 2 * LANES), F32)],
        compiler_params=_cp(("arbitrary",)),
    )(q, kv, kvc, sink, dpa, cos, sa, sb, cos, sa, sb)


def _gelu_parts(x):
    th = jnp.tanh(SQRT_2_OVER_PI * (x + GELU_C * x * x * x))
    return 0.5 * x * (1.0 + th), th


def _gelu_grad(x, th):
    return 0.5 * (1.0 + th) + 0.5 * x * (1.0 - th * th) * SQRT_2_OVER_PI * (1.0 + 3.0 * GELU_C * x * x)


def _layernorm(v):
    mu = jnp.mean(v, axis=-1, keepdims=True)
    vc = v - mu
    rstd = lax.rsqrt(jnp.mean(vc * vc, axis=-1, keepdims=True) + EPS)
    return vc * rstd, rstd


def sgu_fwd(z1, ln_g, ln_b, ws, bst, *, name):
    L, W2 = z1.shape
    W = W2 // 2
    ng = W // LANES

    def body(z_ref, g_ref, b_ref, ws_ref, bs_ref, o_ref):
        z, _ = _gelu_parts(z_ref[...].astype(F32))
        xhat, _ = _layernorm(z[:, W:])
        vln = (xhat * g_ref[...] + b_ref[...]).astype(BF16)
        for gi in range(ng):
            cs = slice(gi * LANES, (gi + 1) * LANES)
            s = _dot(ws_ref[gi].astype(BF16), vln[:, cs]) + bs_ref[:, gi:gi + 1]
            o_ref[:, cs] = (z[:, cs] * s).astype(BF16)

    vec = _full((1, W))
    return pl.pallas_call(
        body, name=name, grid=(L // BLK,),
        in_specs=[pl.BlockSpec((BLK, W2), lambda n: (n, 0)), vec, vec, _full((ng, LANES, LANES)), _full((BLK, ng))],
        out_specs=pl.BlockSpec((BLK, W), lambda n: (n, 0)),
        out_shape=jax.ShapeDtypeStruct((L, W), BF16),
        compiler_params=_cp(("parallel",)),
    )(z1, ln_g, ln_b, ws, bst)


def sgu_bwd(z1, dus, ln_g, ln_b, ws, bst, *, name):
    L, W2 = z1.shape
    W = W2 // 2
    ng = W // LANES

    def body(z_ref, d_ref, g_ref, b_ref, ws_ref, bs_ref, dz_ref, dws_ref, dbs_ref, dg_ref, db_ref, dv_scr):
        @pl.when(pl.program_id(0) == 0)
        def _():
            dws_ref[...] = jnp.zeros_like(dws_ref)
            dbs_ref[...] = jnp.zeros_like(dbs_ref)
            dg_ref[...] = jnp.zeros_like(dg_ref)
            db_ref[...] = jnp.zeros_like(db_ref)

        zp = z_ref[...].astype(F32)
        z, th = _gelu_parts(zp)
        xhat, rstd = _layernorm(z[:, W:])
        vln = (xhat * g_ref[...] + b_ref[...]).astype(BF16)
        d = d_ref[...].astype(F32)
        lane = lax.broadcasted_iota(jnp.int32, (1, LANES), 1)
        dbs = jnp.zeros((BLK, LANES), F32)
        dgel = _gelu_grad(zp, th)
        for gi in range(ng):
            cs = slice(gi * LANES, (gi + 1) * LANES)
            wb = ws_ref[gi].astype(BF16)
            s = _dot(wb, vln[:, cs]) + bs_ref[:, gi:gi + 1]
            dz_ref[:, cs] = (d[:, cs] * s * dgel[:, cs]).astype(BF16)
            ds = d[:, cs] * z[:, cs]
            dbs = dbs + jnp.where(lane == gi, jnp.sum(ds, axis=-1, keepdims=True), 0.0)
            dsb = ds.astype(BF16)
            dws_ref[gi] += _dot_nt(dsb, vln[:, cs])
            dv_scr[:, cs] = _dot_tn(wb, dsb)
        dbs_ref[...] += dbs
        dvln = dv_scr[...]
        dg_ref[...] += _colsum(dvln * xhat)
        db_ref[...] += _colsum(dvln)
        dxh = dvln * g_ref[...]
        dv = rstd * (dxh - jnp.mean(dxh, axis=-1, keepdims=True) - xhat * jnp.mean(dxh * xhat, axis=-1, keepdims=True))
        dz_ref[:, W:] = (dv * dgel[:, W:]).astype(BF16)

    vec = _full((1, W))
    return pl.pallas_call(
        body, name=name, grid=(L // BLK,),
        in_specs=[pl.BlockSpec((BLK, W2), lambda n: (n, 0)), pl.BlockSpec((BLK, W), lambda n: (n, 0)), vec, vec,
                  _full((ng, LANES, LANES)), _full((BLK, ng))],
        out_specs=[pl.BlockSpec((BLK, W2), lambda n: (n, 0)), _full((ng, LANES, LANES)), _full((BLK, LANES)), vec, vec],
        out_shape=[jax.ShapeDtypeStruct((L, W2), BF16), jax.ShapeDtypeStruct((ng, LANES, LANES), F32),
                   jax.ShapeDtypeStruct((BLK, LANES), F32), jax.ShapeDtypeStruct((1, W), F32), jax.ShapeDtypeStruct((1, W), F32)],
        scratch_shapes=[pltpu.VMEM((BLK, W), F32)],
        compiler_params=_cp(("arbitrary",)),
    )(z1, dus, ln_g, ln_b, ws, bst)


def loss_grad(xo, target, *, tm, name):
    T, D = xo.shape

    def body(x_ref, t_ref, l_ref, d_ref):
        @pl.when(pl.program_id(0) == 0)
        def _():
            l_ref[...] = jnp.zeros_like(l_ref)

        e = x_ref[...] - t_ref[...]
        l_ref[...] += 0.5 * jnp.sum(jnp.mean(e * e, axis=-1, keepdims=True), axis=0, keepdims=True)
        d_ref[...] = e * (1.0 / D)

    row = pl.BlockSpec((tm, D), lambda i: (i, 0))
    return pl.pallas_call(
        body, name=name, grid=(T // tm,), in_specs=[row, row], out_specs=[_full((1, 1)), row],
        out_shape=[jax.ShapeDtypeStruct((1, 1), F32), jax.ShapeDtypeStruct((T, D), F32)],
        compiler_params=_cp(("arbitrary",)),
    )(xo, target)


def _adamw_math(w, m, v, g):
    m_ = ADAM_B1 * m + (1.0 - ADAM_B1) * g
    v_ = ADAM_B2 * v + (1.0 - ADAM_B2) * (g * g)
    return -ADAM_LR * ((m_ / BC1) / (jnp.sqrt(v_ / BC2) + ADAM_EPS) + ADAM_WD * w), m_, v_


def adamw(w, m, v, gparts, *, tr, name):
    NL, R, Wd = w.shape
    nr = R // tr

    def body(w_ref, m_ref, v_ref, *rest):
        gp_refs, (g_ref, d_ref, nm_ref, nv_ref) = rest[:NL], rest[NL:]
        for l in range(NL):
            @pl.when(pl.program_id(0) == l)
            def _():
                g = gp_refs[l][0].astype(F32)
                for s in range(1, gp_refs[l].shape[0]):
                    g = g + gp_refs[l][s].astype(F32)
                g_ref[...] = g
                d_ref[...], nm_ref[...], nv_ref[...] = _adamw_math(w_ref[...], m_ref[...], v_ref[...], g)

    row = pl.BlockSpec((None, tr, Wd), lambda l, i: (l, i, 0))
    gspecs = [pl.BlockSpec((gparts[l].shape[0], tr, Wd), (lambda l_, i, l=l: (0, jnp.clip(i + (l_ - l) * nr, 0, nr - 1), 0)))
              for l in range(NL)]
    return pl.pallas_call(
        body, name=name, grid=(NL, nr),
        in_specs=[row, row, row] + gspecs, out_specs=[row] * 4, out_shape=[jax.ShapeDtypeStruct((NL, R, Wd), F32)] * 4,
        compiler_params=_cp(("arbitrary", "arbitrary")),
    )(w, m, v, *gparts)


def small_update(gpacks, me, params, loss_row, *, name):
    n = len(params)

    def body(me_ref, gp_ref, *refs):
        ins, outs, gs_ref = refs[:3 * n], refs[3 * n:-1], refs[-1]
        gs_ref[...] = gp_ref[0].astype(F32)
        for dv in range(1, N_DEV):
            gs_ref[...] += gp_ref[dv].astype(F32)
        for p, (w, _, _, off, per_dev) in enumerate(params):
            w_ref, m_ref, v_ref = ins[3 * p:3 * p + 3]
            g_ref, d_ref, nm_ref, nv_ref = outs[4 * p:4 * p + 4]
            rows, cols = w.shape
            if cols == LANES and rows % 8 == 0 and not per_dev:
                g = gs_ref[off:off + rows, :]
                g_ref[...] = g
                d_ref[...], nm_ref[...], nv_ref[...] = _adamw_math(w_ref[...], m_ref[...], v_ref[...], g)
                continue
            chunks = -(-cols // LANES)
            base = off + me_ref[0] * per_dev if per_dev else off
            for i in range(rows):
                for j in range(chunks):
                    wd = min(LANES, cols - j * LANES)
                    at = (slice(i, i + 1), slice(j * LANES, j * LANES + wd))
                    g = gs_ref[pl.ds(base + i * chunks + j, 1), 0:wd]
                    g_ref[at] = g
                    d_ref[at], nm_ref[at], nv_ref[at] = _adamw_math(w_ref[at], m_ref[at], v_ref[at], g)
        outs[-1][...] = jnp.sum(gs_ref[loss_row:loss_row + 1, :], axis=1, keepdims=True)

    vm = pl.BlockSpec(memory_space=pltpu.VMEM)
    flat = [a for w, m, v, _, _ in params for a in (w, m, v)]
    out_shape = [jax.ShapeDtypeStruct(w.shape, F32) for w, _, _, _, _ in params for _ in range(4)] + [jax.ShapeDtypeStruct((1, 1), F32)]
    return pl.pallas_call(
        body, name=name, in_specs=[pl.BlockSpec(memory_space=pltpu.SMEM)] + [vm] * (1 + len(flat)),
        out_specs=[vm] * len(out_shape), out_shape=out_shape,
        scratch_shapes=[pltpu.VMEM(gpacks.shape[1:], F32)],
        compiler_params=pltpu.CompilerParams(vmem_limit_bytes=VMEM_LIMIT),
    )(me, gpacks, *flat)


def ada_fwd_mm(cs, w_ada, b_loc, *, name):
    R, D = cs.shape
    nl, _, n = w_ada.shape

    def body(c_ref, w_ref, b_ref, s_ref, m_ref):
        c = c_ref[...]
        s = c * jax.nn.sigmoid(c)
        s_ref[...] = s
        for i in range(nl):
            m_ref[i] = _dot(s.astype(BF16), w_ref[i].astype(BF16)) + b_ref[i:i + 1, :]

    return pl.pallas_call(
        body, name=name, in_specs=[_full((R, D)), _full((nl, D, n)), _full((nl, n))],
        out_specs=[_full((R, D)), _full((nl, R, n))], grid=(1,),
        out_shape=[jax.ShapeDtypeStruct((R, D), F32), jax.ShapeDtypeStruct((nl, R, n), F32)],
        compiler_params=_cp(("arbitrary",)),
    )(cs, w_ada, b_loc)


def ada_bwd_mm(s, c_ctx, dall, w_ada, *, name):
    R, D = s.shape
    nl, _, n = w_ada.shape

    def body(s_ref, cc_ref, d_ref, w_ref, gw_ref, dcc_ref):
        sb = s_ref[...].astype(BF16)
        row = lax.broadcasted_iota(jnp.int32, (R, 1), 0)
        dctx = d_ref[0, 1:2, :]
        for dv in range(1, N_DEV):
            dctx = dctx + d_ref[dv, 1:2, :]
        for i in range(nl):
            dm = jnp.zeros((R, n), F32)
            for dv in range(N_DEV):
                dm = dm + jnp.where(row == dv, d_ref[dv, 2 * i:2 * i + 1, :], 0.0)
            if i == 0:
                dm = dm + jnp.where(row == N_DEV, dctx, 0.0)
            gw_ref[i] = _dot_tn(sb, dm.astype(BF16))
        cc = cc_ref[...]
        sg = jax.nn.sigmoid(cc)
        ds = _dot_nt(jnp.broadcast_to(dctx, (8, n)).astype(BF16), w_ref[0].astype(BF16))
        dcc_ref[...] = ds * (sg * (1.0 + cc * (1.0 - sg)))

    return pl.pallas_call(
        body, name=name, grid=(1,),
        in_specs=[_full((R, D)), _full((1, D)), _full((N_DEV, 3, n)), _full((nl, D, n))],
        out_specs=[_full((nl, D, n)), _full((8, D))],
        out_shape=[jax.ShapeDtypeStruct((nl, D, n), F32), jax.ShapeDtypeStruct((8, D), F32)],
        compiler_params=_cp(("arbitrary",)),
    )(s, c_ctx, dall, w_ada)


def _place():
    x, y, c = lax.axis_index("x"), lax.axis_index("y"), lax.axis_index("c")
    return x, y, c


def _lin(p):
    return 4 * p[0] + 2 * p[1] + p[2]


def all_gather_small(xb, *, name):
    R, W = xb.shape

    def body(x_ref, out_ref, send_sems, recv_sems, local_sem):
        x, y, c = _place()
        me, sibling = (x, y, c), (x, y, 1 - c)
        chips = [(1 - x, y), (x, 1 - y), (1 - x, 1 - y)]

        def copy(k, block, to, src=None):
            dst = out_ref.at[_lin(block)]
            return pltpu.make_async_remote_copy(
                src_ref=dst if src is None else src, dst_ref=dst, send_sem=send_sems.at[k], recv_sem=recv_sems.at[k],
                device_id=to, device_id_type=MESH)

        mine = pltpu.make_async_copy(x_ref, out_ref.at[_lin(me)], local_sem)
        mine.start()
        first = [copy(0, me, sibling, src=x_ref)]
        first += [copy(1 + j, me, (*chip, c), src=x_ref) for j, chip in enumerate(chips)]
        for cp in first:
            cp.start()
        passed = [copy(4 + j, (*chip, c), sibling) for j, chip in enumerate(chips)]
        for j, chip in enumerate(chips):
            copy(1 + j, (*chip, c), me).wait_recv()
            passed[j].start()
        copy(0, sibling, me).wait_recv()
        for j, chip in enumerate(chips):
            copy(4 + j, (*chip, 1 - c), me).wait_recv()
        for cp in first + passed:
            cp.wait_send()
        mine.wait()

    vm = pl.BlockSpec(memory_space=pltpu.VMEM)
    return pl.pallas_call(
        body, name=name, in_specs=[vm], out_specs=vm, out_shape=jax.ShapeDtypeStruct((N_DEV, R, W), xb.dtype),
        scratch_shapes=[pltpu.SemaphoreType.DMA((7,)), pltpu.SemaphoreType.DMA((7,)), pltpu.SemaphoreType.DMA],
        compiler_params=pltpu.CompilerParams(vmem_limit_bytes=VMEM_LIMIT),
    )(xb)


HBM_SPEC = pl.BlockSpec(memory_space=pltpu.HBM)
SEM_SPEC = pl.BlockSpec(memory_space=pltpu.SEMAPHORE)
ORDERED_EFFECT = pltpu.SideEffectType.DATAFLOW_SIDE_EFFECTING


def _exchange_copies(srcs, lands, sems, scatter):
    x, y, c = _place()
    me = _lin((x, y, c))
    for j in range(len(srcs)):
        r = lands[j].shape[0] // N_DEV
        block = lambda d, j=j, r=r: pl.ds(pl.multiple_of(d * r, 16), r)
        for k in range(1, N_DEV):
            peer = (x ^ (k >> 2), y ^ ((k >> 1) & 1), c ^ (k & 1))
            src = srcs[j].at[block(_lin(peer)), :] if scatter else srcs[j]
            mk = lambda dst, j=j, k=k, peer=peer, src=src: pltpu.make_async_remote_copy(
                src_ref=src, dst_ref=dst, send_sem=sems[2 * j].at[k - 1], recv_sem=sems[2 * j + 1].at[k - 1],
                device_id=peer, device_id_type=MESH)
            yield mk(lands[j].at[block(me), :]), mk(lands[j].at[block(_lin(peer)), :])


def exchange_start(srcs, lands, *, scatter, name):
    nw = len(srcs)

    def body(*refs):
        for start, _ in _exchange_copies(refs[:nw], refs[nw:2 * nw], refs[2 * nw:4 * nw], scatter):
            start.start()
        refs[-1][...] = jnp.zeros_like(refs[-1])

    thru = [pltpu.HBM(a.shape, a.dtype) for a in (*srcs, *lands)]
    res = pl.pallas_call(
        body, name=name, in_specs=[HBM_SPEC] * (2 * nw),
        out_specs=[SEM_SPEC] * (2 * nw) + [HBM_SPEC] * (2 * nw) + [pl.BlockSpec(memory_space=pltpu.VMEM)],
        out_shape=[pltpu.SemaphoreType.DMA((N_DEV - 1,))] * (2 * nw) + thru + [jax.ShapeDtypeStruct((8, LANES), F32)],
        input_output_aliases={i: 2 * nw + i for i in range(2 * nw)},
        compiler_params=pltpu.CompilerParams(has_side_effects=ORDERED_EFFECT),
    )(*[pltpu.with_memory_space_constraint(a, pltpu.HBM) for a in (*srcs, *lands)])
    return res[:2 * nw], res[2 * nw:3 * nw], res[3 * nw:4 * nw], res[-1]


def exchange_wait(srcs, lands, sems, after, *, scatter, name):
    nw = len(srcs)

    def body(*refs):
        for _, arrive in _exchange_copies(refs[:nw], refs[nw:2 * nw], refs[2 * nw:4 * nw], scatter):
            arrive.wait_send()
            arrive.wait_recv()

    res = pl.pallas_call(
        body, name=name, in_specs=[HBM_SPEC] * (2 * nw) + [SEM_SPEC] * (2 * nw) + [pl.BlockSpec(memory_space=pl.ANY)],
        out_specs=[HBM_SPEC] * (2 * nw), out_shape=[pltpu.HBM(a.shape, a.dtype) for a in (*srcs, *lands)],
        input_output_aliases={i: i for i in range(2 * nw)},
        compiler_params=pltpu.CompilerParams(has_side_effects=ORDERED_EFFECT),
    )(*srcs, *lands, *sems, after)
    return res[nw:]


def place_own(srcs, rows, me, *, scatter, name):
    nw = len(srcs)
    lands = [lax.empty((N_DEV * r, s_.shape[1]), s_.dtype) for r, s_ in zip(rows, srcs)]

    def body(me_ref, *refs):
        for j in range(nw):
            refs[2 * nw + j][...] = refs[j][...]

    mine = lambda i, me_ref: (me_ref[0], 0)
    src_at = mine if scatter else (lambda i, me_ref: (0, 0))
    blocks = [(r, s_.shape[1]) for r, s_ in zip(rows, srcs)]
    return pl.pallas_call(
        body, name=name,
        grid_spec=pltpu.PrefetchScalarGridSpec(
            num_scalar_prefetch=1, grid=(1,),
            in_specs=[pl.BlockSpec(b_, src_at) for b_ in blocks] + [pl.BlockSpec(memory_space=pl.ANY)] * nw,
            out_specs=[pl.BlockSpec(b_, mine) for b_ in blocks]),
        out_shape=[jax.ShapeDtypeStruct(l_.shape, l_.dtype) for l_ in lands],
        input_output_aliases={1 + nw + j: j for j in range(nw)},
        compiler_params=_cp(("arbitrary",)),
    )(jnp.reshape(me, (1,)).astype(jnp.int32), *srcs, *lands)


def _rope_tables(L):
    t = jnp.arange(L)
    inv = ROPE_BASE ** (-jnp.arange(ROPE_FREQS, dtype=F32) / ROPE_FREQS)
    ar = (t // GRID_W).astype(F32)[:, None] * inv
    ac = (t % GRID_W).astype(F32)[:, None] * inv
    z = jnp.zeros_like(ar)
    cos = jnp.concatenate([jnp.cos(ar), jnp.cos(ar), jnp.cos(ac), jnp.cos(ac)], axis=1)
    sa = jnp.concatenate([-jnp.sin(ar), z, -jnp.sin(ac), z], axis=1)
    sb = jnp.concatenate([z, jnp.sin(ar), z, jnp.sin(ac)], axis=1)
    return tuple(jnp.tile(a, (1, LANES // HEAD_DIM)) for a in (cos, sa, sb))


def _nat2d(a):
    return a.reshape(1, -1) if a.ndim == 1 else a.reshape(-1, a.shape[-1])


def _pack_rows(a):
    rows, cols = a.shape
    chunks = -(-cols // LANES)
    f = jnp.pad(a, ((0, 0), (0, chunks * LANES - cols))).reshape(rows * chunks, LANES)
    return jnp.pad(f, ((0, -f.shape[0] % 8), (0, 0)))


def _rows128(a):
    f = a.reshape(-1)
    n = -(-f.shape[0] // (8 * LANES)) * 8 * LANES
    return jnp.pad(f, (0, n - f.shape[0])).reshape(-1, LANES)


def kernel(x, c, ctx, c_ctx, w_ada, b_ada, g_mix_pre, g_mix_post, g_ffn_pre, g_ffn_post, w_in_even, w_pool, pool_scale, attn_sink, w_out_even, w_in_odd, sgu_ln_g, sgu_ln_b, sgu_w, sgu_b, w_out_odd, w_ffn_up, ffn_conv_w, ffn_conv_b, w_ffn_down, loss_target, m_c_ctx, m_w_ada, m_b_ada, m_g_mix_pre, m_g_mix_post, m_g_ffn_pre, m_g_ffn_post, m_w_in_even, m_w_pool, m_pool_scale, m_attn_sink, m_w_out_even, m_w_in_odd, m_sgu_ln_g, m_sgu_ln_b, m_sgu_w, m_sgu_b, m_w_out_odd, m_w_ffn_up, m_ffn_conv_w, m_ffn_conv_b, m_w_ffn_down, v_c_ctx, v_w_ada, v_b_ada, v_g_mix_pre, v_g_mix_post, v_g_ffn_pre, v_g_ffn_post, v_w_in_even, v_w_pool, v_pool_scale, v_attn_sink, v_w_out_even, v_w_in_odd, v_sgu_ln_g, v_sgu_ln_b, v_sgu_w, v_sgu_b, v_w_out_odd, v_w_ffn_up, v_ffn_conv_w, v_ffn_conv_b, v_w_ffn_down):
    P = dict(c_ctx=c_ctx, w_ada=w_ada, b_ada=b_ada, g_mix_pre=g_mix_pre, g_mix_post=g_mix_post, g_ffn_pre=g_ffn_pre,
             g_ffn_post=g_ffn_post, w_in_even=w_in_even, w_pool=w_pool, pool_scale=pool_scale, attn_sink=attn_sink,
             w_out_even=w_out_even, w_in_odd=w_in_odd, sgu_ln_g=sgu_ln_g, sgu_ln_b=sgu_ln_b, sgu_w=sgu_w, sgu_b=sgu_b,
             w_out_odd=w_out_odd, w_ffn_up=w_ffn_up, ffn_conv_w=ffn_conv_w, ffn_conv_b=ffn_conv_b, w_ffn_down=w_ffn_down)
    M = dict(c_ctx=m_c_ctx, w_ada=m_w_ada, b_ada=m_b_ada, g_mix_pre=m_g_mix_pre, g_mix_post=m_g_mix_post, g_ffn_pre=m_g_ffn_pre,
             g_ffn_post=m_g_ffn_post, w_in_even=m_w_in_even, w_pool=m_w_pool, pool_scale=m_pool_scale, attn_sink=m_attn_sink,
             w_out_even=m_w_out_even, w_in_odd=m_w_in_odd, sgu_ln_g=m_sgu_ln_g, sgu_ln_b=m_sgu_ln_b, sgu_w=m_sgu_w, sgu_b=m_sgu_b,
             w_out_odd=m_w_out_odd, w_ffn_up=m_w_ffn_up, ffn_conv_w=m_ffn_conv_w, ffn_conv_b=m_ffn_conv_b, w_ffn_down=m_w_ffn_down)
    V = dict(c_ctx=v_c_ctx, w_ada=v_w_ada, b_ada=v_b_ada, g_mix_pre=v_g_mix_pre, g_mix_post=v_g_mix_post, g_ffn_pre=v_g_ffn_pre,
             g_ffn_post=v_g_ffn_post, w_in_even=v_w_in_even, w_pool=v_w_pool, pool_scale=v_pool_scale, attn_sink=v_attn_sink,
             w_out_even=v_w_out_even, w_in_odd=v_w_in_odd, sgu_ln_g=v_sgu_ln_g, sgu_ln_b=v_sgu_ln_b, sgu_w=v_sgu_w, sgu_b=v_sgu_b,
             w_out_odd=v_w_out_odd, w_ffn_up=v_w_ffn_up, ffn_conv_w=v_ffn_conv_w, ffn_conv_b=v_ffn_conv_b, w_ffn_down=v_w_ffn_down)

    x = x[0]
    ctx = ctx[0]
    target = loss_target[0]
    L, D = x.shape
    C = ctx.shape[0]
    tm = min(512, L)
    tm_up = min(1024, L)
    conv_rows = min(256, L)
    me = 4 * lax.axis_index("x") + 2 * lax.axis_index("y") + lax.axis_index("c")
    n_ada = w_ada.shape[2]
    F = w_ffn_down.shape[1] * N_DEV
    half_f = F // 2

    n_cw = ffn_conv_w.shape[2]
    small = jnp.concatenate([_rows128(c), _rows128(sgu_ln_g), _rows128(sgu_ln_b), _rows128(ffn_conv_w)], axis=0)
    small_all = all_gather_small(small, name="gather_small_inputs")
    c_all = small_all[:, :8].reshape(N_DEV, D)
    ln_g = small_all[:, 8].reshape(1, D)
    ln_b = small_all[:, 16].reshape(1, D)
    conv_w = small_all[:, 24:].reshape(N_DEV, -1)[:, :2 * 3 * n_cw].reshape(N_DEV, 2, 3, n_cw)
    conv_w = conv_w.transpose(1, 2, 0, 3).reshape(2, 3, 2 * F)

    cs = jnp.concatenate([c_all, c_ctx[None, :], jnp.zeros((7, D), F32)], axis=0)
    b_loc = lax.dynamic_slice(b_ada, (0, me * n_ada), (2, n_ada))
    silu_c, mods_loc = ada_fwd_mm(cs, w_ada, b_loc, name="ada_fwd")
    mods_all = all_gather_small(mods_loc.reshape(-1, LANES), name="gather_mods")

    shards = [s.astype(BF16) for s in (w_in_even[0].T, w_out_even[0], w_ffn_up[0].T, w_ffn_down[0],
                                       w_in_odd[0].T, w_out_odd[0], w_ffn_up[1].T, w_ffn_down[1])]
    shards, mods_all = lax.optimization_barrier((shards, mods_all))
    w_sems, w_srcs, w_lands, _ = exchange_start(shards, place_own(shards, [s.shape[0] for s in shards], me, scatter=False, name="gather_own"),
                                              scatter=False, name="gather_start")

    def weight(j, after):
        return exchange_wait([w_srcs[j]], [w_lands[j]], w_sems[2 * j:2 * j + 2], after, scatter=False, name=f"gather_wait_{j}")[0]

    mods_all = mods_all.reshape(N_DEV, 2, 16, n_ada).transpose(1, 2, 0, 3).reshape(2, 16, 6 * D)
    mod = lambda i, row: [m_[None, :] for m_ in jnp.split(lax.dynamic_index_in_dim(mods_all[i], row, 0, False), 6)]
    sh_m, sc_m, gt_m, sh_f, sc_f, gt_f = zip(mod(0, me), mod(1, me))
    csh_m, csc_m = mod(0, N_DEV)[:2]

    row = lambda a, i: a[i][None, :]

    cos, sa, sb = _rope_tables(L)
    sink = attn_sink[0]
    bst = sgu_b[0].T
    wup, wdn = [None, None], [None, None]

    def ffn_fwd(i, xin):
        wup[i] = weight(2 + 4 * i, xin)
        h, hu = pre_mm(xin, row(g_ffn_pre, i), sh_f[i], sc_f[i], wup[i], tm=tm_up, tn=half_f, name=f"ffn_up_{i}")
        a, s1, s2 = conv_fwd(hu, conv_w[i], ffn_conv_b[i][None, :], rows=conv_rows, wblk=2 * LANES, name=f"ffn_conv_{i}")
        wdn[i] = weight(3 + 4 * i, a)
        f, xo = mm_post(a, wdn[i], xin, row(g_ffn_post, i), gt_f[i], tm=tm, name=f"ffn_down_{i}")
        return h, (hu, s1, s2), a, f, xo

    first_mod, cos, sa, sb = lax.optimization_barrier((sh_m[0], cos, sa, sb))
    win_e = permute_heads(weight(0, first_mod))
    h0, u, q, kv = inproj_even(x, row(g_mix_pre, 0), sh_m[0], sc_m[0], win_e, cos, sa, sb, tm=tm, name="in_even")
    hc, kvc = pre_mm(ctx, row(g_mix_pre, 0), csh_m, csc_m, win_e, tm=C, tn=2 * LANES, w_row_off=8 * LANES, name="in_even_ctx")
    pa = jnp.concatenate([pool_fwd(u, w_pool[0], pool_scale, name="pool_fwd"),
                          attn_fwd(q, kv, kvc, sink, name="attn_fwd")], axis=1)
    wout_e = permute_heads(weight(1, pa))
    y0, x1 = mm_post(pa, wout_e, x, row(g_mix_post, 0), gt_m[0], tm=tm, name="out_even")
    h1, hu0, a0, f0, x2 = ffn_fwd(0, x1)
    win_o = weight(4, x2)
    h2, z1 = pre_mm(x2, row(g_mix_pre, 1), sh_m[1], sc_m[1], win_o, tm=tm_up, tn=D, name="in_odd")
    us = sgu_fwd(z1, ln_g, ln_b, sgu_w[0], bst, name="sgu_fwd")
    wout_o = weight(5, us)
    y1, x3 = mm_post(us, wout_o, x2, row(g_mix_post, 1), gt_m[1], tm=tm, name="out_odd")
    h3, hu1, a1, f1, x4 = ffn_fwd(1, x3)
    loss_part, dx4 = loss_grad(x4, target, tm=tm, name="loss")

    g_srcs, g_lands, g_sems = [], [], []

    def scatter(grads, nm):
        own = place_own(grads, [g.shape[0] // N_DEV for g in grads], me, scatter=True, name=nm.replace("start", "own"))
        sems, srcs, lands, tok = exchange_start(grads, own, scatter=True, name=nm)
        g_srcs.extend(srcs)
        g_lands.extend(lands)
        g_sems.extend(sems)
        return tok[0:1, 0:1]

    def ffn_bwd(i, dxo, xin, h, hu, a, f, g_post):
        dyf, da, dg_post, dgt = post_bwd_mm(dxo, f, g_post, gt_f[i], wdn[i], tm=tm, name=f"ffn_down_bwd_{i}")
        dhg, dhu, dcwg, dcwu, dcbg, dcbu = conv_bwd(da, hu[1], hu[2], hu[0], conv_w[i], rows=conv_rows, wblk=2 * LANES,
                                                    name=f"ffn_conv_bwd_{i}")
        dxin, dg_pre, dsh, dsc = mm_pre_bwd([dhg, dhu], wup[i], xin, dxo, row(g_ffn_pre, i), sc_f[i], tm=tm, tk=half_f,
                                            name=f"ffn_up_bwd_{i}")
        g_dn = wgrad([a], dyf, tr=2 * LANES, name=f"wgrad_down_{i}")
        g_up = wgrad([dhg, dhu], h, tr=2 * LANES, name=f"wgrad_up_{i}")
        tok = scatter([g_dn, g_up], f"scatter_start_ffn_{i}")
        return dxin, tok, dict(g_ffn_post=dg_post, g_ffn_pre=dg_pre, gt_f=dgt, sh_f=dsh, sc_f=dsc,
                               ffn_conv_w=jnp.concatenate([dcwg, dcwu], axis=1), ffn_conv_b=jnp.concatenate([dcbg, dcbu], axis=1)[0])

    dx3, tok, sf1 = ffn_bwd(1, dx4, x3, h3, hu1, a1, f1, row(g_ffn_post, 1))
    dy1, dus, dg_mpost1, dgt_m1 = post_bwd_mm(dx3, y1, row(g_mix_post, 1) + tok, gt_m[1], wout_o, tm=tm, name="out_odd_bwd")
    dz1, dws, dbs, dlng, dlnb = sgu_bwd(z1, dus, ln_g, ln_b, sgu_w[0], bst, name="sgu_bwd")
    dx2, dg_mpre1, dsh_m1, dsc_m1 = mm_pre_bwd([dz1], win_o, x2, dx3, row(g_mix_pre, 1), sc_m[1], tm=tm, tk=D, name="in_odd_bwd")
    tok = scatter([wgrad([us], dy1, tr=2 * LANES, name="wgrad_out_odd"), wgrad([dz1], h2, tr=2 * LANES, name="wgrad_in_odd")],
                  "scatter_start_mix_1")

    dx1, tok, sf0 = ffn_bwd(0, dx2, x1, h1, hu0, a0, f0, row(g_ffn_post, 0) + tok)
    dy0, dpa, dg_mpost0, dgt_m0 = post_bwd_mm(dx1, y0, row(g_mix_post, 0) + tok, gt_m[0], wout_e, tm=tm, name="out_even_bwd")
    du, dwp, dps = pool_bwd(u, dpa, w_pool[0], pool_scale, name="pool_bwd")
    dq, dkv, dkvc, dsink = attn_bwd(q, kv, kvc, sink, dpa, cos, sa, sb, name="attn_bwd")
    dz0 = jnp.concatenate([du, dq, dkv], axis=1)
    dzc = jnp.concatenate([jnp.zeros((C, 8 * LANES), BF16), dkvc], axis=1)
    tok = scatter([permute_heads(wgrad([pa], dy0, tr=2 * LANES, name="wgrad_out_even"), inverse=True),
                   permute_heads(wgrad([dz0], h0, tr=2 * LANES, extra=(dzc, hc), name="wgrad_in_even"), inverse=True)],
                  "scatter_start_mix_0")
    grad_x, dg_mpre0, dsh_m0, dsc_m0 = mm_pre_bwd([dz0], win_e, x, dx1, row(g_mix_pre, 0) + tok, sc_m[0], tm=tm, tk=dz0.shape[1],
                                                  name="in_even_bwd")
    _, dg_mpre0c, dcsh, dcsc = mm_pre_bwd([dkvc], win_e, ctx, None, row(g_mix_pre, 0), csc_m, tm=C, tk=2 * LANES,
                                          w_row_off=8 * LANES, name="in_even_ctx_bwd")

    out = {}

    zero = jnp.zeros((1, D), F32)
    dmod0 = jnp.concatenate([dsh_m0, dsc_m0, dgt_m0, sf0["sh_f"], sf0["sc_f"], sf0["gt_f"]], axis=1)
    dmodc = jnp.concatenate([dcsh, dcsc, zero, zero, zero, zero], axis=1)
    dmod1 = jnp.concatenate([dsh_m1, dsc_m1, dgt_m1, sf1["sh_f"], sf1["sc_f"], sf1["gt_f"]], axis=1)
    dmods = jnp.concatenate([dmod0, dmodc, dmod1], axis=0)
    dmods_all = all_gather_small(dmods.reshape(-1, LANES), name="gather_dmods").reshape(N_DEV, 3, N_DEV, n_ada)
    dall = lax.dynamic_index_in_dim(dmods_all, me, 2, False)
    g_w_ada, dcc = ada_bwd_mm(silu_c, c_ctx[None, :], dall, w_ada, name="ada_bwd")

    rep = dict(
        c_ctx=dcc[0:1],
        b_ada=jnp.concatenate([dmod0 + dmodc, dmod1]),
        g_mix_pre=jnp.concatenate([dg_mpre0 + dg_mpre0c, dg_mpre1]),
        g_mix_post=jnp.concatenate([dg_mpost0, dg_mpost1]),
        g_ffn_pre=jnp.concatenate([sf0["g_ffn_pre"], sf1["g_ffn_pre"]]),
        g_ffn_post=jnp.concatenate([sf0["g_ffn_post"], sf1["g_ffn_post"]]),
        w_pool=_nat2d(dwp), pool_scale=dps, attn_sink=dsink[:, :N_Q_HEADS],
        sgu_w=_nat2d(dws), sgu_b=dbs[:, :sgu_b.shape[1]].T,
        ffn_conv_b=jnp.stack([sf0["ffn_conv_b"], sf1["ffn_conv_b"]]),
    )
    hi = loss_part.astype(BF16).astype(F32)
    mid = (loss_part - hi).astype(BF16).astype(F32)
    loss_piece = jnp.pad(jnp.concatenate([hi, mid, loss_part - hi - mid], axis=1), ((0, 7), (0, LANES - 3)))
    conv_g = jnp.stack([sf0["ffn_conv_w"], sf1["ffn_conv_w"]]).reshape(2 * 3, N_DEV, n_cw).swapaxes(0, 1)
    shard_full = dict(sgu_ln_g=dlng.reshape(N_DEV, LANES), sgu_ln_b=dlnb.reshape(N_DEV, LANES),
                      ffn_conv_w=jnp.concatenate([_pack_rows(conv_g[d]) for d in range(N_DEV)], axis=0))
    small_names = list(rep) + list(shard_full)
    pieces = [_pack_rows(rep[k]) for k in rep] + list(shard_full.values()) + [loss_piece]
    sizes = [p.shape[0] for p in pieces]
    offs = [sum(sizes[:i]) for i in range(len(sizes))]
    pieces.append(jnp.zeros((-sum(sizes) % 16, LANES), F32))
    gpack = jnp.concatenate(pieces, axis=0).astype(BF16)
    own = place_own([gpack], [gpack.shape[0]], me, scatter=False, name="smallgrad_own")
    s_sems, s_srcs, s_lands, small_tok = exchange_start([gpack], own, scatter=False, name="smallgrad_start")

    slots = exchange_wait(g_srcs, g_lands, g_sems, small_tok, scatter=True, name="scatter_wait")

    def update(name, lands, transposed):
        w_, m_, v_ = (a.transpose(0, 2, 1) if transposed else a for a in (P[name], M[name], V[name]))
        r = w_.shape[1]
        tr = r // 4 if r % 64 == 0 and r > 256 else r
        res = adamw(w_, m_, v_, [l_.reshape(N_DEV, r, l_.shape[1]) for l_ in lands], tr=tr, name=f"adamw_{name}")
        for kind, val in zip(("grad", "delta", "new_m", "new_v"), res):
            out[(kind, name)] = val.transpose(0, 2, 1) if transposed else val

    update("w_in_even", [slots[7]], True)
    update("w_out_even", [slots[6]], False)
    update("w_in_odd", [slots[3]], True)
    update("w_out_odd", [slots[2]], False)
    update("w_ffn_up", [slots[5], slots[1]], True)
    update("w_ffn_down", [slots[4], slots[0]], False)
    res = adamw(w_ada, m_w_ada, v_w_ada, [g_w_ada[l][None] for l in range(w_ada.shape[0])], tr=D // 4, name="adamw_w_ada")
    for kind, val in zip(("grad", "delta", "new_m", "new_v"), res):
        out[(kind, "w_ada")] = val

    gpacks = exchange_wait(s_srcs, s_lands, s_sems, out[("new_v", "w_ada")], scatter=False, name="smallgrad_wait")[0]
    per_dev = {k: shard_full[k].shape[0] // N_DEV for k in shard_full}
    params = [(_nat2d(P[k]), _nat2d(M[k]), _nat2d(V[k]), offs[i], per_dev.get(k, 0)) for i, k in enumerate(small_names)]
    res = small_update(gpacks.reshape(N_DEV, -1, LANES), jnp.reshape(me, (1,)).astype(jnp.int32), params, offs[-1], name="adamw_small")
    for i, k in enumerate(small_names):
        for kind, val in zip(("grad", "delta", "new_m", "new_v"), res[4 * i:4 * i + 4]):
            out[(kind, k)] = val.reshape(P[k].shape)
    loss = res[-1][0, 0]

    names = list(P)
    final = [loss, grad_x[None]]
    for kind in ("grad", "delta", "new_m", "new_v"):
        for k in names:
            val = out[(kind, k)]
            final.append(val)
    return tuple(final)
```

```python
import functools
import math

import jax
import jax.numpy as jnp
from jax import lax
from jax.experimental import pallas as pl
from jax.experimental.pallas import tpu as pltpu

F32 = jnp.float32
BF16 = jnp.bfloat16
MESH = pl.DeviceIdType.MESH
N_DEV = 8
LANES = 128
VMEM_LIMIT = 48 * 1024 * 1024
EPS = 1e-6
NEG_INF = -1e30
GRID_W = 64
WINDOW = 128
BLK = 128
HEAD_DIM = 64
N_Q_HEADS = 8
N_KV_HEADS = 2
GQA = N_Q_HEADS // N_KV_HEADS
POOL_WINDOWS = (2, 4, 8, 16)
ROPE_BASE = 10000.0
ROPE_FREQS = HEAD_DIM // 4
PAD = 16
ADAM_LR, ADAM_B1, ADAM_B2, ADAM_EPS, ADAM_WD, ADAM_STEP = 0.001, 0.9, 0.999, 1e-08, 0.01, 10
BC1 = 1.0 - ADAM_B1 ** ADAM_STEP
BC2 = 1.0 - ADAM_B2 ** ADAM_STEP
SQRT_2_OVER_PI = math.sqrt(2.0 / math.pi)
GELU_C = 0.044715


def _cp(sem=None):
    return pltpu.CompilerParams(dimension_semantics=sem, vmem_limit_bytes=VMEM_LIMIT)


def _dot(a, b):
    return jnp.dot(a, b, preferred_element_type=F32)


def _dot_nt(a, b):
    return lax.dot_general(a, b, (((1,), (1,)), ((), ())), preferred_element_type=F32)


def _dot_tn(a, b):
    return lax.dot_general(a, b, (((0,), (0,)), ((), ())), preferred_element_type=F32)


def _rms(x):
    r = lax.rsqrt(jnp.mean(x * x, axis=-1, keepdims=True) + EPS)
    return x * r, r


def _rms_bwd(dn, n, r):
    return r * (dn - n * jnp.mean(dn * n, axis=-1, keepdims=True))


def _colsum(a):
    return jnp.sum(a, axis=0, keepdims=True)


def _rope(x, c, sa, sb):
    return x * c + pltpu.roll(x, LANES - ROPE_FREQS, 1) * sa + pltpu.roll(x, ROPE_FREQS, 1) * sb


def _full(shape):
    return pl.BlockSpec(shape, lambda *_: (0,) * len(shape))


def pre_mm(x, g, sh, sc, wt, *, tm, tn, w_row_off=0, name):
    T, D = x.shape
    n_rows = wt.shape[0] - w_row_off
    off = w_row_off // tn

    def body(x_ref, g_ref, sh_ref, sc_ref, w_ref, h_ref, z_ref):
        @pl.when(pl.program_id(1) == 0)
        def _():
            n, _ = _rms(x_ref[...])
            h_ref[...] = (n * g_ref[...] * (1.0 + sc_ref[...]) + sh_ref[...]).astype(BF16)

        z_ref[...] = _dot_nt(h_ref[...], w_ref[...]).astype(BF16)

    vec = pl.BlockSpec((1, D), lambda i, j: (0, 0))
    return pl.pallas_call(
        body, name=name, grid=(T // tm, n_rows // tn),
        in_specs=[pl.BlockSpec((tm, D), lambda i, j: (i, 0)), vec, vec, vec, pl.BlockSpec((tn, D), lambda i, j: (j + off, 0))],
        out_specs=[pl.BlockSpec((tm, D), lambda i, j: (i, 0)), pl.BlockSpec((tm, tn), lambda i, j: (i, j))],
        out_shape=[jax.ShapeDtypeStruct((T, D), BF16), jax.ShapeDtypeStruct((T, n_rows), BF16)],
        compiler_params=_cp(("parallel", "arbitrary")),
    )(x, g, sh, sc, wt)


def inproj_even(x, g, sh, sc, wt, cos, sa, sb, *, tm, name):
    T, D = x.shape
    N = wt.shape[0]

    def body(x_ref, g_ref, sh_ref, sc_ref, w_ref, c_ref, sa_ref, sb_ref, h_ref, u_ref, q_ref, kv_ref):
        n, _ = _rms(x_ref[...])
        h = (n * g_ref[...] * (1.0 + sc_ref[...]) + sh_ref[...]).astype(BF16)
        h_ref[...] = h
        z = _dot_nt(h, w_ref[...])
        u_ref[...] = z[:, :4 * LANES]
        c, a, b = c_ref[...], sa_ref[...], sb_ref[...]
        for s in range(4):
            q_ref[:, s * LANES:(s + 1) * LANES] = _rope(z[:, (4 + s) * LANES:(5 + s) * LANES], c, a, b).astype(BF16)
        kv_ref[:, :LANES] = _rope(z[:, 8 * LANES:9 * LANES], c, a, b).astype(BF16)
        kv_ref[:, LANES:] = z[:, 9 * LANES:].astype(BF16)

    vec = pl.BlockSpec((1, D), lambda i: (0, 0))
    row = lambda w: pl.BlockSpec((tm, w), lambda i: (i, 0))
    return pl.pallas_call(
        body, name=name, grid=(T // tm,),
        in_specs=[row(D), vec, vec, vec, _full((N, D)), row(LANES), row(LANES), row(LANES)],
        out_specs=[row(D), row(4 * LANES), row(4 * LANES), row(2 * LANES)],
        out_shape=[jax.ShapeDtypeStruct((T, D), BF16), jax.ShapeDtypeStruct((T, 4 * LANES), F32),
                   jax.ShapeDtypeStruct((T, 4 * LANES), BF16), jax.ShapeDtypeStruct((T, 2 * LANES), BF16)],
        compiler_params=_cp(("parallel",)),
    )(x, g, sh, sc, wt, cos, sa, sb)


def mm_post(a_parts, w, x, g, gt, *, tm, target=None, name):
    T = a_parts[0].shape[0]
    D = w.shape[1]
    npart = len(a_parts)
    offs = [sum(a_.shape[1] for a_ in a_parts[:p]) for p in range(npart + 1)]
    with_loss = target is not None

    def body(*refs):
        a_refs, (w_ref, x_ref, g_ref, gt_ref) = refs[:npart], refs[npart:npart + 4]
        y = _dot(a_refs[0][...], w_ref[offs[0]:offs[1], :])
        for p in range(1, npart):
            y = y + _dot(a_refs[p][...], w_ref[offs[p]:offs[p + 1], :])
        n, _ = _rms(y)
        xn = x_ref[...] + gt_ref[...] * (n * g_ref[...])
        if not with_loss:
            y_ref, xn_ref = refs[npart + 4:]
            y_ref[...] = y.astype(BF16)
            xn_ref[...] = xn
            return
        t_ref, y_ref, d_ref, l_ref = refs[npart + 4:]
        y_ref[...] = y.astype(BF16)

        @pl.when(pl.program_id(0) == 0)
        def _():
            l_ref[...] = jnp.zeros_like(l_ref)

        e = xn - t_ref[...]
        l_ref[...] += 0.5 * jnp.sum(jnp.mean(e * e, axis=-1, keepdims=True), axis=0, keepdims=True)
        d_ref[...] = e * (1.0 / D)

    vec = pl.BlockSpec((1, D), lambda i: (0, 0))
    row = lambda w_: pl.BlockSpec((tm, w_), lambda i: (i, 0))
    in_specs = [row(a_.shape[1]) for a_ in a_parts] + [_full(w.shape), row(D), vec, vec]
    out_specs = [row(D), row(D)]
    out_shape = [jax.ShapeDtypeStruct((T, D), BF16), jax.ShapeDtypeStruct((T, D), F32)]
    if with_loss:
        in_specs.append(row(D))
        out_specs.append(_full((1, 1)))
        out_shape.append(jax.ShapeDtypeStruct((1, 1), F32))
    return pl.pallas_call(
        body, name=name, grid=(T // tm,), in_specs=in_specs, out_specs=out_specs, out_shape=out_shape,
        compiler_params=_cp(("arbitrary",) if with_loss else ("parallel",)),
    )(*a_parts, w, x, g, gt, *((target,) if with_loss else ()))


def post_bwd_mm(dxn, y, g, gt, w, *, tm, name):
    T, D = y.shape
    K = w.shape[0]

    def body(dxn_ref, y_ref, g_ref, gt_ref, w_ref, dy_ref, da_ref, dg_ref, dgt_ref):
        @pl.when(pl.program_id(0) == 0)
        def _():
            dg_ref[...] = jnp.zeros_like(dg_ref)
            dgt_ref[...] = jnp.zeros_like(dgt_ref)

        d = dxn_ref[...]
        n, r = _rms(y_ref[...].astype(F32))
        g_, gt_ = g_ref[...], gt_ref[...]
        dg_ref[...] += _colsum(d * gt_ * n)
        dgt_ref[...] += _colsum(d * g_ * n)
        dy = _rms_bwd(d * (gt_ * g_), n, r).astype(BF16)
        dy_ref[...] = dy
        da_ref[...] = _dot_nt(dy, w_ref[...]).astype(BF16)

    vec = pl.BlockSpec((1, D), lambda i: (0, 0))
    row = lambda w_: pl.BlockSpec((tm, w_), lambda i: (i, 0))
    return pl.pallas_call(
        body, name=name, grid=(T // tm,),
        in_specs=[row(D), row(D), vec, vec, _full((K, D))],
        out_specs=[row(D), row(K), vec, vec],
        out_shape=[jax.ShapeDtypeStruct((T, D), BF16), jax.ShapeDtypeStruct((T, K), BF16),
                   jax.ShapeDtypeStruct((1, D), F32), jax.ShapeDtypeStruct((1, D), F32)],
        compiler_params=_cp(("arbitrary",)),
    )(dxn, y, g, gt, w)


def mm_pre_bwd(dzs, wt, x, dres, g, sc, *, tm, tk, w_row_off=0, name):
    T, N = dzs[0].shape
    D = x.shape[1]
    nk = N // tk
    npart = len(dzs)
    off = w_row_off // tk
    has_res = dres is not None

    def body(*refs):
        dz_refs = refs[:npart]
        w_refs = refs[npart:2 * npart]
        rest = refs[2 * npart:]
        x_ref = rest[0]
        dres_ref = rest[1] if has_res else None
        g_ref, sc_ref, dx_ref, dg_ref, dsh_ref, dsc_ref, acc = rest[1 + has_res:]
        i, k = pl.program_id(0), pl.program_id(1)

        @pl.when(jnp.logical_and(i == 0, k == 0))
        def _():
            dg_ref[...] = jnp.zeros_like(dg_ref)
            dsh_ref[...] = jnp.zeros_like(dsh_ref)
            dsc_ref[...] = jnp.zeros_like(dsc_ref)

        part = _dot(dz_refs[0][...], w_refs[0][...])
        for p in range(1, npart):
            part = part + _dot(dz_refs[p][...], w_refs[p][...])

        @pl.when(k == 0)
        def _():
            acc[...] = part

        @pl.when(k > 0)
        def _():
            acc[...] += part

        @pl.when(k == nk - 1)
        def _():
            dh = acc[...]
            n, r = _rms(x_ref[...])
            g_, s1 = g_ref[...], 1.0 + sc_ref[...]
            dsh_ref[...] += _colsum(dh)
            dsc_ref[...] += _colsum(dh * n * g_)
            dg_ref[...] += _colsum(dh * s1 * n)
            dxp = _rms_bwd(dh * (g_ * s1), n, r)
            dx_ref[...] = dxp + dres_ref[...] if has_res else dxp

    vec = pl.BlockSpec((1, D), lambda i, k: (0, 0))
    row = pl.BlockSpec((tm, D), lambda i, k: (i, 0))
    w_specs = [pl.BlockSpec((tk, D), (lambda i, k, p=p: (k + off + p * nk, 0))) for p in range(npart)]
    res_specs, res_args = ([row], (dres,)) if has_res else ([], ())
    return pl.pallas_call(
        body, name=name, grid=(T // tm, nk),
        in_specs=[pl.BlockSpec((tm, tk), lambda i, k: (i, k))] * npart + w_specs + [row] + res_specs + [vec, vec],
        out_specs=[row, vec, vec, vec],
        out_shape=[jax.ShapeDtypeStruct((T, D), F32)] + [jax.ShapeDtypeStruct((1, D), F32)] * 3,
        scratch_shapes=[pltpu.VMEM((tm, D), F32)],
        compiler_params=_cp(("arbitrary", "arbitrary")),
    )(*dzs, *([wt] * npart), x, *res_args, g, sc)


def wgrad(a_parts, b, *, tr, extra=None, name):
    T, R = a_parts[0].shape
    D = b.shape[1]
    npart = len(a_parts)
    nr = R // tr

    def body(*refs):
        a_refs, b_ref = refs[:npart], refs[npart]
        g_ref = refs[-1]
        for p in range(npart):
            @pl.when(pl.program_id(0) // nr == p)
            def _():
                acc = _dot_tn(a_refs[p][...], b_ref[...])
                if extra is not None:
                    acc += _dot_tn(refs[npart + 1][...], refs[npart + 2][...])
                g_ref[...] = acc.astype(BF16)

    in_specs = [pl.BlockSpec((T, tr), (lambda r, p=p: (0, jnp.clip(r - p * nr, 0, nr - 1)))) for p in range(npart)]
    in_specs.append(_full((T, D)))
    args = [*a_parts, b]
    if extra is not None:
        a2, b2 = extra
        in_specs += [pl.BlockSpec((a2.shape[0], tr), lambda r: (0, r)), _full(b2.shape)]
        args += [a2, b2]
    return pl.pallas_call(
        body, name=name, grid=(npart * nr,),
        in_specs=in_specs, out_specs=pl.BlockSpec((tr, D), lambda r: (r, 0)),
        out_shape=jax.ShapeDtypeStruct((npart * R, D), BF16),
        compiler_params=_cp(("parallel",)),
    )(*args)


def _conv_ext(ref, r0, rows, total):
    top = ref[pl.ds(pl.multiple_of(jnp.maximum(r0 - PAD, 0), PAD), PAD), :]
    mid = ref[pl.ds(r0, rows), :]
    bot = ref[pl.ds(pl.multiple_of(jnp.minimum(r0 + rows, total - PAD), PAD), PAD), :]
    top = jnp.where(r0 > 0, top, jnp.zeros_like(top))
    bot = jnp.where(r0 + rows < total, bot, jnp.zeros_like(bot))
    return jnp.concatenate([top, mid, bot], axis=0).astype(F32)


def _shift_rows(a, k):
    return pltpu.roll(a, k % a.shape[0], 0)


def _conv3(x, w, b):
    return w[0:1] * _shift_rows(x, 1) + w[1:2] * x + w[2:3] * _shift_rows(x, -1) + b


def _gate_up_specs(rows_, wblk, nb):
    return [pl.BlockSpec((rows_, wblk), lambda j: (0, j)), pl.BlockSpec((rows_, wblk), lambda j: (0, j + nb))]


def conv_fwd(hu, cw, cb, *, rows, wblk, name):
    L, N2 = hu.shape
    nb = N2 // 2 // wblk
    nchunk = L // rows

    def body(hg_ref, hu_ref, wg_ref, wu_ref, bg_ref, bu_ref, a_ref, s1_ref, s2_ref):
        def chunk(ci, carry):
            r0 = pl.multiple_of(ci * rows, rows)
            gate = _conv3(_conv_ext(hg_ref, r0, rows, L), wg_ref[...], bg_ref[...])[PAD:PAD + rows]
            up = _conv3(_conv_ext(hu_ref, r0, rows, L), wu_ref[...], bu_ref[...])[PAD:PAD + rows]
            sg = jax.nn.sigmoid(gate)
            silu = gate * sg
            at = pl.ds(r0, rows)
            a_ref[at, :] = (silu * up).astype(BF16)
            s1_ref[at, :] = silu.astype(BF16)
            s2_ref[at, :] = (up * (sg + silu * (1.0 - sg))).astype(BF16)
            return carry

        lax.fori_loop(0, nchunk, chunk, 0)

    out = pl.BlockSpec((L, wblk), lambda j: (0, j))
    return pl.pallas_call(
        body, name=name, grid=(nb,),
        in_specs=_gate_up_specs(L, wblk, nb) + _gate_up_specs(3, wblk, nb) + _gate_up_specs(1, wblk, nb),
        out_specs=[out] * 3, out_shape=[jax.ShapeDtypeStruct((L, N2 // 2), BF16)] * 3,
        compiler_params=_cp(("parallel",)),
    )(hu, hu, cw, cw, cb, cb)


def conv_bwd(da, s1, s2, hu, cw, *, rows, wblk, name):
    L, N2 = hu.shape
    F = N2 // 2
    nb = F // wblk
    nchunk = L // rows
    mid = slice(PAD, PAD + rows)

    def body(da_ref, s1_ref, s2_ref, hg_ref, hu_ref, wg_ref, wu_ref, dg_ref, du_ref, dwg_ref, dwu_ref, dbg_ref, dbu_ref):
        for ref in (dwg_ref, dwu_ref, dbg_ref, dbu_ref):
            ref[...] = jnp.zeros_like(ref)

        def half_bwd(x_ref, dh, w_ref, dx_ref, dw_ref, db_ref, r0):
            w = w_ref[...]
            nxt, prv = _shift_rows(dh, -1)[mid], _shift_rows(dh, 1)[mid]
            dhm, xm = dh[mid], x_ref[pl.ds(r0, rows), :].astype(F32)
            dx_ref[pl.ds(r0, rows), :] = (w[0:1] * nxt + w[1:2] * dhm + w[2:3] * prv).astype(BF16)
            db_ref[...] += _colsum(dhm)
            dw_ref[0:1, :] += _colsum(nxt * xm)
            dw_ref[1:2, :] += _colsum(dhm * xm)
            dw_ref[2:3, :] += _colsum(prv * xm)

        def chunk(ci, carry):
            r0 = pl.multiple_of(ci * rows, rows)
            d = _conv_ext(da_ref, r0, rows, L)
            half_bwd(hu_ref, d * _conv_ext(s1_ref, r0, rows, L), wu_ref, du_ref, dwu_ref, dbu_ref, r0)
            half_bwd(hg_ref, d * _conv_ext(s2_ref, r0, rows, L), wg_ref, dg_ref, dwg_ref, dbg_ref, r0)
            return carry

        lax.fori_loop(0, nchunk, chunk, 0)

    blk = lambda r: pl.BlockSpec((r, wblk), lambda j: (0, j))
    return pl.pallas_call(
        body, name=name, grid=(nb,),
        in_specs=[blk(L)] * 3 + _gate_up_specs(L, wblk, nb) + _gate_up_specs(3, wblk, nb),
        out_specs=[blk(L), blk(L), blk(3), blk(3), blk(1), blk(1)],
        out_shape=[jax.ShapeDtypeStruct((L, F), BF16)] * 2 + [jax.ShapeDtypeStruct((3, F), F32)] * 2
        + [jax.ShapeDtypeStruct((1, F), F32)] * 2,
        compiler_params=_cp(("parallel",)),
    )(da, s1, s2, hu, hu, cw, cw)


def _window_sums(pad_ref, w, lead):
    a = pad_ref[...]
    k = 1
    while k < w:
        a = a + _shift_rows(a, -k)
        k *= 2
    return _shift_rows(a, lead) if lead else a


def _pool_counts(L, h):
    t = lax.broadcasted_iota(jnp.int32, (L, 1), 0)
    return (jnp.minimum(t + h, L) - jnp.maximum(t - h, 0)).astype(F32)


def _pooled(u_ref, pad_ref, L, w):
    h = w // 2
    pad_ref[pl.ds(PAD, L), :] = u_ref[...]
    win = _window_sums(pad_ref, w, h)[PAD:PAD + L]
    return win / _pool_counts(L, h) - u_ref[...]


def _zero_pad_edges(pad_ref, L):
    z = jnp.zeros((PAD, LANES), F32)
    pad_ref[pl.ds(0, PAD), :] = z
    pad_ref[pl.ds(PAD + L, PAD), :] = z


def pool_fwd(u, w_pool, pool_scale, *, name):
    L = u.shape[0]

    def body(u_ref, w_ref, ps_ref, p_ref, pad_ref):
        _zero_pad_edges(pad_ref, L)
        for gi, win in enumerate(POOL_WINDOWS):
            @pl.when(pl.program_id(0) == gi)
            def _():
                pooled = _pooled(u_ref, pad_ref, L, win)
                p_ref[...] = (_dot(pooled.astype(BF16), w_ref[...].astype(BF16)) * ps_ref[...]).astype(BF16)

    return pl.pallas_call(
        body, name=name, grid=(len(POOL_WINDOWS),),
        in_specs=[pl.BlockSpec((L, LANES), lambda gi: (0, gi)), pl.BlockSpec((None, LANES, LANES), lambda gi: (gi, 0, 0)),
                  pl.BlockSpec((1, LANES), lambda gi: (0, gi))],
        out_specs=pl.BlockSpec((L, LANES), lambda gi: (0, gi)),
        out_shape=jax.ShapeDtypeStruct((L, 4 * LANES), BF16),
        scratch_shapes=[pltpu.VMEM((L + 2 * PAD, LANES), F32)],
        compiler_params=_cp(("parallel",)),
    )(u, w_pool, pool_scale)


def pool_bwd(u, dpa, w_pool, pool_scale, *, name):
    L = u.shape[0]

    def body(u_ref, dp_ref, w_ref, ps_ref, du_ref, dw_ref, dps_ref, pad_ref):
        _zero_pad_edges(pad_ref, L)
        for gi, win in enumerate(POOL_WINDOWS):
            @pl.when(pl.program_id(0) == gi)
            def _():
                h = win // 2
                wb = w_ref[...].astype(BF16)
                pooled = _pooled(u_ref, pad_ref, L, win).astype(BF16)
                dp = dp_ref[...].astype(F32)
                dps_ref[...] = _colsum(dp * _dot(pooled, wb))
                dy = (dp * ps_ref[...]).astype(BF16)
                dw_ref[...] = _dot_tn(pooled, dy)
                dpooled = _dot_nt(dy, wb)
                pad_ref[pl.ds(PAD, L), :] = dpooled / _pool_counts(L, h)
                du_ref[...] = (_window_sums(pad_ref, win, h - 1)[PAD:PAD + L] - dpooled).astype(BF16)

    return pl.pallas_call(
        body, name=name, grid=(len(POOL_WINDOWS),),
        in_specs=[pl.BlockSpec((L, LANES), lambda gi: (0, gi)), pl.BlockSpec((L, LANES), lambda gi: (0, gi)),
                  pl.BlockSpec((None, LANES, LANES), lambda gi: (gi, 0, 0)), pl.BlockSpec((1, LANES), lambda gi: (0, gi))],
        out_specs=[pl.BlockSpec((L, LANES), lambda gi: (0, gi)), pl.BlockSpec((None, LANES, LANES), lambda gi: (gi, 0, 0)),
                   pl.BlockSpec((1, LANES), lambda gi: (0, gi))],
        out_shape=[jax.ShapeDtypeStruct((L, 4 * LANES), BF16), jax.ShapeDtypeStruct((4, LANES, LANES), F32),
                   jax.ShapeDtypeStruct((1, 4 * LANES), F32)],
        scratch_shapes=[pltpu.VMEM((L + 2 * PAD, LANES), F32)],
        compiler_params=_cp(("parallel",)),
    )(u, dpa, w_pool, pool_scale)


def _attn_probs(qk, band_k, ctx_k, sink_ref, kh, mask4):
    s_loc = jnp.where(mask4, _dot_nt(qk, band_k), NEG_INF)
    s_ctx = _dot_nt(qk, ctx_k)
    sk = jnp.concatenate([jnp.full((BLK, 1), sink_ref[kh * GQA + hh], F32) for hh in range(GQA)], axis=0)
    m = jnp.maximum(jnp.maximum(jnp.max(s_loc, axis=-1, keepdims=True), jnp.max(s_ctx, axis=-1, keepdims=True)), sk)
    e_loc, e_ctx, e_s = jnp.exp(s_loc - m), jnp.exp(s_ctx - m), jnp.exp(sk - m)
    inv = 1.0 / (jnp.sum(e_loc, axis=-1, keepdims=True) + jnp.sum(e_ctx, axis=-1, keepdims=True) + e_s)
    return e_loc * inv, e_ctx * inv, e_s * inv


def _attn_block(n, L):
    start = pl.multiple_of(jnp.clip((n - 1) * BLK, 0, L - 3 * BLK), BLK)
    qpos = n * BLK + lax.broadcasted_iota(jnp.int32, (BLK, 3 * BLK), 0)
    kpos = start + lax.broadcasted_iota(jnp.int32, (BLK, 3 * BLK), 1)
    mask = jnp.abs(kpos - qpos) <= WINDOW
    return start, jnp.concatenate([mask] * GQA, axis=0)


def _stack_slabs(ref):
    return jnp.concatenate([ref[:, s * LANES:(s + 1) * LANES] for s in range(GQA)], axis=0)


def _kv_head_lanes(kh):
    return (lax.broadcasted_iota(jnp.int32, (1, LANES), 1) // HEAD_DIM) == kh


def permute_heads(w, inverse=False):
    lo, hi = 4 * LANES, 8 * LANES
    mid = w[lo:hi].reshape(*((GQA, N_KV_HEADS) if inverse else (N_KV_HEADS, GQA)), HEAD_DIM, w.shape[1])
    return jnp.concatenate([w[:lo], mid.swapaxes(0, 1).reshape(hi - lo, w.shape[1]), w[hi:]], axis=0)


def attn_fwd(q, kv, kvc, sink, *, name):
    L = q.shape[0]
    C = kvc.shape[0]
    scale = HEAD_DIM ** -0.5

    def body(q_ref, kv_ref, kvc_ref, sink_ref, o_ref):
        start, mask4 = _attn_block(pl.program_id(0), L)
        band = kv_ref[pl.ds(start, 3 * BLK), :]
        kvc_ = kvc_ref[...]
        qs = _stack_slabs(q_ref) * scale
        o = jnp.zeros((GQA * BLK, LANES), F32)
        for kh in range(N_KV_HEADS):
            grp = _kv_head_lanes(kh)
            qk = jnp.where(grp, qs, jnp.zeros_like(qs))
            p_loc, p_ctx, _ = _attn_probs(qk, band[:, :LANES], kvc_[:, :LANES], sink_ref, kh, mask4)
            o = o + jnp.where(grp, _dot(p_loc.astype(BF16), band[:, LANES:]) + _dot(p_ctx.astype(BF16), kvc_[:, LANES:]), 0.0)
        for s in range(GQA):
            o_ref[:, s * LANES:(s + 1) * LANES] = o[s * BLK:(s + 1) * BLK].astype(BF16)

    return pl.pallas_call(
        body, name=name, grid=(L // BLK,),
        in_specs=[pl.BlockSpec((BLK, 4 * LANES), lambda n: (n, 0)), _full((L, 2 * LANES)), _full((C, 2 * LANES)),
                  pl.BlockSpec(memory_space=pltpu.SMEM)],
        out_specs=pl.BlockSpec((BLK, 4 * LANES), lambda n: (n, 0)),
        out_shape=jax.ShapeDtypeStruct((L, 4 * LANES), BF16),
        compiler_params=_cp(("parallel",)),
    )(q, kv, kvc, sink)


def attn_bwd(q, kv, kvc, sink, dpa, cos, sa, sb, *, name):
    L = q.shape[0]
    C = kvc.shape[0]
    nb = L // BLK
    scale = HEAD_DIM ** -0.5

    def body(q_ref, kv_ref, kvc_ref, sink_ref, do_ref, c_ref, sa_ref, sb_ref, cq_ref, saq_ref, sbq_ref,
             dq_ref, dkv_ref, dkvc_ref, dsink_ref, dkv_acc, dkvc_acc):
        n = pl.program_id(0)

        @pl.when(n == 0)
        def _():
            dkv_acc[...] = jnp.zeros_like(dkv_acc)
            dkvc_acc[...] = jnp.zeros_like(dkvc_acc)
            dsink_ref[...] = jnp.zeros_like(dsink_ref)

        start, mask4 = _attn_block(n, L)
        band = kv_ref[pl.ds(start, 3 * BLK), :]
        kvc_ = kvc_ref[...]
        band_k, band_v, ctx_k, ctx_v = band[:, :LANES], band[:, LANES:], kvc_[:, :LANES], kvc_[:, LANES:]
        qs = _stack_slabs(q_ref) * scale
        dos = _stack_slabs(do_ref)
        lane = lax.broadcasted_iota(jnp.int32, (1, LANES), 1)
        dsink = jnp.zeros((1, LANES), F32)
        dq = jnp.zeros((GQA * BLK, LANES), F32)
        dk = jnp.zeros((3 * BLK, LANES), F32)
        dv = jnp.zeros((3 * BLK, LANES), F32)
        dkc = jnp.zeros((C, LANES), F32)
        dvc = jnp.zeros((C, LANES), F32)
        for kh in range(N_KV_HEADS):
            grp = _kv_head_lanes(kh)
            qk = jnp.where(grp, qs, jnp.zeros_like(qs))
            dok = jnp.where(grp, dos, jnp.zeros_like(dos))
            p_loc, p_ctx, p_s = _attn_probs(qk, band_k, ctx_k, sink_ref, kh, mask4)
            dp_loc = _dot_nt(dok, band_v)
            dp_ctx = _dot_nt(dok, ctx_v)
            delta = jnp.sum(p_loc * dp_loc, axis=-1, keepdims=True) + jnp.sum(p_ctx * dp_ctx, axis=-1, keepdims=True)
            ds_loc = (p_loc * (dp_loc - delta)).astype(BF16)
            ds_ctx = (p_ctx * (dp_ctx - delta)).astype(BF16)
            dsk = p_s * delta
            for hh in range(GQA):
                dsink = dsink - jnp.where(lane == kh * GQA + hh, jnp.sum(dsk[hh * BLK:(hh + 1) * BLK], axis=0, keepdims=True), 0.0)
            dq = dq + jnp.where(grp, _dot(ds_loc, band_k) + _dot(ds_ctx, ctx_k), 0.0)
            dk = dk + _dot_tn(ds_loc, qk)
            dv = dv + _dot_tn(p_loc.astype(BF16), dok)
            dkc = dkc + _dot_tn(ds_ctx, qk)
            dvc = dvc + _dot_tn(p_ctx.astype(BF16), dok)
        dsink_ref[...] += dsink
        dkv_acc[pl.ds(start, 3 * BLK), :LANES] += dk
        dkv_acc[pl.ds(start, 3 * BLK), LANES:] += dv
        dkvc_acc[:, :LANES] += dkc
        dkvc_acc[:, LANES:] += dvc
        c, a, b = cq_ref[...], -saq_ref[...], -sbq_ref[...]
        for s in range(GQA):
            dq_ref[:, s * LANES:(s + 1) * LANES] = _rope(dq[s * BLK:(s + 1) * BLK] * scale, c, a, b).astype(BF16)

        @pl.when(n == nb - 1)
        def _():
            dkv_ref[:, :LANES] = _rope(dkv_acc[:, :LANES], c_ref[...], -sa_ref[...], -sb_ref[...]).astype(BF16)
            dkv_ref[:, LANES:] = dkv_acc[:, LANES:].astype(BF16)
            dkvc_ref[...] = dkvc_acc[...].astype(BF16)

    blk = lambda w: pl.BlockSpec((BLK, w), lambda n: (n, 0))
    return pl.pallas_call(
        body, name=name, grid=(nb,),
        in_specs=[blk(4 * LANES), _full((L, 2 * LANES)), _full((C, 2 * LANES)), pl.BlockSpec(memory_space=pltpu.SMEM),
                  pl.BlockSpec((BLK, 4 * LANES), lambda n: (n, 1)),
                  _full((L, LANES)), _full((L, LANES)), _full((L, LANES)), blk(LANES), blk(LANES), blk(LANES)],
        out_specs=[blk(4 * LANES), _full((L, 2 * LANES)), _full((C, 2 * LANES)), _full((1, LANES))],
        out_shape=[jax.ShapeDtypeStruct((L, 4 * LANES), BF16), jax.ShapeDtypeStruct((L, 2 * LANES), BF16),
                   jax.ShapeDtypeStruct((C, 2 * LANES), BF16), jax.ShapeDtypeStruct((1, LANES), F32)],
        scratch_shapes=[pltpu.VMEM((L, 2 * LANES), F32), pltpu.VMEM((C, 2 * LANES), F32)],
        compiler_params=_cp(("arbitrary",)),
    )(q, kv, kvc, sink, dpa, cos, sa, sb, cos, sa, sb)


def _gelu_parts(x):
    th = jnp.tanh(SQRT_2_OVER_PI * (x + GELU_C * x * x * x))
    return 0.5 * x * (1.0 + th), th


def _gelu_grad(x, th):
    return 0.5 * (1.0 + th) + 0.5 * x * (1.0 - th * th) * SQRT_2_OVER_PI * (1.0 + 3.0 * GELU_C * x * x)


def _layernorm(v):
    mu = jnp.mean(v, axis=-1, keepdims=True)
    vc = v - mu
    rstd = lax.rsqrt(jnp.mean(vc * vc, axis=-1, keepdims=True) + EPS)
    return vc * rstd, rstd


def sgu_fwd(z1, ln_g, ln_b, ws, bst, *, name):
    L, W2 = z1.shape
    W = W2 // 2
    ng = W // LANES

    def body(z_ref, g_ref, b_ref, ws_ref, bs_ref, o_ref):
        z, _ = _gelu_parts(z_ref[...].astype(F32))
        xhat, _ = _layernorm(z[:, W:])
        vln = (xhat * g_ref[...] + b_ref[...]).astype(BF16)
        for gi in range(ng):
            cs = slice(gi * LANES, (gi + 1) * LANES)
            s = _dot(ws_ref[gi].astype(BF16), vln[:, cs]) + bs_ref[:, gi:gi + 1]
            o_ref[:, cs] = (z[:, cs] * s).astype(BF16)

    vec = _full((1, W))
    return pl.pallas_call(
        body, name=name, grid=(L // BLK,),
        in_specs=[pl.BlockSpec((BLK, W2), lambda n: (n, 0)), vec, vec, _full((ng, LANES, LANES)), _full((BLK, ng))],
        out_specs=pl.BlockSpec((BLK, W), lambda n: (n, 0)),
        out_shape=jax.ShapeDtypeStruct((L, W), BF16),
        compiler_params=_cp(("parallel",)),
    )(z1, ln_g, ln_b, ws, bst)


def sgu_bwd(z1, dus, ln_g, ln_b, ws, bst, *, name):
    L, W2 = z1.shape
    W = W2 // 2
    ng = W // LANES

    def body(z_ref, d_ref, g_ref, b_ref, ws_ref, bs_ref, dz_ref, dws_ref, dbs_ref, dg_ref, db_ref, dv_scr):
        @pl.when(pl.program_id(0) == 0)
        def _():
            dws_ref[...] = jnp.zeros_like(dws_ref)
            dbs_ref[...] = jnp.zeros_like(dbs_ref)
            dg_ref[...] = jnp.zeros_like(dg_ref)
            db_ref[...] = jnp.zeros_like(db_ref)

        zp = z_ref[...].astype(F32)
        z, th = _gelu_parts(zp)
        xhat, rstd = _layernorm(z[:, W:])
        vln = (xhat * g_ref[...] + b_ref[...]).astype(BF16)
        d = d_ref[...].astype(F32)
        lane = lax.broadcasted_iota(jnp.int32, (1, LANES), 1)
        dbs = jnp.zeros((BLK, LANES), F32)
        dgel = _gelu_grad(zp, th)
        for gi in range(ng):
            cs = slice(gi * LANES, (gi + 1) * LANES)
            wb = ws_ref[gi].astype(BF16)
            s = _dot(wb, vln[:, cs]) + bs_ref[:, gi:gi + 1]
            dz_ref[:, cs] = (d[:, cs] * s * dgel[:, cs]).astype(BF16)
            ds = d[:, cs] * z[:, cs]
            dbs = dbs + jnp.where(lane == gi, jnp.sum(ds, axis=-1, keepdims=True), 0.0)
            dsb = ds.astype(BF16)
            dws_ref[gi] += _dot_nt(dsb, vln[:, cs])
            dv_scr[:, cs] = _dot_tn(wb, dsb)
        dbs_ref[...] += dbs
        dvln = dv_scr[...]
        dg_ref[...] += _colsum(dvln * xhat)
        db_ref[...] += _colsum(dvln)
        dxh = dvln * g_ref[...]
        dv = rstd * (dxh - jnp.mean(dxh, axis=-1, keepdims=True) - xhat * jnp.mean(dxh * xhat, axis=-1, keepdims=True))
        dz_ref[:, W:] = (dv * dgel[:, W:]).astype(BF16)

    vec = _full((1, W))
    return pl.pallas_call(
        body, name=name, grid=(L // BLK,),
        in_specs=[pl.BlockSpec((BLK, W2), lambda n: (n, 0)), pl.BlockSpec((BLK, W), lambda n: (n, 0)), vec, vec,
                  _full((ng, LANES, LANES)), _full((BLK, ng))],
        out_specs=[pl.BlockSpec((BLK, W2), lambda n: (n, 0)), _full((ng, LANES, LANES)), _full((BLK, LANES)), vec, vec],
        out_shape=[jax.ShapeDtypeStruct((L, W2), BF16), jax.ShapeDtypeStruct((ng, LANES, LANES), F32),
                   jax.ShapeDtypeStruct((BLK, LANES), F32), jax.ShapeDtypeStruct((1, W), F32), jax.ShapeDtypeStruct((1, W), F32)],
        scratch_shapes=[pltpu.VMEM((BLK, W), F32)],
        compiler_params=_cp(("arbitrary",)),
    )(z1, dus, ln_g, ln_b, ws, bst)


def _adamw_math(w, m, v, g):
    m_ = ADAM_B1 * m + (1.0 - ADAM_B1) * g
    v_ = ADAM_B2 * v + (1.0 - ADAM_B2) * (g * g)
    return -ADAM_LR * ((m_ / BC1) / (jnp.sqrt(v_ / BC2) + ADAM_EPS) + ADAM_WD * w), m_, v_


def adamw(w, m, v, gparts, *, tr, name):
    NL, R, Wd = w.shape
    nr = R // tr

    def body(w_ref, m_ref, v_ref, *rest):
        gp_refs, (g_ref, d_ref, nm_ref, nv_ref) = rest[:NL], rest[NL:]
        for l in range(NL):
            @pl.when(pl.program_id(0) == l)
            def _():
                g = gp_refs[l][0].astype(F32)
                for s in range(1, gp_refs[l].shape[0]):
                    g = g + gp_refs[l][s].astype(F32)
                g_ref[...] = g
                d_ref[...], nm_ref[...], nv_ref[...] = _adamw_math(w_ref[...], m_ref[...], v_ref[...], g)

    row = pl.BlockSpec((None, tr, Wd), lambda l, i: (l, i, 0))
    gspecs = [pl.BlockSpec((gparts[l].shape[0], tr, Wd), (lambda l_, i, l=l: (0, jnp.clip(i + (l_ - l) * nr, 0, nr - 1), 0)))
              for l in range(NL)]
    return pl.pallas_call(
        body, name=name, grid=(NL, nr),
        in_specs=[row, row, row] + gspecs, out_specs=[row] * 4, out_shape=[jax.ShapeDtypeStruct((NL, R, Wd), F32)] * 4,
        compiler_params=_cp(("arbitrary", "arbitrary")),
    )(w, m, v, *gparts)


def small_update(gpacks, me, params, loss_row, *, name):
    n = len(params)

    def body(me_ref, gp_ref, *refs):
        ins, outs, gs_ref = refs[:3 * n], refs[3 * n:-1], refs[-1]
        gs_ref[...] = gp_ref[0].astype(F32)
        for dv in range(1, N_DEV):
            gs_ref[...] += gp_ref[dv].astype(F32)
        for p, (w, _, _, off, per_dev) in enumerate(params):
            w_ref, m_ref, v_ref = ins[3 * p:3 * p + 3]
            g_ref, d_ref, nm_ref, nv_ref = outs[4 * p:4 * p + 4]
            rows, cols = w.shape
            if cols == LANES and rows % 8 == 0 and not per_dev:
                g = gs_ref[off:off + rows, :]
                g_ref[...] = g
                d_ref[...], nm_ref[...], nv_ref[...] = _adamw_math(w_ref[...], m_ref[...], v_ref[...], g)
                continue
            chunks = -(-cols // LANES)
            base = off + me_ref[0] * per_dev if per_dev else off
            for i in range(rows):
                for j in range(chunks):
                    wd = min(LANES, cols - j * LANES)
                    at = (slice(i, i + 1), slice(j * LANES, j * LANES + wd))
                    g = gs_ref[pl.ds(base + i * chunks + j, 1), 0:wd]
                    g_ref[at] = g
                    d_ref[at], nm_ref[at], nv_ref[at] = _adamw_math(w_ref[at], m_ref[at], v_ref[at], g)
        outs[-1][...] = jnp.sum(gs_ref[loss_row:loss_row + 1, :], axis=1, keepdims=True)

    vm = pl.BlockSpec(memory_space=pltpu.VMEM)
    flat = [a for w, m, v, _, _ in params for a in (w, m, v)]
    out_shape = [jax.ShapeDtypeStruct(w.shape, F32) for w, _, _, _, _ in params for _ in range(4)] + [jax.ShapeDtypeStruct((1, 1), F32)]
    return pl.pallas_call(
        body, name=name, in_specs=[pl.BlockSpec(memory_space=pltpu.SMEM)] + [vm] * (1 + len(flat)),
        out_specs=[vm] * len(out_shape), out_shape=out_shape,
        scratch_shapes=[pltpu.VMEM(gpacks.shape[1:], F32)],
        compiler_params=pltpu.CompilerParams(vmem_limit_bytes=VMEM_LIMIT),
    )(me, gpacks, *flat)


def ada_fwd_mm(cs, w_ada, b_loc, *, name):
    R, D = cs.shape
    nl, _, n = w_ada.shape

    def body(c_ref, w_ref, b_ref, s_ref, m_ref):
        c = c_ref[...]
        s = c * jax.nn.sigmoid(c)
        s_ref[...] = s
        for i in range(nl):
            m_ref[i] = _dot(s.astype(BF16), w_ref[i].astype(BF16)) + b_ref[i:i + 1, :]

    return pl.pallas_call(
        body, name=name, in_specs=[_full((R, D)), _full((nl, D, n)), _full((nl, n))],
        out_specs=[_full((R, D)), _full((nl, R, n))], grid=(1,),
        out_shape=[jax.ShapeDtypeStruct((R, D), F32), jax.ShapeDtypeStruct((nl, R, n), F32)],
        compiler_params=_cp(("arbitrary",)),
    )(cs, w_ada, b_loc)


def ada_bwd_mm(s, c_ctx, dall, w_ada, *, name):
    R, D = s.shape
    nl, _, n = w_ada.shape

    def body(s_ref, cc_ref, d_ref, w_ref, gw_ref, dcc_ref):
        sb = s_ref[...].astype(BF16)
        row = lax.broadcasted_iota(jnp.int32, (R, 1), 0)
        dctx = d_ref[0, 1:2, :]
        for dv in range(1, N_DEV):
            dctx = dctx + d_ref[dv, 1:2, :]
        for i in range(nl):
            dm = jnp.zeros((R, n), F32)
            for dv in range(N_DEV):
                dm = dm + jnp.where(row == dv, d_ref[dv, 2 * i:2 * i + 1, :], 0.0)
            if i == 0:
                dm = dm + jnp.where(row == N_DEV, dctx, 0.0)
            gw_ref[i] = _dot_tn(sb, dm.astype(BF16))
        cc = cc_ref[...]
        sg = jax.nn.sigmoid(cc)
        ds = _dot_nt(jnp.broadcast_to(dctx, (8, n)).astype(BF16), w_ref[0].astype(BF16))
        dcc_ref[...] = ds * (sg * (1.0 + cc * (1.0 - sg)))

    return pl.pallas_call(
        body, name=name, grid=(1,),
        in_specs=[_full((R, D)), _full((1, D)), _full((N_DEV, 3, n)), _full((nl, D, n))],
        out_specs=[_full((nl, D, n)), _full((8, D))],
        out_shape=[jax.ShapeDtypeStruct((nl, D, n), F32), jax.ShapeDtypeStruct((8, D), F32)],
        compiler_params=_cp(("arbitrary",)),
    )(s, c_ctx, dall, w_ada)


def _place():
    x, y, c = lax.axis_index("x"), lax.axis_index("y"), lax.axis_index("c")
    return x, y, c


def _lin(p):
    return 4 * p[0] + 2 * p[1] + p[2]


def all_gather_small(xb, *, name):
    R, W = xb.shape

    def body(x_ref, out_ref, send_sems, recv_sems, local_sem):
        x, y, c = _place()
        me = _lin((x, y, c))
        mine = pltpu.make_async_copy(x_ref, out_ref.at[me], local_sem)
        mine.start()
        copies = []
        for k in range(1, N_DEV):
            peer = (x ^ (k >> 2), y ^ ((k >> 1) & 1), c ^ (k & 1))
            mk = lambda dst, k=k, peer=peer: pltpu.make_async_remote_copy(
                src_ref=x_ref, dst_ref=dst, send_sem=send_sems.at[k - 1], recv_sem=recv_sems.at[k - 1], device_id=peer, device_id_type=MESH)
            mk(out_ref.at[me]).start()
            copies.append(mk(out_ref.at[_lin(peer)]))
        for cp in copies:
            cp.wait_recv()
        for cp in copies:
            cp.wait_send()
        mine.wait()

    vm = pl.BlockSpec(memory_space=pltpu.VMEM)
    return pl.pallas_call(
        body, name=name, in_specs=[vm], out_specs=vm, out_shape=jax.ShapeDtypeStruct((N_DEV, R, W), xb.dtype),
        scratch_shapes=[pltpu.SemaphoreType.DMA((7,)), pltpu.SemaphoreType.DMA((7,)), pltpu.SemaphoreType.DMA],
        compiler_params=pltpu.CompilerParams(vmem_limit_bytes=VMEM_LIMIT),
    )(xb)


HBM_SPEC = pl.BlockSpec(memory_space=pltpu.HBM)
SEM_SPEC = pl.BlockSpec(memory_space=pltpu.SEMAPHORE)
ORDERED_EFFECT = pltpu.SideEffectType.DATAFLOW_SIDE_EFFECTING


def _exchange_copies(srcs, lands, sems, scatter):
    x, y, c = _place()
    me = _lin((x, y, c))
    for j in range(len(srcs)):
        r = lands[j].shape[0] // N_DEV
        block = lambda d, j=j, r=r: pl.ds(pl.multiple_of(d * r, 16), r)
        for k in range(1, N_DEV):
            peer = (x ^ (k >> 2), y ^ ((k >> 1) & 1), c ^ (k & 1))
            src = srcs[j].at[block(_lin(peer)), :] if scatter else srcs[j]
            mk = lambda dst, j=j, k=k, peer=peer, src=src: pltpu.make_async_remote_copy(
                src_ref=src, dst_ref=dst, send_sem=sems[2 * j].at[k - 1], recv_sem=sems[2 * j + 1].at[k - 1],
                device_id=peer, device_id_type=MESH)
            yield mk(lands[j].at[block(me), :]), mk(lands[j].at[block(_lin(peer)), :])


def exchange_start(srcs, lands, *, scatter, name):
    nw = len(srcs)

    def body(*refs):
        for start, _ in _exchange_copies(refs[:nw], refs[nw:2 * nw], refs[2 * nw:4 * nw], scatter):
            start.start()
        refs[-1][...] = jnp.zeros_like(refs[-1])

    thru = [pltpu.HBM(a.shape, a.dtype) for a in (*srcs, *lands)]
    res = pl.pallas_call(
        body, name=name, in_specs=[HBM_SPEC] * (2 * nw),
        out_specs=[SEM_SPEC] * (2 * nw) + [HBM_SPEC] * (2 * nw) + [pl.BlockSpec(memory_space=pltpu.VMEM)],
        out_shape=[pltpu.SemaphoreType.DMA((N_DEV - 1,))] * (2 * nw) + thru + [jax.ShapeDtypeStruct((8, LANES), F32)],
        input_output_aliases={i: 2 * nw + i for i in range(2 * nw)},
        compiler_params=pltpu.CompilerParams(has_side_effects=ORDERED_EFFECT),
    )(*[pltpu.with_memory_space_constraint(a, pltpu.HBM) for a in (*srcs, *lands)])
    return res[:2 * nw], res[2 * nw:3 * nw], res[3 * nw:4 * nw], res[-1]


def exchange_wait(srcs, lands, sems, after, *, scatter, name):
    nw = len(srcs)

    def body(*refs):
        for _, arrive in _exchange_copies(refs[:nw], refs[nw:2 * nw], refs[2 * nw:4 * nw], scatter):
            arrive.wait_send()
            arrive.wait_recv()

    res = pl.pallas_call(
        body, name=name, in_specs=[HBM_SPEC] * (2 * nw) + [SEM_SPEC] * (2 * nw) + [pl.BlockSpec(memory_space=pl.ANY)],
        out_specs=[HBM_SPEC] * (2 * nw), out_shape=[pltpu.HBM(a.shape, a.dtype) for a in (*srcs, *lands)],
        input_output_aliases={i: i for i in range(2 * nw)},
        compiler_params=pltpu.CompilerParams(has_side_effects=ORDERED_EFFECT),
    )(*srcs, *lands, *sems, after)
    return res[nw:]


def place_own(srcs, rows, me, *, scatter, name):
    nw = len(srcs)
    lands = [lax.empty((N_DEV * r, s_.shape[1]), s_.dtype) for r, s_ in zip(rows, srcs)]

    def body(me_ref, *refs):
        for j in range(nw):
            refs[2 * nw + j][...] = refs[j][...]

    mine = lambda i, me_ref: (me_ref[0], 0)
    src_at = mine if scatter else (lambda i, me_ref: (0, 0))
    blocks = [(r, s_.shape[1]) for r, s_ in zip(rows, srcs)]
    return pl.pallas_call(
        body, name=name,
        grid_spec=pltpu.PrefetchScalarGridSpec(
            num_scalar_prefetch=1, grid=(1,),
            in_specs=[pl.BlockSpec(b_, src_at) for b_ in blocks] + [pl.BlockSpec(memory_space=pl.ANY)] * nw,
            out_specs=[pl.BlockSpec(b_, mine) for b_ in blocks]),
        out_shape=[jax.ShapeDtypeStruct(l_.shape, l_.dtype) for l_ in lands],
        input_output_aliases={1 + nw + j: j for j in range(nw)},
        compiler_params=_cp(("arbitrary",)),
    )(jnp.reshape(me, (1,)).astype(jnp.int32), *srcs, *lands)


def _rope_tables(L):
    t = jnp.arange(L)
    inv = ROPE_BASE ** (-jnp.arange(ROPE_FREQS, dtype=F32) / ROPE_FREQS)
    ar = (t // GRID_W).astype(F32)[:, None] * inv
    ac = (t % GRID_W).astype(F32)[:, None] * inv
    z = jnp.zeros_like(ar)
    cos = jnp.concatenate([jnp.cos(ar), jnp.cos(ar), jnp.cos(ac), jnp.cos(ac)], axis=1)
    sa = jnp.concatenate([-jnp.sin(ar), z, -jnp.sin(ac), z], axis=1)
    sb = jnp.concatenate([z, jnp.sin(ar), z, jnp.sin(ac)], axis=1)
    return tuple(jnp.tile(a, (1, LANES // HEAD_DIM)) for a in (cos, sa, sb))


def _nat2d(a):
    return a.reshape(1, -1) if a.ndim == 1 else a.reshape(-1, a.shape[-1])


def _pack_rows(a):
    rows, cols = a.shape
    chunks = -(-cols // LANES)
    f = jnp.pad(a, ((0, 0), (0, chunks * LANES - cols))).reshape(rows * chunks, LANES)
    return jnp.pad(f, ((0, -f.shape[0] % 8), (0, 0)))


def _rows128(a):
    f = a.reshape(-1)
    n = -(-f.shape[0] // (8 * LANES)) * 8 * LANES
    return jnp.pad(f, (0, n - f.shape[0])).reshape(-1, LANES)


def kernel(x, c, ctx, c_ctx, w_ada, b_ada, g_mix_pre, g_mix_post, g_ffn_pre, g_ffn_post, w_in_even, w_pool, pool_scale, attn_sink, w_out_even, w_in_odd, sgu_ln_g, sgu_ln_b, sgu_w, sgu_b, w_out_odd, w_ffn_up, ffn_conv_w, ffn_conv_b, w_ffn_down, loss_target, m_c_ctx, m_w_ada, m_b_ada, m_g_mix_pre, m_g_mix_post, m_g_ffn_pre, m_g_ffn_post, m_w_in_even, m_w_pool, m_pool_scale, m_attn_sink, m_w_out_even, m_w_in_odd, m_sgu_ln_g, m_sgu_ln_b, m_sgu_w, m_sgu_b, m_w_out_odd, m_w_ffn_up, m_ffn_conv_w, m_ffn_conv_b, m_w_ffn_down, v_c_ctx, v_w_ada, v_b_ada, v_g_mix_pre, v_g_mix_post, v_g_ffn_pre, v_g_ffn_post, v_w_in_even, v_w_pool, v_pool_scale, v_attn_sink, v_w_out_even, v_w_in_odd, v_sgu_ln_g, v_sgu_ln_b, v_sgu_w, v_sgu_b, v_w_out_odd, v_w_ffn_up, v_ffn_conv_w, v_ffn_conv_b, v_w_ffn_down):
    P = dict(c_ctx=c_ctx, w_ada=w_ada, b_ada=b_ada, g_mix_pre=g_mix_pre, g_mix_post=g_mix_post, g_ffn_pre=g_ffn_pre,
             g_ffn_post=g_ffn_post, w_in_even=w_in_even, w_pool=w_pool, pool_scale=pool_scale, attn_sink=attn_sink,
             w_out_even=w_out_even, w_in_odd=w_in_odd, sgu_ln_g=sgu_ln_g, sgu_ln_b=sgu_ln_b, sgu_w=sgu_w, sgu_b=sgu_b,
             w_out_odd=w_out_odd, w_ffn_up=w_ffn_up, ffn_conv_w=ffn_conv_w, ffn_conv_b=ffn_conv_b, w_ffn_down=w_ffn_down)
    M = dict(c_ctx=m_c_ctx, w_ada=m_w_ada, b_ada=m_b_ada, g_mix_pre=m_g_mix_pre, g_mix_post=m_g_mix_post, g_ffn_pre=m_g_ffn_pre,
             g_ffn_post=m_g_ffn_post, w_in_even=m_w_in_even, w_pool=m_w_pool, pool_scale=m_pool_scale, attn_sink=m_attn_sink,
             w_out_even=m_w_out_even, w_in_odd=m_w_in_odd, sgu_ln_g=m_sgu_ln_g, sgu_ln_b=m_sgu_ln_b, sgu_w=m_sgu_w, sgu_b=m_sgu_b,
             w_out_odd=m_w_out_odd, w_ffn_up=m_w_ffn_up, ffn_conv_w=m_ffn_conv_w, ffn_conv_b=m_ffn_conv_b, w_ffn_down=m_w_ffn_down)
    V = dict(c_ctx=v_c_ctx, w_ada=v_w_ada, b_ada=v_b_ada, g_mix_pre=v_g_mix_pre, g_mix_post=v_g_mix_post, g_ffn_pre=v_g_ffn_pre,
             g_ffn_post=v_g_ffn_post, w_in_even=v_w_in_even, w_pool=v_w_pool, pool_scale=v_pool_scale, attn_sink=v_attn_sink,
             w_out_even=v_w_out_even, w_in_odd=v_w_in_odd, sgu_ln_g=v_sgu_ln_g, sgu_ln_b=v_sgu_ln_b, sgu_w=v_sgu_w, sgu_b=v_sgu_b,
             w_out_odd=v_w_out_odd, w_ffn_up=v_w_ffn_up, ffn_conv_w=v_ffn_conv_w, ffn_conv_b=v_ffn_conv_b, w_ffn_down=v_w_ffn_down)

    x = x[0]
    ctx = ctx[0]
    target = loss_target[0]
    L, D = x.shape
    C = ctx.shape[0]
    tm = min(512, L)
    tm_up = min(1024, L)
    conv_rows = min(256, L)
    me = 4 * lax.axis_index("x") + 2 * lax.axis_index("y") + lax.axis_index("c")
    n_ada = w_ada.shape[2]
    F = w_ffn_down.shape[1] * N_DEV
    half_f = F // 2

    n_cw = ffn_conv_w.shape[2]
    small = jnp.concatenate([_rows128(c), _rows128(sgu_ln_g), _rows128(sgu_ln_b), _rows128(ffn_conv_w)], axis=0)
    small_all = all_gather_small(small, name="gather_small_inputs")
    c_all = small_all[:, :8].reshape(N_DEV, D)
    ln_g = small_all[:, 8].reshape(1, D)
    ln_b = small_all[:, 16].reshape(1, D)
    conv_w = small_all[:, 24:].reshape(N_DEV, -1)[:, :2 * 3 * n_cw].reshape(N_DEV, 2, 3, n_cw)
    conv_w = conv_w.transpose(1, 2, 0, 3).reshape(2, 3, 2 * F)

    cs = jnp.concatenate([c_all, c_ctx[None, :], jnp.zeros((7, D), F32)], axis=0)
    b_loc = lax.dynamic_slice(b_ada, (0, me * n_ada), (2, n_ada))
    silu_c, mods_loc = ada_fwd_mm(cs, w_ada, b_loc, name="ada_fwd")
    mods_all = all_gather_small(mods_loc.reshape(-1, LANES), name="gather_mods")

    shards = [s.astype(BF16) for s in (w_in_even[0].T, w_out_even[0], w_ffn_up[0].T, w_ffn_down[0],
                                       w_in_odd[0].T, w_out_odd[0], w_ffn_up[1].T, w_ffn_down[1])]
    shards, mods_all = lax.optimization_barrier((shards, mods_all))
    w_sems, w_srcs, w_lands, _ = exchange_start(shards, place_own(shards, [s.shape[0] for s in shards], me, scatter=False, name="gather_own"),
                                              scatter=False, name="gather_start")

    def weight(j, after):
        return exchange_wait([w_srcs[j]], [w_lands[j]], w_sems[2 * j:2 * j + 2], after, scatter=False, name=f"gather_wait_{j}")[0]

    mods_all = mods_all.reshape(N_DEV, 2, 16, n_ada).transpose(1, 2, 0, 3).reshape(2, 16, 6 * D)
    mod = lambda i, row: [m_[None, :] for m_ in jnp.split(lax.dynamic_index_in_dim(mods_all[i], row, 0, False), 6)]
    sh_m, sc_m, gt_m, sh_f, sc_f, gt_f = zip(mod(0, me), mod(1, me))
    csh_m, csc_m = mod(0, N_DEV)[:2]

    row = lambda a, i: a[i][None, :]

    cos, sa, sb = _rope_tables(L)
    sink = attn_sink[0]
    bst = sgu_b[0].T
    wup, wdn = [None, None], [None, None]

    def ffn_fwd(i, xin):
        wup[i] = weight(2 + 4 * i, xin)
        h, hu = pre_mm(xin, row(g_ffn_pre, i), sh_f[i], sc_f[i], wup[i], tm=tm_up, tn=half_f, name=f"ffn_up_{i}")
        a, s1, s2 = conv_fwd(hu, conv_w[i], ffn_conv_b[i][None, :], rows=conv_rows, wblk=2 * LANES, name=f"ffn_conv_{i}")
        wdn[i] = weight(3 + 4 * i, a)
        res = mm_post([a], wdn[i], xin, row(g_ffn_post, i), gt_f[i], tm=tm, target=target if i == 1 else None, name=f"ffn_down_{i}")
        return (h, (hu, s1, s2), a, *res)

    first_mod, cos, sa, sb = lax.optimization_barrier((sh_m[0], cos, sa, sb))
    win_e = permute_heads(weight(0, first_mod))
    h0, u, q, kv = inproj_even(x, row(g_mix_pre, 0), sh_m[0], sc_m[0], win_e, cos, sa, sb, tm=tm, name="in_even")
    hc, kvc = pre_mm(ctx, row(g_mix_pre, 0), csh_m, csc_m, win_e, tm=C, tn=2 * LANES, w_row_off=8 * LANES, name="in_even_ctx")
    pa = [pool_fwd(u, w_pool[0], pool_scale, name="pool_fwd"), attn_fwd(q, kv, kvc, sink, name="attn_fwd")]
    wout_e = permute_heads(weight(1, pa[1]))
    y0, x1 = mm_post(pa, wout_e, x, row(g_mix_post, 0), gt_m[0], tm=tm, name="out_even")
    h1, hu0, a0, f0, x2 = ffn_fwd(0, x1)
    win_o = weight(4, x2)
    h2, z1 = pre_mm(x2, row(g_mix_pre, 1), sh_m[1], sc_m[1], win_o, tm=tm_up, tn=D, name="in_odd")
    us = sgu_fwd(z1, ln_g, ln_b, sgu_w[0], bst, name="sgu_fwd")
    wout_o = weight(5, us)
    y1, x3 = mm_post([us], wout_o, x2, row(g_mix_post, 1), gt_m[1], tm=tm, name="out_odd")
    h3, hu1, a1, f1, dx4, loss_part = ffn_fwd(1, x3)

    g_srcs, g_lands, g_sems = [], [], []

    def scatter(grads, nm):
        own = place_own(grads, [g.shape[0] // N_DEV for g in grads], me, scatter=True, name=nm.replace("start", "own"))
        sems, srcs, lands, tok = exchange_start(grads, own, scatter=True, name=nm)
        g_srcs.extend(srcs)
        g_lands.extend(lands)
        g_sems.extend(sems)
        return tok[0:1, 0:1]

    def ffn_bwd(i, dxo, xin, h, hu, a, f, g_post):
        dyf, da, dg_post, dgt = post_bwd_mm(dxo, f, g_post, gt_f[i], wdn[i], tm=tm, name=f"ffn_down_bwd_{i}")
        dhg, dhu, dcwg, dcwu, dcbg, dcbu = conv_bwd(da, hu[1], hu[2], hu[0], conv_w[i], rows=conv_rows, wblk=2 * LANES,
                                                    name=f"ffn_conv_bwd_{i}")
        dxin, dg_pre, dsh, dsc = mm_pre_bwd([dhg, dhu], wup[i], xin, dxo, row(g_ffn_pre, i), sc_f[i], tm=tm, tk=half_f,
                                            name=f"ffn_up_bwd_{i}")
        g_dn = wgrad([a], dyf, tr=2 * LANES, name=f"wgrad_down_{i}")
        g_up = wgrad([dhg, dhu], h, tr=2 * LANES, name=f"wgrad_up_{i}")
        tok = scatter([g_dn, g_up], f"scatter_start_ffn_{i}")
        return dxin, tok, dict(g_ffn_post=dg_post, g_ffn_pre=dg_pre, gt_f=dgt, sh_f=dsh, sc_f=dsc,
                               ffn_conv_w=jnp.concatenate([dcwg, dcwu], axis=1), ffn_conv_b=jnp.concatenate([dcbg, dcbu], axis=1)[0])

    dx3, tok, sf1 = ffn_bwd(1, dx4, x3, h3, hu1, a1, f1, row(g_ffn_post, 1))
    dy1, dus, dg_mpost1, dgt_m1 = post_bwd_mm(dx3, y1, row(g_mix_post, 1) + tok, gt_m[1], wout_o, tm=tm, name="out_odd_bwd")
    dz1, dws, dbs, dlng, dlnb = sgu_bwd(z1, dus, ln_g, ln_b, sgu_w[0], bst, name="sgu_bwd")
    dx2, dg_mpre1, dsh_m1, dsc_m1 = mm_pre_bwd([dz1], win_o, x2, dx3, row(g_mix_pre, 1), sc_m[1], tm=tm, tk=D, name="in_odd_bwd")
    tok = scatter([wgrad([us], dy1, tr=2 * LANES, name="wgrad_out_odd"), wgrad([dz1], h2, tr=2 * LANES, name="wgrad_in_odd")],
                  "scatter_start_mix_1")

    dx1, tok, sf0 = ffn_bwd(0, dx2, x1, h1, hu0, a0, f0, row(g_ffn_post, 0) + tok)
    dy0, dpa, dg_mpost0, dgt_m0 = post_bwd_mm(dx1, y0, row(g_mix_post, 0) + tok, gt_m[0], wout_e, tm=tm, name="out_even_bwd")
    du, dwp, dps = pool_bwd(u, dpa, w_pool[0], pool_scale, name="pool_bwd")
    dq, dkv, dkvc, dsink = attn_bwd(q, kv, kvc, sink, dpa, cos, sa, sb, name="attn_bwd")
    dz0 = jnp.concatenate([du, dq, dkv], axis=1)
    dzc = jnp.concatenate([jnp.zeros((C, 8 * LANES), BF16), dkvc], axis=1)
    tok = scatter([permute_heads(wgrad(pa, dy0, tr=2 * LANES, name="wgrad_out_even"), inverse=True),
                   permute_heads(wgrad([dz0], h0, tr=2 * LANES, extra=(dzc, hc), name="wgrad_in_even"), inverse=True)],
                  "scatter_start_mix_0")
    grad_x, dg_mpre0, dsh_m0, dsc_m0 = mm_pre_bwd([dz0], win_e, x, dx1, row(g_mix_pre, 0) + tok, sc_m[0], tm=tm, tk=dz0.shape[1],
                                                  name="in_even_bwd")
    _, dg_mpre0c, dcsh, dcsc = mm_pre_bwd([dkvc], win_e, ctx, None, row(g_mix_pre, 0), csc_m, tm=C, tk=2 * LANES,
                                          w_row_off=8 * LANES, name="in_even_ctx_bwd")

    out = {}

    zero = jnp.zeros((1, D), F32)
    dmod0 = jnp.concatenate([dsh_m0, dsc_m0, dgt_m0, sf0["sh_f"], sf0["sc_f"], sf0["gt_f"]], axis=1)
    dmodc = jnp.concatenate([dcsh, dcsc, zero, zero, zero, zero], axis=1)
    dmod1 = jnp.concatenate([dsh_m1, dsc_m1, dgt_m1, sf1["sh_f"], sf1["sc_f"], sf1["gt_f"]], axis=1)
    dmods = jnp.concatenate([dmod0, dmodc, dmod1], axis=0)
    dmods_all = all_gather_small(dmods.reshape(-1, LANES), name="gather_dmods").reshape(N_DEV, 3, N_DEV, n_ada)
    dall = lax.dynamic_index_in_dim(dmods_all, me, 2, False)
    g_w_ada, dcc = ada_bwd_mm(silu_c, c_ctx[None, :], dall, w_ada, name="ada_bwd")

    rep = dict(
        c_ctx=dcc[0:1],
        b_ada=jnp.concatenate([dmod0 + dmodc, dmod1]),
        g_mix_pre=jnp.concatenate([dg_mpre0 + dg_mpre0c, dg_mpre1]),
        g_mix_post=jnp.concatenate([dg_mpost0, dg_mpost1]),
        g_ffn_pre=jnp.concatenate([sf0["g_ffn_pre"], sf1["g_ffn_pre"]]),
        g_ffn_post=jnp.concatenate([sf0["g_ffn_post"], sf1["g_ffn_post"]]),
        w_pool=_nat2d(dwp), pool_scale=dps, attn_sink=dsink[:, :N_Q_HEADS],
        sgu_w=_nat2d(dws), sgu_b=dbs[:, :sgu_b.shape[1]].T,
        ffn_conv_b=jnp.stack([sf0["ffn_conv_b"], sf1["ffn_conv_b"]]),
    )
    hi = loss_part.astype(BF16).astype(F32)
    mid = (loss_part - hi).astype(BF16).astype(F32)
    loss_piece = jnp.pad(jnp.concatenate([hi, mid, loss_part - hi - mid], axis=1), ((0, 7), (0, LANES - 3)))
    conv_g = jnp.stack([sf0["ffn_conv_w"], sf1["ffn_conv_w"]]).reshape(2 * 3, N_DEV, n_cw).swapaxes(0, 1)
    shard_full = dict(sgu_ln_g=dlng.reshape(N_DEV, LANES), sgu_ln_b=dlnb.reshape(N_DEV, LANES),
                      ffn_conv_w=jnp.concatenate([_pack_rows(conv_g[d]) for d in range(N_DEV)], axis=0))
    small_names = list(rep) + list(shard_full)
    pieces = [_pack_rows(rep[k]) for k in rep] + list(shard_full.values()) + [loss_piece]
    sizes = [p.shape[0] for p in pieces]
    offs = [sum(sizes[:i]) for i in range(len(sizes))]
    pieces.append(jnp.zeros((-sum(sizes) % 16, LANES), F32))
    gpack = jnp.concatenate(pieces, axis=0).astype(BF16)
    own = place_own([gpack], [gpack.shape[0]], me, scatter=False, name="smallgrad_own")
    s_sems, s_srcs, s_lands, small_tok = exchange_start([gpack], own, scatter=False, name="smallgrad_start")

    slots = exchange_wait(g_srcs, g_lands, g_sems, small_tok, scatter=True, name="scatter_wait")

    def update(name, lands, transposed):
        w_, m_, v_ = (a.transpose(0, 2, 1) if transposed else a for a in (P[name], M[name], V[name]))
        r = w_.shape[1]
        tr = r // 4 if r % 64 == 0 and r > 256 else r
        res = adamw(w_, m_, v_, [l_.reshape(N_DEV, r, l_.shape[1]) for l_ in lands], tr=tr, name=f"adamw_{name}")
        for kind, val in zip(("grad", "delta", "new_m", "new_v"), res):
            out[(kind, name)] = val.transpose(0, 2, 1) if transposed else val

    update("w_in_even", [slots[7]], True)
    update("w_out_even", [slots[6]], False)
    update("w_in_odd", [slots[3]], True)
    update("w_out_odd", [slots[2]], False)
    update("w_ffn_up", [slots[5], slots[1]], True)
    update("w_ffn_down", [slots[4], slots[0]], False)
    res = adamw(w_ada, m_w_ada, v_w_ada, [g_w_ada[l][None] for l in range(w_ada.shape[0])], tr=D // 4, name="adamw_w_ada")
    for kind, val in zip(("grad", "delta", "new_m", "new_v"), res):
        out[(kind, "w_ada")] = val

    gpacks = exchange_wait(s_srcs, s_lands, s_sems, out[("new_v", "w_ada")], scatter=False, name="smallgrad_wait")[0]
    per_dev = {k: shard_full[k].shape[0] // N_DEV for k in shard_full}
    params = [(_nat2d(P[k]), _nat2d(M[k]), _nat2d(V[k]), offs[i], per_dev.get(k, 0)) for i, k in enumerate(small_names)]
    res = small_update(gpacks.reshape(N_DEV, -1, LANES), jnp.reshape(me, (1,)).astype(jnp.int32), params, offs[-1], name="adamw_small")
    for i, k in enumerate(small_names):
        for kind, val in zip(("grad", "delta", "new_m", "new_v"), res[4 * i:4 * i + 4]):
            out[(kind, k)] = val.reshape(P[k].shape)
    loss = res[-1][0, 0]

    names = list(P)
    final = [loss, grad_x[None]]
    for kind in ("grad", "delta", "new_m", "new_v"):
        for k in names:
            val = out[(kind, k)]
            final.append(val)
    return tuple(final)
```

```python
import functools
import math

import jax
import jax.numpy as jnp
from jax import lax
from jax.experimental import pallas as pl
from jax.experimental.pallas import tpu as pltpu

F32 = jnp.float32
BF16 = jnp.bfloat16
MESH = pl.DeviceIdType.MESH
N_DEV = 8
LANES = 128
VMEM_LIMIT = 48 * 1024 * 1024
EPS = 1e-6
NEG_INF = -1e30
GRID_W = 64
WINDOW = 128
BLK = 128
HEAD_DIM = 64
N_Q_HEADS = 8
N_KV_HEADS = 2
GQA = N_Q_HEADS // N_KV_HEADS
POOL_WINDOWS = (2, 4, 8, 16)
ROPE_BASE = 10000.0
ROPE_FREQS = HEAD_DIM // 4
PAD = 16
ADAM_LR, ADAM_B1, ADAM_B2, ADAM_EPS, ADAM_WD, ADAM_STEP = 0.001, 0.9, 0.999, 1e-08, 0.01, 10
BC1 = 1.0 - ADAM_B1 ** ADAM_STEP
BC2 = 1.0 - ADAM_B2 ** ADAM_STEP
SQRT_2_OVER_PI = math.sqrt(2.0 / math.pi)
GELU_C = 0.044715


def _cp(sem=None):
    return pltpu.CompilerParams(dimension_semantics=sem, vmem_limit_bytes=VMEM_LIMIT)


def _dot(a, b):
    return jnp.dot(a, b, preferred_element_type=F32)


def _dot_nt(a, b):
    return lax.dot_general(a, b, (((1,), (1,)), ((), ())), preferred_element_type=F32)


def _dot_tn(a, b):
    return lax.dot_general(a, b, (((0,), (0,)), ((), ())), preferred_element_type=F32)


def _rms(x):
    r = lax.rsqrt(jnp.mean(x * x, axis=-1, keepdims=True) + EPS)
    return x * r, r


def _rms_bwd(dn, n, r):
    return r * (dn - n * jnp.mean(dn * n, axis=-1, keepdims=True))


def _colsum(a):
    return jnp.sum(a, axis=0, keepdims=True)


def _rope(x, c, sa, sb):
    return x * c + pltpu.roll(x, LANES - ROPE_FREQS, 1) * sa + pltpu.roll(x, ROPE_FREQS, 1) * sb


def _full(shape):
    return pl.BlockSpec(shape, lambda *_: (0,) * len(shape))


def pre_mm(x, g, sh, sc, wt, *, tm, tn, w_row_off=0, name):
    T, D = x.shape
    n_rows = wt.shape[0] - w_row_off
    off = w_row_off // tn

    def body(x_ref, g_ref, sh_ref, sc_ref, w_ref, h_ref, z_ref):
        @pl.when(pl.program_id(1) == 0)
        def _():
            n, _ = _rms(x_ref[...])
            h_ref[...] = (n * g_ref[...] * (1.0 + sc_ref[...]) + sh_ref[...]).astype(BF16)

        z_ref[...] = _dot_nt(h_ref[...], w_ref[...]).astype(BF16)

    vec = pl.BlockSpec((1, D), lambda i, j: (0, 0))
    return pl.pallas_call(
        body, name=name, grid=(T // tm, n_rows // tn),
        in_specs=[pl.BlockSpec((tm, D), lambda i, j: (i, 0)), vec, vec, vec, pl.BlockSpec((tn, D), lambda i, j: (j + off, 0))],
        out_specs=[pl.BlockSpec((tm, D), lambda i, j: (i, 0)), pl.BlockSpec((tm, tn), lambda i, j: (i, j))],
        out_shape=[jax.ShapeDtypeStruct((T, D), BF16), jax.ShapeDtypeStruct((T, n_rows), BF16)],
        compiler_params=_cp(("parallel", "arbitrary")),
    )(x, g, sh, sc, wt)


def inproj_even(x, g, sh, sc, wt, cos, sa, sb, *, tm, name):
    T, D = x.shape
    N = wt.shape[0]

    def body(x_ref, g_ref, sh_ref, sc_ref, w_ref, c_ref, sa_ref, sb_ref, h_ref, u_ref, q_ref, kv_ref):
        n, _ = _rms(x_ref[...])
        h = (n * g_ref[...] * (1.0 + sc_ref[...]) + sh_ref[...]).astype(BF16)
        h_ref[...] = h
        z = _dot_nt(h, w_ref[...])
        u_ref[...] = z[:, :4 * LANES]
        c, a, b = c_ref[...], sa_ref[...], sb_ref[...]
        for s in range(4):
            q_ref[:, s * LANES:(s + 1) * LANES] = _rope(z[:, (4 + s) * LANES:(5 + s) * LANES], c, a, b).astype(BF16)
        kv_ref[:, :LANES] = _rope(z[:, 8 * LANES:9 * LANES], c, a, b).astype(BF16)
        kv_ref[:, LANES:] = z[:, 9 * LANES:].astype(BF16)

    vec = pl.BlockSpec((1, D), lambda i: (0, 0))
    row = lambda w: pl.BlockSpec((tm, w), lambda i: (i, 0))
    return pl.pallas_call(
        body, name=name, grid=(T // tm,),
        in_specs=[row(D), vec, vec, vec, _full((N, D)), row(LANES), row(LANES), row(LANES)],
        out_specs=[row(D), row(4 * LANES), row(4 * LANES), row(2 * LANES)],
        out_shape=[jax.ShapeDtypeStruct((T, D), BF16), jax.ShapeDtypeStruct((T, 4 * LANES), F32),
                   jax.ShapeDtypeStruct((T, 4 * LANES), BF16), jax.ShapeDtypeStruct((T, 2 * LANES), BF16)],
        compiler_params=_cp(("parallel",)),
    )(x, g, sh, sc, wt, cos, sa, sb)


def mm_post(a_parts, w, x, g, gt, *, tm, target=None, name):
    T = a_parts[0].shape[0]
    D = w.shape[1]
    npart = len(a_parts)
    offs = [sum(a_.shape[1] for a_ in a_parts[:p]) for p in range(npart + 1)]
    with_loss = target is not None

    def body(*refs):
        a_refs, (w_ref, x_ref, g_ref, gt_ref) = refs[:npart], refs[npart:npart + 4]
        y = _dot(a_refs[0][...], w_ref[offs[0]:offs[1], :])
        for p in range(1, npart):
            y = y + _dot(a_refs[p][...], w_ref[offs[p]:offs[p + 1], :])
        n, _ = _rms(y)
        xn = x_ref[...] + gt_ref[...] * (n * g_ref[...])
        if not with_loss:
            y_ref, xn_ref = refs[npart + 4:]
            y_ref[...] = y.astype(BF16)
            xn_ref[...] = xn
            return
        t_ref, y_ref, d_ref, l_ref = refs[npart + 4:]
        y_ref[...] = y.astype(BF16)

        @pl.when(pl.program_id(0) == 0)
        def _():
            l_ref[...] = jnp.zeros_like(l_ref)

        e = xn - t_ref[...]
        l_ref[...] += 0.5 * jnp.sum(jnp.mean(e * e, axis=-1, keepdims=True), axis=0, keepdims=True)
        d_ref[...] = e * (1.0 / D)

    vec = pl.BlockSpec((1, D), lambda i: (0, 0))
    row = lambda w_: pl.BlockSpec((tm, w_), lambda i: (i, 0))
    in_specs = [row(a_.shape[1]) for a_ in a_parts] + [_full(w.shape), row(D), vec, vec]
    out_specs = [row(D), row(D)]
    out_shape = [jax.ShapeDtypeStruct((T, D), BF16), jax.ShapeDtypeStruct((T, D), F32)]
    if with_loss:
        in_specs.append(row(D))
        out_specs.append(_full((1, 1)))
        out_shape.append(jax.ShapeDtypeStruct((1, 1), F32))
    return pl.pallas_call(
        body, name=name, grid=(T // tm,), in_specs=in_specs, out_specs=out_specs, out_shape=out_shape,
        compiler_params=_cp(("arbitrary",) if with_loss else ("parallel",)),
    )(*a_parts, w, x, g, gt, *((target,) if with_loss else ()))


def post_bwd_mm(dxn, y, g, gt, w, *, tm, name):
    T, D = y.shape
    K = w.shape[0]

    def body(dxn_ref, y_ref, g_ref, gt_ref, w_ref, dy_ref, da_ref, dg_ref, dgt_ref):
        @pl.when(pl.program_id(0) == 0)
        def _():
            dg_ref[...] = jnp.zeros_like(dg_ref)
            dgt_ref[...] = jnp.zeros_like(dgt_ref)

        d = dxn_ref[...]
        n, r = _rms(y_ref[...].astype(F32))
        g_, gt_ = g_ref[...], gt_ref[...]
        dg_ref[...] += _colsum(d * gt_ * n)
        dgt_ref[...] += _colsum(d * g_ * n)
        dy = _rms_bwd(d * (gt_ * g_), n, r).astype(BF16)
        dy_ref[...] = dy
        da_ref[...] = _dot_nt(dy, w_ref[...]).astype(BF16)

    vec = pl.BlockSpec((1, D), lambda i: (0, 0))
    row = lambda w_: pl.BlockSpec((tm, w_), lambda i: (i, 0))
    return pl.pallas_call(
        body, name=name, grid=(T // tm,),
        in_specs=[row(D), row(D), vec, vec, _full((K, D))],
        out_specs=[row(D), row(K), vec, vec],
        out_shape=[jax.ShapeDtypeStruct((T, D), BF16), jax.ShapeDtypeStruct((T, K), BF16),
                   jax.ShapeDtypeStruct((1, D), F32), jax.ShapeDtypeStruct((1, D), F32)],
        compiler_params=_cp(("arbitrary",)),
    )(dxn, y, g, gt, w)


def mm_pre_bwd(dzs, wt, x, dres, g, sc, *, tm, tk, w_row_off=0, name):
    T, N = dzs[0].shape
    D = x.shape[1]
    nk = N // tk
    npart = len(dzs)
    off = w_row_off // tk
    has_res = dres is not None

    def body(*refs):
        dz_refs = refs[:npart]
        w_refs = refs[npart:2 * npart]
        rest = refs[2 * npart:]
        x_ref = rest[0]
        dres_ref = rest[1] if has_res else None
        g_ref, sc_ref, dx_ref, dg_ref, dsh_ref, dsc_ref, acc = rest[1 + has_res:]
        i, k = pl.program_id(0), pl.program_id(1)

        @pl.when(jnp.logical_and(i == 0, k == 0))
        def _():
            dg_ref[...] = jnp.zeros_like(dg_ref)
            dsh_ref[...] = jnp.zeros_like(dsh_ref)
            dsc_ref[...] = jnp.zeros_like(dsc_ref)

        part = _dot(dz_refs[0][...], w_refs[0][...])
        for p in range(1, npart):
            part = part + _dot(dz_refs[p][...], w_refs[p][...])

        @pl.when(k == 0)
        def _():
            acc[...] = part

        @pl.when(k > 0)
        def _():
            acc[...] += part

        @pl.when(k == nk - 1)
        def _():
            dh = acc[...]
            n, r = _rms(x_ref[...])
            g_, s1 = g_ref[...], 1.0 + sc_ref[...]
            dsh_ref[...] += _colsum(dh)
            dsc_ref[...] += _colsum(dh * n * g_)
            dg_ref[...] += _colsum(dh * s1 * n)
            dxp = _rms_bwd(dh * (g_ * s1), n, r)
            dx_ref[...] = dxp + dres_ref[...] if has_res else dxp

    vec = pl.BlockSpec((1, D), lambda i, k: (0, 0))
    row = pl.BlockSpec((tm, D), lambda i, k: (i, 0))
    w_specs = [pl.BlockSpec((tk, D), (lambda i, k, p=p: (k + off + p * nk, 0))) for p in range(npart)]
    res_specs, res_args = ([row], (dres,)) if has_res else ([], ())
    return pl.pallas_call(
        body, name=name, grid=(T // tm, nk),
        in_specs=[pl.BlockSpec((tm, tk), lambda i, k: (i, k))] * npart + w_specs + [row] + res_specs + [vec, vec],
        out_specs=[row, vec, vec, vec],
        out_shape=[jax.ShapeDtypeStruct((T, D), F32)] + [jax.ShapeDtypeStruct((1, D), F32)] * 3,
        scratch_shapes=[pltpu.VMEM((tm, D), F32)],
        compiler_params=_cp(("arbitrary", "arbitrary")),
    )(*dzs, *([wt] * npart), x, *res_args, g, sc)


def wgrad(a_parts, b, *, tr, extra=None, name):
    T, R = a_parts[0].shape
    D = b.shape[1]
    npart = len(a_parts)
    nr = R // tr

    def body(*refs):
        a_refs, b_ref = refs[:npart], refs[npart]
        g_ref = refs[-1]
        for p in range(npart):
            @pl.when(pl.program_id(0) // nr == p)
            def _():
                acc = _dot_tn(a_refs[p][...], b_ref[...])
                if extra is not None:
                    acc += _dot_tn(refs[npart + 1][...], refs[npart + 2][...])
                g_ref[...] = acc.astype(BF16)

    in_specs = [pl.BlockSpec((T, tr), (lambda r, p=p: (0, jnp.clip(r - p * nr, 0, nr - 1)))) for p in range(npart)]
    in_specs.append(_full((T, D)))
    args = [*a_parts, b]
    if extra is not None:
        a2, b2 = extra
        in_specs += [pl.BlockSpec((a2.shape[0], tr), lambda r: (0, r)), _full(b2.shape)]
        args += [a2, b2]
    return pl.pallas_call(
        body, name=name, grid=(npart * nr,),
        in_specs=in_specs, out_specs=pl.BlockSpec((tr, D), lambda r: (r, 0)),
        out_shape=jax.ShapeDtypeStruct((npart * R, D), BF16),
        compiler_params=_cp(("parallel",)),
    )(*args)


def _conv_ext(ref, r0, rows, total):
    top = ref[pl.ds(pl.multiple_of(jnp.maximum(r0 - PAD, 0), PAD), PAD), :]
    mid = ref[pl.ds(r0, rows), :]
    bot = ref[pl.ds(pl.multiple_of(jnp.minimum(r0 + rows, total - PAD), PAD), PAD), :]
    top = jnp.where(r0 > 0, top, jnp.zeros_like(top))
    bot = jnp.where(r0 + rows < total, bot, jnp.zeros_like(bot))
    return jnp.concatenate([top, mid, bot], axis=0).astype(F32)


def _shift_rows(a, k):
    return pltpu.roll(a, k % a.shape[0], 0)


def _conv3(x, w, b):
    return w[0:1] * _shift_rows(x, 1) + w[1:2] * x + w[2:3] * _shift_rows(x, -1) + b


def _gate_up_specs(rows_, wblk, nb):
    return [pl.BlockSpec((rows_, wblk), lambda j: (0, j)), pl.BlockSpec((rows_, wblk), lambda j: (0, j + nb))]


def conv_fwd(hu, cw, cb, *, rows, wblk, name):
    L, N2 = hu.shape
    nb = N2 // 2 // wblk
    nchunk = L // rows

    def body(hg_ref, hu_ref, wg_ref, wu_ref, bg_ref, bu_ref, a_ref, s1_ref, s2_ref):
        def chunk(ci, carry):
            r0 = pl.multiple_of(ci * rows, rows)
            gate = _conv3(_conv_ext(hg_ref, r0, rows, L), wg_ref[...], bg_ref[...])[PAD:PAD + rows]
            up = _conv3(_conv_ext(hu_ref, r0, rows, L), wu_ref[...], bu_ref[...])[PAD:PAD + rows]
            sg = jax.nn.sigmoid(gate)
            silu = gate * sg
            at = pl.ds(r0, rows)
            a_ref[at, :] = (silu * up).astype(BF16)
            s1_ref[at, :] = silu.astype(BF16)
            s2_ref[at, :] = (up * (sg + silu * (1.0 - sg))).astype(BF16)
            return carry

        lax.fori_loop(0, nchunk, chunk, 0)

    out = pl.BlockSpec((L, wblk), lambda j: (0, j))
    return pl.pallas_call(
        body, name=name, grid=(nb,),
        in_specs=_gate_up_specs(L, wblk, nb) + _gate_up_specs(3, wblk, nb) + _gate_up_specs(1, wblk, nb),
        out_specs=[out] * 3, out_shape=[jax.ShapeDtypeStruct((L, N2 // 2), BF16)] * 3,
        compiler_params=_cp(("parallel",)),
    )(hu, hu, cw, cw, cb, cb)


def conv_bwd(da, s1, s2, hu, cw, *, rows, wblk, name):
    L, N2 = hu.shape
    F = N2 // 2
    nb = F // wblk
    nchunk = L // rows
    mid = slice(PAD, PAD + rows)

    def body(da_ref, s1_ref, s2_ref, hg_ref, hu_ref, wg_ref, wu_ref, dg_ref, du_ref, dwg_ref, dwu_ref, dbg_ref, dbu_ref):
        for ref in (dwg_ref, dwu_ref, dbg_ref, dbu_ref):
            ref[...] = jnp.zeros_like(ref)

        def half_bwd(x_ref, dh, w_ref, dx_ref, dw_ref, db_ref, r0):
            w = w_ref[...]
            nxt, prv = _shift_rows(dh, -1)[mid], _shift_rows(dh, 1)[mid]
            dhm, xm = dh[mid], x_ref[pl.ds(r0, rows), :].astype(F32)
            dx_ref[pl.ds(r0, rows), :] = (w[0:1] * nxt + w[1:2] * dhm + w[2:3] * prv).astype(BF16)
            db_ref[...] += _colsum(dhm)
            dw_ref[0:1, :] += _colsum(nxt * xm)
            dw_ref[1:2, :] += _colsum(dhm * xm)
            dw_ref[2:3, :] += _colsum(prv * xm)

        def chunk(ci, carry):
            r0 = pl.multiple_of(ci * rows, rows)
            d = _conv_ext(da_ref, r0, rows, L)
            half_bwd(hu_ref, d * _conv_ext(s1_ref, r0, rows, L), wu_ref, du_ref, dwu_ref, dbu_ref, r0)
            half_bwd(hg_ref, d * _conv_ext(s2_ref, r0, rows, L), wg_ref, dg_ref, dwg_ref, dbg_ref, r0)
            return carry

        lax.fori_loop(0, nchunk, chunk, 0)

    blk = lambda r: pl.BlockSpec((r, wblk), lambda j: (0, j))
    return pl.pallas_call(
        body, name=name, grid=(nb,),
        in_specs=[blk(L)] * 3 + _gate_up_specs(L, wblk, nb) + _gate_up_specs(3, wblk, nb),
        out_specs=[blk(L), blk(L), blk(3), blk(3), blk(1), blk(1)],
        out_shape=[jax.ShapeDtypeStruct((L, F), BF16)] * 2 + [jax.ShapeDtypeStruct((3, F), F32)] * 2
        + [jax.ShapeDtypeStruct((1, F), F32)] * 2,
        compiler_params=_cp(("parallel",)),
    )(da, s1, s2, hu, hu, cw, cw)


def _window_sums(pad_ref, w, lead):
    a = pad_ref[...]
    k = 1
    while k < w:
        a = a + _shift_rows(a, -k)
        k *= 2
    return _shift_rows(a, lead) if lead else a


def _pool_counts(L, h):
    t = lax.broadcasted_iota(jnp.int32, (L, 1), 0)
    return (jnp.minimum(t + h, L) - jnp.maximum(t - h, 0)).astype(F32)


def _pooled(u_ref, pad_ref, L, w):
    h = w // 2
    pad_ref[pl.ds(PAD, L), :] = u_ref[...]
    win = _window_sums(pad_ref, w, h)[PAD:PAD + L]
    return win / _pool_counts(L, h) - u_ref[...]


def _zero_pad_edges(pad_ref, L):
    z = jnp.zeros((PAD, LANES), F32)
    pad_ref[pl.ds(0, PAD), :] = z
    pad_ref[pl.ds(PAD + L, PAD), :] = z


def pool_fwd(u, w_pool, pool_scale, *, name):
    L = u.shape[0]

    def body(u_ref, w_ref, ps_ref, p_ref, pad_ref):
        _zero_pad_edges(pad_ref, L)
        for gi, win in enumerate(POOL_WINDOWS):
            @pl.when(pl.program_id(0) == gi)
            def _():
                pooled = _pooled(u_ref, pad_ref, L, win)
                p_ref[...] = (_dot(pooled.astype(BF16), w_ref[...].astype(BF16)) * ps_ref[...]).astype(BF16)

    return pl.pallas_call(
        body, name=name, grid=(len(POOL_WINDOWS),),
        in_specs=[pl.BlockSpec((L, LANES), lambda gi: (0, gi)), pl.BlockSpec((None, LANES, LANES), lambda gi: (gi, 0, 0)),
                  pl.BlockSpec((1, LANES), lambda gi: (0, gi))],
        out_specs=pl.BlockSpec((L, LANES), lambda gi: (0, gi)),
        out_shape=jax.ShapeDtypeStruct((L, 4 * LANES), BF16),
        scratch_shapes=[pltpu.VMEM((L + 2 * PAD, LANES), F32)],
        compiler_params=_cp(("parallel",)),
    )(u, w_pool, pool_scale)


def pool_bwd(u, dpa, w_pool, pool_scale, *, name):
    L = u.shape[0]

    def body(u_ref, dp_ref, w_ref, ps_ref, du_ref, dw_ref, dps_ref, pad_ref):
        _zero_pad_edges(pad_ref, L)
        for gi, win in enumerate(POOL_WINDOWS):
            @pl.when(pl.program_id(0) == gi)
            def _():
                h = win // 2
                wb = w_ref[...].astype(BF16)
                pooled = _pooled(u_ref, pad_ref, L, win).astype(BF16)
                dp = dp_ref[...].astype(F32)
                dps_ref[...] = _colsum(dp * _dot(pooled, wb))
                dy = (dp * ps_ref[...]).astype(BF16)
                dw_ref[...] = _dot_tn(pooled, dy)
                dpooled = _dot_nt(dy, wb)
                pad_ref[pl.ds(PAD, L), :] = dpooled / _pool_counts(L, h)
                du_ref[...] = (_window_sums(pad_ref, win, h - 1)[PAD:PAD + L] - dpooled).astype(BF16)

    return pl.pallas_call(
        body, name=name, grid=(len(POOL_WINDOWS),),
        in_specs=[pl.BlockSpec((L, LANES), lambda gi: (0, gi)), pl.BlockSpec((L, LANES), lambda gi: (0, gi)),
                  pl.BlockSpec((None, LANES, LANES), lambda gi: (gi, 0, 0)), pl.BlockSpec((1, LANES), lambda gi: (0, gi))],
        out_specs=[pl.BlockSpec((L, LANES), lambda gi: (0, gi)), pl.BlockSpec((None, LANES, LANES), lambda gi: (gi, 0, 0)),
                   pl.BlockSpec((1, LANES), lambda gi: (0, gi))],
        out_shape=[jax.ShapeDtypeStruct((L, 4 * LANES), BF16), jax.ShapeDtypeStruct((4, LANES, LANES), F32),
                   jax.ShapeDtypeStruct((1, 4 * LANES), F32)],
        scratch_shapes=[pltpu.VMEM((L + 2 * PAD, LANES), F32)],
        compiler_params=_cp(("parallel",)),
    )(u, dpa, w_pool, pool_scale)


def _attn_probs(qk, band_k, ctx_k, sink_ref, kh, mask4):
    s_loc = jnp.where(mask4, _dot_nt(qk, band_k), NEG_INF)
    s_ctx = _dot_nt(qk, ctx_k)
    sk = jnp.concatenate([jnp.full((BLK, 1), sink_ref[kh * GQA + hh], F32) for hh in range(GQA)], axis=0)
    m = jnp.maximum(jnp.maximum(jnp.max(s_loc, axis=-1, keepdims=True), jnp.max(s_ctx, axis=-1, keepdims=True)), sk)
    e_loc, e_ctx, e_s = jnp.exp(s_loc - m), jnp.exp(s_ctx - m), jnp.exp(sk - m)
    inv = 1.0 / (jnp.sum(e_loc, axis=-1, keepdims=True) + jnp.sum(e_ctx, axis=-1, keepdims=True) + e_s)
    return e_loc * inv, e_ctx * inv, e_s * inv


def _attn_block(n, L):
    start = pl.multiple_of(jnp.clip((n - 1) * BLK, 0, L - 3 * BLK), BLK)
    qpos = n * BLK + lax.broadcasted_iota(jnp.int32, (BLK, 3 * BLK), 0)
    kpos = start + lax.broadcasted_iota(jnp.int32, (BLK, 3 * BLK), 1)
    mask = jnp.abs(kpos - qpos) <= WINDOW
    return start, jnp.concatenate([mask] * GQA, axis=0)


def _stack_slabs(ref):
    return jnp.concatenate([ref[:, s * LANES:(s + 1) * LANES] for s in range(GQA)], axis=0)


def _kv_head_lanes(kh):
    return (lax.broadcasted_iota(jnp.int32, (1, LANES), 1) // HEAD_DIM) == kh


def permute_heads(w, inverse=False):
    lo, hi = 4 * LANES, 8 * LANES
    mid = w[lo:hi].reshape(*((GQA, N_KV_HEADS) if inverse else (N_KV_HEADS, GQA)), HEAD_DIM, w.shape[1])
    return jnp.concatenate([w[:lo], mid.swapaxes(0, 1).reshape(hi - lo, w.shape[1]), w[hi:]], axis=0)


def attn_fwd(q, kv, kvc, sink, *, name):
    L = q.shape[0]
    C = kvc.shape[0]
    scale = HEAD_DIM ** -0.5

    def body(q_ref, kv_ref, kvc_ref, sink_ref, o_ref):
        start, mask4 = _attn_block(pl.program_id(0), L)
        band = kv_ref[pl.ds(start, 3 * BLK), :]
        kvc_ = kvc_ref[...]
        qs = _stack_slabs(q_ref) * scale
        o = jnp.zeros((GQA * BLK, LANES), F32)
        for kh in range(N_KV_HEADS):
            grp = _kv_head_lanes(kh)
            qk = jnp.where(grp, qs, jnp.zeros_like(qs))
            p_loc, p_ctx, _ = _attn_probs(qk, band[:, :LANES], kvc_[:, :LANES], sink_ref, kh, mask4)
            o = o + jnp.where(grp, _dot(p_loc.astype(BF16), band[:, LANES:]) + _dot(p_ctx.astype(BF16), kvc_[:, LANES:]), 0.0)
        for s in range(GQA):
            o_ref[:, s * LANES:(s + 1) * LANES] = o[s * BLK:(s + 1) * BLK].astype(BF16)

    return pl.pallas_call(
        body, name=name, grid=(L // BLK,),
        in_specs=[pl.BlockSpec((BLK, 4 * LANES), lambda n: (n, 0)), _full((L, 2 * LANES)), _full((C, 2 * LANES)),
                  pl.BlockSpec(memory_space=pltpu.SMEM)],
        out_specs=pl.BlockSpec((BLK, 4 * LANES), lambda n: (n, 0)),
        out_shape=jax.ShapeDtypeStruct((L, 4 * LANES), BF16),
        compiler_params=_cp(("parallel",)),
    )(q, kv, kvc, sink)


def attn_bwd(q, kv, kvc, sink, dpa, cos, sa, sb, *, name):
    L = q.shape[0]
    C = kvc.shape[0]
    nb = L // BLK
    scale = HEAD_DIM ** -0.5

    def body(q_ref, kv_ref, kvc_ref, sink_ref, do_ref, c_ref, sa_ref, sb_ref, cq_ref, saq_ref, sbq_ref,
             dq_ref, dkv_ref, dkvc_ref, dsink_ref, dkv_acc, dkvc_acc):
        n = pl.program_id(0)

        @pl.when(n == 0)
        def _():
            dkv_acc[...] = jnp.zeros_like(dkv_acc)
            dkvc_acc[...] = jnp.zeros_like(dkvc_acc)
            dsink_ref[...] = jnp.zeros_like(dsink_ref)

        start, mask4 = _attn_block(n, L)
        band = kv_ref[pl.ds(start, 3 * BLK), :]
        kvc_ = kvc_ref[...]
        band_k, band_v, ctx_k, ctx_v = band[:, :LANES], band[:, LANES:], kvc_[:, :LANES], kvc_[:, LANES:]
        qs = _stack_slabs(q_ref) * scale
        dos = _stack_slabs(do_ref)
        lane = lax.broadcasted_iota(jnp.int32, (1, LANES), 1)
        dsink = jnp.zeros((1, LANES), F32)
        dq = jnp.zeros((GQA * BLK, LANES), F32)
        dk = jnp.zeros((3 * BLK, LANES), F32)
        dv = jnp.zeros((3 * BLK, LANES), F32)
        dkc = jnp.zeros((C, LANES), F32)
        dvc = jnp.zeros((C, LANES), F32)
        for kh in range(N_KV_HEADS):
            grp = _kv_head_lanes(kh)
            qk = jnp.where(grp, qs, jnp.zeros_like(qs))
            dok = jnp.where(grp, dos, jnp.zeros_like(dos))
            p_loc, p_ctx, p_s = _attn_probs(qk, band_k, ctx_k, sink_ref, kh, mask4)
            dp_loc = _dot_nt(dok, band_v)
            dp_ctx = _dot_nt(dok, ctx_v)
            delta = jnp.sum(p_loc * dp_loc, axis=-1, keepdims=True) + jnp.sum(p_ctx * dp_ctx, axis=-1, keepdims=True)
            ds_loc = (p_loc * (dp_loc - delta)).astype(BF16)
            ds_ctx = (p_ctx * (dp_ctx - delta)).astype(BF16)
            dsk = p_s * delta
            for hh in range(GQA):
                dsink = dsink - jnp.where(lane == kh * GQA + hh, jnp.sum(dsk[hh * BLK:(hh + 1) * BLK], axis=0, keepdims=True), 0.0)
            dq = dq + jnp.where(grp, _dot(ds_loc, band_k) + _dot(ds_ctx, ctx_k), 0.0)
            dk = dk + _dot_tn(ds_loc, qk)
            dv = dv + _dot_tn(p_loc.astype(BF16), dok)
            dkc = dkc + _dot_tn(ds_ctx, qk)
            dvc = dvc + _dot_tn(p_ctx.astype(BF16), dok)
        dsink_ref[...] += dsink
        dkv_acc[pl.ds(start, 3 * BLK), :LANES] += dk
        dkv_acc[pl.ds(start, 3 * BLK), LANES:] += dv
        dkvc_acc[:, :LANES] += dkc
        dkvc_acc[:, LANES:] += dvc
        c, a, b = cq_ref[...], -saq_ref[...], -sbq_ref[...]
        for s in range(GQA):
            dq_ref[:, s * LANES:(s + 1) * LANES] = _rope(dq[s * BLK:(s + 1) * BLK] * scale, c, a, b).astype(BF16)

        @pl.when(n == nb - 1)
        def _():
            dkv_ref[:, :LANES] = _rope(dkv_acc[:, :LANES], c_ref[...], -sa_ref[...], -sb_ref[...]).astype(BF16)
            dkv_ref[:, LANES:] = dkv_acc[:, LANES:].astype(BF16)
            dkvc_ref[...] = dkvc_acc[...].astype(BF16)

    blk = lambda w: pl.BlockSpec((BLK, w), lambda n: (n, 0))
    return pl.pallas_call(
        body, name=name, grid=(nb,),
        in_specs=[blk(4 * LANES), _full((L, 2 * LANES)), _full((C, 2 * LANES)), pl.BlockSpec(memory_space=pltpu.SMEM),
                  pl.BlockSpec((BLK, 4 * LANES), lambda n: (n, 1)),
                  _full((L, LANES)), _full((L, LANES)), _full((L, LANES)), blk(LANES), blk(LANES), blk(LANES)],
        out_specs=[blk(4 * LANES), _full((L, 2 * LANES)), _full((C, 2 * LANES)), _full((1, LANES))],
        out_shape=[jax.ShapeDtypeStruct((L, 4 * LANES), BF16), jax.ShapeDtypeStruct((L, 2 * LANES), BF16),
                   jax.ShapeDtypeStruct((C, 2 * LANES), BF16), jax.ShapeDtypeStruct((1, LANES), F32)],
        scratch_shapes=[pltpu.VMEM((L, 2 * LANES), F32), pltpu.VMEM((C, 2 * LANES), F32)],
        compiler_params=_cp(("arbitrary",)),
    )(q, kv, kvc, sink, dpa, cos, sa, sb, cos, sa, sb)


def _gelu_parts(x):
    th = jnp.tanh(SQRT_2_OVER_PI * (x + GELU_C * x * x * x))
    return 0.5 * x * (1.0 + th), th


def _gelu_grad(x, th):
    return 0.5 * (1.0 + th) + 0.5 * x * (1.0 - th * th) * SQRT_2_OVER_PI * (1.0 + 3.0 * GELU_C * x * x)


def _layernorm(v):
    mu = jnp.mean(v, axis=-1, keepdims=True)
    vc = v - mu
    rstd = lax.rsqrt(jnp.mean(vc * vc, axis=-1, keepdims=True) + EPS)
    return vc * rstd, rstd


def sgu_fwd(z1, ln_g, ln_b, ws, bst, *, name):
    L, W2 = z1.shape
    W = W2 // 2
    ng = W // LANES

    def body(z_ref, g_ref, b_ref, ws_ref, bs_ref, o_ref):
        z, _ = _gelu_parts(z_ref[...].astype(F32))
        xhat, _ = _layernorm(z[:, W:])
        vln = (xhat * g_ref[...] + b_ref[...]).astype(BF16)
        for gi in range(ng):
            cs = slice(gi * LANES, (gi + 1) * LANES)
            s = _dot(ws_ref[gi].astype(BF16), vln[:, cs]) + bs_ref[:, gi:gi + 1]
            o_ref[:, cs] = (z[:, cs] * s).astype(BF16)

    vec = _full((1, W))
    return pl.pallas_call(
        body, name=name, grid=(L // BLK,),
        in_specs=[pl.BlockSpec((BLK, W2), lambda n: (n, 0)), vec, vec, _full((ng, LANES, LANES)), _full((BLK, ng))],
        out_specs=pl.BlockSpec((BLK, W), lambda n: (n, 0)),
        out_shape=jax.ShapeDtypeStruct((L, W), BF16),
        compiler_params=_cp(("parallel",)),
    )(z1, ln_g, ln_b, ws, bst)


def sgu_bwd(z1, dus, ln_g, ln_b, ws, bst, *, name):
    L, W2 = z1.shape
    W = W2 // 2
    ng = W // LANES

    def body(z_ref, d_ref, g_ref, b_ref, ws_ref, bs_ref, dz_ref, dws_ref, dbs_ref, dg_ref, db_ref, dv_scr):
        @pl.when(pl.program_id(0) == 0)
        def _():
            dws_ref[...] = jnp.zeros_like(dws_ref)
            dbs_ref[...] = jnp.zeros_like(dbs_ref)
            dg_ref[...] = jnp.zeros_like(dg_ref)
            db_ref[...] = jnp.zeros_like(db_ref)

        zp = z_ref[...].astype(F32)
        z, th = _gelu_parts(zp)
        xhat, rstd = _layernorm(z[:, W:])
        vln = (xhat * g_ref[...] + b_ref[...]).astype(BF16)
        d = d_ref[...].astype(F32)
        lane = lax.broadcasted_iota(jnp.int32, (1, LANES), 1)
        dbs = jnp.zeros((BLK, LANES), F32)
        dgel = _gelu_grad(zp, th)
        for gi in range(ng):
            cs = slice(gi * LANES, (gi + 1) * LANES)
            wb = ws_ref[gi].astype(BF16)
            s = _dot(wb, vln[:, cs]) + bs_ref[:, gi:gi + 1]
            dz_ref[:, cs] = (d[:, cs] * s * dgel[:, cs]).astype(BF16)
            ds = d[:, cs] * z[:, cs]
            dbs = dbs + jnp.where(lane == gi, jnp.sum(ds, axis=-1, keepdims=True), 0.0)
            dsb = ds.astype(BF16)
            dws_ref[gi] += _dot_nt(dsb, vln[:, cs])
            dv_scr[:, cs] = _dot_tn(wb, dsb)
        dbs_ref[...] += dbs
        dvln = dv_scr[...]
        dg_ref[...] += _colsum(dvln * xhat)
        db_ref[...] += _colsum(dvln)
        dxh = dvln * g_ref[...]
        dv = rstd * (dxh - jnp.mean(dxh, axis=-1, keepdims=True) - xhat * jnp.mean(dxh * xhat, axis=-1, keepdims=True))
        dz_ref[:, W:] = (dv * dgel[:, W:]).astype(BF16)

    vec = _full((1, W))
    return pl.pallas_call(
        body, name=name, grid=(L // BLK,),
        in_specs=[pl.BlockSpec((BLK, W2), lambda n: (n, 0)), pl.BlockSpec((BLK, W), lambda n: (n, 0)), vec, vec,
                  _full((ng, LANES, LANES)), _full((BLK, ng))],
        out_specs=[pl.BlockSpec((BLK, W2), lambda n: (n, 0)), _full((ng, LANES, LANES)), _full((BLK, LANES)), vec, vec],
        out_shape=[jax.ShapeDtypeStruct((L, W2), BF16), jax.ShapeDtypeStruct((ng, LANES, LANES), F32),
                   jax.ShapeDtypeStruct((BLK, LANES), F32), jax.ShapeDtypeStruct((1, W), F32), jax.ShapeDtypeStruct((1, W), F32)],
        scratch_shapes=[pltpu.VMEM((BLK, W), F32)],
        compiler_params=_cp(("arbitrary",)),
    )(z1, dus, ln_g, ln_b, ws, bst)


def _adamw_math(w, m, v, g):
    m_ = ADAM_B1 * m + (1.0 - ADAM_B1) * g
    v_ = ADAM_B2 * v + (1.0 - ADAM_B2) * (g * g)
    return -ADAM_LR * ((m_ / BC1) / (jnp.sqrt(v_ / BC2) + ADAM_EPS) + ADAM_WD * w), m_, v_


def adamw(w, m, v, gparts, *, tr, name):
    NL, R, Wd = w.shape
    nr = R // tr

    def body(w_ref, m_ref, v_ref, *rest):
        gp_refs, (g_ref, d_ref, nm_ref, nv_ref) = rest[:NL], rest[NL:]
        for l in range(NL):
            @pl.when(pl.program_id(0) == l)
            def _():
                g = gp_refs[l][0].astype(F32)
                for s in range(1, gp_refs[l].shape[0]):
                    g = g + gp_refs[l][s].astype(F32)
                g_ref[...] = g
                d_ref[...], nm_ref[...], nv_ref[...] = _adamw_math(w_ref[...], m_ref[...], v_ref[...], g)

    row = pl.BlockSpec((None, tr, Wd), lambda l, i: (l, i, 0))
    gspecs = [pl.BlockSpec((gparts[l].shape[0], tr, Wd), (lambda l_, i, l=l: (0, jnp.clip(i + (l_ - l) * nr, 0, nr - 1), 0)))
              for l in range(NL)]
    return pl.pallas_call(
        body, name=name, grid=(NL, nr),
        in_specs=[row, row, row] + gspecs, out_specs=[row] * 4, out_shape=[jax.ShapeDtypeStruct((NL, R, Wd), F32)] * 4,
        compiler_params=_cp(("arbitrary", "arbitrary")),
    )(w, m, v, *gparts)


def small_update(gpacks, me, params, loss_row, *, name):
    n = len(params)

    def body(me_ref, gp_ref, *refs):
        ins, outs, gs_ref = refs[:3 * n], refs[3 * n:-1], refs[-1]
        gs_ref[...] = gp_ref[0].astype(F32)
        for dv in range(1, N_DEV):
            gs_ref[...] += gp_ref[dv].astype(F32)
        for p, (w, _, _, off, per_dev) in enumerate(params):
            w_ref, m_ref, v_ref = ins[3 * p:3 * p + 3]
            g_ref, d_ref, nm_ref, nv_ref = outs[4 * p:4 * p + 4]
            rows, cols = w.shape
            if cols == LANES and rows % 8 == 0 and not per_dev:
                g = gs_ref[off:off + rows, :]
                g_ref[...] = g
                d_ref[...], nm_ref[...], nv_ref[...] = _adamw_math(w_ref[...], m_ref[...], v_ref[...], g)
                continue
            chunks = -(-cols // LANES)
            base = off + me_ref[0] * per_dev if per_dev else off
            for i in range(rows):
                for j in range(chunks):
                    wd = min(LANES, cols - j * LANES)
                    at = (slice(i, i + 1), slice(j * LANES, j * LANES + wd))
                    g = gs_ref[pl.ds(base + i * chunks + j, 1), 0:wd]
                    g_ref[at] = g
                    d_ref[at], nm_ref[at], nv_ref[at] = _adamw_math(w_ref[at], m_ref[at], v_ref[at], g)
        outs[-1][...] = jnp.sum(gs_ref[loss_row:loss_row + 1, :], axis=1, keepdims=True)

    vm = pl.BlockSpec(memory_space=pltpu.VMEM)
    flat = [a for w, m, v, _, _ in params for a in (w, m, v)]
    out_shape = [jax.ShapeDtypeStruct(w.shape, F32) for w, _, _, _, _ in params for _ in range(4)] + [jax.ShapeDtypeStruct((1, 1), F32)]
    return pl.pallas_call(
        body, name=name, in_specs=[pl.BlockSpec(memory_space=pltpu.SMEM)] + [vm] * (1 + len(flat)),
        out_specs=[vm] * len(out_shape), out_shape=out_shape,
        scratch_shapes=[pltpu.VMEM(gpacks.shape[1:], F32)],
        compiler_params=pltpu.CompilerParams(vmem_limit_bytes=VMEM_LIMIT),
    )(me, gpacks, *flat)


def ada_fwd_mm(cs, w_ada, b_loc, *, name):
    R, D = cs.shape
    nl, _, n = w_ada.shape

    def body(c_ref, w_ref, b_ref, s_ref, m_ref):
        c = c_ref[...]
        s = c * jax.nn.sigmoid(c)
        s_ref[...] = s
        for i in range(nl):
            m_ref[i] = _dot(s.astype(BF16), w_ref[i].astype(BF16)) + b_ref[i:i + 1, :]

    return pl.pallas_call(
        body, name=name, in_specs=[_full((R, D)), _full((nl, D, n)), _full((nl, n))],
        out_specs=[_full((R, D)), _full((nl, R, n))], grid=(1,),
        out_shape=[jax.ShapeDtypeStruct((R, D), F32), jax.ShapeDtypeStruct((nl, R, n), F32)],
        compiler_params=_cp(("arbitrary",)),
    )(cs, w_ada, b_loc)


def ada_bwd_mm(s, c_ctx, dall, w_ada, *, name):
    R, D = s.shape
    nl, _, n = w_ada.shape

    def body(s_ref, cc_ref, d_ref, w_ref, gw_ref, dcc_ref):
        sb = s_ref[...].astype(BF16)
        row = lax.broadcasted_iota(jnp.int32, (R, 1), 0)
        dctx = d_ref[0, 1:2, :]
        for dv in range(1, N_DEV):
            dctx = dctx + d_ref[dv, 1:2, :]
        for i in range(nl):
            dm = jnp.zeros((R, n), F32)
            for dv in range(N_DEV):
                dm = dm + jnp.where(row == dv, d_ref[dv, 2 * i:2 * i + 1, :], 0.0)
            if i == 0:
                dm = dm + jnp.where(row == N_DEV, dctx, 0.0)
            gw_ref[i] = _dot_tn(sb, dm.astype(BF16))
        cc = cc_ref[...]
        sg = jax.nn.sigmoid(cc)
        ds = _dot_nt(jnp.broadcast_to(dctx, (8, n)).astype(BF16), w_ref[0].astype(BF16))
        dcc_ref[...] = ds * (sg * (1.0 + cc * (1.0 - sg)))

    return pl.pallas_call(
        body, name=name, grid=(1,),
        in_specs=[_full((R, D)), _full((1, D)), _full((N_DEV, 3, n)), _full((nl, D, n))],
        out_specs=[_full((nl, D, n)), _full((8, D))],
        out_shape=[jax.ShapeDtypeStruct((nl, D, n), F32), jax.ShapeDtypeStruct((8, D), F32)],
        compiler_params=_cp(("arbitrary",)),
    )(s, c_ctx, dall, w_ada)


def _place():
    x, y, c = lax.axis_index("x"), lax.axis_index("y"), lax.axis_index("c")
    return x, y, c


def _lin(p):
    return 4 * p[0] + 2 * p[1] + p[2]


def all_gather_small(xb, *, name):
    R, W = xb.shape

    def body(x_ref, out_ref, send_sems, recv_sems, local_sem):
        x, y, c = _place()
        me = _lin((x, y, c))
        mine = pltpu.make_async_copy(x_ref, out_ref.at[me], local_sem)
        mine.start()
        copies = []
        for k in range(1, N_DEV):
            peer = (x ^ (k >> 2), y ^ ((k >> 1) & 1), c ^ (k & 1))
            mk = lambda dst, k=k, peer=peer: pltpu.make_async_remote_copy(
                src_ref=x_ref, dst_ref=dst, send_sem=send_sems.at[k - 1], recv_sem=recv_sems.at[k - 1], device_id=peer, device_id_type=MESH)
            mk(out_ref.at[me]).start()
            copies.append(mk(out_ref.at[_lin(peer)]))
        for cp in copies:
            cp.wait_recv()
        for cp in copies:
            cp.wait_send()
        mine.wait()

    vm = pl.BlockSpec(memory_space=pltpu.VMEM)
    return pl.pallas_call(
        body, name=name, in_specs=[vm], out_specs=vm, out_shape=jax.ShapeDtypeStruct((N_DEV, R, W), xb.dtype),
        scratch_shapes=[pltpu.SemaphoreType.DMA((7,)), pltpu.SemaphoreType.DMA((7,)), pltpu.SemaphoreType.DMA],
        compiler_params=pltpu.CompilerParams(vmem_limit_bytes=VMEM_LIMIT),
    )(xb)


HBM_SPEC = pl.BlockSpec(memory_space=pltpu.HBM)
SEM_SPEC = pl.BlockSpec(memory_space=pltpu.SEMAPHORE)
ORDERED_EFFECT = pltpu.SideEffectType.DATAFLOW_SIDE_EFFECTING


def _exchange_copies(srcs, lands, sems, scatter):
    x, y, c = _place()
    me = _lin((x, y, c))
    for j in range(len(srcs)):
        r = lands[j].shape[0] // N_DEV
        block = lambda d, j=j, r=r: pl.ds(pl.multiple_of(d * r, 16), r)
        for k in range(1, N_DEV):
            peer = (x ^ (k >> 2), y ^ ((k >> 1) & 1), c ^ (k & 1))
            src = srcs[j].at[block(_lin(peer)), :] if scatter else srcs[j]
            mk = lambda dst, j=j, k=k, peer=peer, src=src: pltpu.make_async_remote_copy(
                src_ref=src, dst_ref=dst, send_sem=sems[2 * j].at[k - 1], recv_sem=sems[2 * j + 1].at[k - 1],
                device_id=peer, device_id_type=MESH)
            yield mk(lands[j].at[block(me), :]), mk(lands[j].at[block(_lin(peer)), :])


def exchange_start(srcs, lands, *, scatter, name):
    nw = len(srcs)

    def body(*refs):
        for start, _ in _exchange_copies(refs[:nw], refs[nw:2 * nw], refs[2 * nw:4 * nw], scatter):
            start.start()
        refs[-1][...] = jnp.zeros_like(refs[-1])

    thru = [pltpu.HBM(a.shape, a.dtype) for a in (*srcs, *lands)]
    res = pl.pallas_call(
        body, name=name, in_specs=[HBM_SPEC] * (2 * nw),
        out_specs=[SEM_SPEC] * (2 * nw) + [HBM_SPEC] * (2 * nw) + [pl.BlockSpec(memory_space=pltpu.VMEM)],
        out_shape=[pltpu.SemaphoreType.DMA((N_DEV - 1,))] * (2 * nw) + thru + [jax.ShapeDtypeStruct((8, LANES), F32)],
        input_output_aliases={i: 2 * nw + i for i in range(2 * nw)},
        compiler_params=pltpu.CompilerParams(has_side_effects=ORDERED_EFFECT),
    )(*[pltpu.with_memory_space_constraint(a, pltpu.HBM) for a in (*srcs, *lands)])
    return res[:2 * nw], res[2 * nw:3 * nw], res[3 * nw:4 * nw], res[-1]


def exchange_wait(srcs, lands, sems, after, *, scatter, name):
    nw = len(srcs)

    def body(*refs):
        for _, arrive in _exchange_copies(refs[:nw], refs[nw:2 * nw], refs[2 * nw:4 * nw], scatter):
            arrive.wait_send()
            arrive.wait_recv()

    res = pl.pallas_call(
        body, name=name, in_specs=[HBM_SPEC] * (2 * nw) + [SEM_SPEC] * (2 * nw) + [pl.BlockSpec(memory_space=pl.ANY)],
        out_specs=[HBM_SPEC] * (2 * nw), out_shape=[pltpu.HBM(a.shape, a.dtype) for a in (*srcs, *lands)],
        input_output_aliases={i: i for i in range(2 * nw)},
        compiler_params=pltpu.CompilerParams(has_side_effects=ORDERED_EFFECT),
    )(*srcs, *lands, *sems, after)
    return res[nw:]


def place_own(srcs, rows, me, *, scatter, name):
    nw = len(srcs)
    lands = [lax.empty((N_DEV * r, s_.shape[1]), s_.dtype) for r, s_ in zip(rows, srcs)]

    def body(me_ref, *refs):
        for j in range(nw):
            refs[2 * nw + j][...] = refs[j][...]

    mine = lambda i, me_ref: (me_ref[0], 0)
    src_at = mine if scatter else (lambda i, me_ref: (0, 0))
    blocks = [(r, s_.shape[1]) for r, s_ in zip(rows, srcs)]
    return pl.pallas_call(
        body, name=name,
        grid_spec=pltpu.PrefetchScalarGridSpec(
            num_scalar_prefetch=1, grid=(1,),
            in_specs=[pl.BlockSpec(b_, src_at) for b_ in blocks] + [pl.BlockSpec(memory_space=pl.ANY)] * nw,
            out_specs=[pl.BlockSpec(b_, mine) for b_ in blocks]),
        out_shape=[jax.ShapeDtypeStruct(l_.shape, l_.dtype) for l_ in lands],
        input_output_aliases={1 + nw + j: j for j in range(nw)},
        compiler_params=_cp(("arbitrary",)),
    )(jnp.reshape(me, (1,)).astype(jnp.int32), *srcs, *lands)


def _rope_tables(L):
    t = jnp.arange(L)
    inv = ROPE_BASE ** (-jnp.arange(ROPE_FREQS, dtype=F32) / ROPE_FREQS)
    ar = (t // GRID_W).astype(F32)[:, None] * inv
    ac = (t % GRID_W).astype(F32)[:, None] * inv
    z = jnp.zeros_like(ar)
    cos = jnp.concatenate([jnp.cos(ar), jnp.cos(ar), jnp.cos(ac), jnp.cos(ac)], axis=1)
    sa = jnp.concatenate([-jnp.sin(ar), z, -jnp.sin(ac), z], axis=1)
    sb = jnp.concatenate([z, jnp.sin(ar), z, jnp.sin(ac)], axis=1)
    return tuple(jnp.tile(a, (1, LANES // HEAD_DIM)) for a in (cos, sa, sb))


def _nat2d(a):
    return a.reshape(1, -1) if a.ndim == 1 else a.reshape(-1, a.shape[-1])


def _pack_rows(a):
    rows, cols = a.shape
    chunks = -(-cols // LANES)
    f = jnp.pad(a, ((0, 0), (0, chunks * LANES - cols))).reshape(rows * chunks, LANES)
    return jnp.pad(f, ((0, -f.shape[0] % 8), (0, 0)))


def _rows128(a):
    f = a.reshape(-1)
    n = -(-f.shape[0] // (8 * LANES)) * 8 * LANES
    return jnp.pad(f, (0, n - f.shape[0])).reshape(-1, LANES)


def kernel(x, c, ctx, c_ctx, w_ada, b_ada, g_mix_pre, g_mix_post, g_ffn_pre, g_ffn_post, w_in_even, w_pool, pool_scale, attn_sink, w_out_even, w_in_odd, sgu_ln_g, sgu_ln_b, sgu_w, sgu_b, w_out_odd, w_ffn_up, ffn_conv_w, ffn_conv_b, w_ffn_down, loss_target, m_c_ctx, m_w_ada, m_b_ada, m_g_mix_pre, m_g_mix_post, m_g_ffn_pre, m_g_ffn_post, m_w_in_even, m_w_pool, m_pool_scale, m_attn_sink, m_w_out_even, m_w_in_odd, m_sgu_ln_g, m_sgu_ln_b, m_sgu_w, m_sgu_b, m_w_out_odd, m_w_ffn_up, m_ffn_conv_w, m_ffn_conv_b, m_w_ffn_down, v_c_ctx, v_w_ada, v_b_ada, v_g_mix_pre, v_g_mix_post, v_g_ffn_pre, v_g_ffn_post, v_w_in_even, v_w_pool, v_pool_scale, v_attn_sink, v_w_out_even, v_w_in_odd, v_sgu_ln_g, v_sgu_ln_b, v_sgu_w, v_sgu_b, v_w_out_odd, v_w_ffn_up, v_ffn_conv_w, v_ffn_conv_b, v_w_ffn_down):
    P = dict(c_ctx=c_ctx, w_ada=w_ada, b_ada=b_ada, g_mix_pre=g_mix_pre, g_mix_post=g_mix_post, g_ffn_pre=g_ffn_pre,
             g_ffn_post=g_ffn_post, w_in_even=w_in_even, w_pool=w_pool, pool_scale=pool_scale, attn_sink=attn_sink,
             w_out_even=w_out_even, w_in_odd=w_in_odd, sgu_ln_g=sgu_ln_g, sgu_ln_b=sgu_ln_b, sgu_w=sgu_w, sgu_b=sgu_b,
             w_out_odd=w_out_odd, w_ffn_up=w_ffn_up, ffn_conv_w=ffn_conv_w, ffn_conv_b=ffn_conv_b, w_ffn_down=w_ffn_down)
    M = dict(c_ctx=m_c_ctx, w_ada=m_w_ada, b_ada=m_b_ada, g_mix_pre=m_g_mix_pre, g_mix_post=m_g_mix_post, g_ffn_pre=m_g_ffn_pre,
             g_ffn_post=m_g_ffn_post, w_in_even=m_w_in_even, w_pool=m_w_pool, pool_scale=m_pool_scale, attn_sink=m_attn_sink,
             w_out_even=m_w_out_even, w_in_odd=m_w_in_odd, sgu_ln_g=m_sgu_ln_g, sgu_ln_b=m_sgu_ln_b, sgu_w=m_sgu_w, sgu_b=m_sgu_b,
             w_out_odd=m_w_out_odd, w_ffn_up=m_w_ffn_up, ffn_conv_w=m_ffn_conv_w, ffn_conv_b=m_ffn_conv_b, w_ffn_down=m_w_ffn_down)
    V = dict(c_ctx=v_c_ctx, w_ada=v_w_ada, b_ada=v_b_ada, g_mix_pre=v_g_mix_pre, g_mix_post=v_g_mix_post, g_ffn_pre=v_g_ffn_pre,
             g_ffn_post=v_g_ffn_post, w_in_even=v_w_in_even, w_pool=v_w_pool, pool_scale=v_pool_scale, attn_sink=v_attn_sink,
             w_out_even=v_w_out_even, w_in_odd=v_w_in_odd, sgu_ln_g=v_sgu_ln_g, sgu_ln_b=v_sgu_ln_b, sgu_w=v_sgu_w, sgu_b=v_sgu_b,
             w_out_odd=v_w_out_odd, w_ffn_up=v_w_ffn_up, ffn_conv_w=v_ffn_conv_w, ffn_conv_b=v_ffn_conv_b, w_ffn_down=v_w_ffn_down)

    x = x[0]
    ctx = ctx[0]
    target = loss_target[0]
    L, D = x.shape
    C = ctx.shape[0]
    tm = min(512, L)
    tm_up = min(1024, L)
    conv_rows = min(512, L)
    me = 4 * lax.axis_index("x") + 2 * lax.axis_index("y") + lax.axis_index("c")
    n_ada = w_ada.shape[2]
    F = w_ffn_down.shape[1] * N_DEV
    half_f = F // 2

    n_cw = ffn_conv_w.shape[2]
    small = jnp.concatenate([_rows128(c), _rows128(sgu_ln_g), _rows128(sgu_ln_b), _rows128(ffn_conv_w)], axis=0)
    small_all = all_gather_small(small, name="gather_small_inputs")
    c_all = small_all[:, :8].reshape(N_DEV, D)
    ln_g = small_all[:, 8].reshape(1, D)
    ln_b = small_all[:, 16].reshape(1, D)
    conv_w = small_all[:, 24:].reshape(N_DEV, -1)[:, :2 * 3 * n_cw].reshape(N_DEV, 2, 3, n_cw)
    conv_w = conv_w.transpose(1, 2, 0, 3).reshape(2, 3, 2 * F)

    cs = jnp.concatenate([c_all, c_ctx[None, :], jnp.zeros((7, D), F32)], axis=0)
    b_loc = lax.dynamic_slice(b_ada, (0, me * n_ada), (2, n_ada))
    silu_c, mods_loc = ada_fwd_mm(cs, w_ada, b_loc, name="ada_fwd")
    mods_all = all_gather_small(mods_loc.reshape(-1, LANES), name="gather_mods")

    shards = [s.astype(BF16) for s in (w_in_even[0].T, w_out_even[0], w_ffn_up[0].T, w_ffn_down[0],
                                       w_in_odd[0].T, w_out_odd[0], w_ffn_up[1].T, w_ffn_down[1])]
    shards, mods_all = lax.optimization_barrier((shards, mods_all))
    w_sems, w_srcs, w_lands, _ = exchange_start(shards, place_own(shards, [s.shape[0] for s in shards], me, scatter=False, name="gather_own"),
                                              scatter=False, name="gather_start")

    def weight(j, after):
        return exchange_wait([w_srcs[j]], [w_lands[j]], w_sems[2 * j:2 * j + 2], after, scatter=False, name=f"gather_wait_{j}")[0]

    mods_all = mods_all.reshape(N_DEV, 2, 16, n_ada).transpose(1, 2, 0, 3).reshape(2, 16, 6 * D)
    mod = lambda i, row: [m_[None, :] for m_ in jnp.split(lax.dynamic_index_in_dim(mods_all[i], row, 0, False), 6)]
    sh_m, sc_m, gt_m, sh_f, sc_f, gt_f = zip(mod(0, me), mod(1, me))
    csh_m, csc_m = mod(0, N_DEV)[:2]

    row = lambda a, i: a[i][None, :]

    cos, sa, sb = _rope_tables(L)
    sink = attn_sink[0]
    bst = sgu_b[0].T
    wup, wdn = [None, None], [None, None]

    def ffn_fwd(i, xin):
        wup[i] = weight(2 + 4 * i, xin)
        h, hu = pre_mm(xin, row(g_ffn_pre, i), sh_f[i], sc_f[i], wup[i], tm=tm_up, tn=half_f, name=f"ffn_up_{i}")
        a, s1, s2 = conv_fwd(hu, conv_w[i], ffn_conv_b[i][None, :], rows=conv_rows, wblk=2 * LANES, name=f"ffn_conv_{i}")
        wdn[i] = weight(3 + 4 * i, a)
        res = mm_post([a], wdn[i], xin, row(g_ffn_post, i), gt_f[i], tm=tm, target=target if i == 1 else None, name=f"ffn_down_{i}")
        return (h, (hu, s1, s2), a, *res)

    first_mod, cos, sa, sb = lax.optimization_barrier((sh_m[0], cos, sa, sb))
    win_e = permute_heads(weight(0, first_mod))
    h0, u, q, kv = inproj_even(x, row(g_mix_pre, 0), sh_m[0], sc_m[0], win_e, cos, sa, sb, tm=tm, name="in_even")
    hc, kvc = pre_mm(ctx, row(g_mix_pre, 0), csh_m, csc_m, win_e, tm=C, tn=2 * LANES, w_row_off=8 * LANES, name="in_even_ctx")
    pa = [pool_fwd(u, w_pool[0], pool_scale, name="pool_fwd"), attn_fwd(q, kv, kvc, sink, name="attn_fwd")]
    wout_e = permute_heads(weight(1, pa[1]))
    y0, x1 = mm_post(pa, wout_e, x, row(g_mix_post, 0), gt_m[0], tm=tm, name="out_even")
    h1, hu0, a0, f0, x2 = ffn_fwd(0, x1)
    win_o = weight(4, x2)
    h2, z1 = pre_mm(x2, row(g_mix_pre, 1), sh_m[1], sc_m[1], win_o, tm=tm_up, tn=D, name="in_odd")
    us = sgu_fwd(z1, ln_g, ln_b, sgu_w[0], bst, name="sgu_fwd")
    wout_o = weight(5, us)
    y1, x3 = mm_post([us], wout_o, x2, row(g_mix_post, 1), gt_m[1], tm=tm, name="out_odd")
    h3, hu1, a1, f1, dx4, loss_part = ffn_fwd(1, x3)

    g_srcs, g_lands, g_sems = [], [], []

    def scatter(grads, nm):
        own = place_own(grads, [g.shape[0] // N_DEV for g in grads], me, scatter=True, name=nm.replace("start", "own"))
        sems, srcs, lands, tok = exchange_start(grads, own, scatter=True, name=nm)
        g_srcs.extend(srcs)
        g_lands.extend(lands)
        g_sems.extend(sems)
        return tok[0:1, 0:1]

    def ffn_bwd(i, dxo, xin, h, hu, a, f, g_post):
        dyf, da, dg_post, dgt = post_bwd_mm(dxo, f, g_post, gt_f[i], wdn[i], tm=tm, name=f"ffn_down_bwd_{i}")
        dhg, dhu, dcwg, dcwu, dcbg, dcbu = conv_bwd(da, hu[1], hu[2], hu[0], conv_w[i], rows=conv_rows, wblk=2 * LANES,
                                                    name=f"ffn_conv_bwd_{i}")
        dxin, dg_pre, dsh, dsc = mm_pre_bwd([dhg, dhu], wup[i], xin, dxo, row(g_ffn_pre, i), sc_f[i], tm=tm, tk=half_f,
                                            name=f"ffn_up_bwd_{i}")
        g_dn = wgrad([a], dyf, tr=2 * LANES, name=f"wgrad_down_{i}")
        g_up = wgrad([dhg, dhu], h, tr=2 * LANES, name=f"wgrad_up_{i}")
        tok = scatter([g_dn, g_up], f"scatter_start_ffn_{i}")
        return dxin, tok, dict(g_ffn_post=dg_post, g_ffn_pre=dg_pre, gt_f=dgt, sh_f=dsh, sc_f=dsc,
                               ffn_conv_w=jnp.concatenate([dcwg, dcwu], axis=1), ffn_conv_b=jnp.concatenate([dcbg, dcbu], axis=1)[0])

    dx3, tok, sf1 = ffn_bwd(1, dx4, x3, h3, hu1, a1, f1, row(g_ffn_post, 1))
    dy1, dus, dg_mpost1, dgt_m1 = post_bwd_mm(dx3, y1, row(g_mix_post, 1) + tok, gt_m[1], wout_o, tm=tm, name="out_odd_bwd")
    dz1, dws, dbs, dlng, dlnb = sgu_bwd(z1, dus, ln_g, ln_b, sgu_w[0], bst, name="sgu_bwd")
    dx2, dg_mpre1, dsh_m1, dsc_m1 = mm_pre_bwd([dz1], win_o, x2, dx3, row(g_mix_pre, 1), sc_m[1], tm=tm, tk=D, name="in_odd_bwd")
    tok = scatter([wgrad([us], dy1, tr=2 * LANES, name="wgrad_out_odd"), wgrad([dz1], h2, tr=2 * LANES, name="wgrad_in_odd")],
                  "scatter_start_mix_1")

    dx1, tok, sf0 = ffn_bwd(0, dx2, x1, h1, hu0, a0, f0, row(g_ffn_post, 0) + tok)
    dy0, dpa, dg_mpost0, dgt_m0 = post_bwd_mm(dx1, y0, row(g_mix_post, 0) + tok, gt_m[0], wout_e, tm=tm, name="out_even_bwd")
    du, dwp, dps = pool_bwd(u, dpa, w_pool[0], pool_scale, name="pool_bwd")
    dq, dkv, dkvc, dsink = attn_bwd(q, kv, kvc, sink, dpa, cos, sa, sb, name="attn_bwd")
    dz0 = jnp.concatenate([du, dq, dkv], axis=1)
    dzc = jnp.concatenate([jnp.zeros((C, 8 * LANES), BF16), dkvc], axis=1)
    tok = scatter([permute_heads(wgrad(pa, dy0, tr=2 * LANES, name="wgrad_out_even"), inverse=True),
                   permute_heads(wgrad([dz0], h0, tr=2 * LANES, extra=(dzc, hc), name="wgrad_in_even"), inverse=True)],
                  "scatter_start_mix_0")
    grad_x, dg_mpre0, dsh_m0, dsc_m0 = mm_pre_bwd([dz0], win_e, x, dx1, row(g_mix_pre, 0) + tok, sc_m[0], tm=tm, tk=dz0.shape[1],
                                                  name="in_even_bwd")
    _, dg_mpre0c, dcsh, dcsc = mm_pre_bwd([dkvc], win_e, ctx, None, row(g_mix_pre, 0), csc_m, tm=C, tk=2 * LANES,
                                          w_row_off=8 * LANES, name="in_even_ctx_bwd")

    out = {}

    def update(name, lands, transposed):
        w_, m_, v_ = (a.transpose(0, 2, 1) if transposed else a for a in (P[name], M[name], V[name]))
        r = w_.shape[1]
        tr = r // 4 if r % 64 == 0 and r > 256 else r
        res = adamw(w_, m_, v_, [l_.reshape(N_DEV, r, l_.shape[1]) for l_ in lands], tr=tr, name=f"adamw_{name}")
        for kind, val in zip(("grad", "delta", "new_m", "new_v"), res):
            out[(kind, name)] = val.transpose(0, 2, 1) if transposed else val

    zero = jnp.zeros((1, D), F32)
    dmod0 = jnp.concatenate([dsh_m0, dsc_m0, dgt_m0, sf0["sh_f"], sf0["sc_f"], sf0["gt_f"]], axis=1)
    dmodc = jnp.concatenate([dcsh, dcsc, zero, zero, zero, zero], axis=1)
    dmod1 = jnp.concatenate([dsh_m1, dsc_m1, dgt_m1, sf1["sh_f"], sf1["sc_f"], sf1["gt_f"]], axis=1)
    dmods = jnp.concatenate([dmod0, dmodc, dmod1], axis=0)
    dm = dmods.reshape(-1, LANES).astype(BF16)
    d_sems, d_srcs, d_lands, d_tok = exchange_start(
        [dm], place_own([dm], [dm.shape[0]], me, scatter=False, name="dmods_own"), scatter=False, name="dmods_start")
    slots = exchange_wait(g_srcs[:6], g_lands[:6], g_sems[:12], d_tok, scatter=True, name="scatter_wait_early")
    update("w_ffn_up", [slots[5], slots[1]], True)
    update("w_ffn_down", [slots[4], slots[0]], False)
    update("w_in_odd", [slots[3]], True)
    update("w_out_odd", [slots[2]], False)
    dmods_all = exchange_wait(d_srcs, d_lands, d_sems, out[("new_v", "w_out_odd")], scatter=False, name="dmods_wait")[0]
    dall = lax.dynamic_index_in_dim(dmods_all.astype(F32).reshape(N_DEV, 3, N_DEV, n_ada), me, 2, False)
    g_w_ada, dcc = ada_bwd_mm(silu_c, c_ctx[None, :], dall, w_ada, name="ada_bwd")

    rep = dict(
        c_ctx=dcc[0:1],
        b_ada=jnp.concatenate([dmod0 + dmodc, dmod1]),
        g_mix_pre=jnp.concatenate([dg_mpre0 + dg_mpre0c, dg_mpre1]),
        g_mix_post=jnp.concatenate([dg_mpost0, dg_mpost1]),
        g_ffn_pre=jnp.concatenate([sf0["g_ffn_pre"], sf1["g_ffn_pre"]]),
        g_ffn_post=jnp.concatenate([sf0["g_ffn_post"], sf1["g_ffn_post"]]),
        w_pool=_nat2d(dwp), pool_scale=dps, attn_sink=dsink[:, :N_Q_HEADS],
        sgu_w=_nat2d(dws), sgu_b=dbs[:, :sgu_b.shape[1]].T,
        ffn_conv_b=jnp.stack([sf0["ffn_conv_b"], sf1["ffn_conv_b"]]),
    )
    hi = loss_part.astype(BF16).astype(F32)
    mid = (loss_part - hi).astype(BF16).astype(F32)
    loss_piece = jnp.pad(jnp.concatenate([hi, mid, loss_part - hi - mid], axis=1), ((0, 7), (0, LANES - 3)))
    conv_g = jnp.stack([sf0["ffn_conv_w"], sf1["ffn_conv_w"]]).reshape(2 * 3, N_DEV, n_cw).swapaxes(0, 1)
    shard_full = dict(sgu_ln_g=dlng.reshape(N_DEV, LANES), sgu_ln_b=dlnb.reshape(N_DEV, LANES),
                      ffn_conv_w=jnp.concatenate([_pack_rows(conv_g[d]) for d in range(N_DEV)], axis=0))
    small_names = list(rep) + list(shard_full)
    pieces = [_pack_rows(rep[k]) for k in rep] + list(shard_full.values()) + [loss_piece]
    sizes = [p.shape[0] for p in pieces]
    offs = [sum(sizes[:i]) for i in range(len(sizes))]
    pieces.append(jnp.zeros((-sum(sizes) % 16, LANES), F32))
    gpack = jnp.concatenate(pieces, axis=0).astype(BF16)
    own = place_own([gpack], [gpack.shape[0]], me, scatter=False, name="smallgrad_own")
    s_sems, s_srcs, s_lands, small_tok = exchange_start([gpack], own, scatter=False, name="smallgrad_start")

    slots = exchange_wait(g_srcs[6:], g_lands[6:], g_sems[12:], small_tok, scatter=True, name="scatter_wait_late")
    update("w_in_even", [slots[1]], True)
    update("w_out_even", [slots[0]], False)
    res = adamw(w_ada, m_w_ada, v_w_ada, [g_w_ada[l][None] for l in range(w_ada.shape[0])], tr=D // 4, name="adamw_w_ada")
    for kind, val in zip(("grad", "delta", "new_m", "new_v"), res):
        out[(kind, "w_ada")] = val

    gpacks = exchange_wait(s_srcs, s_lands, s_sems, out[("new_v", "w_ada")], scatter=False, name="smallgrad_wait")[0]
    per_dev = {k: shard_full[k].shape[0] // N_DEV for k in shard_full}
    params = [(_nat2d(P[k]), _nat2d(M[k]), _nat2d(V[k]), offs[i], per_dev.get(k, 0)) for i, k in enumerate(small_names)]
    res = small_update(gpacks.reshape(N_DEV, -1, LANES), jnp.reshape(me, (1,)).astype(jnp.int32), params, offs[-1], name="adamw_small")
    for i, k in enumerate(small_names):
        for kind, val in zip(("grad", "delta", "new_m", "new_v"), res[4 * i:4 * i + 4]):
            out[(kind, k)] = val.reshape(P[k].shape)
    loss = res[-1][0, 0]

    names = list(P)
    final = [loss, grad_x[None]]
    for kind in ("grad", "delta", "new_m", "new_v"):
        for k in names:
            val = out[(kind, k)]
            final.append(val)
    return tuple(final)
```

```python
import functools
import math

import jax
import jax.numpy as jnp
from jax import lax
from jax.experimental import pallas as pl
from jax.experimental.pallas import tpu as pltpu

F32 = jnp.float32
BF16 = jnp.bfloat16
MESH = pl.DeviceIdType.MESH
N_DEV = 8
LANES = 128
VMEM_LIMIT = 48 * 1024 * 1024
EPS = 1e-6
NEG_INF = -1e30
GRID_W = 64
WINDOW = 128
BLK = 128
HEAD_DIM = 64
N_Q_HEADS = 8
N_KV_HEADS = 2
GQA = N_Q_HEADS // N_KV_HEADS
POOL_WINDOWS = (2, 4, 8, 16)
ROPE_BASE = 10000.0
ROPE_FREQS = HEAD_DIM // 4
PAD = 16
ADAM_LR, ADAM_B1, ADAM_B2, ADAM_EPS, ADAM_WD, ADAM_STEP = 0.001, 0.9, 0.999, 1e-08, 0.01, 10
BC1 = 1.0 - ADAM_B1 ** ADAM_STEP
BC2 = 1.0 - ADAM_B2 ** ADAM_STEP
SQRT_2_OVER_PI = math.sqrt(2.0 / math.pi)
GELU_C = 0.044715


def _cp(sem=None):
    return pltpu.CompilerParams(dimension_semantics=sem, vmem_limit_bytes=VMEM_LIMIT)


def _dot(a, b):
    return jnp.dot(a, b, preferred_element_type=F32)


def _dot_nt(a, b):
    return lax.dot_general(a, b, (((1,), (1,)), ((), ())), preferred_element_type=F32)


def _dot_tn(a, b):
    return lax.dot_general(a, b, (((0,), (0,)), ((), ())), preferred_element_type=F32)


def _rms(x):
    r = lax.rsqrt(jnp.mean(x * x, axis=-1, keepdims=True) + EPS)
    return x * r, r


def _rms_bwd(dn, n, r):
    return r * (dn - n * jnp.mean(dn * n, axis=-1, keepdims=True))


def _colsum(a):
    return jnp.sum(a, axis=0, keepdims=True)


def _rope(x, c, sa, sb):
    return x * c + pltpu.roll(x, LANES - ROPE_FREQS, 1) * sa + pltpu.roll(x, ROPE_FREQS, 1) * sb


def _full(shape):
    return pl.BlockSpec(shape, lambda *_: (0,) * len(shape))


def pre_mm(x, g, sh, sc, wt, *, tm, tn, w_row_off=0, name):
    T, D = x.shape
    n_rows = wt.shape[0] - w_row_off
    off = w_row_off // tn

    def body(x_ref, g_ref, sh_ref, sc_ref, w_ref, h_ref, z_ref):
        @pl.when(pl.program_id(1) == 0)
        def _():
            n, _ = _rms(x_ref[...])
            h_ref[...] = (n * g_ref[...] * (1.0 + sc_ref[...]) + sh_ref[...]).astype(BF16)

        z_ref[...] = _dot_nt(h_ref[...], w_ref[...]).astype(BF16)

    vec = pl.BlockSpec((1, D), lambda i, j: (0, 0))
    return pl.pallas_call(
        body, name=name, grid=(T // tm, n_rows // tn),
        in_specs=[pl.BlockSpec((tm, D), lambda i, j: (i, 0)), vec, vec, vec, pl.BlockSpec((tn, D), lambda i, j: (j + off, 0))],
        out_specs=[pl.BlockSpec((tm, D), lambda i, j: (i, 0)), pl.BlockSpec((tm, tn), lambda i, j: (i, j))],
        out_shape=[jax.ShapeDtypeStruct((T, D), BF16), jax.ShapeDtypeStruct((T, n_rows), BF16)],
        compiler_params=_cp(("parallel", "arbitrary")),
    )(x, g, sh, sc, wt)


def inproj_even(x, g, sh, sc, wt, cos, sa, sb, *, tm, name):
    T, D = x.shape
    N = wt.shape[0]

    def body(x_ref, g_ref, sh_ref, sc_ref, w_ref, c_ref, sa_ref, sb_ref, h_ref, u_ref, q_ref, kv_ref):
        n, _ = _rms(x_ref[...])
        h = (n * g_ref[...] * (1.0 + sc_ref[...]) + sh_ref[...]).astype(BF16)
        h_ref[...] = h
        z = _dot_nt(h, w_ref[...])
        u_ref[...] = z[:, :4 * LANES]
        c, a, b = c_ref[...], sa_ref[...], sb_ref[...]
        for s in range(4):
            q_ref[:, s * LANES:(s + 1) * LANES] = _rope(z[:, (4 + s) * LANES:(5 + s) * LANES], c, a, b).astype(BF16)
        kv_ref[:, :LANES] = _rope(z[:, 8 * LANES:9 * LANES], c, a, b).astype(BF16)
        kv_ref[:, LANES:] = z[:, 9 * LANES:].astype(BF16)

    vec = pl.BlockSpec((1, D), lambda i: (0, 0))
    row = lambda w: pl.BlockSpec((tm, w), lambda i: (i, 0))
    return pl.pallas_call(
        body, name=name, grid=(T // tm,),
        in_specs=[row(D), vec, vec, vec, _full((N, D)), row(LANES), row(LANES), row(LANES)],
        out_specs=[row(D), row(4 * LANES), row(4 * LANES), row(2 * LANES)],
        out_shape=[jax.ShapeDtypeStruct((T, D), BF16), jax.ShapeDtypeStruct((T, 4 * LANES), F32),
                   jax.ShapeDtypeStruct((T, 4 * LANES), BF16), jax.ShapeDtypeStruct((T, 2 * LANES), BF16)],
        compiler_params=_cp(("parallel",)),
    )(x, g, sh, sc, wt, cos, sa, sb)


def mm_post(a_parts, w, x, g, gt, *, tm, target=None, name):
    T = a_parts[0].shape[0]
    D = w.shape[1]
    npart = len(a_parts)
    offs = [sum(a_.shape[1] for a_ in a_parts[:p]) for p in range(npart + 1)]
    with_loss = target is not None

    def body(*refs):
        a_refs, (w_ref, x_ref, g_ref, gt_ref) = refs[:npart], refs[npart:npart + 4]
        y = _dot(a_refs[0][...], w_ref[offs[0]:offs[1], :])
        for p in range(1, npart):
            y = y + _dot(a_refs[p][...], w_ref[offs[p]:offs[p + 1], :])
        n, _ = _rms(y)
        xn = x_ref[...] + gt_ref[...] * (n * g_ref[...])
        if not with_loss:
            y_ref, xn_ref = refs[npart + 4:]
            y_ref[...] = y.astype(BF16)
            xn_ref[...] = xn
            return
        t_ref, y_ref, d_ref, l_ref = refs[npart + 4:]
        y_ref[...] = y.astype(BF16)

        @pl.when(pl.program_id(0) == 0)
        def _():
            l_ref[...] = jnp.zeros_like(l_ref)

        e = xn - t_ref[...]
        l_ref[...] += 0.5 * jnp.sum(jnp.mean(e * e, axis=-1, keepdims=True), axis=0, keepdims=True)
        d_ref[...] = e * (1.0 / D)

    vec = pl.BlockSpec((1, D), lambda i: (0, 0))
    row = lambda w_: pl.BlockSpec((tm, w_), lambda i: (i, 0))
    in_specs = [row(a_.shape[1]) for a_ in a_parts] + [_full(w.shape), row(D), vec, vec]
    out_specs = [row(D), row(D)]
    out_shape = [jax.ShapeDtypeStruct((T, D), BF16), jax.ShapeDtypeStruct((T, D), F32)]
    if with_loss:
        in_specs.append(row(D))
        out_specs.append(_full((1, 1)))
        out_shape.append(jax.ShapeDtypeStruct((1, 1), F32))
    return pl.pallas_call(
        body, name=name, grid=(T // tm,), in_specs=in_specs, out_specs=out_specs, out_shape=out_shape,
        compiler_params=_cp(("arbitrary",) if with_loss else ("parallel",)),
    )(*a_parts, w, x, g, gt, *((target,) if with_loss else ()))


def post_bwd_mm(dxn, y, g, gt, w, *, tm, name):
    T, D = y.shape
    K = w.shape[0]

    def body(dxn_ref, y_ref, g_ref, gt_ref, w_ref, dy_ref, da_ref, dg_ref, dgt_ref):
        @pl.when(pl.program_id(0) == 0)
        def _():
            dg_ref[...] = jnp.zeros_like(dg_ref)
            dgt_ref[...] = jnp.zeros_like(dgt_ref)

        d = dxn_ref[...]
        n, r = _rms(y_ref[...].astype(F32))
        g_, gt_ = g_ref[...], gt_ref[...]
        dg_ref[...] += _colsum(d * gt_ * n)
        dgt_ref[...] += _colsum(d * g_ * n)
        dy = _rms_bwd(d * (gt_ * g_), n, r).astype(BF16)
        dy_ref[...] = dy
        da_ref[...] = _dot_nt(dy, w_ref[...]).astype(BF16)

    vec = pl.BlockSpec((1, D), lambda i: (0, 0))
    row = lambda w_: pl.BlockSpec((tm, w_), lambda i: (i, 0))
    return pl.pallas_call(
        body, name=name, grid=(T // tm,),
        in_specs=[row(D), row(D), vec, vec, _full((K, D))],
        out_specs=[row(D), row(K), vec, vec],
        out_shape=[jax.ShapeDtypeStruct((T, D), BF16), jax.ShapeDtypeStruct((T, K), BF16),
                   jax.ShapeDtypeStruct((1, D), F32), jax.ShapeDtypeStruct((1, D), F32)],
        compiler_params=_cp(("arbitrary",)),
    )(dxn, y, g, gt, w)


def mm_pre_bwd(dzs, wt, x, dres, g, sc, *, tm, tk, w_row_off=0, name):
    T, N = dzs[0].shape
    D = x.shape[1]
    nk = N // tk
    npart = len(dzs)
    off = w_row_off // tk
    has_res = dres is not None

    def body(*refs):
        dz_refs = refs[:npart]
        w_refs = refs[npart:2 * npart]
        rest = refs[2 * npart:]
        x_ref = rest[0]
        dres_ref = rest[1] if has_res else None
        g_ref, sc_ref, dx_ref, dg_ref, dsh_ref, dsc_ref, acc = rest[1 + has_res:]
        i, k = pl.program_id(0), pl.program_id(1)

        @pl.when(jnp.logical_and(i == 0, k == 0))
        def _():
            dg_ref[...] = jnp.zeros_like(dg_ref)
            dsh_ref[...] = jnp.zeros_like(dsh_ref)
            dsc_ref[...] = jnp.zeros_like(dsc_ref)

        part = _dot(dz_refs[0][...], w_refs[0][...])
        for p in range(1, npart):
            part = part + _dot(dz_refs[p][...], w_refs[p][...])

        @pl.when(k == 0)
        def _():
            acc[...] = part

        @pl.when(k > 0)
        def _():
            acc[...] += part

        @pl.when(k == nk - 1)
        def _():
            dh = acc[...]
            n, r = _rms(x_ref[...])
            g_, s1 = g_ref[...], 1.0 + sc_ref[...]
            dsh_ref[...] += _colsum(dh)
            dsc_ref[...] += _colsum(dh * n * g_)
            dg_ref[...] += _colsum(dh * s1 * n)
            dxp = _rms_bwd(dh * (g_ * s1), n, r)
            dx_ref[...] = dxp + dres_ref[...] if has_res else dxp

    vec = pl.BlockSpec((1, D), lambda i, k: (0, 0))
    row = pl.BlockSpec((tm, D), lambda i, k: (i, 0))
    w_specs = [pl.BlockSpec((tk, D), (lambda i, k, p=p: (k + off + p * nk, 0))) for p in range(npart)]
    res_specs, res_args = ([row], (dres,)) if has_res else ([], ())
    return pl.pallas_call(
        body, name=name, grid=(T // tm, nk),
        in_specs=[pl.BlockSpec((tm, tk), lambda i, k: (i, k))] * npart + w_specs + [row] + res_specs + [vec, vec],
        out_specs=[row, vec, vec, vec],
        out_shape=[jax.ShapeDtypeStruct((T, D), F32)] + [jax.ShapeDtypeStruct((1, D), F32)] * 3,
        scratch_shapes=[pltpu.VMEM((tm, D), F32)],
        compiler_params=_cp(("arbitrary", "arbitrary")),
    )(*dzs, *([wt] * npart), x, *res_args, g, sc)


def wgrad(a_parts, b, *, tr, extra=None, name):
    T, R = a_parts[0].shape
    D = b.shape[1]
    npart = len(a_parts)
    nr = R // tr

    def body(*refs):
        a_refs, b_ref = refs[:npart], refs[npart]
        g_ref = refs[-1]
        for p in range(npart):
            @pl.when(pl.program_id(0) // nr == p)
            def _():
                acc = _dot_tn(a_refs[p][...], b_ref[...])
                if extra is not None:
                    acc += _dot_tn(refs[npart + 1][...], refs[npart + 2][...])
                g_ref[...] = acc.astype(BF16)

    in_specs = [pl.BlockSpec((T, tr), (lambda r, p=p: (0, jnp.clip(r - p * nr, 0, nr - 1)))) for p in range(npart)]
    in_specs.append(_full((T, D)))
    args = [*a_parts, b]
    if extra is not None:
        a2, b2 = extra
        in_specs += [pl.BlockSpec((a2.shape[0], tr), lambda r: (0, r)), _full(b2.shape)]
        args += [a2, b2]
    return pl.pallas_call(
        body, name=name, grid=(npart * nr,),
        in_specs=in_specs, out_specs=pl.BlockSpec((tr, D), lambda r: (r, 0)),
        out_shape=jax.ShapeDtypeStruct((npart * R, D), BF16),
        compiler_params=_cp(("parallel",)),
    )(*args)


def _conv_ext(ref, r0, rows, total):
    top = ref[pl.ds(pl.multiple_of(jnp.maximum(r0 - PAD, 0), PAD), PAD), :]
    mid = ref[pl.ds(r0, rows), :]
    bot = ref[pl.ds(pl.multiple_of(jnp.minimum(r0 + rows, total - PAD), PAD), PAD), :]
    top = jnp.where(r0 > 0, top, jnp.zeros_like(top))
    bot = jnp.where(r0 + rows < total, bot, jnp.zeros_like(bot))
    return jnp.concatenate([top, mid, bot], axis=0).astype(F32)


def _shift_rows(a, k):
    return pltpu.roll(a, k % a.shape[0], 0)


def _conv3(x, w, b):
    return w[0:1] * _shift_rows(x, 1) + w[1:2] * x + w[2:3] * _shift_rows(x, -1) + b


def _gate_up_specs(rows_, wblk, nb):
    return [pl.BlockSpec((rows_, wblk), lambda j: (0, j)), pl.BlockSpec((rows_, wblk), lambda j: (0, j + nb))]


def conv_fwd(hu, cw, cb, *, rows, wblk, name):
    L, N2 = hu.shape
    nb = N2 // 2 // wblk
    nchunk = L // rows

    def body(hg_ref, hu_ref, wg_ref, wu_ref, bg_ref, bu_ref, a_ref, s1_ref, s2_ref):
        def chunk(ci, carry):
            r0 = pl.multiple_of(ci * rows, rows)
            gate = _conv3(_conv_ext(hg_ref, r0, rows, L), wg_ref[...], bg_ref[...])[PAD:PAD + rows]
            up = _conv3(_conv_ext(hu_ref, r0, rows, L), wu_ref[...], bu_ref[...])[PAD:PAD + rows]
            sg = jax.nn.sigmoid(gate)
            silu = gate * sg
            at = pl.ds(r0, rows)
            a_ref[at, :] = (silu * up).astype(BF16)
            s1_ref[at, :] = silu.astype(BF16)
            s2_ref[at, :] = (up * (sg + silu * (1.0 - sg))).astype(BF16)
            return carry

        lax.fori_loop(0, nchunk, chunk, 0)

    out = pl.BlockSpec((L, wblk), lambda j: (0, j))
    return pl.pallas_call(
        body, name=name, grid=(nb,),
        in_specs=_gate_up_specs(L, wblk, nb) + _gate_up_specs(3, wblk, nb) + _gate_up_specs(1, wblk, nb),
        out_specs=[out] * 3, out_shape=[jax.ShapeDtypeStruct((L, N2 // 2), BF16)] * 3,
        compiler_params=_cp(("parallel",)),
    )(hu, hu, cw, cw, cb, cb)


def conv_bwd(da, s1, s2, hu, cw, *, rows, wblk, name):
    L, N2 = hu.shape
    F = N2 // 2
    nb = F // wblk
    nchunk = L // rows
    mid = slice(PAD, PAD + rows)

    def body(da_ref, s1_ref, s2_ref, hg_ref, hu_ref, wg_ref, wu_ref, dg_ref, du_ref, dwg_ref, dwu_ref, dbg_ref, dbu_ref):
        for ref in (dwg_ref, dwu_ref, dbg_ref, dbu_ref):
            ref[...] = jnp.zeros_like(ref)

        def half_bwd(x_ref, dh, w_ref, dx_ref, dw_ref, db_ref, r0):
            w = w_ref[...]
            nxt, prv = _shift_rows(dh, -1)[mid], _shift_rows(dh, 1)[mid]
            dhm, xm = dh[mid], x_ref[pl.ds(r0, rows), :].astype(F32)
            dx_ref[pl.ds(r0, rows), :] = (w[0:1] * nxt + w[1:2] * dhm + w[2:3] * prv).astype(BF16)
            db_ref[...] += _colsum(dhm)
            dw_ref[0:1, :] += _colsum(nxt * xm)
            dw_ref[1:2, :] += _colsum(dhm * xm)
            dw_ref[2:3, :] += _colsum(prv * xm)

        def chunk(ci, carry):
            r0 = pl.multiple_of(ci * rows, rows)
            d = _conv_ext(da_ref, r0, rows, L)
            half_bwd(hu_ref, d * _conv_ext(s1_ref, r0, rows, L), wu_ref, du_ref, dwu_ref, dbu_ref, r0)
            half_bwd(hg_ref, d * _conv_ext(s2_ref, r0, rows, L), wg_ref, dg_ref, dwg_ref, dbg_ref, r0)
            return carry

        lax.fori_loop(0, nchunk, chunk, 0)

    blk = lambda r: pl.BlockSpec((r, wblk), lambda j: (0, j))
    return pl.pallas_call(
        body, name=name, grid=(nb,),
        in_specs=[blk(L)] * 3 + _gate_up_specs(L, wblk, nb) + _gate_up_specs(3, wblk, nb),
        out_specs=[blk(L), blk(L), blk(3), blk(3), blk(1), blk(1)],
        out_shape=[jax.ShapeDtypeStruct((L, F), BF16)] * 2 + [jax.ShapeDtypeStruct((3, F), F32)] * 2
        + [jax.ShapeDtypeStruct((1, F), F32)] * 2,
        compiler_params=_cp(("parallel",)),
    )(da, s1, s2, hu, hu, cw, cw)


def _window_sums(pad_ref, w, lead):
    a = pad_ref[...]
    k = 1
    while k < w:
        a = a + _shift_rows(a, -k)
        k *= 2
    return _shift_rows(a, lead) if lead else a


def _pool_counts(L, h):
    t = lax.broadcasted_iota(jnp.int32, (L, 1), 0)
    return (jnp.minimum(t + h, L) - jnp.maximum(t - h, 0)).astype(F32)


def _pooled(u_ref, pad_ref, L, w):
    h = w // 2
    pad_ref[pl.ds(PAD, L), :] = u_ref[...]
    win = _window_sums(pad_ref, w, h)[PAD:PAD + L]
    return win / _pool_counts(L, h) - u_ref[...]


def _zero_pad_edges(pad_ref, L):
    z = jnp.zeros((PAD, LANES), F32)
    pad_ref[pl.ds(0, PAD), :] = z
    pad_ref[pl.ds(PAD + L, PAD), :] = z


def pool_fwd(u, w_pool, pool_scale, *, name):
    L = u.shape[0]

    def body(u_ref, w_ref, ps_ref, p_ref, pad_ref):
        _zero_pad_edges(pad_ref, L)
        for gi, win in enumerate(POOL_WINDOWS):
            @pl.when(pl.program_id(0) == gi)
            def _():
                pooled = _pooled(u_ref, pad_ref, L, win)
                p_ref[...] = (_dot(pooled.astype(BF16), w_ref[...].astype(BF16)) * ps_ref[...]).astype(BF16)

    return pl.pallas_call(
        body, name=name, grid=(len(POOL_WINDOWS),),
        in_specs=[pl.BlockSpec((L, LANES), lambda gi: (0, gi)), pl.BlockSpec((None, LANES, LANES), lambda gi: (gi, 0, 0)),
                  pl.BlockSpec((1, LANES), lambda gi: (0, gi))],
        out_specs=pl.BlockSpec((L, LANES), lambda gi: (0, gi)),
        out_shape=jax.ShapeDtypeStruct((L, 4 * LANES), BF16),
        scratch_shapes=[pltpu.VMEM((L + 2 * PAD, LANES), F32)],
        compiler_params=_cp(("parallel",)),
    )(u, w_pool, pool_scale)


def pool_bwd(u, dpa, w_pool, pool_scale, *, name):
    L = u.shape[0]

    def body(u_ref, dp_ref, w_ref, ps_ref, du_ref, dw_ref, dps_ref, pad_ref):
        _zero_pad_edges(pad_ref, L)
        for gi, win in enumerate(POOL_WINDOWS):
            @pl.when(pl.program_id(0) == gi)
            def _():
                h = win // 2
                wb = w_ref[...].astype(BF16)
                pooled = _pooled(u_ref, pad_ref, L, win).astype(BF16)
                dp = dp_ref[...].astype(F32)
                dps_ref[...] = _colsum(dp * _dot(pooled, wb))
                dy = (dp * ps_ref[...]).astype(BF16)
                dw_ref[...] = _dot_tn(pooled, dy)
                dpooled = _dot_nt(dy, wb)
                pad_ref[pl.ds(PAD, L), :] = dpooled / _pool_counts(L, h)
                du_ref[...] = (_window_sums(pad_ref, win, h - 1)[PAD:PAD + L] - dpooled).astype(BF16)

    return pl.pallas_call(
        body, name=name, grid=(len(POOL_WINDOWS),),
        in_specs=[pl.BlockSpec((L, LANES), lambda gi: (0, gi)), pl.BlockSpec((L, LANES), lambda gi: (0, gi)),
                  pl.BlockSpec((None, LANES, LANES), lambda gi: (gi, 0, 0)), pl.BlockSpec((1, LANES), lambda gi: (0, gi))],
        out_specs=[pl.BlockSpec((L, LANES), lambda gi: (0, gi)), pl.BlockSpec((None, LANES, LANES), lambda gi: (gi, 0, 0)),
                   pl.BlockSpec((1, LANES), lambda gi: (0, gi))],
        out_shape=[jax.ShapeDtypeStruct((L, 4 * LANES), BF16), jax.ShapeDtypeStruct((4, LANES, LANES), F32),
                   jax.ShapeDtypeStruct((1, 4 * LANES), F32)],
        scratch_shapes=[pltpu.VMEM((L + 2 * PAD, LANES), F32)],
        compiler_params=_cp(("parallel",)),
    )(u, dpa, w_pool, pool_scale)


def _attn_probs(qk, band_k, ctx_k, sink_ref, kh, mask4):
    s_loc = jnp.where(mask4, _dot_nt(qk, band_k), NEG_INF)
    s_ctx = _dot_nt(qk, ctx_k)
    sk = jnp.concatenate([jnp.full((BLK, 1), sink_ref[kh * GQA + hh], F32) for hh in range(GQA)], axis=0)
    m = jnp.maximum(jnp.maximum(jnp.max(s_loc, axis=-1, keepdims=True), jnp.max(s_ctx, axis=-1, keepdims=True)), sk)
    e_loc, e_ctx, e_s = jnp.exp(s_loc - m), jnp.exp(s_ctx - m), jnp.exp(sk - m)
    inv = 1.0 / (jnp.sum(e_loc, axis=-1, keepdims=True) + jnp.sum(e_ctx, axis=-1, keepdims=True) + e_s)
    return e_loc * inv, e_ctx * inv, e_s * inv


def _attn_block(n, L):
    start = pl.multiple_of(jnp.clip((n - 1) * BLK, 0, L - 3 * BLK), BLK)
    qpos = n * BLK + lax.broadcasted_iota(jnp.int32, (BLK, 3 * BLK), 0)
    kpos = start + lax.broadcasted_iota(jnp.int32, (BLK, 3 * BLK), 1)
    mask = jnp.abs(kpos - qpos) <= WINDOW
    return start, jnp.concatenate([mask] * GQA, axis=0)


def _stack_slabs(ref):
    return jnp.concatenate([ref[:, s * LANES:(s + 1) * LANES] for s in range(GQA)], axis=0)


def _kv_head_lanes(kh):
    return (lax.broadcasted_iota(jnp.int32, (1, LANES), 1) // HEAD_DIM) == kh


def permute_heads(w, inverse=False):
    lo, hi = 4 * LANES, 8 * LANES
    mid = w[lo:hi].reshape(*((GQA, N_KV_HEADS) if inverse else (N_KV_HEADS, GQA)), HEAD_DIM, w.shape[1])
    return jnp.concatenate([w[:lo], mid.swapaxes(0, 1).reshape(hi - lo, w.shape[1]), w[hi:]], axis=0)


def attn_fwd(q, kv, kvc, sink, *, name):
    L = q.shape[0]
    C = kvc.shape[0]
    scale = HEAD_DIM ** -0.5

    def body(q_ref, kv_ref, kvc_ref, sink_ref, o_ref):
        start, mask4 = _attn_block(pl.program_id(0), L)
        band = kv_ref[pl.ds(start, 3 * BLK), :]
        kvc_ = kvc_ref[...]
        qs = _stack_slabs(q_ref) * scale
        o = jnp.zeros((GQA * BLK, LANES), F32)
        for kh in range(N_KV_HEADS):
            grp = _kv_head_lanes(kh)
            qk = jnp.where(grp, qs, jnp.zeros_like(qs))
            p_loc, p_ctx, _ = _attn_probs(qk, band[:, :LANES], kvc_[:, :LANES], sink_ref, kh, mask4)
            o = o + jnp.where(grp, _dot(p_loc.astype(BF16), band[:, LANES:]) + _dot(p_ctx.astype(BF16), kvc_[:, LANES:]), 0.0)
        for s in range(GQA):
            o_ref[:, s * LANES:(s + 1) * LANES] = o[s * BLK:(s + 1) * BLK].astype(BF16)

    return pl.pallas_call(
        body, name=name, grid=(L // BLK,),
        in_specs=[pl.BlockSpec((BLK, 4 * LANES), lambda n: (n, 0)), _full((L, 2 * LANES)), _full((C, 2 * LANES)),
                  pl.BlockSpec(memory_space=pltpu.SMEM)],
        out_specs=pl.BlockSpec((BLK, 4 * LANES), lambda n: (n, 0)),
        out_shape=jax.ShapeDtypeStruct((L, 4 * LANES), BF16),
        compiler_params=_cp(("parallel",)),
    )(q, kv, kvc, sink)


def attn_bwd(q, kv, kvc, sink, dpa, cos, sa, sb, *, name):
    L = q.shape[0]
    C = kvc.shape[0]
    nb = L // BLK
    scale = HEAD_DIM ** -0.5

    def body(q_ref, kv_ref, kvc_ref, sink_ref, do_ref, c_ref, sa_ref, sb_ref, cq_ref, saq_ref, sbq_ref,
             dq_ref, dkv_ref, dkvc_ref, dsink_ref, dkv_acc, dkvc_acc):
        n = pl.program_id(0)

        @pl.when(n == 0)
        def _():
            dkv_acc[...] = jnp.zeros_like(dkv_acc)
            dkvc_acc[...] = jnp.zeros_like(dkvc_acc)
            dsink_ref[...] = jnp.zeros_like(dsink_ref)

        start, mask4 = _attn_block(n, L)
        band = kv_ref[pl.ds(start, 3 * BLK), :]
        kvc_ = kvc_ref[...]
        band_k, band_v, ctx_k, ctx_v = band[:, :LANES], band[:, LANES:], kvc_[:, :LANES], kvc_[:, LANES:]
        qs = _stack_slabs(q_ref) * scale
        dos = _stack_slabs(do_ref)
        lane = lax.broadcasted_iota(jnp.int32, (1, LANES), 1)
        dsink = jnp.zeros((1, LANES), F32)
        dq = jnp.zeros((GQA * BLK, LANES), F32)
        dk = jnp.zeros((LANES, 3 * BLK), F32)
        dv = jnp.zeros((LANES, 3 * BLK), F32)
        dkc = jnp.zeros((LANES, C), F32)
        dvc = jnp.zeros((LANES, C), F32)
        for kh in range(N_KV_HEADS):
            grp = _kv_head_lanes(kh)
            qk = jnp.where(grp, qs, jnp.zeros_like(qs))
            dok = jnp.where(grp, dos, jnp.zeros_like(dos))
            p_loc, p_ctx, p_s = _attn_probs(qk, band_k, ctx_k, sink_ref, kh, mask4)
            dp_loc = _dot_nt(dok, band_v)
            dp_ctx = _dot_nt(dok, ctx_v)
            delta = jnp.sum(p_loc * dp_loc, axis=-1, keepdims=True) + jnp.sum(p_ctx * dp_ctx, axis=-1, keepdims=True)
            ds_loc = (p_loc * (dp_loc - delta)).astype(BF16)
            ds_ctx = (p_ctx * (dp_ctx - delta)).astype(BF16)
            dsk = p_s * delta
            for hh in range(GQA):
                dsink = dsink - jnp.where(lane == kh * GQA + hh, jnp.sum(dsk[hh * BLK:(hh + 1) * BLK], axis=0, keepdims=True), 0.0)
            dq = dq + jnp.where(grp, _dot(ds_loc, band_k) + _dot(ds_ctx, ctx_k), 0.0)
            dk = dk + _dot_tn(qk, ds_loc)
            dv = dv + _dot_tn(dok, p_loc.astype(BF16))
            dkc = dkc + _dot_tn(qk, ds_ctx)
            dvc = dvc + _dot_tn(dok, p_ctx.astype(BF16))
        dsink_ref[...] += dsink
        dkv_acc[:LANES, pl.ds(start, 3 * BLK)] += dk
        dkv_acc[LANES:, pl.ds(start, 3 * BLK)] += dv
        dkvc_acc[:LANES, :] += dkc
        dkvc_acc[LANES:, :] += dvc
        c, a, b = cq_ref[...], -saq_ref[...], -sbq_ref[...]
        for s in range(GQA):
            dq_ref[:, s * LANES:(s + 1) * LANES] = _rope(dq[s * BLK:(s + 1) * BLK] * scale, c, a, b).astype(BF16)

        @pl.when(n == nb - 1)
        def _():
            dkv_ref[:, :LANES] = _rope(dkv_acc[:LANES, :].T, c_ref[...], -sa_ref[...], -sb_ref[...]).astype(BF16)
            dkv_ref[:, LANES:] = dkv_acc[LANES:, :].T.astype(BF16)
            dkvc_ref[...] = dkvc_acc[...].T.astype(BF16)

    blk = lambda w: pl.BlockSpec((BLK, w), lambda n: (n, 0))
    return pl.pallas_call(
        body, name=name, grid=(nb,),
        in_specs=[blk(4 * LANES), _full((L, 2 * LANES)), _full((C, 2 * LANES)), pl.BlockSpec(memory_space=pltpu.SMEM),
                  pl.BlockSpec((BLK, 4 * LANES), lambda n: (n, 1)),
                  _full((L, LANES)), _full((L, LANES)), _full((L, LANES)), blk(LANES), blk(LANES), blk(LANES)],
        out_specs=[blk(4 * LANES), _full((L, 2 * LANES)), _full((C, 2 * LANES)), _full((1, LANES))],
        out_shape=[jax.ShapeDtypeStruct((L, 4 * LANES), BF16), jax.ShapeDtypeStruct((L, 2 * LANES), BF16),
                   jax.ShapeDtypeStruct((C, 2 * LANES), BF16), jax.ShapeDtypeStruct((1, LANES), F32)],
        scratch_shapes=[pltpu.VMEM((2 * LANES, L), F32), pltpu.VMEM((2 * LANES, C), F32)],
        compiler_params=_cp(("arbitrary",)),
    )(q, kv, kvc, sink, dpa, cos, sa, sb, cos, sa, sb)


def _gelu_parts(x):
    th = jnp.tanh(SQRT_2_OVER_PI * (x + GELU_C * x * x * x))
    return 0.5 * x * (1.0 + th), th


def _gelu_grad(x, th):
    return 0.5 * (1.0 + th) + 0.5 * x * (1.0 - th * th) * SQRT_2_OVER_PI * (1.0 + 3.0 * GELU_C * x * x)


def _layernorm(v):
    mu = jnp.mean(v, axis=-1, keepdims=True)
    vc = v - mu
    rstd = lax.rsqrt(jnp.mean(vc * vc, axis=-1, keepdims=True) + EPS)
    return vc * rstd, rstd


def sgu_fwd(z1, ln_g, ln_b, ws, bst, *, name):
    L, W2 = z1.shape
    W = W2 // 2
    ng = W // LANES

    def body(z_ref, g_ref, b_ref, ws_ref, bs_ref, o_ref):
        z, _ = _gelu_parts(z_ref[...].astype(F32))
        xhat, _ = _layernorm(z[:, W:])
        vln = (xhat * g_ref[...] + b_ref[...]).astype(BF16)
        for gi in range(ng):
            cs = slice(gi * LANES, (gi + 1) * LANES)
            s = _dot(ws_ref[gi].astype(BF16), vln[:, cs]) + bs_ref[:, gi:gi + 1]
            o_ref[:, cs] = (z[:, cs] * s).astype(BF16)

    vec = _full((1, W))
    return pl.pallas_call(
        body, name=name, grid=(L // BLK,),
        in_specs=[pl.BlockSpec((BLK, W2), lambda n: (n, 0)), vec, vec, _full((ng, LANES, LANES)), _full((BLK, ng))],
        out_specs=pl.BlockSpec((BLK, W), lambda n: (n, 0)),
        out_shape=jax.ShapeDtypeStruct((L, W), BF16),
        compiler_params=_cp(("parallel",)),
    )(z1, ln_g, ln_b, ws, bst)


def sgu_bwd(z1, dus, ln_g, ln_b, ws, bst, *, name):
    L, W2 = z1.shape
    W = W2 // 2
    ng = W // LANES

    def body(z_ref, d_ref, g_ref, b_ref, ws_ref, bs_ref, dz_ref, dws_ref, dbs_ref, dg_ref, db_ref, dv_scr):
        @pl.when(pl.program_id(0) == 0)
        def _():
            dws_ref[...] = jnp.zeros_like(dws_ref)
            dbs_ref[...] = jnp.zeros_like(dbs_ref)
            dg_ref[...] = jnp.zeros_like(dg_ref)
            db_ref[...] = jnp.zeros_like(db_ref)

        zp = z_ref[...].astype(F32)
        z, th = _gelu_parts(zp)
        xhat, rstd = _layernorm(z[:, W:])
        vln = (xhat * g_ref[...] + b_ref[...]).astype(BF16)
        d = d_ref[...].astype(F32)
        lane = lax.broadcasted_iota(jnp.int32, (1, LANES), 1)
        dbs = jnp.zeros((BLK, LANES), F32)
        dgel = _gelu_grad(zp, th)
        for gi in range(ng):
            cs = slice(gi * LANES, (gi + 1) * LANES)
            wb = ws_ref[gi].astype(BF16)
            s = _dot(wb, vln[:, cs]) + bs_ref[:, gi:gi + 1]
            dz_ref[:, cs] = (d[:, cs] * s * dgel[:, cs]).astype(BF16)
            ds = d[:, cs] * z[:, cs]
            dbs = dbs + jnp.where(lane == gi, jnp.sum(ds, axis=-1, keepdims=True), 0.0)
            dsb = ds.astype(BF16)
            dws_ref[gi] += _dot_nt(dsb, vln[:, cs])
            dv_scr[:, cs] = _dot_tn(wb, dsb)
        dbs_ref[...] += dbs
        dvln = dv_scr[...]
        dg_ref[...] += _colsum(dvln * xhat)
        db_ref[...] += _colsum(dvln)
        dxh = dvln * g_ref[...]
        dv = rstd * (dxh - jnp.mean(dxh, axis=-1, keepdims=True) - xhat * jnp.mean(dxh * xhat, axis=-1, keepdims=True))
        dz_ref[:, W:] = (dv * dgel[:, W:]).astype(BF16)

    vec = _full((1, W))
    return pl.pallas_call(
        body, name=name, grid=(L // BLK,),
        in_specs=[pl.BlockSpec((BLK, W2), lambda n: (n, 0)), pl.BlockSpec((BLK, W), lambda n: (n, 0)), vec, vec,
                  _full((ng, LANES, LANES)), _full((BLK, ng))],
        out_specs=[pl.BlockSpec((BLK, W2), lambda n: (n, 0)), _full((ng, LANES, LANES)), _full((BLK, LANES)), vec, vec],
        out_shape=[jax.ShapeDtypeStruct((L, W2), BF16), jax.ShapeDtypeStruct((ng, LANES, LANES), F32),
                   jax.ShapeDtypeStruct((BLK, LANES), F32), jax.ShapeDtypeStruct((1, W), F32), jax.ShapeDtypeStruct((1, W), F32)],
        scratch_shapes=[pltpu.VMEM((BLK, W), F32)],
        compiler_params=_cp(("arbitrary",)),
    )(z1, dus, ln_g, ln_b, ws, bst)


def _adamw_math(w, m, v, g):
    m_ = ADAM_B1 * m + (1.0 - ADAM_B1) * g
    v_ = ADAM_B2 * v + (1.0 - ADAM_B2) * (g * g)
    return -ADAM_LR * ((m_ / BC1) / (jnp.sqrt(v_ / BC2) + ADAM_EPS) + ADAM_WD * w), m_, v_


def adamw(w, m, v, gparts, *, tr, name):
    NL, R, Wd = w.shape
    nr = R // tr

    def body(w_ref, m_ref, v_ref, *rest):
        gp_refs, (g_ref, d_ref, nm_ref, nv_ref) = rest[:NL], rest[NL:]
        for l in range(NL):
            @pl.when(pl.program_id(0) == l)
            def _():
                g = gp_refs[l][0].astype(F32)
                for s in range(1, gp_refs[l].shape[0]):
                    g = g + gp_refs[l][s].astype(F32)
                g_ref[...] = g
                d_ref[...], nm_ref[...], nv_ref[...] = _adamw_math(w_ref[...], m_ref[...], v_ref[...], g)

    row = pl.BlockSpec((None, tr, Wd), lambda l, i: (l, i, 0))
    gspecs = [pl.BlockSpec((gparts[l].shape[0], tr, Wd), (lambda l_, i, l=l: (0, jnp.clip(i + (l_ - l) * nr, 0, nr - 1), 0)))
              for l in range(NL)]
    return pl.pallas_call(
        body, name=name, grid=(NL, nr),
        in_specs=[row, row, row] + gspecs, out_specs=[row] * 4, out_shape=[jax.ShapeDtypeStruct((NL, R, Wd), F32)] * 4,
        compiler_params=_cp(("arbitrary", "arbitrary")),
    )(w, m, v, *gparts)


def small_update(gpacks, me, params, loss_row, *, name):
    n = len(params)

    def body(me_ref, gp_ref, *refs):
        ins, outs, gs_ref = refs[:3 * n], refs[3 * n:-1], refs[-1]
        gs_ref[...] = gp_ref[0].astype(F32)
        for dv in range(1, N_DEV):
            gs_ref[...] += gp_ref[dv].astype(F32)
        for p, (w, _, _, off, per_dev) in enumerate(params):
            w_ref, m_ref, v_ref = ins[3 * p:3 * p + 3]
            g_ref, d_ref, nm_ref, nv_ref = outs[4 * p:4 * p + 4]
            rows, cols = w.shape
            if cols == LANES and rows % 8 == 0 and not per_dev:
                g = gs_ref[off:off + rows, :]
                g_ref[...] = g
                d_ref[...], nm_ref[...], nv_ref[...] = _adamw_math(w_ref[...], m_ref[...], v_ref[...], g)
                continue
            chunks = -(-cols // LANES)
            base = off + me_ref[0] * per_dev if per_dev else off
            for i in range(rows):
                for j in range(chunks):
                    wd = min(LANES, cols - j * LANES)
                    at = (slice(i, i + 1), slice(j * LANES, j * LANES + wd))
                    g = gs_ref[pl.ds(base + i * chunks + j, 1), 0:wd]
                    g_ref[at] = g
                    d_ref[at], nm_ref[at], nv_ref[at] = _adamw_math(w_ref[at], m_ref[at], v_ref[at], g)
        outs[-1][...] = jnp.sum(gs_ref[loss_row:loss_row + 1, :], axis=1, keepdims=True)

    vm = pl.BlockSpec(memory_space=pltpu.VMEM)
    flat = [a for w, m, v, _, _ in params for a in (w, m, v)]
    out_shape = [jax.ShapeDtypeStruct(w.shape, F32) for w, _, _, _, _ in params for _ in range(4)] + [jax.ShapeDtypeStruct((1, 1), F32)]
    return pl.pallas_call(
        body, name=name, in_specs=[pl.BlockSpec(memory_space=pltpu.SMEM)] + [vm] * (1 + len(flat)),
        out_specs=[vm] * len(out_shape), out_shape=out_shape,
        scratch_shapes=[pltpu.VMEM(gpacks.shape[1:], F32)],
        compiler_params=pltpu.CompilerParams(vmem_limit_bytes=VMEM_LIMIT),
    )(me, gpacks, *flat)


def ada_fwd_mm(cs, w_ada, b_loc, *, name):
    R, D = cs.shape
    nl, _, n = w_ada.shape

    def body(c_ref, w_ref, b_ref, s_ref, m_ref):
        c = c_ref[...]
        s = c * jax.nn.sigmoid(c)
        s_ref[...] = s
        for i in range(nl):
            m_ref[i] = _dot(s.astype(BF16), w_ref[i].astype(BF16)) + b_ref[i:i + 1, :]

    return pl.pallas_call(
        body, name=name, in_specs=[_full((R, D)), _full((nl, D, n)), _full((nl, n))],
        out_specs=[_full((R, D)), _full((nl, R, n))], grid=(1,),
        out_shape=[jax.ShapeDtypeStruct((R, D), F32), jax.ShapeDtypeStruct((nl, R, n), F32)],
        compiler_params=_cp(("arbitrary",)),
    )(cs, w_ada, b_loc)


def ada_bwd_mm(s, c_ctx, dall, w_ada, *, name):
    R, D = s.shape
    nl, _, n = w_ada.shape

    def body(s_ref, cc_ref, d_ref, w_ref, gw_ref, dcc_ref):
        sb = s_ref[...].astype(BF16)
        row = lax.broadcasted_iota(jnp.int32, (R, 1), 0)
        dctx = d_ref[0, 1:2, :]
        for dv in range(1, N_DEV):
            dctx = dctx + d_ref[dv, 1:2, :]
        for i in range(nl):
            dm = jnp.zeros((R, n), F32)
            for dv in range(N_DEV):
                dm = dm + jnp.where(row == dv, d_ref[dv, 2 * i:2 * i + 1, :], 0.0)
            if i == 0:
                dm = dm + jnp.where(row == N_DEV, dctx, 0.0)
            gw_ref[i] = _dot_tn(sb, dm.astype(BF16))
        cc = cc_ref[...]
        sg = jax.nn.sigmoid(cc)
        ds = _dot_nt(jnp.broadcast_to(dctx, (8, n)).astype(BF16), w_ref[0].astype(BF16))
        dcc_ref[...] = ds * (sg * (1.0 + cc * (1.0 - sg)))

    return pl.pallas_call(
        body, name=name, grid=(1,),
        in_specs=[_full((R, D)), _full((1, D)), _full((N_DEV, 3, n)), _full((nl, D, n))],
        out_specs=[_full((nl, D, n)), _full((8, D))],
        out_shape=[jax.ShapeDtypeStruct((nl, D, n), F32), jax.ShapeDtypeStruct((8, D), F32)],
        compiler_params=_cp(("arbitrary",)),
    )(s, c_ctx, dall, w_ada)


def _place():
    x, y, c = lax.axis_index("x"), lax.axis_index("y"), lax.axis_index("c")
    return x, y, c


def _lin(p):
    return 4 * p[0] + 2 * p[1] + p[2]


def all_gather_small(xb, *, name):
    R, W = xb.shape

    def body(x_ref, out_ref, send_sems, recv_sems, local_sem):
        x, y, c = _place()
        me = _lin((x, y, c))
        mine = pltpu.make_async_copy(x_ref, out_ref.at[me], local_sem)
        mine.start()
        copies = []
        for k in range(1, N_DEV):
            peer = (x ^ (k >> 2), y ^ ((k >> 1) & 1), c ^ (k & 1))
            mk = lambda dst, k=k, peer=peer: pltpu.make_async_remote_copy(
                src_ref=x_ref, dst_ref=dst, send_sem=send_sems.at[k - 1], recv_sem=recv_sems.at[k - 1], device_id=peer, device_id_type=MESH)
            mk(out_ref.at[me]).start()
            copies.append(mk(out_ref.at[_lin(peer)]))
        for cp in copies:
            cp.wait_recv()
        for cp in copies:
            cp.wait_send()
        mine.wait()

    vm = pl.BlockSpec(memory_space=pltpu.VMEM)
    return pl.pallas_call(
        body, name=name, in_specs=[vm], out_specs=vm, out_shape=jax.ShapeDtypeStruct((N_DEV, R, W), xb.dtype),
        scratch_shapes=[pltpu.SemaphoreType.DMA((7,)), pltpu.SemaphoreType.DMA((7,)), pltpu.SemaphoreType.DMA],
        compiler_params=pltpu.CompilerParams(vmem_limit_bytes=VMEM_LIMIT),
    )(xb)


HBM_SPEC = pl.BlockSpec(memory_space=pltpu.HBM)
SEM_SPEC = pl.BlockSpec(memory_space=pltpu.SEMAPHORE)
ORDERED_EFFECT = pltpu.SideEffectType.DATAFLOW_SIDE_EFFECTING


def _exchange_copies(srcs, lands, sems, scatter):
    x, y, c = _place()
    me = _lin((x, y, c))
    for j in range(len(srcs)):
        r = lands[j].shape[0] // N_DEV
        block = lambda d, j=j, r=r: pl.ds(pl.multiple_of(d * r, 16), r)
        for k in range(1, N_DEV):
            peer = (x ^ (k >> 2), y ^ ((k >> 1) & 1), c ^ (k & 1))
            src = srcs[j].at[block(_lin(peer)), :] if scatter else srcs[j]
            mk = lambda dst, j=j, k=k, peer=peer, src=src: pltpu.make_async_remote_copy(
                src_ref=src, dst_ref=dst, send_sem=sems[2 * j].at[k - 1], recv_sem=sems[2 * j + 1].at[k - 1],
                device_id=peer, device_id_type=MESH)
            yield mk(lands[j].at[block(me), :]), mk(lands[j].at[block(_lin(peer)), :])


def exchange_start(srcs, lands, *, scatter, name):
    nw = len(srcs)

    def body(*refs):
        for start, _ in _exchange_copies(refs[:nw], refs[nw:2 * nw], refs[2 * nw:4 * nw], scatter):
            start.start()
        refs[-1][...] = jnp.zeros_like(refs[-1])

    thru = [pltpu.HBM(a.shape, a.dtype) for a in (*srcs, *lands)]
    res = pl.pallas_call(
        body, name=name, in_specs=[HBM_SPEC] * (2 * nw),
        out_specs=[SEM_SPEC] * (2 * nw) + [HBM_SPEC] * (2 * nw) + [pl.BlockSpec(memory_space=pltpu.VMEM)],
        out_shape=[pltpu.SemaphoreType.DMA((N_DEV - 1,))] * (2 * nw) + thru + [jax.ShapeDtypeStruct((8, LANES), F32)],
        input_output_aliases={i: 2 * nw + i for i in range(2 * nw)},
        compiler_params=pltpu.CompilerParams(has_side_effects=ORDERED_EFFECT),
    )(*[pltpu.with_memory_space_constraint(a, pltpu.HBM) for a in (*srcs, *lands)])
    return res[:2 * nw], res[2 * nw:3 * nw], res[3 * nw:4 * nw], res[-1]


def exchange_wait(srcs, lands, sems, after, *, scatter, name):
    nw = len(srcs)

    def body(*refs):
        for _, arrive in _exchange_copies(refs[:nw], refs[nw:2 * nw], refs[2 * nw:4 * nw], scatter):
            arrive.wait_send()
            arrive.wait_recv()

    res = pl.pallas_call(
        body, name=name, in_specs=[HBM_SPEC] * (2 * nw) + [SEM_SPEC] * (2 * nw) + [pl.BlockSpec(memory_space=pl.ANY)],
        out_specs=[HBM_SPEC] * (2 * nw), out_shape=[pltpu.HBM(a.shape, a.dtype) for a in (*srcs, *lands)],
        input_output_aliases={i: i for i in range(2 * nw)},
        compiler_params=pltpu.CompilerParams(has_side_effects=ORDERED_EFFECT),
    )(*srcs, *lands, *sems, after)
    return res[nw:]


def place_own(srcs, rows, me, *, scatter, name):
    nw = len(srcs)
    lands = [lax.empty((N_DEV * r, s_.shape[1]), s_.dtype) for r, s_ in zip(rows, srcs)]

    def body(me_ref, *refs):
        for j in range(nw):
            refs[2 * nw + j][...] = refs[j][...]

    mine = lambda i, me_ref: (me_ref[0], 0)
    src_at = mine if scatter else (lambda i, me_ref: (0, 0))
    blocks = [(r, s_.shape[1]) for r, s_ in zip(rows, srcs)]
    return pl.pallas_call(
        body, name=name,
        grid_spec=pltpu.PrefetchScalarGridSpec(
            num_scalar_prefetch=1, grid=(1,),
            in_specs=[pl.BlockSpec(b_, src_at) for b_ in blocks] + [pl.BlockSpec(memory_space=pl.ANY)] * nw,
            out_specs=[pl.BlockSpec(b_, mine) for b_ in blocks]),
        out_shape=[jax.ShapeDtypeStruct(l_.shape, l_.dtype) for l_ in lands],
        input_output_aliases={1 + nw + j: j for j in range(nw)},
        compiler_params=_cp(("arbitrary",)),
    )(jnp.reshape(me, (1,)).astype(jnp.int32), *srcs, *lands)


def _rope_tables(L):
    t = jnp.arange(L)
    inv = ROPE_BASE ** (-jnp.arange(ROPE_FREQS, dtype=F32) / ROPE_FREQS)
    ar = (t // GRID_W).astype(F32)[:, None] * inv
    ac = (t % GRID_W).astype(F32)[:, None] * inv
    z = jnp.zeros_like(ar)
    cos = jnp.concatenate([jnp.cos(ar), jnp.cos(ar), jnp.cos(ac), jnp.cos(ac)], axis=1)
    sa = jnp.concatenate([-jnp.sin(ar), z, -jnp.sin(ac), z], axis=1)
    sb = jnp.concatenate([z, jnp.sin(ar), z, jnp.sin(ac)], axis=1)
    return tuple(jnp.tile(a, (1, LANES // HEAD_DIM)) for a in (cos, sa, sb))


def _nat2d(a):
    return a.reshape(1, -1) if a.ndim == 1 else a.reshape(-1, a.shape[-1])


def _pack_rows(a):
    rows, cols = a.shape
    chunks = -(-cols // LANES)
    f = jnp.pad(a, ((0, 0), (0, chunks * LANES - cols))).reshape(rows * chunks, LANES)
    return jnp.pad(f, ((0, -f.shape[0] % 8), (0, 0)))


def _rows128(a):
    f = a.reshape(-1)
    n = -(-f.shape[0] // (8 * LANES)) * 8 * LANES
    return jnp.pad(f, (0, n - f.shape[0])).reshape(-1, LANES)


def kernel(x, c, ctx, c_ctx, w_ada, b_ada, g_mix_pre, g_mix_post, g_ffn_pre, g_ffn_post, w_in_even, w_pool, pool_scale, attn_sink, w_out_even, w_in_odd, sgu_ln_g, sgu_ln_b, sgu_w, sgu_b, w_out_odd, w_ffn_up, ffn_conv_w, ffn_conv_b, w_ffn_down, loss_target, m_c_ctx, m_w_ada, m_b_ada, m_g_mix_pre, m_g_mix_post, m_g_ffn_pre, m_g_ffn_post, m_w_in_even, m_w_pool, m_pool_scale, m_attn_sink, m_w_out_even, m_w_in_odd, m_sgu_ln_g, m_sgu_ln_b, m_sgu_w, m_sgu_b, m_w_out_odd, m_w_ffn_up, m_ffn_conv_w, m_ffn_conv_b, m_w_ffn_down, v_c_ctx, v_w_ada, v_b_ada, v_g_mix_pre, v_g_mix_post, v_g_ffn_pre, v_g_ffn_post, v_w_in_even, v_w_pool, v_pool_scale, v_attn_sink, v_w_out_even, v_w_in_odd, v_sgu_ln_g, v_sgu_ln_b, v_sgu_w, v_sgu_b, v_w_out_odd, v_w_ffn_up, v_ffn_conv_w, v_ffn_conv_b, v_w_ffn_down):
    P = dict(c_ctx=c_ctx, w_ada=w_ada, b_ada=b_ada, g_mix_pre=g_mix_pre, g_mix_post=g_mix_post, g_ffn_pre=g_ffn_pre,
             g_ffn_post=g_ffn_post, w_in_even=w_in_even, w_pool=w_pool, pool_scale=pool_scale, attn_sink=attn_sink,
             w_out_even=w_out_even, w_in_odd=w_in_odd, sgu_ln_g=sgu_ln_g, sgu_ln_b=sgu_ln_b, sgu_w=sgu_w, sgu_b=sgu_b,
             w_out_odd=w_out_odd, w_ffn_up=w_ffn_up, ffn_conv_w=ffn_conv_w, ffn_conv_b=ffn_conv_b, w_ffn_down=w_ffn_down)
    M = dict(c_ctx=m_c_ctx, w_ada=m_w_ada, b_ada=m_b_ada, g_mix_pre=m_g_mix_pre, g_mix_post=m_g_mix_post, g_ffn_pre=m_g_ffn_pre,
             g_ffn_post=m_g_ffn_post, w_in_even=m_w_in_even, w_pool=m_w_pool, pool_scale=m_pool_scale, attn_sink=m_attn_sink,
             w_out_even=m_w_out_even, w_in_odd=m_w_in_odd, sgu_ln_g=m_sgu_ln_g, sgu_ln_b=m_sgu_ln_b, sgu_w=m_sgu_w, sgu_b=m_sgu_b,
             w_out_odd=m_w_out_odd, w_ffn_up=m_w_ffn_up, ffn_conv_w=m_ffn_conv_w, ffn_conv_b=m_ffn_conv_b, w_ffn_down=m_w_ffn_down)
    V = dict(c_ctx=v_c_ctx, w_ada=v_w_ada, b_ada=v_b_ada, g_mix_pre=v_g_mix_pre, g_mix_post=v_g_mix_post, g_ffn_pre=v_g_ffn_pre,
             g_ffn_post=v_g_ffn_post, w_in_even=v_w_in_even, w_pool=v_w_pool, pool_scale=v_pool_scale, attn_sink=v_attn_sink,
             w_out_even=v_w_out_even, w_in_odd=v_w_in_odd, sgu_ln_g=v_sgu_ln_g, sgu_ln_b=v_sgu_ln_b, sgu_w=v_sgu_w, sgu_b=v_sgu_b,
             w_out_odd=v_w_out_odd, w_ffn_up=v_w_ffn_up, ffn_conv_w=v_ffn_conv_w, ffn_conv_b=v_ffn_conv_b, w_ffn_down=v_w_ffn_down)

    x = x[0]
    ctx = ctx[0]
    target = loss_target[0]
    L, D = x.shape
    C = ctx.shape[0]
    tm = min(512, L)
    tm_up = min(1024, L)
    conv_rows = min(512, L)
    me = 4 * lax.axis_index("x") + 2 * lax.axis_index("y") + lax.axis_index("c")
    n_ada = w_ada.shape[2]
    F = w_ffn_down.shape[1] * N_DEV
    half_f = F // 2

    n_cw = ffn_conv_w.shape[2]
    small = jnp.concatenate([_rows128(c), _rows128(sgu_ln_g), _rows128(sgu_ln_b), _rows128(ffn_conv_w)], axis=0)
    small_all = all_gather_small(small, name="gather_small_inputs")
    c_all = small_all[:, :8].reshape(N_DEV, D)
    ln_g = small_all[:, 8].reshape(1, D)
    ln_b = small_all[:, 16].reshape(1, D)
    conv_w = small_all[:, 24:].reshape(N_DEV, -1)[:, :2 * 3 * n_cw].reshape(N_DEV, 2, 3, n_cw)
    conv_w = conv_w.transpose(1, 2, 0, 3).reshape(2, 3, 2 * F)

    cs = jnp.concatenate([c_all, c_ctx[None, :], jnp.zeros((7, D), F32)], axis=0)
    b_loc = lax.dynamic_slice(b_ada, (0, me * n_ada), (2, n_ada))
    silu_c, mods_loc = ada_fwd_mm(cs, w_ada, b_loc, name="ada_fwd")
    mods_all = all_gather_small(mods_loc.reshape(-1, LANES), name="gather_mods")

    shards = [s.astype(BF16) for s in (w_in_even[0].T, w_out_even[0], w_ffn_up[0].T, w_ffn_down[0],
                                       w_in_odd[0].T, w_out_odd[0], w_ffn_up[1].T, w_ffn_down[1])]
    shards, mods_all = lax.optimization_barrier((shards, mods_all))
    w_sems, w_srcs, w_lands, _ = exchange_start(shards, place_own(shards, [s.shape[0] for s in shards], me, scatter=False, name="gather_own"),
                                              scatter=False, name="gather_start")

    def weight(j, after):
        return exchange_wait([w_srcs[j]], [w_lands[j]], w_sems[2 * j:2 * j + 2], after, scatter=False, name=f"gather_wait_{j}")[0]

    mods_all = mods_all.reshape(N_DEV, 2, 16, n_ada).transpose(1, 2, 0, 3).reshape(2, 16, 6 * D)
    mod = lambda i, row: [m_[None, :] for m_ in jnp.split(lax.dynamic_index_in_dim(mods_all[i], row, 0, False), 6)]
    sh_m, sc_m, gt_m, sh_f, sc_f, gt_f = zip(mod(0, me), mod(1, me))
    csh_m, csc_m = mod(0, N_DEV)[:2]

    row = lambda a, i: a[i][None, :]

    cos, sa, sb = _rope_tables(L)
    sink = attn_sink[0]
    bst = sgu_b[0].T
    wup, wdn = [None, None], [None, None]

    def ffn_fwd(i, xin):
        wup[i] = weight(2 + 4 * i, xin)
        h, hu = pre_mm(xin, row(g_ffn_pre, i), sh_f[i], sc_f[i], wup[i], tm=tm_up, tn=half_f, name=f"ffn_up_{i}")
        a, s1, s2 = conv_fwd(hu, conv_w[i], ffn_conv_b[i][None, :], rows=conv_rows, wblk=2 * LANES, name=f"ffn_conv_{i}")
        wdn[i] = weight(3 + 4 * i, a)
        res = mm_post([a], wdn[i], xin, row(g_ffn_post, i), gt_f[i], tm=tm, target=target if i == 1 else None, name=f"ffn_down_{i}")
        return (h, (hu, s1, s2), a, *res)

    first_mod, cos, sa, sb = lax.optimization_barrier((sh_m[0], cos, sa, sb))
    win_e = permute_heads(weight(0, first_mod))
    h0, u, q, kv = inproj_even(x, row(g_mix_pre, 0), sh_m[0], sc_m[0], win_e, cos, sa, sb, tm=tm, name="in_even")
    hc, kvc = pre_mm(ctx, row(g_mix_pre, 0), csh_m, csc_m, win_e, tm=C, tn=2 * LANES, w_row_off=8 * LANES, name="in_even_ctx")
    pa = [pool_fwd(u, w_pool[0], pool_scale, name="pool_fwd"), attn_fwd(q, kv, kvc, sink, name="attn_fwd")]
    wout_e = permute_heads(weight(1, pa[1]))
    y0, x1 = mm_post(pa, wout_e, x, row(g_mix_post, 0), gt_m[0], tm=tm, name="out_even")
    h1, hu0, a0, f0, x2 = ffn_fwd(0, x1)
    win_o = weight(4, x2)
    h2, z1 = pre_mm(x2, row(g_mix_pre, 1), sh_m[1], sc_m[1], win_o, tm=tm_up, tn=D, name="in_odd")
    us = sgu_fwd(z1, ln_g, ln_b, sgu_w[0], bst, name="sgu_fwd")
    wout_o = weight(5, us)
    y1, x3 = mm_post([us], wout_o, x2, row(g_mix_post, 1), gt_m[1], tm=tm, name="out_odd")
    h3, hu1, a1, f1, dx4, loss_part = ffn_fwd(1, x3)

    g_srcs, g_lands, g_sems = [], [], []

    def scatter(grads, nm):
        own = place_own(grads, [g.shape[0] // N_DEV for g in grads], me, scatter=True, name=nm.replace("start", "own"))
        sems, srcs, lands, tok = exchange_start(grads, own, scatter=True, name=nm)
        g_srcs.extend(srcs)
        g_lands.extend(lands)
        g_sems.extend(sems)
        return tok[0:1, 0:1]

    def ffn_bwd(i, dxo, xin, h, hu, a, f, g_post):
        dyf, da, dg_post, dgt = post_bwd_mm(dxo, f, g_post, gt_f[i], wdn[i], tm=tm, name=f"ffn_down_bwd_{i}")
        dhg, dhu, dcwg, dcwu, dcbg, dcbu = conv_bwd(da, hu[1], hu[2], hu[0], conv_w[i], rows=conv_rows, wblk=2 * LANES,
                                                    name=f"ffn_conv_bwd_{i}")
        dxin, dg_pre, dsh, dsc = mm_pre_bwd([dhg, dhu], wup[i], xin, dxo, row(g_ffn_pre, i), sc_f[i], tm=tm, tk=half_f,
                                            name=f"ffn_up_bwd_{i}")
        g_dn = wgrad([a], dyf, tr=2 * LANES, name=f"wgrad_down_{i}")
        g_up = wgrad([dhg, dhu], h, tr=2 * LANES, name=f"wgrad_up_{i}")
        tok = scatter([g_dn, g_up], f"scatter_start_ffn_{i}")
        return dxin, tok, dict(g_ffn_post=dg_post, g_ffn_pre=dg_pre, gt_f=dgt, sh_f=dsh, sc_f=dsc,
                               ffn_conv_w=jnp.concatenate([dcwg, dcwu], axis=1), ffn_conv_b=jnp.concatenate([dcbg, dcbu], axis=1)[0])

    dx3, tok, sf1 = ffn_bwd(1, dx4, x3, h3, hu1, a1, f1, row(g_ffn_post, 1))
    dy1, dus, dg_mpost1, dgt_m1 = post_bwd_mm(dx3, y1, row(g_mix_post, 1) + tok, gt_m[1], wout_o, tm=tm, name="out_odd_bwd")
    dz1, dws, dbs, dlng, dlnb = sgu_bwd(z1, dus, ln_g, ln_b, sgu_w[0], bst, name="sgu_bwd")
    dx2, dg_mpre1, dsh_m1, dsc_m1 = mm_pre_bwd([dz1], win_o, x2, dx3, row(g_mix_pre, 1), sc_m[1], tm=tm, tk=D, name="in_odd_bwd")
    tok = scatter([wgrad([us], dy1, tr=2 * LANES, name="wgrad_out_odd"), wgrad([dz1], h2, tr=2 * LANES, name="wgrad_in_odd")],
                  "scatter_start_mix_1")

    dx1, tok, sf0 = ffn_bwd(0, dx2, x1, h1, hu0, a0, f0, row(g_ffn_post, 0) + tok)
    dy0, dpa, dg_mpost0, dgt_m0 = post_bwd_mm(dx1, y0, row(g_mix_post, 0) + tok, gt_m[0], wout_e, tm=tm, name="out_even_bwd")
    tok = scatter([permute_heads(wgrad(pa, dy0, tr=2 * LANES, name="wgrad_out_even"), inverse=True)], "scatter_start_out_0")
    du, dwp, dps = pool_bwd(u, dpa, w_pool[0], pool_scale + tok, name="pool_bwd")
    dq, dkv, dkvc, dsink = attn_bwd(q, kv, kvc, sink, dpa, cos, sa, sb, name="attn_bwd")
    dz0 = jnp.concatenate([du, dq, dkv], axis=1)
    dzc = jnp.concatenate([jnp.zeros((C, 8 * LANES), BF16), dkvc], axis=1)
    tok = scatter([permute_heads(wgrad([dz0], h0, tr=2 * LANES, extra=(dzc, hc), name="wgrad_in_even"), inverse=True)],
                  "scatter_start_in_0")
    grad_x, dg_mpre0, dsh_m0, dsc_m0 = mm_pre_bwd([dz0], win_e, x, dx1, row(g_mix_pre, 0) + tok, sc_m[0], tm=tm, tk=dz0.shape[1],
                                                  name="in_even_bwd")
    _, dg_mpre0c, dcsh, dcsc = mm_pre_bwd([dkvc], win_e, ctx, None, row(g_mix_pre, 0), csc_m, tm=C, tk=2 * LANES,
                                          w_row_off=8 * LANES, name="in_even_ctx_bwd")

    out = {}

    def update(name, lands, transposed):
        w_, m_, v_ = (a.transpose(0, 2, 1) if transposed else a for a in (P[name], M[name], V[name]))
        r = w_.shape[1]
        tr = r // 4 if r % 64 == 0 and r > 256 else r
        res = adamw(w_, m_, v_, [l_.reshape(N_DEV, r, l_.shape[1]) for l_ in lands], tr=tr, name=f"adamw_{name}")
        for kind, val in zip(("grad", "delta", "new_m", "new_v"), res):
            out[(kind, name)] = val.transpose(0, 2, 1) if transposed else val

    zero = jnp.zeros((1, D), F32)
    dmod0 = jnp.concatenate([dsh_m0, dsc_m0, dgt_m0, sf0["sh_f"], sf0["sc_f"], sf0["gt_f"]], axis=1)
    dmodc = jnp.concatenate([dcsh, dcsc, zero, zero, zero, zero], axis=1)
    dmod1 = jnp.concatenate([dsh_m1, dsc_m1, dgt_m1, sf1["sh_f"], sf1["sc_f"], sf1["gt_f"]], axis=1)
    dmods = jnp.concatenate([dmod0, dmodc, dmod1], axis=0)
    dm = dmods.reshape(-1, LANES).astype(BF16)
    d_sems, d_srcs, d_lands, d_tok = exchange_start(
        [dm], place_own([dm], [dm.shape[0]], me, scatter=False, name="dmods_own"), scatter=False, name="dmods_start")
    slots = exchange_wait(g_srcs[:6], g_lands[:6], g_sems[:12], d_tok, scatter=True, name="scatter_wait_early")
    update("w_ffn_up", [slots[5], slots[1]], True)
    update("w_ffn_down", [slots[4], slots[0]], False)
    update("w_in_odd", [slots[3]], True)
    update("w_out_odd", [slots[2]], False)
    dmods_all = exchange_wait(d_srcs, d_lands, d_sems, out[("new_v", "w_out_odd")], scatter=False, name="dmods_wait")[0]
    dall = lax.dynamic_index_in_dim(dmods_all.astype(F32).reshape(N_DEV, 3, N_DEV, n_ada), me, 2, False)
    g_w_ada, dcc = ada_bwd_mm(silu_c, c_ctx[None, :], dall, w_ada, name="ada_bwd")

    rep = dict(
        c_ctx=dcc[0:1],
        b_ada=jnp.concatenate([dmod0 + dmodc, dmod1]),
        g_mix_pre=jnp.concatenate([dg_mpre0 + dg_mpre0c, dg_mpre1]),
        g_mix_post=jnp.concatenate([dg_mpost0, dg_mpost1]),
        g_ffn_pre=jnp.concatenate([sf0["g_ffn_pre"], sf1["g_ffn_pre"]]),
        g_ffn_post=jnp.concatenate([sf0["g_ffn_post"], sf1["g_ffn_post"]]),
        w_pool=_nat2d(dwp), pool_scale=dps, attn_sink=dsink[:, :N_Q_HEADS],
        sgu_w=_nat2d(dws), sgu_b=dbs[:, :sgu_b.shape[1]].T,
        ffn_conv_b=jnp.stack([sf0["ffn_conv_b"], sf1["ffn_conv_b"]]),
    )
    hi = loss_part.astype(BF16).astype(F32)
    mid = (loss_part - hi).astype(BF16).astype(F32)
    loss_piece = jnp.pad(jnp.concatenate([hi, mid, loss_part - hi - mid], axis=1), ((0, 7), (0, LANES - 3)))
    conv_g = jnp.stack([sf0["ffn_conv_w"], sf1["ffn_conv_w"]]).reshape(2 * 3, N_DEV, n_cw).swapaxes(0, 1)
    shard_full = dict(sgu_ln_g=dlng.reshape(N_DEV, LANES), sgu_ln_b=dlnb.reshape(N_DEV, LANES),
                      ffn_conv_w=jnp.concatenate([_pack_rows(conv_g[d]) for d in range(N_DEV)], axis=0))
    small_names = list(rep) + list(shard_full)
    pieces = [_pack_rows(rep[k]) for k in rep] + list(shard_full.values()) + [loss_piece]
    sizes = [p.shape[0] for p in pieces]
    offs = [sum(sizes[:i]) for i in range(len(sizes))]
    pieces.append(jnp.zeros((-sum(sizes) % 16, LANES), F32))
    gpack = jnp.concatenate(pieces, axis=0).astype(BF16)
    own = place_own([gpack], [gpack.shape[0]], me, scatter=False, name="smallgrad_own")
    s_sems, s_srcs, s_lands, small_tok = exchange_start([gpack], own, scatter=False, name="smallgrad_start")

    slots = exchange_wait(g_srcs[6:], g_lands[6:], g_sems[12:], small_tok, scatter=True, name="scatter_wait_late")
    update("w_in_even", [slots[1]], True)
    update("w_out_even", [slots[0]], False)
    res = adamw(w_ada, m_w_ada, v_w_ada, [g_w_ada[l][None] for l in range(w_ada.shape[0])], tr=D // 4, name="adamw_w_ada")
    for kind, val in zip(("grad", "delta", "new_m", "new_v"), res):
        out[(kind, "w_ada")] = val

    gpacks = exchange_wait(s_srcs, s_lands, s_sems, out[("new_v", "w_ada")], scatter=False, name="smallgrad_wait")[0]
    per_dev = {k: shard_full[k].shape[0] // N_DEV for k in shard_full}
    params = [(_nat2d(P[k]), _nat2d(M[k]), _nat2d(V[k]), offs[i], per_dev.get(k, 0)) for i, k in enumerate(small_names)]
    res = small_update(gpacks.reshape(N_DEV, -1, LANES), jnp.reshape(me, (1,)).astype(jnp.int32), params, offs[-1], name="adamw_small")
    for i, k in enumerate(small_names):
        for kind, val in zip(("grad", "delta", "new_m", "new_v"), res[4 * i:4 * i + 4]):
            out[(kind, k)] = val.reshape(P[k].shape)
    loss = res[-1][0, 0]

    names = list(P)
    final = [loss, grad_x[None]]
    for kind in ("grad", "delta", "new_m", "new_v"):
        for k in names:
            val = out[(kind, k)]
            final.append(val)
    return tuple(final)
```

```python
import functools
import math

import jax
import jax.numpy as jnp
from jax import lax
from jax.experimental import pallas as pl
from jax.experimental.pallas import tpu as pltpu

F32 = jnp.float32
BF16 = jnp.bfloat16
MESH = pl.DeviceIdType.MESH
N_DEV = 8
LANES = 128
VMEM_LIMIT = 48 * 1024 * 1024
EPS = 1e-6
NEG_INF = -1e30
GRID_W = 64
WINDOW = 128
BLK = 128
HEAD_DIM = 64
N_Q_HEADS = 8
N_KV_HEADS = 2
GQA = N_Q_HEADS // N_KV_HEADS
POOL_WINDOWS = (2, 4, 8, 16)
ROPE_BASE = 10000.0
ROPE_FREQS = HEAD_DIM // 4
PAD = 16
ADAM_LR, ADAM_B1, ADAM_B2, ADAM_EPS, ADAM_WD, ADAM_STEP = 0.001, 0.9, 0.999, 1e-08, 0.01, 10
BC1 = 1.0 - ADAM_B1 ** ADAM_STEP
BC2 = 1.0 - ADAM_B2 ** ADAM_STEP
SQRT_2_OVER_PI = math.sqrt(2.0 / math.pi)
GELU_C = 0.044715


def _cp(sem=None):
    return pltpu.CompilerParams(dimension_semantics=sem, vmem_limit_bytes=VMEM_LIMIT)


def _dot(a, b):
    return jnp.dot(a, b, preferred_element_type=F32)


def _dot_nt(a, b):
    return lax.dot_general(a, b, (((1,), (1,)), ((), ())), preferred_element_type=F32)


def _dot_tn(a, b):
    return lax.dot_general(a, b, (((0,), (0,)), ((), ())), preferred_element_type=F32)


def _rms(x):
    r = lax.rsqrt(jnp.mean(x * x, axis=-1, keepdims=True) + EPS)
    return x * r, r


def _rms_bwd(dn, n, r):
    return r * (dn - n * jnp.mean(dn * n, axis=-1, keepdims=True))


def _colsum(a):
    return jnp.sum(a, axis=0, keepdims=True)


def _rope(x, c, sa, sb):
    return x * c + pltpu.roll(x, LANES - ROPE_FREQS, 1) * sa + pltpu.roll(x, ROPE_FREQS, 1) * sb


def _full(shape):
    return pl.BlockSpec(shape, lambda *_: (0,) * len(shape))


def pre_mm(x, g, sh, sc, wt, *, tm, tn, w_row_off=0, name):
    T, D = x.shape
    n_rows = wt.shape[0] - w_row_off
    off = w_row_off // tn

    def body(x_ref, g_ref, sh_ref, sc_ref, w_ref, h_ref, z_ref):
        @pl.when(pl.program_id(1) == 0)
        def _():
            n, _ = _rms(x_ref[...])
            h_ref[...] = (n * g_ref[...] * (1.0 + sc_ref[...]) + sh_ref[...]).astype(BF16)

        z_ref[...] = _dot_nt(h_ref[...], w_ref[...]).astype(BF16)

    vec = pl.BlockSpec((1, D), lambda i, j: (0, 0))
    return pl.pallas_call(
        body, name=name, grid=(T // tm, n_rows // tn),
        in_specs=[pl.BlockSpec((tm, D), lambda i, j: (i, 0)), vec, vec, vec, pl.BlockSpec((tn, D), lambda i, j: (j + off, 0))],
        out_specs=[pl.BlockSpec((tm, D), lambda i, j: (i, 0)), pl.BlockSpec((tm, tn), lambda i, j: (i, j))],
        out_shape=[jax.ShapeDtypeStruct((T, D), BF16), jax.ShapeDtypeStruct((T, n_rows), BF16)],
        compiler_params=_cp(("parallel", "arbitrary")),
    )(x, g, sh, sc, wt)


def inproj_even(x, g, sh, sc, wt, cos, sa, sb, *, tm, name):
    T, D = x.shape
    N = wt.shape[0]

    def body(x_ref, g_ref, sh_ref, sc_ref, w_ref, c_ref, sa_ref, sb_ref, h_ref, u_ref, q_ref, kv_ref):
        n, _ = _rms(x_ref[...])
        h = (n * g_ref[...] * (1.0 + sc_ref[...]) + sh_ref[...]).astype(BF16)
        h_ref[...] = h
        z = _dot_nt(h, w_ref[...])
        u_ref[...] = z[:, :4 * LANES]
        c, a, b = c_ref[...], sa_ref[...], sb_ref[...]
        for s in range(4):
            q_ref[:, s * LANES:(s + 1) * LANES] = _rope(z[:, (4 + s) * LANES:(5 + s) * LANES], c, a, b).astype(BF16)
        kv_ref[:, :LANES] = _rope(z[:, 8 * LANES:9 * LANES], c, a, b).astype(BF16)
        kv_ref[:, LANES:] = z[:, 9 * LANES:].astype(BF16)

    vec = pl.BlockSpec((1, D), lambda i: (0, 0))
    row = lambda w: pl.BlockSpec((tm, w), lambda i: (i, 0))
    return pl.pallas_call(
        body, name=name, grid=(T // tm,),
        in_specs=[row(D), vec, vec, vec, _full((N, D)), row(LANES), row(LANES), row(LANES)],
        out_specs=[row(D), row(4 * LANES), row(4 * LANES), row(2 * LANES)],
        out_shape=[jax.ShapeDtypeStruct((T, D), BF16), jax.ShapeDtypeStruct((T, 4 * LANES), F32),
                   jax.ShapeDtypeStruct((T, 4 * LANES), BF16), jax.ShapeDtypeStruct((T, 2 * LANES), BF16)],
        compiler_params=_cp(("parallel",)),
    )(x, g, sh, sc, wt, cos, sa, sb)


def mm_post(a_parts, w, x, g, gt, *, tm, target=None, name):
    T = a_parts[0].shape[0]
    D = w.shape[1]
    npart = len(a_parts)
    offs = [sum(a_.shape[1] for a_ in a_parts[:p]) for p in range(npart + 1)]
    with_loss = target is not None

    def body(*refs):
        a_refs, (w_ref, x_ref, g_ref, gt_ref) = refs[:npart], refs[npart:npart + 4]
        y = _dot(a_refs[0][...], w_ref[offs[0]:offs[1], :])
        for p in range(1, npart):
            y = y + _dot(a_refs[p][...], w_ref[offs[p]:offs[p + 1], :])
        n, _ = _rms(y)
        xn = x_ref[...] + gt_ref[...] * (n * g_ref[...])
        if not with_loss:
            y_ref, xn_ref = refs[npart + 4:]
            y_ref[...] = y.astype(BF16)
            xn_ref[...] = xn
            return
        t_ref, y_ref, d_ref, l_ref = refs[npart + 4:]
        y_ref[...] = y.astype(BF16)

        @pl.when(pl.program_id(0) == 0)
        def _():
            l_ref[...] = jnp.zeros_like(l_ref)

        e = xn - t_ref[...]
        l_ref[...] += 0.5 * jnp.sum(jnp.mean(e * e, axis=-1, keepdims=True), axis=0, keepdims=True)
        d_ref[...] = e * (1.0 / D)

    vec = pl.BlockSpec((1, D), lambda i: (0, 0))
    row = lambda w_: pl.BlockSpec((tm, w_), lambda i: (i, 0))
    in_specs = [row(a_.shape[1]) for a_ in a_parts] + [_full(w.shape), row(D), vec, vec]
    out_specs = [row(D), row(D)]
    out_shape = [jax.ShapeDtypeStruct((T, D), BF16), jax.ShapeDtypeStruct((T, D), F32)]
    if with_loss:
        in_specs.append(row(D))
        out_specs.append(_full((1, 1)))
        out_shape.append(jax.ShapeDtypeStruct((1, 1), F32))
    return pl.pallas_call(
        body, name=name, grid=(T // tm,), in_specs=in_specs, out_specs=out_specs, out_shape=out_shape,
        compiler_params=_cp(("arbitrary",) if with_loss else ("parallel",)),
    )(*a_parts, w, x, g, gt, *((target,) if with_loss else ()))


def post_bwd_mm(dxn, y, g, gt, w, *, tm, name):
    T, D = y.shape
    K = w.shape[0]

    def body(dxn_ref, y_ref, g_ref, gt_ref, w_ref, dy_ref, da_ref, dg_ref, dgt_ref):
        @pl.when(pl.program_id(0) == 0)
        def _():
            dg_ref[...] = jnp.zeros_like(dg_ref)
            dgt_ref[...] = jnp.zeros_like(dgt_ref)

        d = dxn_ref[...]
        n, r = _rms(y_ref[...].astype(F32))
        g_, gt_ = g_ref[...], gt_ref[...]
        dg_ref[...] += _colsum(d * gt_ * n)
        dgt_ref[...] += _colsum(d * g_ * n)
        dy = _rms_bwd(d * (gt_ * g_), n, r).astype(BF16)
        dy_ref[...] = dy
        da_ref[...] = _dot_nt(dy, w_ref[...]).astype(BF16)

    vec = pl.BlockSpec((1, D), lambda i: (0, 0))
    row = lambda w_: pl.BlockSpec((tm, w_), lambda i: (i, 0))
    return pl.pallas_call(
        body, name=name, grid=(T // tm,),
        in_specs=[row(D), row(D), vec, vec, _full((K, D))],
        out_specs=[row(D), row(K), vec, vec],
        out_shape=[jax.ShapeDtypeStruct((T, D), BF16), jax.ShapeDtypeStruct((T, K), BF16),
                   jax.ShapeDtypeStruct((1, D), F32), jax.ShapeDtypeStruct((1, D), F32)],
        compiler_params=_cp(("arbitrary",)),
    )(dxn, y, g, gt, w)


def mm_pre_bwd(dzs, wt, x, dres, g, sc, *, tm, tk, w_row_off=0, name):
    T, N = dzs[0].shape
    D = x.shape[1]
    nk = N // tk
    npart = len(dzs)
    off = w_row_off // tk
    has_res = dres is not None

    def body(*refs):
        dz_refs = refs[:npart]
        w_refs = refs[npart:2 * npart]
        rest = refs[2 * npart:]
        x_ref = rest[0]
        dres_ref = rest[1] if has_res else None
        g_ref, sc_ref, dx_ref, dg_ref, dsh_ref, dsc_ref, acc = rest[1 + has_res:]
        i, k = pl.program_id(0), pl.program_id(1)

        @pl.when(jnp.logical_and(i == 0, k == 0))
        def _():
            dg_ref[...] = jnp.zeros_like(dg_ref)
            dsh_ref[...] = jnp.zeros_like(dsh_ref)
            dsc_ref[...] = jnp.zeros_like(dsc_ref)

        part = _dot(dz_refs[0][...], w_refs[0][...])
        for p in range(1, npart):
            part = part + _dot(dz_refs[p][...], w_refs[p][...])

        @pl.when(k == 0)
        def _():
            acc[...] = part

        @pl.when(k > 0)
        def _():
            acc[...] += part

        @pl.when(k == nk - 1)
        def _():
            dh = acc[...]
            n, r = _rms(x_ref[...])
            g_, s1 = g_ref[...], 1.0 + sc_ref[...]
            dsh_ref[...] += _colsum(dh)
            dsc_ref[...] += _colsum(dh * n * g_)
            dg_ref[...] += _colsum(dh * s1 * n)
            dxp = _rms_bwd(dh * (g_ * s1), n, r)
            dx_ref[...] = dxp + dres_ref[...] if has_res else dxp

    vec = pl.BlockSpec((1, D), lambda i, k: (0, 0))
    row = pl.BlockSpec((tm, D), lambda i, k: (i, 0))
    w_specs = [pl.BlockSpec((tk, D), (lambda i, k, p=p: (k + off + p * nk, 0))) for p in range(npart)]
    res_specs, res_args = ([row], (dres,)) if has_res else ([], ())
    return pl.pallas_call(
        body, name=name, grid=(T // tm, nk),
        in_specs=[pl.BlockSpec((tm, tk), lambda i, k: (i, k))] * npart + w_specs + [row] + res_specs + [vec, vec],
        out_specs=[row, vec, vec, vec],
        out_shape=[jax.ShapeDtypeStruct((T, D), F32)] + [jax.ShapeDtypeStruct((1, D), F32)] * 3,
        scratch_shapes=[pltpu.VMEM((tm, D), F32)],
        compiler_params=_cp(("arbitrary", "arbitrary")),
    )(*dzs, *([wt] * npart), x, *res_args, g, sc)


def wgrad(a_parts, b, *, tr, extra=None, name):
    T, R = a_parts[0].shape
    D = b.shape[1]
    npart = len(a_parts)
    nr = R // tr

    def body(*refs):
        a_refs, b_ref = refs[:npart], refs[npart]
        g_ref = refs[-1]
        for p in range(npart):
            @pl.when(pl.program_id(0) // nr == p)
            def _():
                acc = _dot_tn(a_refs[p][...], b_ref[...])
                if extra is not None:
                    acc += _dot_tn(refs[npart + 1][...], refs[npart + 2][...])
                g_ref[...] = acc.astype(BF16)

    in_specs = [pl.BlockSpec((T, tr), (lambda r, p=p: (0, jnp.clip(r - p * nr, 0, nr - 1)))) for p in range(npart)]
    in_specs.append(_full((T, D)))
    args = [*a_parts, b]
    if extra is not None:
        a2, b2 = extra
        in_specs += [pl.BlockSpec((a2.shape[0], tr), lambda r: (0, r)), _full(b2.shape)]
        args += [a2, b2]
    return pl.pallas_call(
        body, name=name, grid=(npart * nr,),
        in_specs=in_specs, out_specs=pl.BlockSpec((tr, D), lambda r: (r, 0)),
        out_shape=jax.ShapeDtypeStruct((npart * R, D), BF16),
        compiler_params=_cp(("parallel",)),
    )(*args)


def _conv_ext(ref, r0, rows, total):
    top = ref[pl.ds(pl.multiple_of(jnp.maximum(r0 - PAD, 0), PAD), PAD), :]
    mid = ref[pl.ds(r0, rows), :]
    bot = ref[pl.ds(pl.multiple_of(jnp.minimum(r0 + rows, total - PAD), PAD), PAD), :]
    top = jnp.where(r0 > 0, top, jnp.zeros_like(top))
    bot = jnp.where(r0 + rows < total, bot, jnp.zeros_like(bot))
    return jnp.concatenate([top, mid, bot], axis=0).astype(F32)


def _shift_rows(a, k):
    return pltpu.roll(a, k % a.shape[0], 0)


def _conv3(x, w, b):
    return w[0:1] * _shift_rows(x, 1) + w[1:2] * x + w[2:3] * _shift_rows(x, -1) + b


def _gate_up_specs(rows_, wblk, nb):
    return [pl.BlockSpec((rows_, wblk), lambda j: (0, j)), pl.BlockSpec((rows_, wblk), lambda j: (0, j + nb))]


def conv_fwd(hu, cw, cb, *, rows, wblk, name):
    L, N2 = hu.shape
    nb = N2 // 2 // wblk
    nchunk = L // rows

    def body(hg_ref, hu_ref, wg_ref, wu_ref, bg_ref, bu_ref, a_ref, s1_ref, s2_ref):
        def chunk(ci, carry):
            r0 = pl.multiple_of(ci * rows, rows)
            gate = _conv3(_conv_ext(hg_ref, r0, rows, L), wg_ref[...], bg_ref[...])[PAD:PAD + rows]
            up = _conv3(_conv_ext(hu_ref, r0, rows, L), wu_ref[...], bu_ref[...])[PAD:PAD + rows]
            sg = jax.nn.sigmoid(gate)
            silu = gate * sg
            at = pl.ds(r0, rows)
            a_ref[at, :] = (silu * up).astype(BF16)
            s1_ref[at, :] = silu.astype(BF16)
            s2_ref[at, :] = (up * (sg + silu * (1.0 - sg))).astype(BF16)
            return carry

        lax.fori_loop(0, nchunk, chunk, 0)

    out = pl.BlockSpec((L, wblk), lambda j: (0, j))
    return pl.pallas_call(
        body, name=name, grid=(nb,),
        in_specs=_gate_up_specs(L, wblk, nb) + _gate_up_specs(3, wblk, nb) + _gate_up_specs(1, wblk, nb),
        out_specs=[out] * 3, out_shape=[jax.ShapeDtypeStruct((L, N2 // 2), BF16)] * 3,
        compiler_params=_cp(("parallel",)),
    )(hu, hu, cw, cw, cb, cb)


def conv_bwd(da, s1, s2, hu, cw, *, rows, wblk, name):
    L, N2 = hu.shape
    F = N2 // 2
    nb = F // wblk
    nchunk = L // rows
    mid = slice(PAD, PAD + rows)

    def body(da_ref, s1_ref, s2_ref, hg_ref, hu_ref, wg_ref, wu_ref, dg_ref, du_ref, dwg_ref, dwu_ref, dbg_ref, dbu_ref):
        for ref in (dwg_ref, dwu_ref, dbg_ref, dbu_ref):
            ref[...] = jnp.zeros_like(ref)

        def half_bwd(x_ref, dh, w_ref, dx_ref, dw_ref, db_ref, r0):
            w = w_ref[...]
            nxt, prv = _shift_rows(dh, -1)[mid], _shift_rows(dh, 1)[mid]
            dhm, xm = dh[mid], x_ref[pl.ds(r0, rows), :].astype(F32)
            dx_ref[pl.ds(r0, rows), :] = (w[0:1] * nxt + w[1:2] * dhm + w[2:3] * prv).astype(BF16)
            db_ref[...] += _colsum(dhm)
            dw_ref[0:1, :] += _colsum(nxt * xm)
            dw_ref[1:2, :] += _colsum(dhm * xm)
            dw_ref[2:3, :] += _colsum(prv * xm)

        def chunk(ci, carry):
            r0 = pl.multiple_of(ci * rows, rows)
            d = _conv_ext(da_ref, r0, rows, L)
            half_bwd(hu_ref, d * _conv_ext(s1_ref, r0, rows, L), wu_ref, du_ref, dwu_ref, dbu_ref, r0)
            half_bwd(hg_ref, d * _conv_ext(s2_ref, r0, rows, L), wg_ref, dg_ref, dwg_ref, dbg_ref, r0)
            return carry

        lax.fori_loop(0, nchunk, chunk, 0)

    blk = lambda r: pl.BlockSpec((r, wblk), lambda j: (0, j))
    return pl.pallas_call(
        body, name=name, grid=(nb,),
        in_specs=[blk(L)] * 3 + _gate_up_specs(L, wblk, nb) + _gate_up_specs(3, wblk, nb),
        out_specs=[blk(L), blk(L), blk(3), blk(3), blk(1), blk(1)],
        out_shape=[jax.ShapeDtypeStruct((L, F), BF16)] * 2 + [jax.ShapeDtypeStruct((3, F), F32)] * 2
        + [jax.ShapeDtypeStruct((1, F), F32)] * 2,
        compiler_params=_cp(("parallel",)),
    )(da, s1, s2, hu, hu, cw, cw)


def _window_sums(pad_ref, w, lead):
    a = pad_ref[...]
    k = 1
    while k < w:
        a = a + _shift_rows(a, -k)
        k *= 2
    return _shift_rows(a, lead) if lead else a


def _pool_counts(L, h):
    t = lax.broadcasted_iota(jnp.int32, (L, 1), 0)
    return (jnp.minimum(t + h, L) - jnp.maximum(t - h, 0)).astype(F32)


def _pooled(u_ref, pad_ref, L, w):
    h = w // 2
    pad_ref[pl.ds(PAD, L), :] = u_ref[...]
    win = _window_sums(pad_ref, w, h)[PAD:PAD + L]
    return win / _pool_counts(L, h) - u_ref[...]


def _zero_pad_edges(pad_ref, L):
    z = jnp.zeros((PAD, LANES), F32)
    pad_ref[pl.ds(0, PAD), :] = z
    pad_ref[pl.ds(PAD + L, PAD), :] = z


def pool_fwd(u, w_pool, pool_scale, *, name):
    L = u.shape[0]

    def body(u_ref, w_ref, ps_ref, p_ref, pad_ref):
        _zero_pad_edges(pad_ref, L)
        for gi, win in enumerate(POOL_WINDOWS):
            @pl.when(pl.program_id(0) == gi)
            def _():
                pooled = _pooled(u_ref, pad_ref, L, win)
                p_ref[...] = (_dot(pooled.astype(BF16), w_ref[...].astype(BF16)) * ps_ref[...]).astype(BF16)

    return pl.pallas_call(
        body, name=name, grid=(len(POOL_WINDOWS),),
        in_specs=[pl.BlockSpec((L, LANES), lambda gi: (0, gi)), pl.BlockSpec((None, LANES, LANES), lambda gi: (gi, 0, 0)),
                  pl.BlockSpec((1, LANES), lambda gi: (0, gi))],
        out_specs=pl.BlockSpec((L, LANES), lambda gi: (0, gi)),
        out_shape=jax.ShapeDtypeStruct((L, 4 * LANES), BF16),
        scratch_shapes=[pltpu.VMEM((L + 2 * PAD, LANES), F32)],
        compiler_params=_cp(("parallel",)),
    )(u, w_pool, pool_scale)


def pool_bwd(u, dpa, w_pool, pool_scale, *, name):
    L = u.shape[0]

    def body(u_ref, dp_ref, w_ref, ps_ref, du_ref, dw_ref, dps_ref, pad_ref):
        _zero_pad_edges(pad_ref, L)
        for gi, win in enumerate(POOL_WINDOWS):
            @pl.when(pl.program_id(0) == gi)
            def _():
                h = win // 2
                wb = w_ref[...].astype(BF16)
                pooled = _pooled(u_ref, pad_ref, L, win).astype(BF16)
                dp = dp_ref[...].astype(F32)
                dps_ref[...] = _colsum(dp * _dot(pooled, wb))
                dy = (dp * ps_ref[...]).astype(BF16)
                dw_ref[...] = _dot_tn(pooled, dy)
                dpooled = _dot_nt(dy, wb)
                pad_ref[pl.ds(PAD, L), :] = dpooled / _pool_counts(L, h)
                du_ref[...] = (_window_sums(pad_ref, win, h - 1)[PAD:PAD + L] - dpooled).astype(BF16)

    return pl.pallas_call(
        body, name=name, grid=(len(POOL_WINDOWS),),
        in_specs=[pl.BlockSpec((L, LANES), lambda gi: (0, gi)), pl.BlockSpec((L, LANES), lambda gi: (0, gi)),
                  pl.BlockSpec((None, LANES, LANES), lambda gi: (gi, 0, 0)), pl.BlockSpec((1, LANES), lambda gi: (0, gi))],
        out_specs=[pl.BlockSpec((L, LANES), lambda gi: (0, gi)), pl.BlockSpec((None, LANES, LANES), lambda gi: (gi, 0, 0)),
                   pl.BlockSpec((1, LANES), lambda gi: (0, gi))],
        out_shape=[jax.ShapeDtypeStruct((L, 4 * LANES), BF16), jax.ShapeDtypeStruct((4, LANES, LANES), F32),
                   jax.ShapeDtypeStruct((1, 4 * LANES), F32)],
        scratch_shapes=[pltpu.VMEM((L + 2 * PAD, LANES), F32)],
        compiler_params=_cp(("parallel",)),
    )(u, dpa, w_pool, pool_scale)


def _attn_probs(qk, band_k, ctx_k, sink_ref, kh, mask4):
    s_loc = jnp.where(mask4, _dot_nt(qk, band_k), NEG_INF)
    s_ctx = _dot_nt(qk, ctx_k)
    sk = jnp.concatenate([jnp.full((BLK, 1), sink_ref[kh * GQA + hh], F32) for hh in range(GQA)], axis=0)
    m = jnp.maximum(jnp.maximum(jnp.max(s_loc, axis=-1, keepdims=True), jnp.max(s_ctx, axis=-1, keepdims=True)), sk)
    e_loc, e_ctx, e_s = jnp.exp(s_loc - m), jnp.exp(s_ctx - m), jnp.exp(sk - m)
    inv = 1.0 / (jnp.sum(e_loc, axis=-1, keepdims=True) + jnp.sum(e_ctx, axis=-1, keepdims=True) + e_s)
    return e_loc * inv, e_ctx * inv, e_s * inv


def _attn_block(n, L):
    start = pl.multiple_of(jnp.clip((n - 1) * BLK, 0, L - 3 * BLK), BLK)
    qpos = n * BLK + lax.broadcasted_iota(jnp.int32, (BLK, 3 * BLK), 0)
    kpos = start + lax.broadcasted_iota(jnp.int32, (BLK, 3 * BLK), 1)
    mask = jnp.abs(kpos - qpos) <= WINDOW
    return start, jnp.concatenate([mask] * GQA, axis=0)


def _stack_slabs(ref):
    return jnp.concatenate([ref[:, s * LANES:(s + 1) * LANES] for s in range(GQA)], axis=0)


def _kv_head_lanes(kh):
    return (lax.broadcasted_iota(jnp.int32, (1, LANES), 1) // HEAD_DIM) == kh


def permute_heads(w, inverse=False):
    lo, hi = 4 * LANES, 8 * LANES
    mid = w[lo:hi].reshape(*((GQA, N_KV_HEADS) if inverse else (N_KV_HEADS, GQA)), HEAD_DIM, w.shape[1])
    return jnp.concatenate([w[:lo], mid.swapaxes(0, 1).reshape(hi - lo, w.shape[1]), w[hi:]], axis=0)


def attn_fwd(q, kv, kvc, sink, *, name):
    L = q.shape[0]
    C = kvc.shape[0]
    scale = HEAD_DIM ** -0.5

    def body(q_ref, kv_ref, kvc_ref, sink_ref, o_ref):
        start, mask4 = _attn_block(pl.program_id(0), L)
        band = kv_ref[pl.ds(start, 3 * BLK), :]
        kvc_ = kvc_ref[...]
        qs = _stack_slabs(q_ref) * scale
        o = jnp.zeros((GQA * BLK, LANES), F32)
        for kh in range(N_KV_HEADS):
            grp = _kv_head_lanes(kh)
            qk = jnp.where(grp, qs, jnp.zeros_like(qs))
            p_loc, p_ctx, _ = _attn_probs(qk, band[:, :LANES], kvc_[:, :LANES], sink_ref, kh, mask4)
            o = o + jnp.where(grp, _dot(p_loc.astype(BF16), band[:, LANES:]) + _dot(p_ctx.astype(BF16), kvc_[:, LANES:]), 0.0)
        for s in range(GQA):
            o_ref[:, s * LANES:(s + 1) * LANES] = o[s * BLK:(s + 1) * BLK].astype(BF16)

    return pl.pallas_call(
        body, name=name, grid=(L // BLK,),
        in_specs=[pl.BlockSpec((BLK, 4 * LANES), lambda n: (n, 0)), _full((L, 2 * LANES)), _full((C, 2 * LANES)),
                  pl.BlockSpec(memory_space=pltpu.SMEM)],
        out_specs=pl.BlockSpec((BLK, 4 * LANES), lambda n: (n, 0)),
        out_shape=jax.ShapeDtypeStruct((L, 4 * LANES), BF16),
        compiler_params=_cp(("parallel",)),
    )(q, kv, kvc, sink)


def attn_bwd(q, kv, kvc, sink, dpa, cos, sa, sb, *, name):
    L = q.shape[0]
    C = kvc.shape[0]
    nb = L // BLK
    scale = HEAD_DIM ** -0.5

    def body(q_ref, kv_ref, kvc_ref, sink_ref, do_ref, c_ref, sa_ref, sb_ref, cq_ref, saq_ref, sbq_ref,
             dq_ref, dkv_ref, dkvc_ref, dsink_ref, dkv_acc, dkvc_acc):
        n = pl.program_id(0)

        @pl.when(n == 0)
        def _():
            dkv_acc[...] = jnp.zeros_like(dkv_acc)
            dkvc_acc[...] = jnp.zeros_like(dkvc_acc)
            dsink_ref[...] = jnp.zeros_like(dsink_ref)

        start, mask4 = _attn_block(n, L)
        band = kv_ref[pl.ds(start, 3 * BLK), :]
        kvc_ = kvc_ref[...]
        band_k, band_v, ctx_k, ctx_v = band[:, :LANES], band[:, LANES:], kvc_[:, :LANES], kvc_[:, LANES:]
        qs = _stack_slabs(q_ref) * scale
        dos = _stack_slabs(do_ref)
        lane = lax.broadcasted_iota(jnp.int32, (1, LANES), 1)
        dsink = jnp.zeros((1, LANES), F32)
        dq = jnp.zeros((GQA * BLK, LANES), F32)
        dk = jnp.zeros((LANES, 3 * BLK), F32)
        dv = jnp.zeros((LANES, 3 * BLK), F32)
        dkc = jnp.zeros((LANES, C), F32)
        dvc = jnp.zeros((LANES, C), F32)
        for kh in range(N_KV_HEADS):
            grp = _kv_head_lanes(kh)
            qk = jnp.where(grp, qs, jnp.zeros_like(qs))
            dok = jnp.where(grp, dos, jnp.zeros_like(dos))
            p_loc, p_ctx, p_s = _attn_probs(qk, band_k, ctx_k, sink_ref, kh, mask4)
            dp_loc = _dot_nt(dok, band_v)
            dp_ctx = _dot_nt(dok, ctx_v)
            delta = jnp.sum(p_loc * dp_loc, axis=-1, keepdims=True) + jnp.sum(p_ctx * dp_ctx, axis=-1, keepdims=True)
            ds_loc = (p_loc * (dp_loc - delta)).astype(BF16)
            ds_ctx = (p_ctx * (dp_ctx - delta)).astype(BF16)
            dsk = p_s * delta
            for hh in range(GQA):
                dsink = dsink - jnp.where(lane == kh * GQA + hh, jnp.sum(dsk[hh * BLK:(hh + 1) * BLK], axis=0, keepdims=True), 0.0)
            dq = dq + jnp.where(grp, _dot(ds_loc, band_k) + _dot(ds_ctx, ctx_k), 0.0)
            dk = dk + _dot_tn(qk, ds_loc)
            dv = dv + _dot_tn(dok, p_loc.astype(BF16))
            dkc = dkc + _dot_tn(qk, ds_ctx)
            dvc = dvc + _dot_tn(dok, p_ctx.astype(BF16))
        dsink_ref[...] += dsink
        dkv_acc[:LANES, pl.ds(start, 3 * BLK)] += dk
        dkv_acc[LANES:, pl.ds(start, 3 * BLK)] += dv
        dkvc_acc[:LANES, :] += dkc
        dkvc_acc[LANES:, :] += dvc
        c, a, b = cq_ref[...], -saq_ref[...], -sbq_ref[...]
        for s in range(GQA):
            dq_ref[:, s * LANES:(s + 1) * LANES] = _rope(dq[s * BLK:(s + 1) * BLK] * scale, c, a, b).astype(BF16)

        @pl.when(n == nb - 1)
        def _():
            dkv_ref[:, :LANES] = _rope(dkv_acc[:LANES, :].T, c_ref[...], -sa_ref[...], -sb_ref[...]).astype(BF16)
            dkv_ref[:, LANES:] = dkv_acc[LANES:, :].T.astype(BF16)
            dkvc_ref[...] = dkvc_acc[...].T.astype(BF16)

    blk = lambda w: pl.BlockSpec((BLK, w), lambda n: (n, 0))
    return pl.pallas_call(
        body, name=name, grid=(nb,),
        in_specs=[blk(4 * LANES), _full((L, 2 * LANES)), _full((C, 2 * LANES)), pl.BlockSpec(memory_space=pltpu.SMEM),
                  pl.BlockSpec((BLK, 4 * LANES), lambda n: (n, 1)),
                  _full((L, LANES)), _full((L, LANES)), _full((L, LANES)), blk(LANES), blk(LANES), blk(LANES)],
        out_specs=[blk(4 * LANES), _full((L, 2 * LANES)), _full((C, 2 * LANES)), _full((1, LANES))],
        out_shape=[jax.ShapeDtypeStruct((L, 4 * LANES), BF16), jax.ShapeDtypeStruct((L, 2 * LANES), BF16),
                   jax.ShapeDtypeStruct((C, 2 * LANES), BF16), jax.ShapeDtypeStruct((1, LANES), F32)],
        scratch_shapes=[pltpu.VMEM((2 * LANES, L), F32), pltpu.VMEM((2 * LANES, C), F32)],
        compiler_params=_cp(("arbitrary",)),
    )(q, kv, kvc, sink, dpa, cos, sa, sb, cos, sa, sb)


def _gelu_parts(x):
    th = jnp.tanh(SQRT_2_OVER_PI * (x + GELU_C * x * x * x))
    return 0.5 * x * (1.0 + th), th


def _gelu_grad(x, th):
    return 0.5 * (1.0 + th) + 0.5 * x * (1.0 - th * th) * SQRT_2_OVER_PI * (1.0 + 3.0 * GELU_C * x * x)


def _layernorm(v):
    mu = jnp.mean(v, axis=-1, keepdims=True)
    vc = v - mu
    rstd = lax.rsqrt(jnp.mean(vc * vc, axis=-1, keepdims=True) + EPS)
    return vc * rstd, rstd


def sgu_fwd(z1, ln_g, ln_b, ws, bst, *, name):
    L, W2 = z1.shape
    W = W2 // 2
    ng = W // LANES

    def body(z_ref, g_ref, b_ref, ws_ref, bs_ref, o_ref):
        z, _ = _gelu_parts(z_ref[...].astype(F32))
        xhat, _ = _layernorm(z[:, W:])
        vln = (xhat * g_ref[...] + b_ref[...]).astype(BF16)
        for gi in range(ng):
            cs = slice(gi * LANES, (gi + 1) * LANES)
            s = _dot(ws_ref[gi], vln[:, cs]) + bs_ref[:, gi:gi + 1]
            o_ref[:, cs] = (z[:, cs] * s).astype(BF16)

    vec = _full((1, W))
    return pl.pallas_call(
        body, name=name, grid=(L // BLK,),
        in_specs=[pl.BlockSpec((BLK, W2), lambda n: (n, 0)), vec, vec, _full((ng, LANES, LANES)), _full((BLK, ng))],
        out_specs=pl.BlockSpec((BLK, W), lambda n: (n, 0)),
        out_shape=jax.ShapeDtypeStruct((L, W), BF16),
        compiler_params=_cp(("parallel",)),
    )(z1, ln_g, ln_b, ws, bst)


def sgu_bwd(z1, dus, ln_g, ln_b, ws, wst, bst, *, name):
    L, W2 = z1.shape
    W = W2 // 2
    ng = W // LANES

    def body(z_ref, d_ref, g_ref, b_ref, ws_ref, wst_ref, bs_ref, dz_ref, dws_ref, dbs_ref, dg_ref, db_ref, dv_scr):
        @pl.when(pl.program_id(0) == 0)
        def _():
            dws_ref[...] = jnp.zeros_like(dws_ref)
            dbs_ref[...] = jnp.zeros_like(dbs_ref)
            dg_ref[...] = jnp.zeros_like(dg_ref)
            db_ref[...] = jnp.zeros_like(db_ref)

        zp = z_ref[...].astype(F32)
        z, th = _gelu_parts(zp)
        xhat, rstd = _layernorm(z[:, W:])
        vln = (xhat * g_ref[...] + b_ref[...]).astype(BF16)
        d = d_ref[...].astype(F32)
        lane = lax.broadcasted_iota(jnp.int32, (1, LANES), 1)
        dbs = jnp.zeros((BLK, LANES), F32)
        dgel = _gelu_grad(zp, th)
        for gi in range(ng):
            cs = slice(gi * LANES, (gi + 1) * LANES)
            s = _dot(ws_ref[gi], vln[:, cs]) + bs_ref[:, gi:gi + 1]
            dz_ref[:, cs] = (d[:, cs] * s * dgel[:, cs]).astype(BF16)
            ds = d[:, cs] * z[:, cs]
            dbs = dbs + jnp.where(lane == gi, jnp.sum(ds, axis=-1, keepdims=True), 0.0)
            dsb = ds.astype(BF16)
            dws_ref[gi] += _dot_nt(dsb, vln[:, cs])
            dv_scr[:, cs] = _dot(wst_ref[gi], dsb)
        dbs_ref[...] += dbs
        dvln = dv_scr[...]
        dg_ref[...] += _colsum(dvln * xhat)
        db_ref[...] += _colsum(dvln)
        dxh = dvln * g_ref[...]
        dv = rstd * (dxh - jnp.mean(dxh, axis=-1, keepdims=True) - xhat * jnp.mean(dxh * xhat, axis=-1, keepdims=True))
        dz_ref[:, W:] = (dv * dgel[:, W:]).astype(BF16)

    vec = _full((1, W))
    return pl.pallas_call(
        body, name=name, grid=(L // BLK,),
        in_specs=[pl.BlockSpec((BLK, W2), lambda n: (n, 0)), pl.BlockSpec((BLK, W), lambda n: (n, 0)), vec, vec,
                  _full((ng, LANES, LANES)), _full((ng, LANES, LANES)), _full((BLK, ng))],
        out_specs=[pl.BlockSpec((BLK, W2), lambda n: (n, 0)), _full((ng, LANES, LANES)), _full((BLK, LANES)), vec, vec],
        out_shape=[jax.ShapeDtypeStruct((L, W2), BF16), jax.ShapeDtypeStruct((ng, LANES, LANES), F32),
                   jax.ShapeDtypeStruct((BLK, LANES), F32), jax.ShapeDtypeStruct((1, W), F32), jax.ShapeDtypeStruct((1, W), F32)],
        scratch_shapes=[pltpu.VMEM((BLK, W), F32)],
        compiler_params=_cp(("arbitrary",)),
    )(z1, dus, ln_g, ln_b, ws, wst, bst)


def _adamw_math(w, m, v, g):
    m_ = ADAM_B1 * m + (1.0 - ADAM_B1) * g
    v_ = ADAM_B2 * v + (1.0 - ADAM_B2) * (g * g)
    return -ADAM_LR * ((m_ / BC1) / (jnp.sqrt(v_ / BC2) + ADAM_EPS) + ADAM_WD * w), m_, v_


def adamw(w, m, v, gparts, *, tr, name):
    NL, R, Wd = w.shape
    nr = R // tr

    def body(w_ref, m_ref, v_ref, *rest):
        gp_refs, (g_ref, d_ref, nm_ref, nv_ref) = rest[:NL], rest[NL:]
        for l in range(NL):
            @pl.when(pl.program_id(0) == l)
            def _():
                g = gp_refs[l][0].astype(F32)
                for s in range(1, gp_refs[l].shape[0]):
                    g = g + gp_refs[l][s].astype(F32)
                g_ref[...] = g
                d_ref[...], nm_ref[...], nv_ref[...] = _adamw_math(w_ref[...], m_ref[...], v_ref[...], g)

    row = pl.BlockSpec((None, tr, Wd), lambda l, i: (l, i, 0))
    gspecs = [pl.BlockSpec((gparts[l].shape[0], tr, Wd), (lambda l_, i, l=l: (0, jnp.clip(i + (l_ - l) * nr, 0, nr - 1), 0)))
              for l in range(NL)]
    return pl.pallas_call(
        body, name=name, grid=(NL, nr),
        in_specs=[row, row, row] + gspecs, out_specs=[row] * 4, out_shape=[jax.ShapeDtypeStruct((NL, R, Wd), F32)] * 4,
        compiler_params=_cp(("arbitrary", "arbitrary")),
    )(w, m, v, *gparts)


def small_update(gpacks, me, params, loss_row, *, name):
    n = len(params)

    def body(me_ref, gp_ref, *refs):
        ins, outs, gs_ref = refs[:3 * n], refs[3 * n:-1], refs[-1]
        gs_ref[...] = gp_ref[0].astype(F32)
        for dv in range(1, N_DEV):
            gs_ref[...] += gp_ref[dv].astype(F32)
        for p, (w, _, _, off, per_dev) in enumerate(params):
            w_ref, m_ref, v_ref = ins[3 * p:3 * p + 3]
            g_ref, d_ref, nm_ref, nv_ref = outs[4 * p:4 * p + 4]
            rows, cols = w.shape
            if cols == LANES and rows % 8 == 0 and not per_dev:
                g = gs_ref[off:off + rows, :]
                g_ref[...] = g
                d_ref[...], nm_ref[...], nv_ref[...] = _adamw_math(w_ref[...], m_ref[...], v_ref[...], g)
                continue
            chunks = -(-cols // LANES)
            base = off + me_ref[0] * per_dev if per_dev else off
            for i in range(rows):
                for j in range(chunks):
                    wd = min(LANES, cols - j * LANES)
                    at = (slice(i, i + 1), slice(j * LANES, j * LANES + wd))
                    g = gs_ref[pl.ds(base + i * chunks + j, 1), 0:wd]
                    g_ref[at] = g
                    d_ref[at], nm_ref[at], nv_ref[at] = _adamw_math(w_ref[at], m_ref[at], v_ref[at], g)
        outs[-1][...] = jnp.sum(gs_ref[loss_row:loss_row + 1, :], axis=1, keepdims=True)

    vm = pl.BlockSpec(memory_space=pltpu.VMEM)
    flat = [a for w, m, v, _, _ in params for a in (w, m, v)]
    out_shape = [jax.ShapeDtypeStruct(w.shape, F32) for w, _, _, _, _ in params for _ in range(4)] + [jax.ShapeDtypeStruct((1, 1), F32)]
    return pl.pallas_call(
        body, name=name, in_specs=[pl.BlockSpec(memory_space=pltpu.SMEM)] + [vm] * (1 + len(flat)),
        out_specs=[vm] * len(out_shape), out_shape=out_shape,
        scratch_shapes=[pltpu.VMEM(gpacks.shape[1:], F32)],
        compiler_params=pltpu.CompilerParams(vmem_limit_bytes=VMEM_LIMIT),
    )(me, gpacks, *flat)


def ada_fwd_mm(cs, w_ada, b_loc, *, name):
    R, D = cs.shape
    nl, _, n = w_ada.shape

    def body(c_ref, w_ref, b_ref, s_ref, m_ref):
        c = c_ref[...]
        s = c * jax.nn.sigmoid(c)
        s_ref[...] = s
        for i in range(nl):
            m_ref[i] = _dot(s.astype(BF16), w_ref[i].astype(BF16)) + b_ref[i:i + 1, :]

    return pl.pallas_call(
        body, name=name, in_specs=[_full((R, D)), _full((nl, D, n)), _full((nl, n))],
        out_specs=[_full((R, D)), _full((nl, R, n))], grid=(1,),
        out_shape=[jax.ShapeDtypeStruct((R, D), F32), jax.ShapeDtypeStruct((nl, R, n), F32)],
        compiler_params=_cp(("arbitrary",)),
    )(cs, w_ada, b_loc)


def ada_bwd_mm(s, c_ctx, dall, w_ada, *, name):
    R, D = s.shape
    nl, _, n = w_ada.shape

    def body(s_ref, cc_ref, d_ref, w_ref, gw_ref, dcc_ref):
        sb = s_ref[...].astype(BF16)
        row = lax.broadcasted_iota(jnp.int32, (R, 1), 0)
        dctx = d_ref[0, 1:2, :]
        for dv in range(1, N_DEV):
            dctx = dctx + d_ref[dv, 1:2, :]
        for i in range(nl):
            dm = jnp.zeros((R, n), F32)
            for dv in range(N_DEV):
                dm = dm + jnp.where(row == dv, d_ref[dv, 2 * i:2 * i + 1, :], 0.0)
            if i == 0:
                dm = dm + jnp.where(row == N_DEV, dctx, 0.0)
            gw_ref[i] = _dot_tn(sb, dm.astype(BF16))
        cc = cc_ref[...]
        sg = jax.nn.sigmoid(cc)
        ds = _dot_nt(jnp.broadcast_to(dctx, (8, n)).astype(BF16), w_ref[0].astype(BF16))
        dcc_ref[...] = ds * (sg * (1.0 + cc * (1.0 - sg)))

    return pl.pallas_call(
        body, name=name, grid=(1,),
        in_specs=[_full((R, D)), _full((1, D)), _full((N_DEV, 3, n)), _full((nl, D, n))],
        out_specs=[_full((nl, D, n)), _full((8, D))],
        out_shape=[jax.ShapeDtypeStruct((nl, D, n), F32), jax.ShapeDtypeStruct((8, D), F32)],
        compiler_params=_cp(("arbitrary",)),
    )(s, c_ctx, dall, w_ada)


def _place():
    x, y, c = lax.axis_index("x"), lax.axis_index("y"), lax.axis_index("c")
    return x, y, c


def _lin(p):
    return 4 * p[0] + 2 * p[1] + p[2]


def all_gather_small(xb, *, name):
    R, W = xb.shape

    def body(x_ref, out_ref, send_sems, recv_sems, local_sem):
        x, y, c = _place()
        me = _lin((x, y, c))
        mine = pltpu.make_async_copy(x_ref, out_ref.at[me], local_sem)
        mine.start()
        copies = []
        for k in range(1, N_DEV):
            peer = (x ^ (k >> 2), y ^ ((k >> 1) & 1), c ^ (k & 1))
            mk = lambda dst, k=k, peer=peer: pltpu.make_async_remote_copy(
                src_ref=x_ref, dst_ref=dst, send_sem=send_sems.at[k - 1], recv_sem=recv_sems.at[k - 1], device_id=peer, device_id_type=MESH)
            mk(out_ref.at[me]).start()
            copies.append(mk(out_ref.at[_lin(peer)]))
        for cp in copies:
            cp.wait_recv()
        for cp in copies:
            cp.wait_send()
        mine.wait()

    vm = pl.BlockSpec(memory_space=pltpu.VMEM)
    return pl.pallas_call(
        body, name=name, in_specs=[vm], out_specs=vm, out_shape=jax.ShapeDtypeStruct((N_DEV, R, W), xb.dtype),
        scratch_shapes=[pltpu.SemaphoreType.DMA((7,)), pltpu.SemaphoreType.DMA((7,)), pltpu.SemaphoreType.DMA],
        compiler_params=pltpu.CompilerParams(vmem_limit_bytes=VMEM_LIMIT),
    )(xb)


HBM_SPEC = pl.BlockSpec(memory_space=pltpu.HBM)
SEM_SPEC = pl.BlockSpec(memory_space=pltpu.SEMAPHORE)
ORDERED_EFFECT = pltpu.SideEffectType.DATAFLOW_SIDE_EFFECTING


def _exchange_copies(srcs, lands, sems, scatter):
    x, y, c = _place()
    me = _lin((x, y, c))
    for j in range(len(srcs)):
        r = lands[j].shape[0] // N_DEV
        block = lambda d, j=j, r=r: pl.ds(pl.multiple_of(d * r, 16), r)
        for k in range(1, N_DEV):
            peer = (x ^ (k >> 2), y ^ ((k >> 1) & 1), c ^ (k & 1))
            src = srcs[j].at[block(_lin(peer)), :] if scatter else srcs[j]
            mk = lambda dst, j=j, k=k, peer=peer, src=src: pltpu.make_async_remote_copy(
                src_ref=src, dst_ref=dst, send_sem=sems[2 * j].at[k - 1], recv_sem=sems[2 * j + 1].at[k - 1],
                device_id=peer, device_id_type=MESH)
            yield mk(lands[j].at[block(me), :]), mk(lands[j].at[block(_lin(peer)), :])


def exchange_start(srcs, lands, *, scatter, name):
    nw = len(srcs)

    def body(*refs):
        for start, _ in _exchange_copies(refs[:nw], refs[nw:2 * nw], refs[2 * nw:4 * nw], scatter):
            start.start()
        refs[-1][...] = jnp.zeros_like(refs[-1])

    thru = [pltpu.HBM(a.shape, a.dtype) for a in (*srcs, *lands)]
    res = pl.pallas_call(
        body, name=name, in_specs=[HBM_SPEC] * (2 * nw),
        out_specs=[SEM_SPEC] * (2 * nw) + [HBM_SPEC] * (2 * nw) + [pl.BlockSpec(memory_space=pltpu.VMEM)],
        out_shape=[pltpu.SemaphoreType.DMA((N_DEV - 1,))] * (2 * nw) + thru + [jax.ShapeDtypeStruct((8, LANES), F32)],
        input_output_aliases={i: 2 * nw + i for i in range(2 * nw)},
        compiler_params=pltpu.CompilerParams(has_side_effects=ORDERED_EFFECT),
    )(*[pltpu.with_memory_space_constraint(a, pltpu.HBM) for a in (*srcs, *lands)])
    return res[:2 * nw], res[2 * nw:3 * nw], res[3 * nw:4 * nw], res[-1]


def exchange_wait(srcs, lands, sems, after, *, scatter, name):
    nw = len(srcs)

    def body(*refs):
        for _, arrive in _exchange_copies(refs[:nw], refs[nw:2 * nw], refs[2 * nw:4 * nw], scatter):
            arrive.wait_send()
            arrive.wait_recv()

    res = pl.pallas_call(
        body, name=name, in_specs=[HBM_SPEC] * (2 * nw) + [SEM_SPEC] * (2 * nw) + [pl.BlockSpec(memory_space=pl.ANY)],
        out_specs=[HBM_SPEC] * (2 * nw), out_shape=[pltpu.HBM(a.shape, a.dtype) for a in (*srcs, *lands)],
        input_output_aliases={i: i for i in range(2 * nw)},
        compiler_params=pltpu.CompilerParams(has_side_effects=ORDERED_EFFECT),
    )(*srcs, *lands, *sems, after)
    return res[nw:]


def place_own(srcs, rows, me, *, scatter, name):
    nw = len(srcs)
    lands = [lax.empty((N_DEV * r, s_.shape[1]), s_.dtype) for r, s_ in zip(rows, srcs)]

    def body(me_ref, *refs):
        for j in range(nw):
            refs[2 * nw + j][...] = refs[j][...]

    mine = lambda i, me_ref: (me_ref[0], 0)
    src_at = mine if scatter else (lambda i, me_ref: (0, 0))
    blocks = [(r, s_.shape[1]) for r, s_ in zip(rows, srcs)]
    return pl.pallas_call(
        body, name=name,
        grid_spec=pltpu.PrefetchScalarGridSpec(
            num_scalar_prefetch=1, grid=(1,),
            in_specs=[pl.BlockSpec(b_, src_at) for b_ in blocks] + [pl.BlockSpec(memory_space=pl.ANY)] * nw,
            out_specs=[pl.BlockSpec(b_, mine) for b_ in blocks]),
        out_shape=[jax.ShapeDtypeStruct(l_.shape, l_.dtype) for l_ in lands],
        input_output_aliases={1 + nw + j: j for j in range(nw)},
        compiler_params=_cp(("arbitrary",)),
    )(jnp.reshape(me, (1,)).astype(jnp.int32), *srcs, *lands)


def _rope_tables(L):
    t = jnp.arange(L)
    inv = ROPE_BASE ** (-jnp.arange(ROPE_FREQS, dtype=F32) / ROPE_FREQS)
    ar = (t // GRID_W).astype(F32)[:, None] * inv
    ac = (t % GRID_W).astype(F32)[:, None] * inv
    z = jnp.zeros_like(ar)
    cos = jnp.concatenate([jnp.cos(ar), jnp.cos(ar), jnp.cos(ac), jnp.cos(ac)], axis=1)
    sa = jnp.concatenate([-jnp.sin(ar), z, -jnp.sin(ac), z], axis=1)
    sb = jnp.concatenate([z, jnp.sin(ar), z, jnp.sin(ac)], axis=1)
    return tuple(jnp.tile(a, (1, LANES // HEAD_DIM)) for a in (cos, sa, sb))


def _nat2d(a):
    return a.reshape(1, -1) if a.ndim == 1 else a.reshape(-1, a.shape[-1])


def _pack_rows(a):
    rows, cols = a.shape
    chunks = -(-cols // LANES)
    f = jnp.pad(a, ((0, 0), (0, chunks * LANES - cols))).reshape(rows * chunks, LANES)
    return jnp.pad(f, ((0, -f.shape[0] % 8), (0, 0)))


def _rows128(a):
    f = a.reshape(-1)
    n = -(-f.shape[0] // (8 * LANES)) * 8 * LANES
    return jnp.pad(f, (0, n - f.shape[0])).reshape(-1, LANES)


def kernel(x, c, ctx, c_ctx, w_ada, b_ada, g_mix_pre, g_mix_post, g_ffn_pre, g_ffn_post, w_in_even, w_pool, pool_scale, attn_sink, w_out_even, w_in_odd, sgu_ln_g, sgu_ln_b, sgu_w, sgu_b, w_out_odd, w_ffn_up, ffn_conv_w, ffn_conv_b, w_ffn_down, loss_target, m_c_ctx, m_w_ada, m_b_ada, m_g_mix_pre, m_g_mix_post, m_g_ffn_pre, m_g_ffn_post, m_w_in_even, m_w_pool, m_pool_scale, m_attn_sink, m_w_out_even, m_w_in_odd, m_sgu_ln_g, m_sgu_ln_b, m_sgu_w, m_sgu_b, m_w_out_odd, m_w_ffn_up, m_ffn_conv_w, m_ffn_conv_b, m_w_ffn_down, v_c_ctx, v_w_ada, v_b_ada, v_g_mix_pre, v_g_mix_post, v_g_ffn_pre, v_g_ffn_post, v_w_in_even, v_w_pool, v_pool_scale, v_attn_sink, v_w_out_even, v_w_in_odd, v_sgu_ln_g, v_sgu_ln_b, v_sgu_w, v_sgu_b, v_w_out_odd, v_w_ffn_up, v_ffn_conv_w, v_ffn_conv_b, v_w_ffn_down):
    P = dict(c_ctx=c_ctx, w_ada=w_ada, b_ada=b_ada, g_mix_pre=g_mix_pre, g_mix_post=g_mix_post, g_ffn_pre=g_ffn_pre,
             g_ffn_post=g_ffn_post, w_in_even=w_in_even, w_pool=w_pool, pool_scale=pool_scale, attn_sink=attn_sink,
             w_out_even=w_out_even, w_in_odd=w_in_odd, sgu_ln_g=sgu_ln_g, sgu_ln_b=sgu_ln_b, sgu_w=sgu_w, sgu_b=sgu_b,
             w_out_odd=w_out_odd, w_ffn_up=w_ffn_up, ffn_conv_w=ffn_conv_w, ffn_conv_b=ffn_conv_b, w_ffn_down=w_ffn_down)
    M = dict(c_ctx=m_c_ctx, w_ada=m_w_ada, b_ada=m_b_ada, g_mix_pre=m_g_mix_pre, g_mix_post=m_g_mix_post, g_ffn_pre=m_g_ffn_pre,
             g_ffn_post=m_g_ffn_post, w_in_even=m_w_in_even, w_pool=m_w_pool, pool_scale=m_pool_scale, attn_sink=m_attn_sink,
             w_out_even=m_w_out_even, w_in_odd=m_w_in_odd, sgu_ln_g=m_sgu_ln_g, sgu_ln_b=m_sgu_ln_b, sgu_w=m_sgu_w, sgu_b=m_sgu_b,
             w_out_odd=m_w_out_odd, w_ffn_up=m_w_ffn_up, ffn_conv_w=m_ffn_conv_w, ffn_conv_b=m_ffn_conv_b, w_ffn_down=m_w_ffn_down)
    V = dict(c_ctx=v_c_ctx, w_ada=v_w_ada, b_ada=v_b_ada, g_mix_pre=v_g_mix_pre, g_mix_post=v_g_mix_post, g_ffn_pre=v_g_ffn_pre,
             g_ffn_post=v_g_ffn_post, w_in_even=v_w_in_even, w_pool=v_w_pool, pool_scale=v_pool_scale, attn_sink=v_attn_sink,
             w_out_even=v_w_out_even, w_in_odd=v_w_in_odd, sgu_ln_g=v_sgu_ln_g, sgu_ln_b=v_sgu_ln_b, sgu_w=v_sgu_w, sgu_b=v_sgu_b,
             w_out_odd=v_w_out_odd, w_ffn_up=v_w_ffn_up, ffn_conv_w=v_ffn_conv_w, ffn_conv_b=v_ffn_conv_b, w_ffn_down=v_w_ffn_down)

    x = x[0]
    ctx = ctx[0]
    target = loss_target[0]
    L, D = x.shape
    C = ctx.shape[0]
    tm = min(512, L)
    tm_up = min(1024, L)
    conv_rows = min(512, L)
    me = 4 * lax.axis_index("x") + 2 * lax.axis_index("y") + lax.axis_index("c")
    n_ada = w_ada.shape[2]
    F = w_ffn_down.shape[1] * N_DEV
    half_f = F // 2

    n_cw = ffn_conv_w.shape[2]
    small = jnp.concatenate([_rows128(c), _rows128(sgu_ln_g), _rows128(sgu_ln_b), _rows128(ffn_conv_w)], axis=0)
    small_all = all_gather_small(small, name="gather_small_inputs")
    c_all = small_all[:, :8].reshape(N_DEV, D)
    ln_g = small_all[:, 8].reshape(1, D)
    ln_b = small_all[:, 16].reshape(1, D)
    conv_w = small_all[:, 24:].reshape(N_DEV, -1)[:, :2 * 3 * n_cw].reshape(N_DEV, 2, 3, n_cw)
    conv_w = conv_w.transpose(1, 2, 0, 3).reshape(2, 3, 2 * F)

    cs = jnp.concatenate([c_all, c_ctx[None, :], jnp.zeros((7, D), F32)], axis=0)
    b_loc = lax.dynamic_slice(b_ada, (0, me * n_ada), (2, n_ada))
    silu_c, mods_loc = ada_fwd_mm(cs, w_ada, b_loc, name="ada_fwd")
    mods_all = all_gather_small(mods_loc.reshape(-1, LANES), name="gather_mods")

    shards = [s.astype(BF16) for s in (w_in_even[0].T, w_out_even[0], w_ffn_up[0].T, w_ffn_down[0],
                                       w_in_odd[0].T, w_out_odd[0], w_ffn_up[1].T, w_ffn_down[1])]
    shards, mods_all = lax.optimization_barrier((shards, mods_all))
    w_sems, w_srcs, w_lands, _ = exchange_start(shards, place_own(shards, [s.shape[0] for s in shards], me, scatter=False, name="gather_own"),
                                              scatter=False, name="gather_start")

    def weight(j, after):
        return exchange_wait([w_srcs[j]], [w_lands[j]], w_sems[2 * j:2 * j + 2], after, scatter=False, name=f"gather_wait_{j}")[0]

    mods_all = mods_all.reshape(N_DEV, 2, 16, n_ada).transpose(1, 2, 0, 3).reshape(2, 16, 6 * D)
    mod = lambda i, row: [m_[None, :] for m_ in jnp.split(lax.dynamic_index_in_dim(mods_all[i], row, 0, False), 6)]
    sh_m, sc_m, gt_m, sh_f, sc_f, gt_f = zip(mod(0, me), mod(1, me))
    csh_m, csc_m = mod(0, N_DEV)[:2]

    row = lambda a, i: a[i][None, :]

    cos, sa, sb = _rope_tables(L)
    sink = attn_sink[0]
    bst = sgu_b[0].T
    sgu_wb, sgu_wtb = sgu_w[0].astype(BF16), sgu_w[0].swapaxes(1, 2).astype(BF16)
    wup, wdn = [None, None], [None, None]

    def ffn_fwd(i, xin):
        wup[i] = weight(2 + 4 * i, xin)
        h, hu = pre_mm(xin, row(g_ffn_pre, i), sh_f[i], sc_f[i], wup[i], tm=tm_up, tn=half_f, name=f"ffn_up_{i}")
        a, s1, s2 = conv_fwd(hu, conv_w[i], ffn_conv_b[i][None, :], rows=conv_rows, wblk=2 * LANES, name=f"ffn_conv_{i}")
        wdn[i] = weight(3 + 4 * i, a)
        res = mm_post([a], wdn[i], xin, row(g_ffn_post, i), gt_f[i], tm=tm, target=target if i == 1 else None, name=f"ffn_down_{i}")
        return (h, (hu, s1, s2), a, *res)

    first_mod, cos, sa, sb = lax.optimization_barrier((sh_m[0], cos, sa, sb))
    win_e = permute_heads(weight(0, first_mod))
    h0, u, q, kv = inproj_even(x, row(g_mix_pre, 0), sh_m[0], sc_m[0], win_e, cos, sa, sb, tm=tm, name="in_even")
    hc, kvc = pre_mm(ctx, row(g_mix_pre, 0), csh_m, csc_m, win_e, tm=C, tn=2 * LANES, w_row_off=8 * LANES, name="in_even_ctx")
    pa = [pool_fwd(u, w_pool[0], pool_scale, name="pool_fwd"), attn_fwd(q, kv, kvc, sink, name="attn_fwd")]
    wout_e = permute_heads(weight(1, pa[1]))
    y0, x1 = mm_post(pa, wout_e, x, row(g_mix_post, 0), gt_m[0], tm=tm, name="out_even")
    h1, hu0, a0, f0, x2 = ffn_fwd(0, x1)
    win_o = weight(4, x2)
    h2, z1 = pre_mm(x2, row(g_mix_pre, 1), sh_m[1], sc_m[1], win_o, tm=tm_up, tn=D, name="in_odd")
    us = sgu_fwd(z1, ln_g, ln_b, sgu_wb, bst, name="sgu_fwd")
    wout_o = weight(5, us)
    y1, x3 = mm_post([us], wout_o, x2, row(g_mix_post, 1), gt_m[1], tm=tm, name="out_odd")
    h3, hu1, a1, f1, dx4, loss_part = ffn_fwd(1, x3)

    g_srcs, g_lands, g_sems = [], [], []

    def scatter(grads, nm):
        own = place_own(grads, [g.shape[0] // N_DEV for g in grads], me, scatter=True, name=nm.replace("start", "own"))
        sems, srcs, lands, tok = exchange_start(grads, own, scatter=True, name=nm)
        g_srcs.extend(srcs)
        g_lands.extend(lands)
        g_sems.extend(sems)
        return tok[0:1, 0:1]

    def ffn_bwd(i, dxo, xin, h, hu, a, f, g_post):
        dyf, da, dg_post, dgt = post_bwd_mm(dxo, f, g_post, gt_f[i], wdn[i], tm=tm, name=f"ffn_down_bwd_{i}")
        dhg, dhu, dcwg, dcwu, dcbg, dcbu = conv_bwd(da, hu[1], hu[2], hu[0], conv_w[i], rows=conv_rows, wblk=2 * LANES,
                                                    name=f"ffn_conv_bwd_{i}")
        dxin, dg_pre, dsh, dsc = mm_pre_bwd([dhg, dhu], wup[i], xin, dxo, row(g_ffn_pre, i), sc_f[i], tm=tm, tk=half_f,
                                            name=f"ffn_up_bwd_{i}")
        g_dn = wgrad([a], dyf, tr=2 * LANES, name=f"wgrad_down_{i}")
        g_up = wgrad([dhg, dhu], h, tr=2 * LANES, name=f"wgrad_up_{i}")
        tok = scatter([g_dn, g_up], f"scatter_start_ffn_{i}")
        return dxin, tok, dict(g_ffn_post=dg_post, g_ffn_pre=dg_pre, gt_f=dgt, sh_f=dsh, sc_f=dsc,
                               ffn_conv_w=jnp.concatenate([dcwg, dcwu], axis=1), ffn_conv_b=jnp.concatenate([dcbg, dcbu], axis=1)[0])

    dx3, tok, sf1 = ffn_bwd(1, dx4, x3, h3, hu1, a1, f1, row(g_ffn_post, 1))
    dy1, dus, dg_mpost1, dgt_m1 = post_bwd_mm(dx3, y1, row(g_mix_post, 1) + tok, gt_m[1], wout_o, tm=tm, name="out_odd_bwd")
    dz1, dws, dbs, dlng, dlnb = sgu_bwd(z1, dus, ln_g, ln_b, sgu_wb, sgu_wtb, bst, name="sgu_bwd")
    dx2, dg_mpre1, dsh_m1, dsc_m1 = mm_pre_bwd([dz1], win_o, x2, dx3, row(g_mix_pre, 1), sc_m[1], tm=tm, tk=D, name="in_odd_bwd")
    tok = scatter([wgrad([us], dy1, tr=2 * LANES, name="wgrad_out_odd"), wgrad([dz1], h2, tr=2 * LANES, name="wgrad_in_odd")],
                  "scatter_start_mix_1")

    dx1, tok, sf0 = ffn_bwd(0, dx2, x1, h1, hu0, a0, f0, row(g_ffn_post, 0) + tok)
    dy0, dpa, dg_mpost0, dgt_m0 = post_bwd_mm(dx1, y0, row(g_mix_post, 0) + tok, gt_m[0], wout_e, tm=tm, name="out_even_bwd")
    tok = scatter([permute_heads(wgrad(pa, dy0, tr=2 * LANES, name="wgrad_out_even"), inverse=True)], "scatter_start_out_0")
    du, dwp, dps = pool_bwd(u, dpa, w_pool[0], pool_scale + tok, name="pool_bwd")
    dq, dkv, dkvc, dsink = attn_bwd(q, kv, kvc, sink, dpa, cos, sa, sb, name="attn_bwd")
    dz0 = jnp.concatenate([du, dq, dkv], axis=1)
    dzc = jnp.concatenate([jnp.zeros((C, 8 * LANES), BF16), dkvc], axis=1)
    tok = scatter([permute_heads(wgrad([dz0], h0, tr=2 * LANES, extra=(dzc, hc), name="wgrad_in_even"), inverse=True)],
                  "scatter_start_in_0")
    grad_x, dg_mpre0, dsh_m0, dsc_m0 = mm_pre_bwd([dz0], win_e, x, dx1, row(g_mix_pre, 0) + tok, sc_m[0], tm=tm, tk=dz0.shape[1],
                                                  name="in_even_bwd")
    _, dg_mpre0c, dcsh, dcsc = mm_pre_bwd([dkvc], win_e, ctx, None, row(g_mix_pre, 0), csc_m, tm=C, tk=2 * LANES,
                                          w_row_off=8 * LANES, name="in_even_ctx_bwd")

    out = {}

    def update(name, lands, transposed):
        w_, m_, v_ = (a.transpose(0, 2, 1) if transposed else a for a in (P[name], M[name], V[name]))
        r = w_.shape[1]
        tr = r // 4 if r % 64 == 0 and r > 256 else r
        res = adamw(w_, m_, v_, [l_.reshape(N_DEV, r, l_.shape[1]) for l_ in lands], tr=tr, name=f"adamw_{name}")
        for kind, val in zip(("grad", "delta", "new_m", "new_v"), res):
            out[(kind, name)] = val.transpose(0, 2, 1) if transposed else val

    zero = jnp.zeros((1, D), F32)
    dmod0 = jnp.concatenate([dsh_m0, dsc_m0, dgt_m0, sf0["sh_f"], sf0["sc_f"], sf0["gt_f"]], axis=1)
    dmodc = jnp.concatenate([dcsh, dcsc, zero, zero, zero, zero], axis=1)
    dmod1 = jnp.concatenate([dsh_m1, dsc_m1, dgt_m1, sf1["sh_f"], sf1["sc_f"], sf1["gt_f"]], axis=1)
    dmods = jnp.concatenate([dmod0, dmodc, dmod1], axis=0)
    dm = dmods.reshape(-1, LANES).astype(BF16)
    d_sems, d_srcs, d_lands, d_tok = exchange_start(
        [dm], place_own([dm], [dm.shape[0]], me, scatter=False, name="dmods_own"), scatter=False, name="dmods_start")
    slots = exchange_wait(g_srcs[:6], g_lands[:6], g_sems[:12], d_tok, scatter=True, name="scatter_wait_early")
    early = slots
    update("w_ffn_down", [slots[4], slots[0]], False)
    update("w_in_odd", [slots[3]], True)
    update("w_out_odd", [slots[2]], False)
    dmods_all = exchange_wait(d_srcs, d_lands, d_sems, out[("new_v", "w_out_odd")], scatter=False, name="dmods_wait")[0]
    dall = lax.dynamic_index_in_dim(dmods_all.astype(F32).reshape(N_DEV, 3, N_DEV, n_ada), me, 2, False)
    g_w_ada, dcc = ada_bwd_mm(silu_c, c_ctx[None, :], dall, w_ada, name="ada_bwd")

    rep = dict(
        c_ctx=dcc[0:1],
        b_ada=jnp.concatenate([dmod0 + dmodc, dmod1]),
        g_mix_pre=jnp.concatenate([dg_mpre0 + dg_mpre0c, dg_mpre1]),
        g_mix_post=jnp.concatenate([dg_mpost0, dg_mpost1]),
        g_ffn_pre=jnp.concatenate([sf0["g_ffn_pre"], sf1["g_ffn_pre"]]),
        g_ffn_post=jnp.concatenate([sf0["g_ffn_post"], sf1["g_ffn_post"]]),
        w_pool=_nat2d(dwp), pool_scale=dps, attn_sink=dsink[:, :N_Q_HEADS],
        sgu_w=_nat2d(dws), sgu_b=dbs[:, :sgu_b.shape[1]].T,
        ffn_conv_b=jnp.stack([sf0["ffn_conv_b"], sf1["ffn_conv_b"]]),
    )
    hi = loss_part.astype(BF16).astype(F32)
    mid = (loss_part - hi).astype(BF16).astype(F32)
    loss_piece = jnp.pad(jnp.concatenate([hi, mid, loss_part - hi - mid], axis=1), ((0, 7), (0, LANES - 3)))
    conv_g = jnp.stack([sf0["ffn_conv_w"], sf1["ffn_conv_w"]]).reshape(2 * 3, N_DEV, n_cw).swapaxes(0, 1)
    shard_full = dict(sgu_ln_g=dlng.reshape(N_DEV, LANES), sgu_ln_b=dlnb.reshape(N_DEV, LANES),
                      ffn_conv_w=jnp.concatenate([_pack_rows(conv_g[d]) for d in range(N_DEV)], axis=0))
    small_names = list(rep) + list(shard_full)
    pieces = [_pack_rows(rep[k]) for k in rep] + list(shard_full.values()) + [loss_piece]
    sizes = [p.shape[0] for p in pieces]
    offs = [sum(sizes[:i]) for i in range(len(sizes))]
    pieces.append(jnp.zeros((-sum(sizes) % 16, LANES), F32))
    gpack = jnp.concatenate(pieces, axis=0).astype(BF16)
    own = place_own([gpack], [gpack.shape[0]], me, scatter=False, name="smallgrad_own")
    s_sems, s_srcs, s_lands, small_tok = exchange_start([gpack], own, scatter=False, name="smallgrad_start")

    slots = exchange_wait(g_srcs[6:], g_lands[6:], g_sems[12:], small_tok, scatter=True, name="scatter_wait_late")
    update("w_in_even", [slots[1]], True)
    update("w_out_even", [slots[0]], False)
    update("w_ffn_up", [early[5], early[1]], True)
    res = adamw(w_ada, m_w_ada, v_w_ada, [g_w_ada[l][None] for l in range(w_ada.shape[0])], tr=D // 4, name="adamw_w_ada")
    for kind, val in zip(("grad", "delta", "new_m", "new_v"), res):
        out[(kind, "w_ada")] = val

    gpacks = exchange_wait(s_srcs, s_lands, s_sems, out[("new_v", "w_ada")], scatter=False, name="smallgrad_wait")[0]
    per_dev = {k: shard_full[k].shape[0] // N_DEV for k in shard_full}
    params = [(_nat2d(P[k]), _nat2d(M[k]), _nat2d(V[k]), offs[i], per_dev.get(k, 0)) for i, k in enumerate(small_names)]
    res = small_update(gpacks.reshape(N_DEV, -1, LANES), jnp.reshape(me, (1,)).astype(jnp.int32), params, offs[-1], name="adamw_small")
    for i, k in enumerate(small_names):
        for kind, val in zip(("grad", "delta", "new_m", "new_v"), res[4 * i:4 * i + 4]):
            out[(kind, k)] = val.reshape(P[k].shape)
    loss = res[-1][0, 0]

    names = list(P)
    final = [loss, grad_x[None]]
    for kind in ("grad", "delta", "new_m", "new_v"):
        for k in names:
            val = out[(kind, k)]
            final.append(val)
    return tuple(final)
```

```python
import functools
import math

import jax
import jax.numpy as jnp
from jax import lax
from jax.experimental import pallas as pl
from jax.experimental.pallas import tpu as pltpu

F32 = jnp.float32
BF16 = jnp.bfloat16
MESH = pl.DeviceIdType.MESH
N_DEV = 8
LANES = 128
VMEM_LIMIT = 48 * 1024 * 1024
EPS = 1e-6
NEG_INF = -1e30
GRID_W = 64
WINDOW = 128
BLK = 128
HEAD_DIM = 64
N_Q_HEADS = 8
N_KV_HEADS = 2
GQA = N_Q_HEADS // N_KV_HEADS
POOL_WINDOWS = (2, 4, 8, 16)
ROPE_BASE = 10000.0
ROPE_FREQS = HEAD_DIM // 4
PAD = 16
ADAM_LR, ADAM_B1, ADAM_B2, ADAM_EPS, ADAM_WD, ADAM_STEP = 0.001, 0.9, 0.999, 1e-08, 0.01, 10
BC1 = 1.0 - ADAM_B1 ** ADAM_STEP
BC2 = 1.0 - ADAM_B2 ** ADAM_STEP
SQRT_2_OVER_PI = math.sqrt(2.0 / math.pi)
GELU_C = 0.044715


def _cp(sem=None):
    return pltpu.CompilerParams(dimension_semantics=sem, vmem_limit_bytes=VMEM_LIMIT)


def _dot(a, b):
    return jnp.dot(a, b, preferred_element_type=F32)


def _dot_nt(a, b):
    return lax.dot_general(a, b, (((1,), (1,)), ((), ())), preferred_element_type=F32)


def _dot_tn(a, b):
    return lax.dot_general(a, b, (((0,), (0,)), ((), ())), preferred_element_type=F32)


def _rms(x):
    r = lax.rsqrt(jnp.mean(x * x, axis=-1, keepdims=True) + EPS)
    return x * r, r


def _rms_bwd(dn, n, r):
    return r * (dn - n * jnp.mean(dn * n, axis=-1, keepdims=True))


def _colsum(a):
    return jnp.sum(a, axis=0, keepdims=True)


def _rope(x, c, sa, sb):
    return x * c + pltpu.roll(x, LANES - ROPE_FREQS, 1) * sa + pltpu.roll(x, ROPE_FREQS, 1) * sb


def _full(shape):
    return pl.BlockSpec(shape, lambda *_: (0,) * len(shape))


def pre_mm(x, g, sh, sc, wt, *, tm, tn, w_row_off=0, name):
    T, D = x.shape
    n_rows = wt.shape[0] - w_row_off
    off = w_row_off // tn

    def body(x_ref, g_ref, sh_ref, sc_ref, w_ref, h_ref, z_ref):
        @pl.when(pl.program_id(1) == 0)
        def _():
            n, _ = _rms(x_ref[...])
            h_ref[...] = (n * g_ref[...] * (1.0 + sc_ref[...]) + sh_ref[...]).astype(BF16)

        z_ref[...] = _dot_nt(h_ref[...], w_ref[...]).astype(BF16)

    vec = pl.BlockSpec((1, D), lambda i, j: (0, 0))
    return pl.pallas_call(
        body, name=name, grid=(T // tm, n_rows // tn),
        in_specs=[pl.BlockSpec((tm, D), lambda i, j: (i, 0)), vec, vec, vec, pl.BlockSpec((tn, D), lambda i, j: (j + off, 0))],
        out_specs=[pl.BlockSpec((tm, D), lambda i, j: (i, 0)), pl.BlockSpec((tm, tn), lambda i, j: (i, j))],
        out_shape=[jax.ShapeDtypeStruct((T, D), BF16), jax.ShapeDtypeStruct((T, n_rows), BF16)],
        compiler_params=_cp(("parallel", "arbitrary")),
    )(x, g, sh, sc, wt)


def inproj_even(x, g, sh, sc, wt, cos, sa, sb, *, tm, name):
    T, D = x.shape
    N = wt.shape[0]

    def body(x_ref, g_ref, sh_ref, sc_ref, w_ref, c_ref, sa_ref, sb_ref, h_ref, u_ref, q_ref, kv_ref):
        n, _ = _rms(x_ref[...])
        h = (n * g_ref[...] * (1.0 + sc_ref[...]) + sh_ref[...]).astype(BF16)
        h_ref[...] = h
        z = _dot_nt(h, w_ref[...])
        u_ref[...] = z[:, :4 * LANES]
        c, a, b = c_ref[...], sa_ref[...], sb_ref[...]
        for s in range(4):
            q_ref[:, s * LANES:(s + 1) * LANES] = _rope(z[:, (4 + s) * LANES:(5 + s) * LANES], c, a, b).astype(BF16)
        kv_ref[:, :LANES] = _rope(z[:, 8 * LANES:9 * LANES], c, a, b).astype(BF16)
        kv_ref[:, LANES:] = z[:, 9 * LANES:].astype(BF16)

    vec = pl.BlockSpec((1, D), lambda i: (0, 0))
    row = lambda w: pl.BlockSpec((tm, w), lambda i: (i, 0))
    return pl.pallas_call(
        body, name=name, grid=(T // tm,),
        in_specs=[row(D), vec, vec, vec, _full((N, D)), row(LANES), row(LANES), row(LANES)],
        out_specs=[row(D), row(4 * LANES), row(4 * LANES), row(2 * LANES)],
        out_shape=[jax.ShapeDtypeStruct((T, D), BF16), jax.ShapeDtypeStruct((T, 4 * LANES), F32),
                   jax.ShapeDtypeStruct((T, 4 * LANES), BF16), jax.ShapeDtypeStruct((T, 2 * LANES), BF16)],
        compiler_params=_cp(("parallel",)),
    )(x, g, sh, sc, wt, cos, sa, sb)


def mm_post(a_parts, w, x, g, gt, *, tm, target=None, name):
    T = a_parts[0].shape[0]
    D = w.shape[1]
    npart = len(a_parts)
    offs = [sum(a_.shape[1] for a_ in a_parts[:p]) for p in range(npart + 1)]
    with_loss = target is not None

    def body(*refs):
        a_refs, (w_ref, x_ref, g_ref, gt_ref) = refs[:npart], refs[npart:npart + 4]
        y = _dot(a_refs[0][...], w_ref[offs[0]:offs[1], :])
        for p in range(1, npart):
            y = y + _dot(a_refs[p][...], w_ref[offs[p]:offs[p + 1], :])
        n, _ = _rms(y)
        xn = x_ref[...] + gt_ref[...] * (n * g_ref[...])
        if not with_loss:
            y_ref, xn_ref = refs[npart + 4:]
            y_ref[...] = y.astype(BF16)
            xn_ref[...] = xn
            return
        t_ref, y_ref, d_ref, l_ref = refs[npart + 4:]
        y_ref[...] = y.astype(BF16)

        @pl.when(pl.program_id(0) == 0)
        def _():
            l_ref[...] = jnp.zeros_like(l_ref)

        e = xn - t_ref[...]
        l_ref[...] += 0.5 * jnp.sum(jnp.mean(e * e, axis=-1, keepdims=True), axis=0, keepdims=True)
        d_ref[...] = e * (1.0 / D)

    vec = pl.BlockSpec((1, D), lambda i: (0, 0))
    row = lambda w_: pl.BlockSpec((tm, w_), lambda i: (i, 0))
    in_specs = [row(a_.shape[1]) for a_ in a_parts] + [_full(w.shape), row(D), vec, vec]
    out_specs = [row(D), row(D)]
    out_shape = [jax.ShapeDtypeStruct((T, D), BF16), jax.ShapeDtypeStruct((T, D), F32)]
    if with_loss:
        in_specs.append(row(D))
        out_specs.append(_full((1, 1)))
        out_shape.append(jax.ShapeDtypeStruct((1, 1), F32))
    return pl.pallas_call(
        body, name=name, grid=(T // tm,), in_specs=in_specs, out_specs=out_specs, out_shape=out_shape,
        compiler_params=_cp(("arbitrary",) if with_loss else ("parallel",)),
    )(*a_parts, w, x, g, gt, *((target,) if with_loss else ()))


def post_bwd_mm(dxn, y, g, gt, w, *, tm, name):
    T, D = y.shape
    K = w.shape[0]

    def body(dxn_ref, y_ref, g_ref, gt_ref, w_ref, dy_ref, da_ref, dg_ref, dgt_ref):
        @pl.when(pl.program_id(0) == 0)
        def _():
            dg_ref[...] = jnp.zeros_like(dg_ref)
            dgt_ref[...] = jnp.zeros_like(dgt_ref)

        d = dxn_ref[...]
        n, r = _rms(y_ref[...].astype(F32))
        g_, gt_ = g_ref[...], gt_ref[...]
        dg_ref[...] += _colsum(d * gt_ * n)
        dgt_ref[...] += _colsum(d * g_ * n)
        dy = _rms_bwd(d * (gt_ * g_), n, r).astype(BF16)
        dy_ref[...] = dy
        da_ref[...] = _dot_nt(dy, w_ref[...]).astype(BF16)

    vec = pl.BlockSpec((1, D), lambda i: (0, 0))
    row = lambda w_: pl.BlockSpec((tm, w_), lambda i: (i, 0))
    return pl.pallas_call(
        body, name=name, grid=(T // tm,),
        in_specs=[row(D), row(D), vec, vec, _full((K, D))],
        out_specs=[row(D), row(K), vec, vec],
        out_shape=[jax.ShapeDtypeStruct((T, D), BF16), jax.ShapeDtypeStruct((T, K), BF16),
                   jax.ShapeDtypeStruct((1, D), F32), jax.ShapeDtypeStruct((1, D), F32)],
        compiler_params=_cp(("arbitrary",)),
    )(dxn, y, g, gt, w)


def mm_pre_bwd(dzs, wt, x, dres, g, sc, *, tm, tk, w_row_off=0, name):
    T, N = dzs[0].shape
    D = x.shape[1]
    nk = N // tk
    npart = len(dzs)
    off = w_row_off // tk
    has_res = dres is not None

    def body(*refs):
        dz_refs = refs[:npart]
        w_refs = refs[npart:2 * npart]
        rest = refs[2 * npart:]
        x_ref = rest[0]
        dres_ref = rest[1] if has_res else None
        g_ref, sc_ref, dx_ref, dg_ref, dsh_ref, dsc_ref, acc = rest[1 + has_res:]
        i, k = pl.program_id(0), pl.program_id(1)

        @pl.when(jnp.logical_and(i == 0, k == 0))
        def _():
            dg_ref[...] = jnp.zeros_like(dg_ref)
            dsh_ref[...] = jnp.zeros_like(dsh_ref)
            dsc_ref[...] = jnp.zeros_like(dsc_ref)

        part = _dot(dz_refs[0][...], w_refs[0][...])
        for p in range(1, npart):
            part = part + _dot(dz_refs[p][...], w_refs[p][...])

        @pl.when(k == 0)
        def _():
            acc[...] = part

        @pl.when(k > 0)
        def _():
            acc[...] += part

        @pl.when(k == nk - 1)
        def _():
            dh = acc[...]
            n, r = _rms(x_ref[...])
            g_, s1 = g_ref[...], 1.0 + sc_ref[...]
            dsh_ref[...] += _colsum(dh)
            dsc_ref[...] += _colsum(dh * n * g_)
            dg_ref[...] += _colsum(dh * s1 * n)
            dxp = _rms_bwd(dh * (g_ * s1), n, r)
            dx_ref[...] = dxp + dres_ref[...] if has_res else dxp

    vec = pl.BlockSpec((1, D), lambda i, k: (0, 0))
    row = pl.BlockSpec((tm, D), lambda i, k: (i, 0))
    w_specs = [pl.BlockSpec((tk, D), (lambda i, k, p=p: (k + off + p * nk, 0))) for p in range(npart)]
    res_specs, res_args = ([row], (dres,)) if has_res else ([], ())
    return pl.pallas_call(
        body, name=name, grid=(T // tm, nk),
        in_specs=[pl.BlockSpec((tm, tk), lambda i, k: (i, k))] * npart + w_specs + [row] + res_specs + [vec, vec],
        out_specs=[row, vec, vec, vec],
        out_shape=[jax.ShapeDtypeStruct((T, D), F32)] + [jax.ShapeDtypeStruct((1, D), F32)] * 3,
        scratch_shapes=[pltpu.VMEM((tm, D), F32)],
        compiler_params=_cp(("arbitrary", "arbitrary")),
    )(*dzs, *([wt] * npart), x, *res_args, g, sc)


def wgrad(a_parts, b, *, tr, extra=None, name):
    T, R = a_parts[0].shape
    D = b.shape[1]
    npart = len(a_parts)
    nr = R // tr

    def body(*refs):
        a_refs, b_ref = refs[:npart], refs[npart]
        g_ref = refs[-1]
        for p in range(npart):
            @pl.when(pl.program_id(0) // nr == p)
            def _():
                acc = _dot_tn(a_refs[p][...], b_ref[...])
                if extra is not None:
                    acc += _dot_tn(refs[npart + 1][...], refs[npart + 2][...])
                g_ref[...] = acc.astype(BF16)

    in_specs = [pl.BlockSpec((T, tr), (lambda r, p=p: (0, jnp.clip(r - p * nr, 0, nr - 1)))) for p in range(npart)]
    in_specs.append(_full((T, D)))
    args = [*a_parts, b]
    if extra is not None:
        a2, b2 = extra
        in_specs += [pl.BlockSpec((a2.shape[0], tr), lambda r: (0, r)), _full(b2.shape)]
        args += [a2, b2]
    return pl.pallas_call(
        body, name=name, grid=(npart * nr,),
        in_specs=in_specs, out_specs=pl.BlockSpec((tr, D), lambda r: (r, 0)),
        out_shape=jax.ShapeDtypeStruct((npart * R, D), BF16),
        compiler_params=_cp(("parallel",)),
    )(*args)


def _conv_ext(ref, r0, rows, total):
    top = ref[pl.ds(pl.multiple_of(jnp.maximum(r0 - PAD, 0), PAD), PAD), :]
    mid = ref[pl.ds(r0, rows), :]
    bot = ref[pl.ds(pl.multiple_of(jnp.minimum(r0 + rows, total - PAD), PAD), PAD), :]
    top = jnp.where(r0 > 0, top, jnp.zeros_like(top))
    bot = jnp.where(r0 + rows < total, bot, jnp.zeros_like(bot))
    return jnp.concatenate([top, mid, bot], axis=0).astype(F32)


def _shift_rows(a, k):
    return pltpu.roll(a, k % a.shape[0], 0)


def _conv3(x, w, b):
    return w[0:1] * _shift_rows(x, 1) + w[1:2] * x + w[2:3] * _shift_rows(x, -1) + b


def _gate_up_specs(rows_, wblk, nb):
    return [pl.BlockSpec((rows_, wblk), lambda j: (0, j)), pl.BlockSpec((rows_, wblk), lambda j: (0, j + nb))]


def conv_fwd(hu, cw, cb, *, rows, wblk, name):
    L, N2 = hu.shape
    nb = N2 // 2 // wblk
    nchunk = L // rows

    def body(hg_ref, hu_ref, wg_ref, wu_ref, bg_ref, bu_ref, a_ref, s1_ref, s2_ref):
        def chunk(ci, carry):
            r0 = pl.multiple_of(ci * rows, rows)
            gate = _conv3(_conv_ext(hg_ref, r0, rows, L), wg_ref[...], bg_ref[...])[PAD:PAD + rows]
            up = _conv3(_conv_ext(hu_ref, r0, rows, L), wu_ref[...], bu_ref[...])[PAD:PAD + rows]
            sg = jax.nn.sigmoid(gate)
            silu = gate * sg
            at = pl.ds(r0, rows)
            a_ref[at, :] = (silu * up).astype(BF16)
            s1_ref[at, :] = silu.astype(BF16)
            s2_ref[at, :] = (up * (sg + silu * (1.0 - sg))).astype(BF16)
            return carry

        lax.fori_loop(0, nchunk, chunk, 0)

    out = pl.BlockSpec((L, wblk), lambda j: (0, j))
    return pl.pallas_call(
        body, name=name, grid=(nb,),
        in_specs=_gate_up_specs(L, wblk, nb) + _gate_up_specs(3, wblk, nb) + _gate_up_specs(1, wblk, nb),
        out_specs=[out] * 3, out_shape=[jax.ShapeDtypeStruct((L, N2 // 2), BF16)] * 3,
        compiler_params=_cp(("parallel",)),
    )(hu, hu, cw, cw, cb, cb)


def conv_bwd(da, s1, s2, hu, cw, *, rows, wblk, name):
    L, N2 = hu.shape
    F = N2 // 2
    nb = F // wblk
    nchunk = L // rows
    mid = slice(PAD, PAD + rows)

    def body(da_ref, s1_ref, s2_ref, hg_ref, hu_ref, wg_ref, wu_ref, dg_ref, du_ref, dwg_ref, dwu_ref, dbg_ref, dbu_ref):
        for ref in (dwg_ref, dwu_ref, dbg_ref, dbu_ref):
            ref[...] = jnp.zeros_like(ref)

        def half_bwd(x_ref, dh, w_ref, dx_ref, dw_ref, db_ref, r0):
            w = w_ref[...]
            nxt, prv = _shift_rows(dh, -1)[mid], _shift_rows(dh, 1)[mid]
            dhm, xm = dh[mid], x_ref[pl.ds(r0, rows), :].astype(F32)
            dx_ref[pl.ds(r0, rows), :] = (w[0:1] * nxt + w[1:2] * dhm + w[2:3] * prv).astype(BF16)
            db_ref[...] += _colsum(dhm)
            dw_ref[0:1, :] += _colsum(nxt * xm)
            dw_ref[1:2, :] += _colsum(dhm * xm)
            dw_ref[2:3, :] += _colsum(prv * xm)

        def chunk(ci, carry):
            r0 = pl.multiple_of(ci * rows, rows)
            d = _conv_ext(da_ref, r0, rows, L)
            half_bwd(hu_ref, d * _conv_ext(s1_ref, r0, rows, L), wu_ref, du_ref, dwu_ref, dbu_ref, r0)
            half_bwd(hg_ref, d * _conv_ext(s2_ref, r0, rows, L), wg_ref, dg_ref, dwg_ref, dbg_ref, r0)
            return carry

        lax.fori_loop(0, nchunk, chunk, 0)

    blk = lambda r: pl.BlockSpec((r, wblk), lambda j: (0, j))
    return pl.pallas_call(
        body, name=name, grid=(nb,),
        in_specs=[blk(L)] * 3 + _gate_up_specs(L, wblk, nb) + _gate_up_specs(3, wblk, nb),
        out_specs=[blk(L), blk(L), blk(3), blk(3), blk(1), blk(1)],
        out_shape=[jax.ShapeDtypeStruct((L, F), BF16)] * 2 + [jax.ShapeDtypeStruct((3, F), F32)] * 2
        + [jax.ShapeDtypeStruct((1, F), F32)] * 2,
        compiler_params=_cp(("parallel",)),
    )(da, s1, s2, hu, hu, cw, cw)


def _window_sums(pad_ref, w, lead):
    a = pad_ref[...]
    k = 1
    while k < w:
        a = a + _shift_rows(a, -k)
        k *= 2
    return _shift_rows(a, lead) if lead else a


def _pool_counts(L, h):
    t = lax.broadcasted_iota(jnp.int32, (L, 1), 0)
    return (jnp.minimum(t + h, L) - jnp.maximum(t - h, 0)).astype(F32)


def _pooled(u_ref, pad_ref, L, w):
    h = w // 2
    pad_ref[pl.ds(PAD, L), :] = u_ref[...]
    win = _window_sums(pad_ref, w, h)[PAD:PAD + L]
    return win / _pool_counts(L, h) - u_ref[...]


def _zero_pad_edges(pad_ref, L):
    z = jnp.zeros((PAD, LANES), F32)
    pad_ref[pl.ds(0, PAD), :] = z
    pad_ref[pl.ds(PAD + L, PAD), :] = z


def pool_fwd(u, w_pool, pool_scale, *, name):
    L = u.shape[0]

    def body(u_ref, w_ref, ps_ref, p_ref, pad_ref):
        _zero_pad_edges(pad_ref, L)
        for gi, win in enumerate(POOL_WINDOWS):
            @pl.when(pl.program_id(0) == gi)
            def _():
                pooled = _pooled(u_ref, pad_ref, L, win)
                p_ref[...] = (_dot(pooled.astype(BF16), w_ref[...].astype(BF16)) * ps_ref[...]).astype(BF16)

    return pl.pallas_call(
        body, name=name, grid=(len(POOL_WINDOWS),),
        in_specs=[pl.BlockSpec((L, LANES), lambda gi: (0, gi)), pl.BlockSpec((None, LANES, LANES), lambda gi: (gi, 0, 0)),
                  pl.BlockSpec((1, LANES), lambda gi: (0, gi))],
        out_specs=pl.BlockSpec((L, LANES), lambda gi: (0, gi)),
        out_shape=jax.ShapeDtypeStruct((L, 4 * LANES), BF16),
        scratch_shapes=[pltpu.VMEM((L + 2 * PAD, LANES), F32)],
        compiler_params=_cp(("parallel",)),
    )(u, w_pool, pool_scale)


def pool_bwd(u, dpa, w_pool, pool_scale, *, name):
    L = u.shape[0]

    def body(u_ref, dp_ref, w_ref, ps_ref, du_ref, dw_ref, dps_ref, pad_ref):
        _zero_pad_edges(pad_ref, L)
        for gi, win in enumerate(POOL_WINDOWS):
            @pl.when(pl.program_id(0) == gi)
            def _():
                h = win // 2
                wb = w_ref[...].astype(BF16)
                pooled = _pooled(u_ref, pad_ref, L, win).astype(BF16)
                dp = dp_ref[...].astype(F32)
                dps_ref[...] = _colsum(dp * _dot(pooled, wb))
                dy = (dp * ps_ref[...]).astype(BF16)
                dw_ref[...] = _dot_tn(pooled, dy)
                dpooled = _dot_nt(dy, wb)
                pad_ref[pl.ds(PAD, L), :] = dpooled / _pool_counts(L, h)
                du_ref[...] = (_window_sums(pad_ref, win, h - 1)[PAD:PAD + L] - dpooled).astype(BF16)

    return pl.pallas_call(
        body, name=name, grid=(len(POOL_WINDOWS),),
        in_specs=[pl.BlockSpec((L, LANES), lambda gi: (0, gi)), pl.BlockSpec((L, LANES), lambda gi: (0, gi)),
                  pl.BlockSpec((None, LANES, LANES), lambda gi: (gi, 0, 0)), pl.BlockSpec((1, LANES), lambda gi: (0, gi))],
        out_specs=[pl.BlockSpec((L, LANES), lambda gi: (0, gi)), pl.BlockSpec((None, LANES, LANES), lambda gi: (gi, 0, 0)),
                   pl.BlockSpec((1, LANES), lambda gi: (0, gi))],
        out_shape=[jax.ShapeDtypeStruct((L, 4 * LANES), BF16), jax.ShapeDtypeStruct((4, LANES, LANES), F32),
                   jax.ShapeDtypeStruct((1, 4 * LANES), F32)],
        scratch_shapes=[pltpu.VMEM((L + 2 * PAD, LANES), F32)],
        compiler_params=_cp(("parallel",)),
    )(u, dpa, w_pool, pool_scale)


def _attn_probs(qk, band_k, ctx_k, sink_ref, kh, mask4):
    s_loc = jnp.where(mask4, _dot_nt(qk, band_k), NEG_INF)
    s_ctx = _dot_nt(qk, ctx_k)
    sk = jnp.concatenate([jnp.full((BLK, 1), sink_ref[kh * GQA + hh], F32) for hh in range(GQA)], axis=0)
    m = jnp.maximum(jnp.maximum(jnp.max(s_loc, axis=-1, keepdims=True), jnp.max(s_ctx, axis=-1, keepdims=True)), sk)
    e_loc, e_ctx, e_s = jnp.exp(s_loc - m), jnp.exp(s_ctx - m), jnp.exp(sk - m)
    inv = 1.0 / (jnp.sum(e_loc, axis=-1, keepdims=True) + jnp.sum(e_ctx, axis=-1, keepdims=True) + e_s)
    return e_loc * inv, e_ctx * inv, e_s * inv


def _attn_block(n, L):
    start = pl.multiple_of(jnp.clip((n - 1) * BLK, 0, L - 3 * BLK), BLK)
    qpos = n * BLK + lax.broadcasted_iota(jnp.int32, (BLK, 3 * BLK), 0)
    kpos = start + lax.broadcasted_iota(jnp.int32, (BLK, 3 * BLK), 1)
    mask = jnp.abs(kpos - qpos) <= WINDOW
    return start, jnp.concatenate([mask] * GQA, axis=0)


def _stack_slabs(ref):
    return jnp.concatenate([ref[:, s * LANES:(s + 1) * LANES] for s in range(GQA)], axis=0)


def _kv_head_lanes(kh):
    return (lax.broadcasted_iota(jnp.int32, (1, LANES), 1) // HEAD_DIM) == kh


def permute_heads(w, inverse=False):
    lo, hi = 4 * LANES, 8 * LANES
    mid = w[lo:hi].reshape(*((GQA, N_KV_HEADS) if inverse else (N_KV_HEADS, GQA)), HEAD_DIM, w.shape[1])
    return jnp.concatenate([w[:lo], mid.swapaxes(0, 1).reshape(hi - lo, w.shape[1]), w[hi:]], axis=0)


def attn_fwd(q, kv, kvc, sink, *, name):
    L = q.shape[0]
    C = kvc.shape[0]
    scale = HEAD_DIM ** -0.5

    def body(q_ref, kv_ref, kvc_ref, sink_ref, o_ref):
        start, mask4 = _attn_block(pl.program_id(0), L)
        band = kv_ref[pl.ds(start, 3 * BLK), :]
        kvc_ = kvc_ref[...]
        qs = _stack_slabs(q_ref) * scale
        o = jnp.zeros((GQA * BLK, LANES), F32)
        for kh in range(N_KV_HEADS):
            grp = _kv_head_lanes(kh)
            qk = jnp.where(grp, qs, jnp.zeros_like(qs))
            p_loc, p_ctx, _ = _attn_probs(qk, band[:, :LANES], kvc_[:, :LANES], sink_ref, kh, mask4)
            o = o + jnp.where(grp, _dot(p_loc.astype(BF16), band[:, LANES:]) + _dot(p_ctx.astype(BF16), kvc_[:, LANES:]), 0.0)
        for s in range(GQA):
            o_ref[:, s * LANES:(s + 1) * LANES] = o[s * BLK:(s + 1) * BLK].astype(BF16)

    return pl.pallas_call(
        body, name=name, grid=(L // BLK,),
        in_specs=[pl.BlockSpec((BLK, 4 * LANES), lambda n: (n, 0)), _full((L, 2 * LANES)), _full((C, 2 * LANES)),
                  pl.BlockSpec(memory_space=pltpu.SMEM)],
        out_specs=pl.BlockSpec((BLK, 4 * LANES), lambda n: (n, 0)),
        out_shape=jax.ShapeDtypeStruct((L, 4 * LANES), BF16),
        compiler_params=_cp(("parallel",)),
    )(q, kv, kvc, sink)


def attn_bwd(q, kv, kvc, sink, dpa, cos, sa, sb, *, name):
    L = q.shape[0]
    C = kvc.shape[0]
    nb = L // BLK
    scale = HEAD_DIM ** -0.5

    def body(q_ref, kv_ref, kvc_ref, sink_ref, do_ref, c_ref, sa_ref, sb_ref, cq_ref, saq_ref, sbq_ref,
             dq_ref, dkv_ref, dkvc_ref, dsink_ref, dkv_acc, dkvc_acc):
        n = pl.program_id(0)

        @pl.when(n == 0)
        def _():
            dkv_acc[...] = jnp.zeros_like(dkv_acc)
            dkvc_acc[...] = jnp.zeros_like(dkvc_acc)
            dsink_ref[...] = jnp.zeros_like(dsink_ref)

        start, mask4 = _attn_block(n, L)
        band = kv_ref[pl.ds(start, 3 * BLK), :]
        kvc_ = kvc_ref[...]
        band_k, band_v, ctx_k, ctx_v = band[:, :LANES], band[:, LANES:], kvc_[:, :LANES], kvc_[:, LANES:]
        qs = _stack_slabs(q_ref) * scale
        dos = _stack_slabs(do_ref)
        lane = lax.broadcasted_iota(jnp.int32, (1, LANES), 1)
        dsink = jnp.zeros((1, LANES), F32)
        dq = jnp.zeros((GQA * BLK, LANES), F32)
        dk = jnp.zeros((LANES, 3 * BLK), F32)
        dv = jnp.zeros((LANES, 3 * BLK), F32)
        dkc = jnp.zeros((LANES, C), F32)
        dvc = jnp.zeros((LANES, C), F32)
        for kh in range(N_KV_HEADS):
            grp = _kv_head_lanes(kh)
            qk = jnp.where(grp, qs, jnp.zeros_like(qs))
            dok = jnp.where(grp, dos, jnp.zeros_like(dos))
            p_loc, p_ctx, p_s = _attn_probs(qk, band_k, ctx_k, sink_ref, kh, mask4)
            dp_loc = _dot_nt(dok, band_v)
            dp_ctx = _dot_nt(dok, ctx_v)
            delta = jnp.sum(p_loc * dp_loc, axis=-1, keepdims=True) + jnp.sum(p_ctx * dp_ctx, axis=-1, keepdims=True)
            ds_loc = (p_loc * (dp_loc - delta)).astype(BF16)
            ds_ctx = (p_ctx * (dp_ctx - delta)).astype(BF16)
            dsk = p_s * delta
            for hh in range(GQA):
                dsink = dsink - jnp.where(lane == kh * GQA + hh, jnp.sum(dsk[hh * BLK:(hh + 1) * BLK], axis=0, keepdims=True), 0.0)
            dq = dq + jnp.where(grp, _dot(ds_loc, band_k) + _dot(ds_ctx, ctx_k), 0.0)
            dk = dk + _dot_tn(qk, ds_loc)
            dv = dv + _dot_tn(dok, p_loc.astype(BF16))
            dkc = dkc + _dot_tn(qk, ds_ctx)
            dvc = dvc + _dot_tn(dok, p_ctx.astype(BF16))
        dsink_ref[...] += dsink
        dkv_acc[:LANES, pl.ds(start, 3 * BLK)] += dk
        dkv_acc[LANES:, pl.ds(start, 3 * BLK)] += dv
        dkvc_acc[:LANES, :] += dkc
        dkvc_acc[LANES:, :] += dvc
        c, a, b = cq_ref[...], -saq_ref[...], -sbq_ref[...]
        for s in range(GQA):
            dq_ref[:, s * LANES:(s + 1) * LANES] = _rope(dq[s * BLK:(s + 1) * BLK] * scale, c, a, b).astype(BF16)

        @pl.when(n == nb - 1)
        def _():
            dkv_ref[:, :LANES] = _rope(dkv_acc[:LANES, :].T, c_ref[...], -sa_ref[...], -sb_ref[...]).astype(BF16)
            dkv_ref[:, LANES:] = dkv_acc[LANES:, :].T.astype(BF16)
            dkvc_ref[...] = dkvc_acc[...].T.astype(BF16)

    blk = lambda w: pl.BlockSpec((BLK, w), lambda n: (n, 0))
    return pl.pallas_call(
        body, name=name, grid=(nb,),
        in_specs=[blk(4 * LANES), _full((L, 2 * LANES)), _full((C, 2 * LANES)), pl.BlockSpec(memory_space=pltpu.SMEM),
                  pl.BlockSpec((BLK, 4 * LANES), lambda n: (n, 1)),
                  _full((L, LANES)), _full((L, LANES)), _full((L, LANES)), blk(LANES), blk(LANES), blk(LANES)],
        out_specs=[blk(4 * LANES), _full((L, 2 * LANES)), _full((C, 2 * LANES)), _full((1, LANES))],
        out_shape=[jax.ShapeDtypeStruct((L, 4 * LANES), BF16), jax.ShapeDtypeStruct((L, 2 * LANES), BF16),
                   jax.ShapeDtypeStruct((C, 2 * LANES), BF16), jax.ShapeDtypeStruct((1, LANES), F32)],
        scratch_shapes=[pltpu.VMEM((2 * LANES, L), F32), pltpu.VMEM((2 * LANES, C), F32)],
        compiler_params=_cp(("arbitrary",)),
    )(q, kv, kvc, sink, dpa, cos, sa, sb, cos, sa, sb)


def _gelu_parts(x):
    th = jnp.tanh(SQRT_2_OVER_PI * (x + GELU_C * x * x * x))
    return 0.5 * x * (1.0 + th), th


def _gelu_grad(x, th):
    return 0.5 * (1.0 + th) + 0.5 * x * (1.0 - th * th) * SQRT_2_OVER_PI * (1.0 + 3.0 * GELU_C * x * x)


def _layernorm(v):
    mu = jnp.mean(v, axis=-1, keepdims=True)
    vc = v - mu
    rstd = lax.rsqrt(jnp.mean(vc * vc, axis=-1, keepdims=True) + EPS)
    return vc * rstd, rstd


def sgu_fwd(z1, ln_g, ln_b, ws, bst, *, name):
    L, W2 = z1.shape
    W = W2 // 2
    ng = W // LANES

    def body(z_ref, g_ref, b_ref, ws_ref, bs_ref, o_ref):
        z, _ = _gelu_parts(z_ref[...].astype(F32))
        xhat, _ = _layernorm(z[:, W:])
        vln = (xhat * g_ref[...] + b_ref[...]).astype(BF16)
        for gi in range(ng):
            cs = slice(gi * LANES, (gi + 1) * LANES)
            s = _dot(ws_ref[gi], vln[:, cs]) + bs_ref[:, gi:gi + 1]
            o_ref[:, cs] = (z[:, cs] * s).astype(BF16)

    vec = _full((1, W))
    return pl.pallas_call(
        body, name=name, grid=(L // BLK,),
        in_specs=[pl.BlockSpec((BLK, W2), lambda n: (n, 0)), vec, vec, _full((ng, LANES, LANES)), _full((BLK, ng))],
        out_specs=pl.BlockSpec((BLK, W), lambda n: (n, 0)),
        out_shape=jax.ShapeDtypeStruct((L, W), BF16),
        compiler_params=_cp(("parallel",)),
    )(z1, ln_g, ln_b, ws, bst)


def sgu_bwd(z1, dus, ln_g, ln_b, ws, wst, bst, *, name):
    L, W2 = z1.shape
    W = W2 // 2
    ng = W // LANES

    def body(z_ref, d_ref, g_ref, b_ref, ws_ref, wst_ref, bs_ref, dz_ref, dws_ref, dbs_ref, dg_ref, db_ref, dv_scr):
        @pl.when(pl.program_id(0) == 0)
        def _():
            dws_ref[...] = jnp.zeros_like(dws_ref)
            dbs_ref[...] = jnp.zeros_like(dbs_ref)
            dg_ref[...] = jnp.zeros_like(dg_ref)
            db_ref[...] = jnp.zeros_like(db_ref)

        zp = z_ref[...].astype(F32)
        z, th = _gelu_parts(zp)
        xhat, rstd = _layernorm(z[:, W:])
        vln = (xhat * g_ref[...] + b_ref[...]).astype(BF16)
        d = d_ref[...].astype(F32)
        lane = lax.broadcasted_iota(jnp.int32, (1, LANES), 1)
        dbs = jnp.zeros((BLK, LANES), F32)
        dgel = _gelu_grad(zp, th)
        for gi in range(ng):
            cs = slice(gi * LANES, (gi + 1) * LANES)
            s = _dot(ws_ref[gi], vln[:, cs]) + bs_ref[:, gi:gi + 1]
            dz_ref[:, cs] = (d[:, cs] * s * dgel[:, cs]).astype(BF16)
            ds = d[:, cs] * z[:, cs]
            dbs = dbs + jnp.where(lane == gi, jnp.sum(ds, axis=-1, keepdims=True), 0.0)
            dsb = ds.astype(BF16)
            dws_ref[gi] += _dot_nt(dsb, vln[:, cs])
            dv_scr[:, cs] = _dot(wst_ref[gi], dsb)
        dbs_ref[...] += dbs
        dvln = dv_scr[...]
        dg_ref[...] += _colsum(dvln * xhat)
        db_ref[...] += _colsum(dvln)
        dxh = dvln * g_ref[...]
        dv = rstd * (dxh - jnp.mean(dxh, axis=-1, keepdims=True) - xhat * jnp.mean(dxh * xhat, axis=-1, keepdims=True))
        dz_ref[:, W:] = (dv * dgel[:, W:]).astype(BF16)

    vec = _full((1, W))
    return pl.pallas_call(
        body, name=name, grid=(L // BLK,),
        in_specs=[pl.BlockSpec((BLK, W2), lambda n: (n, 0)), pl.BlockSpec((BLK, W), lambda n: (n, 0)), vec, vec,
                  _full((ng, LANES, LANES)), _full((ng, LANES, LANES)), _full((BLK, ng))],
        out_specs=[pl.BlockSpec((BLK, W2), lambda n: (n, 0)), _full((ng, LANES, LANES)), _full((BLK, LANES)), vec, vec],
        out_shape=[jax.ShapeDtypeStruct((L, W2), BF16), jax.ShapeDtypeStruct((ng, LANES, LANES), F32),
                   jax.ShapeDtypeStruct((BLK, LANES), F32), jax.ShapeDtypeStruct((1, W), F32), jax.ShapeDtypeStruct((1, W), F32)],
        scratch_shapes=[pltpu.VMEM((BLK, W), F32)],
        compiler_params=_cp(("arbitrary",)),
    )(z1, dus, ln_g, ln_b, ws, wst, bst)


def _adamw_math(w, m, v, g):
    m_ = ADAM_B1 * m + (1.0 - ADAM_B1) * g
    v_ = ADAM_B2 * v + (1.0 - ADAM_B2) * (g * g)
    return -ADAM_LR * ((m_ / BC1) / (jnp.sqrt(v_ / BC2) + ADAM_EPS) + ADAM_WD * w), m_, v_


def adamw(w, m, v, gparts, *, tr, name):
    NL, R, Wd = w.shape
    nr = R // tr

    def body(w_ref, m_ref, v_ref, *rest):
        gp_refs, (g_ref, d_ref, nm_ref, nv_ref) = rest[:NL], rest[NL:]
        for l in range(NL):
            @pl.when(pl.program_id(0) == l)
            def _():
                g = gp_refs[l][0].astype(F32)
                for s in range(1, gp_refs[l].shape[0]):
                    g = g + gp_refs[l][s].astype(F32)
                g_ref[...] = g
                d_ref[...], nm_ref[...], nv_ref[...] = _adamw_math(w_ref[...], m_ref[...], v_ref[...], g)

    row = pl.BlockSpec((None, tr, Wd), lambda l, i: (l, i, 0))
    gspecs = [pl.BlockSpec((gparts[l].shape[0], tr, Wd), (lambda l_, i, l=l: (0, jnp.clip(i + (l_ - l) * nr, 0, nr - 1), 0)))
              for l in range(NL)]
    return pl.pallas_call(
        body, name=name, grid=(NL, nr),
        in_specs=[row, row, row] + gspecs, out_specs=[row] * 4, out_shape=[jax.ShapeDtypeStruct((NL, R, Wd), F32)] * 4,
        compiler_params=_cp(("arbitrary", "arbitrary")),
    )(w, m, v, *gparts)


def small_update(gpacks, me, params, loss_row, *, name):
    n = len(params)

    def body(me_ref, gp_ref, *refs):
        ins, outs, gs_ref = refs[:3 * n], refs[3 * n:-1], refs[-1]
        gs_ref[...] = gp_ref[0].astype(F32)
        for dv in range(1, N_DEV):
            gs_ref[...] += gp_ref[dv].astype(F32)
        for p, (w, _, _, off, per_dev) in enumerate(params):
            w_ref, m_ref, v_ref = ins[3 * p:3 * p + 3]
            g_ref, d_ref, nm_ref, nv_ref = outs[4 * p:4 * p + 4]
            rows, cols = w.shape
            if cols == LANES and rows % 8 == 0 and not per_dev:
                g = gs_ref[off:off + rows, :]
                g_ref[...] = g
                d_ref[...], nm_ref[...], nv_ref[...] = _adamw_math(w_ref[...], m_ref[...], v_ref[...], g)
                continue
            chunks = -(-cols // LANES)
            base = off + me_ref[0] * per_dev if per_dev else off
            for i in range(rows):
                for j in range(chunks):
                    wd = min(LANES, cols - j * LANES)
                    at = (slice(i, i + 1), slice(j * LANES, j * LANES + wd))
                    g = gs_ref[pl.ds(base + i * chunks + j, 1), 0:wd]
                    g_ref[at] = g
                    d_ref[at], nm_ref[at], nv_ref[at] = _adamw_math(w_ref[at], m_ref[at], v_ref[at], g)
        outs[-1][...] = jnp.sum(gs_ref[loss_row:loss_row + 1, :], axis=1, keepdims=True)

    vm = pl.BlockSpec(memory_space=pltpu.VMEM)
    flat = [a for w, m, v, _, _ in params for a in (w, m, v)]
    out_shape = [jax.ShapeDtypeStruct(w.shape, F32) for w, _, _, _, _ in params for _ in range(4)] + [jax.ShapeDtypeStruct((1, 1), F32)]
    return pl.pallas_call(
        body, name=name, in_specs=[pl.BlockSpec(memory_space=pltpu.SMEM)] + [vm] * (1 + len(flat)),
        out_specs=[vm] * len(out_shape), out_shape=out_shape,
        scratch_shapes=[pltpu.VMEM(gpacks.shape[1:], F32)],
        compiler_params=pltpu.CompilerParams(vmem_limit_bytes=VMEM_LIMIT),
    )(me, gpacks, *flat)


def ada_fwd_mm(cs, w_ada, b_loc, *, name):
    R, D = cs.shape
    nl, _, n = w_ada.shape

    def body(c_ref, w_ref, b_ref, s_ref, m_ref):
        c = c_ref[...]
        s = c * jax.nn.sigmoid(c)
        s_ref[...] = s
        for i in range(nl):
            m_ref[i] = _dot(s.astype(BF16), w_ref[i].astype(BF16)) + b_ref[i:i + 1, :]

    return pl.pallas_call(
        body, name=name, in_specs=[_full((R, D)), _full((nl, D, n)), _full((nl, n))],
        out_specs=[_full((R, D)), _full((nl, R, n))], grid=(1,),
        out_shape=[jax.ShapeDtypeStruct((R, D), F32), jax.ShapeDtypeStruct((nl, R, n), F32)],
        compiler_params=_cp(("arbitrary",)),
    )(cs, w_ada, b_loc)


def ada_bwd_mm(s, c_ctx, dall, w_ada, *, name):
    R, D = s.shape
    nl, _, n = w_ada.shape

    def body(s_ref, cc_ref, d_ref, w_ref, gw_ref, dcc_ref):
        sb = s_ref[...].astype(BF16)
        row = lax.broadcasted_iota(jnp.int32, (R, 1), 0)
        dctx = d_ref[0, 1:2, :]
        for dv in range(1, N_DEV):
            dctx = dctx + d_ref[dv, 1:2, :]
        for i in range(nl):
            dm = jnp.zeros((R, n), F32)
            for dv in range(N_DEV):
                dm = dm + jnp.where(row == dv, d_ref[dv, 2 * i:2 * i + 1, :], 0.0)
            if i == 0:
                dm = dm + jnp.where(row == N_DEV, dctx, 0.0)
            gw_ref[i] = _dot_tn(sb, dm.astype(BF16))
        cc = cc_ref[...]
        sg = jax.nn.sigmoid(cc)
        ds = _dot_nt(jnp.broadcast_to(dctx, (8, n)).astype(BF16), w_ref[0].astype(BF16))
        dcc_ref[...] = ds * (sg * (1.0 + cc * (1.0 - sg)))

    return pl.pallas_call(
        body, name=name, grid=(1,),
        in_specs=[_full((R, D)), _full((1, D)), _full((N_DEV, 3, n)), _full((nl, D, n))],
        out_specs=[_full((nl, D, n)), _full((8, D))],
        out_shape=[jax.ShapeDtypeStruct((nl, D, n), F32), jax.ShapeDtypeStruct((8, D), F32)],
        compiler_params=_cp(("arbitrary",)),
    )(s, c_ctx, dall, w_ada)


def _place():
    x, y, c = lax.axis_index("x"), lax.axis_index("y"), lax.axis_index("c")
    return x, y, c


def _lin(p):
    return 4 * p[0] + 2 * p[1] + p[2]


def all_gather_small(xb, *, name):
    R, W = xb.shape

    def body(x_ref, out_ref, send_sems, recv_sems, local_sem):
        x, y, c = _place()
        me = _lin((x, y, c))
        mine = pltpu.make_async_copy(x_ref, out_ref.at[me], local_sem)
        mine.start()
        copies = []
        for k in range(1, N_DEV):
            peer = (x ^ (k >> 2), y ^ ((k >> 1) & 1), c ^ (k & 1))
            mk = lambda dst, k=k, peer=peer: pltpu.make_async_remote_copy(
                src_ref=x_ref, dst_ref=dst, send_sem=send_sems.at[k - 1], recv_sem=recv_sems.at[k - 1], device_id=peer, device_id_type=MESH)
            mk(out_ref.at[me]).start()
            copies.append(mk(out_ref.at[_lin(peer)]))
        for cp in copies:
            cp.wait_recv()
        for cp in copies:
            cp.wait_send()
        mine.wait()

    vm = pl.BlockSpec(memory_space=pltpu.VMEM)
    return pl.pallas_call(
        body, name=name, in_specs=[vm], out_specs=vm, out_shape=jax.ShapeDtypeStruct((N_DEV, R, W), xb.dtype),
        scratch_shapes=[pltpu.SemaphoreType.DMA((7,)), pltpu.SemaphoreType.DMA((7,)), pltpu.SemaphoreType.DMA],
        compiler_params=pltpu.CompilerParams(vmem_limit_bytes=VMEM_LIMIT),
    )(xb)


HBM_SPEC = pl.BlockSpec(memory_space=pltpu.HBM)
SEM_SPEC = pl.BlockSpec(memory_space=pltpu.SEMAPHORE)
ORDERED_EFFECT = pltpu.SideEffectType.DATAFLOW_SIDE_EFFECTING


def _exchange_copies(srcs, lands, sems, scatter):
    x, y, c = _place()
    me = _lin((x, y, c))
    for j in range(len(srcs)):
        r = lands[j].shape[0] // N_DEV
        block = lambda d, j=j, r=r: pl.ds(pl.multiple_of(d * r, 16), r)
        for k in range(1, N_DEV):
            peer = (x ^ (k >> 2), y ^ ((k >> 1) & 1), c ^ (k & 1))
            src = srcs[j].at[block(_lin(peer)), :] if scatter else srcs[j]
            mk = lambda dst, j=j, k=k, peer=peer, src=src: pltpu.make_async_remote_copy(
                src_ref=src, dst_ref=dst, send_sem=sems[2 * j].at[k - 1], recv_sem=sems[2 * j + 1].at[k - 1],
                device_id=peer, device_id_type=MESH)
            yield mk(lands[j].at[block(me), :]), mk(lands[j].at[block(_lin(peer)), :])


def exchange_start(srcs, lands, *, scatter, name):
    nw = len(srcs)

    def body(*refs):
        for start, _ in _exchange_copies(refs[:nw], refs[nw:2 * nw], refs[2 * nw:4 * nw], scatter):
            start.start()
        refs[-1][...] = jnp.zeros_like(refs[-1])

    thru = [pltpu.HBM(a.shape, a.dtype) for a in (*srcs, *lands)]
    res = pl.pallas_call(
        body, name=name, in_specs=[HBM_SPEC] * (2 * nw),
        out_specs=[SEM_SPEC] * (2 * nw) + [HBM_SPEC] * (2 * nw) + [pl.BlockSpec(memory_space=pltpu.VMEM)],
        out_shape=[pltpu.SemaphoreType.DMA((N_DEV - 1,))] * (2 * nw) + thru + [jax.ShapeDtypeStruct((8, LANES), F32)],
        input_output_aliases={i: 2 * nw + i for i in range(2 * nw)},
        compiler_params=pltpu.CompilerParams(has_side_effects=ORDERED_EFFECT),
    )(*[pltpu.with_memory_space_constraint(a, pltpu.HBM) for a in (*srcs, *lands)])
    return res[:2 * nw], res[2 * nw:3 * nw], res[3 * nw:4 * nw], res[-1]


def exchange_wait(srcs, lands, sems, after, *, scatter, name):
    nw = len(srcs)

    def body(*refs):
        for _, arrive in _exchange_copies(refs[:nw], refs[nw:2 * nw], refs[2 * nw:4 * nw], scatter):
            arrive.wait_send()
            arrive.wait_recv()

    res = pl.pallas_call(
        body, name=name, in_specs=[HBM_SPEC] * (2 * nw) + [SEM_SPEC] * (2 * nw) + [pl.BlockSpec(memory_space=pl.ANY)],
        out_specs=[HBM_SPEC] * (2 * nw), out_shape=[pltpu.HBM(a.shape, a.dtype) for a in (*srcs, *lands)],
        input_output_aliases={i: i for i in range(2 * nw)},
        compiler_params=pltpu.CompilerParams(has_side_effects=ORDERED_EFFECT),
    )(*srcs, *lands, *sems, after)
    return res[nw:]


def place_own(srcs, rows, me, *, scatter, name):
    nw = len(srcs)
    lands = [lax.empty((N_DEV * r, s_.shape[1]), s_.dtype) for r, s_ in zip(rows, srcs)]

    def body(me_ref, *refs):
        for j in range(nw):
            refs[2 * nw + j][...] = refs[j][...]

    mine = lambda i, me_ref: (me_ref[0], 0)
    src_at = mine if scatter else (lambda i, me_ref: (0, 0))
    blocks = [(r, s_.shape[1]) for r, s_ in zip(rows, srcs)]
    return pl.pallas_call(
        body, name=name,
        grid_spec=pltpu.PrefetchScalarGridSpec(
            num_scalar_prefetch=1, grid=(1,),
            in_specs=[pl.BlockSpec(b_, src_at) for b_ in blocks] + [pl.BlockSpec(memory_space=pl.ANY)] * nw,
            out_specs=[pl.BlockSpec(b_, mine) for b_ in blocks]),
        out_shape=[jax.ShapeDtypeStruct(l_.shape, l_.dtype) for l_ in lands],
        input_output_aliases={1 + nw + j: j for j in range(nw)},
        compiler_params=_cp(("arbitrary",)),
    )(jnp.reshape(me, (1,)).astype(jnp.int32), *srcs, *lands)


def _rope_tables(L):
    t = jnp.arange(L)
    inv = ROPE_BASE ** (-jnp.arange(ROPE_FREQS, dtype=F32) / ROPE_FREQS)
    ar = (t // GRID_W).astype(F32)[:, None] * inv
    ac = (t % GRID_W).astype(F32)[:, None] * inv
    z = jnp.zeros_like(ar)
    cos = jnp.concatenate([jnp.cos(ar), jnp.cos(ar), jnp.cos(ac), jnp.cos(ac)], axis=1)
    sa = jnp.concatenate([-jnp.sin(ar), z, -jnp.sin(ac), z], axis=1)
    sb = jnp.concatenate([z, jnp.sin(ar), z, jnp.sin(ac)], axis=1)
    return tuple(jnp.tile(a, (1, LANES // HEAD_DIM)) for a in (cos, sa, sb))


def _nat2d(a):
    return a.reshape(1, -1) if a.ndim == 1 else a.reshape(-1, a.shape[-1])


def _pack_rows(a):
    rows, cols = a.shape
    chunks = -(-cols // LANES)
    f = jnp.pad(a, ((0, 0), (0, chunks * LANES - cols))).reshape(rows * chunks, LANES)
    return jnp.pad(f, ((0, -f.shape[0] % 8), (0, 0)))


def _rows128(a):
    f = a.reshape(-1)
    n = -(-f.shape[0] // (8 * LANES)) * 8 * LANES
    return jnp.pad(f, (0, n - f.shape[0])).reshape(-1, LANES)


def kernel(x, c, ctx, c_ctx, w_ada, b_ada, g_mix_pre, g_mix_post, g_ffn_pre, g_ffn_post, w_in_even, w_pool, pool_scale, attn_sink, w_out_even, w_in_odd, sgu_ln_g, sgu_ln_b, sgu_w, sgu_b, w_out_odd, w_ffn_up, ffn_conv_w, ffn_conv_b, w_ffn_down, loss_target, m_c_ctx, m_w_ada, m_b_ada, m_g_mix_pre, m_g_mix_post, m_g_ffn_pre, m_g_ffn_post, m_w_in_even, m_w_pool, m_pool_scale, m_attn_sink, m_w_out_even, m_w_in_odd, m_sgu_ln_g, m_sgu_ln_b, m_sgu_w, m_sgu_b, m_w_out_odd, m_w_ffn_up, m_ffn_conv_w, m_ffn_conv_b, m_w_ffn_down, v_c_ctx, v_w_ada, v_b_ada, v_g_mix_pre, v_g_mix_post, v_g_ffn_pre, v_g_ffn_post, v_w_in_even, v_w_pool, v_pool_scale, v_attn_sink, v_w_out_even, v_w_in_odd, v_sgu_ln_g, v_sgu_ln_b, v_sgu_w, v_sgu_b, v_w_out_odd, v_w_ffn_up, v_ffn_conv_w, v_ffn_conv_b, v_w_ffn_down):
    P = dict(c_ctx=c_ctx, w_ada=w_ada, b_ada=b_ada, g_mix_pre=g_mix_pre, g_mix_post=g_mix_post, g_ffn_pre=g_ffn_pre,
             g_ffn_post=g_ffn_post, w_in_even=w_in_even, w_pool=w_pool, pool_scale=pool_scale, attn_sink=attn_sink,
             w_out_even=w_out_even, w_in_odd=w_in_odd, sgu_ln_g=sgu_ln_g, sgu_ln_b=sgu_ln_b, sgu_w=sgu_w, sgu_b=sgu_b,
             w_out_odd=w_out_odd, w_ffn_up=w_ffn_up, ffn_conv_w=ffn_conv_w, ffn_conv_b=ffn_conv_b, w_ffn_down=w_ffn_down)
    M = dict(c_ctx=m_c_ctx, w_ada=m_w_ada, b_ada=m_b_ada, g_mix_pre=m_g_mix_pre, g_mix_post=m_g_mix_post, g_ffn_pre=m_g_ffn_pre,
             g_ffn_post=m_g_ffn_post, w_in_even=m_w_in_even, w_pool=m_w_pool, pool_scale=m_pool_scale, attn_sink=m_attn_sink,
             w_out_even=m_w_out_even, w_in_odd=m_w_in_odd, sgu_ln_g=m_sgu_ln_g, sgu_ln_b=m_sgu_ln_b, sgu_w=m_sgu_w, sgu_b=m_sgu_b,
             w_out_odd=m_w_out_odd, w_ffn_up=m_w_ffn_up, ffn_conv_w=m_ffn_conv_w, ffn_conv_b=m_ffn_conv_b, w_ffn_down=m_w_ffn_down)
    V = dict(c_ctx=v_c_ctx, w_ada=v_w_ada, b_ada=v_b_ada, g_mix_pre=v_g_mix_pre, g_mix_post=v_g_mix_post, g_ffn_pre=v_g_ffn_pre,
             g_ffn_post=v_g_ffn_post, w_in_even=v_w_in_even, w_pool=v_w_pool, pool_scale=v_pool_scale, attn_sink=v_attn_sink,
             w_out_even=v_w_out_even, w_in_odd=v_w_in_odd, sgu_ln_g=v_sgu_ln_g, sgu_ln_b=v_sgu_ln_b, sgu_w=v_sgu_w, sgu_b=v_sgu_b,
             w_out_odd=v_w_out_odd, w_ffn_up=v_w_ffn_up, ffn_conv_w=v_ffn_conv_w, ffn_conv_b=v_ffn_conv_b, w_ffn_down=v_w_ffn_down)

    x = x[0]
    ctx = ctx[0]
    target = loss_target[0]
    L, D = x.shape
    C = ctx.shape[0]
    tm = min(512, L)
    tm_up = min(1024, L)
    conv_rows = min(512, L)
    me = 4 * lax.axis_index("x") + 2 * lax.axis_index("y") + lax.axis_index("c")
    n_ada = w_ada.shape[2]
    F = w_ffn_down.shape[1] * N_DEV
    half_f = F // 2

    n_cw = ffn_conv_w.shape[2]
    small = jnp.concatenate([_rows128(c), _rows128(sgu_ln_g), _rows128(sgu_ln_b), _rows128(ffn_conv_w)], axis=0)
    small_all = all_gather_small(small, name="gather_small_inputs")
    c_all = small_all[:, :8].reshape(N_DEV, D)
    ln_g = small_all[:, 8].reshape(1, D)
    ln_b = small_all[:, 16].reshape(1, D)
    conv_w = small_all[:, 24:].reshape(N_DEV, -1)[:, :2 * 3 * n_cw].reshape(N_DEV, 2, 3, n_cw)
    conv_w = conv_w.transpose(1, 2, 0, 3).reshape(2, 3, 2 * F)

    cs = jnp.concatenate([c_all, c_ctx[None, :], jnp.zeros((7, D), F32)], axis=0)
    b_loc = lax.dynamic_slice(b_ada, (0, me * n_ada), (2, n_ada))
    silu_c, mods_loc = ada_fwd_mm(cs, w_ada, b_loc, name="ada_fwd")
    mods_all = all_gather_small(mods_loc.reshape(-1, LANES), name="gather_mods")

    shards = [s.astype(BF16) for s in (w_in_even[0].T, w_out_even[0], w_ffn_up[0].T, w_ffn_down[0],
                                       w_in_odd[0].T, w_out_odd[0], w_ffn_up[1].T, w_ffn_down[1])]
    shards, mods_all = lax.optimization_barrier((shards, mods_all))
    w_sems, w_srcs, w_lands, _ = exchange_start(shards, place_own(shards, [s.shape[0] for s in shards], me, scatter=False, name="gather_own"),
                                              scatter=False, name="gather_start")

    def weight(j, after):
        return exchange_wait([w_srcs[j]], [w_lands[j]], w_sems[2 * j:2 * j + 2], after, scatter=False, name=f"gather_wait_{j}")[0]

    mods_all = mods_all.reshape(N_DEV, 2, 16, n_ada).transpose(1, 2, 0, 3).reshape(2, 16, 6 * D)
    mod = lambda i, row: [m_[None, :] for m_ in jnp.split(lax.dynamic_index_in_dim(mods_all[i], row, 0, False), 6)]
    sh_m, sc_m, gt_m, sh_f, sc_f, gt_f = zip(mod(0, me), mod(1, me))
    csh_m, csc_m = mod(0, N_DEV)[:2]

    row = lambda a, i: a[i][None, :]

    cos, sa, sb = _rope_tables(L)
    sink = attn_sink[0]
    bst = sgu_b[0].T
    sgu_wb, sgu_wtb = sgu_w[0].astype(BF16), sgu_w[0].swapaxes(1, 2).astype(BF16)
    wup, wdn = [None, None], [None, None]

    def ffn_fwd(i, xin):
        wup[i] = weight(2 + 4 * i, xin)
        h, hu = pre_mm(xin, row(g_ffn_pre, i), sh_f[i], sc_f[i], wup[i], tm=tm_up, tn=half_f, name=f"ffn_up_{i}")
        a, s1, s2 = conv_fwd(hu, conv_w[i], ffn_conv_b[i][None, :], rows=conv_rows, wblk=2 * LANES, name=f"ffn_conv_{i}")
        wdn[i] = weight(3 + 4 * i, a)
        res = mm_post([a], wdn[i], xin, row(g_ffn_post, i), gt_f[i], tm=tm, target=target if i == 1 else None, name=f"ffn_down_{i}")
        return (h, (hu, s1, s2), a, *res)

    first_mod, cos, sa, sb = lax.optimization_barrier((sh_m[0], cos, sa, sb))
    win_e = permute_heads(weight(0, first_mod))
    h0, u, q, kv = inproj_even(x, row(g_mix_pre, 0), sh_m[0], sc_m[0], win_e, cos, sa, sb, tm=tm, name="in_even")
    hc, kvc = pre_mm(ctx, row(g_mix_pre, 0), csh_m, csc_m, win_e, tm=C, tn=2 * LANES, w_row_off=8 * LANES, name="in_even_ctx")
    pa = [pool_fwd(u, w_pool[0], pool_scale, name="pool_fwd"), attn_fwd(q, kv, kvc, sink, name="attn_fwd")]
    wout_e = permute_heads(weight(1, pa[1]))
    y0, x1 = mm_post(pa, wout_e, x, row(g_mix_post, 0), gt_m[0], tm=tm, name="out_even")
    h1, hu0, a0, f0, x2 = ffn_fwd(0, x1)
    win_o = weight(4, x2)
    h2, z1 = pre_mm(x2, row(g_mix_pre, 1), sh_m[1], sc_m[1], win_o, tm=tm_up, tn=D, name="in_odd")
    us = sgu_fwd(z1, ln_g, ln_b, sgu_wb, bst, name="sgu_fwd")
    wout_o = weight(5, us)
    y1, x3 = mm_post([us], wout_o, x2, row(g_mix_post, 1), gt_m[1], tm=tm, name="out_odd")
    h3, hu1, a1, f1, dx4, loss_part = ffn_fwd(1, x3)

    g_srcs, g_lands, g_sems = [], [], []

    def scatter(grads, nm):
        own = place_own(grads, [g.shape[0] // N_DEV for g in grads], me, scatter=True, name=nm.replace("start", "own"))
        sems, srcs, lands, tok = exchange_start(grads, own, scatter=True, name=nm)
        g_srcs.extend(srcs)
        g_lands.extend(lands)
        g_sems.extend(sems)
        return tok[0:1, 0:1]

    def ffn_bwd(i, dxo, xin, h, hu, a, f, g_post):
        dyf, da, dg_post, dgt = post_bwd_mm(dxo, f, g_post, gt_f[i], wdn[i], tm=tm, name=f"ffn_down_bwd_{i}")
        dhg, dhu, dcwg, dcwu, dcbg, dcbu = conv_bwd(da, hu[1], hu[2], hu[0], conv_w[i], rows=conv_rows, wblk=2 * LANES,
                                                    name=f"ffn_conv_bwd_{i}")
        dxin, dg_pre, dsh, dsc = mm_pre_bwd([dhg, dhu], wup[i], xin, dxo, row(g_ffn_pre, i), sc_f[i], tm=tm, tk=half_f,
                                            name=f"ffn_up_bwd_{i}")
        g_dn = wgrad([a], dyf, tr=2 * LANES, name=f"wgrad_down_{i}")
        g_up = wgrad([dhg, dhu], h, tr=2 * LANES, name=f"wgrad_up_{i}")
        tok = scatter([g_dn, g_up], f"scatter_start_ffn_{i}")
        return dxin, tok, dict(g_ffn_post=dg_post, g_ffn_pre=dg_pre, gt_f=dgt, sh_f=dsh, sc_f=dsc,
                               ffn_conv_w=jnp.concatenate([dcwg, dcwu], axis=1), ffn_conv_b=jnp.concatenate([dcbg, dcbu], axis=1)[0])

    dx3, tok, sf1 = ffn_bwd(1, dx4, x3, h3, hu1, a1, f1, row(g_ffn_post, 1))
    dy1, dus, dg_mpost1, dgt_m1 = post_bwd_mm(dx3, y1, row(g_mix_post, 1) + tok, gt_m[1], wout_o, tm=tm, name="out_odd_bwd")
    dz1, dws, dbs, dlng, dlnb = sgu_bwd(z1, dus, ln_g, ln_b, sgu_wb, sgu_wtb, bst, name="sgu_bwd")
    dx2, dg_mpre1, dsh_m1, dsc_m1 = mm_pre_bwd([dz1], win_o, x2, dx3, row(g_mix_pre, 1), sc_m[1], tm=tm, tk=D, name="in_odd_bwd")
    tok = scatter([wgrad([us], dy1, tr=2 * LANES, name="wgrad_out_odd"), wgrad([dz1], h2, tr=2 * LANES, name="wgrad_in_odd")],
                  "scatter_start_mix_1")

    dx1, tok, sf0 = ffn_bwd(0, dx2, x1, h1, hu0, a0, f0, row(g_ffn_post, 0) + tok)
    dy0, dpa, dg_mpost0, dgt_m0 = post_bwd_mm(dx1, y0, row(g_mix_post, 0) + tok, gt_m[0], wout_e, tm=tm, name="out_even_bwd")
    tok = scatter([permute_heads(wgrad(pa, dy0, tr=2 * LANES, name="wgrad_out_even"), inverse=True)], "scatter_start_out_0")
    du, dwp, dps = pool_bwd(u, dpa, w_pool[0], pool_scale + tok, name="pool_bwd")
    dq, dkv, dkvc, dsink = attn_bwd(q, kv, kvc, sink, dpa, cos, sa, sb, name="attn_bwd")
    dz0 = jnp.concatenate([du, dq, dkv], axis=1)
    dzc = jnp.concatenate([jnp.zeros((C, 8 * LANES), BF16), dkvc], axis=1)
    tok = scatter([permute_heads(wgrad([dz0], h0, tr=2 * LANES, extra=(dzc, hc), name="wgrad_in_even"), inverse=True)],
                  "scatter_start_in_0")
    grad_x, dg_mpre0, dsh_m0, dsc_m0 = mm_pre_bwd([dz0], win_e, x, dx1, row(g_mix_pre, 0) + tok, sc_m[0], tm=tm, tk=dz0.shape[1],
                                                  name="in_even_bwd")
    _, dg_mpre0c, dcsh, dcsc = mm_pre_bwd([dkvc], win_e, ctx, None, row(g_mix_pre, 0), csc_m, tm=C, tk=2 * LANES,
                                          w_row_off=8 * LANES, name="in_even_ctx_bwd")

    out = {}

    def update(name, lands, transposed):
        w_, m_, v_ = (a.transpose(0, 2, 1) if transposed else a for a in (P[name], M[name], V[name]))
        r = w_.shape[1]
        tr = r // 4 if r % 64 == 0 and r > 256 else r
        res = adamw(w_, m_, v_, [l_.reshape(N_DEV, r, l_.shape[1]) for l_ in lands], tr=tr, name=f"adamw_{name}")
        for kind, val in zip(("grad", "delta", "new_m", "new_v"), res):
            out[(kind, name)] = val.transpose(0, 2, 1) if transposed else val

    zero = jnp.zeros((1, D), F32)
    dmod0 = jnp.concatenate([dsh_m0, dsc_m0, dgt_m0, sf0["sh_f"], sf0["sc_f"], sf0["gt_f"]], axis=1)
    dmodc = jnp.concatenate([dcsh, dcsc, zero, zero, zero, zero], axis=1)
    dmod1 = jnp.concatenate([dsh_m1, dsc_m1, dgt_m1, sf1["sh_f"], sf1["sc_f"], sf1["gt_f"]], axis=1)
    dmods = jnp.concatenate([dmod0, dmodc, dmod1], axis=0)
    dm = dmods.reshape(-1, LANES).astype(BF16)
    d_sems, d_srcs, d_lands, d_tok = exchange_start(
        [dm], place_own([dm], [dm.shape[0]], me, scatter=False, name="dmods_own"), scatter=False, name="dmods_start")
    slots = exchange_wait(g_srcs[:6], g_lands[:6], g_sems[:12], d_tok, scatter=True, name="scatter_wait_early")
    early = slots
    update("w_ffn_down", [slots[4], slots[0]], False)
    update("w_in_odd", [slots[3]], True)
    update("w_out_odd", [slots[2]], False)
    done = lax.optimization_barrier([out[("new_v", k)] for k in ("w_ffn_down", "w_in_odd", "w_out_odd")])
    dmods_all = exchange_wait(d_srcs, d_lands, d_sems, done[0], scatter=False, name="dmods_wait")[0]
    dall = lax.dynamic_index_in_dim(dmods_all.astype(F32).reshape(N_DEV, 3, N_DEV, n_ada), me, 2, False)
    g_w_ada, dcc = ada_bwd_mm(silu_c, c_ctx[None, :], dall, w_ada, name="ada_bwd")

    rep = dict(
        c_ctx=dcc[0:1],
        b_ada=jnp.concatenate([dmod0 + dmodc, dmod1]),
        g_mix_pre=jnp.concatenate([dg_mpre0 + dg_mpre0c, dg_mpre1]),
        g_mix_post=jnp.concatenate([dg_mpost0, dg_mpost1]),
        g_ffn_pre=jnp.concatenate([sf0["g_ffn_pre"], sf1["g_ffn_pre"]]),
        g_ffn_post=jnp.concatenate([sf0["g_ffn_post"], sf1["g_ffn_post"]]),
        w_pool=_nat2d(dwp), pool_scale=dps, attn_sink=dsink[:, :N_Q_HEADS],
        sgu_w=_nat2d(dws), sgu_b=dbs[:, :sgu_b.shape[1]].T,
        ffn_conv_b=jnp.stack([sf0["ffn_conv_b"], sf1["ffn_conv_b"]]),
    )
    hi = loss_part.astype(BF16).astype(F32)
    mid = (loss_part - hi).astype(BF16).astype(F32)
    loss_piece = jnp.pad(jnp.concatenate([hi, mid, loss_part - hi - mid], axis=1), ((0, 7), (0, LANES - 3)))
    conv_g = jnp.stack([sf0["ffn_conv_w"], sf1["ffn_conv_w"]]).reshape(2 * 3, N_DEV, n_cw).swapaxes(0, 1)
    shard_full = dict(sgu_ln_g=dlng.reshape(N_DEV, LANES), sgu_ln_b=dlnb.reshape(N_DEV, LANES),
                      ffn_conv_w=jnp.concatenate([_pack_rows(conv_g[d]) for d in range(N_DEV)], axis=0))
    small_names = list(rep) + list(shard_full)
    pieces = [_pack_rows(rep[k]) for k in rep] + list(shard_full.values()) + [loss_piece]
    sizes = [p.shape[0] for p in pieces]
    offs = [sum(sizes[:i]) for i in range(len(sizes))]
    pieces.append(jnp.zeros((-sum(sizes) % 16, LANES), F32))
    gpack = jnp.concatenate(pieces, axis=0).astype(BF16)
    own = place_own([gpack], [gpack.shape[0]], me, scatter=False, name="smallgrad_own")
    s_sems, s_srcs, s_lands, small_tok = exchange_start([gpack], own, scatter=False, name="smallgrad_start")

    slots = exchange_wait(g_srcs[6:], g_lands[6:], g_sems[12:], small_tok, scatter=True, name="scatter_wait_late")
    update("w_in_even", [slots[1]], True)
    update("w_out_even", [slots[0]], False)
    update("w_ffn_up", [early[5], early[1]], True)
    res = adamw(w_ada, m_w_ada, v_w_ada, [g_w_ada[l][None] for l in range(w_ada.shape[0])], tr=D // 4, name="adamw_w_ada")
    for kind, val in zip(("grad", "delta", "new_m", "new_v"), res):
        out[(kind, "w_ada")] = val

    done = lax.optimization_barrier([out[("new_v", k)] for k in ("w_in_even", "w_out_even", "w_ffn_up", "w_ada")])
    gpacks = exchange_wait(s_srcs, s_lands, s_sems, done[0], scatter=False, name="smallgrad_wait")[0]
    per_dev = {k: shard_full[k].shape[0] // N_DEV for k in shard_full}
    params = [(_nat2d(P[k]), _nat2d(M[k]), _nat2d(V[k]), offs[i], per_dev.get(k, 0)) for i, k in enumerate(small_names)]
    res = small_update(gpacks.reshape(N_DEV, -1, LANES), jnp.reshape(me, (1,)).astype(jnp.int32), params, offs[-1], name="adamw_small")
    for i, k in enumerate(small_names):
        for kind, val in zip(("grad", "delta", "new_m", "new_v"), res[4 * i:4 * i + 4]):
            out[(kind, k)] = val.reshape(P[k].shape)
    loss = res[-1][0, 0]

    names = list(P)
    final = [loss, grad_x[None]]
    for kind in ("grad", "delta", "new_m", "new_v"):
        for k in names:
            val = out[(kind, k)]
            final.append(val)
    return tuple(final)
```

```python
import functools
import math

import jax
import jax.numpy as jnp
from jax import lax
from jax.experimental import pallas as pl
from jax.experimental.pallas import tpu as pltpu

F32 = jnp.float32
BF16 = jnp.bfloat16
MESH = pl.DeviceIdType.MESH
N_DEV = 8
LANES = 128
VMEM_LIMIT = 48 * 1024 * 1024
EPS = 1e-6
NEG_INF = -1e30
GRID_W = 64
WINDOW = 128
BLK = 128
HEAD_DIM = 64
N_Q_HEADS = 8
N_KV_HEADS = 2
GQA = N_Q_HEADS // N_KV_HEADS
POOL_WINDOWS = (2, 4, 8, 16)
ROPE_BASE = 10000.0
ROPE_FREQS = HEAD_DIM // 4
PAD = 16
ADAM_LR, ADAM_B1, ADAM_B2, ADAM_EPS, ADAM_WD, ADAM_STEP = 0.001, 0.9, 0.999, 1e-08, 0.01, 10
BC1 = 1.0 - ADAM_B1 ** ADAM_STEP
BC2 = 1.0 - ADAM_B2 ** ADAM_STEP
SQRT_2_OVER_PI = math.sqrt(2.0 / math.pi)
GELU_C = 0.044715


def _cp(sem=None):
    return pltpu.CompilerParams(dimension_semantics=sem, vmem_limit_bytes=VMEM_LIMIT)


def _dot(a, b):
    return jnp.dot(a, b, preferred_element_type=F32)


def _dot_nt(a, b):
    return lax.dot_general(a, b, (((1,), (1,)), ((), ())), preferred_element_type=F32)


def _dot_tn(a, b):
    return lax.dot_general(a, b, (((0,), (0,)), ((), ())), preferred_element_type=F32)


def _rms(x):
    r = lax.rsqrt(jnp.mean(x * x, axis=-1, keepdims=True) + EPS)
    return x * r, r


def _rms_bwd(dn, n, r):
    return r * (dn - n * jnp.mean(dn * n, axis=-1, keepdims=True))


def _colsum(a):
    return jnp.sum(a, axis=0, keepdims=True)


def _rope(x, c, sa, sb):
    return x * c + pltpu.roll(x, LANES - ROPE_FREQS, 1) * sa + pltpu.roll(x, ROPE_FREQS, 1) * sb


def _full(shape):
    return pl.BlockSpec(shape, lambda *_: (0,) * len(shape))


def pre_mm(x, g, sh, sc, wt, *, tm, tn, w_row_off=0, name):
    T, D = x.shape
    n_rows = wt.shape[0] - w_row_off
    off = w_row_off // tn

    def body(x_ref, g_ref, sh_ref, sc_ref, w_ref, h_ref, z_ref):
        @pl.when(pl.program_id(1) == 0)
        def _():
            n, _ = _rms(x_ref[...])
            h_ref[...] = (n * g_ref[...] * (1.0 + sc_ref[...]) + sh_ref[...]).astype(BF16)

        z_ref[...] = _dot_nt(h_ref[...], w_ref[...]).astype(BF16)

    vec = pl.BlockSpec((1, D), lambda i, j: (0, 0))
    return pl.pallas_call(
        body, name=name, grid=(T // tm, n_rows // tn),
        in_specs=[pl.BlockSpec((tm, D), lambda i, j: (i, 0)), vec, vec, vec, pl.BlockSpec((tn, D), lambda i, j: (j + off, 0))],
        out_specs=[pl.BlockSpec((tm, D), lambda i, j: (i, 0)), pl.BlockSpec((tm, tn), lambda i, j: (i, j))],
        out_shape=[jax.ShapeDtypeStruct((T, D), BF16), jax.ShapeDtypeStruct((T, n_rows), BF16)],
        compiler_params=_cp(("parallel", "arbitrary")),
    )(x, g, sh, sc, wt)


def inproj_even(x, g, sh, sc, wt, cos, sa, sb, *, tm, name):
    T, D = x.shape
    N = wt.shape[0]

    def body(x_ref, g_ref, sh_ref, sc_ref, w_ref, c_ref, sa_ref, sb_ref, h_ref, u_ref, q_ref, kv_ref):
        n, _ = _rms(x_ref[...])
        h = (n * g_ref[...] * (1.0 + sc_ref[...]) + sh_ref[...]).astype(BF16)
        h_ref[...] = h
        z = _dot_nt(h, w_ref[...])
        u_ref[...] = z[:, :4 * LANES]
        c, a, b = c_ref[...], sa_ref[...], sb_ref[...]
        for s in range(4):
            q_ref[:, s * LANES:(s + 1) * LANES] = _rope(z[:, (4 + s) * LANES:(5 + s) * LANES], c, a, b).astype(BF16)
        kv_ref[:, :LANES] = _rope(z[:, 8 * LANES:9 * LANES], c, a, b).astype(BF16)
        kv_ref[:, LANES:] = z[:, 9 * LANES:].astype(BF16)

    vec = pl.BlockSpec((1, D), lambda i: (0, 0))
    row = lambda w: pl.BlockSpec((tm, w), lambda i: (i, 0))
    return pl.pallas_call(
        body, name=name, grid=(T // tm,),
        in_specs=[row(D), vec, vec, vec, _full((N, D)), row(LANES), row(LANES), row(LANES)],
        out_specs=[row(D), row(4 * LANES), row(4 * LANES), row(2 * LANES)],
        out_shape=[jax.ShapeDtypeStruct((T, D), BF16), jax.ShapeDtypeStruct((T, 4 * LANES), F32),
                   jax.ShapeDtypeStruct((T, 4 * LANES), BF16), jax.ShapeDtypeStruct((T, 2 * LANES), BF16)],
        compiler_params=_cp(("parallel",)),
    )(x, g, sh, sc, wt, cos, sa, sb)


def mm_post(a_parts, w, x, g, gt, *, tm, target=None, name):
    T = a_parts[0].shape[0]
    D = w.shape[1]
    npart = len(a_parts)
    offs = [sum(a_.shape[1] for a_ in a_parts[:p]) for p in range(npart + 1)]
    with_loss = target is not None

    def body(*refs):
        a_refs, (w_ref, x_ref, g_ref, gt_ref) = refs[:npart], refs[npart:npart + 4]
        y = _dot(a_refs[0][...], w_ref[offs[0]:offs[1], :])
        for p in range(1, npart):
            y = y + _dot(a_refs[p][...], w_ref[offs[p]:offs[p + 1], :])
        n, _ = _rms(y)
        xn = x_ref[...] + gt_ref[...] * (n * g_ref[...])
        if not with_loss:
            y_ref, xn_ref = refs[npart + 4:]
            y_ref[...] = y.astype(BF16)
            xn_ref[...] = xn
            return
        t_ref, y_ref, d_ref, l_ref = refs[npart + 4:]
        y_ref[...] = y.astype(BF16)

        @pl.when(pl.program_id(0) == 0)
        def _():
            l_ref[...] = jnp.zeros_like(l_ref)

        e = xn - t_ref[...]
        l_ref[...] += 0.5 * jnp.sum(jnp.mean(e * e, axis=-1, keepdims=True), axis=0, keepdims=True)
        d_ref[...] = e * (1.0 / D)

    vec = pl.BlockSpec((1, D), lambda i: (0, 0))
    row = lambda w_: pl.BlockSpec((tm, w_), lambda i: (i, 0))
    in_specs = [row(a_.shape[1]) for a_ in a_parts] + [_full(w.shape), row(D), vec, vec]
    out_specs = [row(D), row(D)]
    out_shape = [jax.ShapeDtypeStruct((T, D), BF16), jax.ShapeDtypeStruct((T, D), F32)]
    if with_loss:
        in_specs.append(row(D))
        out_specs.append(_full((1, 1)))
        out_shape.append(jax.ShapeDtypeStruct((1, 1), F32))
    return pl.pallas_call(
        body, name=name, grid=(T // tm,), in_specs=in_specs, out_specs=out_specs, out_shape=out_shape,
        compiler_params=_cp(("arbitrary",) if with_loss else ("parallel",)),
    )(*a_parts, w, x, g, gt, *((target,) if with_loss else ()))


def post_bwd_mm(dxn, y, g, gt, w, *, tm, name):
    T, D = y.shape
    K = w.shape[0]

    def body(dxn_ref, y_ref, g_ref, gt_ref, w_ref, dy_ref, da_ref, dg_ref, dgt_ref):
        @pl.when(pl.program_id(0) == 0)
        def _():
            dg_ref[...] = jnp.zeros_like(dg_ref)
            dgt_ref[...] = jnp.zeros_like(dgt_ref)

        d = dxn_ref[...]
        n, r = _rms(y_ref[...].astype(F32))
        g_, gt_ = g_ref[...], gt_ref[...]
        dg_ref[...] += _colsum(d * gt_ * n)
        dgt_ref[...] += _colsum(d * g_ * n)
        dy = _rms_bwd(d * (gt_ * g_), n, r).astype(BF16)
        dy_ref[...] = dy
        da_ref[...] = _dot_nt(dy, w_ref[...]).astype(BF16)

    vec = pl.BlockSpec((1, D), lambda i: (0, 0))
    row = lambda w_: pl.BlockSpec((tm, w_), lambda i: (i, 0))
    return pl.pallas_call(
        body, name=name, grid=(T // tm,),
        in_specs=[row(D), row(D), vec, vec, _full((K, D))],
        out_specs=[row(D), row(K), vec, vec],
        out_shape=[jax.ShapeDtypeStruct((T, D), BF16), jax.ShapeDtypeStruct((T, K), BF16),
                   jax.ShapeDtypeStruct((1, D), F32), jax.ShapeDtypeStruct((1, D), F32)],
        compiler_params=_cp(("arbitrary",)),
    )(dxn, y, g, gt, w)


def mm_pre_bwd(dzs, wt, x, dres, g, sc, *, tm, tk, w_row_off=0, name):
    T, N = dzs[0].shape
    D = x.shape[1]
    nk = N // tk
    npart = len(dzs)
    off = w_row_off // tk
    has_res = dres is not None

    def body(*refs):
        dz_refs = refs[:npart]
        w_refs = refs[npart:2 * npart]
        rest = refs[2 * npart:]
        x_ref = rest[0]
        dres_ref = rest[1] if has_res else None
        g_ref, sc_ref, dx_ref, dg_ref, dsh_ref, dsc_ref, acc = rest[1 + has_res:]
        i, k = pl.program_id(0), pl.program_id(1)

        @pl.when(jnp.logical_and(i == 0, k == 0))
        def _():
            dg_ref[...] = jnp.zeros_like(dg_ref)
            dsh_ref[...] = jnp.zeros_like(dsh_ref)
            dsc_ref[...] = jnp.zeros_like(dsc_ref)

        part = _dot(dz_refs[0][...], w_refs[0][...])
        for p in range(1, npart):
            part = part + _dot(dz_refs[p][...], w_refs[p][...])

        @pl.when(k == 0)
        def _():
            acc[...] = part

        @pl.when(k > 0)
        def _():
            acc[...] += part

        @pl.when(k == nk - 1)
        def _():
            dh = acc[...]
            n, r = _rms(x_ref[...])
            g_, s1 = g_ref[...], 1.0 + sc_ref[...]
            dsh_ref[...] += _colsum(dh)
            dsc_ref[...] += _colsum(dh * n * g_)
            dg_ref[...] += _colsum(dh * s1 * n)
            dxp = _rms_bwd(dh * (g_ * s1), n, r)
            dx_ref[...] = dxp + dres_ref[...] if has_res else dxp

    vec = pl.BlockSpec((1, D), lambda i, k: (0, 0))
    row = pl.BlockSpec((tm, D), lambda i, k: (i, 0))
    w_specs = [pl.BlockSpec((tk, D), (lambda i, k, p=p: (k + off + p * nk, 0))) for p in range(npart)]
    res_specs, res_args = ([row], (dres,)) if has_res else ([], ())
    return pl.pallas_call(
        body, name=name, grid=(T // tm, nk),
        in_specs=[pl.BlockSpec((tm, tk), lambda i, k: (i, k))] * npart + w_specs + [row] + res_specs + [vec, vec],
        out_specs=[row, vec, vec, vec],
        out_shape=[jax.ShapeDtypeStruct((T, D), F32)] + [jax.ShapeDtypeStruct((1, D), F32)] * 3,
        scratch_shapes=[pltpu.VMEM((tm, D), F32)],
        compiler_params=_cp(("arbitrary", "arbitrary")),
    )(*dzs, *([wt] * npart), x, *res_args, g, sc)


def wgrad(a_parts, b, *, tr, extra=None, name):
    T, R = a_parts[0].shape
    D = b.shape[1]
    npart = len(a_parts)
    nr = R // tr

    def body(*refs):
        a_refs, b_ref = refs[:npart], refs[npart]
        g_ref = refs[-1]
        for p in range(npart):
            @pl.when(pl.program_id(0) // nr == p)
            def _():
                acc = _dot_tn(a_refs[p][...], b_ref[...])
                if extra is not None:
                    acc += _dot_tn(refs[npart + 1][...], refs[npart + 2][...])
                g_ref[...] = acc.astype(BF16)

    in_specs = [pl.BlockSpec((T, tr), (lambda r, p=p: (0, jnp.clip(r - p * nr, 0, nr - 1)))) for p in range(npart)]
    in_specs.append(_full((T, D)))
    args = [*a_parts, b]
    if extra is not None:
        a2, b2 = extra
        in_specs += [pl.BlockSpec((a2.shape[0], tr), lambda r: (0, r)), _full(b2.shape)]
        args += [a2, b2]
    return pl.pallas_call(
        body, name=name, grid=(npart * nr,),
        in_specs=in_specs, out_specs=pl.BlockSpec((tr, D), lambda r: (r, 0)),
        out_shape=jax.ShapeDtypeStruct((npart * R, D), BF16),
        compiler_params=_cp(("parallel",)),
    )(*args)


def _conv_ext(ref, r0, rows, total):
    top = ref[pl.ds(pl.multiple_of(jnp.maximum(r0 - PAD, 0), PAD), PAD), :]
    mid = ref[pl.ds(r0, rows), :]
    bot = ref[pl.ds(pl.multiple_of(jnp.minimum(r0 + rows, total - PAD), PAD), PAD), :]
    top = jnp.where(r0 > 0, top, jnp.zeros_like(top))
    bot = jnp.where(r0 + rows < total, bot, jnp.zeros_like(bot))
    return jnp.concatenate([top, mid, bot], axis=0).astype(F32)


def _shift_rows(a, k):
    return pltpu.roll(a, k % a.shape[0], 0)


def _conv3(x, w, b):
    return w[0:1] * _shift_rows(x, 1) + w[1:2] * x + w[2:3] * _shift_rows(x, -1) + b


def _gate_up_specs(rows_, wblk, nb):
    return [pl.BlockSpec((rows_, wblk), lambda j: (0, j)), pl.BlockSpec((rows_, wblk), lambda j: (0, j + nb))]


def conv_fwd(hu, cw, cb, *, rows, wblk, name):
    L, N2 = hu.shape
    nb = N2 // 2 // wblk
    nchunk = L // rows

    def body(hg_ref, hu_ref, wg_ref, wu_ref, bg_ref, bu_ref, a_ref, s1_ref, s2_ref):
        def chunk(ci, carry):
            r0 = pl.multiple_of(ci * rows, rows)
            gate = _conv3(_conv_ext(hg_ref, r0, rows, L), wg_ref[...], bg_ref[...])[PAD:PAD + rows]
            up = _conv3(_conv_ext(hu_ref, r0, rows, L), wu_ref[...], bu_ref[...])[PAD:PAD + rows]
            sg = jax.nn.sigmoid(gate)
            silu = gate * sg
            at = pl.ds(r0, rows)
            a_ref[at, :] = (silu * up).astype(BF16)
            s1_ref[at, :] = silu.astype(BF16)
            s2_ref[at, :] = (up * (sg + silu * (1.0 - sg))).astype(BF16)
            return carry

        lax.fori_loop(0, nchunk, chunk, 0)

    out = pl.BlockSpec((L, wblk), lambda j: (0, j))
    return pl.pallas_call(
        body, name=name, grid=(nb,),
        in_specs=_gate_up_specs(L, wblk, nb) + _gate_up_specs(3, wblk, nb) + _gate_up_specs(1, wblk, nb),
        out_specs=[out] * 3, out_shape=[jax.ShapeDtypeStruct((L, N2 // 2), BF16)] * 3,
        compiler_params=_cp(("parallel",)),
    )(hu, hu, cw, cw, cb, cb)


def conv_bwd(da, s1, s2, hu, cw, *, rows, wblk, name):
    L, N2 = hu.shape
    F = N2 // 2
    nb = F // wblk
    nchunk = L // rows
    mid = slice(PAD, PAD + rows)

    def body(da_ref, s1_ref, s2_ref, hg_ref, hu_ref, wg_ref, wu_ref, dg_ref, du_ref, dwg_ref, dwu_ref, dbg_ref, dbu_ref):
        for ref in (dwg_ref, dwu_ref, dbg_ref, dbu_ref):
            ref[...] = jnp.zeros_like(ref)

        def half_bwd(x_ref, dh, w_ref, dx_ref, dw_ref, db_ref, r0):
            w = w_ref[...]
            nxt, prv = _shift_rows(dh, -1)[mid], _shift_rows(dh, 1)[mid]
            dhm, xm = dh[mid], x_ref[pl.ds(r0, rows), :].astype(F32)
            dx_ref[pl.ds(r0, rows), :] = (w[0:1] * nxt + w[1:2] * dhm + w[2:3] * prv).astype(BF16)
            db_ref[...] += _colsum(dhm)
            dw_ref[0:1, :] += _colsum(nxt * xm)
            dw_ref[1:2, :] += _colsum(dhm * xm)
            dw_ref[2:3, :] += _colsum(prv * xm)

        def chunk(ci, carry):
            r0 = pl.multiple_of(ci * rows, rows)
            d = _conv_ext(da_ref, r0, rows, L)
            half_bwd(hu_ref, d * _conv_ext(s1_ref, r0, rows, L), wu_ref, du_ref, dwu_ref, dbu_ref, r0)
            half_bwd(hg_ref, d * _conv_ext(s2_ref, r0, rows, L), wg_ref, dg_ref, dwg_ref, dbg_ref, r0)
            return carry

        lax.fori_loop(0, nchunk, chunk, 0)

    blk = lambda r: pl.BlockSpec((r, wblk), lambda j: (0, j))
    return pl.pallas_call(
        body, name=name, grid=(nb,),
        in_specs=[blk(L)] * 3 + _gate_up_specs(L, wblk, nb) + _gate_up_specs(3, wblk, nb),
        out_specs=[blk(L), blk(L), blk(3), blk(3), blk(1), blk(1)],
        out_shape=[jax.ShapeDtypeStruct((L, F), BF16)] * 2 + [jax.ShapeDtypeStruct((3, F), F32)] * 2
        + [jax.ShapeDtypeStruct((1, F), F32)] * 2,
        compiler_params=_cp(("parallel",)),
    )(da, s1, s2, hu, hu, cw, cw)


def _window_sums(pad_ref, w, lead):
    a = pad_ref[...]
    k = 1
    while k < w:
        a = a + _shift_rows(a, -k)
        k *= 2
    return _shift_rows(a, lead) if lead else a


def _pool_counts(L, h):
    t = lax.broadcasted_iota(jnp.int32, (L, 1), 0)
    return (jnp.minimum(t + h, L) - jnp.maximum(t - h, 0)).astype(F32)


def _pooled(u_ref, pad_ref, L, w):
    h = w // 2
    pad_ref[pl.ds(PAD, L), :] = u_ref[...]
    win = _window_sums(pad_ref, w, h)[PAD:PAD + L]
    return win / _pool_counts(L, h) - u_ref[...]


def _zero_pad_edges(pad_ref, L):
    z = jnp.zeros((PAD, LANES), F32)
    pad_ref[pl.ds(0, PAD), :] = z
    pad_ref[pl.ds(PAD + L, PAD), :] = z


def pool_fwd(u, w_pool, pool_scale, *, name):
    L = u.shape[0]

    def body(u_ref, w_ref, ps_ref, p_ref, pad_ref):
        _zero_pad_edges(pad_ref, L)
        for gi, win in enumerate(POOL_WINDOWS):
            @pl.when(pl.program_id(0) == gi)
            def _():
                pooled = _pooled(u_ref, pad_ref, L, win)
                p_ref[...] = (_dot(pooled.astype(BF16), w_ref[...].astype(BF16)) * ps_ref[...]).astype(BF16)

    return pl.pallas_call(
        body, name=name, grid=(len(POOL_WINDOWS),),
        in_specs=[pl.BlockSpec((L, LANES), lambda gi: (0, gi)), pl.BlockSpec((None, LANES, LANES), lambda gi: (gi, 0, 0)),
                  pl.BlockSpec((1, LANES), lambda gi: (0, gi))],
        out_specs=pl.BlockSpec((L, LANES), lambda gi: (0, gi)),
        out_shape=jax.ShapeDtypeStruct((L, 4 * LANES), BF16),
        scratch_shapes=[pltpu.VMEM((L + 2 * PAD, LANES), F32)],
        compiler_params=_cp(("parallel",)),
    )(u, w_pool, pool_scale)


def pool_bwd(u, dpa, w_pool, pool_scale, *, name):
    L = u.shape[0]

    def body(u_ref, dp_ref, w_ref, ps_ref, du_ref, dw_ref, dps_ref, pad_ref):
        _zero_pad_edges(pad_ref, L)
        for gi, win in enumerate(POOL_WINDOWS):
            @pl.when(pl.program_id(0) == gi)
            def _():
                h = win // 2
                wb = w_ref[...].astype(BF16)
                pooled = _pooled(u_ref, pad_ref, L, win).astype(BF16)
                dp = dp_ref[...].astype(F32)
                dps_ref[...] = _colsum(dp * _dot(pooled, wb))
                dy = (dp * ps_ref[...]).astype(BF16)
                dw_ref[...] = _dot_tn(pooled, dy)
                dpooled = _dot_nt(dy, wb)
                pad_ref[pl.ds(PAD, L), :] = dpooled / _pool_counts(L, h)
                du_ref[...] = (_window_sums(pad_ref, win, h - 1)[PAD:PAD + L] - dpooled).astype(BF16)

    return pl.pallas_call(
        body, name=name, grid=(len(POOL_WINDOWS),),
        in_specs=[pl.BlockSpec((L, LANES), lambda gi: (0, gi)), pl.BlockSpec((L, LANES), lambda gi: (0, gi)),
                  pl.BlockSpec((None, LANES, LANES), lambda gi: (gi, 0, 0)), pl.BlockSpec((1, LANES), lambda gi: (0, gi))],
        out_specs=[pl.BlockSpec((L, LANES), lambda gi: (0, gi)), pl.BlockSpec((None, LANES, LANES), lambda gi: (gi, 0, 0)),
                   pl.BlockSpec((1, LANES), lambda gi: (0, gi))],
        out_shape=[jax.ShapeDtypeStruct((L, 4 * LANES), BF16), jax.ShapeDtypeStruct((4, LANES, LANES), F32),
                   jax.ShapeDtypeStruct((1, 4 * LANES), F32)],
        scratch_shapes=[pltpu.VMEM((L + 2 * PAD, LANES), F32)],
        compiler_params=_cp(("parallel",)),
    )(u, dpa, w_pool, pool_scale)


def _attn_probs(qk, band_k, ctx_k, sink_ref, kh, mask4):
    s_loc = jnp.where(mask4, _dot_nt(qk, band_k), NEG_INF)
    s_ctx = _dot_nt(qk, ctx_k)
    sk = jnp.concatenate([jnp.full((BLK, 1), sink_ref[kh * GQA + hh], F32) for hh in range(GQA)], axis=0)
    m = jnp.maximum(jnp.maximum(jnp.max(s_loc, axis=-1, keepdims=True), jnp.max(s_ctx, axis=-1, keepdims=True)), sk)
    e_loc, e_ctx, e_s = jnp.exp(s_loc - m), jnp.exp(s_ctx - m), jnp.exp(sk - m)
    inv = 1.0 / (jnp.sum(e_loc, axis=-1, keepdims=True) + jnp.sum(e_ctx, axis=-1, keepdims=True) + e_s)
    return e_loc * inv, e_ctx * inv, e_s * inv


def _attn_block(n, L):
    start = pl.multiple_of(jnp.clip((n - 1) * BLK, 0, L - 3 * BLK), BLK)
    qpos = n * BLK + lax.broadcasted_iota(jnp.int32, (BLK, 3 * BLK), 0)
    kpos = start + lax.broadcasted_iota(jnp.int32, (BLK, 3 * BLK), 1)
    mask = jnp.abs(kpos - qpos) <= WINDOW
    return start, jnp.concatenate([mask] * GQA, axis=0)


def _stack_slabs(ref):
    return jnp.concatenate([ref[:, s * LANES:(s + 1) * LANES] for s in range(GQA)], axis=0)


def _kv_head_lanes(kh):
    return (lax.broadcasted_iota(jnp.int32, (1, LANES), 1) // HEAD_DIM) == kh


def permute_heads(w, inverse=False):
    lo, hi = 4 * LANES, 8 * LANES
    mid = w[lo:hi].reshape(*((GQA, N_KV_HEADS) if inverse else (N_KV_HEADS, GQA)), HEAD_DIM, w.shape[1])
    return jnp.concatenate([w[:lo], mid.swapaxes(0, 1).reshape(hi - lo, w.shape[1]), w[hi:]], axis=0)


def attn_fwd(q, kv, kvc, sink, *, name):
    L = q.shape[0]
    C = kvc.shape[0]
    scale = HEAD_DIM ** -0.5

    def body(q_ref, kv_ref, kvc_ref, sink_ref, o_ref):
        start, mask4 = _attn_block(pl.program_id(0), L)
        band = kv_ref[pl.ds(start, 3 * BLK), :]
        kvc_ = kvc_ref[...]
        qs = _stack_slabs(q_ref) * scale
        o = jnp.zeros((GQA * BLK, LANES), F32)
        for kh in range(N_KV_HEADS):
            grp = _kv_head_lanes(kh)
            qk = jnp.where(grp, qs, jnp.zeros_like(qs))
            p_loc, p_ctx, _ = _attn_probs(qk, band[:, :LANES], kvc_[:, :LANES], sink_ref, kh, mask4)
            o = o + jnp.where(grp, _dot(p_loc.astype(BF16), band[:, LANES:]) + _dot(p_ctx.astype(BF16), kvc_[:, LANES:]), 0.0)
        for s in range(GQA):
            o_ref[:, s * LANES:(s + 1) * LANES] = o[s * BLK:(s + 1) * BLK].astype(BF16)

    return pl.pallas_call(
        body, name=name, grid=(L // BLK,),
        in_specs=[pl.BlockSpec((BLK, 4 * LANES), lambda n: (n, 0)), _full((L, 2 * LANES)), _full((C, 2 * LANES)),
                  pl.BlockSpec(memory_space=pltpu.SMEM)],
        out_specs=pl.BlockSpec((BLK, 4 * LANES), lambda n: (n, 0)),
        out_shape=jax.ShapeDtypeStruct((L, 4 * LANES), BF16),
        compiler_params=_cp(("parallel",)),
    )(q, kv, kvc, sink)


def attn_bwd(q, kv, kvc, sink, dpa, cos, sa, sb, *, name):
    L = q.shape[0]
    C = kvc.shape[0]
    nb = L // BLK
    scale = HEAD_DIM ** -0.5

    def body(q_ref, kv_ref, kvc_ref, sink_ref, do_ref, c_ref, sa_ref, sb_ref, cq_ref, saq_ref, sbq_ref,
             dq_ref, dkv_ref, dkvc_ref, dsink_ref, dkv_acc, dkvc_acc):
        n = pl.program_id(0)

        @pl.when(n == 0)
        def _():
            dkv_acc[...] = jnp.zeros_like(dkv_acc)
            dkvc_acc[...] = jnp.zeros_like(dkvc_acc)
            dsink_ref[...] = jnp.zeros_like(dsink_ref)

        start, mask4 = _attn_block(n, L)
        band = kv_ref[pl.ds(start, 3 * BLK), :]
        kvc_ = kvc_ref[...]
        band_k, band_v, ctx_k, ctx_v = band[:, :LANES], band[:, LANES:], kvc_[:, :LANES], kvc_[:, LANES:]
        qs = _stack_slabs(q_ref) * scale
        dos = _stack_slabs(do_ref)
        lane = lax.broadcasted_iota(jnp.int32, (1, LANES), 1)
        dsink = jnp.zeros((1, LANES), F32)
        dq = jnp.zeros((GQA * BLK, LANES), F32)
        dk = jnp.zeros((LANES, 3 * BLK), F32)
        dv = jnp.zeros((LANES, 3 * BLK), F32)
        dkc = jnp.zeros((LANES, C), F32)
        dvc = jnp.zeros((LANES, C), F32)
        for kh in range(N_KV_HEADS):
            grp = _kv_head_lanes(kh)
            qk = jnp.where(grp, qs, jnp.zeros_like(qs))
            dok = jnp.where(grp, dos, jnp.zeros_like(dos))
            p_loc, p_ctx, p_s = _attn_probs(qk, band_k, ctx_k, sink_ref, kh, mask4)
            dp_loc = _dot_nt(dok, band_v)
            dp_ctx = _dot_nt(dok, ctx_v)
            delta = jnp.sum(p_loc * dp_loc, axis=-1, keepdims=True) + jnp.sum(p_ctx * dp_ctx, axis=-1, keepdims=True)
            ds_loc = (p_loc * (dp_loc - delta)).astype(BF16)
            ds_ctx = (p_ctx * (dp_ctx - delta)).astype(BF16)
            dsk = p_s * delta
            for hh in range(GQA):
                dsink = dsink - jnp.where(lane == kh * GQA + hh, jnp.sum(dsk[hh * BLK:(hh + 1) * BLK], axis=0, keepdims=True), 0.0)
            dq = dq + jnp.where(grp, _dot(ds_loc, band_k) + _dot(ds_ctx, ctx_k), 0.0)
            dk = dk + _dot_tn(qk, ds_loc)
            dv = dv + _dot_tn(dok, p_loc.astype(BF16))
            dkc = dkc + _dot_tn(qk, ds_ctx)
            dvc = dvc + _dot_tn(dok, p_ctx.astype(BF16))
        dsink_ref[...] += dsink
        dkv_acc[:LANES, pl.ds(start, 3 * BLK)] += dk
        dkv_acc[LANES:, pl.ds(start, 3 * BLK)] += dv
        dkvc_acc[:LANES, :] += dkc
        dkvc_acc[LANES:, :] += dvc
        c, a, b = cq_ref[...], -saq_ref[...], -sbq_ref[...]
        for s in range(GQA):
            dq_ref[:, s * LANES:(s + 1) * LANES] = _rope(dq[s * BLK:(s + 1) * BLK] * scale, c, a, b).astype(BF16)

        @pl.when(n == nb - 1)
        def _():
            dkv_ref[:, :LANES] = _rope(dkv_acc[:LANES, :].T, c_ref[...], -sa_ref[...], -sb_ref[...]).astype(BF16)
            dkv_ref[:, LANES:] = dkv_acc[LANES:, :].T.astype(BF16)
            dkvc_ref[...] = dkvc_acc[...].T.astype(BF16)

    blk = lambda w: pl.BlockSpec((BLK, w), lambda n: (n, 0))
    return pl.pallas_call(
        body, name=name, grid=(nb,),
        in_specs=[blk(4 * LANES), _full((L, 2 * LANES)), _full((C, 2 * LANES)), pl.BlockSpec(memory_space=pltpu.SMEM),
                  pl.BlockSpec((BLK, 4 * LANES), lambda n: (n, 1)),
                  _full((L, LANES)), _full((L, LANES)), _full((L, LANES)), blk(LANES), blk(LANES), blk(LANES)],
        out_specs=[blk(4 * LANES), _full((L, 2 * LANES)), _full((C, 2 * LANES)), _full((1, LANES))],
        out_shape=[jax.ShapeDtypeStruct((L, 4 * LANES), BF16), jax.ShapeDtypeStruct((L, 2 * LANES), BF16),
                   jax.ShapeDtypeStruct((C, 2 * LANES), BF16), jax.ShapeDtypeStruct((1, LANES), F32)],
        scratch_shapes=[pltpu.VMEM((2 * LANES, L), F32), pltpu.VMEM((2 * LANES, C), F32)],
        compiler_params=_cp(("arbitrary",)),
    )(q, kv, kvc, sink, dpa, cos, sa, sb, cos, sa, sb)


def _gelu_parts(x):
    th = jnp.tanh(SQRT_2_OVER_PI * (x + GELU_C * x * x * x))
    return 0.5 * x * (1.0 + th), th


def _gelu_grad(x, th):
    return 0.5 * (1.0 + th) + 0.5 * x * (1.0 - th * th) * SQRT_2_OVER_PI * (1.0 + 3.0 * GELU_C * x * x)


def _layernorm(v):
    mu = jnp.mean(v, axis=-1, keepdims=True)
    vc = v - mu
    rstd = lax.rsqrt(jnp.mean(vc * vc, axis=-1, keepdims=True) + EPS)
    return vc * rstd, rstd


def sgu_fwd(z1, ln_g, ln_b, ws, bst, *, name):
    L, W2 = z1.shape
    W = W2 // 2
    ng = W // LANES

    def body(z_ref, g_ref, b_ref, ws_ref, bs_ref, o_ref):
        z, _ = _gelu_parts(z_ref[...].astype(F32))
        xhat, _ = _layernorm(z[:, W:])
        vln = (xhat * g_ref[...] + b_ref[...]).astype(BF16)
        for gi in range(ng):
            cs = slice(gi * LANES, (gi + 1) * LANES)
            s = _dot(ws_ref[gi], vln[:, cs]) + bs_ref[:, gi:gi + 1]
            o_ref[:, cs] = (z[:, cs] * s).astype(BF16)

    vec = _full((1, W))
    return pl.pallas_call(
        body, name=name, grid=(L // BLK,),
        in_specs=[pl.BlockSpec((BLK, W2), lambda n: (n, 0)), vec, vec, _full((ng, LANES, LANES)), _full((BLK, ng))],
        out_specs=pl.BlockSpec((BLK, W), lambda n: (n, 0)),
        out_shape=jax.ShapeDtypeStruct((L, W), BF16),
        compiler_params=_cp(("parallel",)),
    )(z1, ln_g, ln_b, ws, bst)


def sgu_bwd(z1, dus, ln_g, ln_b, ws, wst, bst, *, name):
    L, W2 = z1.shape
    W = W2 // 2
    ng = W // LANES

    def body(z_ref, d_ref, g_ref, b_ref, ws_ref, wst_ref, bs_ref, dz_ref, dws_ref, dbs_ref, dg_ref, db_ref, dv_scr):
        @pl.when(pl.program_id(0) == 0)
        def _():
            dws_ref[...] = jnp.zeros_like(dws_ref)
            dbs_ref[...] = jnp.zeros_like(dbs_ref)
            dg_ref[...] = jnp.zeros_like(dg_ref)
            db_ref[...] = jnp.zeros_like(db_ref)

        zp = z_ref[...].astype(F32)
        z, th = _gelu_parts(zp)
        xhat, rstd = _layernorm(z[:, W:])
        vln = (xhat * g_ref[...] + b_ref[...]).astype(BF16)
        d = d_ref[...].astype(F32)
        lane = lax.broadcasted_iota(jnp.int32, (1, LANES), 1)
        dbs = jnp.zeros((BLK, LANES), F32)
        dgel = _gelu_grad(zp, th)
        for gi in range(ng):
            cs = slice(gi * LANES, (gi + 1) * LANES)
            s = _dot(ws_ref[gi], vln[:, cs]) + bs_ref[:, gi:gi + 1]
            dz_ref[:, cs] = (d[:, cs] * s * dgel[:, cs]).astype(BF16)
            ds = d[:, cs] * z[:, cs]
            dbs = dbs + jnp.where(lane == gi, jnp.sum(ds, axis=-1, keepdims=True), 0.0)
            dsb = ds.astype(BF16)
            dws_ref[gi] += _dot_nt(dsb, vln[:, cs])
            dv_scr[:, cs] = _dot(wst_ref[gi], dsb)
        dbs_ref[...] += dbs
        dvln = dv_scr[...]
        dg_ref[...] += _colsum(dvln * xhat)
        db_ref[...] += _colsum(dvln)
        dxh = dvln * g_ref[...]
        dv = rstd * (dxh - jnp.mean(dxh, axis=-1, keepdims=True) - xhat * jnp.mean(dxh * xhat, axis=-1, keepdims=True))
        dz_ref[:, W:] = (dv * dgel[:, W:]).astype(BF16)

    vec = _full((1, W))
    return pl.pallas_call(
        body, name=name, grid=(L // BLK,),
        in_specs=[pl.BlockSpec((BLK, W2), lambda n: (n, 0)), pl.BlockSpec((BLK, W), lambda n: (n, 0)), vec, vec,
                  _full((ng, LANES, LANES)), _full((ng, LANES, LANES)), _full((BLK, ng))],
        out_specs=[pl.BlockSpec((BLK, W2), lambda n: (n, 0)), _full((ng, LANES, LANES)), _full((BLK, LANES)), vec, vec],
        out_shape=[jax.ShapeDtypeStruct((L, W2), BF16), jax.ShapeDtypeStruct((ng, LANES, LANES), F32),
                   jax.ShapeDtypeStruct((BLK, LANES), F32), jax.ShapeDtypeStruct((1, W), F32), jax.ShapeDtypeStruct((1, W), F32)],
        scratch_shapes=[pltpu.VMEM((BLK, W), F32)],
        compiler_params=_cp(("arbitrary",)),
    )(z1, dus, ln_g, ln_b, ws, wst, bst)


def _adamw_math(w, m, v, g):
    m_ = ADAM_B1 * m + (1.0 - ADAM_B1) * g
    v_ = ADAM_B2 * v + (1.0 - ADAM_B2) * (g * g)
    return -ADAM_LR * ((m_ / BC1) / (jnp.sqrt(v_ / BC2) + ADAM_EPS) + ADAM_WD * w), m_, v_


def adamw(w, m, v, gparts, *, tr, name):
    NL, R, Wd = w.shape
    nr = R // tr

    def body(w_ref, m_ref, v_ref, *rest):
        gp_refs, (g_ref, d_ref, nm_ref, nv_ref) = rest[:NL], rest[NL:]
        for l in range(NL):
            @pl.when(pl.program_id(0) == l)
            def _():
                g = gp_refs[l][0].astype(F32)
                for s in range(1, gp_refs[l].shape[0]):
                    g = g + gp_refs[l][s].astype(F32)
                g_ref[...] = g
                d_ref[...], nm_ref[...], nv_ref[...] = _adamw_math(w_ref[...], m_ref[...], v_ref[...], g)

    row = pl.BlockSpec((None, tr, Wd), lambda l, i: (l, i, 0))
    gspecs = [pl.BlockSpec((gparts[l].shape[0], tr, Wd), (lambda l_, i, l=l: (0, jnp.clip(i + (l_ - l) * nr, 0, nr - 1), 0)))
              for l in range(NL)]
    return pl.pallas_call(
        body, name=name, grid=(NL, nr),
        in_specs=[row, row, row] + gspecs, out_specs=[row] * 4, out_shape=[jax.ShapeDtypeStruct((NL, R, Wd), F32)] * 4,
        compiler_params=_cp(("arbitrary", "arbitrary")),
    )(w, m, v, *gparts)


def small_update(gpacks, me, params, loss_row, *, name):
    n = len(params)

    def body(me_ref, gp_ref, *refs):
        ins, outs, gs_ref = refs[:3 * n], refs[3 * n:-1], refs[-1]
        gs_ref[...] = gp_ref[0].astype(F32)
        for dv in range(1, N_DEV):
            gs_ref[...] += gp_ref[dv].astype(F32)
        for p, (w, _, _, off, per_dev) in enumerate(params):
            w_ref, m_ref, v_ref = ins[3 * p:3 * p + 3]
            g_ref, d_ref, nm_ref, nv_ref = outs[4 * p:4 * p + 4]
            rows, cols = w.shape
            if cols == LANES and rows % 8 == 0 and not per_dev:
                g = gs_ref[off:off + rows, :]
                g_ref[...] = g
                d_ref[...], nm_ref[...], nv_ref[...] = _adamw_math(w_ref[...], m_ref[...], v_ref[...], g)
                continue
            chunks = -(-cols // LANES)
            base = off + me_ref[0] * per_dev if per_dev else off
            for i in range(rows):
                for j in range(chunks):
                    wd = min(LANES, cols - j * LANES)
                    at = (slice(i, i + 1), slice(j * LANES, j * LANES + wd))
                    g = gs_ref[pl.ds(base + i * chunks + j, 1), 0:wd]
                    g_ref[at] = g
                    d_ref[at], nm_ref[at], nv_ref[at] = _adamw_math(w_ref[at], m_ref[at], v_ref[at], g)
        outs[-1][...] = jnp.sum(gs_ref[loss_row:loss_row + 1, :], axis=1, keepdims=True)

    vm = pl.BlockSpec(memory_space=pltpu.VMEM)
    flat = [a for w, m, v, _, _ in params for a in (w, m, v)]
    out_shape = [jax.ShapeDtypeStruct(w.shape, F32) for w, _, _, _, _ in params for _ in range(4)] + [jax.ShapeDtypeStruct((1, 1), F32)]
    return pl.pallas_call(
        body, name=name, in_specs=[pl.BlockSpec(memory_space=pltpu.SMEM)] + [vm] * (1 + len(flat)),
        out_specs=[vm] * len(out_shape), out_shape=out_shape,
        scratch_shapes=[pltpu.VMEM(gpacks.shape[1:], F32)],
        compiler_params=pltpu.CompilerParams(vmem_limit_bytes=VMEM_LIMIT),
    )(me, gpacks, *flat)


def ada_fwd_mm(cs, w_ada, b_loc, *, name):
    R, D = cs.shape
    nl, _, n = w_ada.shape

    def body(c_ref, w_ref, b_ref, s_ref, m_ref):
        c = c_ref[...]
        s = c * jax.nn.sigmoid(c)
        s_ref[...] = s
        for i in range(nl):
            m_ref[i] = _dot(s.astype(BF16), w_ref[i].astype(BF16)) + b_ref[i:i + 1, :]

    return pl.pallas_call(
        body, name=name, in_specs=[_full((R, D)), _full((nl, D, n)), _full((nl, n))],
        out_specs=[_full((R, D)), _full((nl, R, n))], grid=(1,),
        out_shape=[jax.ShapeDtypeStruct((R, D), F32), jax.ShapeDtypeStruct((nl, R, n), F32)],
        compiler_params=_cp(("arbitrary",)),
    )(cs, w_ada, b_loc)


def ada_bwd_mm(s, c_ctx, dall, w_ada, *, name):
    R, D = s.shape
    nl, _, n = w_ada.shape

    def body(s_ref, cc_ref, d_ref, w_ref, gw_ref, dcc_ref):
        sb = s_ref[...].astype(BF16)
        row = lax.broadcasted_iota(jnp.int32, (R, 1), 0)
        dctx = d_ref[0, 1:2, :]
        for dv in range(1, N_DEV):
            dctx = dctx + d_ref[dv, 1:2, :]
        for i in range(nl):
            dm = jnp.zeros((R, n), F32)
            for dv in range(N_DEV):
                dm = dm + jnp.where(row == dv, d_ref[dv, 2 * i:2 * i + 1, :], 0.0)
            if i == 0:
                dm = dm + jnp.where(row == N_DEV, dctx, 0.0)
            gw_ref[i] = _dot_tn(sb, dm.astype(BF16))
        cc = cc_ref[...]
        sg = jax.nn.sigmoid(cc)
        ds = _dot_nt(jnp.broadcast_to(dctx, (8, n)).astype(BF16), w_ref[0].astype(BF16))
        dcc_ref[...] = ds * (sg * (1.0 + cc * (1.0 - sg)))

    return pl.pallas_call(
        body, name=name, grid=(1,),
        in_specs=[_full((R, D)), _full((1, D)), _full((N_DEV, 3, n)), _full((nl, D, n))],
        out_specs=[_full((nl, D, n)), _full((8, D))],
        out_shape=[jax.ShapeDtypeStruct((nl, D, n), F32), jax.ShapeDtypeStruct((8, D), F32)],
        compiler_params=_cp(("arbitrary",)),
    )(s, c_ctx, dall, w_ada)


def _place():
    x, y, c = lax.axis_index("x"), lax.axis_index("y"), lax.axis_index("c")
    return x, y, c


def _lin(p):
    return 4 * p[0] + 2 * p[1] + p[2]


def all_gather_small(xb, *, name):
    R, W = xb.shape

    def body(x_ref, out_ref, send_sems, recv_sems, local_sem):
        x, y, c = _place()
        me = _lin((x, y, c))
        mine = pltpu.make_async_copy(x_ref, out_ref.at[me], local_sem)
        mine.start()
        copies = []
        for k in range(1, N_DEV):
            peer = (x ^ (k >> 2), y ^ ((k >> 1) & 1), c ^ (k & 1))
            mk = lambda dst, k=k, peer=peer: pltpu.make_async_remote_copy(
                src_ref=x_ref, dst_ref=dst, send_sem=send_sems.at[k - 1], recv_sem=recv_sems.at[k - 1], device_id=peer, device_id_type=MESH)
            mk(out_ref.at[me]).start()
            copies.append(mk(out_ref.at[_lin(peer)]))
        for cp in copies:
            cp.wait_recv()
        for cp in copies:
            cp.wait_send()
        mine.wait()

    vm = pl.BlockSpec(memory_space=pltpu.VMEM)
    return pl.pallas_call(
        body, name=name, in_specs=[vm], out_specs=vm, out_shape=jax.ShapeDtypeStruct((N_DEV, R, W), xb.dtype),
        scratch_shapes=[pltpu.SemaphoreType.DMA((7,)), pltpu.SemaphoreType.DMA((7,)), pltpu.SemaphoreType.DMA],
        compiler_params=pltpu.CompilerParams(vmem_limit_bytes=VMEM_LIMIT),
    )(xb)


HBM_SPEC = pl.BlockSpec(memory_space=pltpu.HBM)
SEM_SPEC = pl.BlockSpec(memory_space=pltpu.SEMAPHORE)
ORDERED_EFFECT = pltpu.SideEffectType.DATAFLOW_SIDE_EFFECTING


def _exchange_copies(srcs, lands, sems, scatter):
    x, y, c = _place()
    me = _lin((x, y, c))
    for j in range(len(srcs)):
        r = lands[j].shape[0] // N_DEV
        block = lambda d, j=j, r=r: pl.ds(pl.multiple_of(d * r, 16), r)
        for k in range(1, N_DEV):
            peer = (x ^ (k >> 2), y ^ ((k >> 1) & 1), c ^ (k & 1))
            src = srcs[j].at[block(_lin(peer)), :] if scatter else srcs[j]
            mk = lambda dst, j=j, k=k, peer=peer, src=src: pltpu.make_async_remote_copy(
                src_ref=src, dst_ref=dst, send_sem=sems[2 * j].at[k - 1], recv_sem=sems[2 * j + 1].at[k - 1],
                device_id=peer, device_id_type=MESH)
            yield mk(lands[j].at[block(me), :]), mk(lands[j].at[block(_lin(peer)), :])


def exchange_start(srcs, lands, *, scatter, name):
    nw = len(srcs)

    def body(*refs):
        for start, _ in _exchange_copies(refs[:nw], refs[nw:2 * nw], refs[2 * nw:4 * nw], scatter):
            start.start()
        refs[-1][...] = jnp.zeros_like(refs[-1])

    thru = [pltpu.HBM(a.shape, a.dtype) for a in (*srcs, *lands)]
    res = pl.pallas_call(
        body, name=name, in_specs=[HBM_SPEC] * (2 * nw),
        out_specs=[SEM_SPEC] * (2 * nw) + [HBM_SPEC] * (2 * nw) + [pl.BlockSpec(memory_space=pltpu.VMEM)],
        out_shape=[pltpu.SemaphoreType.DMA((N_DEV - 1,))] * (2 * nw) + thru + [jax.ShapeDtypeStruct((8, LANES), F32)],
        input_output_aliases={i: 2 * nw + i for i in range(2 * nw)},
        compiler_params=pltpu.CompilerParams(has_side_effects=ORDERED_EFFECT),
    )(*[pltpu.with_memory_space_constraint(a, pltpu.HBM) for a in (*srcs, *lands)])
    return res[:2 * nw], res[2 * nw:3 * nw], res[3 * nw:4 * nw], res[-1]


def exchange_wait(srcs, lands, sems, after, *, scatter, name):
    nw = len(srcs)
    after = list(after) if isinstance(after, (list, tuple)) else [after]

    def body(*refs):
        for _, arrive in _exchange_copies(refs[:nw], refs[nw:2 * nw], refs[2 * nw:4 * nw], scatter):
            arrive.wait_send()
            arrive.wait_recv()

    res = pl.pallas_call(
        body, name=name, in_specs=[HBM_SPEC] * (2 * nw) + [SEM_SPEC] * (2 * nw) + [pl.BlockSpec(memory_space=pl.ANY)] * len(after),
        out_specs=[HBM_SPEC] * (2 * nw), out_shape=[pltpu.HBM(a.shape, a.dtype) for a in (*srcs, *lands)],
        input_output_aliases={i: i for i in range(2 * nw)},
        compiler_params=pltpu.CompilerParams(has_side_effects=ORDERED_EFFECT),
    )(*srcs, *lands, *sems, *after)
    return res[nw:]


def place_own(srcs, rows, me, *, scatter, name):
    nw = len(srcs)
    lands = [lax.empty((N_DEV * r, s_.shape[1]), s_.dtype) for r, s_ in zip(rows, srcs)]

    def body(me_ref, *refs):
        for j in range(nw):
            refs[2 * nw + j][...] = refs[j][...]

    mine = lambda i, me_ref: (me_ref[0], 0)
    src_at = mine if scatter else (lambda i, me_ref: (0, 0))
    blocks = [(r, s_.shape[1]) for r, s_ in zip(rows, srcs)]
    return pl.pallas_call(
        body, name=name,
        grid_spec=pltpu.PrefetchScalarGridSpec(
            num_scalar_prefetch=1, grid=(1,),
            in_specs=[pl.BlockSpec(b_, src_at) for b_ in blocks] + [pl.BlockSpec(memory_space=pl.ANY)] * nw,
            out_specs=[pl.BlockSpec(b_, mine) for b_ in blocks]),
        out_shape=[jax.ShapeDtypeStruct(l_.shape, l_.dtype) for l_ in lands],
        input_output_aliases={1 + nw + j: j for j in range(nw)},
        compiler_params=_cp(("arbitrary",)),
    )(jnp.reshape(me, (1,)).astype(jnp.int32), *srcs, *lands)


def _rope_tables(L):
    t = jnp.arange(L)
    inv = ROPE_BASE ** (-jnp.arange(ROPE_FREQS, dtype=F32) / ROPE_FREQS)
    ar = (t // GRID_W).astype(F32)[:, None] * inv
    ac = (t % GRID_W).astype(F32)[:, None] * inv
    z = jnp.zeros_like(ar)
    cos = jnp.concatenate([jnp.cos(ar), jnp.cos(ar), jnp.cos(ac), jnp.cos(ac)], axis=1)
    sa = jnp.concatenate([-jnp.sin(ar), z, -jnp.sin(ac), z], axis=1)
    sb = jnp.concatenate([z, jnp.sin(ar), z, jnp.sin(ac)], axis=1)
    return tuple(jnp.tile(a, (1, LANES // HEAD_DIM)) for a in (cos, sa, sb))


def _nat2d(a):
    return a.reshape(1, -1) if a.ndim == 1 else a.reshape(-1, a.shape[-1])


def _pack_rows(a):
    rows, cols = a.shape
    chunks = -(-cols // LANES)
    f = jnp.pad(a, ((0, 0), (0, chunks * LANES - cols))).reshape(rows * chunks, LANES)
    return jnp.pad(f, ((0, -f.shape[0] % 8), (0, 0)))


def _rows128(a):
    f = a.reshape(-1)
    n = -(-f.shape[0] // (8 * LANES)) * 8 * LANES
    return jnp.pad(f, (0, n - f.shape[0])).reshape(-1, LANES)


def kernel(x, c, ctx, c_ctx, w_ada, b_ada, g_mix_pre, g_mix_post, g_ffn_pre, g_ffn_post, w_in_even, w_pool, pool_scale, attn_sink, w_out_even, w_in_odd, sgu_ln_g, sgu_ln_b, sgu_w, sgu_b, w_out_odd, w_ffn_up, ffn_conv_w, ffn_conv_b, w_ffn_down, loss_target, m_c_ctx, m_w_ada, m_b_ada, m_g_mix_pre, m_g_mix_post, m_g_ffn_pre, m_g_ffn_post, m_w_in_even, m_w_pool, m_pool_scale, m_attn_sink, m_w_out_even, m_w_in_odd, m_sgu_ln_g, m_sgu_ln_b, m_sgu_w, m_sgu_b, m_w_out_odd, m_w_ffn_up, m_ffn_conv_w, m_ffn_conv_b, m_w_ffn_down, v_c_ctx, v_w_ada, v_b_ada, v_g_mix_pre, v_g_mix_post, v_g_ffn_pre, v_g_ffn_post, v_w_in_even, v_w_pool, v_pool_scale, v_attn_sink, v_w_out_even, v_w_in_odd, v_sgu_ln_g, v_sgu_ln_b, v_sgu_w, v_sgu_b, v_w_out_odd, v_w_ffn_up, v_ffn_conv_w, v_ffn_conv_b, v_w_ffn_down):
    P = dict(c_ctx=c_ctx, w_ada=w_ada, b_ada=b_ada, g_mix_pre=g_mix_pre, g_mix_post=g_mix_post, g_ffn_pre=g_ffn_pre,
             g_ffn_post=g_ffn_post, w_in_even=w_in_even, w_pool=w_pool, pool_scale=pool_scale, attn_sink=attn_sink,
             w_out_even=w_out_even, w_in_odd=w_in_odd, sgu_ln_g=sgu_ln_g, sgu_ln_b=sgu_ln_b, sgu_w=sgu_w, sgu_b=sgu_b,
             w_out_odd=w_out_odd, w_ffn_up=w_ffn_up, ffn_conv_w=ffn_conv_w, ffn_conv_b=ffn_conv_b, w_ffn_down=w_ffn_down)
    M = dict(c_ctx=m_c_ctx, w_ada=m_w_ada, b_ada=m_b_ada, g_mix_pre=m_g_mix_pre, g_mix_post=m_g_mix_post, g_ffn_pre=m_g_ffn_pre,
             g_ffn_post=m_g_ffn_post, w_in_even=m_w_in_even, w_pool=m_w_pool, pool_scale=m_pool_scale, attn_sink=m_attn_sink,
             w_out_even=m_w_out_even, w_in_odd=m_w_in_odd, sgu_ln_g=m_sgu_ln_g, sgu_ln_b=m_sgu_ln_b, sgu_w=m_sgu_w, sgu_b=m_sgu_b,
             w_out_odd=m_w_out_odd, w_ffn_up=m_w_ffn_up, ffn_conv_w=m_ffn_conv_w, ffn_conv_b=m_ffn_conv_b, w_ffn_down=m_w_ffn_down)
    V = dict(c_ctx=v_c_ctx, w_ada=v_w_ada, b_ada=v_b_ada, g_mix_pre=v_g_mix_pre, g_mix_post=v_g_mix_post, g_ffn_pre=v_g_ffn_pre,
             g_ffn_post=v_g_ffn_post, w_in_even=v_w_in_even, w_pool=v_w_pool, pool_scale=v_pool_scale, attn_sink=v_attn_sink,
             w_out_even=v_w_out_even, w_in_odd=v_w_in_odd, sgu_ln_g=v_sgu_ln_g, sgu_ln_b=v_sgu_ln_b, sgu_w=v_sgu_w, sgu_b=v_sgu_b,
             w_out_odd=v_w_out_odd, w_ffn_up=v_w_ffn_up, ffn_conv_w=v_ffn_conv_w, ffn_conv_b=v_ffn_conv_b, w_ffn_down=v_w_ffn_down)

    x = x[0]
    ctx = ctx[0]
    target = loss_target[0]
    L, D = x.shape
    C = ctx.shape[0]
    tm = min(512, L)
    tm_up = min(1024, L)
    conv_rows = min(512, L)
    me = 4 * lax.axis_index("x") + 2 * lax.axis_index("y") + lax.axis_index("c")
    n_ada = w_ada.shape[2]
    F = w_ffn_down.shape[1] * N_DEV
    half_f = F // 2

    n_cw = ffn_conv_w.shape[2]
    small = jnp.concatenate([_rows128(c), _rows128(sgu_ln_g), _rows128(sgu_ln_b), _rows128(ffn_conv_w)], axis=0)
    small_all = all_gather_small(small, name="gather_small_inputs")
    c_all = small_all[:, :8].reshape(N_DEV, D)
    ln_g = small_all[:, 8].reshape(1, D)
    ln_b = small_all[:, 16].reshape(1, D)
    conv_w = small_all[:, 24:].reshape(N_DEV, -1)[:, :2 * 3 * n_cw].reshape(N_DEV, 2, 3, n_cw)
    conv_w = conv_w.transpose(1, 2, 0, 3).reshape(2, 3, 2 * F)

    cs = jnp.concatenate([c_all, c_ctx[None, :], jnp.zeros((7, D), F32)], axis=0)
    b_loc = lax.dynamic_slice(b_ada, (0, me * n_ada), (2, n_ada))
    silu_c, mods_loc = ada_fwd_mm(cs, w_ada, b_loc, name="ada_fwd")
    mods_all = all_gather_small(mods_loc.reshape(-1, LANES), name="gather_mods")

    shards = [s.astype(BF16) for s in (w_in_even[0].T, w_out_even[0], w_ffn_up[0].T, w_ffn_down[0],
                                       w_in_odd[0].T, w_out_odd[0], w_ffn_up[1].T, w_ffn_down[1])]
    shards, mods_all = lax.optimization_barrier((shards, mods_all))
    w_sems, w_srcs, w_lands, _ = exchange_start(shards, place_own(shards, [s.shape[0] for s in shards], me, scatter=False, name="gather_own"),
                                              scatter=False, name="gather_start")

    def weight(j, after):
        return exchange_wait([w_srcs[j]], [w_lands[j]], w_sems[2 * j:2 * j + 2], after, scatter=False, name=f"gather_wait_{j}")[0]

    mods_all = mods_all.reshape(N_DEV, 2, 16, n_ada).transpose(1, 2, 0, 3).reshape(2, 16, 6 * D)
    mod = lambda i, row: [m_[None, :] for m_ in jnp.split(lax.dynamic_index_in_dim(mods_all[i], row, 0, False), 6)]
    sh_m, sc_m, gt_m, sh_f, sc_f, gt_f = zip(mod(0, me), mod(1, me))
    csh_m, csc_m = mod(0, N_DEV)[:2]

    row = lambda a, i: a[i][None, :]

    cos, sa, sb = _rope_tables(L)
    sink = attn_sink[0]
    bst = sgu_b[0].T
    sgu_wb, sgu_wtb = sgu_w[0].astype(BF16), sgu_w[0].swapaxes(1, 2).astype(BF16)
    wup, wdn = [None, None], [None, None]

    def ffn_fwd(i, xin):
        wup[i] = weight(2 + 4 * i, xin)
        h, hu = pre_mm(xin, row(g_ffn_pre, i), sh_f[i], sc_f[i], wup[i], tm=tm_up, tn=half_f, name=f"ffn_up_{i}")
        a, s1, s2 = conv_fwd(hu, conv_w[i], ffn_conv_b[i][None, :], rows=conv_rows, wblk=2 * LANES, name=f"ffn_conv_{i}")
        wdn[i] = weight(3 + 4 * i, a)
        res = mm_post([a], wdn[i], xin, row(g_ffn_post, i), gt_f[i], tm=tm, target=target if i == 1 else None, name=f"ffn_down_{i}")
        return (h, (hu, s1, s2), a, *res)

    first_mod, cos, sa, sb = lax.optimization_barrier((sh_m[0], cos, sa, sb))
    win_e = permute_heads(weight(0, first_mod))
    h0, u, q, kv = inproj_even(x, row(g_mix_pre, 0), sh_m[0], sc_m[0], win_e, cos, sa, sb, tm=tm, name="in_even")
    hc, kvc = pre_mm(ctx, row(g_mix_pre, 0), csh_m, csc_m, win_e, tm=C, tn=2 * LANES, w_row_off=8 * LANES, name="in_even_ctx")
    pa = [pool_fwd(u, w_pool[0], pool_scale, name="pool_fwd"), attn_fwd(q, kv, kvc, sink, name="attn_fwd")]
    wout_e = permute_heads(weight(1, pa[1]))
    y0, x1 = mm_post(pa, wout_e, x, row(g_mix_post, 0), gt_m[0], tm=tm, name="out_even")
    h1, hu0, a0, f0, x2 = ffn_fwd(0, x1)
    win_o = weight(4, x2)
    h2, z1 = pre_mm(x2, row(g_mix_pre, 1), sh_m[1], sc_m[1], win_o, tm=tm_up, tn=D, name="in_odd")
    us = sgu_fwd(z1, ln_g, ln_b, sgu_wb, bst, name="sgu_fwd")
    wout_o = weight(5, us)
    y1, x3 = mm_post([us], wout_o, x2, row(g_mix_post, 1), gt_m[1], tm=tm, name="out_odd")
    h3, hu1, a1, f1, dx4, loss_part = ffn_fwd(1, x3)

    g_srcs, g_lands, g_sems = [], [], []

    def scatter(grads, nm):
        own = place_own(grads, [g.shape[0] // N_DEV for g in grads], me, scatter=True, name=nm.replace("start", "own"))
        sems, srcs, lands, tok = exchange_start(grads, own, scatter=True, name=nm)
        g_srcs.extend(srcs)
        g_lands.extend(lands)
        g_sems.extend(sems)
        return tok[0:1, 0:1]

    def ffn_bwd(i, dxo, xin, h, hu, a, f, g_post):
        dyf, da, dg_post, dgt = post_bwd_mm(dxo, f, g_post, gt_f[i], wdn[i], tm=tm, name=f"ffn_down_bwd_{i}")
        dhg, dhu, dcwg, dcwu, dcbg, dcbu = conv_bwd(da, hu[1], hu[2], hu[0], conv_w[i], rows=conv_rows, wblk=2 * LANES,
                                                    name=f"ffn_conv_bwd_{i}")
        dxin, dg_pre, dsh, dsc = mm_pre_bwd([dhg, dhu], wup[i], xin, dxo, row(g_ffn_pre, i), sc_f[i], tm=tm, tk=half_f,
                                            name=f"ffn_up_bwd_{i}")
        g_dn = wgrad([a], dyf, tr=2 * LANES, name=f"wgrad_down_{i}")
        g_up = wgrad([dhg, dhu], h, tr=2 * LANES, name=f"wgrad_up_{i}")
        tok = scatter([g_dn, g_up], f"scatter_start_ffn_{i}")
        return dxin, tok, dict(g_ffn_post=dg_post, g_ffn_pre=dg_pre, gt_f=dgt, sh_f=dsh, sc_f=dsc,
                               ffn_conv_w=jnp.concatenate([dcwg, dcwu], axis=1), ffn_conv_b=jnp.concatenate([dcbg, dcbu], axis=1)[0])

    dx3, tok, sf1 = ffn_bwd(1, dx4, x3, h3, hu1, a1, f1, row(g_ffn_post, 1))
    dy1, dus, dg_mpost1, dgt_m1 = post_bwd_mm(dx3, y1, row(g_mix_post, 1) + tok, gt_m[1], wout_o, tm=tm, name="out_odd_bwd")
    dz1, dws, dbs, dlng, dlnb = sgu_bwd(z1, dus, ln_g, ln_b, sgu_wb, sgu_wtb, bst, name="sgu_bwd")
    dx2, dg_mpre1, dsh_m1, dsc_m1 = mm_pre_bwd([dz1], win_o, x2, dx3, row(g_mix_pre, 1), sc_m[1], tm=tm, tk=D, name="in_odd_bwd")
    tok = scatter([wgrad([us], dy1, tr=2 * LANES, name="wgrad_out_odd"), wgrad([dz1], h2, tr=2 * LANES, name="wgrad_in_odd")],
                  "scatter_start_mix_1")

    dx1, tok, sf0 = ffn_bwd(0, dx2, x1, h1, hu0, a0, f0, row(g_ffn_post, 0) + tok)
    dy0, dpa, dg_mpost0, dgt_m0 = post_bwd_mm(dx1, y0, row(g_mix_post, 0) + tok, gt_m[0], wout_e, tm=tm, name="out_even_bwd")
    tok = scatter([permute_heads(wgrad(pa, dy0, tr=2 * LANES, name="wgrad_out_even"), inverse=True)], "scatter_start_out_0")
    du, dwp, dps = pool_bwd(u, dpa, w_pool[0], pool_scale + tok, name="pool_bwd")
    dq, dkv, dkvc, dsink = attn_bwd(q, kv, kvc, sink, dpa, cos, sa, sb, name="attn_bwd")
    dz0 = jnp.concatenate([du, dq, dkv], axis=1)
    dzc = jnp.concatenate([jnp.zeros((C, 8 * LANES), BF16), dkvc], axis=1)
    tok = scatter([permute_heads(wgrad([dz0], h0, tr=2 * LANES, extra=(dzc, hc), name="wgrad_in_even"), inverse=True)],
                  "scatter_start_in_0")
    grad_x, dg_mpre0, dsh_m0, dsc_m0 = mm_pre_bwd([dz0], win_e, x, dx1, row(g_mix_pre, 0) + tok, sc_m[0], tm=tm, tk=dz0.shape[1],
                                                  name="in_even_bwd")
    _, dg_mpre0c, dcsh, dcsc = mm_pre_bwd([dkvc], win_e, ctx, None, row(g_mix_pre, 0), csc_m, tm=C, tk=2 * LANES,
                                          w_row_off=8 * LANES, name="in_even_ctx_bwd")

    out, ran = {}, {}

    def update(name, lands, transposed):
        w_, m_, v_ = (a.transpose(0, 2, 1) if transposed else a for a in (P[name], M[name], V[name]))
        r = w_.shape[1]
        tr = r // 4 if r % 64 == 0 and r > 256 else r
        res = adamw(w_, m_, v_, [l_.reshape(N_DEV, r, l_.shape[1]) for l_ in lands], tr=tr, name=f"adamw_{name}")
        ran[name] = res[0]
        for kind, val in zip(("grad", "delta", "new_m", "new_v"), res):
            out[(kind, name)] = val.transpose(0, 2, 1) if transposed else val

    zero = jnp.zeros((1, D), F32)
    dmod0 = jnp.concatenate([dsh_m0, dsc_m0, dgt_m0, sf0["sh_f"], sf0["sc_f"], sf0["gt_f"]], axis=1)
    dmodc = jnp.concatenate([dcsh, dcsc, zero, zero, zero, zero], axis=1)
    dmod1 = jnp.concatenate([dsh_m1, dsc_m1, dgt_m1, sf1["sh_f"], sf1["sc_f"], sf1["gt_f"]], axis=1)
    dmods = jnp.concatenate([dmod0, dmodc, dmod1], axis=0)
    dm = dmods.reshape(-1, LANES).astype(BF16)
    d_sems, d_srcs, d_lands, d_tok = exchange_start(
        [dm], place_own([dm], [dm.shape[0]], me, scatter=False, name="dmods_own"), scatter=False, name="dmods_start")
    slots = exchange_wait(g_srcs[:6], g_lands[:6], g_sems[:12], d_tok, scatter=True, name="scatter_wait_early")
    early = slots
    update("w_ffn_down", [slots[4], slots[0]], False)
    update("w_in_odd", [slots[3]], True)
    update("w_out_odd", [slots[2]], False)
    updated = lambda names: [ran[k] for k in names]
    dmods_all = exchange_wait(d_srcs, d_lands, d_sems, updated(("w_ffn_down", "w_in_odd", "w_out_odd")), scatter=False, name="dmods_wait")[0]
    dall = lax.dynamic_index_in_dim(dmods_all.astype(F32).reshape(N_DEV, 3, N_DEV, n_ada), me, 2, False)
    g_w_ada, dcc = ada_bwd_mm(silu_c, c_ctx[None, :], dall, w_ada, name="ada_bwd")

    rep = dict(
        c_ctx=dcc[0:1],
        b_ada=jnp.concatenate([dmod0 + dmodc, dmod1]),
        g_mix_pre=jnp.concatenate([dg_mpre0 + dg_mpre0c, dg_mpre1]),
        g_mix_post=jnp.concatenate([dg_mpost0, dg_mpost1]),
        g_ffn_pre=jnp.concatenate([sf0["g_ffn_pre"], sf1["g_ffn_pre"]]),
        g_ffn_post=jnp.concatenate([sf0["g_ffn_post"], sf1["g_ffn_post"]]),
        w_pool=_nat2d(dwp), pool_scale=dps, attn_sink=dsink[:, :N_Q_HEADS],
        sgu_w=_nat2d(dws), sgu_b=dbs[:, :sgu_b.shape[1]].T,
        ffn_conv_b=jnp.stack([sf0["ffn_conv_b"], sf1["ffn_conv_b"]]),
    )
    hi = loss_part.astype(BF16).astype(F32)
    mid = (loss_part - hi).astype(BF16).astype(F32)
    loss_piece = jnp.pad(jnp.concatenate([hi, mid, loss_part - hi - mid], axis=1), ((0, 7), (0, LANES - 3)))
    conv_g = jnp.stack([sf0["ffn_conv_w"], sf1["ffn_conv_w"]]).reshape(2 * 3, N_DEV, n_cw).swapaxes(0, 1)
    shard_full = dict(sgu_ln_g=dlng.reshape(N_DEV, LANES), sgu_ln_b=dlnb.reshape(N_DEV, LANES),
                      ffn_conv_w=jnp.concatenate([_pack_rows(conv_g[d]) for d in range(N_DEV)], axis=0))
    small_names = list(rep) + list(shard_full)
    pieces = [_pack_rows(rep[k]) for k in rep] + list(shard_full.values()) + [loss_piece]
    sizes = [p.shape[0] for p in pieces]
    offs = [sum(sizes[:i]) for i in range(len(sizes))]
    pieces.append(jnp.zeros((-sum(sizes) % 16, LANES), F32))
    gpack = jnp.concatenate(pieces, axis=0).astype(BF16)
    own = place_own([gpack], [gpack.shape[0]], me, scatter=False, name="smallgrad_own")
    s_sems, s_srcs, s_lands, small_tok = exchange_start([gpack], own, scatter=False, name="smallgrad_start")

    slots = exchange_wait(g_srcs[6:], g_lands[6:], g_sems[12:], small_tok, scatter=True, name="scatter_wait_late")
    update("w_in_even", [slots[1]], True)
    update("w_out_even", [slots[0]], False)
    update("w_ffn_up", [early[5], early[1]], True)
    res = adamw(w_ada, m_w_ada, v_w_ada, [g_w_ada[l][None] for l in range(w_ada.shape[0])], tr=D // 4, name="adamw_w_ada")
    ran["w_ada"] = res[0]
    for kind, val in zip(("grad", "delta", "new_m", "new_v"), res):
        out[(kind, "w_ada")] = val

    gpacks = exchange_wait(s_srcs, s_lands, s_sems, updated(("w_in_even", "w_out_even", "w_ffn_up", "w_ada")), scatter=False,
                           name="smallgrad_wait")[0]
    per_dev = {k: shard_full[k].shape[0] // N_DEV for k in shard_full}
    params = [(_nat2d(P[k]), _nat2d(M[k]), _nat2d(V[k]), offs[i], per_dev.get(k, 0)) for i, k in enumerate(small_names)]
    res = small_update(gpacks.reshape(N_DEV, -1, LANES), jnp.reshape(me, (1,)).astype(jnp.int32), params, offs[-1], name="adamw_small")
    for i, k in enumerate(small_names):
        for kind, val in zip(("grad", "delta", "new_m", "new_v"), res[4 * i:4 * i + 4]):
            out[(kind, k)] = val.reshape(P[k].shape)
    loss = res[-1][0, 0]

    names = list(P)
    final = [loss, grad_x[None]]
    for kind in ("grad", "delta", "new_m", "new_v"):
        for k in names:
            val = out[(kind, k)]
            final.append(val)
    return tuple(final)
```

```python
import functools
import math

import jax
import jax.numpy as jnp
from jax import lax
from jax.experimental import pallas as pl
from jax.experimental.pallas import tpu as pltpu

F32 = jnp.float32
BF16 = jnp.bfloat16
MESH = pl.DeviceIdType.MESH
N_DEV = 8
LANES = 128
VMEM_LIMIT = 48 * 1024 * 1024
EPS = 1e-6
NEG_INF = -1e30
GRID_W = 64
WINDOW = 128
BLK = 128
HEAD_DIM = 64
N_Q_HEADS = 8
N_KV_HEADS = 2
GQA = N_Q_HEADS // N_KV_HEADS
POOL_WINDOWS = (2, 4, 8, 16)
ROPE_BASE = 10000.0
ROPE_FREQS = HEAD_DIM // 4
PAD = 16
ADAM_LR, ADAM_B1, ADAM_B2, ADAM_EPS, ADAM_WD, ADAM_STEP = 0.001, 0.9, 0.999, 1e-08, 0.01, 10
BC1 = 1.0 - ADAM_B1 ** ADAM_STEP
BC2 = 1.0 - ADAM_B2 ** ADAM_STEP
SQRT_2_OVER_PI = math.sqrt(2.0 / math.pi)
GELU_C = 0.044715


def _cp(sem=None):
    return pltpu.CompilerParams(dimension_semantics=sem, vmem_limit_bytes=VMEM_LIMIT)


def _dot(a, b):
    return jnp.dot(a, b, preferred_element_type=F32)


def _dot_nt(a, b):
    return lax.dot_general(a, b, (((1,), (1,)), ((), ())), preferred_element_type=F32)


def _dot_tn(a, b):
    return lax.dot_general(a, b, (((0,), (0,)), ((), ())), preferred_element_type=F32)


def _rms(x):
    r = lax.rsqrt(jnp.mean(x * x, axis=-1, keepdims=True) + EPS)
    return x * r, r


def _rms_bwd(dn, n, r):
    return r * (dn - n * jnp.mean(dn * n, axis=-1, keepdims=True))


def _colsum(a):
    return jnp.sum(a, axis=0, keepdims=True)


def _rope(x, c, sa, sb):
    return x * c + pltpu.roll(x, LANES - ROPE_FREQS, 1) * sa + pltpu.roll(x, ROPE_FREQS, 1) * sb


def _full(shape):
    return pl.BlockSpec(shape, lambda *_: (0,) * len(shape))


def pre_mm(x, g, sh, sc, wt, *, tm, tn, w_row_off=0, name):
    T, D = x.shape
    n_rows = wt.shape[0] - w_row_off
    off = w_row_off // tn

    def body(x_ref, g_ref, sh_ref, sc_ref, w_ref, h_ref, z_ref):
        @pl.when(pl.program_id(1) == 0)
        def _():
            n, _ = _rms(x_ref[...])
            h_ref[...] = (n * g_ref[...] * (1.0 + sc_ref[...]) + sh_ref[...]).astype(BF16)

        z_ref[...] = _dot_nt(h_ref[...], w_ref[...]).astype(BF16)

    vec = pl.BlockSpec((1, D), lambda i, j: (0, 0))
    return pl.pallas_call(
        body, name=name, grid=(T // tm, n_rows // tn),
        in_specs=[pl.BlockSpec((tm, D), lambda i, j: (i, 0)), vec, vec, vec, pl.BlockSpec((tn, D), lambda i, j: (j + off, 0))],
        out_specs=[pl.BlockSpec((tm, D), lambda i, j: (i, 0)), pl.BlockSpec((tm, tn), lambda i, j: (i, j))],
        out_shape=[jax.ShapeDtypeStruct((T, D), BF16), jax.ShapeDtypeStruct((T, n_rows), BF16)],
        compiler_params=_cp(("parallel", "arbitrary")),
    )(x, g, sh, sc, wt)


def inproj_even(x, g, sh, sc, wt, cos, sa, sb, *, tm, name):
    T, D = x.shape
    N = wt.shape[0]

    def body(x_ref, g_ref, sh_ref, sc_ref, w_ref, c_ref, sa_ref, sb_ref, h_ref, u_ref, q_ref, kv_ref):
        n, _ = _rms(x_ref[...])
        h = (n * g_ref[...] * (1.0 + sc_ref[...]) + sh_ref[...]).astype(BF16)
        h_ref[...] = h
        z = _dot_nt(h, w_ref[...])
        u_ref[...] = z[:, :4 * LANES]
        c, a, b = c_ref[...], sa_ref[...], sb_ref[...]
        for s in range(4):
            q_ref[:, s * LANES:(s + 1) * LANES] = _rope(z[:, (4 + s) * LANES:(5 + s) * LANES], c, a, b).astype(BF16)
        kv_ref[:, :LANES] = _rope(z[:, 8 * LANES:9 * LANES], c, a, b).astype(BF16)
        kv_ref[:, LANES:] = z[:, 9 * LANES:].astype(BF16)

    vec = pl.BlockSpec((1, D), lambda i: (0, 0))
    row = lambda w: pl.BlockSpec((tm, w), lambda i: (i, 0))
    return pl.pallas_call(
        body, name=name, grid=(T // tm,),
        in_specs=[row(D), vec, vec, vec, _full((N, D)), row(LANES), row(LANES), row(LANES)],
        out_specs=[row(D), row(4 * LANES), row(4 * LANES), row(2 * LANES)],
        out_shape=[jax.ShapeDtypeStruct((T, D), BF16), jax.ShapeDtypeStruct((T, 4 * LANES), F32),
                   jax.ShapeDtypeStruct((T, 4 * LANES), BF16), jax.ShapeDtypeStruct((T, 2 * LANES), BF16)],
        compiler_params=_cp(("parallel",)),
    )(x, g, sh, sc, wt, cos, sa, sb)


def mm_post(a_parts, w, x, g, gt, *, tm, target=None, name):
    T = a_parts[0].shape[0]
    D = w.shape[1]
    npart = len(a_parts)
    offs = [sum(a_.shape[1] for a_ in a_parts[:p]) for p in range(npart + 1)]
    with_loss = target is not None

    def body(*refs):
        a_refs, (w_ref, x_ref, g_ref, gt_ref) = refs[:npart], refs[npart:npart + 4]
        y = _dot(a_refs[0][...], w_ref[offs[0]:offs[1], :])
        for p in range(1, npart):
            y = y + _dot(a_refs[p][...], w_ref[offs[p]:offs[p + 1], :])
        n, _ = _rms(y)
        xn = x_ref[...] + gt_ref[...] * (n * g_ref[...])
        if not with_loss:
            y_ref, xn_ref = refs[npart + 4:]
            y_ref[...] = y.astype(BF16)
            xn_ref[...] = xn
            return
        t_ref, y_ref, d_ref, l_ref = refs[npart + 4:]
        y_ref[...] = y.astype(BF16)

        @pl.when(pl.program_id(0) == 0)
        def _():
            l_ref[...] = jnp.zeros_like(l_ref)

        e = xn - t_ref[...]
        l_ref[...] += 0.5 * jnp.sum(jnp.mean(e * e, axis=-1, keepdims=True), axis=0, keepdims=True)
        d_ref[...] = e * (1.0 / D)

    vec = pl.BlockSpec((1, D), lambda i: (0, 0))
    row = lambda w_: pl.BlockSpec((tm, w_), lambda i: (i, 0))
    in_specs = [row(a_.shape[1]) for a_ in a_parts] + [_full(w.shape), row(D), vec, vec]
    out_specs = [row(D), row(D)]
    out_shape = [jax.ShapeDtypeStruct((T, D), BF16), jax.ShapeDtypeStruct((T, D), F32)]
    if with_loss:
        in_specs.append(row(D))
        out_specs.append(_full((1, 1)))
        out_shape.append(jax.ShapeDtypeStruct((1, 1), F32))
    return pl.pallas_call(
        body, name=name, grid=(T // tm,), in_specs=in_specs, out_specs=out_specs, out_shape=out_shape,
        compiler_params=_cp(("arbitrary",) if with_loss else ("parallel",)),
    )(*a_parts, w, x, g, gt, *((target,) if with_loss else ()))


def post_bwd_mm(dxn, y, g, gt, w, *, tm, name):
    T, D = y.shape
    K = w.shape[0]

    def body(dxn_ref, y_ref, g_ref, gt_ref, w_ref, dy_ref, da_ref, dg_ref, dgt_ref):
        @pl.when(pl.program_id(0) == 0)
        def _():
            dg_ref[...] = jnp.zeros_like(dg_ref)
            dgt_ref[...] = jnp.zeros_like(dgt_ref)

        d = dxn_ref[...]
        n, r = _rms(y_ref[...].astype(F32))
        g_, gt_ = g_ref[...], gt_ref[...]
        dg_ref[...] += _colsum(d * gt_ * n)
        dgt_ref[...] += _colsum(d * g_ * n)
        dy = _rms_bwd(d * (gt_ * g_), n, r).astype(BF16)
        dy_ref[...] = dy
        da_ref[...] = _dot_nt(dy, w_ref[...]).astype(BF16)

    vec = pl.BlockSpec((1, D), lambda i: (0, 0))
    row = lambda w_: pl.BlockSpec((tm, w_), lambda i: (i, 0))
    return pl.pallas_call(
        body, name=name, grid=(T // tm,),
        in_specs=[row(D), row(D), vec, vec, _full((K, D))],
        out_specs=[row(D), row(K), vec, vec],
        out_shape=[jax.ShapeDtypeStruct((T, D), BF16), jax.ShapeDtypeStruct((T, K), BF16),
                   jax.ShapeDtypeStruct((1, D), F32), jax.ShapeDtypeStruct((1, D), F32)],
        compiler_params=_cp(("arbitrary",)),
    )(dxn, y, g, gt, w)


def mm_pre_bwd(dzs, wt, x, dres, g, sc, *, tm, tk, w_row_off=0, name):
    T, N = dzs[0].shape
    D = x.shape[1]
    nk = N // tk
    npart = len(dzs)
    off = w_row_off // tk
    has_res = dres is not None

    def body(*refs):
        dz_refs = refs[:npart]
        w_refs = refs[npart:2 * npart]
        rest = refs[2 * npart:]
        x_ref = rest[0]
        dres_ref = rest[1] if has_res else None
        g_ref, sc_ref, dx_ref, dg_ref, dsh_ref, dsc_ref, acc = rest[1 + has_res:]
        i, k = pl.program_id(0), pl.program_id(1)

        @pl.when(jnp.logical_and(i == 0, k == 0))
        def _():
            dg_ref[...] = jnp.zeros_like(dg_ref)
            dsh_ref[...] = jnp.zeros_like(dsh_ref)
            dsc_ref[...] = jnp.zeros_like(dsc_ref)

        part = _dot(dz_refs[0][...], w_refs[0][...])
        for p in range(1, npart):
            part = part + _dot(dz_refs[p][...], w_refs[p][...])

        @pl.when(k == 0)
        def _():
            acc[...] = part

        @pl.when(k > 0)
        def _():
            acc[...] += part

        @pl.when(k == nk - 1)
        def _():
            dh = acc[...]
            n, r = _rms(x_ref[...])
            g_, s1 = g_ref[...], 1.0 + sc_ref[...]
            dsh_ref[...] += _colsum(dh)
            dsc_ref[...] += _colsum(dh * n * g_)
            dg_ref[...] += _colsum(dh * s1 * n)
            dxp = _rms_bwd(dh * (g_ * s1), n, r)
            dx_ref[...] = dxp + dres_ref[...] if has_res else dxp

    vec = pl.BlockSpec((1, D), lambda i, k: (0, 0))
    row = pl.BlockSpec((tm, D), lambda i, k: (i, 0))
    w_specs = [pl.BlockSpec((tk, D), (lambda i, k, p=p: (k + off + p * nk, 0))) for p in range(npart)]
    res_specs, res_args = ([row], (dres,)) if has_res else ([], ())
    return pl.pallas_call(
        body, name=name, grid=(T // tm, nk),
        in_specs=[pl.BlockSpec((tm, tk), lambda i, k: (i, k))] * npart + w_specs + [row] + res_specs + [vec, vec],
        out_specs=[row, vec, vec, vec],
        out_shape=[jax.ShapeDtypeStruct((T, D), F32)] + [jax.ShapeDtypeStruct((1, D), F32)] * 3,
        scratch_shapes=[pltpu.VMEM((tm, D), F32)],
        compiler_params=_cp(("arbitrary", "arbitrary")),
    )(*dzs, *([wt] * npart), x, *res_args, g, sc)


def wgrad(a_parts, b, *, tr, extra=None, name):
    T, R = a_parts[0].shape
    D = b.shape[1]
    npart = len(a_parts)
    nr = R // tr

    def body(*refs):
        a_refs, b_ref = refs[:npart], refs[npart]
        g_ref = refs[-1]
        for p in range(npart):
            @pl.when(pl.program_id(0) // nr == p)
            def _():
                acc = _dot_tn(a_refs[p][...], b_ref[...])
                if extra is not None:
                    acc += _dot_tn(refs[npart + 1][...], refs[npart + 2][...])
                g_ref[...] = acc.astype(BF16)

    in_specs = [pl.BlockSpec((T, tr), (lambda r, p=p: (0, jnp.clip(r - p * nr, 0, nr - 1)))) for p in range(npart)]
    in_specs.append(_full((T, D)))
    args = [*a_parts, b]
    if extra is not None:
        a2, b2 = extra
        in_specs += [pl.BlockSpec((a2.shape[0], tr), lambda r: (0, r)), _full(b2.shape)]
        args += [a2, b2]
    return pl.pallas_call(
        body, name=name, grid=(npart * nr,),
        in_specs=in_specs, out_specs=pl.BlockSpec((tr, D), lambda r: (r, 0)),
        out_shape=jax.ShapeDtypeStruct((npart * R, D), BF16),
        compiler_params=_cp(("parallel",)),
    )(*args)


def _conv_ext(ref, r0, rows, total):
    top = ref[pl.ds(pl.multiple_of(jnp.maximum(r0 - PAD, 0), PAD), PAD), :]
    mid = ref[pl.ds(r0, rows), :]
    bot = ref[pl.ds(pl.multiple_of(jnp.minimum(r0 + rows, total - PAD), PAD), PAD), :]
    top = jnp.where(r0 > 0, top, jnp.zeros_like(top))
    bot = jnp.where(r0 + rows < total, bot, jnp.zeros_like(bot))
    return jnp.concatenate([top, mid, bot], axis=0).astype(F32)


def _shift_rows(a, k):
    return pltpu.roll(a, k % a.shape[0], 0)


def _conv3(x, w, b):
    return w[0:1] * _shift_rows(x, 1) + w[1:2] * x + w[2:3] * _shift_rows(x, -1) + b


def _gate_up_specs(rows_, wblk, nb):
    return [pl.BlockSpec((rows_, wblk), lambda j: (0, j)), pl.BlockSpec((rows_, wblk), lambda j: (0, j + nb))]


def conv_fwd(hu, cw, cb, *, rows, wblk, name):
    L, N2 = hu.shape
    nb = N2 // 2 // wblk
    nchunk = L // rows

    def body(hg_ref, hu_ref, wg_ref, wu_ref, bg_ref, bu_ref, a_ref, s1_ref, s2_ref):
        def chunk(ci, carry):
            r0 = pl.multiple_of(ci * rows, rows)
            gate = _conv3(_conv_ext(hg_ref, r0, rows, L), wg_ref[...], bg_ref[...])[PAD:PAD + rows]
            up = _conv3(_conv_ext(hu_ref, r0, rows, L), wu_ref[...], bu_ref[...])[PAD:PAD + rows]
            sg = jax.nn.sigmoid(gate)
            silu = gate * sg
            at = pl.ds(r0, rows)
            a_ref[at, :] = (silu * up).astype(BF16)
            s1_ref[at, :] = silu.astype(BF16)
            s2_ref[at, :] = (up * (sg + silu * (1.0 - sg))).astype(BF16)
            return carry

        lax.fori_loop(0, nchunk, chunk, 0)

    out = pl.BlockSpec((L, wblk), lambda j: (0, j))
    return pl.pallas_call(
        body, name=name, grid=(nb,),
        in_specs=_gate_up_specs(L, wblk, nb) + _gate_up_specs(3, wblk, nb) + _gate_up_specs(1, wblk, nb),
        out_specs=[out] * 3, out_shape=[jax.ShapeDtypeStruct((L, N2 // 2), BF16)] * 3,
        compiler_params=_cp(("parallel",)),
    )(hu, hu, cw, cw, cb, cb)


def conv_bwd(da, s1, s2, hu, cw, *, rows, wblk, name):
    L, N2 = hu.shape
    F = N2 // 2
    nb = F // wblk
    nchunk = L // rows
    mid = slice(PAD, PAD + rows)

    def body(da_ref, s1_ref, s2_ref, hg_ref, hu_ref, wg_ref, wu_ref, dg_ref, du_ref, dwg_ref, dwu_ref, dbg_ref, dbu_ref):
        for ref in (dwg_ref, dwu_ref, dbg_ref, dbu_ref):
            ref[...] = jnp.zeros_like(ref)

        def half_bwd(x_ref, dh, w_ref, dx_ref, dw_ref, db_ref, r0):
            w = w_ref[...]
            nxt, prv = _shift_rows(dh, -1)[mid], _shift_rows(dh, 1)[mid]
            dhm, xm = dh[mid], x_ref[pl.ds(r0, rows), :].astype(F32)
            dx_ref[pl.ds(r0, rows), :] = (w[0:1] * nxt + w[1:2] * dhm + w[2:3] * prv).astype(BF16)
            db_ref[...] += _colsum(dhm)
            dw_ref[0:1, :] += _colsum(nxt * xm)
            dw_ref[1:2, :] += _colsum(dhm * xm)
            dw_ref[2:3, :] += _colsum(prv * xm)

        def chunk(ci, carry):
            r0 = pl.multiple_of(ci * rows, rows)
            d = _conv_ext(da_ref, r0, rows, L)
            half_bwd(hu_ref, d * _conv_ext(s1_ref, r0, rows, L), wu_ref, du_ref, dwu_ref, dbu_ref, r0)
            half_bwd(hg_ref, d * _conv_ext(s2_ref, r0, rows, L), wg_ref, dg_ref, dwg_ref, dbg_ref, r0)
            return carry

        lax.fori_loop(0, nchunk, chunk, 0)

    blk = lambda r: pl.BlockSpec((r, wblk), lambda j: (0, j))
    return pl.pallas_call(
        body, name=name, grid=(nb,),
        in_specs=[blk(L)] * 3 + _gate_up_specs(L, wblk, nb) + _gate_up_specs(3, wblk, nb),
        out_specs=[blk(L), blk(L), blk(3), blk(3), blk(1), blk(1)],
        out_shape=[jax.ShapeDtypeStruct((L, F), BF16)] * 2 + [jax.ShapeDtypeStruct((3, F), F32)] * 2
        + [jax.ShapeDtypeStruct((1, F), F32)] * 2,
        compiler_params=_cp(("parallel",)),
    )(da, s1, s2, hu, hu, cw, cw)


def _window_sums(pad_ref, w, lead):
    a = pad_ref[...]
    k = 1
    while k < w:
        a = a + _shift_rows(a, -k)
        k *= 2
    return _shift_rows(a, lead) if lead else a


def _pool_counts(L, h):
    t = lax.broadcasted_iota(jnp.int32, (L, 1), 0)
    return (jnp.minimum(t + h, L) - jnp.maximum(t - h, 0)).astype(F32)


def _pooled(u_ref, pad_ref, L, w):
    h = w // 2
    pad_ref[pl.ds(PAD, L), :] = u_ref[...]
    win = _window_sums(pad_ref, w, h)[PAD:PAD + L]
    return win / _pool_counts(L, h) - u_ref[...]


def _zero_pad_edges(pad_ref, L):
    z = jnp.zeros((PAD, LANES), F32)
    pad_ref[pl.ds(0, PAD), :] = z
    pad_ref[pl.ds(PAD + L, PAD), :] = z


def pool_fwd(u, w_pool, pool_scale, *, name):
    L = u.shape[0]

    def body(u_ref, w_ref, ps_ref, p_ref, pad_ref):
        _zero_pad_edges(pad_ref, L)
        for gi, win in enumerate(POOL_WINDOWS):
            @pl.when(pl.program_id(0) == gi)
            def _():
                pooled = _pooled(u_ref, pad_ref, L, win)
                p_ref[...] = (_dot(pooled.astype(BF16), w_ref[...].astype(BF16)) * ps_ref[...]).astype(BF16)

    return pl.pallas_call(
        body, name=name, grid=(len(POOL_WINDOWS),),
        in_specs=[pl.BlockSpec((L, LANES), lambda gi: (0, gi)), pl.BlockSpec((None, LANES, LANES), lambda gi: (gi, 0, 0)),
                  pl.BlockSpec((1, LANES), lambda gi: (0, gi))],
        out_specs=pl.BlockSpec((L, LANES), lambda gi: (0, gi)),
        out_shape=jax.ShapeDtypeStruct((L, 4 * LANES), BF16),
        scratch_shapes=[pltpu.VMEM((L + 2 * PAD, LANES), F32)],
        compiler_params=_cp(("parallel",)),
    )(u, w_pool, pool_scale)


def pool_bwd(u, dpa, w_pool, pool_scale, *, name):
    L = u.shape[0]

    def body(u_ref, dp_ref, w_ref, ps_ref, du_ref, dw_ref, dps_ref, pad_ref):
        _zero_pad_edges(pad_ref, L)
        for gi, win in enumerate(POOL_WINDOWS):
            @pl.when(pl.program_id(0) == gi)
            def _():
                h = win // 2
                wb = w_ref[...].astype(BF16)
                pooled = _pooled(u_ref, pad_ref, L, win).astype(BF16)
                dp = dp_ref[...].astype(F32)
                dps_ref[...] = _colsum(dp * _dot(pooled, wb))
                dy = (dp * ps_ref[...]).astype(BF16)
                dw_ref[...] = _dot_tn(pooled, dy)
                dpooled = _dot_nt(dy, wb)
                pad_ref[pl.ds(PAD, L), :] = dpooled / _pool_counts(L, h)
                du_ref[...] = (_window_sums(pad_ref, win, h - 1)[PAD:PAD + L] - dpooled).astype(BF16)

    return pl.pallas_call(
        body, name=name, grid=(len(POOL_WINDOWS),),
        in_specs=[pl.BlockSpec((L, LANES), lambda gi: (0, gi)), pl.BlockSpec((L, LANES), lambda gi: (0, gi)),
                  pl.BlockSpec((None, LANES, LANES), lambda gi: (gi, 0, 0)), pl.BlockSpec((1, LANES), lambda gi: (0, gi))],
        out_specs=[pl.BlockSpec((L, LANES), lambda gi: (0, gi)), pl.BlockSpec((None, LANES, LANES), lambda gi: (gi, 0, 0)),
                   pl.BlockSpec((1, LANES), lambda gi: (0, gi))],
        out_shape=[jax.ShapeDtypeStruct((L, 4 * LANES), BF16), jax.ShapeDtypeStruct((4, LANES, LANES), F32),
                   jax.ShapeDtypeStruct((1, 4 * LANES), F32)],
        scratch_shapes=[pltpu.VMEM((L + 2 * PAD, LANES), F32)],
        compiler_params=_cp(("parallel",)),
    )(u, dpa, w_pool, pool_scale)


def _attn_probs(qk, band_k, ctx_k, sink_ref, kh, mask4):
    s_loc = jnp.where(mask4, _dot_nt(qk, band_k), NEG_INF)
    s_ctx = _dot_nt(qk, ctx_k)
    sk = jnp.concatenate([jnp.full((BLK, 1), sink_ref[kh * GQA + hh], F32) for hh in range(GQA)], axis=0)
    m = jnp.maximum(jnp.maximum(jnp.max(s_loc, axis=-1, keepdims=True), jnp.max(s_ctx, axis=-1, keepdims=True)), sk)
    e_loc, e_ctx, e_s = jnp.exp(s_loc - m), jnp.exp(s_ctx - m), jnp.exp(sk - m)
    inv = 1.0 / (jnp.sum(e_loc, axis=-1, keepdims=True) + jnp.sum(e_ctx, axis=-1, keepdims=True) + e_s)
    return e_loc * inv, e_ctx * inv, e_s * inv


def _attn_block(n, L):
    start = pl.multiple_of(jnp.clip((n - 1) * BLK, 0, L - 3 * BLK), BLK)
    qpos = n * BLK + lax.broadcasted_iota(jnp.int32, (BLK, 3 * BLK), 0)
    kpos = start + lax.broadcasted_iota(jnp.int32, (BLK, 3 * BLK), 1)
    mask = jnp.abs(kpos - qpos) <= WINDOW
    return start, jnp.concatenate([mask] * GQA, axis=0)


def _stack_slabs(ref):
    return jnp.concatenate([ref[:, s * LANES:(s + 1) * LANES] for s in range(GQA)], axis=0)


def _kv_head_lanes(kh):
    return (lax.broadcasted_iota(jnp.int32, (1, LANES), 1) // HEAD_DIM) == kh


def permute_heads(w, inverse=False):
    lo, hi = 4 * LANES, 8 * LANES
    mid = w[lo:hi].reshape(*((GQA, N_KV_HEADS) if inverse else (N_KV_HEADS, GQA)), HEAD_DIM, w.shape[1])
    return jnp.concatenate([w[:lo], mid.swapaxes(0, 1).reshape(hi - lo, w.shape[1]), w[hi:]], axis=0)


def attn_fwd(q, kv, kvc, sink, *, name):
    L = q.shape[0]
    C = kvc.shape[0]
    scale = HEAD_DIM ** -0.5

    def body(q_ref, kv_ref, kvc_ref, sink_ref, o_ref):
        start, mask4 = _attn_block(pl.program_id(0), L)
        band = kv_ref[pl.ds(start, 3 * BLK), :]
        kvc_ = kvc_ref[...]
        qs = _stack_slabs(q_ref) * scale
        o = jnp.zeros((GQA * BLK, LANES), F32)
        for kh in range(N_KV_HEADS):
            grp = _kv_head_lanes(kh)
            qk = jnp.where(grp, qs, jnp.zeros_like(qs))
            p_loc, p_ctx, _ = _attn_probs(qk, band[:, :LANES], kvc_[:, :LANES], sink_ref, kh, mask4)
            o = o + jnp.where(grp, _dot(p_loc.astype(BF16), band[:, LANES:]) + _dot(p_ctx.astype(BF16), kvc_[:, LANES:]), 0.0)
        for s in range(GQA):
            o_ref[:, s * LANES:(s + 1) * LANES] = o[s * BLK:(s + 1) * BLK].astype(BF16)

    return pl.pallas_call(
        body, name=name, grid=(L // BLK,),
        in_specs=[pl.BlockSpec((BLK, 4 * LANES), lambda n: (n, 0)), _full((L, 2 * LANES)), _full((C, 2 * LANES)),
                  pl.BlockSpec(memory_space=pltpu.SMEM)],
        out_specs=pl.BlockSpec((BLK, 4 * LANES), lambda n: (n, 0)),
        out_shape=jax.ShapeDtypeStruct((L, 4 * LANES), BF16),
        compiler_params=_cp(("parallel",)),
    )(q, kv, kvc, sink)


def attn_bwd(q, kv, kvc, sink, dpa, cos, sa, sb, *, name):
    L = q.shape[0]
    C = kvc.shape[0]
    nb = L // BLK
    scale = HEAD_DIM ** -0.5

    def body(q_ref, kv_ref, kvc_ref, sink_ref, do_ref, c_ref, sa_ref, sb_ref, cq_ref, saq_ref, sbq_ref,
             dq_ref, dkv_ref, dkvc_ref, dsink_ref, dkv_acc, dkvc_acc):
        n = pl.program_id(0)

        @pl.when(n == 0)
        def _():
            dkv_acc[...] = jnp.zeros_like(dkv_acc)
            dkvc_acc[...] = jnp.zeros_like(dkvc_acc)
            dsink_ref[...] = jnp.zeros_like(dsink_ref)

        start, mask4 = _attn_block(n, L)
        band = kv_ref[pl.ds(start, 3 * BLK), :]
        kvc_ = kvc_ref[...]
        band_k, band_v, ctx_k, ctx_v = band[:, :LANES], band[:, LANES:], kvc_[:, :LANES], kvc_[:, LANES:]
        qs = _stack_slabs(q_ref) * scale
        dos = _stack_slabs(do_ref)
        lane = lax.broadcasted_iota(jnp.int32, (1, LANES), 1)
        dsink = jnp.zeros((1, LANES), F32)
        dq = jnp.zeros((GQA * BLK, LANES), F32)
        dk = jnp.zeros((LANES, 3 * BLK), F32)
        dv = jnp.zeros((LANES, 3 * BLK), F32)
        dkc = jnp.zeros((LANES, C), F32)
        dvc = jnp.zeros((LANES, C), F32)
        for kh in range(N_KV_HEADS):
            grp = _kv_head_lanes(kh)
            qk = jnp.where(grp, qs, jnp.zeros_like(qs))
            dok = jnp.where(grp, dos, jnp.zeros_like(dos))
            p_loc, p_ctx, p_s = _attn_probs(qk, band_k, ctx_k, sink_ref, kh, mask4)
            dp_loc = _dot_nt(dok, band_v)
            dp_ctx = _dot_nt(dok, ctx_v)
            delta = jnp.sum(p_loc * dp_loc, axis=-1, keepdims=True) + jnp.sum(p_ctx * dp_ctx, axis=-1, keepdims=True)
            ds_loc = (p_loc * (dp_loc - delta)).astype(BF16)
            ds_ctx = (p_ctx * (dp_ctx - delta)).astype(BF16)
            dsk = p_s * delta
            for hh in range(GQA):
                dsink = dsink - jnp.where(lane == kh * GQA + hh, jnp.sum(dsk[hh * BLK:(hh + 1) * BLK], axis=0, keepdims=True), 0.0)
            dq = dq + jnp.where(grp, _dot(ds_loc, band_k) + _dot(ds_ctx, ctx_k), 0.0)
            dk = dk + _dot_tn(qk, ds_loc)
            dv = dv + _dot_tn(dok, p_loc.astype(BF16))
            dkc = dkc + _dot_tn(qk, ds_ctx)
            dvc = dvc + _dot_tn(dok, p_ctx.astype(BF16))
        dsink_ref[...] += dsink
        dkv_acc[:LANES, pl.ds(start, 3 * BLK)] += dk
        dkv_acc[LANES:, pl.ds(start, 3 * BLK)] += dv
        dkvc_acc[:LANES, :] += dkc
        dkvc_acc[LANES:, :] += dvc
        c, a, b = cq_ref[...], -saq_ref[...], -sbq_ref[...]
        for s in range(GQA):
            dq_ref[:, s * LANES:(s + 1) * LANES] = _rope(dq[s * BLK:(s + 1) * BLK] * scale, c, a, b).astype(BF16)

        @pl.when(n == nb - 1)
        def _():
            dkv_ref[:, :LANES] = _rope(dkv_acc[:LANES, :].T, c_ref[...], -sa_ref[...], -sb_ref[...]).astype(BF16)
            dkv_ref[:, LANES:] = dkv_acc[LANES:, :].T.astype(BF16)
            dkvc_ref[...] = dkvc_acc[...].T.astype(BF16)

    blk = lambda w: pl.BlockSpec((BLK, w), lambda n: (n, 0))
    return pl.pallas_call(
        body, name=name, grid=(nb,),
        in_specs=[blk(4 * LANES), _full((L, 2 * LANES)), _full((C, 2 * LANES)), pl.BlockSpec(memory_space=pltpu.SMEM),
                  pl.BlockSpec((BLK, 4 * LANES), lambda n: (n, 1)),
                  _full((L, LANES)), _full((L, LANES)), _full((L, LANES)), blk(LANES), blk(LANES), blk(LANES)],
        out_specs=[blk(4 * LANES), _full((L, 2 * LANES)), _full((C, 2 * LANES)), _full((1, LANES))],
        out_shape=[jax.ShapeDtypeStruct((L, 4 * LANES), BF16), jax.ShapeDtypeStruct((L, 2 * LANES), BF16),
                   jax.ShapeDtypeStruct((C, 2 * LANES), BF16), jax.ShapeDtypeStruct((1, LANES), F32)],
        scratch_shapes=[pltpu.VMEM((2 * LANES, L), F32), pltpu.VMEM((2 * LANES, C), F32)],
        compiler_params=_cp(("arbitrary",)),
    )(q, kv, kvc, sink, dpa, cos, sa, sb, cos, sa, sb)


def _gelu_parts(x):
    th = jnp.tanh(SQRT_2_OVER_PI * (x + GELU_C * x * x * x))
    return 0.5 * x * (1.0 + th), th


def _gelu_grad(x, th):
    return 0.5 * (1.0 + th) + 0.5 * x * (1.0 - th * th) * SQRT_2_OVER_PI * (1.0 + 3.0 * GELU_C * x * x)


def _layernorm(v):
    mu = jnp.mean(v, axis=-1, keepdims=True)
    vc = v - mu
    rstd = lax.rsqrt(jnp.mean(vc * vc, axis=-1, keepdims=True) + EPS)
    return vc * rstd, rstd


def sgu_fwd(z1, ln_g, ln_b, ws, bst, *, name):
    L, W2 = z1.shape
    W = W2 // 2
    ng = W // LANES

    def body(z_ref, g_ref, b_ref, ws_ref, bs_ref, o_ref):
        z, _ = _gelu_parts(z_ref[...].astype(F32))
        xhat, _ = _layernorm(z[:, W:])
        vln = (xhat * g_ref[...] + b_ref[...]).astype(BF16)
        for gi in range(ng):
            cs = slice(gi * LANES, (gi + 1) * LANES)
            s = _dot(ws_ref[gi], vln[:, cs]) + bs_ref[:, gi:gi + 1]
            o_ref[:, cs] = (z[:, cs] * s).astype(BF16)

    vec = _full((1, W))
    return pl.pallas_call(
        body, name=name, grid=(L // BLK,),
        in_specs=[pl.BlockSpec((BLK, W2), lambda n: (n, 0)), vec, vec, _full((ng, LANES, LANES)), _full((BLK, ng))],
        out_specs=pl.BlockSpec((BLK, W), lambda n: (n, 0)),
        out_shape=jax.ShapeDtypeStruct((L, W), BF16),
        compiler_params=_cp(("parallel",)),
    )(z1, ln_g, ln_b, ws, bst)


def sgu_bwd(z1, dus, ln_g, ln_b, ws, wst, bst, *, name):
    L, W2 = z1.shape
    W = W2 // 2
    ng = W // LANES

    def body(z_ref, d_ref, g_ref, b_ref, ws_ref, wst_ref, bs_ref, dz_ref, dws_ref, dbs_ref, dg_ref, db_ref, dv_scr):
        @pl.when(pl.program_id(0) == 0)
        def _():
            dws_ref[...] = jnp.zeros_like(dws_ref)
            dbs_ref[...] = jnp.zeros_like(dbs_ref)
            dg_ref[...] = jnp.zeros_like(dg_ref)
            db_ref[...] = jnp.zeros_like(db_ref)

        zp = z_ref[...].astype(F32)
        z, th = _gelu_parts(zp)
        xhat, rstd = _layernorm(z[:, W:])
        vln = (xhat * g_ref[...] + b_ref[...]).astype(BF16)
        d = d_ref[...].astype(F32)
        lane = lax.broadcasted_iota(jnp.int32, (1, LANES), 1)
        dbs = jnp.zeros((BLK, LANES), F32)
        dgel = _gelu_grad(zp, th)
        for gi in range(ng):
            cs = slice(gi * LANES, (gi + 1) * LANES)
            s = _dot(ws_ref[gi], vln[:, cs]) + bs_ref[:, gi:gi + 1]
            dz_ref[:, cs] = (d[:, cs] * s * dgel[:, cs]).astype(BF16)
            ds = d[:, cs] * z[:, cs]
            dbs = dbs + jnp.where(lane == gi, jnp.sum(ds, axis=-1, keepdims=True), 0.0)
            dsb = ds.astype(BF16)
            dws_ref[gi] += _dot_nt(dsb, vln[:, cs])
            dv_scr[:, cs] = _dot(wst_ref[gi], dsb)
        dbs_ref[...] += dbs
        dvln = dv_scr[...]
        dg_ref[...] += _colsum(dvln * xhat)
        db_ref[...] += _colsum(dvln)
        dxh = dvln * g_ref[...]
        dv = rstd * (dxh - jnp.mean(dxh, axis=-1, keepdims=True) - xhat * jnp.mean(dxh * xhat, axis=-1, keepdims=True))
        dz_ref[:, W:] = (dv * dgel[:, W:]).astype(BF16)

    vec = _full((1, W))
    return pl.pallas_call(
        body, name=name, grid=(L // BLK,),
        in_specs=[pl.BlockSpec((BLK, W2), lambda n: (n, 0)), pl.BlockSpec((BLK, W), lambda n: (n, 0)), vec, vec,
                  _full((ng, LANES, LANES)), _full((ng, LANES, LANES)), _full((BLK, ng))],
        out_specs=[pl.BlockSpec((BLK, W2), lambda n: (n, 0)), _full((ng, LANES, LANES)), _full((BLK, LANES)), vec, vec],
        out_shape=[jax.ShapeDtypeStruct((L, W2), BF16), jax.ShapeDtypeStruct((ng, LANES, LANES), F32),
                   jax.ShapeDtypeStruct((BLK, LANES), F32), jax.ShapeDtypeStruct((1, W), F32), jax.ShapeDtypeStruct((1, W), F32)],
        scratch_shapes=[pltpu.VMEM((BLK, W), F32)],
        compiler_params=_cp(("arbitrary",)),
    )(z1, dus, ln_g, ln_b, ws, wst, bst)


def _adamw_math(w, m, v, g):
    m_ = ADAM_B1 * m + (1.0 - ADAM_B1) * g
    v_ = ADAM_B2 * v + (1.0 - ADAM_B2) * (g * g)
    return -ADAM_LR * ((m_ / BC1) / (jnp.sqrt(v_ / BC2) + ADAM_EPS) + ADAM_WD * w), m_, v_


def adamw(w, m, v, gparts, *, tr, name):
    NL, R, Wd = w.shape
    nr = R // tr

    def body(w_ref, m_ref, v_ref, *rest):
        gp_refs, (g_ref, d_ref, nm_ref, nv_ref) = rest[:NL], rest[NL:]
        for l in range(NL):
            @pl.when(pl.program_id(0) == l)
            def _():
                g = gp_refs[l][0].astype(F32)
                for s in range(1, gp_refs[l].shape[0]):
                    g = g + gp_refs[l][s].astype(F32)
                g_ref[...] = g
                d_ref[...], nm_ref[...], nv_ref[...] = _adamw_math(w_ref[...], m_ref[...], v_ref[...], g)

    row = pl.BlockSpec((None, tr, Wd), lambda l, i: (l, i, 0))
    gspecs = [pl.BlockSpec((gparts[l].shape[0], tr, Wd), (lambda l_, i, l=l: (0, jnp.clip(i + (l_ - l) * nr, 0, nr - 1), 0)))
              for l in range(NL)]
    return pl.pallas_call(
        body, name=name, grid=(NL, nr),
        in_specs=[row, row, row] + gspecs, out_specs=[row] * 4, out_shape=[jax.ShapeDtypeStruct((NL, R, Wd), F32)] * 4,
        compiler_params=_cp(("arbitrary", "arbitrary")),
    )(w, m, v, *gparts)


def small_update(gpacks, me, params, loss_row, *, name):
    n = len(params)

    def body(me_ref, gp_ref, *refs):
        ins, outs, gs_ref = refs[:3 * n], refs[3 * n:-1], refs[-1]
        gs_ref[...] = gp_ref[0].astype(F32)
        for dv in range(1, N_DEV):
            gs_ref[...] += gp_ref[dv].astype(F32)
        for p, (w, _, _, off, per_dev) in enumerate(params):
            w_ref, m_ref, v_ref = ins[3 * p:3 * p + 3]
            g_ref, d_ref, nm_ref, nv_ref = outs[4 * p:4 * p + 4]
            rows, cols = w.shape
            if cols == LANES and rows % 8 == 0 and not per_dev:
                g = gs_ref[off:off + rows, :]
                g_ref[...] = g
                d_ref[...], nm_ref[...], nv_ref[...] = _adamw_math(w_ref[...], m_ref[...], v_ref[...], g)
                continue
            chunks = -(-cols // LANES)
            base = off + me_ref[0] * per_dev if per_dev else off
            for i in range(rows):
                for j in range(chunks):
                    wd = min(LANES, cols - j * LANES)
                    at = (slice(i, i + 1), slice(j * LANES, j * LANES + wd))
                    g = gs_ref[pl.ds(base + i * chunks + j, 1), 0:wd]
                    g_ref[at] = g
                    d_ref[at], nm_ref[at], nv_ref[at] = _adamw_math(w_ref[at], m_ref[at], v_ref[at], g)
        outs[-1][...] = jnp.sum(gs_ref[loss_row:loss_row + 1, :], axis=1, keepdims=True)

    flat = [a for w, m, v, _, _ in params for a in (w, m, v)]
    out_shape = [jax.ShapeDtypeStruct(w.shape, F32) for w, _, _, _, _ in params for _ in range(4)] + [jax.ShapeDtypeStruct((1, 1), F32)]
    return pl.pallas_call(
        body, name=name, grid=(1,),
        in_specs=[pl.BlockSpec(memory_space=pltpu.SMEM), _full(gpacks.shape)] + [_full(a.shape) for a in flat],
        out_specs=[_full(o.shape) for o in out_shape], out_shape=out_shape,
        scratch_shapes=[pltpu.VMEM(gpacks.shape[1:], F32)],
        compiler_params=_cp(("arbitrary",)),
    )(me, gpacks, *flat)


def ada_fwd_mm(cs, w_ada, b_loc, *, name):
    R, D = cs.shape
    nl, _, n = w_ada.shape

    def body(c_ref, w_ref, b_ref, s_ref, m_ref):
        c = c_ref[...]
        s = c * jax.nn.sigmoid(c)
        s_ref[...] = s
        for i in range(nl):
            m_ref[i] = _dot(s.astype(BF16), w_ref[i].astype(BF16)) + b_ref[i:i + 1, :]

    return pl.pallas_call(
        body, name=name, in_specs=[_full((R, D)), _full((nl, D, n)), _full((nl, n))],
        out_specs=[_full((R, D)), _full((nl, R, n))], grid=(1,),
        out_shape=[jax.ShapeDtypeStruct((R, D), F32), jax.ShapeDtypeStruct((nl, R, n), F32)],
        compiler_params=_cp(("arbitrary",)),
    )(cs, w_ada, b_loc)


def ada_bwd_mm(s, c_ctx, dall, w_ada, *, name):
    R, D = s.shape
    nl, _, n = w_ada.shape

    def body(s_ref, cc_ref, d_ref, w_ref, gw_ref, dcc_ref):
        sb = s_ref[...].astype(BF16)
        row = lax.broadcasted_iota(jnp.int32, (R, 1), 0)
        dctx = d_ref[0, 1:2, :]
        for dv in range(1, N_DEV):
            dctx = dctx + d_ref[dv, 1:2, :]
        for i in range(nl):
            dm = jnp.zeros((R, n), F32)
            for dv in range(N_DEV):
                dm = dm + jnp.where(row == dv, d_ref[dv, 2 * i:2 * i + 1, :], 0.0)
            if i == 0:
                dm = dm + jnp.where(row == N_DEV, dctx, 0.0)
            gw_ref[i] = _dot_tn(sb, dm.astype(BF16))
        cc = cc_ref[...]
        sg = jax.nn.sigmoid(cc)
        ds = _dot_nt(jnp.broadcast_to(dctx, (8, n)).astype(BF16), w_ref[0].astype(BF16))
        dcc_ref[...] = ds * (sg * (1.0 + cc * (1.0 - sg)))

    return pl.pallas_call(
        body, name=name, grid=(1,),
        in_specs=[_full((R, D)), _full((1, D)), _full((N_DEV, 3, n)), _full((nl, D, n))],
        out_specs=[_full((nl, D, n)), _full((8, D))],
        out_shape=[jax.ShapeDtypeStruct((nl, D, n), F32), jax.ShapeDtypeStruct((8, D), F32)],
        compiler_params=_cp(("arbitrary",)),
    )(s, c_ctx, dall, w_ada)


def _place():
    x, y, c = lax.axis_index("x"), lax.axis_index("y"), lax.axis_index("c")
    return x, y, c


def _lin(p):
    return 4 * p[0] + 2 * p[1] + p[2]


def all_gather_small(xb, *, name):
    R, W = xb.shape

    def body(x_ref, out_ref, send_sems, recv_sems, local_sem):
        x, y, c = _place()
        me = _lin((x, y, c))
        mine = pltpu.make_async_copy(x_ref, out_ref.at[me], local_sem)
        mine.start()
        copies = []
        for k in range(1, N_DEV):
            peer = (x ^ (k >> 2), y ^ ((k >> 1) & 1), c ^ (k & 1))
            mk = lambda dst, k=k, peer=peer: pltpu.make_async_remote_copy(
                src_ref=x_ref, dst_ref=dst, send_sem=send_sems.at[k - 1], recv_sem=recv_sems.at[k - 1], device_id=peer, device_id_type=MESH)
            mk(out_ref.at[me]).start()
            copies.append(mk(out_ref.at[_lin(peer)]))
        for cp in copies:
            cp.wait_recv()
        for cp in copies:
            cp.wait_send()
        mine.wait()

    vm = pl.BlockSpec(memory_space=pltpu.VMEM)
    return pl.pallas_call(
        body, name=name, in_specs=[vm], out_specs=vm, out_shape=jax.ShapeDtypeStruct((N_DEV, R, W), xb.dtype),
        scratch_shapes=[pltpu.SemaphoreType.DMA((7,)), pltpu.SemaphoreType.DMA((7,)), pltpu.SemaphoreType.DMA],
        compiler_params=pltpu.CompilerParams(vmem_limit_bytes=VMEM_LIMIT),
    )(xb)


HBM_SPEC = pl.BlockSpec(memory_space=pltpu.HBM)
SEM_SPEC = pl.BlockSpec(memory_space=pltpu.SEMAPHORE)
ORDERED_EFFECT = pltpu.SideEffectType.DATAFLOW_SIDE_EFFECTING


def _exchange_copies(srcs, lands, sems, scatter):
    x, y, c = _place()
    me = _lin((x, y, c))
    for j in range(len(srcs)):
        r = lands[j].shape[0] // N_DEV
        block = lambda d, j=j, r=r: pl.ds(pl.multiple_of(d * r, 16), r)
        for k in range(1, N_DEV):
            peer = (x ^ (k >> 2), y ^ ((k >> 1) & 1), c ^ (k & 1))
            src = srcs[j].at[block(_lin(peer)), :] if scatter else srcs[j]
            mk = lambda dst, j=j, k=k, peer=peer, src=src: pltpu.make_async_remote_copy(
                src_ref=src, dst_ref=dst, send_sem=sems[2 * j].at[k - 1], recv_sem=sems[2 * j + 1].at[k - 1],
                device_id=peer, device_id_type=MESH)
            yield mk(lands[j].at[block(me), :]), mk(lands[j].at[block(_lin(peer)), :])


def exchange_start(srcs, lands, *, scatter, name):
    nw = len(srcs)

    def body(*refs):
        for start, _ in _exchange_copies(refs[:nw], refs[nw:2 * nw], refs[2 * nw:4 * nw], scatter):
            start.start()
        refs[-1][...] = jnp.zeros_like(refs[-1])

    thru = [pltpu.HBM(a.shape, a.dtype) for a in (*srcs, *lands)]
    res = pl.pallas_call(
        body, name=name, in_specs=[HBM_SPEC] * (2 * nw),
        out_specs=[SEM_SPEC] * (2 * nw) + [HBM_SPEC] * (2 * nw) + [pl.BlockSpec(memory_space=pltpu.VMEM)],
        out_shape=[pltpu.SemaphoreType.DMA((N_DEV - 1,))] * (2 * nw) + thru + [jax.ShapeDtypeStruct((8, LANES), F32)],
        input_output_aliases={i: 2 * nw + i for i in range(2 * nw)},
        compiler_params=pltpu.CompilerParams(has_side_effects=ORDERED_EFFECT),
    )(*[pltpu.with_memory_space_constraint(a, pltpu.HBM) for a in (*srcs, *lands)])
    return res[:2 * nw], res[2 * nw:3 * nw], res[3 * nw:4 * nw], res[-1]


def exchange_wait(srcs, lands, sems, after, *, scatter, name):
    nw = len(srcs)
    after = list(after) if isinstance(after, (list, tuple)) else [after]

    def body(*refs):
        for _, arrive in _exchange_copies(refs[:nw], refs[nw:2 * nw], refs[2 * nw:4 * nw], scatter):
            arrive.wait_send()
            arrive.wait_recv()

    res = pl.pallas_call(
        body, name=name, in_specs=[HBM_SPEC] * (2 * nw) + [SEM_SPEC] * (2 * nw) + [pl.BlockSpec(memory_space=pl.ANY)] * len(after),
        out_specs=[HBM_SPEC] * (2 * nw), out_shape=[pltpu.HBM(a.shape, a.dtype) for a in (*srcs, *lands)],
        input_output_aliases={i: i for i in range(2 * nw)},
        compiler_params=pltpu.CompilerParams(has_side_effects=ORDERED_EFFECT),
    )(*srcs, *lands, *sems, *after)
    return res[nw:]


def place_own(srcs, rows, me, *, scatter, name):
    nw = len(srcs)
    lands = [lax.empty((N_DEV * r, s_.shape[1]), s_.dtype) for r, s_ in zip(rows, srcs)]

    def body(me_ref, *refs):
        for j in range(nw):
            refs[2 * nw + j][...] = refs[j][...]

    mine = lambda i, me_ref: (me_ref[0], 0)
    src_at = mine if scatter else (lambda i, me_ref: (0, 0))
    blocks = [(r, s_.shape[1]) for r, s_ in zip(rows, srcs)]
    return pl.pallas_call(
        body, name=name,
        grid_spec=pltpu.PrefetchScalarGridSpec(
            num_scalar_prefetch=1, grid=(1,),
            in_specs=[pl.BlockSpec(b_, src_at) for b_ in blocks] + [pl.BlockSpec(memory_space=pl.ANY)] * nw,
            out_specs=[pl.BlockSpec(b_, mine) for b_ in blocks]),
        out_shape=[jax.ShapeDtypeStruct(l_.shape, l_.dtype) for l_ in lands],
        input_output_aliases={1 + nw + j: j for j in range(nw)},
        compiler_params=_cp(("arbitrary",)),
    )(jnp.reshape(me, (1,)).astype(jnp.int32), *srcs, *lands)


def _rope_tables(L):
    t = jnp.arange(L)
    inv = ROPE_BASE ** (-jnp.arange(ROPE_FREQS, dtype=F32) / ROPE_FREQS)
    ar = (t // GRID_W).astype(F32)[:, None] * inv
    ac = (t % GRID_W).astype(F32)[:, None] * inv
    z = jnp.zeros_like(ar)
    cos = jnp.concatenate([jnp.cos(ar), jnp.cos(ar), jnp.cos(ac), jnp.cos(ac)], axis=1)
    sa = jnp.concatenate([-jnp.sin(ar), z, -jnp.sin(ac), z], axis=1)
    sb = jnp.concatenate([z, jnp.sin(ar), z, jnp.sin(ac)], axis=1)
    return tuple(jnp.tile(a, (1, LANES // HEAD_DIM)) for a in (cos, sa, sb))


def _nat2d(a):
    return a.reshape(1, -1) if a.ndim == 1 else a.reshape(-1, a.shape[-1])


def _pack_rows(a):
    rows, cols = a.shape
    chunks = -(-cols // LANES)
    f = jnp.pad(a, ((0, 0), (0, chunks * LANES - cols))).reshape(rows * chunks, LANES)
    return jnp.pad(f, ((0, -f.shape[0] % 8), (0, 0)))


def _rows128(a):
    f = a.reshape(-1)
    n = -(-f.shape[0] // (8 * LANES)) * 8 * LANES
    return jnp.pad(f, (0, n - f.shape[0])).reshape(-1, LANES)


def kernel(x, c, ctx, c_ctx, w_ada, b_ada, g_mix_pre, g_mix_post, g_ffn_pre, g_ffn_post, w_in_even, w_pool, pool_scale, attn_sink, w_out_even, w_in_odd, sgu_ln_g, sgu_ln_b, sgu_w, sgu_b, w_out_odd, w_ffn_up, ffn_conv_w, ffn_conv_b, w_ffn_down, loss_target, m_c_ctx, m_w_ada, m_b_ada, m_g_mix_pre, m_g_mix_post, m_g_ffn_pre, m_g_ffn_post, m_w_in_even, m_w_pool, m_pool_scale, m_attn_sink, m_w_out_even, m_w_in_odd, m_sgu_ln_g, m_sgu_ln_b, m_sgu_w, m_sgu_b, m_w_out_odd, m_w_ffn_up, m_ffn_conv_w, m_ffn_conv_b, m_w_ffn_down, v_c_ctx, v_w_ada, v_b_ada, v_g_mix_pre, v_g_mix_post, v_g_ffn_pre, v_g_ffn_post, v_w_in_even, v_w_pool, v_pool_scale, v_attn_sink, v_w_out_even, v_w_in_odd, v_sgu_ln_g, v_sgu_ln_b, v_sgu_w, v_sgu_b, v_w_out_odd, v_w_ffn_up, v_ffn_conv_w, v_ffn_conv_b, v_w_ffn_down):
    P = dict(c_ctx=c_ctx, w_ada=w_ada, b_ada=b_ada, g_mix_pre=g_mix_pre, g_mix_post=g_mix_post, g_ffn_pre=g_ffn_pre,
             g_ffn_post=g_ffn_post, w_in_even=w_in_even, w_pool=w_pool, pool_scale=pool_scale, attn_sink=attn_sink,
             w_out_even=w_out_even, w_in_odd=w_in_odd, sgu_ln_g=sgu_ln_g, sgu_ln_b=sgu_ln_b, sgu_w=sgu_w, sgu_b=sgu_b,
             w_out_odd=w_out_odd, w_ffn_up=w_ffn_up, ffn_conv_w=ffn_conv_w, ffn_conv_b=ffn_conv_b, w_ffn_down=w_ffn_down)
    M = dict(c_ctx=m_c_ctx, w_ada=m_w_ada, b_ada=m_b_ada, g_mix_pre=m_g_mix_pre, g_mix_post=m_g_mix_post, g_ffn_pre=m_g_ffn_pre,
             g_ffn_post=m_g_ffn_post, w_in_even=m_w_in_even, w_pool=m_w_pool, pool_scale=m_pool_scale, attn_sink=m_attn_sink,
             w_out_even=m_w_out_even, w_in_odd=m_w_in_odd, sgu_ln_g=m_sgu_ln_g, sgu_ln_b=m_sgu_ln_b, sgu_w=m_sgu_w, sgu_b=m_sgu_b,
             w_out_odd=m_w_out_odd, w_ffn_up=m_w_ffn_up, ffn_conv_w=m_ffn_conv_w, ffn_conv_b=m_ffn_conv_b, w_ffn_down=m_w_ffn_down)
    V = dict(c_ctx=v_c_ctx, w_ada=v_w_ada, b_ada=v_b_ada, g_mix_pre=v_g_mix_pre, g_mix_post=v_g_mix_post, g_ffn_pre=v_g_ffn_pre,
             g_ffn_post=v_g_ffn_post, w_in_even=v_w_in_even, w_pool=v_w_pool, pool_scale=v_pool_scale, attn_sink=v_attn_sink,
             w_out_even=v_w_out_even, w_in_odd=v_w_in_odd, sgu_ln_g=v_sgu_ln_g, sgu_ln_b=v_sgu_ln_b, sgu_w=v_sgu_w, sgu_b=v_sgu_b,
             w_out_odd=v_w_out_odd, w_ffn_up=v_w_ffn_up, ffn_conv_w=v_ffn_conv_w, ffn_conv_b=v_ffn_conv_b, w_ffn_down=v_w_ffn_down)

    x = x[0]
    ctx = ctx[0]
    target = loss_target[0]
    L, D = x.shape
    C = ctx.shape[0]
    tm = min(512, L)
    tm_up = min(1024, L)
    conv_rows = min(512, L)
    me = 4 * lax.axis_index("x") + 2 * lax.axis_index("y") + lax.axis_index("c")
    n_ada = w_ada.shape[2]
    F = w_ffn_down.shape[1] * N_DEV
    half_f = F // 2

    n_cw = ffn_conv_w.shape[2]
    small = jnp.concatenate([_rows128(c), _rows128(sgu_ln_g), _rows128(sgu_ln_b), _rows128(ffn_conv_w)], axis=0)
    small_all = all_gather_small(small, name="gather_small_inputs")
    c_all = small_all[:, :8].reshape(N_DEV, D)
    ln_g = small_all[:, 8].reshape(1, D)
    ln_b = small_all[:, 16].reshape(1, D)
    conv_w = small_all[:, 24:].reshape(N_DEV, -1)[:, :2 * 3 * n_cw].reshape(N_DEV, 2, 3, n_cw)
    conv_w = conv_w.transpose(1, 2, 0, 3).reshape(2, 3, 2 * F)

    cs = jnp.concatenate([c_all, c_ctx[None, :], jnp.zeros((7, D), F32)], axis=0)
    b_loc = lax.dynamic_slice(b_ada, (0, me * n_ada), (2, n_ada))
    silu_c, mods_loc = ada_fwd_mm(cs, w_ada, b_loc, name="ada_fwd")
    mods_all = all_gather_small(mods_loc.reshape(-1, LANES), name="gather_mods")

    shards = [s.astype(BF16) for s in (w_in_even[0].T, w_out_even[0], w_ffn_up[0].T, w_ffn_down[0],
                                       w_in_odd[0].T, w_out_odd[0], w_ffn_up[1].T, w_ffn_down[1])]
    shards, mods_all = lax.optimization_barrier((shards, mods_all))
    w_sems, w_srcs, w_lands, _ = exchange_start(shards, place_own(shards, [s.shape[0] for s in shards], me, scatter=False, name="gather_own"),
                                              scatter=False, name="gather_start")

    def weight(j, after):
        return exchange_wait([w_srcs[j]], [w_lands[j]], w_sems[2 * j:2 * j + 2], after, scatter=False, name=f"gather_wait_{j}")[0]

    mods_all = mods_all.reshape(N_DEV, 2, 16, n_ada).transpose(1, 2, 0, 3).reshape(2, 16, 6 * D)
    mod = lambda i, row: [m_[None, :] for m_ in jnp.split(lax.dynamic_index_in_dim(mods_all[i], row, 0, False), 6)]
    sh_m, sc_m, gt_m, sh_f, sc_f, gt_f = zip(mod(0, me), mod(1, me))
    csh_m, csc_m = mod(0, N_DEV)[:2]

    row = lambda a, i: a[i][None, :]

    cos, sa, sb = _rope_tables(L)
    sink = attn_sink[0]
    bst = sgu_b[0].T
    sgu_wb, sgu_wtb = sgu_w[0].astype(BF16), sgu_w[0].swapaxes(1, 2).astype(BF16)
    wup, wdn = [None, None], [None, None]

    def ffn_fwd(i, xin):
        wup[i] = weight(2 + 4 * i, xin)
        h, hu = pre_mm(xin, row(g_ffn_pre, i), sh_f[i], sc_f[i], wup[i], tm=tm_up, tn=half_f, name=f"ffn_up_{i}")
        a, s1, s2 = conv_fwd(hu, conv_w[i], ffn_conv_b[i][None, :], rows=conv_rows, wblk=2 * LANES, name=f"ffn_conv_{i}")
        wdn[i] = weight(3 + 4 * i, a)
        res = mm_post([a], wdn[i], xin, row(g_ffn_post, i), gt_f[i], tm=tm, target=target if i == 1 else None, name=f"ffn_down_{i}")
        return (h, (hu, s1, s2), a, *res)

    first_mod, cos, sa, sb = lax.optimization_barrier((sh_m[0], cos, sa, sb))
    win_e = permute_heads(weight(0, first_mod))
    h0, u, q, kv = inproj_even(x, row(g_mix_pre, 0), sh_m[0], sc_m[0], win_e, cos, sa, sb, tm=tm, name="in_even")
    hc, kvc = pre_mm(ctx, row(g_mix_pre, 0), csh_m, csc_m, win_e, tm=C, tn=2 * LANES, w_row_off=8 * LANES, name="in_even_ctx")
    pa = [pool_fwd(u, w_pool[0], pool_scale, name="pool_fwd"), attn_fwd(q, kv, kvc, sink, name="attn_fwd")]
    wout_e = permute_heads(weight(1, pa[1]))
    y0, x1 = mm_post(pa, wout_e, x, row(g_mix_post, 0), gt_m[0], tm=tm, name="out_even")
    h1, hu0, a0, f0, x2 = ffn_fwd(0, x1)
    win_o = weight(4, x2)
    h2, z1 = pre_mm(x2, row(g_mix_pre, 1), sh_m[1], sc_m[1], win_o, tm=tm_up, tn=D, name="in_odd")
    us = sgu_fwd(z1, ln_g, ln_b, sgu_wb, bst, name="sgu_fwd")
    wout_o = weight(5, us)
    y1, x3 = mm_post([us], wout_o, x2, row(g_mix_post, 1), gt_m[1], tm=tm, name="out_odd")
    h3, hu1, a1, f1, dx4, loss_part = ffn_fwd(1, x3)

    g_srcs, g_lands, g_sems = [], [], []

    def scatter(grads, nm):
        own = place_own(grads, [g.shape[0] // N_DEV for g in grads], me, scatter=True, name=nm.replace("start", "own"))
        sems, srcs, lands, tok = exchange_start(grads, own, scatter=True, name=nm)
        g_srcs.extend(srcs)
        g_lands.extend(lands)
        g_sems.extend(sems)
        return tok[0:1, 0:1]

    def ffn_bwd(i, dxo, xin, h, hu, a, f, g_post):
        dyf, da, dg_post, dgt = post_bwd_mm(dxo, f, g_post, gt_f[i], wdn[i], tm=tm, name=f"ffn_down_bwd_{i}")
        dhg, dhu, dcwg, dcwu, dcbg, dcbu = conv_bwd(da, hu[1], hu[2], hu[0], conv_w[i], rows=conv_rows, wblk=2 * LANES,
                                                    name=f"ffn_conv_bwd_{i}")
        dxin, dg_pre, dsh, dsc = mm_pre_bwd([dhg, dhu], wup[i], xin, dxo, row(g_ffn_pre, i), sc_f[i], tm=tm, tk=half_f,
                                            name=f"ffn_up_bwd_{i}")
        g_dn = wgrad([a], dyf, tr=2 * LANES, name=f"wgrad_down_{i}")
        g_up = wgrad([dhg, dhu], h, tr=2 * LANES, name=f"wgrad_up_{i}")
        tok = scatter([g_dn, g_up], f"scatter_start_ffn_{i}")
        return dxin, tok, dict(g_ffn_post=dg_post, g_ffn_pre=dg_pre, gt_f=dgt, sh_f=dsh, sc_f=dsc,
                               ffn_conv_w=jnp.concatenate([dcwg, dcwu], axis=1), ffn_conv_b=jnp.concatenate([dcbg, dcbu], axis=1)[0])

    dx3, tok, sf1 = ffn_bwd(1, dx4, x3, h3, hu1, a1, f1, row(g_ffn_post, 1))
    dy1, dus, dg_mpost1, dgt_m1 = post_bwd_mm(dx3, y1, row(g_mix_post, 1) + tok, gt_m[1], wout_o, tm=tm, name="out_odd_bwd")
    dz1, dws, dbs, dlng, dlnb = sgu_bwd(z1, dus, ln_g, ln_b, sgu_wb, sgu_wtb, bst, name="sgu_bwd")
    dx2, dg_mpre1, dsh_m1, dsc_m1 = mm_pre_bwd([dz1], win_o, x2, dx3, row(g_mix_pre, 1), sc_m[1], tm=tm, tk=D, name="in_odd_bwd")
    tok = scatter([wgrad([us], dy1, tr=2 * LANES, name="wgrad_out_odd"), wgrad([dz1], h2, tr=2 * LANES, name="wgrad_in_odd")],
                  "scatter_start_mix_1")

    dx1, tok, sf0 = ffn_bwd(0, dx2, x1, h1, hu0, a0, f0, row(g_ffn_post, 0) + tok)
    dy0, dpa, dg_mpost0, dgt_m0 = post_bwd_mm(dx1, y0, row(g_mix_post, 0) + tok, gt_m[0], wout_e, tm=tm, name="out_even_bwd")
    tok = scatter([permute_heads(wgrad(pa, dy0, tr=2 * LANES, name="wgrad_out_even"), inverse=True)], "scatter_start_out_0")
    du, dwp, dps = pool_bwd(u, dpa, w_pool[0], pool_scale + tok, name="pool_bwd")
    dq, dkv, dkvc, dsink = attn_bwd(q, kv, kvc, sink, dpa, cos, sa, sb, name="attn_bwd")
    dz0 = jnp.concatenate([du, dq, dkv], axis=1)
    dzc = jnp.concatenate([jnp.zeros((C, 8 * LANES), BF16), dkvc], axis=1)
    tok = scatter([permute_heads(wgrad([dz0], h0, tr=2 * LANES, extra=(dzc, hc), name="wgrad_in_even"), inverse=True)],
                  "scatter_start_in_0")
    grad_x, dg_mpre0, dsh_m0, dsc_m0 = mm_pre_bwd([dz0], win_e, x, dx1, row(g_mix_pre, 0) + tok, sc_m[0], tm=tm, tk=dz0.shape[1],
                                                  name="in_even_bwd")
    _, dg_mpre0c, dcsh, dcsc = mm_pre_bwd([dkvc], win_e, ctx, None, row(g_mix_pre, 0), csc_m, tm=C, tk=2 * LANES,
                                          w_row_off=8 * LANES, name="in_even_ctx_bwd")

    out, ran = {}, {}

    def update(name, lands, transposed):
        w_, m_, v_ = (a.transpose(0, 2, 1) if transposed else a for a in (P[name], M[name], V[name]))
        r = w_.shape[1]
        tr = r // 4 if r % 64 == 0 and r > 256 else r
        res = adamw(w_, m_, v_, [l_.reshape(N_DEV, r, l_.shape[1]) for l_ in lands], tr=tr, name=f"adamw_{name}")
        ran[name] = res[0]
        for kind, val in zip(("grad", "delta", "new_m", "new_v"), res):
            out[(kind, name)] = val.transpose(0, 2, 1) if transposed else val

    zero = jnp.zeros((1, D), F32)
    dmod0 = jnp.concatenate([dsh_m0, dsc_m0, dgt_m0, sf0["sh_f"], sf0["sc_f"], sf0["gt_f"]], axis=1)
    dmodc = jnp.concatenate([dcsh, dcsc, zero, zero, zero, zero], axis=1)
    dmod1 = jnp.concatenate([dsh_m1, dsc_m1, dgt_m1, sf1["sh_f"], sf1["sc_f"], sf1["gt_f"]], axis=1)
    dmods = jnp.concatenate([dmod0, dmodc, dmod1], axis=0)
    dm = dmods.reshape(-1, LANES).astype(BF16)
    d_sems, d_srcs, d_lands, d_tok = exchange_start(
        [dm], place_own([dm], [dm.shape[0]], me, scatter=False, name="dmods_own"), scatter=False, name="dmods_start")
    slots = exchange_wait(g_srcs[:6], g_lands[:6], g_sems[:12], d_tok, scatter=True, name="scatter_wait_early")
    early = slots
    update("w_ffn_down", [slots[4], slots[0]], False)
    update("w_in_odd", [slots[3]], True)
    update("w_out_odd", [slots[2]], False)
    updated = lambda names: [ran[k] for k in names]
    dmods_all = exchange_wait(d_srcs, d_lands, d_sems, updated(("w_ffn_down", "w_in_odd", "w_out_odd")), scatter=False, name="dmods_wait")[0]
    dall = lax.dynamic_index_in_dim(dmods_all.astype(F32).reshape(N_DEV, 3, N_DEV, n_ada), me, 2, False)
    g_w_ada, dcc = ada_bwd_mm(silu_c, c_ctx[None, :], dall, w_ada, name="ada_bwd")

    rep = dict(
        c_ctx=dcc[0:1],
        b_ada=jnp.concatenate([dmod0 + dmodc, dmod1]),
        g_mix_pre=jnp.concatenate([dg_mpre0 + dg_mpre0c, dg_mpre1]),
        g_mix_post=jnp.concatenate([dg_mpost0, dg_mpost1]),
        g_ffn_pre=jnp.concatenate([sf0["g_ffn_pre"], sf1["g_ffn_pre"]]),
        g_ffn_post=jnp.concatenate([sf0["g_ffn_post"], sf1["g_ffn_post"]]),
        w_pool=_nat2d(dwp), pool_scale=dps, attn_sink=dsink[:, :N_Q_HEADS],
        sgu_w=_nat2d(dws), sgu_b=dbs[:, :sgu_b.shape[1]].T,
        ffn_conv_b=jnp.stack([sf0["ffn_conv_b"], sf1["ffn_conv_b"]]),
    )
    hi = loss_part.astype(BF16).astype(F32)
    mid = (loss_part - hi).astype(BF16).astype(F32)
    loss_piece = jnp.pad(jnp.concatenate([hi, mid, loss_part - hi - mid], axis=1), ((0, 7), (0, LANES - 3)))
    conv_g = jnp.stack([sf0["ffn_conv_w"], sf1["ffn_conv_w"]]).reshape(2 * 3, N_DEV, n_cw).swapaxes(0, 1)
    shard_full = dict(sgu_ln_g=dlng.reshape(N_DEV, LANES), sgu_ln_b=dlnb.reshape(N_DEV, LANES),
                      ffn_conv_w=jnp.concatenate([_pack_rows(conv_g[d]) for d in range(N_DEV)], axis=0))
    small_names = list(rep) + list(shard_full)
    pieces = [_pack_rows(rep[k]) for k in rep] + list(shard_full.values()) + [loss_piece]
    sizes = [p.shape[0] for p in pieces]
    offs = [sum(sizes[:i]) for i in range(len(sizes))]
    pieces.append(jnp.zeros((-sum(sizes) % 16, LANES), F32))
    gpack = jnp.concatenate(pieces, axis=0).astype(BF16)
    own = place_own([gpack], [gpack.shape[0]], me, scatter=False, name="smallgrad_own")
    s_sems, s_srcs, s_lands, small_tok = exchange_start([gpack], own, scatter=False, name="smallgrad_start")

    slots = exchange_wait(g_srcs[6:], g_lands[6:], g_sems[12:], small_tok, scatter=True, name="scatter_wait_late")
    update("w_in_even", [slots[1]], True)
    update("w_out_even", [slots[0]], False)
    update("w_ffn_up", [early[5], early[1]], True)
    res = adamw(w_ada, m_w_ada, v_w_ada, [g_w_ada[l][None] for l in range(w_ada.shape[0])], tr=D // 4, name="adamw_w_ada")
    ran["w_ada"] = res[0]
    for kind, val in zip(("grad", "delta", "new_m", "new_v"), res):
        out[(kind, "w_ada")] = val

    gpacks = exchange_wait(s_srcs, s_lands, s_sems, updated(("w_in_even", "w_out_even", "w_ffn_up", "w_ada")), scatter=False,
                           name="smallgrad_wait")[0]
    per_dev = {k: shard_full[k].shape[0] // N_DEV for k in shard_full}
    params = [(_nat2d(P[k]), _nat2d(M[k]), _nat2d(V[k]), offs[i], per_dev.get(k, 0)) for i, k in enumerate(small_names)]
    res = small_update(gpacks.reshape(N_DEV, -1, LANES), jnp.reshape(me, (1,)).astype(jnp.int32), params, offs[-1], name="adamw_small")
    for i, k in enumerate(small_names):
        for kind, val in zip(("grad", "delta", "new_m", "new_v"), res[4 * i:4 * i + 4]):
            out[(kind, k)] = val.reshape(P[k].shape)
    loss = res[-1][0, 0]

    names = list(P)
    final = [loss, grad_x[None]]
    for kind in ("grad", "delta", "new_m", "new_v"):
        for k in names:
            val = out[(kind, k)]
            final.append(val)
    return tuple(final)
```

```python
import functools
import math

import jax
import jax.numpy as jnp
from jax import lax
from jax.experimental import pallas as pl
from jax.experimental.pallas import tpu as pltpu

F32 = jnp.float32
BF16 = jnp.bfloat16
MESH = pl.DeviceIdType.MESH
N_DEV = 8
LANES = 128
VMEM_LIMIT = 48 * 1024 * 1024
EPS = 1e-6
NEG_INF = -1e30
GRID_W = 64
WINDOW = 128
BLK = 128
HEAD_DIM = 64
N_Q_HEADS = 8
N_KV_HEADS = 2
GQA = N_Q_HEADS // N_KV_HEADS
POOL_WINDOWS = (2, 4, 8, 16)
ROPE_BASE = 10000.0
ROPE_FREQS = HEAD_DIM // 4
PAD = 16
ADAM_LR, ADAM_B1, ADAM_B2, ADAM_EPS, ADAM_WD, ADAM_STEP = 0.001, 0.9, 0.999, 1e-08, 0.01, 10
BC1 = 1.0 - ADAM_B1 ** ADAM_STEP
BC2 = 1.0 - ADAM_B2 ** ADAM_STEP
SQRT_2_OVER_PI = math.sqrt(2.0 / math.pi)
GELU_C = 0.044715


def _cp(sem=None):
    return pltpu.CompilerParams(dimension_semantics=sem, vmem_limit_bytes=VMEM_LIMIT)


def _dot(a, b):
    return jnp.dot(a, b, preferred_element_type=F32)


def _dot_nt(a, b):
    return lax.dot_general(a, b, (((1,), (1,)), ((), ())), preferred_element_type=F32)


def _dot_tn(a, b):
    return lax.dot_general(a, b, (((0,), (0,)), ((), ())), preferred_element_type=F32)


def _rms(x):
    r = lax.rsqrt(jnp.mean(x * x, axis=-1, keepdims=True) + EPS)
    return x * r, r


def _rms_bwd(dn, n, r):
    return r * (dn - n * jnp.mean(dn * n, axis=-1, keepdims=True))


def _colsum(a):
    return jnp.sum(a, axis=0, keepdims=True)


def _rope(x, c, sa, sb):
    return x * c + pltpu.roll(x, LANES - ROPE_FREQS, 1) * sa + pltpu.roll(x, ROPE_FREQS, 1) * sb


def _full(shape):
    return pl.BlockSpec(shape, lambda *_: (0,) * len(shape))


def pre_mm(x, g, sh, sc, wt, *, tm, tn, w_row_off=0, name):
    T, D = x.shape
    n_rows = wt.shape[0] - w_row_off
    off = w_row_off // tn

    def body(x_ref, g_ref, sh_ref, sc_ref, w_ref, h_ref, z_ref):
        @pl.when(pl.program_id(1) == 0)
        def _():
            n, _ = _rms(x_ref[...])
            h_ref[...] = (n * g_ref[...] * (1.0 + sc_ref[...]) + sh_ref[...]).astype(BF16)

        z_ref[...] = _dot_nt(h_ref[...], w_ref[...]).astype(BF16)

    vec = pl.BlockSpec((1, D), lambda i, j: (0, 0))
    return pl.pallas_call(
        body, name=name, grid=(T // tm, n_rows // tn),
        in_specs=[pl.BlockSpec((tm, D), lambda i, j: (i, 0)), vec, vec, vec, pl.BlockSpec((tn, D), lambda i, j: (j + off, 0))],
        out_specs=[pl.BlockSpec((tm, D), lambda i, j: (i, 0)), pl.BlockSpec((tm, tn), lambda i, j: (i, j))],
        out_shape=[jax.ShapeDtypeStruct((T, D), BF16), jax.ShapeDtypeStruct((T, n_rows), BF16)],
        compiler_params=_cp(("parallel", "arbitrary")),
    )(x, g, sh, sc, wt)


def inproj_even(x, g, sh, sc, wt, cos, sa, sb, *, tm, name):
    T, D = x.shape
    N = wt.shape[0]

    def body(x_ref, g_ref, sh_ref, sc_ref, w_ref, c_ref, sa_ref, sb_ref, h_ref, u_ref, q_ref, kv_ref):
        n, _ = _rms(x_ref[...])
        h = (n * g_ref[...] * (1.0 + sc_ref[...]) + sh_ref[...]).astype(BF16)
        h_ref[...] = h
        z = _dot_nt(h, w_ref[...])
        u_ref[...] = z[:, :4 * LANES]
        c, a, b = c_ref[...], sa_ref[...], sb_ref[...]
        for s in range(4):
            q_ref[:, s * LANES:(s + 1) * LANES] = _rope(z[:, (4 + s) * LANES:(5 + s) * LANES], c, a, b).astype(BF16)
        kv_ref[:, :LANES] = _rope(z[:, 8 * LANES:9 * LANES], c, a, b).astype(BF16)
        kv_ref[:, LANES:] = z[:, 9 * LANES:].astype(BF16)

    vec = pl.BlockSpec((1, D), lambda i: (0, 0))
    row = lambda w: pl.BlockSpec((tm, w), lambda i: (i, 0))
    return pl.pallas_call(
        body, name=name, grid=(T // tm,),
        in_specs=[row(D), vec, vec, vec, _full((N, D)), row(LANES), row(LANES), row(LANES)],
        out_specs=[row(D), row(4 * LANES), row(4 * LANES), row(2 * LANES)],
        out_shape=[jax.ShapeDtypeStruct((T, D), BF16), jax.ShapeDtypeStruct((T, 4 * LANES), F32),
                   jax.ShapeDtypeStruct((T, 4 * LANES), BF16), jax.ShapeDtypeStruct((T, 2 * LANES), BF16)],
        compiler_params=_cp(("parallel",)),
    )(x, g, sh, sc, wt, cos, sa, sb)


def mm_post(a_parts, w, x, g, gt, *, tm, target=None, name):
    T = a_parts[0].shape[0]
    D = w.shape[1]
    npart = len(a_parts)
    offs = [sum(a_.shape[1] for a_ in a_parts[:p]) for p in range(npart + 1)]
    with_loss = target is not None

    def body(*refs):
        a_refs, (w_ref, x_ref, g_ref, gt_ref) = refs[:npart], refs[npart:npart + 4]
        y = _dot(a_refs[0][...], w_ref[offs[0]:offs[1], :])
        for p in range(1, npart):
            y = y + _dot(a_refs[p][...], w_ref[offs[p]:offs[p + 1], :])
        n, _ = _rms(y)
        xn = x_ref[...] + gt_ref[...] * (n * g_ref[...])
        if not with_loss:
            y_ref, xn_ref = refs[npart + 4:]
            y_ref[...] = y.astype(BF16)
            xn_ref[...] = xn
            return
        t_ref, y_ref, d_ref, l_ref = refs[npart + 4:]
        y_ref[...] = y.astype(BF16)

        @pl.when(pl.program_id(0) == 0)
        def _():
            l_ref[...] = jnp.zeros_like(l_ref)

        e = xn - t_ref[...]
        l_ref[...] += 0.5 * jnp.sum(jnp.mean(e * e, axis=-1, keepdims=True), axis=0, keepdims=True)
        d_ref[...] = e * (1.0 / D)

    vec = pl.BlockSpec((1, D), lambda i: (0, 0))
    row = lambda w_: pl.BlockSpec((tm, w_), lambda i: (i, 0))
    in_specs = [row(a_.shape[1]) for a_ in a_parts] + [_full(w.shape), row(D), vec, vec]
    out_specs = [row(D), row(D)]
    out_shape = [jax.ShapeDtypeStruct((T, D), BF16), jax.ShapeDtypeStruct((T, D), F32)]
    if with_loss:
        in_specs.append(row(D))
        out_specs.append(_full((1, 1)))
        out_shape.append(jax.ShapeDtypeStruct((1, 1), F32))
    return pl.pallas_call(
        body, name=name, grid=(T // tm,), in_specs=in_specs, out_specs=out_specs, out_shape=out_shape,
        compiler_params=_cp(("arbitrary",) if with_loss else ("parallel",)),
    )(*a_parts, w, x, g, gt, *((target,) if with_loss else ()))


def post_bwd_mm(dxn, y, g, gt, w, *, tm, name):
    T, D = y.shape
    K = w.shape[0]

    def body(dxn_ref, y_ref, g_ref, gt_ref, w_ref, dy_ref, da_ref, dg_ref, dgt_ref):
        @pl.when(pl.program_id(0) == 0)
        def _():
            dg_ref[...] = jnp.zeros_like(dg_ref)
            dgt_ref[...] = jnp.zeros_like(dgt_ref)

        d = dxn_ref[...]
        n, r = _rms(y_ref[...].astype(F32))
        g_, gt_ = g_ref[...], gt_ref[...]
        dg_ref[...] += _colsum(d * gt_ * n)
        dgt_ref[...] += _colsum(d * g_ * n)
        dy = _rms_bwd(d * (gt_ * g_), n, r).astype(BF16)
        dy_ref[...] = dy
        da_ref[...] = _dot_nt(dy, w_ref[...]).astype(BF16)

    vec = pl.BlockSpec((1, D), lambda i: (0, 0))
    row = lambda w_: pl.BlockSpec((tm, w_), lambda i: (i, 0))
    return pl.pallas_call(
        body, name=name, grid=(T // tm,),
        in_specs=[row(D), row(D), vec, vec, _full((K, D))],
        out_specs=[row(D), row(K), vec, vec],
        out_shape=[jax.ShapeDtypeStruct((T, D), BF16), jax.ShapeDtypeStruct((T, K), BF16),
                   jax.ShapeDtypeStruct((1, D), F32), jax.ShapeDtypeStruct((1, D), F32)],
        compiler_params=_cp(("arbitrary",)),
    )(dxn, y, g, gt, w)


def mm_pre_bwd(dzs, wt, x, dres, g, sc, *, tm, tk, w_row_off=0, name):
    T, N = dzs[0].shape
    D = x.shape[1]
    nk = N // tk
    npart = len(dzs)
    off = w_row_off // tk
    has_res = dres is not None

    def body(*refs):
        dz_refs = refs[:npart]
        w_refs = refs[npart:2 * npart]
        rest = refs[2 * npart:]
        x_ref = rest[0]
        dres_ref = rest[1] if has_res else None
        g_ref, sc_ref, dx_ref, dg_ref, dsh_ref, dsc_ref, acc = rest[1 + has_res:]
        i, k = pl.program_id(0), pl.program_id(1)

        @pl.when(jnp.logical_and(i == 0, k == 0))
        def _():
            dg_ref[...] = jnp.zeros_like(dg_ref)
            dsh_ref[...] = jnp.zeros_like(dsh_ref)
            dsc_ref[...] = jnp.zeros_like(dsc_ref)

        part = _dot(dz_refs[0][...], w_refs[0][...])
        for p in range(1, npart):
            part = part + _dot(dz_refs[p][...], w_refs[p][...])

        @pl.when(k == 0)
        def _():
            acc[...] = part

        @pl.when(k > 0)
        def _():
            acc[...] += part

        @pl.when(k == nk - 1)
        def _():
            dh = acc[...]
            n, r = _rms(x_ref[...])
            g_, s1 = g_ref[...], 1.0 + sc_ref[...]
            dsh_ref[...] += _colsum(dh)
            dsc_ref[...] += _colsum(dh * n * g_)
            dg_ref[...] += _colsum(dh * s1 * n)
            dxp = _rms_bwd(dh * (g_ * s1), n, r)
            dx_ref[...] = dxp + dres_ref[...] if has_res else dxp

    vec = pl.BlockSpec((1, D), lambda i, k: (0, 0))
    row = pl.BlockSpec((tm, D), lambda i, k: (i, 0))
    w_specs = [pl.BlockSpec((tk, D), (lambda i, k, p=p: (k + off + p * nk, 0))) for p in range(npart)]
    res_specs, res_args = ([row], (dres,)) if has_res else ([], ())
    return pl.pallas_call(
        body, name=name, grid=(T // tm, nk),
        in_specs=[pl.BlockSpec((tm, tk), lambda i, k: (i, k))] * npart + w_specs + [row] + res_specs + [vec, vec],
        out_specs=[row, vec, vec, vec],
        out_shape=[jax.ShapeDtypeStruct((T, D), F32)] + [jax.ShapeDtypeStruct((1, D), F32)] * 3,
        scratch_shapes=[pltpu.VMEM((tm, D), F32)],
        compiler_params=_cp(("arbitrary", "arbitrary")),
    )(*dzs, *([wt] * npart), x, *res_args, g, sc)


def wgrad(a_parts, b, *, tr, extra=None, name):
    T, R = a_parts[0].shape
    D = b.shape[1]
    npart = len(a_parts)
    nr = R // tr

    def body(*refs):
        a_refs, b_ref = refs[:npart], refs[npart]
        g_ref = refs[-1]
        for p in range(npart):
            @pl.when(pl.program_id(0) // nr == p)
            def _():
                acc = _dot_tn(a_refs[p][...], b_ref[...])
                if extra is not None:
                    acc += _dot_tn(refs[npart + 1][...], refs[npart + 2][...])
                g_ref[...] = acc.astype(BF16)

    in_specs = [pl.BlockSpec((T, tr), (lambda r, p=p: (0, jnp.clip(r - p * nr, 0, nr - 1)))) for p in range(npart)]
    in_specs.append(_full((T, D)))
    args = [*a_parts, b]
    if extra is not None:
        a2, b2 = extra
        in_specs += [pl.BlockSpec((a2.shape[0], tr), lambda r: (0, r)), _full(b2.shape)]
        args += [a2, b2]
    return pl.pallas_call(
        body, name=name, grid=(npart * nr,),
        in_specs=in_specs, out_specs=pl.BlockSpec((tr, D), lambda r: (r, 0)),
        out_shape=jax.ShapeDtypeStruct((npart * R, D), BF16),
        compiler_params=_cp(("parallel",)),
    )(*args)


def _conv_ext(ref, r0, rows, total):
    top = ref[pl.ds(pl.multiple_of(jnp.maximum(r0 - PAD, 0), PAD), PAD), :]
    mid = ref[pl.ds(r0, rows), :]
    bot = ref[pl.ds(pl.multiple_of(jnp.minimum(r0 + rows, total - PAD), PAD), PAD), :]
    top = jnp.where(r0 > 0, top, jnp.zeros_like(top))
    bot = jnp.where(r0 + rows < total, bot, jnp.zeros_like(bot))
    return jnp.concatenate([top, mid, bot], axis=0).astype(F32)


def _shift_rows(a, k):
    return pltpu.roll(a, k % a.shape[0], 0)


def _conv3(x, w, b):
    return w[0:1] * _shift_rows(x, 1) + w[1:2] * x + w[2:3] * _shift_rows(x, -1) + b


def _gate_up_specs(rows_, wblk, nb):
    return [pl.BlockSpec((rows_, wblk), lambda j: (0, j)), pl.BlockSpec((rows_, wblk), lambda j: (0, j + nb))]


def conv_fwd(hu, cw, cb, *, rows, wblk, name):
    L, N2 = hu.shape
    nb = N2 // 2 // wblk
    nchunk = L // rows

    def body(hg_ref, hu_ref, wg_ref, wu_ref, bg_ref, bu_ref, a_ref, s1_ref, s2_ref):
        def chunk(ci, carry):
            r0 = pl.multiple_of(ci * rows, rows)
            gate = _conv3(_conv_ext(hg_ref, r0, rows, L), wg_ref[...], bg_ref[...])[PAD:PAD + rows]
            up = _conv3(_conv_ext(hu_ref, r0, rows, L), wu_ref[...], bu_ref[...])[PAD:PAD + rows]
            sg = jax.nn.sigmoid(gate)
            silu = gate * sg
            at = pl.ds(r0, rows)
            a_ref[at, :] = (silu * up).astype(BF16)
            s1_ref[at, :] = silu.astype(BF16)
            s2_ref[at, :] = (up * (sg + silu * (1.0 - sg))).astype(BF16)
            return carry

        lax.fori_loop(0, nchunk, chunk, 0)

    out = pl.BlockSpec((L, wblk), lambda j: (0, j))
    return pl.pallas_call(
        body, name=name, grid=(nb,),
        in_specs=_gate_up_specs(L, wblk, nb) + _gate_up_specs(3, wblk, nb) + _gate_up_specs(1, wblk, nb),
        out_specs=[out] * 3, out_shape=[jax.ShapeDtypeStruct((L, N2 // 2), BF16)] * 3,
        compiler_params=_cp(("parallel",)),
    )(hu, hu, cw, cw, cb, cb)


def conv_bwd(da, s1, s2, hu, cw, *, rows, wblk, name):
    L, N2 = hu.shape
    F = N2 // 2
    nb = F // wblk
    nchunk = L // rows
    mid = slice(PAD, PAD + rows)

    def body(da_ref, s1_ref, s2_ref, hg_ref, hu_ref, wg_ref, wu_ref, dg_ref, du_ref, dwg_ref, dwu_ref, dbg_ref, dbu_ref):
        for ref in (dwg_ref, dwu_ref, dbg_ref, dbu_ref):
            ref[...] = jnp.zeros_like(ref)

        def half_bwd(x_ref, dh, w_ref, dx_ref, dw_ref, db_ref, r0):
            w = w_ref[...]
            nxt, prv = _shift_rows(dh, -1)[mid], _shift_rows(dh, 1)[mid]
            dhm, xm = dh[mid], x_ref[pl.ds(r0, rows), :].astype(F32)
            dx_ref[pl.ds(r0, rows), :] = (w[0:1] * nxt + w[1:2] * dhm + w[2:3] * prv).astype(BF16)
            db_ref[...] += _colsum(dhm)
            dw_ref[0:1, :] += _colsum(nxt * xm)
            dw_ref[1:2, :] += _colsum(dhm * xm)
            dw_ref[2:3, :] += _colsum(prv * xm)

        def chunk(ci, carry):
            r0 = pl.multiple_of(ci * rows, rows)
            d = _conv_ext(da_ref, r0, rows, L)
            half_bwd(hu_ref, d * _conv_ext(s1_ref, r0, rows, L), wu_ref, du_ref, dwu_ref, dbu_ref, r0)
            half_bwd(hg_ref, d * _conv_ext(s2_ref, r0, rows, L), wg_ref, dg_ref, dwg_ref, dbg_ref, r0)
            return carry

        lax.fori_loop(0, nchunk, chunk, 0)

    blk = lambda r: pl.BlockSpec((r, wblk), lambda j: (0, j))
    return pl.pallas_call(
        body, name=name, grid=(nb,),
        in_specs=[blk(L)] * 3 + _gate_up_specs(L, wblk, nb) + _gate_up_specs(3, wblk, nb),
        out_specs=[blk(L), blk(L), blk(3), blk(3), blk(1), blk(1)],
        out_shape=[jax.ShapeDtypeStruct((L, F), BF16)] * 2 + [jax.ShapeDtypeStruct((3, F), F32)] * 2
        + [jax.ShapeDtypeStruct((1, F), F32)] * 2,
        compiler_params=_cp(("parallel",)),
    )(da, s1, s2, hu, hu, cw, cw)


def _window_sums(pad_ref, w, lead):
    a = pad_ref[...]
    k = 1
    while k < w:
        a = a + _shift_rows(a, -k)
        k *= 2
    return _shift_rows(a, lead) if lead else a


def _pool_counts(L, h):
    t = lax.broadcasted_iota(jnp.int32, (L, 1), 0)
    return (jnp.minimum(t + h, L) - jnp.maximum(t - h, 0)).astype(F32)


def _pooled(u_ref, pad_ref, L, w):
    h = w // 2
    pad_ref[pl.ds(PAD, L), :] = u_ref[...]
    win = _window_sums(pad_ref, w, h)[PAD:PAD + L]
    return win / _pool_counts(L, h) - u_ref[...]


def _zero_pad_edges(pad_ref, L):
    z = jnp.zeros((PAD, LANES), F32)
    pad_ref[pl.ds(0, PAD), :] = z
    pad_ref[pl.ds(PAD + L, PAD), :] = z


def pool_fwd(u, w_pool, pool_scale, *, name):
    L = u.shape[0]

    def body(u_ref, w_ref, ps_ref, p_ref, pad_ref):
        _zero_pad_edges(pad_ref, L)
        for gi, win in enumerate(POOL_WINDOWS):
            @pl.when(pl.program_id(0) == gi)
            def _():
                pooled = _pooled(u_ref, pad_ref, L, win)
                p_ref[...] = (_dot(pooled.astype(BF16), w_ref[...].astype(BF16)) * ps_ref[...]).astype(BF16)

    return pl.pallas_call(
        body, name=name, grid=(len(POOL_WINDOWS),),
        in_specs=[pl.BlockSpec((L, LANES), lambda gi: (0, gi)), pl.BlockSpec((None, LANES, LANES), lambda gi: (gi, 0, 0)),
                  pl.BlockSpec((1, LANES), lambda gi: (0, gi))],
        out_specs=pl.BlockSpec((L, LANES), lambda gi: (0, gi)),
        out_shape=jax.ShapeDtypeStruct((L, 4 * LANES), BF16),
        scratch_shapes=[pltpu.VMEM((L + 2 * PAD, LANES), F32)],
        compiler_params=_cp(("parallel",)),
    )(u, w_pool, pool_scale)


def pool_bwd(u, dpa, w_pool, pool_scale, *, name):
    L = u.shape[0]

    def body(u_ref, dp_ref, w_ref, ps_ref, du_ref, dw_ref, dps_ref, pad_ref):
        _zero_pad_edges(pad_ref, L)
        for gi, win in enumerate(POOL_WINDOWS):
            @pl.when(pl.program_id(0) == gi)
            def _():
                h = win // 2
                wb = w_ref[...].astype(BF16)
                pooled = _pooled(u_ref, pad_ref, L, win).astype(BF16)
                dp = dp_ref[...].astype(F32)
                dps_ref[...] = _colsum(dp * _dot(pooled, wb))
                dy = (dp * ps_ref[...]).astype(BF16)
                dw_ref[...] = _dot_tn(pooled, dy)
                dpooled = _dot_nt(dy, wb)
                pad_ref[pl.ds(PAD, L), :] = dpooled / _pool_counts(L, h)
                du_ref[...] = (_window_sums(pad_ref, win, h - 1)[PAD:PAD + L] - dpooled).astype(BF16)

    return pl.pallas_call(
        body, name=name, grid=(len(POOL_WINDOWS),),
        in_specs=[pl.BlockSpec((L, LANES), lambda gi: (0, gi)), pl.BlockSpec((L, LANES), lambda gi: (0, gi)),
                  pl.BlockSpec((None, LANES, LANES), lambda gi: (gi, 0, 0)), pl.BlockSpec((1, LANES), lambda gi: (0, gi))],
        out_specs=[pl.BlockSpec((L, LANES), lambda gi: (0, gi)), pl.BlockSpec((None, LANES, LANES), lambda gi: (gi, 0, 0)),
                   pl.BlockSpec((1, LANES), lambda gi: (0, gi))],
        out_shape=[jax.ShapeDtypeStruct((L, 4 * LANES), BF16), jax.ShapeDtypeStruct((4, LANES, LANES), F32),
                   jax.ShapeDtypeStruct((1, 4 * LANES), F32)],
        scratch_shapes=[pltpu.VMEM((L + 2 * PAD, LANES), F32)],
        compiler_params=_cp(("parallel",)),
    )(u, dpa, w_pool, pool_scale)


def _attn_probs(qk, band_k, ctx_k, sink_ref, kh, mask4):
    s_loc = jnp.where(mask4, _dot_nt(qk, band_k), NEG_INF)
    s_ctx = _dot_nt(qk, ctx_k)
    sk = jnp.concatenate([jnp.full((BLK, 1), sink_ref[kh * GQA + hh], F32) for hh in range(GQA)], axis=0)
    m = jnp.maximum(jnp.maximum(jnp.max(s_loc, axis=-1, keepdims=True), jnp.max(s_ctx, axis=-1, keepdims=True)), sk)
    e_loc, e_ctx, e_s = jnp.exp(s_loc - m), jnp.exp(s_ctx - m), jnp.exp(sk - m)
    inv = 1.0 / (jnp.sum(e_loc, axis=-1, keepdims=True) + jnp.sum(e_ctx, axis=-1, keepdims=True) + e_s)
    return e_loc * inv, e_ctx * inv, e_s * inv


def _attn_block(n, L):
    start = pl.multiple_of(jnp.clip((n - 1) * BLK, 0, L - 3 * BLK), BLK)
    qpos = n * BLK + lax.broadcasted_iota(jnp.int32, (BLK, 3 * BLK), 0)
    kpos = start + lax.broadcasted_iota(jnp.int32, (BLK, 3 * BLK), 1)
    mask = jnp.abs(kpos - qpos) <= WINDOW
    return start, jnp.concatenate([mask] * GQA, axis=0)


def _stack_slabs(ref):
    return jnp.concatenate([ref[:, s * LANES:(s + 1) * LANES] for s in range(GQA)], axis=0)


def _kv_head_lanes(kh):
    return (lax.broadcasted_iota(jnp.int32, (1, LANES), 1) // HEAD_DIM) == kh


def permute_heads(w, inverse=False):
    lo, hi = 4 * LANES, 8 * LANES
    mid = w[lo:hi].reshape(*((GQA, N_KV_HEADS) if inverse else (N_KV_HEADS, GQA)), HEAD_DIM, w.shape[1])
    return jnp.concatenate([w[:lo], mid.swapaxes(0, 1).reshape(hi - lo, w.shape[1]), w[hi:]], axis=0)


def attn_fwd(q, kv, kvc, sink, *, name):
    L = q.shape[0]
    C = kvc.shape[0]
    scale = HEAD_DIM ** -0.5

    def body(q_ref, kv_ref, kvc_ref, sink_ref, o_ref):
        start, mask4 = _attn_block(pl.program_id(0), L)
        band = kv_ref[pl.ds(start, 3 * BLK), :]
        kvc_ = kvc_ref[...]
        qs = _stack_slabs(q_ref) * scale
        o = jnp.zeros((GQA * BLK, LANES), F32)
        for kh in range(N_KV_HEADS):
            grp = _kv_head_lanes(kh)
            qk = jnp.where(grp, qs, jnp.zeros_like(qs))
            p_loc, p_ctx, _ = _attn_probs(qk, band[:, :LANES], kvc_[:, :LANES], sink_ref, kh, mask4)
            o = o + jnp.where(grp, _dot(p_loc.astype(BF16), band[:, LANES:]) + _dot(p_ctx.astype(BF16), kvc_[:, LANES:]), 0.0)
        for s in range(GQA):
            o_ref[:, s * LANES:(s + 1) * LANES] = o[s * BLK:(s + 1) * BLK].astype(BF16)

    return pl.pallas_call(
        body, name=name, grid=(L // BLK,),
        in_specs=[pl.BlockSpec((BLK, 4 * LANES), lambda n: (n, 0)), _full((L, 2 * LANES)), _full((C, 2 * LANES)),
                  pl.BlockSpec(memory_space=pltpu.SMEM)],
        out_specs=pl.BlockSpec((BLK, 4 * LANES), lambda n: (n, 0)),
        out_shape=jax.ShapeDtypeStruct((L, 4 * LANES), BF16),
        compiler_params=_cp(("parallel",)),
    )(q, kv, kvc, sink)


def attn_bwd(q, kv, kvc, sink, dpa, cos, sa, sb, *, name):
    L = q.shape[0]
    C = kvc.shape[0]
    nb = L // BLK
    scale = HEAD_DIM ** -0.5

    def body(q_ref, kv_ref, kvc_ref, sink_ref, do_ref, c_ref, sa_ref, sb_ref, cq_ref, saq_ref, sbq_ref,
             dq_ref, dkv_ref, dkvc_ref, dsink_ref, dkv_acc, dkvc_acc):
        n = pl.program_id(0)

        @pl.when(n == 0)
        def _():
            dkv_acc[...] = jnp.zeros_like(dkv_acc)
            dkvc_acc[...] = jnp.zeros_like(dkvc_acc)
            dsink_ref[...] = jnp.zeros_like(dsink_ref)

        start, mask4 = _attn_block(n, L)
        band = kv_ref[pl.ds(start, 3 * BLK), :]
        kvc_ = kvc_ref[...]
        band_k, band_v, ctx_k, ctx_v = band[:, :LANES], band[:, LANES:], kvc_[:, :LANES], kvc_[:, LANES:]
        qs = _stack_slabs(q_ref) * scale
        dos = _stack_slabs(do_ref)
        lane = lax.broadcasted_iota(jnp.int32, (1, LANES), 1)
        dsink = jnp.zeros((1, LANES), F32)
        dq = jnp.zeros((GQA * BLK, LANES), F32)
        dk = jnp.zeros((LANES, 3 * BLK), F32)
        dv = jnp.zeros((LANES, 3 * BLK), F32)
        dkc = jnp.zeros((LANES, C), F32)
        dvc = jnp.zeros((LANES, C), F32)
        for kh in range(N_KV_HEADS):
            grp = _kv_head_lanes(kh)
            qk = jnp.where(grp, qs, jnp.zeros_like(qs))
            dok = jnp.where(grp, dos, jnp.zeros_like(dos))
            p_loc, p_ctx, p_s = _attn_probs(qk, band_k, ctx_k, sink_ref, kh, mask4)
            dp_loc = _dot_nt(dok, band_v)
            dp_ctx = _dot_nt(dok, ctx_v)
            delta = jnp.sum(p_loc * dp_loc, axis=-1, keepdims=True) + jnp.sum(p_ctx * dp_ctx, axis=-1, keepdims=True)
            ds_loc = (p_loc * (dp_loc - delta)).astype(BF16)
            ds_ctx = (p_ctx * (dp_ctx - delta)).astype(BF16)
            dsk = p_s * delta
            for hh in range(GQA):
                dsink = dsink - jnp.where(lane == kh * GQA + hh, jnp.sum(dsk[hh * BLK:(hh + 1) * BLK], axis=0, keepdims=True), 0.0)
            dq = dq + jnp.where(grp, _dot(ds_loc, band_k) + _dot(ds_ctx, ctx_k), 0.0)
            dk = dk + _dot_tn(qk, ds_loc)
            dv = dv + _dot_tn(dok, p_loc.astype(BF16))
            dkc = dkc + _dot_tn(qk, ds_ctx)
            dvc = dvc + _dot_tn(dok, p_ctx.astype(BF16))
        dsink_ref[...] += dsink
        dkv_acc[:LANES, pl.ds(start, 3 * BLK)] += dk
        dkv_acc[LANES:, pl.ds(start, 3 * BLK)] += dv
        dkvc_acc[:LANES, :] += dkc
        dkvc_acc[LANES:, :] += dvc
        c, a, b = cq_ref[...], -saq_ref[...], -sbq_ref[...]
        for s in range(GQA):
            dq_ref[:, s * LANES:(s + 1) * LANES] = _rope(dq[s * BLK:(s + 1) * BLK] * scale, c, a, b).astype(BF16)

        @pl.when(n == nb - 1)
        def _():
            dkv_ref[:, :LANES] = _rope(dkv_acc[:LANES, :].T, c_ref[...], -sa_ref[...], -sb_ref[...]).astype(BF16)
            dkv_ref[:, LANES:] = dkv_acc[LANES:, :].T.astype(BF16)
            dkvc_ref[...] = dkvc_acc[...].T.astype(BF16)

    blk = lambda w: pl.BlockSpec((BLK, w), lambda n: (n, 0))
    return pl.pallas_call(
        body, name=name, grid=(nb,),
        in_specs=[blk(4 * LANES), _full((L, 2 * LANES)), _full((C, 2 * LANES)), pl.BlockSpec(memory_space=pltpu.SMEM),
                  pl.BlockSpec((BLK, 4 * LANES), lambda n: (n, 1)),
                  _full((L, LANES)), _full((L, LANES)), _full((L, LANES)), blk(LANES), blk(LANES), blk(LANES)],
        out_specs=[blk(4 * LANES), _full((L, 2 * LANES)), _full((C, 2 * LANES)), _full((1, LANES))],
        out_shape=[jax.ShapeDtypeStruct((L, 4 * LANES), BF16), jax.ShapeDtypeStruct((L, 2 * LANES), BF16),
                   jax.ShapeDtypeStruct((C, 2 * LANES), BF16), jax.ShapeDtypeStruct((1, LANES), F32)],
        scratch_shapes=[pltpu.VMEM((2 * LANES, L), F32), pltpu.VMEM((2 * LANES, C), F32)],
        compiler_params=_cp(("arbitrary",)),
    )(q, kv, kvc, sink, dpa, cos, sa, sb, cos, sa, sb)


def _gelu_parts(x):
    th = jnp.tanh(SQRT_2_OVER_PI * (x + GELU_C * x * x * x))
    return 0.5 * x * (1.0 + th), th


def _gelu_grad(x, th):
    return 0.5 * (1.0 + th) + 0.5 * x * (1.0 - th * th) * SQRT_2_OVER_PI * (1.0 + 3.0 * GELU_C * x * x)


def _layernorm(v):
    mu = jnp.mean(v, axis=-1, keepdims=True)
    vc = v - mu
    rstd = lax.rsqrt(jnp.mean(vc * vc, axis=-1, keepdims=True) + EPS)
    return vc * rstd, rstd


def sgu_fwd(z1, ln_g, ln_b, ws, bst, *, name):
    L, W2 = z1.shape
    W = W2 // 2
    ng = W // LANES

    def body(z_ref, g_ref, b_ref, ws_ref, bs_ref, o_ref):
        z, _ = _gelu_parts(z_ref[...].astype(F32))
        xhat, _ = _layernorm(z[:, W:])
        vln = (xhat * g_ref[...] + b_ref[...]).astype(BF16)
        for gi in range(ng):
            cs = slice(gi * LANES, (gi + 1) * LANES)
            s = _dot(ws_ref[gi], vln[:, cs]) + bs_ref[:, gi:gi + 1]
            o_ref[:, cs] = (z[:, cs] * s).astype(BF16)

    vec = _full((1, W))
    return pl.pallas_call(
        body, name=name, grid=(L // BLK,),
        in_specs=[pl.BlockSpec((BLK, W2), lambda n: (n, 0)), vec, vec, _full((ng, LANES, LANES)), _full((BLK, ng))],
        out_specs=pl.BlockSpec((BLK, W), lambda n: (n, 0)),
        out_shape=jax.ShapeDtypeStruct((L, W), BF16),
        compiler_params=_cp(("parallel",)),
    )(z1, ln_g, ln_b, ws, bst)


def sgu_bwd(z1, dus, ln_g, ln_b, ws, wst, bst, *, name):
    L, W2 = z1.shape
    W = W2 // 2
    ng = W // LANES

    def body(z_ref, d_ref, g_ref, b_ref, ws_ref, wst_ref, bs_ref, dz_ref, dws_ref, dbs_ref, dg_ref, db_ref, dv_scr):
        @pl.when(pl.program_id(0) == 0)
        def _():
            dws_ref[...] = jnp.zeros_like(dws_ref)
            dbs_ref[...] = jnp.zeros_like(dbs_ref)
            dg_ref[...] = jnp.zeros_like(dg_ref)
            db_ref[...] = jnp.zeros_like(db_ref)

        zp = z_ref[...].astype(F32)
        z, th = _gelu_parts(zp)
        xhat, rstd = _layernorm(z[:, W:])
        vln = (xhat * g_ref[...] + b_ref[...]).astype(BF16)
        d = d_ref[...].astype(F32)
        lane = lax.broadcasted_iota(jnp.int32, (1, LANES), 1)
        dbs = jnp.zeros((BLK, LANES), F32)
        dgel = _gelu_grad(zp, th)
        for gi in range(ng):
            cs = slice(gi * LANES, (gi + 1) * LANES)
            s = _dot(ws_ref[gi], vln[:, cs]) + bs_ref[:, gi:gi + 1]
            dz_ref[:, cs] = (d[:, cs] * s * dgel[:, cs]).astype(BF16)
            ds = d[:, cs] * z[:, cs]
            dbs = dbs + jnp.where(lane == gi, jnp.sum(ds, axis=-1, keepdims=True), 0.0)
            dsb = ds.astype(BF16)
            dws_ref[gi] += _dot_nt(dsb, vln[:, cs])
            dv_scr[:, cs] = _dot(wst_ref[gi], dsb)
        dbs_ref[...] += dbs
        dvln = dv_scr[...]
        dg_ref[...] += _colsum(dvln * xhat)
        db_ref[...] += _colsum(dvln)
        dxh = dvln * g_ref[...]
        dv = rstd * (dxh - jnp.mean(dxh, axis=-1, keepdims=True) - xhat * jnp.mean(dxh * xhat, axis=-1, keepdims=True))
        dz_ref[:, W:] = (dv * dgel[:, W:]).astype(BF16)

    vec = _full((1, W))
    return pl.pallas_call(
        body, name=name, grid=(L // BLK,),
        in_specs=[pl.BlockSpec((BLK, W2), lambda n: (n, 0)), pl.BlockSpec((BLK, W), lambda n: (n, 0)), vec, vec,
                  _full((ng, LANES, LANES)), _full((ng, LANES, LANES)), _full((BLK, ng))],
        out_specs=[pl.BlockSpec((BLK, W2), lambda n: (n, 0)), _full((ng, LANES, LANES)), _full((BLK, LANES)), vec, vec],
        out_shape=[jax.ShapeDtypeStruct((L, W2), BF16), jax.ShapeDtypeStruct((ng, LANES, LANES), F32),
                   jax.ShapeDtypeStruct((BLK, LANES), F32), jax.ShapeDtypeStruct((1, W), F32), jax.ShapeDtypeStruct((1, W), F32)],
        scratch_shapes=[pltpu.VMEM((BLK, W), F32)],
        compiler_params=_cp(("arbitrary",)),
    )(z1, dus, ln_g, ln_b, ws, wst, bst)


def _adamw_math(w, m, v, g):
    m_ = ADAM_B1 * m + (1.0 - ADAM_B1) * g
    v_ = ADAM_B2 * v + (1.0 - ADAM_B2) * (g * g)
    return -ADAM_LR * ((m_ / BC1) / (jnp.sqrt(v_ / BC2) + ADAM_EPS) + ADAM_WD * w), m_, v_


def adamw(w, m, v, gparts, *, tr, name):
    NL, R, Wd = w.shape
    nr = R // tr

    def body(w_ref, m_ref, v_ref, *rest):
        gp_refs, (g_ref, d_ref, nm_ref, nv_ref) = rest[:NL], rest[NL:]
        for l in range(NL):
            @pl.when(pl.program_id(0) == l)
            def _():
                g = gp_refs[l][0].astype(F32)
                for s in range(1, gp_refs[l].shape[0]):
                    g = g + gp_refs[l][s].astype(F32)
                g_ref[...] = g
                d_ref[...], nm_ref[...], nv_ref[...] = _adamw_math(w_ref[...], m_ref[...], v_ref[...], g)

    row = pl.BlockSpec((None, tr, Wd), lambda l, i: (l, i, 0))
    gspecs = [pl.BlockSpec((gparts[l].shape[0], tr, Wd), (lambda l_, i, l=l: (0, jnp.clip(i + (l_ - l) * nr, 0, nr - 1), 0)))
              for l in range(NL)]
    return pl.pallas_call(
        body, name=name, grid=(NL, nr),
        in_specs=[row, row, row] + gspecs, out_specs=[row] * 4, out_shape=[jax.ShapeDtypeStruct((NL, R, Wd), F32)] * 4,
        compiler_params=_cp(("arbitrary", "arbitrary")),
    )(w, m, v, *gparts)


def small_update(gpacks, me, params, loss_row, *, name):
    n = len(params)

    def body(me_ref, gp_ref, *refs):
        ins, outs, gs_ref = refs[:3 * n], refs[3 * n:-1], refs[-1]
        gs_ref[...] = gp_ref[0].astype(F32)
        for dv in range(1, N_DEV):
            gs_ref[...] += gp_ref[dv].astype(F32)
        for p, (w, _, _, off, per_dev) in enumerate(params):
            w_ref, m_ref, v_ref = ins[3 * p:3 * p + 3]
            g_ref, d_ref, nm_ref, nv_ref = outs[4 * p:4 * p + 4]
            rows, cols = w.shape
            if cols == LANES and rows % 8 == 0 and not per_dev:
                g = gs_ref[off:off + rows, :]
                g_ref[...] = g
                d_ref[...], nm_ref[...], nv_ref[...] = _adamw_math(w_ref[...], m_ref[...], v_ref[...], g)
                continue
            chunks = -(-cols // LANES)
            base = off + me_ref[0] * per_dev if per_dev else off
            for i in range(rows):
                for j in range(chunks):
                    wd = min(LANES, cols - j * LANES)
                    at = (slice(i, i + 1), slice(j * LANES, j * LANES + wd))
                    g = gs_ref[pl.ds(base + i * chunks + j, 1), 0:wd]
                    g_ref[at] = g
                    d_ref[at], nm_ref[at], nv_ref[at] = _adamw_math(w_ref[at], m_ref[at], v_ref[at], g)
        outs[-1][...] = jnp.sum(gs_ref[loss_row:loss_row + 1, :], axis=1, keepdims=True)

    flat = [a for w, m, v, _, _ in params for a in (w, m, v)]
    out_shape = [jax.ShapeDtypeStruct(w.shape, F32) for w, _, _, _, _ in params for _ in range(4)] + [jax.ShapeDtypeStruct((1, 1), F32)]
    return pl.pallas_call(
        body, name=name, grid=(1,),
        in_specs=[pl.BlockSpec(memory_space=pltpu.SMEM), _full(gpacks.shape)] + [_full(a.shape) for a in flat],
        out_specs=[_full(o.shape) for o in out_shape], out_shape=out_shape,
        scratch_shapes=[pltpu.VMEM(gpacks.shape[1:], F32)],
        compiler_params=_cp(("arbitrary",)),
    )(me, gpacks, *flat)


def ada_fwd_mm(cs, w_ada, b_loc, *, name):
    R, D = cs.shape
    nl, _, n = w_ada.shape

    def body(c_ref, w_ref, b_ref, s_ref, m_ref):
        c = c_ref[...]
        s = c * jax.nn.sigmoid(c)
        s_ref[...] = s
        for i in range(nl):
            m_ref[i] = _dot(s.astype(BF16), w_ref[i].astype(BF16)) + b_ref[i:i + 1, :]

    return pl.pallas_call(
        body, name=name, in_specs=[_full((R, D)), _full((nl, D, n)), _full((nl, n))],
        out_specs=[_full((R, D)), _full((nl, R, n))], grid=(1,),
        out_shape=[jax.ShapeDtypeStruct((R, D), F32), jax.ShapeDtypeStruct((nl, R, n), F32)],
        compiler_params=_cp(("arbitrary",)),
    )(cs, w_ada, b_loc)


def ada_bwd_mm(s, c_ctx, dall, w_ada, *, name):
    R, D = s.shape
    nl, _, n = w_ada.shape

    def body(s_ref, cc_ref, d_ref, w_ref, gw_ref, dcc_ref):
        sb = s_ref[...].astype(BF16)
        row = lax.broadcasted_iota(jnp.int32, (R, 1), 0)
        dctx = d_ref[0, 1:2, :]
        for dv in range(1, N_DEV):
            dctx = dctx + d_ref[dv, 1:2, :]
        for i in range(nl):
            dm = jnp.zeros((R, n), F32)
            for dv in range(N_DEV):
                dm = dm + jnp.where(row == dv, d_ref[dv, 2 * i:2 * i + 1, :], 0.0)
            if i == 0:
                dm = dm + jnp.where(row == N_DEV, dctx, 0.0)
            gw_ref[i] = _dot_tn(sb, dm.astype(BF16))
        cc = cc_ref[...]
        sg = jax.nn.sigmoid(cc)
        ds = _dot_nt(jnp.broadcast_to(dctx, (8, n)).astype(BF16), w_ref[0].astype(BF16))
        dcc_ref[...] = ds * (sg * (1.0 + cc * (1.0 - sg)))

    return pl.pallas_call(
        body, name=name, grid=(1,),
        in_specs=[_full((R, D)), _full((1, D)), _full((N_DEV, 3, n)), _full((nl, D, n))],
        out_specs=[_full((nl, D, n)), _full((8, D))],
        out_shape=[jax.ShapeDtypeStruct((nl, D, n), F32), jax.ShapeDtypeStruct((8, D), F32)],
        compiler_params=_cp(("arbitrary",)),
    )(s, c_ctx, dall, w_ada)


def _place():
    x, y, c = lax.axis_index("x"), lax.axis_index("y"), lax.axis_index("c")
    return x, y, c


def _lin(p):
    return 4 * p[0] + 2 * p[1] + p[2]


def all_gather_small(xb, *, name):
    R, W = xb.shape

    def body(x_ref, out_ref, send_sems, recv_sems, local_sem):
        x, y, c = _place()
        me = _lin((x, y, c))
        mine = pltpu.make_async_copy(x_ref, out_ref.at[me], local_sem)
        mine.start()
        copies = []
        for k in range(1, N_DEV):
            peer = (x ^ (k >> 2), y ^ ((k >> 1) & 1), c ^ (k & 1))
            mk = lambda dst, k=k, peer=peer: pltpu.make_async_remote_copy(
                src_ref=x_ref, dst_ref=dst, send_sem=send_sems.at[k - 1], recv_sem=recv_sems.at[k - 1], device_id=peer, device_id_type=MESH)
            mk(out_ref.at[me]).start()
            copies.append(mk(out_ref.at[_lin(peer)]))
        for cp in copies:
            cp.wait_recv()
        for cp in copies:
            cp.wait_send()
        mine.wait()

    vm = pl.BlockSpec(memory_space=pltpu.VMEM)
    return pl.pallas_call(
        body, name=name, in_specs=[vm], out_specs=vm, out_shape=jax.ShapeDtypeStruct((N_DEV, R, W), xb.dtype),
        scratch_shapes=[pltpu.SemaphoreType.DMA((7,)), pltpu.SemaphoreType.DMA((7,)), pltpu.SemaphoreType.DMA],
        compiler_params=pltpu.CompilerParams(vmem_limit_bytes=VMEM_LIMIT),
    )(xb)


HBM_SPEC = pl.BlockSpec(memory_space=pltpu.HBM)
SEM_SPEC = pl.BlockSpec(memory_space=pltpu.SEMAPHORE)
ORDERED_EFFECT = pltpu.SideEffectType.DATAFLOW_SIDE_EFFECTING


def _exchange_copies(srcs, lands, sems, scatter):
    x, y, c = _place()
    me = _lin((x, y, c))
    for j in range(len(srcs)):
        r = lands[j].shape[0] // N_DEV
        block = lambda d, j=j, r=r: pl.ds(pl.multiple_of(d * r, 16), r)
        for k in range(1, N_DEV):
            peer = (x ^ (k >> 2), y ^ ((k >> 1) & 1), c ^ (k & 1))
            src = srcs[j].at[block(_lin(peer)), :] if scatter else srcs[j]
            mk = lambda dst, j=j, k=k, peer=peer, src=src: pltpu.make_async_remote_copy(
                src_ref=src, dst_ref=dst, send_sem=sems[2 * j].at[k - 1], recv_sem=sems[2 * j + 1].at[k - 1],
                device_id=peer, device_id_type=MESH)
            yield mk(lands[j].at[block(me), :]), mk(lands[j].at[block(_lin(peer)), :])


def exchange_start(srcs, lands, *, scatter, name):
    nw = len(srcs)

    def body(*refs):
        for start, _ in _exchange_copies(refs[:nw], refs[nw:2 * nw], refs[2 * nw:4 * nw], scatter):
            start.start()
        refs[-1][...] = jnp.zeros_like(refs[-1])

    thru = [pltpu.HBM(a.shape, a.dtype) for a in (*srcs, *lands)]
    res = pl.pallas_call(
        body, name=name, in_specs=[HBM_SPEC] * (2 * nw),
        out_specs=[SEM_SPEC] * (2 * nw) + [HBM_SPEC] * (2 * nw) + [pl.BlockSpec(memory_space=pltpu.VMEM)],
        out_shape=[pltpu.SemaphoreType.DMA((N_DEV - 1,))] * (2 * nw) + thru + [jax.ShapeDtypeStruct((8, LANES), F32)],
        input_output_aliases={i: 2 * nw + i for i in range(2 * nw)},
        compiler_params=pltpu.CompilerParams(has_side_effects=ORDERED_EFFECT),
    )(*[pltpu.with_memory_space_constraint(a, pltpu.HBM) for a in (*srcs, *lands)])
    return res[:2 * nw], res[2 * nw:3 * nw], res[3 * nw:4 * nw], res[-1]


def exchange_wait(srcs, lands, sems, after, *, scatter, name):
    nw = len(srcs)
    after = list(after) if isinstance(after, (list, tuple)) else [after]

    def body(*refs):
        for _, arrive in _exchange_copies(refs[:nw], refs[nw:2 * nw], refs[2 * nw:4 * nw], scatter):
            arrive.wait_send()
            arrive.wait_recv()

    res = pl.pallas_call(
        body, name=name, in_specs=[HBM_SPEC] * (2 * nw) + [SEM_SPEC] * (2 * nw) + [pl.BlockSpec(memory_space=pl.ANY)] * len(after),
        out_specs=[HBM_SPEC] * (2 * nw), out_shape=[pltpu.HBM(a.shape, a.dtype) for a in (*srcs, *lands)],
        input_output_aliases={i: i for i in range(2 * nw)},
        compiler_params=pltpu.CompilerParams(has_side_effects=ORDERED_EFFECT),
    )(*srcs, *lands, *sems, *after)
    return res[nw:]


def place_own(srcs, rows, me, *, scatter, name):
    nw = len(srcs)
    lands = [lax.empty((N_DEV * r, s_.shape[1]), s_.dtype) for r, s_ in zip(rows, srcs)]

    def body(me_ref, *refs):
        for j in range(nw):
            refs[2 * nw + j][...] = refs[j][...]

    mine = lambda i, me_ref: (me_ref[0], 0)
    src_at = mine if scatter else (lambda i, me_ref: (0, 0))
    blocks = [(r, s_.shape[1]) for r, s_ in zip(rows, srcs)]
    return pl.pallas_call(
        body, name=name,
        grid_spec=pltpu.PrefetchScalarGridSpec(
            num_scalar_prefetch=1, grid=(1,),
            in_specs=[pl.BlockSpec(b_, src_at) for b_ in blocks] + [pl.BlockSpec(memory_space=pl.ANY)] * nw,
            out_specs=[pl.BlockSpec(b_, mine) for b_ in blocks]),
        out_shape=[jax.ShapeDtypeStruct(l_.shape, l_.dtype) for l_ in lands],
        input_output_aliases={1 + nw + j: j for j in range(nw)},
        compiler_params=_cp(("arbitrary",)),
    )(jnp.reshape(me, (1,)).astype(jnp.int32), *srcs, *lands)


def _rope_tables(L):
    t = jnp.arange(L)
    inv = ROPE_BASE ** (-jnp.arange(ROPE_FREQS, dtype=F32) / ROPE_FREQS)
    ar = (t // GRID_W).astype(F32)[:, None] * inv
    ac = (t % GRID_W).astype(F32)[:, None] * inv
    z = jnp.zeros_like(ar)
    cos = jnp.concatenate([jnp.cos(ar), jnp.cos(ar), jnp.cos(ac), jnp.cos(ac)], axis=1)
    sa = jnp.concatenate([-jnp.sin(ar), z, -jnp.sin(ac), z], axis=1)
    sb = jnp.concatenate([z, jnp.sin(ar), z, jnp.sin(ac)], axis=1)
    return tuple(jnp.tile(a, (1, LANES // HEAD_DIM)) for a in (cos, sa, sb))


def _nat2d(a):
    return a.reshape(1, -1) if a.ndim == 1 else a.reshape(-1, a.shape[-1])


def _pack_rows(a):
    rows, cols = a.shape
    chunks = -(-cols // LANES)
    f = jnp.pad(a, ((0, 0), (0, chunks * LANES - cols))).reshape(rows * chunks, LANES)
    return jnp.pad(f, ((0, -f.shape[0] % 8), (0, 0)))


def _rows128(a):
    f = a.reshape(-1)
    n = -(-f.shape[0] // (8 * LANES)) * 8 * LANES
    return jnp.pad(f, (0, n - f.shape[0])).reshape(-1, LANES)


def kernel(x, c, ctx, c_ctx, w_ada, b_ada, g_mix_pre, g_mix_post, g_ffn_pre, g_ffn_post, w_in_even, w_pool, pool_scale, attn_sink, w_out_even, w_in_odd, sgu_ln_g, sgu_ln_b, sgu_w, sgu_b, w_out_odd, w_ffn_up, ffn_conv_w, ffn_conv_b, w_ffn_down, loss_target, m_c_ctx, m_w_ada, m_b_ada, m_g_mix_pre, m_g_mix_post, m_g_ffn_pre, m_g_ffn_post, m_w_in_even, m_w_pool, m_pool_scale, m_attn_sink, m_w_out_even, m_w_in_odd, m_sgu_ln_g, m_sgu_ln_b, m_sgu_w, m_sgu_b, m_w_out_odd, m_w_ffn_up, m_ffn_conv_w, m_ffn_conv_b, m_w_ffn_down, v_c_ctx, v_w_ada, v_b_ada, v_g_mix_pre, v_g_mix_post, v_g_ffn_pre, v_g_ffn_post, v_w_in_even, v_w_pool, v_pool_scale, v_attn_sink, v_w_out_even, v_w_in_odd, v_sgu_ln_g, v_sgu_ln_b, v_sgu_w, v_sgu_b, v_w_out_odd, v_w_ffn_up, v_ffn_conv_w, v_ffn_conv_b, v_w_ffn_down):
    P = dict(c_ctx=c_ctx, w_ada=w_ada, b_ada=b_ada, g_mix_pre=g_mix_pre, g_mix_post=g_mix_post, g_ffn_pre=g_ffn_pre,
             g_ffn_post=g_ffn_post, w_in_even=w_in_even, w_pool=w_pool, pool_scale=pool_scale, attn_sink=attn_sink,
             w_out_even=w_out_even, w_in_odd=w_in_odd, sgu_ln_g=sgu_ln_g, sgu_ln_b=sgu_ln_b, sgu_w=sgu_w, sgu_b=sgu_b,
             w_out_odd=w_out_odd, w_ffn_up=w_ffn_up, ffn_conv_w=ffn_conv_w, ffn_conv_b=ffn_conv_b, w_ffn_down=w_ffn_down)
    M = dict(c_ctx=m_c_ctx, w_ada=m_w_ada, b_ada=m_b_ada, g_mix_pre=m_g_mix_pre, g_mix_post=m_g_mix_post, g_ffn_pre=m_g_ffn_pre,
             g_ffn_post=m_g_ffn_post, w_in_even=m_w_in_even, w_pool=m_w_pool, pool_scale=m_pool_scale, attn_sink=m_attn_sink,
             w_out_even=m_w_out_even, w_in_odd=m_w_in_odd, sgu_ln_g=m_sgu_ln_g, sgu_ln_b=m_sgu_ln_b, sgu_w=m_sgu_w, sgu_b=m_sgu_b,
             w_out_odd=m_w_out_odd, w_ffn_up=m_w_ffn_up, ffn_conv_w=m_ffn_conv_w, ffn_conv_b=m_ffn_conv_b, w_ffn_down=m_w_ffn_down)
    V = dict(c_ctx=v_c_ctx, w_ada=v_w_ada, b_ada=v_b_ada, g_mix_pre=v_g_mix_pre, g_mix_post=v_g_mix_post, g_ffn_pre=v_g_ffn_pre,
             g_ffn_post=v_g_ffn_post, w_in_even=v_w_in_even, w_pool=v_w_pool, pool_scale=v_pool_scale, attn_sink=v_attn_sink,
             w_out_even=v_w_out_even, w_in_odd=v_w_in_odd, sgu_ln_g=v_sgu_ln_g, sgu_ln_b=v_sgu_ln_b, sgu_w=v_sgu_w, sgu_b=v_sgu_b,
             w_out_odd=v_w_out_odd, w_ffn_up=v_w_ffn_up, ffn_conv_w=v_ffn_conv_w, ffn_conv_b=v_ffn_conv_b, w_ffn_down=v_w_ffn_down)

    x = x[0]
    ctx = ctx[0]
    target = loss_target[0]
    L, D = x.shape
    C = ctx.shape[0]
    tm = min(512, L)
    tm_up = min(1024, L)
    conv_rows = min(512, L)
    me = 4 * lax.axis_index("x") + 2 * lax.axis_index("y") + lax.axis_index("c")
    n_ada = w_ada.shape[2]
    F = w_ffn_down.shape[1] * N_DEV
    half_f = F // 2

    n_cw = ffn_conv_w.shape[2]
    small = jnp.concatenate([_rows128(c), _rows128(sgu_ln_g), _rows128(sgu_ln_b), _rows128(ffn_conv_w)], axis=0)
    small_all = all_gather_small(small, name="gather_small_inputs")
    c_all = small_all[:, :8].reshape(N_DEV, D)
    ln_g = small_all[:, 8].reshape(1, D)
    ln_b = small_all[:, 16].reshape(1, D)
    conv_w = small_all[:, 24:].reshape(N_DEV, -1)[:, :2 * 3 * n_cw].reshape(N_DEV, 2, 3, n_cw)
    conv_w = conv_w.transpose(1, 2, 0, 3).reshape(2, 3, 2 * F)

    cs = jnp.concatenate([c_all, c_ctx[None, :], jnp.zeros((7, D), F32)], axis=0)
    b_loc = lax.dynamic_slice(b_ada, (0, me * n_ada), (2, n_ada))
    silu_c, mods_loc = ada_fwd_mm(cs, w_ada, b_loc, name="ada_fwd")
    mods_all = all_gather_small(mods_loc.reshape(-1, LANES), name="gather_mods")

    shards = [s.astype(BF16) for s in (w_in_even[0].T, w_out_even[0], w_ffn_up[0].T, w_ffn_down[0],
                                       w_in_odd[0].T, w_out_odd[0], w_ffn_up[1].T, w_ffn_down[1])]
    shards, mods_all = lax.optimization_barrier((shards, mods_all))
    w_sems, w_srcs, w_lands, _ = exchange_start(shards, place_own(shards, [s.shape[0] for s in shards], me, scatter=False, name="gather_own"),
                                              scatter=False, name="gather_start")

    def weight(j, after):
        return exchange_wait([w_srcs[j]], [w_lands[j]], w_sems[2 * j:2 * j + 2], after, scatter=False, name=f"gather_wait_{j}")[0]

    mods_all = mods_all.reshape(N_DEV, 2, 16, n_ada).transpose(1, 2, 0, 3).reshape(2, 16, 6 * D)
    mod = lambda i, row: [m_[None, :] for m_ in jnp.split(lax.dynamic_index_in_dim(mods_all[i], row, 0, False), 6)]
    sh_m, sc_m, gt_m, sh_f, sc_f, gt_f = zip(mod(0, me), mod(1, me))
    csh_m, csc_m = mod(0, N_DEV)[:2]

    row = lambda a, i: a[i][None, :]

    cos, sa, sb = _rope_tables(L)
    sink = attn_sink[0]
    bst = sgu_b[0].T
    sgu_wb, sgu_wtb = sgu_w[0].astype(BF16), sgu_w[0].swapaxes(1, 2).astype(BF16)
    wup, wdn = [None, None], [None, None]

    def ffn_fwd(i, xin):
        wup[i] = weight(2 + 4 * i, xin)
        h, hu = pre_mm(xin, row(g_ffn_pre, i), sh_f[i], sc_f[i], wup[i], tm=tm_up, tn=half_f, name=f"ffn_up_{i}")
        a, s1, s2 = conv_fwd(hu, conv_w[i], ffn_conv_b[i][None, :], rows=conv_rows, wblk=2 * LANES, name=f"ffn_conv_{i}")
        wdn[i] = weight(3 + 4 * i, a)
        res = mm_post([a], wdn[i], xin, row(g_ffn_post, i), gt_f[i], tm=tm, target=target if i == 1 else None, name=f"ffn_down_{i}")
        return (h, (hu, s1, s2), a, *res)

    first_mod, cos, sa, sb = lax.optimization_barrier((sh_m[0], cos, sa, sb))
    win_e = permute_heads(weight(0, first_mod))
    h0, u, q, kv = inproj_even(x, row(g_mix_pre, 0), sh_m[0], sc_m[0], win_e, cos, sa, sb, tm=tm, name="in_even")
    hc, kvc = pre_mm(ctx, row(g_mix_pre, 0), csh_m, csc_m, win_e, tm=C, tn=2 * LANES, w_row_off=8 * LANES, name="in_even_ctx")
    pa = [pool_fwd(u, w_pool[0], pool_scale, name="pool_fwd"), attn_fwd(q, kv, kvc, sink, name="attn_fwd")]
    wout_e = permute_heads(weight(1, pa[1]))
    y0, x1 = mm_post(pa, wout_e, x, row(g_mix_post, 0), gt_m[0], tm=tm, name="out_even")
    h1, hu0, a0, f0, x2 = ffn_fwd(0, x1)
    win_o = weight(4, x2)
    h2, z1 = pre_mm(x2, row(g_mix_pre, 1), sh_m[1], sc_m[1], win_o, tm=tm_up, tn=D, name="in_odd")
    us = sgu_fwd(z1, ln_g, ln_b, sgu_wb, bst, name="sgu_fwd")
    wout_o = weight(5, us)
    y1, x3 = mm_post([us], wout_o, x2, row(g_mix_post, 1), gt_m[1], tm=tm, name="out_odd")
    h3, hu1, a1, f1, dx4, loss_part = ffn_fwd(1, x3)

    g_srcs, g_lands, g_sems = [], [], []

    def scatter(grads, nm):
        own = place_own(grads, [g.shape[0] // N_DEV for g in grads], me, scatter=True, name=nm.replace("start", "own"))
        sems, srcs, lands, tok = exchange_start(grads, own, scatter=True, name=nm)
        g_srcs.extend(srcs)
        g_lands.extend(lands)
        g_sems.extend(sems)
        return tok[0:1, 0:1]

    def ffn_bwd(i, dxo, xin, h, hu, a, f, g_post):
        dyf, da, dg_post, dgt = post_bwd_mm(dxo, f, g_post, gt_f[i], wdn[i], tm=tm, name=f"ffn_down_bwd_{i}")
        dhg, dhu, dcwg, dcwu, dcbg, dcbu = conv_bwd(da, hu[1], hu[2], hu[0], conv_w[i], rows=conv_rows, wblk=2 * LANES,
                                                    name=f"ffn_conv_bwd_{i}")
        dxin, dg_pre, dsh, dsc = mm_pre_bwd([dhg, dhu], wup[i], xin, dxo, row(g_ffn_pre, i), sc_f[i], tm=tm, tk=half_f,
                                            name=f"ffn_up_bwd_{i}")
        g_dn = wgrad([a], dyf, tr=2 * LANES, name=f"wgrad_down_{i}")
        g_up = wgrad([dhg, dhu], h, tr=2 * LANES, name=f"wgrad_up_{i}")
        tok = scatter([g_dn, g_up], f"scatter_start_ffn_{i}")
        return dxin, tok, dict(g_ffn_post=dg_post, g_ffn_pre=dg_pre, gt_f=dgt, sh_f=dsh, sc_f=dsc,
                               ffn_conv_w=jnp.concatenate([dcwg, dcwu], axis=1), ffn_conv_b=jnp.concatenate([dcbg, dcbu], axis=1)[0])

    dx3, tok, sf1 = ffn_bwd(1, dx4, x3, h3, hu1, a1, f1, row(g_ffn_post, 1))
    dy1, dus, dg_mpost1, dgt_m1 = post_bwd_mm(dx3, y1, row(g_mix_post, 1) + tok, gt_m[1], wout_o, tm=tm, name="out_odd_bwd")
    dz1, dws, dbs, dlng, dlnb = sgu_bwd(z1, dus, ln_g, ln_b, sgu_wb, sgu_wtb, bst, name="sgu_bwd")
    dx2, dg_mpre1, dsh_m1, dsc_m1 = mm_pre_bwd([dz1], win_o, x2, dx3, row(g_mix_pre, 1), sc_m[1], tm=tm, tk=2 * D, name="in_odd_bwd")
    tok = scatter([wgrad([us], dy1, tr=2 * LANES, name="wgrad_out_odd"), wgrad([dz1], h2, tr=2 * LANES, name="wgrad_in_odd")],
                  "scatter_start_mix_1")

    dx1, tok, sf0 = ffn_bwd(0, dx2, x1, h1, hu0, a0, f0, row(g_ffn_post, 0) + tok)
    dy0, dpa, dg_mpost0, dgt_m0 = post_bwd_mm(dx1, y0, row(g_mix_post, 0) + tok, gt_m[0], wout_e, tm=tm, name="out_even_bwd")
    tok = scatter([permute_heads(wgrad(pa, dy0, tr=2 * LANES, name="wgrad_out_even"), inverse=True)], "scatter_start_out_0")
    du, dwp, dps = pool_bwd(u, dpa, w_pool[0], pool_scale + tok, name="pool_bwd")
    dq, dkv, dkvc, dsink = attn_bwd(q, kv, kvc, sink, dpa, cos, sa, sb, name="attn_bwd")
    dz0 = jnp.concatenate([du, dq, dkv], axis=1)
    dzc = jnp.concatenate([jnp.zeros((C, 8 * LANES), BF16), dkvc], axis=1)
    tok = scatter([permute_heads(wgrad([dz0], h0, tr=2 * LANES, extra=(dzc, hc), name="wgrad_in_even"), inverse=True)],
                  "scatter_start_in_0")
    grad_x, dg_mpre0, dsh_m0, dsc_m0 = mm_pre_bwd([dz0], win_e, x, dx1, row(g_mix_pre, 0) + tok, sc_m[0], tm=tm, tk=dz0.shape[1],
                                                  name="in_even_bwd")
    _, dg_mpre0c, dcsh, dcsc = mm_pre_bwd([dkvc], win_e, ctx, None, row(g_mix_pre, 0), csc_m, tm=C, tk=2 * LANES,
                                          w_row_off=8 * LANES, name="in_even_ctx_bwd")

    out, ran = {}, {}

    def update(name, lands, transposed):
        w_, m_, v_ = (a.transpose(0, 2, 1) if transposed else a for a in (P[name], M[name], V[name]))
        r = w_.shape[1]
        tr = r // 4 if r % 64 == 0 and r > 256 else r
        res = adamw(w_, m_, v_, [l_.reshape(N_DEV, r, l_.shape[1]) for l_ in lands], tr=tr, name=f"adamw_{name}")
        ran[name] = res[0]
        for kind, val in zip(("grad", "delta", "new_m", "new_v"), res):
            out[(kind, name)] = val.transpose(0, 2, 1) if transposed else val

    zero = jnp.zeros((1, D), F32)
    dmod0 = jnp.concatenate([dsh_m0, dsc_m0, dgt_m0, sf0["sh_f"], sf0["sc_f"], sf0["gt_f"]], axis=1)
    dmodc = jnp.concatenate([dcsh, dcsc, zero, zero, zero, zero], axis=1)
    dmod1 = jnp.concatenate([dsh_m1, dsc_m1, dgt_m1, sf1["sh_f"], sf1["sc_f"], sf1["gt_f"]], axis=1)
    dmods = jnp.concatenate([dmod0, dmodc, dmod1], axis=0)
    dm = dmods.reshape(-1, LANES).astype(BF16)
    d_sems, d_srcs, d_lands, d_tok = exchange_start(
        [dm], place_own([dm], [dm.shape[0]], me, scatter=False, name="dmods_own"), scatter=False, name="dmods_start")
    slots = exchange_wait(g_srcs[:6], g_lands[:6], g_sems[:12], d_tok, scatter=True, name="scatter_wait_early")
    early = slots
    update("w_ffn_down", [slots[4], slots[0]], False)
    update("w_in_odd", [slots[3]], True)
    update("w_out_odd", [slots[2]], False)
    updated = lambda names: [ran[k] for k in names]
    dmods_all = exchange_wait(d_srcs, d_lands, d_sems, updated(("w_out_odd",)), scatter=False, name="dmods_wait")[0]
    dall = lax.dynamic_index_in_dim(dmods_all.astype(F32).reshape(N_DEV, 3, N_DEV, n_ada), me, 2, False)
    g_w_ada, dcc = ada_bwd_mm(silu_c, c_ctx[None, :], dall, w_ada, name="ada_bwd")

    rep = dict(
        c_ctx=dcc[0:1],
        b_ada=jnp.concatenate([dmod0 + dmodc, dmod1]),
        g_mix_pre=jnp.concatenate([dg_mpre0 + dg_mpre0c, dg_mpre1]),
        g_mix_post=jnp.concatenate([dg_mpost0, dg_mpost1]),
        g_ffn_pre=jnp.concatenate([sf0["g_ffn_pre"], sf1["g_ffn_pre"]]),
        g_ffn_post=jnp.concatenate([sf0["g_ffn_post"], sf1["g_ffn_post"]]),
        w_pool=_nat2d(dwp), pool_scale=dps, attn_sink=dsink[:, :N_Q_HEADS],
        sgu_w=_nat2d(dws), sgu_b=dbs[:, :sgu_b.shape[1]].T,
        ffn_conv_b=jnp.stack([sf0["ffn_conv_b"], sf1["ffn_conv_b"]]),
    )
    hi = loss_part.astype(BF16).astype(F32)
    mid = (loss_part - hi).astype(BF16).astype(F32)
    loss_piece = jnp.pad(jnp.concatenate([hi, mid, loss_part - hi - mid], axis=1), ((0, 7), (0, LANES - 3)))
    conv_g = jnp.stack([sf0["ffn_conv_w"], sf1["ffn_conv_w"]]).reshape(2 * 3, N_DEV, n_cw).swapaxes(0, 1)
    shard_full = dict(sgu_ln_g=dlng.reshape(N_DEV, LANES), sgu_ln_b=dlnb.reshape(N_DEV, LANES),
                      ffn_conv_w=jnp.concatenate([_pack_rows(conv_g[d]) for d in range(N_DEV)], axis=0))
    small_names = list(rep) + list(shard_full)
    pieces = [_pack_rows(rep[k]) for k in rep] + list(shard_full.values()) + [loss_piece]
    sizes = [p.shape[0] for p in pieces]
    offs = [sum(sizes[:i]) for i in range(len(sizes))]
    pieces.append(jnp.zeros((-sum(sizes) % 16, LANES), F32))
    gpack = jnp.concatenate(pieces, axis=0).astype(BF16)
    own = place_own([gpack], [gpack.shape[0]], me, scatter=False, name="smallgrad_own")
    s_sems, s_srcs, s_lands, small_tok = exchange_start([gpack], own, scatter=False, name="smallgrad_start")

    slots = exchange_wait(g_srcs[6:], g_lands[6:], g_sems[12:], small_tok, scatter=True, name="scatter_wait_late")
    update("w_in_even", [slots[1]], True)
    update("w_out_even", [slots[0]], False)
    update("w_ffn_up", [early[5], early[1]], True)
    res = adamw(w_ada, m_w_ada, v_w_ada, [g_w_ada[l][None] for l in range(w_ada.shape[0])], tr=D // 4, name="adamw_w_ada")
    ran["w_ada"] = res[0]
    for kind, val in zip(("grad", "delta", "new_m", "new_v"), res):
        out[(kind, "w_ada")] = val

    gpacks = exchange_wait(s_srcs, s_lands, s_sems, updated(("w_ada",)), scatter=False,
                           name="smallgrad_wait")[0]
    per_dev = {k: shard_full[k].shape[0] // N_DEV for k in shard_full}
    params = [(_nat2d(P[k]), _nat2d(M[k]), _nat2d(V[k]), offs[i], per_dev.get(k, 0)) for i, k in enumerate(small_names)]
    res = small_update(gpacks.reshape(N_DEV, -1, LANES), jnp.reshape(me, (1,)).astype(jnp.int32), params, offs[-1], name="adamw_small")
    for i, k in enumerate(small_names):
        for kind, val in zip(("grad", "delta", "new_m", "new_v"), res[4 * i:4 * i + 4]):
            out[(kind, k)] = val.reshape(P[k].shape)
    loss = res[-1][0, 0]

    names = list(P)
    final = [loss, grad_x[None]]
    for kind in ("grad", "delta", "new_m", "new_v"):
        for k in names:
            val = out[(kind, k)]
            final.append(val)
    return tuple(final)
```

```python
import functools
import math

import jax
import jax.numpy as jnp
from jax import lax
from jax.experimental import pallas as pl
from jax.experimental.pallas import tpu as pltpu

F32 = jnp.float32
BF16 = jnp.bfloat16
MESH = pl.DeviceIdType.MESH
N_DEV = 8
LANES = 128
VMEM_LIMIT = 48 * 1024 * 1024
EPS = 1e-6
NEG_INF = -1e30
GRID_W = 64
WINDOW = 128
BLK = 128
HEAD_DIM = 64
N_Q_HEADS = 8
N_KV_HEADS = 2
GQA = N_Q_HEADS // N_KV_HEADS
POOL_WINDOWS = (2, 4, 8, 16)
ROPE_BASE = 10000.0
ROPE_FREQS = HEAD_DIM // 4
PAD = 16
ADAM_LR, ADAM_B1, ADAM_B2, ADAM_EPS, ADAM_WD, ADAM_STEP = 0.001, 0.9, 0.999, 1e-08, 0.01, 10
BC1 = 1.0 - ADAM_B1 ** ADAM_STEP
BC2 = 1.0 - ADAM_B2 ** ADAM_STEP
SQRT_2_OVER_PI = math.sqrt(2.0 / math.pi)
GELU_C = 0.044715


def _cp(sem=None):
    return pltpu.CompilerParams(dimension_semantics=sem, vmem_limit_bytes=VMEM_LIMIT)


def _dot(a, b):
    return jnp.dot(a, b, preferred_element_type=F32)


def _dot_nt(a, b):
    return lax.dot_general(a, b, (((1,), (1,)), ((), ())), preferred_element_type=F32)


def _dot_tn(a, b):
    return lax.dot_general(a, b, (((0,), (0,)), ((), ())), preferred_element_type=F32)


def _rms(x):
    r = lax.rsqrt(jnp.mean(x * x, axis=-1, keepdims=True) + EPS)
    return x * r, r


def _rms_bwd(dn, n, r):
    return r * (dn - n * jnp.mean(dn * n, axis=-1, keepdims=True))


def _colsum(a):
    return jnp.sum(a, axis=0, keepdims=True)


def _rope(x, c, sa, sb):
    return x * c + pltpu.roll(x, LANES - ROPE_FREQS, 1) * sa + pltpu.roll(x, ROPE_FREQS, 1) * sb


def _full(shape):
    return pl.BlockSpec(shape, lambda *_: (0,) * len(shape))


def pre_mm(x, g, sh, sc, wt, *, tm, tn, w_row_off=0, name):
    T, D = x.shape
    n_rows = wt.shape[0] - w_row_off
    off = w_row_off // tn

    def body(x_ref, g_ref, sh_ref, sc_ref, w_ref, h_ref, z_ref):
        @pl.when(pl.program_id(1) == 0)
        def _():
            n, _ = _rms(x_ref[...])
            h_ref[...] = (n * g_ref[...] * (1.0 + sc_ref[...]) + sh_ref[...]).astype(BF16)

        z_ref[...] = _dot_nt(h_ref[...], w_ref[...]).astype(BF16)

    vec = pl.BlockSpec((1, D), lambda i, j: (0, 0))
    return pl.pallas_call(
        body, name=name, grid=(T // tm, n_rows // tn),
        in_specs=[pl.BlockSpec((tm, D), lambda i, j: (i, 0)), vec, vec, vec, pl.BlockSpec((tn, D), lambda i, j: (j + off, 0))],
        out_specs=[pl.BlockSpec((tm, D), lambda i, j: (i, 0)), pl.BlockSpec((tm, tn), lambda i, j: (i, j))],
        out_shape=[jax.ShapeDtypeStruct((T, D), BF16), jax.ShapeDtypeStruct((T, n_rows), BF16)],
        compiler_params=_cp(("parallel", "arbitrary")),
    )(x, g, sh, sc, wt)


def inproj_even(x, g, sh, sc, wt, cos, sa, sb, *, tm, name):
    T, D = x.shape
    N = wt.shape[0]

    def body(x_ref, g_ref, sh_ref, sc_ref, w_ref, c_ref, sa_ref, sb_ref, h_ref, u_ref, q_ref, kv_ref):
        n, _ = _rms(x_ref[...])
        h = (n * g_ref[...] * (1.0 + sc_ref[...]) + sh_ref[...]).astype(BF16)
        h_ref[...] = h
        z = _dot_nt(h, w_ref[...])
        u_ref[...] = z[:, :4 * LANES]
        c, a, b = c_ref[...], sa_ref[...], sb_ref[...]
        for s in range(4):
            q_ref[:, s * LANES:(s + 1) * LANES] = _rope(z[:, (4 + s) * LANES:(5 + s) * LANES], c, a, b).astype(BF16)
        kv_ref[:, :LANES] = _rope(z[:, 8 * LANES:9 * LANES], c, a, b).astype(BF16)
        kv_ref[:, LANES:] = z[:, 9 * LANES:].astype(BF16)

    vec = pl.BlockSpec((1, D), lambda i: (0, 0))
    row = lambda w: pl.BlockSpec((tm, w), lambda i: (i, 0))
    return pl.pallas_call(
        body, name=name, grid=(T // tm,),
        in_specs=[row(D), vec, vec, vec, _full((N, D)), row(LANES), row(LANES), row(LANES)],
        out_specs=[row(D), row(4 * LANES), row(4 * LANES), row(2 * LANES)],
        out_shape=[jax.ShapeDtypeStruct((T, D), BF16), jax.ShapeDtypeStruct((T, 4 * LANES), F32),
                   jax.ShapeDtypeStruct((T, 4 * LANES), BF16), jax.ShapeDtypeStruct((T, 2 * LANES), BF16)],
        compiler_params=_cp(("parallel",)),
    )(x, g, sh, sc, wt, cos, sa, sb)


def mm_post(a_parts, w, x, g, gt, *, tm, target=None, name):
    T = a_parts[0].shape[0]
    D = w.shape[1]
    npart = len(a_parts)
    offs = [sum(a_.shape[1] for a_ in a_parts[:p]) for p in range(npart + 1)]
    with_loss = target is not None

    def body(*refs):
        a_refs, (w_ref, x_ref, g_ref, gt_ref) = refs[:npart], refs[npart:npart + 4]
        y = _dot(a_refs[0][...], w_ref[offs[0]:offs[1], :])
        for p in range(1, npart):
            y = y + _dot(a_refs[p][...], w_ref[offs[p]:offs[p + 1], :])
        n, _ = _rms(y)
        xn = x_ref[...] + gt_ref[...] * (n * g_ref[...])
        if not with_loss:
            y_ref, xn_ref = refs[npart + 4:]
            y_ref[...] = y.astype(BF16)
            xn_ref[...] = xn
            return
        t_ref, y_ref, d_ref, l_ref = refs[npart + 4:]
        y_ref[...] = y.astype(BF16)

        @pl.when(pl.program_id(0) == 0)
        def _():
            l_ref[...] = jnp.zeros_like(l_ref)

        e = xn - t_ref[...]
        l_ref[...] += 0.5 * jnp.sum(jnp.mean(e * e, axis=-1, keepdims=True), axis=0, keepdims=True)
        d_ref[...] = e * (1.0 / D)

    vec = pl.BlockSpec((1, D), lambda i: (0, 0))
    row = lambda w_: pl.BlockSpec((tm, w_), lambda i: (i, 0))
    in_specs = [row(a_.shape[1]) for a_ in a_parts] + [_full(w.shape), row(D), vec, vec]
    out_specs = [row(D), row(D)]
    out_shape = [jax.ShapeDtypeStruct((T, D), BF16), jax.ShapeDtypeStruct((T, D), F32)]
    if with_loss:
        in_specs.append(row(D))
        out_specs.append(_full((1, 1)))
        out_shape.append(jax.ShapeDtypeStruct((1, 1), F32))
    return pl.pallas_call(
        body, name=name, grid=(T // tm,), in_specs=in_specs, out_specs=out_specs, out_shape=out_shape,
        compiler_params=_cp(("arbitrary",) if with_loss else ("parallel",)),
    )(*a_parts, w, x, g, gt, *((target,) if with_loss else ()))


def post_bwd_mm(dxn, y, g, gt, w, *, tm, name):
    T, D = y.shape
    K = w.shape[0]

    def body(dxn_ref, y_ref, g_ref, gt_ref, w_ref, dy_ref, da_ref, dg_ref, dgt_ref):
        @pl.when(pl.program_id(0) == 0)
        def _():
            dg_ref[...] = jnp.zeros_like(dg_ref)
            dgt_ref[...] = jnp.zeros_like(dgt_ref)

        d = dxn_ref[...]
        n, r = _rms(y_ref[...].astype(F32))
        g_, gt_ = g_ref[...], gt_ref[...]
        dg_ref[...] += _colsum(d * gt_ * n)
        dgt_ref[...] += _colsum(d * g_ * n)
        dy = _rms_bwd(d * (gt_ * g_), n, r).astype(BF16)
        dy_ref[...] = dy
        da_ref[...] = _dot_nt(dy, w_ref[...]).astype(BF16)

    vec = pl.BlockSpec((1, D), lambda i: (0, 0))
    row = lambda w_: pl.BlockSpec((tm, w_), lambda i: (i, 0))
    return pl.pallas_call(
        body, name=name, grid=(T // tm,),
        in_specs=[row(D), row(D), vec, vec, _full((K, D))],
        out_specs=[row(D), row(K), vec, vec],
        out_shape=[jax.ShapeDtypeStruct((T, D), BF16), jax.ShapeDtypeStruct((T, K), BF16),
                   jax.ShapeDtypeStruct((1, D), F32), jax.ShapeDtypeStruct((1, D), F32)],
        compiler_params=_cp(("arbitrary",)),
    )(dxn, y, g, gt, w)


def mm_pre_bwd(dzs, wt, x, dres, g, sc, *, tm, w_row_off=0, name):
    T, N = dzs[0].shape
    D = x.shape[1]
    npart = len(dzs)
    off = w_row_off // N
    has_res = dres is not None

    def body(*refs):
        dz_refs = refs[:npart]
        w_refs = refs[npart:2 * npart]
        rest = refs[2 * npart:]
        x_ref = rest[0]
        dres_ref = rest[1] if has_res else None
        g_ref, sc_ref, dx_ref, dg_ref, dsh_ref, dsc_ref = rest[1 + has_res:]

        @pl.when(pl.program_id(0) == 0)
        def _():
            dg_ref[...] = jnp.zeros_like(dg_ref)
            dsh_ref[...] = jnp.zeros_like(dsh_ref)
            dsc_ref[...] = jnp.zeros_like(dsc_ref)

        dh = _dot(dz_refs[0][...], w_refs[0][...])
        for p in range(1, npart):
            dh = dh + _dot(dz_refs[p][...], w_refs[p][...])
        n, r = _rms(x_ref[...])
        g_, s1 = g_ref[...], 1.0 + sc_ref[...]
        dsh_ref[...] += _colsum(dh)
        dsc_ref[...] += _colsum(dh * n * g_)
        dg_ref[...] += _colsum(dh * s1 * n)
        dxp = _rms_bwd(dh * (g_ * s1), n, r)
        dx_ref[...] = dxp + dres_ref[...] if has_res else dxp

    vec = pl.BlockSpec((1, D), lambda i: (0, 0))
    row = pl.BlockSpec((tm, D), lambda i: (i, 0))
    w_specs = [pl.BlockSpec((N, D), (lambda i, p=p: (off + p, 0)), pipeline_mode=pl.Buffered(1)) for p in range(npart)]
    res_specs, res_args = ([row], (dres,)) if has_res else ([], ())
    return pl.pallas_call(
        body, name=name, grid=(T // tm,),
        in_specs=[pl.BlockSpec((tm, N), lambda i: (i, 0))] * npart + w_specs + [row] + res_specs + [vec, vec],
        out_specs=[row, vec, vec, vec],
        out_shape=[jax.ShapeDtypeStruct((T, D), F32)] + [jax.ShapeDtypeStruct((1, D), F32)] * 3,
        compiler_params=_cp(("arbitrary",)),
    )(*dzs, *([wt] * npart), x, *res_args, g, sc)


def wgrad(a_parts, b, *, tr, extra=None, name):
    T, R = a_parts[0].shape
    D = b.shape[1]
    npart = len(a_parts)
    nr = R // tr

    def body(*refs):
        a_refs, b_ref = refs[:npart], refs[npart]
        g_ref = refs[-1]
        for p in range(npart):
            @pl.when(pl.program_id(0) // nr == p)
            def _():
                acc = _dot_tn(a_refs[p][...], b_ref[...])
                if extra is not None:
                    acc += _dot_tn(refs[npart + 1][...], refs[npart + 2][...])
                g_ref[...] = acc.astype(BF16)

    in_specs = [pl.BlockSpec((T, tr), (lambda r, p=p: (0, jnp.clip(r - p * nr, 0, nr - 1)))) for p in range(npart)]
    in_specs.append(_full((T, D)))
    args = [*a_parts, b]
    if extra is not None:
        a2, b2 = extra
        in_specs += [pl.BlockSpec((a2.shape[0], tr), lambda r: (0, r)), _full(b2.shape)]
        args += [a2, b2]
    return pl.pallas_call(
        body, name=name, grid=(npart * nr,),
        in_specs=in_specs, out_specs=pl.BlockSpec((tr, D), lambda r: (r, 0)),
        out_shape=jax.ShapeDtypeStruct((npart * R, D), BF16),
        compiler_params=_cp(("parallel",)),
    )(*args)


def _conv_ext(ref, r0, rows, total):
    top = ref[pl.ds(pl.multiple_of(jnp.maximum(r0 - PAD, 0), PAD), PAD), :]
    mid = ref[pl.ds(r0, rows), :]
    bot = ref[pl.ds(pl.multiple_of(jnp.minimum(r0 + rows, total - PAD), PAD), PAD), :]
    top = jnp.where(r0 > 0, top, jnp.zeros_like(top))
    bot = jnp.where(r0 + rows < total, bot, jnp.zeros_like(bot))
    return jnp.concatenate([top, mid, bot], axis=0).astype(F32)


def _shift_rows(a, k):
    return pltpu.roll(a, k % a.shape[0], 0)


def _conv3(x, w, b):
    return w[0:1] * _shift_rows(x, 1) + w[1:2] * x + w[2:3] * _shift_rows(x, -1) + b


def _gate_up_specs(rows_, wblk, nb):
    return [pl.BlockSpec((rows_, wblk), lambda j: (0, j)), pl.BlockSpec((rows_, wblk), lambda j: (0, j + nb))]


def conv_fwd(hu, cw, cb, *, rows, wblk, name):
    L, N2 = hu.shape
    nb = N2 // 2 // wblk
    nchunk = L // rows

    def body(hg_ref, hu_ref, wg_ref, wu_ref, bg_ref, bu_ref, a_ref, s1_ref, s2_ref):
        def chunk(ci, carry):
            r0 = pl.multiple_of(ci * rows, rows)
            gate = _conv3(_conv_ext(hg_ref, r0, rows, L), wg_ref[...], bg_ref[...])[PAD:PAD + rows]
            up = _conv3(_conv_ext(hu_ref, r0, rows, L), wu_ref[...], bu_ref[...])[PAD:PAD + rows]
            sg = jax.nn.sigmoid(gate)
            silu = gate * sg
            at = pl.ds(r0, rows)
            a_ref[at, :] = (silu * up).astype(BF16)
            s1_ref[at, :] = silu.astype(BF16)
            s2_ref[at, :] = (up * (sg + silu * (1.0 - sg))).astype(BF16)
            return carry

        lax.fori_loop(0, nchunk, chunk, 0)

    out = pl.BlockSpec((L, wblk), lambda j: (0, j))
    return pl.pallas_call(
        body, name=name, grid=(nb,),
        in_specs=_gate_up_specs(L, wblk, nb) + _gate_up_specs(3, wblk, nb) + _gate_up_specs(1, wblk, nb),
        out_specs=[out] * 3, out_shape=[jax.ShapeDtypeStruct((L, N2 // 2), BF16)] * 3,
        compiler_params=_cp(("parallel",)),
    )(hu, hu, cw, cw, cb, cb)


def conv_bwd(da, s1, s2, hu, cw, *, rows, wblk, name):
    L, N2 = hu.shape
    F = N2 // 2
    nb = F // wblk
    nchunk = L // rows
    mid = slice(PAD, PAD + rows)

    def body(da_ref, s1_ref, s2_ref, hg_ref, hu_ref, wg_ref, wu_ref, dg_ref, du_ref, dwg_ref, dwu_ref, dbg_ref, dbu_ref):
        for ref in (dwg_ref, dwu_ref, dbg_ref, dbu_ref):
            ref[...] = jnp.zeros_like(ref)

        def half_bwd(x_ref, dh, w_ref, dx_ref, dw_ref, db_ref, r0):
            w = w_ref[...]
            nxt, prv = _shift_rows(dh, -1)[mid], _shift_rows(dh, 1)[mid]
            dhm, xm = dh[mid], x_ref[pl.ds(r0, rows), :].astype(F32)
            dx_ref[pl.ds(r0, rows), :] = (w[0:1] * nxt + w[1:2] * dhm + w[2:3] * prv).astype(BF16)
            db_ref[...] += _colsum(dhm)
            dw_ref[0:1, :] += _colsum(nxt * xm)
            dw_ref[1:2, :] += _colsum(dhm * xm)
            dw_ref[2:3, :] += _colsum(prv * xm)

        def chunk(ci, carry):
            r0 = pl.multiple_of(ci * rows, rows)
            d = _conv_ext(da_ref, r0, rows, L)
            half_bwd(hu_ref, d * _conv_ext(s1_ref, r0, rows, L), wu_ref, du_ref, dwu_ref, dbu_ref, r0)
            half_bwd(hg_ref, d * _conv_ext(s2_ref, r0, rows, L), wg_ref, dg_ref, dwg_ref, dbg_ref, r0)
            return carry

        lax.fori_loop(0, nchunk, chunk, 0)

    blk = lambda r: pl.BlockSpec((r, wblk), lambda j: (0, j))
    return pl.pallas_call(
        body, name=name, grid=(nb,),
        in_specs=[blk(L)] * 3 + _gate_up_specs(L, wblk, nb) + _gate_up_specs(3, wblk, nb),
        out_specs=[blk(L), blk(L), blk(3), blk(3), blk(1), blk(1)],
        out_shape=[jax.ShapeDtypeStruct((L, F), BF16)] * 2 + [jax.ShapeDtypeStruct((3, F), F32)] * 2
        + [jax.ShapeDtypeStruct((1, F), F32)] * 2,
        compiler_params=_cp(("parallel",)),
    )(da, s1, s2, hu, hu, cw, cw)


def _window_sums(pad_ref, w, lead):
    a = pad_ref[...]
    k = 1
    while k < w:
        a = a + _shift_rows(a, -k)
        k *= 2
    return _shift_rows(a, lead) if lead else a


def _pool_counts(L, h):
    t = lax.broadcasted_iota(jnp.int32, (L, 1), 0)
    return (jnp.minimum(t + h, L) - jnp.maximum(t - h, 0)).astype(F32)


def _pooled(u_ref, pad_ref, L, w):
    h = w // 2
    pad_ref[pl.ds(PAD, L), :] = u_ref[...]
    win = _window_sums(pad_ref, w, h)[PAD:PAD + L]
    return win / _pool_counts(L, h) - u_ref[...]


def _zero_pad_edges(pad_ref, L):
    z = jnp.zeros((PAD, LANES), F32)
    pad_ref[pl.ds(0, PAD), :] = z
    pad_ref[pl.ds(PAD + L, PAD), :] = z


def pool_fwd(u, w_pool, pool_scale, *, name):
    L = u.shape[0]

    def body(u_ref, w_ref, ps_ref, p_ref, pad_ref):
        _zero_pad_edges(pad_ref, L)
        for gi, win in enumerate(POOL_WINDOWS):
            @pl.when(pl.program_id(0) == gi)
            def _():
                pooled = _pooled(u_ref, pad_ref, L, win)
                p_ref[...] = (_dot(pooled.astype(BF16), w_ref[...].astype(BF16)) * ps_ref[...]).astype(BF16)

    return pl.pallas_call(
        body, name=name, grid=(len(POOL_WINDOWS),),
        in_specs=[pl.BlockSpec((L, LANES), lambda gi: (0, gi)), pl.BlockSpec((None, LANES, LANES), lambda gi: (gi, 0, 0)),
                  pl.BlockSpec((1, LANES), lambda gi: (0, gi))],
        out_specs=pl.BlockSpec((L, LANES), lambda gi: (0, gi)),
        out_shape=jax.ShapeDtypeStruct((L, 4 * LANES), BF16),
        scratch_shapes=[pltpu.VMEM((L + 2 * PAD, LANES), F32)],
        compiler_params=_cp(("parallel",)),
    )(u, w_pool, pool_scale)


def pool_bwd(u, dpa, w_pool, pool_scale, *, name):
    L = u.shape[0]

    def body(u_ref, dp_ref, w_ref, ps_ref, du_ref, dw_ref, dps_ref, pad_ref):
        _zero_pad_edges(pad_ref, L)
        for gi, win in enumerate(POOL_WINDOWS):
            @pl.when(pl.program_id(0) == gi)
            def _():
                h = win // 2
                wb = w_ref[...].astype(BF16)
                pooled = _pooled(u_ref, pad_ref, L, win).astype(BF16)
                dp = dp_ref[...].astype(F32)
                dps_ref[...] = _colsum(dp * _dot(pooled, wb))
                dy = (dp * ps_ref[...]).astype(BF16)
                dw_ref[...] = _dot_tn(pooled, dy)
                dpooled = _dot_nt(dy, wb)
                pad_ref[pl.ds(PAD, L), :] = dpooled / _pool_counts(L, h)
                du_ref[...] = (_window_sums(pad_ref, win, h - 1)[PAD:PAD + L] - dpooled).astype(BF16)

    return pl.pallas_call(
        body, name=name, grid=(len(POOL_WINDOWS),),
        in_specs=[pl.BlockSpec((L, LANES), lambda gi: (0, gi)), pl.BlockSpec((L, LANES), lambda gi: (0, gi)),
                  pl.BlockSpec((None, LANES, LANES), lambda gi: (gi, 0, 0)), pl.BlockSpec((1, LANES), lambda gi: (0, gi))],
        out_specs=[pl.BlockSpec((L, LANES), lambda gi: (0, gi)), pl.BlockSpec((None, LANES, LANES), lambda gi: (gi, 0, 0)),
                   pl.BlockSpec((1, LANES), lambda gi: (0, gi))],
        out_shape=[jax.ShapeDtypeStruct((L, 4 * LANES), BF16), jax.ShapeDtypeStruct((4, LANES, LANES), F32),
                   jax.ShapeDtypeStruct((1, 4 * LANES), F32)],
        scratch_shapes=[pltpu.VMEM((L + 2 * PAD, LANES), F32)],
        compiler_params=_cp(("parallel",)),
    )(u, dpa, w_pool, pool_scale)


def _attn_probs(qk, band_k, ctx_k, sink_ref, kh, mask4):
    s_loc = jnp.where(mask4, _dot_nt(qk, band_k), NEG_INF)
    s_ctx = _dot_nt(qk, ctx_k)
    sk = jnp.concatenate([jnp.full((BLK, 1), sink_ref[kh * GQA + hh], F32) for hh in range(GQA)], axis=0)
    m = jnp.maximum(jnp.maximum(jnp.max(s_loc, axis=-1, keepdims=True), jnp.max(s_ctx, axis=-1, keepdims=True)), sk)
    e_loc, e_ctx, e_s = jnp.exp(s_loc - m), jnp.exp(s_ctx - m), jnp.exp(sk - m)
    inv = 1.0 / (jnp.sum(e_loc, axis=-1, keepdims=True) + jnp.sum(e_ctx, axis=-1, keepdims=True) + e_s)
    return e_loc * inv, e_ctx * inv, e_s * inv


def _attn_block(n, L):
    start = pl.multiple_of(jnp.clip((n - 1) * BLK, 0, L - 3 * BLK), BLK)
    qpos = n * BLK + lax.broadcasted_iota(jnp.int32, (BLK, 3 * BLK), 0)
    kpos = start + lax.broadcasted_iota(jnp.int32, (BLK, 3 * BLK), 1)
    mask = jnp.abs(kpos - qpos) <= WINDOW
    return start, jnp.concatenate([mask] * GQA, axis=0)


def _stack_slabs(ref):
    return jnp.concatenate([ref[:, s * LANES:(s + 1) * LANES] for s in range(GQA)], axis=0)


def _kv_head_lanes(kh):
    return (lax.broadcasted_iota(jnp.int32, (1, LANES), 1) // HEAD_DIM) == kh


def permute_heads(w, inverse=False):
    lo, hi = 4 * LANES, 8 * LANES
    mid = w[lo:hi].reshape(*((GQA, N_KV_HEADS) if inverse else (N_KV_HEADS, GQA)), HEAD_DIM, w.shape[1])
    return jnp.concatenate([w[:lo], mid.swapaxes(0, 1).reshape(hi - lo, w.shape[1]), w[hi:]], axis=0)


def attn_fwd(q, kv, kvc, sink, *, name):
    L = q.shape[0]
    C = kvc.shape[0]
    scale = HEAD_DIM ** -0.5

    def body(q_ref, kv_ref, kvc_ref, sink_ref, o_ref):
        start, mask4 = _attn_block(pl.program_id(0), L)
        band = kv_ref[pl.ds(start, 3 * BLK), :]
        kvc_ = kvc_ref[...]
        qs = _stack_slabs(q_ref) * scale
        o = jnp.zeros((GQA * BLK, LANES), F32)
        for kh in range(N_KV_HEADS):
            grp = _kv_head_lanes(kh)
            qk = jnp.where(grp, qs, jnp.zeros_like(qs))
            p_loc, p_ctx, _ = _attn_probs(qk, band[:, :LANES], kvc_[:, :LANES], sink_ref, kh, mask4)
            o = o + jnp.where(grp, _dot(p_loc.astype(BF16), band[:, LANES:]) + _dot(p_ctx.astype(BF16), kvc_[:, LANES:]), 0.0)
        for s in range(GQA):
            o_ref[:, s * LANES:(s + 1) * LANES] = o[s * BLK:(s + 1) * BLK].astype(BF16)

    return pl.pallas_call(
        body, name=name, grid=(L // BLK,),
        in_specs=[pl.BlockSpec((BLK, 4 * LANES), lambda n: (n, 0)), _full((L, 2 * LANES)), _full((C, 2 * LANES)),
                  pl.BlockSpec(memory_space=pltpu.SMEM)],
        out_specs=pl.BlockSpec((BLK, 4 * LANES), lambda n: (n, 0)),
        out_shape=jax.ShapeDtypeStruct((L, 4 * LANES), BF16),
        compiler_params=_cp(("parallel",)),
    )(q, kv, kvc, sink)


def attn_bwd(q, kv, kvc, sink, dpa, cos, sa, sb, *, name):
    L = q.shape[0]
    C = kvc.shape[0]
    nb = L // BLK
    scale = HEAD_DIM ** -0.5

    def body(q_ref, kv_ref, kvc_ref, sink_ref, do_ref, c_ref, sa_ref, sb_ref, cq_ref, saq_ref, sbq_ref,
             dq_ref, dkv_ref, dkvc_ref, dsink_ref, dkv_acc, dkvc_acc):
        n = pl.program_id(0)

        @pl.when(n == 0)
        def _():
            dkv_acc[...] = jnp.zeros_like(dkv_acc)
            dkvc_acc[...] = jnp.zeros_like(dkvc_acc)
            dsink_ref[...] = jnp.zeros_like(dsink_ref)

        start, mask4 = _attn_block(n, L)
        band = kv_ref[pl.ds(start, 3 * BLK), :]
        kvc_ = kvc_ref[...]
        band_k, band_v, ctx_k, ctx_v = band[:, :LANES], band[:, LANES:], kvc_[:, :LANES], kvc_[:, LANES:]
        qs = _stack_slabs(q_ref) * scale
        dos = _stack_slabs(do_ref)
        lane = lax.broadcasted_iota(jnp.int32, (1, LANES), 1)
        dsink = jnp.zeros((1, LANES), F32)
        dq = jnp.zeros((GQA * BLK, LANES), F32)
        dk = jnp.zeros((LANES, 3 * BLK), F32)
        dv = jnp.zeros((LANES, 3 * BLK), F32)
        dkc = jnp.zeros((LANES, C), F32)
        dvc = jnp.zeros((LANES, C), F32)
        for kh in range(N_KV_HEADS):
            grp = _kv_head_lanes(kh)
            qk = jnp.where(grp, qs, jnp.zeros_like(qs))
            dok = jnp.where(grp, dos, jnp.zeros_like(dos))
            p_loc, p_ctx, p_s = _attn_probs(qk, band_k, ctx_k, sink_ref, kh, mask4)
            dp_loc = _dot_nt(dok, band_v)
            dp_ctx = _dot_nt(dok, ctx_v)
            delta = jnp.sum(p_loc * dp_loc, axis=-1, keepdims=True) + jnp.sum(p_ctx * dp_ctx, axis=-1, keepdims=True)
            ds_loc = (p_loc * (dp_loc - delta)).astype(BF16)
            ds_ctx = (p_ctx * (dp_ctx - delta)).astype(BF16)
            dsk = p_s * delta
            for hh in range(GQA):
                dsink = dsink - jnp.where(lane == kh * GQA + hh, jnp.sum(dsk[hh * BLK:(hh + 1) * BLK], axis=0, keepdims=True), 0.0)
            dq = dq + jnp.where(grp, _dot(ds_loc, band_k) + _dot(ds_ctx, ctx_k), 0.0)
            dk = dk + _dot_tn(qk, ds_loc)
            dv = dv + _dot_tn(dok, p_loc.astype(BF16))
            dkc = dkc + _dot_tn(qk, ds_ctx)
            dvc = dvc + _dot_tn(dok, p_ctx.astype(BF16))
        dsink_ref[...] += dsink
        dkv_acc[:LANES, pl.ds(start, 3 * BLK)] += dk
        dkv_acc[LANES:, pl.ds(start, 3 * BLK)] += dv
        dkvc_acc[:LANES, :] += dkc
        dkvc_acc[LANES:, :] += dvc
        c, a, b = cq_ref[...], -saq_ref[...], -sbq_ref[...]
        for s in range(GQA):
            dq_ref[:, s * LANES:(s + 1) * LANES] = _rope(dq[s * BLK:(s + 1) * BLK] * scale, c, a, b).astype(BF16)

        @pl.when(n == nb - 1)
        def _():
            dkv_ref[:, :LANES] = _rope(dkv_acc[:LANES, :].T, c_ref[...], -sa_ref[...], -sb_ref[...]).astype(BF16)
            dkv_ref[:, LANES:] = dkv_acc[LANES:, :].T.astype(BF16)
            dkvc_ref[...] = dkvc_acc[...].T.astype(BF16)

    blk = lambda w: pl.BlockSpec((BLK, w), lambda n: (n, 0))
    return pl.pallas_call(
        body, name=name, grid=(nb,),
        in_specs=[blk(4 * LANES), _full((L, 2 * LANES)), _full((C, 2 * LANES)), pl.BlockSpec(memory_space=pltpu.SMEM),
                  pl.BlockSpec((BLK, 4 * LANES), lambda n: (n, 1)),
                  _full((L, LANES)), _full((L, LANES)), _full((L, LANES)), blk(LANES), blk(LANES), blk(LANES)],
        out_specs=[blk(4 * LANES), _full((L, 2 * LANES)), _full((C, 2 * LANES)), _full((1, LANES))],
        out_shape=[jax.ShapeDtypeStruct((L, 4 * LANES), BF16), jax.ShapeDtypeStruct((L, 2 * LANES), BF16),
                   jax.ShapeDtypeStruct((C, 2 * LANES), BF16), jax.ShapeDtypeStruct((1, LANES), F32)],
        scratch_shapes=[pltpu.VMEM((2 * LANES, L), F32), pltpu.VMEM((2 * LANES, C), F32)],
        compiler_params=_cp(("arbitrary",)),
    )(q, kv, kvc, sink, dpa, cos, sa, sb, cos, sa, sb)


def _gelu_parts(x):
    th = jnp.tanh(SQRT_2_OVER_PI * (x + GELU_C * x * x * x))
    return 0.5 * x * (1.0 + th), th


def _gelu_grad(x, th):
    return 0.5 * (1.0 + th) + 0.5 * x * (1.0 - th * th) * SQRT_2_OVER_PI * (1.0 + 3.0 * GELU_C * x * x)


def _layernorm(v):
    mu = jnp.mean(v, axis=-1, keepdims=True)
    vc = v - mu
    rstd = lax.rsqrt(jnp.mean(vc * vc, axis=-1, keepdims=True) + EPS)
    return vc * rstd, rstd


def sgu_fwd(z1, ln_g, ln_b, ws, bst, *, name):
    L, W2 = z1.shape
    W = W2 // 2
    ng = W // LANES

    def body(z_ref, g_ref, b_ref, ws_ref, bs_ref, o_ref):
        z, _ = _gelu_parts(z_ref[...].astype(F32))
        xhat, _ = _layernorm(z[:, W:])
        vln = (xhat * g_ref[...] + b_ref[...]).astype(BF16)
        for gi in range(ng):
            cs = slice(gi * LANES, (gi + 1) * LANES)
            s = _dot(ws_ref[gi], vln[:, cs]) + bs_ref[:, gi:gi + 1]
            o_ref[:, cs] = (z[:, cs] * s).astype(BF16)

    vec = _full((1, W))
    return pl.pallas_call(
        body, name=name, grid=(L // BLK,),
        in_specs=[pl.BlockSpec((BLK, W2), lambda n: (n, 0)), vec, vec, _full((ng, LANES, LANES)), _full((BLK, ng))],
        out_specs=pl.BlockSpec((BLK, W), lambda n: (n, 0)),
        out_shape=jax.ShapeDtypeStruct((L, W), BF16),
        compiler_params=_cp(("parallel",)),
    )(z1, ln_g, ln_b, ws, bst)


def sgu_bwd(z1, dus, ln_g, ln_b, ws, wst, bst, *, name):
    L, W2 = z1.shape
    W = W2 // 2
    ng = W // LANES

    def body(z_ref, d_ref, g_ref, b_ref, ws_ref, wst_ref, bs_ref, dz_ref, dws_ref, dbs_ref, dg_ref, db_ref, dv_scr):
        @pl.when(pl.program_id(0) == 0)
        def _():
            dws_ref[...] = jnp.zeros_like(dws_ref)
            dbs_ref[...] = jnp.zeros_like(dbs_ref)
            dg_ref[...] = jnp.zeros_like(dg_ref)
            db_ref[...] = jnp.zeros_like(db_ref)

        zp = z_ref[...].astype(F32)
        z, th = _gelu_parts(zp)
        xhat, rstd = _layernorm(z[:, W:])
        vln = (xhat * g_ref[...] + b_ref[...]).astype(BF16)
        d = d_ref[...].astype(F32)
        lane = lax.broadcasted_iota(jnp.int32, (1, LANES), 1)
        dbs = jnp.zeros((BLK, LANES), F32)
        dgel = _gelu_grad(zp, th)
        for gi in range(ng):
            cs = slice(gi * LANES, (gi + 1) * LANES)
            s = _dot(ws_ref[gi], vln[:, cs]) + bs_ref[:, gi:gi + 1]
            dz_ref[:, cs] = (d[:, cs] * s * dgel[:, cs]).astype(BF16)
            ds = d[:, cs] * z[:, cs]
            dbs = dbs + jnp.where(lane == gi, jnp.sum(ds, axis=-1, keepdims=True), 0.0)
            dsb = ds.astype(BF16)
            dws_ref[gi] += _dot_nt(dsb, vln[:, cs])
            dv_scr[:, cs] = _dot(wst_ref[gi], dsb)
        dbs_ref[...] += dbs
        dvln = dv_scr[...]
        dg_ref[...] += _colsum(dvln * xhat)
        db_ref[...] += _colsum(dvln)
        dxh = dvln * g_ref[...]
        dv = rstd * (dxh - jnp.mean(dxh, axis=-1, keepdims=True) - xhat * jnp.mean(dxh * xhat, axis=-1, keepdims=True))
        dz_ref[:, W:] = (dv * dgel[:, W:]).astype(BF16)

    vec = _full((1, W))
    return pl.pallas_call(
        body, name=name, grid=(L // BLK,),
        in_specs=[pl.BlockSpec((BLK, W2), lambda n: (n, 0)), pl.BlockSpec((BLK, W), lambda n: (n, 0)), vec, vec,
                  _full((ng, LANES, LANES)), _full((ng, LANES, LANES)), _full((BLK, ng))],
        out_specs=[pl.BlockSpec((BLK, W2), lambda n: (n, 0)), _full((ng, LANES, LANES)), _full((BLK, LANES)), vec, vec],
        out_shape=[jax.ShapeDtypeStruct((L, W2), BF16), jax.ShapeDtypeStruct((ng, LANES, LANES), F32),
                   jax.ShapeDtypeStruct((BLK, LANES), F32), jax.ShapeDtypeStruct((1, W), F32), jax.ShapeDtypeStruct((1, W), F32)],
        scratch_shapes=[pltpu.VMEM((BLK, W), F32)],
        compiler_params=_cp(("arbitrary",)),
    )(z1, dus, ln_g, ln_b, ws, wst, bst)


def _adamw_math(w, m, v, g):
    m_ = ADAM_B1 * m + (1.0 - ADAM_B1) * g
    v_ = ADAM_B2 * v + (1.0 - ADAM_B2) * (g * g)
    return -ADAM_LR * ((m_ / BC1) / (jnp.sqrt(v_ / BC2) + ADAM_EPS) + ADAM_WD * w), m_, v_


def adamw(w, m, v, gparts, *, tr, name):
    NL, R, Wd = w.shape
    nr = R // tr

    def body(w_ref, m_ref, v_ref, *rest):
        gp_refs, (g_ref, d_ref, nm_ref, nv_ref) = rest[:NL], rest[NL:]
        for l in range(NL):
            @pl.when(pl.program_id(0) == l)
            def _():
                g = gp_refs[l][0].astype(F32)
                for s in range(1, gp_refs[l].shape[0]):
                    g = g + gp_refs[l][s].astype(F32)
                g_ref[...] = g
                d_ref[...], nm_ref[...], nv_ref[...] = _adamw_math(w_ref[...], m_ref[...], v_ref[...], g)

    row = pl.BlockSpec((None, tr, Wd), lambda l, i: (l, i, 0))
    gspecs = [pl.BlockSpec((gparts[l].shape[0], tr, Wd), (lambda l_, i, l=l: (0, jnp.clip(i + (l_ - l) * nr, 0, nr - 1), 0)))
              for l in range(NL)]
    return pl.pallas_call(
        body, name=name, grid=(NL, nr),
        in_specs=[row, row, row] + gspecs, out_specs=[row] * 4, out_shape=[jax.ShapeDtypeStruct((NL, R, Wd), F32)] * 4,
        compiler_params=_cp(("arbitrary", "arbitrary")),
    )(w, m, v, *gparts)


def small_update(gpacks, me, params, loss_row, *, name):
    n = len(params)

    def body(me_ref, gp_ref, *refs):
        ins, outs, gs_ref = refs[:3 * n], refs[3 * n:-1], refs[-1]
        gs_ref[...] = gp_ref[0].astype(F32)
        for dv in range(1, N_DEV):
            gs_ref[...] += gp_ref[dv].astype(F32)
        for p, (w, _, _, off, per_dev) in enumerate(params):
            w_ref, m_ref, v_ref = ins[3 * p:3 * p + 3]
            g_ref, d_ref, nm_ref, nv_ref = outs[4 * p:4 * p + 4]
            rows, cols = w.shape
            if cols == LANES and rows % 8 == 0 and not per_dev:
                g = gs_ref[off:off + rows, :]
                g_ref[...] = g
                d_ref[...], nm_ref[...], nv_ref[...] = _adamw_math(w_ref[...], m_ref[...], v_ref[...], g)
                continue
            chunks = -(-cols // LANES)
            base = off + me_ref[0] * per_dev if per_dev else off
            for i in range(rows):
                for j in range(chunks):
                    wd = min(LANES, cols - j * LANES)
                    at = (slice(i, i + 1), slice(j * LANES, j * LANES + wd))
                    g = gs_ref[pl.ds(base + i * chunks + j, 1), 0:wd]
                    g_ref[at] = g
                    d_ref[at], nm_ref[at], nv_ref[at] = _adamw_math(w_ref[at], m_ref[at], v_ref[at], g)
        outs[-1][...] = jnp.sum(gs_ref[loss_row:loss_row + 1, :], axis=1, keepdims=True)

    flat = [a for w, m, v, _, _ in params for a in (w, m, v)]
    out_shape = [jax.ShapeDtypeStruct(w.shape, F32) for w, _, _, _, _ in params for _ in range(4)] + [jax.ShapeDtypeStruct((1, 1), F32)]
    return pl.pallas_call(
        body, name=name, grid=(1,),
        in_specs=[pl.BlockSpec(memory_space=pltpu.SMEM), _full(gpacks.shape)] + [_full(a.shape) for a in flat],
        out_specs=[_full(o.shape) for o in out_shape], out_shape=out_shape,
        scratch_shapes=[pltpu.VMEM(gpacks.shape[1:], F32)],
        compiler_params=_cp(("arbitrary",)),
    )(me, gpacks, *flat)


def ada_fwd_mm(cs, w_ada, b_loc, *, name):
    R, D = cs.shape
    nl, _, n = w_ada.shape

    def body(c_ref, w_ref, b_ref, s_ref, m_ref):
        c = c_ref[...]
        s = c * jax.nn.sigmoid(c)
        s_ref[...] = s
        for i in range(nl):
            m_ref[i] = _dot(s.astype(BF16), w_ref[i].astype(BF16)) + b_ref[i:i + 1, :]

    return pl.pallas_call(
        body, name=name, in_specs=[_full((R, D)), _full((nl, D, n)), _full((nl, n))],
        out_specs=[_full((R, D)), _full((nl, R, n))], grid=(1,),
        out_shape=[jax.ShapeDtypeStruct((R, D), F32), jax.ShapeDtypeStruct((nl, R, n), F32)],
        compiler_params=_cp(("arbitrary",)),
    )(cs, w_ada, b_loc)


def ada_bwd_mm(s, c_ctx, dall, w_ada, *, name):
    R, D = s.shape
    nl, _, n = w_ada.shape

    def body(s_ref, cc_ref, d_ref, w_ref, gw_ref, dcc_ref):
        sb = s_ref[...].astype(BF16)
        row = lax.broadcasted_iota(jnp.int32, (R, 1), 0)
        dctx = d_ref[0, 1:2, :]
        for dv in range(1, N_DEV):
            dctx = dctx + d_ref[dv, 1:2, :]
        for i in range(nl):
            dm = jnp.zeros((R, n), F32)
            for dv in range(N_DEV):
                dm = dm + jnp.where(row == dv, d_ref[dv, 2 * i:2 * i + 1, :], 0.0)
            if i == 0:
                dm = dm + jnp.where(row == N_DEV, dctx, 0.0)
            gw_ref[i] = _dot_tn(sb, dm.astype(BF16))
        cc = cc_ref[...]
        sg = jax.nn.sigmoid(cc)
        ds = _dot_nt(jnp.broadcast_to(dctx, (8, n)).astype(BF16), w_ref[0].astype(BF16))
        dcc_ref[...] = ds * (sg * (1.0 + cc * (1.0 - sg)))

    return pl.pallas_call(
        body, name=name, grid=(1,),
        in_specs=[_full((R, D)), _full((1, D)), _full((N_DEV, 3, n)), _full((nl, D, n))],
        out_specs=[_full((nl, D, n)), _full((8, D))],
        out_shape=[jax.ShapeDtypeStruct((nl, D, n), F32), jax.ShapeDtypeStruct((8, D), F32)],
        compiler_params=_cp(("arbitrary",)),
    )(s, c_ctx, dall, w_ada)


def _place():
    x, y, c = lax.axis_index("x"), lax.axis_index("y"), lax.axis_index("c")
    return x, y, c


def _lin(p):
    return 4 * p[0] + 2 * p[1] + p[2]


def all_gather_small(xb, *, name):
    R, W = xb.shape

    def body(x_ref, out_ref, send_sems, recv_sems, local_sem):
        x, y, c = _place()
        me = _lin((x, y, c))
        mine = pltpu.make_async_copy(x_ref, out_ref.at[me], local_sem)
        mine.start()
        copies = []
        for k in range(1, N_DEV):
            peer = (x ^ (k >> 2), y ^ ((k >> 1) & 1), c ^ (k & 1))
            mk = lambda dst, k=k, peer=peer: pltpu.make_async_remote_copy(
                src_ref=x_ref, dst_ref=dst, send_sem=send_sems.at[k - 1], recv_sem=recv_sems.at[k - 1], device_id=peer, device_id_type=MESH)
            mk(out_ref.at[me]).start()
            copies.append(mk(out_ref.at[_lin(peer)]))
        for cp in copies:
            cp.wait_recv()
        for cp in copies:
            cp.wait_send()
        mine.wait()

    vm = pl.BlockSpec(memory_space=pltpu.VMEM)
    return pl.pallas_call(
        body, name=name, in_specs=[vm], out_specs=vm, out_shape=jax.ShapeDtypeStruct((N_DEV, R, W), xb.dtype),
        scratch_shapes=[pltpu.SemaphoreType.DMA((7,)), pltpu.SemaphoreType.DMA((7,)), pltpu.SemaphoreType.DMA],
        compiler_params=pltpu.CompilerParams(vmem_limit_bytes=VMEM_LIMIT),
    )(xb)


HBM_SPEC = pl.BlockSpec(memory_space=pltpu.HBM)
SEM_SPEC = pl.BlockSpec(memory_space=pltpu.SEMAPHORE)
ORDERED_EFFECT = pltpu.SideEffectType.DATAFLOW_SIDE_EFFECTING


def _exchange_copies(srcs, lands, sems, scatter):
    x, y, c = _place()
    me = _lin((x, y, c))
    for j in range(len(srcs)):
        r = lands[j].shape[0] // N_DEV
        block = lambda d, j=j, r=r: pl.ds(pl.multiple_of(d * r, 16), r)
        for k in range(1, N_DEV):
            peer = (x ^ (k >> 2), y ^ ((k >> 1) & 1), c ^ (k & 1))
            src = srcs[j].at[block(_lin(peer)), :] if scatter else srcs[j]
            mk = lambda dst, j=j, k=k, peer=peer, src=src: pltpu.make_async_remote_copy(
                src_ref=src, dst_ref=dst, send_sem=sems[2 * j].at[k - 1], recv_sem=sems[2 * j + 1].at[k - 1],
                device_id=peer, device_id_type=MESH)
            yield mk(lands[j].at[block(me), :]), mk(lands[j].at[block(_lin(peer)), :])


def exchange_start(srcs, lands, *, scatter, name):
    nw = len(srcs)

    def body(*refs):
        for start, _ in _exchange_copies(refs[:nw], refs[nw:2 * nw], refs[2 * nw:4 * nw], scatter):
            start.start()
        refs[-1][...] = jnp.zeros_like(refs[-1])

    thru = [pltpu.HBM(a.shape, a.dtype) for a in (*srcs, *lands)]
    res = pl.pallas_call(
        body, name=name, in_specs=[HBM_SPEC] * (2 * nw),
        out_specs=[SEM_SPEC] * (2 * nw) + [HBM_SPEC] * (2 * nw) + [pl.BlockSpec(memory_space=pltpu.VMEM)],
        out_shape=[pltpu.SemaphoreType.DMA((N_DEV - 1,))] * (2 * nw) + thru + [jax.ShapeDtypeStruct((8, LANES), F32)],
        input_output_aliases={i: 2 * nw + i for i in range(2 * nw)},
        compiler_params=pltpu.CompilerParams(has_side_effects=ORDERED_EFFECT),
    )(*[pltpu.with_memory_space_constraint(a, pltpu.HBM) for a in (*srcs, *lands)])
    return res[:2 * nw], res[2 * nw:3 * nw], res[3 * nw:4 * nw], res[-1]


def exchange_wait(srcs, lands, sems, after, *, scatter, name):
    nw = len(srcs)
    after = list(after) if isinstance(after, (list, tuple)) else [after]

    def body(*refs):
        for _, arrive in _exchange_copies(refs[:nw], refs[nw:2 * nw], refs[2 * nw:4 * nw], scatter):
            arrive.wait_send()
            arrive.wait_recv()

    res = pl.pallas_call(
        body, name=name, in_specs=[HBM_SPEC] * (2 * nw) + [SEM_SPEC] * (2 * nw) + [pl.BlockSpec(memory_space=pl.ANY)] * len(after),
        out_specs=[HBM_SPEC] * (2 * nw), out_shape=[pltpu.HBM(a.shape, a.dtype) for a in (*srcs, *lands)],
        input_output_aliases={i: i for i in range(2 * nw)},
        compiler_params=pltpu.CompilerParams(has_side_effects=ORDERED_EFFECT),
    )(*srcs, *lands, *sems, *after)
    return res[nw:]


def place_own(srcs, rows, me, *, scatter, name):
    nw = len(srcs)
    lands = [lax.empty((N_DEV * r, s_.shape[1]), s_.dtype) for r, s_ in zip(rows, srcs)]

    def body(me_ref, *refs):
        for j in range(nw):
            refs[2 * nw + j][...] = refs[j][...]

    mine = lambda i, me_ref: (me_ref[0], 0)
    src_at = mine if scatter else (lambda i, me_ref: (0, 0))
    blocks = [(r, s_.shape[1]) for r, s_ in zip(rows, srcs)]
    return pl.pallas_call(
        body, name=name,
        grid_spec=pltpu.PrefetchScalarGridSpec(
            num_scalar_prefetch=1, grid=(1,),
            in_specs=[pl.BlockSpec(b_, src_at) for b_ in blocks] + [pl.BlockSpec(memory_space=pl.ANY)] * nw,
            out_specs=[pl.BlockSpec(b_, mine) for b_ in blocks]),
        out_shape=[jax.ShapeDtypeStruct(l_.shape, l_.dtype) for l_ in lands],
        input_output_aliases={1 + nw + j: j for j in range(nw)},
        compiler_params=_cp(("arbitrary",)),
    )(jnp.reshape(me, (1,)).astype(jnp.int32), *srcs, *lands)


def _rope_tables(L):
    t = jnp.arange(L)
    inv = ROPE_BASE ** (-jnp.arange(ROPE_FREQS, dtype=F32) / ROPE_FREQS)
    ar = (t // GRID_W).astype(F32)[:, None] * inv
    ac = (t % GRID_W).astype(F32)[:, None] * inv
    z = jnp.zeros_like(ar)
    cos = jnp.concatenate([jnp.cos(ar), jnp.cos(ar), jnp.cos(ac), jnp.cos(ac)], axis=1)
    sa = jnp.concatenate([-jnp.sin(ar), z, -jnp.sin(ac), z], axis=1)
    sb = jnp.concatenate([z, jnp.sin(ar), z, jnp.sin(ac)], axis=1)
    return tuple(jnp.tile(a, (1, LANES // HEAD_DIM)) for a in (cos, sa, sb))


def _nat2d(a):
    return a.reshape(1, -1) if a.ndim == 1 else a.reshape(-1, a.shape[-1])


def _pack_rows(a):
    rows, cols = a.shape
    chunks = -(-cols // LANES)
    f = jnp.pad(a, ((0, 0), (0, chunks * LANES - cols))).reshape(rows * chunks, LANES)
    return jnp.pad(f, ((0, -f.shape[0] % 8), (0, 0)))


def _rows128(a):
    f = a.reshape(-1)
    n = -(-f.shape[0] // (8 * LANES)) * 8 * LANES
    return jnp.pad(f, (0, n - f.shape[0])).reshape(-1, LANES)


def kernel(x, c, ctx, c_ctx, w_ada, b_ada, g_mix_pre, g_mix_post, g_ffn_pre, g_ffn_post, w_in_even, w_pool, pool_scale, attn_sink, w_out_even, w_in_odd, sgu_ln_g, sgu_ln_b, sgu_w, sgu_b, w_out_odd, w_ffn_up, ffn_conv_w, ffn_conv_b, w_ffn_down, loss_target, m_c_ctx, m_w_ada, m_b_ada, m_g_mix_pre, m_g_mix_post, m_g_ffn_pre, m_g_ffn_post, m_w_in_even, m_w_pool, m_pool_scale, m_attn_sink, m_w_out_even, m_w_in_odd, m_sgu_ln_g, m_sgu_ln_b, m_sgu_w, m_sgu_b, m_w_out_odd, m_w_ffn_up, m_ffn_conv_w, m_ffn_conv_b, m_w_ffn_down, v_c_ctx, v_w_ada, v_b_ada, v_g_mix_pre, v_g_mix_post, v_g_ffn_pre, v_g_ffn_post, v_w_in_even, v_w_pool, v_pool_scale, v_attn_sink, v_w_out_even, v_w_in_odd, v_sgu_ln_g, v_sgu_ln_b, v_sgu_w, v_sgu_b, v_w_out_odd, v_w_ffn_up, v_ffn_conv_w, v_ffn_conv_b, v_w_ffn_down):
    P = dict(c_ctx=c_ctx, w_ada=w_ada, b_ada=b_ada, g_mix_pre=g_mix_pre, g_mix_post=g_mix_post, g_ffn_pre=g_ffn_pre,
             g_ffn_post=g_ffn_post, w_in_even=w_in_even, w_pool=w_pool, pool_scale=pool_scale, attn_sink=attn_sink,
             w_out_even=w_out_even, w_in_odd=w_in_odd, sgu_ln_g=sgu_ln_g, sgu_ln_b=sgu_ln_b, sgu_w=sgu_w, sgu_b=sgu_b,
             w_out_odd=w_out_odd, w_ffn_up=w_ffn_up, ffn_conv_w=ffn_conv_w, ffn_conv_b=ffn_conv_b, w_ffn_down=w_ffn_down)
    M = dict(c_ctx=m_c_ctx, w_ada=m_w_ada, b_ada=m_b_ada, g_mix_pre=m_g_mix_pre, g_mix_post=m_g_mix_post, g_ffn_pre=m_g_ffn_pre,
             g_ffn_post=m_g_ffn_post, w_in_even=m_w_in_even, w_pool=m_w_pool, pool_scale=m_pool_scale, attn_sink=m_attn_sink,
             w_out_even=m_w_out_even, w_in_odd=m_w_in_odd, sgu_ln_g=m_sgu_ln_g, sgu_ln_b=m_sgu_ln_b, sgu_w=m_sgu_w, sgu_b=m_sgu_b,
             w_out_odd=m_w_out_odd, w_ffn_up=m_w_ffn_up, ffn_conv_w=m_ffn_conv_w, ffn_conv_b=m_ffn_conv_b, w_ffn_down=m_w_ffn_down)
    V = dict(c_ctx=v_c_ctx, w_ada=v_w_ada, b_ada=v_b_ada, g_mix_pre=v_g_mix_pre, g_mix_post=v_g_mix_post, g_ffn_pre=v_g_ffn_pre,
             g_ffn_post=v_g_ffn_post, w_in_even=v_w_in_even, w_pool=v_w_pool, pool_scale=v_pool_scale, attn_sink=v_attn_sink,
             w_out_even=v_w_out_even, w_in_odd=v_w_in_odd, sgu_ln_g=v_sgu_ln_g, sgu_ln_b=v_sgu_ln_b, sgu_w=v_sgu_w, sgu_b=v_sgu_b,
             w_out_odd=v_w_out_odd, w_ffn_up=v_w_ffn_up, ffn_conv_w=v_ffn_conv_w, ffn_conv_b=v_ffn_conv_b, w_ffn_down=v_w_ffn_down)

    x = x[0]
    ctx = ctx[0]
    target = loss_target[0]
    L, D = x.shape
    C = ctx.shape[0]
    tm = min(512, L)
    tm_up = min(1024, L)
    conv_rows = min(512, L)
    me = 4 * lax.axis_index("x") + 2 * lax.axis_index("y") + lax.axis_index("c")
    n_ada = w_ada.shape[2]
    F = w_ffn_down.shape[1] * N_DEV
    half_f = F // 2

    n_cw = ffn_conv_w.shape[2]
    small = jnp.concatenate([_rows128(c), _rows128(sgu_ln_g), _rows128(sgu_ln_b), _rows128(ffn_conv_w)], axis=0)
    small_all = all_gather_small(small, name="gather_small_inputs")
    c_all = small_all[:, :8].reshape(N_DEV, D)
    ln_g = small_all[:, 8].reshape(1, D)
    ln_b = small_all[:, 16].reshape(1, D)
    conv_w = small_all[:, 24:].reshape(N_DEV, -1)[:, :2 * 3 * n_cw].reshape(N_DEV, 2, 3, n_cw)
    conv_w = conv_w.transpose(1, 2, 0, 3).reshape(2, 3, 2 * F)

    cs = jnp.concatenate([c_all, c_ctx[None, :], jnp.zeros((7, D), F32)], axis=0)
    b_loc = lax.dynamic_slice(b_ada, (0, me * n_ada), (2, n_ada))
    silu_c, mods_loc = ada_fwd_mm(cs, w_ada, b_loc, name="ada_fwd")
    mods_all = all_gather_small(mods_loc.reshape(-1, LANES), name="gather_mods")

    shards = [s.astype(BF16) for s in (w_in_even[0].T, w_out_even[0], w_ffn_up[0].T, w_ffn_down[0],
                                       w_in_odd[0].T, w_out_odd[0], w_ffn_up[1].T, w_ffn_down[1])]
    shards, mods_all = lax.optimization_barrier((shards, mods_all))
    w_sems, w_srcs, w_lands, _ = exchange_start(shards, place_own(shards, [s.shape[0] for s in shards], me, scatter=False, name="gather_own"),
                                              scatter=False, name="gather_start")

    def weight(j, after):
        return exchange_wait([w_srcs[j]], [w_lands[j]], w_sems[2 * j:2 * j + 2], after, scatter=False, name=f"gather_wait_{j}")[0]

    mods_all = mods_all.reshape(N_DEV, 2, 16, n_ada).transpose(1, 2, 0, 3).reshape(2, 16, 6 * D)
    mod = lambda i, row: [m_[None, :] for m_ in jnp.split(lax.dynamic_index_in_dim(mods_all[i], row, 0, False), 6)]
    sh_m, sc_m, gt_m, sh_f, sc_f, gt_f = zip(mod(0, me), mod(1, me))
    csh_m, csc_m = mod(0, N_DEV)[:2]

    row = lambda a, i: a[i][None, :]

    cos, sa, sb = _rope_tables(L)
    sink = attn_sink[0]
    bst = sgu_b[0].T
    sgu_wb, sgu_wtb = sgu_w[0].astype(BF16), sgu_w[0].swapaxes(1, 2).astype(BF16)
    wup, wdn = [None, None], [None, None]

    def ffn_fwd(i, xin):
        wup[i] = weight(2 + 4 * i, xin)
        h, hu = pre_mm(xin, row(g_ffn_pre, i), sh_f[i], sc_f[i], wup[i], tm=tm_up, tn=half_f, name=f"ffn_up_{i}")
        a, s1, s2 = conv_fwd(hu, conv_w[i], ffn_conv_b[i][None, :], rows=conv_rows, wblk=2 * LANES, name=f"ffn_conv_{i}")
        wdn[i] = weight(3 + 4 * i, a)
        res = mm_post([a], wdn[i], xin, row(g_ffn_post, i), gt_f[i], tm=tm, target=target if i == 1 else None, name=f"ffn_down_{i}")
        return (h, (hu, s1, s2), a, *res)

    first_mod, cos, sa, sb = lax.optimization_barrier((sh_m[0], cos, sa, sb))
    win_e = permute_heads(weight(0, first_mod))
    h0, u, q, kv = inproj_even(x, row(g_mix_pre, 0), sh_m[0], sc_m[0], win_e, cos, sa, sb, tm=tm, name="in_even")
    hc, kvc = pre_mm(ctx, row(g_mix_pre, 0), csh_m, csc_m, win_e, tm=C, tn=2 * LANES, w_row_off=8 * LANES, name="in_even_ctx")
    pa = [pool_fwd(u, w_pool[0], pool_scale, name="pool_fwd"), attn_fwd(q, kv, kvc, sink, name="attn_fwd")]
    wout_e = permute_heads(weight(1, pa[1]))
    y0, x1 = mm_post(pa, wout_e, x, row(g_mix_post, 0), gt_m[0], tm=tm, name="out_even")
    h1, hu0, a0, f0, x2 = ffn_fwd(0, x1)
    win_o = weight(4, x2)
    h2, z1 = pre_mm(x2, row(g_mix_pre, 1), sh_m[1], sc_m[1], win_o, tm=tm_up, tn=D, name="in_odd")
    us = sgu_fwd(z1, ln_g, ln_b, sgu_wb, bst, name="sgu_fwd")
    wout_o = weight(5, us)
    y1, x3 = mm_post([us], wout_o, x2, row(g_mix_post, 1), gt_m[1], tm=tm, name="out_odd")
    h3, hu1, a1, f1, dx4, loss_part = ffn_fwd(1, x3)

    g_srcs, g_lands, g_sems = [], [], []

    def scatter(grads, nm):
        own = place_own(grads, [g.shape[0] // N_DEV for g in grads], me, scatter=True, name=nm.replace("start", "own"))
        sems, srcs, lands, tok = exchange_start(grads, own, scatter=True, name=nm)
        g_srcs.extend(srcs)
        g_lands.extend(lands)
        g_sems.extend(sems)
        return tok[0:1, 0:1]

    def ffn_bwd(i, dxo, xin, h, hu, a, f, g_post):
        dyf, da, dg_post, dgt = post_bwd_mm(dxo, f, g_post, gt_f[i], wdn[i], tm=tm, name=f"ffn_down_bwd_{i}")
        dhg, dhu, dcwg, dcwu, dcbg, dcbu = conv_bwd(da, hu[1], hu[2], hu[0], conv_w[i], rows=conv_rows, wblk=2 * LANES,
                                                    name=f"ffn_conv_bwd_{i}")
        dxin, dg_pre, dsh, dsc = mm_pre_bwd([dhg, dhu], wup[i], xin, dxo, row(g_ffn_pre, i), sc_f[i], tm=tm,
                                            name=f"ffn_up_bwd_{i}")
        g_dn = wgrad([a], dyf, tr=2 * LANES, name=f"wgrad_down_{i}")
        g_up = wgrad([dhg, dhu], h, tr=2 * LANES, name=f"wgrad_up_{i}")
        tok = scatter([g_dn, g_up], f"scatter_start_ffn_{i}")
        return dxin, tok, dict(g_ffn_post=dg_post, g_ffn_pre=dg_pre, gt_f=dgt, sh_f=dsh, sc_f=dsc,
                               ffn_conv_w=jnp.concatenate([dcwg, dcwu], axis=1), ffn_conv_b=jnp.concatenate([dcbg, dcbu], axis=1)[0])

    dx3, tok, sf1 = ffn_bwd(1, dx4, x3, h3, hu1, a1, f1, row(g_ffn_post, 1))
    dy1, dus, dg_mpost1, dgt_m1 = post_bwd_mm(dx3, y1, row(g_mix_post, 1) + tok, gt_m[1], wout_o, tm=tm, name="out_odd_bwd")
    dz1, dws, dbs, dlng, dlnb = sgu_bwd(z1, dus, ln_g, ln_b, sgu_wb, sgu_wtb, bst, name="sgu_bwd")
    dx2, dg_mpre1, dsh_m1, dsc_m1 = mm_pre_bwd([dz1], win_o, x2, dx3, row(g_mix_pre, 1), sc_m[1], tm=tm, name="in_odd_bwd")
    tok = scatter([wgrad([us], dy1, tr=2 * LANES, name="wgrad_out_odd"), wgrad([dz1], h2, tr=2 * LANES, name="wgrad_in_odd")],
                  "scatter_start_mix_1")

    dx1, tok, sf0 = ffn_bwd(0, dx2, x1, h1, hu0, a0, f0, row(g_ffn_post, 0) + tok)
    dy0, dpa, dg_mpost0, dgt_m0 = post_bwd_mm(dx1, y0, row(g_mix_post, 0) + tok, gt_m[0], wout_e, tm=tm, name="out_even_bwd")
    tok = scatter([permute_heads(wgrad(pa, dy0, tr=2 * LANES, name="wgrad_out_even"), inverse=True)], "scatter_start_out_0")
    du, dwp, dps = pool_bwd(u, dpa, w_pool[0], pool_scale + tok, name="pool_bwd")
    dq, dkv, dkvc, dsink = attn_bwd(q, kv, kvc, sink, dpa, cos, sa, sb, name="attn_bwd")
    dz0 = jnp.concatenate([du, dq, dkv], axis=1)
    dzc = jnp.concatenate([jnp.zeros((C, 8 * LANES), BF16), dkvc], axis=1)
    tok = scatter([permute_heads(wgrad([dz0], h0, tr=2 * LANES, extra=(dzc, hc), name="wgrad_in_even"), inverse=True)],
                  "scatter_start_in_0")
    grad_x, dg_mpre0, dsh_m0, dsc_m0 = mm_pre_bwd([dz0], win_e, x, dx1, row(g_mix_pre, 0) + tok, sc_m[0], tm=tm,
                                                  name="in_even_bwd")
    _, dg_mpre0c, dcsh, dcsc = mm_pre_bwd([dkvc], win_e, ctx, None, row(g_mix_pre, 0), csc_m, tm=C,
                                          w_row_off=8 * LANES, name="in_even_ctx_bwd")

    out, ran = {}, {}

    def update(name, lands, transposed):
        w_, m_, v_ = (a.transpose(0, 2, 1) if transposed else a for a in (P[name], M[name], V[name]))
        r = w_.shape[1]
        tr = r // 4 if r % 64 == 0 and r > 256 else r
        res = adamw(w_, m_, v_, [l_.reshape(N_DEV, r, l_.shape[1]) for l_ in lands], tr=tr, name=f"adamw_{name}")
        ran[name] = res[0]
        for kind, val in zip(("grad", "delta", "new_m", "new_v"), res):
            out[(kind, name)] = val.transpose(0, 2, 1) if transposed else val

    zero = jnp.zeros((1, D), F32)
    dmod0 = jnp.concatenate([dsh_m0, dsc_m0, dgt_m0, sf0["sh_f"], sf0["sc_f"], sf0["gt_f"]], axis=1)
    dmodc = jnp.concatenate([dcsh, dcsc, zero, zero, zero, zero], axis=1)
    dmod1 = jnp.concatenate([dsh_m1, dsc_m1, dgt_m1, sf1["sh_f"], sf1["sc_f"], sf1["gt_f"]], axis=1)
    dmods = jnp.concatenate([dmod0, dmodc, dmod1], axis=0)
    dm = dmods.reshape(-1, LANES).astype(BF16)
    d_sems, d_srcs, d_lands, d_tok = exchange_start(
        [dm], place_own([dm], [dm.shape[0]], me, scatter=False, name="dmods_own"), scatter=False, name="dmods_start")
    slots = exchange_wait(g_srcs[:6], g_lands[:6], g_sems[:12], d_tok, scatter=True, name="scatter_wait_early")
    early = slots
    update("w_ffn_down", [slots[4], slots[0]], False)
    update("w_in_odd", [slots[3]], True)
    update("w_out_odd", [slots[2]], False)
    updated = lambda names: [ran[k] for k in names]
    dmods_all = exchange_wait(d_srcs, d_lands, d_sems, updated(("w_out_odd",)), scatter=False, name="dmods_wait")[0]
    dall = lax.dynamic_index_in_dim(dmods_all.astype(F32).reshape(N_DEV, 3, N_DEV, n_ada), me, 2, False)
    g_w_ada, dcc = ada_bwd_mm(silu_c, c_ctx[None, :], dall, w_ada, name="ada_bwd")

    rep = dict(
        c_ctx=dcc[0:1],
        b_ada=jnp.concatenate([dmod0 + dmodc, dmod1]),
        g_mix_pre=jnp.concatenate([dg_mpre0 + dg_mpre0c, dg_mpre1]),
        g_mix_post=jnp.concatenate([dg_mpost0, dg_mpost1]),
        g_ffn_pre=jnp.concatenate([sf0["g_ffn_pre"], sf1["g_ffn_pre"]]),
        g_ffn_post=jnp.concatenate([sf0["g_ffn_post"], sf1["g_ffn_post"]]),
        w_pool=_nat2d(dwp), pool_scale=dps, attn_sink=dsink[:, :N_Q_HEADS],
        sgu_w=_nat2d(dws), sgu_b=dbs[:, :sgu_b.shape[1]].T,
        ffn_conv_b=jnp.stack([sf0["ffn_conv_b"], sf1["ffn_conv_b"]]),
    )
    hi = loss_part.astype(BF16).astype(F32)
    mid = (loss_part - hi).astype(BF16).astype(F32)
    loss_piece = jnp.pad(jnp.concatenate([hi, mid, loss_part - hi - mid], axis=1), ((0, 7), (0, LANES - 3)))
    conv_g = jnp.stack([sf0["ffn_conv_w"], sf1["ffn_conv_w"]]).reshape(2 * 3, N_DEV, n_cw).swapaxes(0, 1)
    shard_full = dict(sgu_ln_g=dlng.reshape(N_DEV, LANES), sgu_ln_b=dlnb.reshape(N_DEV, LANES),
                      ffn_conv_w=jnp.concatenate([_pack_rows(conv_g[d]) for d in range(N_DEV)], axis=0))
    small_names = list(rep) + list(shard_full)
    pieces = [_pack_rows(rep[k]) for k in rep] + list(shard_full.values()) + [loss_piece]
    sizes = [p.shape[0] for p in pieces]
    offs = [sum(sizes[:i]) for i in range(len(sizes))]
    pieces.append(jnp.zeros((-sum(sizes) % 16, LANES), F32))
    gpack = jnp.concatenate(pieces, axis=0).astype(BF16)
    own = place_own([gpack], [gpack.shape[0]], me, scatter=False, name="smallgrad_own")
    s_sems, s_srcs, s_lands, small_tok = exchange_start([gpack], own, scatter=False, name="smallgrad_start")

    slots = exchange_wait(g_srcs[6:], g_lands[6:], g_sems[12:], small_tok, scatter=True, name="scatter_wait_late")
    update("w_in_even", [slots[1]], True)
    update("w_out_even", [slots[0]], False)
    update("w_ffn_up", [early[5], early[1]], True)
    res = adamw(w_ada, m_w_ada, v_w_ada, [g_w_ada[l][None] for l in range(w_ada.shape[0])], tr=D // 4, name="adamw_w_ada")
    ran["w_ada"] = res[0]
    for kind, val in zip(("grad", "delta", "new_m", "new_v"), res):
        out[(kind, "w_ada")] = val

    gpacks = exchange_wait(s_srcs, s_lands, s_sems, updated(("w_ada",)), scatter=False,
                           name="smallgrad_wait")[0]
    per_dev = {k: shard_full[k].shape[0] // N_DEV for k in shard_full}
    params = [(_nat2d(P[k]), _nat2d(M[k]), _nat2d(V[k]), offs[i], per_dev.get(k, 0)) for i, k in enumerate(small_names)]
    res = small_update(gpacks.reshape(N_DEV, -1, LANES), jnp.reshape(me, (1,)).astype(jnp.int32), params, offs[-1], name="adamw_small")
    for i, k in enumerate(small_names):
        for kind, val in zip(("grad", "delta", "new_m", "new_v"), res[4 * i:4 * i + 4]):
            out[(kind, k)] = val.reshape(P[k].shape)
    loss = res[-1][0, 0]

    names = list(P)
    final = [loss, grad_x[None]]
    for kind in ("grad", "delta", "new_m", "new_v"):
        for k in names:
            val = out[(kind, k)]
            final.append(val)
    return tuple(final)
```

```python
import functools
import math

import jax
import jax.numpy as jnp
from jax import lax
from jax.experimental import pallas as pl
from jax.experimental.pallas import tpu as pltpu

F32 = jnp.float32
BF16 = jnp.bfloat16
MESH = pl.DeviceIdType.MESH
N_DEV = 8
LANES = 128
VMEM_LIMIT = 48 * 1024 * 1024
EPS = 1e-6
NEG_INF = -1e30
GRID_W = 64
WINDOW = 128
BLK = 128
HEAD_DIM = 64
N_Q_HEADS = 8
N_KV_HEADS = 2
GQA = N_Q_HEADS // N_KV_HEADS
POOL_WINDOWS = (2, 4, 8, 16)
ROPE_BASE = 10000.0
ROPE_FREQS = HEAD_DIM // 4
PAD = 16
ADAM_LR, ADAM_B1, ADAM_B2, ADAM_EPS, ADAM_WD, ADAM_STEP = 0.001, 0.9, 0.999, 1e-08, 0.01, 10
BC1 = 1.0 - ADAM_B1 ** ADAM_STEP
BC2 = 1.0 - ADAM_B2 ** ADAM_STEP
SQRT_2_OVER_PI = math.sqrt(2.0 / math.pi)
GELU_C = 0.044715


def _cp(sem=None):
    return pltpu.CompilerParams(dimension_semantics=sem, vmem_limit_bytes=VMEM_LIMIT)


def _dot(a, b):
    return jnp.dot(a, b, preferred_element_type=F32)


def _dot_nt(a, b):
    return lax.dot_general(a, b, (((1,), (1,)), ((), ())), preferred_element_type=F32)


def _dot_tn(a, b):
    return lax.dot_general(a, b, (((0,), (0,)), ((), ())), preferred_element_type=F32)


def _rms(x):
    r = lax.rsqrt(jnp.mean(x * x, axis=-1, keepdims=True) + EPS)
    return x * r, r


def _rms_bwd(dn, n, r):
    return r * (dn - n * jnp.mean(dn * n, axis=-1, keepdims=True))


def _colsum(a):
    return jnp.sum(a, axis=0, keepdims=True)


def _rope(x, c, sa, sb):
    return x * c + pltpu.roll(x, LANES - ROPE_FREQS, 1) * sa + pltpu.roll(x, ROPE_FREQS, 1) * sb


def _full(shape):
    return pl.BlockSpec(shape, lambda *_: (0,) * len(shape))


def pre_mm(x, g, sh, sc, wt, *, tm, tn, w_row_off=0, name):
    T, D = x.shape
    n_rows = wt.shape[0] - w_row_off

    def body(x_ref, g_ref, sh_ref, sc_ref, w_ref, h_ref, z_ref):
        n, _ = _rms(x_ref[...])
        h = (n * g_ref[...] * (1.0 + sc_ref[...]) + sh_ref[...]).astype(BF16)
        h_ref[...] = h
        for c0 in range(0, n_rows, tn):
            z_ref[:, c0:c0 + tn] = _dot_nt(h, w_ref[c0:c0 + tn, :]).astype(BF16)

    vec = pl.BlockSpec((1, D), lambda i: (0, 0))
    return pl.pallas_call(
        body, name=name, grid=(T // tm,),
        in_specs=[pl.BlockSpec((tm, D), lambda i: (i, 0)), vec, vec, vec,
                  pl.BlockSpec((n_rows, D), lambda i: (w_row_off // n_rows, 0), pipeline_mode=pl.Buffered(1))],
        out_specs=[pl.BlockSpec((tm, D), lambda i: (i, 0)), pl.BlockSpec((tm, n_rows), lambda i: (i, 0))],
        out_shape=[jax.ShapeDtypeStruct((T, D), BF16), jax.ShapeDtypeStruct((T, n_rows), BF16)],
        compiler_params=_cp(("parallel",)),
    )(x, g, sh, sc, wt)


def inproj_even(x, g, sh, sc, wt, cos, sa, sb, *, tm, name):
    T, D = x.shape
    N = wt.shape[0]

    def body(x_ref, g_ref, sh_ref, sc_ref, w_ref, c_ref, sa_ref, sb_ref, h_ref, u_ref, q_ref, kv_ref):
        n, _ = _rms(x_ref[...])
        h = (n * g_ref[...] * (1.0 + sc_ref[...]) + sh_ref[...]).astype(BF16)
        h_ref[...] = h
        z = _dot_nt(h, w_ref[...])
        u_ref[...] = z[:, :4 * LANES]
        c, a, b = c_ref[...], sa_ref[...], sb_ref[...]
        for s in range(4):
            q_ref[:, s * LANES:(s + 1) * LANES] = _rope(z[:, (4 + s) * LANES:(5 + s) * LANES], c, a, b).astype(BF16)
        kv_ref[:, :LANES] = _rope(z[:, 8 * LANES:9 * LANES], c, a, b).astype(BF16)
        kv_ref[:, LANES:] = z[:, 9 * LANES:].astype(BF16)

    vec = pl.BlockSpec((1, D), lambda i: (0, 0))
    row = lambda w: pl.BlockSpec((tm, w), lambda i: (i, 0))
    return pl.pallas_call(
        body, name=name, grid=(T // tm,),
        in_specs=[row(D), vec, vec, vec, _full((N, D)), row(LANES), row(LANES), row(LANES)],
        out_specs=[row(D), row(4 * LANES), row(4 * LANES), row(2 * LANES)],
        out_shape=[jax.ShapeDtypeStruct((T, D), BF16), jax.ShapeDtypeStruct((T, 4 * LANES), F32),
                   jax.ShapeDtypeStruct((T, 4 * LANES), BF16), jax.ShapeDtypeStruct((T, 2 * LANES), BF16)],
        compiler_params=_cp(("parallel",)),
    )(x, g, sh, sc, wt, cos, sa, sb)


def mm_post(a_parts, w, x, g, gt, *, tm, target=None, name):
    T = a_parts[0].shape[0]
    D = w.shape[1]
    npart = len(a_parts)
    offs = [sum(a_.shape[1] for a_ in a_parts[:p]) for p in range(npart + 1)]
    with_loss = target is not None

    def body(*refs):
        a_refs, (w_ref, x_ref, g_ref, gt_ref) = refs[:npart], refs[npart:npart + 4]
        y = _dot(a_refs[0][...], w_ref[offs[0]:offs[1], :])
        for p in range(1, npart):
            y = y + _dot(a_refs[p][...], w_ref[offs[p]:offs[p + 1], :])
        n, _ = _rms(y)
        xn = x_ref[...] + gt_ref[...] * (n * g_ref[...])
        if not with_loss:
            y_ref, xn_ref = refs[npart + 4:]
            y_ref[...] = y.astype(BF16)
            xn_ref[...] = xn
            return
        t_ref, y_ref, d_ref, l_ref = refs[npart + 4:]
        y_ref[...] = y.astype(BF16)

        @pl.when(pl.program_id(0) == 0)
        def _():
            l_ref[...] = jnp.zeros_like(l_ref)

        e = xn - t_ref[...]
        l_ref[...] += 0.5 * jnp.sum(jnp.mean(e * e, axis=-1, keepdims=True), axis=0, keepdims=True)
        d_ref[...] = e * (1.0 / D)

    vec = pl.BlockSpec((1, D), lambda i: (0, 0))
    row = lambda w_: pl.BlockSpec((tm, w_), lambda i: (i, 0))
    in_specs = [row(a_.shape[1]) for a_ in a_parts] + [_full(w.shape), row(D), vec, vec]
    out_specs = [row(D), row(D)]
    out_shape = [jax.ShapeDtypeStruct((T, D), BF16), jax.ShapeDtypeStruct((T, D), F32)]
    if with_loss:
        in_specs.append(row(D))
        out_specs.append(_full((1, 1)))
        out_shape.append(jax.ShapeDtypeStruct((1, 1), F32))
    return pl.pallas_call(
        body, name=name, grid=(T // tm,), in_specs=in_specs, out_specs=out_specs, out_shape=out_shape,
        compiler_params=_cp(("arbitrary",) if with_loss else ("parallel",)),
    )(*a_parts, w, x, g, gt, *((target,) if with_loss else ()))


def post_bwd_mm(dxn, y, g, gt, w, *, tm, name):
    T, D = y.shape
    K = w.shape[0]

    def body(dxn_ref, y_ref, g_ref, gt_ref, w_ref, dy_ref, da_ref, dg_ref, dgt_ref):
        @pl.when(pl.program_id(0) == 0)
        def _():
            dg_ref[...] = jnp.zeros_like(dg_ref)
            dgt_ref[...] = jnp.zeros_like(dgt_ref)

        d = dxn_ref[...]
        n, r = _rms(y_ref[...].astype(F32))
        g_, gt_ = g_ref[...], gt_ref[...]
        dg_ref[...] += _colsum(d * gt_ * n)
        dgt_ref[...] += _colsum(d * g_ * n)
        dy = _rms_bwd(d * (gt_ * g_), n, r).astype(BF16)
        dy_ref[...] = dy
        da_ref[...] = _dot_nt(dy, w_ref[...]).astype(BF16)

    vec = pl.BlockSpec((1, D), lambda i: (0, 0))
    row = lambda w_: pl.BlockSpec((tm, w_), lambda i: (i, 0))
    return pl.pallas_call(
        body, name=name, grid=(T // tm,),
        in_specs=[row(D), row(D), vec, vec, _full((K, D))],
        out_specs=[row(D), row(K), vec, vec],
        out_shape=[jax.ShapeDtypeStruct((T, D), BF16), jax.ShapeDtypeStruct((T, K), BF16),
                   jax.ShapeDtypeStruct((1, D), F32), jax.ShapeDtypeStruct((1, D), F32)],
        compiler_params=_cp(("arbitrary",)),
    )(dxn, y, g, gt, w)


def mm_pre_bwd(dzs, wt, x, dres, g, sc, *, tm, w_row_off=0, name):
    T, N = dzs[0].shape
    D = x.shape[1]
    npart = len(dzs)
    off = w_row_off // N
    has_res = dres is not None

    def body(*refs):
        dz_refs = refs[:npart]
        w_refs = refs[npart:2 * npart]
        rest = refs[2 * npart:]
        x_ref = rest[0]
        dres_ref = rest[1] if has_res else None
        g_ref, sc_ref, dx_ref, dg_ref, dsh_ref, dsc_ref = rest[1 + has_res:]

        @pl.when(pl.program_id(0) == 0)
        def _():
            dg_ref[...] = jnp.zeros_like(dg_ref)
            dsh_ref[...] = jnp.zeros_like(dsh_ref)
            dsc_ref[...] = jnp.zeros_like(dsc_ref)

        dh = _dot(dz_refs[0][...], w_refs[0][...])
        for p in range(1, npart):
            dh = dh + _dot(dz_refs[p][...], w_refs[p][...])
        n, r = _rms(x_ref[...])
        g_, s1 = g_ref[...], 1.0 + sc_ref[...]
        dsh_ref[...] += _colsum(dh)
        dsc_ref[...] += _colsum(dh * n * g_)
        dg_ref[...] += _colsum(dh * s1 * n)
        dxp = _rms_bwd(dh * (g_ * s1), n, r)
        dx_ref[...] = dxp + dres_ref[...] if has_res else dxp

    vec = pl.BlockSpec((1, D), lambda i: (0, 0))
    row = pl.BlockSpec((tm, D), lambda i: (i, 0))
    w_specs = [pl.BlockSpec((N, D), (lambda i, p=p: (off + p, 0)), pipeline_mode=pl.Buffered(1)) for p in range(npart)]
    res_specs, res_args = ([row], (dres,)) if has_res else ([], ())
    return pl.pallas_call(
        body, name=name, grid=(T // tm,),
        in_specs=[pl.BlockSpec((tm, N), lambda i: (i, 0))] * npart + w_specs + [row] + res_specs + [vec, vec],
        out_specs=[row, vec, vec, vec],
        out_shape=[jax.ShapeDtypeStruct((T, D), F32)] + [jax.ShapeDtypeStruct((1, D), F32)] * 3,
        compiler_params=_cp(("arbitrary",)),
    )(*dzs, *([wt] * npart), x, *res_args, g, sc)


def wgrad(a_parts, b, *, tr, extra=None, name):
    T, R = a_parts[0].shape
    D = b.shape[1]
    npart = len(a_parts)
    nr = R // tr

    def body(*refs):
        a_refs, b_ref = refs[:npart], refs[npart]
        g_ref = refs[-1]
        for p in range(npart):
            @pl.when(pl.program_id(0) // nr == p)
            def _():
                acc = _dot_tn(a_refs[p][...], b_ref[...])
                if extra is not None:
                    acc += _dot_tn(refs[npart + 1][...], refs[npart + 2][...])
                g_ref[...] = acc.astype(BF16)

    in_specs = [pl.BlockSpec((T, tr), (lambda r, p=p: (0, jnp.clip(r - p * nr, 0, nr - 1)))) for p in range(npart)]
    in_specs.append(_full((T, D)))
    args = [*a_parts, b]
    if extra is not None:
        a2, b2 = extra
        in_specs += [pl.BlockSpec((a2.shape[0], tr), lambda r: (0, r)), _full(b2.shape)]
        args += [a2, b2]
    return pl.pallas_call(
        body, name=name, grid=(npart * nr,),
        in_specs=in_specs, out_specs=pl.BlockSpec((tr, D), lambda r: (r, 0)),
        out_shape=jax.ShapeDtypeStruct((npart * R, D), BF16),
        compiler_params=_cp(("parallel",)),
    )(*args)


def _conv_ext(ref, r0, rows, total):
    top = ref[pl.ds(pl.multiple_of(jnp.maximum(r0 - PAD, 0), PAD), PAD), :]
    mid = ref[pl.ds(r0, rows), :]
    bot = ref[pl.ds(pl.multiple_of(jnp.minimum(r0 + rows, total - PAD), PAD), PAD), :]
    top = jnp.where(r0 > 0, top, jnp.zeros_like(top))
    bot = jnp.where(r0 + rows < total, bot, jnp.zeros_like(bot))
    return jnp.concatenate([top, mid, bot], axis=0).astype(F32)


def _shift_rows(a, k):
    return pltpu.roll(a, k % a.shape[0], 0)


def _conv3(x, w, b):
    return w[0:1] * _shift_rows(x, 1) + w[1:2] * x + w[2:3] * _shift_rows(x, -1) + b


def _gate_up_specs(rows_, wblk, nb):
    return [pl.BlockSpec((rows_, wblk), lambda j: (0, j)), pl.BlockSpec((rows_, wblk), lambda j: (0, j + nb))]


def conv_fwd(hu, cw, cb, *, rows, wblk, name):
    L, N2 = hu.shape
    nb = N2 // 2 // wblk
    nchunk = L // rows

    def body(hg_ref, hu_ref, wg_ref, wu_ref, bg_ref, bu_ref, a_ref, s1_ref, s2_ref):
        def chunk(ci, carry):
            r0 = pl.multiple_of(ci * rows, rows)
            gate = _conv3(_conv_ext(hg_ref, r0, rows, L), wg_ref[...], bg_ref[...])[PAD:PAD + rows]
            up = _conv3(_conv_ext(hu_ref, r0, rows, L), wu_ref[...], bu_ref[...])[PAD:PAD + rows]
            sg = jax.nn.sigmoid(gate)
            silu = gate * sg
            at = pl.ds(r0, rows)
            a_ref[at, :] = (silu * up).astype(BF16)
            s1_ref[at, :] = silu.astype(BF16)
            s2_ref[at, :] = (up * (sg + silu * (1.0 - sg))).astype(BF16)
            return carry

        lax.fori_loop(0, nchunk, chunk, 0)

    out = pl.BlockSpec((L, wblk), lambda j: (0, j))
    return pl.pallas_call(
        body, name=name, grid=(nb,),
        in_specs=_gate_up_specs(L, wblk, nb) + _gate_up_specs(3, wblk, nb) + _gate_up_specs(1, wblk, nb),
        out_specs=[out] * 3, out_shape=[jax.ShapeDtypeStruct((L, N2 // 2), BF16)] * 3,
        compiler_params=_cp(("parallel",)),
    )(hu, hu, cw, cw, cb, cb)


def conv_bwd(da, s1, s2, hu, cw, *, rows, wblk, name):
    L, N2 = hu.shape
    F = N2 // 2
    nb = F // wblk
    nchunk = L // rows
    mid = slice(PAD, PAD + rows)

    def body(da_ref, s1_ref, s2_ref, hg_ref, hu_ref, wg_ref, wu_ref, dg_ref, du_ref, dwg_ref, dwu_ref, dbg_ref, dbu_ref):
        for ref in (dwg_ref, dwu_ref, dbg_ref, dbu_ref):
            ref[...] = jnp.zeros_like(ref)

        def half_bwd(x_ref, dh, w_ref, dx_ref, dw_ref, db_ref, r0):
            w = w_ref[...]
            nxt, prv = _shift_rows(dh, -1)[mid], _shift_rows(dh, 1)[mid]
            dhm, xm = dh[mid], x_ref[pl.ds(r0, rows), :].astype(F32)
            dx_ref[pl.ds(r0, rows), :] = (w[0:1] * nxt + w[1:2] * dhm + w[2:3] * prv).astype(BF16)
            db_ref[...] += _colsum(dhm)
            dw_ref[0:1, :] += _colsum(nxt * xm)
            dw_ref[1:2, :] += _colsum(dhm * xm)
            dw_ref[2:3, :] += _colsum(prv * xm)

        def chunk(ci, carry):
            r0 = pl.multiple_of(ci * rows, rows)
            d = _conv_ext(da_ref, r0, rows, L)
            half_bwd(hu_ref, d * _conv_ext(s1_ref, r0, rows, L), wu_ref, du_ref, dwu_ref, dbu_ref, r0)
            half_bwd(hg_ref, d * _conv_ext(s2_ref, r0, rows, L), wg_ref, dg_ref, dwg_ref, dbg_ref, r0)
            return carry

        lax.fori_loop(0, nchunk, chunk, 0)

    blk = lambda r: pl.BlockSpec((r, wblk), lambda j: (0, j))
    return pl.pallas_call(
        body, name=name, grid=(nb,),
        in_specs=[blk(L)] * 3 + _gate_up_specs(L, wblk, nb) + _gate_up_specs(3, wblk, nb),
        out_specs=[blk(L), blk(L), blk(3), blk(3), blk(1), blk(1)],
        out_shape=[jax.ShapeDtypeStruct((L, F), BF16)] * 2 + [jax.ShapeDtypeStruct((3, F), F32)] * 2
        + [jax.ShapeDtypeStruct((1, F), F32)] * 2,
        compiler_params=_cp(("parallel",)),
    )(da, s1, s2, hu, hu, cw, cw)


def _window_sums(pad_ref, w, lead):
    a = pad_ref[...]
    k = 1
    while k < w:
        a = a + _shift_rows(a, -k)
        k *= 2
    return _shift_rows(a, lead) if lead else a


def _pool_counts(L, h):
    t = lax.broadcasted_iota(jnp.int32, (L, 1), 0)
    return (jnp.minimum(t + h, L) - jnp.maximum(t - h, 0)).astype(F32)


def _pooled(u_ref, pad_ref, L, w):
    h = w // 2
    pad_ref[pl.ds(PAD, L), :] = u_ref[...]
    win = _window_sums(pad_ref, w, h)[PAD:PAD + L]
    return win / _pool_counts(L, h) - u_ref[...]


def _zero_pad_edges(pad_ref, L):
    z = jnp.zeros((PAD, LANES), F32)
    pad_ref[pl.ds(0, PAD), :] = z
    pad_ref[pl.ds(PAD + L, PAD), :] = z


def pool_fwd(u, w_pool, pool_scale, *, name):
    L = u.shape[0]

    def body(u_ref, w_ref, ps_ref, p_ref, pad_ref):
        _zero_pad_edges(pad_ref, L)
        for gi, win in enumerate(POOL_WINDOWS):
            @pl.when(pl.program_id(0) == gi)
            def _():
                pooled = _pooled(u_ref, pad_ref, L, win)
                p_ref[...] = (_dot(pooled.astype(BF16), w_ref[...].astype(BF16)) * ps_ref[...]).astype(BF16)

    return pl.pallas_call(
        body, name=name, grid=(len(POOL_WINDOWS),),
        in_specs=[pl.BlockSpec((L, LANES), lambda gi: (0, gi)), pl.BlockSpec((None, LANES, LANES), lambda gi: (gi, 0, 0)),
                  pl.BlockSpec((1, LANES), lambda gi: (0, gi))],
        out_specs=pl.BlockSpec((L, LANES), lambda gi: (0, gi)),
        out_shape=jax.ShapeDtypeStruct((L, 4 * LANES), BF16),
        scratch_shapes=[pltpu.VMEM((L + 2 * PAD, LANES), F32)],
        compiler_params=_cp(("parallel",)),
    )(u, w_pool, pool_scale)


def pool_bwd(u, dpa, w_pool, pool_scale, *, name):
    L = u.shape[0]

    def body(u_ref, dp_ref, w_ref, ps_ref, du_ref, dw_ref, dps_ref, pad_ref):
        _zero_pad_edges(pad_ref, L)
        for gi, win in enumerate(POOL_WINDOWS):
            @pl.when(pl.program_id(0) == gi)
            def _():
                h = win // 2
                wb = w_ref[...].astype(BF16)
                pooled = _pooled(u_ref, pad_ref, L, win).astype(BF16)
                dp = dp_ref[...].astype(F32)
                dps_ref[...] = _colsum(dp * _dot(pooled, wb))
                dy = (dp * ps_ref[...]).astype(BF16)
                dw_ref[...] = _dot_tn(pooled, dy)
                dpooled = _dot_nt(dy, wb)
                pad_ref[pl.ds(PAD, L), :] = dpooled / _pool_counts(L, h)
                du_ref[...] = (_window_sums(pad_ref, win, h - 1)[PAD:PAD + L] - dpooled).astype(BF16)

    return pl.pallas_call(
        body, name=name, grid=(len(POOL_WINDOWS),),
        in_specs=[pl.BlockSpec((L, LANES), lambda gi: (0, gi)), pl.BlockSpec((L, LANES), lambda gi: (0, gi)),
                  pl.BlockSpec((None, LANES, LANES), lambda gi: (gi, 0, 0)), pl.BlockSpec((1, LANES), lambda gi: (0, gi))],
        out_specs=[pl.BlockSpec((L, LANES), lambda gi: (0, gi)), pl.BlockSpec((None, LANES, LANES), lambda gi: (gi, 0, 0)),
                   pl.BlockSpec((1, LANES), lambda gi: (0, gi))],
        out_shape=[jax.ShapeDtypeStruct((L, 4 * LANES), BF16), jax.ShapeDtypeStruct((4, LANES, LANES), F32),
                   jax.ShapeDtypeStruct((1, 4 * LANES), F32)],
        scratch_shapes=[pltpu.VMEM((L + 2 * PAD, LANES), F32)],
        compiler_params=_cp(("parallel",)),
    )(u, dpa, w_pool, pool_scale)


def _attn_probs(qk, band_k, ctx_k, sink_ref, kh, mask4):
    s_loc = jnp.where(mask4, _dot_nt(qk, band_k), NEG_INF)
    s_ctx = _dot_nt(qk, ctx_k)
    sk = jnp.concatenate([jnp.full((BLK, 1), sink_ref[kh * GQA + hh], F32) for hh in range(GQA)], axis=0)
    m = jnp.maximum(jnp.maximum(jnp.max(s_loc, axis=-1, keepdims=True), jnp.max(s_ctx, axis=-1, keepdims=True)), sk)
    e_loc, e_ctx, e_s = jnp.exp(s_loc - m), jnp.exp(s_ctx - m), jnp.exp(sk - m)
    inv = 1.0 / (jnp.sum(e_loc, axis=-1, keepdims=True) + jnp.sum(e_ctx, axis=-1, keepdims=True) + e_s)
    return e_loc * inv, e_ctx * inv, e_s * inv


def _attn_block(n, L):
    start = pl.multiple_of(jnp.clip((n - 1) * BLK, 0, L - 3 * BLK), BLK)
    qpos = n * BLK + lax.broadcasted_iota(jnp.int32, (BLK, 3 * BLK), 0)
    kpos = start + lax.broadcasted_iota(jnp.int32, (BLK, 3 * BLK), 1)
    mask = jnp.abs(kpos - qpos) <= WINDOW
    return start, jnp.concatenate([mask] * GQA, axis=0)


def _stack_slabs(ref):
    return jnp.concatenate([ref[:, s * LANES:(s + 1) * LANES] for s in range(GQA)], axis=0)


def _kv_head_lanes(kh):
    return (lax.broadcasted_iota(jnp.int32, (1, LANES), 1) // HEAD_DIM) == kh


def permute_heads(w, inverse=False):
    lo, hi = 4 * LANES, 8 * LANES
    mid = w[lo:hi].reshape(*((GQA, N_KV_HEADS) if inverse else (N_KV_HEADS, GQA)), HEAD_DIM, w.shape[1])
    return jnp.concatenate([w[:lo], mid.swapaxes(0, 1).reshape(hi - lo, w.shape[1]), w[hi:]], axis=0)


def attn_fwd(q, kv, kvc, sink, *, name):
    L = q.shape[0]
    C = kvc.shape[0]
    scale = HEAD_DIM ** -0.5

    def body(q_ref, kv_ref, kvc_ref, sink_ref, o_ref):
        start, mask4 = _attn_block(pl.program_id(0), L)
        band = kv_ref[pl.ds(start, 3 * BLK), :]
        kvc_ = kvc_ref[...]
        qs = _stack_slabs(q_ref) * scale
        o = jnp.zeros((GQA * BLK, LANES), F32)
        for kh in range(N_KV_HEADS):
            grp = _kv_head_lanes(kh)
            qk = jnp.where(grp, qs, jnp.zeros_like(qs))
            p_loc, p_ctx, _ = _attn_probs(qk, band[:, :LANES], kvc_[:, :LANES], sink_ref, kh, mask4)
            o = o + jnp.where(grp, _dot(p_loc.astype(BF16), band[:, LANES:]) + _dot(p_ctx.astype(BF16), kvc_[:, LANES:]), 0.0)
        for s in range(GQA):
            o_ref[:, s * LANES:(s + 1) * LANES] = o[s * BLK:(s + 1) * BLK].astype(BF16)

    return pl.pallas_call(
        body, name=name, grid=(L // BLK,),
        in_specs=[pl.BlockSpec((BLK, 4 * LANES), lambda n: (n, 0)), _full((L, 2 * LANES)), _full((C, 2 * LANES)),
                  pl.BlockSpec(memory_space=pltpu.SMEM)],
        out_specs=pl.BlockSpec((BLK, 4 * LANES), lambda n: (n, 0)),
        out_shape=jax.ShapeDtypeStruct((L, 4 * LANES), BF16),
        compiler_params=_cp(("parallel",)),
    )(q, kv, kvc, sink)


def attn_bwd(q, kv, kvc, sink, dpa, cos, sa, sb, *, name):
    L = q.shape[0]
    C = kvc.shape[0]
    nb = L // BLK
    scale = HEAD_DIM ** -0.5

    def body(q_ref, kv_ref, kvc_ref, sink_ref, do_ref, c_ref, sa_ref, sb_ref, cq_ref, saq_ref, sbq_ref,
             dq_ref, dkv_ref, dkvc_ref, dsink_ref, dkv_acc, dkvc_acc):
        n = pl.program_id(0)

        @pl.when(n == 0)
        def _():
            dkv_acc[...] = jnp.zeros_like(dkv_acc)
            dkvc_acc[...] = jnp.zeros_like(dkvc_acc)
            dsink_ref[...] = jnp.zeros_like(dsink_ref)

        start, mask4 = _attn_block(n, L)
        band = kv_ref[pl.ds(start, 3 * BLK), :]
        kvc_ = kvc_ref[...]
        band_k, band_v, ctx_k, ctx_v = band[:, :LANES], band[:, LANES:], kvc_[:, :LANES], kvc_[:, LANES:]
        qs = _stack_slabs(q_ref) * scale
        dos = _stack_slabs(do_ref)
        lane = lax.broadcasted_iota(jnp.int32, (1, LANES), 1)
        dsink = jnp.zeros((1, LANES), F32)
        dq = jnp.zeros((GQA * BLK, LANES), F32)
        dk = jnp.zeros((LANES, 3 * BLK), F32)
        dv = jnp.zeros((LANES, 3 * BLK), F32)
        dkc = jnp.zeros((LANES, C), F32)
        dvc = jnp.zeros((LANES, C), F32)
        for kh in range(N_KV_HEADS):
            grp = _kv_head_lanes(kh)
            qk = jnp.where(grp, qs, jnp.zeros_like(qs))
            dok = jnp.where(grp, dos, jnp.zeros_like(dos))
            p_loc, p_ctx, p_s = _attn_probs(qk, band_k, ctx_k, sink_ref, kh, mask4)
            dp_loc = _dot_nt(dok, band_v)
            dp_ctx = _dot_nt(dok, ctx_v)
            delta = jnp.sum(p_loc * dp_loc, axis=-1, keepdims=True) + jnp.sum(p_ctx * dp_ctx, axis=-1, keepdims=True)
            ds_loc = (p_loc * (dp_loc - delta)).astype(BF16)
            ds_ctx = (p_ctx * (dp_ctx - delta)).astype(BF16)
            dsk = p_s * delta
            for hh in range(GQA):
                dsink = dsink - jnp.where(lane == kh * GQA + hh, jnp.sum(dsk[hh * BLK:(hh + 1) * BLK], axis=0, keepdims=True), 0.0)
            dq = dq + jnp.where(grp, _dot(ds_loc, band_k) + _dot(ds_ctx, ctx_k), 0.0)
            dk = dk + _dot_tn(qk, ds_loc)
            dv = dv + _dot_tn(dok, p_loc.astype(BF16))
            dkc = dkc + _dot_tn(qk, ds_ctx)
            dvc = dvc + _dot_tn(dok, p_ctx.astype(BF16))
        dsink_ref[...] += dsink
        dkv_acc[:LANES, pl.ds(start, 3 * BLK)] += dk
        dkv_acc[LANES:, pl.ds(start, 3 * BLK)] += dv
        dkvc_acc[:LANES, :] += dkc
        dkvc_acc[LANES:, :] += dvc
        c, a, b = cq_ref[...], -saq_ref[...], -sbq_ref[...]
        for s in range(GQA):
            dq_ref[:, s * LANES:(s + 1) * LANES] = _rope(dq[s * BLK:(s + 1) * BLK] * scale, c, a, b).astype(BF16)

        @pl.when(n == nb - 1)
        def _():
            dkv_ref[:, :LANES] = _rope(dkv_acc[:LANES, :].T, c_ref[...], -sa_ref[...], -sb_ref[...]).astype(BF16)
            dkv_ref[:, LANES:] = dkv_acc[LANES:, :].T.astype(BF16)
            dkvc_ref[...] = dkvc_acc[...].T.astype(BF16)

    blk = lambda w: pl.BlockSpec((BLK, w), lambda n: (n, 0))
    return pl.pallas_call(
        body, name=name, grid=(nb,),
        in_specs=[blk(4 * LANES), _full((L, 2 * LANES)), _full((C, 2 * LANES)), pl.BlockSpec(memory_space=pltpu.SMEM),
                  pl.BlockSpec((BLK, 4 * LANES), lambda n: (n, 1)),
                  _full((L, LANES)), _full((L, LANES)), _full((L, LANES)), blk(LANES), blk(LANES), blk(LANES)],
        out_specs=[blk(4 * LANES), _full((L, 2 * LANES)), _full((C, 2 * LANES)), _full((1, LANES))],
        out_shape=[jax.ShapeDtypeStruct((L, 4 * LANES), BF16), jax.ShapeDtypeStruct((L, 2 * LANES), BF16),
                   jax.ShapeDtypeStruct((C, 2 * LANES), BF16), jax.ShapeDtypeStruct((1, LANES), F32)],
        scratch_shapes=[pltpu.VMEM((2 * LANES, L), F32), pltpu.VMEM((2 * LANES, C), F32)],
        compiler_params=_cp(("arbitrary",)),
    )(q, kv, kvc, sink, dpa, cos, sa, sb, cos, sa, sb)


def _gelu_parts(x):
    th = jnp.tanh(SQRT_2_OVER_PI * (x + GELU_C * x * x * x))
    return 0.5 * x * (1.0 + th), th


def _gelu_grad(x, th):
    return 0.5 * (1.0 + th) + 0.5 * x * (1.0 - th * th) * SQRT_2_OVER_PI * (1.0 + 3.0 * GELU_C * x * x)


def _layernorm(v):
    mu = jnp.mean(v, axis=-1, keepdims=True)
    vc = v - mu
    rstd = lax.rsqrt(jnp.mean(vc * vc, axis=-1, keepdims=True) + EPS)
    return vc * rstd, rstd


def sgu_fwd(z1, ln_g, ln_b, ws, bst, *, name):
    L, W2 = z1.shape
    W = W2 // 2
    ng = W // LANES

    def body(z_ref, g_ref, b_ref, ws_ref, bs_ref, o_ref):
        z, _ = _gelu_parts(z_ref[...].astype(F32))
        xhat, _ = _layernorm(z[:, W:])
        vln = (xhat * g_ref[...] + b_ref[...]).astype(BF16)
        for gi in range(ng):
            cs = slice(gi * LANES, (gi + 1) * LANES)
            s = _dot(ws_ref[gi], vln[:, cs]) + bs_ref[:, gi:gi + 1]
            o_ref[:, cs] = (z[:, cs] * s).astype(BF16)

    vec = _full((1, W))
    return pl.pallas_call(
        body, name=name, grid=(L // BLK,),
        in_specs=[pl.BlockSpec((BLK, W2), lambda n: (n, 0)), vec, vec, _full((ng, LANES, LANES)), _full((BLK, ng))],
        out_specs=pl.BlockSpec((BLK, W), lambda n: (n, 0)),
        out_shape=jax.ShapeDtypeStruct((L, W), BF16),
        compiler_params=_cp(("parallel",)),
    )(z1, ln_g, ln_b, ws, bst)


def sgu_bwd(z1, dus, ln_g, ln_b, ws, wst, bst, *, name):
    L, W2 = z1.shape
    W = W2 // 2
    ng = W // LANES

    def body(z_ref, d_ref, g_ref, b_ref, ws_ref, wst_ref, bs_ref, dz_ref, dws_ref, dbs_ref, dg_ref, db_ref, dv_scr):
        @pl.when(pl.program_id(0) == 0)
        def _():
            dws_ref[...] = jnp.zeros_like(dws_ref)
            dbs_ref[...] = jnp.zeros_like(dbs_ref)
            dg_ref[...] = jnp.zeros_like(dg_ref)
            db_ref[...] = jnp.zeros_like(db_ref)

        zp = z_ref[...].astype(F32)
        z, th = _gelu_parts(zp)
        xhat, rstd = _layernorm(z[:, W:])
        vln = (xhat * g_ref[...] + b_ref[...]).astype(BF16)
        d = d_ref[...].astype(F32)
        lane = lax.broadcasted_iota(jnp.int32, (1, LANES), 1)
        dbs = jnp.zeros((BLK, LANES), F32)
        dgel = _gelu_grad(zp, th)
        for gi in range(ng):
            cs = slice(gi * LANES, (gi + 1) * LANES)
            s = _dot(ws_ref[gi], vln[:, cs]) + bs_ref[:, gi:gi + 1]
            dz_ref[:, cs] = (d[:, cs] * s * dgel[:, cs]).astype(BF16)
            ds = d[:, cs] * z[:, cs]
            dbs = dbs + jnp.where(lane == gi, jnp.sum(ds, axis=-1, keepdims=True), 0.0)
            dsb = ds.astype(BF16)
            dws_ref[gi] += _dot_nt(dsb, vln[:, cs])
            dv_scr[:, cs] = _dot(wst_ref[gi], dsb)
        dbs_ref[...] += dbs
        dvln = dv_scr[...]
        dg_ref[...] += _colsum(dvln * xhat)
        db_ref[...] += _colsum(dvln)
        dxh = dvln * g_ref[...]
        dv = rstd * (dxh - jnp.mean(dxh, axis=-1, keepdims=True) - xhat * jnp.mean(dxh * xhat, axis=-1, keepdims=True))
        dz_ref[:, W:] = (dv * dgel[:, W:]).astype(BF16)

    vec = _full((1, W))
    return pl.pallas_call(
        body, name=name, grid=(L // BLK,),
        in_specs=[pl.BlockSpec((BLK, W2), lambda n: (n, 0)), pl.BlockSpec((BLK, W), lambda n: (n, 0)), vec, vec,
                  _full((ng, LANES, LANES)), _full((ng, LANES, LANES)), _full((BLK, ng))],
        out_specs=[pl.BlockSpec((BLK, W2), lambda n: (n, 0)), _full((ng, LANES, LANES)), _full((BLK, LANES)), vec, vec],
        out_shape=[jax.ShapeDtypeStruct((L, W2), BF16), jax.ShapeDtypeStruct((ng, LANES, LANES), F32),
                   jax.ShapeDtypeStruct((BLK, LANES), F32), jax.ShapeDtypeStruct((1, W), F32), jax.ShapeDtypeStruct((1, W), F32)],
        scratch_shapes=[pltpu.VMEM((BLK, W), F32)],
        compiler_params=_cp(("arbitrary",)),
    )(z1, dus, ln_g, ln_b, ws, wst, bst)


def _adamw_math(w, m, v, g):
    m_ = ADAM_B1 * m + (1.0 - ADAM_B1) * g
    v_ = ADAM_B2 * v + (1.0 - ADAM_B2) * (g * g)
    return -ADAM_LR * ((m_ / BC1) / (jnp.sqrt(v_ / BC2) + ADAM_EPS) + ADAM_WD * w), m_, v_


def adamw(w, m, v, gparts, *, tr, name):
    NL, R, Wd = w.shape
    nr = R // tr

    def body(w_ref, m_ref, v_ref, *rest):
        gp_refs, (g_ref, d_ref, nm_ref, nv_ref) = rest[:NL], rest[NL:]
        for l in range(NL):
            @pl.when(pl.program_id(0) == l)
            def _():
                g = gp_refs[l][0].astype(F32)
                for s in range(1, gp_refs[l].shape[0]):
                    g = g + gp_refs[l][s].astype(F32)
                g_ref[...] = g
                d_ref[...], nm_ref[...], nv_ref[...] = _adamw_math(w_ref[...], m_ref[...], v_ref[...], g)

    row = pl.BlockSpec((None, tr, Wd), lambda l, i: (l, i, 0))
    gspecs = [pl.BlockSpec((gparts[l].shape[0], tr, Wd), (lambda l_, i, l=l: (0, jnp.clip(i + (l_ - l) * nr, 0, nr - 1), 0)))
              for l in range(NL)]
    return pl.pallas_call(
        body, name=name, grid=(NL, nr),
        in_specs=[row, row, row] + gspecs, out_specs=[row] * 4, out_shape=[jax.ShapeDtypeStruct((NL, R, Wd), F32)] * 4,
        compiler_params=_cp(("arbitrary", "arbitrary")),
    )(w, m, v, *gparts)


def small_update(gpacks, me, params, loss_row, *, name):
    n = len(params)

    def body(me_ref, gp_ref, *refs):
        ins, outs, gs_ref = refs[:3 * n], refs[3 * n:-1], refs[-1]
        gs_ref[...] = gp_ref[0].astype(F32)
        for dv in range(1, N_DEV):
            gs_ref[...] += gp_ref[dv].astype(F32)
        for p, (w, _, _, off, per_dev) in enumerate(params):
            w_ref, m_ref, v_ref = ins[3 * p:3 * p + 3]
            g_ref, d_ref, nm_ref, nv_ref = outs[4 * p:4 * p + 4]
            rows, cols = w.shape
            if cols == LANES and rows % 8 == 0 and not per_dev:
                g = gs_ref[off:off + rows, :]
                g_ref[...] = g
                d_ref[...], nm_ref[...], nv_ref[...] = _adamw_math(w_ref[...], m_ref[...], v_ref[...], g)
                continue
            chunks = -(-cols // LANES)
            base = off + me_ref[0] * per_dev if per_dev else off
            for i in range(rows):
                for j in range(chunks):
                    wd = min(LANES, cols - j * LANES)
                    at = (slice(i, i + 1), slice(j * LANES, j * LANES + wd))
                    g = gs_ref[pl.ds(base + i * chunks + j, 1), 0:wd]
                    g_ref[at] = g
                    d_ref[at], nm_ref[at], nv_ref[at] = _adamw_math(w_ref[at], m_ref[at], v_ref[at], g)
        outs[-1][...] = jnp.sum(gs_ref[loss_row:loss_row + 1, :], axis=1, keepdims=True)

    flat = [a for w, m, v, _, _ in params for a in (w, m, v)]
    out_shape = [jax.ShapeDtypeStruct(w.shape, F32) for w, _, _, _, _ in params for _ in range(4)] + [jax.ShapeDtypeStruct((1, 1), F32)]
    return pl.pallas_call(
        body, name=name, grid=(1,),
        in_specs=[pl.BlockSpec(memory_space=pltpu.SMEM), _full(gpacks.shape)] + [_full(a.shape) for a in flat],
        out_specs=[_full(o.shape) for o in out_shape], out_shape=out_shape,
        scratch_shapes=[pltpu.VMEM(gpacks.shape[1:], F32)],
        compiler_params=_cp(("arbitrary",)),
    )(me, gpacks, *flat)


def ada_fwd_mm(cs, w_ada, b_loc, *, name):
    R, D = cs.shape
    nl, _, n = w_ada.shape

    def body(c_ref, w_ref, b_ref, s_ref, m_ref):
        c = c_ref[...]
        s = c * jax.nn.sigmoid(c)
        s_ref[...] = s
        for i in range(nl):
            m_ref[i] = _dot(s.astype(BF16), w_ref[i].astype(BF16)) + b_ref[i:i + 1, :]

    return pl.pallas_call(
        body, name=name, in_specs=[_full((R, D)), _full((nl, D, n)), _full((nl, n))],
        out_specs=[_full((R, D)), _full((nl, R, n))], grid=(1,),
        out_shape=[jax.ShapeDtypeStruct((R, D), F32), jax.ShapeDtypeStruct((nl, R, n), F32)],
        compiler_params=_cp(("arbitrary",)),
    )(cs, w_ada, b_loc)


def ada_bwd_mm(s, c_ctx, dall, w_ada, *, name):
    R, D = s.shape
    nl, _, n = w_ada.shape

    def body(s_ref, cc_ref, d_ref, w_ref, gw_ref, dcc_ref):
        sb = s_ref[...].astype(BF16)
        row = lax.broadcasted_iota(jnp.int32, (R, 1), 0)
        dctx = d_ref[0, 1:2, :]
        for dv in range(1, N_DEV):
            dctx = dctx + d_ref[dv, 1:2, :]
        for i in range(nl):
            dm = jnp.zeros((R, n), F32)
            for dv in range(N_DEV):
                dm = dm + jnp.where(row == dv, d_ref[dv, 2 * i:2 * i + 1, :], 0.0)
            if i == 0:
                dm = dm + jnp.where(row == N_DEV, dctx, 0.0)
            gw_ref[i] = _dot_tn(sb, dm.astype(BF16))
        cc = cc_ref[...]
        sg = jax.nn.sigmoid(cc)
        ds = _dot_nt(jnp.broadcast_to(dctx, (8, n)).astype(BF16), w_ref[0].astype(BF16))
        dcc_ref[...] = ds * (sg * (1.0 + cc * (1.0 - sg)))

    return pl.pallas_call(
        body, name=name, grid=(1,),
        in_specs=[_full((R, D)), _full((1, D)), _full((N_DEV, 3, n)), _full((nl, D, n))],
        out_specs=[_full((nl, D, n)), _full((8, D))],
        out_shape=[jax.ShapeDtypeStruct((nl, D, n), F32), jax.ShapeDtypeStruct((8, D), F32)],
        compiler_params=_cp(("arbitrary",)),
    )(s, c_ctx, dall, w_ada)


def _place():
    x, y, c = lax.axis_index("x"), lax.axis_index("y"), lax.axis_index("c")
    return x, y, c


def _lin(p):
    return 4 * p[0] + 2 * p[1] + p[2]


def all_gather_small(xb, *, name):
    R, W = xb.shape

    def body(x_ref, out_ref, send_sems, recv_sems, local_sem):
        x, y, c = _place()
        me = _lin((x, y, c))
        mine = pltpu.make_async_copy(x_ref, out_ref.at[me], local_sem)
        mine.start()
        copies = []
        for k in range(1, N_DEV):
            peer = (x ^ (k >> 2), y ^ ((k >> 1) & 1), c ^ (k & 1))
            mk = lambda dst, k=k, peer=peer: pltpu.make_async_remote_copy(
                src_ref=x_ref, dst_ref=dst, send_sem=send_sems.at[k - 1], recv_sem=recv_sems.at[k - 1], device_id=peer, device_id_type=MESH)
            mk(out_ref.at[me]).start()
            copies.append(mk(out_ref.at[_lin(peer)]))
        for cp in copies:
            cp.wait_recv()
        for cp in copies:
            cp.wait_send()
        mine.wait()

    vm = pl.BlockSpec(memory_space=pltpu.VMEM)
    return pl.pallas_call(
        body, name=name, in_specs=[vm], out_specs=vm, out_shape=jax.ShapeDtypeStruct((N_DEV, R, W), xb.dtype),
        scratch_shapes=[pltpu.SemaphoreType.DMA((7,)), pltpu.SemaphoreType.DMA((7,)), pltpu.SemaphoreType.DMA],
        compiler_params=pltpu.CompilerParams(vmem_limit_bytes=VMEM_LIMIT),
    )(xb)


HBM_SPEC = pl.BlockSpec(memory_space=pltpu.HBM)
SEM_SPEC = pl.BlockSpec(memory_space=pltpu.SEMAPHORE)
ORDERED_EFFECT = pltpu.SideEffectType.DATAFLOW_SIDE_EFFECTING


def _exchange_copies(srcs, lands, sems, scatter):
    x, y, c = _place()
    me = _lin((x, y, c))
    for j in range(len(srcs)):
        r = lands[j].shape[0] // N_DEV
        block = lambda d, j=j, r=r: pl.ds(pl.multiple_of(d * r, 16), r)
        for k in range(1, N_DEV):
            peer = (x ^ (k >> 2), y ^ ((k >> 1) & 1), c ^ (k & 1))
            src = srcs[j].at[block(_lin(peer)), :] if scatter else srcs[j]
            mk = lambda dst, j=j, k=k, peer=peer, src=src: pltpu.make_async_remote_copy(
                src_ref=src, dst_ref=dst, send_sem=sems[2 * j].at[k - 1], recv_sem=sems[2 * j + 1].at[k - 1],
                device_id=peer, device_id_type=MESH)
            yield mk(lands[j].at[block(me), :]), mk(lands[j].at[block(_lin(peer)), :])


def exchange_start(srcs, lands, *, scatter, name):
    nw = len(srcs)

    def body(*refs):
        for start, _ in _exchange_copies(refs[:nw], refs[nw:2 * nw], refs[2 * nw:4 * nw], scatter):
            start.start()
        refs[-1][...] = jnp.zeros_like(refs[-1])

    thru = [pltpu.HBM(a.shape, a.dtype) for a in (*srcs, *lands)]
    res = pl.pallas_call(
        body, name=name, in_specs=[HBM_SPEC] * (2 * nw),
        out_specs=[SEM_SPEC] * (2 * nw) + [HBM_SPEC] * (2 * nw) + [pl.BlockSpec(memory_space=pltpu.VMEM)],
        out_shape=[pltpu.SemaphoreType.DMA((N_DEV - 1,))] * (2 * nw) + thru + [jax.ShapeDtypeStruct((8, LANES), F32)],
        input_output_aliases={i: 2 * nw + i for i in range(2 * nw)},
        compiler_params=pltpu.CompilerParams(has_side_effects=ORDERED_EFFECT),
    )(*[pltpu.with_memory_space_constraint(a, pltpu.HBM) for a in (*srcs, *lands)])
    return res[:2 * nw], res[2 * nw:3 * nw], res[3 * nw:4 * nw], res[-1]


def exchange_wait(srcs, lands, sems, after, *, scatter, name):
    nw = len(srcs)
    after = list(after) if isinstance(after, (list, tuple)) else [after]

    def body(*refs):
        for _, arrive in _exchange_copies(refs[:nw], refs[nw:2 * nw], refs[2 * nw:4 * nw], scatter):
            arrive.wait_send()
            arrive.wait_recv()

    res = pl.pallas_call(
        body, name=name, in_specs=[HBM_SPEC] * (2 * nw) + [SEM_SPEC] * (2 * nw) + [pl.BlockSpec(memory_space=pl.ANY)] * len(after),
        out_specs=[HBM_SPEC] * (2 * nw), out_shape=[pltpu.HBM(a.shape, a.dtype) for a in (*srcs, *lands)],
        input_output_aliases={i: i for i in range(2 * nw)},
        compiler_params=pltpu.CompilerParams(has_side_effects=ORDERED_EFFECT),
    )(*srcs, *lands, *sems, *after)
    return res[nw:]


def place_own(srcs, rows, me, *, scatter, name):
    nw = len(srcs)
    lands = [lax.empty((N_DEV * r, s_.shape[1]), s_.dtype) for r, s_ in zip(rows, srcs)]

    def body(me_ref, *refs):
        for j in range(nw):
            refs[2 * nw + j][...] = refs[j][...]

    mine = lambda i, me_ref: (me_ref[0], 0)
    src_at = mine if scatter else (lambda i, me_ref: (0, 0))
    blocks = [(r, s_.shape[1]) for r, s_ in zip(rows, srcs)]
    return pl.pallas_call(
        body, name=name,
        grid_spec=pltpu.PrefetchScalarGridSpec(
            num_scalar_prefetch=1, grid=(1,),
            in_specs=[pl.BlockSpec(b_, src_at) for b_ in blocks] + [pl.BlockSpec(memory_space=pl.ANY)] * nw,
            out_specs=[pl.BlockSpec(b_, mine) for b_ in blocks]),
        out_shape=[jax.ShapeDtypeStruct(l_.shape, l_.dtype) for l_ in lands],
        input_output_aliases={1 + nw + j: j for j in range(nw)},
        compiler_params=_cp(("arbitrary",)),
    )(jnp.reshape(me, (1,)).astype(jnp.int32), *srcs, *lands)


def _rope_tables(L):
    t = jnp.arange(L)
    inv = ROPE_BASE ** (-jnp.arange(ROPE_FREQS, dtype=F32) / ROPE_FREQS)
    ar = (t // GRID_W).astype(F32)[:, None] * inv
    ac = (t % GRID_W).astype(F32)[:, None] * inv
    z = jnp.zeros_like(ar)
    cos = jnp.concatenate([jnp.cos(ar), jnp.cos(ar), jnp.cos(ac), jnp.cos(ac)], axis=1)
    sa = jnp.concatenate([-jnp.sin(ar), z, -jnp.sin(ac), z], axis=1)
    sb = jnp.concatenate([z, jnp.sin(ar), z, jnp.sin(ac)], axis=1)
    return tuple(jnp.tile(a, (1, LANES // HEAD_DIM)) for a in (cos, sa, sb))


def _nat2d(a):
    return a.reshape(1, -1) if a.ndim == 1 else a.reshape(-1, a.shape[-1])


def _pack_rows(a):
    rows, cols = a.shape
    chunks = -(-cols // LANES)
    f = jnp.pad(a, ((0, 0), (0, chunks * LANES - cols))).reshape(rows * chunks, LANES)
    return jnp.pad(f, ((0, -f.shape[0] % 8), (0, 0)))


def _rows128(a):
    f = a.reshape(-1)
    n = -(-f.shape[0] // (8 * LANES)) * 8 * LANES
    return jnp.pad(f, (0, n - f.shape[0])).reshape(-1, LANES)


def kernel(x, c, ctx, c_ctx, w_ada, b_ada, g_mix_pre, g_mix_post, g_ffn_pre, g_ffn_post, w_in_even, w_pool, pool_scale, attn_sink, w_out_even, w_in_odd, sgu_ln_g, sgu_ln_b, sgu_w, sgu_b, w_out_odd, w_ffn_up, ffn_conv_w, ffn_conv_b, w_ffn_down, loss_target, m_c_ctx, m_w_ada, m_b_ada, m_g_mix_pre, m_g_mix_post, m_g_ffn_pre, m_g_ffn_post, m_w_in_even, m_w_pool, m_pool_scale, m_attn_sink, m_w_out_even, m_w_in_odd, m_sgu_ln_g, m_sgu_ln_b, m_sgu_w, m_sgu_b, m_w_out_odd, m_w_ffn_up, m_ffn_conv_w, m_ffn_conv_b, m_w_ffn_down, v_c_ctx, v_w_ada, v_b_ada, v_g_mix_pre, v_g_mix_post, v_g_ffn_pre, v_g_ffn_post, v_w_in_even, v_w_pool, v_pool_scale, v_attn_sink, v_w_out_even, v_w_in_odd, v_sgu_ln_g, v_sgu_ln_b, v_sgu_w, v_sgu_b, v_w_out_odd, v_w_ffn_up, v_ffn_conv_w, v_ffn_conv_b, v_w_ffn_down):
    P = dict(c_ctx=c_ctx, w_ada=w_ada, b_ada=b_ada, g_mix_pre=g_mix_pre, g_mix_post=g_mix_post, g_ffn_pre=g_ffn_pre,
             g_ffn_post=g_ffn_post, w_in_even=w_in_even, w_pool=w_pool, pool_scale=pool_scale, attn_sink=attn_sink,
             w_out_even=w_out_even, w_in_odd=w_in_odd, sgu_ln_g=sgu_ln_g, sgu_ln_b=sgu_ln_b, sgu_w=sgu_w, sgu_b=sgu_b,
             w_out_odd=w_out_odd, w_ffn_up=w_ffn_up, ffn_conv_w=ffn_conv_w, ffn_conv_b=ffn_conv_b, w_ffn_down=w_ffn_down)
    M = dict(c_ctx=m_c_ctx, w_ada=m_w_ada, b_ada=m_b_ada, g_mix_pre=m_g_mix_pre, g_mix_post=m_g_mix_post, g_ffn_pre=m_g_ffn_pre,
             g_ffn_post=m_g_ffn_post, w_in_even=m_w_in_even, w_pool=m_w_pool, pool_scale=m_pool_scale, attn_sink=m_attn_sink,
             w_out_even=m_w_out_even, w_in_odd=m_w_in_odd, sgu_ln_g=m_sgu_ln_g, sgu_ln_b=m_sgu_ln_b, sgu_w=m_sgu_w, sgu_b=m_sgu_b,
             w_out_odd=m_w_out_odd, w_ffn_up=m_w_ffn_up, ffn_conv_w=m_ffn_conv_w, ffn_conv_b=m_ffn_conv_b, w_ffn_down=m_w_ffn_down)
    V = dict(c_ctx=v_c_ctx, w_ada=v_w_ada, b_ada=v_b_ada, g_mix_pre=v_g_mix_pre, g_mix_post=v_g_mix_post, g_ffn_pre=v_g_ffn_pre,
             g_ffn_post=v_g_ffn_post, w_in_even=v_w_in_even, w_pool=v_w_pool, pool_scale=v_pool_scale, attn_sink=v_attn_sink,
             w_out_even=v_w_out_even, w_in_odd=v_w_in_odd, sgu_ln_g=v_sgu_ln_g, sgu_ln_b=v_sgu_ln_b, sgu_w=v_sgu_w, sgu_b=v_sgu_b,
             w_out_odd=v_w_out_odd, w_ffn_up=v_w_ffn_up, ffn_conv_w=v_ffn_conv_w, ffn_conv_b=v_ffn_conv_b, w_ffn_down=v_w_ffn_down)

    x = x[0]
    ctx = ctx[0]
    target = loss_target[0]
    L, D = x.shape
    C = ctx.shape[0]
    tm = min(512, L)
    conv_rows = min(512, L)
    me = 4 * lax.axis_index("x") + 2 * lax.axis_index("y") + lax.axis_index("c")
    n_ada = w_ada.shape[2]
    F = w_ffn_down.shape[1] * N_DEV
    half_f = F // 2

    n_cw = ffn_conv_w.shape[2]
    small = jnp.concatenate([_rows128(c), _rows128(sgu_ln_g), _rows128(sgu_ln_b), _rows128(ffn_conv_w)], axis=0)
    small_all = all_gather_small(small, name="gather_small_inputs")
    c_all = small_all[:, :8].reshape(N_DEV, D)
    ln_g = small_all[:, 8].reshape(1, D)
    ln_b = small_all[:, 16].reshape(1, D)
    conv_w = small_all[:, 24:].reshape(N_DEV, -1)[:, :2 * 3 * n_cw].reshape(N_DEV, 2, 3, n_cw)
    conv_w = conv_w.transpose(1, 2, 0, 3).reshape(2, 3, 2 * F)

    cs = jnp.concatenate([c_all, c_ctx[None, :], jnp.zeros((7, D), F32)], axis=0)
    b_loc = lax.dynamic_slice(b_ada, (0, me * n_ada), (2, n_ada))
    silu_c, mods_loc = ada_fwd_mm(cs, w_ada, b_loc, name="ada_fwd")
    mods_all = all_gather_small(mods_loc.reshape(-1, LANES), name="gather_mods")

    shards = [s.astype(BF16) for s in (w_in_even[0].T, w_out_even[0], w_ffn_up[0].T, w_ffn_down[0],
                                       w_in_odd[0].T, w_out_odd[0], w_ffn_up[1].T, w_ffn_down[1])]
    shards, mods_all = lax.optimization_barrier((shards, mods_all))
    w_sems, w_srcs, w_lands, _ = exchange_start(shards, place_own(shards, [s.shape[0] for s in shards], me, scatter=False, name="gather_own"),
                                              scatter=False, name="gather_start")

    def weight(j, after):
        return exchange_wait([w_srcs[j]], [w_lands[j]], w_sems[2 * j:2 * j + 2], after, scatter=False, name=f"gather_wait_{j}")[0]

    mods_all = mods_all.reshape(N_DEV, 2, 16, n_ada).transpose(1, 2, 0, 3).reshape(2, 16, 6 * D)
    mod = lambda i, row: [m_[None, :] for m_ in jnp.split(lax.dynamic_index_in_dim(mods_all[i], row, 0, False), 6)]
    sh_m, sc_m, gt_m, sh_f, sc_f, gt_f = zip(mod(0, me), mod(1, me))
    csh_m, csc_m = mod(0, N_DEV)[:2]

    row = lambda a, i: a[i][None, :]

    cos, sa, sb = _rope_tables(L)
    sink = attn_sink[0]
    bst = sgu_b[0].T
    sgu_wb, sgu_wtb = sgu_w[0].astype(BF16), sgu_w[0].swapaxes(1, 2).astype(BF16)
    wup, wdn = [None, None], [None, None]

    def ffn_fwd(i, xin):
        wup[i] = weight(2 + 4 * i, xin)
        h, hu = pre_mm(xin, row(g_ffn_pre, i), sh_f[i], sc_f[i], wup[i], tm=tm, tn=half_f, name=f"ffn_up_{i}")
        a, s1, s2 = conv_fwd(hu, conv_w[i], ffn_conv_b[i][None, :], rows=conv_rows, wblk=2 * LANES, name=f"ffn_conv_{i}")
        wdn[i] = weight(3 + 4 * i, a)
        res = mm_post([a], wdn[i], xin, row(g_ffn_post, i), gt_f[i], tm=tm, target=target if i == 1 else None, name=f"ffn_down_{i}")
        return (h, (hu, s1, s2), a, *res)

    first_mod, cos, sa, sb = lax.optimization_barrier((sh_m[0], cos, sa, sb))
    win_e = permute_heads(weight(0, first_mod))
    h0, u, q, kv = inproj_even(x, row(g_mix_pre, 0), sh_m[0], sc_m[0], win_e, cos, sa, sb, tm=tm, name="in_even")
    hc, kvc = pre_mm(ctx, row(g_mix_pre, 0), csh_m, csc_m, win_e, tm=C, tn=2 * LANES, w_row_off=8 * LANES, name="in_even_ctx")
    pa = [pool_fwd(u, w_pool[0], pool_scale, name="pool_fwd"), attn_fwd(q, kv, kvc, sink, name="attn_fwd")]
    wout_e = permute_heads(weight(1, pa[1]))
    y0, x1 = mm_post(pa, wout_e, x, row(g_mix_post, 0), gt_m[0], tm=tm, name="out_even")
    h1, hu0, a0, f0, x2 = ffn_fwd(0, x1)
    win_o = weight(4, x2)
    h2, z1 = pre_mm(x2, row(g_mix_pre, 1), sh_m[1], sc_m[1], win_o, tm=tm, tn=D, name="in_odd")
    us = sgu_fwd(z1, ln_g, ln_b, sgu_wb, bst, name="sgu_fwd")
    wout_o = weight(5, us)
    y1, x3 = mm_post([us], wout_o, x2, row(g_mix_post, 1), gt_m[1], tm=tm, name="out_odd")
    h3, hu1, a1, f1, dx4, loss_part = ffn_fwd(1, x3)

    g_srcs, g_lands, g_sems = [], [], []

    def scatter(grads, nm):
        own = place_own(grads, [g.shape[0] // N_DEV for g in grads], me, scatter=True, name=nm.replace("start", "own"))
        sems, srcs, lands, tok = exchange_start(grads, own, scatter=True, name=nm)
        g_srcs.extend(srcs)
        g_lands.extend(lands)
        g_sems.extend(sems)
        return tok[0:1, 0:1]

    def ffn_bwd(i, dxo, xin, h, hu, a, f, g_post):
        dyf, da, dg_post, dgt = post_bwd_mm(dxo, f, g_post, gt_f[i], wdn[i], tm=tm, name=f"ffn_down_bwd_{i}")
        dhg, dhu, dcwg, dcwu, dcbg, dcbu = conv_bwd(da, hu[1], hu[2], hu[0], conv_w[i], rows=conv_rows, wblk=2 * LANES,
                                                    name=f"ffn_conv_bwd_{i}")
        dxin, dg_pre, dsh, dsc = mm_pre_bwd([dhg, dhu], wup[i], xin, dxo, row(g_ffn_pre, i), sc_f[i], tm=tm,
                                            name=f"ffn_up_bwd_{i}")
        g_dn = wgrad([a], dyf, tr=2 * LANES, name=f"wgrad_down_{i}")
        g_up = wgrad([dhg, dhu], h, tr=2 * LANES, name=f"wgrad_up_{i}")
        tok = scatter([g_dn, g_up], f"scatter_start_ffn_{i}")
        return dxin, tok, dict(g_ffn_post=dg_post, g_ffn_pre=dg_pre, gt_f=dgt, sh_f=dsh, sc_f=dsc,
                               ffn_conv_w=jnp.concatenate([dcwg, dcwu], axis=1), ffn_conv_b=jnp.concatenate([dcbg, dcbu], axis=1)[0])

    dx3, tok, sf1 = ffn_bwd(1, dx4, x3, h3, hu1, a1, f1, row(g_ffn_post, 1))
    dy1, dus, dg_mpost1, dgt_m1 = post_bwd_mm(dx3, y1, row(g_mix_post, 1) + tok, gt_m[1], wout_o, tm=tm, name="out_odd_bwd")
    dz1, dws, dbs, dlng, dlnb = sgu_bwd(z1, dus, ln_g, ln_b, sgu_wb, sgu_wtb, bst, name="sgu_bwd")
    dx2, dg_mpre1, dsh_m1, dsc_m1 = mm_pre_bwd([dz1], win_o, x2, dx3, row(g_mix_pre, 1), sc_m[1], tm=tm, name="in_odd_bwd")
    tok = scatter([wgrad([us], dy1, tr=2 * LANES, name="wgrad_out_odd"), wgrad([dz1], h2, tr=2 * LANES, name="wgrad_in_odd")],
                  "scatter_start_mix_1")

    dx1, tok, sf0 = ffn_bwd(0, dx2, x1, h1, hu0, a0, f0, row(g_ffn_post, 0) + tok)
    dy0, dpa, dg_mpost0, dgt_m0 = post_bwd_mm(dx1, y0, row(g_mix_post, 0) + tok, gt_m[0], wout_e, tm=tm, name="out_even_bwd")
    tok = scatter([permute_heads(wgrad(pa, dy0, tr=2 * LANES, name="wgrad_out_even"), inverse=True)], "scatter_start_out_0")
    du, dwp, dps = pool_bwd(u, dpa, w_pool[0], pool_scale + tok, name="pool_bwd")
    dq, dkv, dkvc, dsink = attn_bwd(q, kv, kvc, sink, dpa, cos, sa, sb, name="attn_bwd")
    dz0 = jnp.concatenate([du, dq, dkv], axis=1)
    dzc = jnp.concatenate([jnp.zeros((C, 8 * LANES), BF16), dkvc], axis=1)
    tok = scatter([permute_heads(wgrad([dz0], h0, tr=2 * LANES, extra=(dzc, hc), name="wgrad_in_even"), inverse=True)],
                  "scatter_start_in_0")
    grad_x, dg_mpre0, dsh_m0, dsc_m0 = mm_pre_bwd([dz0], win_e, x, dx1, row(g_mix_pre, 0) + tok, sc_m[0], tm=tm,
                                                  name="in_even_bwd")
    _, dg_mpre0c, dcsh, dcsc = mm_pre_bwd([dkvc], win_e, ctx, None, row(g_mix_pre, 0), csc_m, tm=C,
                                          w_row_off=8 * LANES, name="in_even_ctx_bwd")

    out, ran = {}, {}

    def update(name, lands, transposed):
        w_, m_, v_ = (a.transpose(0, 2, 1) if transposed else a for a in (P[name], M[name], V[name]))
        r = w_.shape[1]
        tr = r // 4 if r % 64 == 0 and r > 256 else r
        res = adamw(w_, m_, v_, [l_.reshape(N_DEV, r, l_.shape[1]) for l_ in lands], tr=tr, name=f"adamw_{name}")
        ran[name] = res[0]
        for kind, val in zip(("grad", "delta", "new_m", "new_v"), res):
            out[(kind, name)] = val.transpose(0, 2, 1) if transposed else val

    zero = jnp.zeros((1, D), F32)
    dmod0 = jnp.concatenate([dsh_m0, dsc_m0, dgt_m0, sf0["sh_f"], sf0["sc_f"], sf0["gt_f"]], axis=1)
    dmodc = jnp.concatenate([dcsh, dcsc, zero, zero, zero, zero], axis=1)
    dmod1 = jnp.concatenate([dsh_m1, dsc_m1, dgt_m1, sf1["sh_f"], sf1["sc_f"], sf1["gt_f"]], axis=1)
    dmods = jnp.concatenate([dmod0, dmodc, dmod1], axis=0)
    dm = dmods.reshape(-1, LANES).astype(BF16)
    d_sems, d_srcs, d_lands, d_tok = exchange_start(
        [dm], place_own([dm], [dm.shape[0]], me, scatter=False, name="dmods_own"), scatter=False, name="dmods_start")
    slots = exchange_wait(g_srcs[:6], g_lands[:6], g_sems[:12], d_tok, scatter=True, name="scatter_wait_early")
    early = slots
    update("w_ffn_down", [slots[4], slots[0]], False)
    update("w_in_odd", [slots[3]], True)
    update("w_out_odd", [slots[2]], False)
    updated = lambda names: [ran[k] for k in names]
    dmods_all = exchange_wait(d_srcs, d_lands, d_sems, updated(("w_out_odd",)), scatter=False, name="dmods_wait")[0]
    dall = lax.dynamic_index_in_dim(dmods_all.astype(F32).reshape(N_DEV, 3, N_DEV, n_ada), me, 2, False)
    g_w_ada, dcc = ada_bwd_mm(silu_c, c_ctx[None, :], dall, w_ada, name="ada_bwd")

    rep = dict(
        c_ctx=dcc[0:1],
        b_ada=jnp.concatenate([dmod0 + dmodc, dmod1]),
        g_mix_pre=jnp.concatenate([dg_mpre0 + dg_mpre0c, dg_mpre1]),
        g_mix_post=jnp.concatenate([dg_mpost0, dg_mpost1]),
        g_ffn_pre=jnp.concatenate([sf0["g_ffn_pre"], sf1["g_ffn_pre"]]),
        g_ffn_post=jnp.concatenate([sf0["g_ffn_post"], sf1["g_ffn_post"]]),
        w_pool=_nat2d(dwp), pool_scale=dps, attn_sink=dsink[:, :N_Q_HEADS],
        sgu_w=_nat2d(dws), sgu_b=dbs[:, :sgu_b.shape[1]].T,
        ffn_conv_b=jnp.stack([sf0["ffn_conv_b"], sf1["ffn_conv_b"]]),
    )
    hi = loss_part.astype(BF16).astype(F32)
    mid = (loss_part - hi).astype(BF16).astype(F32)
    loss_piece = jnp.pad(jnp.concatenate([hi, mid, loss_part - hi - mid], axis=1), ((0, 7), (0, LANES - 3)))
    conv_g = jnp.stack([sf0["ffn_conv_w"], sf1["ffn_conv_w"]]).reshape(2 * 3, N_DEV, n_cw).swapaxes(0, 1)
    shard_full = dict(sgu_ln_g=dlng.reshape(N_DEV, LANES), sgu_ln_b=dlnb.reshape(N_DEV, LANES),
                      ffn_conv_w=jnp.concatenate([_pack_rows(conv_g[d]) for d in range(N_DEV)], axis=0))
    small_names = list(rep) + list(shard_full)
    pieces = [_pack_rows(rep[k]) for k in rep] + list(shard_full.values()) + [loss_piece]
    sizes = [p.shape[0] for p in pieces]
    offs = [sum(sizes[:i]) for i in range(len(sizes))]
    pieces.append(jnp.zeros((-sum(sizes) % 16, LANES), F32))
    gpack = jnp.concatenate(pieces, axis=0).astype(BF16)
    own = place_own([gpack], [gpack.shape[0]], me, scatter=False, name="smallgrad_own")
    s_sems, s_srcs, s_lands, small_tok = exchange_start([gpack], own, scatter=False, name="smallgrad_start")

    slots = exchange_wait(g_srcs[6:], g_lands[6:], g_sems[12:], small_tok, scatter=True, name="scatter_wait_late")
    update("w_in_even", [slots[1]], True)
    update("w_out_even", [slots[0]], False)
    update("w_ffn_up", [early[5], early[1]], True)
    res = adamw(w_ada, m_w_ada, v_w_ada, [g_w_ada[l][None] for l in range(w_ada.shape[0])], tr=D // 4, name="adamw_w_ada")
    ran["w_ada"] = res[0]
    for kind, val in zip(("grad", "delta", "new_m", "new_v"), res):
        out[(kind, "w_ada")] = val

    gpacks = exchange_wait(s_srcs, s_lands, s_sems, updated(("w_ada",)), scatter=False,
                           name="smallgrad_wait")[0]
    per_dev = {k: shard_full[k].shape[0] // N_DEV for k in shard_full}
    params = [(_nat2d(P[k]), _nat2d(M[k]), _nat2d(V[k]), offs[i], per_dev.get(k, 0)) for i, k in enumerate(small_names)]
    res = small_update(gpacks.reshape(N_DEV, -1, LANES), jnp.reshape(me, (1,)).astype(jnp.int32), params, offs[-1], name="adamw_small")
    for i, k in enumerate(small_names):
        for kind, val in zip(("grad", "delta", "new_m", "new_v"), res[4 * i:4 * i + 4]):
            out[(kind, k)] = val.reshape(P[k].shape)
    loss = res[-1][0, 0]

    names = list(P)
    final = [loss, grad_x[None]]
    for kind in ("grad", "delta", "new_m", "new_v"):
        for k in names:
            val = out[(kind, k)]
            final.append(val)
    return tuple(final)
```

```python
import functools
import math

import jax
import jax.numpy as jnp
from jax import lax
from jax.experimental import pallas as pl
from jax.experimental.pallas import tpu as pltpu

F32 = jnp.float32
BF16 = jnp.bfloat16
MESH = pl.DeviceIdType.MESH
N_DEV = 8
LANES = 128
VMEM_LIMIT = 48 * 1024 * 1024
EPS = 1e-6
NEG_INF = -1e30
GRID_W = 64
WINDOW = 128
BLK = 128
HEAD_DIM = 64
N_Q_HEADS = 8
N_KV_HEADS = 2
GQA = N_Q_HEADS // N_KV_HEADS
POOL_WINDOWS = (2, 4, 8, 16)
ROPE_BASE = 10000.0
ROPE_FREQS = HEAD_DIM // 4
PAD = 16
ADAM_LR, ADAM_B1, ADAM_B2, ADAM_EPS, ADAM_WD, ADAM_STEP = 0.001, 0.9, 0.999, 1e-08, 0.01, 10
BC1 = 1.0 - ADAM_B1 ** ADAM_STEP
BC2 = 1.0 - ADAM_B2 ** ADAM_STEP
SQRT_2_OVER_PI = math.sqrt(2.0 / math.pi)
GELU_C = 0.044715


def _cp(sem=None):
    return pltpu.CompilerParams(dimension_semantics=sem, vmem_limit_bytes=VMEM_LIMIT)


def _dot(a, b):
    return jnp.dot(a, b, preferred_element_type=F32)


def _dot_nt(a, b):
    return lax.dot_general(a, b, (((1,), (1,)), ((), ())), preferred_element_type=F32)


def _dot_tn(a, b):
    return lax.dot_general(a, b, (((0,), (0,)), ((), ())), preferred_element_type=F32)


def _rms(x):
    r = lax.rsqrt(jnp.mean(x * x, axis=-1, keepdims=True) + EPS)
    return x * r, r


def _rms_bwd(dn, n, r):
    return r * (dn - n * jnp.mean(dn * n, axis=-1, keepdims=True))


def _colsum(a):
    return jnp.sum(a, axis=0, keepdims=True)


def _rope(x, c, sa, sb):
    return x * c + pltpu.roll(x, LANES - ROPE_FREQS, 1) * sa + pltpu.roll(x, ROPE_FREQS, 1) * sb


def _full(shape):
    return pl.BlockSpec(shape, lambda *_: (0,) * len(shape))


def pre_mm(x, g, sh, sc, wt, *, tm, tn, w_row_off=0, name):
    T, D = x.shape
    n_rows = wt.shape[0] - w_row_off

    def body(x_ref, g_ref, sh_ref, sc_ref, w_ref, h_ref, z_ref):
        n, _ = _rms(x_ref[...])
        h = (n * g_ref[...] * (1.0 + sc_ref[...]) + sh_ref[...]).astype(BF16)
        h_ref[...] = h
        for c0 in range(0, n_rows, tn):
            z_ref[:, c0:c0 + tn] = _dot_nt(h, w_ref[c0:c0 + tn, :]).astype(BF16)

    vec = pl.BlockSpec((1, D), lambda i: (0, 0))
    return pl.pallas_call(
        body, name=name, grid=(T // tm,),
        in_specs=[pl.BlockSpec((tm, D), lambda i: (i, 0)), vec, vec, vec,
                  pl.BlockSpec((n_rows, D), lambda i: (w_row_off // n_rows, 0), pipeline_mode=pl.Buffered(1))],
        out_specs=[pl.BlockSpec((tm, D), lambda i: (i, 0)), pl.BlockSpec((tm, n_rows), lambda i: (i, 0))],
        out_shape=[jax.ShapeDtypeStruct((T, D), BF16), jax.ShapeDtypeStruct((T, n_rows), BF16)],
        compiler_params=_cp(("parallel",)),
    )(x, g, sh, sc, wt)


def inproj_even(x, g, sh, sc, wt, cos, sa, sb, *, tm, name):
    T, D = x.shape
    N = wt.shape[0]

    def body(x_ref, g_ref, sh_ref, sc_ref, w_ref, c_ref, sa_ref, sb_ref, h_ref, u_ref, q_ref, kv_ref):
        n, _ = _rms(x_ref[...])
        h = (n * g_ref[...] * (1.0 + sc_ref[...]) + sh_ref[...]).astype(BF16)
        h_ref[...] = h
        z = _dot_nt(h, w_ref[...])
        u_ref[...] = z[:, :4 * LANES]
        c, a, b = c_ref[...], sa_ref[...], sb_ref[...]
        for s in range(4):
            q_ref[:, s * LANES:(s + 1) * LANES] = _rope(z[:, (4 + s) * LANES:(5 + s) * LANES], c, a, b).astype(BF16)
        kv_ref[:, :LANES] = _rope(z[:, 8 * LANES:9 * LANES], c, a, b).astype(BF16)
        kv_ref[:, LANES:] = z[:, 9 * LANES:].astype(BF16)

    vec = pl.BlockSpec((1, D), lambda i: (0, 0))
    row = lambda w: pl.BlockSpec((tm, w), lambda i: (i, 0))
    return pl.pallas_call(
        body, name=name, grid=(T // tm,),
        in_specs=[row(D), vec, vec, vec, _full((N, D)), row(LANES), row(LANES), row(LANES)],
        out_specs=[row(D), row(4 * LANES), row(4 * LANES), row(2 * LANES)],
        out_shape=[jax.ShapeDtypeStruct((T, D), BF16), jax.ShapeDtypeStruct((T, 4 * LANES), F32),
                   jax.ShapeDtypeStruct((T, 4 * LANES), BF16), jax.ShapeDtypeStruct((T, 2 * LANES), BF16)],
        compiler_params=_cp(("parallel",)),
    )(x, g, sh, sc, wt, cos, sa, sb)


def mm_post(a_parts, w, x, g, gt, *, tm, target=None, name):
    T = a_parts[0].shape[0]
    D = w.shape[1]
    npart = len(a_parts)
    offs = [sum(a_.shape[1] for a_ in a_parts[:p]) for p in range(npart + 1)]
    with_loss = target is not None

    def body(*refs):
        a_refs, (w_ref, x_ref, g_ref, gt_ref) = refs[:npart], refs[npart:npart + 4]
        y = _dot(a_refs[0][...], w_ref[offs[0]:offs[1], :])
        for p in range(1, npart):
            y = y + _dot(a_refs[p][...], w_ref[offs[p]:offs[p + 1], :])
        n, _ = _rms(y)
        xn = x_ref[...] + gt_ref[...] * (n * g_ref[...])
        if not with_loss:
            y_ref, xn_ref = refs[npart + 4:]
            y_ref[...] = y.astype(BF16)
            xn_ref[...] = xn
            return
        t_ref, y_ref, d_ref, l_ref = refs[npart + 4:]
        y_ref[...] = y.astype(BF16)

        @pl.when(pl.program_id(0) == 0)
        def _():
            l_ref[...] = jnp.zeros_like(l_ref)

        e = xn - t_ref[...]
        l_ref[...] += 0.5 * jnp.sum(jnp.mean(e * e, axis=-1, keepdims=True), axis=0, keepdims=True)
        d_ref[...] = e * (1.0 / D)

    vec = pl.BlockSpec((1, D), lambda i: (0, 0))
    row = lambda w_: pl.BlockSpec((tm, w_), lambda i: (i, 0))
    in_specs = [row(a_.shape[1]) for a_ in a_parts] + [pl.BlockSpec(w.shape, lambda i: (0, 0), pipeline_mode=pl.Buffered(1)), row(D), vec, vec]
    out_specs = [row(D), row(D)]
    out_shape = [jax.ShapeDtypeStruct((T, D), BF16), jax.ShapeDtypeStruct((T, D), F32)]
    if with_loss:
        in_specs.append(row(D))
        out_specs.append(_full((1, 1)))
        out_shape.append(jax.ShapeDtypeStruct((1, 1), F32))
    return pl.pallas_call(
        body, name=name, grid=(T // tm,), in_specs=in_specs, out_specs=out_specs, out_shape=out_shape,
        compiler_params=_cp(("arbitrary",) if with_loss else ("parallel",)),
    )(*a_parts, w, x, g, gt, *((target,) if with_loss else ()))


def post_bwd_mm(dxn, y, g, gt, w, *, tm, name):
    T, D = y.shape
    K = w.shape[0]
    ck = K // 2 if K % (2 * LANES) == 0 and K > 2 * D else K

    def body(dxn_ref, y_ref, g_ref, gt_ref, w_ref, dy_ref, da_ref, dg_ref, dgt_ref):
        @pl.when(pl.program_id(0) == 0)
        def _():
            dg_ref[...] = jnp.zeros_like(dg_ref)
            dgt_ref[...] = jnp.zeros_like(dgt_ref)

        d = dxn_ref[...]
        n, r = _rms(y_ref[...].astype(F32))
        g_, gt_ = g_ref[...], gt_ref[...]
        dg_ref[...] += _colsum(d * gt_ * n)
        dgt_ref[...] += _colsum(d * g_ * n)
        dy = _rms_bwd(d * (gt_ * g_), n, r).astype(BF16)
        dy_ref[...] = dy
        for c0 in range(0, K, ck):
            da_ref[:, c0:c0 + ck] = _dot_nt(dy, w_ref[c0:c0 + ck, :]).astype(BF16)

    vec = pl.BlockSpec((1, D), lambda i: (0, 0))
    row = lambda w_: pl.BlockSpec((tm, w_), lambda i: (i, 0))
    return pl.pallas_call(
        body, name=name, grid=(T // tm,),
        in_specs=[row(D), row(D), vec, vec, pl.BlockSpec((K, D), lambda i: (0, 0), pipeline_mode=pl.Buffered(1))],
        out_specs=[row(D), row(K), vec, vec],
        out_shape=[jax.ShapeDtypeStruct((T, D), BF16), jax.ShapeDtypeStruct((T, K), BF16),
                   jax.ShapeDtypeStruct((1, D), F32), jax.ShapeDtypeStruct((1, D), F32)],
        compiler_params=_cp(("arbitrary",)),
    )(dxn, y, g, gt, w)


def mm_pre_bwd(dzs, wt, x, dres, g, sc, *, tm, w_row_off=0, name):
    T, N = dzs[0].shape
    D = x.shape[1]
    npart = len(dzs)
    off = w_row_off // N
    has_res = dres is not None

    def body(*refs):
        dz_refs = refs[:npart]
        w_refs = refs[npart:2 * npart]
        rest = refs[2 * npart:]
        x_ref = rest[0]
        dres_ref = rest[1] if has_res else None
        g_ref, sc_ref, dx_ref, dg_ref, dsh_ref, dsc_ref = rest[1 + has_res:]

        @pl.when(pl.program_id(0) == 0)
        def _():
            dg_ref[...] = jnp.zeros_like(dg_ref)
            dsh_ref[...] = jnp.zeros_like(dsh_ref)
            dsc_ref[...] = jnp.zeros_like(dsc_ref)

        dh = _dot(dz_refs[0][...], w_refs[0][...])
        for p in range(1, npart):
            dh = dh + _dot(dz_refs[p][...], w_refs[p][...])
        n, r = _rms(x_ref[...])
        g_, s1 = g_ref[...], 1.0 + sc_ref[...]
        dsh_ref[...] += _colsum(dh)
        dsc_ref[...] += _colsum(dh * n * g_)
        dg_ref[...] += _colsum(dh * s1 * n)
        dxp = _rms_bwd(dh * (g_ * s1), n, r)
        dx_ref[...] = dxp + dres_ref[...] if has_res else dxp

    vec = pl.BlockSpec((1, D), lambda i: (0, 0))
    row = pl.BlockSpec((tm, D), lambda i: (i, 0))
    w_specs = [pl.BlockSpec((N, D), (lambda i, p=p: (off + p, 0)), pipeline_mode=pl.Buffered(1)) for p in range(npart)]
    res_specs, res_args = ([row], (dres,)) if has_res else ([], ())
    return pl.pallas_call(
        body, name=name, grid=(T // tm,),
        in_specs=[pl.BlockSpec((tm, N), lambda i: (i, 0))] * npart + w_specs + [row] + res_specs + [vec, vec],
        out_specs=[row, vec, vec, vec],
        out_shape=[jax.ShapeDtypeStruct((T, D), F32)] + [jax.ShapeDtypeStruct((1, D), F32)] * 3,
        compiler_params=_cp(("arbitrary",)),
    )(*dzs, *([wt] * npart), x, *res_args, g, sc)


def wgrad(a_parts, b, *, tr, extra=None, name):
    T, R = a_parts[0].shape
    D = b.shape[1]
    npart = len(a_parts)
    nr = R // tr

    def body(*refs):
        a_refs, b_ref = refs[:npart], refs[npart]
        g_ref = refs[-1]
        for p in range(npart):
            @pl.when(pl.program_id(0) // nr == p)
            def _():
                acc = _dot_tn(a_refs[p][...], b_ref[...])
                if extra is not None:
                    acc += _dot_tn(refs[npart + 1][...], refs[npart + 2][...])
                g_ref[...] = acc.astype(BF16)

    in_specs = [pl.BlockSpec((T, tr), (lambda r, p=p: (0, jnp.clip(r - p * nr, 0, nr - 1)))) for p in range(npart)]
    in_specs.append(_full((T, D)))
    args = [*a_parts, b]
    if extra is not None:
        a2, b2 = extra
        in_specs += [pl.BlockSpec((a2.shape[0], tr), lambda r: (0, r)), _full(b2.shape)]
        args += [a2, b2]
    return pl.pallas_call(
        body, name=name, grid=(npart * nr,),
        in_specs=in_specs, out_specs=pl.BlockSpec((tr, D), lambda r: (r, 0)),
        out_shape=jax.ShapeDtypeStruct((npart * R, D), BF16),
        compiler_params=_cp(("parallel",)),
    )(*args)


def _conv_ext(ref, r0, rows, total):
    top = ref[pl.ds(pl.multiple_of(jnp.maximum(r0 - PAD, 0), PAD), PAD), :]
    mid = ref[pl.ds(r0, rows), :]
    bot = ref[pl.ds(pl.multiple_of(jnp.minimum(r0 + rows, total - PAD), PAD), PAD), :]
    top = jnp.where(r0 > 0, top, jnp.zeros_like(top))
    bot = jnp.where(r0 + rows < total, bot, jnp.zeros_like(bot))
    return jnp.concatenate([top, mid, bot], axis=0).astype(F32)


def _shift_rows(a, k):
    return pltpu.roll(a, k % a.shape[0], 0)


def _conv3(x, w, b):
    return w[0:1] * _shift_rows(x, 1) + w[1:2] * x + w[2:3] * _shift_rows(x, -1) + b


def _gate_up_specs(rows_, wblk, nb):
    return [pl.BlockSpec((rows_, wblk), lambda j: (0, j)), pl.BlockSpec((rows_, wblk), lambda j: (0, j + nb))]


def conv_fwd(hu, cw, cb, *, rows, wblk, name):
    L, N2 = hu.shape
    nb = N2 // 2 // wblk
    nchunk = L // rows

    def body(hg_ref, hu_ref, wg_ref, wu_ref, bg_ref, bu_ref, a_ref, s1_ref, s2_ref):
        def chunk(ci, carry):
            r0 = pl.multiple_of(ci * rows, rows)
            gate = _conv3(_conv_ext(hg_ref, r0, rows, L), wg_ref[...], bg_ref[...])[PAD:PAD + rows]
            up = _conv3(_conv_ext(hu_ref, r0, rows, L), wu_ref[...], bu_ref[...])[PAD:PAD + rows]
            sg = jax.nn.sigmoid(gate)
            silu = gate * sg
            at = pl.ds(r0, rows)
            a_ref[at, :] = (silu * up).astype(BF16)
            s1_ref[at, :] = silu.astype(BF16)
            s2_ref[at, :] = (up * (sg + silu * (1.0 - sg))).astype(BF16)
            return carry

        lax.fori_loop(0, nchunk, chunk, 0)

    out = pl.BlockSpec((L, wblk), lambda j: (0, j))
    return pl.pallas_call(
        body, name=name, grid=(nb,),
        in_specs=_gate_up_specs(L, wblk, nb) + _gate_up_specs(3, wblk, nb) + _gate_up_specs(1, wblk, nb),
        out_specs=[out] * 3, out_shape=[jax.ShapeDtypeStruct((L, N2 // 2), BF16)] * 3,
        compiler_params=_cp(("parallel",)),
    )(hu, hu, cw, cw, cb, cb)


def conv_bwd(da, s1, s2, hu, cw, *, rows, wblk, name):
    L, N2 = hu.shape
    F = N2 // 2
    nb = F // wblk
    nchunk = L // rows
    mid = slice(PAD, PAD + rows)

    def body(da_ref, s1_ref, s2_ref, hg_ref, hu_ref, wg_ref, wu_ref, dg_ref, du_ref, dwg_ref, dwu_ref, dbg_ref, dbu_ref):
        for ref in (dwg_ref, dwu_ref, dbg_ref, dbu_ref):
            ref[...] = jnp.zeros_like(ref)

        def half_bwd(x_ref, dh, w_ref, dx_ref, dw_ref, db_ref, r0):
            w = w_ref[...]
            nxt, prv = _shift_rows(dh, -1)[mid], _shift_rows(dh, 1)[mid]
            dhm, xm = dh[mid], x_ref[pl.ds(r0, rows), :].astype(F32)
            dx_ref[pl.ds(r0, rows), :] = (w[0:1] * nxt + w[1:2] * dhm + w[2:3] * prv).astype(BF16)
            db_ref[...] += _colsum(dhm)
            dw_ref[0:1, :] += _colsum(nxt * xm)
            dw_ref[1:2, :] += _colsum(dhm * xm)
            dw_ref[2:3, :] += _colsum(prv * xm)

        def chunk(ci, carry):
            r0 = pl.multiple_of(ci * rows, rows)
            d = _conv_ext(da_ref, r0, rows, L)
            half_bwd(hu_ref, d * _conv_ext(s1_ref, r0, rows, L), wu_ref, du_ref, dwu_ref, dbu_ref, r0)
            half_bwd(hg_ref, d * _conv_ext(s2_ref, r0, rows, L), wg_ref, dg_ref, dwg_ref, dbg_ref, r0)
            return carry

        lax.fori_loop(0, nchunk, chunk, 0)

    blk = lambda r: pl.BlockSpec((r, wblk), lambda j: (0, j))
    return pl.pallas_call(
        body, name=name, grid=(nb,),
        in_specs=[blk(L)] * 3 + _gate_up_specs(L, wblk, nb) + _gate_up_specs(3, wblk, nb),
        out_specs=[blk(L), blk(L), blk(3), blk(3), blk(1), blk(1)],
        out_shape=[jax.ShapeDtypeStruct((L, F), BF16)] * 2 + [jax.ShapeDtypeStruct((3, F), F32)] * 2
        + [jax.ShapeDtypeStruct((1, F), F32)] * 2,
        compiler_params=_cp(("parallel",)),
    )(da, s1, s2, hu, hu, cw, cw)


def _window_sums(pad_ref, w, lead):
    a = pad_ref[...]
    k = 1
    while k < w:
        a = a + _shift_rows(a, -k)
        k *= 2
    return _shift_rows(a, lead) if lead else a


def _pool_counts(L, h):
    t = lax.broadcasted_iota(jnp.int32, (L, 1), 0)
    return (jnp.minimum(t + h, L) - jnp.maximum(t - h, 0)).astype(F32)


def _pooled(u_ref, pad_ref, L, w):
    h = w // 2
    pad_ref[pl.ds(PAD, L), :] = u_ref[...]
    win = _window_sums(pad_ref, w, h)[PAD:PAD + L]
    return win / _pool_counts(L, h) - u_ref[...]


def _zero_pad_edges(pad_ref, L):
    z = jnp.zeros((PAD, LANES), F32)
    pad_ref[pl.ds(0, PAD), :] = z
    pad_ref[pl.ds(PAD + L, PAD), :] = z


def pool_fwd(u, w_pool, pool_scale, *, name):
    L = u.shape[0]

    def body(u_ref, w_ref, ps_ref, p_ref, pad_ref):
        _zero_pad_edges(pad_ref, L)
        for gi, win in enumerate(POOL_WINDOWS):
            @pl.when(pl.program_id(0) == gi)
            def _():
                pooled = _pooled(u_ref, pad_ref, L, win)
                p_ref[...] = (_dot(pooled.astype(BF16), w_ref[...].astype(BF16)) * ps_ref[...]).astype(BF16)

    return pl.pallas_call(
        body, name=name, grid=(len(POOL_WINDOWS),),
        in_specs=[pl.BlockSpec((L, LANES), lambda gi: (0, gi)), pl.BlockSpec((None, LANES, LANES), lambda gi: (gi, 0, 0)),
                  pl.BlockSpec((1, LANES), lambda gi: (0, gi))],
        out_specs=pl.BlockSpec((L, LANES), lambda gi: (0, gi)),
        out_shape=jax.ShapeDtypeStruct((L, 4 * LANES), BF16),
        scratch_shapes=[pltpu.VMEM((L + 2 * PAD, LANES), F32)],
        compiler_params=_cp(("parallel",)),
    )(u, w_pool, pool_scale)


def pool_bwd(u, dpa, w_pool, pool_scale, *, name):
    L = u.shape[0]

    def body(u_ref, dp_ref, w_ref, ps_ref, du_ref, dw_ref, dps_ref, pad_ref):
        _zero_pad_edges(pad_ref, L)
        for gi, win in enumerate(POOL_WINDOWS):
            @pl.when(pl.program_id(0) == gi)
            def _():
                h = win // 2
                wb = w_ref[...].astype(BF16)
                pooled = _pooled(u_ref, pad_ref, L, win).astype(BF16)
                dp = dp_ref[...].astype(F32)
                dps_ref[...] = _colsum(dp * _dot(pooled, wb))
                dy = (dp * ps_ref[...]).astype(BF16)
                dw_ref[...] = _dot_tn(pooled, dy)
                dpooled = _dot_nt(dy, wb)
                pad_ref[pl.ds(PAD, L), :] = dpooled / _pool_counts(L, h)
                du_ref[...] = (_window_sums(pad_ref, win, h - 1)[PAD:PAD + L] - dpooled).astype(BF16)

    return pl.pallas_call(
        body, name=name, grid=(len(POOL_WINDOWS),),
        in_specs=[pl.BlockSpec((L, LANES), lambda gi: (0, gi)), pl.BlockSpec((L, LANES), lambda gi: (0, gi)),
                  pl.BlockSpec((None, LANES, LANES), lambda gi: (gi, 0, 0)), pl.BlockSpec((1, LANES), lambda gi: (0, gi))],
        out_specs=[pl.BlockSpec((L, LANES), lambda gi: (0, gi)), pl.BlockSpec((None, LANES, LANES), lambda gi: (gi, 0, 0)),
                   pl.BlockSpec((1, LANES), lambda gi: (0, gi))],
        out_shape=[jax.ShapeDtypeStruct((L, 4 * LANES), BF16), jax.ShapeDtypeStruct((4, LANES, LANES), F32),
                   jax.ShapeDtypeStruct((1, 4 * LANES), F32)],
        scratch_shapes=[pltpu.VMEM((L + 2 * PAD, LANES), F32)],
        compiler_params=_cp(("parallel",)),
    )(u, dpa, w_pool, pool_scale)


def _attn_probs(qk, band_k, ctx_k, sink_ref, kh, mask4):
    s_loc = jnp.where(mask4, _dot_nt(qk, band_k), NEG_INF)
    s_ctx = _dot_nt(qk, ctx_k)
    sk = jnp.concatenate([jnp.full((BLK, 1), sink_ref[kh * GQA + hh], F32) for hh in range(GQA)], axis=0)
    m = jnp.maximum(jnp.maximum(jnp.max(s_loc, axis=-1, keepdims=True), jnp.max(s_ctx, axis=-1, keepdims=True)), sk)
    e_loc, e_ctx, e_s = jnp.exp(s_loc - m), jnp.exp(s_ctx - m), jnp.exp(sk - m)
    inv = 1.0 / (jnp.sum(e_loc, axis=-1, keepdims=True) + jnp.sum(e_ctx, axis=-1, keepdims=True) + e_s)
    return e_loc * inv, e_ctx * inv, e_s * inv


def _attn_block(n, L):
    start = pl.multiple_of(jnp.clip((n - 1) * BLK, 0, L - 3 * BLK), BLK)
    qpos = n * BLK + lax.broadcasted_iota(jnp.int32, (BLK, 3 * BLK), 0)
    kpos = start + lax.broadcasted_iota(jnp.int32, (BLK, 3 * BLK), 1)
    mask = jnp.abs(kpos - qpos) <= WINDOW
    return start, jnp.concatenate([mask] * GQA, axis=0)


def _stack_slabs(ref):
    return jnp.concatenate([ref[:, s * LANES:(s + 1) * LANES] for s in range(GQA)], axis=0)


def _kv_head_lanes(kh):
    return (lax.broadcasted_iota(jnp.int32, (1, LANES), 1) // HEAD_DIM) == kh


def permute_heads(w, inverse=False):
    lo, hi = 4 * LANES, 8 * LANES
    mid = w[lo:hi].reshape(*((GQA, N_KV_HEADS) if inverse else (N_KV_HEADS, GQA)), HEAD_DIM, w.shape[1])
    return jnp.concatenate([w[:lo], mid.swapaxes(0, 1).reshape(hi - lo, w.shape[1]), w[hi:]], axis=0)


def attn_fwd(q, kv, kvc, sink, *, name):
    L = q.shape[0]
    C = kvc.shape[0]
    scale = HEAD_DIM ** -0.5

    def body(q_ref, kv_ref, kvc_ref, sink_ref, o_ref):
        start, mask4 = _attn_block(pl.program_id(0), L)
        band = kv_ref[pl.ds(start, 3 * BLK), :]
        kvc_ = kvc_ref[...]
        qs = _stack_slabs(q_ref) * scale
        o = jnp.zeros((GQA * BLK, LANES), F32)
        for kh in range(N_KV_HEADS):
            grp = _kv_head_lanes(kh)
            qk = jnp.where(grp, qs, jnp.zeros_like(qs))
            p_loc, p_ctx, _ = _attn_probs(qk, band[:, :LANES], kvc_[:, :LANES], sink_ref, kh, mask4)
            o = o + jnp.where(grp, _dot(p_loc.astype(BF16), band[:, LANES:]) + _dot(p_ctx.astype(BF16), kvc_[:, LANES:]), 0.0)
        for s in range(GQA):
            o_ref[:, s * LANES:(s + 1) * LANES] = o[s * BLK:(s + 1) * BLK].astype(BF16)

    return pl.pallas_call(
        body, name=name, grid=(L // BLK,),
        in_specs=[pl.BlockSpec((BLK, 4 * LANES), lambda n: (n, 0)), _full((L, 2 * LANES)), _full((C, 2 * LANES)),
                  pl.BlockSpec(memory_space=pltpu.SMEM)],
        out_specs=pl.BlockSpec((BLK, 4 * LANES), lambda n: (n, 0)),
        out_shape=jax.ShapeDtypeStruct((L, 4 * LANES), BF16),
        compiler_params=_cp(("parallel",)),
    )(q, kv, kvc, sink)


def attn_bwd(q, kv, kvc, sink, dpa, cos, sa, sb, *, name):
    L = q.shape[0]
    C = kvc.shape[0]
    nb = L // BLK
    scale = HEAD_DIM ** -0.5

    def body(q_ref, kv_ref, kvc_ref, sink_ref, do_ref, c_ref, sa_ref, sb_ref, cq_ref, saq_ref, sbq_ref,
             dq_ref, dkv_ref, dkvc_ref, dsink_ref, dkv_acc, dkvc_acc):
        n = pl.program_id(0)

        @pl.when(n == 0)
        def _():
            dkv_acc[...] = jnp.zeros_like(dkv_acc)
            dkvc_acc[...] = jnp.zeros_like(dkvc_acc)
            dsink_ref[...] = jnp.zeros_like(dsink_ref)

        start, mask4 = _attn_block(n, L)
        band = kv_ref[pl.ds(start, 3 * BLK), :]
        kvc_ = kvc_ref[...]
        band_k, band_v, ctx_k, ctx_v = band[:, :LANES], band[:, LANES:], kvc_[:, :LANES], kvc_[:, LANES:]
        qs = _stack_slabs(q_ref) * scale
        dos = _stack_slabs(do_ref)
        lane = lax.broadcasted_iota(jnp.int32, (1, LANES), 1)
        dsink = jnp.zeros((1, LANES), F32)
        dq = jnp.zeros((GQA * BLK, LANES), F32)
        dk = jnp.zeros((LANES, 3 * BLK), F32)
        dv = jnp.zeros((LANES, 3 * BLK), F32)
        dkc = jnp.zeros((LANES, C), F32)
        dvc = jnp.zeros((LANES, C), F32)
        for kh in range(N_KV_HEADS):
            grp = _kv_head_lanes(kh)
            qk = jnp.where(grp, qs, jnp.zeros_like(qs))
            dok = jnp.where(grp, dos, jnp.zeros_like(dos))
            p_loc, p_ctx, p_s = _attn_probs(qk, band_k, ctx_k, sink_ref, kh, mask4)
            dp_loc = _dot_nt(dok, band_v)
            dp_ctx = _dot_nt(dok, ctx_v)
            delta = jnp.sum(p_loc * dp_loc, axis=-1, keepdims=True) + jnp.sum(p_ctx * dp_ctx, axis=-1, keepdims=True)
            ds_loc = (p_loc * (dp_loc - delta)).astype(BF16)
            ds_ctx = (p_ctx * (dp_ctx - delta)).astype(BF16)
            dsk = p_s * delta
            for hh in range(GQA):
                dsink = dsink - jnp.where(lane == kh * GQA + hh, jnp.sum(dsk[hh * BLK:(hh + 1) * BLK], axis=0, keepdims=True), 0.0)
            dq = dq + jnp.where(grp, _dot(ds_loc, band_k) + _dot(ds_ctx, ctx_k), 0.0)
            dk = dk + _dot_tn(qk, ds_loc)
            dv = dv + _dot_tn(dok, p_loc.astype(BF16))
            dkc = dkc + _dot_tn(qk, ds_ctx)
            dvc = dvc + _dot_tn(dok, p_ctx.astype(BF16))
        dsink_ref[...] += dsink
        dkv_acc[:LANES, pl.ds(start, 3 * BLK)] += dk
        dkv_acc[LANES:, pl.ds(start, 3 * BLK)] += dv
        dkvc_acc[:LANES, :] += dkc
        dkvc_acc[LANES:, :] += dvc
        c, a, b = cq_ref[...], -saq_ref[...], -sbq_ref[...]
        for s in range(GQA):
            dq_ref[:, s * LANES:(s + 1) * LANES] = _rope(dq[s * BLK:(s + 1) * BLK] * scale, c, a, b).astype(BF16)

        @pl.when(n == nb - 1)
        def _():
            dkv_ref[:, :LANES] = _rope(dkv_acc[:LANES, :].T, c_ref[...], -sa_ref[...], -sb_ref[...]).astype(BF16)
            dkv_ref[:, LANES:] = dkv_acc[LANES:, :].T.astype(BF16)
            dkvc_ref[...] = dkvc_acc[...].T.astype(BF16)

    blk = lambda w: pl.BlockSpec((BLK, w), lambda n: (n, 0))
    return pl.pallas_call(
        body, name=name, grid=(nb,),
        in_specs=[blk(4 * LANES), _full((L, 2 * LANES)), _full((C, 2 * LANES)), pl.BlockSpec(memory_space=pltpu.SMEM),
                  pl.BlockSpec((BLK, 4 * LANES), lambda n: (n, 1)),
                  _full((L, LANES)), _full((L, LANES)), _full((L, LANES)), blk(LANES), blk(LANES), blk(LANES)],
        out_specs=[blk(4 * LANES), _full((L, 2 * LANES)), _full((C, 2 * LANES)), _full((1, LANES))],
        out_shape=[jax.ShapeDtypeStruct((L, 4 * LANES), BF16), jax.ShapeDtypeStruct((L, 2 * LANES), BF16),
                   jax.ShapeDtypeStruct((C, 2 * LANES), BF16), jax.ShapeDtypeStruct((1, LANES), F32)],
        scratch_shapes=[pltpu.VMEM((2 * LANES, L), F32), pltpu.VMEM((2 * LANES, C), F32)],
        compiler_params=_cp(("arbitrary",)),
    )(q, kv, kvc, sink, dpa, cos, sa, sb, cos, sa, sb)


def _gelu_parts(x):
    th = jnp.tanh(SQRT_2_OVER_PI * (x + GELU_C * x * x * x))
    return 0.5 * x * (1.0 + th), th


def _gelu_grad(x, th):
    return 0.5 * (1.0 + th) + 0.5 * x * (1.0 - th * th) * SQRT_2_OVER_PI * (1.0 + 3.0 * GELU_C * x * x)


def _layernorm(v):
    mu = jnp.mean(v, axis=-1, keepdims=True)
    vc = v - mu
    rstd = lax.rsqrt(jnp.mean(vc * vc, axis=-1, keepdims=True) + EPS)
    return vc * rstd, rstd


def sgu_fwd(z1, ln_g, ln_b, ws, bst, *, name):
    L, W2 = z1.shape
    W = W2 // 2
    ng = W // LANES

    def body(z_ref, g_ref, b_ref, ws_ref, bs_ref, o_ref):
        z, _ = _gelu_parts(z_ref[...].astype(F32))
        xhat, _ = _layernorm(z[:, W:])
        vln = (xhat * g_ref[...] + b_ref[...]).astype(BF16)
        for gi in range(ng):
            cs = slice(gi * LANES, (gi + 1) * LANES)
            s = _dot(ws_ref[gi], vln[:, cs]) + bs_ref[:, gi:gi + 1]
            o_ref[:, cs] = (z[:, cs] * s).astype(BF16)

    vec = _full((1, W))
    return pl.pallas_call(
        body, name=name, grid=(L // BLK,),
        in_specs=[pl.BlockSpec((BLK, W2), lambda n: (n, 0)), vec, vec, _full((ng, LANES, LANES)), _full((BLK, ng))],
        out_specs=pl.BlockSpec((BLK, W), lambda n: (n, 0)),
        out_shape=jax.ShapeDtypeStruct((L, W), BF16),
        compiler_params=_cp(("parallel",)),
    )(z1, ln_g, ln_b, ws, bst)


def sgu_bwd(z1, dus, ln_g, ln_b, ws, wst, bst, *, name):
    L, W2 = z1.shape
    W = W2 // 2
    ng = W // LANES

    def body(z_ref, d_ref, g_ref, b_ref, ws_ref, wst_ref, bs_ref, dz_ref, dws_ref, dbs_ref, dg_ref, db_ref, dv_scr):
        @pl.when(pl.program_id(0) == 0)
        def _():
            dws_ref[...] = jnp.zeros_like(dws_ref)
            dbs_ref[...] = jnp.zeros_like(dbs_ref)
            dg_ref[...] = jnp.zeros_like(dg_ref)
            db_ref[...] = jnp.zeros_like(db_ref)

        zp = z_ref[...].astype(F32)
        z, th = _gelu_parts(zp)
        xhat, rstd = _layernorm(z[:, W:])
        vln = (xhat * g_ref[...] + b_ref[...]).astype(BF16)
        d = d_ref[...].astype(F32)
        lane = lax.broadcasted_iota(jnp.int32, (1, LANES), 1)
        dbs = jnp.zeros((BLK, LANES), F32)
        dgel = _gelu_grad(zp, th)
        for gi in range(ng):
            cs = slice(gi * LANES, (gi + 1) * LANES)
            s = _dot(ws_ref[gi], vln[:, cs]) + bs_ref[:, gi:gi + 1]
            dz_ref[:, cs] = (d[:, cs] * s * dgel[:, cs]).astype(BF16)
            ds = d[:, cs] * z[:, cs]
            dbs = dbs + jnp.where(lane == gi, jnp.sum(ds, axis=-1, keepdims=True), 0.0)
            dsb = ds.astype(BF16)
            dws_ref[gi] += _dot_nt(dsb, vln[:, cs])
            dv_scr[:, cs] = _dot(wst_ref[gi], dsb)
        dbs_ref[...] += dbs
        dvln = dv_scr[...]
        dg_ref[...] += _colsum(dvln * xhat)
        db_ref[...] += _colsum(dvln)
        dxh = dvln * g_ref[...]
        dv = rstd * (dxh - jnp.mean(dxh, axis=-1, keepdims=True) - xhat * jnp.mean(dxh * xhat, axis=-1, keepdims=True))
        dz_ref[:, W:] = (dv * dgel[:, W:]).astype(BF16)

    vec = _full((1, W))
    return pl.pallas_call(
        body, name=name, grid=(L // BLK,),
        in_specs=[pl.BlockSpec((BLK, W2), lambda n: (n, 0)), pl.BlockSpec((BLK, W), lambda n: (n, 0)), vec, vec,
                  _full((ng, LANES, LANES)), _full((ng, LANES, LANES)), _full((BLK, ng))],
        out_specs=[pl.BlockSpec((BLK, W2), lambda n: (n, 0)), _full((ng, LANES, LANES)), _full((BLK, LANES)), vec, vec],
        out_shape=[jax.ShapeDtypeStruct((L, W2), BF16), jax.ShapeDtypeStruct((ng, LANES, LANES), F32),
                   jax.ShapeDtypeStruct((BLK, LANES), F32), jax.ShapeDtypeStruct((1, W), F32), jax.ShapeDtypeStruct((1, W), F32)],
        scratch_shapes=[pltpu.VMEM((BLK, W), F32)],
        compiler_params=_cp(("arbitrary",)),
    )(z1, dus, ln_g, ln_b, ws, wst, bst)


def _adamw_math(w, m, v, g):
    m_ = ADAM_B1 * m + (1.0 - ADAM_B1) * g
    v_ = ADAM_B2 * v + (1.0 - ADAM_B2) * (g * g)
    return -ADAM_LR * ((m_ / BC1) / (jnp.sqrt(v_ / BC2) + ADAM_EPS) + ADAM_WD * w), m_, v_


def adamw(w, m, v, gparts, *, tr, name):
    NL, R, Wd = w.shape
    nr = R // tr

    def body(w_ref, m_ref, v_ref, *rest):
        gp_refs, (g_ref, d_ref, nm_ref, nv_ref) = rest[:NL], rest[NL:]
        for l in range(NL):
            @pl.when(pl.program_id(0) == l)
            def _():
                g = gp_refs[l][0].astype(F32)
                for s in range(1, gp_refs[l].shape[0]):
                    g = g + gp_refs[l][s].astype(F32)
                g_ref[...] = g
                d_ref[...], nm_ref[...], nv_ref[...] = _adamw_math(w_ref[...], m_ref[...], v_ref[...], g)

    row = pl.BlockSpec((None, tr, Wd), lambda l, i: (l, i, 0))
    gspecs = [pl.BlockSpec((gparts[l].shape[0], tr, Wd), (lambda l_, i, l=l: (0, jnp.clip(i + (l_ - l) * nr, 0, nr - 1), 0)))
              for l in range(NL)]
    return pl.pallas_call(
        body, name=name, grid=(NL, nr),
        in_specs=[row, row, row] + gspecs, out_specs=[row] * 4, out_shape=[jax.ShapeDtypeStruct((NL, R, Wd), F32)] * 4,
        compiler_params=_cp(("arbitrary", "arbitrary")),
    )(w, m, v, *gparts)


def small_update(gpacks, me, params, loss_row, *, name):
    n = len(params)

    def body(me_ref, gp_ref, *refs):
        ins, outs, gs_ref = refs[:3 * n], refs[3 * n:-1], refs[-1]
        gs_ref[...] = gp_ref[0].astype(F32)
        for dv in range(1, N_DEV):
            gs_ref[...] += gp_ref[dv].astype(F32)
        for p, (w, _, _, off, per_dev) in enumerate(params):
            w_ref, m_ref, v_ref = ins[3 * p:3 * p + 3]
            g_ref, d_ref, nm_ref, nv_ref = outs[4 * p:4 * p + 4]
            rows, cols = w.shape
            if cols == LANES and rows % 8 == 0 and not per_dev:
                g = gs_ref[off:off + rows, :]
                g_ref[...] = g
                d_ref[...], nm_ref[...], nv_ref[...] = _adamw_math(w_ref[...], m_ref[...], v_ref[...], g)
                continue
            chunks = -(-cols // LANES)
            base = off + me_ref[0] * per_dev if per_dev else off
            for i in range(rows):
                for j in range(chunks):
                    wd = min(LANES, cols - j * LANES)
                    at = (slice(i, i + 1), slice(j * LANES, j * LANES + wd))
                    g = gs_ref[pl.ds(base + i * chunks + j, 1), 0:wd]
                    g_ref[at] = g
                    d_ref[at], nm_ref[at], nv_ref[at] = _adamw_math(w_ref[at], m_ref[at], v_ref[at], g)
        outs[-1][...] = jnp.sum(gs_ref[loss_row:loss_row + 1, :], axis=1, keepdims=True)

    flat = [a for w, m, v, _, _ in params for a in (w, m, v)]
    out_shape = [jax.ShapeDtypeStruct(w.shape, F32) for w, _, _, _, _ in params for _ in range(4)] + [jax.ShapeDtypeStruct((1, 1), F32)]
    return pl.pallas_call(
        body, name=name, grid=(1,),
        in_specs=[pl.BlockSpec(memory_space=pltpu.SMEM), _full(gpacks.shape)] + [_full(a.shape) for a in flat],
        out_specs=[_full(o.shape) for o in out_shape], out_shape=out_shape,
        scratch_shapes=[pltpu.VMEM(gpacks.shape[1:], F32)],
        compiler_params=_cp(("arbitrary",)),
    )(me, gpacks, *flat)


def ada_fwd_mm(cs, w_ada, b_loc, *, name):
    R, D = cs.shape
    nl, _, n = w_ada.shape

    def body(c_ref, w_ref, b_ref, s_ref, m_ref):
        c = c_ref[...]
        s = c * jax.nn.sigmoid(c)
        s_ref[...] = s
        for i in range(nl):
            m_ref[i] = _dot(s.astype(BF16), w_ref[i].astype(BF16)) + b_ref[i:i + 1, :]

    return pl.pallas_call(
        body, name=name, in_specs=[_full((R, D)), _full((nl, D, n)), _full((nl, n))],
        out_specs=[_full((R, D)), _full((nl, R, n))], grid=(1,),
        out_shape=[jax.ShapeDtypeStruct((R, D), F32), jax.ShapeDtypeStruct((nl, R, n), F32)],
        compiler_params=_cp(("arbitrary",)),
    )(cs, w_ada, b_loc)


def ada_bwd_mm(s, c_ctx, dall, w_ada, *, name):
    R, D = s.shape
    nl, _, n = w_ada.shape

    def body(s_ref, cc_ref, d_ref, w_ref, gw_ref, dcc_ref):
        sb = s_ref[...].astype(BF16)
        row = lax.broadcasted_iota(jnp.int32, (R, 1), 0)
        dctx = d_ref[0, 1:2, :]
        for dv in range(1, N_DEV):
            dctx = dctx + d_ref[dv, 1:2, :]
        for i in range(nl):
            dm = jnp.zeros((R, n), F32)
            for dv in range(N_DEV):
                dm = dm + jnp.where(row == dv, d_ref[dv, 2 * i:2 * i + 1, :], 0.0)
            if i == 0:
                dm = dm + jnp.where(row == N_DEV, dctx, 0.0)
            gw_ref[i] = _dot_tn(sb, dm.astype(BF16))
        cc = cc_ref[...]
        sg = jax.nn.sigmoid(cc)
        ds = _dot_nt(jnp.broadcast_to(dctx, (8, n)).astype(BF16), w_ref[0].astype(BF16))
        dcc_ref[...] = ds * (sg * (1.0 + cc * (1.0 - sg)))

    return pl.pallas_call(
        body, name=name, grid=(1,),
        in_specs=[_full((R, D)), _full((1, D)), _full((N_DEV, 3, n)), _full((nl, D, n))],
        out_specs=[_full((nl, D, n)), _full((8, D))],
        out_shape=[jax.ShapeDtypeStruct((nl, D, n), F32), jax.ShapeDtypeStruct((8, D), F32)],
        compiler_params=_cp(("arbitrary",)),
    )(s, c_ctx, dall, w_ada)


def _place():
    x, y, c = lax.axis_index("x"), lax.axis_index("y"), lax.axis_index("c")
    return x, y, c


def _lin(p):
    return 4 * p[0] + 2 * p[1] + p[2]


def all_gather_small(xb, *, name):
    R, W = xb.shape

    def body(x_ref, out_ref, send_sems, recv_sems, local_sem):
        x, y, c = _place()
        me = _lin((x, y, c))
        mine = pltpu.make_async_copy(x_ref, out_ref.at[me], local_sem)
        mine.start()
        copies = []
        for k in range(1, N_DEV):
            peer = (x ^ (k >> 2), y ^ ((k >> 1) & 1), c ^ (k & 1))
            mk = lambda dst, k=k, peer=peer: pltpu.make_async_remote_copy(
                src_ref=x_ref, dst_ref=dst, send_sem=send_sems.at[k - 1], recv_sem=recv_sems.at[k - 1], device_id=peer, device_id_type=MESH)
            mk(out_ref.at[me]).start()
            copies.append(mk(out_ref.at[_lin(peer)]))
        for cp in copies:
            cp.wait_recv()
        for cp in copies:
            cp.wait_send()
        mine.wait()

    vm = pl.BlockSpec(memory_space=pltpu.VMEM)
    return pl.pallas_call(
        body, name=name, in_specs=[vm], out_specs=vm, out_shape=jax.ShapeDtypeStruct((N_DEV, R, W), xb.dtype),
        scratch_shapes=[pltpu.SemaphoreType.DMA((7,)), pltpu.SemaphoreType.DMA((7,)), pltpu.SemaphoreType.DMA],
        compiler_params=pltpu.CompilerParams(vmem_limit_bytes=VMEM_LIMIT),
    )(xb)


HBM_SPEC = pl.BlockSpec(memory_space=pltpu.HBM)
SEM_SPEC = pl.BlockSpec(memory_space=pltpu.SEMAPHORE)
ORDERED_EFFECT = pltpu.SideEffectType.DATAFLOW_SIDE_EFFECTING


def _exchange_copies(srcs, lands, sems, scatter):
    x, y, c = _place()
    me = _lin((x, y, c))
    for j in range(len(srcs)):
        r = lands[j].shape[0] // N_DEV
        block = lambda d, j=j, r=r: pl.ds(pl.multiple_of(d * r, 16), r)
        for k in range(1, N_DEV):
            peer = (x ^ (k >> 2), y ^ ((k >> 1) & 1), c ^ (k & 1))
            src = srcs[j].at[block(_lin(peer)), :] if scatter else srcs[j]
            mk = lambda dst, j=j, k=k, peer=peer, src=src: pltpu.make_async_remote_copy(
                src_ref=src, dst_ref=dst, send_sem=sems[2 * j].at[k - 1], recv_sem=sems[2 * j + 1].at[k - 1],
                device_id=peer, device_id_type=MESH)
            yield mk(lands[j].at[block(me), :]), mk(lands[j].at[block(_lin(peer)), :])


def exchange_start(srcs, lands, *, scatter, name):
    nw = len(srcs)

    def body(*refs):
        for start, _ in _exchange_copies(refs[:nw], refs[nw:2 * nw], refs[2 * nw:4 * nw], scatter):
            start.start()
        refs[-1][...] = jnp.zeros_like(refs[-1])

    thru = [pltpu.HBM(a.shape, a.dtype) for a in (*srcs, *lands)]
    res = pl.pallas_call(
        body, name=name, in_specs=[HBM_SPEC] * (2 * nw),
        out_specs=[SEM_SPEC] * (2 * nw) + [HBM_SPEC] * (2 * nw) + [pl.BlockSpec(memory_space=pltpu.VMEM)],
        out_shape=[pltpu.SemaphoreType.DMA((N_DEV - 1,))] * (2 * nw) + thru + [jax.ShapeDtypeStruct((8, LANES), F32)],
        input_output_aliases={i: 2 * nw + i for i in range(2 * nw)},
        compiler_params=pltpu.CompilerParams(has_side_effects=ORDERED_EFFECT),
    )(*[pltpu.with_memory_space_constraint(a, pltpu.HBM) for a in (*srcs, *lands)])
    return res[:2 * nw], res[2 * nw:3 * nw], res[3 * nw:4 * nw], res[-1]


def exchange_wait(srcs, lands, sems, after, *, scatter, name):
    nw = len(srcs)
    after = list(after) if isinstance(after, (list, tuple)) else [after]

    def body(*refs):
        for _, arrive in _exchange_copies(refs[:nw], refs[nw:2 * nw], refs[2 * nw:4 * nw], scatter):
            arrive.wait_send()
            arrive.wait_recv()

    res = pl.pallas_call(
        body, name=name, in_specs=[HBM_SPEC] * (2 * nw) + [SEM_SPEC] * (2 * nw) + [pl.BlockSpec(memory_space=pl.ANY)] * len(after),
        out_specs=[HBM_SPEC] * (2 * nw), out_shape=[pltpu.HBM(a.shape, a.dtype) for a in (*srcs, *lands)],
        input_output_aliases={i: i for i in range(2 * nw)},
        compiler_params=pltpu.CompilerParams(has_side_effects=ORDERED_EFFECT),
    )(*srcs, *lands, *sems, *after)
    return res[nw:]


def place_own(srcs, rows, me, *, scatter, name):
    nw = len(srcs)
    lands = [lax.empty((N_DEV * r, s_.shape[1]), s_.dtype) for r, s_ in zip(rows, srcs)]

    def body(me_ref, *refs):
        for j in range(nw):
            refs[2 * nw + j][...] = refs[j][...]

    mine = lambda i, me_ref: (me_ref[0], 0)
    src_at = mine if scatter else (lambda i, me_ref: (0, 0))
    blocks = [(r, s_.shape[1]) for r, s_ in zip(rows, srcs)]
    return pl.pallas_call(
        body, name=name,
        grid_spec=pltpu.PrefetchScalarGridSpec(
            num_scalar_prefetch=1, grid=(1,),
            in_specs=[pl.BlockSpec(b_, src_at) for b_ in blocks] + [pl.BlockSpec(memory_space=pl.ANY)] * nw,
            out_specs=[pl.BlockSpec(b_, mine) for b_ in blocks]),
        out_shape=[jax.ShapeDtypeStruct(l_.shape, l_.dtype) for l_ in lands],
        input_output_aliases={1 + nw + j: j for j in range(nw)},
        compiler_params=_cp(("arbitrary",)),
    )(jnp.reshape(me, (1,)).astype(jnp.int32), *srcs, *lands)


def _rope_tables(L):
    t = jnp.arange(L)
    inv = ROPE_BASE ** (-jnp.arange(ROPE_FREQS, dtype=F32) / ROPE_FREQS)
    ar = (t // GRID_W).astype(F32)[:, None] * inv
    ac = (t % GRID_W).astype(F32)[:, None] * inv
    z = jnp.zeros_like(ar)
    cos = jnp.concatenate([jnp.cos(ar), jnp.cos(ar), jnp.cos(ac), jnp.cos(ac)], axis=1)
    sa = jnp.concatenate([-jnp.sin(ar), z, -jnp.sin(ac), z], axis=1)
    sb = jnp.concatenate([z, jnp.sin(ar), z, jnp.sin(ac)], axis=1)
    return tuple(jnp.tile(a, (1, LANES // HEAD_DIM)) for a in (cos, sa, sb))


def _nat2d(a):
    return a.reshape(1, -1) if a.ndim == 1 else a.reshape(-1, a.shape[-1])


def _pack_rows(a):
    rows, cols = a.shape
    chunks = -(-cols // LANES)
    f = jnp.pad(a, ((0, 0), (0, chunks * LANES - cols))).reshape(rows * chunks, LANES)
    return jnp.pad(f, ((0, -f.shape[0] % 8), (0, 0)))


def _rows128(a):
    f = a.reshape(-1)
    n = -(-f.shape[0] // (8 * LANES)) * 8 * LANES
    return jnp.pad(f, (0, n - f.shape[0])).reshape(-1, LANES)


def kernel(x, c, ctx, c_ctx, w_ada, b_ada, g_mix_pre, g_mix_post, g_ffn_pre, g_ffn_post, w_in_even, w_pool, pool_scale, attn_sink, w_out_even, w_in_odd, sgu_ln_g, sgu_ln_b, sgu_w, sgu_b, w_out_odd, w_ffn_up, ffn_conv_w, ffn_conv_b, w_ffn_down, loss_target, m_c_ctx, m_w_ada, m_b_ada, m_g_mix_pre, m_g_mix_post, m_g_ffn_pre, m_g_ffn_post, m_w_in_even, m_w_pool, m_pool_scale, m_attn_sink, m_w_out_even, m_w_in_odd, m_sgu_ln_g, m_sgu_ln_b, m_sgu_w, m_sgu_b, m_w_out_odd, m_w_ffn_up, m_ffn_conv_w, m_ffn_conv_b, m_w_ffn_down, v_c_ctx, v_w_ada, v_b_ada, v_g_mix_pre, v_g_mix_post, v_g_ffn_pre, v_g_ffn_post, v_w_in_even, v_w_pool, v_pool_scale, v_attn_sink, v_w_out_even, v_w_in_odd, v_sgu_ln_g, v_sgu_ln_b, v_sgu_w, v_sgu_b, v_w_out_odd, v_w_ffn_up, v_ffn_conv_w, v_ffn_conv_b, v_w_ffn_down):
    P = dict(c_ctx=c_ctx, w_ada=w_ada, b_ada=b_ada, g_mix_pre=g_mix_pre, g_mix_post=g_mix_post, g_ffn_pre=g_ffn_pre,
             g_ffn_post=g_ffn_post, w_in_even=w_in_even, w_pool=w_pool, pool_scale=pool_scale, attn_sink=attn_sink,
             w_out_even=w_out_even, w_in_odd=w_in_odd, sgu_ln_g=sgu_ln_g, sgu_ln_b=sgu_ln_b, sgu_w=sgu_w, sgu_b=sgu_b,
             w_out_odd=w_out_odd, w_ffn_up=w_ffn_up, ffn_conv_w=ffn_conv_w, ffn_conv_b=ffn_conv_b, w_ffn_down=w_ffn_down)
    M = dict(c_ctx=m_c_ctx, w_ada=m_w_ada, b_ada=m_b_ada, g_mix_pre=m_g_mix_pre, g_mix_post=m_g_mix_post, g_ffn_pre=m_g_ffn_pre,
             g_ffn_post=m_g_ffn_post, w_in_even=m_w_in_even, w_pool=m_w_pool, pool_scale=m_pool_scale, attn_sink=m_attn_sink,
             w_out_even=m_w_out_even, w_in_odd=m_w_in_odd, sgu_ln_g=m_sgu_ln_g, sgu_ln_b=m_sgu_ln_b, sgu_w=m_sgu_w, sgu_b=m_sgu_b,
             w_out_odd=m_w_out_odd, w_ffn_up=m_w_ffn_up, ffn_conv_w=m_ffn_conv_w, ffn_conv_b=m_ffn_conv_b, w_ffn_down=m_w_ffn_down)
    V = dict(c_ctx=v_c_ctx, w_ada=v_w_ada, b_ada=v_b_ada, g_mix_pre=v_g_mix_pre, g_mix_post=v_g_mix_post, g_ffn_pre=v_g_ffn_pre,
             g_ffn_post=v_g_ffn_post, w_in_even=v_w_in_even, w_pool=v_w_pool, pool_scale=v_pool_scale, attn_sink=v_attn_sink,
             w_out_even=v_w_out_even, w_in_odd=v_w_in_odd, sgu_ln_g=v_sgu_ln_g, sgu_ln_b=v_sgu_ln_b, sgu_w=v_sgu_w, sgu_b=v_sgu_b,
             w_out_odd=v_w_out_odd, w_ffn_up=v_w_ffn_up, ffn_conv_w=v_ffn_conv_w, ffn_conv_b=v_ffn_conv_b, w_ffn_down=v_w_ffn_down)

    x = x[0]
    ctx = ctx[0]
    target = loss_target[0]
    L, D = x.shape
    C = ctx.shape[0]
    tm = min(512, L)
    conv_rows = min(512, L)
    me = 4 * lax.axis_index("x") + 2 * lax.axis_index("y") + lax.axis_index("c")
    n_ada = w_ada.shape[2]
    F = w_ffn_down.shape[1] * N_DEV
    half_f = F // 2

    n_cw = ffn_conv_w.shape[2]
    small = jnp.concatenate([_rows128(c), _rows128(sgu_ln_g), _rows128(sgu_ln_b), _rows128(ffn_conv_w)], axis=0)
    small_all = all_gather_small(small, name="gather_small_inputs")
    c_all = small_all[:, :8].reshape(N_DEV, D)
    ln_g = small_all[:, 8].reshape(1, D)
    ln_b = small_all[:, 16].reshape(1, D)
    conv_w = small_all[:, 24:].reshape(N_DEV, -1)[:, :2 * 3 * n_cw].reshape(N_DEV, 2, 3, n_cw)
    conv_w = conv_w.transpose(1, 2, 0, 3).reshape(2, 3, 2 * F)

    cs = jnp.concatenate([c_all, c_ctx[None, :], jnp.zeros((7, D), F32)], axis=0)
    b_loc = lax.dynamic_slice(b_ada, (0, me * n_ada), (2, n_ada))
    silu_c, mods_loc = ada_fwd_mm(cs, w_ada, b_loc, name="ada_fwd")
    mods_all = all_gather_small(mods_loc.reshape(-1, LANES), name="gather_mods")

    shards = [s.astype(BF16) for s in (w_in_even[0].T, w_out_even[0], w_ffn_up[0].T, w_ffn_down[0],
                                       w_in_odd[0].T, w_out_odd[0], w_ffn_up[1].T, w_ffn_down[1])]
    shards, mods_all = lax.optimization_barrier((shards, mods_all))
    w_sems, w_srcs, w_lands, _ = exchange_start(shards, place_own(shards, [s.shape[0] for s in shards], me, scatter=False, name="gather_own"),
                                              scatter=False, name="gather_start")

    def weight(j, after):
        return exchange_wait([w_srcs[j]], [w_lands[j]], w_sems[2 * j:2 * j + 2], after, scatter=False, name=f"gather_wait_{j}")[0]

    mods_all = mods_all.reshape(N_DEV, 2, 16, n_ada).transpose(1, 2, 0, 3).reshape(2, 16, 6 * D)
    mod = lambda i, row: [m_[None, :] for m_ in jnp.split(lax.dynamic_index_in_dim(mods_all[i], row, 0, False), 6)]
    sh_m, sc_m, gt_m, sh_f, sc_f, gt_f = zip(mod(0, me), mod(1, me))
    csh_m, csc_m = mod(0, N_DEV)[:2]

    row = lambda a, i: a[i][None, :]

    cos, sa, sb = _rope_tables(L)
    sink = attn_sink[0]
    bst = sgu_b[0].T
    sgu_wb, sgu_wtb = sgu_w[0].astype(BF16), sgu_w[0].swapaxes(1, 2).astype(BF16)
    wup, wdn = [None, None], [None, None]

    def ffn_fwd(i, xin):
        wup[i] = weight(2 + 4 * i, xin)
        h, hu = pre_mm(xin, row(g_ffn_pre, i), sh_f[i], sc_f[i], wup[i], tm=tm, tn=half_f, name=f"ffn_up_{i}")
        a, s1, s2 = conv_fwd(hu, conv_w[i], ffn_conv_b[i][None, :], rows=conv_rows, wblk=2 * LANES, name=f"ffn_conv_{i}")
        wdn[i] = weight(3 + 4 * i, a)
        res = mm_post([a], wdn[i], xin, row(g_ffn_post, i), gt_f[i], tm=tm, target=target if i == 1 else None, name=f"ffn_down_{i}")
        return (h, (hu, s1, s2), a, *res)

    first_mod, cos, sa, sb = lax.optimization_barrier((sh_m[0], cos, sa, sb))
    win_e = permute_heads(weight(0, first_mod))
    h0, u, q, kv = inproj_even(x, row(g_mix_pre, 0), sh_m[0], sc_m[0], win_e, cos, sa, sb, tm=tm, name="in_even")
    hc, kvc = pre_mm(ctx, row(g_mix_pre, 0), csh_m, csc_m, win_e, tm=C, tn=2 * LANES, w_row_off=8 * LANES, name="in_even_ctx")
    pa = [pool_fwd(u, w_pool[0], pool_scale, name="pool_fwd"), attn_fwd(q, kv, kvc, sink, name="attn_fwd")]
    wout_e = permute_heads(weight(1, pa[1]))
    y0, x1 = mm_post(pa, wout_e, x, row(g_mix_post, 0), gt_m[0], tm=tm, name="out_even")
    h1, hu0, a0, f0, x2 = ffn_fwd(0, x1)
    win_o = weight(4, x2)
    h2, z1 = pre_mm(x2, row(g_mix_pre, 1), sh_m[1], sc_m[1], win_o, tm=tm, tn=D, name="in_odd")
    us = sgu_fwd(z1, ln_g, ln_b, sgu_wb, bst, name="sgu_fwd")
    wout_o = weight(5, us)
    y1, x3 = mm_post([us], wout_o, x2, row(g_mix_post, 1), gt_m[1], tm=tm, name="out_odd")
    h3, hu1, a1, f1, dx4, loss_part = ffn_fwd(1, x3)

    g_srcs, g_lands, g_sems = [], [], []

    def scatter(grads, nm):
        own = place_own(grads, [g.shape[0] // N_DEV for g in grads], me, scatter=True, name=nm.replace("start", "own"))
        sems, srcs, lands, tok = exchange_start(grads, own, scatter=True, name=nm)
        g_srcs.extend(srcs)
        g_lands.extend(lands)
        g_sems.extend(sems)
        return tok[0:1, 0:1]

    def ffn_bwd(i, dxo, xin, h, hu, a, f, g_post):
        dyf, da, dg_post, dgt = post_bwd_mm(dxo, f, g_post, gt_f[i], wdn[i], tm=tm, name=f"ffn_down_bwd_{i}")
        dhg, dhu, dcwg, dcwu, dcbg, dcbu = conv_bwd(da, hu[1], hu[2], hu[0], conv_w[i], rows=conv_rows, wblk=2 * LANES,
                                                    name=f"ffn_conv_bwd_{i}")
        dxin, dg_pre, dsh, dsc = mm_pre_bwd([dhg, dhu], wup[i], xin, dxo, row(g_ffn_pre, i), sc_f[i], tm=tm,
                                            name=f"ffn_up_bwd_{i}")
        g_dn = wgrad([a], dyf, tr=2 * LANES, name=f"wgrad_down_{i}")
        g_up = wgrad([dhg, dhu], h, tr=2 * LANES, name=f"wgrad_up_{i}")
        tok = scatter([g_dn, g_up], f"scatter_start_ffn_{i}")
        return dxin, tok, dict(g_ffn_post=dg_post, g_ffn_pre=dg_pre, gt_f=dgt, sh_f=dsh, sc_f=dsc,
                               ffn_conv_w=jnp.concatenate([dcwg, dcwu], axis=1), ffn_conv_b=jnp.concatenate([dcbg, dcbu], axis=1)[0])

    dx3, tok, sf1 = ffn_bwd(1, dx4, x3, h3, hu1, a1, f1, row(g_ffn_post, 1))
    dy1, dus, dg_mpost1, dgt_m1 = post_bwd_mm(dx3, y1, row(g_mix_post, 1) + tok, gt_m[1], wout_o, tm=tm, name="out_odd_bwd")
    dz1, dws, dbs, dlng, dlnb = sgu_bwd(z1, dus, ln_g, ln_b, sgu_wb, sgu_wtb, bst, name="sgu_bwd")
    dx2, dg_mpre1, dsh_m1, dsc_m1 = mm_pre_bwd([dz1], win_o, x2, dx3, row(g_mix_pre, 1), sc_m[1], tm=tm, name="in_odd_bwd")
    tok = scatter([wgrad([us], dy1, tr=2 * LANES, name="wgrad_out_odd"), wgrad([dz1], h2, tr=2 * LANES, name="wgrad_in_odd")],
                  "scatter_start_mix_1")

    dx1, tok, sf0 = ffn_bwd(0, dx2, x1, h1, hu0, a0, f0, row(g_ffn_post, 0) + tok)
    dy0, dpa, dg_mpost0, dgt_m0 = post_bwd_mm(dx1, y0, row(g_mix_post, 0) + tok, gt_m[0], wout_e, tm=tm, name="out_even_bwd")
    tok = scatter([permute_heads(wgrad(pa, dy0, tr=2 * LANES, name="wgrad_out_even"), inverse=True)], "scatter_start_out_0")
    du, dwp, dps = pool_bwd(u, dpa, w_pool[0], pool_scale + tok, name="pool_bwd")
    dq, dkv, dkvc, dsink = attn_bwd(q, kv, kvc, sink, dpa, cos, sa, sb, name="attn_bwd")
    dz0 = jnp.concatenate([du, dq, dkv], axis=1)
    dzc = jnp.concatenate([jnp.zeros((C, 8 * LANES), BF16), dkvc], axis=1)
    tok = scatter([permute_heads(wgrad([dz0], h0, tr=2 * LANES, extra=(dzc, hc), name="wgrad_in_even"), inverse=True)],
                  "scatter_start_in_0")
    grad_x, dg_mpre0, dsh_m0, dsc_m0 = mm_pre_bwd([dz0], win_e, x, dx1, row(g_mix_pre, 0) + tok, sc_m[0], tm=tm,
                                                  name="in_even_bwd")
    _, dg_mpre0c, dcsh, dcsc = mm_pre_bwd([dkvc], win_e, ctx, None, row(g_mix_pre, 0), csc_m, tm=C,
                                          w_row_off=8 * LANES, name="in_even_ctx_bwd")

    out, ran = {}, {}

    def update(name, lands, transposed):
        w_, m_, v_ = (a.transpose(0, 2, 1) if transposed else a for a in (P[name], M[name], V[name]))
        r = w_.shape[1]
        tr = r // 4 if r % 64 == 0 and r > 256 else r
        res = adamw(w_, m_, v_, [l_.reshape(N_DEV, r, l_.shape[1]) for l_ in lands], tr=tr, name=f"adamw_{name}")
        ran[name] = res[0]
        for kind, val in zip(("grad", "delta", "new_m", "new_v"), res):
            out[(kind, name)] = val.transpose(0, 2, 1) if transposed else val

    zero = jnp.zeros((1, D), F32)
    dmod0 = jnp.concatenate([dsh_m0, dsc_m0, dgt_m0, sf0["sh_f"], sf0["sc_f"], sf0["gt_f"]], axis=1)
    dmodc = jnp.concatenate([dcsh, dcsc, zero, zero, zero, zero], axis=1)
    dmod1 = jnp.concatenate([dsh_m1, dsc_m1, dgt_m1, sf1["sh_f"], sf1["sc_f"], sf1["gt_f"]], axis=1)
    dmods = jnp.concatenate([dmod0, dmodc, dmod1], axis=0)
    dm = dmods.reshape(-1, LANES).astype(BF16)
    d_sems, d_srcs, d_lands, d_tok = exchange_start(
        [dm], place_own([dm], [dm.shape[0]], me, scatter=False, name="dmods_own"), scatter=False, name="dmods_start")
    slots = exchange_wait(g_srcs[:6], g_lands[:6], g_sems[:12], d_tok, scatter=True, name="scatter_wait_early")
    early = slots
    update("w_ffn_down", [slots[4], slots[0]], False)
    update("w_in_odd", [slots[3]], True)
    update("w_out_odd", [slots[2]], False)
    updated = lambda names: [ran[k] for k in names]
    dmods_all = exchange_wait(d_srcs, d_lands, d_sems, updated(("w_out_odd",)), scatter=False, name="dmods_wait")[0]
    dall = lax.dynamic_index_in_dim(dmods_all.astype(F32).reshape(N_DEV, 3, N_DEV, n_ada), me, 2, False)
    g_w_ada, dcc = ada_bwd_mm(silu_c, c_ctx[None, :], dall, w_ada, name="ada_bwd")

    rep = dict(
        c_ctx=dcc[0:1],
        b_ada=jnp.concatenate([dmod0 + dmodc, dmod1]),
        g_mix_pre=jnp.concatenate([dg_mpre0 + dg_mpre0c, dg_mpre1]),
        g_mix_post=jnp.concatenate([dg_mpost0, dg_mpost1]),
        g_ffn_pre=jnp.concatenate([sf0["g_ffn_pre"], sf1["g_ffn_pre"]]),
        g_ffn_post=jnp.concatenate([sf0["g_ffn_post"], sf1["g_ffn_post"]]),
        w_pool=_nat2d(dwp), pool_scale=dps, attn_sink=dsink[:, :N_Q_HEADS],
        sgu_w=_nat2d(dws), sgu_b=dbs[:, :sgu_b.shape[1]].T,
        ffn_conv_b=jnp.stack([sf0["ffn_conv_b"], sf1["ffn_conv_b"]]),
    )
    hi = loss_part.astype(BF16).astype(F32)
    mid = (loss_part - hi).astype(BF16).astype(F32)
    loss_piece = jnp.pad(jnp.concatenate([hi, mid, loss_part - hi - mid], axis=1), ((0, 7), (0, LANES - 3)))
    conv_g = jnp.stack([sf0["ffn_conv_w"], sf1["ffn_conv_w"]]).reshape(2 * 3, N_DEV, n_cw).swapaxes(0, 1)
    shard_full = dict(sgu_ln_g=dlng.reshape(N_DEV, LANES), sgu_ln_b=dlnb.reshape(N_DEV, LANES),
                      ffn_conv_w=jnp.concatenate([_pack_rows(conv_g[d]) for d in range(N_DEV)], axis=0))
    small_names = list(rep) + list(shard_full)
    pieces = [_pack_rows(rep[k]) for k in rep] + list(shard_full.values()) + [loss_piece]
    sizes = [p.shape[0] for p in pieces]
    offs = [sum(sizes[:i]) for i in range(len(sizes))]
    pieces.append(jnp.zeros((-sum(sizes) % 16, LANES), F32))
    gpack = jnp.concatenate(pieces, axis=0).astype(BF16)
    own = place_own([gpack], [gpack.shape[0]], me, scatter=False, name="smallgrad_own")
    s_sems, s_srcs, s_lands, small_tok = exchange_start([gpack], own, scatter=False, name="smallgrad_start")

    slots = exchange_wait(g_srcs[6:], g_lands[6:], g_sems[12:], small_tok, scatter=True, name="scatter_wait_late")
    update("w_in_even", [slots[1]], True)
    update("w_out_even", [slots[0]], False)
    update("w_ffn_up", [early[5], early[1]], True)
    res = adamw(w_ada, m_w_ada, v_w_ada, [g_w_ada[l][None] for l in range(w_ada.shape[0])], tr=D // 4, name="adamw_w_ada")
    ran["w_ada"] = res[0]
    for kind, val in zip(("grad", "delta", "new_m", "new_v"), res):
        out[(kind, "w_ada")] = val

    gpacks = exchange_wait(s_srcs, s_lands, s_sems, updated(("w_ada",)), scatter=False,
                           name="smallgrad_wait")[0]
    per_dev = {k: shard_full[k].shape[0] // N_DEV for k in shard_full}
    params = [(_nat2d(P[k]), _nat2d(M[k]), _nat2d(V[k]), offs[i], per_dev.get(k, 0)) for i, k in enumerate(small_names)]
    res = small_update(gpacks.reshape(N_DEV, -1, LANES), jnp.reshape(me, (1,)).astype(jnp.int32), params, offs[-1], name="adamw_small")
    for i, k in enumerate(small_names):
        for kind, val in zip(("grad", "delta", "new_m", "new_v"), res[4 * i:4 * i + 4]):
            out[(kind, k)] = val.reshape(P[k].shape)
    loss = res[-1][0, 0]

    names = list(P)
    final = [loss, grad_x[None]]
    for kind in ("grad", "delta", "new_m", "new_v"):
        for k in names:
            val = out[(kind, k)]
            final.append(val)
    return tuple(final)
```

```python
import functools
import math

import jax
import jax.numpy as jnp
from jax import lax
from jax.experimental import pallas as pl
from jax.experimental.pallas import tpu as pltpu

F32 = jnp.float32
BF16 = jnp.bfloat16
MESH = pl.DeviceIdType.MESH
N_DEV = 8
LANES = 128
VMEM_LIMIT = 48 * 1024 * 1024
EPS = 1e-6
NEG_INF = -1e30
GRID_W = 64
WINDOW = 128
BLK = 128
HEAD_DIM = 64
N_Q_HEADS = 8
N_KV_HEADS = 2
GQA = N_Q_HEADS // N_KV_HEADS
POOL_WINDOWS = (2, 4, 8, 16)
ROPE_BASE = 10000.0
ROPE_FREQS = HEAD_DIM // 4
PAD = 16
ADAM_LR, ADAM_B1, ADAM_B2, ADAM_EPS, ADAM_WD, ADAM_STEP = 0.001, 0.9, 0.999, 1e-08, 0.01, 10
BC1 = 1.0 - ADAM_B1 ** ADAM_STEP
BC2 = 1.0 - ADAM_B2 ** ADAM_STEP
SQRT_2_OVER_PI = math.sqrt(2.0 / math.pi)
GELU_C = 0.044715


def _cp(sem=None):
    return pltpu.CompilerParams(dimension_semantics=sem, vmem_limit_bytes=VMEM_LIMIT)


def _dot(a, b):
    return jnp.dot(a, b, preferred_element_type=F32)


def _dot_nt(a, b):
    return lax.dot_general(a, b, (((1,), (1,)), ((), ())), preferred_element_type=F32)


def _dot_tn(a, b):
    return lax.dot_general(a, b, (((0,), (0,)), ((), ())), preferred_element_type=F32)


def _rms(x):
    r = lax.rsqrt(jnp.mean(x * x, axis=-1, keepdims=True) + EPS)
    return x * r, r


def _rms_bwd(dn, n, r):
    return r * (dn - n * jnp.mean(dn * n, axis=-1, keepdims=True))


def _colsum(a):
    return jnp.sum(a, axis=0, keepdims=True)


def _rope(x, c, sa, sb):
    return x * c + pltpu.roll(x, LANES - ROPE_FREQS, 1) * sa + pltpu.roll(x, ROPE_FREQS, 1) * sb


def _full(shape):
    return pl.BlockSpec(shape, lambda *_: (0,) * len(shape))


def pre_mm(x, g, sh, sc, wt, *, tm, tn, w_row_off=0, name):
    T, D = x.shape
    n_rows = wt.shape[0] - w_row_off

    def body(x_ref, g_ref, sh_ref, sc_ref, w_ref, h_ref, z_ref):
        n, _ = _rms(x_ref[...])
        h = (n * g_ref[...] * (1.0 + sc_ref[...]) + sh_ref[...]).astype(BF16)
        h_ref[...] = h
        for c0 in range(0, n_rows, tn):
            z_ref[:, c0:c0 + tn] = _dot_nt(h, w_ref[c0:c0 + tn, :]).astype(BF16)

    vec = pl.BlockSpec((1, D), lambda i: (0, 0))
    return pl.pallas_call(
        body, name=name, grid=(T // tm,),
        in_specs=[pl.BlockSpec((tm, D), lambda i: (i, 0)), vec, vec, vec,
                  pl.BlockSpec((n_rows, D), lambda i: (w_row_off // n_rows, 0), pipeline_mode=pl.Buffered(1))],
        out_specs=[pl.BlockSpec((tm, D), lambda i: (i, 0)), pl.BlockSpec((tm, n_rows), lambda i: (i, 0))],
        out_shape=[jax.ShapeDtypeStruct((T, D), BF16), jax.ShapeDtypeStruct((T, n_rows), BF16)],
        compiler_params=_cp(("parallel",)),
    )(x, g, sh, sc, wt)


def inproj_even(x, g, sh, sc, wt, cos, sa, sb, *, tm, name):
    T, D = x.shape
    N = wt.shape[0]

    def body(x_ref, g_ref, sh_ref, sc_ref, w_ref, c_ref, sa_ref, sb_ref, h_ref, u_ref, q_ref, kv_ref):
        n, _ = _rms(x_ref[...])
        h = (n * g_ref[...] * (1.0 + sc_ref[...]) + sh_ref[...]).astype(BF16)
        h_ref[...] = h
        z = _dot_nt(h, w_ref[...])
        u_ref[...] = z[:, :4 * LANES]
        c, a, b = c_ref[...], sa_ref[...], sb_ref[...]
        for s in range(4):
            q_ref[:, s * LANES:(s + 1) * LANES] = _rope(z[:, (4 + s) * LANES:(5 + s) * LANES], c, a, b).astype(BF16)
        kv_ref[:, :LANES] = _rope(z[:, 8 * LANES:9 * LANES], c, a, b).astype(BF16)
        kv_ref[:, LANES:] = z[:, 9 * LANES:].astype(BF16)

    vec = pl.BlockSpec((1, D), lambda i: (0, 0))
    row = lambda w: pl.BlockSpec((tm, w), lambda i: (i, 0))
    return pl.pallas_call(
        body, name=name, grid=(T // tm,),
        in_specs=[row(D), vec, vec, vec, _full((N, D)), row(LANES), row(LANES), row(LANES)],
        out_specs=[row(D), row(4 * LANES), row(4 * LANES), row(2 * LANES)],
        out_shape=[jax.ShapeDtypeStruct((T, D), BF16), jax.ShapeDtypeStruct((T, 4 * LANES), F32),
                   jax.ShapeDtypeStruct((T, 4 * LANES), BF16), jax.ShapeDtypeStruct((T, 2 * LANES), BF16)],
        compiler_params=_cp(("parallel",)),
    )(x, g, sh, sc, wt, cos, sa, sb)


def mm_post(a_parts, w, x, g, gt, *, tm, target=None, name):
    T = a_parts[0].shape[0]
    D = w.shape[1]
    npart = len(a_parts)
    offs = [sum(a_.shape[1] for a_ in a_parts[:p]) for p in range(npart + 1)]
    with_loss = target is not None

    def body(*refs):
        a_refs, (w_ref, x_ref, g_ref, gt_ref) = refs[:npart], refs[npart:npart + 4]
        y = _dot(a_refs[0][...], w_ref[offs[0]:offs[1], :])
        for p in range(1, npart):
            y = y + _dot(a_refs[p][...], w_ref[offs[p]:offs[p + 1], :])
        n, _ = _rms(y)
        xn = x_ref[...] + gt_ref[...] * (n * g_ref[...])
        if not with_loss:
            y_ref, xn_ref = refs[npart + 4:]
            y_ref[...] = y.astype(BF16)
            xn_ref[...] = xn
            return
        t_ref, y_ref, d_ref, l_ref = refs[npart + 4:]
        y_ref[...] = y.astype(BF16)

        @pl.when(pl.program_id(0) == 0)
        def _():
            l_ref[...] = jnp.zeros_like(l_ref)

        e = xn - t_ref[...]
        l_ref[...] += 0.5 * jnp.sum(jnp.mean(e * e, axis=-1, keepdims=True), axis=0, keepdims=True)
        d_ref[...] = e * (1.0 / D)

    vec = pl.BlockSpec((1, D), lambda i: (0, 0))
    row = lambda w_: pl.BlockSpec((tm, w_), lambda i: (i, 0))
    in_specs = [row(a_.shape[1]) for a_ in a_parts] + [_full(w.shape), row(D), vec, vec]
    out_specs = [row(D), row(D)]
    out_shape = [jax.ShapeDtypeStruct((T, D), BF16), jax.ShapeDtypeStruct((T, D), F32)]
    if with_loss:
        in_specs.append(row(D))
        out_specs.append(_full((1, 1)))
        out_shape.append(jax.ShapeDtypeStruct((1, 1), F32))
    return pl.pallas_call(
        body, name=name, grid=(T // tm,), in_specs=in_specs, out_specs=out_specs, out_shape=out_shape,
        compiler_params=_cp(("arbitrary",) if with_loss else ("parallel",)),
    )(*a_parts, w, x, g, gt, *((target,) if with_loss else ()))


def post_bwd_mm(dxn, y, g, gt, w, *, tm, name):
    T, D = y.shape
    K = w.shape[0]

    def body(dxn_ref, y_ref, g_ref, gt_ref, w_ref, dy_ref, da_ref, dg_ref, dgt_ref):
        @pl.when(pl.program_id(0) == 0)
        def _():
            dg_ref[...] = jnp.zeros_like(dg_ref)
            dgt_ref[...] = jnp.zeros_like(dgt_ref)

        d = dxn_ref[...]
        n, r = _rms(y_ref[...].astype(F32))
        g_, gt_ = g_ref[...], gt_ref[...]
        dg_ref[...] += _colsum(d * gt_ * n)
        dgt_ref[...] += _colsum(d * g_ * n)
        dy = _rms_bwd(d * (gt_ * g_), n, r).astype(BF16)
        dy_ref[...] = dy
        da_ref[...] = _dot_nt(dy, w_ref[...]).astype(BF16)

    vec = pl.BlockSpec((1, D), lambda i: (0, 0))
    row = lambda w_: pl.BlockSpec((tm, w_), lambda i: (i, 0))
    return pl.pallas_call(
        body, name=name, grid=(T // tm,),
        in_specs=[row(D), row(D), vec, vec, _full((K, D))],
        out_specs=[row(D), row(K), vec, vec],
        out_shape=[jax.ShapeDtypeStruct((T, D), BF16), jax.ShapeDtypeStruct((T, K), BF16),
                   jax.ShapeDtypeStruct((1, D), F32), jax.ShapeDtypeStruct((1, D), F32)],
        compiler_params=_cp(("arbitrary",)),
    )(dxn, y, g, gt, w)


def mm_pre_bwd(dzs, wt, x, dres, g, sc, *, tm, w_row_off=0, name):
    T, N = dzs[0].shape
    D = x.shape[1]
    npart = len(dzs)
    off = w_row_off // N
    has_res = dres is not None

    def body(*refs):
        dz_refs = refs[:npart]
        w_refs = refs[npart:2 * npart]
        rest = refs[2 * npart:]
        x_ref = rest[0]
        dres_ref = rest[1] if has_res else None
        g_ref, sc_ref, dx_ref, dg_ref, dsh_ref, dsc_ref = rest[1 + has_res:]

        @pl.when(pl.program_id(0) == 0)
        def _():
            dg_ref[...] = jnp.zeros_like(dg_ref)
            dsh_ref[...] = jnp.zeros_like(dsh_ref)
            dsc_ref[...] = jnp.zeros_like(dsc_ref)

        dh = _dot(dz_refs[0][...], w_refs[0][...])
        for p in range(1, npart):
            dh = dh + _dot(dz_refs[p][...], w_refs[p][...])
        n, r = _rms(x_ref[...])
        g_, s1 = g_ref[...], 1.0 + sc_ref[...]
        dsh_ref[...] += _colsum(dh)
        dsc_ref[...] += _colsum(dh * n * g_)
        dg_ref[...] += _colsum(dh * s1 * n)
        dxp = _rms_bwd(dh * (g_ * s1), n, r)
        dx_ref[...] = dxp + dres_ref[...] if has_res else dxp

    vec = pl.BlockSpec((1, D), lambda i: (0, 0))
    row = pl.BlockSpec((tm, D), lambda i: (i, 0))
    w_specs = [pl.BlockSpec((N, D), (lambda i, p=p: (off + p, 0)), pipeline_mode=pl.Buffered(1)) for p in range(npart)]
    res_specs, res_args = ([row], (dres,)) if has_res else ([], ())
    return pl.pallas_call(
        body, name=name, grid=(T // tm,),
        in_specs=[pl.BlockSpec((tm, N), lambda i: (i, 0))] * npart + w_specs + [row] + res_specs + [vec, vec],
        out_specs=[row, vec, vec, vec],
        out_shape=[jax.ShapeDtypeStruct((T, D), F32)] + [jax.ShapeDtypeStruct((1, D), F32)] * 3,
        compiler_params=_cp(("arbitrary",)),
    )(*dzs, *([wt] * npart), x, *res_args, g, sc)


def wgrad(a_parts, b, *, tr, extra=None, name):
    T, R = a_parts[0].shape
    D = b.shape[1]
    npart = len(a_parts)
    nr = R // tr

    def body(*refs):
        a_refs, b_ref = refs[:npart], refs[npart]
        g_ref = refs[-1]
        for p in range(npart):
            @pl.when(pl.program_id(0) // nr == p)
            def _():
                acc = _dot_tn(a_refs[p][...], b_ref[...])
                if extra is not None:
                    acc += _dot_tn(refs[npart + 1][...], refs[npart + 2][...])
                g_ref[...] = acc.astype(BF16)

    in_specs = [pl.BlockSpec((T, tr), (lambda r, p=p: (0, jnp.clip(r - p * nr, 0, nr - 1)))) for p in range(npart)]
    in_specs.append(_full((T, D)))
    args = [*a_parts, b]
    if extra is not None:
        a2, b2 = extra
        in_specs += [pl.BlockSpec((a2.shape[0], tr), lambda r: (0, r)), _full(b2.shape)]
        args += [a2, b2]
    return pl.pallas_call(
        body, name=name, grid=(npart * nr,),
        in_specs=in_specs, out_specs=pl.BlockSpec((tr, D), lambda r: (r, 0)),
        out_shape=jax.ShapeDtypeStruct((npart * R, D), BF16),
        compiler_params=_cp(("parallel",)),
    )(*args)


def _conv_ext(ref, r0, rows, total):
    top = ref[pl.ds(pl.multiple_of(jnp.maximum(r0 - PAD, 0), PAD), PAD), :]
    mid = ref[pl.ds(r0, rows), :]
    bot = ref[pl.ds(pl.multiple_of(jnp.minimum(r0 + rows, total - PAD), PAD), PAD), :]
    top = jnp.where(r0 > 0, top, jnp.zeros_like(top))
    bot = jnp.where(r0 + rows < total, bot, jnp.zeros_like(bot))
    return jnp.concatenate([top, mid, bot], axis=0).astype(F32)


def _shift_rows(a, k):
    return pltpu.roll(a, k % a.shape[0], 0)


def _conv3(x, w, b):
    return w[0:1] * _shift_rows(x, 1) + w[1:2] * x + w[2:3] * _shift_rows(x, -1) + b


def _gate_up_specs(rows_, wblk, nb):
    return [pl.BlockSpec((rows_, wblk), lambda j: (0, j)), pl.BlockSpec((rows_, wblk), lambda j: (0, j + nb))]


def conv_fwd(hu, cw, cb, *, rows, wblk, name):
    L, N2 = hu.shape
    nb = N2 // 2 // wblk
    nchunk = L // rows

    def body(hg_ref, hu_ref, wg_ref, wu_ref, bg_ref, bu_ref, a_ref, s1_ref, s2_ref):
        def chunk(ci, carry):
            r0 = pl.multiple_of(ci * rows, rows)
            gate = _conv3(_conv_ext(hg_ref, r0, rows, L), wg_ref[...], bg_ref[...])[PAD:PAD + rows]
            up = _conv3(_conv_ext(hu_ref, r0, rows, L), wu_ref[...], bu_ref[...])[PAD:PAD + rows]
            sg = jax.nn.sigmoid(gate)
            silu = gate * sg
            at = pl.ds(r0, rows)
            a_ref[at, :] = (silu * up).astype(BF16)
            s1_ref[at, :] = silu.astype(BF16)
            s2_ref[at, :] = (up * (sg + silu * (1.0 - sg))).astype(BF16)
            return carry

        lax.fori_loop(0, nchunk, chunk, 0)

    out = pl.BlockSpec((L, wblk), lambda j: (0, j))
    return pl.pallas_call(
        body, name=name, grid=(nb,),
        in_specs=_gate_up_specs(L, wblk, nb) + _gate_up_specs(3, wblk, nb) + _gate_up_specs(1, wblk, nb),
        out_specs=[out] * 3, out_shape=[jax.ShapeDtypeStruct((L, N2 // 2), BF16)] * 3,
        compiler_params=_cp(("parallel",)),
    )(hu, hu, cw, cw, cb, cb)


def conv_bwd(da, s1, s2, hu, cw, *, rows, wblk, name):
    L, N2 = hu.shape
    F = N2 // 2
    nb = F // wblk
    nchunk = L // rows
    mid = slice(PAD, PAD + rows)

    def body(da_ref, s1_ref, s2_ref, hg_ref, hu_ref, wg_ref, wu_ref, dg_ref, du_ref, dwg_ref, dwu_ref, dbg_ref, dbu_ref):
        for ref in (dwg_ref, dwu_ref, dbg_ref, dbu_ref):
            ref[...] = jnp.zeros_like(ref)

        def half_bwd(x_ref, dh, w_ref, dx_ref, dw_ref, db_ref, r0):
            w = w_ref[...]
            nxt, prv = _shift_rows(dh, -1)[mid], _shift_rows(dh, 1)[mid]
            dhm, xm = dh[mid], x_ref[pl.ds(r0, rows), :].astype(F32)
            dx_ref[pl.ds(r0, rows), :] = (w[0:1] * nxt + w[1:2] * dhm + w[2:3] * prv).astype(BF16)
            db_ref[...] += _colsum(dhm)
            dw_ref[0:1, :] += _colsum(nxt * xm)
            dw_ref[1:2, :] += _colsum(dhm * xm)
            dw_ref[2:3, :] += _colsum(prv * xm)

        def chunk(ci, carry):
            r0 = pl.multiple_of(ci * rows, rows)
            d = _conv_ext(da_ref, r0, rows, L)
            half_bwd(hu_ref, d * _conv_ext(s1_ref, r0, rows, L), wu_ref, du_ref, dwu_ref, dbu_ref, r0)
            half_bwd(hg_ref, d * _conv_ext(s2_ref, r0, rows, L), wg_ref, dg_ref, dwg_ref, dbg_ref, r0)
            return carry

        lax.fori_loop(0, nchunk, chunk, 0)

    blk = lambda r: pl.BlockSpec((r, wblk), lambda j: (0, j))
    return pl.pallas_call(
        body, name=name, grid=(nb,),
        in_specs=[blk(L)] * 3 + _gate_up_specs(L, wblk, nb) + _gate_up_specs(3, wblk, nb),
        out_specs=[blk(L), blk(L), blk(3), blk(3), blk(1), blk(1)],
        out_shape=[jax.ShapeDtypeStruct((L, F), BF16)] * 2 + [jax.ShapeDtypeStruct((3, F), F32)] * 2
        + [jax.ShapeDtypeStruct((1, F), F32)] * 2,
        compiler_params=_cp(("parallel",)),
    )(da, s1, s2, hu, hu, cw, cw)


def _window_sums(pad_ref, w, lead):
    a = pad_ref[...]
    k = 1
    while k < w:
        a = a + _shift_rows(a, -k)
        k *= 2
    return _shift_rows(a, lead) if lead else a


def _pool_counts(L, h):
    t = lax.broadcasted_iota(jnp.int32, (L, 1), 0)
    return (jnp.minimum(t + h, L) - jnp.maximum(t - h, 0)).astype(F32)


def _pooled(u_ref, pad_ref, L, w):
    h = w // 2
    pad_ref[pl.ds(PAD, L), :] = u_ref[...]
    win = _window_sums(pad_ref, w, h)[PAD:PAD + L]
    return win / _pool_counts(L, h) - u_ref[...]


def _zero_pad_edges(pad_ref, L):
    z = jnp.zeros((PAD, LANES), F32)
    pad_ref[pl.ds(0, PAD), :] = z
    pad_ref[pl.ds(PAD + L, PAD), :] = z


def pool_fwd(u, w_pool, pool_scale, *, name):
    L = u.shape[0]

    def body(u_ref, w_ref, ps_ref, p_ref, pad_ref):
        _zero_pad_edges(pad_ref, L)
        for gi, win in enumerate(POOL_WINDOWS):
            @pl.when(pl.program_id(0) == gi)
            def _():
                pooled = _pooled(u_ref, pad_ref, L, win)
                p_ref[...] = (_dot(pooled.astype(BF16), w_ref[...].astype(BF16)) * ps_ref[...]).astype(BF16)

    return pl.pallas_call(
        body, name=name, grid=(len(POOL_WINDOWS),),
        in_specs=[pl.BlockSpec((L, LANES), lambda gi: (0, gi)), pl.BlockSpec((None, LANES, LANES), lambda gi: (gi, 0, 0)),
                  pl.BlockSpec((1, LANES), lambda gi: (0, gi))],
        out_specs=pl.BlockSpec((L, LANES), lambda gi: (0, gi)),
        out_shape=jax.ShapeDtypeStruct((L, 4 * LANES), BF16),
        scratch_shapes=[pltpu.VMEM((L + 2 * PAD, LANES), F32)],
        compiler_params=_cp(("parallel",)),
    )(u, w_pool, pool_scale)


def pool_bwd(u, dpa, w_pool, pool_scale, *, name):
    L = u.shape[0]

    def body(u_ref, dp_ref, w_ref, ps_ref, du_ref, dw_ref, dps_ref, pad_ref):
        _zero_pad_edges(pad_ref, L)
        for gi, win in enumerate(POOL_WINDOWS):
            @pl.when(pl.program_id(0) == gi)
            def _():
                h = win // 2
                wb = w_ref[...].astype(BF16)
                pooled = _pooled(u_ref, pad_ref, L, win).astype(BF16)
                dp = dp_ref[...].astype(F32)
                dps_ref[...] = _colsum(dp * _dot(pooled, wb))
                dy = (dp * ps_ref[...]).astype(BF16)
                dw_ref[...] = _dot_tn(pooled, dy)
                dpooled = _dot_nt(dy, wb)
                pad_ref[pl.ds(PAD, L), :] = dpooled / _pool_counts(L, h)
                du_ref[...] = (_window_sums(pad_ref, win, h - 1)[PAD:PAD + L] - dpooled).astype(BF16)

    return pl.pallas_call(
        body, name=name, grid=(len(POOL_WINDOWS),),
        in_specs=[pl.BlockSpec((L, LANES), lambda gi: (0, gi)), pl.BlockSpec((L, LANES), lambda gi: (0, gi)),
                  pl.BlockSpec((None, LANES, LANES), lambda gi: (gi, 0, 0)), pl.BlockSpec((1, LANES), lambda gi: (0, gi))],
        out_specs=[pl.BlockSpec((L, LANES), lambda gi: (0, gi)), pl.BlockSpec((None, LANES, LANES), lambda gi: (gi, 0, 0)),
                   pl.BlockSpec((1, LANES), lambda gi: (0, gi))],
        out_shape=[jax.ShapeDtypeStruct((L, 4 * LANES), BF16), jax.ShapeDtypeStruct((4, LANES, LANES), F32),
                   jax.ShapeDtypeStruct((1, 4 * LANES), F32)],
        scratch_shapes=[pltpu.VMEM((L + 2 * PAD, LANES), F32)],
        compiler_params=_cp(("parallel",)),
    )(u, dpa, w_pool, pool_scale)


def _attn_probs(qk, band_k, ctx_k, sink_ref, kh, mask4):
    s_loc = jnp.where(mask4, _dot_nt(qk, band_k), NEG_INF)
    s_ctx = _dot_nt(qk, ctx_k)
    sk = jnp.concatenate([jnp.full((BLK, 1), sink_ref[kh * GQA + hh], F32) for hh in range(GQA)], axis=0)
    m = jnp.maximum(jnp.maximum(jnp.max(s_loc, axis=-1, keepdims=True), jnp.max(s_ctx, axis=-1, keepdims=True)), sk)
    e_loc, e_ctx, e_s = jnp.exp(s_loc - m), jnp.exp(s_ctx - m), jnp.exp(sk - m)
    inv = 1.0 / (jnp.sum(e_loc, axis=-1, keepdims=True) + jnp.sum(e_ctx, axis=-1, keepdims=True) + e_s)
    return e_loc * inv, e_ctx * inv, e_s * inv


def _attn_block(n, L):
    start = pl.multiple_of(jnp.clip((n - 1) * BLK, 0, L - 3 * BLK), BLK)
    qpos = n * BLK + lax.broadcasted_iota(jnp.int32, (BLK, 3 * BLK), 0)
    kpos = start + lax.broadcasted_iota(jnp.int32, (BLK, 3 * BLK), 1)
    mask = jnp.abs(kpos - qpos) <= WINDOW
    return start, jnp.concatenate([mask] * GQA, axis=0)


def _stack_slabs(ref):
    return jnp.concatenate([ref[:, s * LANES:(s + 1) * LANES] for s in range(GQA)], axis=0)


def _kv_head_lanes(kh):
    return (lax.broadcasted_iota(jnp.int32, (1, LANES), 1) // HEAD_DIM) == kh


def permute_heads(w, inverse=False):
    lo, hi = 4 * LANES, 8 * LANES
    mid = w[lo:hi].reshape(*((GQA, N_KV_HEADS) if inverse else (N_KV_HEADS, GQA)), HEAD_DIM, w.shape[1])
    return jnp.concatenate([w[:lo], mid.swapaxes(0, 1).reshape(hi - lo, w.shape[1]), w[hi:]], axis=0)


def attn_fwd(q, kv, kvc, sink, *, name):
    L = q.shape[0]
    C = kvc.shape[0]
    scale = HEAD_DIM ** -0.5

    def body(q_ref, kv_ref, kvc_ref, sink_ref, o_ref):
        start, mask4 = _attn_block(pl.program_id(0), L)
        band = kv_ref[pl.ds(start, 3 * BLK), :]
        kvc_ = kvc_ref[...]
        qs = _stack_slabs(q_ref) * scale
        o = jnp.zeros((GQA * BLK, LANES), F32)
        for kh in range(N_KV_HEADS):
            grp = _kv_head_lanes(kh)
            qk = jnp.where(grp, qs, jnp.zeros_like(qs))
            p_loc, p_ctx, _ = _attn_probs(qk, band[:, :LANES], kvc_[:, :LANES], sink_ref, kh, mask4)
            o = o + jnp.where(grp, _dot(p_loc.astype(BF16), band[:, LANES:]) + _dot(p_ctx.astype(BF16), kvc_[:, LANES:]), 0.0)
        for s in range(GQA):
            o_ref[:, s * LANES:(s + 1) * LANES] = o[s * BLK:(s + 1) * BLK].astype(BF16)

    return pl.pallas_call(
        body, name=name, grid=(L // BLK,),
        in_specs=[pl.BlockSpec((BLK, 4 * LANES), lambda n: (n, 0)), _full((L, 2 * LANES)), _full((C, 2 * LANES)),
                  pl.BlockSpec(memory_space=pltpu.SMEM)],
        out_specs=pl.BlockSpec((BLK, 4 * LANES), lambda n: (n, 0)),
        out_shape=jax.ShapeDtypeStruct((L, 4 * LANES), BF16),
        compiler_params=_cp(("parallel",)),
    )(q, kv, kvc, sink)


def attn_bwd(q, kv, kvc, sink, dpa, cos, sa, sb, *, name):
    L = q.shape[0]
    C = kvc.shape[0]
    nb = L // BLK
    scale = HEAD_DIM ** -0.5

    def body(q_ref, kv_ref, kvc_ref, sink_ref, do_ref, c_ref, sa_ref, sb_ref, cq_ref, saq_ref, sbq_ref,
             dq_ref, dkv_ref, dkvc_ref, dsink_ref, dkv_acc, dkvc_acc):
        n = pl.program_id(0)

        @pl.when(n == 0)
        def _():
            dkv_acc[...] = jnp.zeros_like(dkv_acc)
            dkvc_acc[...] = jnp.zeros_like(dkvc_acc)
            dsink_ref[...] = jnp.zeros_like(dsink_ref)

        start, mask4 = _attn_block(n, L)
        band = kv_ref[pl.ds(start, 3 * BLK), :]
        kvc_ = kvc_ref[...]
        band_k, band_v, ctx_k, ctx_v = band[:, :LANES], band[:, LANES:], kvc_[:, :LANES], kvc_[:, LANES:]
        qs = _stack_slabs(q_ref) * scale
        dos = _stack_slabs(do_ref)
        lane = lax.broadcasted_iota(jnp.int32, (1, LANES), 1)
        dsink = jnp.zeros((1, LANES), F32)
        dq = jnp.zeros((GQA * BLK, LANES), F32)
        dk = jnp.zeros((LANES, 3 * BLK), F32)
        dv = jnp.zeros((LANES, 3 * BLK), F32)
        dkc = jnp.zeros((LANES, C), F32)
        dvc = jnp.zeros((LANES, C), F32)
        for kh in range(N_KV_HEADS):
            grp = _kv_head_lanes(kh)
            qk = jnp.where(grp, qs, jnp.zeros_like(qs))
            dok = jnp.where(grp, dos, jnp.zeros_like(dos))
            p_loc, p_ctx, p_s = _attn_probs(qk, band_k, ctx_k, sink_ref, kh, mask4)
            dp_loc = _dot_nt(dok, band_v)
            dp_ctx = _dot_nt(dok, ctx_v)
            delta = jnp.sum(p_loc * dp_loc, axis=-1, keepdims=True) + jnp.sum(p_ctx * dp_ctx, axis=-1, keepdims=True)
            ds_loc = (p_loc * (dp_loc - delta)).astype(BF16)
            ds_ctx = (p_ctx * (dp_ctx - delta)).astype(BF16)
            dsk = p_s * delta
            for hh in range(GQA):
                dsink = dsink - jnp.where(lane == kh * GQA + hh, jnp.sum(dsk[hh * BLK:(hh + 1) * BLK], axis=0, keepdims=True), 0.0)
            dq = dq + jnp.where(grp, _dot(ds_loc, band_k) + _dot(ds_ctx, ctx_k), 0.0)
            dk = dk + _dot_tn(qk, ds_loc)
            dv = dv + _dot_tn(dok, p_loc.astype(BF16))
            dkc = dkc + _dot_tn(qk, ds_ctx)
            dvc = dvc + _dot_tn(dok, p_ctx.astype(BF16))
        dsink_ref[...] += dsink
        dkv_acc[:LANES, pl.ds(start, 3 * BLK)] += dk
        dkv_acc[LANES:, pl.ds(start, 3 * BLK)] += dv
        dkvc_acc[:LANES, :] += dkc
        dkvc_acc[LANES:, :] += dvc
        c, a, b = cq_ref[...], -saq_ref[...], -sbq_ref[...]
        for s in range(GQA):
            dq_ref[:, s * LANES:(s + 1) * LANES] = _rope(dq[s * BLK:(s + 1) * BLK] * scale, c, a, b).astype(BF16)

        @pl.when(n == nb - 1)
        def _():
            dkv_ref[:, :LANES] = _rope(dkv_acc[:LANES, :].T, c_ref[...], -sa_ref[...], -sb_ref[...]).astype(BF16)
            dkv_ref[:, LANES:] = dkv_acc[LANES:, :].T.astype(BF16)
            dkvc_ref[...] = dkvc_acc[...].T.astype(BF16)

    blk = lambda w: pl.BlockSpec((BLK, w), lambda n: (n, 0))
    return pl.pallas_call(
        body, name=name, grid=(nb,),
        in_specs=[blk(4 * LANES), _full((L, 2 * LANES)), _full((C, 2 * LANES)), pl.BlockSpec(memory_space=pltpu.SMEM),
                  pl.BlockSpec((BLK, 4 * LANES), lambda n: (n, 1)),
                  _full((L, LANES)), _full((L, LANES)), _full((L, LANES)), blk(LANES), blk(LANES), blk(LANES)],
        out_specs=[blk(4 * LANES), _full((L, 2 * LANES)), _full((C, 2 * LANES)), _full((1, LANES))],
        out_shape=[jax.ShapeDtypeStruct((L, 4 * LANES), BF16), jax.ShapeDtypeStruct((L, 2 * LANES), BF16),
                   jax.ShapeDtypeStruct((C, 2 * LANES), BF16), jax.ShapeDtypeStruct((1, LANES), F32)],
        scratch_shapes=[pltpu.VMEM((2 * LANES, L), F32), pltpu.VMEM((2 * LANES, C), F32)],
        compiler_params=_cp(("arbitrary",)),
    )(q, kv, kvc, sink, dpa, cos, sa, sb, cos, sa, sb)


def _gelu_parts(x):
    th = jnp.tanh(SQRT_2_OVER_PI * (x + GELU_C * x * x * x))
    return 0.5 * x * (1.0 + th), th


def _gelu_grad(x, th):
    return 0.5 * (1.0 + th) + 0.5 * x * (1.0 - th * th) * SQRT_2_OVER_PI * (1.0 + 3.0 * GELU_C * x * x)


def _layernorm(v):
    mu = jnp.mean(v, axis=-1, keepdims=True)
    vc = v - mu
    rstd = lax.rsqrt(jnp.mean(vc * vc, axis=-1, keepdims=True) + EPS)
    return vc * rstd, rstd


def sgu_fwd(z1, ln_g, ln_b, ws, bst, *, rows, name):
    L, W2 = z1.shape
    W = W2 // 2
    ng = W // LANES

    def body(z_ref, g_ref, b_ref, ws_ref, bs_ref, o_ref):
        for c in range(rows // BLK):
            at = slice(c * BLK, (c + 1) * BLK)
            z, _ = _gelu_parts(z_ref[at, :].astype(F32))
            xhat, _ = _layernorm(z[:, W:])
            vln = (xhat * g_ref[...] + b_ref[...]).astype(BF16)
            for gi in range(ng):
                cs = slice(gi * LANES, (gi + 1) * LANES)
                s = _dot(ws_ref[gi], vln[:, cs]) + bs_ref[:, gi:gi + 1]
                o_ref[at, cs] = (z[:, cs] * s).astype(BF16)

    vec = _full((1, W))
    return pl.pallas_call(
        body, name=name, grid=(L // rows,),
        in_specs=[pl.BlockSpec((rows, W2), lambda n: (n, 0)), vec, vec, _full((ng, LANES, LANES)), _full((BLK, ng))],
        out_specs=pl.BlockSpec((rows, W), lambda n: (n, 0)),
        out_shape=jax.ShapeDtypeStruct((L, W), BF16),
        compiler_params=_cp(("parallel",)),
    )(z1, ln_g, ln_b, ws, bst)


def sgu_bwd(z1, dus, ln_g, ln_b, ws, wst, bst, *, rows, name):
    L, W2 = z1.shape
    W = W2 // 2
    ng = W // LANES

    def body(z_ref, d_ref, g_ref, b_ref, ws_ref, wst_ref, bs_ref, dz_ref, dws_ref, dbs_ref, dg_ref, db_ref, dv_scr):
        @pl.when(pl.program_id(0) == 0)
        def _():
            dws_ref[...] = jnp.zeros_like(dws_ref)
            dbs_ref[...] = jnp.zeros_like(dbs_ref)
            dg_ref[...] = jnp.zeros_like(dg_ref)
            db_ref[...] = jnp.zeros_like(db_ref)

        for c in range(rows // BLK):
            at = slice(c * BLK, (c + 1) * BLK)
            zp = z_ref[at, :].astype(F32)
            z, th = _gelu_parts(zp)
            xhat, rstd = _layernorm(z[:, W:])
            vln = (xhat * g_ref[...] + b_ref[...]).astype(BF16)
            d = d_ref[at, :].astype(F32)
            lane = lax.broadcasted_iota(jnp.int32, (1, LANES), 1)
            dbs = jnp.zeros((BLK, LANES), F32)
            dgel = _gelu_grad(zp, th)
            for gi in range(ng):
                cs = slice(gi * LANES, (gi + 1) * LANES)
                s = _dot(ws_ref[gi], vln[:, cs]) + bs_ref[:, gi:gi + 1]
                dz_ref[at, cs] = (d[:, cs] * s * dgel[:, cs]).astype(BF16)
                ds = d[:, cs] * z[:, cs]
                dbs = dbs + jnp.where(lane == gi, jnp.sum(ds, axis=-1, keepdims=True), 0.0)
                dsb = ds.astype(BF16)
                dws_ref[gi] += _dot_nt(dsb, vln[:, cs])
                dv_scr[:, cs] = _dot(wst_ref[gi], dsb)
            dbs_ref[...] += dbs
            dvln = dv_scr[...]
            dg_ref[...] += _colsum(dvln * xhat)
            db_ref[...] += _colsum(dvln)
            dxh = dvln * g_ref[...]
            dv = rstd * (dxh - jnp.mean(dxh, axis=-1, keepdims=True) - xhat * jnp.mean(dxh * xhat, axis=-1, keepdims=True))
            dz_ref[at, W:] = (dv * dgel[:, W:]).astype(BF16)

    vec = _full((1, W))
    return pl.pallas_call(
        body, name=name, grid=(L // rows,),
        in_specs=[pl.BlockSpec((rows, W2), lambda n: (n, 0)), pl.BlockSpec((rows, W), lambda n: (n, 0)), vec, vec,
                  _full((ng, LANES, LANES)), _full((ng, LANES, LANES)), _full((BLK, ng))],
        out_specs=[pl.BlockSpec((rows, W2), lambda n: (n, 0)), _full((ng, LANES, LANES)), _full((BLK, LANES)), vec, vec],
        out_shape=[jax.ShapeDtypeStruct((L, W2), BF16), jax.ShapeDtypeStruct((ng, LANES, LANES), F32),
                   jax.ShapeDtypeStruct((BLK, LANES), F32), jax.ShapeDtypeStruct((1, W), F32), jax.ShapeDtypeStruct((1, W), F32)],
        scratch_shapes=[pltpu.VMEM((BLK, W), F32)],
        compiler_params=_cp(("arbitrary",)),
    )(z1, dus, ln_g, ln_b, ws, wst, bst)


def _adamw_math(w, m, v, g):
    m_ = ADAM_B1 * m + (1.0 - ADAM_B1) * g
    v_ = ADAM_B2 * v + (1.0 - ADAM_B2) * (g * g)
    return -ADAM_LR * ((m_ / BC1) / (jnp.sqrt(v_ / BC2) + ADAM_EPS) + ADAM_WD * w), m_, v_


def adamw(w, m, v, gparts, *, tr, name):
    NL, R, Wd = w.shape
    nr = R // tr

    def body(w_ref, m_ref, v_ref, *rest):
        gp_refs, (g_ref, d_ref, nm_ref, nv_ref) = rest[:NL], rest[NL:]
        for l in range(NL):
            @pl.when(pl.program_id(0) == l)
            def _():
                g = gp_refs[l][0].astype(F32)
                for s in range(1, gp_refs[l].shape[0]):
                    g = g + gp_refs[l][s].astype(F32)
                g_ref[...] = g
                d_ref[...], nm_ref[...], nv_ref[...] = _adamw_math(w_ref[...], m_ref[...], v_ref[...], g)

    row = pl.BlockSpec((None, tr, Wd), lambda l, i: (l, i, 0))
    gspecs = [pl.BlockSpec((gparts[l].shape[0], tr, Wd), (lambda l_, i, l=l: (0, jnp.clip(i + (l_ - l) * nr, 0, nr - 1), 0)))
              for l in range(NL)]
    return pl.pallas_call(
        body, name=name, grid=(NL, nr),
        in_specs=[row, row, row] + gspecs, out_specs=[row] * 4, out_shape=[jax.ShapeDtypeStruct((NL, R, Wd), F32)] * 4,
        compiler_params=_cp(("arbitrary", "arbitrary")),
    )(w, m, v, *gparts)


def small_update(gpacks, me, params, loss_row, *, name):
    n = len(params)

    def body(me_ref, gp_ref, *refs):
        ins, outs, gs_ref = refs[:3 * n], refs[3 * n:-1], refs[-1]
        gs_ref[...] = gp_ref[0].astype(F32)
        for dv in range(1, N_DEV):
            gs_ref[...] += gp_ref[dv].astype(F32)
        for p, (w, _, _, off, per_dev) in enumerate(params):
            w_ref, m_ref, v_ref = ins[3 * p:3 * p + 3]
            g_ref, d_ref, nm_ref, nv_ref = outs[4 * p:4 * p + 4]
            rows, cols = w.shape
            if cols == LANES and rows % 8 == 0 and not per_dev:
                g = gs_ref[off:off + rows, :]
                g_ref[...] = g
                d_ref[...], nm_ref[...], nv_ref[...] = _adamw_math(w_ref[...], m_ref[...], v_ref[...], g)
                continue
            chunks = -(-cols // LANES)
            base = off + me_ref[0] * per_dev if per_dev else off
            for i in range(rows):
                for j in range(chunks):
                    wd = min(LANES, cols - j * LANES)
                    at = (slice(i, i + 1), slice(j * LANES, j * LANES + wd))
                    g = gs_ref[pl.ds(base + i * chunks + j, 1), 0:wd]
                    g_ref[at] = g
                    d_ref[at], nm_ref[at], nv_ref[at] = _adamw_math(w_ref[at], m_ref[at], v_ref[at], g)
        outs[-1][...] = jnp.sum(gs_ref[loss_row:loss_row + 1, :], axis=1, keepdims=True)

    flat = [a for w, m, v, _, _ in params for a in (w, m, v)]
    out_shape = [jax.ShapeDtypeStruct(w.shape, F32) for w, _, _, _, _ in params for _ in range(4)] + [jax.ShapeDtypeStruct((1, 1), F32)]
    return pl.pallas_call(
        body, name=name, grid=(1,),
        in_specs=[pl.BlockSpec(memory_space=pltpu.SMEM), _full(gpacks.shape)] + [_full(a.shape) for a in flat],
        out_specs=[_full(o.shape) for o in out_shape], out_shape=out_shape,
        scratch_shapes=[pltpu.VMEM(gpacks.shape[1:], F32)],
        compiler_params=_cp(("arbitrary",)),
    )(me, gpacks, *flat)


def ada_fwd_mm(cs, w_ada, b_loc, *, name):
    R, D = cs.shape
    nl, _, n = w_ada.shape

    def body(c_ref, w_ref, b_ref, s_ref, m_ref):
        c = c_ref[...]
        s = c * jax.nn.sigmoid(c)
        s_ref[...] = s
        for i in range(nl):
            m_ref[i] = _dot(s.astype(BF16), w_ref[i].astype(BF16)) + b_ref[i:i + 1, :]

    return pl.pallas_call(
        body, name=name, in_specs=[_full((R, D)), _full((nl, D, n)), _full((nl, n))],
        out_specs=[_full((R, D)), _full((nl, R, n))], grid=(1,),
        out_shape=[jax.ShapeDtypeStruct((R, D), F32), jax.ShapeDtypeStruct((nl, R, n), F32)],
        compiler_params=_cp(("arbitrary",)),
    )(cs, w_ada, b_loc)


def ada_bwd_mm(s, c_ctx, dall, w_ada, *, name):
    R, D = s.shape
    nl, _, n = w_ada.shape

    def body(s_ref, cc_ref, d_ref, w_ref, gw_ref, dcc_ref):
        sb = s_ref[...].astype(BF16)
        row = lax.broadcasted_iota(jnp.int32, (R, 1), 0)
        dctx = d_ref[0, 1:2, :]
        for dv in range(1, N_DEV):
            dctx = dctx + d_ref[dv, 1:2, :]
        for i in range(nl):
            dm = jnp.zeros((R, n), F32)
            for dv in range(N_DEV):
                dm = dm + jnp.where(row == dv, d_ref[dv, 2 * i:2 * i + 1, :], 0.0)
            if i == 0:
                dm = dm + jnp.where(row == N_DEV, dctx, 0.0)
            gw_ref[i] = _dot_tn(sb, dm.astype(BF16))
        cc = cc_ref[...]
        sg = jax.nn.sigmoid(cc)
        ds = _dot_nt(jnp.broadcast_to(dctx, (8, n)).astype(BF16), w_ref[0].astype(BF16))
        dcc_ref[...] = ds * (sg * (1.0 + cc * (1.0 - sg)))

    return pl.pallas_call(
        body, name=name, grid=(1,),
        in_specs=[_full((R, D)), _full((1, D)), _full((N_DEV, 3, n)), _full((nl, D, n))],
        out_specs=[_full((nl, D, n)), _full((8, D))],
        out_shape=[jax.ShapeDtypeStruct((nl, D, n), F32), jax.ShapeDtypeStruct((8, D), F32)],
        compiler_params=_cp(("arbitrary",)),
    )(s, c_ctx, dall, w_ada)


def _place():
    x, y, c = lax.axis_index("x"), lax.axis_index("y"), lax.axis_index("c")
    return x, y, c


def _lin(p):
    return 4 * p[0] + 2 * p[1] + p[2]


def all_gather_small(xb, *, name):
    R, W = xb.shape

    def body(x_ref, out_ref, send_sems, recv_sems, local_sem):
        x, y, c = _place()
        me = _lin((x, y, c))
        mine = pltpu.make_async_copy(x_ref, out_ref.at[me], local_sem)
        mine.start()
        copies = []
        for k in range(1, N_DEV):
            peer = (x ^ (k >> 2), y ^ ((k >> 1) & 1), c ^ (k & 1))
            mk = lambda dst, k=k, peer=peer: pltpu.make_async_remote_copy(
                src_ref=x_ref, dst_ref=dst, send_sem=send_sems.at[k - 1], recv_sem=recv_sems.at[k - 1], device_id=peer, device_id_type=MESH)
            mk(out_ref.at[me]).start()
            copies.append(mk(out_ref.at[_lin(peer)]))
        for cp in copies:
            cp.wait_recv()
        for cp in copies:
            cp.wait_send()
        mine.wait()

    vm = pl.BlockSpec(memory_space=pltpu.VMEM)
    return pl.pallas_call(
        body, name=name, in_specs=[vm], out_specs=vm, out_shape=jax.ShapeDtypeStruct((N_DEV, R, W), xb.dtype),
        scratch_shapes=[pltpu.SemaphoreType.DMA((7,)), pltpu.SemaphoreType.DMA((7,)), pltpu.SemaphoreType.DMA],
        compiler_params=pltpu.CompilerParams(vmem_limit_bytes=VMEM_LIMIT),
    )(xb)


HBM_SPEC = pl.BlockSpec(memory_space=pltpu.HBM)
SEM_SPEC = pl.BlockSpec(memory_space=pltpu.SEMAPHORE)
ORDERED_EFFECT = pltpu.SideEffectType.DATAFLOW_SIDE_EFFECTING


def _exchange_copies(srcs, lands, sems, scatter):
    x, y, c = _place()
    me = _lin((x, y, c))
    for j in range(len(srcs)):
        r = lands[j].shape[0] // N_DEV
        block = lambda d, j=j, r=r: pl.ds(pl.multiple_of(d * r, 16), r)
        for k in range(1, N_DEV):
            peer = (x ^ (k >> 2), y ^ ((k >> 1) & 1), c ^ (k & 1))
            src = srcs[j].at[block(_lin(peer)), :] if scatter else srcs[j]
            mk = lambda dst, j=j, k=k, peer=peer, src=src: pltpu.make_async_remote_copy(
                src_ref=src, dst_ref=dst, send_sem=sems[2 * j].at[k - 1], recv_sem=sems[2 * j + 1].at[k - 1],
                device_id=peer, device_id_type=MESH)
            yield mk(lands[j].at[block(me), :]), mk(lands[j].at[block(_lin(peer)), :])


def exchange_start(srcs, lands, *, scatter, name):
    nw = len(srcs)

    def body(*refs):
        for start, _ in _exchange_copies(refs[:nw], refs[nw:2 * nw], refs[2 * nw:4 * nw], scatter):
            start.start()
        refs[-1][...] = jnp.zeros_like(refs[-1])

    thru = [pltpu.HBM(a.shape, a.dtype) for a in (*srcs, *lands)]
    res = pl.pallas_call(
        body, name=name, in_specs=[HBM_SPEC] * (2 * nw),
        out_specs=[SEM_SPEC] * (2 * nw) + [HBM_SPEC] * (2 * nw) + [pl.BlockSpec(memory_space=pltpu.VMEM)],
        out_shape=[pltpu.SemaphoreType.DMA((N_DEV - 1,))] * (2 * nw) + thru + [jax.ShapeDtypeStruct((8, LANES), F32)],
        input_output_aliases={i: 2 * nw + i for i in range(2 * nw)},
        compiler_params=pltpu.CompilerParams(has_side_effects=ORDERED_EFFECT),
    )(*[pltpu.with_memory_space_constraint(a, pltpu.HBM) for a in (*srcs, *lands)])
    return res[:2 * nw], res[2 * nw:3 * nw], res[3 * nw:4 * nw], res[-1]


def exchange_wait(srcs, lands, sems, after, *, scatter, name):
    nw = len(srcs)
    after = list(after) if isinstance(after, (list, tuple)) else [after]

    def body(*refs):
        for _, arrive in _exchange_copies(refs[:nw], refs[nw:2 * nw], refs[2 * nw:4 * nw], scatter):
            arrive.wait_send()
            arrive.wait_recv()

    res = pl.pallas_call(
        body, name=name, in_specs=[HBM_SPEC] * (2 * nw) + [SEM_SPEC] * (2 * nw) + [pl.BlockSpec(memory_space=pl.ANY)] * len(after),
        out_specs=[HBM_SPEC] * (2 * nw), out_shape=[pltpu.HBM(a.shape, a.dtype) for a in (*srcs, *lands)],
        input_output_aliases={i: i for i in range(2 * nw)},
        compiler_params=pltpu.CompilerParams(has_side_effects=ORDERED_EFFECT),
    )(*srcs, *lands, *sems, *after)
    return res[nw:]


def place_own(srcs, rows, me, *, scatter, name):
    nw = len(srcs)
    lands = [lax.empty((N_DEV * r, s_.shape[1]), s_.dtype) for r, s_ in zip(rows, srcs)]

    def body(me_ref, *refs):
        for j in range(nw):
            refs[2 * nw + j][...] = refs[j][...]

    mine = lambda i, me_ref: (me_ref[0], 0)
    src_at = mine if scatter else (lambda i, me_ref: (0, 0))
    blocks = [(r, s_.shape[1]) for r, s_ in zip(rows, srcs)]
    return pl.pallas_call(
        body, name=name,
        grid_spec=pltpu.PrefetchScalarGridSpec(
            num_scalar_prefetch=1, grid=(1,),
            in_specs=[pl.BlockSpec(b_, src_at) for b_ in blocks] + [pl.BlockSpec(memory_space=pl.ANY)] * nw,
            out_specs=[pl.BlockSpec(b_, mine) for b_ in blocks]),
        out_shape=[jax.ShapeDtypeStruct(l_.shape, l_.dtype) for l_ in lands],
        input_output_aliases={1 + nw + j: j for j in range(nw)},
        compiler_params=_cp(("arbitrary",)),
    )(jnp.reshape(me, (1,)).astype(jnp.int32), *srcs, *lands)


def _rope_tables(L):
    t = jnp.arange(L)
    inv = ROPE_BASE ** (-jnp.arange(ROPE_FREQS, dtype=F32) / ROPE_FREQS)
    ar = (t // GRID_W).astype(F32)[:, None] * inv
    ac = (t % GRID_W).astype(F32)[:, None] * inv
    z = jnp.zeros_like(ar)
    cos = jnp.concatenate([jnp.cos(ar), jnp.cos(ar), jnp.cos(ac), jnp.cos(ac)], axis=1)
    sa = jnp.concatenate([-jnp.sin(ar), z, -jnp.sin(ac), z], axis=1)
    sb = jnp.concatenate([z, jnp.sin(ar), z, jnp.sin(ac)], axis=1)
    return tuple(jnp.tile(a, (1, LANES // HEAD_DIM)) for a in (cos, sa, sb))


def _nat2d(a):
    return a.reshape(1, -1) if a.ndim == 1 else a.reshape(-1, a.shape[-1])


def _pack_rows(a):
    rows, cols = a.shape
    chunks = -(-cols // LANES)
    f = jnp.pad(a, ((0, 0), (0, chunks * LANES - cols))).reshape(rows * chunks, LANES)
    return jnp.pad(f, ((0, -f.shape[0] % 8), (0, 0)))


def _rows128(a):
    f = a.reshape(-1)
    n = -(-f.shape[0] // (8 * LANES)) * 8 * LANES
    return jnp.pad(f, (0, n - f.shape[0])).reshape(-1, LANES)


def kernel(x, c, ctx, c_ctx, w_ada, b_ada, g_mix_pre, g_mix_post, g_ffn_pre, g_ffn_post, w_in_even, w_pool, pool_scale, attn_sink, w_out_even, w_in_odd, sgu_ln_g, sgu_ln_b, sgu_w, sgu_b, w_out_odd, w_ffn_up, ffn_conv_w, ffn_conv_b, w_ffn_down, loss_target, m_c_ctx, m_w_ada, m_b_ada, m_g_mix_pre, m_g_mix_post, m_g_ffn_pre, m_g_ffn_post, m_w_in_even, m_w_pool, m_pool_scale, m_attn_sink, m_w_out_even, m_w_in_odd, m_sgu_ln_g, m_sgu_ln_b, m_sgu_w, m_sgu_b, m_w_out_odd, m_w_ffn_up, m_ffn_conv_w, m_ffn_conv_b, m_w_ffn_down, v_c_ctx, v_w_ada, v_b_ada, v_g_mix_pre, v_g_mix_post, v_g_ffn_pre, v_g_ffn_post, v_w_in_even, v_w_pool, v_pool_scale, v_attn_sink, v_w_out_even, v_w_in_odd, v_sgu_ln_g, v_sgu_ln_b, v_sgu_w, v_sgu_b, v_w_out_odd, v_w_ffn_up, v_ffn_conv_w, v_ffn_conv_b, v_w_ffn_down):
    P = dict(c_ctx=c_ctx, w_ada=w_ada, b_ada=b_ada, g_mix_pre=g_mix_pre, g_mix_post=g_mix_post, g_ffn_pre=g_ffn_pre,
             g_ffn_post=g_ffn_post, w_in_even=w_in_even, w_pool=w_pool, pool_scale=pool_scale, attn_sink=attn_sink,
             w_out_even=w_out_even, w_in_odd=w_in_odd, sgu_ln_g=sgu_ln_g, sgu_ln_b=sgu_ln_b, sgu_w=sgu_w, sgu_b=sgu_b,
             w_out_odd=w_out_odd, w_ffn_up=w_ffn_up, ffn_conv_w=ffn_conv_w, ffn_conv_b=ffn_conv_b, w_ffn_down=w_ffn_down)
    M = dict(c_ctx=m_c_ctx, w_ada=m_w_ada, b_ada=m_b_ada, g_mix_pre=m_g_mix_pre, g_mix_post=m_g_mix_post, g_ffn_pre=m_g_ffn_pre,
             g_ffn_post=m_g_ffn_post, w_in_even=m_w_in_even, w_pool=m_w_pool, pool_scale=m_pool_scale, attn_sink=m_attn_sink,
             w_out_even=m_w_out_even, w_in_odd=m_w_in_odd, sgu_ln_g=m_sgu_ln_g, sgu_ln_b=m_sgu_ln_b, sgu_w=m_sgu_w, sgu_b=m_sgu_b,
             w_out_odd=m_w_out_odd, w_ffn_up=m_w_ffn_up, ffn_conv_w=m_ffn_conv_w, ffn_conv_b=m_ffn_conv_b, w_ffn_down=m_w_ffn_down)
    V = dict(c_ctx=v_c_ctx, w_ada=v_w_ada, b_ada=v_b_ada, g_mix_pre=v_g_mix_pre, g_mix_post=v_g_mix_post, g_ffn_pre=v_g_ffn_pre,
             g_ffn_post=v_g_ffn_post, w_in_even=v_w_in_even, w_pool=v_w_pool, pool_scale=v_pool_scale, attn_sink=v_attn_sink,
             w_out_even=v_w_out_even, w_in_odd=v_w_in_odd, sgu_ln_g=v_sgu_ln_g, sgu_ln_b=v_sgu_ln_b, sgu_w=v_sgu_w, sgu_b=v_sgu_b,
             w_out_odd=v_w_out_odd, w_ffn_up=v_w_ffn_up, ffn_conv_w=v_ffn_conv_w, ffn_conv_b=v_ffn_conv_b, w_ffn_down=v_w_ffn_down)

    x = x[0]
    ctx = ctx[0]
    target = loss_target[0]
    L, D = x.shape
    C = ctx.shape[0]
    tm = min(512, L)
    conv_rows = min(512, L)
    me = 4 * lax.axis_index("x") + 2 * lax.axis_index("y") + lax.axis_index("c")
    n_ada = w_ada.shape[2]
    F = w_ffn_down.shape[1] * N_DEV
    half_f = F // 2

    n_cw = ffn_conv_w.shape[2]
    small = jnp.concatenate([_rows128(c), _rows128(sgu_ln_g), _rows128(sgu_ln_b), _rows128(ffn_conv_w)], axis=0)
    small_all = all_gather_small(small, name="gather_small_inputs")
    c_all = small_all[:, :8].reshape(N_DEV, D)
    ln_g = small_all[:, 8].reshape(1, D)
    ln_b = small_all[:, 16].reshape(1, D)
    conv_w = small_all[:, 24:].reshape(N_DEV, -1)[:, :2 * 3 * n_cw].reshape(N_DEV, 2, 3, n_cw)
    conv_w = conv_w.transpose(1, 2, 0, 3).reshape(2, 3, 2 * F)

    cs = jnp.concatenate([c_all, c_ctx[None, :], jnp.zeros((7, D), F32)], axis=0)
    b_loc = lax.dynamic_slice(b_ada, (0, me * n_ada), (2, n_ada))
    silu_c, mods_loc = ada_fwd_mm(cs, w_ada, b_loc, name="ada_fwd")
    mods_all = all_gather_small(mods_loc.reshape(-1, LANES), name="gather_mods")

    shards = [s.astype(BF16) for s in (w_in_even[0].T, w_out_even[0], w_ffn_up[0].T, w_ffn_down[0],
                                       w_in_odd[0].T, w_out_odd[0], w_ffn_up[1].T, w_ffn_down[1])]
    shards, mods_all = lax.optimization_barrier((shards, mods_all))
    w_sems, w_srcs, w_lands, _ = exchange_start(shards, place_own(shards, [s.shape[0] for s in shards], me, scatter=False, name="gather_own"),
                                              scatter=False, name="gather_start")

    def weight(j, after):
        return exchange_wait([w_srcs[j]], [w_lands[j]], w_sems[2 * j:2 * j + 2], after, scatter=False, name=f"gather_wait_{j}")[0]

    mods_all = mods_all.reshape(N_DEV, 2, 16, n_ada).transpose(1, 2, 0, 3).reshape(2, 16, 6 * D)
    mod = lambda i, row: [m_[None, :] for m_ in jnp.split(lax.dynamic_index_in_dim(mods_all[i], row, 0, False), 6)]
    sh_m, sc_m, gt_m, sh_f, sc_f, gt_f = zip(mod(0, me), mod(1, me))
    csh_m, csc_m = mod(0, N_DEV)[:2]

    row = lambda a, i: a[i][None, :]

    cos, sa, sb = _rope_tables(L)
    sink = attn_sink[0]
    bst = sgu_b[0].T
    sgu_wb, sgu_wtb = sgu_w[0].astype(BF16), sgu_w[0].swapaxes(1, 2).astype(BF16)
    wup, wdn = [None, None], [None, None]

    def ffn_fwd(i, xin):
        wup[i] = weight(2 + 4 * i, xin)
        h, hu = pre_mm(xin, row(g_ffn_pre, i), sh_f[i], sc_f[i], wup[i], tm=tm, tn=half_f, name=f"ffn_up_{i}")
        a, s1, s2 = conv_fwd(hu, conv_w[i], ffn_conv_b[i][None, :], rows=conv_rows, wblk=2 * LANES, name=f"ffn_conv_{i}")
        wdn[i] = weight(3 + 4 * i, a)
        res = mm_post([a], wdn[i], xin, row(g_ffn_post, i), gt_f[i], tm=tm, target=target if i == 1 else None, name=f"ffn_down_{i}")
        return (h, (hu, s1, s2), a, *res)

    first_mod, cos, sa, sb = lax.optimization_barrier((sh_m[0], cos, sa, sb))
    win_e = permute_heads(weight(0, first_mod))
    h0, u, q, kv = inproj_even(x, row(g_mix_pre, 0), sh_m[0], sc_m[0], win_e, cos, sa, sb, tm=tm, name="in_even")
    hc, kvc = pre_mm(ctx, row(g_mix_pre, 0), csh_m, csc_m, win_e, tm=C, tn=2 * LANES, w_row_off=8 * LANES, name="in_even_ctx")
    pa = [pool_fwd(u, w_pool[0], pool_scale, name="pool_fwd"), attn_fwd(q, kv, kvc, sink, name="attn_fwd")]
    wout_e = permute_heads(weight(1, pa[1]))
    y0, x1 = mm_post(pa, wout_e, x, row(g_mix_post, 0), gt_m[0], tm=tm, name="out_even")
    h1, hu0, a0, f0, x2 = ffn_fwd(0, x1)
    win_o = weight(4, x2)
    h2, z1 = pre_mm(x2, row(g_mix_pre, 1), sh_m[1], sc_m[1], win_o, tm=tm, tn=D, name="in_odd")
    us = sgu_fwd(z1, ln_g, ln_b, sgu_wb, bst, rows=tm, name="sgu_fwd")
    wout_o = weight(5, us)
    y1, x3 = mm_post([us], wout_o, x2, row(g_mix_post, 1), gt_m[1], tm=tm, name="out_odd")
    h3, hu1, a1, f1, dx4, loss_part = ffn_fwd(1, x3)

    g_srcs, g_lands, g_sems = [], [], []

    def scatter(grads, nm):
        own = place_own(grads, [g.shape[0] // N_DEV for g in grads], me, scatter=True, name=nm.replace("start", "own"))
        sems, srcs, lands, tok = exchange_start(grads, own, scatter=True, name=nm)
        g_srcs.extend(srcs)
        g_lands.extend(lands)
        g_sems.extend(sems)
        return tok[0:1, 0:1]

    def ffn_bwd(i, dxo, xin, h, hu, a, f, g_post):
        dyf, da, dg_post, dgt = post_bwd_mm(dxo, f, g_post, gt_f[i], wdn[i], tm=tm, name=f"ffn_down_bwd_{i}")
        dhg, dhu, dcwg, dcwu, dcbg, dcbu = conv_bwd(da, hu[1], hu[2], hu[0], conv_w[i], rows=conv_rows, wblk=2 * LANES,
                                                    name=f"ffn_conv_bwd_{i}")
        dxin, dg_pre, dsh, dsc = mm_pre_bwd([dhg, dhu], wup[i], xin, dxo, row(g_ffn_pre, i), sc_f[i], tm=tm,
                                            name=f"ffn_up_bwd_{i}")
        g_dn = wgrad([a], dyf, tr=2 * LANES, name=f"wgrad_down_{i}")
        g_up = wgrad([dhg, dhu], h, tr=2 * LANES, name=f"wgrad_up_{i}")
        tok = scatter([g_dn, g_up], f"scatter_start_ffn_{i}")
        return dxin, tok, dict(g_ffn_post=dg_post, g_ffn_pre=dg_pre, gt_f=dgt, sh_f=dsh, sc_f=dsc,
                               ffn_conv_w=jnp.concatenate([dcwg, dcwu], axis=1), ffn_conv_b=jnp.concatenate([dcbg, dcbu], axis=1)[0])

    dx3, tok, sf1 = ffn_bwd(1, dx4, x3, h3, hu1, a1, f1, row(g_ffn_post, 1))
    dy1, dus, dg_mpost1, dgt_m1 = post_bwd_mm(dx3, y1, row(g_mix_post, 1) + tok, gt_m[1], wout_o, tm=tm, name="out_odd_bwd")
    dz1, dws, dbs, dlng, dlnb = sgu_bwd(z1, dus, ln_g, ln_b, sgu_wb, sgu_wtb, bst, rows=tm, name="sgu_bwd")
    dx2, dg_mpre1, dsh_m1, dsc_m1 = mm_pre_bwd([dz1], win_o, x2, dx3, row(g_mix_pre, 1), sc_m[1], tm=tm, name="in_odd_bwd")
    tok = scatter([wgrad([us], dy1, tr=2 * LANES, name="wgrad_out_odd"), wgrad([dz1], h2, tr=2 * LANES, name="wgrad_in_odd")],
                  "scatter_start_mix_1")

    dx1, tok, sf0 = ffn_bwd(0, dx2, x1, h1, hu0, a0, f0, row(g_ffn_post, 0) + tok)
    dy0, dpa, dg_mpost0, dgt_m0 = post_bwd_mm(dx1, y0, row(g_mix_post, 0) + tok, gt_m[0], wout_e, tm=tm, name="out_even_bwd")
    tok = scatter([permute_heads(wgrad(pa, dy0, tr=2 * LANES, name="wgrad_out_even"), inverse=True)], "scatter_start_out_0")
    du, dwp, dps = pool_bwd(u, dpa, w_pool[0], pool_scale + tok, name="pool_bwd")
    dq, dkv, dkvc, dsink = attn_bwd(q, kv, kvc, sink, dpa, cos, sa, sb, name="attn_bwd")
    dz0 = jnp.concatenate([du, dq, dkv], axis=1)
    dzc = jnp.concatenate([jnp.zeros((C, 8 * LANES), BF16), dkvc], axis=1)
    tok = scatter([permute_heads(wgrad([dz0], h0, tr=2 * LANES, extra=(dzc, hc), name="wgrad_in_even"), inverse=True)],
                  "scatter_start_in_0")
    grad_x, dg_mpre0, dsh_m0, dsc_m0 = mm_pre_bwd([dz0], win_e, x, dx1, row(g_mix_pre, 0) + tok, sc_m[0], tm=tm,
                                                  name="in_even_bwd")
    _, dg_mpre0c, dcsh, dcsc = mm_pre_bwd([dkvc], win_e, ctx, None, row(g_mix_pre, 0), csc_m, tm=C,
                                          w_row_off=8 * LANES, name="in_even_ctx_bwd")

    out, ran = {}, {}

    def update(name, lands, transposed):
        w_, m_, v_ = (a.transpose(0, 2, 1) if transposed else a for a in (P[name], M[name], V[name]))
        r = w_.shape[1]
        tr = r // 4 if r % 64 == 0 and r > 256 else r
        res = adamw(w_, m_, v_, [l_.reshape(N_DEV, r, l_.shape[1]) for l_ in lands], tr=tr, name=f"adamw_{name}")
        ran[name] = res[0]
        for kind, val in zip(("grad", "delta", "new_m", "new_v"), res):
            out[(kind, name)] = val.transpose(0, 2, 1) if transposed else val

    zero = jnp.zeros((1, D), F32)
    dmod0 = jnp.concatenate([dsh_m0, dsc_m0, dgt_m0, sf0["sh_f"], sf0["sc_f"], sf0["gt_f"]], axis=1)
    dmodc = jnp.concatenate([dcsh, dcsc, zero, zero, zero, zero], axis=1)
    dmod1 = jnp.concatenate([dsh_m1, dsc_m1, dgt_m1, sf1["sh_f"], sf1["sc_f"], sf1["gt_f"]], axis=1)
    dmods = jnp.concatenate([dmod0, dmodc, dmod1], axis=0)
    dm = dmods.reshape(-1, LANES).astype(BF16)
    d_sems, d_srcs, d_lands, d_tok = exchange_start(
        [dm], place_own([dm], [dm.shape[0]], me, scatter=False, name="dmods_own"), scatter=False, name="dmods_start")
    slots = exchange_wait(g_srcs[:6], g_lands[:6], g_sems[:12], d_tok, scatter=True, name="scatter_wait_early")
    early = slots
    update("w_ffn_down", [slots[4], slots[0]], False)
    update("w_in_odd", [slots[3]], True)
    update("w_out_odd", [slots[2]], False)
    updated = lambda names: [ran[k] for k in names]
    dmods_all = exchange_wait(d_srcs, d_lands, d_sems, updated(("w_out_odd",)), scatter=False, name="dmods_wait")[0]
    dall = lax.dynamic_index_in_dim(dmods_all.astype(F32).reshape(N_DEV, 3, N_DEV, n_ada), me, 2, False)
    g_w_ada, dcc = ada_bwd_mm(silu_c, c_ctx[None, :], dall, w_ada, name="ada_bwd")

    rep = dict(
        c_ctx=dcc[0:1],
        b_ada=jnp.concatenate([dmod0 + dmodc, dmod1]),
        g_mix_pre=jnp.concatenate([dg_mpre0 + dg_mpre0c, dg_mpre1]),
        g_mix_post=jnp.concatenate([dg_mpost0, dg_mpost1]),
        g_ffn_pre=jnp.concatenate([sf0["g_ffn_pre"], sf1["g_ffn_pre"]]),
        g_ffn_post=jnp.concatenate([sf0["g_ffn_post"], sf1["g_ffn_post"]]),
        w_pool=_nat2d(dwp), pool_scale=dps, attn_sink=dsink[:, :N_Q_HEADS],
        sgu_w=_nat2d(dws), sgu_b=dbs[:, :sgu_b.shape[1]].T,
        ffn_conv_b=jnp.stack([sf0["ffn_conv_b"], sf1["ffn_conv_b"]]),
    )
    hi = loss_part.astype(BF16).astype(F32)
    mid = (loss_part - hi).astype(BF16).astype(F32)
    loss_piece = jnp.pad(jnp.concatenate([hi, mid, loss_part - hi - mid], axis=1), ((0, 7), (0, LANES - 3)))
    conv_g = jnp.stack([sf0["ffn_conv_w"], sf1["ffn_conv_w"]]).reshape(2 * 3, N_DEV, n_cw).swapaxes(0, 1)
    shard_full = dict(sgu_ln_g=dlng.reshape(N_DEV, LANES), sgu_ln_b=dlnb.reshape(N_DEV, LANES),
                      ffn_conv_w=jnp.concatenate([_pack_rows(conv_g[d]) for d in range(N_DEV)], axis=0))
    small_names = list(rep) + list(shard_full)
    pieces = [_pack_rows(rep[k]) for k in rep] + list(shard_full.values()) + [loss_piece]
    sizes = [p.shape[0] for p in pieces]
    offs = [sum(sizes[:i]) for i in range(len(sizes))]
    pieces.append(jnp.zeros((-sum(sizes) % 16, LANES), F32))
    gpack = jnp.concatenate(pieces, axis=0).astype(BF16)
    own = place_own([gpack], [gpack.shape[0]], me, scatter=False, name="smallgrad_own")
    s_sems, s_srcs, s_lands, small_tok = exchange_start([gpack], own, scatter=False, name="smallgrad_start")

    slots = exchange_wait(g_srcs[6:], g_lands[6:], g_sems[12:], small_tok, scatter=True, name="scatter_wait_late")
    update("w_in_even", [slots[1]], True)
    update("w_out_even", [slots[0]], False)
    update("w_ffn_up", [early[5], early[1]], True)
    res = adamw(w_ada, m_w_ada, v_w_ada, [g_w_ada[l][None] for l in range(w_ada.shape[0])], tr=D // 4, name="adamw_w_ada")
    ran["w_ada"] = res[0]
    for kind, val in zip(("grad", "delta", "new_m", "new_v"), res):
        out[(kind, "w_ada")] = val

    gpacks = exchange_wait(s_srcs, s_lands, s_sems, updated(("w_ada",)), scatter=False,
                           name="smallgrad_wait")[0]
    per_dev = {k: shard_full[k].shape[0] // N_DEV for k in shard_full}
    params = [(_nat2d(P[k]), _nat2d(M[k]), _nat2d(V[k]), offs[i], per_dev.get(k, 0)) for i, k in enumerate(small_names)]
    res = small_update(gpacks.reshape(N_DEV, -1, LANES), jnp.reshape(me, (1,)).astype(jnp.int32), params, offs[-1], name="adamw_small")
    for i, k in enumerate(small_names):
        for kind, val in zip(("grad", "delta", "new_m", "new_v"), res[4 * i:4 * i + 4]):
            out[(kind, k)] = val.reshape(P[k].shape)
    loss = res[-1][0, 0]

    names = list(P)
    final = [loss, grad_x[None]]
    for kind in ("grad", "delta", "new_m", "new_v"):
        for k in names:
            val = out[(kind, k)]
            final.append(val)
    return tuple(final)
```

```python
import functools
import math

import jax
import jax.numpy as jnp
from jax import lax
from jax.experimental import pallas as pl
from jax.experimental.pallas import tpu as pltpu

F32 = jnp.float32
BF16 = jnp.bfloat16
MESH = pl.DeviceIdType.MESH
N_DEV = 8
LANES = 128
VMEM_LIMIT = 48 * 1024 * 1024
EPS = 1e-6
NEG_INF = -1e30
GRID_W = 64
WINDOW = 128
BLK = 128
HEAD_DIM = 64
N_Q_HEADS = 8
N_KV_HEADS = 2
GQA = N_Q_HEADS // N_KV_HEADS
POOL_WINDOWS = (2, 4, 8, 16)
ROPE_BASE = 10000.0
ROPE_FREQS = HEAD_DIM // 4
PAD = 16
ADAM_LR, ADAM_B1, ADAM_B2, ADAM_EPS, ADAM_WD, ADAM_STEP = 0.001, 0.9, 0.999, 1e-08, 0.01, 10
BC1 = 1.0 - ADAM_B1 ** ADAM_STEP
BC2 = 1.0 - ADAM_B2 ** ADAM_STEP
SQRT_2_OVER_PI = math.sqrt(2.0 / math.pi)
GELU_C = 0.044715


def _cp(sem=None):
    return pltpu.CompilerParams(dimension_semantics=sem, vmem_limit_bytes=VMEM_LIMIT)


def _dot(a, b):
    return jnp.dot(a, b, preferred_element_type=F32)


def _dot_nt(a, b):
    return lax.dot_general(a, b, (((1,), (1,)), ((), ())), preferred_element_type=F32)


def _dot_tn(a, b):
    return lax.dot_general(a, b, (((0,), (0,)), ((), ())), preferred_element_type=F32)


def _rms(x):
    r = lax.rsqrt(jnp.mean(x * x, axis=-1, keepdims=True) + EPS)
    return x * r, r


def _rms_bwd(dn, n, r):
    return r * (dn - n * jnp.mean(dn * n, axis=-1, keepdims=True))


def _colsum(a):
    return jnp.sum(a, axis=0, keepdims=True)


def _rope(x, c, sa, sb):
    return x * c + pltpu.roll(x, LANES - ROPE_FREQS, 1) * sa + pltpu.roll(x, ROPE_FREQS, 1) * sb


def _full(shape):
    return pl.BlockSpec(shape, lambda *_: (0,) * len(shape))


def pre_mm(x, g, sh, sc, wt, *, tm, tn, w_row_off=0, name):
    T, D = x.shape
    n_rows = wt.shape[0] - w_row_off

    def body(x_ref, g_ref, sh_ref, sc_ref, w_ref, h_ref, z_ref):
        n, _ = _rms(x_ref[...])
        h = (n * g_ref[...] * (1.0 + sc_ref[...]) + sh_ref[...]).astype(BF16)
        h_ref[...] = h
        for c0 in range(0, n_rows, tn):
            z_ref[:, c0:c0 + tn] = _dot_nt(h, w_ref[c0:c0 + tn, :]).astype(BF16)

    vec = pl.BlockSpec((1, D), lambda i: (0, 0))
    return pl.pallas_call(
        body, name=name, grid=(T // tm,),
        in_specs=[pl.BlockSpec((tm, D), lambda i: (i, 0)), vec, vec, vec,
                  pl.BlockSpec((n_rows, D), lambda i: (w_row_off // n_rows, 0), pipeline_mode=pl.Buffered(1))],
        out_specs=[pl.BlockSpec((tm, D), lambda i: (i, 0)), pl.BlockSpec((tm, n_rows), lambda i: (i, 0))],
        out_shape=[jax.ShapeDtypeStruct((T, D), BF16), jax.ShapeDtypeStruct((T, n_rows), BF16)],
        compiler_params=_cp(("parallel",)),
    )(x, g, sh, sc, wt)


def inproj_even(x, g, sh, sc, wt, cos, sa, sb, *, tm, name):
    T, D = x.shape
    N = wt.shape[0]

    def body(x_ref, g_ref, sh_ref, sc_ref, w_ref, c_ref, sa_ref, sb_ref, h_ref, u_ref, q_ref, kv_ref):
        n, _ = _rms(x_ref[...])
        h = (n * g_ref[...] * (1.0 + sc_ref[...]) + sh_ref[...]).astype(BF16)
        h_ref[...] = h
        z = _dot_nt(h, w_ref[...])
        u_ref[...] = z[:, :4 * LANES]
        c, a, b = c_ref[...], sa_ref[...], sb_ref[...]
        for s in range(4):
            q_ref[:, s * LANES:(s + 1) * LANES] = _rope(z[:, (4 + s) * LANES:(5 + s) * LANES], c, a, b).astype(BF16)
        kv_ref[:, :LANES] = _rope(z[:, 8 * LANES:9 * LANES], c, a, b).astype(BF16)
        kv_ref[:, LANES:] = z[:, 9 * LANES:].astype(BF16)

    vec = pl.BlockSpec((1, D), lambda i: (0, 0))
    row = lambda w: pl.BlockSpec((tm, w), lambda i: (i, 0))
    return pl.pallas_call(
        body, name=name, grid=(T // tm,),
        in_specs=[row(D), vec, vec, vec, _full((N, D)), row(LANES), row(LANES), row(LANES)],
        out_specs=[row(D), row(4 * LANES), row(4 * LANES), row(2 * LANES)],
        out_shape=[jax.ShapeDtypeStruct((T, D), BF16), jax.ShapeDtypeStruct((T, 4 * LANES), F32),
                   jax.ShapeDtypeStruct((T, 4 * LANES), BF16), jax.ShapeDtypeStruct((T, 2 * LANES), BF16)],
        compiler_params=_cp(("parallel",)),
    )(x, g, sh, sc, wt, cos, sa, sb)


def mm_post(a_parts, w, x, g, gt, *, tm, target=None, name):
    T = a_parts[0].shape[0]
    D = w.shape[1]
    npart = len(a_parts)
    offs = [sum(a_.shape[1] for a_ in a_parts[:p]) for p in range(npart + 1)]
    with_loss = target is not None

    def body(*refs):
        a_refs, (w_ref, x_ref, g_ref, gt_ref) = refs[:npart], refs[npart:npart + 4]
        y = _dot(a_refs[0][...], w_ref[offs[0]:offs[1], :])
        for p in range(1, npart):
            y = y + _dot(a_refs[p][...], w_ref[offs[p]:offs[p + 1], :])
        n, _ = _rms(y)
        xn = x_ref[...] + gt_ref[...] * (n * g_ref[...])
        if not with_loss:
            y_ref, xn_ref = refs[npart + 4:]
            y_ref[...] = y.astype(BF16)
            xn_ref[...] = xn
            return
        t_ref, y_ref, d_ref, l_ref = refs[npart + 4:]
        y_ref[...] = y.astype(BF16)

        @pl.when(pl.program_id(0) == 0)
        def _():
            l_ref[...] = jnp.zeros_like(l_ref)

        e = xn - t_ref[...]
        l_ref[...] += 0.5 * jnp.sum(jnp.mean(e * e, axis=-1, keepdims=True), axis=0, keepdims=True)
        d_ref[...] = e * (1.0 / D)

    vec = pl.BlockSpec((1, D), lambda i: (0, 0))
    row = lambda w_: pl.BlockSpec((tm, w_), lambda i: (i, 0))
    in_specs = [row(a_.shape[1]) for a_ in a_parts] + [_full(w.shape), row(D), vec, vec]
    out_specs = [row(D), row(D)]
    out_shape = [jax.ShapeDtypeStruct((T, D), BF16), jax.ShapeDtypeStruct((T, D), F32)]
    if with_loss:
        in_specs.append(row(D))
        out_specs.append(_full((1, 1)))
        out_shape.append(jax.ShapeDtypeStruct((1, 1), F32))
    return pl.pallas_call(
        body, name=name, grid=(T // tm,), in_specs=in_specs, out_specs=out_specs, out_shape=out_shape,
        compiler_params=_cp(("arbitrary",) if with_loss else ("parallel",)),
    )(*a_parts, w, x, g, gt, *((target,) if with_loss else ()))


def post_bwd_mm(dxn, y, g, gt, w, *, tm, name):
    T, D = y.shape
    K = w.shape[0]

    def body(dxn_ref, y_ref, g_ref, gt_ref, w_ref, dy_ref, da_ref, dg_ref, dgt_ref):
        @pl.when(pl.program_id(0) == 0)
        def _():
            dg_ref[...] = jnp.zeros_like(dg_ref)
            dgt_ref[...] = jnp.zeros_like(dgt_ref)

        d = dxn_ref[...]
        n, r = _rms(y_ref[...].astype(F32))
        g_, gt_ = g_ref[...], gt_ref[...]
        dg_ref[...] += _colsum(d * gt_ * n)
        dgt_ref[...] += _colsum(d * g_ * n)
        dy = _rms_bwd(d * (gt_ * g_), n, r).astype(BF16)
        dy_ref[...] = dy
        da_ref[...] = _dot_nt(dy, w_ref[...]).astype(BF16)

    vec = pl.BlockSpec((1, D), lambda i: (0, 0))
    row = lambda w_: pl.BlockSpec((tm, w_), lambda i: (i, 0))
    return pl.pallas_call(
        body, name=name, grid=(T // tm,),
        in_specs=[row(D), row(D), vec, vec, _full((K, D))],
        out_specs=[row(D), row(K), vec, vec],
        out_shape=[jax.ShapeDtypeStruct((T, D), BF16), jax.ShapeDtypeStruct((T, K), BF16),
                   jax.ShapeDtypeStruct((1, D), F32), jax.ShapeDtypeStruct((1, D), F32)],
        compiler_params=_cp(("arbitrary",)),
    )(dxn, y, g, gt, w)


def mm_pre_bwd(dzs, wt, x, dres, g, sc, *, tm, w_row_off=0, name):
    T, N = dzs[0].shape
    D = x.shape[1]
    npart = len(dzs)
    off = w_row_off // N
    has_res = dres is not None

    def body(*refs):
        dz_refs = refs[:npart]
        w_refs = refs[npart:2 * npart]
        rest = refs[2 * npart:]
        x_ref = rest[0]
        dres_ref = rest[1] if has_res else None
        g_ref, sc_ref, dx_ref, dg_ref, dsh_ref, dsc_ref = rest[1 + has_res:]

        @pl.when(pl.program_id(0) == 0)
        def _():
            dg_ref[...] = jnp.zeros_like(dg_ref)
            dsh_ref[...] = jnp.zeros_like(dsh_ref)
            dsc_ref[...] = jnp.zeros_like(dsc_ref)

        dh = _dot(dz_refs[0][...], w_refs[0][...])
        for p in range(1, npart):
            dh = dh + _dot(dz_refs[p][...], w_refs[p][...])
        n, r = _rms(x_ref[...])
        g_, s1 = g_ref[...], 1.0 + sc_ref[...]
        dsh_ref[...] += _colsum(dh)
        dsc_ref[...] += _colsum(dh * n * g_)
        dg_ref[...] += _colsum(dh * s1 * n)
        dxp = _rms_bwd(dh * (g_ * s1), n, r)
        dx_ref[...] = dxp + dres_ref[...] if has_res else dxp

    vec = pl.BlockSpec((1, D), lambda i: (0, 0))
    row = pl.BlockSpec((tm, D), lambda i: (i, 0))
    w_specs = [pl.BlockSpec((N, D), (lambda i, p=p: (off + p, 0)), pipeline_mode=pl.Buffered(1)) for p in range(npart)]
    res_specs, res_args = ([row], (dres,)) if has_res else ([], ())
    return pl.pallas_call(
        body, name=name, grid=(T // tm,),
        in_specs=[pl.BlockSpec((tm, N), lambda i: (i, 0))] * npart + w_specs + [row] + res_specs + [vec, vec],
        out_specs=[row, vec, vec, vec],
        out_shape=[jax.ShapeDtypeStruct((T, D), F32)] + [jax.ShapeDtypeStruct((1, D), F32)] * 3,
        compiler_params=_cp(("arbitrary",)),
    )(*dzs, *([wt] * npart), x, *res_args, g, sc)


def wgrad(a_parts, b, *, tr, extra=None, name):
    T, R = a_parts[0].shape
    D = b.shape[1]
    npart = len(a_parts)
    nr = R // tr

    def body(*refs):
        a_refs, b_ref = refs[:npart], refs[npart]
        g_ref = refs[-1]
        for p in range(npart):
            @pl.when(pl.program_id(0) // nr == p)
            def _():
                acc = _dot_tn(a_refs[p][...], b_ref[...])
                if extra is not None:
                    acc += _dot_tn(refs[npart + 1][...], refs[npart + 2][...])
                g_ref[...] = acc.astype(BF16)

    in_specs = [pl.BlockSpec((T, tr), (lambda r, p=p: (0, jnp.clip(r - p * nr, 0, nr - 1)))) for p in range(npart)]
    in_specs.append(_full((T, D)))
    args = [*a_parts, b]
    if extra is not None:
        a2, b2 = extra
        in_specs += [pl.BlockSpec((a2.shape[0], tr), lambda r: (0, r)), _full(b2.shape)]
        args += [a2, b2]
    return pl.pallas_call(
        body, name=name, grid=(npart * nr,),
        in_specs=in_specs, out_specs=pl.BlockSpec((tr, D), lambda r: (r, 0)),
        out_shape=jax.ShapeDtypeStruct((npart * R, D), BF16),
        compiler_params=_cp(("parallel",)),
    )(*args)


def _conv_ext(ref, r0, rows, total):
    top = ref[pl.ds(pl.multiple_of(jnp.maximum(r0 - PAD, 0), PAD), PAD), :]
    mid = ref[pl.ds(r0, rows), :]
    bot = ref[pl.ds(pl.multiple_of(jnp.minimum(r0 + rows, total - PAD), PAD), PAD), :]
    top = jnp.where(r0 > 0, top, jnp.zeros_like(top))
    bot = jnp.where(r0 + rows < total, bot, jnp.zeros_like(bot))
    return jnp.concatenate([top, mid, bot], axis=0).astype(F32)


def _shift_rows(a, k):
    return pltpu.roll(a, k % a.shape[0], 0)


def _conv3(x, w, b):
    return w[0:1] * _shift_rows(x, 1) + w[1:2] * x + w[2:3] * _shift_rows(x, -1) + b


def _gate_up_specs(rows_, wblk, nb):
    return [pl.BlockSpec((rows_, wblk), lambda j: (0, j)), pl.BlockSpec((rows_, wblk), lambda j: (0, j + nb))]


def conv_fwd(hu, cw, cb, *, rows, wblk, name):
    L, N2 = hu.shape
    nb = N2 // 2 // wblk
    nchunk = L // rows

    def body(hg_ref, hu_ref, wg_ref, wu_ref, bg_ref, bu_ref, a_ref, s1_ref, s2_ref):
        def chunk(ci, carry):
            r0 = pl.multiple_of(ci * rows, rows)
            gate = _conv3(_conv_ext(hg_ref, r0, rows, L), wg_ref[...], bg_ref[...])[PAD:PAD + rows]
            up = _conv3(_conv_ext(hu_ref, r0, rows, L), wu_ref[...], bu_ref[...])[PAD:PAD + rows]
            sg = jax.nn.sigmoid(gate)
            silu = gate * sg
            at = pl.ds(r0, rows)
            a_ref[at, :] = (silu * up).astype(BF16)
            s1_ref[at, :] = silu.astype(BF16)
            s2_ref[at, :] = (up * (sg + silu * (1.0 - sg))).astype(BF16)
            return carry

        lax.fori_loop(0, nchunk, chunk, 0)

    out = pl.BlockSpec((L, wblk), lambda j: (0, j))
    return pl.pallas_call(
        body, name=name, grid=(nb,),
        in_specs=_gate_up_specs(L, wblk, nb) + _gate_up_specs(3, wblk, nb) + _gate_up_specs(1, wblk, nb),
        out_specs=[out] * 3, out_shape=[jax.ShapeDtypeStruct((L, N2 // 2), BF16)] * 3,
        compiler_params=_cp(("parallel",)),
    )(hu, hu, cw, cw, cb, cb)


def conv_bwd(da, s1, s2, hu, cw, *, rows, wblk, name):
    L, N2 = hu.shape
    F = N2 // 2
    nb = F // wblk
    nchunk = L // rows
    mid = slice(PAD, PAD + rows)

    def body(da_ref, s1_ref, s2_ref, hg_ref, hu_ref, wg_ref, wu_ref, dg_ref, du_ref, dwg_ref, dwu_ref, dbg_ref, dbu_ref):
        for ref in (dwg_ref, dwu_ref, dbg_ref, dbu_ref):
            ref[...] = jnp.zeros_like(ref)

        def half_bwd(x_ref, dh, w_ref, dx_ref, dw_ref, db_ref, r0):
            w = w_ref[...]
            nxt, prv = _shift_rows(dh, -1)[mid], _shift_rows(dh, 1)[mid]
            dhm, xm = dh[mid], x_ref[pl.ds(r0, rows), :].astype(F32)
            dx_ref[pl.ds(r0, rows), :] = (w[0:1] * nxt + w[1:2] * dhm + w[2:3] * prv).astype(BF16)
            db_ref[...] += _colsum(dhm)
            dw_ref[0:1, :] += _colsum(nxt * xm)
            dw_ref[1:2, :] += _colsum(dhm * xm)
            dw_ref[2:3, :] += _colsum(prv * xm)

        def chunk(ci, carry):
            r0 = pl.multiple_of(ci * rows, rows)
            d = _conv_ext(da_ref, r0, rows, L)
            half_bwd(hu_ref, d * _conv_ext(s1_ref, r0, rows, L), wu_ref, du_ref, dwu_ref, dbu_ref, r0)
            half_bwd(hg_ref, d * _conv_ext(s2_ref, r0, rows, L), wg_ref, dg_ref, dwg_ref, dbg_ref, r0)
            return carry

        lax.fori_loop(0, nchunk, chunk, 0)

    blk = lambda r: pl.BlockSpec((r, wblk), lambda j: (0, j))
    return pl.pallas_call(
        body, name=name, grid=(nb,),
        in_specs=[blk(L)] * 3 + _gate_up_specs(L, wblk, nb) + _gate_up_specs(3, wblk, nb),
        out_specs=[blk(L), blk(L), blk(3), blk(3), blk(1), blk(1)],
        out_shape=[jax.ShapeDtypeStruct((L, F), BF16)] * 2 + [jax.ShapeDtypeStruct((3, F), F32)] * 2
        + [jax.ShapeDtypeStruct((1, F), F32)] * 2,
        compiler_params=_cp(("parallel",)),
    )(da, s1, s2, hu, hu, cw, cw)


def _window_sums(pad_ref, w, lead):
    a = pad_ref[...]
    k = 1
    while k < w:
        a = a + _shift_rows(a, -k)
        k *= 2
    return _shift_rows(a, lead) if lead else a


def _pool_counts(L, h):
    t = lax.broadcasted_iota(jnp.int32, (L, 1), 0)
    return (jnp.minimum(t + h, L) - jnp.maximum(t - h, 0)).astype(F32)


def _pooled(u_ref, pad_ref, L, w):
    h = w // 2
    pad_ref[pl.ds(PAD, L), :] = u_ref[...]
    win = _window_sums(pad_ref, w, h)[PAD:PAD + L]
    return win / _pool_counts(L, h) - u_ref[...]


def _zero_pad_edges(pad_ref, L):
    z = jnp.zeros((PAD, LANES), F32)
    pad_ref[pl.ds(0, PAD), :] = z
    pad_ref[pl.ds(PAD + L, PAD), :] = z


def pool_fwd(u, w_pool, pool_scale, *, name):
    L = u.shape[0]

    def body(u_ref, w_ref, ps_ref, p_ref, pad_ref):
        _zero_pad_edges(pad_ref, L)
        for gi, win in enumerate(POOL_WINDOWS):
            @pl.when(pl.program_id(0) == gi)
            def _():
                pooled = _pooled(u_ref, pad_ref, L, win)
                p_ref[...] = (_dot(pooled.astype(BF16), w_ref[...].astype(BF16)) * ps_ref[...]).astype(BF16)

    return pl.pallas_call(
        body, name=name, grid=(len(POOL_WINDOWS),),
        in_specs=[pl.BlockSpec((L, LANES), lambda gi: (0, gi)), pl.BlockSpec((None, LANES, LANES), lambda gi: (gi, 0, 0)),
                  pl.BlockSpec((1, LANES), lambda gi: (0, gi))],
        out_specs=pl.BlockSpec((L, LANES), lambda gi: (0, gi)),
        out_shape=jax.ShapeDtypeStruct((L, 4 * LANES), BF16),
        scratch_shapes=[pltpu.VMEM((L + 2 * PAD, LANES), F32)],
        compiler_params=_cp(("parallel",)),
    )(u, w_pool, pool_scale)


def pool_bwd(u, dpa, w_pool, pool_scale, *, name):
    L = u.shape[0]

    def body(u_ref, dp_ref, w_ref, ps_ref, du_ref, dw_ref, dps_ref, pad_ref):
        _zero_pad_edges(pad_ref, L)
        for gi, win in enumerate(POOL_WINDOWS):
            @pl.when(pl.program_id(0) == gi)
            def _():
                h = win // 2
                wb = w_ref[...].astype(BF16)
                pooled = _pooled(u_ref, pad_ref, L, win).astype(BF16)
                dp = dp_ref[...].astype(F32)
                dps_ref[...] = _colsum(dp * _dot(pooled, wb))
                dy = (dp * ps_ref[...]).astype(BF16)
                dw_ref[...] = _dot_tn(pooled, dy)
                dpooled = _dot_nt(dy, wb)
                pad_ref[pl.ds(PAD, L), :] = dpooled / _pool_counts(L, h)
                du_ref[...] = (_window_sums(pad_ref, win, h - 1)[PAD:PAD + L] - dpooled).astype(BF16)

    return pl.pallas_call(
        body, name=name, grid=(len(POOL_WINDOWS),),
        in_specs=[pl.BlockSpec((L, LANES), lambda gi: (0, gi)), pl.BlockSpec((L, LANES), lambda gi: (0, gi)),
                  pl.BlockSpec((None, LANES, LANES), lambda gi: (gi, 0, 0)), pl.BlockSpec((1, LANES), lambda gi: (0, gi))],
        out_specs=[pl.BlockSpec((L, LANES), lambda gi: (0, gi)), pl.BlockSpec((None, LANES, LANES), lambda gi: (gi, 0, 0)),
                   pl.BlockSpec((1, LANES), lambda gi: (0, gi))],
        out_shape=[jax.ShapeDtypeStruct((L, 4 * LANES), BF16), jax.ShapeDtypeStruct((4, LANES, LANES), F32),
                   jax.ShapeDtypeStruct((1, 4 * LANES), F32)],
        scratch_shapes=[pltpu.VMEM((L + 2 * PAD, LANES), F32)],
        compiler_params=_cp(("parallel",)),
    )(u, dpa, w_pool, pool_scale)


def _attn_probs(qk, band_k, ctx_k, sink_ref, kh, mask4):
    s_loc = jnp.where(mask4, _dot_nt(qk, band_k), NEG_INF)
    s_ctx = _dot_nt(qk, ctx_k)
    sk = jnp.concatenate([jnp.full((BLK, 1), sink_ref[kh * GQA + hh], F32) for hh in range(GQA)], axis=0)
    m = jnp.maximum(jnp.maximum(jnp.max(s_loc, axis=-1, keepdims=True), jnp.max(s_ctx, axis=-1, keepdims=True)), sk)
    e_loc, e_ctx, e_s = jnp.exp(s_loc - m), jnp.exp(s_ctx - m), jnp.exp(sk - m)
    inv = 1.0 / (jnp.sum(e_loc, axis=-1, keepdims=True) + jnp.sum(e_ctx, axis=-1, keepdims=True) + e_s)
    return e_loc * inv, e_ctx * inv, e_s * inv


def _attn_block(n, L):
    start = pl.multiple_of(jnp.clip((n - 1) * BLK, 0, L - 3 * BLK), BLK)
    qpos = n * BLK + lax.broadcasted_iota(jnp.int32, (BLK, 3 * BLK), 0)
    kpos = start + lax.broadcasted_iota(jnp.int32, (BLK, 3 * BLK), 1)
    mask = jnp.abs(kpos - qpos) <= WINDOW
    return start, jnp.concatenate([mask] * GQA, axis=0)


def _stack_slabs(ref, rows=slice(None)):
    return jnp.concatenate([ref[rows, s * LANES:(s + 1) * LANES] for s in range(GQA)], axis=0)


def _kv_head_lanes(kh):
    return (lax.broadcasted_iota(jnp.int32, (1, LANES), 1) // HEAD_DIM) == kh


def permute_heads(w, inverse=False):
    lo, hi = 4 * LANES, 8 * LANES
    mid = w[lo:hi].reshape(*((GQA, N_KV_HEADS) if inverse else (N_KV_HEADS, GQA)), HEAD_DIM, w.shape[1])
    return jnp.concatenate([w[:lo], mid.swapaxes(0, 1).reshape(hi - lo, w.shape[1]), w[hi:]], axis=0)


def attn_fwd(q, kv, kvc, sink, *, qb, name):
    L = q.shape[0]
    C = kvc.shape[0]
    scale = HEAD_DIM ** -0.5

    def body(q_ref, kv_ref, kvc_ref, sink_ref, o_ref):
        kvc_ = kvc_ref[...]
        for b in range(qb):
            rows = slice(b * BLK, (b + 1) * BLK)
            start, mask4 = _attn_block(pl.program_id(0) * qb + b, L)
            band = kv_ref[pl.ds(start, 3 * BLK), :]
            qs = _stack_slabs(q_ref, rows) * scale
            o = jnp.zeros((GQA * BLK, LANES), F32)
            for kh in range(N_KV_HEADS):
                grp = _kv_head_lanes(kh)
                qk = jnp.where(grp, qs, jnp.zeros_like(qs))
                p_loc, p_ctx, _ = _attn_probs(qk, band[:, :LANES], kvc_[:, :LANES], sink_ref, kh, mask4)
                o = o + jnp.where(grp, _dot(p_loc.astype(BF16), band[:, LANES:]) + _dot(p_ctx.astype(BF16), kvc_[:, LANES:]), 0.0)
            for s in range(GQA):
                o_ref[rows, s * LANES:(s + 1) * LANES] = o[s * BLK:(s + 1) * BLK].astype(BF16)

    return pl.pallas_call(
        body, name=name, grid=(L // (qb * BLK),),
        in_specs=[pl.BlockSpec((qb * BLK, 4 * LANES), lambda n: (n, 0)), _full((L, 2 * LANES)), _full((C, 2 * LANES)),
                  pl.BlockSpec(memory_space=pltpu.SMEM)],
        out_specs=pl.BlockSpec((qb * BLK, 4 * LANES), lambda n: (n, 0)),
        out_shape=jax.ShapeDtypeStruct((L, 4 * LANES), BF16),
        compiler_params=_cp(("parallel",)),
    )(q, kv, kvc, sink)


def attn_bwd(q, kv, kvc, sink, dpa, cos, sa, sb, *, name):
    L = q.shape[0]
    C = kvc.shape[0]
    nb = L // BLK
    scale = HEAD_DIM ** -0.5

    def body(q_ref, kv_ref, kvc_ref, sink_ref, do_ref, c_ref, sa_ref, sb_ref, cq_ref, saq_ref, sbq_ref,
             dq_ref, dkv_ref, dkvc_ref, dsink_ref, dkv_acc, dkvc_acc):
        n = pl.program_id(0)

        @pl.when(n == 0)
        def _():
            dkv_acc[...] = jnp.zeros_like(dkv_acc)
            dkvc_acc[...] = jnp.zeros_like(dkvc_acc)
            dsink_ref[...] = jnp.zeros_like(dsink_ref)

        start, mask4 = _attn_block(n, L)
        band = kv_ref[pl.ds(start, 3 * BLK), :]
        kvc_ = kvc_ref[...]
        band_k, band_v, ctx_k, ctx_v = band[:, :LANES], band[:, LANES:], kvc_[:, :LANES], kvc_[:, LANES:]
        qs = _stack_slabs(q_ref) * scale
        dos = _stack_slabs(do_ref)
        lane = lax.broadcasted_iota(jnp.int32, (1, LANES), 1)
        dsink = jnp.zeros((1, LANES), F32)
        dq = jnp.zeros((GQA * BLK, LANES), F32)
        dk = jnp.zeros((LANES, 3 * BLK), F32)
        dv = jnp.zeros((LANES, 3 * BLK), F32)
        dkc = jnp.zeros((LANES, C), F32)
        dvc = jnp.zeros((LANES, C), F32)
        for kh in range(N_KV_HEADS):
            grp = _kv_head_lanes(kh)
            qk = jnp.where(grp, qs, jnp.zeros_like(qs))
            dok = jnp.where(grp, dos, jnp.zeros_like(dos))
            p_loc, p_ctx, p_s = _attn_probs(qk, band_k, ctx_k, sink_ref, kh, mask4)
            dp_loc = _dot_nt(dok, band_v)
            dp_ctx = _dot_nt(dok, ctx_v)
            delta = jnp.sum(p_loc * dp_loc, axis=-1, keepdims=True) + jnp.sum(p_ctx * dp_ctx, axis=-1, keepdims=True)
            ds_loc = (p_loc * (dp_loc - delta)).astype(BF16)
            ds_ctx = (p_ctx * (dp_ctx - delta)).astype(BF16)
            dsk = p_s * delta
            for hh in range(GQA):
                dsink = dsink - jnp.where(lane == kh * GQA + hh, jnp.sum(dsk[hh * BLK:(hh + 1) * BLK], axis=0, keepdims=True), 0.0)
            dq = dq + jnp.where(grp, _dot(ds_loc, band_k) + _dot(ds_ctx, ctx_k), 0.0)
            dk = dk + _dot_tn(qk, ds_loc)
            dv = dv + _dot_tn(dok, p_loc.astype(BF16))
            dkc = dkc + _dot_tn(qk, ds_ctx)
            dvc = dvc + _dot_tn(dok, p_ctx.astype(BF16))
        dsink_ref[...] += dsink
        dkv_acc[:LANES, pl.ds(start, 3 * BLK)] += dk
        dkv_acc[LANES:, pl.ds(start, 3 * BLK)] += dv
        dkvc_acc[:LANES, :] += dkc
        dkvc_acc[LANES:, :] += dvc
        c, a, b = cq_ref[...], -saq_ref[...], -sbq_ref[...]
        for s in range(GQA):
            dq_ref[:, s * LANES:(s + 1) * LANES] = _rope(dq[s * BLK:(s + 1) * BLK] * scale, c, a, b).astype(BF16)

        @pl.when(n == nb - 1)
        def _():
            dkv_ref[:, :LANES] = _rope(dkv_acc[:LANES, :].T, c_ref[...], -sa_ref[...], -sb_ref[...]).astype(BF16)
            dkv_ref[:, LANES:] = dkv_acc[LANES:, :].T.astype(BF16)
            dkvc_ref[...] = dkvc_acc[...].T.astype(BF16)

    blk = lambda w: pl.BlockSpec((BLK, w), lambda n: (n, 0))
    return pl.pallas_call(
        body, name=name, grid=(nb,),
        in_specs=[blk(4 * LANES), _full((L, 2 * LANES)), _full((C, 2 * LANES)), pl.BlockSpec(memory_space=pltpu.SMEM),
                  pl.BlockSpec((BLK, 4 * LANES), lambda n: (n, 1)),
                  _full((L, LANES)), _full((L, LANES)), _full((L, LANES)), blk(LANES), blk(LANES), blk(LANES)],
        out_specs=[blk(4 * LANES), _full((L, 2 * LANES)), _full((C, 2 * LANES)), _full((1, LANES))],
        out_shape=[jax.ShapeDtypeStruct((L, 4 * LANES), BF16), jax.ShapeDtypeStruct((L, 2 * LANES), BF16),
                   jax.ShapeDtypeStruct((C, 2 * LANES), BF16), jax.ShapeDtypeStruct((1, LANES), F32)],
        scratch_shapes=[pltpu.VMEM((2 * LANES, L), F32), pltpu.VMEM((2 * LANES, C), F32)],
        compiler_params=_cp(("arbitrary",)),
    )(q, kv, kvc, sink, dpa, cos, sa, sb, cos, sa, sb)


def _gelu_parts(x):
    th = jnp.tanh(SQRT_2_OVER_PI * (x + GELU_C * x * x * x))
    return 0.5 * x * (1.0 + th), th


def _gelu_grad(x, th):
    return 0.5 * (1.0 + th) + 0.5 * x * (1.0 - th * th) * SQRT_2_OVER_PI * (1.0 + 3.0 * GELU_C * x * x)


def _layernorm(v):
    mu = jnp.mean(v, axis=-1, keepdims=True)
    vc = v - mu
    rstd = lax.rsqrt(jnp.mean(vc * vc, axis=-1, keepdims=True) + EPS)
    return vc * rstd, rstd


def sgu_fwd(z1, ln_g, ln_b, ws, bst, *, rows, name):
    L, W2 = z1.shape
    W = W2 // 2
    ng = W // LANES

    def body(z_ref, g_ref, b_ref, ws_ref, bs_ref, o_ref):
        for c in range(rows // BLK):
            at = slice(c * BLK, (c + 1) * BLK)
            z, _ = _gelu_parts(z_ref[at, :].astype(F32))
            xhat, _ = _layernorm(z[:, W:])
            vln = (xhat * g_ref[...] + b_ref[...]).astype(BF16)
            for gi in range(ng):
                cs = slice(gi * LANES, (gi + 1) * LANES)
                s = _dot(ws_ref[gi], vln[:, cs]) + bs_ref[:, gi:gi + 1]
                o_ref[at, cs] = (z[:, cs] * s).astype(BF16)

    vec = _full((1, W))
    return pl.pallas_call(
        body, name=name, grid=(L // rows,),
        in_specs=[pl.BlockSpec((rows, W2), lambda n: (n, 0)), vec, vec, _full((ng, LANES, LANES)), _full((BLK, ng))],
        out_specs=pl.BlockSpec((rows, W), lambda n: (n, 0)),
        out_shape=jax.ShapeDtypeStruct((L, W), BF16),
        compiler_params=_cp(("parallel",)),
    )(z1, ln_g, ln_b, ws, bst)


def sgu_bwd(z1, dus, ln_g, ln_b, ws, wst, bst, *, rows, name):
    L, W2 = z1.shape
    W = W2 // 2
    ng = W // LANES

    def body(z_ref, d_ref, g_ref, b_ref, ws_ref, wst_ref, bs_ref, dz_ref, dws_ref, dbs_ref, dg_ref, db_ref, dv_scr):
        @pl.when(pl.program_id(0) == 0)
        def _():
            dws_ref[...] = jnp.zeros_like(dws_ref)
            dbs_ref[...] = jnp.zeros_like(dbs_ref)
            dg_ref[...] = jnp.zeros_like(dg_ref)
            db_ref[...] = jnp.zeros_like(db_ref)

        for c in range(rows // BLK):
            at = slice(c * BLK, (c + 1) * BLK)
            zp = z_ref[at, :].astype(F32)
            z, th = _gelu_parts(zp)
            xhat, rstd = _layernorm(z[:, W:])
            vln = (xhat * g_ref[...] + b_ref[...]).astype(BF16)
            d = d_ref[at, :].astype(F32)
            lane = lax.broadcasted_iota(jnp.int32, (1, LANES), 1)
            dbs = jnp.zeros((BLK, LANES), F32)
            dgel = _gelu_grad(zp, th)
            for gi in range(ng):
                cs = slice(gi * LANES, (gi + 1) * LANES)
                s = _dot(ws_ref[gi], vln[:, cs]) + bs_ref[:, gi:gi + 1]
                dz_ref[at, cs] = (d[:, cs] * s * dgel[:, cs]).astype(BF16)
                ds = d[:, cs] * z[:, cs]
                dbs = dbs + jnp.where(lane == gi, jnp.sum(ds, axis=-1, keepdims=True), 0.0)
                dsb = ds.astype(BF16)
                dws_ref[gi] += _dot_nt(dsb, vln[:, cs])
                dv_scr[:, cs] = _dot(wst_ref[gi], dsb)
            dbs_ref[...] += dbs
            dvln = dv_scr[...]
            dg_ref[...] += _colsum(dvln * xhat)
            db_ref[...] += _colsum(dvln)
            dxh = dvln * g_ref[...]
            dv = rstd * (dxh - jnp.mean(dxh, axis=-1, keepdims=True) - xhat * jnp.mean(dxh * xhat, axis=-1, keepdims=True))
            dz_ref[at, W:] = (dv * dgel[:, W:]).astype(BF16)

    vec = _full((1, W))
    return pl.pallas_call(
        body, name=name, grid=(L // rows,),
        in_specs=[pl.BlockSpec((rows, W2), lambda n: (n, 0)), pl.BlockSpec((rows, W), lambda n: (n, 0)), vec, vec,
                  _full((ng, LANES, LANES)), _full((ng, LANES, LANES)), _full((BLK, ng))],
        out_specs=[pl.BlockSpec((rows, W2), lambda n: (n, 0)), _full((ng, LANES, LANES)), _full((BLK, LANES)), vec, vec],
        out_shape=[jax.ShapeDtypeStruct((L, W2), BF16), jax.ShapeDtypeStruct((ng, LANES, LANES), F32),
                   jax.ShapeDtypeStruct((BLK, LANES), F32), jax.ShapeDtypeStruct((1, W), F32), jax.ShapeDtypeStruct((1, W), F32)],
        scratch_shapes=[pltpu.VMEM((BLK, W), F32)],
        compiler_params=_cp(("arbitrary",)),
    )(z1, dus, ln_g, ln_b, ws, wst, bst)


def _adamw_math(w, m, v, g):
    m_ = ADAM_B1 * m + (1.0 - ADAM_B1) * g
    v_ = ADAM_B2 * v + (1.0 - ADAM_B2) * (g * g)
    return -ADAM_LR * ((m_ / BC1) / (jnp.sqrt(v_ / BC2) + ADAM_EPS) + ADAM_WD * w), m_, v_


def adamw(w, m, v, gparts, *, tr, name):
    NL, R, Wd = w.shape
    nr = R // tr

    def body(w_ref, m_ref, v_ref, *rest):
        gp_refs, (g_ref, d_ref, nm_ref, nv_ref) = rest[:NL], rest[NL:]
        for l in range(NL):
            @pl.when(pl.program_id(0) == l)
            def _():
                g = gp_refs[l][0].astype(F32)
                for s in range(1, gp_refs[l].shape[0]):
                    g = g + gp_refs[l][s].astype(F32)
                g_ref[...] = g
                d_ref[...], nm_ref[...], nv_ref[...] = _adamw_math(w_ref[...], m_ref[...], v_ref[...], g)

    row = pl.BlockSpec((None, tr, Wd), lambda l, i: (l, i, 0))
    gspecs = [pl.BlockSpec((gparts[l].shape[0], tr, Wd), (lambda l_, i, l=l: (0, jnp.clip(i + (l_ - l) * nr, 0, nr - 1), 0)))
              for l in range(NL)]
    return pl.pallas_call(
        body, name=name, grid=(NL, nr),
        in_specs=[row, row, row] + gspecs, out_specs=[row] * 4, out_shape=[jax.ShapeDtypeStruct((NL, R, Wd), F32)] * 4,
        compiler_params=_cp(("arbitrary", "arbitrary")),
    )(w, m, v, *gparts)


def small_update(gpacks, me, params, loss_row, *, name):
    n = len(params)

    def body(me_ref, gp_ref, *refs):
        ins, outs, gs_ref = refs[:3 * n], refs[3 * n:-1], refs[-1]
        gs_ref[...] = gp_ref[0].astype(F32)
        for dv in range(1, N_DEV):
            gs_ref[...] += gp_ref[dv].astype(F32)
        for p, (w, _, _, off, per_dev) in enumerate(params):
            w_ref, m_ref, v_ref = ins[3 * p:3 * p + 3]
            g_ref, d_ref, nm_ref, nv_ref = outs[4 * p:4 * p + 4]
            rows, cols = w.shape
            if cols == LANES and rows % 8 == 0 and not per_dev:
                g = gs_ref[off:off + rows, :]
                g_ref[...] = g
                d_ref[...], nm_ref[...], nv_ref[...] = _adamw_math(w_ref[...], m_ref[...], v_ref[...], g)
                continue
            chunks = -(-cols // LANES)
            base = off + me_ref[0] * per_dev if per_dev else off
            for i in range(rows):
                for j in range(chunks):
                    wd = min(LANES, cols - j * LANES)
                    at = (slice(i, i + 1), slice(j * LANES, j * LANES + wd))
                    g = gs_ref[pl.ds(base + i * chunks + j, 1), 0:wd]
                    g_ref[at] = g
                    d_ref[at], nm_ref[at], nv_ref[at] = _adamw_math(w_ref[at], m_ref[at], v_ref[at], g)
        outs[-1][...] = jnp.sum(gs_ref[loss_row:loss_row + 1, :], axis=1, keepdims=True)

    flat = [a for w, m, v, _, _ in params for a in (w, m, v)]
    out_shape = [jax.ShapeDtypeStruct(w.shape, F32) for w, _, _, _, _ in params for _ in range(4)] + [jax.ShapeDtypeStruct((1, 1), F32)]
    return pl.pallas_call(
        body, name=name, grid=(1,),
        in_specs=[pl.BlockSpec(memory_space=pltpu.SMEM), _full(gpacks.shape)] + [_full(a.shape) for a in flat],
        out_specs=[_full(o.shape) for o in out_shape], out_shape=out_shape,
        scratch_shapes=[pltpu.VMEM(gpacks.shape[1:], F32)],
        compiler_params=_cp(("arbitrary",)),
    )(me, gpacks, *flat)


def ada_fwd_mm(cs, w_ada, b_loc, *, name):
    R, D = cs.shape
    nl, _, n = w_ada.shape

    def body(c_ref, w_ref, b_ref, s_ref, m_ref):
        c = c_ref[...]
        s = c * jax.nn.sigmoid(c)
        s_ref[...] = s
        for i in range(nl):
            m_ref[i] = _dot(s.astype(BF16), w_ref[i].astype(BF16)) + b_ref[i:i + 1, :]

    return pl.pallas_call(
        body, name=name, in_specs=[_full((R, D)), _full((nl, D, n)), _full((nl, n))],
        out_specs=[_full((R, D)), _full((nl, R, n))], grid=(1,),
        out_shape=[jax.ShapeDtypeStruct((R, D), F32), jax.ShapeDtypeStruct((nl, R, n), F32)],
        compiler_params=_cp(("arbitrary",)),
    )(cs, w_ada, b_loc)


def ada_bwd_mm(s, c_ctx, dall, w_ada, *, name):
    R, D = s.shape
    nl, _, n = w_ada.shape

    def body(s_ref, cc_ref, d_ref, w_ref, gw_ref, dcc_ref):
        sb = s_ref[...].astype(BF16)
        row = lax.broadcasted_iota(jnp.int32, (R, 1), 0)
        dctx = d_ref[0, 1:2, :]
        for dv in range(1, N_DEV):
            dctx = dctx + d_ref[dv, 1:2, :]
        for i in range(nl):
            dm = jnp.zeros((R, n), F32)
            for dv in range(N_DEV):
                dm = dm + jnp.where(row == dv, d_ref[dv, 2 * i:2 * i + 1, :], 0.0)
            if i == 0:
                dm = dm + jnp.where(row == N_DEV, dctx, 0.0)
            gw_ref[i] = _dot_tn(sb, dm.astype(BF16))
        cc = cc_ref[...]
        sg = jax.nn.sigmoid(cc)
        ds = _dot_nt(jnp.broadcast_to(dctx, (8, n)).astype(BF16), w_ref[0].astype(BF16))
        dcc_ref[...] = ds * (sg * (1.0 + cc * (1.0 - sg)))

    return pl.pallas_call(
        body, name=name, grid=(1,),
        in_specs=[_full((R, D)), _full((1, D)), _full((N_DEV, 3, n)), _full((nl, D, n))],
        out_specs=[_full((nl, D, n)), _full((8, D))],
        out_shape=[jax.ShapeDtypeStruct((nl, D, n), F32), jax.ShapeDtypeStruct((8, D), F32)],
        compiler_params=_cp(("arbitrary",)),
    )(s, c_ctx, dall, w_ada)


def _place():
    x, y, c = lax.axis_index("x"), lax.axis_index("y"), lax.axis_index("c")
    return x, y, c


def _lin(p):
    return 4 * p[0] + 2 * p[1] + p[2]


def all_gather_small(xb, *, name):
    R, W = xb.shape

    def body(x_ref, out_ref, send_sems, recv_sems, local_sem):
        x, y, c = _place()
        me = _lin((x, y, c))
        mine = pltpu.make_async_copy(x_ref, out_ref.at[me], local_sem)
        mine.start()
        copies = []
        for k in range(1, N_DEV):
            peer = (x ^ (k >> 2), y ^ ((k >> 1) & 1), c ^ (k & 1))
            mk = lambda dst, k=k, peer=peer: pltpu.make_async_remote_copy(
                src_ref=x_ref, dst_ref=dst, send_sem=send_sems.at[k - 1], recv_sem=recv_sems.at[k - 1], device_id=peer, device_id_type=MESH)
            mk(out_ref.at[me]).start()
            copies.append(mk(out_ref.at[_lin(peer)]))
        for cp in copies:
            cp.wait_recv()
        for cp in copies:
            cp.wait_send()
        mine.wait()

    vm = pl.BlockSpec(memory_space=pltpu.VMEM)
    return pl.pallas_call(
        body, name=name, in_specs=[vm], out_specs=vm, out_shape=jax.ShapeDtypeStruct((N_DEV, R, W), xb.dtype),
        scratch_shapes=[pltpu.SemaphoreType.DMA((7,)), pltpu.SemaphoreType.DMA((7,)), pltpu.SemaphoreType.DMA],
        compiler_params=pltpu.CompilerParams(vmem_limit_bytes=VMEM_LIMIT),
    )(xb)


HBM_SPEC = pl.BlockSpec(memory_space=pltpu.HBM)
SEM_SPEC = pl.BlockSpec(memory_space=pltpu.SEMAPHORE)
ORDERED_EFFECT = pltpu.SideEffectType.DATAFLOW_SIDE_EFFECTING


def _exchange_copies(srcs, lands, sems, scatter):
    x, y, c = _place()
    me = _lin((x, y, c))
    for j in range(len(srcs)):
        r = lands[j].shape[0] // N_DEV
        block = lambda d, j=j, r=r: pl.ds(pl.multiple_of(d * r, 16), r)
        for k in range(1, N_DEV):
            peer = (x ^ (k >> 2), y ^ ((k >> 1) & 1), c ^ (k & 1))
            src = srcs[j].at[block(_lin(peer)), :] if scatter else srcs[j]
            mk = lambda dst, j=j, k=k, peer=peer, src=src: pltpu.make_async_remote_copy(
                src_ref=src, dst_ref=dst, send_sem=sems[2 * j].at[k - 1], recv_sem=sems[2 * j + 1].at[k - 1],
                device_id=peer, device_id_type=MESH)
            yield mk(lands[j].at[block(me), :]), mk(lands[j].at[block(_lin(peer)), :])


def exchange_start(srcs, lands, *, scatter, name):
    nw = len(srcs)

    def body(*refs):
        for start, _ in _exchange_copies(refs[:nw], refs[nw:2 * nw], refs[2 * nw:4 * nw], scatter):
            start.start()
        refs[-1][...] = jnp.zeros_like(refs[-1])

    thru = [pltpu.HBM(a.shape, a.dtype) for a in (*srcs, *lands)]
    res = pl.pallas_call(
        body, name=name, in_specs=[HBM_SPEC] * (2 * nw),
        out_specs=[SEM_SPEC] * (2 * nw) + [HBM_SPEC] * (2 * nw) + [pl.BlockSpec(memory_space=pltpu.VMEM)],
        out_shape=[pltpu.SemaphoreType.DMA((N_DEV - 1,))] * (2 * nw) + thru + [jax.ShapeDtypeStruct((8, LANES), F32)],
        input_output_aliases={i: 2 * nw + i for i in range(2 * nw)},
        compiler_params=pltpu.CompilerParams(has_side_effects=ORDERED_EFFECT),
    )(*[pltpu.with_memory_space_constraint(a, pltpu.HBM) for a in (*srcs, *lands)])
    return res[:2 * nw], res[2 * nw:3 * nw], res[3 * nw:4 * nw], res[-1]


def exchange_wait(srcs, lands, sems, after, *, scatter, name):
    nw = len(srcs)
    after = list(after) if isinstance(after, (list, tuple)) else [after]

    def body(*refs):
        for _, arrive in _exchange_copies(refs[:nw], refs[nw:2 * nw], refs[2 * nw:4 * nw], scatter):
            arrive.wait_send()
            arrive.wait_recv()

    res = pl.pallas_call(
        body, name=name, in_specs=[HBM_SPEC] * (2 * nw) + [SEM_SPEC] * (2 * nw) + [pl.BlockSpec(memory_space=pl.ANY)] * len(after),
        out_specs=[HBM_SPEC] * (2 * nw), out_shape=[pltpu.HBM(a.shape, a.dtype) for a in (*srcs, *lands)],
        input_output_aliases={i: i for i in range(2 * nw)},
        compiler_params=pltpu.CompilerParams(has_side_effects=ORDERED_EFFECT),
    )(*srcs, *lands, *sems, *after)
    return res[nw:]


def place_own(srcs, rows, me, *, scatter, name):
    nw = len(srcs)
    lands = [lax.empty((N_DEV * r, s_.shape[1]), s_.dtype) for r, s_ in zip(rows, srcs)]

    def body(me_ref, *refs):
        for j in range(nw):
            refs[2 * nw + j][...] = refs[j][...]

    mine = lambda i, me_ref: (me_ref[0], 0)
    src_at = mine if scatter else (lambda i, me_ref: (0, 0))
    blocks = [(r, s_.shape[1]) for r, s_ in zip(rows, srcs)]
    return pl.pallas_call(
        body, name=name,
        grid_spec=pltpu.PrefetchScalarGridSpec(
            num_scalar_prefetch=1, grid=(1,),
            in_specs=[pl.BlockSpec(b_, src_at) for b_ in blocks] + [pl.BlockSpec(memory_space=pl.ANY)] * nw,
            out_specs=[pl.BlockSpec(b_, mine) for b_ in blocks]),
        out_shape=[jax.ShapeDtypeStruct(l_.shape, l_.dtype) for l_ in lands],
        input_output_aliases={1 + nw + j: j for j in range(nw)},
        compiler_params=_cp(("arbitrary",)),
    )(jnp.reshape(me, (1,)).astype(jnp.int32), *srcs, *lands)


def _rope_tables(L):
    t = jnp.arange(L)
    inv = ROPE_BASE ** (-jnp.arange(ROPE_FREQS, dtype=F32) / ROPE_FREQS)
    ar = (t // GRID_W).astype(F32)[:, None] * inv
    ac = (t % GRID_W).astype(F32)[:, None] * inv
    z = jnp.zeros_like(ar)
    cos = jnp.concatenate([jnp.cos(ar), jnp.cos(ar), jnp.cos(ac), jnp.cos(ac)], axis=1)
    sa = jnp.concatenate([-jnp.sin(ar), z, -jnp.sin(ac), z], axis=1)
    sb = jnp.concatenate([z, jnp.sin(ar), z, jnp.sin(ac)], axis=1)
    return tuple(jnp.tile(a, (1, LANES // HEAD_DIM)) for a in (cos, sa, sb))


def _nat2d(a):
    return a.reshape(1, -1) if a.ndim == 1 else a.reshape(-1, a.shape[-1])


def _pack_rows(a):
    rows, cols = a.shape
    chunks = -(-cols // LANES)
    f = jnp.pad(a, ((0, 0), (0, chunks * LANES - cols))).reshape(rows * chunks, LANES)
    return jnp.pad(f, ((0, -f.shape[0] % 8), (0, 0)))


def _rows128(a):
    f = a.reshape(-1)
    n = -(-f.shape[0] // (8 * LANES)) * 8 * LANES
    return jnp.pad(f, (0, n - f.shape[0])).reshape(-1, LANES)


def kernel(x, c, ctx, c_ctx, w_ada, b_ada, g_mix_pre, g_mix_post, g_ffn_pre, g_ffn_post, w_in_even, w_pool, pool_scale, attn_sink, w_out_even, w_in_odd, sgu_ln_g, sgu_ln_b, sgu_w, sgu_b, w_out_odd, w_ffn_up, ffn_conv_w, ffn_conv_b, w_ffn_down, loss_target, m_c_ctx, m_w_ada, m_b_ada, m_g_mix_pre, m_g_mix_post, m_g_ffn_pre, m_g_ffn_post, m_w_in_even, m_w_pool, m_pool_scale, m_attn_sink, m_w_out_even, m_w_in_odd, m_sgu_ln_g, m_sgu_ln_b, m_sgu_w, m_sgu_b, m_w_out_odd, m_w_ffn_up, m_ffn_conv_w, m_ffn_conv_b, m_w_ffn_down, v_c_ctx, v_w_ada, v_b_ada, v_g_mix_pre, v_g_mix_post, v_g_ffn_pre, v_g_ffn_post, v_w_in_even, v_w_pool, v_pool_scale, v_attn_sink, v_w_out_even, v_w_in_odd, v_sgu_ln_g, v_sgu_ln_b, v_sgu_w, v_sgu_b, v_w_out_odd, v_w_ffn_up, v_ffn_conv_w, v_ffn_conv_b, v_w_ffn_down):
    P = dict(c_ctx=c_ctx, w_ada=w_ada, b_ada=b_ada, g_mix_pre=g_mix_pre, g_mix_post=g_mix_post, g_ffn_pre=g_ffn_pre,
             g_ffn_post=g_ffn_post, w_in_even=w_in_even, w_pool=w_pool, pool_scale=pool_scale, attn_sink=attn_sink,
             w_out_even=w_out_even, w_in_odd=w_in_odd, sgu_ln_g=sgu_ln_g, sgu_ln_b=sgu_ln_b, sgu_w=sgu_w, sgu_b=sgu_b,
             w_out_odd=w_out_odd, w_ffn_up=w_ffn_up, ffn_conv_w=ffn_conv_w, ffn_conv_b=ffn_conv_b, w_ffn_down=w_ffn_down)
    M = dict(c_ctx=m_c_ctx, w_ada=m_w_ada, b_ada=m_b_ada, g_mix_pre=m_g_mix_pre, g_mix_post=m_g_mix_post, g_ffn_pre=m_g_ffn_pre,
             g_ffn_post=m_g_ffn_post, w_in_even=m_w_in_even, w_pool=m_w_pool, pool_scale=m_pool_scale, attn_sink=m_attn_sink,
             w_out_even=m_w_out_even, w_in_odd=m_w_in_odd, sgu_ln_g=m_sgu_ln_g, sgu_ln_b=m_sgu_ln_b, sgu_w=m_sgu_w, sgu_b=m_sgu_b,
             w_out_odd=m_w_out_odd, w_ffn_up=m_w_ffn_up, ffn_conv_w=m_ffn_conv_w, ffn_conv_b=m_ffn_conv_b, w_ffn_down=m_w_ffn_down)
    V = dict(c_ctx=v_c_ctx, w_ada=v_w_ada, b_ada=v_b_ada, g_mix_pre=v_g_mix_pre, g_mix_post=v_g_mix_post, g_ffn_pre=v_g_ffn_pre,
             g_ffn_post=v_g_ffn_post, w_in_even=v_w_in_even, w_pool=v_w_pool, pool_scale=v_pool_scale, attn_sink=v_attn_sink,
             w_out_even=v_w_out_even, w_in_odd=v_w_in_odd, sgu_ln_g=v_sgu_ln_g, sgu_ln_b=v_sgu_ln_b, sgu_w=v_sgu_w, sgu_b=v_sgu_b,
             w_out_odd=v_w_out_odd, w_ffn_up=v_w_ffn_up, ffn_conv_w=v_ffn_conv_w, ffn_conv_b=v_ffn_conv_b, w_ffn_down=v_w_ffn_down)

    x = x[0]
    ctx = ctx[0]
    target = loss_target[0]
    L, D = x.shape
    C = ctx.shape[0]
    tm = min(512, L)
    conv_rows = min(512, L)
    me = 4 * lax.axis_index("x") + 2 * lax.axis_index("y") + lax.axis_index("c")
    n_ada = w_ada.shape[2]
    F = w_ffn_down.shape[1] * N_DEV
    half_f = F // 2

    n_cw = ffn_conv_w.shape[2]
    small = jnp.concatenate([_rows128(c), _rows128(sgu_ln_g), _rows128(sgu_ln_b), _rows128(ffn_conv_w)], axis=0)
    small_all = all_gather_small(small, name="gather_small_inputs")
    c_all = small_all[:, :8].reshape(N_DEV, D)
    ln_g = small_all[:, 8].reshape(1, D)
    ln_b = small_all[:, 16].reshape(1, D)
    conv_w = small_all[:, 24:].reshape(N_DEV, -1)[:, :2 * 3 * n_cw].reshape(N_DEV, 2, 3, n_cw)
    conv_w = conv_w.transpose(1, 2, 0, 3).reshape(2, 3, 2 * F)

    cs = jnp.concatenate([c_all, c_ctx[None, :], jnp.zeros((7, D), F32)], axis=0)
    b_loc = lax.dynamic_slice(b_ada, (0, me * n_ada), (2, n_ada))
    silu_c, mods_loc = ada_fwd_mm(cs, w_ada, b_loc, name="ada_fwd")
    mods_all = all_gather_small(mods_loc.reshape(-1, LANES), name="gather_mods")

    shards = [s.astype(BF16) for s in (w_in_even[0].T, w_out_even[0], w_ffn_up[0].T, w_ffn_down[0],
                                       w_in_odd[0].T, w_out_odd[0], w_ffn_up[1].T, w_ffn_down[1])]
    shards, mods_all = lax.optimization_barrier((shards, mods_all))
    w_sems, w_srcs, w_lands, _ = exchange_start(shards, place_own(shards, [s.shape[0] for s in shards], me, scatter=False, name="gather_own"),
                                              scatter=False, name="gather_start")

    def weight(j, after):
        return exchange_wait([w_srcs[j]], [w_lands[j]], w_sems[2 * j:2 * j + 2], after, scatter=False, name=f"gather_wait_{j}")[0]

    mods_all = mods_all.reshape(N_DEV, 2, 16, n_ada).transpose(1, 2, 0, 3).reshape(2, 16, 6 * D)
    mod = lambda i, row: [m_[None, :] for m_ in jnp.split(lax.dynamic_index_in_dim(mods_all[i], row, 0, False), 6)]
    sh_m, sc_m, gt_m, sh_f, sc_f, gt_f = zip(mod(0, me), mod(1, me))
    csh_m, csc_m = mod(0, N_DEV)[:2]

    row = lambda a, i: a[i][None, :]

    cos, sa, sb = _rope_tables(L)
    sink = attn_sink[0]
    bst = sgu_b[0].T
    sgu_wb, sgu_wtb = sgu_w[0].astype(BF16), sgu_w[0].swapaxes(1, 2).astype(BF16)
    wup, wdn = [None, None], [None, None]

    def ffn_fwd(i, xin):
        wup[i] = weight(2 + 4 * i, xin)
        h, hu = pre_mm(xin, row(g_ffn_pre, i), sh_f[i], sc_f[i], wup[i], tm=tm, tn=half_f, name=f"ffn_up_{i}")
        a, s1, s2 = conv_fwd(hu, conv_w[i], ffn_conv_b[i][None, :], rows=conv_rows, wblk=2 * LANES, name=f"ffn_conv_{i}")
        wdn[i] = weight(3 + 4 * i, a)
        res = mm_post([a], wdn[i], xin, row(g_ffn_post, i), gt_f[i], tm=tm, target=target if i == 1 else None, name=f"ffn_down_{i}")
        return (h, (hu, s1, s2), a, *res)

    first_mod, cos, sa, sb = lax.optimization_barrier((sh_m[0], cos, sa, sb))
    win_e = permute_heads(weight(0, first_mod))
    h0, u, q, kv = inproj_even(x, row(g_mix_pre, 0), sh_m[0], sc_m[0], win_e, cos, sa, sb, tm=tm, name="in_even")
    hc, kvc = pre_mm(ctx, row(g_mix_pre, 0), csh_m, csc_m, win_e, tm=C, tn=2 * LANES, w_row_off=8 * LANES, name="in_even_ctx")
    pa = [pool_fwd(u, w_pool[0], pool_scale, name="pool_fwd"), attn_fwd(q, kv, kvc, sink, qb=2, name="attn_fwd")]
    wout_e = permute_heads(weight(1, pa[1]))
    y0, x1 = mm_post(pa, wout_e, x, row(g_mix_post, 0), gt_m[0], tm=tm, name="out_even")
    h1, hu0, a0, f0, x2 = ffn_fwd(0, x1)
    win_o = weight(4, x2)
    h2, z1 = pre_mm(x2, row(g_mix_pre, 1), sh_m[1], sc_m[1], win_o, tm=tm, tn=D, name="in_odd")
    us = sgu_fwd(z1, ln_g, ln_b, sgu_wb, bst, rows=tm, name="sgu_fwd")
    wout_o = weight(5, us)
    y1, x3 = mm_post([us], wout_o, x2, row(g_mix_post, 1), gt_m[1], tm=tm, name="out_odd")
    h3, hu1, a1, f1, dx4, loss_part = ffn_fwd(1, x3)

    g_srcs, g_lands, g_sems = [], [], []

    def scatter(grads, nm):
        own = place_own(grads, [g.shape[0] // N_DEV for g in grads], me, scatter=True, name=nm.replace("start", "own"))
        sems, srcs, lands, tok = exchange_start(grads, own, scatter=True, name=nm)
        g_srcs.extend(srcs)
        g_lands.extend(lands)
        g_sems.extend(sems)
        return tok[0:1, 0:1]

    def ffn_bwd(i, dxo, xin, h, hu, a, f, g_post):
        dyf, da, dg_post, dgt = post_bwd_mm(dxo, f, g_post, gt_f[i], wdn[i], tm=tm, name=f"ffn_down_bwd_{i}")
        dhg, dhu, dcwg, dcwu, dcbg, dcbu = conv_bwd(da, hu[1], hu[2], hu[0], conv_w[i], rows=conv_rows, wblk=2 * LANES,
                                                    name=f"ffn_conv_bwd_{i}")
        dxin, dg_pre, dsh, dsc = mm_pre_bwd([dhg, dhu], wup[i], xin, dxo, row(g_ffn_pre, i), sc_f[i], tm=tm,
                                            name=f"ffn_up_bwd_{i}")
        g_dn = wgrad([a], dyf, tr=2 * LANES, name=f"wgrad_down_{i}")
        g_up = wgrad([dhg, dhu], h, tr=2 * LANES, name=f"wgrad_up_{i}")
        tok = scatter([g_dn, g_up], f"scatter_start_ffn_{i}")
        return dxin, tok, dict(g_ffn_post=dg_post, g_ffn_pre=dg_pre, gt_f=dgt, sh_f=dsh, sc_f=dsc,
                               ffn_conv_w=jnp.concatenate([dcwg, dcwu], axis=1), ffn_conv_b=jnp.concatenate([dcbg, dcbu], axis=1)[0])

    dx3, tok, sf1 = ffn_bwd(1, dx4, x3, h3, hu1, a1, f1, row(g_ffn_post, 1))
    dy1, dus, dg_mpost1, dgt_m1 = post_bwd_mm(dx3, y1, row(g_mix_post, 1) + tok, gt_m[1], wout_o, tm=tm, name="out_odd_bwd")
    dz1, dws, dbs, dlng, dlnb = sgu_bwd(z1, dus, ln_g, ln_b, sgu_wb, sgu_wtb, bst, rows=tm, name="sgu_bwd")
    dx2, dg_mpre1, dsh_m1, dsc_m1 = mm_pre_bwd([dz1], win_o, x2, dx3, row(g_mix_pre, 1), sc_m[1], tm=tm, name="in_odd_bwd")
    tok = scatter([wgrad([us], dy1, tr=2 * LANES, name="wgrad_out_odd"), wgrad([dz1], h2, tr=2 * LANES, name="wgrad_in_odd")],
                  "scatter_start_mix_1")

    dx1, tok, sf0 = ffn_bwd(0, dx2, x1, h1, hu0, a0, f0, row(g_ffn_post, 0) + tok)
    dy0, dpa, dg_mpost0, dgt_m0 = post_bwd_mm(dx1, y0, row(g_mix_post, 0) + tok, gt_m[0], wout_e, tm=tm, name="out_even_bwd")
    tok = scatter([permute_heads(wgrad(pa, dy0, tr=2 * LANES, name="wgrad_out_even"), inverse=True)], "scatter_start_out_0")
    du, dwp, dps = pool_bwd(u, dpa, w_pool[0], pool_scale + tok, name="pool_bwd")
    dq, dkv, dkvc, dsink = attn_bwd(q, kv, kvc, sink, dpa, cos, sa, sb, name="attn_bwd")
    dz0 = jnp.concatenate([du, dq, dkv], axis=1)
    dzc = jnp.concatenate([jnp.zeros((C, 8 * LANES), BF16), dkvc], axis=1)
    tok = scatter([permute_heads(wgrad([dz0], h0, tr=2 * LANES, extra=(dzc, hc), name="wgrad_in_even"), inverse=True)],
                  "scatter_start_in_0")
    grad_x, dg_mpre0, dsh_m0, dsc_m0 = mm_pre_bwd([dz0], win_e, x, dx1, row(g_mix_pre, 0) + tok, sc_m[0], tm=tm,
                                                  name="in_even_bwd")
    _, dg_mpre0c, dcsh, dcsc = mm_pre_bwd([dkvc], win_e, ctx, None, row(g_mix_pre, 0), csc_m, tm=C,
                                          w_row_off=8 * LANES, name="in_even_ctx_bwd")

    out, ran = {}, {}

    def update(name, lands, transposed):
        w_, m_, v_ = (a.transpose(0, 2, 1) if transposed else a for a in (P[name], M[name], V[name]))
        r = w_.shape[1]
        tr = r // 4 if r % 64 == 0 and r > 256 else r
        res = adamw(w_, m_, v_, [l_.reshape(N_DEV, r, l_.shape[1]) for l_ in lands], tr=tr, name=f"adamw_{name}")
        ran[name] = res[0]
        for kind, val in zip(("grad", "delta", "new_m", "new_v"), res):
            out[(kind, name)] = val.transpose(0, 2, 1) if transposed else val

    zero = jnp.zeros((1, D), F32)
    dmod0 = jnp.concatenate([dsh_m0, dsc_m0, dgt_m0, sf0["sh_f"], sf0["sc_f"], sf0["gt_f"]], axis=1)
    dmodc = jnp.concatenate([dcsh, dcsc, zero, zero, zero, zero], axis=1)
    dmod1 = jnp.concatenate([dsh_m1, dsc_m1, dgt_m1, sf1["sh_f"], sf1["sc_f"], sf1["gt_f"]], axis=1)
    dmods = jnp.concatenate([dmod0, dmodc, dmod1], axis=0)
    dm = dmods.reshape(-1, LANES).astype(BF16)
    d_sems, d_srcs, d_lands, d_tok = exchange_start(
        [dm], place_own([dm], [dm.shape[0]], me, scatter=False, name="dmods_own"), scatter=False, name="dmods_start")
    slots = exchange_wait(g_srcs[:6], g_lands[:6], g_sems[:12], d_tok, scatter=True, name="scatter_wait_early")
    early = slots
    update("w_ffn_down", [slots[4], slots[0]], False)
    update("w_in_odd", [slots[3]], True)
    update("w_out_odd", [slots[2]], False)
    updated = lambda names: [ran[k] for k in names]
    dmods_all = exchange_wait(d_srcs, d_lands, d_sems, updated(("w_out_odd",)), scatter=False, name="dmods_wait")[0]
    dall = lax.dynamic_index_in_dim(dmods_all.astype(F32).reshape(N_DEV, 3, N_DEV, n_ada), me, 2, False)
    g_w_ada, dcc = ada_bwd_mm(silu_c, c_ctx[None, :], dall, w_ada, name="ada_bwd")

    rep = dict(
        c_ctx=dcc[0:1],
        b_ada=jnp.concatenate([dmod0 + dmodc, dmod1]),
        g_mix_pre=jnp.concatenate([dg_mpre0 + dg_mpre0c, dg_mpre1]),
        g_mix_post=jnp.concatenate([dg_mpost0, dg_mpost1]),
        g_ffn_pre=jnp.concatenate([sf0["g_ffn_pre"], sf1["g_ffn_pre"]]),
        g_ffn_post=jnp.concatenate([sf0["g_ffn_post"], sf1["g_ffn_post"]]),
        w_pool=_nat2d(dwp), pool_scale=dps, attn_sink=dsink[:, :N_Q_HEADS],
        sgu_w=_nat2d(dws), sgu_b=dbs[:, :sgu_b.shape[1]].T,
        ffn_conv_b=jnp.stack([sf0["ffn_conv_b"], sf1["ffn_conv_b"]]),
    )
    hi = loss_part.astype(BF16).astype(F32)
    mid = (loss_part - hi).astype(BF16).astype(F32)
    loss_piece = jnp.pad(jnp.concatenate([hi, mid, loss_part - hi - mid], axis=1), ((0, 7), (0, LANES - 3)))
    conv_g = jnp.stack([sf0["ffn_conv_w"], sf1["ffn_conv_w"]]).reshape(2 * 3, N_DEV, n_cw).swapaxes(0, 1)
    shard_full = dict(sgu_ln_g=dlng.reshape(N_DEV, LANES), sgu_ln_b=dlnb.reshape(N_DEV, LANES),
                      ffn_conv_w=jnp.concatenate([_pack_rows(conv_g[d]) for d in range(N_DEV)], axis=0))
    small_names = list(rep) + list(shard_full)
    pieces = [_pack_rows(rep[k]) for k in rep] + list(shard_full.values()) + [loss_piece]
    sizes = [p.shape[0] for p in pieces]
    offs = [sum(sizes[:i]) for i in range(len(sizes))]
    pieces.append(jnp.zeros((-sum(sizes) % 16, LANES), F32))
    gpack = jnp.concatenate(pieces, axis=0).astype(BF16)
    own = place_own([gpack], [gpack.shape[0]], me, scatter=False, name="smallgrad_own")
    s_sems, s_srcs, s_lands, small_tok = exchange_start([gpack], own, scatter=False, name="smallgrad_start")

    slots = exchange_wait(g_srcs[6:], g_lands[6:], g_sems[12:], small_tok, scatter=True, name="scatter_wait_late")
    update("w_in_even", [slots[1]], True)
    update("w_out_even", [slots[0]], False)
    update("w_ffn_up", [early[5], early[1]], True)
    res = adamw(w_ada, m_w_ada, v_w_ada, [g_w_ada[l][None] for l in range(w_ada.shape[0])], tr=D // 4, name="adamw_w_ada")
    ran["w_ada"] = res[0]
    for kind, val in zip(("grad", "delta", "new_m", "new_v"), res):
        out[(kind, "w_ada")] = val

    gpacks = exchange_wait(s_srcs, s_lands, s_sems, updated(("w_ada",)), scatter=False,
                           name="smallgrad_wait")[0]
    per_dev = {k: shard_full[k].shape[0] // N_DEV for k in shard_full}
    params = [(_nat2d(P[k]), _nat2d(M[k]), _nat2d(V[k]), offs[i], per_dev.get(k, 0)) for i, k in enumerate(small_names)]
    res = small_update(gpacks.reshape(N_DEV, -1, LANES), jnp.reshape(me, (1,)).astype(jnp.int32), params, offs[-1], name="adamw_small")
    for i, k in enumerate(small_names):
        for kind, val in zip(("grad", "delta", "new_m", "new_v"), res[4 * i:4 * i + 4]):
            out[(kind, k)] = val.reshape(P[k].shape)
    loss = res[-1][0, 0]

    names = list(P)
    final = [loss, grad_x[None]]
    for kind in ("grad", "delta", "new_m", "new_v"):
        for k in names:
            val = out[(kind, k)]
            final.append(val)
    return tuple(final)
```

```python
import functools
import math

import jax
import jax.numpy as jnp
from jax import lax
from jax.experimental import pallas as pl
from jax.experimental.pallas import tpu as pltpu

F32 = jnp.float32
BF16 = jnp.bfloat16
MESH = pl.DeviceIdType.MESH
N_DEV = 8
LANES = 128
VMEM_LIMIT = 48 * 1024 * 1024
EPS = 1e-6
NEG_INF = -1e30
GRID_W = 64
WINDOW = 128
BLK = 128
HEAD_DIM = 64
N_Q_HEADS = 8
N_KV_HEADS = 2
GQA = N_Q_HEADS // N_KV_HEADS
POOL_WINDOWS = (2, 4, 8, 16)
ROPE_BASE = 10000.0
ROPE_FREQS = HEAD_DIM // 4
PAD = 16
ADAM_LR, ADAM_B1, ADAM_B2, ADAM_EPS, ADAM_WD, ADAM_STEP = 0.001, 0.9, 0.999, 1e-08, 0.01, 10
BC1 = 1.0 - ADAM_B1 ** ADAM_STEP
BC2 = 1.0 - ADAM_B2 ** ADAM_STEP
SQRT_2_OVER_PI = math.sqrt(2.0 / math.pi)
GELU_C = 0.044715


def _cp(sem=None):
    return pltpu.CompilerParams(dimension_semantics=sem, vmem_limit_bytes=VMEM_LIMIT)


def _dot(a, b):
    return jnp.dot(a, b, preferred_element_type=F32)


def _dot_nt(a, b):
    return lax.dot_general(a, b, (((1,), (1,)), ((), ())), preferred_element_type=F32)


def _dot_tn(a, b):
    return lax.dot_general(a, b, (((0,), (0,)), ((), ())), preferred_element_type=F32)


def _rms(x):
    r = lax.rsqrt(jnp.mean(x * x, axis=-1, keepdims=True) + EPS)
    return x * r, r


def _rms_bwd(dn, n, r):
    return r * (dn - n * jnp.mean(dn * n, axis=-1, keepdims=True))


def _colsum(a):
    return jnp.sum(a, axis=0, keepdims=True)


def _rope(x, c, sa, sb):
    return x * c + pltpu.roll(x, LANES - ROPE_FREQS, 1) * sa + pltpu.roll(x, ROPE_FREQS, 1) * sb


def _full(shape):
    return pl.BlockSpec(shape, lambda *_: (0,) * len(shape))


def pre_mm(x, g, sh, sc, wt, *, tm, tn, w_row_off=0, name):
    T, D = x.shape
    n_rows = wt.shape[0] - w_row_off

    def body(x_ref, g_ref, sh_ref, sc_ref, w_ref, h_ref, z_ref):
        n, _ = _rms(x_ref[...])
        h = (n * g_ref[...] * (1.0 + sc_ref[...]) + sh_ref[...]).astype(BF16)
        h_ref[...] = h
        for c0 in range(0, n_rows, tn):
            z_ref[:, c0:c0 + tn] = _dot_nt(h, w_ref[c0:c0 + tn, :]).astype(BF16)

    vec = pl.BlockSpec((1, D), lambda i: (0, 0))
    return pl.pallas_call(
        body, name=name, grid=(T // tm,),
        in_specs=[pl.BlockSpec((tm, D), lambda i: (i, 0)), vec, vec, vec,
                  pl.BlockSpec((n_rows, D), lambda i: (w_row_off // n_rows, 0), pipeline_mode=pl.Buffered(1))],
        out_specs=[pl.BlockSpec((tm, D), lambda i: (i, 0)), pl.BlockSpec((tm, n_rows), lambda i: (i, 0))],
        out_shape=[jax.ShapeDtypeStruct((T, D), BF16), jax.ShapeDtypeStruct((T, n_rows), BF16)],
        compiler_params=_cp(("parallel",)),
    )(x, g, sh, sc, wt)


def inproj_even(x, g, sh, sc, wt, cos, sa, sb, *, tm, name):
    T, D = x.shape
    N = wt.shape[0]

    def body(x_ref, g_ref, sh_ref, sc_ref, w_ref, c_ref, sa_ref, sb_ref, h_ref, u_ref, q_ref, kv_ref):
        n, _ = _rms(x_ref[...])
        h = (n * g_ref[...] * (1.0 + sc_ref[...]) + sh_ref[...]).astype(BF16)
        h_ref[...] = h
        z = _dot_nt(h, w_ref[...])
        u_ref[...] = z[:, :4 * LANES]
        c, a, b = c_ref[...], sa_ref[...], sb_ref[...]
        for s in range(4):
            q_ref[:, s * LANES:(s + 1) * LANES] = _rope(z[:, (4 + s) * LANES:(5 + s) * LANES], c, a, b).astype(BF16)
        kv_ref[:, :LANES] = _rope(z[:, 8 * LANES:9 * LANES], c, a, b).astype(BF16)
        kv_ref[:, LANES:] = z[:, 9 * LANES:].astype(BF16)

    vec = pl.BlockSpec((1, D), lambda i: (0, 0))
    row = lambda w: pl.BlockSpec((tm, w), lambda i: (i, 0))
    return pl.pallas_call(
        body, name=name, grid=(T // tm,),
        in_specs=[row(D), vec, vec, vec, _full((N, D)), row(LANES), row(LANES), row(LANES)],
        out_specs=[row(D), row(4 * LANES), row(4 * LANES), row(2 * LANES)],
        out_shape=[jax.ShapeDtypeStruct((T, D), BF16), jax.ShapeDtypeStruct((T, 4 * LANES), F32),
                   jax.ShapeDtypeStruct((T, 4 * LANES), BF16), jax.ShapeDtypeStruct((T, 2 * LANES), BF16)],
        compiler_params=_cp(("parallel",)),
    )(x, g, sh, sc, wt, cos, sa, sb)


def mm_post(a_parts, w, x, g, gt, *, tm, target=None, name):
    T = a_parts[0].shape[0]
    D = w.shape[1]
    npart = len(a_parts)
    offs = [sum(a_.shape[1] for a_ in a_parts[:p]) for p in range(npart + 1)]
    with_loss = target is not None

    def body(*refs):
        a_refs, (w_ref, x_ref, g_ref, gt_ref) = refs[:npart], refs[npart:npart + 4]
        y = _dot(a_refs[0][...], w_ref[offs[0]:offs[1], :])
        for p in range(1, npart):
            y = y + _dot(a_refs[p][...], w_ref[offs[p]:offs[p + 1], :])
        n, _ = _rms(y)
        xn = x_ref[...] + gt_ref[...] * (n * g_ref[...])
        if not with_loss:
            y_ref, xn_ref = refs[npart + 4:]
            y_ref[...] = y.astype(BF16)
            xn_ref[...] = xn
            return
        t_ref, y_ref, d_ref, l_ref = refs[npart + 4:]
        y_ref[...] = y.astype(BF16)

        @pl.when(pl.program_id(0) == 0)
        def _():
            l_ref[...] = jnp.zeros_like(l_ref)

        e = xn - t_ref[...]
        l_ref[...] += 0.5 * jnp.sum(jnp.mean(e * e, axis=-1, keepdims=True), axis=0, keepdims=True)
        d_ref[...] = e * (1.0 / D)

    vec = pl.BlockSpec((1, D), lambda i: (0, 0))
    row = lambda w_: pl.BlockSpec((tm, w_), lambda i: (i, 0))
    in_specs = [row(a_.shape[1]) for a_ in a_parts] + [_full(w.shape), row(D), vec, vec]
    out_specs = [row(D), row(D)]
    out_shape = [jax.ShapeDtypeStruct((T, D), BF16), jax.ShapeDtypeStruct((T, D), F32)]
    if with_loss:
        in_specs.append(row(D))
        out_specs.append(_full((1, 1)))
        out_shape.append(jax.ShapeDtypeStruct((1, 1), F32))
    return pl.pallas_call(
        body, name=name, grid=(T // tm,), in_specs=in_specs, out_specs=out_specs, out_shape=out_shape,
        compiler_params=_cp(("arbitrary",) if with_loss else ("parallel",)),
    )(*a_parts, w, x, g, gt, *((target,) if with_loss else ()))


def post_bwd_mm(dxn, y, g, gt, w, *, tm, name):
    T, D = y.shape
    K = w.shape[0]

    def body(dxn_ref, y_ref, g_ref, gt_ref, w_ref, dy_ref, da_ref, dg_ref, dgt_ref):
        @pl.when(pl.program_id(0) == 0)
        def _():
            dg_ref[...] = jnp.zeros_like(dg_ref)
            dgt_ref[...] = jnp.zeros_like(dgt_ref)

        d = dxn_ref[...]
        n, r = _rms(y_ref[...].astype(F32))
        g_, gt_ = g_ref[...], gt_ref[...]
        dg_ref[...] += _colsum(d * gt_ * n)
        dgt_ref[...] += _colsum(d * g_ * n)
        dy = _rms_bwd(d * (gt_ * g_), n, r).astype(BF16)
        dy_ref[...] = dy
        da_ref[...] = _dot_nt(dy, w_ref[...]).astype(BF16)

    vec = pl.BlockSpec((1, D), lambda i: (0, 0))
    row = lambda w_: pl.BlockSpec((tm, w_), lambda i: (i, 0))
    return pl.pallas_call(
        body, name=name, grid=(T // tm,),
        in_specs=[row(D), row(D), vec, vec, _full((K, D))],
        out_specs=[row(D), row(K), vec, vec],
        out_shape=[jax.ShapeDtypeStruct((T, D), BF16), jax.ShapeDtypeStruct((T, K), BF16),
                   jax.ShapeDtypeStruct((1, D), F32), jax.ShapeDtypeStruct((1, D), F32)],
        compiler_params=_cp(("arbitrary",)),
    )(dxn, y, g, gt, w)


def mm_pre_bwd(dzs, wt, x, dres, g, sc, *, tm, w_row_off=0, name):
    T, N = dzs[0].shape
    D = x.shape[1]
    npart = len(dzs)
    off = w_row_off // N
    has_res = dres is not None

    def body(*refs):
        dz_refs = refs[:npart]
        w_refs = refs[npart:2 * npart]
        rest = refs[2 * npart:]
        x_ref = rest[0]
        dres_ref = rest[1] if has_res else None
        g_ref, sc_ref, dx_ref, dg_ref, dsh_ref, dsc_ref = rest[1 + has_res:]

        @pl.when(pl.program_id(0) == 0)
        def _():
            dg_ref[...] = jnp.zeros_like(dg_ref)
            dsh_ref[...] = jnp.zeros_like(dsh_ref)
            dsc_ref[...] = jnp.zeros_like(dsc_ref)

        dh = _dot(dz_refs[0][...], w_refs[0][...])
        for p in range(1, npart):
            dh = dh + _dot(dz_refs[p][...], w_refs[p][...])
        n, r = _rms(x_ref[...])
        g_, s1 = g_ref[...], 1.0 + sc_ref[...]
        dsh_ref[...] += _colsum(dh)
        dsc_ref[...] += _colsum(dh * n * g_)
        dg_ref[...] += _colsum(dh * s1 * n)
        dxp = _rms_bwd(dh * (g_ * s1), n, r)
        dx_ref[...] = dxp + dres_ref[...] if has_res else dxp

    vec = pl.BlockSpec((1, D), lambda i: (0, 0))
    row = pl.BlockSpec((tm, D), lambda i: (i, 0))
    w_specs = [pl.BlockSpec((N, D), (lambda i, p=p: (off + p, 0)), pipeline_mode=pl.Buffered(1)) for p in range(npart)]
    res_specs, res_args = ([row], (dres,)) if has_res else ([], ())
    return pl.pallas_call(
        body, name=name, grid=(T // tm,),
        in_specs=[pl.BlockSpec((tm, N), lambda i: (i, 0))] * npart + w_specs + [row] + res_specs + [vec, vec],
        out_specs=[row, vec, vec, vec],
        out_shape=[jax.ShapeDtypeStruct((T, D), F32)] + [jax.ShapeDtypeStruct((1, D), F32)] * 3,
        compiler_params=_cp(("arbitrary",)),
    )(*dzs, *([wt] * npart), x, *res_args, g, sc)


def wgrad(a_parts, b, *, tr, extra=None, name):
    T, R = a_parts[0].shape
    D = b.shape[1]
    npart = len(a_parts)
    nr = R // tr

    def body(*refs):
        a_refs, b_ref = refs[:npart], refs[npart]
        g_ref = refs[-1]
        for p in range(npart):
            @pl.when(pl.program_id(0) // nr == p)
            def _():
                acc = _dot_tn(a_refs[p][...], b_ref[...])
                if extra is not None:
                    acc += _dot_tn(refs[npart + 1][...], refs[npart + 2][...])
                g_ref[...] = acc.astype(BF16)

    in_specs = [pl.BlockSpec((T, tr), (lambda r, p=p: (0, jnp.clip(r - p * nr, 0, nr - 1)))) for p in range(npart)]
    in_specs.append(_full((T, D)))
    args = [*a_parts, b]
    if extra is not None:
        a2, b2 = extra
        in_specs += [pl.BlockSpec((a2.shape[0], tr), lambda r: (0, r)), _full(b2.shape)]
        args += [a2, b2]
    return pl.pallas_call(
        body, name=name, grid=(npart * nr,),
        in_specs=in_specs, out_specs=pl.BlockSpec((tr, D), lambda r: (r, 0)),
        out_shape=jax.ShapeDtypeStruct((npart * R, D), BF16),
        compiler_params=_cp(("parallel",)),
    )(*args)


def _conv_ext(ref, r0, rows, total):
    top = ref[pl.ds(pl.multiple_of(jnp.maximum(r0 - PAD, 0), PAD), PAD), :]
    mid = ref[pl.ds(r0, rows), :]
    bot = ref[pl.ds(pl.multiple_of(jnp.minimum(r0 + rows, total - PAD), PAD), PAD), :]
    top = jnp.where(r0 > 0, top, jnp.zeros_like(top))
    bot = jnp.where(r0 + rows < total, bot, jnp.zeros_like(bot))
    return jnp.concatenate([top, mid, bot], axis=0).astype(F32)


def _shift_rows(a, k):
    return pltpu.roll(a, k % a.shape[0], 0)


def _conv3(x, w, b):
    return w[0:1] * _shift_rows(x, 1) + w[1:2] * x + w[2:3] * _shift_rows(x, -1) + b


def _gate_up_specs(rows_, wblk, nb):
    return [pl.BlockSpec((rows_, wblk), lambda j: (0, j)), pl.BlockSpec((rows_, wblk), lambda j: (0, j + nb))]


def conv_fwd(hu, cw, cb, *, rows, wblk, name):
    L, N2 = hu.shape
    nb = N2 // 2 // wblk
    nchunk = L // rows

    def body(hg_ref, hu_ref, wg_ref, wu_ref, bg_ref, bu_ref, a_ref, s1_ref, s2_ref):
        def chunk(ci, carry):
            r0 = pl.multiple_of(ci * rows, rows)
            gate = _conv3(_conv_ext(hg_ref, r0, rows, L), wg_ref[...], bg_ref[...])[PAD:PAD + rows]
            up = _conv3(_conv_ext(hu_ref, r0, rows, L), wu_ref[...], bu_ref[...])[PAD:PAD + rows]
            sg = jax.nn.sigmoid(gate)
            silu = gate * sg
            at = pl.ds(r0, rows)
            a_ref[at, :] = (silu * up).astype(BF16)
            s1_ref[at, :] = silu.astype(BF16)
            s2_ref[at, :] = (up * (sg + silu * (1.0 - sg))).astype(BF16)
            return carry

        lax.fori_loop(0, nchunk, chunk, 0)

    out = pl.BlockSpec((L, wblk), lambda j: (0, j))
    return pl.pallas_call(
        body, name=name, grid=(nb,),
        in_specs=_gate_up_specs(L, wblk, nb) + _gate_up_specs(3, wblk, nb) + _gate_up_specs(1, wblk, nb),
        out_specs=[out] * 3, out_shape=[jax.ShapeDtypeStruct((L, N2 // 2), BF16)] * 3,
        compiler_params=_cp(("parallel",)),
    )(hu, hu, cw, cw, cb, cb)


def conv_bwd(da, s1, s2, hu, cw, *, rows, wblk, name):
    L, N2 = hu.shape
    F = N2 // 2
    nb = F // wblk
    nchunk = L // rows
    mid = slice(PAD, PAD + rows)

    def body(da_ref, s1_ref, s2_ref, hg_ref, hu_ref, wg_ref, wu_ref, dg_ref, du_ref, dwg_ref, dwu_ref, dbg_ref, dbu_ref):
        for ref in (dwg_ref, dwu_ref, dbg_ref, dbu_ref):
            ref[...] = jnp.zeros_like(ref)

        def half_bwd(x_ref, dh, w_ref, dx_ref, dw_ref, db_ref, r0):
            w = w_ref[...]
            nxt, prv = _shift_rows(dh, -1)[mid], _shift_rows(dh, 1)[mid]
            dhm, xm = dh[mid], x_ref[pl.ds(r0, rows), :].astype(F32)
            dx_ref[pl.ds(r0, rows), :] = (w[0:1] * nxt + w[1:2] * dhm + w[2:3] * prv).astype(BF16)
            db_ref[...] += _colsum(dhm)
            dw_ref[0:1, :] += _colsum(nxt * xm)
            dw_ref[1:2, :] += _colsum(dhm * xm)
            dw_ref[2:3, :] += _colsum(prv * xm)

        def chunk(ci, carry):
            r0 = pl.multiple_of(ci * rows, rows)
            d = _conv_ext(da_ref, r0, rows, L)
            half_bwd(hu_ref, d * _conv_ext(s1_ref, r0, rows, L), wu_ref, du_ref, dwu_ref, dbu_ref, r0)
            half_bwd(hg_ref, d * _conv_ext(s2_ref, r0, rows, L), wg_ref, dg_ref, dwg_ref, dbg_ref, r0)
            return carry

        lax.fori_loop(0, nchunk, chunk, 0)

    blk = lambda r: pl.BlockSpec((r, wblk), lambda j: (0, j))
    return pl.pallas_call(
        body, name=name, grid=(nb,),
        in_specs=[blk(L)] * 3 + _gate_up_specs(L, wblk, nb) + _gate_up_specs(3, wblk, nb),
        out_specs=[blk(L), blk(L), blk(3), blk(3), blk(1), blk(1)],
        out_shape=[jax.ShapeDtypeStruct((L, F), BF16)] * 2 + [jax.ShapeDtypeStruct((3, F), F32)] * 2
        + [jax.ShapeDtypeStruct((1, F), F32)] * 2,
        compiler_params=_cp(("parallel",)),
    )(da, s1, s2, hu, hu, cw, cw)


def _window_sums(pad_ref, w, lead):
    a = pad_ref[...]
    k = 1
    while k < w:
        a = a + _shift_rows(a, -k)
        k *= 2
    return _shift_rows(a, lead) if lead else a


def _pool_counts(L, h):
    t = lax.broadcasted_iota(jnp.int32, (L, 1), 0)
    return (jnp.minimum(t + h, L) - jnp.maximum(t - h, 0)).astype(F32)


def _pooled(u_ref, pad_ref, L, w):
    h = w // 2
    pad_ref[pl.ds(PAD, L), :] = u_ref[...]
    win = _window_sums(pad_ref, w, h)[PAD:PAD + L]
    return win / _pool_counts(L, h) - u_ref[...]


def _zero_pad_edges(pad_ref, L):
    z = jnp.zeros((PAD, LANES), F32)
    pad_ref[pl.ds(0, PAD), :] = z
    pad_ref[pl.ds(PAD + L, PAD), :] = z


def pool_fwd(u, w_pool, pool_scale, *, name):
    L = u.shape[0]

    def body(u_ref, w_ref, ps_ref, p_ref, pad_ref):
        _zero_pad_edges(pad_ref, L)
        for gi, win in enumerate(POOL_WINDOWS):
            @pl.when(pl.program_id(0) == gi)
            def _():
                pooled = _pooled(u_ref, pad_ref, L, win)
                p_ref[...] = (_dot(pooled.astype(BF16), w_ref[...].astype(BF16)) * ps_ref[...]).astype(BF16)

    return pl.pallas_call(
        body, name=name, grid=(len(POOL_WINDOWS),),
        in_specs=[pl.BlockSpec((L, LANES), lambda gi: (0, gi)), pl.BlockSpec((None, LANES, LANES), lambda gi: (gi, 0, 0)),
                  pl.BlockSpec((1, LANES), lambda gi: (0, gi))],
        out_specs=pl.BlockSpec((L, LANES), lambda gi: (0, gi)),
        out_shape=jax.ShapeDtypeStruct((L, 4 * LANES), BF16),
        scratch_shapes=[pltpu.VMEM((L + 2 * PAD, LANES), F32)],
        compiler_params=_cp(("parallel",)),
    )(u, w_pool, pool_scale)


def pool_bwd(u, dpa, w_pool, pool_scale, *, name):
    L = u.shape[0]

    def body(u_ref, dp_ref, w_ref, ps_ref, du_ref, dw_ref, dps_ref, pad_ref):
        _zero_pad_edges(pad_ref, L)
        for gi, win in enumerate(POOL_WINDOWS):
            @pl.when(pl.program_id(0) == gi)
            def _():
                h = win // 2
                wb = w_ref[...].astype(BF16)
                pooled = _pooled(u_ref, pad_ref, L, win).astype(BF16)
                dp = dp_ref[...].astype(F32)
                dps_ref[...] = _colsum(dp * _dot(pooled, wb))
                dy = (dp * ps_ref[...]).astype(BF16)
                dw_ref[...] = _dot_tn(pooled, dy)
                dpooled = _dot_nt(dy, wb)
                pad_ref[pl.ds(PAD, L), :] = dpooled / _pool_counts(L, h)
                du_ref[...] = (_window_sums(pad_ref, win, h - 1)[PAD:PAD + L] - dpooled).astype(BF16)

    return pl.pallas_call(
        body, name=name, grid=(len(POOL_WINDOWS),),
        in_specs=[pl.BlockSpec((L, LANES), lambda gi: (0, gi)), pl.BlockSpec((L, LANES), lambda gi: (0, gi)),
                  pl.BlockSpec((None, LANES, LANES), lambda gi: (gi, 0, 0)), pl.BlockSpec((1, LANES), lambda gi: (0, gi))],
        out_specs=[pl.BlockSpec((L, LANES), lambda gi: (0, gi)), pl.BlockSpec((None, LANES, LANES), lambda gi: (gi, 0, 0)),
                   pl.BlockSpec((1, LANES), lambda gi: (0, gi))],
        out_shape=[jax.ShapeDtypeStruct((L, 4 * LANES), BF16), jax.ShapeDtypeStruct((4, LANES, LANES), F32),
                   jax.ShapeDtypeStruct((1, 4 * LANES), F32)],
        scratch_shapes=[pltpu.VMEM((L + 2 * PAD, LANES), F32)],
        compiler_params=_cp(("parallel",)),
    )(u, dpa, w_pool, pool_scale)


def _attn_probs(qk, band_k, ctx_k, sink_ref, kh, mask4):
    s_loc = jnp.where(mask4, _dot_nt(qk, band_k), NEG_INF)
    s_ctx = _dot_nt(qk, ctx_k)
    sk = jnp.concatenate([jnp.full((BLK, 1), sink_ref[kh * GQA + hh], F32) for hh in range(GQA)], axis=0)
    m = jnp.maximum(jnp.maximum(jnp.max(s_loc, axis=-1, keepdims=True), jnp.max(s_ctx, axis=-1, keepdims=True)), sk)
    e_loc, e_ctx, e_s = jnp.exp(s_loc - m), jnp.exp(s_ctx - m), jnp.exp(sk - m)
    inv = 1.0 / (jnp.sum(e_loc, axis=-1, keepdims=True) + jnp.sum(e_ctx, axis=-1, keepdims=True) + e_s)
    return e_loc * inv, e_ctx * inv, e_s * inv


def _attn_block(n, L):
    start = pl.multiple_of(jnp.clip((n - 1) * BLK, 0, L - 3 * BLK), BLK)
    qpos = n * BLK + lax.broadcasted_iota(jnp.int32, (BLK, 3 * BLK), 0)
    kpos = start + lax.broadcasted_iota(jnp.int32, (BLK, 3 * BLK), 1)
    mask = jnp.abs(kpos - qpos) <= WINDOW
    return start, jnp.concatenate([mask] * GQA, axis=0)


def _stack_slabs(ref, rows=slice(None)):
    return jnp.concatenate([ref[rows, s * LANES:(s + 1) * LANES] for s in range(GQA)], axis=0)


def _kv_head_lanes(kh):
    return (lax.broadcasted_iota(jnp.int32, (1, LANES), 1) // HEAD_DIM) == kh


def permute_heads(w, inverse=False):
    lo, hi = 4 * LANES, 8 * LANES
    mid = w[lo:hi].reshape(*((GQA, N_KV_HEADS) if inverse else (N_KV_HEADS, GQA)), HEAD_DIM, w.shape[1])
    return jnp.concatenate([w[:lo], mid.swapaxes(0, 1).reshape(hi - lo, w.shape[1]), w[hi:]], axis=0)


def attn_fwd(q, kv, kvc, sink, *, qb, name):
    L = q.shape[0]
    C = kvc.shape[0]
    scale = HEAD_DIM ** -0.5

    def body(q_ref, kv_ref, kvc_ref, sink_ref, o_ref):
        kvc_ = kvc_ref[...]
        for b in range(qb):
            rows = slice(b * BLK, (b + 1) * BLK)
            start, mask4 = _attn_block(pl.program_id(0) * qb + b, L)
            band = kv_ref[pl.ds(start, 3 * BLK), :]
            qs = _stack_slabs(q_ref, rows) * scale
            o = jnp.zeros((GQA * BLK, LANES), F32)
            for kh in range(N_KV_HEADS):
                grp = _kv_head_lanes(kh)
                qk = jnp.where(grp, qs, jnp.zeros_like(qs))
                p_loc, p_ctx, _ = _attn_probs(qk, band[:, :LANES], kvc_[:, :LANES], sink_ref, kh, mask4)
                o = o + jnp.where(grp, _dot(p_loc.astype(BF16), band[:, LANES:]) + _dot(p_ctx.astype(BF16), kvc_[:, LANES:]), 0.0)
            for s in range(GQA):
                o_ref[rows, s * LANES:(s + 1) * LANES] = o[s * BLK:(s + 1) * BLK].astype(BF16)

    return pl.pallas_call(
        body, name=name, grid=(L // (qb * BLK),),
        in_specs=[pl.BlockSpec((qb * BLK, 4 * LANES), lambda n: (n, 0)), _full((L, 2 * LANES)), _full((C, 2 * LANES)),
                  pl.BlockSpec(memory_space=pltpu.SMEM)],
        out_specs=pl.BlockSpec((qb * BLK, 4 * LANES), lambda n: (n, 0)),
        out_shape=jax.ShapeDtypeStruct((L, 4 * LANES), BF16),
        compiler_params=_cp(("parallel",)),
    )(q, kv, kvc, sink)


def attn_bwd(q, kv, kvc, sink, dpa, cos, sa, sb, *, name):
    L = q.shape[0]
    C = kvc.shape[0]
    nb = L // BLK
    scale = HEAD_DIM ** -0.5

    def body(q_ref, kv_ref, kvc_ref, sink_ref, do_ref, c_ref, sa_ref, sb_ref, cq_ref, saq_ref, sbq_ref,
             dq_ref, dkv_ref, dkvc_ref, dsink_ref, dkv_acc, dkvc_acc):
        n = pl.program_id(0)

        @pl.when(n == 0)
        def _():
            dkv_acc[...] = jnp.zeros_like(dkv_acc)
            dkvc_acc[...] = jnp.zeros_like(dkvc_acc)
            dsink_ref[...] = jnp.zeros_like(dsink_ref)

        start, mask4 = _attn_block(n, L)
        band = kv_ref[pl.ds(start, 3 * BLK), :]
        kvc_ = kvc_ref[...]
        band_k, band_v, ctx_k, ctx_v = band[:, :LANES], band[:, LANES:], kvc_[:, :LANES], kvc_[:, LANES:]
        qs = _stack_slabs(q_ref) * scale
        dos = _stack_slabs(do_ref)
        lane = lax.broadcasted_iota(jnp.int32, (1, LANES), 1)
        dsink = jnp.zeros((1, LANES), F32)
        dq = jnp.zeros((GQA * BLK, LANES), F32)
        dk = jnp.zeros((LANES, 3 * BLK), F32)
        dv = jnp.zeros((LANES, 3 * BLK), F32)
        dkc = jnp.zeros((LANES, C), F32)
        dvc = jnp.zeros((LANES, C), F32)
        for kh in range(N_KV_HEADS):
            grp = _kv_head_lanes(kh)
            qk = jnp.where(grp, qs, jnp.zeros_like(qs))
            dok = jnp.where(grp, dos, jnp.zeros_like(dos))
            p_loc, p_ctx, p_s = _attn_probs(qk, band_k, ctx_k, sink_ref, kh, mask4)
            dp_loc = _dot_nt(dok, band_v)
            dp_ctx = _dot_nt(dok, ctx_v)
            delta = jnp.sum(p_loc * dp_loc, axis=-1, keepdims=True) + jnp.sum(p_ctx * dp_ctx, axis=-1, keepdims=True)
            ds_loc = (p_loc * (dp_loc - delta)).astype(BF16)
            ds_ctx = (p_ctx * (dp_ctx - delta)).astype(BF16)
            dsk = p_s * delta
            for hh in range(GQA):
                dsink = dsink - jnp.where(lane == kh * GQA + hh, jnp.sum(dsk[hh * BLK:(hh + 1) * BLK], axis=0, keepdims=True), 0.0)
            dq = dq + jnp.where(grp, _dot(ds_loc, band_k) + _dot(ds_ctx, ctx_k), 0.0)
            dk = dk + _dot_tn(qk, ds_loc)
            dv = dv + _dot_tn(dok, p_loc.astype(BF16))
            dkc = dkc + _dot_tn(qk, ds_ctx)
            dvc = dvc + _dot_tn(dok, p_ctx.astype(BF16))
        dsink_ref[...] += dsink
        dkv_acc[:LANES, pl.ds(start, 3 * BLK)] += dk
        dkv_acc[LANES:, pl.ds(start, 3 * BLK)] += dv
        dkvc_acc[:LANES, :] += dkc
        dkvc_acc[LANES:, :] += dvc
        c, a, b = cq_ref[...], -saq_ref[...], -sbq_ref[...]
        for s in range(GQA):
            dq_ref[:, s * LANES:(s + 1) * LANES] = _rope(dq[s * BLK:(s + 1) * BLK] * scale, c, a, b).astype(BF16)

        @pl.when(n == nb - 1)
        def _():
            dkv_ref[:, :LANES] = _rope(dkv_acc[:LANES, :].T, c_ref[...], -sa_ref[...], -sb_ref[...]).astype(BF16)
            dkv_ref[:, LANES:] = dkv_acc[LANES:, :].T.astype(BF16)
            dkvc_ref[...] = dkvc_acc[...].T.astype(BF16)

    blk = lambda w: pl.BlockSpec((BLK, w), lambda n: (n, 0))
    return pl.pallas_call(
        body, name=name, grid=(nb,),
        in_specs=[blk(4 * LANES), _full((L, 2 * LANES)), _full((C, 2 * LANES)), pl.BlockSpec(memory_space=pltpu.SMEM),
                  pl.BlockSpec((BLK, 4 * LANES), lambda n: (n, 1)),
                  _full((L, LANES)), _full((L, LANES)), _full((L, LANES)), blk(LANES), blk(LANES), blk(LANES)],
        out_specs=[blk(4 * LANES), _full((L, 2 * LANES)), _full((C, 2 * LANES)), _full((1, LANES))],
        out_shape=[jax.ShapeDtypeStruct((L, 4 * LANES), BF16), jax.ShapeDtypeStruct((L, 2 * LANES), BF16),
                   jax.ShapeDtypeStruct((C, 2 * LANES), BF16), jax.ShapeDtypeStruct((1, LANES), F32)],
        scratch_shapes=[pltpu.VMEM((2 * LANES, L), F32), pltpu.VMEM((2 * LANES, C), F32)],
        compiler_params=_cp(("arbitrary",)),
    )(q, kv, kvc, sink, dpa, cos, sa, sb, cos, sa, sb)


def _gelu_parts(x):
    th = jnp.tanh(SQRT_2_OVER_PI * (x + GELU_C * x * x * x))
    return 0.5 * x * (1.0 + th), th


def _gelu_grad(x, th):
    return 0.5 * (1.0 + th) + 0.5 * x * (1.0 - th * th) * SQRT_2_OVER_PI * (1.0 + 3.0 * GELU_C * x * x)


def _layernorm(v):
    mu = jnp.mean(v, axis=-1, keepdims=True)
    vc = v - mu
    rstd = lax.rsqrt(jnp.mean(vc * vc, axis=-1, keepdims=True) + EPS)
    return vc * rstd, rstd


def sgu_fwd(z1, ln_g, ln_b, ws, bst, *, rows, name):
    L, W2 = z1.shape
    W = W2 // 2
    ng = W // LANES

    def body(z_ref, g_ref, b_ref, ws_ref, bs_ref, o_ref):
        for c in range(rows // BLK):
            at = slice(c * BLK, (c + 1) * BLK)
            z, _ = _gelu_parts(z_ref[at, :].astype(F32))
            xhat, _ = _layernorm(z[:, W:])
            vln = (xhat * g_ref[...] + b_ref[...]).astype(BF16)
            for gi in range(ng):
                cs = slice(gi * LANES, (gi + 1) * LANES)
                s = _dot(ws_ref[gi], vln[:, cs]) + bs_ref[:, gi:gi + 1]
                o_ref[at, cs] = (z[:, cs] * s).astype(BF16)

    vec = _full((1, W))
    return pl.pallas_call(
        body, name=name, grid=(L // rows,),
        in_specs=[pl.BlockSpec((rows, W2), lambda n: (n, 0)), vec, vec, _full((ng, LANES, LANES)), _full((BLK, ng))],
        out_specs=pl.BlockSpec((rows, W), lambda n: (n, 0)),
        out_shape=jax.ShapeDtypeStruct((L, W), BF16),
        compiler_params=_cp(("parallel",)),
    )(z1, ln_g, ln_b, ws, bst)


def sgu_bwd(z1, dus, ln_g, ln_b, ws, wst, bst, *, rows, name):
    L, W2 = z1.shape
    W = W2 // 2
    ng = W // LANES

    def body(z_ref, d_ref, g_ref, b_ref, ws_ref, wst_ref, bs_ref, dz_ref, dws_ref, dbs_ref, dg_ref, db_ref, dv_scr):
        @pl.when(pl.program_id(0) == 0)
        def _():
            dws_ref[...] = jnp.zeros_like(dws_ref)
            dbs_ref[...] = jnp.zeros_like(dbs_ref)
            dg_ref[...] = jnp.zeros_like(dg_ref)
            db_ref[...] = jnp.zeros_like(db_ref)

        for c in range(rows // BLK):
            at = slice(c * BLK, (c + 1) * BLK)
            zp = z_ref[at, :].astype(F32)
            z, th = _gelu_parts(zp)
            xhat, rstd = _layernorm(z[:, W:])
            vln = (xhat * g_ref[...] + b_ref[...]).astype(BF16)
            d = d_ref[at, :].astype(F32)
            lane = lax.broadcasted_iota(jnp.int32, (1, LANES), 1)
            dbs = jnp.zeros((BLK, LANES), F32)
            dgel = _gelu_grad(zp, th)
            for gi in range(ng):
                cs = slice(gi * LANES, (gi + 1) * LANES)
                s = _dot(ws_ref[gi], vln[:, cs]) + bs_ref[:, gi:gi + 1]
                dz_ref[at, cs] = (d[:, cs] * s * dgel[:, cs]).astype(BF16)
                ds = d[:, cs] * z[:, cs]
                dbs = dbs + jnp.where(lane == gi, jnp.sum(ds, axis=-1, keepdims=True), 0.0)
                dsb = ds.astype(BF16)
                dws_ref[gi] += _dot_nt(dsb, vln[:, cs])
                dv_scr[:, cs] = _dot(wst_ref[gi], dsb)
            dbs_ref[...] += dbs
            dvln = dv_scr[...]
            dg_ref[...] += _colsum(dvln * xhat)
            db_ref[...] += _colsum(dvln)
            dxh = dvln * g_ref[...]
            dv = rstd * (dxh - jnp.mean(dxh, axis=-1, keepdims=True) - xhat * jnp.mean(dxh * xhat, axis=-1, keepdims=True))
            dz_ref[at, W:] = (dv * dgel[:, W:]).astype(BF16)

    vec = _full((1, W))
    return pl.pallas_call(
        body, name=name, grid=(L // rows,),
        in_specs=[pl.BlockSpec((rows, W2), lambda n: (n, 0)), pl.BlockSpec((rows, W), lambda n: (n, 0)), vec, vec,
                  _full((ng, LANES, LANES)), _full((ng, LANES, LANES)), _full((BLK, ng))],
        out_specs=[pl.BlockSpec((rows, W2), lambda n: (n, 0)), _full((ng, LANES, LANES)), _full((BLK, LANES)), vec, vec],
        out_shape=[jax.ShapeDtypeStruct((L, W2), BF16), jax.ShapeDtypeStruct((ng, LANES, LANES), F32),
                   jax.ShapeDtypeStruct((BLK, LANES), F32), jax.ShapeDtypeStruct((1, W), F32), jax.ShapeDtypeStruct((1, W), F32)],
        scratch_shapes=[pltpu.VMEM((BLK, W), F32)],
        compiler_params=_cp(("arbitrary",)),
    )(z1, dus, ln_g, ln_b, ws, wst, bst)


def _adamw_math(w, m, v, g):
    m_ = ADAM_B1 * m + (1.0 - ADAM_B1) * g
    v_ = ADAM_B2 * v + (1.0 - ADAM_B2) * (g * g)
    return -ADAM_LR * ((m_ / BC1) / (jnp.sqrt(v_ / BC2) + ADAM_EPS) + ADAM_WD * w), m_, v_


def adamw(w, m, v, gparts, *, tr, name):
    NL, R, Wd = w.shape
    nr = R // tr

    def body(w_ref, m_ref, v_ref, *rest):
        gp_refs, (g_ref, d_ref, nm_ref, nv_ref) = rest[:NL], rest[NL:]
        for l in range(NL):
            @pl.when(pl.program_id(0) == l)
            def _():
                g = gp_refs[l][0].astype(F32)
                for s in range(1, gp_refs[l].shape[0]):
                    g = g + gp_refs[l][s].astype(F32)
                g_ref[...] = g
                d_ref[...], nm_ref[...], nv_ref[...] = _adamw_math(w_ref[...], m_ref[...], v_ref[...], g)

    row = pl.BlockSpec((None, tr, Wd), lambda l, i: (l, i, 0))
    gspecs = [pl.BlockSpec((gparts[l].shape[0], tr, Wd), (lambda l_, i, l=l: (0, jnp.clip(i + (l_ - l) * nr, 0, nr - 1), 0)))
              for l in range(NL)]
    return pl.pallas_call(
        body, name=name, grid=(NL, nr),
        in_specs=[row, row, row] + gspecs, out_specs=[row] * 4, out_shape=[jax.ShapeDtypeStruct((NL, R, Wd), F32)] * 4,
        compiler_params=_cp(("arbitrary", "arbitrary")),
    )(w, m, v, *gparts)


def small_update(gpacks, me, params, loss_row, *, name):
    n = len(params)

    def body(me_ref, gp_ref, *refs):
        ins, outs, gs_ref = refs[:3 * n], refs[3 * n:-1], refs[-1]
        gs_ref[...] = gp_ref[0].astype(F32)
        for dv in range(1, N_DEV):
            gs_ref[...] += gp_ref[dv].astype(F32)
        for p, (w, _, _, off, per_dev) in enumerate(params):
            w_ref, m_ref, v_ref = ins[3 * p:3 * p + 3]
            g_ref, d_ref, nm_ref, nv_ref = outs[4 * p:4 * p + 4]
            rows, cols = w.shape
            if cols == LANES and rows % 8 == 0 and not per_dev:
                g = gs_ref[off:off + rows, :]
                g_ref[...] = g
                d_ref[...], nm_ref[...], nv_ref[...] = _adamw_math(w_ref[...], m_ref[...], v_ref[...], g)
                continue
            chunks = -(-cols // LANES)
            base = off + me_ref[0] * per_dev if per_dev else off
            for i in range(rows):
                for j in range(chunks):
                    wd = min(LANES, cols - j * LANES)
                    at = (slice(i, i + 1), slice(j * LANES, j * LANES + wd))
                    g = gs_ref[pl.ds(base + i * chunks + j, 1), 0:wd]
                    g_ref[at] = g
                    d_ref[at], nm_ref[at], nv_ref[at] = _adamw_math(w_ref[at], m_ref[at], v_ref[at], g)
        outs[-1][...] = jnp.sum(gs_ref[loss_row:loss_row + 1, :], axis=1, keepdims=True)

    flat = [a for w, m, v, _, _ in params for a in (w, m, v)]
    out_shape = [jax.ShapeDtypeStruct(w.shape, F32) for w, _, _, _, _ in params for _ in range(4)] + [jax.ShapeDtypeStruct((1, 1), F32)]
    return pl.pallas_call(
        body, name=name, grid=(1,),
        in_specs=[pl.BlockSpec(memory_space=pltpu.SMEM), _full(gpacks.shape)] + [_full(a.shape) for a in flat],
        out_specs=[_full(o.shape) for o in out_shape], out_shape=out_shape,
        scratch_shapes=[pltpu.VMEM(gpacks.shape[1:], F32)],
        compiler_params=_cp(("arbitrary",)),
    )(me, gpacks, *flat)


def ada_fwd_mm(cs, w_ada, b_loc, *, name):
    R, D = cs.shape
    nl, _, n = w_ada.shape

    def body(c_ref, w_ref, b_ref, s_ref, m_ref):
        c = c_ref[...]
        s = c * jax.nn.sigmoid(c)
        s_ref[...] = s
        for i in range(nl):
            m_ref[i] = _dot(s.astype(BF16), w_ref[i].astype(BF16)) + b_ref[i:i + 1, :]

    return pl.pallas_call(
        body, name=name, in_specs=[_full((R, D)), _full((nl, D, n)), _full((nl, n))],
        out_specs=[_full((R, D)), _full((nl, R, n))], grid=(1,),
        out_shape=[jax.ShapeDtypeStruct((R, D), F32), jax.ShapeDtypeStruct((nl, R, n), F32)],
        compiler_params=_cp(("arbitrary",)),
    )(cs, w_ada, b_loc)


def ada_bwd_mm(s, c_ctx, dall, w_ada, *, name):
    R, D = s.shape
    nl, _, n = w_ada.shape

    def body(s_ref, cc_ref, d_ref, w_ref, gw_ref, dcc_ref):
        sb = s_ref[...].astype(BF16)
        row = lax.broadcasted_iota(jnp.int32, (R, 1), 0)
        dctx = d_ref[0, 1:2, :]
        for dv in range(1, N_DEV):
            dctx = dctx + d_ref[dv, 1:2, :]
        for i in range(nl):
            dm = jnp.zeros((R, n), F32)
            for dv in range(N_DEV):
                dm = dm + jnp.where(row == dv, d_ref[dv, 2 * i:2 * i + 1, :], 0.0)
            if i == 0:
                dm = dm + jnp.where(row == N_DEV, dctx, 0.0)
            gw_ref[i] = _dot_tn(sb, dm.astype(BF16))
        cc = cc_ref[...]
        sg = jax.nn.sigmoid(cc)
        ds = _dot_nt(jnp.broadcast_to(dctx, (8, n)).astype(BF16), w_ref[0].astype(BF16))
        dcc_ref[...] = ds * (sg * (1.0 + cc * (1.0 - sg)))

    return pl.pallas_call(
        body, name=name, grid=(1,),
        in_specs=[_full((R, D)), _full((1, D)), _full((N_DEV, 3, n)), _full((nl, D, n))],
        out_specs=[_full((nl, D, n)), _full((8, D))],
        out_shape=[jax.ShapeDtypeStruct((nl, D, n), F32), jax.ShapeDtypeStruct((8, D), F32)],
        compiler_params=_cp(("arbitrary",)),
    )(s, c_ctx, dall, w_ada)


def _place():
    x, y, c = lax.axis_index("x"), lax.axis_index("y"), lax.axis_index("c")
    return x, y, c


def _lin(p):
    return 4 * p[0] + 2 * p[1] + p[2]


def all_gather_small(xb, *, name):
    R, W = xb.shape

    def body(x_ref, out_ref, send_sems, recv_sems, local_sem):
        x, y, c = _place()
        me = _lin((x, y, c))
        mine = pltpu.make_async_copy(x_ref, out_ref.at[me], local_sem)
        mine.start()
        copies = []
        for k in range(1, N_DEV):
            peer = (x ^ (k >> 2), y ^ ((k >> 1) & 1), c ^ (k & 1))
            mk = lambda dst, k=k, peer=peer: pltpu.make_async_remote_copy(
                src_ref=x_ref, dst_ref=dst, send_sem=send_sems.at[k - 1], recv_sem=recv_sems.at[k - 1], device_id=peer, device_id_type=MESH)
            mk(out_ref.at[me]).start()
            copies.append(mk(out_ref.at[_lin(peer)]))
        for cp in copies:
            cp.wait_recv()
        for cp in copies:
            cp.wait_send()
        mine.wait()

    vm = pl.BlockSpec(memory_space=pltpu.VMEM)
    return pl.pallas_call(
        body, name=name, in_specs=[vm], out_specs=vm, out_shape=jax.ShapeDtypeStruct((N_DEV, R, W), xb.dtype),
        scratch_shapes=[pltpu.SemaphoreType.DMA((7,)), pltpu.SemaphoreType.DMA((7,)), pltpu.SemaphoreType.DMA],
        compiler_params=pltpu.CompilerParams(vmem_limit_bytes=VMEM_LIMIT),
    )(xb)


HBM_SPEC = pl.BlockSpec(memory_space=pltpu.HBM)
SEM_SPEC = pl.BlockSpec(memory_space=pltpu.SEMAPHORE)
ORDERED_EFFECT = pltpu.SideEffectType.DATAFLOW_SIDE_EFFECTING


def _exchange_copies(srcs, lands, sems, scatter):
    x, y, c = _place()
    me = _lin((x, y, c))
    for j in range(len(srcs)):
        r = lands[j].shape[0] // N_DEV
        block = lambda d, j=j, r=r: pl.ds(pl.multiple_of(d * r, 16), r)
        for k in range(1, N_DEV):
            peer = (x ^ (k >> 2), y ^ ((k >> 1) & 1), c ^ (k & 1))
            src = srcs[j].at[block(_lin(peer)), :] if scatter else srcs[j]
            mk = lambda dst, j=j, k=k, peer=peer, src=src: pltpu.make_async_remote_copy(
                src_ref=src, dst_ref=dst, send_sem=sems[2 * j].at[k - 1], recv_sem=sems[2 * j + 1].at[k - 1],
                device_id=peer, device_id_type=MESH)
            yield mk(lands[j].at[block(me), :]), mk(lands[j].at[block(_lin(peer)), :])


def exchange_start(srcs, lands, *, scatter, name):
    nw = len(srcs)

    def body(*refs):
        for start, _ in _exchange_copies(refs[:nw], refs[nw:2 * nw], refs[2 * nw:4 * nw], scatter):
            start.start()
        refs[-1][...] = jnp.zeros_like(refs[-1])

    thru = [pltpu.HBM(a.shape, a.dtype) for a in (*srcs, *lands)]
    res = pl.pallas_call(
        body, name=name, in_specs=[HBM_SPEC] * (2 * nw),
        out_specs=[SEM_SPEC] * (2 * nw) + [HBM_SPEC] * (2 * nw) + [pl.BlockSpec(memory_space=pltpu.VMEM)],
        out_shape=[pltpu.SemaphoreType.DMA((N_DEV - 1,))] * (2 * nw) + thru + [jax.ShapeDtypeStruct((8, LANES), F32)],
        input_output_aliases={i: 2 * nw + i for i in range(2 * nw)},
        compiler_params=pltpu.CompilerParams(has_side_effects=ORDERED_EFFECT),
    )(*[pltpu.with_memory_space_constraint(a, pltpu.HBM) for a in (*srcs, *lands)])
    return res[:2 * nw], res[2 * nw:3 * nw], res[3 * nw:4 * nw], res[-1]


def exchange_wait(srcs, lands, sems, after, *, scatter, name):
    nw = len(srcs)
    after = list(after) if isinstance(after, (list, tuple)) else [after]

    def body(*refs):
        for _, arrive in _exchange_copies(refs[:nw], refs[nw:2 * nw], refs[2 * nw:4 * nw], scatter):
            arrive.wait_send()
            arrive.wait_recv()

    res = pl.pallas_call(
        body, name=name, in_specs=[HBM_SPEC] * (2 * nw) + [SEM_SPEC] * (2 * nw) + [pl.BlockSpec(memory_space=pl.ANY)] * len(after),
        out_specs=[HBM_SPEC] * (2 * nw), out_shape=[pltpu.HBM(a.shape, a.dtype) for a in (*srcs, *lands)],
        input_output_aliases={i: i for i in range(2 * nw)},
        compiler_params=pltpu.CompilerParams(has_side_effects=ORDERED_EFFECT),
    )(*srcs, *lands, *sems, *after)
    return res[nw:]


def place_own(srcs, rows, me, *, scatter, name):
    nw = len(srcs)
    lands = [lax.empty((N_DEV * r, s_.shape[1]), s_.dtype) for r, s_ in zip(rows, srcs)]

    def body(me_ref, *refs):
        for j in range(nw):
            refs[2 * nw + j][...] = refs[j][...]

    mine = lambda i, me_ref: (me_ref[0], 0)
    src_at = mine if scatter else (lambda i, me_ref: (0, 0))
    blocks = [(r, s_.shape[1]) for r, s_ in zip(rows, srcs)]
    return pl.pallas_call(
        body, name=name,
        grid_spec=pltpu.PrefetchScalarGridSpec(
            num_scalar_prefetch=1, grid=(1,),
            in_specs=[pl.BlockSpec(b_, src_at) for b_ in blocks] + [pl.BlockSpec(memory_space=pl.ANY)] * nw,
            out_specs=[pl.BlockSpec(b_, mine) for b_ in blocks]),
        out_shape=[jax.ShapeDtypeStruct(l_.shape, l_.dtype) for l_ in lands],
        input_output_aliases={1 + nw + j: j for j in range(nw)},
        compiler_params=_cp(("arbitrary",)),
    )(jnp.reshape(me, (1,)).astype(jnp.int32), *srcs, *lands)


def _rope_tables(L):
    t = jnp.arange(L)
    inv = ROPE_BASE ** (-jnp.arange(ROPE_FREQS, dtype=F32) / ROPE_FREQS)
    ar = (t // GRID_W).astype(F32)[:, None] * inv
    ac = (t % GRID_W).astype(F32)[:, None] * inv
    z = jnp.zeros_like(ar)
    cos = jnp.concatenate([jnp.cos(ar), jnp.cos(ar), jnp.cos(ac), jnp.cos(ac)], axis=1)
    sa = jnp.concatenate([-jnp.sin(ar), z, -jnp.sin(ac), z], axis=1)
    sb = jnp.concatenate([z, jnp.sin(ar), z, jnp.sin(ac)], axis=1)
    return tuple(jnp.tile(a, (1, LANES // HEAD_DIM)) for a in (cos, sa, sb))


def _nat2d(a):
    return a.reshape(1, -1) if a.ndim == 1 else a.reshape(-1, a.shape[-1])


def _pack_rows(a):
    rows, cols = a.shape
    chunks = -(-cols // LANES)
    f = jnp.pad(a, ((0, 0), (0, chunks * LANES - cols))).reshape(rows * chunks, LANES)
    return jnp.pad(f, ((0, -f.shape[0] % 8), (0, 0)))


def _rows128(a):
    f = a.reshape(-1)
    n = -(-f.shape[0] // (8 * LANES)) * 8 * LANES
    return jnp.pad(f, (0, n - f.shape[0])).reshape(-1, LANES)


def kernel(x, c, ctx, c_ctx, w_ada, b_ada, g_mix_pre, g_mix_post, g_ffn_pre, g_ffn_post, w_in_even, w_pool, pool_scale, attn_sink, w_out_even, w_in_odd, sgu_ln_g, sgu_ln_b, sgu_w, sgu_b, w_out_odd, w_ffn_up, ffn_conv_w, ffn_conv_b, w_ffn_down, loss_target, m_c_ctx, m_w_ada, m_b_ada, m_g_mix_pre, m_g_mix_post, m_g_ffn_pre, m_g_ffn_post, m_w_in_even, m_w_pool, m_pool_scale, m_attn_sink, m_w_out_even, m_w_in_odd, m_sgu_ln_g, m_sgu_ln_b, m_sgu_w, m_sgu_b, m_w_out_odd, m_w_ffn_up, m_ffn_conv_w, m_ffn_conv_b, m_w_ffn_down, v_c_ctx, v_w_ada, v_b_ada, v_g_mix_pre, v_g_mix_post, v_g_ffn_pre, v_g_ffn_post, v_w_in_even, v_w_pool, v_pool_scale, v_attn_sink, v_w_out_even, v_w_in_odd, v_sgu_ln_g, v_sgu_ln_b, v_sgu_w, v_sgu_b, v_w_out_odd, v_w_ffn_up, v_ffn_conv_w, v_ffn_conv_b, v_w_ffn_down):
    P = dict(c_ctx=c_ctx, w_ada=w_ada, b_ada=b_ada, g_mix_pre=g_mix_pre, g_mix_post=g_mix_post, g_ffn_pre=g_ffn_pre,
             g_ffn_post=g_ffn_post, w_in_even=w_in_even, w_pool=w_pool, pool_scale=pool_scale, attn_sink=attn_sink,
             w_out_even=w_out_even, w_in_odd=w_in_odd, sgu_ln_g=sgu_ln_g, sgu_ln_b=sgu_ln_b, sgu_w=sgu_w, sgu_b=sgu_b,
             w_out_odd=w_out_odd, w_ffn_up=w_ffn_up, ffn_conv_w=ffn_conv_w, ffn_conv_b=ffn_conv_b, w_ffn_down=w_ffn_down)
    M = dict(c_ctx=m_c_ctx, w_ada=m_w_ada, b_ada=m_b_ada, g_mix_pre=m_g_mix_pre, g_mix_post=m_g_mix_post, g_ffn_pre=m_g_ffn_pre,
             g_ffn_post=m_g_ffn_post, w_in_even=m_w_in_even, w_pool=m_w_pool, pool_scale=m_pool_scale, attn_sink=m_attn_sink,
             w_out_even=m_w_out_even, w_in_odd=m_w_in_odd, sgu_ln_g=m_sgu_ln_g, sgu_ln_b=m_sgu_ln_b, sgu_w=m_sgu_w, sgu_b=m_sgu_b,
             w_out_odd=m_w_out_odd, w_ffn_up=m_w_ffn_up, ffn_conv_w=m_ffn_conv_w, ffn_conv_b=m_ffn_conv_b, w_ffn_down=m_w_ffn_down)
    V = dict(c_ctx=v_c_ctx, w_ada=v_w_ada, b_ada=v_b_ada, g_mix_pre=v_g_mix_pre, g_mix_post=v_g_mix_post, g_ffn_pre=v_g_ffn_pre,
             g_ffn_post=v_g_ffn_post, w_in_even=v_w_in_even, w_pool=v_w_pool, pool_scale=v_pool_scale, attn_sink=v_attn_sink,
             w_out_even=v_w_out_even, w_in_odd=v_w_in_odd, sgu_ln_g=v_sgu_ln_g, sgu_ln_b=v_sgu_ln_b, sgu_w=v_sgu_w, sgu_b=v_sgu_b,
             w_out_odd=v_w_out_odd, w_ffn_up=v_w_ffn_up, ffn_conv_w=v_ffn_conv_w, ffn_conv_b=v_ffn_conv_b, w_ffn_down=v_w_ffn_down)

    x = x[0]
    ctx = ctx[0]
    target = loss_target[0]
    L, D = x.shape
    C = ctx.shape[0]
    tm = min(512, L)
    conv_rows = min(1024, L)
    me = 4 * lax.axis_index("x") + 2 * lax.axis_index("y") + lax.axis_index("c")
    n_ada = w_ada.shape[2]
    F = w_ffn_down.shape[1] * N_DEV
    half_f = F // 2

    n_cw = ffn_conv_w.shape[2]
    small = jnp.concatenate([_rows128(c), _rows128(sgu_ln_g), _rows128(sgu_ln_b), _rows128(ffn_conv_w)], axis=0)
    small_all = all_gather_small(small, name="gather_small_inputs")
    c_all = small_all[:, :8].reshape(N_DEV, D)
    ln_g = small_all[:, 8].reshape(1, D)
    ln_b = small_all[:, 16].reshape(1, D)
    conv_w = small_all[:, 24:].reshape(N_DEV, -1)[:, :2 * 3 * n_cw].reshape(N_DEV, 2, 3, n_cw)
    conv_w = conv_w.transpose(1, 2, 0, 3).reshape(2, 3, 2 * F)

    cs = jnp.concatenate([c_all, c_ctx[None, :], jnp.zeros((7, D), F32)], axis=0)
    b_loc = lax.dynamic_slice(b_ada, (0, me * n_ada), (2, n_ada))
    silu_c, mods_loc = ada_fwd_mm(cs, w_ada, b_loc, name="ada_fwd")
    mods_all = all_gather_small(mods_loc.reshape(-1, LANES), name="gather_mods")

    shards = [s.astype(BF16) for s in (w_in_even[0].T, w_out_even[0], w_ffn_up[0].T, w_ffn_down[0],
                                       w_in_odd[0].T, w_out_odd[0], w_ffn_up[1].T, w_ffn_down[1])]
    shards, mods_all = lax.optimization_barrier((shards, mods_all))
    w_sems, w_srcs, w_lands, _ = exchange_start(shards, place_own(shards, [s.shape[0] for s in shards], me, scatter=False, name="gather_own"),
                                              scatter=False, name="gather_start")

    def weight(j, after):
        return exchange_wait([w_srcs[j]], [w_lands[j]], w_sems[2 * j:2 * j + 2], after, scatter=False, name=f"gather_wait_{j}")[0]

    mods_all = mods_all.reshape(N_DEV, 2, 16, n_ada).transpose(1, 2, 0, 3).reshape(2, 16, 6 * D)
    mod = lambda i, row: [m_[None, :] for m_ in jnp.split(lax.dynamic_index_in_dim(mods_all[i], row, 0, False), 6)]
    sh_m, sc_m, gt_m, sh_f, sc_f, gt_f = zip(mod(0, me), mod(1, me))
    csh_m, csc_m = mod(0, N_DEV)[:2]

    row = lambda a, i: a[i][None, :]

    cos, sa, sb = _rope_tables(L)
    sink = attn_sink[0]
    bst = sgu_b[0].T
    sgu_wb, sgu_wtb = sgu_w[0].astype(BF16), sgu_w[0].swapaxes(1, 2).astype(BF16)
    wup, wdn = [None, None], [None, None]

    def ffn_fwd(i, xin):
        wup[i] = weight(2 + 4 * i, xin)
        h, hu = pre_mm(xin, row(g_ffn_pre, i), sh_f[i], sc_f[i], wup[i], tm=tm, tn=half_f, name=f"ffn_up_{i}")
        a, s1, s2 = conv_fwd(hu, conv_w[i], ffn_conv_b[i][None, :], rows=conv_rows, wblk=2 * LANES, name=f"ffn_conv_{i}")
        wdn[i] = weight(3 + 4 * i, a)
        res = mm_post([a], wdn[i], xin, row(g_ffn_post, i), gt_f[i], tm=tm, target=target if i == 1 else None, name=f"ffn_down_{i}")
        return (h, (hu, s1, s2), a, *res)

    first_mod, cos, sa, sb = lax.optimization_barrier((sh_m[0], cos, sa, sb))
    win_e = permute_heads(weight(0, first_mod))
    h0, u, q, kv = inproj_even(x, row(g_mix_pre, 0), sh_m[0], sc_m[0], win_e, cos, sa, sb, tm=tm, name="in_even")
    hc, kvc = pre_mm(ctx, row(g_mix_pre, 0), csh_m, csc_m, win_e, tm=C, tn=2 * LANES, w_row_off=8 * LANES, name="in_even_ctx")
    pa = [pool_fwd(u, w_pool[0], pool_scale, name="pool_fwd"), attn_fwd(q, kv, kvc, sink, qb=2, name="attn_fwd")]
    wout_e = permute_heads(weight(1, pa[1]))
    y0, x1 = mm_post(pa, wout_e, x, row(g_mix_post, 0), gt_m[0], tm=tm, name="out_even")
    h1, hu0, a0, f0, x2 = ffn_fwd(0, x1)
    win_o = weight(4, x2)
    h2, z1 = pre_mm(x2, row(g_mix_pre, 1), sh_m[1], sc_m[1], win_o, tm=tm, tn=D, name="in_odd")
    us = sgu_fwd(z1, ln_g, ln_b, sgu_wb, bst, rows=tm, name="sgu_fwd")
    wout_o = weight(5, us)
    y1, x3 = mm_post([us], wout_o, x2, row(g_mix_post, 1), gt_m[1], tm=tm, name="out_odd")
    h3, hu1, a1, f1, dx4, loss_part = ffn_fwd(1, x3)

    g_srcs, g_lands, g_sems = [], [], []

    def scatter(grads, nm):
        own = place_own(grads, [g.shape[0] // N_DEV for g in grads], me, scatter=True, name=nm.replace("start", "own"))
        sems, srcs, lands, tok = exchange_start(grads, own, scatter=True, name=nm)
        g_srcs.extend(srcs)
        g_lands.extend(lands)
        g_sems.extend(sems)
        return tok[0:1, 0:1]

    def ffn_bwd(i, dxo, xin, h, hu, a, f, g_post):
        dyf, da, dg_post, dgt = post_bwd_mm(dxo, f, g_post, gt_f[i], wdn[i], tm=tm, name=f"ffn_down_bwd_{i}")
        dhg, dhu, dcwg, dcwu, dcbg, dcbu = conv_bwd(da, hu[1], hu[2], hu[0], conv_w[i], rows=conv_rows, wblk=2 * LANES,
                                                    name=f"ffn_conv_bwd_{i}")
        dxin, dg_pre, dsh, dsc = mm_pre_bwd([dhg, dhu], wup[i], xin, dxo, row(g_ffn_pre, i), sc_f[i], tm=tm,
                                            name=f"ffn_up_bwd_{i}")
        g_dn = wgrad([a], dyf, tr=2 * LANES, name=f"wgrad_down_{i}")
        g_up = wgrad([dhg, dhu], h, tr=2 * LANES, name=f"wgrad_up_{i}")
        tok = scatter([g_dn, g_up], f"scatter_start_ffn_{i}")
        return dxin, tok, dict(g_ffn_post=dg_post, g_ffn_pre=dg_pre, gt_f=dgt, sh_f=dsh, sc_f=dsc,
                               ffn_conv_w=jnp.concatenate([dcwg, dcwu], axis=1), ffn_conv_b=jnp.concatenate([dcbg, dcbu], axis=1)[0])

    dx3, tok, sf1 = ffn_bwd(1, dx4, x3, h3, hu1, a1, f1, row(g_ffn_post, 1))
    dy1, dus, dg_mpost1, dgt_m1 = post_bwd_mm(dx3, y1, row(g_mix_post, 1) + tok, gt_m[1], wout_o, tm=tm, name="out_odd_bwd")
    dz1, dws, dbs, dlng, dlnb = sgu_bwd(z1, dus, ln_g, ln_b, sgu_wb, sgu_wtb, bst, rows=tm, name="sgu_bwd")
    dx2, dg_mpre1, dsh_m1, dsc_m1 = mm_pre_bwd([dz1], win_o, x2, dx3, row(g_mix_pre, 1), sc_m[1], tm=tm, name="in_odd_bwd")
    tok = scatter([wgrad([us], dy1, tr=4 * LANES, name="wgrad_out_odd"), wgrad([dz1], h2, tr=4 * LANES, name="wgrad_in_odd")],
                  "scatter_start_mix_1")

    dx1, tok, sf0 = ffn_bwd(0, dx2, x1, h1, hu0, a0, f0, row(g_ffn_post, 0) + tok)
    dy0, dpa, dg_mpost0, dgt_m0 = post_bwd_mm(dx1, y0, row(g_mix_post, 0) + tok, gt_m[0], wout_e, tm=tm, name="out_even_bwd")
    tok = scatter([permute_heads(wgrad(pa, dy0, tr=4 * LANES, name="wgrad_out_even"), inverse=True)], "scatter_start_out_0")
    du, dwp, dps = pool_bwd(u, dpa, w_pool[0], pool_scale + tok, name="pool_bwd")
    dq, dkv, dkvc, dsink = attn_bwd(q, kv, kvc, sink, dpa, cos, sa, sb, name="attn_bwd")
    dz0 = jnp.concatenate([du, dq, dkv], axis=1)
    dzc = jnp.concatenate([jnp.zeros((C, 8 * LANES), BF16), dkvc], axis=1)
    tok = scatter([permute_heads(wgrad([dz0], h0, tr=2 * LANES, extra=(dzc, hc), name="wgrad_in_even"), inverse=True)],
                  "scatter_start_in_0")
    grad_x, dg_mpre0, dsh_m0, dsc_m0 = mm_pre_bwd([dz0], win_e, x, dx1, row(g_mix_pre, 0) + tok, sc_m[0], tm=tm,
                                                  name="in_even_bwd")
    _, dg_mpre0c, dcsh, dcsc = mm_pre_bwd([dkvc], win_e, ctx, None, row(g_mix_pre, 0), csc_m, tm=C,
                                          w_row_off=8 * LANES, name="in_even_ctx_bwd")

    out, ran = {}, {}

    def update(name, lands, transposed):
        w_, m_, v_ = (a.transpose(0, 2, 1) if transposed else a for a in (P[name], M[name], V[name]))
        r = w_.shape[1]
        tr = r // 4 if r % 64 == 0 and r > 256 else r
        res = adamw(w_, m_, v_, [l_.reshape(N_DEV, r, l_.shape[1]) for l_ in lands], tr=tr, name=f"adamw_{name}")
        ran[name] = res[0]
        for kind, val in zip(("grad", "delta", "new_m", "new_v"), res):
            out[(kind, name)] = val.transpose(0, 2, 1) if transposed else val

    zero = jnp.zeros((1, D), F32)
    dmod0 = jnp.concatenate([dsh_m0, dsc_m0, dgt_m0, sf0["sh_f"], sf0["sc_f"], sf0["gt_f"]], axis=1)
    dmodc = jnp.concatenate([dcsh, dcsc, zero, zero, zero, zero], axis=1)
    dmod1 = jnp.concatenate([dsh_m1, dsc_m1, dgt_m1, sf1["sh_f"], sf1["sc_f"], sf1["gt_f"]], axis=1)
    dmods = jnp.concatenate([dmod0, dmodc, dmod1], axis=0)
    dm = dmods.reshape(-1, LANES).astype(BF16)
    d_sems, d_srcs, d_lands, d_tok = exchange_start(
        [dm], place_own([dm], [dm.shape[0]], me, scatter=False, name="dmods_own"), scatter=False, name="dmods_start")
    slots = exchange_wait(g_srcs[:6], g_lands[:6], g_sems[:12], d_tok, scatter=True, name="scatter_wait_early")
    early = slots
    update("w_ffn_down", [slots[4], slots[0]], False)
    update("w_in_odd", [slots[3]], True)
    update("w_out_odd", [slots[2]], False)
    updated = lambda names: [ran[k] for k in names]
    dmods_all = exchange_wait(d_srcs, d_lands, d_sems, updated(("w_out_odd",)), scatter=False, name="dmods_wait")[0]
    dall = lax.dynamic_index_in_dim(dmods_all.astype(F32).reshape(N_DEV, 3, N_DEV, n_ada), me, 2, False)
    g_w_ada, dcc = ada_bwd_mm(silu_c, c_ctx[None, :], dall, w_ada, name="ada_bwd")

    rep = dict(
        c_ctx=dcc[0:1],
        b_ada=jnp.concatenate([dmod0 + dmodc, dmod1]),
        g_mix_pre=jnp.concatenate([dg_mpre0 + dg_mpre0c, dg_mpre1]),
        g_mix_post=jnp.concatenate([dg_mpost0, dg_mpost1]),
        g_ffn_pre=jnp.concatenate([sf0["g_ffn_pre"], sf1["g_ffn_pre"]]),
        g_ffn_post=jnp.concatenate([sf0["g_ffn_post"], sf1["g_ffn_post"]]),
        w_pool=_nat2d(dwp), pool_scale=dps, attn_sink=dsink[:, :N_Q_HEADS],
        sgu_w=_nat2d(dws), sgu_b=dbs[:, :sgu_b.shape[1]].T,
        ffn_conv_b=jnp.stack([sf0["ffn_conv_b"], sf1["ffn_conv_b"]]),
    )
    hi = loss_part.astype(BF16).astype(F32)
    mid = (loss_part - hi).astype(BF16).astype(F32)
    loss_piece = jnp.pad(jnp.concatenate([hi, mid, loss_part - hi - mid], axis=1), ((0, 7), (0, LANES - 3)))
    conv_g = jnp.stack([sf0["ffn_conv_w"], sf1["ffn_conv_w"]]).reshape(2 * 3, N_DEV, n_cw).swapaxes(0, 1)
    shard_full = dict(sgu_ln_g=dlng.reshape(N_DEV, LANES), sgu_ln_b=dlnb.reshape(N_DEV, LANES),
                      ffn_conv_w=jnp.concatenate([_pack_rows(conv_g[d]) for d in range(N_DEV)], axis=0))
    small_names = list(rep) + list(shard_full)
    pieces = [_pack_rows(rep[k]) for k in rep] + list(shard_full.values()) + [loss_piece]
    sizes = [p.shape[0] for p in pieces]
    offs = [sum(sizes[:i]) for i in range(len(sizes))]
    pieces.append(jnp.zeros((-sum(sizes) % 16, LANES), F32))
    gpack = jnp.concatenate(pieces, axis=0).astype(BF16)
    own = place_own([gpack], [gpack.shape[0]], me, scatter=False, name="smallgrad_own")
    s_sems, s_srcs, s_lands, small_tok = exchange_start([gpack], own, scatter=False, name="smallgrad_start")

    slots = exchange_wait(g_srcs[6:], g_lands[6:], g_sems[12:], small_tok, scatter=True, name="scatter_wait_late")
    update("w_in_even", [slots[1]], True)
    update("w_out_even", [slots[0]], False)
    update("w_ffn_up", [early[5], early[1]], True)
    res = adamw(w_ada, m_w_ada, v_w_ada, [g_w_ada[l][None] for l in range(w_ada.shape[0])], tr=D // 4, name="adamw_w_ada")
    ran["w_ada"] = res[0]
    for kind, val in zip(("grad", "delta", "new_m", "new_v"), res):
        out[(kind, "w_ada")] = val

    gpacks = exchange_wait(s_srcs, s_lands, s_sems, updated(("w_ada",)), scatter=False,
                           name="smallgrad_wait")[0]
    per_dev = {k: shard_full[k].shape[0] // N_DEV for k in shard_full}
    params = [(_nat2d(P[k]), _nat2d(M[k]), _nat2d(V[k]), offs[i], per_dev.get(k, 0)) for i, k in enumerate(small_names)]
    res = small_update(gpacks.reshape(N_DEV, -1, LANES), jnp.reshape(me, (1,)).astype(jnp.int32), params, offs[-1], name="adamw_small")
    for i, k in enumerate(small_names):
        for kind, val in zip(("grad", "delta", "new_m", "new_v"), res[4 * i:4 * i + 4]):
            out[(kind, k)] = val.reshape(P[k].shape)
    loss = res[-1][0, 0]

    names = list(P)
    final = [loss, grad_x[None]]
    for kind in ("grad", "delta", "new_m", "new_v"):
        for k in names:
            val = out[(kind, k)]
            final.append(val)
    return tuple(final)
```

```python
import functools
import math

import jax
import jax.numpy as jnp
from jax import lax
from jax.experimental import pallas as pl
from jax.experimental.pallas import tpu as pltpu

F32 = jnp.float32
BF16 = jnp.bfloat16
MESH = pl.DeviceIdType.MESH
N_DEV = 8
LANES = 128
VMEM_LIMIT = 48 * 1024 * 1024
EPS = 1e-6
NEG_INF = -1e30
GRID_W = 64
WINDOW = 128
BLK = 128
HEAD_DIM = 64
N_Q_HEADS = 8
N_KV_HEADS = 2
GQA = N_Q_HEADS // N_KV_HEADS
POOL_WINDOWS = (2, 4, 8, 16)
ROPE_BASE = 10000.0
ROPE_FREQS = HEAD_DIM // 4
PAD = 16
ADAM_LR, ADAM_B1, ADAM_B2, ADAM_EPS, ADAM_WD, ADAM_STEP = 0.001, 0.9, 0.999, 1e-08, 0.01, 10
BC1 = 1.0 - ADAM_B1 ** ADAM_STEP
BC2 = 1.0 - ADAM_B2 ** ADAM_STEP
SQRT_2_OVER_PI = math.sqrt(2.0 / math.pi)
GELU_C = 0.044715


def _cp(sem=None):
    return pltpu.CompilerParams(dimension_semantics=sem, vmem_limit_bytes=VMEM_LIMIT)


def _dot(a, b):
    return jnp.dot(a, b, preferred_element_type=F32)


def _dot_nt(a, b):
    return lax.dot_general(a, b, (((1,), (1,)), ((), ())), preferred_element_type=F32)


def _dot_tn(a, b):
    return lax.dot_general(a, b, (((0,), (0,)), ((), ())), preferred_element_type=F32)


def _rms(x):
    r = lax.rsqrt(jnp.mean(x * x, axis=-1, keepdims=True) + EPS)
    return x * r, r


def _rms_bwd(dn, n, r):
    return r * (dn - n * jnp.mean(dn * n, axis=-1, keepdims=True))


def _colsum(a):
    return jnp.sum(a, axis=0, keepdims=True)


def _rope(x, c, sa, sb):
    return x * c + pltpu.roll(x, LANES - ROPE_FREQS, 1) * sa + pltpu.roll(x, ROPE_FREQS, 1) * sb


def _full(shape):
    return pl.BlockSpec(shape, lambda *_: (0,) * len(shape))


def pre_mm(x, g, sh, sc, wt, *, tm, tn, w_row_off=0, name):
    T, D = x.shape
    n_rows = wt.shape[0] - w_row_off

    def body(x_ref, g_ref, sh_ref, sc_ref, w_ref, h_ref, z_ref):
        n, _ = _rms(x_ref[...])
        h = (n * g_ref[...] * (1.0 + sc_ref[...]) + sh_ref[...]).astype(BF16)
        h_ref[...] = h
        for c0 in range(0, n_rows, tn):
            z_ref[:, c0:c0 + tn] = _dot_nt(h, w_ref[c0:c0 + tn, :]).astype(BF16)

    vec = pl.BlockSpec((1, D), lambda i: (0, 0))
    return pl.pallas_call(
        body, name=name, grid=(T // tm,),
        in_specs=[pl.BlockSpec((tm, D), lambda i: (i, 0)), vec, vec, vec,
                  pl.BlockSpec((n_rows, D), lambda i: (w_row_off // n_rows, 0), pipeline_mode=pl.Buffered(1))],
        out_specs=[pl.BlockSpec((tm, D), lambda i: (i, 0)), pl.BlockSpec((tm, n_rows), lambda i: (i, 0))],
        out_shape=[jax.ShapeDtypeStruct((T, D), BF16), jax.ShapeDtypeStruct((T, n_rows), BF16)],
        compiler_params=_cp(("parallel",)),
    )(x, g, sh, sc, wt)


def inproj_even(x, g, sh, sc, wt, cos, sa, sb, *, tm, name):
    T, D = x.shape
    N = wt.shape[0]

    def body(x_ref, g_ref, sh_ref, sc_ref, w_ref, c_ref, sa_ref, sb_ref, h_ref, u_ref, q_ref, kv_ref):
        n, _ = _rms(x_ref[...])
        h = (n * g_ref[...] * (1.0 + sc_ref[...]) + sh_ref[...]).astype(BF16)
        h_ref[...] = h
        z = _dot_nt(h, w_ref[...])
        u_ref[...] = z[:, :4 * LANES]
        c, a, b = c_ref[...], sa_ref[...], sb_ref[...]
        for s in range(4):
            q_ref[:, s * LANES:(s + 1) * LANES] = _rope(z[:, (4 + s) * LANES:(5 + s) * LANES], c, a, b).astype(BF16)
        kv_ref[:, :LANES] = _rope(z[:, 8 * LANES:9 * LANES], c, a, b).astype(BF16)
        kv_ref[:, LANES:] = z[:, 9 * LANES:].astype(BF16)

    vec = pl.BlockSpec((1, D), lambda i: (0, 0))
    row = lambda w: pl.BlockSpec((tm, w), lambda i: (i, 0))
    return pl.pallas_call(
        body, name=name, grid=(T // tm,),
        in_specs=[row(D), vec, vec, vec, _full((N, D)), row(LANES), row(LANES), row(LANES)],
        out_specs=[row(D), row(4 * LANES), row(4 * LANES), row(2 * LANES)],
        out_shape=[jax.ShapeDtypeStruct((T, D), BF16), jax.ShapeDtypeStruct((T, 4 * LANES), F32),
                   jax.ShapeDtypeStruct((T, 4 * LANES), BF16), jax.ShapeDtypeStruct((T, 2 * LANES), BF16)],
        compiler_params=_cp(("parallel",)),
    )(x, g, sh, sc, wt, cos, sa, sb)


def mm_post(a_parts, w, x, g, gt, *, tm, target=None, name):
    T = a_parts[0].shape[0]
    D = w.shape[1]
    npart = len(a_parts)
    offs = [sum(a_.shape[1] for a_ in a_parts[:p]) for p in range(npart + 1)]
    with_loss = target is not None

    def body(*refs):
        a_refs, (w_ref, x_ref, g_ref, gt_ref) = refs[:npart], refs[npart:npart + 4]
        y = _dot(a_refs[0][...], w_ref[offs[0]:offs[1], :])
        for p in range(1, npart):
            y = y + _dot(a_refs[p][...], w_ref[offs[p]:offs[p + 1], :])
        n, _ = _rms(y)
        xn = x_ref[...] + gt_ref[...] * (n * g_ref[...])
        if not with_loss:
            y_ref, xn_ref = refs[npart + 4:]
            y_ref[...] = y.astype(BF16)
            xn_ref[...] = xn
            return
        t_ref, y_ref, d_ref, l_ref = refs[npart + 4:]
        y_ref[...] = y.astype(BF16)

        @pl.when(pl.program_id(0) == 0)
        def _():
            l_ref[...] = jnp.zeros_like(l_ref)

        e = xn - t_ref[...]
        l_ref[...] += 0.5 * jnp.sum(jnp.mean(e * e, axis=-1, keepdims=True), axis=0, keepdims=True)
        d_ref[...] = e * (1.0 / D)

    vec = pl.BlockSpec((1, D), lambda i: (0, 0))
    row = lambda w_: pl.BlockSpec((tm, w_), lambda i: (i, 0))
    in_specs = [row(a_.shape[1]) for a_ in a_parts] + [_full(w.shape), row(D), vec, vec]
    out_specs = [row(D), row(D)]
    out_shape = [jax.ShapeDtypeStruct((T, D), BF16), jax.ShapeDtypeStruct((T, D), F32)]
    if with_loss:
        in_specs.append(row(D))
        out_specs.append(_full((1, 1)))
        out_shape.append(jax.ShapeDtypeStruct((1, 1), F32))
    return pl.pallas_call(
        body, name=name, grid=(T // tm,), in_specs=in_specs, out_specs=out_specs, out_shape=out_shape,
        compiler_params=_cp(("arbitrary",) if with_loss else ("parallel",)),
    )(*a_parts, w, x, g, gt, *((target,) if with_loss else ()))


def post_bwd_mm(dxn, y, g, gt, w, *, tm, name):
    T, D = y.shape
    K = w.shape[0]

    def body(dxn_ref, y_ref, g_ref, gt_ref, w_ref, dy_ref, da_ref, dg_ref, dgt_ref):
        @pl.when(pl.program_id(0) == 0)
        def _():
            dg_ref[...] = jnp.zeros_like(dg_ref)
            dgt_ref[...] = jnp.zeros_like(dgt_ref)

        d = dxn_ref[...]
        n, r = _rms(y_ref[...].astype(F32))
        g_, gt_ = g_ref[...], gt_ref[...]
        dg_ref[...] += _colsum(d * gt_ * n)
        dgt_ref[...] += _colsum(d * g_ * n)
        dy = _rms_bwd(d * (gt_ * g_), n, r).astype(BF16)
        dy_ref[...] = dy
        da_ref[...] = _dot_nt(dy, w_ref[...]).astype(BF16)

    vec = pl.BlockSpec((1, D), lambda i: (0, 0))
    row = lambda w_: pl.BlockSpec((tm, w_), lambda i: (i, 0))
    return pl.pallas_call(
        body, name=name, grid=(T // tm,),
        in_specs=[row(D), row(D), vec, vec, _full((K, D))],
        out_specs=[row(D), row(K), vec, vec],
        out_shape=[jax.ShapeDtypeStruct((T, D), BF16), jax.ShapeDtypeStruct((T, K), BF16),
                   jax.ShapeDtypeStruct((1, D), F32), jax.ShapeDtypeStruct((1, D), F32)],
        compiler_params=_cp(("arbitrary",)),
    )(dxn, y, g, gt, w)


def mm_pre_bwd(dzs, wt, x, dres, g, sc, *, tm, w_row_off=0, name):
    T, N = dzs[0].shape
    D = x.shape[1]
    npart = len(dzs)
    off = w_row_off // N
    has_res = dres is not None

    def body(*refs):
        dz_refs = refs[:npart]
        w_refs = refs[npart:2 * npart]
        rest = refs[2 * npart:]
        x_ref = rest[0]
        dres_ref = rest[1] if has_res else None
        g_ref, sc_ref, dx_ref, dg_ref, dsh_ref, dsc_ref = rest[1 + has_res:]

        @pl.when(pl.program_id(0) == 0)
        def _():
            dg_ref[...] = jnp.zeros_like(dg_ref)
            dsh_ref[...] = jnp.zeros_like(dsh_ref)
            dsc_ref[...] = jnp.zeros_like(dsc_ref)

        dh = _dot(dz_refs[0][...], w_refs[0][...])
        for p in range(1, npart):
            dh = dh + _dot(dz_refs[p][...], w_refs[p][...])
        n, r = _rms(x_ref[...])
        g_, s1 = g_ref[...], 1.0 + sc_ref[...]
        dsh_ref[...] += _colsum(dh)
        dsc_ref[...] += _colsum(dh * n * g_)
        dg_ref[...] += _colsum(dh * s1 * n)
        dxp = _rms_bwd(dh * (g_ * s1), n, r)
        dx_ref[...] = dxp + dres_ref[...] if has_res else dxp

    vec = pl.BlockSpec((1, D), lambda i: (0, 0))
    row = pl.BlockSpec((tm, D), lambda i: (i, 0))
    w_specs = [pl.BlockSpec((N, D), (lambda i, p=p: (off + p, 0)), pipeline_mode=pl.Buffered(1)) for p in range(npart)]
    res_specs, res_args = ([row], (dres,)) if has_res else ([], ())
    return pl.pallas_call(
        body, name=name, grid=(T // tm,),
        in_specs=[pl.BlockSpec((tm, N), lambda i: (i, 0))] * npart + w_specs + [row] + res_specs + [vec, vec],
        out_specs=[row, vec, vec, vec],
        out_shape=[jax.ShapeDtypeStruct((T, D), F32)] + [jax.ShapeDtypeStruct((1, D), F32)] * 3,
        compiler_params=_cp(("arbitrary",)),
    )(*dzs, *([wt] * npart), x, *res_args, g, sc)


def wgrad(a_parts, b, *, tr, extra=None, name):
    T, R = a_parts[0].shape
    D = b.shape[1]
    npart = len(a_parts)
    nr = R // tr

    def body(*refs):
        a_refs, b_ref = refs[:npart], refs[npart]
        g_ref = refs[-1]
        for p in range(npart):
            @pl.when(pl.program_id(0) // nr == p)
            def _():
                acc = _dot_tn(a_refs[p][...], b_ref[...])
                if extra is not None:
                    acc += _dot_tn(refs[npart + 1][...], refs[npart + 2][...])
                g_ref[...] = acc.astype(BF16)

    in_specs = [pl.BlockSpec((T, tr), (lambda r, p=p: (0, jnp.clip(r - p * nr, 0, nr - 1)))) for p in range(npart)]
    in_specs.append(_full((T, D)))
    args = [*a_parts, b]
    if extra is not None:
        a2, b2 = extra
        in_specs += [pl.BlockSpec((a2.shape[0], tr), lambda r: (0, r)), _full(b2.shape)]
        args += [a2, b2]
    return pl.pallas_call(
        body, name=name, grid=(npart * nr,),
        in_specs=in_specs, out_specs=pl.BlockSpec((tr, D), lambda r: (r, 0)),
        out_shape=jax.ShapeDtypeStruct((npart * R, D), BF16),
        compiler_params=_cp(("parallel",)),
    )(*args)


def _conv_ext(ref, r0, rows, total):
    top = ref[pl.ds(pl.multiple_of(jnp.maximum(r0 - PAD, 0), PAD), PAD), :]
    mid = ref[pl.ds(r0, rows), :]
    bot = ref[pl.ds(pl.multiple_of(jnp.minimum(r0 + rows, total - PAD), PAD), PAD), :]
    top = jnp.where(r0 > 0, top, jnp.zeros_like(top))
    bot = jnp.where(r0 + rows < total, bot, jnp.zeros_like(bot))
    return jnp.concatenate([top, mid, bot], axis=0).astype(F32)


def _shift_rows(a, k):
    return pltpu.roll(a, k % a.shape[0], 0)


def _conv3(x, w, b):
    return w[0:1] * _shift_rows(x, 1) + w[1:2] * x + w[2:3] * _shift_rows(x, -1) + b


def _gate_up_specs(rows_, wblk, nb):
    return [pl.BlockSpec((rows_, wblk), lambda j: (0, j)), pl.BlockSpec((rows_, wblk), lambda j: (0, j + nb))]


def conv_fwd(hu, cw, cb, *, rows, wblk, name):
    L, N2 = hu.shape
    nb = N2 // 2 // wblk
    nchunk = L // rows

    def body(hg_ref, hu_ref, wg_ref, wu_ref, bg_ref, bu_ref, a_ref, s1_ref, s2_ref):
        def chunk(ci, carry):
            r0 = pl.multiple_of(ci * rows, rows)
            gate = _conv3(_conv_ext(hg_ref, r0, rows, L), wg_ref[...], bg_ref[...])[PAD:PAD + rows]
            up = _conv3(_conv_ext(hu_ref, r0, rows, L), wu_ref[...], bu_ref[...])[PAD:PAD + rows]
            sg = jax.nn.sigmoid(gate)
            silu = gate * sg
            at = pl.ds(r0, rows)
            a_ref[at, :] = (silu * up).astype(BF16)
            s1_ref[at, :] = silu.astype(BF16)
            s2_ref[at, :] = (up * (sg + silu * (1.0 - sg))).astype(BF16)
            return carry

        lax.fori_loop(0, nchunk, chunk, 0)

    out = pl.BlockSpec((L, wblk), lambda j: (0, j))
    return pl.pallas_call(
        body, name=name, grid=(nb,),
        in_specs=_gate_up_specs(L, wblk, nb) + _gate_up_specs(3, wblk, nb) + _gate_up_specs(1, wblk, nb),
        out_specs=[out] * 3, out_shape=[jax.ShapeDtypeStruct((L, N2 // 2), BF16)] * 3,
        compiler_params=_cp(("parallel",)),
    )(hu, hu, cw, cw, cb, cb)


def conv_bwd(da, s1, s2, hu, cw, *, rows, wblk, name):
    L, N2 = hu.shape
    F = N2 // 2
    nb = F // wblk
    nchunk = L // rows
    mid = slice(PAD, PAD + rows)

    def body(da_ref, s1_ref, s2_ref, hg_ref, hu_ref, wg_ref, wu_ref, dg_ref, du_ref, dwg_ref, dwu_ref, dbg_ref, dbu_ref):
        for ref in (dwg_ref, dwu_ref, dbg_ref, dbu_ref):
            ref[...] = jnp.zeros_like(ref)

        def half_bwd(x_ref, dh, w_ref, dx_ref, dw_ref, db_ref, r0):
            w = w_ref[...]
            nxt, prv = _shift_rows(dh, -1)[mid], _shift_rows(dh, 1)[mid]
            dhm, xm = dh[mid], x_ref[pl.ds(r0, rows), :].astype(F32)
            dx_ref[pl.ds(r0, rows), :] = (w[0:1] * nxt + w[1:2] * dhm + w[2:3] * prv).astype(BF16)
            db_ref[...] += _colsum(dhm)
            dw_ref[0:1, :] += _colsum(nxt * xm)
            dw_ref[1:2, :] += _colsum(dhm * xm)
            dw_ref[2:3, :] += _colsum(prv * xm)

        def chunk(ci, carry):
            r0 = pl.multiple_of(ci * rows, rows)
            d = _conv_ext(da_ref, r0, rows, L)
            half_bwd(hu_ref, d * _conv_ext(s1_ref, r0, rows, L), wu_ref, du_ref, dwu_ref, dbu_ref, r0)
            half_bwd(hg_ref, d * _conv_ext(s2_ref, r0, rows, L), wg_ref, dg_ref, dwg_ref, dbg_ref, r0)
            return carry

        lax.fori_loop(0, nchunk, chunk, 0)

    blk = lambda r: pl.BlockSpec((r, wblk), lambda j: (0, j))
    return pl.pallas_call(
        body, name=name, grid=(nb,),
        in_specs=[blk(L)] * 3 + _gate_up_specs(L, wblk, nb) + _gate_up_specs(3, wblk, nb),
        out_specs=[blk(L), blk(L), blk(3), blk(3), blk(1), blk(1)],
        out_shape=[jax.ShapeDtypeStruct((L, F), BF16)] * 2 + [jax.ShapeDtypeStruct((3, F), F32)] * 2
        + [jax.ShapeDtypeStruct((1, F), F32)] * 2,
        compiler_params=_cp(("parallel",)),
    )(da, s1, s2, hu, hu, cw, cw)


def _window_sums(pad_ref, w, lead):
    a = pad_ref[...]
    k = 1
    while k < w:
        a = a + _shift_rows(a, -k)
        k *= 2
    return _shift_rows(a, lead) if lead else a


def _pool_counts(L, h):
    t = lax.broadcasted_iota(jnp.int32, (L, 1), 0)
    return (jnp.minimum(t + h, L) - jnp.maximum(t - h, 0)).astype(F32)


def _pooled(u_ref, pad_ref, L, w):
    h = w // 2
    pad_ref[pl.ds(PAD, L), :] = u_ref[...]
    win = _window_sums(pad_ref, w, h)[PAD:PAD + L]
    return win / _pool_counts(L, h) - u_ref[...]


def _zero_pad_edges(pad_ref, L):
    z = jnp.zeros((PAD, LANES), F32)
    pad_ref[pl.ds(0, PAD), :] = z
    pad_ref[pl.ds(PAD + L, PAD), :] = z


def pool_fwd(u, w_pool, pool_scale, *, name):
    L = u.shape[0]

    def body(u_ref, w_ref, ps_ref, p_ref, pad_ref):
        _zero_pad_edges(pad_ref, L)
        for gi, win in enumerate(POOL_WINDOWS):
            @pl.when(pl.program_id(0) == gi)
            def _():
                pooled = _pooled(u_ref, pad_ref, L, win)
                p_ref[...] = (_dot(pooled.astype(BF16), w_ref[...].astype(BF16)) * ps_ref[...]).astype(BF16)

    return pl.pallas_call(
        body, name=name, grid=(len(POOL_WINDOWS),),
        in_specs=[pl.BlockSpec((L, LANES), lambda gi: (0, gi)), pl.BlockSpec((None, LANES, LANES), lambda gi: (gi, 0, 0)),
                  pl.BlockSpec((1, LANES), lambda gi: (0, gi))],
        out_specs=pl.BlockSpec((L, LANES), lambda gi: (0, gi)),
        out_shape=jax.ShapeDtypeStruct((L, 4 * LANES), BF16),
        scratch_shapes=[pltpu.VMEM((L + 2 * PAD, LANES), F32)],
        compiler_params=_cp(("parallel",)),
    )(u, w_pool, pool_scale)


def pool_bwd(u, dpa, w_pool, pool_scale, *, name):
    L = u.shape[0]

    def body(u_ref, dp_ref, w_ref, ps_ref, du_ref, dw_ref, dps_ref, pad_ref):
        _zero_pad_edges(pad_ref, L)
        for gi, win in enumerate(POOL_WINDOWS):
            @pl.when(pl.program_id(0) == gi)
            def _():
                h = win // 2
                wb = w_ref[...].astype(BF16)
                pooled = _pooled(u_ref, pad_ref, L, win).astype(BF16)
                dp = dp_ref[...].astype(F32)
                dps_ref[...] = _colsum(dp * _dot(pooled, wb))
                dy = (dp * ps_ref[...]).astype(BF16)
                dw_ref[...] = _dot_tn(pooled, dy)
                dpooled = _dot_nt(dy, wb)
                pad_ref[pl.ds(PAD, L), :] = dpooled / _pool_counts(L, h)
                du_ref[...] = (_window_sums(pad_ref, win, h - 1)[PAD:PAD + L] - dpooled).astype(BF16)

    return pl.pallas_call(
        body, name=name, grid=(len(POOL_WINDOWS),),
        in_specs=[pl.BlockSpec((L, LANES), lambda gi: (0, gi)), pl.BlockSpec((L, LANES), lambda gi: (0, gi)),
                  pl.BlockSpec((None, LANES, LANES), lambda gi: (gi, 0, 0)), pl.BlockSpec((1, LANES), lambda gi: (0, gi))],
        out_specs=[pl.BlockSpec((L, LANES), lambda gi: (0, gi)), pl.BlockSpec((None, LANES, LANES), lambda gi: (gi, 0, 0)),
                   pl.BlockSpec((1, LANES), lambda gi: (0, gi))],
        out_shape=[jax.ShapeDtypeStruct((L, 4 * LANES), BF16), jax.ShapeDtypeStruct((4, LANES, LANES), F32),
                   jax.ShapeDtypeStruct((1, 4 * LANES), F32)],
        scratch_shapes=[pltpu.VMEM((L + 2 * PAD, LANES), F32)],
        compiler_params=_cp(("parallel",)),
    )(u, dpa, w_pool, pool_scale)


def _attn_probs(qk, band_k, ctx_k, sink_ref, kh, mask4):
    s_loc = jnp.where(mask4, _dot_nt(qk, band_k), NEG_INF)
    s_ctx = _dot_nt(qk, ctx_k)
    sk = jnp.concatenate([jnp.full((BLK, 1), sink_ref[kh * GQA + hh], F32) for hh in range(GQA)], axis=0)
    m = jnp.maximum(jnp.maximum(jnp.max(s_loc, axis=-1, keepdims=True), jnp.max(s_ctx, axis=-1, keepdims=True)), sk)
    e_loc, e_ctx, e_s = jnp.exp(s_loc - m), jnp.exp(s_ctx - m), jnp.exp(sk - m)
    inv = 1.0 / (jnp.sum(e_loc, axis=-1, keepdims=True) + jnp.sum(e_ctx, axis=-1, keepdims=True) + e_s)
    return e_loc * inv, e_ctx * inv, e_s * inv


def _attn_block(n, L):
    start = pl.multiple_of(jnp.clip((n - 1) * BLK, 0, L - 3 * BLK), BLK)
    qpos = n * BLK + lax.broadcasted_iota(jnp.int32, (BLK, 3 * BLK), 0)
    kpos = start + lax.broadcasted_iota(jnp.int32, (BLK, 3 * BLK), 1)
    mask = jnp.abs(kpos - qpos) <= WINDOW
    return start, jnp.concatenate([mask] * GQA, axis=0)


def _stack_slabs(ref, rows=slice(None)):
    return jnp.concatenate([ref[rows, s * LANES:(s + 1) * LANES] for s in range(GQA)], axis=0)


def _kv_head_lanes(kh):
    return (lax.broadcasted_iota(jnp.int32, (1, LANES), 1) // HEAD_DIM) == kh


def permute_heads(w, inverse=False):
    lo, hi = 4 * LANES, 8 * LANES
    mid = w[lo:hi].reshape(*((GQA, N_KV_HEADS) if inverse else (N_KV_HEADS, GQA)), HEAD_DIM, w.shape[1])
    return jnp.concatenate([w[:lo], mid.swapaxes(0, 1).reshape(hi - lo, w.shape[1]), w[hi:]], axis=0)


def attn_fwd(q, kv, kvc, sink, *, qb, name):
    L = q.shape[0]
    C = kvc.shape[0]
    scale = HEAD_DIM ** -0.5

    def body(q_ref, kv_ref, kvc_ref, sink_ref, o_ref):
        kvc_ = kvc_ref[...]
        for b in range(qb):
            rows = slice(b * BLK, (b + 1) * BLK)
            start, mask4 = _attn_block(pl.program_id(0) * qb + b, L)
            band = kv_ref[pl.ds(start, 3 * BLK), :]
            qs = _stack_slabs(q_ref, rows) * scale
            o = jnp.zeros((GQA * BLK, LANES), F32)
            for kh in range(N_KV_HEADS):
                grp = _kv_head_lanes(kh)
                qk = jnp.where(grp, qs, jnp.zeros_like(qs))
                p_loc, p_ctx, _ = _attn_probs(qk, band[:, :LANES], kvc_[:, :LANES], sink_ref, kh, mask4)
                o = o + jnp.where(grp, _dot(p_loc.astype(BF16), band[:, LANES:]) + _dot(p_ctx.astype(BF16), kvc_[:, LANES:]), 0.0)
            for s in range(GQA):
                o_ref[rows, s * LANES:(s + 1) * LANES] = o[s * BLK:(s + 1) * BLK].astype(BF16)

    return pl.pallas_call(
        body, name=name, grid=(L // (qb * BLK),),
        in_specs=[pl.BlockSpec((qb * BLK, 4 * LANES), lambda n: (n, 0)), _full((L, 2 * LANES)), _full((C, 2 * LANES)),
                  pl.BlockSpec(memory_space=pltpu.SMEM)],
        out_specs=pl.BlockSpec((qb * BLK, 4 * LANES), lambda n: (n, 0)),
        out_shape=jax.ShapeDtypeStruct((L, 4 * LANES), BF16),
        compiler_params=_cp(("parallel",)),
    )(q, kv, kvc, sink)


def attn_bwd(q, kv, kvc, sink, dpa, cos, sa, sb, *, name):
    L = q.shape[0]
    C = kvc.shape[0]
    nb = L // BLK
    scale = HEAD_DIM ** -0.5

    def body(q_ref, kv_ref, kvc_ref, sink_ref, do_ref, c_ref, sa_ref, sb_ref, cq_ref, saq_ref, sbq_ref,
             dq_ref, dkv_ref, dkvc_ref, dsink_ref, dkv_acc, dkvc_acc):
        n = pl.program_id(0)

        @pl.when(n == 0)
        def _():
            dkv_acc[...] = jnp.zeros_like(dkv_acc)
            dkvc_acc[...] = jnp.zeros_like(dkvc_acc)
            dsink_ref[...] = jnp.zeros_like(dsink_ref)

        start, mask4 = _attn_block(n, L)
        band = kv_ref[pl.ds(start, 3 * BLK), :]
        kvc_ = kvc_ref[...]
        band_k, band_v, ctx_k, ctx_v = band[:, :LANES], band[:, LANES:], kvc_[:, :LANES], kvc_[:, LANES:]
        qs = _stack_slabs(q_ref) * scale
        dos = _stack_slabs(do_ref)
        lane = lax.broadcasted_iota(jnp.int32, (1, LANES), 1)
        dsink = jnp.zeros((1, LANES), F32)
        dq = jnp.zeros((GQA * BLK, LANES), F32)
        dk = jnp.zeros((LANES, 3 * BLK), F32)
        dv = jnp.zeros((LANES, 3 * BLK), F32)
        dkc = jnp.zeros((LANES, C), F32)
        dvc = jnp.zeros((LANES, C), F32)
        for kh in range(N_KV_HEADS):
            grp = _kv_head_lanes(kh)
            qk = jnp.where(grp, qs, jnp.zeros_like(qs))
            dok = jnp.where(grp, dos, jnp.zeros_like(dos))
            p_loc, p_ctx, p_s = _attn_probs(qk, band_k, ctx_k, sink_ref, kh, mask4)
            dp_loc = _dot_nt(dok, band_v)
            dp_ctx = _dot_nt(dok, ctx_v)
            delta = jnp.sum(p_loc * dp_loc, axis=-1, keepdims=True) + jnp.sum(p_ctx * dp_ctx, axis=-1, keepdims=True)
            ds_loc = (p_loc * (dp_loc - delta)).astype(BF16)
            ds_ctx = (p_ctx * (dp_ctx - delta)).astype(BF16)
            dsk = p_s * delta
            for hh in range(GQA):
                dsink = dsink - jnp.where(lane == kh * GQA + hh, jnp.sum(dsk[hh * BLK:(hh + 1) * BLK], axis=0, keepdims=True), 0.0)
            dq = dq + jnp.where(grp, _dot(ds_loc, band_k) + _dot(ds_ctx, ctx_k), 0.0)
            dk = dk + _dot_tn(qk, ds_loc)
            dv = dv + _dot_tn(dok, p_loc.astype(BF16))
            dkc = dkc + _dot_tn(qk, ds_ctx)
            dvc = dvc + _dot_tn(dok, p_ctx.astype(BF16))
        dsink_ref[...] += dsink
        dkv_acc[:LANES, pl.ds(start, 3 * BLK)] += dk
        dkv_acc[LANES:, pl.ds(start, 3 * BLK)] += dv
        dkvc_acc[:LANES, :] += dkc
        dkvc_acc[LANES:, :] += dvc
        c, a, b = cq_ref[...], -saq_ref[...], -sbq_ref[...]
        for s in range(GQA):
            dq_ref[:, s * LANES:(s + 1) * LANES] = _rope(dq[s * BLK:(s + 1) * BLK] * scale, c, a, b).astype(BF16)

        @pl.when(n == nb - 1)
        def _():
            dkv_ref[:, :LANES] = _rope(dkv_acc[:LANES, :].T, c_ref[...], -sa_ref[...], -sb_ref[...]).astype(BF16)
            dkv_ref[:, LANES:] = dkv_acc[LANES:, :].T.astype(BF16)
            dkvc_ref[...] = dkvc_acc[...].T.astype(BF16)

    blk = lambda w: pl.BlockSpec((BLK, w), lambda n: (n, 0))
    return pl.pallas_call(
        body, name=name, grid=(nb,),
        in_specs=[blk(4 * LANES), _full((L, 2 * LANES)), _full((C, 2 * LANES)), pl.BlockSpec(memory_space=pltpu.SMEM),
                  pl.BlockSpec((BLK, 4 * LANES), lambda n: (n, 1)),
                  _full((L, LANES)), _full((L, LANES)), _full((L, LANES)), blk(LANES), blk(LANES), blk(LANES)],
        out_specs=[blk(4 * LANES), _full((L, 2 * LANES)), _full((C, 2 * LANES)), _full((1, LANES))],
        out_shape=[jax.ShapeDtypeStruct((L, 4 * LANES), BF16), jax.ShapeDtypeStruct((L, 2 * LANES), BF16),
                   jax.ShapeDtypeStruct((C, 2 * LANES), BF16), jax.ShapeDtypeStruct((1, LANES), F32)],
        scratch_shapes=[pltpu.VMEM((2 * LANES, L), F32), pltpu.VMEM((2 * LANES, C), F32)],
        compiler_params=_cp(("arbitrary",)),
    )(q, kv, kvc, sink, dpa, cos, sa, sb, cos, sa, sb)


def _gelu_parts(x):
    th = jnp.tanh(SQRT_2_OVER_PI * (x + GELU_C * x * x * x))
    return 0.5 * x * (1.0 + th), th


def _gelu_grad(x, th):
    return 0.5 * (1.0 + th) + 0.5 * x * (1.0 - th * th) * SQRT_2_OVER_PI * (1.0 + 3.0 * GELU_C * x * x)


def _layernorm(v):
    mu = jnp.mean(v, axis=-1, keepdims=True)
    vc = v - mu
    rstd = lax.rsqrt(jnp.mean(vc * vc, axis=-1, keepdims=True) + EPS)
    return vc * rstd, rstd


def sgu_fwd(z1, ln_g, ln_b, ws, bst, *, rows, name):
    L, W2 = z1.shape
    W = W2 // 2
    ng = W // LANES

    def body(z_ref, g_ref, b_ref, ws_ref, bs_ref, o_ref):
        for c in range(rows // BLK):
            at = slice(c * BLK, (c + 1) * BLK)
            z, _ = _gelu_parts(z_ref[at, :].astype(F32))
            xhat, _ = _layernorm(z[:, W:])
            vln = (xhat * g_ref[...] + b_ref[...]).astype(BF16)
            for gi in range(ng):
                cs = slice(gi * LANES, (gi + 1) * LANES)
                s = _dot(ws_ref[gi], vln[:, cs]) + bs_ref[:, gi:gi + 1]
                o_ref[at, cs] = (z[:, cs] * s).astype(BF16)

    vec = _full((1, W))
    return pl.pallas_call(
        body, name=name, grid=(L // rows,),
        in_specs=[pl.BlockSpec((rows, W2), lambda n: (n, 0)), vec, vec, _full((ng, LANES, LANES)), _full((BLK, ng))],
        out_specs=pl.BlockSpec((rows, W), lambda n: (n, 0)),
        out_shape=jax.ShapeDtypeStruct((L, W), BF16),
        compiler_params=_cp(("parallel",)),
    )(z1, ln_g, ln_b, ws, bst)


def sgu_bwd(z1, dus, ln_g, ln_b, ws, wst, bst, *, rows, name):
    L, W2 = z1.shape
    W = W2 // 2
    ng = W // LANES

    def body(z_ref, d_ref, g_ref, b_ref, ws_ref, wst_ref, bs_ref, dz_ref, dws_ref, dbs_ref, dg_ref, db_ref, dv_scr):
        @pl.when(pl.program_id(0) == 0)
        def _():
            dws_ref[...] = jnp.zeros_like(dws_ref)
            dbs_ref[...] = jnp.zeros_like(dbs_ref)
            dg_ref[...] = jnp.zeros_like(dg_ref)
            db_ref[...] = jnp.zeros_like(db_ref)

        for c in range(rows // BLK):
            at = slice(c * BLK, (c + 1) * BLK)
            zp = z_ref[at, :].astype(F32)
            z, th = _gelu_parts(zp)
            xhat, rstd = _layernorm(z[:, W:])
            vln = (xhat * g_ref[...] + b_ref[...]).astype(BF16)
            d = d_ref[at, :].astype(F32)
            lane = lax.broadcasted_iota(jnp.int32, (1, LANES), 1)
            dbs = jnp.zeros((BLK, LANES), F32)
            dgel = _gelu_grad(zp, th)
            for gi in range(ng):
                cs = slice(gi * LANES, (gi + 1) * LANES)
                s = _dot(ws_ref[gi], vln[:, cs]) + bs_ref[:, gi:gi + 1]
                dz_ref[at, cs] = (d[:, cs] * s * dgel[:, cs]).astype(BF16)
                ds = d[:, cs] * z[:, cs]
                dbs = dbs + jnp.where(lane == gi, jnp.sum(ds, axis=-1, keepdims=True), 0.0)
                dsb = ds.astype(BF16)
                dws_ref[gi] += _dot_nt(dsb, vln[:, cs])
                dv_scr[:, cs] = _dot(wst_ref[gi], dsb)
            dbs_ref[...] += dbs
            dvln = dv_scr[...]
            dg_ref[...] += _colsum(dvln * xhat)
            db_ref[...] += _colsum(dvln)
            dxh = dvln * g_ref[...]
            dv = rstd * (dxh - jnp.mean(dxh, axis=-1, keepdims=True) - xhat * jnp.mean(dxh * xhat, axis=-1, keepdims=True))
            dz_ref[at, W:] = (dv * dgel[:, W:]).astype(BF16)

    vec = _full((1, W))
    return pl.pallas_call(
        body, name=name, grid=(L // rows,),
        in_specs=[pl.BlockSpec((rows, W2), lambda n: (n, 0)), pl.BlockSpec((rows, W), lambda n: (n, 0)), vec, vec,
                  _full((ng, LANES, LANES)), _full((ng, LANES, LANES)), _full((BLK, ng))],
        out_specs=[pl.BlockSpec((rows, W2), lambda n: (n, 0)), _full((ng, LANES, LANES)), _full((BLK, LANES)), vec, vec],
        out_shape=[jax.ShapeDtypeStruct((L, W2), BF16), jax.ShapeDtypeStruct((ng, LANES, LANES), F32),
                   jax.ShapeDtypeStruct((BLK, LANES), F32), jax.ShapeDtypeStruct((1, W), F32), jax.ShapeDtypeStruct((1, W), F32)],
        scratch_shapes=[pltpu.VMEM((BLK, W), F32)],
        compiler_params=_cp(("arbitrary",)),
    )(z1, dus, ln_g, ln_b, ws, wst, bst)


def _adamw_math(w, m, v, g):
    m_ = ADAM_B1 * m + (1.0 - ADAM_B1) * g
    v_ = ADAM_B2 * v + (1.0 - ADAM_B2) * (g * g)
    return -ADAM_LR * ((m_ / BC1) / (jnp.sqrt(v_ / BC2) + ADAM_EPS) + ADAM_WD * w), m_, v_


def adamw(w, m, v, gparts, *, tr, name):
    NL, R, Wd = w.shape
    nr = R // tr

    def body(w_ref, m_ref, v_ref, *rest):
        gp_refs, (g_ref, d_ref, nm_ref, nv_ref) = rest[:NL], rest[NL:]
        for l in range(NL):
            @pl.when(pl.program_id(0) == l)
            def _():
                g = gp_refs[l][0].astype(F32)
                for s in range(1, gp_refs[l].shape[0]):
                    g = g + gp_refs[l][s].astype(F32)
                g_ref[...] = g
                d_ref[...], nm_ref[...], nv_ref[...] = _adamw_math(w_ref[...], m_ref[...], v_ref[...], g)

    row = pl.BlockSpec((None, tr, Wd), lambda l, i: (l, i, 0))
    gspecs = [pl.BlockSpec((gparts[l].shape[0], tr, Wd), (lambda l_, i, l=l: (0, jnp.clip(i + (l_ - l) * nr, 0, nr - 1), 0)))
              for l in range(NL)]
    return pl.pallas_call(
        body, name=name, grid=(NL, nr),
        in_specs=[row, row, row] + gspecs, out_specs=[row] * 4, out_shape=[jax.ShapeDtypeStruct((NL, R, Wd), F32)] * 4,
        compiler_params=_cp(("arbitrary", "arbitrary")),
    )(w, m, v, *gparts)


def small_update(gpacks, me, params, loss_row, *, name):
    n = len(params)

    def body(me_ref, gp_ref, *refs):
        ins, outs, gs_ref = refs[:3 * n], refs[3 * n:-1], refs[-1]
        gs_ref[...] = gp_ref[0].astype(F32)
        for dv in range(1, N_DEV):
            gs_ref[...] += gp_ref[dv].astype(F32)
        for p, (w, _, _, off, per_dev) in enumerate(params):
            w_ref, m_ref, v_ref = ins[3 * p:3 * p + 3]
            g_ref, d_ref, nm_ref, nv_ref = outs[4 * p:4 * p + 4]
            rows, cols = w.shape
            if cols == LANES and rows % 8 == 0 and not per_dev:
                g = gs_ref[off:off + rows, :]
                g_ref[...] = g
                d_ref[...], nm_ref[...], nv_ref[...] = _adamw_math(w_ref[...], m_ref[...], v_ref[...], g)
                continue
            chunks = -(-cols // LANES)
            base = off + me_ref[0] * per_dev if per_dev else off
            for i in range(rows):
                for j in range(chunks):
                    wd = min(LANES, cols - j * LANES)
                    at = (slice(i, i + 1), slice(j * LANES, j * LANES + wd))
                    g = gs_ref[pl.ds(base + i * chunks + j, 1), 0:wd]
                    g_ref[at] = g
                    d_ref[at], nm_ref[at], nv_ref[at] = _adamw_math(w_ref[at], m_ref[at], v_ref[at], g)
        outs[-1][...] = jnp.sum(gs_ref[loss_row:loss_row + 1, :], axis=1, keepdims=True)

    flat = [a for w, m, v, _, _ in params for a in (w, m, v)]
    out_shape = [jax.ShapeDtypeStruct(w.shape, F32) for w, _, _, _, _ in params for _ in range(4)] + [jax.ShapeDtypeStruct((1, 1), F32)]
    return pl.pallas_call(
        body, name=name, grid=(1,),
        in_specs=[pl.BlockSpec(memory_space=pltpu.SMEM), _full(gpacks.shape)] + [_full(a.shape) for a in flat],
        out_specs=[_full(o.shape) for o in out_shape], out_shape=out_shape,
        scratch_shapes=[pltpu.VMEM(gpacks.shape[1:], F32)],
        compiler_params=_cp(("arbitrary",)),
    )(me, gpacks, *flat)


def ada_fwd_mm(cs, w_ada, b_loc, *, name):
    R, D = cs.shape
    nl, _, n = w_ada.shape

    def body(c_ref, w_ref, b_ref, s_ref, m_ref):
        c = c_ref[...]
        s = c * jax.nn.sigmoid(c)
        s_ref[...] = s
        for i in range(nl):
            m_ref[i] = _dot(s.astype(BF16), w_ref[i].astype(BF16)) + b_ref[i:i + 1, :]

    return pl.pallas_call(
        body, name=name, in_specs=[_full((R, D)), _full((nl, D, n)), _full((nl, n))],
        out_specs=[_full((R, D)), _full((nl, R, n))], grid=(1,),
        out_shape=[jax.ShapeDtypeStruct((R, D), F32), jax.ShapeDtypeStruct((nl, R, n), F32)],
        compiler_params=_cp(("arbitrary",)),
    )(cs, w_ada, b_loc)


def ada_bwd_mm(s, c_ctx, dall, w_ada, *, name):
    R, D = s.shape
    nl, _, n = w_ada.shape

    def body(s_ref, cc_ref, d_ref, w_ref, gw_ref, dcc_ref):
        sb = s_ref[...].astype(BF16)
        row = lax.broadcasted_iota(jnp.int32, (R, 1), 0)
        dctx = d_ref[0, 1:2, :]
        for dv in range(1, N_DEV):
            dctx = dctx + d_ref[dv, 1:2, :]
        for i in range(nl):
            dm = jnp.zeros((R, n), F32)
            for dv in range(N_DEV):
                dm = dm + jnp.where(row == dv, d_ref[dv, 2 * i:2 * i + 1, :], 0.0)
            if i == 0:
                dm = dm + jnp.where(row == N_DEV, dctx, 0.0)
            gw_ref[i] = _dot_tn(sb, dm.astype(BF16))
        cc = cc_ref[...]
        sg = jax.nn.sigmoid(cc)
        ds = _dot_nt(jnp.broadcast_to(dctx, (8, n)).astype(BF16), w_ref[0].astype(BF16))
        dcc_ref[...] = ds * (sg * (1.0 + cc * (1.0 - sg)))

    return pl.pallas_call(
        body, name=name, grid=(1,),
        in_specs=[_full((R, D)), _full((1, D)), _full((N_DEV, 3, n)), _full((nl, D, n))],
        out_specs=[_full((nl, D, n)), _full((8, D))],
        out_shape=[jax.ShapeDtypeStruct((nl, D, n), F32), jax.ShapeDtypeStruct((8, D), F32)],
        compiler_params=_cp(("arbitrary",)),
    )(s, c_ctx, dall, w_ada)


def _place():
    x, y, c = lax.axis_index("x"), lax.axis_index("y"), lax.axis_index("c")
    return x, y, c


def _lin(p):
    return 4 * p[0] + 2 * p[1] + p[2]


def all_gather_small(xb, *, name):
    R, W = xb.shape

    def body(x_ref, out_ref, send_sems, recv_sems, local_sem):
        x, y, c = _place()
        me = _lin((x, y, c))
        mine = pltpu.make_async_copy(x_ref, out_ref.at[me], local_sem)
        mine.start()
        copies = []
        for k in range(1, N_DEV):
            peer = (x ^ (k >> 2), y ^ ((k >> 1) & 1), c ^ (k & 1))
            mk = lambda dst, k=k, peer=peer: pltpu.make_async_remote_copy(
                src_ref=x_ref, dst_ref=dst, send_sem=send_sems.at[k - 1], recv_sem=recv_sems.at[k - 1], device_id=peer, device_id_type=MESH)
            mk(out_ref.at[me]).start()
            copies.append(mk(out_ref.at[_lin(peer)]))
        for cp in copies:
            cp.wait_recv()
        for cp in copies:
            cp.wait_send()
        mine.wait()

    vm = pl.BlockSpec(memory_space=pltpu.VMEM)
    return pl.pallas_call(
        body, name=name, in_specs=[vm], out_specs=vm, out_shape=jax.ShapeDtypeStruct((N_DEV, R, W), xb.dtype),
        scratch_shapes=[pltpu.SemaphoreType.DMA((7,)), pltpu.SemaphoreType.DMA((7,)), pltpu.SemaphoreType.DMA],
        compiler_params=pltpu.CompilerParams(vmem_limit_bytes=VMEM_LIMIT),
    )(xb)


HBM_SPEC = pl.BlockSpec(memory_space=pltpu.HBM)
SEM_SPEC = pl.BlockSpec(memory_space=pltpu.SEMAPHORE)
ORDERED_EFFECT = pltpu.SideEffectType.DATAFLOW_SIDE_EFFECTING


def _exchange_copies(srcs, lands, sems, scatter):
    x, y, c = _place()
    me = _lin((x, y, c))
    for j in range(len(srcs)):
        r = lands[j].shape[0] // N_DEV
        block = lambda d, j=j, r=r: pl.ds(pl.multiple_of(d * r, 16), r)
        for k in range(1, N_DEV):
            peer = (x ^ (k >> 2), y ^ ((k >> 1) & 1), c ^ (k & 1))
            src = srcs[j].at[block(_lin(peer)), :] if scatter else srcs[j]
            mk = lambda dst, j=j, k=k, peer=peer, src=src: pltpu.make_async_remote_copy(
                src_ref=src, dst_ref=dst, send_sem=sems[2 * j].at[k - 1], recv_sem=sems[2 * j + 1].at[k - 1],
                device_id=peer, device_id_type=MESH)
            yield mk(lands[j].at[block(me), :]), mk(lands[j].at[block(_lin(peer)), :])


def exchange_start(srcs, lands, *, scatter, name):
    nw = len(srcs)

    def body(*refs):
        for start, _ in _exchange_copies(refs[:nw], refs[nw:2 * nw], refs[2 * nw:4 * nw], scatter):
            start.start()
        refs[-1][...] = jnp.zeros_like(refs[-1])

    thru = [pltpu.HBM(a.shape, a.dtype) for a in (*srcs, *lands)]
    res = pl.pallas_call(
        body, name=name, in_specs=[HBM_SPEC] * (2 * nw),
        out_specs=[SEM_SPEC] * (2 * nw) + [HBM_SPEC] * (2 * nw) + [pl.BlockSpec(memory_space=pltpu.VMEM)],
        out_shape=[pltpu.SemaphoreType.DMA((N_DEV - 1,))] * (2 * nw) + thru + [jax.ShapeDtypeStruct((8, LANES), F32)],
        input_output_aliases={i: 2 * nw + i for i in range(2 * nw)},
        compiler_params=pltpu.CompilerParams(has_side_effects=ORDERED_EFFECT),
    )(*[pltpu.with_memory_space_constraint(a, pltpu.HBM) for a in (*srcs, *lands)])
    return res[:2 * nw], res[2 * nw:3 * nw], res[3 * nw:4 * nw], res[-1]


def exchange_wait(srcs, lands, sems, after, *, scatter, name):
    nw = len(srcs)
    after = list(after) if isinstance(after, (list, tuple)) else [after]

    def body(*refs):
        for _, arrive in _exchange_copies(refs[:nw], refs[nw:2 * nw], refs[2 * nw:4 * nw], scatter):
            arrive.wait_send()
            arrive.wait_recv()

    res = pl.pallas_call(
        body, name=name, in_specs=[HBM_SPEC] * (2 * nw) + [SEM_SPEC] * (2 * nw) + [pl.BlockSpec(memory_space=pl.ANY)] * len(after),
        out_specs=[HBM_SPEC] * (2 * nw), out_shape=[pltpu.HBM(a.shape, a.dtype) for a in (*srcs, *lands)],
        input_output_aliases={i: i for i in range(2 * nw)},
        compiler_params=pltpu.CompilerParams(has_side_effects=ORDERED_EFFECT),
    )(*srcs, *lands, *sems, *after)
    return res[nw:]


def place_own(srcs, rows, me, *, scatter, name):
    nw = len(srcs)
    lands = [lax.empty((N_DEV * r, s_.shape[1]), s_.dtype) for r, s_ in zip(rows, srcs)]

    def body(me_ref, *refs):
        for j in range(nw):
            refs[2 * nw + j][...] = refs[j][...]

    mine = lambda i, me_ref: (me_ref[0], 0)
    src_at = mine if scatter else (lambda i, me_ref: (0, 0))
    blocks = [(r, s_.shape[1]) for r, s_ in zip(rows, srcs)]
    return pl.pallas_call(
        body, name=name,
        grid_spec=pltpu.PrefetchScalarGridSpec(
            num_scalar_prefetch=1, grid=(1,),
            in_specs=[pl.BlockSpec(b_, src_at) for b_ in blocks] + [pl.BlockSpec(memory_space=pl.ANY)] * nw,
            out_specs=[pl.BlockSpec(b_, mine) for b_ in blocks]),
        out_shape=[jax.ShapeDtypeStruct(l_.shape, l_.dtype) for l_ in lands],
        input_output_aliases={1 + nw + j: j for j in range(nw)},
        compiler_params=_cp(("arbitrary",)),
    )(jnp.reshape(me, (1,)).astype(jnp.int32), *srcs, *lands)


def _rope_tables(L):
    t = jnp.arange(L)
    inv = ROPE_BASE ** (-jnp.arange(ROPE_FREQS, dtype=F32) / ROPE_FREQS)
    ar = (t // GRID_W).astype(F32)[:, None] * inv
    ac = (t % GRID_W).astype(F32)[:, None] * inv
    z = jnp.zeros_like(ar)
    cos = jnp.concatenate([jnp.cos(ar), jnp.cos(ar), jnp.cos(ac), jnp.cos(ac)], axis=1)
    sa = jnp.concatenate([-jnp.sin(ar), z, -jnp.sin(ac), z], axis=1)
    sb = jnp.concatenate([z, jnp.sin(ar), z, jnp.sin(ac)], axis=1)
    return tuple(jnp.tile(a, (1, LANES // HEAD_DIM)) for a in (cos, sa, sb))


def _nat2d(a):
    return a.reshape(1, -1) if a.ndim == 1 else a.reshape(-1, a.shape[-1])


def _pack_rows(a):
    rows, cols = a.shape
    chunks = -(-cols // LANES)
    f = jnp.pad(a, ((0, 0), (0, chunks * LANES - cols))).reshape(rows * chunks, LANES)
    return jnp.pad(f, ((0, -f.shape[0] % 8), (0, 0)))


def _rows128(a):
    f = a.reshape(-1)
    n = -(-f.shape[0] // (8 * LANES)) * 8 * LANES
    return jnp.pad(f, (0, n - f.shape[0])).reshape(-1, LANES)


def kernel(x, c, ctx, c_ctx, w_ada, b_ada, g_mix_pre, g_mix_post, g_ffn_pre, g_ffn_post, w_in_even, w_pool, pool_scale, attn_sink, w_out_even, w_in_odd, sgu_ln_g, sgu_ln_b, sgu_w, sgu_b, w_out_odd, w_ffn_up, ffn_conv_w, ffn_conv_b, w_ffn_down, loss_target, m_c_ctx, m_w_ada, m_b_ada, m_g_mix_pre, m_g_mix_post, m_g_ffn_pre, m_g_ffn_post, m_w_in_even, m_w_pool, m_pool_scale, m_attn_sink, m_w_out_even, m_w_in_odd, m_sgu_ln_g, m_sgu_ln_b, m_sgu_w, m_sgu_b, m_w_out_odd, m_w_ffn_up, m_ffn_conv_w, m_ffn_conv_b, m_w_ffn_down, v_c_ctx, v_w_ada, v_b_ada, v_g_mix_pre, v_g_mix_post, v_g_ffn_pre, v_g_ffn_post, v_w_in_even, v_w_pool, v_pool_scale, v_attn_sink, v_w_out_even, v_w_in_odd, v_sgu_ln_g, v_sgu_ln_b, v_sgu_w, v_sgu_b, v_w_out_odd, v_w_ffn_up, v_ffn_conv_w, v_ffn_conv_b, v_w_ffn_down):
    P = dict(c_ctx=c_ctx, w_ada=w_ada, b_ada=b_ada, g_mix_pre=g_mix_pre, g_mix_post=g_mix_post, g_ffn_pre=g_ffn_pre,
             g_ffn_post=g_ffn_post, w_in_even=w_in_even, w_pool=w_pool, pool_scale=pool_scale, attn_sink=attn_sink,
             w_out_even=w_out_even, w_in_odd=w_in_odd, sgu_ln_g=sgu_ln_g, sgu_ln_b=sgu_ln_b, sgu_w=sgu_w, sgu_b=sgu_b,
             w_out_odd=w_out_odd, w_ffn_up=w_ffn_up, ffn_conv_w=ffn_conv_w, ffn_conv_b=ffn_conv_b, w_ffn_down=w_ffn_down)
    M = dict(c_ctx=m_c_ctx, w_ada=m_w_ada, b_ada=m_b_ada, g_mix_pre=m_g_mix_pre, g_mix_post=m_g_mix_post, g_ffn_pre=m_g_ffn_pre,
             g_ffn_post=m_g_ffn_post, w_in_even=m_w_in_even, w_pool=m_w_pool, pool_scale=m_pool_scale, attn_sink=m_attn_sink,
             w_out_even=m_w_out_even, w_in_odd=m_w_in_odd, sgu_ln_g=m_sgu_ln_g, sgu_ln_b=m_sgu_ln_b, sgu_w=m_sgu_w, sgu_b=m_sgu_b,
             w_out_odd=m_w_out_odd, w_ffn_up=m_w_ffn_up, ffn_conv_w=m_ffn_conv_w, ffn_conv_b=m_ffn_conv_b, w_ffn_down=m_w_ffn_down)
    V = dict(c_ctx=v_c_ctx, w_ada=v_w_ada, b_ada=v_b_ada, g_mix_pre=v_g_mix_pre, g_mix_post=v_g_mix_post, g_ffn_pre=v_g_ffn_pre,
             g_ffn_post=v_g_ffn_post, w_in_even=v_w_in_even, w_pool=v_w_pool, pool_scale=v_pool_scale, attn_sink=v_attn_sink,
             w_out_even=v_w_out_even, w_in_odd=v_w_in_odd, sgu_ln_g=v_sgu_ln_g, sgu_ln_b=v_sgu_ln_b, sgu_w=v_sgu_w, sgu_b=v_sgu_b,
             w_out_odd=v_w_out_odd, w_ffn_up=v_w_ffn_up, ffn_conv_w=v_ffn_conv_w, ffn_conv_b=v_ffn_conv_b, w_ffn_down=v_w_ffn_down)

    x = x[0]
    ctx = ctx[0]
    target = loss_target[0]
    L, D = x.shape
    C = ctx.shape[0]
    tm = min(512, L)
    tm_out = min(1024, L)
    conv_rows = min(1024, L)
    me = 4 * lax.axis_index("x") + 2 * lax.axis_index("y") + lax.axis_index("c")
    n_ada = w_ada.shape[2]
    F = w_ffn_down.shape[1] * N_DEV
    half_f = F // 2

    n_cw = ffn_conv_w.shape[2]
    small = jnp.concatenate([_rows128(c), _rows128(sgu_ln_g), _rows128(sgu_ln_b), _rows128(ffn_conv_w)], axis=0)
    small_all = all_gather_small(small, name="gather_small_inputs")
    c_all = small_all[:, :8].reshape(N_DEV, D)
    ln_g = small_all[:, 8].reshape(1, D)
    ln_b = small_all[:, 16].reshape(1, D)
    conv_w = small_all[:, 24:].reshape(N_DEV, -1)[:, :2 * 3 * n_cw].reshape(N_DEV, 2, 3, n_cw)
    conv_w = conv_w.transpose(1, 2, 0, 3).reshape(2, 3, 2 * F)

    cs = jnp.concatenate([c_all, c_ctx[None, :], jnp.zeros((7, D), F32)], axis=0)
    b_loc = lax.dynamic_slice(b_ada, (0, me * n_ada), (2, n_ada))
    silu_c, mods_loc = ada_fwd_mm(cs, w_ada, b_loc, name="ada_fwd")
    mods_all = all_gather_small(mods_loc.reshape(-1, LANES), name="gather_mods")

    shards = [s.astype(BF16) for s in (w_in_even[0].T, w_out_even[0], w_ffn_up[0].T, w_ffn_down[0],
                                       w_in_odd[0].T, w_out_odd[0], w_ffn_up[1].T, w_ffn_down[1])]
    shards, mods_all = lax.optimization_barrier((shards, mods_all))
    w_sems, w_srcs, w_lands, _ = exchange_start(shards, place_own(shards, [s.shape[0] for s in shards], me, scatter=False, name="gather_own"),
                                              scatter=False, name="gather_start")

    def weight(j, after):
        return exchange_wait([w_srcs[j]], [w_lands[j]], w_sems[2 * j:2 * j + 2], after, scatter=False, name=f"gather_wait_{j}")[0]

    mods_all = mods_all.reshape(N_DEV, 2, 16, n_ada).transpose(1, 2, 0, 3).reshape(2, 16, 6 * D)
    mod = lambda i, row: [m_[None, :] for m_ in jnp.split(lax.dynamic_index_in_dim(mods_all[i], row, 0, False), 6)]
    sh_m, sc_m, gt_m, sh_f, sc_f, gt_f = zip(mod(0, me), mod(1, me))
    csh_m, csc_m = mod(0, N_DEV)[:2]

    row = lambda a, i: a[i][None, :]

    cos, sa, sb = _rope_tables(L)
    sink = attn_sink[0]
    bst = sgu_b[0].T
    sgu_wb, sgu_wtb = sgu_w[0].astype(BF16), sgu_w[0].swapaxes(1, 2).astype(BF16)
    wup, wdn = [None, None], [None, None]

    def ffn_fwd(i, xin):
        wup[i] = weight(2 + 4 * i, xin)
        h, hu = pre_mm(xin, row(g_ffn_pre, i), sh_f[i], sc_f[i], wup[i], tm=tm, tn=half_f, name=f"ffn_up_{i}")
        a, s1, s2 = conv_fwd(hu, conv_w[i], ffn_conv_b[i][None, :], rows=conv_rows, wblk=2 * LANES, name=f"ffn_conv_{i}")
        wdn[i] = weight(3 + 4 * i, a)
        res = mm_post([a], wdn[i], xin, row(g_ffn_post, i), gt_f[i], tm=tm, target=target if i == 1 else None, name=f"ffn_down_{i}")
        return (h, (hu, s1, s2), a, *res)

    first_mod, cos, sa, sb = lax.optimization_barrier((sh_m[0], cos, sa, sb))
    win_e = permute_heads(weight(0, first_mod))
    h0, u, q, kv = inproj_even(x, row(g_mix_pre, 0), sh_m[0], sc_m[0], win_e, cos, sa, sb, tm=tm, name="in_even")
    hc, kvc = pre_mm(ctx, row(g_mix_pre, 0), csh_m, csc_m, win_e, tm=C, tn=2 * LANES, w_row_off=8 * LANES, name="in_even_ctx")
    pa = [pool_fwd(u, w_pool[0], pool_scale, name="pool_fwd"), attn_fwd(q, kv, kvc, sink, qb=2, name="attn_fwd")]
    wout_e = permute_heads(weight(1, pa[1]))
    y0, x1 = mm_post(pa, wout_e, x, row(g_mix_post, 0), gt_m[0], tm=tm_out, name="out_even")
    h1, hu0, a0, f0, x2 = ffn_fwd(0, x1)
    win_o = weight(4, x2)
    h2, z1 = pre_mm(x2, row(g_mix_pre, 1), sh_m[1], sc_m[1], win_o, tm=tm, tn=D, name="in_odd")
    us = sgu_fwd(z1, ln_g, ln_b, sgu_wb, bst, rows=tm, name="sgu_fwd")
    wout_o = weight(5, us)
    y1, x3 = mm_post([us], wout_o, x2, row(g_mix_post, 1), gt_m[1], tm=tm_out, name="out_odd")
    h3, hu1, a1, f1, dx4, loss_part = ffn_fwd(1, x3)

    g_srcs, g_lands, g_sems = [], [], []

    def scatter(grads, nm):
        own = place_own(grads, [g.shape[0] // N_DEV for g in grads], me, scatter=True, name=nm.replace("start", "own"))
        sems, srcs, lands, tok = exchange_start(grads, own, scatter=True, name=nm)
        g_srcs.extend(srcs)
        g_lands.extend(lands)
        g_sems.extend(sems)
        return tok[0:1, 0:1]

    def ffn_bwd(i, dxo, xin, h, hu, a, f, g_post):
        dyf, da, dg_post, dgt = post_bwd_mm(dxo, f, g_post, gt_f[i], wdn[i], tm=tm, name=f"ffn_down_bwd_{i}")
        dhg, dhu, dcwg, dcwu, dcbg, dcbu = conv_bwd(da, hu[1], hu[2], hu[0], conv_w[i], rows=conv_rows, wblk=2 * LANES,
                                                    name=f"ffn_conv_bwd_{i}")
        dxin, dg_pre, dsh, dsc = mm_pre_bwd([dhg, dhu], wup[i], xin, dxo, row(g_ffn_pre, i), sc_f[i], tm=tm,
                                            name=f"ffn_up_bwd_{i}")
        g_dn = wgrad([a], dyf, tr=2 * LANES, name=f"wgrad_down_{i}")
        g_up = wgrad([dhg, dhu], h, tr=2 * LANES, name=f"wgrad_up_{i}")
        tok = scatter([g_dn, g_up], f"scatter_start_ffn_{i}")
        return dxin, tok, dict(g_ffn_post=dg_post, g_ffn_pre=dg_pre, gt_f=dgt, sh_f=dsh, sc_f=dsc,
                               ffn_conv_w=jnp.concatenate([dcwg, dcwu], axis=1), ffn_conv_b=jnp.concatenate([dcbg, dcbu], axis=1)[0])

    dx3, tok, sf1 = ffn_bwd(1, dx4, x3, h3, hu1, a1, f1, row(g_ffn_post, 1))
    dy1, dus, dg_mpost1, dgt_m1 = post_bwd_mm(dx3, y1, row(g_mix_post, 1) + tok, gt_m[1], wout_o, tm=tm_out, name="out_odd_bwd")
    dz1, dws, dbs, dlng, dlnb = sgu_bwd(z1, dus, ln_g, ln_b, sgu_wb, sgu_wtb, bst, rows=tm, name="sgu_bwd")
    dx2, dg_mpre1, dsh_m1, dsc_m1 = mm_pre_bwd([dz1], win_o, x2, dx3, row(g_mix_pre, 1), sc_m[1], tm=tm, name="in_odd_bwd")
    tok = scatter([wgrad([us], dy1, tr=4 * LANES, name="wgrad_out_odd"), wgrad([dz1], h2, tr=4 * LANES, name="wgrad_in_odd")],
                  "scatter_start_mix_1")

    dx1, tok, sf0 = ffn_bwd(0, dx2, x1, h1, hu0, a0, f0, row(g_ffn_post, 0) + tok)
    dy0, dpa, dg_mpost0, dgt_m0 = post_bwd_mm(dx1, y0, row(g_mix_post, 0) + tok, gt_m[0], wout_e, tm=tm_out, name="out_even_bwd")
    tok = scatter([permute_heads(wgrad(pa, dy0, tr=4 * LANES, name="wgrad_out_even"), inverse=True)], "scatter_start_out_0")
    du, dwp, dps = pool_bwd(u, dpa, w_pool[0], pool_scale + tok, name="pool_bwd")
    dq, dkv, dkvc, dsink = attn_bwd(q, kv, kvc, sink, dpa, cos, sa, sb, name="attn_bwd")
    dz0 = jnp.concatenate([du, dq, dkv], axis=1)
    dzc = jnp.concatenate([jnp.zeros((C, 8 * LANES), BF16), dkvc], axis=1)
    tok = scatter([permute_heads(wgrad([dz0], h0, tr=2 * LANES, extra=(dzc, hc), name="wgrad_in_even"), inverse=True)],
                  "scatter_start_in_0")
    grad_x, dg_mpre0, dsh_m0, dsc_m0 = mm_pre_bwd([dz0], win_e, x, dx1, row(g_mix_pre, 0) + tok, sc_m[0], tm=tm,
                                                  name="in_even_bwd")
    _, dg_mpre0c, dcsh, dcsc = mm_pre_bwd([dkvc], win_e, ctx, None, row(g_mix_pre, 0), csc_m, tm=C,
                                          w_row_off=8 * LANES, name="in_even_ctx_bwd")

    out, ran = {}, {}

    def update(name, lands, transposed):
        w_, m_, v_ = (a.transpose(0, 2, 1) if transposed else a for a in (P[name], M[name], V[name]))
        r = w_.shape[1]
        tr = r // 4 if r % 64 == 0 and r > 256 else r
        res = adamw(w_, m_, v_, [l_.reshape(N_DEV, r, l_.shape[1]) for l_ in lands], tr=tr, name=f"adamw_{name}")
        ran[name] = res[0]
        for kind, val in zip(("grad", "delta", "new_m", "new_v"), res):
            out[(kind, name)] = val.transpose(0, 2, 1) if transposed else val

    zero = jnp.zeros((1, D), F32)
    dmod0 = jnp.concatenate([dsh_m0, dsc_m0, dgt_m0, sf0["sh_f"], sf0["sc_f"], sf0["gt_f"]], axis=1)
    dmodc = jnp.concatenate([dcsh, dcsc, zero, zero, zero, zero], axis=1)
    dmod1 = jnp.concatenate([dsh_m1, dsc_m1, dgt_m1, sf1["sh_f"], sf1["sc_f"], sf1["gt_f"]], axis=1)
    dmods = jnp.concatenate([dmod0, dmodc, dmod1], axis=0)
    dm = dmods.reshape(-1, LANES).astype(BF16)
    d_sems, d_srcs, d_lands, d_tok = exchange_start(
        [dm], place_own([dm], [dm.shape[0]], me, scatter=False, name="dmods_own"), scatter=False, name="dmods_start")
    slots = exchange_wait(g_srcs[:6], g_lands[:6], g_sems[:12], d_tok, scatter=True, name="scatter_wait_early")
    early = slots
    update("w_ffn_down", [slots[4], slots[0]], False)
    update("w_in_odd", [slots[3]], True)
    update("w_out_odd", [slots[2]], False)
    updated = lambda names: [ran[k] for k in names]
    dmods_all = exchange_wait(d_srcs, d_lands, d_sems, updated(("w_out_odd",)), scatter=False, name="dmods_wait")[0]
    dall = lax.dynamic_index_in_dim(dmods_all.astype(F32).reshape(N_DEV, 3, N_DEV, n_ada), me, 2, False)
    g_w_ada, dcc = ada_bwd_mm(silu_c, c_ctx[None, :], dall, w_ada, name="ada_bwd")

    rep = dict(
        c_ctx=dcc[0:1],
        b_ada=jnp.concatenate([dmod0 + dmodc, dmod1]),
        g_mix_pre=jnp.concatenate([dg_mpre0 + dg_mpre0c, dg_mpre1]),
        g_mix_post=jnp.concatenate([dg_mpost0, dg_mpost1]),
        g_ffn_pre=jnp.concatenate([sf0["g_ffn_pre"], sf1["g_ffn_pre"]]),
        g_ffn_post=jnp.concatenate([sf0["g_ffn_post"], sf1["g_ffn_post"]]),
        w_pool=_nat2d(dwp), pool_scale=dps, attn_sink=dsink[:, :N_Q_HEADS],
        sgu_w=_nat2d(dws), sgu_b=dbs[:, :sgu_b.shape[1]].T,
        ffn_conv_b=jnp.stack([sf0["ffn_conv_b"], sf1["ffn_conv_b"]]),
    )
    hi = loss_part.astype(BF16).astype(F32)
    mid = (loss_part - hi).astype(BF16).astype(F32)
    loss_piece = jnp.pad(jnp.concatenate([hi, mid, loss_part - hi - mid], axis=1), ((0, 7), (0, LANES - 3)))
    conv_g = jnp.stack([sf0["ffn_conv_w"], sf1["ffn_conv_w"]]).reshape(2 * 3, N_DEV, n_cw).swapaxes(0, 1)
    shard_full = dict(sgu_ln_g=dlng.reshape(N_DEV, LANES), sgu_ln_b=dlnb.reshape(N_DEV, LANES),
                      ffn_conv_w=jnp.concatenate([_pack_rows(conv_g[d]) for d in range(N_DEV)], axis=0))
    small_names = list(rep) + list(shard_full)
    pieces = [_pack_rows(rep[k]) for k in rep] + list(shard_full.values()) + [loss_piece]
    sizes = [p.shape[0] for p in pieces]
    offs = [sum(sizes[:i]) for i in range(len(sizes))]
    pieces.append(jnp.zeros((-sum(sizes) % 16, LANES), F32))
    gpack = jnp.concatenate(pieces, axis=0).astype(BF16)
    own = place_own([gpack], [gpack.shape[0]], me, scatter=False, name="smallgrad_own")
    s_sems, s_srcs, s_lands, small_tok = exchange_start([gpack], own, scatter=False, name="smallgrad_start")

    slots = exchange_wait(g_srcs[6:], g_lands[6:], g_sems[12:], small_tok, scatter=True, name="scatter_wait_late")
    update("w_in_even", [slots[1]], True)
    update("w_out_even", [slots[0]], False)
    update("w_ffn_up", [early[5], early[1]], True)
    res = adamw(w_ada, m_w_ada, v_w_ada, [g_w_ada[l][None] for l in range(w_ada.shape[0])], tr=D // 4, name="adamw_w_ada")
    ran["w_ada"] = res[0]
    for kind, val in zip(("grad", "delta", "new_m", "new_v"), res):
        out[(kind, "w_ada")] = val

    gpacks = exchange_wait(s_srcs, s_lands, s_sems, updated(("w_ada",)), scatter=False,
                           name="smallgrad_wait")[0]
    per_dev = {k: shard_full[k].shape[0] // N_DEV for k in shard_full}
    params = [(_nat2d(P[k]), _nat2d(M[k]), _nat2d(V[k]), offs[i], per_dev.get(k, 0)) for i, k in enumerate(small_names)]
    res = small_update(gpacks.reshape(N_DEV, -1, LANES), jnp.reshape(me, (1,)).astype(jnp.int32), params, offs[-1], name="adamw_small")
    for i, k in enumerate(small_names):
        for kind, val in zip(("grad", "delta", "new_m", "new_v"), res[4 * i:4 * i + 4]):
            out[(kind, k)] = val.reshape(P[k].shape)
    loss = res[-1][0, 0]

    names = list(P)
    final = [loss, grad_x[None]]
    for kind in ("grad", "delta", "new_m", "new_v"):
        for k in names:
            val = out[(kind, k)]
            final.append(val)
    return tuple(final)
```

```python
import functools
import math

import jax
import jax.numpy as jnp
from jax import lax
from jax.experimental import pallas as pl
from jax.experimental.pallas import tpu as pltpu

F32 = jnp.float32
BF16 = jnp.bfloat16
MESH = pl.DeviceIdType.MESH
N_DEV = 8
LANES = 128
VMEM_LIMIT = 48 * 1024 * 1024
EPS = 1e-6
NEG_INF = -1e30
GRID_W = 64
WINDOW = 128
BLK = 128
HEAD_DIM = 64
N_Q_HEADS = 8
N_KV_HEADS = 2
GQA = N_Q_HEADS // N_KV_HEADS
POOL_WINDOWS = (2, 4, 8, 16)
ROPE_BASE = 10000.0
ROPE_FREQS = HEAD_DIM // 4
PAD = 16
ADAM_LR, ADAM_B1, ADAM_B2, ADAM_EPS, ADAM_WD, ADAM_STEP = 0.001, 0.9, 0.999, 1e-08, 0.01, 10
BC1 = 1.0 - ADAM_B1 ** ADAM_STEP
BC2 = 1.0 - ADAM_B2 ** ADAM_STEP
SQRT_2_OVER_PI = math.sqrt(2.0 / math.pi)
GELU_C = 0.044715


def _cp(sem=None):
    return pltpu.CompilerParams(dimension_semantics=sem, vmem_limit_bytes=VMEM_LIMIT)


def _dot(a, b):
    return jnp.dot(a, b, preferred_element_type=F32)


def _dot_nt(a, b):
    return lax.dot_general(a, b, (((1,), (1,)), ((), ())), preferred_element_type=F32)


def _dot_tn(a, b):
    return lax.dot_general(a, b, (((0,), (0,)), ((), ())), preferred_element_type=F32)


def _rms(x):
    r = lax.rsqrt(jnp.mean(x * x, axis=-1, keepdims=True) + EPS)
    return x * r, r


def _rms_bwd(dn, n, r):
    return r * (dn - n * jnp.mean(dn * n, axis=-1, keepdims=True))


def _colsum(a):
    return jnp.sum(a, axis=0, keepdims=True)


def _rope(x, c, sa, sb):
    return x * c + pltpu.roll(x, LANES - ROPE_FREQS, 1) * sa + pltpu.roll(x, ROPE_FREQS, 1) * sb


def _full(shape):
    return pl.BlockSpec(shape, lambda *_: (0,) * len(shape))


def pre_mm(x, g, sh, sc, wt, *, tm, tn, w_row_off=0, name):
    T, D = x.shape
    n_rows = wt.shape[0] - w_row_off

    def body(x_ref, g_ref, sh_ref, sc_ref, w_ref, h_ref, z_ref):
        n, _ = _rms(x_ref[...])
        h = (n * g_ref[...] * (1.0 + sc_ref[...]) + sh_ref[...]).astype(BF16)
        h_ref[...] = h
        for c0 in range(0, n_rows, tn):
            z_ref[:, c0:c0 + tn] = _dot_nt(h, w_ref[c0:c0 + tn, :]).astype(BF16)

    vec = pl.BlockSpec((1, D), lambda i: (0, 0))
    return pl.pallas_call(
        body, name=name, grid=(T // tm,),
        in_specs=[pl.BlockSpec((tm, D), lambda i: (i, 0)), vec, vec, vec,
                  pl.BlockSpec((n_rows, D), lambda i: (w_row_off // n_rows, 0), pipeline_mode=pl.Buffered(1))],
        out_specs=[pl.BlockSpec((tm, D), lambda i: (i, 0)), pl.BlockSpec((tm, n_rows), lambda i: (i, 0))],
        out_shape=[jax.ShapeDtypeStruct((T, D), BF16), jax.ShapeDtypeStruct((T, n_rows), BF16)],
        compiler_params=_cp(("parallel",)),
    )(x, g, sh, sc, wt)


def inproj_even(x, g, sh, sc, wt, cos, sa, sb, *, tm, name):
    T, D = x.shape
    N = wt.shape[0]

    def body(x_ref, g_ref, sh_ref, sc_ref, w_ref, c_ref, sa_ref, sb_ref, h_ref, u_ref, q_ref, kv_ref):
        n, _ = _rms(x_ref[...])
        h = (n * g_ref[...] * (1.0 + sc_ref[...]) + sh_ref[...]).astype(BF16)
        h_ref[...] = h
        z = _dot_nt(h, w_ref[...])
        u_ref[...] = z[:, :4 * LANES]
        c, a, b = c_ref[...], sa_ref[...], sb_ref[...]
        for s in range(4):
            q_ref[:, s * LANES:(s + 1) * LANES] = _rope(z[:, (4 + s) * LANES:(5 + s) * LANES], c, a, b).astype(BF16)
        kv_ref[:, :LANES] = _rope(z[:, 8 * LANES:9 * LANES], c, a, b).astype(BF16)
        kv_ref[:, LANES:] = z[:, 9 * LANES:].astype(BF16)

    vec = pl.BlockSpec((1, D), lambda i: (0, 0))
    row = lambda w: pl.BlockSpec((tm, w), lambda i: (i, 0))
    return pl.pallas_call(
        body, name=name, grid=(T // tm,),
        in_specs=[row(D), vec, vec, vec, _full((N, D)), row(LANES), row(LANES), row(LANES)],
        out_specs=[row(D), row(4 * LANES), row(4 * LANES), row(2 * LANES)],
        out_shape=[jax.ShapeDtypeStruct((T, D), BF16), jax.ShapeDtypeStruct((T, 4 * LANES), F32),
                   jax.ShapeDtypeStruct((T, 4 * LANES), BF16), jax.ShapeDtypeStruct((T, 2 * LANES), BF16)],
        compiler_params=_cp(("parallel",)),
    )(x, g, sh, sc, wt, cos, sa, sb)


def mm_post(a_parts, w, x, g, gt, *, tm, target=None, name):
    T = a_parts[0].shape[0]
    D = w.shape[1]
    npart = len(a_parts)
    offs = [sum(a_.shape[1] for a_ in a_parts[:p]) for p in range(npart + 1)]
    with_loss = target is not None

    def body(*refs):
        a_refs, (w_ref, x_ref, g_ref, gt_ref) = refs[:npart], refs[npart:npart + 4]
        y = _dot(a_refs[0][...], w_ref[offs[0]:offs[1], :])
        for p in range(1, npart):
            y = y + _dot(a_refs[p][...], w_ref[offs[p]:offs[p + 1], :])
        n, _ = _rms(y)
        xn = x_ref[...] + gt_ref[...] * (n * g_ref[...])
        if not with_loss:
            y_ref, xn_ref = refs[npart + 4:]
            y_ref[...] = y.astype(BF16)
            xn_ref[...] = xn
            return
        t_ref, y_ref, d_ref, l_ref = refs[npart + 4:]
        y_ref[...] = y.astype(BF16)

        @pl.when(pl.program_id(0) == 0)
        def _():
            l_ref[...] = jnp.zeros_like(l_ref)

        e = xn - t_ref[...]
        l_ref[...] += 0.5 * jnp.sum(jnp.mean(e * e, axis=-1, keepdims=True), axis=0, keepdims=True)
        d_ref[...] = e * (1.0 / D)

    vec = pl.BlockSpec((1, D), lambda i: (0, 0))
    row = lambda w_: pl.BlockSpec((tm, w_), lambda i: (i, 0))
    in_specs = [row(a_.shape[1]) for a_ in a_parts] + [_full(w.shape), row(D), vec, vec]
    out_specs = [row(D), row(D)]
    out_shape = [jax.ShapeDtypeStruct((T, D), BF16), jax.ShapeDtypeStruct((T, D), F32)]
    if with_loss:
        in_specs.append(row(D))
        out_specs.append(_full((1, 1)))
        out_shape.append(jax.ShapeDtypeStruct((1, 1), F32))
    return pl.pallas_call(
        body, name=name, grid=(T // tm,), in_specs=in_specs, out_specs=out_specs, out_shape=out_shape,
        compiler_params=_cp(("arbitrary",) if with_loss else ("parallel",)),
    )(*a_parts, w, x, g, gt, *((target,) if with_loss else ()))


def post_bwd_mm(dxn, y, g, gt, w, *, tm, name):
    T, D = y.shape
    K = w.shape[0]

    def body(dxn_ref, y_ref, g_ref, gt_ref, w_ref, dy_ref, da_ref, dg_ref, dgt_ref):
        @pl.when(pl.program_id(0) == 0)
        def _():
            dg_ref[...] = jnp.zeros_like(dg_ref)
            dgt_ref[...] = jnp.zeros_like(dgt_ref)

        d = dxn_ref[...]
        n, r = _rms(y_ref[...].astype(F32))
        g_, gt_ = g_ref[...], gt_ref[...]
        dg_ref[...] += _colsum(d * gt_ * n)
        dgt_ref[...] += _colsum(d * g_ * n)
        dy = _rms_bwd(d * (gt_ * g_), n, r).astype(BF16)
        dy_ref[...] = dy
        da_ref[...] = _dot_nt(dy, w_ref[...]).astype(BF16)

    vec = pl.BlockSpec((1, D), lambda i: (0, 0))
    row = lambda w_: pl.BlockSpec((tm, w_), lambda i: (i, 0))
    return pl.pallas_call(
        body, name=name, grid=(T // tm,),
        in_specs=[row(D), row(D), vec, vec, _full((K, D))],
        out_specs=[row(D), row(K), vec, vec],
        out_shape=[jax.ShapeDtypeStruct((T, D), BF16), jax.ShapeDtypeStruct((T, K), BF16),
                   jax.ShapeDtypeStruct((1, D), F32), jax.ShapeDtypeStruct((1, D), F32)],
        compiler_params=_cp(("arbitrary",)),
    )(dxn, y, g, gt, w)


def mm_pre_bwd(dzs, wt, x, dres, g, sc, *, tm, w_row_off=0, name):
    T, N = dzs[0].shape
    D = x.shape[1]
    npart = len(dzs)
    off = w_row_off // N
    has_res = dres is not None

    def body(*refs):
        dz_refs = refs[:npart]
        w_refs = refs[npart:2 * npart]
        rest = refs[2 * npart:]
        x_ref = rest[0]
        dres_ref = rest[1] if has_res else None
        g_ref, sc_ref, dx_ref, dg_ref, dsh_ref, dsc_ref = rest[1 + has_res:]

        @pl.when(pl.program_id(0) == 0)
        def _():
            dg_ref[...] = jnp.zeros_like(dg_ref)
            dsh_ref[...] = jnp.zeros_like(dsh_ref)
            dsc_ref[...] = jnp.zeros_like(dsc_ref)

        dh = _dot(dz_refs[0][...], w_refs[0][...])
        for p in range(1, npart):
            dh = dh + _dot(dz_refs[p][...], w_refs[p][...])
        n, r = _rms(x_ref[...])
        g_, s1 = g_ref[...], 1.0 + sc_ref[...]
        dsh_ref[...] += _colsum(dh)
        dsc_ref[...] += _colsum(dh * n * g_)
        dg_ref[...] += _colsum(dh * s1 * n)
        dxp = _rms_bwd(dh * (g_ * s1), n, r)
        dx_ref[...] = dxp + dres_ref[...] if has_res else dxp

    vec = pl.BlockSpec((1, D), lambda i: (0, 0))
    row = pl.BlockSpec((tm, D), lambda i: (i, 0))
    w_specs = [pl.BlockSpec((N, D), (lambda i, p=p: (off + p, 0)), pipeline_mode=pl.Buffered(1)) for p in range(npart)]
    res_specs, res_args = ([row], (dres,)) if has_res else ([], ())
    return pl.pallas_call(
        body, name=name, grid=(T // tm,),
        in_specs=[pl.BlockSpec((tm, N), lambda i: (i, 0))] * npart + w_specs + [row] + res_specs + [vec, vec],
        out_specs=[row, vec, vec, vec],
        out_shape=[jax.ShapeDtypeStruct((T, D), F32)] + [jax.ShapeDtypeStruct((1, D), F32)] * 3,
        compiler_params=_cp(("arbitrary",)),
    )(*dzs, *([wt] * npart), x, *res_args, g, sc)


def wgrad(a_parts, b, *, tr, extra=None, name):
    T, R = a_parts[0].shape
    D = b.shape[1]
    npart = len(a_parts)
    nr = R // tr

    def body(*refs):
        a_refs, b_ref = refs[:npart], refs[npart]
        g_ref = refs[-1]
        for p in range(npart):
            @pl.when(pl.program_id(0) // nr == p)
            def _():
                acc = _dot_tn(a_refs[p][...], b_ref[...])
                if extra is not None:
                    acc += _dot_tn(refs[npart + 1][...], refs[npart + 2][...])
                g_ref[...] = acc.astype(BF16)

    in_specs = [pl.BlockSpec((T, tr), (lambda r, p=p: (0, jnp.clip(r - p * nr, 0, nr - 1)))) for p in range(npart)]
    in_specs.append(_full((T, D)))
    args = [*a_parts, b]
    if extra is not None:
        a2, b2 = extra
        in_specs += [pl.BlockSpec((a2.shape[0], tr), lambda r: (0, r)), _full(b2.shape)]
        args += [a2, b2]
    return pl.pallas_call(
        body, name=name, grid=(npart * nr,),
        in_specs=in_specs, out_specs=pl.BlockSpec((tr, D), lambda r: (r, 0)),
        out_shape=jax.ShapeDtypeStruct((npart * R, D), BF16),
        compiler_params=_cp(("parallel",)),
    )(*args)


def _conv_ext(ref, r0, rows, total):
    top = ref[pl.ds(pl.multiple_of(jnp.maximum(r0 - PAD, 0), PAD), PAD), :]
    mid = ref[pl.ds(r0, rows), :]
    bot = ref[pl.ds(pl.multiple_of(jnp.minimum(r0 + rows, total - PAD), PAD), PAD), :]
    top = jnp.where(r0 > 0, top, jnp.zeros_like(top))
    bot = jnp.where(r0 + rows < total, bot, jnp.zeros_like(bot))
    return jnp.concatenate([top, mid, bot], axis=0).astype(F32)


def _shift_rows(a, k):
    return pltpu.roll(a, k % a.shape[0], 0)


def _conv3(x, w, b):
    return w[0:1] * _shift_rows(x, 1) + w[1:2] * x + w[2:3] * _shift_rows(x, -1) + b


def _gate_up_specs(rows_, wblk, nb):
    return [pl.BlockSpec((rows_, wblk), lambda j: (0, j)), pl.BlockSpec((rows_, wblk), lambda j: (0, j + nb))]


def conv_fwd(hu, cw, cb, *, rows, wblk, name):
    L, N2 = hu.shape
    nb = N2 // 2 // wblk
    nchunk = L // rows

    def body(hg_ref, hu_ref, wg_ref, wu_ref, bg_ref, bu_ref, a_ref, s1_ref, s2_ref):
        def chunk(ci, carry):
            r0 = pl.multiple_of(ci * rows, rows)
            gate = _conv3(_conv_ext(hg_ref, r0, rows, L), wg_ref[...], bg_ref[...])[PAD:PAD + rows]
            up = _conv3(_conv_ext(hu_ref, r0, rows, L), wu_ref[...], bu_ref[...])[PAD:PAD + rows]
            sg = jax.nn.sigmoid(gate)
            silu = gate * sg
            at = pl.ds(r0, rows)
            a_ref[at, :] = (silu * up).astype(BF16)
            s1_ref[at, :] = silu.astype(BF16)
            s2_ref[at, :] = (up * (sg + silu * (1.0 - sg))).astype(BF16)
            return carry

        lax.fori_loop(0, nchunk, chunk, 0)

    out = pl.BlockSpec((L, wblk), lambda j: (0, j))
    return pl.pallas_call(
        body, name=name, grid=(nb,),
        in_specs=_gate_up_specs(L, wblk, nb) + _gate_up_specs(3, wblk, nb) + _gate_up_specs(1, wblk, nb),
        out_specs=[out] * 3, out_shape=[jax.ShapeDtypeStruct((L, N2 // 2), BF16)] * 3,
        compiler_params=_cp(("parallel",)),
    )(hu, hu, cw, cw, cb, cb)


def conv_bwd(da, s1, s2, hu, cw, *, rows, wblk, name):
    L, N2 = hu.shape
    F = N2 // 2
    nb = F // wblk
    nchunk = L // rows
    mid = slice(PAD, PAD + rows)

    def body(da_ref, s1_ref, s2_ref, hg_ref, hu_ref, wg_ref, wu_ref, dg_ref, du_ref, dwg_ref, dwu_ref, dbg_ref, dbu_ref):
        for ref in (dwg_ref, dwu_ref, dbg_ref, dbu_ref):
            ref[...] = jnp.zeros_like(ref)

        def half_bwd(x_ref, dh, w_ref, dx_ref, dw_ref, db_ref, r0):
            w = w_ref[...]
            nxt, prv = _shift_rows(dh, -1)[mid], _shift_rows(dh, 1)[mid]
            dhm, xm = dh[mid], x_ref[pl.ds(r0, rows), :].astype(F32)
            dx_ref[pl.ds(r0, rows), :] = (w[0:1] * nxt + w[1:2] * dhm + w[2:3] * prv).astype(BF16)
            db_ref[...] += _colsum(dhm)
            dw_ref[0:1, :] += _colsum(nxt * xm)
            dw_ref[1:2, :] += _colsum(dhm * xm)
            dw_ref[2:3, :] += _colsum(prv * xm)

        def chunk(ci, carry):
            r0 = pl.multiple_of(ci * rows, rows)
            d = _conv_ext(da_ref, r0, rows, L)
            half_bwd(hu_ref, d * _conv_ext(s1_ref, r0, rows, L), wu_ref, du_ref, dwu_ref, dbu_ref, r0)
            half_bwd(hg_ref, d * _conv_ext(s2_ref, r0, rows, L), wg_ref, dg_ref, dwg_ref, dbg_ref, r0)
            return carry

        lax.fori_loop(0, nchunk, chunk, 0)

    blk = lambda r: pl.BlockSpec((r, wblk), lambda j: (0, j))
    return pl.pallas_call(
        body, name=name, grid=(nb,),
        in_specs=[blk(L)] * 3 + _gate_up_specs(L, wblk, nb) + _gate_up_specs(3, wblk, nb),
        out_specs=[blk(L), blk(L), blk(3), blk(3), blk(1), blk(1)],
        out_shape=[jax.ShapeDtypeStruct((L, F), BF16)] * 2 + [jax.ShapeDtypeStruct((3, F), F32)] * 2
        + [jax.ShapeDtypeStruct((1, F), F32)] * 2,
        compiler_params=_cp(("parallel",)),
    )(da, s1, s2, hu, hu, cw, cw)


def _window_sums(pad_ref, w, lead):
    a = pad_ref[...]
    k = 1
    while k < w:
        a = a + _shift_rows(a, -k)
        k *= 2
    return _shift_rows(a, lead) if lead else a


def _pool_counts(L, h):
    t = lax.broadcasted_iota(jnp.int32, (L, 1), 0)
    return (jnp.minimum(t + h, L) - jnp.maximum(t - h, 0)).astype(F32)


def _pooled(u_ref, pad_ref, L, w):
    h = w // 2
    pad_ref[pl.ds(PAD, L), :] = u_ref[...]
    win = _window_sums(pad_ref, w, h)[PAD:PAD + L]
    return win / _pool_counts(L, h) - u_ref[...]


def _zero_pad_edges(pad_ref, L):
    z = jnp.zeros((PAD, LANES), F32)
    pad_ref[pl.ds(0, PAD), :] = z
    pad_ref[pl.ds(PAD + L, PAD), :] = z


def pool_fwd(u, w_pool, pool_scale, *, name):
    L = u.shape[0]

    def body(u_ref, w_ref, ps_ref, p_ref, pad_ref):
        _zero_pad_edges(pad_ref, L)
        for gi, win in enumerate(POOL_WINDOWS):
            @pl.when(pl.program_id(0) == gi)
            def _():
                pooled = _pooled(u_ref, pad_ref, L, win)
                p_ref[...] = (_dot(pooled.astype(BF16), w_ref[...].astype(BF16)) * ps_ref[...]).astype(BF16)

    return pl.pallas_call(
        body, name=name, grid=(len(POOL_WINDOWS),),
        in_specs=[pl.BlockSpec((L, LANES), lambda gi: (0, gi)), pl.BlockSpec((None, LANES, LANES), lambda gi: (gi, 0, 0)),
                  pl.BlockSpec((1, LANES), lambda gi: (0, gi))],
        out_specs=pl.BlockSpec((L, LANES), lambda gi: (0, gi)),
        out_shape=jax.ShapeDtypeStruct((L, 4 * LANES), BF16),
        scratch_shapes=[pltpu.VMEM((L + 2 * PAD, LANES), F32)],
        compiler_params=_cp(("parallel",)),
    )(u, w_pool, pool_scale)


def pool_bwd(u, dpa, w_pool, pool_scale, *, name):
    L = u.shape[0]

    def body(u_ref, dp_ref, w_ref, ps_ref, du_ref, dw_ref, dps_ref, pad_ref):
        _zero_pad_edges(pad_ref, L)
        for gi, win in enumerate(POOL_WINDOWS):
            @pl.when(pl.program_id(0) == gi)
            def _():
                h = win // 2
                wb = w_ref[...].astype(BF16)
                pooled = _pooled(u_ref, pad_ref, L, win).astype(BF16)
                dp = dp_ref[...].astype(F32)
                dps_ref[...] = _colsum(dp * _dot(pooled, wb))
                dy = (dp * ps_ref[...]).astype(BF16)
                dw_ref[...] = _dot_tn(pooled, dy)
                dpooled = _dot_nt(dy, wb)
                pad_ref[pl.ds(PAD, L), :] = dpooled / _pool_counts(L, h)
                du_ref[...] = (_window_sums(pad_ref, win, h - 1)[PAD:PAD + L] - dpooled).astype(BF16)

    return pl.pallas_call(
        body, name=name, grid=(len(POOL_WINDOWS),),
        in_specs=[pl.BlockSpec((L, LANES), lambda gi: (0, gi)), pl.BlockSpec((L, LANES), lambda gi: (0, gi)),
                  pl.BlockSpec((None, LANES, LANES), lambda gi: (gi, 0, 0)), pl.BlockSpec((1, LANES), lambda gi: (0, gi))],
        out_specs=[pl.BlockSpec((L, LANES), lambda gi: (0, gi)), pl.BlockSpec((None, LANES, LANES), lambda gi: (gi, 0, 0)),
                   pl.BlockSpec((1, LANES), lambda gi: (0, gi))],
        out_shape=[jax.ShapeDtypeStruct((L, 4 * LANES), BF16), jax.ShapeDtypeStruct((4, LANES, LANES), F32),
                   jax.ShapeDtypeStruct((1, 4 * LANES), F32)],
        scratch_shapes=[pltpu.VMEM((L + 2 * PAD, LANES), F32)],
        compiler_params=_cp(("parallel",)),
    )(u, dpa, w_pool, pool_scale)


def _attn_probs(qk, band_k, ctx_k, sink_ref, kh, mask4):
    s_loc = jnp.where(mask4, _dot_nt(qk, band_k), NEG_INF)
    s_ctx = _dot_nt(qk, ctx_k)
    sk = jnp.concatenate([jnp.full((BLK, 1), sink_ref[kh * GQA + hh], F32) for hh in range(GQA)], axis=0)
    m = jnp.maximum(jnp.maximum(jnp.max(s_loc, axis=-1, keepdims=True), jnp.max(s_ctx, axis=-1, keepdims=True)), sk)
    e_loc, e_ctx, e_s = jnp.exp(s_loc - m), jnp.exp(s_ctx - m), jnp.exp(sk - m)
    inv = 1.0 / (jnp.sum(e_loc, axis=-1, keepdims=True) + jnp.sum(e_ctx, axis=-1, keepdims=True) + e_s)
    return e_loc * inv, e_ctx * inv, e_s * inv


def _attn_block(n, L):
    start = pl.multiple_of(jnp.clip((n - 1) * BLK, 0, L - 3 * BLK), BLK)
    qpos = n * BLK + lax.broadcasted_iota(jnp.int32, (BLK, 3 * BLK), 0)
    kpos = start + lax.broadcasted_iota(jnp.int32, (BLK, 3 * BLK), 1)
    mask = jnp.abs(kpos - qpos) <= WINDOW
    return start, jnp.concatenate([mask] * GQA, axis=0)


def _stack_slabs(ref, rows=slice(None)):
    return jnp.concatenate([ref[rows, s * LANES:(s + 1) * LANES] for s in range(GQA)], axis=0)


def _kv_head_lanes(kh):
    return (lax.broadcasted_iota(jnp.int32, (1, LANES), 1) // HEAD_DIM) == kh


def permute_heads(w, inverse=False):
    lo, hi = 4 * LANES, 8 * LANES
    mid = w[lo:hi].reshape(*((GQA, N_KV_HEADS) if inverse else (N_KV_HEADS, GQA)), HEAD_DIM, w.shape[1])
    return jnp.concatenate([w[:lo], mid.swapaxes(0, 1).reshape(hi - lo, w.shape[1]), w[hi:]], axis=0)


def attn_fwd(q, kv, kvc, sink, *, qb, name):
    L = q.shape[0]
    C = kvc.shape[0]
    scale = HEAD_DIM ** -0.5

    def body(q_ref, kv_ref, kvc_ref, sink_ref, o_ref):
        kvc_ = kvc_ref[...]
        for b in range(qb):
            rows = slice(b * BLK, (b + 1) * BLK)
            start, mask4 = _attn_block(pl.program_id(0) * qb + b, L)
            band = kv_ref[pl.ds(start, 3 * BLK), :]
            qs = _stack_slabs(q_ref, rows) * scale
            o = jnp.zeros((GQA * BLK, LANES), F32)
            for kh in range(N_KV_HEADS):
                grp = _kv_head_lanes(kh)
                qk = jnp.where(grp, qs, jnp.zeros_like(qs))
                p_loc, p_ctx, _ = _attn_probs(qk, band[:, :LANES], kvc_[:, :LANES], sink_ref, kh, mask4)
                o = o + jnp.where(grp, _dot(p_loc.astype(BF16), band[:, LANES:]) + _dot(p_ctx.astype(BF16), kvc_[:, LANES:]), 0.0)
            for s in range(GQA):
                o_ref[rows, s * LANES:(s + 1) * LANES] = o[s * BLK:(s + 1) * BLK].astype(BF16)

    return pl.pallas_call(
        body, name=name, grid=(L // (qb * BLK),),
        in_specs=[pl.BlockSpec((qb * BLK, 4 * LANES), lambda n: (n, 0)), _full((L, 2 * LANES)), _full((C, 2 * LANES)),
                  pl.BlockSpec(memory_space=pltpu.SMEM)],
        out_specs=pl.BlockSpec((qb * BLK, 4 * LANES), lambda n: (n, 0)),
        out_shape=jax.ShapeDtypeStruct((L, 4 * LANES), BF16),
        compiler_params=_cp(("parallel",)),
    )(q, kv, kvc, sink)


def attn_bwd(q, kv, kvc, sink, dpa, cos, sa, sb, *, qb, name):
    L = q.shape[0]
    C = kvc.shape[0]
    nsteps = L // (qb * BLK)
    scale = HEAD_DIM ** -0.5

    def body(q_ref, kv_ref, kvc_ref, sink_ref, do_ref, c_ref, sa_ref, sb_ref, cq_ref, saq_ref, sbq_ref,
             dq_ref, dkv_ref, dkvc_ref, dsink_ref, dkv_acc, dkvc_acc):
        step = pl.program_id(0)

        @pl.when(step == 0)
        def _():
            dkv_acc[...] = jnp.zeros_like(dkv_acc)
            dkvc_acc[...] = jnp.zeros_like(dkvc_acc)
            dsink_ref[...] = jnp.zeros_like(dsink_ref)

        for b in range(qb):
            one_block(step * qb + b, slice(b * BLK, (b + 1) * BLK), q_ref, kv_ref, kvc_ref, sink_ref, do_ref, cq_ref, saq_ref, sbq_ref,
                      dq_ref, dsink_ref, dkv_acc, dkvc_acc)

        @pl.when(step == nsteps - 1)
        def _():
            dkv_ref[:, :LANES] = _rope(dkv_acc[:LANES, :].T, c_ref[...], -sa_ref[...], -sb_ref[...]).astype(BF16)
            dkv_ref[:, LANES:] = dkv_acc[LANES:, :].T.astype(BF16)
            dkvc_ref[...] = dkvc_acc[...].T.astype(BF16)

    def one_block(n, rows, q_ref, kv_ref, kvc_ref, sink_ref, do_ref, cq_ref, saq_ref, sbq_ref, dq_ref, dsink_ref, dkv_acc, dkvc_acc):
        start, mask4 = _attn_block(n, L)
        band = kv_ref[pl.ds(start, 3 * BLK), :]
        kvc_ = kvc_ref[...]
        band_k, band_v, ctx_k, ctx_v = band[:, :LANES], band[:, LANES:], kvc_[:, :LANES], kvc_[:, LANES:]
        qs = _stack_slabs(q_ref, rows) * scale
        dos = _stack_slabs(do_ref, rows)
        lane = lax.broadcasted_iota(jnp.int32, (1, LANES), 1)
        dsink = jnp.zeros((1, LANES), F32)
        dq = jnp.zeros((GQA * BLK, LANES), F32)
        dk = jnp.zeros((LANES, 3 * BLK), F32)
        dv = jnp.zeros((LANES, 3 * BLK), F32)
        dkc = jnp.zeros((LANES, C), F32)
        dvc = jnp.zeros((LANES, C), F32)
        for kh in range(N_KV_HEADS):
            grp = _kv_head_lanes(kh)
            qk = jnp.where(grp, qs, jnp.zeros_like(qs))
            dok = jnp.where(grp, dos, jnp.zeros_like(dos))
            p_loc, p_ctx, p_s = _attn_probs(qk, band_k, ctx_k, sink_ref, kh, mask4)
            dp_loc = _dot_nt(dok, band_v)
            dp_ctx = _dot_nt(dok, ctx_v)
            delta = jnp.sum(p_loc * dp_loc, axis=-1, keepdims=True) + jnp.sum(p_ctx * dp_ctx, axis=-1, keepdims=True)
            ds_loc = (p_loc * (dp_loc - delta)).astype(BF16)
            ds_ctx = (p_ctx * (dp_ctx - delta)).astype(BF16)
            dsk = p_s * delta
            for hh in range(GQA):
                dsink = dsink - jnp.where(lane == kh * GQA + hh, jnp.sum(dsk[hh * BLK:(hh + 1) * BLK], axis=0, keepdims=True), 0.0)
            dq = dq + jnp.where(grp, _dot(ds_loc, band_k) + _dot(ds_ctx, ctx_k), 0.0)
            dk = dk + _dot_tn(qk, ds_loc)
            dv = dv + _dot_tn(dok, p_loc.astype(BF16))
            dkc = dkc + _dot_tn(qk, ds_ctx)
            dvc = dvc + _dot_tn(dok, p_ctx.astype(BF16))
        dsink_ref[...] += dsink
        dkv_acc[:LANES, pl.ds(start, 3 * BLK)] += dk
        dkv_acc[LANES:, pl.ds(start, 3 * BLK)] += dv
        dkvc_acc[:LANES, :] += dkc
        dkvc_acc[LANES:, :] += dvc
        c, a, b_ = cq_ref[rows, :], -saq_ref[rows, :], -sbq_ref[rows, :]
        for s in range(GQA):
            dq_ref[rows, s * LANES:(s + 1) * LANES] = _rope(dq[s * BLK:(s + 1) * BLK] * scale, c, a, b_).astype(BF16)

    blk = lambda w: pl.BlockSpec((qb * BLK, w), lambda n: (n, 0))
    return pl.pallas_call(
        body, name=name, grid=(nsteps,),
        in_specs=[blk(4 * LANES), _full((L, 2 * LANES)), _full((C, 2 * LANES)), pl.BlockSpec(memory_space=pltpu.SMEM),
                  pl.BlockSpec((qb * BLK, 4 * LANES), lambda n: (n, 1)),
                  _full((L, LANES)), _full((L, LANES)), _full((L, LANES)), blk(LANES), blk(LANES), blk(LANES)],
        out_specs=[blk(4 * LANES), _full((L, 2 * LANES)), _full((C, 2 * LANES)), _full((1, LANES))],
        out_shape=[jax.ShapeDtypeStruct((L, 4 * LANES), BF16), jax.ShapeDtypeStruct((L, 2 * LANES), BF16),
                   jax.ShapeDtypeStruct((C, 2 * LANES), BF16), jax.ShapeDtypeStruct((1, LANES), F32)],
        scratch_shapes=[pltpu.VMEM((2 * LANES, L), F32), pltpu.VMEM((2 * LANES, C), F32)],
        compiler_params=_cp(("arbitrary",)),
    )(q, kv, kvc, sink, dpa, cos, sa, sb, cos, sa, sb)


def _gelu_parts(x):
    th = jnp.tanh(SQRT_2_OVER_PI * (x + GELU_C * x * x * x))
    return 0.5 * x * (1.0 + th), th


def _gelu_grad(x, th):
    return 0.5 * (1.0 + th) + 0.5 * x * (1.0 - th * th) * SQRT_2_OVER_PI * (1.0 + 3.0 * GELU_C * x * x)


def _layernorm(v):
    mu = jnp.mean(v, axis=-1, keepdims=True)
    vc = v - mu
    rstd = lax.rsqrt(jnp.mean(vc * vc, axis=-1, keepdims=True) + EPS)
    return vc * rstd, rstd


def sgu_fwd(z1, ln_g, ln_b, ws, bst, *, rows, name):
    L, W2 = z1.shape
    W = W2 // 2
    ng = W // LANES

    def body(z_ref, g_ref, b_ref, ws_ref, bs_ref, o_ref):
        for c in range(rows // BLK):
            at = slice(c * BLK, (c + 1) * BLK)
            z, _ = _gelu_parts(z_ref[at, :].astype(F32))
            xhat, _ = _layernorm(z[:, W:])
            vln = (xhat * g_ref[...] + b_ref[...]).astype(BF16)
            for gi in range(ng):
                cs = slice(gi * LANES, (gi + 1) * LANES)
                s = _dot(ws_ref[gi], vln[:, cs]) + bs_ref[:, gi:gi + 1]
                o_ref[at, cs] = (z[:, cs] * s).astype(BF16)

    vec = _full((1, W))
    return pl.pallas_call(
        body, name=name, grid=(L // rows,),
        in_specs=[pl.BlockSpec((rows, W2), lambda n: (n, 0)), vec, vec, _full((ng, LANES, LANES)), _full((BLK, ng))],
        out_specs=pl.BlockSpec((rows, W), lambda n: (n, 0)),
        out_shape=jax.ShapeDtypeStruct((L, W), BF16),
        compiler_params=_cp(("parallel",)),
    )(z1, ln_g, ln_b, ws, bst)


def sgu_bwd(z1, dus, ln_g, ln_b, ws, wst, bst, *, rows, name):
    L, W2 = z1.shape
    W = W2 // 2
    ng = W // LANES

    def body(z_ref, d_ref, g_ref, b_ref, ws_ref, wst_ref, bs_ref, dz_ref, dws_ref, dbs_ref, dg_ref, db_ref, dv_scr):
        @pl.when(pl.program_id(0) == 0)
        def _():
            dws_ref[...] = jnp.zeros_like(dws_ref)
            dbs_ref[...] = jnp.zeros_like(dbs_ref)
            dg_ref[...] = jnp.zeros_like(dg_ref)
            db_ref[...] = jnp.zeros_like(db_ref)

        for c in range(rows // BLK):
            at = slice(c * BLK, (c + 1) * BLK)
            zp = z_ref[at, :].astype(F32)
            z, th = _gelu_parts(zp)
            xhat, rstd = _layernorm(z[:, W:])
            vln = (xhat * g_ref[...] + b_ref[...]).astype(BF16)
            d = d_ref[at, :].astype(F32)
            lane = lax.broadcasted_iota(jnp.int32, (1, LANES), 1)
            dbs = jnp.zeros((BLK, LANES), F32)
            dgel = _gelu_grad(zp, th)
            for gi in range(ng):
                cs = slice(gi * LANES, (gi + 1) * LANES)
                s = _dot(ws_ref[gi], vln[:, cs]) + bs_ref[:, gi:gi + 1]
                dz_ref[at, cs] = (d[:, cs] * s * dgel[:, cs]).astype(BF16)
                ds = d[:, cs] * z[:, cs]
                dbs = dbs + jnp.where(lane == gi, jnp.sum(ds, axis=-1, keepdims=True), 0.0)
                dsb = ds.astype(BF16)
                dws_ref[gi] += _dot_nt(dsb, vln[:, cs])
                dv_scr[:, cs] = _dot(wst_ref[gi], dsb)
            dbs_ref[...] += dbs
            dvln = dv_scr[...]
            dg_ref[...] += _colsum(dvln * xhat)
            db_ref[...] += _colsum(dvln)
            dxh = dvln * g_ref[...]
            dv = rstd * (dxh - jnp.mean(dxh, axis=-1, keepdims=True) - xhat * jnp.mean(dxh * xhat, axis=-1, keepdims=True))
            dz_ref[at, W:] = (dv * dgel[:, W:]).astype(BF16)

    vec = _full((1, W))
    return pl.pallas_call(
        body, name=name, grid=(L // rows,),
        in_specs=[pl.BlockSpec((rows, W2), lambda n: (n, 0)), pl.BlockSpec((rows, W), lambda n: (n, 0)), vec, vec,
                  _full((ng, LANES, LANES)), _full((ng, LANES, LANES)), _full((BLK, ng))],
        out_specs=[pl.BlockSpec((rows, W2), lambda n: (n, 0)), _full((ng, LANES, LANES)), _full((BLK, LANES)), vec, vec],
        out_shape=[jax.ShapeDtypeStruct((L, W2), BF16), jax.ShapeDtypeStruct((ng, LANES, LANES), F32),
                   jax.ShapeDtypeStruct((BLK, LANES), F32), jax.ShapeDtypeStruct((1, W), F32), jax.ShapeDtypeStruct((1, W), F32)],
        scratch_shapes=[pltpu.VMEM((BLK, W), F32)],
        compiler_params=_cp(("arbitrary",)),
    )(z1, dus, ln_g, ln_b, ws, wst, bst)


def _adamw_math(w, m, v, g):
    m_ = ADAM_B1 * m + (1.0 - ADAM_B1) * g
    v_ = ADAM_B2 * v + (1.0 - ADAM_B2) * (g * g)
    return -ADAM_LR * ((m_ / BC1) / (jnp.sqrt(v_ / BC2) + ADAM_EPS) + ADAM_WD * w), m_, v_


def adamw(w, m, v, gparts, *, tr, name):
    NL, R, Wd = w.shape
    nr = R // tr

    def body(w_ref, m_ref, v_ref, *rest):
        gp_refs, (g_ref, d_ref, nm_ref, nv_ref) = rest[:NL], rest[NL:]
        for l in range(NL):
            @pl.when(pl.program_id(0) == l)
            def _():
                g = gp_refs[l][0].astype(F32)
                for s in range(1, gp_refs[l].shape[0]):
                    g = g + gp_refs[l][s].astype(F32)
                g_ref[...] = g
                d_ref[...], nm_ref[...], nv_ref[...] = _adamw_math(w_ref[...], m_ref[...], v_ref[...], g)

    row = pl.BlockSpec((None, tr, Wd), lambda l, i: (l, i, 0))
    gspecs = [pl.BlockSpec((gparts[l].shape[0], tr, Wd), (lambda l_, i, l=l: (0, jnp.clip(i + (l_ - l) * nr, 0, nr - 1), 0)))
              for l in range(NL)]
    return pl.pallas_call(
        body, name=name, grid=(NL, nr),
        in_specs=[row, row, row] + gspecs, out_specs=[row] * 4, out_shape=[jax.ShapeDtypeStruct((NL, R, Wd), F32)] * 4,
        compiler_params=_cp(("arbitrary", "arbitrary")),
    )(w, m, v, *gparts)


def small_update(gpacks, me, params, loss_row, *, name):
    n = len(params)

    def body(me_ref, gp_ref, *refs):
        ins, outs, gs_ref = refs[:3 * n], refs[3 * n:-1], refs[-1]
        gs_ref[...] = gp_ref[0].astype(F32)
        for dv in range(1, N_DEV):
            gs_ref[...] += gp_ref[dv].astype(F32)
        for p, (w, _, _, off, per_dev) in enumerate(params):
            w_ref, m_ref, v_ref = ins[3 * p:3 * p + 3]
            g_ref, d_ref, nm_ref, nv_ref = outs[4 * p:4 * p + 4]
            rows, cols = w.shape
            if cols == LANES and rows % 8 == 0 and not per_dev:
                g = gs_ref[off:off + rows, :]
                g_ref[...] = g
                d_ref[...], nm_ref[...], nv_ref[...] = _adamw_math(w_ref[...], m_ref[...], v_ref[...], g)
                continue
            chunks = -(-cols // LANES)
            base = off + me_ref[0] * per_dev if per_dev else off
            for i in range(rows):
                for j in range(chunks):
                    wd = min(LANES, cols - j * LANES)
                    at = (slice(i, i + 1), slice(j * LANES, j * LANES + wd))
                    g = gs_ref[pl.ds(base + i * chunks + j, 1), 0:wd]
                    g_ref[at] = g
                    d_ref[at], nm_ref[at], nv_ref[at] = _adamw_math(w_ref[at], m_ref[at], v_ref[at], g)
        outs[-1][...] = jnp.sum(gs_ref[loss_row:loss_row + 1, :], axis=1, keepdims=True)

    flat = [a for w, m, v, _, _ in params for a in (w, m, v)]
    out_shape = [jax.ShapeDtypeStruct(w.shape, F32) for w, _, _, _, _ in params for _ in range(4)] + [jax.ShapeDtypeStruct((1, 1), F32)]
    return pl.pallas_call(
        body, name=name, grid=(1,),
        in_specs=[pl.BlockSpec(memory_space=pltpu.SMEM), _full(gpacks.shape)] + [_full(a.shape) for a in flat],
        out_specs=[_full(o.shape) for o in out_shape], out_shape=out_shape,
        scratch_shapes=[pltpu.VMEM(gpacks.shape[1:], F32)],
        compiler_params=_cp(("arbitrary",)),
    )(me, gpacks, *flat)


def ada_fwd_mm(cs, w_ada, b_loc, *, name):
    R, D = cs.shape
    nl, _, n = w_ada.shape

    def body(c_ref, w_ref, b_ref, s_ref, m_ref):
        c = c_ref[...]
        s = c * jax.nn.sigmoid(c)
        s_ref[...] = s
        for i in range(nl):
            m_ref[i] = _dot(s.astype(BF16), w_ref[i].astype(BF16)) + b_ref[i:i + 1, :]

    return pl.pallas_call(
        body, name=name, in_specs=[_full((R, D)), _full((nl, D, n)), _full((nl, n))],
        out_specs=[_full((R, D)), _full((nl, R, n))], grid=(1,),
        out_shape=[jax.ShapeDtypeStruct((R, D), F32), jax.ShapeDtypeStruct((nl, R, n), F32)],
        compiler_params=_cp(("arbitrary",)),
    )(cs, w_ada, b_loc)


def ada_bwd_mm(s, c_ctx, dall, w_ada, *, name):
    R, D = s.shape
    nl, _, n = w_ada.shape

    def body(s_ref, cc_ref, d_ref, w_ref, gw_ref, dcc_ref):
        sb = s_ref[...].astype(BF16)
        row = lax.broadcasted_iota(jnp.int32, (R, 1), 0)
        dctx = d_ref[0, 1:2, :]
        for dv in range(1, N_DEV):
            dctx = dctx + d_ref[dv, 1:2, :]
        for i in range(nl):
            dm = jnp.zeros((R, n), F32)
            for dv in range(N_DEV):
                dm = dm + jnp.where(row == dv, d_ref[dv, 2 * i:2 * i + 1, :], 0.0)
            if i == 0:
                dm = dm + jnp.where(row == N_DEV, dctx, 0.0)
            gw_ref[i] = _dot_tn(sb, dm.astype(BF16))
        cc = cc_ref[...]
        sg = jax.nn.sigmoid(cc)
        ds = _dot_nt(jnp.broadcast_to(dctx, (8, n)).astype(BF16), w_ref[0].astype(BF16))
        dcc_ref[...] = ds * (sg * (1.0 + cc * (1.0 - sg)))

    return pl.pallas_call(
        body, name=name, grid=(1,),
        in_specs=[_full((R, D)), _full((1, D)), _full((N_DEV, 3, n)), _full((nl, D, n))],
        out_specs=[_full((nl, D, n)), _full((8, D))],
        out_shape=[jax.ShapeDtypeStruct((nl, D, n), F32), jax.ShapeDtypeStruct((8, D), F32)],
        compiler_params=_cp(("arbitrary",)),
    )(s, c_ctx, dall, w_ada)


def _place():
    x, y, c = lax.axis_index("x"), lax.axis_index("y"), lax.axis_index("c")
    return x, y, c


def _lin(p):
    return 4 * p[0] + 2 * p[1] + p[2]


def all_gather_small(xb, *, name):
    R, W = xb.shape

    def body(x_ref, out_ref, send_sems, recv_sems, local_sem):
        x, y, c = _place()
        me = _lin((x, y, c))
        mine = pltpu.make_async_copy(x_ref, out_ref.at[me], local_sem)
        mine.start()
        copies = []
        for k in range(1, N_DEV):
            peer = (x ^ (k >> 2), y ^ ((k >> 1) & 1), c ^ (k & 1))
            mk = lambda dst, k=k, peer=peer: pltpu.make_async_remote_copy(
                src_ref=x_ref, dst_ref=dst, send_sem=send_sems.at[k - 1], recv_sem=recv_sems.at[k - 1], device_id=peer, device_id_type=MESH)
            mk(out_ref.at[me]).start()
            copies.append(mk(out_ref.at[_lin(peer)]))
        for cp in copies:
            cp.wait_recv()
        for cp in copies:
            cp.wait_send()
        mine.wait()

    vm = pl.BlockSpec(memory_space=pltpu.VMEM)
    return pl.pallas_call(
        body, name=name, in_specs=[vm], out_specs=vm, out_shape=jax.ShapeDtypeStruct((N_DEV, R, W), xb.dtype),
        scratch_shapes=[pltpu.SemaphoreType.DMA((7,)), pltpu.SemaphoreType.DMA((7,)), pltpu.SemaphoreType.DMA],
        compiler_params=pltpu.CompilerParams(vmem_limit_bytes=VMEM_LIMIT),
    )(xb)


HBM_SPEC = pl.BlockSpec(memory_space=pltpu.HBM)
SEM_SPEC = pl.BlockSpec(memory_space=pltpu.SEMAPHORE)
ORDERED_EFFECT = pltpu.SideEffectType.DATAFLOW_SIDE_EFFECTING


def _exchange_copies(srcs, lands, sems, scatter):
    x, y, c = _place()
    me = _lin((x, y, c))
    for j in range(len(srcs)):
        r = lands[j].shape[0] // N_DEV
        block = lambda d, j=j, r=r: pl.ds(pl.multiple_of(d * r, 16), r)
        for k in range(1, N_DEV):
            peer = (x ^ (k >> 2), y ^ ((k >> 1) & 1), c ^ (k & 1))
            src = srcs[j].at[block(_lin(peer)), :] if scatter else srcs[j]
            mk = lambda dst, j=j, k=k, peer=peer, src=src: pltpu.make_async_remote_copy(
                src_ref=src, dst_ref=dst, send_sem=sems[2 * j].at[k - 1], recv_sem=sems[2 * j + 1].at[k - 1],
                device_id=peer, device_id_type=MESH)
            yield mk(lands[j].at[block(me), :]), mk(lands[j].at[block(_lin(peer)), :])


def exchange_start(srcs, lands, *, scatter, name):
    nw = len(srcs)

    def body(*refs):
        for start, _ in _exchange_copies(refs[:nw], refs[nw:2 * nw], refs[2 * nw:4 * nw], scatter):
            start.start()
        refs[-1][...] = jnp.zeros_like(refs[-1])

    thru = [pltpu.HBM(a.shape, a.dtype) for a in (*srcs, *lands)]
    res = pl.pallas_call(
        body, name=name, in_specs=[HBM_SPEC] * (2 * nw),
        out_specs=[SEM_SPEC] * (2 * nw) + [HBM_SPEC] * (2 * nw) + [pl.BlockSpec(memory_space=pltpu.VMEM)],
        out_shape=[pltpu.SemaphoreType.DMA((N_DEV - 1,))] * (2 * nw) + thru + [jax.ShapeDtypeStruct((8, LANES), F32)],
        input_output_aliases={i: 2 * nw + i for i in range(2 * nw)},
        compiler_params=pltpu.CompilerParams(has_side_effects=ORDERED_EFFECT),
    )(*[pltpu.with_memory_space_constraint(a, pltpu.HBM) for a in (*srcs, *lands)])
    return res[:2 * nw], res[2 * nw:3 * nw], res[3 * nw:4 * nw], res[-1]


def exchange_wait(srcs, lands, sems, after, *, scatter, name):
    nw = len(srcs)
    after = list(after) if isinstance(after, (list, tuple)) else [after]

    def body(*refs):
        for _, arrive in _exchange_copies(refs[:nw], refs[nw:2 * nw], refs[2 * nw:4 * nw], scatter):
            arrive.wait_send()
            arrive.wait_recv()

    res = pl.pallas_call(
        body, name=name, in_specs=[HBM_SPEC] * (2 * nw) + [SEM_SPEC] * (2 * nw) + [pl.BlockSpec(memory_space=pl.ANY)] * len(after),
        out_specs=[HBM_SPEC] * (2 * nw), out_shape=[pltpu.HBM(a.shape, a.dtype) for a in (*srcs, *lands)],
        input_output_aliases={i: i for i in range(2 * nw)},
        compiler_params=pltpu.CompilerParams(has_side_effects=ORDERED_EFFECT),
    )(*srcs, *lands, *sems, *after)
    return res[nw:]


def place_own(srcs, rows, me, *, scatter, name):
    nw = len(srcs)
    lands = [lax.empty((N_DEV * r, s_.shape[1]), s_.dtype) for r, s_ in zip(rows, srcs)]

    def body(me_ref, *refs):
        for j in range(nw):
            refs[2 * nw + j][...] = refs[j][...]

    mine = lambda i, me_ref: (me_ref[0], 0)
    src_at = mine if scatter else (lambda i, me_ref: (0, 0))
    blocks = [(r, s_.shape[1]) for r, s_ in zip(rows, srcs)]
    return pl.pallas_call(
        body, name=name,
        grid_spec=pltpu.PrefetchScalarGridSpec(
            num_scalar_prefetch=1, grid=(1,),
            in_specs=[pl.BlockSpec(b_, src_at) for b_ in blocks] + [pl.BlockSpec(memory_space=pl.ANY)] * nw,
            out_specs=[pl.BlockSpec(b_, mine) for b_ in blocks]),
        out_shape=[jax.ShapeDtypeStruct(l_.shape, l_.dtype) for l_ in lands],
        input_output_aliases={1 + nw + j: j for j in range(nw)},
        compiler_params=_cp(("arbitrary",)),
    )(jnp.reshape(me, (1,)).astype(jnp.int32), *srcs, *lands)


def _rope_tables(L):
    t = jnp.arange(L)
    inv = ROPE_BASE ** (-jnp.arange(ROPE_FREQS, dtype=F32) / ROPE_FREQS)
    ar = (t // GRID_W).astype(F32)[:, None] * inv
    ac = (t % GRID_W).astype(F32)[:, None] * inv
    z = jnp.zeros_like(ar)
    cos = jnp.concatenate([jnp.cos(ar), jnp.cos(ar), jnp.cos(ac), jnp.cos(ac)], axis=1)
    sa = jnp.concatenate([-jnp.sin(ar), z, -jnp.sin(ac), z], axis=1)
    sb = jnp.concatenate([z, jnp.sin(ar), z, jnp.sin(ac)], axis=1)
    return tuple(jnp.tile(a, (1, LANES // HEAD_DIM)) for a in (cos, sa, sb))


def _nat2d(a):
    return a.reshape(1, -1) if a.ndim == 1 else a.reshape(-1, a.shape[-1])


def _pack_rows(a):
    rows, cols = a.shape
    chunks = -(-cols // LANES)
    f = jnp.pad(a, ((0, 0), (0, chunks * LANES - cols))).reshape(rows * chunks, LANES)
    return jnp.pad(f, ((0, -f.shape[0] % 8), (0, 0)))


def _rows128(a):
    f = a.reshape(-1)
    n = -(-f.shape[0] // (8 * LANES)) * 8 * LANES
    return jnp.pad(f, (0, n - f.shape[0])).reshape(-1, LANES)


def kernel(x, c, ctx, c_ctx, w_ada, b_ada, g_mix_pre, g_mix_post, g_ffn_pre, g_ffn_post, w_in_even, w_pool, pool_scale, attn_sink, w_out_even, w_in_odd, sgu_ln_g, sgu_ln_b, sgu_w, sgu_b, w_out_odd, w_ffn_up, ffn_conv_w, ffn_conv_b, w_ffn_down, loss_target, m_c_ctx, m_w_ada, m_b_ada, m_g_mix_pre, m_g_mix_post, m_g_ffn_pre, m_g_ffn_post, m_w_in_even, m_w_pool, m_pool_scale, m_attn_sink, m_w_out_even, m_w_in_odd, m_sgu_ln_g, m_sgu_ln_b, m_sgu_w, m_sgu_b, m_w_out_odd, m_w_ffn_up, m_ffn_conv_w, m_ffn_conv_b, m_w_ffn_down, v_c_ctx, v_w_ada, v_b_ada, v_g_mix_pre, v_g_mix_post, v_g_ffn_pre, v_g_ffn_post, v_w_in_even, v_w_pool, v_pool_scale, v_attn_sink, v_w_out_even, v_w_in_odd, v_sgu_ln_g, v_sgu_ln_b, v_sgu_w, v_sgu_b, v_w_out_odd, v_w_ffn_up, v_ffn_conv_w, v_ffn_conv_b, v_w_ffn_down):
    P = dict(c_ctx=c_ctx, w_ada=w_ada, b_ada=b_ada, g_mix_pre=g_mix_pre, g_mix_post=g_mix_post, g_ffn_pre=g_ffn_pre,
             g_ffn_post=g_ffn_post, w_in_even=w_in_even, w_pool=w_pool, pool_scale=pool_scale, attn_sink=attn_sink,
             w_out_even=w_out_even, w_in_odd=w_in_odd, sgu_ln_g=sgu_ln_g, sgu_ln_b=sgu_ln_b, sgu_w=sgu_w, sgu_b=sgu_b,
             w_out_odd=w_out_odd, w_ffn_up=w_ffn_up, ffn_conv_w=ffn_conv_w, ffn_conv_b=ffn_conv_b, w_ffn_down=w_ffn_down)
    M = dict(c_ctx=m_c_ctx, w_ada=m_w_ada, b_ada=m_b_ada, g_mix_pre=m_g_mix_pre, g_mix_post=m_g_mix_post, g_ffn_pre=m_g_ffn_pre,
             g_ffn_post=m_g_ffn_post, w_in_even=m_w_in_even, w_pool=m_w_pool, pool_scale=m_pool_scale, attn_sink=m_attn_sink,
             w_out_even=m_w_out_even, w_in_odd=m_w_in_odd, sgu_ln_g=m_sgu_ln_g, sgu_ln_b=m_sgu_ln_b, sgu_w=m_sgu_w, sgu_b=m_sgu_b,
             w_out_odd=m_w_out_odd, w_ffn_up=m_w_ffn_up, ffn_conv_w=m_ffn_conv_w, ffn_conv_b=m_ffn_conv_b, w_ffn_down=m_w_ffn_down)
    V = dict(c_ctx=v_c_ctx, w_ada=v_w_ada, b_ada=v_b_ada, g_mix_pre=v_g_mix_pre, g_mix_post=v_g_mix_post, g_ffn_pre=v_g_ffn_pre,
             g_ffn_post=v_g_ffn_post, w_in_even=v_w_in_even, w_pool=v_w_pool, pool_scale=v_pool_scale, attn_sink=v_attn_sink,
             w_out_even=v_w_out_even, w_in_odd=v_w_in_odd, sgu_ln_g=v_sgu_ln_g, sgu_ln_b=v_sgu_ln_b, sgu_w=v_sgu_w, sgu_b=v_sgu_b,
             w_out_odd=v_w_out_odd, w_ffn_up=v_w_ffn_up, ffn_conv_w=v_ffn_conv_w, ffn_conv_b=v_ffn_conv_b, w_ffn_down=v_w_ffn_down)

    x = x[0]
    ctx = ctx[0]
    target = loss_target[0]
    L, D = x.shape
    C = ctx.shape[0]
    tm = min(512, L)
    tm_out = min(1024, L)
    conv_rows = min(1024, L)
    me = 4 * lax.axis_index("x") + 2 * lax.axis_index("y") + lax.axis_index("c")
    n_ada = w_ada.shape[2]
    F = w_ffn_down.shape[1] * N_DEV
    half_f = F // 2

    n_cw = ffn_conv_w.shape[2]
    small = jnp.concatenate([_rows128(c), _rows128(sgu_ln_g), _rows128(sgu_ln_b), _rows128(ffn_conv_w)], axis=0)
    small_all = all_gather_small(small, name="gather_small_inputs")
    c_all = small_all[:, :8].reshape(N_DEV, D)
    ln_g = small_all[:, 8].reshape(1, D)
    ln_b = small_all[:, 16].reshape(1, D)
    conv_w = small_all[:, 24:].reshape(N_DEV, -1)[:, :2 * 3 * n_cw].reshape(N_DEV, 2, 3, n_cw)
    conv_w = conv_w.transpose(1, 2, 0, 3).reshape(2, 3, 2 * F)

    cs = jnp.concatenate([c_all, c_ctx[None, :], jnp.zeros((7, D), F32)], axis=0)
    b_loc = lax.dynamic_slice(b_ada, (0, me * n_ada), (2, n_ada))
    silu_c, mods_loc = ada_fwd_mm(cs, w_ada, b_loc, name="ada_fwd")
    mods_all = all_gather_small(mods_loc.reshape(-1, LANES), name="gather_mods")

    shards = [s.astype(BF16) for s in (w_in_even[0].T, w_out_even[0], w_ffn_up[0].T, w_ffn_down[0],
                                       w_in_odd[0].T, w_out_odd[0], w_ffn_up[1].T, w_ffn_down[1])]
    shards, mods_all = lax.optimization_barrier((shards, mods_all))
    w_sems, w_srcs, w_lands, _ = exchange_start(shards, place_own(shards, [s.shape[0] for s in shards], me, scatter=False, name="gather_own"),
                                              scatter=False, name="gather_start")

    def weight(j, after):
        return exchange_wait([w_srcs[j]], [w_lands[j]], w_sems[2 * j:2 * j + 2], after, scatter=False, name=f"gather_wait_{j}")[0]

    mods_all = mods_all.reshape(N_DEV, 2, 16, n_ada).transpose(1, 2, 0, 3).reshape(2, 16, 6 * D)
    mod = lambda i, row: [m_[None, :] for m_ in jnp.split(lax.dynamic_index_in_dim(mods_all[i], row, 0, False), 6)]
    sh_m, sc_m, gt_m, sh_f, sc_f, gt_f = zip(mod(0, me), mod(1, me))
    csh_m, csc_m = mod(0, N_DEV)[:2]

    row = lambda a, i: a[i][None, :]

    cos, sa, sb = _rope_tables(L)
    sink = attn_sink[0]
    bst = sgu_b[0].T
    sgu_wb, sgu_wtb = sgu_w[0].astype(BF16), sgu_w[0].swapaxes(1, 2).astype(BF16)
    wup, wdn = [None, None], [None, None]

    def ffn_fwd(i, xin):
        wup[i] = weight(2 + 4 * i, xin)
        h, hu = pre_mm(xin, row(g_ffn_pre, i), sh_f[i], sc_f[i], wup[i], tm=tm, tn=half_f, name=f"ffn_up_{i}")
        a, s1, s2 = conv_fwd(hu, conv_w[i], ffn_conv_b[i][None, :], rows=conv_rows, wblk=2 * LANES, name=f"ffn_conv_{i}")
        wdn[i] = weight(3 + 4 * i, a)
        res = mm_post([a], wdn[i], xin, row(g_ffn_post, i), gt_f[i], tm=tm, target=target if i == 1 else None, name=f"ffn_down_{i}")
        return (h, (hu, s1, s2), a, *res)

    first_mod, cos, sa, sb = lax.optimization_barrier((sh_m[0], cos, sa, sb))
    win_e = permute_heads(weight(0, first_mod))
    h0, u, q, kv = inproj_even(x, row(g_mix_pre, 0), sh_m[0], sc_m[0], win_e, cos, sa, sb, tm=tm, name="in_even")
    hc, kvc = pre_mm(ctx, row(g_mix_pre, 0), csh_m, csc_m, win_e, tm=C, tn=2 * LANES, w_row_off=8 * LANES, name="in_even_ctx")
    pa = [pool_fwd(u, w_pool[0], pool_scale, name="pool_fwd"), attn_fwd(q, kv, kvc, sink, qb=2, name="attn_fwd")]
    wout_e = permute_heads(weight(1, pa[1]))
    y0, x1 = mm_post(pa, wout_e, x, row(g_mix_post, 0), gt_m[0], tm=tm_out, name="out_even")
    h1, hu0, a0, f0, x2 = ffn_fwd(0, x1)
    win_o = weight(4, x2)
    h2, z1 = pre_mm(x2, row(g_mix_pre, 1), sh_m[1], sc_m[1], win_o, tm=tm, tn=D, name="in_odd")
    us = sgu_fwd(z1, ln_g, ln_b, sgu_wb, bst, rows=tm, name="sgu_fwd")
    wout_o = weight(5, us)
    y1, x3 = mm_post([us], wout_o, x2, row(g_mix_post, 1), gt_m[1], tm=tm_out, name="out_odd")
    h3, hu1, a1, f1, dx4, loss_part = ffn_fwd(1, x3)

    g_srcs, g_lands, g_sems = [], [], []

    def scatter(grads, nm):
        own = place_own(grads, [g.shape[0] // N_DEV for g in grads], me, scatter=True, name=nm.replace("start", "own"))
        sems, srcs, lands, tok = exchange_start(grads, own, scatter=True, name=nm)
        g_srcs.extend(srcs)
        g_lands.extend(lands)
        g_sems.extend(sems)
        return tok[0:1, 0:1]

    def ffn_bwd(i, dxo, xin, h, hu, a, f, g_post):
        dyf, da, dg_post, dgt = post_bwd_mm(dxo, f, g_post, gt_f[i], wdn[i], tm=tm, name=f"ffn_down_bwd_{i}")
        dhg, dhu, dcwg, dcwu, dcbg, dcbu = conv_bwd(da, hu[1], hu[2], hu[0], conv_w[i], rows=conv_rows, wblk=2 * LANES,
                                                    name=f"ffn_conv_bwd_{i}")
        dxin, dg_pre, dsh, dsc = mm_pre_bwd([dhg, dhu], wup[i], xin, dxo, row(g_ffn_pre, i), sc_f[i], tm=tm,
                                            name=f"ffn_up_bwd_{i}")
        g_dn = wgrad([a], dyf, tr=2 * LANES, name=f"wgrad_down_{i}")
        g_up = wgrad([dhg, dhu], h, tr=2 * LANES, name=f"wgrad_up_{i}")
        tok = scatter([g_dn, g_up], f"scatter_start_ffn_{i}")
        return dxin, tok, dict(g_ffn_post=dg_post, g_ffn_pre=dg_pre, gt_f=dgt, sh_f=dsh, sc_f=dsc,
                               ffn_conv_w=jnp.concatenate([dcwg, dcwu], axis=1), ffn_conv_b=jnp.concatenate([dcbg, dcbu], axis=1)[0])

    dx3, tok, sf1 = ffn_bwd(1, dx4, x3, h3, hu1, a1, f1, row(g_ffn_post, 1))
    dy1, dus, dg_mpost1, dgt_m1 = post_bwd_mm(dx3, y1, row(g_mix_post, 1) + tok, gt_m[1], wout_o, tm=tm_out, name="out_odd_bwd")
    dz1, dws, dbs, dlng, dlnb = sgu_bwd(z1, dus, ln_g, ln_b, sgu_wb, sgu_wtb, bst, rows=tm, name="sgu_bwd")
    dx2, dg_mpre1, dsh_m1, dsc_m1 = mm_pre_bwd([dz1], win_o, x2, dx3, row(g_mix_pre, 1), sc_m[1], tm=tm, name="in_odd_bwd")
    tok = scatter([wgrad([us], dy1, tr=4 * LANES, name="wgrad_out_odd"), wgrad([dz1], h2, tr=4 * LANES, name="wgrad_in_odd")],
                  "scatter_start_mix_1")

    dx1, tok, sf0 = ffn_bwd(0, dx2, x1, h1, hu0, a0, f0, row(g_ffn_post, 0) + tok)
    dy0, dpa, dg_mpost0, dgt_m0 = post_bwd_mm(dx1, y0, row(g_mix_post, 0) + tok, gt_m[0], wout_e, tm=tm_out, name="out_even_bwd")
    tok = scatter([permute_heads(wgrad(pa, dy0, tr=4 * LANES, name="wgrad_out_even"), inverse=True)], "scatter_start_out_0")
    du, dwp, dps = pool_bwd(u, dpa, w_pool[0], pool_scale + tok, name="pool_bwd")
    dq, dkv, dkvc, dsink = attn_bwd(q, kv, kvc, sink, dpa, cos, sa, sb, qb=2, name="attn_bwd")
    dz0 = jnp.concatenate([du, dq, dkv], axis=1)
    dzc = jnp.concatenate([jnp.zeros((C, 8 * LANES), BF16), dkvc], axis=1)
    tok = scatter([permute_heads(wgrad([dz0], h0, tr=2 * LANES, extra=(dzc, hc), name="wgrad_in_even"), inverse=True)],
                  "scatter_start_in_0")
    grad_x, dg_mpre0, dsh_m0, dsc_m0 = mm_pre_bwd([dz0], win_e, x, dx1, row(g_mix_pre, 0) + tok, sc_m[0], tm=tm,
                                                  name="in_even_bwd")
    _, dg_mpre0c, dcsh, dcsc = mm_pre_bwd([dkvc], win_e, ctx, None, row(g_mix_pre, 0), csc_m, tm=C,
                                          w_row_off=8 * LANES, name="in_even_ctx_bwd")

    out, ran = {}, {}

    def update(name, lands, transposed):
        w_, m_, v_ = (a.transpose(0, 2, 1) if transposed else a for a in (P[name], M[name], V[name]))
        r = w_.shape[1]
        tr = r // 4 if r % 64 == 0 and r > 256 else r
        res = adamw(w_, m_, v_, [l_.reshape(N_DEV, r, l_.shape[1]) for l_ in lands], tr=tr, name=f"adamw_{name}")
        ran[name] = res[0]
        for kind, val in zip(("grad", "delta", "new_m", "new_v"), res):
            out[(kind, name)] = val.transpose(0, 2, 1) if transposed else val

    zero = jnp.zeros((1, D), F32)
    dmod0 = jnp.concatenate([dsh_m0, dsc_m0, dgt_m0, sf0["sh_f"], sf0["sc_f"], sf0["gt_f"]], axis=1)
    dmodc = jnp.concatenate([dcsh, dcsc, zero, zero, zero, zero], axis=1)
    dmod1 = jnp.concatenate([dsh_m1, dsc_m1, dgt_m1, sf1["sh_f"], sf1["sc_f"], sf1["gt_f"]], axis=1)
    dmods = jnp.concatenate([dmod0, dmodc, dmod1], axis=0)
    dm = dmods.reshape(-1, LANES).astype(BF16)
    d_sems, d_srcs, d_lands, d_tok = exchange_start(
        [dm], place_own([dm], [dm.shape[0]], me, scatter=False, name="dmods_own"), scatter=False, name="dmods_start")
    slots = exchange_wait(g_srcs[:6], g_lands[:6], g_sems[:12], d_tok, scatter=True, name="scatter_wait_early")
    early = slots
    update("w_ffn_down", [slots[4], slots[0]], False)
    update("w_in_odd", [slots[3]], True)
    update("w_out_odd", [slots[2]], False)
    updated = lambda names: [ran[k] for k in names]
    dmods_all = exchange_wait(d_srcs, d_lands, d_sems, updated(("w_out_odd",)), scatter=False, name="dmods_wait")[0]
    dall = lax.dynamic_index_in_dim(dmods_all.astype(F32).reshape(N_DEV, 3, N_DEV, n_ada), me, 2, False)
    g_w_ada, dcc = ada_bwd_mm(silu_c, c_ctx[None, :], dall, w_ada, name="ada_bwd")

    rep = dict(
        c_ctx=dcc[0:1],
        b_ada=jnp.concatenate([dmod0 + dmodc, dmod1]),
        g_mix_pre=jnp.concatenate([dg_mpre0 + dg_mpre0c, dg_mpre1]),
        g_mix_post=jnp.concatenate([dg_mpost0, dg_mpost1]),
        g_ffn_pre=jnp.concatenate([sf0["g_ffn_pre"], sf1["g_ffn_pre"]]),
        g_ffn_post=jnp.concatenate([sf0["g_ffn_post"], sf1["g_ffn_post"]]),
        w_pool=_nat2d(dwp), pool_scale=dps, attn_sink=dsink[:, :N_Q_HEADS],
        sgu_w=_nat2d(dws), sgu_b=dbs[:, :sgu_b.shape[1]].T,
        ffn_conv_b=jnp.stack([sf0["ffn_conv_b"], sf1["ffn_conv_b"]]),
    )
    hi = loss_part.astype(BF16).astype(F32)
    mid = (loss_part - hi).astype(BF16).astype(F32)
    loss_piece = jnp.pad(jnp.concatenate([hi, mid, loss_part - hi - mid], axis=1), ((0, 7), (0, LANES - 3)))
    conv_g = jnp.stack([sf0["ffn_conv_w"], sf1["ffn_conv_w"]]).reshape(2 * 3, N_DEV, n_cw).swapaxes(0, 1)
    shard_full = dict(sgu_ln_g=dlng.reshape(N_DEV, LANES), sgu_ln_b=dlnb.reshape(N_DEV, LANES),
                      ffn_conv_w=jnp.concatenate([_pack_rows(conv_g[d]) for d in range(N_DEV)], axis=0))
    small_names = list(rep) + list(shard_full)
    pieces = [_pack_rows(rep[k]) for k in rep] + list(shard_full.values()) + [loss_piece]
    sizes = [p.shape[0] for p in pieces]
    offs = [sum(sizes[:i]) for i in range(len(sizes))]
    pieces.append(jnp.zeros((-sum(sizes) % 16, LANES), F32))
    gpack = jnp.concatenate(pieces, axis=0).astype(BF16)
    own = place_own([gpack], [gpack.shape[0]], me, scatter=False, name="smallgrad_own")
    s_sems, s_srcs, s_lands, small_tok = exchange_start([gpack], own, scatter=False, name="smallgrad_start")

    slots = exchange_wait(g_srcs[6:], g_lands[6:], g_sems[12:], small_tok, scatter=True, name="scatter_wait_late")
    update("w_in_even", [slots[1]], True)
    update("w_out_even", [slots[0]], False)
    update("w_ffn_up", [early[5], early[1]], True)
    res = adamw(w_ada, m_w_ada, v_w_ada, [g_w_ada[l][None] for l in range(w_ada.shape[0])], tr=D // 4, name="adamw_w_ada")
    ran["w_ada"] = res[0]
    for kind, val in zip(("grad", "delta", "new_m", "new_v"), res):
        out[(kind, "w_ada")] = val

    gpacks = exchange_wait(s_srcs, s_lands, s_sems, updated(("w_ada",)), scatter=False,
                           name="smallgrad_wait")[0]
    per_dev = {k: shard_full[k].shape[0] // N_DEV for k in shard_full}
    params = [(_nat2d(P[k]), _nat2d(M[k]), _nat2d(V[k]), offs[i], per_dev.get(k, 0)) for i, k in enumerate(small_names)]
    res = small_update(gpacks.reshape(N_DEV, -1, LANES), jnp.reshape(me, (1,)).astype(jnp.int32), params, offs[-1], name="adamw_small")
    for i, k in enumerate(small_names):
        for kind, val in zip(("grad", "delta", "new_m", "new_v"), res[4 * i:4 * i + 4]):
            out[(kind, k)] = val.reshape(P[k].shape)
    loss = res[-1][0, 0]

    names = list(P)
    final = [loss, grad_x[None]]
    for kind in ("grad", "delta", "new_m", "new_v"):
        for k in names:
            val = out[(kind, k)]
            final.append(val)
    return tuple(final)
```

```python
import functools
import math

import jax
import jax.numpy as jnp
from jax import lax
from jax.experimental import pallas as pl
from jax.experimental.pallas import tpu as pltpu

F32 = jnp.float32
BF16 = jnp.bfloat16
MESH = pl.DeviceIdType.MESH
N_DEV = 8
LANES = 128
VMEM_LIMIT = 48 * 1024 * 1024
EPS = 1e-6
NEG_INF = -1e30
GRID_W = 64
WINDOW = 128
BLK = 128
HEAD_DIM = 64
N_Q_HEADS = 8
N_KV_HEADS = 2
GQA = N_Q_HEADS // N_KV_HEADS
POOL_WINDOWS = (2, 4, 8, 16)
ROPE_BASE = 10000.0
ROPE_FREQS = HEAD_DIM // 4
PAD = 16
ADAM_LR, ADAM_B1, ADAM_B2, ADAM_EPS, ADAM_WD, ADAM_STEP = 0.001, 0.9, 0.999, 1e-08, 0.01, 10
BC1 = 1.0 - ADAM_B1 ** ADAM_STEP
BC2 = 1.0 - ADAM_B2 ** ADAM_STEP
SQRT_2_OVER_PI = math.sqrt(2.0 / math.pi)
GELU_C = 0.044715


def _cp(sem=None):
    return pltpu.CompilerParams(dimension_semantics=sem, vmem_limit_bytes=VMEM_LIMIT)


def _dot(a, b):
    return jnp.dot(a, b, preferred_element_type=F32)


def _dot_nt(a, b):
    return lax.dot_general(a, b, (((1,), (1,)), ((), ())), preferred_element_type=F32)


def _dot_tn(a, b):
    return lax.dot_general(a, b, (((0,), (0,)), ((), ())), preferred_element_type=F32)


def _rms(x):
    r = lax.rsqrt(jnp.mean(x * x, axis=-1, keepdims=True) + EPS)
    return x * r, r


def _rms_bwd(dn, n, r):
    return r * (dn - n * jnp.mean(dn * n, axis=-1, keepdims=True))


def _colsum(a):
    return jnp.sum(a, axis=0, keepdims=True)


def _rope(x, c, sa, sb):
    return x * c + pltpu.roll(x, LANES - ROPE_FREQS, 1) * sa + pltpu.roll(x, ROPE_FREQS, 1) * sb


def _full(shape):
    return pl.BlockSpec(shape, lambda *_: (0,) * len(shape))


def pre_mm(x, g, sh, sc, wt, *, tm, tn, w_row_off=0, name):
    T, D = x.shape
    n_rows = wt.shape[0] - w_row_off

    def body(x_ref, g_ref, sh_ref, sc_ref, w_ref, h_ref, z_ref):
        n, _ = _rms(x_ref[...])
        h = (n * g_ref[...] * (1.0 + sc_ref[...]) + sh_ref[...]).astype(BF16)
        h_ref[...] = h
        for c0 in range(0, n_rows, tn):
            z_ref[:, c0:c0 + tn] = _dot_nt(h, w_ref[c0:c0 + tn, :]).astype(BF16)

    vec = pl.BlockSpec((1, D), lambda i: (0, 0))
    return pl.pallas_call(
        body, name=name, grid=(T // tm,),
        in_specs=[pl.BlockSpec((tm, D), lambda i: (i, 0)), vec, vec, vec,
                  pl.BlockSpec((n_rows, D), lambda i: (w_row_off // n_rows, 0), pipeline_mode=pl.Buffered(1))],
        out_specs=[pl.BlockSpec((tm, D), lambda i: (i, 0)), pl.BlockSpec((tm, n_rows), lambda i: (i, 0))],
        out_shape=[jax.ShapeDtypeStruct((T, D), BF16), jax.ShapeDtypeStruct((T, n_rows), BF16)],
        compiler_params=_cp(("parallel",)),
    )(x, g, sh, sc, wt)


def inproj_even(x, g, sh, sc, wt, cos, sa, sb, *, tm, name):
    T, D = x.shape
    N = wt.shape[0]

    def body(x_ref, g_ref, sh_ref, sc_ref, w_ref, c_ref, sa_ref, sb_ref, h_ref, u_ref, q_ref, kv_ref):
        n, _ = _rms(x_ref[...])
        h = (n * g_ref[...] * (1.0 + sc_ref[...]) + sh_ref[...]).astype(BF16)
        h_ref[...] = h
        z = _dot_nt(h, w_ref[...])
        u_ref[...] = z[:, :4 * LANES]
        c, a, b = c_ref[...], sa_ref[...], sb_ref[...]
        for s in range(4):
            q_ref[:, s * LANES:(s + 1) * LANES] = _rope(z[:, (4 + s) * LANES:(5 + s) * LANES], c, a, b).astype(BF16)
        kv_ref[:, :LANES] = _rope(z[:, 8 * LANES:9 * LANES], c, a, b).astype(BF16)
        kv_ref[:, LANES:] = z[:, 9 * LANES:].astype(BF16)

    vec = pl.BlockSpec((1, D), lambda i: (0, 0))
    row = lambda w: pl.BlockSpec((tm, w), lambda i: (i, 0))
    return pl.pallas_call(
        body, name=name, grid=(T // tm,),
        in_specs=[row(D), vec, vec, vec, _full((N, D)), row(LANES), row(LANES), row(LANES)],
        out_specs=[row(D), row(4 * LANES), row(4 * LANES), row(2 * LANES)],
        out_shape=[jax.ShapeDtypeStruct((T, D), BF16), jax.ShapeDtypeStruct((T, 4 * LANES), F32),
                   jax.ShapeDtypeStruct((T, 4 * LANES), BF16), jax.ShapeDtypeStruct((T, 2 * LANES), BF16)],
        compiler_params=_cp(("parallel",)),
    )(x, g, sh, sc, wt, cos, sa, sb)


def mm_post(a_parts, w, x, g, gt, *, tm, target=None, name):
    T = a_parts[0].shape[0]
    D = w.shape[1]
    npart = len(a_parts)
    offs = [sum(a_.shape[1] for a_ in a_parts[:p]) for p in range(npart + 1)]
    with_loss = target is not None

    def body(*refs):
        a_refs, (w_ref, x_ref, g_ref, gt_ref) = refs[:npart], refs[npart:npart + 4]
        y = _dot(a_refs[0][...], w_ref[offs[0]:offs[1], :])
        for p in range(1, npart):
            y = y + _dot(a_refs[p][...], w_ref[offs[p]:offs[p + 1], :])
        n, _ = _rms(y)
        xn = x_ref[...] + gt_ref[...] * (n * g_ref[...])
        if not with_loss:
            y_ref, xn_ref = refs[npart + 4:]
            y_ref[...] = y.astype(BF16)
            xn_ref[...] = xn
            return
        t_ref, y_ref, d_ref, l_ref = refs[npart + 4:]
        y_ref[...] = y.astype(BF16)

        @pl.when(pl.program_id(0) == 0)
        def _():
            l_ref[...] = jnp.zeros_like(l_ref)

        e = xn - t_ref[...]
        l_ref[...] += 0.5 * jnp.sum(jnp.mean(e * e, axis=-1, keepdims=True), axis=0, keepdims=True)
        d_ref[...] = e * (1.0 / D)

    vec = pl.BlockSpec((1, D), lambda i: (0, 0))
    row = lambda w_: pl.BlockSpec((tm, w_), lambda i: (i, 0))
    in_specs = [row(a_.shape[1]) for a_ in a_parts] + [_full(w.shape), row(D), vec, vec]
    out_specs = [row(D), row(D)]
    out_shape = [jax.ShapeDtypeStruct((T, D), BF16), jax.ShapeDtypeStruct((T, D), F32)]
    if with_loss:
        in_specs.append(row(D))
        out_specs.append(_full((1, 1)))
        out_shape.append(jax.ShapeDtypeStruct((1, 1), F32))
    return pl.pallas_call(
        body, name=name, grid=(T // tm,), in_specs=in_specs, out_specs=out_specs, out_shape=out_shape,
        compiler_params=_cp(("arbitrary",) if with_loss else ("parallel",)),
    )(*a_parts, w, x, g, gt, *((target,) if with_loss else ()))


def post_bwd_mm(dxn, y, g, gt, w, *, tm, name):
    T, D = y.shape
    K = w.shape[0]

    def body(dxn_ref, y_ref, g_ref, gt_ref, w_ref, dy_ref, da_ref, dg_ref, dgt_ref):
        @pl.when(pl.program_id(0) == 0)
        def _():
            dg_ref[...] = jnp.zeros_like(dg_ref)
            dgt_ref[...] = jnp.zeros_like(dgt_ref)

        d = dxn_ref[...]
        n, r = _rms(y_ref[...].astype(F32))
        g_, gt_ = g_ref[...], gt_ref[...]
        dg_ref[...] += _colsum(d * gt_ * n)
        dgt_ref[...] += _colsum(d * g_ * n)
        dy = _rms_bwd(d * (gt_ * g_), n, r).astype(BF16)
        dy_ref[...] = dy
        da_ref[...] = _dot_nt(dy, w_ref[...]).astype(BF16)

    vec = pl.BlockSpec((1, D), lambda i: (0, 0))
    row = lambda w_: pl.BlockSpec((tm, w_), lambda i: (i, 0))
    return pl.pallas_call(
        body, name=name, grid=(T // tm,),
        in_specs=[row(D), row(D), vec, vec, _full((K, D))],
        out_specs=[row(D), row(K), vec, vec],
        out_shape=[jax.ShapeDtypeStruct((T, D), BF16), jax.ShapeDtypeStruct((T, K), BF16),
                   jax.ShapeDtypeStruct((1, D), F32), jax.ShapeDtypeStruct((1, D), F32)],
        compiler_params=_cp(("arbitrary",)),
    )(dxn, y, g, gt, w)


def mm_pre_bwd(dzs, wt, x, dres, g, sc, *, tm, w_row_off=0, name):
    T, N = dzs[0].shape
    D = x.shape[1]
    npart = len(dzs)
    off = w_row_off // N
    has_res = dres is not None

    def body(*refs):
        dz_refs = refs[:npart]
        w_refs = refs[npart:2 * npart]
        rest = refs[2 * npart:]
        x_ref = rest[0]
        dres_ref = rest[1] if has_res else None
        g_ref, sc_ref, dx_ref, dg_ref, dsh_ref, dsc_ref = rest[1 + has_res:]

        @pl.when(pl.program_id(0) == 0)
        def _():
            dg_ref[...] = jnp.zeros_like(dg_ref)
            dsh_ref[...] = jnp.zeros_like(dsh_ref)
            dsc_ref[...] = jnp.zeros_like(dsc_ref)

        dh = _dot(dz_refs[0][...], w_refs[0][...])
        for p in range(1, npart):
            dh = dh + _dot(dz_refs[p][...], w_refs[p][...])
        n, r = _rms(x_ref[...])
        g_, s1 = g_ref[...], 1.0 + sc_ref[...]
        dsh_ref[...] += _colsum(dh)
        dsc_ref[...] += _colsum(dh * n * g_)
        dg_ref[...] += _colsum(dh * s1 * n)
        dxp = _rms_bwd(dh * (g_ * s1), n, r)
        dx_ref[...] = dxp + dres_ref[...] if has_res else dxp

    vec = pl.BlockSpec((1, D), lambda i: (0, 0))
    row = pl.BlockSpec((tm, D), lambda i: (i, 0))
    w_specs = [pl.BlockSpec((N, D), (lambda i, p=p: (off + p, 0)), pipeline_mode=pl.Buffered(1)) for p in range(npart)]
    res_specs, res_args = ([row], (dres,)) if has_res else ([], ())
    return pl.pallas_call(
        body, name=name, grid=(T // tm,),
        in_specs=[pl.BlockSpec((tm, N), lambda i: (i, 0))] * npart + w_specs + [row] + res_specs + [vec, vec],
        out_specs=[row, vec, vec, vec],
        out_shape=[jax.ShapeDtypeStruct((T, D), F32)] + [jax.ShapeDtypeStruct((1, D), F32)] * 3,
        compiler_params=_cp(("arbitrary",)),
    )(*dzs, *([wt] * npart), x, *res_args, g, sc)


def wgrad(a_parts, b, *, tr, extra=None, name):
    T, R = a_parts[0].shape
    D = b.shape[1]
    npart = len(a_parts)
    nr = R // tr

    def body(*refs):
        a_refs, b_ref = refs[:npart], refs[npart]
        g_ref = refs[-1]
        for p in range(npart):
            @pl.when(pl.program_id(0) // nr == p)
            def _():
                acc = _dot_tn(a_refs[p][...], b_ref[...])
                if extra is not None:
                    acc += _dot_tn(refs[npart + 1][...], refs[npart + 2][...])
                g_ref[...] = acc.astype(BF16)

    in_specs = [pl.BlockSpec((T, tr), (lambda r, p=p: (0, jnp.clip(r - p * nr, 0, nr - 1)))) for p in range(npart)]
    in_specs.append(_full((T, D)))
    args = [*a_parts, b]
    if extra is not None:
        a2, b2 = extra
        in_specs += [pl.BlockSpec((a2.shape[0], tr), lambda r: (0, r)), _full(b2.shape)]
        args += [a2, b2]
    return pl.pallas_call(
        body, name=name, grid=(npart * nr,),
        in_specs=in_specs, out_specs=pl.BlockSpec((tr, D), lambda r: (r, 0)),
        out_shape=jax.ShapeDtypeStruct((npart * R, D), BF16),
        compiler_params=_cp(("parallel",)),
    )(*args)


def _conv_ext(ref, r0, rows, total):
    top = ref[pl.ds(pl.multiple_of(jnp.maximum(r0 - PAD, 0), PAD), PAD), :]
    mid = ref[pl.ds(r0, rows), :]
    bot = ref[pl.ds(pl.multiple_of(jnp.minimum(r0 + rows, total - PAD), PAD), PAD), :]
    top = jnp.where(r0 > 0, top, jnp.zeros_like(top))
    bot = jnp.where(r0 + rows < total, bot, jnp.zeros_like(bot))
    return jnp.concatenate([top, mid, bot], axis=0).astype(F32)


def _shift_rows(a, k):
    return pltpu.roll(a, k % a.shape[0], 0)


def _conv3(x, w, b):
    return w[0:1] * _shift_rows(x, 1) + w[1:2] * x + w[2:3] * _shift_rows(x, -1) + b


def _gate_up_specs(rows_, wblk, nb):
    return [pl.BlockSpec((rows_, wblk), lambda j: (0, j)), pl.BlockSpec((rows_, wblk), lambda j: (0, j + nb))]


def conv_fwd(hu, cw, cb, *, rows, wblk, name):
    L, N2 = hu.shape
    nb = N2 // 2 // wblk
    nchunk = L // rows

    def body(hg_ref, hu_ref, wg_ref, wu_ref, bg_ref, bu_ref, a_ref, s1_ref, s2_ref):
        def chunk(ci, carry):
            r0 = pl.multiple_of(ci * rows, rows)
            gate = _conv3(_conv_ext(hg_ref, r0, rows, L), wg_ref[...], bg_ref[...])[PAD:PAD + rows]
            up = _conv3(_conv_ext(hu_ref, r0, rows, L), wu_ref[...], bu_ref[...])[PAD:PAD + rows]
            sg = jax.nn.sigmoid(gate)
            silu = gate * sg
            at = pl.ds(r0, rows)
            a_ref[at, :] = (silu * up).astype(BF16)
            s1_ref[at, :] = silu.astype(BF16)
            s2_ref[at, :] = (up * (sg + silu * (1.0 - sg))).astype(BF16)
            return carry

        lax.fori_loop(0, nchunk, chunk, 0)

    out = pl.BlockSpec((L, wblk), lambda j: (0, j))
    return pl.pallas_call(
        body, name=name, grid=(nb,),
        in_specs=_gate_up_specs(L, wblk, nb) + _gate_up_specs(3, wblk, nb) + _gate_up_specs(1, wblk, nb),
        out_specs=[out] * 3, out_shape=[jax.ShapeDtypeStruct((L, N2 // 2), BF16)] * 3,
        compiler_params=_cp(("parallel",)),
    )(hu, hu, cw, cw, cb, cb)


def conv_bwd(da, s1, s2, hu, cw, *, rows, wblk, name):
    L, N2 = hu.shape
    F = N2 // 2
    nb = F // wblk
    nchunk = L // rows
    mid = slice(PAD, PAD + rows)

    def body(da_ref, s1_ref, s2_ref, hg_ref, hu_ref, wg_ref, wu_ref, dg_ref, du_ref, dwg_ref, dwu_ref, dbg_ref, dbu_ref):
        for ref in (dwg_ref, dwu_ref, dbg_ref, dbu_ref):
            ref[...] = jnp.zeros_like(ref)

        def half_bwd(x_ref, dh, w_ref, dx_ref, dw_ref, db_ref, r0):
            w = w_ref[...]
            nxt, prv = _shift_rows(dh, -1)[mid], _shift_rows(dh, 1)[mid]
            dhm, xm = dh[mid], x_ref[pl.ds(r0, rows), :].astype(F32)
            dx_ref[pl.ds(r0, rows), :] = (w[0:1] * nxt + w[1:2] * dhm + w[2:3] * prv).astype(BF16)
            db_ref[...] += _colsum(dhm)
            dw_ref[0:1, :] += _colsum(nxt * xm)
            dw_ref[1:2, :] += _colsum(dhm * xm)
            dw_ref[2:3, :] += _colsum(prv * xm)

        def chunk(ci, carry):
            r0 = pl.multiple_of(ci * rows, rows)
            d = _conv_ext(da_ref, r0, rows, L)
            half_bwd(hu_ref, d * _conv_ext(s1_ref, r0, rows, L), wu_ref, du_ref, dwu_ref, dbu_ref, r0)
            half_bwd(hg_ref, d * _conv_ext(s2_ref, r0, rows, L), wg_ref, dg_ref, dwg_ref, dbg_ref, r0)
            return carry

        lax.fori_loop(0, nchunk, chunk, 0)

    blk = lambda r: pl.BlockSpec((r, wblk), lambda j: (0, j))
    return pl.pallas_call(
        body, name=name, grid=(nb,),
        in_specs=[blk(L)] * 3 + _gate_up_specs(L, wblk, nb) + _gate_up_specs(3, wblk, nb),
        out_specs=[blk(L), blk(L), blk(3), blk(3), blk(1), blk(1)],
        out_shape=[jax.ShapeDtypeStruct((L, F), BF16)] * 2 + [jax.ShapeDtypeStruct((3, F), F32)] * 2
        + [jax.ShapeDtypeStruct((1, F), F32)] * 2,
        compiler_params=_cp(("parallel",)),
    )(da, s1, s2, hu, hu, cw, cw)


def _window_sums(pad_ref, w, lead):
    a = pad_ref[...]
    k = 1
    while k < w:
        a = a + _shift_rows(a, -k)
        k *= 2
    return _shift_rows(a, lead) if lead else a


def _pool_counts(L, h):
    t = lax.broadcasted_iota(jnp.int32, (L, 1), 0)
    return (jnp.minimum(t + h, L) - jnp.maximum(t - h, 0)).astype(F32)


def _pooled(u_ref, pad_ref, L, w):
    h = w // 2
    pad_ref[pl.ds(PAD, L), :] = u_ref[...]
    win = _window_sums(pad_ref, w, h)[PAD:PAD + L]
    return win / _pool_counts(L, h) - u_ref[...]


def _zero_pad_edges(pad_ref, L):
    z = jnp.zeros((PAD, LANES), F32)
    pad_ref[pl.ds(0, PAD), :] = z
    pad_ref[pl.ds(PAD + L, PAD), :] = z


def pool_fwd(u, w_pool, pool_scale, *, name):
    L = u.shape[0]

    def body(u_ref, w_ref, ps_ref, p_ref, pad_ref):
        _zero_pad_edges(pad_ref, L)
        for gi, win in enumerate(POOL_WINDOWS):
            @pl.when(pl.program_id(0) == gi)
            def _():
                pooled = _pooled(u_ref, pad_ref, L, win)
                p_ref[...] = (_dot(pooled.astype(BF16), w_ref[...].astype(BF16)) * ps_ref[...]).astype(BF16)

    return pl.pallas_call(
        body, name=name, grid=(len(POOL_WINDOWS),),
        in_specs=[pl.BlockSpec((L, LANES), lambda gi: (0, gi)), pl.BlockSpec((None, LANES, LANES), lambda gi: (gi, 0, 0)),
                  pl.BlockSpec((1, LANES), lambda gi: (0, gi))],
        out_specs=pl.BlockSpec((L, LANES), lambda gi: (0, gi)),
        out_shape=jax.ShapeDtypeStruct((L, 4 * LANES), BF16),
        scratch_shapes=[pltpu.VMEM((L + 2 * PAD, LANES), F32)],
        compiler_params=_cp(("parallel",)),
    )(u, w_pool, pool_scale)


def pool_bwd(u, dpa, w_pool, pool_scale, *, name):
    L = u.shape[0]

    def body(u_ref, dp_ref, w_ref, ps_ref, du_ref, dw_ref, dps_ref, pad_ref):
        _zero_pad_edges(pad_ref, L)
        for gi, win in enumerate(POOL_WINDOWS):
            @pl.when(pl.program_id(0) == gi)
            def _():
                h = win // 2
                wb = w_ref[...].astype(BF16)
                pooled = _pooled(u_ref, pad_ref, L, win).astype(BF16)
                dp = dp_ref[...].astype(F32)
                dps_ref[...] = _colsum(dp * _dot(pooled, wb))
                dy = (dp * ps_ref[...]).astype(BF16)
                dw_ref[...] = _dot_tn(pooled, dy)
                dpooled = _dot_nt(dy, wb)
                pad_ref[pl.ds(PAD, L), :] = dpooled / _pool_counts(L, h)
                du_ref[...] = (_window_sums(pad_ref, win, h - 1)[PAD:PAD + L] - dpooled).astype(BF16)

    return pl.pallas_call(
        body, name=name, grid=(len(POOL_WINDOWS),),
        in_specs=[pl.BlockSpec((L, LANES), lambda gi: (0, gi)), pl.BlockSpec((L, LANES), lambda gi: (0, gi)),
                  pl.BlockSpec((None, LANES, LANES), lambda gi: (gi, 0, 0)), pl.BlockSpec((1, LANES), lambda gi: (0, gi))],
        out_specs=[pl.BlockSpec((L, LANES), lambda gi: (0, gi)), pl.BlockSpec((None, LANES, LANES), lambda gi: (gi, 0, 0)),
                   pl.BlockSpec((1, LANES), lambda gi: (0, gi))],
        out_shape=[jax.ShapeDtypeStruct((L, 4 * LANES), BF16), jax.ShapeDtypeStruct((4, LANES, LANES), F32),
                   jax.ShapeDtypeStruct((1, 4 * LANES), F32)],
        scratch_shapes=[pltpu.VMEM((L + 2 * PAD, LANES), F32)],
        compiler_params=_cp(("parallel",)),
    )(u, dpa, w_pool, pool_scale)


def _attn_probs(qk, band_k, ctx_k, sink_ref, kh, mask4):
    s_loc = jnp.where(mask4, _dot_nt(qk, band_k), NEG_INF)
    s_ctx = _dot_nt(qk, ctx_k)
    sk = jnp.concatenate([jnp.full((BLK, 1), sink_ref[kh * GQA + hh], F32) for hh in range(GQA)], axis=0)
    m = jnp.maximum(jnp.maximum(jnp.max(s_loc, axis=-1, keepdims=True), jnp.max(s_ctx, axis=-1, keepdims=True)), sk)
    e_loc, e_ctx, e_s = jnp.exp(s_loc - m), jnp.exp(s_ctx - m), jnp.exp(sk - m)
    inv = 1.0 / (jnp.sum(e_loc, axis=-1, keepdims=True) + jnp.sum(e_ctx, axis=-1, keepdims=True) + e_s)
    return e_loc * inv, e_ctx * inv, e_s * inv


def _attn_block(n, L):
    start = pl.multiple_of(jnp.clip((n - 1) * BLK, 0, L - 3 * BLK), BLK)
    qpos = n * BLK + lax.broadcasted_iota(jnp.int32, (BLK, 3 * BLK), 0)
    kpos = start + lax.broadcasted_iota(jnp.int32, (BLK, 3 * BLK), 1)
    mask = jnp.abs(kpos - qpos) <= WINDOW
    return start, jnp.concatenate([mask] * GQA, axis=0)


def _stack_slabs(ref, rows=slice(None)):
    return jnp.concatenate([ref[rows, s * LANES:(s + 1) * LANES] for s in range(GQA)], axis=0)


def _kv_head_lanes(kh):
    return (lax.broadcasted_iota(jnp.int32, (1, LANES), 1) // HEAD_DIM) == kh


def permute_heads(w, inverse=False):
    lo, hi = 4 * LANES, 8 * LANES
    mid = w[lo:hi].reshape(*((GQA, N_KV_HEADS) if inverse else (N_KV_HEADS, GQA)), HEAD_DIM, w.shape[1])
    return jnp.concatenate([w[:lo], mid.swapaxes(0, 1).reshape(hi - lo, w.shape[1]), w[hi:]], axis=0)


def attn_fwd(q, kv, kvc, sink, *, qb, name):
    L = q.shape[0]
    C = kvc.shape[0]
    scale = HEAD_DIM ** -0.5

    def body(q_ref, kv_ref, kvc_ref, sink_ref, o_ref):
        kvc_ = kvc_ref[...]
        for b in range(qb):
            rows = slice(b * BLK, (b + 1) * BLK)
            start, mask4 = _attn_block(pl.program_id(0) * qb + b, L)
            band = kv_ref[pl.ds(start, 3 * BLK), :]
            qs = _stack_slabs(q_ref, rows) * scale
            o = jnp.zeros((GQA * BLK, LANES), F32)
            for kh in range(N_KV_HEADS):
                grp = _kv_head_lanes(kh)
                qk = jnp.where(grp, qs, jnp.zeros_like(qs))
                p_loc, p_ctx, _ = _attn_probs(qk, band[:, :LANES], kvc_[:, :LANES], sink_ref, kh, mask4)
                o = o + jnp.where(grp, _dot(p_loc.astype(BF16), band[:, LANES:]) + _dot(p_ctx.astype(BF16), kvc_[:, LANES:]), 0.0)
            for s in range(GQA):
                o_ref[rows, s * LANES:(s + 1) * LANES] = o[s * BLK:(s + 1) * BLK].astype(BF16)

    return pl.pallas_call(
        body, name=name, grid=(L // (qb * BLK),),
        in_specs=[pl.BlockSpec((qb * BLK, 4 * LANES), lambda n: (n, 0)), _full((L, 2 * LANES)), _full((C, 2 * LANES)),
                  pl.BlockSpec(memory_space=pltpu.SMEM)],
        out_specs=pl.BlockSpec((qb * BLK, 4 * LANES), lambda n: (n, 0)),
        out_shape=jax.ShapeDtypeStruct((L, 4 * LANES), BF16),
        compiler_params=_cp(("parallel",)),
    )(q, kv, kvc, sink)


def attn_bwd(q, kv, kvc, sink, dpa, cos, sa, sb, *, qb, name):
    L = q.shape[0]
    C = kvc.shape[0]
    nsteps = L // (qb * BLK)
    scale = HEAD_DIM ** -0.5

    def body(q_ref, kv_ref, kvc_ref, sink_ref, do_ref, c_ref, sa_ref, sb_ref, cq_ref, saq_ref, sbq_ref,
             dq_ref, dkv_ref, dkvc_ref, dsink_ref, dkv_acc, dkvc_acc):
        step = pl.program_id(0)

        @pl.when(step == 0)
        def _():
            dkv_acc[...] = jnp.zeros_like(dkv_acc)
            dkvc_acc[...] = jnp.zeros_like(dkvc_acc)
            dsink_ref[...] = jnp.zeros_like(dsink_ref)

        for b in range(qb):
            one_block(step * qb + b, slice(b * BLK, (b + 1) * BLK), q_ref, kv_ref, kvc_ref, sink_ref, do_ref, cq_ref, saq_ref, sbq_ref,
                      dq_ref, dsink_ref, dkv_acc, dkvc_acc)

        @pl.when(step == nsteps - 1)
        def _():
            dkv_ref[:, :LANES] = _rope(dkv_acc[:LANES, :].T, c_ref[...], -sa_ref[...], -sb_ref[...]).astype(BF16)
            dkv_ref[:, LANES:] = dkv_acc[LANES:, :].T.astype(BF16)
            dkvc_ref[...] = dkvc_acc[...].T.astype(BF16)

    def one_block(n, rows, q_ref, kv_ref, kvc_ref, sink_ref, do_ref, cq_ref, saq_ref, sbq_ref, dq_ref, dsink_ref, dkv_acc, dkvc_acc):
        start, mask4 = _attn_block(n, L)
        band = kv_ref[pl.ds(start, 3 * BLK), :]
        kvc_ = kvc_ref[...]
        band_k, band_v, ctx_k, ctx_v = band[:, :LANES], band[:, LANES:], kvc_[:, :LANES], kvc_[:, LANES:]
        qs = _stack_slabs(q_ref, rows) * scale
        dos = _stack_slabs(do_ref, rows)
        lane = lax.broadcasted_iota(jnp.int32, (1, LANES), 1)
        dsink = jnp.zeros((1, LANES), F32)
        dq = jnp.zeros((GQA * BLK, LANES), F32)
        dk = jnp.zeros((LANES, 3 * BLK), F32)
        dv = jnp.zeros((LANES, 3 * BLK), F32)
        dkc = jnp.zeros((LANES, C), F32)
        dvc = jnp.zeros((LANES, C), F32)
        for kh in range(N_KV_HEADS):
            grp = _kv_head_lanes(kh)
            qk = jnp.where(grp, qs, jnp.zeros_like(qs))
            dok = jnp.where(grp, dos, jnp.zeros_like(dos))
            p_loc, p_ctx, p_s = _attn_probs(qk, band_k, ctx_k, sink_ref, kh, mask4)
            dp_loc = _dot_nt(dok, band_v)
            dp_ctx = _dot_nt(dok, ctx_v)
            delta = jnp.sum(p_loc * dp_loc, axis=-1, keepdims=True) + jnp.sum(p_ctx * dp_ctx, axis=-1, keepdims=True)
            ds_loc = (p_loc * (dp_loc - delta)).astype(BF16)
            ds_ctx = (p_ctx * (dp_ctx - delta)).astype(BF16)
            dsk = p_s * delta
            for hh in range(GQA):
                dsink = dsink - jnp.where(lane == kh * GQA + hh, jnp.sum(dsk[hh * BLK:(hh + 1) * BLK], axis=0, keepdims=True), 0.0)
            dq = dq + jnp.where(grp, _dot(ds_loc, band_k) + _dot(ds_ctx, ctx_k), 0.0)
            dk = dk + _dot_tn(qk, ds_loc)
            dv = dv + _dot_tn(dok, p_loc.astype(BF16))
            dkc = dkc + _dot_tn(qk, ds_ctx)
            dvc = dvc + _dot_tn(dok, p_ctx.astype(BF16))
        dsink_ref[...] += dsink
        dkv_acc[:LANES, pl.ds(start, 3 * BLK)] += dk
        dkv_acc[LANES:, pl.ds(start, 3 * BLK)] += dv
        dkvc_acc[:LANES, :] += dkc
        dkvc_acc[LANES:, :] += dvc
        c, a, b_ = cq_ref[rows, :], -saq_ref[rows, :], -sbq_ref[rows, :]
        for s in range(GQA):
            dq_ref[rows, s * LANES:(s + 1) * LANES] = _rope(dq[s * BLK:(s + 1) * BLK] * scale, c, a, b_).astype(BF16)

    blk = lambda w: pl.BlockSpec((qb * BLK, w), lambda n: (n, 0))
    return pl.pallas_call(
        body, name=name, grid=(nsteps,),
        in_specs=[blk(4 * LANES), _full((L, 2 * LANES)), _full((C, 2 * LANES)), pl.BlockSpec(memory_space=pltpu.SMEM),
                  pl.BlockSpec((qb * BLK, 4 * LANES), lambda n: (n, 1)),
                  _full((L, LANES)), _full((L, LANES)), _full((L, LANES)), blk(LANES), blk(LANES), blk(LANES)],
        out_specs=[blk(4 * LANES), _full((L, 2 * LANES)), _full((C, 2 * LANES)), _full((1, LANES))],
        out_shape=[jax.ShapeDtypeStruct((L, 4 * LANES), BF16), jax.ShapeDtypeStruct((L, 2 * LANES), BF16),
                   jax.ShapeDtypeStruct((C, 2 * LANES), BF16), jax.ShapeDtypeStruct((1, LANES), F32)],
        scratch_shapes=[pltpu.VMEM((2 * LANES, L), F32), pltpu.VMEM((2 * LANES, C), F32)],
        compiler_params=_cp(("arbitrary",)),
    )(q, kv, kvc, sink, dpa, cos, sa, sb, cos, sa, sb)


def _gelu_parts(x):
    th = jnp.tanh(SQRT_2_OVER_PI * (x + GELU_C * x * x * x))
    return 0.5 * x * (1.0 + th), th


def _gelu_grad(x, th):
    return 0.5 * (1.0 + th) + 0.5 * x * (1.0 - th * th) * SQRT_2_OVER_PI * (1.0 + 3.0 * GELU_C * x * x)


def _layernorm(v):
    mu = jnp.mean(v, axis=-1, keepdims=True)
    vc = v - mu
    rstd = lax.rsqrt(jnp.mean(vc * vc, axis=-1, keepdims=True) + EPS)
    return vc * rstd, rstd


def sgu_fwd(z1, ln_g, ln_b, ws, bst, *, rows, name):
    L, W2 = z1.shape
    W = W2 // 2
    ng = W // LANES

    def body(z_ref, g_ref, b_ref, ws_ref, bs_ref, o_ref):
        for c in range(rows // BLK):
            at = slice(c * BLK, (c + 1) * BLK)
            z, _ = _gelu_parts(z_ref[at, :].astype(F32))
            xhat, _ = _layernorm(z[:, W:])
            vln = (xhat * g_ref[...] + b_ref[...]).astype(BF16)
            for gi in range(ng):
                cs = slice(gi * LANES, (gi + 1) * LANES)
                s = _dot(ws_ref[gi], vln[:, cs]) + bs_ref[:, gi:gi + 1]
                o_ref[at, cs] = (z[:, cs] * s).astype(BF16)

    vec = _full((1, W))
    return pl.pallas_call(
        body, name=name, grid=(L // rows,),
        in_specs=[pl.BlockSpec((rows, W2), lambda n: (n, 0)), vec, vec, _full((ng, LANES, LANES)), _full((BLK, ng))],
        out_specs=pl.BlockSpec((rows, W), lambda n: (n, 0)),
        out_shape=jax.ShapeDtypeStruct((L, W), BF16),
        compiler_params=_cp(("parallel",)),
    )(z1, ln_g, ln_b, ws, bst)


def sgu_bwd(z1, dus, ln_g, ln_b, ws, wst, bst, *, rows, name):
    L, W2 = z1.shape
    W = W2 // 2
    ng = W // LANES

    def body(z_ref, d_ref, g_ref, b_ref, ws_ref, wst_ref, bs_ref, dz_ref, dws_ref, dbs_ref, dg_ref, db_ref, dv_scr):
        @pl.when(pl.program_id(0) == 0)
        def _():
            dws_ref[...] = jnp.zeros_like(dws_ref)
            dbs_ref[...] = jnp.zeros_like(dbs_ref)
            dg_ref[...] = jnp.zeros_like(dg_ref)
            db_ref[...] = jnp.zeros_like(db_ref)

        for c in range(rows // BLK):
            at = slice(c * BLK, (c + 1) * BLK)
            zp = z_ref[at, :].astype(F32)
            z, th = _gelu_parts(zp)
            xhat, rstd = _layernorm(z[:, W:])
            vln = (xhat * g_ref[...] + b_ref[...]).astype(BF16)
            d = d_ref[at, :].astype(F32)
            lane = lax.broadcasted_iota(jnp.int32, (1, LANES), 1)
            dbs = jnp.zeros((BLK, LANES), F32)
            dgel = _gelu_grad(zp, th)
            for gi in range(ng):
                cs = slice(gi * LANES, (gi + 1) * LANES)
                s = _dot(ws_ref[gi], vln[:, cs]) + bs_ref[:, gi:gi + 1]
                dz_ref[at, cs] = (d[:, cs] * s * dgel[:, cs]).astype(BF16)
                ds = d[:, cs] * z[:, cs]
                dbs = dbs + jnp.where(lane == gi, jnp.sum(ds, axis=-1, keepdims=True), 0.0)
                dsb = ds.astype(BF16)
                dws_ref[gi] += _dot_nt(dsb, vln[:, cs])
                dv_scr[:, cs] = _dot(wst_ref[gi], dsb)
            dbs_ref[...] += dbs
            dvln = dv_scr[...]
            dg_ref[...] += _colsum(dvln * xhat)
            db_ref[...] += _colsum(dvln)
            dxh = dvln * g_ref[...]
            dv = rstd * (dxh - jnp.mean(dxh, axis=-1, keepdims=True) - xhat * jnp.mean(dxh * xhat, axis=-1, keepdims=True))
            dz_ref[at, W:] = (dv * dgel[:, W:]).astype(BF16)

    vec = _full((1, W))
    return pl.pallas_call(
        body, name=name, grid=(L // rows,),
        in_specs=[pl.BlockSpec((rows, W2), lambda n: (n, 0)), pl.BlockSpec((rows, W), lambda n: (n, 0)), vec, vec,
                  _full((ng, LANES, LANES)), _full((ng, LANES, LANES)), _full((BLK, ng))],
        out_specs=[pl.BlockSpec((rows, W2), lambda n: (n, 0)), _full((ng, LANES, LANES)), _full((BLK, LANES)), vec, vec],
        out_shape=[jax.ShapeDtypeStruct((L, W2), BF16), jax.ShapeDtypeStruct((ng, LANES, LANES), F32),
                   jax.ShapeDtypeStruct((BLK, LANES), F32), jax.ShapeDtypeStruct((1, W), F32), jax.ShapeDtypeStruct((1, W), F32)],
        scratch_shapes=[pltpu.VMEM((BLK, W), F32)],
        compiler_params=_cp(("arbitrary",)),
    )(z1, dus, ln_g, ln_b, ws, wst, bst)


def _adamw_math(w, m, v, g):
    m_ = ADAM_B1 * m + (1.0 - ADAM_B1) * g
    v_ = ADAM_B2 * v + (1.0 - ADAM_B2) * (g * g)
    return -ADAM_LR * ((m_ / BC1) / (jnp.sqrt(v_ / BC2) + ADAM_EPS) + ADAM_WD * w), m_, v_


def adamw(w, m, v, gparts, *, tr, name):
    NL, R, Wd = w.shape
    nr = R // tr

    def body(w_ref, m_ref, v_ref, *rest):
        gp_refs, (g_ref, d_ref, nm_ref, nv_ref) = rest[:NL], rest[NL:]
        for l in range(NL):
            @pl.when(pl.program_id(0) == l)
            def _():
                g = gp_refs[l][0].astype(F32)
                for s in range(1, gp_refs[l].shape[0]):
                    g = g + gp_refs[l][s].astype(F32)
                g_ref[...] = g
                d_ref[...], nm_ref[...], nv_ref[...] = _adamw_math(w_ref[...], m_ref[...], v_ref[...], g)

    row = pl.BlockSpec((None, tr, Wd), lambda l, i: (l, i, 0))
    gspecs = [pl.BlockSpec((gparts[l].shape[0], tr, Wd), (lambda l_, i, l=l: (0, jnp.clip(i + (l_ - l) * nr, 0, nr - 1), 0)))
              for l in range(NL)]
    return pl.pallas_call(
        body, name=name, grid=(NL, nr),
        in_specs=[row, row, row] + gspecs, out_specs=[row] * 4, out_shape=[jax.ShapeDtypeStruct((NL, R, Wd), F32)] * 4,
        compiler_params=_cp(("arbitrary", "arbitrary")),
    )(w, m, v, *gparts)


def small_update(gpacks, me, params, loss_row, *, name):
    n = len(params)

    def body(me_ref, gp_ref, *refs):
        ins, outs, gs_ref = refs[:3 * n], refs[3 * n:-1], refs[-1]
        gs_ref[...] = gp_ref[0].astype(F32)
        for dv in range(1, N_DEV):
            gs_ref[...] += gp_ref[dv].astype(F32)
        for p, (w, _, _, off, per_dev) in enumerate(params):
            w_ref, m_ref, v_ref = ins[3 * p:3 * p + 3]
            g_ref, d_ref, nm_ref, nv_ref = outs[4 * p:4 * p + 4]
            rows, cols = w.shape
            if cols == LANES and rows % 8 == 0 and not per_dev:
                g = gs_ref[off:off + rows, :]
                g_ref[...] = g
                d_ref[...], nm_ref[...], nv_ref[...] = _adamw_math(w_ref[...], m_ref[...], v_ref[...], g)
                continue
            chunks = -(-cols // LANES)
            base = off + me_ref[0] * per_dev if per_dev else off
            for i in range(rows):
                for j in range(chunks):
                    wd = min(LANES, cols - j * LANES)
                    at = (slice(i, i + 1), slice(j * LANES, j * LANES + wd))
                    g = gs_ref[pl.ds(base + i * chunks + j, 1), 0:wd]
                    g_ref[at] = g
                    d_ref[at], nm_ref[at], nv_ref[at] = _adamw_math(w_ref[at], m_ref[at], v_ref[at], g)
        outs[-1][...] = jnp.sum(gs_ref[loss_row:loss_row + 1, :], axis=1, keepdims=True)

    flat = [a for w, m, v, _, _ in params for a in (w, m, v)]
    out_shape = [jax.ShapeDtypeStruct(w.shape, F32) for w, _, _, _, _ in params for _ in range(4)] + [jax.ShapeDtypeStruct((1, 1), F32)]
    return pl.pallas_call(
        body, name=name, grid=(1,),
        in_specs=[pl.BlockSpec(memory_space=pltpu.SMEM), _full(gpacks.shape)] + [_full(a.shape) for a in flat],
        out_specs=[_full(o.shape) for o in out_shape], out_shape=out_shape,
        scratch_shapes=[pltpu.VMEM(gpacks.shape[1:], F32)],
        compiler_params=_cp(("arbitrary",)),
    )(me, gpacks, *flat)


def ada_fwd_mm(cs, w_ada, b_loc, *, name):
    R, D = cs.shape
    nl, _, n = w_ada.shape

    def body(c_ref, w_ref, b_ref, s_ref, m_ref):
        c = c_ref[...]
        s = c * jax.nn.sigmoid(c)
        s_ref[...] = s
        for i in range(nl):
            m_ref[i] = _dot(s.astype(BF16), w_ref[i].astype(BF16)) + b_ref[i:i + 1, :]

    return pl.pallas_call(
        body, name=name, in_specs=[_full((R, D)), _full((nl, D, n)), _full((nl, n))],
        out_specs=[_full((R, D)), _full((nl, R, n))], grid=(1,),
        out_shape=[jax.ShapeDtypeStruct((R, D), F32), jax.ShapeDtypeStruct((nl, R, n), F32)],
        compiler_params=_cp(("arbitrary",)),
    )(cs, w_ada, b_loc)


def ada_bwd_mm(s, c_ctx, dall, w_ada, *, name):
    R, D = s.shape
    nl, _, n = w_ada.shape

    def body(s_ref, cc_ref, d_ref, w_ref, gw_ref, dcc_ref):
        sb = s_ref[...].astype(BF16)
        row = lax.broadcasted_iota(jnp.int32, (R, 1), 0)
        dctx = d_ref[0, 1:2, :]
        for dv in range(1, N_DEV):
            dctx = dctx + d_ref[dv, 1:2, :]
        for i in range(nl):
            dm = jnp.zeros((R, n), F32)
            for dv in range(N_DEV):
                dm = dm + jnp.where(row == dv, d_ref[dv, 2 * i:2 * i + 1, :], 0.0)
            if i == 0:
                dm = dm + jnp.where(row == N_DEV, dctx, 0.0)
            gw_ref[i] = _dot_tn(sb, dm.astype(BF16))
        cc = cc_ref[...]
        sg = jax.nn.sigmoid(cc)
        ds = _dot_nt(jnp.broadcast_to(dctx, (8, n)).astype(BF16), w_ref[0].astype(BF16))
        dcc_ref[...] = ds * (sg * (1.0 + cc * (1.0 - sg)))

    return pl.pallas_call(
        body, name=name, grid=(1,),
        in_specs=[_full((R, D)), _full((1, D)), _full((N_DEV, 3, n)), _full((nl, D, n))],
        out_specs=[_full((nl, D, n)), _full((8, D))],
        out_shape=[jax.ShapeDtypeStruct((nl, D, n), F32), jax.ShapeDtypeStruct((8, D), F32)],
        compiler_params=_cp(("arbitrary",)),
    )(s, c_ctx, dall, w_ada)


def _place():
    x, y, c = lax.axis_index("x"), lax.axis_index("y"), lax.axis_index("c")
    return x, y, c


def _lin(p):
    return 4 * p[0] + 2 * p[1] + p[2]


def all_gather_small(xb, *, name):
    R, W = xb.shape

    def body(x_ref, out_ref, send_sems, recv_sems, local_sem):
        x, y, c = _place()
        me = _lin((x, y, c))
        mine = pltpu.make_async_copy(x_ref, out_ref.at[me], local_sem)
        mine.start()
        copies = []
        for k in range(1, N_DEV):
            peer = (x ^ (k >> 2), y ^ ((k >> 1) & 1), c ^ (k & 1))
            mk = lambda dst, k=k, peer=peer: pltpu.make_async_remote_copy(
                src_ref=x_ref, dst_ref=dst, send_sem=send_sems.at[k - 1], recv_sem=recv_sems.at[k - 1], device_id=peer, device_id_type=MESH)
            mk(out_ref.at[me]).start()
            copies.append(mk(out_ref.at[_lin(peer)]))
        for cp in copies:
            cp.wait_recv()
        for cp in copies:
            cp.wait_send()
        mine.wait()

    vm = pl.BlockSpec(memory_space=pltpu.VMEM)
    return pl.pallas_call(
        body, name=name, in_specs=[vm], out_specs=vm, out_shape=jax.ShapeDtypeStruct((N_DEV, R, W), xb.dtype),
        scratch_shapes=[pltpu.SemaphoreType.DMA((7,)), pltpu.SemaphoreType.DMA((7,)), pltpu.SemaphoreType.DMA],
        compiler_params=pltpu.CompilerParams(vmem_limit_bytes=VMEM_LIMIT),
    )(xb)


HBM_SPEC = pl.BlockSpec(memory_space=pltpu.HBM)
SEM_SPEC = pl.BlockSpec(memory_space=pltpu.SEMAPHORE)
ORDERED_EFFECT = pltpu.SideEffectType.DATAFLOW_SIDE_EFFECTING


def _exchange_copies(srcs, lands, sems, scatter):
    x, y, c = _place()
    me = _lin((x, y, c))
    for j in range(len(srcs)):
        r = lands[j].shape[0] // N_DEV
        block = lambda d, j=j, r=r: pl.ds(pl.multiple_of(d * r, 16), r)
        for k in range(1, N_DEV):
            peer = (x ^ (k >> 2), y ^ ((k >> 1) & 1), c ^ (k & 1))
            src = srcs[j].at[block(_lin(peer)), :] if scatter else srcs[j]
            mk = lambda dst, j=j, k=k, peer=peer, src=src: pltpu.make_async_remote_copy(
                src_ref=src, dst_ref=dst, send_sem=sems[2 * j].at[k - 1], recv_sem=sems[2 * j + 1].at[k - 1],
                device_id=peer, device_id_type=MESH)
            yield mk(lands[j].at[block(me), :]), mk(lands[j].at[block(_lin(peer)), :])


def exchange_start(srcs, lands, *, scatter, name):
    nw = len(srcs)

    def body(*refs):
        for start, _ in _exchange_copies(refs[:nw], refs[nw:2 * nw], refs[2 * nw:4 * nw], scatter):
            start.start()
        refs[-1][...] = jnp.zeros_like(refs[-1])

    thru = [pltpu.HBM(a.shape, a.dtype) for a in (*srcs, *lands)]
    res = pl.pallas_call(
        body, name=name, in_specs=[HBM_SPEC] * (2 * nw),
        out_specs=[SEM_SPEC] * (2 * nw) + [HBM_SPEC] * (2 * nw) + [pl.BlockSpec(memory_space=pltpu.VMEM)],
        out_shape=[pltpu.SemaphoreType.DMA((N_DEV - 1,))] * (2 * nw) + thru + [jax.ShapeDtypeStruct((8, LANES), F32)],
        input_output_aliases={i: 2 * nw + i for i in range(2 * nw)},
        compiler_params=pltpu.CompilerParams(has_side_effects=ORDERED_EFFECT),
    )(*[pltpu.with_memory_space_constraint(a, pltpu.HBM) for a in (*srcs, *lands)])
    return res[:2 * nw], res[2 * nw:3 * nw], res[3 * nw:4 * nw], res[-1]


def exchange_wait(srcs, lands, sems, after, *, scatter, name):
    nw = len(srcs)
    after = list(after) if isinstance(after, (list, tuple)) else [after]

    def body(*refs):
        for _, arrive in _exchange_copies(refs[:nw], refs[nw:2 * nw], refs[2 * nw:4 * nw], scatter):
            arrive.wait_send()
            arrive.wait_recv()

    res = pl.pallas_call(
        body, name=name, in_specs=[HBM_SPEC] * (2 * nw) + [SEM_SPEC] * (2 * nw) + [pl.BlockSpec(memory_space=pl.ANY)] * len(after),
        out_specs=[HBM_SPEC] * (2 * nw), out_shape=[pltpu.HBM(a.shape, a.dtype) for a in (*srcs, *lands)],
        input_output_aliases={i: i for i in range(2 * nw)},
        compiler_params=pltpu.CompilerParams(has_side_effects=ORDERED_EFFECT),
    )(*srcs, *lands, *sems, *after)
    return res[nw:]


def place_own(srcs, rows, me, *, scatter, name):
    nw = len(srcs)
    lands = [lax.empty((N_DEV * r, s_.shape[1]), s_.dtype) for r, s_ in zip(rows, srcs)]

    def body(me_ref, *refs):
        for j in range(nw):
            refs[2 * nw + j][...] = refs[j][...]

    mine = lambda i, me_ref: (me_ref[0], 0)
    src_at = mine if scatter else (lambda i, me_ref: (0, 0))
    blocks = [(r, s_.shape[1]) for r, s_ in zip(rows, srcs)]
    return pl.pallas_call(
        body, name=name,
        grid_spec=pltpu.PrefetchScalarGridSpec(
            num_scalar_prefetch=1, grid=(1,),
            in_specs=[pl.BlockSpec(b_, src_at) for b_ in blocks] + [pl.BlockSpec(memory_space=pl.ANY)] * nw,
            out_specs=[pl.BlockSpec(b_, mine) for b_ in blocks]),
        out_shape=[jax.ShapeDtypeStruct(l_.shape, l_.dtype) for l_ in lands],
        input_output_aliases={1 + nw + j: j for j in range(nw)},
        compiler_params=_cp(("arbitrary",)),
    )(jnp.reshape(me, (1,)).astype(jnp.int32), *srcs, *lands)


def _rope_tables(L):
    t = jnp.arange(L)
    inv = ROPE_BASE ** (-jnp.arange(ROPE_FREQS, dtype=F32) / ROPE_FREQS)
    ar = (t // GRID_W).astype(F32)[:, None] * inv
    ac = (t % GRID_W).astype(F32)[:, None] * inv
    z = jnp.zeros_like(ar)
    cos = jnp.concatenate([jnp.cos(ar), jnp.cos(ar), jnp.cos(ac), jnp.cos(ac)], axis=1)
    sa = jnp.concatenate([-jnp.sin(ar), z, -jnp.sin(ac), z], axis=1)
    sb = jnp.concatenate([z, jnp.sin(ar), z, jnp.sin(ac)], axis=1)
    return tuple(jnp.tile(a, (1, LANES // HEAD_DIM)) for a in (cos, sa, sb))


def _nat2d(a):
    return a.reshape(1, -1) if a.ndim == 1 else a.reshape(-1, a.shape[-1])


def _pack_rows(a):
    rows, cols = a.shape
    chunks = -(-cols // LANES)
    f = jnp.pad(a, ((0, 0), (0, chunks * LANES - cols))).reshape(rows * chunks, LANES)
    return jnp.pad(f, ((0, -f.shape[0] % 8), (0, 0)))


def _rows128(a):
    f = a.reshape(-1)
    n = -(-f.shape[0] // (8 * LANES)) * 8 * LANES
    return jnp.pad(f, (0, n - f.shape[0])).reshape(-1, LANES)


def kernel(x, c, ctx, c_ctx, w_ada, b_ada, g_mix_pre, g_mix_post, g_ffn_pre, g_ffn_post, w_in_even, w_pool, pool_scale, attn_sink, w_out_even, w_in_odd, sgu_ln_g, sgu_ln_b, sgu_w, sgu_b, w_out_odd, w_ffn_up, ffn_conv_w, ffn_conv_b, w_ffn_down, loss_target, m_c_ctx, m_w_ada, m_b_ada, m_g_mix_pre, m_g_mix_post, m_g_ffn_pre, m_g_ffn_post, m_w_in_even, m_w_pool, m_pool_scale, m_attn_sink, m_w_out_even, m_w_in_odd, m_sgu_ln_g, m_sgu_ln_b, m_sgu_w, m_sgu_b, m_w_out_odd, m_w_ffn_up, m_ffn_conv_w, m_ffn_conv_b, m_w_ffn_down, v_c_ctx, v_w_ada, v_b_ada, v_g_mix_pre, v_g_mix_post, v_g_ffn_pre, v_g_ffn_post, v_w_in_even, v_w_pool, v_pool_scale, v_attn_sink, v_w_out_even, v_w_in_odd, v_sgu_ln_g, v_sgu_ln_b, v_sgu_w, v_sgu_b, v_w_out_odd, v_w_ffn_up, v_ffn_conv_w, v_ffn_conv_b, v_w_ffn_down):
    P = dict(c_ctx=c_ctx, w_ada=w_ada, b_ada=b_ada, g_mix_pre=g_mix_pre, g_mix_post=g_mix_post, g_ffn_pre=g_ffn_pre,
             g_ffn_post=g_ffn_post, w_in_even=w_in_even, w_pool=w_pool, pool_scale=pool_scale, attn_sink=attn_sink,
             w_out_even=w_out_even, w_in_odd=w_in_odd, sgu_ln_g=sgu_ln_g, sgu_ln_b=sgu_ln_b, sgu_w=sgu_w, sgu_b=sgu_b,
             w_out_odd=w_out_odd, w_ffn_up=w_ffn_up, ffn_conv_w=ffn_conv_w, ffn_conv_b=ffn_conv_b, w_ffn_down=w_ffn_down)
    M = dict(c_ctx=m_c_ctx, w_ada=m_w_ada, b_ada=m_b_ada, g_mix_pre=m_g_mix_pre, g_mix_post=m_g_mix_post, g_ffn_pre=m_g_ffn_pre,
             g_ffn_post=m_g_ffn_post, w_in_even=m_w_in_even, w_pool=m_w_pool, pool_scale=m_pool_scale, attn_sink=m_attn_sink,
             w_out_even=m_w_out_even, w_in_odd=m_w_in_odd, sgu_ln_g=m_sgu_ln_g, sgu_ln_b=m_sgu_ln_b, sgu_w=m_sgu_w, sgu_b=m_sgu_b,
             w_out_odd=m_w_out_odd, w_ffn_up=m_w_ffn_up, ffn_conv_w=m_ffn_conv_w, ffn_conv_b=m_ffn_conv_b, w_ffn_down=m_w_ffn_down)
    V = dict(c_ctx=v_c_ctx, w_ada=v_w_ada, b_ada=v_b_ada, g_mix_pre=v_g_mix_pre, g_mix_post=v_g_mix_post, g_ffn_pre=v_g_ffn_pre,
             g_ffn_post=v_g_ffn_post, w_in_even=v_w_in_even, w_pool=v_w_pool, pool_scale=v_pool_scale, attn_sink=v_attn_sink,
             w_out_even=v_w_out_even, w_in_odd=v_w_in_odd, sgu_ln_g=v_sgu_ln_g, sgu_ln_b=v_sgu_ln_b, sgu_w=v_sgu_w, sgu_b=v_sgu_b,
             w_out_odd=v_w_out_odd, w_ffn_up=v_w_ffn_up, ffn_conv_w=v_ffn_conv_w, ffn_conv_b=v_ffn_conv_b, w_ffn_down=v_w_ffn_down)

    x = x[0]
    ctx = ctx[0]
    target = loss_target[0]
    L, D = x.shape
    C = ctx.shape[0]
    tm = min(512, L)
    tm_out = min(1024, L)
    conv_rows = min(1024, L)
    me = 4 * lax.axis_index("x") + 2 * lax.axis_index("y") + lax.axis_index("c")
    n_ada = w_ada.shape[2]
    F = w_ffn_down.shape[1] * N_DEV
    half_f = F // 2

    n_cw = ffn_conv_w.shape[2]
    small = jnp.concatenate([_rows128(c), _rows128(sgu_ln_g), _rows128(sgu_ln_b), _rows128(ffn_conv_w)], axis=0)
    small_all = all_gather_small(small, name="gather_small_inputs")
    c_all = small_all[:, :8].reshape(N_DEV, D)
    ln_g = small_all[:, 8].reshape(1, D)
    ln_b = small_all[:, 16].reshape(1, D)
    conv_w = small_all[:, 24:].reshape(N_DEV, -1)[:, :2 * 3 * n_cw].reshape(N_DEV, 2, 3, n_cw)
    conv_w = conv_w.transpose(1, 2, 0, 3).reshape(2, 3, 2 * F)

    cs = jnp.concatenate([c_all, c_ctx[None, :], jnp.zeros((7, D), F32)], axis=0)
    b_loc = lax.dynamic_slice(b_ada, (0, me * n_ada), (2, n_ada))
    silu_c, mods_loc = ada_fwd_mm(cs, w_ada, b_loc, name="ada_fwd")
    mods_all = all_gather_small(mods_loc.reshape(-1, LANES), name="gather_mods")

    shards = [s.astype(BF16) for s in (w_in_even[0].T, w_out_even[0], w_ffn_up[0].T, w_ffn_down[0],
                                       w_in_odd[0].T, w_out_odd[0], w_ffn_up[1].T, w_ffn_down[1])]
    shards, mods_all = lax.optimization_barrier((shards, mods_all))
    w_sems, w_srcs, w_lands, _ = exchange_start(shards, place_own(shards, [s.shape[0] for s in shards], me, scatter=False, name="gather_own"),
                                              scatter=False, name="gather_start")

    def weight(j, after):
        return exchange_wait([w_srcs[j]], [w_lands[j]], w_sems[2 * j:2 * j + 2], after, scatter=False, name=f"gather_wait_{j}")[0]

    mods_all = mods_all.reshape(N_DEV, 2, 16, n_ada).transpose(1, 2, 0, 3).reshape(2, 16, 6 * D)
    mod = lambda i, row: [m_[None, :] for m_ in jnp.split(lax.dynamic_index_in_dim(mods_all[i], row, 0, False), 6)]
    sh_m, sc_m, gt_m, sh_f, sc_f, gt_f = zip(mod(0, me), mod(1, me))
    csh_m, csc_m = mod(0, N_DEV)[:2]

    row = lambda a, i: a[i][None, :]

    cos, sa, sb = _rope_tables(L)
    sink = attn_sink[0]
    bst = sgu_b[0].T
    sgu_wb, sgu_wtb = sgu_w[0].astype(BF16), sgu_w[0].swapaxes(1, 2).astype(BF16)
    wup, wdn = [None, None], [None, None]

    def ffn_fwd(i, xin):
        wup[i] = weight(2 + 4 * i, xin)
        h, hu = pre_mm(xin, row(g_ffn_pre, i), sh_f[i], sc_f[i], wup[i], tm=tm, tn=half_f, name=f"ffn_up_{i}")
        a, s1, s2 = conv_fwd(hu, conv_w[i], ffn_conv_b[i][None, :], rows=conv_rows, wblk=2 * LANES, name=f"ffn_conv_{i}")
        wdn[i] = weight(3 + 4 * i, a)
        res = mm_post([a], wdn[i], xin, row(g_ffn_post, i), gt_f[i], tm=tm, target=target if i == 1 else None, name=f"ffn_down_{i}")
        return (h, (hu, s1, s2), a, *res)

    first_mod, cos, sa, sb = lax.optimization_barrier((sh_m[0], cos, sa, sb))
    win_e = permute_heads(weight(0, first_mod))
    h0, u, q, kv = inproj_even(x, row(g_mix_pre, 0), sh_m[0], sc_m[0], win_e, cos, sa, sb, tm=tm, name="in_even")
    hc, kvc = pre_mm(ctx, row(g_mix_pre, 0), csh_m, csc_m, win_e, tm=C, tn=2 * LANES, w_row_off=8 * LANES, name="in_even_ctx")
    pa = [pool_fwd(u, w_pool[0], pool_scale, name="pool_fwd"), attn_fwd(q, kv, kvc, sink, qb=4, name="attn_fwd")]
    wout_e = permute_heads(weight(1, pa[1]))
    y0, x1 = mm_post(pa, wout_e, x, row(g_mix_post, 0), gt_m[0], tm=tm_out, name="out_even")
    h1, hu0, a0, f0, x2 = ffn_fwd(0, x1)
    win_o = weight(4, x2)
    h2, z1 = pre_mm(x2, row(g_mix_pre, 1), sh_m[1], sc_m[1], win_o, tm=tm, tn=D, name="in_odd")
    us = sgu_fwd(z1, ln_g, ln_b, sgu_wb, bst, rows=tm, name="sgu_fwd")
    wout_o = weight(5, us)
    y1, x3 = mm_post([us], wout_o, x2, row(g_mix_post, 1), gt_m[1], tm=tm_out, name="out_odd")
    h3, hu1, a1, f1, dx4, loss_part = ffn_fwd(1, x3)

    g_srcs, g_lands, g_sems = [], [], []

    def scatter(grads, nm):
        own = place_own(grads, [g.shape[0] // N_DEV for g in grads], me, scatter=True, name=nm.replace("start", "own"))
        sems, srcs, lands, tok = exchange_start(grads, own, scatter=True, name=nm)
        g_srcs.extend(srcs)
        g_lands.extend(lands)
        g_sems.extend(sems)
        return tok[0:1, 0:1]

    def ffn_bwd(i, dxo, xin, h, hu, a, f, g_post):
        dyf, da, dg_post, dgt = post_bwd_mm(dxo, f, g_post, gt_f[i], wdn[i], tm=tm, name=f"ffn_down_bwd_{i}")
        dhg, dhu, dcwg, dcwu, dcbg, dcbu = conv_bwd(da, hu[1], hu[2], hu[0], conv_w[i], rows=conv_rows, wblk=2 * LANES,
                                                    name=f"ffn_conv_bwd_{i}")
        dxin, dg_pre, dsh, dsc = mm_pre_bwd([dhg, dhu], wup[i], xin, dxo, row(g_ffn_pre, i), sc_f[i], tm=tm,
                                            name=f"ffn_up_bwd_{i}")
        g_dn = wgrad([a], dyf, tr=2 * LANES, name=f"wgrad_down_{i}")
        g_up = wgrad([dhg, dhu], h, tr=2 * LANES, name=f"wgrad_up_{i}")
        tok = scatter([g_dn, g_up], f"scatter_start_ffn_{i}")
        return dxin, tok, dict(g_ffn_post=dg_post, g_ffn_pre=dg_pre, gt_f=dgt, sh_f=dsh, sc_f=dsc,
                               ffn_conv_w=jnp.concatenate([dcwg, dcwu], axis=1), ffn_conv_b=jnp.concatenate([dcbg, dcbu], axis=1)[0])

    dx3, tok, sf1 = ffn_bwd(1, dx4, x3, h3, hu1, a1, f1, row(g_ffn_post, 1))
    dy1, dus, dg_mpost1, dgt_m1 = post_bwd_mm(dx3, y1, row(g_mix_post, 1) + tok, gt_m[1], wout_o, tm=tm_out, name="out_odd_bwd")
    dz1, dws, dbs, dlng, dlnb = sgu_bwd(z1, dus, ln_g, ln_b, sgu_wb, sgu_wtb, bst, rows=tm, name="sgu_bwd")
    dx2, dg_mpre1, dsh_m1, dsc_m1 = mm_pre_bwd([dz1], win_o, x2, dx3, row(g_mix_pre, 1), sc_m[1], tm=tm, name="in_odd_bwd")
    tok = scatter([wgrad([us], dy1, tr=4 * LANES, name="wgrad_out_odd"), wgrad([dz1], h2, tr=4 * LANES, name="wgrad_in_odd")],
                  "scatter_start_mix_1")

    dx1, tok, sf0 = ffn_bwd(0, dx2, x1, h1, hu0, a0, f0, row(g_ffn_post, 0) + tok)
    dy0, dpa, dg_mpost0, dgt_m0 = post_bwd_mm(dx1, y0, row(g_mix_post, 0) + tok, gt_m[0], wout_e, tm=tm_out, name="out_even_bwd")
    tok = scatter([permute_heads(wgrad(pa, dy0, tr=4 * LANES, name="wgrad_out_even"), inverse=True)], "scatter_start_out_0")
    du, dwp, dps = pool_bwd(u, dpa, w_pool[0], pool_scale + tok, name="pool_bwd")
    dq, dkv, dkvc, dsink = attn_bwd(q, kv, kvc, sink, dpa, cos, sa, sb, qb=2, name="attn_bwd")
    dz0 = jnp.concatenate([du, dq, dkv], axis=1)
    dzc = jnp.concatenate([jnp.zeros((C, 8 * LANES), BF16), dkvc], axis=1)
    tok = scatter([permute_heads(wgrad([dz0], h0, tr=2 * LANES, extra=(dzc, hc), name="wgrad_in_even"), inverse=True)],
                  "scatter_start_in_0")
    grad_x, dg_mpre0, dsh_m0, dsc_m0 = mm_pre_bwd([dz0], win_e, x, dx1, row(g_mix_pre, 0) + tok, sc_m[0], tm=tm,
                                                  name="in_even_bwd")
    _, dg_mpre0c, dcsh, dcsc = mm_pre_bwd([dkvc], win_e, ctx, None, row(g_mix_pre, 0), csc_m, tm=C,
                                          w_row_off=8 * LANES, name="in_even_ctx_bwd")

    out, ran = {}, {}

    def update(name, lands, transposed):
        w_, m_, v_ = (a.transpose(0, 2, 1) if transposed else a for a in (P[name], M[name], V[name]))
        r = w_.shape[1]
        tr = r // 4 if r % 64 == 0 and r > 256 else r
        res = adamw(w_, m_, v_, [l_.reshape(N_DEV, r, l_.shape[1]) for l_ in lands], tr=tr, name=f"adamw_{name}")
        ran[name] = res[0]
        for kind, val in zip(("grad", "delta", "new_m", "new_v"), res):
            out[(kind, name)] = val.transpose(0, 2, 1) if transposed else val

    zero = jnp.zeros((1, D), F32)
    dmod0 = jnp.concatenate([dsh_m0, dsc_m0, dgt_m0, sf0["sh_f"], sf0["sc_f"], sf0["gt_f"]], axis=1)
    dmodc = jnp.concatenate([dcsh, dcsc, zero, zero, zero, zero], axis=1)
    dmod1 = jnp.concatenate([dsh_m1, dsc_m1, dgt_m1, sf1["sh_f"], sf1["sc_f"], sf1["gt_f"]], axis=1)
    dmods = jnp.concatenate([dmod0, dmodc, dmod1], axis=0)
    dm = dmods.reshape(-1, LANES).astype(BF16)
    d_sems, d_srcs, d_lands, d_tok = exchange_start(
        [dm], place_own([dm], [dm.shape[0]], me, scatter=False, name="dmods_own"), scatter=False, name="dmods_start")
    slots = exchange_wait(g_srcs[:6], g_lands[:6], g_sems[:12], d_tok, scatter=True, name="scatter_wait_early")
    early = slots
    update("w_ffn_down", [slots[4], slots[0]], False)
    update("w_in_odd", [slots[3]], True)
    update("w_out_odd", [slots[2]], False)
    updated = lambda names: [ran[k] for k in names]
    dmods_all = exchange_wait(d_srcs, d_lands, d_sems, updated(("w_out_odd",)), scatter=False, name="dmods_wait")[0]
    dall = lax.dynamic_index_in_dim(dmods_all.astype(F32).reshape(N_DEV, 3, N_DEV, n_ada), me, 2, False)
    g_w_ada, dcc = ada_bwd_mm(silu_c, c_ctx[None, :], dall, w_ada, name="ada_bwd")

    rep = dict(
        c_ctx=dcc[0:1],
        b_ada=jnp.concatenate([dmod0 + dmodc, dmod1]),
        g_mix_pre=jnp.concatenate([dg_mpre0 + dg_mpre0c, dg_mpre1]),
        g_mix_post=jnp.concatenate([dg_mpost0, dg_mpost1]),
        g_ffn_pre=jnp.concatenate([sf0["g_ffn_pre"], sf1["g_ffn_pre"]]),
        g_ffn_post=jnp.concatenate([sf0["g_ffn_post"], sf1["g_ffn_post"]]),
        w_pool=_nat2d(dwp), pool_scale=dps, attn_sink=dsink[:, :N_Q_HEADS],
        sgu_w=_nat2d(dws), sgu_b=dbs[:, :sgu_b.shape[1]].T,
        ffn_conv_b=jnp.stack([sf0["ffn_conv_b"], sf1["ffn_conv_b"]]),
    )
    hi = loss_part.astype(BF16).astype(F32)
    mid = (loss_part - hi).astype(BF16).astype(F32)
    loss_piece = jnp.pad(jnp.concatenate([hi, mid, loss_part - hi - mid], axis=1), ((0, 7), (0, LANES - 3)))
    conv_g = jnp.stack([sf0["ffn_conv_w"], sf1["ffn_conv_w"]]).reshape(2 * 3, N_DEV, n_cw).swapaxes(0, 1)
    shard_full = dict(sgu_ln_g=dlng.reshape(N_DEV, LANES), sgu_ln_b=dlnb.reshape(N_DEV, LANES),
                      ffn_conv_w=jnp.concatenate([_pack_rows(conv_g[d]) for d in range(N_DEV)], axis=0))
    small_names = list(rep) + list(shard_full)
    pieces = [_pack_rows(rep[k]) for k in rep] + list(shard_full.values()) + [loss_piece]
    sizes = [p.shape[0] for p in pieces]
    offs = [sum(sizes[:i]) for i in range(len(sizes))]
    pieces.append(jnp.zeros((-sum(sizes) % 16, LANES), F32))
    gpack = jnp.concatenate(pieces, axis=0).astype(BF16)
    own = place_own([gpack], [gpack.shape[0]], me, scatter=False, name="smallgrad_own")
    s_sems, s_srcs, s_lands, small_tok = exchange_start([gpack], own, scatter=False, name="smallgrad_start")

    slots = exchange_wait(g_srcs[6:], g_lands[6:], g_sems[12:], small_tok, scatter=True, name="scatter_wait_late")
    update("w_in_even", [slots[1]], True)
    update("w_out_even", [slots[0]], False)
    update("w_ffn_up", [early[5], early[1]], True)
    res = adamw(w_ada, m_w_ada, v_w_ada, [g_w_ada[l][None] for l in range(w_ada.shape[0])], tr=D // 4, name="adamw_w_ada")
    ran["w_ada"] = res[0]
    for kind, val in zip(("grad", "delta", "new_m", "new_v"), res):
        out[(kind, "w_ada")] = val

    gpacks = exchange_wait(s_srcs, s_lands, s_sems, updated(("w_ada",)), scatter=False,
                           name="smallgrad_wait")[0]
    per_dev = {k: shard_full[k].shape[0] // N_DEV for k in shard_full}
    params = [(_nat2d(P[k]), _nat2d(M[k]), _nat2d(V[k]), offs[i], per_dev.get(k, 0)) for i, k in enumerate(small_names)]
    res = small_update(gpacks.reshape(N_DEV, -1, LANES), jnp.reshape(me, (1,)).astype(jnp.int32), params, offs[-1], name="adamw_small")
    for i, k in enumerate(small_names):
        for kind, val in zip(("grad", "delta", "new_m", "new_v"), res[4 * i:4 * i + 4]):
            out[(kind, k)] = val.reshape(P[k].shape)
    loss = res[-1][0, 0]

    names = list(P)
    final = [loss, grad_x[None]]
    for kind in ("grad", "delta", "new_m", "new_v"):
        for k in names:
            val = out[(kind, k)]
            final.append(val)
    return tuple(final)
```

```python
import functools
import math

import jax
import jax.numpy as jnp
from jax import lax
from jax.experimental import pallas as pl
from jax.experimental.pallas import tpu as pltpu

F32 = jnp.float32
BF16 = jnp.bfloat16
MESH = pl.DeviceIdType.MESH
N_DEV = 8
LANES = 128
VMEM_LIMIT = 48 * 1024 * 1024
EPS = 1e-6
NEG_INF = -1e30
GRID_W = 64
WINDOW = 128
BLK = 128
HEAD_DIM = 64
N_Q_HEADS = 8
N_KV_HEADS = 2
GQA = N_Q_HEADS // N_KV_HEADS
POOL_WINDOWS = (2, 4, 8, 16)
ROPE_BASE = 10000.0
ROPE_FREQS = HEAD_DIM // 4
PAD = 16
ADAM_LR, ADAM_B1, ADAM_B2, ADAM_EPS, ADAM_WD, ADAM_STEP = 0.001, 0.9, 0.999, 1e-08, 0.01, 10
BC1 = 1.0 - ADAM_B1 ** ADAM_STEP
BC2 = 1.0 - ADAM_B2 ** ADAM_STEP
SQRT_2_OVER_PI = math.sqrt(2.0 / math.pi)
GELU_C = 0.044715


def _cp(sem=None):
    return pltpu.CompilerParams(dimension_semantics=sem, vmem_limit_bytes=VMEM_LIMIT)


def _dot(a, b):
    return jnp.dot(a, b, preferred_element_type=F32)


def _dot_nt(a, b):
    return lax.dot_general(a, b, (((1,), (1,)), ((), ())), preferred_element_type=F32)


def _dot_tn(a, b):
    return lax.dot_general(a, b, (((0,), (0,)), ((), ())), preferred_element_type=F32)


def _rms(x):
    r = lax.rsqrt(jnp.mean(x * x, axis=-1, keepdims=True) + EPS)
    return x * r, r


def _rms_bwd(dn, n, r):
    return r * (dn - n * jnp.mean(dn * n, axis=-1, keepdims=True))


def _colsum(a):
    return jnp.sum(a, axis=0, keepdims=True)


def _rope(x, c, sa, sb):
    return x * c + pltpu.roll(x, LANES - ROPE_FREQS, 1) * sa + pltpu.roll(x, ROPE_FREQS, 1) * sb


def _full(shape):
    return pl.BlockSpec(shape, lambda *_: (0,) * len(shape))


def pre_mm(x, g, sh, sc, wt, *, tm, tn, w_row_off=0, name):
    T, D = x.shape
    n_rows = wt.shape[0] - w_row_off

    def body(x_ref, g_ref, sh_ref, sc_ref, w_ref, h_ref, z_ref):
        n, _ = _rms(x_ref[...])
        h = (n * g_ref[...] * (1.0 + sc_ref[...]) + sh_ref[...]).astype(BF16)
        h_ref[...] = h
        for c0 in range(0, n_rows, tn):
            z_ref[:, c0:c0 + tn] = _dot_nt(h, w_ref[c0:c0 + tn, :]).astype(BF16)

    vec = pl.BlockSpec((1, D), lambda i: (0, 0))
    return pl.pallas_call(
        body, name=name, grid=(T // tm,),
        in_specs=[pl.BlockSpec((tm, D), lambda i: (i, 0)), vec, vec, vec,
                  pl.BlockSpec((n_rows, D), lambda i: (w_row_off // n_rows, 0), pipeline_mode=pl.Buffered(1))],
        out_specs=[pl.BlockSpec((tm, D), lambda i: (i, 0)), pl.BlockSpec((tm, n_rows), lambda i: (i, 0))],
        out_shape=[jax.ShapeDtypeStruct((T, D), BF16), jax.ShapeDtypeStruct((T, n_rows), BF16)],
        compiler_params=_cp(("parallel",)),
    )(x, g, sh, sc, wt)


def inproj_even(x, g, sh, sc, wt, cos, sa, sb, *, tm, name):
    T, D = x.shape
    N = wt.shape[0]

    def body(x_ref, g_ref, sh_ref, sc_ref, w_ref, c_ref, sa_ref, sb_ref, h_ref, u_ref, q_ref, kv_ref):
        n, _ = _rms(x_ref[...])
        h = (n * g_ref[...] * (1.0 + sc_ref[...]) + sh_ref[...]).astype(BF16)
        h_ref[...] = h
        z = _dot_nt(h, w_ref[...])
        u_ref[...] = z[:, :4 * LANES]
        c, a, b = c_ref[...], sa_ref[...], sb_ref[...]
        for s in range(4):
            q_ref[:, s * LANES:(s + 1) * LANES] = _rope(z[:, (4 + s) * LANES:(5 + s) * LANES], c, a, b).astype(BF16)
        kv_ref[:, :LANES] = _rope(z[:, 8 * LANES:9 * LANES], c, a, b).astype(BF16)
        kv_ref[:, LANES:] = z[:, 9 * LANES:].astype(BF16)

    vec = pl.BlockSpec((1, D), lambda i: (0, 0))
    row = lambda w: pl.BlockSpec((tm, w), lambda i: (i, 0))
    return pl.pallas_call(
        body, name=name, grid=(T // tm,),
        in_specs=[row(D), vec, vec, vec, _full((N, D)), row(LANES), row(LANES), row(LANES)],
        out_specs=[row(D), row(4 * LANES), row(4 * LANES), row(2 * LANES)],
        out_shape=[jax.ShapeDtypeStruct((T, D), BF16), jax.ShapeDtypeStruct((T, 4 * LANES), F32),
                   jax.ShapeDtypeStruct((T, 4 * LANES), BF16), jax.ShapeDtypeStruct((T, 2 * LANES), BF16)],
        compiler_params=_cp(("parallel",)),
    )(x, g, sh, sc, wt, cos, sa, sb)


def mm_post(a_parts, w, x, g, gt, *, tm, target=None, name):
    T = a_parts[0].shape[0]
    D = w.shape[1]
    npart = len(a_parts)
    offs = [sum(a_.shape[1] for a_ in a_parts[:p]) for p in range(npart + 1)]
    with_loss = target is not None

    def body(*refs):
        a_refs, (w_ref, x_ref, g_ref, gt_ref) = refs[:npart], refs[npart:npart + 4]
        y = _dot(a_refs[0][...], w_ref[offs[0]:offs[1], :])
        for p in range(1, npart):
            y = y + _dot(a_refs[p][...], w_ref[offs[p]:offs[p + 1], :])
        n, _ = _rms(y)
        xn = x_ref[...] + gt_ref[...] * (n * g_ref[...])
        if not with_loss:
            y_ref, xn_ref = refs[npart + 4:]
            y_ref[...] = y.astype(BF16)
            xn_ref[...] = xn
            return
        t_ref, y_ref, d_ref, l_ref = refs[npart + 4:]
        y_ref[...] = y.astype(BF16)

        @pl.when(pl.program_id(0) == 0)
        def _():
            l_ref[...] = jnp.zeros_like(l_ref)

        e = xn - t_ref[...]
        l_ref[...] += 0.5 * jnp.sum(jnp.mean(e * e, axis=-1, keepdims=True), axis=0, keepdims=True)
        d_ref[...] = e * (1.0 / D)

    vec = pl.BlockSpec((1, D), lambda i: (0, 0))
    row = lambda w_: pl.BlockSpec((tm, w_), lambda i: (i, 0))
    in_specs = [row(a_.shape[1]) for a_ in a_parts] + [_full(w.shape), row(D), vec, vec]
    out_specs = [row(D), row(D)]
    out_shape = [jax.ShapeDtypeStruct((T, D), BF16), jax.ShapeDtypeStruct((T, D), F32)]
    if with_loss:
        in_specs.append(row(D))
        out_specs.append(_full((1, 1)))
        out_shape.append(jax.ShapeDtypeStruct((1, 1), F32))
    return pl.pallas_call(
        body, name=name, grid=(T // tm,), in_specs=in_specs, out_specs=out_specs, out_shape=out_shape,
        compiler_params=_cp(("arbitrary",) if with_loss else ("parallel",)),
    )(*a_parts, w, x, g, gt, *((target,) if with_loss else ()))


def post_bwd_mm(dxn, y, g, gt, w, *, tm, name):
    T, D = y.shape
    K = w.shape[0]

    def body(dxn_ref, y_ref, g_ref, gt_ref, w_ref, dy_ref, da_ref, dg_ref, dgt_ref):
        @pl.when(pl.program_id(0) == 0)
        def _():
            dg_ref[...] = jnp.zeros_like(dg_ref)
            dgt_ref[...] = jnp.zeros_like(dgt_ref)

        d = dxn_ref[...]
        n, r = _rms(y_ref[...].astype(F32))
        g_, gt_ = g_ref[...], gt_ref[...]
        dg_ref[...] += _colsum(d * gt_ * n)
        dgt_ref[...] += _colsum(d * g_ * n)
        dy = _rms_bwd(d * (gt_ * g_), n, r).astype(BF16)
        dy_ref[...] = dy
        da_ref[...] = _dot_nt(dy, w_ref[...]).astype(BF16)

    vec = pl.BlockSpec((1, D), lambda i: (0, 0))
    row = lambda w_: pl.BlockSpec((tm, w_), lambda i: (i, 0))
    return pl.pallas_call(
        body, name=name, grid=(T // tm,),
        in_specs=[row(D), row(D), vec, vec, _full((K, D))],
        out_specs=[row(D), row(K), vec, vec],
        out_shape=[jax.ShapeDtypeStruct((T, D), BF16), jax.ShapeDtypeStruct((T, K), BF16),
                   jax.ShapeDtypeStruct((1, D), F32), jax.ShapeDtypeStruct((1, D), F32)],
        compiler_params=_cp(("arbitrary",)),
    )(dxn, y, g, gt, w)


def mm_pre_bwd(dzs, wt, x, dres, g, sc, *, tm, w_row_off=0, name):
    T, N = dzs[0].shape
    D = x.shape[1]
    npart = len(dzs)
    off = w_row_off // N
    has_res = dres is not None

    def body(*refs):
        dz_refs = refs[:npart]
        w_refs = refs[npart:2 * npart]
        rest = refs[2 * npart:]
        x_ref = rest[0]
        dres_ref = rest[1] if has_res else None
        g_ref, sc_ref, dx_ref, dg_ref, dsh_ref, dsc_ref = rest[1 + has_res:]

        @pl.when(pl.program_id(0) == 0)
        def _():
            dg_ref[...] = jnp.zeros_like(dg_ref)
            dsh_ref[...] = jnp.zeros_like(dsh_ref)
            dsc_ref[...] = jnp.zeros_like(dsc_ref)

        dh = _dot(dz_refs[0][...], w_refs[0][...])
        for p in range(1, npart):
            dh = dh + _dot(dz_refs[p][...], w_refs[p][...])
        n, r = _rms(x_ref[...])
        g_, s1 = g_ref[...], 1.0 + sc_ref[...]
        dsh_ref[...] += _colsum(dh)
        dsc_ref[...] += _colsum(dh * n * g_)
        dg_ref[...] += _colsum(dh * s1 * n)
        dxp = _rms_bwd(dh * (g_ * s1), n, r)
        dx_ref[...] = dxp + dres_ref[...] if has_res else dxp

    vec = pl.BlockSpec((1, D), lambda i: (0, 0))
    row = pl.BlockSpec((tm, D), lambda i: (i, 0))
    w_specs = [pl.BlockSpec((N, D), (lambda i, p=p: (off + p, 0)), pipeline_mode=pl.Buffered(1)) for p in range(npart)]
    res_specs, res_args = ([row], (dres,)) if has_res else ([], ())
    return pl.pallas_call(
        body, name=name, grid=(T // tm,),
        in_specs=[pl.BlockSpec((tm, N), lambda i: (i, 0))] * npart + w_specs + [row] + res_specs + [vec, vec],
        out_specs=[row, vec, vec, vec],
        out_shape=[jax.ShapeDtypeStruct((T, D), F32)] + [jax.ShapeDtypeStruct((1, D), F32)] * 3,
        compiler_params=_cp(("arbitrary",)),
    )(*dzs, *([wt] * npart), x, *res_args, g, sc)


def wgrad(a_parts, b, *, tr, extra=None, name):
    T, R = a_parts[0].shape
    D = b.shape[1]
    npart = len(a_parts)
    nr = R // tr

    def body(*refs):
        a_refs, b_ref = refs[:npart], refs[npart]
        g_ref = refs[-1]
        for p in range(npart):
            @pl.when(pl.program_id(0) // nr == p)
            def _():
                acc = _dot_tn(a_refs[p][...], b_ref[...])
                if extra is not None:
                    acc += _dot_tn(refs[npart + 1][...], refs[npart + 2][...])
                g_ref[...] = acc.astype(BF16)

    in_specs = [pl.BlockSpec((T, tr), (lambda r, p=p: (0, jnp.clip(r - p * nr, 0, nr - 1)))) for p in range(npart)]
    in_specs.append(_full((T, D)))
    args = [*a_parts, b]
    if extra is not None:
        a2, b2 = extra
        in_specs += [pl.BlockSpec((a2.shape[0], tr), lambda r: (0, r)), _full(b2.shape)]
        args += [a2, b2]
    return pl.pallas_call(
        body, name=name, grid=(npart * nr,),
        in_specs=in_specs, out_specs=pl.BlockSpec((tr, D), lambda r: (r, 0)),
        out_shape=jax.ShapeDtypeStruct((npart * R, D), BF16),
        compiler_params=_cp(("parallel",)),
    )(*args)


def _conv_ext(ref, r0, rows, total):
    top = ref[pl.ds(pl.multiple_of(jnp.maximum(r0 - PAD, 0), PAD), PAD), :]
    mid = ref[pl.ds(r0, rows), :]
    bot = ref[pl.ds(pl.multiple_of(jnp.minimum(r0 + rows, total - PAD), PAD), PAD), :]
    top = jnp.where(r0 > 0, top, jnp.zeros_like(top))
    bot = jnp.where(r0 + rows < total, bot, jnp.zeros_like(bot))
    return jnp.concatenate([top, mid, bot], axis=0).astype(F32)


def _shift_rows(a, k):
    return pltpu.roll(a, k % a.shape[0], 0)


def _conv3(x, w, b):
    return w[0:1] * _shift_rows(x, 1) + w[1:2] * x + w[2:3] * _shift_rows(x, -1) + b


def _gate_up_specs(rows_, wblk, nb):
    return [pl.BlockSpec((rows_, wblk), lambda j: (0, j)), pl.BlockSpec((rows_, wblk), lambda j: (0, j + nb))]


def conv_fwd(hu, cw, cb, *, rows, wblk, name):
    L, N2 = hu.shape
    nb = N2 // 2 // wblk
    nchunk = L // rows

    def body(hg_ref, hu_ref, wg_ref, wu_ref, bg_ref, bu_ref, a_ref, s1_ref, s2_ref):
        def chunk(ci, carry):
            r0 = pl.multiple_of(ci * rows, rows)
            gate = _conv3(_conv_ext(hg_ref, r0, rows, L), wg_ref[...], bg_ref[...])[PAD:PAD + rows]
            up = _conv3(_conv_ext(hu_ref, r0, rows, L), wu_ref[...], bu_ref[...])[PAD:PAD + rows]
            sg = jax.nn.sigmoid(gate)
            silu = gate * sg
            at = pl.ds(r0, rows)
            a_ref[at, :] = (silu * up).astype(BF16)
            s1_ref[at, :] = silu.astype(BF16)
            s2_ref[at, :] = (up * (sg + silu * (1.0 - sg))).astype(BF16)
            return carry

        lax.fori_loop(0, nchunk, chunk, 0)

    out = pl.BlockSpec((L, wblk), lambda j: (0, j))
    return pl.pallas_call(
        body, name=name, grid=(nb,),
        in_specs=_gate_up_specs(L, wblk, nb) + _gate_up_specs(3, wblk, nb) + _gate_up_specs(1, wblk, nb),
        out_specs=[out] * 3, out_shape=[jax.ShapeDtypeStruct((L, N2 // 2), BF16)] * 3,
        compiler_params=_cp(("parallel",)),
    )(hu, hu, cw, cw, cb, cb)


def conv_bwd(da, s1, s2, hu, cw, *, rows, wblk, name):
    L, N2 = hu.shape
    F = N2 // 2
    nb = F // wblk
    nchunk = L // rows
    mid = slice(PAD, PAD + rows)

    def body(da_ref, s1_ref, s2_ref, hg_ref, hu_ref, wg_ref, wu_ref, dg_ref, du_ref, dwg_ref, dwu_ref, dbg_ref, dbu_ref):
        for ref in (dwg_ref, dwu_ref, dbg_ref, dbu_ref):
            ref[...] = jnp.zeros_like(ref)

        def half_bwd(x_ref, dh, w_ref, dx_ref, dw_ref, db_ref, r0):
            w = w_ref[...]
            nxt, prv = _shift_rows(dh, -1)[mid], _shift_rows(dh, 1)[mid]
            dhm, xm = dh[mid], x_ref[pl.ds(r0, rows), :].astype(F32)
            dx_ref[pl.ds(r0, rows), :] = (w[0:1] * nxt + w[1:2] * dhm + w[2:3] * prv).astype(BF16)
            db_ref[...] += _colsum(dhm)
            dw_ref[0:1, :] += _colsum(nxt * xm)
            dw_ref[1:2, :] += _colsum(dhm * xm)
            dw_ref[2:3, :] += _colsum(prv * xm)

        def chunk(ci, carry):
            r0 = pl.multiple_of(ci * rows, rows)
            d = _conv_ext(da_ref, r0, rows, L)
            half_bwd(hu_ref, d * _conv_ext(s1_ref, r0, rows, L), wu_ref, du_ref, dwu_ref, dbu_ref, r0)
            half_bwd(hg_ref, d * _conv_ext(s2_ref, r0, rows, L), wg_ref, dg_ref, dwg_ref, dbg_ref, r0)
            return carry

        lax.fori_loop(0, nchunk, chunk, 0)

    blk = lambda r: pl.BlockSpec((r, wblk), lambda j: (0, j))
    return pl.pallas_call(
        body, name=name, grid=(nb,),
        in_specs=[blk(L)] * 3 + _gate_up_specs(L, wblk, nb) + _gate_up_specs(3, wblk, nb),
        out_specs=[blk(L), blk(L), blk(3), blk(3), blk(1), blk(1)],
        out_shape=[jax.ShapeDtypeStruct((L, F), BF16)] * 2 + [jax.ShapeDtypeStruct((3, F), F32)] * 2
        + [jax.ShapeDtypeStruct((1, F), F32)] * 2,
        compiler_params=_cp(("parallel",)),
    )(da, s1, s2, hu, hu, cw, cw)


def _window_sums(pad_ref, w, lead):
    a = pad_ref[...]
    k = 1
    while k < w:
        a = a + _shift_rows(a, -k)
        k *= 2
    return _shift_rows(a, lead) if lead else a


def _pool_counts(L, h):
    t = lax.broadcasted_iota(jnp.int32, (L, 1), 0)
    return (jnp.minimum(t + h, L) - jnp.maximum(t - h, 0)).astype(F32)


def _pooled(u_ref, pad_ref, L, w):
    h = w // 2
    pad_ref[pl.ds(PAD, L), :] = u_ref[...]
    win = _window_sums(pad_ref, w, h)[PAD:PAD + L]
    return win / _pool_counts(L, h) - u_ref[...]


def _zero_pad_edges(pad_ref, L):
    z = jnp.zeros((PAD, LANES), F32)
    pad_ref[pl.ds(0, PAD), :] = z
    pad_ref[pl.ds(PAD + L, PAD), :] = z


def pool_fwd(u, w_pool, pool_scale, *, name):
    L = u.shape[0]

    def body(u_ref, w_ref, ps_ref, p_ref, pad_ref):
        _zero_pad_edges(pad_ref, L)
        for gi, win in enumerate(POOL_WINDOWS):
            @pl.when(pl.program_id(0) == gi)
            def _():
                pooled = _pooled(u_ref, pad_ref, L, win)
                p_ref[...] = (_dot(pooled.astype(BF16), w_ref[...].astype(BF16)) * ps_ref[...]).astype(BF16)

    return pl.pallas_call(
        body, name=name, grid=(len(POOL_WINDOWS),),
        in_specs=[pl.BlockSpec((L, LANES), lambda gi: (0, gi)), pl.BlockSpec((None, LANES, LANES), lambda gi: (gi, 0, 0)),
                  pl.BlockSpec((1, LANES), lambda gi: (0, gi))],
        out_specs=pl.BlockSpec((L, LANES), lambda gi: (0, gi)),
        out_shape=jax.ShapeDtypeStruct((L, 4 * LANES), BF16),
        scratch_shapes=[pltpu.VMEM((L + 2 * PAD, LANES), F32)],
        compiler_params=_cp(("parallel",)),
    )(u, w_pool, pool_scale)


def pool_bwd(u, dpa, w_pool, pool_scale, *, name):
    L = u.shape[0]

    def body(u_ref, dp_ref, w_ref, ps_ref, du_ref, dw_ref, dps_ref, pad_ref):
        _zero_pad_edges(pad_ref, L)
        for gi, win in enumerate(POOL_WINDOWS):
            @pl.when(pl.program_id(0) == gi)
            def _():
                h = win // 2
                wb = w_ref[...].astype(BF16)
                pooled = _pooled(u_ref, pad_ref, L, win).astype(BF16)
                dp = dp_ref[...].astype(F32)
                dps_ref[...] = _colsum(dp * _dot(pooled, wb))
                dy = (dp * ps_ref[...]).astype(BF16)
                dw_ref[...] = _dot_tn(pooled, dy)
                dpooled = _dot_nt(dy, wb)
                pad_ref[pl.ds(PAD, L), :] = dpooled / _pool_counts(L, h)
                du_ref[...] = (_window_sums(pad_ref, win, h - 1)[PAD:PAD + L] - dpooled).astype(BF16)

    return pl.pallas_call(
        body, name=name, grid=(len(POOL_WINDOWS),),
        in_specs=[pl.BlockSpec((L, LANES), lambda gi: (0, gi)), pl.BlockSpec((L, LANES), lambda gi: (0, gi)),
                  pl.BlockSpec((None, LANES, LANES), lambda gi: (gi, 0, 0)), pl.BlockSpec((1, LANES), lambda gi: (0, gi))],
        out_specs=[pl.BlockSpec((L, LANES), lambda gi: (0, gi)), pl.BlockSpec((None, LANES, LANES), lambda gi: (gi, 0, 0)),
                   pl.BlockSpec((1, LANES), lambda gi: (0, gi))],
        out_shape=[jax.ShapeDtypeStruct((L, 4 * LANES), BF16), jax.ShapeDtypeStruct((4, LANES, LANES), F32),
                   jax.ShapeDtypeStruct((1, 4 * LANES), F32)],
        scratch_shapes=[pltpu.VMEM((L + 2 * PAD, LANES), F32)],
        compiler_params=_cp(("parallel",)),
    )(u, dpa, w_pool, pool_scale)


def _attn_probs(qk, band_k, ctx_k, sink_ref, kh, mask4):
    s_loc = jnp.where(mask4, _dot_nt(qk, band_k), NEG_INF)
    s_ctx = _dot_nt(qk, ctx_k)
    sk = jnp.concatenate([jnp.full((BLK, 1), sink_ref[kh * GQA + hh], F32) for hh in range(GQA)], axis=0)
    m = jnp.maximum(jnp.maximum(jnp.max(s_loc, axis=-1, keepdims=True), jnp.max(s_ctx, axis=-1, keepdims=True)), sk)
    e_loc, e_ctx, e_s = jnp.exp(s_loc - m), jnp.exp(s_ctx - m), jnp.exp(sk - m)
    inv = 1.0 / (jnp.sum(e_loc, axis=-1, keepdims=True) + jnp.sum(e_ctx, axis=-1, keepdims=True) + e_s)
    return e_loc * inv, e_ctx * inv, e_s * inv


def _attn_block(n, L):
    start = pl.multiple_of(jnp.clip((n - 1) * BLK, 0, L - 3 * BLK), BLK)
    qpos = n * BLK + lax.broadcasted_iota(jnp.int32, (BLK, 3 * BLK), 0)
    kpos = start + lax.broadcasted_iota(jnp.int32, (BLK, 3 * BLK), 1)
    mask = jnp.abs(kpos - qpos) <= WINDOW
    return start, jnp.concatenate([mask] * GQA, axis=0)


def _stack_slabs(ref, rows=slice(None)):
    return jnp.concatenate([ref[rows, s * LANES:(s + 1) * LANES] for s in range(GQA)], axis=0)


def _kv_head_lanes(kh):
    return (lax.broadcasted_iota(jnp.int32, (1, LANES), 1) // HEAD_DIM) == kh


def permute_heads(w, inverse=False):
    lo, hi = 4 * LANES, 8 * LANES
    mid = w[lo:hi].reshape(*((GQA, N_KV_HEADS) if inverse else (N_KV_HEADS, GQA)), HEAD_DIM, w.shape[1])
    return jnp.concatenate([w[:lo], mid.swapaxes(0, 1).reshape(hi - lo, w.shape[1]), w[hi:]], axis=0)


def attn_fwd(q, kv, kvc, sink, *, qb, name):
    L = q.shape[0]
    C = kvc.shape[0]
    scale = HEAD_DIM ** -0.5

    def body(q_ref, kv_ref, kvc_ref, sink_ref, o_ref):
        kvc_ = kvc_ref[...]
        for b in range(qb):
            rows = slice(b * BLK, (b + 1) * BLK)
            start, mask4 = _attn_block(pl.program_id(0) * qb + b, L)
            band = kv_ref[pl.ds(start, 3 * BLK), :]
            qs = _stack_slabs(q_ref, rows) * scale
            o = jnp.zeros((GQA * BLK, LANES), F32)
            for kh in range(N_KV_HEADS):
                grp = _kv_head_lanes(kh)
                qk = jnp.where(grp, qs, jnp.zeros_like(qs))
                p_loc, p_ctx, _ = _attn_probs(qk, band[:, :LANES], kvc_[:, :LANES], sink_ref, kh, mask4)
                o = o + jnp.where(grp, _dot(p_loc.astype(BF16), band[:, LANES:]) + _dot(p_ctx.astype(BF16), kvc_[:, LANES:]), 0.0)
            for s in range(GQA):
                o_ref[rows, s * LANES:(s + 1) * LANES] = o[s * BLK:(s + 1) * BLK].astype(BF16)

    return pl.pallas_call(
        body, name=name, grid=(L // (qb * BLK),),
        in_specs=[pl.BlockSpec((qb * BLK, 4 * LANES), lambda n: (n, 0)), _full((L, 2 * LANES)), _full((C, 2 * LANES)),
                  pl.BlockSpec(memory_space=pltpu.SMEM)],
        out_specs=pl.BlockSpec((qb * BLK, 4 * LANES), lambda n: (n, 0)),
        out_shape=jax.ShapeDtypeStruct((L, 4 * LANES), BF16),
        compiler_params=_cp(("parallel",)),
    )(q, kv, kvc, sink)


def attn_bwd(q, kv, kvc, sink, dpa, cos, sa, sb, *, qb, name):
    L = q.shape[0]
    C = kvc.shape[0]
    nsteps = L // (qb * BLK)
    scale = HEAD_DIM ** -0.5

    def body(q_ref, kv_ref, kvc_ref, sink_ref, do_ref, c_ref, sa_ref, sb_ref, cq_ref, saq_ref, sbq_ref,
             dq_ref, dkv_ref, dkvc_ref, dsink_ref, dkv_acc, dkvc_acc):
        step = pl.program_id(0)

        @pl.when(step == 0)
        def _():
            dkv_acc[...] = jnp.zeros_like(dkv_acc)
            dkvc_acc[...] = jnp.zeros_like(dkvc_acc)
            dsink_ref[...] = jnp.zeros_like(dsink_ref)

        for b in range(qb):
            one_block(step * qb + b, slice(b * BLK, (b + 1) * BLK), q_ref, kv_ref, kvc_ref, sink_ref, do_ref, cq_ref, saq_ref, sbq_ref,
                      dq_ref, dsink_ref, dkv_acc, dkvc_acc)

        @pl.when(step == nsteps - 1)
        def _():
            dkv_ref[:, :LANES] = _rope(dkv_acc[:LANES, :].T, c_ref[...], -sa_ref[...], -sb_ref[...]).astype(BF16)
            dkv_ref[:, LANES:] = dkv_acc[LANES:, :].T.astype(BF16)
            dkvc_ref[...] = dkvc_acc[...].T.astype(BF16)

    def one_block(n, rows, q_ref, kv_ref, kvc_ref, sink_ref, do_ref, cq_ref, saq_ref, sbq_ref, dq_ref, dsink_ref, dkv_acc, dkvc_acc):
        start, mask4 = _attn_block(n, L)
        band = kv_ref[pl.ds(start, 3 * BLK), :]
        kvc_ = kvc_ref[...]
        band_k, band_v, ctx_k, ctx_v = band[:, :LANES], band[:, LANES:], kvc_[:, :LANES], kvc_[:, LANES:]
        qs = _stack_slabs(q_ref, rows) * scale
        dos = _stack_slabs(do_ref, rows)
        lane = lax.broadcasted_iota(jnp.int32, (1, LANES), 1)
        dsink = jnp.zeros((1, LANES), F32)
        dq = jnp.zeros((GQA * BLK, LANES), F32)
        dk = jnp.zeros((LANES, 3 * BLK), F32)
        dv = jnp.zeros((LANES, 3 * BLK), F32)
        dkc = jnp.zeros((LANES, C), F32)
        dvc = jnp.zeros((LANES, C), F32)
        for kh in range(N_KV_HEADS):
            grp = _kv_head_lanes(kh)
            qk = jnp.where(grp, qs, jnp.zeros_like(qs))
            dok = jnp.where(grp, dos, jnp.zeros_like(dos))
            p_loc, p_ctx, p_s = _attn_probs(qk, band_k, ctx_k, sink_ref, kh, mask4)
            dp_loc = _dot_nt(dok, band_v)
            dp_ctx = _dot_nt(dok, ctx_v)
            delta = jnp.sum(p_loc * dp_loc, axis=-1, keepdims=True) + jnp.sum(p_ctx * dp_ctx, axis=-1, keepdims=True)
            ds_loc = (p_loc * (dp_loc - delta)).astype(BF16)
            ds_ctx = (p_ctx * (dp_ctx - delta)).astype(BF16)
            dsk = p_s * delta
            for hh in range(GQA):
                dsink = dsink - jnp.where(lane == kh * GQA + hh, jnp.sum(dsk[hh * BLK:(hh + 1) * BLK], axis=0, keepdims=True), 0.0)
            dq = dq + jnp.where(grp, _dot(ds_loc, band_k) + _dot(ds_ctx, ctx_k), 0.0)
            dk = dk + _dot_tn(qk, ds_loc)
            dv = dv + _dot_tn(dok, p_loc.astype(BF16))
            dkc = dkc + _dot_tn(qk, ds_ctx)
            dvc = dvc + _dot_tn(dok, p_ctx.astype(BF16))
        dsink_ref[...] += dsink
        dkv_acc[:LANES, pl.ds(start, 3 * BLK)] += dk
        dkv_acc[LANES:, pl.ds(start, 3 * BLK)] += dv
        dkvc_acc[:LANES, :] += dkc
        dkvc_acc[LANES:, :] += dvc
        c, a, b_ = cq_ref[rows, :], -saq_ref[rows, :], -sbq_ref[rows, :]
        for s in range(GQA):
            dq_ref[rows, s * LANES:(s + 1) * LANES] = _rope(dq[s * BLK:(s + 1) * BLK] * scale, c, a, b_).astype(BF16)

    blk = lambda w: pl.BlockSpec((qb * BLK, w), lambda n: (n, 0))
    return pl.pallas_call(
        body, name=name, grid=(nsteps,),
        in_specs=[blk(4 * LANES), _full((L, 2 * LANES)), _full((C, 2 * LANES)), pl.BlockSpec(memory_space=pltpu.SMEM),
                  pl.BlockSpec((qb * BLK, 4 * LANES), lambda n: (n, 1)),
                  _full((L, LANES)), _full((L, LANES)), _full((L, LANES)), blk(LANES), blk(LANES), blk(LANES)],
        out_specs=[blk(4 * LANES), _full((L, 2 * LANES)), _full((C, 2 * LANES)), _full((1, LANES))],
        out_shape=[jax.ShapeDtypeStruct((L, 4 * LANES), BF16), jax.ShapeDtypeStruct((L, 2 * LANES), BF16),
                   jax.ShapeDtypeStruct((C, 2 * LANES), BF16), jax.ShapeDtypeStruct((1, LANES), F32)],
        scratch_shapes=[pltpu.VMEM((2 * LANES, L), F32), pltpu.VMEM((2 * LANES, C), F32)],
        compiler_params=_cp(("arbitrary",)),
    )(q, kv, kvc, sink, dpa, cos, sa, sb, cos, sa, sb)


def _gelu_parts(x):
    th = jnp.tanh(SQRT_2_OVER_PI * (x + GELU_C * x * x * x))
    return 0.5 * x * (1.0 + th), th


def _gelu_grad(x, th):
    return 0.5 * (1.0 + th) + 0.5 * x * (1.0 - th * th) * SQRT_2_OVER_PI * (1.0 + 3.0 * GELU_C * x * x)


def _layernorm(v):
    mu = jnp.mean(v, axis=-1, keepdims=True)
    vc = v - mu
    rstd = lax.rsqrt(jnp.mean(vc * vc, axis=-1, keepdims=True) + EPS)
    return vc * rstd, rstd


def sgu_fwd(z1, ln_g, ln_b, ws, bst, *, rows, name):
    L, W2 = z1.shape
    W = W2 // 2
    ng = W // LANES

    def body(z_ref, g_ref, b_ref, ws_ref, bs_ref, o_ref):
        for c in range(rows // BLK):
            at = slice(c * BLK, (c + 1) * BLK)
            z, _ = _gelu_parts(z_ref[at, :].astype(F32))
            xhat, _ = _layernorm(z[:, W:])
            vln = (xhat * g_ref[...] + b_ref[...]).astype(BF16)
            for gi in range(ng):
                cs = slice(gi * LANES, (gi + 1) * LANES)
                s = _dot(ws_ref[gi], vln[:, cs]) + bs_ref[:, gi:gi + 1]
                o_ref[at, cs] = (z[:, cs] * s).astype(BF16)

    vec = _full((1, W))
    return pl.pallas_call(
        body, name=name, grid=(L // rows,),
        in_specs=[pl.BlockSpec((rows, W2), lambda n: (n, 0)), vec, vec, _full((ng, LANES, LANES)), _full((BLK, ng))],
        out_specs=pl.BlockSpec((rows, W), lambda n: (n, 0)),
        out_shape=jax.ShapeDtypeStruct((L, W), BF16),
        compiler_params=_cp(("parallel",)),
    )(z1, ln_g, ln_b, ws, bst)


def sgu_bwd(z1, dus, ln_g, ln_b, ws, wst, bst, *, rows, name):
    L, W2 = z1.shape
    W = W2 // 2
    ng = W // LANES

    def body(z_ref, d_ref, g_ref, b_ref, ws_ref, wst_ref, bs_ref, dz_ref, dws_ref, dbs_ref, dg_ref, db_ref, dv_scr):
        @pl.when(pl.program_id(0) == 0)
        def _():
            dws_ref[...] = jnp.zeros_like(dws_ref)
            dbs_ref[...] = jnp.zeros_like(dbs_ref)
            dg_ref[...] = jnp.zeros_like(dg_ref)
            db_ref[...] = jnp.zeros_like(db_ref)

        for c in range(rows // BLK):
            at = slice(c * BLK, (c + 1) * BLK)
            zp = z_ref[at, :].astype(F32)
            z, th = _gelu_parts(zp)
            xhat, rstd = _layernorm(z[:, W:])
            vln = (xhat * g_ref[...] + b_ref[...]).astype(BF16)
            d = d_ref[at, :].astype(F32)
            lane = lax.broadcasted_iota(jnp.int32, (1, LANES), 1)
            dbs = jnp.zeros((BLK, LANES), F32)
            dgel = _gelu_grad(zp, th)
            for gi in range(ng):
                cs = slice(gi * LANES, (gi + 1) * LANES)
                s = _dot(ws_ref[gi], vln[:, cs]) + bs_ref[:, gi:gi + 1]
                dz_ref[at, cs] = (d[:, cs] * s * dgel[:, cs]).astype(BF16)
                ds = d[:, cs] * z[:, cs]
                dbs = dbs + jnp.where(lane == gi, jnp.sum(ds, axis=-1, keepdims=True), 0.0)
                dsb = ds.astype(BF16)
                dws_ref[gi] += _dot_nt(dsb, vln[:, cs])
                dv_scr[:, cs] = _dot(wst_ref[gi], dsb)
            dbs_ref[...] += dbs
            dvln = dv_scr[...]
            dg_ref[...] += _colsum(dvln * xhat)
            db_ref[...] += _colsum(dvln)
            dxh = dvln * g_ref[...]
            dv = rstd * (dxh - jnp.mean(dxh, axis=-1, keepdims=True) - xhat * jnp.mean(dxh * xhat, axis=-1, keepdims=True))
            dz_ref[at, W:] = (dv * dgel[:, W:]).astype(BF16)

    vec = _full((1, W))
    return pl.pallas_call(
        body, name=name, grid=(L // rows,),
        in_specs=[pl.BlockSpec((rows, W2), lambda n: (n, 0)), pl.BlockSpec((rows, W), lambda n: (n, 0)), vec, vec,
                  _full((ng, LANES, LANES)), _full((ng, LANES, LANES)), _full((BLK, ng))],
        out_specs=[pl.BlockSpec((rows, W2), lambda n: (n, 0)), _full((ng, LANES, LANES)), _full((BLK, LANES)), vec, vec],
        out_shape=[jax.ShapeDtypeStruct((L, W2), BF16), jax.ShapeDtypeStruct((ng, LANES, LANES), F32),
                   jax.ShapeDtypeStruct((BLK, LANES), F32), jax.ShapeDtypeStruct((1, W), F32), jax.ShapeDtypeStruct((1, W), F32)],
        scratch_shapes=[pltpu.VMEM((BLK, W), F32)],
        compiler_params=_cp(("arbitrary",)),
    )(z1, dus, ln_g, ln_b, ws, wst, bst)


def _adamw_math(w, m, v, g):
    m_ = ADAM_B1 * m + (1.0 - ADAM_B1) * g
    v_ = ADAM_B2 * v + (1.0 - ADAM_B2) * (g * g)
    return -ADAM_LR * ((m_ / BC1) / (jnp.sqrt(v_ / BC2) + ADAM_EPS) + ADAM_WD * w), m_, v_


def adamw(w, m, v, gparts, *, tr, name):
    NL, R, Wd = w.shape
    nr = R // tr

    def body(w_ref, m_ref, v_ref, *rest):
        gp_refs, (g_ref, d_ref, nm_ref, nv_ref) = rest[:NL], rest[NL:]
        for l in range(NL):
            @pl.when(pl.program_id(0) == l)
            def _():
                g = gp_refs[l][0].astype(F32)
                for s in range(1, gp_refs[l].shape[0]):
                    g = g + gp_refs[l][s].astype(F32)
                g_ref[...] = g
                d_ref[...], nm_ref[...], nv_ref[...] = _adamw_math(w_ref[...], m_ref[...], v_ref[...], g)

    row = pl.BlockSpec((None, tr, Wd), lambda l, i: (l, i, 0))
    gspecs = [pl.BlockSpec((gparts[l].shape[0], tr, Wd), (lambda l_, i, l=l: (0, jnp.clip(i + (l_ - l) * nr, 0, nr - 1), 0)))
              for l in range(NL)]
    return pl.pallas_call(
        body, name=name, grid=(NL, nr),
        in_specs=[row, row, row] + gspecs, out_specs=[row] * 4, out_shape=[jax.ShapeDtypeStruct((NL, R, Wd), F32)] * 4,
        compiler_params=_cp(("arbitrary", "arbitrary")),
    )(w, m, v, *gparts)


def small_update(gpacks, me, params, loss_row, *, name):
    n = len(params)

    def body(me_ref, gp_ref, *refs):
        ins, outs, gs_ref = refs[:3 * n], refs[3 * n:-1], refs[-1]
        gs_ref[...] = gp_ref[0].astype(F32)
        for dv in range(1, N_DEV):
            gs_ref[...] += gp_ref[dv].astype(F32)
        for p, (w, _, _, off, per_dev) in enumerate(params):
            w_ref, m_ref, v_ref = ins[3 * p:3 * p + 3]
            g_ref, d_ref, nm_ref, nv_ref = outs[4 * p:4 * p + 4]
            rows, cols = w.shape
            if cols == LANES and rows % 8 == 0 and not per_dev:
                g = gs_ref[off:off + rows, :]
                g_ref[...] = g
                d_ref[...], nm_ref[...], nv_ref[...] = _adamw_math(w_ref[...], m_ref[...], v_ref[...], g)
                continue
            chunks = -(-cols // LANES)
            base = off + me_ref[0] * per_dev if per_dev else off
            for i in range(rows):
                for j in range(chunks):
                    wd = min(LANES, cols - j * LANES)
                    at = (slice(i, i + 1), slice(j * LANES, j * LANES + wd))
                    g = gs_ref[pl.ds(base + i * chunks + j, 1), 0:wd]
                    g_ref[at] = g
                    d_ref[at], nm_ref[at], nv_ref[at] = _adamw_math(w_ref[at], m_ref[at], v_ref[at], g)
        outs[-1][...] = jnp.sum(gs_ref[loss_row:loss_row + 1, :], axis=1, keepdims=True)

    flat = [a for w, m, v, _, _ in params for a in (w, m, v)]
    out_shape = [jax.ShapeDtypeStruct(w.shape, F32) for w, _, _, _, _ in params for _ in range(4)] + [jax.ShapeDtypeStruct((1, 1), F32)]
    return pl.pallas_call(
        body, name=name, grid=(1,),
        in_specs=[pl.BlockSpec(memory_space=pltpu.SMEM), _full(gpacks.shape)] + [_full(a.shape) for a in flat],
        out_specs=[_full(o.shape) for o in out_shape], out_shape=out_shape,
        scratch_shapes=[pltpu.VMEM(gpacks.shape[1:], F32)],
        compiler_params=_cp(("arbitrary",)),
    )(me, gpacks, *flat)


def ada_fwd_mm(cs, w_ada, b_loc, *, name):
    R, D = cs.shape
    nl, _, n = w_ada.shape

    def body(c_ref, w_ref, b_ref, s_ref, m_ref):
        c = c_ref[...]
        s = c * jax.nn.sigmoid(c)
        s_ref[...] = s
        for i in range(nl):
            m_ref[i] = _dot(s.astype(BF16), w_ref[i].astype(BF16)) + b_ref[i:i + 1, :]

    return pl.pallas_call(
        body, name=name, in_specs=[_full((R, D)), _full((nl, D, n)), _full((nl, n))],
        out_specs=[_full((R, D)), _full((nl, R, n))], grid=(1,),
        out_shape=[jax.ShapeDtypeStruct((R, D), F32), jax.ShapeDtypeStruct((nl, R, n), F32)],
        compiler_params=_cp(("arbitrary",)),
    )(cs, w_ada, b_loc)


def ada_bwd_mm(s, c_ctx, dall, w_ada, *, name):
    R, D = s.shape
    nl, _, n = w_ada.shape

    def body(s_ref, cc_ref, d_ref, w_ref, gw_ref, dcc_ref):
        sb = s_ref[...].astype(BF16)
        row = lax.broadcasted_iota(jnp.int32, (R, 1), 0)
        dctx = d_ref[0, 1:2, :]
        for dv in range(1, N_DEV):
            dctx = dctx + d_ref[dv, 1:2, :]
        for i in range(nl):
            dm = jnp.zeros((R, n), F32)
            for dv in range(N_DEV):
                dm = dm + jnp.where(row == dv, d_ref[dv, 2 * i:2 * i + 1, :], 0.0)
            if i == 0:
                dm = dm + jnp.where(row == N_DEV, dctx, 0.0)
            gw_ref[i] = _dot_tn(sb, dm.astype(BF16))
        cc = cc_ref[...]
        sg = jax.nn.sigmoid(cc)
        ds = _dot_nt(jnp.broadcast_to(dctx, (8, n)).astype(BF16), w_ref[0].astype(BF16))
        dcc_ref[...] = ds * (sg * (1.0 + cc * (1.0 - sg)))

    return pl.pallas_call(
        body, name=name, grid=(1,),
        in_specs=[_full((R, D)), _full((1, D)), _full((N_DEV, 3, n)), _full((nl, D, n))],
        out_specs=[_full((nl, D, n)), _full((8, D))],
        out_shape=[jax.ShapeDtypeStruct((nl, D, n), F32), jax.ShapeDtypeStruct((8, D), F32)],
        compiler_params=_cp(("arbitrary",)),
    )(s, c_ctx, dall, w_ada)


def _place():
    x, y, c = lax.axis_index("x"), lax.axis_index("y"), lax.axis_index("c")
    return x, y, c


def _lin(p):
    return 4 * p[0] + 2 * p[1] + p[2]


def all_gather_small(xb, *, name):
    R, W = xb.shape

    def body(x_ref, out_ref, send_sems, recv_sems, local_sem):
        x, y, c = _place()
        me = _lin((x, y, c))
        mine = pltpu.make_async_copy(x_ref, out_ref.at[me], local_sem)
        mine.start()
        copies = []
        for k in range(1, N_DEV):
            peer = (x ^ (k >> 2), y ^ ((k >> 1) & 1), c ^ (k & 1))
            mk = lambda dst, k=k, peer=peer: pltpu.make_async_remote_copy(
                src_ref=x_ref, dst_ref=dst, send_sem=send_sems.at[k - 1], recv_sem=recv_sems.at[k - 1], device_id=peer, device_id_type=MESH)
            mk(out_ref.at[me]).start()
            copies.append(mk(out_ref.at[_lin(peer)]))
        for cp in copies:
            cp.wait_recv()
        for cp in copies:
            cp.wait_send()
        mine.wait()

    vm = pl.BlockSpec(memory_space=pltpu.VMEM)
    return pl.pallas_call(
        body, name=name, in_specs=[vm], out_specs=vm, out_shape=jax.ShapeDtypeStruct((N_DEV, R, W), xb.dtype),
        scratch_shapes=[pltpu.SemaphoreType.DMA((7,)), pltpu.SemaphoreType.DMA((7,)), pltpu.SemaphoreType.DMA],
        compiler_params=pltpu.CompilerParams(vmem_limit_bytes=VMEM_LIMIT),
    )(xb)


HBM_SPEC = pl.BlockSpec(memory_space=pltpu.HBM)
SEM_SPEC = pl.BlockSpec(memory_space=pltpu.SEMAPHORE)
ORDERED_EFFECT = pltpu.SideEffectType.DATAFLOW_SIDE_EFFECTING


def _exchange_copies(srcs, lands, sems, scatter):
    x, y, c = _place()
    me = _lin((x, y, c))
    for j in range(len(srcs)):
        r = lands[j].shape[0] // N_DEV
        block = lambda d, j=j, r=r: pl.ds(pl.multiple_of(d * r, 16), r)
        for k in range(1, N_DEV):
            peer = (x ^ (k >> 2), y ^ ((k >> 1) & 1), c ^ (k & 1))
            src = srcs[j].at[block(_lin(peer)), :] if scatter else srcs[j]
            mk = lambda dst, j=j, k=k, peer=peer, src=src: pltpu.make_async_remote_copy(
                src_ref=src, dst_ref=dst, send_sem=sems[2 * j].at[k - 1], recv_sem=sems[2 * j + 1].at[k - 1],
                device_id=peer, device_id_type=MESH)
            yield mk(lands[j].at[block(me), :]), mk(lands[j].at[block(_lin(peer)), :])


def exchange_start(srcs, lands, *, scatter, name):
    nw = len(srcs)

    def body(*refs):
        for start, _ in _exchange_copies(refs[:nw], refs[nw:2 * nw], refs[2 * nw:4 * nw], scatter):
            start.start()
        refs[-1][...] = jnp.zeros_like(refs[-1])

    thru = [pltpu.HBM(a.shape, a.dtype) for a in (*srcs, *lands)]
    res = pl.pallas_call(
        body, name=name, in_specs=[HBM_SPEC] * (2 * nw),
        out_specs=[SEM_SPEC] * (2 * nw) + [HBM_SPEC] * (2 * nw) + [pl.BlockSpec(memory_space=pltpu.VMEM)],
        out_shape=[pltpu.SemaphoreType.DMA((N_DEV - 1,))] * (2 * nw) + thru + [jax.ShapeDtypeStruct((8, LANES), F32)],
        input_output_aliases={i: 2 * nw + i for i in range(2 * nw)},
        compiler_params=pltpu.CompilerParams(has_side_effects=ORDERED_EFFECT),
    )(*[pltpu.with_memory_space_constraint(a, pltpu.HBM) for a in (*srcs, *lands)])
    return res[:2 * nw], res[2 * nw:3 * nw], res[3 * nw:4 * nw], res[-1]


def exchange_wait(srcs, lands, sems, after, *, scatter, name):
    nw = len(srcs)
    after = list(after) if isinstance(after, (list, tuple)) else [after]

    def body(*refs):
        for _, arrive in _exchange_copies(refs[:nw], refs[nw:2 * nw], refs[2 * nw:4 * nw], scatter):
            arrive.wait_send()
            arrive.wait_recv()

    res = pl.pallas_call(
        body, name=name, in_specs=[HBM_SPEC] * (2 * nw) + [SEM_SPEC] * (2 * nw) + [pl.BlockSpec(memory_space=pl.ANY)] * len(after),
        out_specs=[HBM_SPEC] * (2 * nw), out_shape=[pltpu.HBM(a.shape, a.dtype) for a in (*srcs, *lands)],
        input_output_aliases={i: i for i in range(2 * nw)},
        compiler_params=pltpu.CompilerParams(has_side_effects=ORDERED_EFFECT),
    )(*srcs, *lands, *sems, *after)
    return res[nw:]


def place_own(srcs, rows, me, *, scatter, name):
    nw = len(srcs)
    lands = [lax.empty((N_DEV * r, s_.shape[1]), s_.dtype) for r, s_ in zip(rows, srcs)]

    def body(me_ref, *refs):
        for j in range(nw):
            refs[2 * nw + j][...] = refs[j][...]

    mine = lambda i, me_ref: (me_ref[0], 0)
    src_at = mine if scatter else (lambda i, me_ref: (0, 0))
    blocks = [(r, s_.shape[1]) for r, s_ in zip(rows, srcs)]
    return pl.pallas_call(
        body, name=name,
        grid_spec=pltpu.PrefetchScalarGridSpec(
            num_scalar_prefetch=1, grid=(1,),
            in_specs=[pl.BlockSpec(b_, src_at) for b_ in blocks] + [pl.BlockSpec(memory_space=pl.ANY)] * nw,
            out_specs=[pl.BlockSpec(b_, mine) for b_ in blocks]),
        out_shape=[jax.ShapeDtypeStruct(l_.shape, l_.dtype) for l_ in lands],
        input_output_aliases={1 + nw + j: j for j in range(nw)},
        compiler_params=_cp(("arbitrary",)),
    )(jnp.reshape(me, (1,)).astype(jnp.int32), *srcs, *lands)


def _rope_tables(L):
    t = jnp.arange(L)
    inv = ROPE_BASE ** (-jnp.arange(ROPE_FREQS, dtype=F32) / ROPE_FREQS)
    ar = (t // GRID_W).astype(F32)[:, None] * inv
    ac = (t % GRID_W).astype(F32)[:, None] * inv
    z = jnp.zeros_like(ar)
    cos = jnp.concatenate([jnp.cos(ar), jnp.cos(ar), jnp.cos(ac), jnp.cos(ac)], axis=1)
    sa = jnp.concatenate([-jnp.sin(ar), z, -jnp.sin(ac), z], axis=1)
    sb = jnp.concatenate([z, jnp.sin(ar), z, jnp.sin(ac)], axis=1)
    return tuple(jnp.tile(a, (1, LANES // HEAD_DIM)) for a in (cos, sa, sb))


def _nat2d(a):
    return a.reshape(1, -1) if a.ndim == 1 else a.reshape(-1, a.shape[-1])


def _pack_rows(a):
    rows, cols = a.shape
    chunks = -(-cols // LANES)
    f = jnp.pad(a, ((0, 0), (0, chunks * LANES - cols))).reshape(rows * chunks, LANES)
    return jnp.pad(f, ((0, -f.shape[0] % 8), (0, 0)))


def _rows128(a):
    f = a.reshape(-1)
    n = -(-f.shape[0] // (8 * LANES)) * 8 * LANES
    return jnp.pad(f, (0, n - f.shape[0])).reshape(-1, LANES)


def kernel(x, c, ctx, c_ctx, w_ada, b_ada, g_mix_pre, g_mix_post, g_ffn_pre, g_ffn_post, w_in_even, w_pool, pool_scale, attn_sink, w_out_even, w_in_odd, sgu_ln_g, sgu_ln_b, sgu_w, sgu_b, w_out_odd, w_ffn_up, ffn_conv_w, ffn_conv_b, w_ffn_down, loss_target, m_c_ctx, m_w_ada, m_b_ada, m_g_mix_pre, m_g_mix_post, m_g_ffn_pre, m_g_ffn_post, m_w_in_even, m_w_pool, m_pool_scale, m_attn_sink, m_w_out_even, m_w_in_odd, m_sgu_ln_g, m_sgu_ln_b, m_sgu_w, m_sgu_b, m_w_out_odd, m_w_ffn_up, m_ffn_conv_w, m_ffn_conv_b, m_w_ffn_down, v_c_ctx, v_w_ada, v_b_ada, v_g_mix_pre, v_g_mix_post, v_g_ffn_pre, v_g_ffn_post, v_w_in_even, v_w_pool, v_pool_scale, v_attn_sink, v_w_out_even, v_w_in_odd, v_sgu_ln_g, v_sgu_ln_b, v_sgu_w, v_sgu_b, v_w_out_odd, v_w_ffn_up, v_ffn_conv_w, v_ffn_conv_b, v_w_ffn_down):
    P = dict(c_ctx=c_ctx, w_ada=w_ada, b_ada=b_ada, g_mix_pre=g_mix_pre, g_mix_post=g_mix_post, g_ffn_pre=g_ffn_pre,
             g_ffn_post=g_ffn_post, w_in_even=w_in_even, w_pool=w_pool, pool_scale=pool_scale, attn_sink=attn_sink,
             w_out_even=w_out_even, w_in_odd=w_in_odd, sgu_ln_g=sgu_ln_g, sgu_ln_b=sgu_ln_b, sgu_w=sgu_w, sgu_b=sgu_b,
             w_out_odd=w_out_odd, w_ffn_up=w_ffn_up, ffn_conv_w=ffn_conv_w, ffn_conv_b=ffn_conv_b, w_ffn_down=w_ffn_down)
    M = dict(c_ctx=m_c_ctx, w_ada=m_w_ada, b_ada=m_b_ada, g_mix_pre=m_g_mix_pre, g_mix_post=m_g_mix_post, g_ffn_pre=m_g_ffn_pre,
             g_ffn_post=m_g_ffn_post, w_in_even=m_w_in_even, w_pool=m_w_pool, pool_scale=m_pool_scale, attn_sink=m_attn_sink,
             w_out_even=m_w_out_even, w_in_odd=m_w_in_odd, sgu_ln_g=m_sgu_ln_g, sgu_ln_b=m_sgu_ln_b, sgu_w=m_sgu_w, sgu_b=m_sgu_b,
             w_out_odd=m_w_out_odd, w_ffn_up=m_w_ffn_up, ffn_conv_w=m_ffn_conv_w, ffn_conv_b=m_ffn_conv_b, w_ffn_down=m_w_ffn_down)
    V = dict(c_ctx=v_c_ctx, w_ada=v_w_ada, b_ada=v_b_ada, g_mix_pre=v_g_mix_pre, g_mix_post=v_g_mix_post, g_ffn_pre=v_g_ffn_pre,
             g_ffn_post=v_g_ffn_post, w_in_even=v_w_in_even, w_pool=v_w_pool, pool_scale=v_pool_scale, attn_sink=v_attn_sink,
             w_out_even=v_w_out_even, w_in_odd=v_w_in_odd, sgu_ln_g=v_sgu_ln_g, sgu_ln_b=v_sgu_ln_b, sgu_w=v_sgu_w, sgu_b=v_sgu_b,
             w_out_odd=v_w_out_odd, w_ffn_up=v_w_ffn_up, ffn_conv_w=v_ffn_conv_w, ffn_conv_b=v_ffn_conv_b, w_ffn_down=v_w_ffn_down)

    x = x[0]
    ctx = ctx[0]
    target = loss_target[0]
    L, D = x.shape
    C = ctx.shape[0]
    tm = min(512, L)
    tm_out = min(1024, L)
    conv_rows = min(1024, L)
    me = 4 * lax.axis_index("x") + 2 * lax.axis_index("y") + lax.axis_index("c")
    n_ada = w_ada.shape[2]
    F = w_ffn_down.shape[1] * N_DEV
    half_f = F // 2

    n_cw = ffn_conv_w.shape[2]
    small = jnp.concatenate([_rows128(c), _rows128(sgu_ln_g), _rows128(sgu_ln_b), _rows128(ffn_conv_w)], axis=0)
    small_all = all_gather_small(small, name="gather_small_inputs")
    c_all = small_all[:, :8].reshape(N_DEV, D)
    ln_g = small_all[:, 8].reshape(1, D)
    ln_b = small_all[:, 16].reshape(1, D)
    conv_w = small_all[:, 24:].reshape(N_DEV, -1)[:, :2 * 3 * n_cw].reshape(N_DEV, 2, 3, n_cw)
    conv_w = conv_w.transpose(1, 2, 0, 3).reshape(2, 3, 2 * F)

    cs = jnp.concatenate([c_all, c_ctx[None, :], jnp.zeros((7, D), F32)], axis=0)
    b_loc = lax.dynamic_slice(b_ada, (0, me * n_ada), (2, n_ada))
    silu_c, mods_loc = ada_fwd_mm(cs, w_ada, b_loc, name="ada_fwd")
    mods_all = all_gather_small(mods_loc.reshape(-1, LANES), name="gather_mods")

    shards = [s.astype(BF16) for s in (w_in_even[0].T, w_out_even[0], w_ffn_up[0].T, w_ffn_down[0],
                                       w_in_odd[0].T, w_out_odd[0], w_ffn_up[1].T, w_ffn_down[1])]
    shards, mods_all = lax.optimization_barrier((shards, mods_all))
    w_sems, w_srcs, w_lands, _ = exchange_start(shards, place_own(shards, [s.shape[0] for s in shards], me, scatter=False, name="gather_own"),
                                              scatter=False, name="gather_start")

    def weight(j, after):
        return exchange_wait([w_srcs[j]], [w_lands[j]], w_sems[2 * j:2 * j + 2], after, scatter=False, name=f"gather_wait_{j}")[0]

    mods_all = mods_all.reshape(N_DEV, 2, 16, n_ada).transpose(1, 2, 0, 3).reshape(2, 16, 6 * D)
    mod = lambda i, row: [m_[None, :] for m_ in jnp.split(lax.dynamic_index_in_dim(mods_all[i], row, 0, False), 6)]
    sh_m, sc_m, gt_m, sh_f, sc_f, gt_f = zip(mod(0, me), mod(1, me))
    csh_m, csc_m = mod(0, N_DEV)[:2]

    row = lambda a, i: a[i][None, :]

    cos, sa, sb = _rope_tables(L)
    sink = attn_sink[0]
    bst = sgu_b[0].T
    sgu_wb, sgu_wtb = sgu_w[0].astype(BF16), sgu_w[0].swapaxes(1, 2).astype(BF16)
    wup, wdn = [None, None], [None, None]

    def ffn_fwd(i, xin):
        wup[i] = weight(2 + 4 * i, xin)
        h, hu = pre_mm(xin, row(g_ffn_pre, i), sh_f[i], sc_f[i], wup[i], tm=tm, tn=half_f, name=f"ffn_up_{i}")
        a, s1, s2 = conv_fwd(hu, conv_w[i], ffn_conv_b[i][None, :], rows=conv_rows, wblk=2 * LANES, name=f"ffn_conv_{i}")
        wdn[i] = weight(3 + 4 * i, a)
        res = mm_post([a], wdn[i], xin, row(g_ffn_post, i), gt_f[i], tm=tm, target=target if i == 1 else None, name=f"ffn_down_{i}")
        return (h, (hu, s1, s2), a, *res)

    first_mod, cos, sa, sb = lax.optimization_barrier((sh_m[0], cos, sa, sb))
    win_e = permute_heads(weight(0, first_mod))
    h0, u, q, kv = inproj_even(x, row(g_mix_pre, 0), sh_m[0], sc_m[0], win_e, cos, sa, sb, tm=tm_out, name="in_even")
    hc, kvc = pre_mm(ctx, row(g_mix_pre, 0), csh_m, csc_m, win_e, tm=C, tn=2 * LANES, w_row_off=8 * LANES, name="in_even_ctx")
    pa = [pool_fwd(u, w_pool[0], pool_scale, name="pool_fwd"), attn_fwd(q, kv, kvc, sink, qb=2, name="attn_fwd")]
    wout_e = permute_heads(weight(1, pa[1]))
    y0, x1 = mm_post(pa, wout_e, x, row(g_mix_post, 0), gt_m[0], tm=tm_out, name="out_even")
    h1, hu0, a0, f0, x2 = ffn_fwd(0, x1)
    win_o = weight(4, x2)
    h2, z1 = pre_mm(x2, row(g_mix_pre, 1), sh_m[1], sc_m[1], win_o, tm=tm, tn=D, name="in_odd")
    us = sgu_fwd(z1, ln_g, ln_b, sgu_wb, bst, rows=tm, name="sgu_fwd")
    wout_o = weight(5, us)
    y1, x3 = mm_post([us], wout_o, x2, row(g_mix_post, 1), gt_m[1], tm=tm_out, name="out_odd")
    h3, hu1, a1, f1, dx4, loss_part = ffn_fwd(1, x3)

    g_srcs, g_lands, g_sems = [], [], []

    def scatter(grads, nm):
        own = place_own(grads, [g.shape[0] // N_DEV for g in grads], me, scatter=True, name=nm.replace("start", "own"))
        sems, srcs, lands, tok = exchange_start(grads, own, scatter=True, name=nm)
        g_srcs.extend(srcs)
        g_lands.extend(lands)
        g_sems.extend(sems)
        return tok[0:1, 0:1]

    def ffn_bwd(i, dxo, xin, h, hu, a, f, g_post):
        dyf, da, dg_post, dgt = post_bwd_mm(dxo, f, g_post, gt_f[i], wdn[i], tm=tm, name=f"ffn_down_bwd_{i}")
        dhg, dhu, dcwg, dcwu, dcbg, dcbu = conv_bwd(da, hu[1], hu[2], hu[0], conv_w[i], rows=conv_rows, wblk=2 * LANES,
                                                    name=f"ffn_conv_bwd_{i}")
        dxin, dg_pre, dsh, dsc = mm_pre_bwd([dhg, dhu], wup[i], xin, dxo, row(g_ffn_pre, i), sc_f[i], tm=tm,
                                            name=f"ffn_up_bwd_{i}")
        g_dn = wgrad([a], dyf, tr=2 * LANES, name=f"wgrad_down_{i}")
        g_up = wgrad([dhg, dhu], h, tr=2 * LANES, name=f"wgrad_up_{i}")
        tok = scatter([g_dn, g_up], f"scatter_start_ffn_{i}")
        return dxin, tok, dict(g_ffn_post=dg_post, g_ffn_pre=dg_pre, gt_f=dgt, sh_f=dsh, sc_f=dsc,
                               ffn_conv_w=jnp.concatenate([dcwg, dcwu], axis=1), ffn_conv_b=jnp.concatenate([dcbg, dcbu], axis=1)[0])

    dx3, tok, sf1 = ffn_bwd(1, dx4, x3, h3, hu1, a1, f1, row(g_ffn_post, 1))
    dy1, dus, dg_mpost1, dgt_m1 = post_bwd_mm(dx3, y1, row(g_mix_post, 1) + tok, gt_m[1], wout_o, tm=tm_out, name="out_odd_bwd")
    dz1, dws, dbs, dlng, dlnb = sgu_bwd(z1, dus, ln_g, ln_b, sgu_wb, sgu_wtb, bst, rows=tm, name="sgu_bwd")
    dx2, dg_mpre1, dsh_m1, dsc_m1 = mm_pre_bwd([dz1], win_o, x2, dx3, row(g_mix_pre, 1), sc_m[1], tm=tm, name="in_odd_bwd")
    tok = scatter([wgrad([us], dy1, tr=4 * LANES, name="wgrad_out_odd"), wgrad([dz1], h2, tr=4 * LANES, name="wgrad_in_odd")],
                  "scatter_start_mix_1")

    dx1, tok, sf0 = ffn_bwd(0, dx2, x1, h1, hu0, a0, f0, row(g_ffn_post, 0) + tok)
    dy0, dpa, dg_mpost0, dgt_m0 = post_bwd_mm(dx1, y0, row(g_mix_post, 0) + tok, gt_m[0], wout_e, tm=tm_out, name="out_even_bwd")
    tok = scatter([permute_heads(wgrad(pa, dy0, tr=4 * LANES, name="wgrad_out_even"), inverse=True)], "scatter_start_out_0")
    du, dwp, dps = pool_bwd(u, dpa, w_pool[0], pool_scale + tok, name="pool_bwd")
    dq, dkv, dkvc, dsink = attn_bwd(q, kv, kvc, sink, dpa, cos, sa, sb, qb=4, name="attn_bwd")
    dz0 = jnp.concatenate([du, dq, dkv], axis=1)
    dzc = jnp.concatenate([jnp.zeros((C, 8 * LANES), BF16), dkvc], axis=1)
    tok = scatter([permute_heads(wgrad([dz0], h0, tr=2 * LANES, extra=(dzc, hc), name="wgrad_in_even"), inverse=True)],
                  "scatter_start_in_0")
    grad_x, dg_mpre0, dsh_m0, dsc_m0 = mm_pre_bwd([dz0], win_e, x, dx1, row(g_mix_pre, 0) + tok, sc_m[0], tm=tm,
                                                  name="in_even_bwd")
    _, dg_mpre0c, dcsh, dcsc = mm_pre_bwd([dkvc], win_e, ctx, None, row(g_mix_pre, 0), csc_m, tm=C,
                                          w_row_off=8 * LANES, name="in_even_ctx_bwd")

    out, ran = {}, {}

    def update(name, lands, transposed):
        w_, m_, v_ = (a.transpose(0, 2, 1) if transposed else a for a in (P[name], M[name], V[name]))
        r = w_.shape[1]
        tr = r // 4 if r % 64 == 0 and r > 256 else r
        res = adamw(w_, m_, v_, [l_.reshape(N_DEV, r, l_.shape[1]) for l_ in lands], tr=tr, name=f"adamw_{name}")
        ran[name] = res[0]
        for kind, val in zip(("grad", "delta", "new_m", "new_v"), res):
            out[(kind, name)] = val.transpose(0, 2, 1) if transposed else val

    zero = jnp.zeros((1, D), F32)
    dmod0 = jnp.concatenate([dsh_m0, dsc_m0, dgt_m0, sf0["sh_f"], sf0["sc_f"], sf0["gt_f"]], axis=1)
    dmodc = jnp.concatenate([dcsh, dcsc, zero, zero, zero, zero], axis=1)
    dmod1 = jnp.concatenate([dsh_m1, dsc_m1, dgt_m1, sf1["sh_f"], sf1["sc_f"], sf1["gt_f"]], axis=1)
    dmods = jnp.concatenate([dmod0, dmodc, dmod1], axis=0)
    dm = dmods.reshape(-1, LANES).astype(BF16)
    d_sems, d_srcs, d_lands, d_tok = exchange_start(
        [dm], place_own([dm], [dm.shape[0]], me, scatter=False, name="dmods_own"), scatter=False, name="dmods_start")
    slots = exchange_wait(g_srcs[:6], g_lands[:6], g_sems[:12], d_tok, scatter=True, name="scatter_wait_early")
    early = slots
    update("w_ffn_down", [slots[4], slots[0]], False)
    update("w_in_odd", [slots[3]], True)
    update("w_out_odd", [slots[2]], False)
    updated = lambda names: [ran[k] for k in names]
    dmods_all = exchange_wait(d_srcs, d_lands, d_sems, updated(("w_out_odd",)), scatter=False, name="dmods_wait")[0]
    dall = lax.dynamic_index_in_dim(dmods_all.astype(F32).reshape(N_DEV, 3, N_DEV, n_ada), me, 2, False)
    g_w_ada, dcc = ada_bwd_mm(silu_c, c_ctx[None, :], dall, w_ada, name="ada_bwd")

    rep = dict(
        c_ctx=dcc[0:1],
        b_ada=jnp.concatenate([dmod0 + dmodc, dmod1]),
        g_mix_pre=jnp.concatenate([dg_mpre0 + dg_mpre0c, dg_mpre1]),
        g_mix_post=jnp.concatenate([dg_mpost0, dg_mpost1]),
        g_ffn_pre=jnp.concatenate([sf0["g_ffn_pre"], sf1["g_ffn_pre"]]),
        g_ffn_post=jnp.concatenate([sf0["g_ffn_post"], sf1["g_ffn_post"]]),
        w_pool=_nat2d(dwp), pool_scale=dps, attn_sink=dsink[:, :N_Q_HEADS],
        sgu_w=_nat2d(dws), sgu_b=dbs[:, :sgu_b.shape[1]].T,
        ffn_conv_b=jnp.stack([sf0["ffn_conv_b"], sf1["ffn_conv_b"]]),
    )
    hi = loss_part.astype(BF16).astype(F32)
    mid = (loss_part - hi).astype(BF16).astype(F32)
    loss_piece = jnp.pad(jnp.concatenate([hi, mid, loss_part - hi - mid], axis=1), ((0, 7), (0, LANES - 3)))
    conv_g = jnp.stack([sf0["ffn_conv_w"], sf1["ffn_conv_w"]]).reshape(2 * 3, N_DEV, n_cw).swapaxes(0, 1)
    shard_full = dict(sgu_ln_g=dlng.reshape(N_DEV, LANES), sgu_ln_b=dlnb.reshape(N_DEV, LANES),
                      ffn_conv_w=jnp.concatenate([_pack_rows(conv_g[d]) for d in range(N_DEV)], axis=0))
    small_names = list(rep) + list(shard_full)
    pieces = [_pack_rows(rep[k]) for k in rep] + list(shard_full.values()) + [loss_piece]
    sizes = [p.shape[0] for p in pieces]
    offs = [sum(sizes[:i]) for i in range(len(sizes))]
    pieces.append(jnp.zeros((-sum(sizes) % 16, LANES), F32))
    gpack = jnp.concatenate(pieces, axis=0).astype(BF16)
    own = place_own([gpack], [gpack.shape[0]], me, scatter=False, name="smallgrad_own")
    s_sems, s_srcs, s_lands, small_tok = exchange_start([gpack], own, scatter=False, name="smallgrad_start")

    slots = exchange_wait(g_srcs[6:], g_lands[6:], g_sems[12:], small_tok, scatter=True, name="scatter_wait_late")
    update("w_in_even", [slots[1]], True)
    update("w_out_even", [slots[0]], False)
    update("w_ffn_up", [early[5], early[1]], True)
    res = adamw(w_ada, m_w_ada, v_w_ada, [g_w_ada[l][None] for l in range(w_ada.shape[0])], tr=D // 4, name="adamw_w_ada")
    ran["w_ada"] = res[0]
    for kind, val in zip(("grad", "delta", "new_m", "new_v"), res):
        out[(kind, "w_ada")] = val

    gpacks = exchange_wait(s_srcs, s_lands, s_sems, updated(("w_ada",)), scatter=False,
                           name="smallgrad_wait")[0]
    per_dev = {k: shard_full[k].shape[0] // N_DEV for k in shard_full}
    params = [(_nat2d(P[k]), _nat2d(M[k]), _nat2d(V[k]), offs[i], per_dev.get(k, 0)) for i, k in enumerate(small_names)]
    res = small_update(gpacks.reshape(N_DEV, -1, LANES), jnp.reshape(me, (1,)).astype(jnp.int32), params, offs[-1], name="adamw_small")
    for i, k in enumerate(small_names):
        for kind, val in zip(("grad", "delta", "new_m", "new_v"), res[4 * i:4 * i + 4]):
            out[(kind, k)] = val.reshape(P[k].shape)
    loss = res[-1][0, 0]

    names = list(P)
    final = [loss, grad_x[None]]
    for kind in ("grad", "delta", "new_m", "new_v"):
        for k in names:
            val = out[(kind, k)]
            final.append(val)
    return tuple(final)
```

```python
import functools
import math

import jax
import jax.numpy as jnp
from jax import lax
from jax.experimental import pallas as pl
from jax.experimental.pallas import tpu as pltpu

F32 = jnp.float32
BF16 = jnp.bfloat16
MESH = pl.DeviceIdType.MESH
N_DEV = 8
LANES = 128
VMEM_LIMIT = 48 * 1024 * 1024
EPS = 1e-6
NEG_INF = -1e30
GRID_W = 64
WINDOW = 128
BLK = 128
HEAD_DIM = 64
N_Q_HEADS = 8
N_KV_HEADS = 2
GQA = N_Q_HEADS // N_KV_HEADS
POOL_WINDOWS = (2, 4, 8, 16)
ROPE_BASE = 10000.0
ROPE_FREQS = HEAD_DIM // 4
PAD = 16
ADAM_LR, ADAM_B1, ADAM_B2, ADAM_EPS, ADAM_WD, ADAM_STEP = 0.001, 0.9, 0.999, 1e-08, 0.01, 10
BC1 = 1.0 - ADAM_B1 ** ADAM_STEP
BC2 = 1.0 - ADAM_B2 ** ADAM_STEP
SQRT_2_OVER_PI = math.sqrt(2.0 / math.pi)
GELU_C = 0.044715


def _cp(sem=None):
    return pltpu.CompilerParams(dimension_semantics=sem, vmem_limit_bytes=VMEM_LIMIT)


def _dot(a, b):
    return jnp.dot(a, b, preferred_element_type=F32)


def _dot_nt(a, b):
    return lax.dot_general(a, b, (((1,), (1,)), ((), ())), preferred_element_type=F32)


def _dot_tn(a, b):
    return lax.dot_general(a, b, (((0,), (0,)), ((), ())), preferred_element_type=F32)


def _rms(x):
    r = lax.rsqrt(jnp.mean(x * x, axis=-1, keepdims=True) + EPS)
    return x * r, r


def _rms_bwd(dn, n, r):
    return r * (dn - n * jnp.mean(dn * n, axis=-1, keepdims=True))


def _colsum(a):
    return jnp.sum(a, axis=0, keepdims=True)


def _rope(x, c, sa, sb):
    return x * c + pltpu.roll(x, LANES - ROPE_FREQS, 1) * sa + pltpu.roll(x, ROPE_FREQS, 1) * sb


def _full(shape):
    return pl.BlockSpec(shape, lambda *_: (0,) * len(shape))


def pre_mm(x, g, sh, sc, wt, *, tm, tn, w_row_off=0, name):
    T, D = x.shape
    n_rows = wt.shape[0] - w_row_off

    def body(x_ref, g_ref, sh_ref, sc_ref, w_ref, h_ref, z_ref):
        n, _ = _rms(x_ref[...])
        h = (n * g_ref[...] * (1.0 + sc_ref[...]) + sh_ref[...]).astype(BF16)
        h_ref[...] = h
        for c0 in range(0, n_rows, tn):
            z_ref[:, c0:c0 + tn] = _dot_nt(h, w_ref[c0:c0 + tn, :]).astype(BF16)

    vec = pl.BlockSpec((1, D), lambda i: (0, 0))
    return pl.pallas_call(
        body, name=name, grid=(T // tm,),
        in_specs=[pl.BlockSpec((tm, D), lambda i: (i, 0)), vec, vec, vec,
                  pl.BlockSpec((n_rows, D), lambda i: (w_row_off // n_rows, 0), pipeline_mode=pl.Buffered(1))],
        out_specs=[pl.BlockSpec((tm, D), lambda i: (i, 0)), pl.BlockSpec((tm, n_rows), lambda i: (i, 0))],
        out_shape=[jax.ShapeDtypeStruct((T, D), BF16), jax.ShapeDtypeStruct((T, n_rows), BF16)],
        compiler_params=_cp(("parallel",)),
    )(x, g, sh, sc, wt)


def prenorm(x, g, sh, sc, *, tm, name):
    T, D = x.shape

    def body(x_ref, g_ref, sh_ref, sc_ref, h_ref):
        n, _ = _rms(x_ref[...])
        h_ref[...] = (n * g_ref[...] * (1.0 + sc_ref[...]) + sh_ref[...]).astype(BF16)

    vec = pl.BlockSpec((1, D), lambda i: (0, 0))
    row = pl.BlockSpec((tm, D), lambda i: (i, 0))
    return pl.pallas_call(
        body, name=name, grid=(T // tm,), in_specs=[row, vec, vec, vec], out_specs=row,
        out_shape=jax.ShapeDtypeStruct((T, D), BF16), compiler_params=_cp(("parallel",)),
    )(x, g, sh, sc)


def inproj_even(h, wt, cos, sa, sb, *, tm, name):
    T, D = h.shape
    N = wt.shape[0]

    def body(h_ref, w_ref, c_ref, sa_ref, sb_ref, u_ref, q_ref, kv_ref):
        z = _dot_nt(h_ref[...], w_ref[...])
        u_ref[...] = z[:, :4 * LANES]
        c, a, b = c_ref[...], sa_ref[...], sb_ref[...]
        for s in range(4):
            q_ref[:, s * LANES:(s + 1) * LANES] = _rope(z[:, (4 + s) * LANES:(5 + s) * LANES], c, a, b).astype(BF16)
        kv_ref[:, :LANES] = _rope(z[:, 8 * LANES:9 * LANES], c, a, b).astype(BF16)
        kv_ref[:, LANES:] = z[:, 9 * LANES:].astype(BF16)

    row = lambda w: pl.BlockSpec((tm, w), lambda i: (i, 0))
    return pl.pallas_call(
        body, name=name, grid=(T // tm,),
        in_specs=[row(D), _full((N, D)), row(LANES), row(LANES), row(LANES)],
        out_specs=[row(4 * LANES), row(4 * LANES), row(2 * LANES)],
        out_shape=[jax.ShapeDtypeStruct((T, 4 * LANES), F32),
                   jax.ShapeDtypeStruct((T, 4 * LANES), BF16), jax.ShapeDtypeStruct((T, 2 * LANES), BF16)],
        compiler_params=_cp(("parallel",)),
    )(h, wt, cos, sa, sb)


def mm_post(a_parts, w, x, g, gt, *, tm, target=None, name):
    T = a_parts[0].shape[0]
    D = w.shape[1]
    npart = len(a_parts)
    offs = [sum(a_.shape[1] for a_ in a_parts[:p]) for p in range(npart + 1)]
    with_loss = target is not None

    def body(*refs):
        a_refs, (w_ref, x_ref, g_ref, gt_ref) = refs[:npart], refs[npart:npart + 4]
        y = _dot(a_refs[0][...], w_ref[offs[0]:offs[1], :])
        for p in range(1, npart):
            y = y + _dot(a_refs[p][...], w_ref[offs[p]:offs[p + 1], :])
        n, _ = _rms(y)
        xn = x_ref[...] + gt_ref[...] * (n * g_ref[...])
        if not with_loss:
            y_ref, xn_ref = refs[npart + 4:]
            y_ref[...] = y.astype(BF16)
            xn_ref[...] = xn
            return
        t_ref, y_ref, d_ref, l_ref = refs[npart + 4:]
        y_ref[...] = y.astype(BF16)

        @pl.when(pl.program_id(0) == 0)
        def _():
            l_ref[...] = jnp.zeros_like(l_ref)

        e = xn - t_ref[...]
        l_ref[...] += 0.5 * jnp.sum(jnp.mean(e * e, axis=-1, keepdims=True), axis=0, keepdims=True)
        d_ref[...] = e * (1.0 / D)

    vec = pl.BlockSpec((1, D), lambda i: (0, 0))
    row = lambda w_: pl.BlockSpec((tm, w_), lambda i: (i, 0))
    in_specs = [row(a_.shape[1]) for a_ in a_parts] + [_full(w.shape), row(D), vec, vec]
    out_specs = [row(D), row(D)]
    out_shape = [jax.ShapeDtypeStruct((T, D), BF16), jax.ShapeDtypeStruct((T, D), F32)]
    if with_loss:
        in_specs.append(row(D))
        out_specs.append(_full((1, 1)))
        out_shape.append(jax.ShapeDtypeStruct((1, 1), F32))
    return pl.pallas_call(
        body, name=name, grid=(T // tm,), in_specs=in_specs, out_specs=out_specs, out_shape=out_shape,
        compiler_params=_cp(("arbitrary",) if with_loss else ("parallel",)),
    )(*a_parts, w, x, g, gt, *((target,) if with_loss else ()))


def post_bwd_mm(dxn, y, g, gt, w, *, tm, name):
    T, D = y.shape
    K = w.shape[0]

    def body(dxn_ref, y_ref, g_ref, gt_ref, w_ref, dy_ref, da_ref, dg_ref, dgt_ref):
        @pl.when(pl.program_id(0) == 0)
        def _():
            dg_ref[...] = jnp.zeros_like(dg_ref)
            dgt_ref[...] = jnp.zeros_like(dgt_ref)

        d = dxn_ref[...]
        n, r = _rms(y_ref[...].astype(F32))
        g_, gt_ = g_ref[...], gt_ref[...]
        dg_ref[...] += _colsum(d * gt_ * n)
        dgt_ref[...] += _colsum(d * g_ * n)
        dy = _rms_bwd(d * (gt_ * g_), n, r).astype(BF16)
        dy_ref[...] = dy
        da_ref[...] = _dot_nt(dy, w_ref[...]).astype(BF16)

    vec = pl.BlockSpec((1, D), lambda i: (0, 0))
    row = lambda w_: pl.BlockSpec((tm, w_), lambda i: (i, 0))
    return pl.pallas_call(
        body, name=name, grid=(T // tm,),
        in_specs=[row(D), row(D), vec, vec, _full((K, D))],
        out_specs=[row(D), row(K), vec, vec],
        out_shape=[jax.ShapeDtypeStruct((T, D), BF16), jax.ShapeDtypeStruct((T, K), BF16),
                   jax.ShapeDtypeStruct((1, D), F32), jax.ShapeDtypeStruct((1, D), F32)],
        compiler_params=_cp(("arbitrary",)),
    )(dxn, y, g, gt, w)


def mm_pre_bwd(dzs, wt, x, dres, g, sc, *, tm, w_row_off=0, name):
    T, N = dzs[0].shape
    D = x.shape[1]
    npart = len(dzs)
    off = w_row_off // N
    has_res = dres is not None

    def body(*refs):
        dz_refs = refs[:npart]
        w_refs = refs[npart:2 * npart]
        rest = refs[2 * npart:]
        x_ref = rest[0]
        dres_ref = rest[1] if has_res else None
        g_ref, sc_ref, dx_ref, dg_ref, dsh_ref, dsc_ref = rest[1 + has_res:]

        @pl.when(pl.program_id(0) == 0)
        def _():
            dg_ref[...] = jnp.zeros_like(dg_ref)
            dsh_ref[...] = jnp.zeros_like(dsh_ref)
            dsc_ref[...] = jnp.zeros_like(dsc_ref)

        dh = _dot(dz_refs[0][...], w_refs[0][...])
        for p in range(1, npart):
            dh = dh + _dot(dz_refs[p][...], w_refs[p][...])
        n, r = _rms(x_ref[...])
        g_, s1 = g_ref[...], 1.0 + sc_ref[...]
        dsh_ref[...] += _colsum(dh)
        dsc_ref[...] += _colsum(dh * n * g_)
        dg_ref[...] += _colsum(dh * s1 * n)
        dxp = _rms_bwd(dh * (g_ * s1), n, r)
        dx_ref[...] = dxp + dres_ref[...] if has_res else dxp

    vec = pl.BlockSpec((1, D), lambda i: (0, 0))
    row = pl.BlockSpec((tm, D), lambda i: (i, 0))
    w_specs = [pl.BlockSpec((N, D), (lambda i, p=p: (off + p, 0)), pipeline_mode=pl.Buffered(1)) for p in range(npart)]
    res_specs, res_args = ([row], (dres,)) if has_res else ([], ())
    return pl.pallas_call(
        body, name=name, grid=(T // tm,),
        in_specs=[pl.BlockSpec((tm, N), lambda i: (i, 0))] * npart + w_specs + [row] + res_specs + [vec, vec],
        out_specs=[row, vec, vec, vec],
        out_shape=[jax.ShapeDtypeStruct((T, D), F32)] + [jax.ShapeDtypeStruct((1, D), F32)] * 3,
        compiler_params=_cp(("arbitrary",)),
    )(*dzs, *([wt] * npart), x, *res_args, g, sc)


def wgrad(a_parts, b, *, tr, extra=None, name):
    T, R = a_parts[0].shape
    D = b.shape[1]
    npart = len(a_parts)
    nr = R // tr

    def body(*refs):
        a_refs, b_ref = refs[:npart], refs[npart]
        g_ref = refs[-1]
        for p in range(npart):
            @pl.when(pl.program_id(0) // nr == p)
            def _():
                acc = _dot_tn(a_refs[p][...], b_ref[...])
                if extra is not None:
                    acc += _dot_tn(refs[npart + 1][...], refs[npart + 2][...])
                g_ref[...] = acc.astype(BF16)

    in_specs = [pl.BlockSpec((T, tr), (lambda r, p=p: (0, jnp.clip(r - p * nr, 0, nr - 1)))) for p in range(npart)]
    in_specs.append(_full((T, D)))
    args = [*a_parts, b]
    if extra is not None:
        a2, b2 = extra
        in_specs += [pl.BlockSpec((a2.shape[0], tr), lambda r: (0, r)), _full(b2.shape)]
        args += [a2, b2]
    return pl.pallas_call(
        body, name=name, grid=(npart * nr,),
        in_specs=in_specs, out_specs=pl.BlockSpec((tr, D), lambda r: (r, 0)),
        out_shape=jax.ShapeDtypeStruct((npart * R, D), BF16),
        compiler_params=_cp(("parallel",)),
    )(*args)


def _conv_ext(ref, r0, rows, total):
    top = ref[pl.ds(pl.multiple_of(jnp.maximum(r0 - PAD, 0), PAD), PAD), :]
    mid = ref[pl.ds(r0, rows), :]
    bot = ref[pl.ds(pl.multiple_of(jnp.minimum(r0 + rows, total - PAD), PAD), PAD), :]
    top = jnp.where(r0 > 0, top, jnp.zeros_like(top))
    bot = jnp.where(r0 + rows < total, bot, jnp.zeros_like(bot))
    return jnp.concatenate([top, mid, bot], axis=0).astype(F32)


def _shift_rows(a, k):
    return pltpu.roll(a, k % a.shape[0], 0)


def _conv3(x, w, b):
    return w[0:1] * _shift_rows(x, 1) + w[1:2] * x + w[2:3] * _shift_rows(x, -1) + b


def _gate_up_specs(rows_, wblk, nb):
    return [pl.BlockSpec((rows_, wblk), lambda j: (0, j)), pl.BlockSpec((rows_, wblk), lambda j: (0, j + nb))]


def conv_fwd(hu, cw, cb, *, rows, wblk, name):
    L, N2 = hu.shape
    nb = N2 // 2 // wblk
    nchunk = L // rows

    def body(hg_ref, hu_ref, wg_ref, wu_ref, bg_ref, bu_ref, a_ref, s1_ref, s2_ref):
        def chunk(ci, carry):
            r0 = pl.multiple_of(ci * rows, rows)
            gate = _conv3(_conv_ext(hg_ref, r0, rows, L), wg_ref[...], bg_ref[...])[PAD:PAD + rows]
            up = _conv3(_conv_ext(hu_ref, r0, rows, L), wu_ref[...], bu_ref[...])[PAD:PAD + rows]
            sg = jax.nn.sigmoid(gate)
            silu = gate * sg
            at = pl.ds(r0, rows)
            a_ref[at, :] = (silu * up).astype(BF16)
            s1_ref[at, :] = silu.astype(BF16)
            s2_ref[at, :] = (up * (sg + silu * (1.0 - sg))).astype(BF16)
            return carry

        lax.fori_loop(0, nchunk, chunk, 0)

    out = pl.BlockSpec((L, wblk), lambda j: (0, j))
    return pl.pallas_call(
        body, name=name, grid=(nb,),
        in_specs=_gate_up_specs(L, wblk, nb) + _gate_up_specs(3, wblk, nb) + _gate_up_specs(1, wblk, nb),
        out_specs=[out] * 3, out_shape=[jax.ShapeDtypeStruct((L, N2 // 2), BF16)] * 3,
        compiler_params=_cp(("parallel",)),
    )(hu, hu, cw, cw, cb, cb)


def conv_bwd(da, s1, s2, hu, cw, *, rows, wblk, name):
    L, N2 = hu.shape
    F = N2 // 2
    nb = F // wblk
    nchunk = L // rows
    mid = slice(PAD, PAD + rows)

    def body(da_ref, s1_ref, s2_ref, hg_ref, hu_ref, wg_ref, wu_ref, dg_ref, du_ref, dwg_ref, dwu_ref, dbg_ref, dbu_ref):
        for ref in (dwg_ref, dwu_ref, dbg_ref, dbu_ref):
            ref[...] = jnp.zeros_like(ref)

        def half_bwd(x_ref, dh, w_ref, dx_ref, dw_ref, db_ref, r0):
            w = w_ref[...]
            nxt, prv = _shift_rows(dh, -1)[mid], _shift_rows(dh, 1)[mid]
            dhm, xm = dh[mid], x_ref[pl.ds(r0, rows), :].astype(F32)
            dx_ref[pl.ds(r0, rows), :] = (w[0:1] * nxt + w[1:2] * dhm + w[2:3] * prv).astype(BF16)
            db_ref[...] += _colsum(dhm)
            dw_ref[0:1, :] += _colsum(nxt * xm)
            dw_ref[1:2, :] += _colsum(dhm * xm)
            dw_ref[2:3, :] += _colsum(prv * xm)

        def chunk(ci, carry):
            r0 = pl.multiple_of(ci * rows, rows)
            d = _conv_ext(da_ref, r0, rows, L)
            half_bwd(hu_ref, d * _conv_ext(s1_ref, r0, rows, L), wu_ref, du_ref, dwu_ref, dbu_ref, r0)
            half_bwd(hg_ref, d * _conv_ext(s2_ref, r0, rows, L), wg_ref, dg_ref, dwg_ref, dbg_ref, r0)
            return carry

        lax.fori_loop(0, nchunk, chunk, 0)

    blk = lambda r: pl.BlockSpec((r, wblk), lambda j: (0, j))
    return pl.pallas_call(
        body, name=name, grid=(nb,),
        in_specs=[blk(L)] * 3 + _gate_up_specs(L, wblk, nb) + _gate_up_specs(3, wblk, nb),
        out_specs=[blk(L), blk(L), blk(3), blk(3), blk(1), blk(1)],
        out_shape=[jax.ShapeDtypeStruct((L, F), BF16)] * 2 + [jax.ShapeDtypeStruct((3, F), F32)] * 2
        + [jax.ShapeDtypeStruct((1, F), F32)] * 2,
        compiler_params=_cp(("parallel",)),
    )(da, s1, s2, hu, hu, cw, cw)


def _window_sums(pad_ref, w, lead):
    a = pad_ref[...]
    k = 1
    while k < w:
        a = a + _shift_rows(a, -k)
        k *= 2
    return _shift_rows(a, lead) if lead else a


def _pool_counts(L, h):
    t = lax.broadcasted_iota(jnp.int32, (L, 1), 0)
    return (jnp.minimum(t + h, L) - jnp.maximum(t - h, 0)).astype(F32)


def _pooled(u_ref, pad_ref, L, w):
    h = w // 2
    pad_ref[pl.ds(PAD, L), :] = u_ref[...]
    win = _window_sums(pad_ref, w, h)[PAD:PAD + L]
    return win / _pool_counts(L, h) - u_ref[...]


def _zero_pad_edges(pad_ref, L):
    z = jnp.zeros((PAD, LANES), F32)
    pad_ref[pl.ds(0, PAD), :] = z
    pad_ref[pl.ds(PAD + L, PAD), :] = z


def pool_fwd(u, w_pool, pool_scale, *, name):
    L = u.shape[0]

    def body(u_ref, w_ref, ps_ref, p_ref, pad_ref):
        _zero_pad_edges(pad_ref, L)
        for gi, win in enumerate(POOL_WINDOWS):
            @pl.when(pl.program_id(0) == gi)
            def _():
                pooled = _pooled(u_ref, pad_ref, L, win)
                p_ref[...] = (_dot(pooled.astype(BF16), w_ref[...].astype(BF16)) * ps_ref[...]).astype(BF16)

    return pl.pallas_call(
        body, name=name, grid=(len(POOL_WINDOWS),),
        in_specs=[pl.BlockSpec((L, LANES), lambda gi: (0, gi)), pl.BlockSpec((None, LANES, LANES), lambda gi: (gi, 0, 0)),
                  pl.BlockSpec((1, LANES), lambda gi: (0, gi))],
        out_specs=pl.BlockSpec((L, LANES), lambda gi: (0, gi)),
        out_shape=jax.ShapeDtypeStruct((L, 4 * LANES), BF16),
        scratch_shapes=[pltpu.VMEM((L + 2 * PAD, LANES), F32)],
        compiler_params=_cp(("parallel",)),
    )(u, w_pool, pool_scale)


def pool_bwd(u, dpa, w_pool, pool_scale, *, name):
    L = u.shape[0]

    def body(u_ref, dp_ref, w_ref, ps_ref, du_ref, dw_ref, dps_ref, pad_ref):
        _zero_pad_edges(pad_ref, L)
        for gi, win in enumerate(POOL_WINDOWS):
            @pl.when(pl.program_id(0) == gi)
            def _():
                h = win // 2
                wb = w_ref[...].astype(BF16)
                pooled = _pooled(u_ref, pad_ref, L, win).astype(BF16)
                dp = dp_ref[...].astype(F32)
                dps_ref[...] = _colsum(dp * _dot(pooled, wb))
                dy = (dp * ps_ref[...]).astype(BF16)
                dw_ref[...] = _dot_tn(pooled, dy)
                dpooled = _dot_nt(dy, wb)
                pad_ref[pl.ds(PAD, L), :] = dpooled / _pool_counts(L, h)
                du_ref[...] = (_window_sums(pad_ref, win, h - 1)[PAD:PAD + L] - dpooled).astype(BF16)

    return pl.pallas_call(
        body, name=name, grid=(len(POOL_WINDOWS),),
        in_specs=[pl.BlockSpec((L, LANES), lambda gi: (0, gi)), pl.BlockSpec((L, LANES), lambda gi: (0, gi)),
                  pl.BlockSpec((None, LANES, LANES), lambda gi: (gi, 0, 0)), pl.BlockSpec((1, LANES), lambda gi: (0, gi))],
        out_specs=[pl.BlockSpec((L, LANES), lambda gi: (0, gi)), pl.BlockSpec((None, LANES, LANES), lambda gi: (gi, 0, 0)),
                   pl.BlockSpec((1, LANES), lambda gi: (0, gi))],
        out_shape=[jax.ShapeDtypeStruct((L, 4 * LANES), BF16), jax.ShapeDtypeStruct((4, LANES, LANES), F32),
                   jax.ShapeDtypeStruct((1, 4 * LANES), F32)],
        scratch_shapes=[pltpu.VMEM((L + 2 * PAD, LANES), F32)],
        compiler_params=_cp(("parallel",)),
    )(u, dpa, w_pool, pool_scale)


def _attn_probs(qk, band_k, ctx_k, sink_ref, kh, mask4):
    s_loc = jnp.where(mask4, _dot_nt(qk, band_k), NEG_INF)
    s_ctx = _dot_nt(qk, ctx_k)
    sk = jnp.concatenate([jnp.full((BLK, 1), sink_ref[kh * GQA + hh], F32) for hh in range(GQA)], axis=0)
    m = jnp.maximum(jnp.maximum(jnp.max(s_loc, axis=-1, keepdims=True), jnp.max(s_ctx, axis=-1, keepdims=True)), sk)
    e_loc, e_ctx, e_s = jnp.exp(s_loc - m), jnp.exp(s_ctx - m), jnp.exp(sk - m)
    inv = 1.0 / (jnp.sum(e_loc, axis=-1, keepdims=True) + jnp.sum(e_ctx, axis=-1, keepdims=True) + e_s)
    return e_loc * inv, e_ctx * inv, e_s * inv


def _attn_block(n, L):
    start = pl.multiple_of(jnp.clip((n - 1) * BLK, 0, L - 3 * BLK), BLK)
    qpos = n * BLK + lax.broadcasted_iota(jnp.int32, (BLK, 3 * BLK), 0)
    kpos = start + lax.broadcasted_iota(jnp.int32, (BLK, 3 * BLK), 1)
    mask = jnp.abs(kpos - qpos) <= WINDOW
    return start, jnp.concatenate([mask] * GQA, axis=0)


def _stack_slabs(ref, rows=slice(None)):
    return jnp.concatenate([ref[rows, s * LANES:(s + 1) * LANES] for s in range(GQA)], axis=0)


def _kv_head_lanes(kh):
    return (lax.broadcasted_iota(jnp.int32, (1, LANES), 1) // HEAD_DIM) == kh


def permute_heads(w, inverse=False):
    lo, hi = 4 * LANES, 8 * LANES
    mid = w[lo:hi].reshape(*((GQA, N_KV_HEADS) if inverse else (N_KV_HEADS, GQA)), HEAD_DIM, w.shape[1])
    return jnp.concatenate([w[:lo], mid.swapaxes(0, 1).reshape(hi - lo, w.shape[1]), w[hi:]], axis=0)


def attn_fwd(q, kv, kvc, sink, *, qb, name):
    L = q.shape[0]
    C = kvc.shape[0]
    scale = HEAD_DIM ** -0.5

    def body(q_ref, kv_ref, kvc_ref, sink_ref, o_ref):
        kvc_ = kvc_ref[...]
        for b in range(qb):
            rows = slice(b * BLK, (b + 1) * BLK)
            start, mask4 = _attn_block(pl.program_id(0) * qb + b, L)
            band = kv_ref[pl.ds(start, 3 * BLK), :]
            qs = _stack_slabs(q_ref, rows) * scale
            o = jnp.zeros((GQA * BLK, LANES), F32)
            for kh in range(N_KV_HEADS):
                grp = _kv_head_lanes(kh)
                qk = jnp.where(grp, qs, jnp.zeros_like(qs))
                p_loc, p_ctx, _ = _attn_probs(qk, band[:, :LANES], kvc_[:, :LANES], sink_ref, kh, mask4)
                o = o + jnp.where(grp, _dot(p_loc.astype(BF16), band[:, LANES:]) + _dot(p_ctx.astype(BF16), kvc_[:, LANES:]), 0.0)
            for s in range(GQA):
                o_ref[rows, s * LANES:(s + 1) * LANES] = o[s * BLK:(s + 1) * BLK].astype(BF16)

    return pl.pallas_call(
        body, name=name, grid=(L // (qb * BLK),),
        in_specs=[pl.BlockSpec((qb * BLK, 4 * LANES), lambda n: (n, 0)), _full((L, 2 * LANES)), _full((C, 2 * LANES)),
                  pl.BlockSpec(memory_space=pltpu.SMEM)],
        out_specs=pl.BlockSpec((qb * BLK, 4 * LANES), lambda n: (n, 0)),
        out_shape=jax.ShapeDtypeStruct((L, 4 * LANES), BF16),
        compiler_params=_cp(("parallel",)),
    )(q, kv, kvc, sink)


def attn_bwd(q, kv, kvc, sink, dpa, cos, sa, sb, *, qb, name):
    L = q.shape[0]
    C = kvc.shape[0]
    nsteps = L // (qb * BLK)
    scale = HEAD_DIM ** -0.5

    def body(q_ref, kv_ref, kvc_ref, sink_ref, do_ref, c_ref, sa_ref, sb_ref, cq_ref, saq_ref, sbq_ref,
             dq_ref, dkv_ref, dkvc_ref, dsink_ref, dkv_acc, dkvc_acc):
        step = pl.program_id(0)

        @pl.when(step == 0)
        def _():
            dkv_acc[...] = jnp.zeros_like(dkv_acc)
            dkvc_acc[...] = jnp.zeros_like(dkvc_acc)
            dsink_ref[...] = jnp.zeros_like(dsink_ref)

        for b in range(qb):
            one_block(step * qb + b, slice(b * BLK, (b + 1) * BLK), q_ref, kv_ref, kvc_ref, sink_ref, do_ref, cq_ref, saq_ref, sbq_ref,
                      dq_ref, dsink_ref, dkv_acc, dkvc_acc)

        @pl.when(step == nsteps - 1)
        def _():
            dkv_ref[:, :LANES] = _rope(dkv_acc[:LANES, :].T, c_ref[...], -sa_ref[...], -sb_ref[...]).astype(BF16)
            dkv_ref[:, LANES:] = dkv_acc[LANES:, :].T.astype(BF16)
            dkvc_ref[...] = dkvc_acc[...].T.astype(BF16)

    def one_block(n, rows, q_ref, kv_ref, kvc_ref, sink_ref, do_ref, cq_ref, saq_ref, sbq_ref, dq_ref, dsink_ref, dkv_acc, dkvc_acc):
        start, mask4 = _attn_block(n, L)
        band = kv_ref[pl.ds(start, 3 * BLK), :]
        kvc_ = kvc_ref[...]
        band_k, band_v, ctx_k, ctx_v = band[:, :LANES], band[:, LANES:], kvc_[:, :LANES], kvc_[:, LANES:]
        qs = _stack_slabs(q_ref, rows) * scale
        dos = _stack_slabs(do_ref, rows)
        lane = lax.broadcasted_iota(jnp.int32, (1, LANES), 1)
        dsink = jnp.zeros((1, LANES), F32)
        dq = jnp.zeros((GQA * BLK, LANES), F32)
        dk = jnp.zeros((LANES, 3 * BLK), F32)
        dv = jnp.zeros((LANES, 3 * BLK), F32)
        dkc = jnp.zeros((LANES, C), F32)
        dvc = jnp.zeros((LANES, C), F32)
        for kh in range(N_KV_HEADS):
            grp = _kv_head_lanes(kh)
            qk = jnp.where(grp, qs, jnp.zeros_like(qs))
            dok = jnp.where(grp, dos, jnp.zeros_like(dos))
            p_loc, p_ctx, p_s = _attn_probs(qk, band_k, ctx_k, sink_ref, kh, mask4)
            dp_loc = _dot_nt(dok, band_v)
            dp_ctx = _dot_nt(dok, ctx_v)
            delta = jnp.sum(p_loc * dp_loc, axis=-1, keepdims=True) + jnp.sum(p_ctx * dp_ctx, axis=-1, keepdims=True)
            ds_loc = (p_loc * (dp_loc - delta)).astype(BF16)
            ds_ctx = (p_ctx * (dp_ctx - delta)).astype(BF16)
            dsk = p_s * delta
            for hh in range(GQA):
                dsink = dsink - jnp.where(lane == kh * GQA + hh, jnp.sum(dsk[hh * BLK:(hh + 1) * BLK], axis=0, keepdims=True), 0.0)
            dq = dq + jnp.where(grp, _dot(ds_loc, band_k) + _dot(ds_ctx, ctx_k), 0.0)
            dk = dk + _dot_tn(qk, ds_loc)
            dv = dv + _dot_tn(dok, p_loc.astype(BF16))
            dkc = dkc + _dot_tn(qk, ds_ctx)
            dvc = dvc + _dot_tn(dok, p_ctx.astype(BF16))
        dsink_ref[...] += dsink
        dkv_acc[:LANES, pl.ds(start, 3 * BLK)] += dk
        dkv_acc[LANES:, pl.ds(start, 3 * BLK)] += dv
        dkvc_acc[:LANES, :] += dkc
        dkvc_acc[LANES:, :] += dvc
        c, a, b_ = cq_ref[rows, :], -saq_ref[rows, :], -sbq_ref[rows, :]
        for s in range(GQA):
            dq_ref[rows, s * LANES:(s + 1) * LANES] = _rope(dq[s * BLK:(s + 1) * BLK] * scale, c, a, b_).astype(BF16)

    blk = lambda w: pl.BlockSpec((qb * BLK, w), lambda n: (n, 0))
    return pl.pallas_call(
        body, name=name, grid=(nsteps,),
        in_specs=[blk(4 * LANES), _full((L, 2 * LANES)), _full((C, 2 * LANES)), pl.BlockSpec(memory_space=pltpu.SMEM),
                  pl.BlockSpec((qb * BLK, 4 * LANES), lambda n: (n, 1)),
                  _full((L, LANES)), _full((L, LANES)), _full((L, LANES)), blk(LANES), blk(LANES), blk(LANES)],
        out_specs=[blk(4 * LANES), _full((L, 2 * LANES)), _full((C, 2 * LANES)), _full((1, LANES))],
        out_shape=[jax.ShapeDtypeStruct((L, 4 * LANES), BF16), jax.ShapeDtypeStruct((L, 2 * LANES), BF16),
                   jax.ShapeDtypeStruct((C, 2 * LANES), BF16), jax.ShapeDtypeStruct((1, LANES), F32)],
        scratch_shapes=[pltpu.VMEM((2 * LANES, L), F32), pltpu.VMEM((2 * LANES, C), F32)],
        compiler_params=_cp(("arbitrary",)),
    )(q, kv, kvc, sink, dpa, cos, sa, sb, cos, sa, sb)


def _gelu_parts(x):
    th = jnp.tanh(SQRT_2_OVER_PI * (x + GELU_C * x * x * x))
    return 0.5 * x * (1.0 + th), th


def _gelu_grad(x, th):
    return 0.5 * (1.0 + th) + 0.5 * x * (1.0 - th * th) * SQRT_2_OVER_PI * (1.0 + 3.0 * GELU_C * x * x)


def _layernorm(v):
    mu = jnp.mean(v, axis=-1, keepdims=True)
    vc = v - mu
    rstd = lax.rsqrt(jnp.mean(vc * vc, axis=-1, keepdims=True) + EPS)
    return vc * rstd, rstd


def sgu_fwd(z1, ln_g, ln_b, ws, bst, *, rows, name):
    L, W2 = z1.shape
    W = W2 // 2
    ng = W // LANES

    def body(z_ref, g_ref, b_ref, ws_ref, bs_ref, o_ref):
        for c in range(rows // BLK):
            at = slice(c * BLK, (c + 1) * BLK)
            z, _ = _gelu_parts(z_ref[at, :].astype(F32))
            xhat, _ = _layernorm(z[:, W:])
            vln = (xhat * g_ref[...] + b_ref[...]).astype(BF16)
            for gi in range(ng):
                cs = slice(gi * LANES, (gi + 1) * LANES)
                s = _dot(ws_ref[gi], vln[:, cs]) + bs_ref[:, gi:gi + 1]
                o_ref[at, cs] = (z[:, cs] * s).astype(BF16)

    vec = _full((1, W))
    return pl.pallas_call(
        body, name=name, grid=(L // rows,),
        in_specs=[pl.BlockSpec((rows, W2), lambda n: (n, 0)), vec, vec, _full((ng, LANES, LANES)), _full((BLK, ng))],
        out_specs=pl.BlockSpec((rows, W), lambda n: (n, 0)),
        out_shape=jax.ShapeDtypeStruct((L, W), BF16),
        compiler_params=_cp(("parallel",)),
    )(z1, ln_g, ln_b, ws, bst)


def sgu_bwd(z1, dus, ln_g, ln_b, ws, wst, bst, *, rows, name):
    L, W2 = z1.shape
    W = W2 // 2
    ng = W // LANES

    def body(z_ref, d_ref, g_ref, b_ref, ws_ref, wst_ref, bs_ref, dz_ref, dws_ref, dbs_ref, dg_ref, db_ref, dv_scr):
        @pl.when(pl.program_id(0) == 0)
        def _():
            dws_ref[...] = jnp.zeros_like(dws_ref)
            dbs_ref[...] = jnp.zeros_like(dbs_ref)
            dg_ref[...] = jnp.zeros_like(dg_ref)
            db_ref[...] = jnp.zeros_like(db_ref)

        for c in range(rows // BLK):
            at = slice(c * BLK, (c + 1) * BLK)
            zp = z_ref[at, :].astype(F32)
            z, th = _gelu_parts(zp)
            xhat, rstd = _layernorm(z[:, W:])
            vln = (xhat * g_ref[...] + b_ref[...]).astype(BF16)
            d = d_ref[at, :].astype(F32)
            lane = lax.broadcasted_iota(jnp.int32, (1, LANES), 1)
            dbs = jnp.zeros((BLK, LANES), F32)
            dgel = _gelu_grad(zp, th)
            for gi in range(ng):
                cs = slice(gi * LANES, (gi + 1) * LANES)
                s = _dot(ws_ref[gi], vln[:, cs]) + bs_ref[:, gi:gi + 1]
                dz_ref[at, cs] = (d[:, cs] * s * dgel[:, cs]).astype(BF16)
                ds = d[:, cs] * z[:, cs]
                dbs = dbs + jnp.where(lane == gi, jnp.sum(ds, axis=-1, keepdims=True), 0.0)
                dsb = ds.astype(BF16)
                dws_ref[gi] += _dot_nt(dsb, vln[:, cs])
                dv_scr[:, cs] = _dot(wst_ref[gi], dsb)
            dbs_ref[...] += dbs
            dvln = dv_scr[...]
            dg_ref[...] += _colsum(dvln * xhat)
            db_ref[...] += _colsum(dvln)
            dxh = dvln * g_ref[...]
            dv = rstd * (dxh - jnp.mean(dxh, axis=-1, keepdims=True) - xhat * jnp.mean(dxh * xhat, axis=-1, keepdims=True))
            dz_ref[at, W:] = (dv * dgel[:, W:]).astype(BF16)

    vec = _full((1, W))
    return pl.pallas_call(
        body, name=name, grid=(L // rows,),
        in_specs=[pl.BlockSpec((rows, W2), lambda n: (n, 0)), pl.BlockSpec((rows, W), lambda n: (n, 0)), vec, vec,
                  _full((ng, LANES, LANES)), _full((ng, LANES, LANES)), _full((BLK, ng))],
        out_specs=[pl.BlockSpec((rows, W2), lambda n: (n, 0)), _full((ng, LANES, LANES)), _full((BLK, LANES)), vec, vec],
        out_shape=[jax.ShapeDtypeStruct((L, W2), BF16), jax.ShapeDtypeStruct((ng, LANES, LANES), F32),
                   jax.ShapeDtypeStruct((BLK, LANES), F32), jax.ShapeDtypeStruct((1, W), F32), jax.ShapeDtypeStruct((1, W), F32)],
        scratch_shapes=[pltpu.VMEM((BLK, W), F32)],
        compiler_params=_cp(("arbitrary",)),
    )(z1, dus, ln_g, ln_b, ws, wst, bst)


def _adamw_math(w, m, v, g):
    m_ = ADAM_B1 * m + (1.0 - ADAM_B1) * g
    v_ = ADAM_B2 * v + (1.0 - ADAM_B2) * (g * g)
    return -ADAM_LR * ((m_ / BC1) / (jnp.sqrt(v_ / BC2) + ADAM_EPS) + ADAM_WD * w), m_, v_


def adamw(w, m, v, gparts, *, tr, name):
    NL, R, Wd = w.shape
    nr = R // tr

    def body(w_ref, m_ref, v_ref, *rest):
        gp_refs, (g_ref, d_ref, nm_ref, nv_ref) = rest[:NL], rest[NL:]
        for l in range(NL):
            @pl.when(pl.program_id(0) == l)
            def _():
                g = gp_refs[l][0].astype(F32)
                for s in range(1, gp_refs[l].shape[0]):
                    g = g + gp_refs[l][s].astype(F32)
                g_ref[...] = g
                d_ref[...], nm_ref[...], nv_ref[...] = _adamw_math(w_ref[...], m_ref[...], v_ref[...], g)

    row = pl.BlockSpec((None, tr, Wd), lambda l, i: (l, i, 0))
    gspecs = [pl.BlockSpec((gparts[l].shape[0], tr, Wd), (lambda l_, i, l=l: (0, jnp.clip(i + (l_ - l) * nr, 0, nr - 1), 0)))
              for l in range(NL)]
    return pl.pallas_call(
        body, name=name, grid=(NL, nr),
        in_specs=[row, row, row] + gspecs, out_specs=[row] * 4, out_shape=[jax.ShapeDtypeStruct((NL, R, Wd), F32)] * 4,
        compiler_params=_cp(("arbitrary", "arbitrary")),
    )(w, m, v, *gparts)


def small_update(gpacks, me, params, loss_row, *, name):
    n = len(params)

    def body(me_ref, gp_ref, *refs):
        ins, outs, gs_ref = refs[:3 * n], refs[3 * n:-1], refs[-1]
        gs_ref[...] = gp_ref[0].astype(F32)
        for dv in range(1, N_DEV):
            gs_ref[...] += gp_ref[dv].astype(F32)
        for p, (w, _, _, off, per_dev) in enumerate(params):
            w_ref, m_ref, v_ref = ins[3 * p:3 * p + 3]
            g_ref, d_ref, nm_ref, nv_ref = outs[4 * p:4 * p + 4]
            rows, cols = w.shape
            if cols == LANES and rows % 8 == 0 and not per_dev:
                g = gs_ref[off:off + rows, :]
                g_ref[...] = g
                d_ref[...], nm_ref[...], nv_ref[...] = _adamw_math(w_ref[...], m_ref[...], v_ref[...], g)
                continue
            chunks = -(-cols // LANES)
            base = off + me_ref[0] * per_dev if per_dev else off
            for i in range(rows):
                for j in range(chunks):
                    wd = min(LANES, cols - j * LANES)
                    at = (slice(i, i + 1), slice(j * LANES, j * LANES + wd))
                    g = gs_ref[pl.ds(base + i * chunks + j, 1), 0:wd]
                    g_ref[at] = g
                    d_ref[at], nm_ref[at], nv_ref[at] = _adamw_math(w_ref[at], m_ref[at], v_ref[at], g)
        outs[-1][...] = jnp.sum(gs_ref[loss_row:loss_row + 1, :], axis=1, keepdims=True)

    flat = [a for w, m, v, _, _ in params for a in (w, m, v)]
    out_shape = [jax.ShapeDtypeStruct(w.shape, F32) for w, _, _, _, _ in params for _ in range(4)] + [jax.ShapeDtypeStruct((1, 1), F32)]
    return pl.pallas_call(
        body, name=name, grid=(1,),
        in_specs=[pl.BlockSpec(memory_space=pltpu.SMEM), _full(gpacks.shape)] + [_full(a.shape) for a in flat],
        out_specs=[_full(o.shape) for o in out_shape], out_shape=out_shape,
        scratch_shapes=[pltpu.VMEM(gpacks.shape[1:], F32)],
        compiler_params=_cp(("arbitrary",)),
    )(me, gpacks, *flat)


def ada_fwd_mm(cs, w_ada, b_loc, *, name):
    R, D = cs.shape
    nl, _, n = w_ada.shape

    def body(c_ref, w_ref, b_ref, s_ref, m_ref):
        c = c_ref[...]
        s = c * jax.nn.sigmoid(c)
        s_ref[...] = s
        for i in range(nl):
            m_ref[i] = _dot(s.astype(BF16), w_ref[i].astype(BF16)) + b_ref[i:i + 1, :]

    return pl.pallas_call(
        body, name=name, in_specs=[_full((R, D)), _full((nl, D, n)), _full((nl, n))],
        out_specs=[_full((R, D)), _full((nl, R, n))], grid=(1,),
        out_shape=[jax.ShapeDtypeStruct((R, D), F32), jax.ShapeDtypeStruct((nl, R, n), F32)],
        compiler_params=_cp(("arbitrary",)),
    )(cs, w_ada, b_loc)


def ada_bwd_mm(s, c_ctx, dall, w_ada, *, name):
    R, D = s.shape
    nl, _, n = w_ada.shape

    def body(s_ref, cc_ref, d_ref, w_ref, gw_ref, dcc_ref):
        sb = s_ref[...].astype(BF16)
        row = lax.broadcasted_iota(jnp.int32, (R, 1), 0)
        dctx = d_ref[0, 1:2, :]
        for dv in range(1, N_DEV):
            dctx = dctx + d_ref[dv, 1:2, :]
        for i in range(nl):
            dm = jnp.zeros((R, n), F32)
            for dv in range(N_DEV):
                dm = dm + jnp.where(row == dv, d_ref[dv, 2 * i:2 * i + 1, :], 0.0)
            if i == 0:
                dm = dm + jnp.where(row == N_DEV, dctx, 0.0)
            gw_ref[i] = _dot_tn(sb, dm.astype(BF16))
        cc = cc_ref[...]
        sg = jax.nn.sigmoid(cc)
        ds = _dot_nt(jnp.broadcast_to(dctx, (8, n)).astype(BF16), w_ref[0].astype(BF16))
        dcc_ref[...] = ds * (sg * (1.0 + cc * (1.0 - sg)))

    return pl.pallas_call(
        body, name=name, grid=(1,),
        in_specs=[_full((R, D)), _full((1, D)), _full((N_DEV, 3, n)), _full((nl, D, n))],
        out_specs=[_full((nl, D, n)), _full((8, D))],
        out_shape=[jax.ShapeDtypeStruct((nl, D, n), F32), jax.ShapeDtypeStruct((8, D), F32)],
        compiler_params=_cp(("arbitrary",)),
    )(s, c_ctx, dall, w_ada)


def _place():
    x, y, c = lax.axis_index("x"), lax.axis_index("y"), lax.axis_index("c")
    return x, y, c


def _lin(p):
    return 4 * p[0] + 2 * p[1] + p[2]


def all_gather_small(xb, *, name):
    R, W = xb.shape

    def body(x_ref, out_ref, send_sems, recv_sems, local_sem):
        x, y, c = _place()
        me = _lin((x, y, c))
        mine = pltpu.make_async_copy(x_ref, out_ref.at[me], local_sem)
        mine.start()
        copies = []
        for k in range(1, N_DEV):
            peer = (x ^ (k >> 2), y ^ ((k >> 1) & 1), c ^ (k & 1))
            mk = lambda dst, k=k, peer=peer: pltpu.make_async_remote_copy(
                src_ref=x_ref, dst_ref=dst, send_sem=send_sems.at[k - 1], recv_sem=recv_sems.at[k - 1], device_id=peer, device_id_type=MESH)
            mk(out_ref.at[me]).start()
            copies.append(mk(out_ref.at[_lin(peer)]))
        for cp in copies:
            cp.wait_recv()
        for cp in copies:
            cp.wait_send()
        mine.wait()

    vm = pl.BlockSpec(memory_space=pltpu.VMEM)
    return pl.pallas_call(
        body, name=name, in_specs=[vm], out_specs=vm, out_shape=jax.ShapeDtypeStruct((N_DEV, R, W), xb.dtype),
        scratch_shapes=[pltpu.SemaphoreType.DMA((7,)), pltpu.SemaphoreType.DMA((7,)), pltpu.SemaphoreType.DMA],
        compiler_params=pltpu.CompilerParams(vmem_limit_bytes=VMEM_LIMIT),
    )(xb)


HBM_SPEC = pl.BlockSpec(memory_space=pltpu.HBM)
SEM_SPEC = pl.BlockSpec(memory_space=pltpu.SEMAPHORE)
ORDERED_EFFECT = pltpu.SideEffectType.DATAFLOW_SIDE_EFFECTING


def _exchange_copies(srcs, lands, sems, scatter):
    x, y, c = _place()
    me = _lin((x, y, c))
    for j in range(len(srcs)):
        r = lands[j].shape[0] // N_DEV
        block = lambda d, j=j, r=r: pl.ds(pl.multiple_of(d * r, 16), r)
        for k in range(1, N_DEV):
            peer = (x ^ (k >> 2), y ^ ((k >> 1) & 1), c ^ (k & 1))
            src = srcs[j].at[block(_lin(peer)), :] if scatter else srcs[j]
            mk = lambda dst, j=j, k=k, peer=peer, src=src: pltpu.make_async_remote_copy(
                src_ref=src, dst_ref=dst, send_sem=sems[2 * j].at[k - 1], recv_sem=sems[2 * j + 1].at[k - 1],
                device_id=peer, device_id_type=MESH)
            yield mk(lands[j].at[block(me), :]), mk(lands[j].at[block(_lin(peer)), :])


def exchange_start(srcs, lands, *, scatter, name):
    nw = len(srcs)

    def body(*refs):
        for start, _ in _exchange_copies(refs[:nw], refs[nw:2 * nw], refs[2 * nw:4 * nw], scatter):
            start.start()
        refs[-1][...] = jnp.zeros_like(refs[-1])

    thru = [pltpu.HBM(a.shape, a.dtype) for a in (*srcs, *lands)]
    res = pl.pallas_call(
        body, name=name, in_specs=[HBM_SPEC] * (2 * nw),
        out_specs=[SEM_SPEC] * (2 * nw) + [HBM_SPEC] * (2 * nw) + [pl.BlockSpec(memory_space=pltpu.VMEM)],
        out_shape=[pltpu.SemaphoreType.DMA((N_DEV - 1,))] * (2 * nw) + thru + [jax.ShapeDtypeStruct((8, LANES), F32)],
        input_output_aliases={i: 2 * nw + i for i in range(2 * nw)},
        compiler_params=pltpu.CompilerParams(has_side_effects=ORDERED_EFFECT),
    )(*[pltpu.with_memory_space_constraint(a, pltpu.HBM) for a in (*srcs, *lands)])
    return res[:2 * nw], res[2 * nw:3 * nw], res[3 * nw:4 * nw], res[-1]


def exchange_wait(srcs, lands, sems, after, *, scatter, name):
    nw = len(srcs)
    after = list(after) if isinstance(after, (list, tuple)) else [after]

    def body(*refs):
        for _, arrive in _exchange_copies(refs[:nw], refs[nw:2 * nw], refs[2 * nw:4 * nw], scatter):
            arrive.wait_send()
            arrive.wait_recv()

    res = pl.pallas_call(
        body, name=name, in_specs=[HBM_SPEC] * (2 * nw) + [SEM_SPEC] * (2 * nw) + [pl.BlockSpec(memory_space=pl.ANY)] * len(after),
        out_specs=[HBM_SPEC] * (2 * nw), out_shape=[pltpu.HBM(a.shape, a.dtype) for a in (*srcs, *lands)],
        input_output_aliases={i: i for i in range(2 * nw)},
        compiler_params=pltpu.CompilerParams(has_side_effects=ORDERED_EFFECT),
    )(*srcs, *lands, *sems, *after)
    return res[nw:]


def place_own(srcs, rows, me, *, scatter, name):
    nw = len(srcs)
    lands = [lax.empty((N_DEV * r, s_.shape[1]), s_.dtype) for r, s_ in zip(rows, srcs)]

    def body(me_ref, *refs):
        for j in range(nw):
            refs[2 * nw + j][...] = refs[j][...]

    mine = lambda i, me_ref: (me_ref[0], 0)
    src_at = mine if scatter else (lambda i, me_ref: (0, 0))
    blocks = [(r, s_.shape[1]) for r, s_ in zip(rows, srcs)]
    return pl.pallas_call(
        body, name=name,
        grid_spec=pltpu.PrefetchScalarGridSpec(
            num_scalar_prefetch=1, grid=(1,),
            in_specs=[pl.BlockSpec(b_, src_at) for b_ in blocks] + [pl.BlockSpec(memory_space=pl.ANY)] * nw,
            out_specs=[pl.BlockSpec(b_, mine) for b_ in blocks]),
        out_shape=[jax.ShapeDtypeStruct(l_.shape, l_.dtype) for l_ in lands],
        input_output_aliases={1 + nw + j: j for j in range(nw)},
        compiler_params=_cp(("arbitrary",)),
    )(jnp.reshape(me, (1,)).astype(jnp.int32), *srcs, *lands)


def _rope_tables(L):
    t = jnp.arange(L)
    inv = ROPE_BASE ** (-jnp.arange(ROPE_FREQS, dtype=F32) / ROPE_FREQS)
    ar = (t // GRID_W).astype(F32)[:, None] * inv
    ac = (t % GRID_W).astype(F32)[:, None] * inv
    z = jnp.zeros_like(ar)
    cos = jnp.concatenate([jnp.cos(ar), jnp.cos(ar), jnp.cos(ac), jnp.cos(ac)], axis=1)
    sa = jnp.concatenate([-jnp.sin(ar), z, -jnp.sin(ac), z], axis=1)
    sb = jnp.concatenate([z, jnp.sin(ar), z, jnp.sin(ac)], axis=1)
    return tuple(jnp.tile(a, (1, LANES // HEAD_DIM)) for a in (cos, sa, sb))


def _nat2d(a):
    return a.reshape(1, -1) if a.ndim == 1 else a.reshape(-1, a.shape[-1])


def _pack_rows(a):
    rows, cols = a.shape
    chunks = -(-cols // LANES)
    f = jnp.pad(a, ((0, 0), (0, chunks * LANES - cols))).reshape(rows * chunks, LANES)
    return jnp.pad(f, ((0, -f.shape[0] % 8), (0, 0)))


def _rows128(a):
    f = a.reshape(-1)
    n = -(-f.shape[0] // (8 * LANES)) * 8 * LANES
    return jnp.pad(f, (0, n - f.shape[0])).reshape(-1, LANES)


def kernel(x, c, ctx, c_ctx, w_ada, b_ada, g_mix_pre, g_mix_post, g_ffn_pre, g_ffn_post, w_in_even, w_pool, pool_scale, attn_sink, w_out_even, w_in_odd, sgu_ln_g, sgu_ln_b, sgu_w, sgu_b, w_out_odd, w_ffn_up, ffn_conv_w, ffn_conv_b, w_ffn_down, loss_target, m_c_ctx, m_w_ada, m_b_ada, m_g_mix_pre, m_g_mix_post, m_g_ffn_pre, m_g_ffn_post, m_w_in_even, m_w_pool, m_pool_scale, m_attn_sink, m_w_out_even, m_w_in_odd, m_sgu_ln_g, m_sgu_ln_b, m_sgu_w, m_sgu_b, m_w_out_odd, m_w_ffn_up, m_ffn_conv_w, m_ffn_conv_b, m_w_ffn_down, v_c_ctx, v_w_ada, v_b_ada, v_g_mix_pre, v_g_mix_post, v_g_ffn_pre, v_g_ffn_post, v_w_in_even, v_w_pool, v_pool_scale, v_attn_sink, v_w_out_even, v_w_in_odd, v_sgu_ln_g, v_sgu_ln_b, v_sgu_w, v_sgu_b, v_w_out_odd, v_w_ffn_up, v_ffn_conv_w, v_ffn_conv_b, v_w_ffn_down):
    P = dict(c_ctx=c_ctx, w_ada=w_ada, b_ada=b_ada, g_mix_pre=g_mix_pre, g_mix_post=g_mix_post, g_ffn_pre=g_ffn_pre,
             g_ffn_post=g_ffn_post, w_in_even=w_in_even, w_pool=w_pool, pool_scale=pool_scale, attn_sink=attn_sink,
             w_out_even=w_out_even, w_in_odd=w_in_odd, sgu_ln_g=sgu_ln_g, sgu_ln_b=sgu_ln_b, sgu_w=sgu_w, sgu_b=sgu_b,
             w_out_odd=w_out_odd, w_ffn_up=w_ffn_up, ffn_conv_w=ffn_conv_w, ffn_conv_b=ffn_conv_b, w_ffn_down=w_ffn_down)
    M = dict(c_ctx=m_c_ctx, w_ada=m_w_ada, b_ada=m_b_ada, g_mix_pre=m_g_mix_pre, g_mix_post=m_g_mix_post, g_ffn_pre=m_g_ffn_pre,
             g_ffn_post=m_g_ffn_post, w_in_even=m_w_in_even, w_pool=m_w_pool, pool_scale=m_pool_scale, attn_sink=m_attn_sink,
             w_out_even=m_w_out_even, w_in_odd=m_w_in_odd, sgu_ln_g=m_sgu_ln_g, sgu_ln_b=m_sgu_ln_b, sgu_w=m_sgu_w, sgu_b=m_sgu_b,
             w_out_odd=m_w_out_odd, w_ffn_up=m_w_ffn_up, ffn_conv_w=m_ffn_conv_w, ffn_conv_b=m_ffn_conv_b, w_ffn_down=m_w_ffn_down)
    V = dict(c_ctx=v_c_ctx, w_ada=v_w_ada, b_ada=v_b_ada, g_mix_pre=v_g_mix_pre, g_mix_post=v_g_mix_post, g_ffn_pre=v_g_ffn_pre,
             g_ffn_post=v_g_ffn_post, w_in_even=v_w_in_even, w_pool=v_w_pool, pool_scale=v_pool_scale, attn_sink=v_attn_sink,
             w_out_even=v_w_out_even, w_in_odd=v_w_in_odd, sgu_ln_g=v_sgu_ln_g, sgu_ln_b=v_sgu_ln_b, sgu_w=v_sgu_w, sgu_b=v_sgu_b,
             w_out_odd=v_w_out_odd, w_ffn_up=v_w_ffn_up, ffn_conv_w=v_ffn_conv_w, ffn_conv_b=v_ffn_conv_b, w_ffn_down=v_w_ffn_down)

    x = x[0]
    ctx = ctx[0]
    target = loss_target[0]
    L, D = x.shape
    C = ctx.shape[0]
    tm = min(512, L)
    tm_out = min(1024, L)
    conv_rows = min(1024, L)
    me = 4 * lax.axis_index("x") + 2 * lax.axis_index("y") + lax.axis_index("c")
    n_ada = w_ada.shape[2]
    F = w_ffn_down.shape[1] * N_DEV
    half_f = F // 2

    n_cw = ffn_conv_w.shape[2]
    small = jnp.concatenate([_rows128(c), _rows128(sgu_ln_g), _rows128(sgu_ln_b), _rows128(ffn_conv_w)], axis=0)
    small_all = all_gather_small(small, name="gather_small_inputs")
    c_all = small_all[:, :8].reshape(N_DEV, D)
    ln_g = small_all[:, 8].reshape(1, D)
    ln_b = small_all[:, 16].reshape(1, D)
    conv_w = small_all[:, 24:].reshape(N_DEV, -1)[:, :2 * 3 * n_cw].reshape(N_DEV, 2, 3, n_cw)
    conv_w = conv_w.transpose(1, 2, 0, 3).reshape(2, 3, 2 * F)

    cs = jnp.concatenate([c_all, c_ctx[None, :], jnp.zeros((7, D), F32)], axis=0)
    b_loc = lax.dynamic_slice(b_ada, (0, me * n_ada), (2, n_ada))
    silu_c, mods_loc = ada_fwd_mm(cs, w_ada, b_loc, name="ada_fwd")
    mods_all = all_gather_small(mods_loc.reshape(-1, LANES), name="gather_mods")

    shards = [s.astype(BF16) for s in (w_in_even[0].T, w_out_even[0], w_ffn_up[0].T, w_ffn_down[0],
                                       w_in_odd[0].T, w_out_odd[0], w_ffn_up[1].T, w_ffn_down[1])]
    shards, mods_all = lax.optimization_barrier((shards, mods_all))
    w_sems, w_srcs, w_lands, w_tok = exchange_start(shards, place_own(shards, [s.shape[0] for s in shards], me, scatter=False, name="gather_own"),
                                              scatter=False, name="gather_start")

    def weight(j, after):
        return exchange_wait([w_srcs[j]], [w_lands[j]], w_sems[2 * j:2 * j + 2], after, scatter=False, name=f"gather_wait_{j}")[0]

    mods_all = mods_all.reshape(N_DEV, 2, 16, n_ada).transpose(1, 2, 0, 3).reshape(2, 16, 6 * D)
    mod = lambda i, row: [m_[None, :] for m_ in jnp.split(lax.dynamic_index_in_dim(mods_all[i], row, 0, False), 6)]
    sh_m, sc_m, gt_m, sh_f, sc_f, gt_f = zip(mod(0, me), mod(1, me))
    csh_m, csc_m = mod(0, N_DEV)[:2]

    row = lambda a, i: a[i][None, :]

    cos, sa, sb = _rope_tables(L)
    sink = attn_sink[0]
    bst = sgu_b[0].T
    sgu_wb, sgu_wtb = sgu_w[0].astype(BF16), sgu_w[0].swapaxes(1, 2).astype(BF16)
    wup, wdn = [None, None], [None, None]

    def ffn_fwd(i, xin):
        wup[i] = weight(2 + 4 * i, xin)
        h, hu = pre_mm(xin, row(g_ffn_pre, i), sh_f[i], sc_f[i], wup[i], tm=tm, tn=half_f, name=f"ffn_up_{i}")
        a, s1, s2 = conv_fwd(hu, conv_w[i], ffn_conv_b[i][None, :], rows=conv_rows, wblk=2 * LANES, name=f"ffn_conv_{i}")
        wdn[i] = weight(3 + 4 * i, a)
        res = mm_post([a], wdn[i], xin, row(g_ffn_post, i), gt_f[i], tm=tm, target=target if i == 1 else None, name=f"ffn_down_{i}")
        return (h, (hu, s1, s2), a, *res)

    h0 = prenorm(x, row(g_mix_pre, 0) + w_tok[0:1, 0:1], sh_m[0], sc_m[0], tm=tm_out, name="prenorm_even")
    win_e = permute_heads(weight(0, [h0, cos, sa, sb]))
    u, q, kv = inproj_even(h0, win_e, cos, sa, sb, tm=tm_out, name="in_even")
    hc, kvc = pre_mm(ctx, row(g_mix_pre, 0), csh_m, csc_m, win_e, tm=C, tn=2 * LANES, w_row_off=8 * LANES, name="in_even_ctx")
    pa = [pool_fwd(u, w_pool[0], pool_scale, name="pool_fwd"), attn_fwd(q, kv, kvc, sink, qb=2, name="attn_fwd")]
    wout_e = permute_heads(weight(1, pa[1]))
    y0, x1 = mm_post(pa, wout_e, x, row(g_mix_post, 0), gt_m[0], tm=tm_out, name="out_even")
    h1, hu0, a0, f0, x2 = ffn_fwd(0, x1)
    win_o = weight(4, x2)
    h2, z1 = pre_mm(x2, row(g_mix_pre, 1), sh_m[1], sc_m[1], win_o, tm=tm, tn=D, name="in_odd")
    us = sgu_fwd(z1, ln_g, ln_b, sgu_wb, bst, rows=tm, name="sgu_fwd")
    wout_o = weight(5, us)
    y1, x3 = mm_post([us], wout_o, x2, row(g_mix_post, 1), gt_m[1], tm=tm_out, name="out_odd")
    h3, hu1, a1, f1, dx4, loss_part = ffn_fwd(1, x3)

    g_srcs, g_lands, g_sems = [], [], []

    def scatter(grads, nm):
        own = place_own(grads, [g.shape[0] // N_DEV for g in grads], me, scatter=True, name=nm.replace("start", "own"))
        sems, srcs, lands, tok = exchange_start(grads, own, scatter=True, name=nm)
        g_srcs.extend(srcs)
        g_lands.extend(lands)
        g_sems.extend(sems)
        return tok[0:1, 0:1]

    def ffn_bwd(i, dxo, xin, h, hu, a, f, g_post):
        dyf, da, dg_post, dgt = post_bwd_mm(dxo, f, g_post, gt_f[i], wdn[i], tm=tm, name=f"ffn_down_bwd_{i}")
        dhg, dhu, dcwg, dcwu, dcbg, dcbu = conv_bwd(da, hu[1], hu[2], hu[0], conv_w[i], rows=conv_rows, wblk=2 * LANES,
                                                    name=f"ffn_conv_bwd_{i}")
        dxin, dg_pre, dsh, dsc = mm_pre_bwd([dhg, dhu], wup[i], xin, dxo, row(g_ffn_pre, i), sc_f[i], tm=tm,
                                            name=f"ffn_up_bwd_{i}")
        g_dn = wgrad([a], dyf, tr=2 * LANES, name=f"wgrad_down_{i}")
        g_up = wgrad([dhg, dhu], h, tr=2 * LANES, name=f"wgrad_up_{i}")
        tok = scatter([g_dn, g_up], f"scatter_start_ffn_{i}")
        return dxin, tok, dict(g_ffn_post=dg_post, g_ffn_pre=dg_pre, gt_f=dgt, sh_f=dsh, sc_f=dsc,
                               ffn_conv_w=jnp.concatenate([dcwg, dcwu], axis=1), ffn_conv_b=jnp.concatenate([dcbg, dcbu], axis=1)[0])

    dx3, tok, sf1 = ffn_bwd(1, dx4, x3, h3, hu1, a1, f1, row(g_ffn_post, 1))
    dy1, dus, dg_mpost1, dgt_m1 = post_bwd_mm(dx3, y1, row(g_mix_post, 1) + tok, gt_m[1], wout_o, tm=tm_out, name="out_odd_bwd")
    dz1, dws, dbs, dlng, dlnb = sgu_bwd(z1, dus, ln_g, ln_b, sgu_wb, sgu_wtb, bst, rows=tm, name="sgu_bwd")
    dx2, dg_mpre1, dsh_m1, dsc_m1 = mm_pre_bwd([dz1], win_o, x2, dx3, row(g_mix_pre, 1), sc_m[1], tm=tm, name="in_odd_bwd")
    tok = scatter([wgrad([us], dy1, tr=4 * LANES, name="wgrad_out_odd"), wgrad([dz1], h2, tr=4 * LANES, name="wgrad_in_odd")],
                  "scatter_start_mix_1")

    dx1, tok, sf0 = ffn_bwd(0, dx2, x1, h1, hu0, a0, f0, row(g_ffn_post, 0) + tok)
    dy0, dpa, dg_mpost0, dgt_m0 = post_bwd_mm(dx1, y0, row(g_mix_post, 0) + tok, gt_m[0], wout_e, tm=tm_out, name="out_even_bwd")
    tok = scatter([permute_heads(wgrad(pa, dy0, tr=4 * LANES, name="wgrad_out_even"), inverse=True)], "scatter_start_out_0")
    du, dwp, dps = pool_bwd(u, dpa, w_pool[0], pool_scale + tok, name="pool_bwd")
    dq, dkv, dkvc, dsink = attn_bwd(q, kv, kvc, sink, dpa, cos, sa, sb, qb=4, name="attn_bwd")
    dz0 = jnp.concatenate([du, dq, dkv], axis=1)
    dzc = jnp.concatenate([jnp.zeros((C, 8 * LANES), BF16), dkvc], axis=1)
    tok = scatter([permute_heads(wgrad([dz0], h0, tr=2 * LANES, extra=(dzc, hc), name="wgrad_in_even"), inverse=True)],
                  "scatter_start_in_0")
    grad_x, dg_mpre0, dsh_m0, dsc_m0 = mm_pre_bwd([dz0], win_e, x, dx1, row(g_mix_pre, 0) + tok, sc_m[0], tm=tm,
                                                  name="in_even_bwd")
    _, dg_mpre0c, dcsh, dcsc = mm_pre_bwd([dkvc], win_e, ctx, None, row(g_mix_pre, 0), csc_m, tm=C,
                                          w_row_off=8 * LANES, name="in_even_ctx_bwd")

    out, ran = {}, {}

    def update(name, lands, transposed):
        w_, m_, v_ = (a.transpose(0, 2, 1) if transposed else a for a in (P[name], M[name], V[name]))
        r = w_.shape[1]
        tr = r // 4 if r % 64 == 0 and r > 256 else r
        res = adamw(w_, m_, v_, [l_.reshape(N_DEV, r, l_.shape[1]) for l_ in lands], tr=tr, name=f"adamw_{name}")
        ran[name] = res[0]
        for kind, val in zip(("grad", "delta", "new_m", "new_v"), res):
            out[(kind, name)] = val.transpose(0, 2, 1) if transposed else val

    zero = jnp.zeros((1, D), F32)
    dmod0 = jnp.concatenate([dsh_m0, dsc_m0, dgt_m0, sf0["sh_f"], sf0["sc_f"], sf0["gt_f"]], axis=1)
    dmodc = jnp.concatenate([dcsh, dcsc, zero, zero, zero, zero], axis=1)
    dmod1 = jnp.concatenate([dsh_m1, dsc_m1, dgt_m1, sf1["sh_f"], sf1["sc_f"], sf1["gt_f"]], axis=1)
    dmods = jnp.concatenate([dmod0, dmodc, dmod1], axis=0)
    dm = dmods.reshape(-1, LANES).astype(BF16)
    d_sems, d_srcs, d_lands, d_tok = exchange_start(
        [dm], place_own([dm], [dm.shape[0]], me, scatter=False, name="dmods_own"), scatter=False, name="dmods_start")
    slots = exchange_wait(g_srcs[:6], g_lands[:6], g_sems[:12], d_tok, scatter=True, name="scatter_wait_early")
    early = slots
    update("w_ffn_down", [slots[4], slots[0]], False)
    update("w_in_odd", [slots[3]], True)
    update("w_out_odd", [slots[2]], False)
    updated = lambda names: [ran[k] for k in names]
    dmods_all = exchange_wait(d_srcs, d_lands, d_sems, updated(("w_out_odd",)), scatter=False, name="dmods_wait")[0]
    dall = lax.dynamic_index_in_dim(dmods_all.astype(F32).reshape(N_DEV, 3, N_DEV, n_ada), me, 2, False)
    g_w_ada, dcc = ada_bwd_mm(silu_c, c_ctx[None, :], dall, w_ada, name="ada_bwd")

    rep = dict(
        c_ctx=dcc[0:1],
        b_ada=jnp.concatenate([dmod0 + dmodc, dmod1]),
        g_mix_pre=jnp.concatenate([dg_mpre0 + dg_mpre0c, dg_mpre1]),
        g_mix_post=jnp.concatenate([dg_mpost0, dg_mpost1]),
        g_ffn_pre=jnp.concatenate([sf0["g_ffn_pre"], sf1["g_ffn_pre"]]),
        g_ffn_post=jnp.concatenate([sf0["g_ffn_post"], sf1["g_ffn_post"]]),
        w_pool=_nat2d(dwp), pool_scale=dps, attn_sink=dsink[:, :N_Q_HEADS],
        sgu_w=_nat2d(dws), sgu_b=dbs[:, :sgu_b.shape[1]].T,
        ffn_conv_b=jnp.stack([sf0["ffn_conv_b"], sf1["ffn_conv_b"]]),
    )
    hi = loss_part.astype(BF16).astype(F32)
    mid = (loss_part - hi).astype(BF16).astype(F32)
    loss_piece = jnp.pad(jnp.concatenate([hi, mid, loss_part - hi - mid], axis=1), ((0, 7), (0, LANES - 3)))
    conv_g = jnp.stack([sf0["ffn_conv_w"], sf1["ffn_conv_w"]]).reshape(2 * 3, N_DEV, n_cw).swapaxes(0, 1)
    shard_full = dict(sgu_ln_g=dlng.reshape(N_DEV, LANES), sgu_ln_b=dlnb.reshape(N_DEV, LANES),
                      ffn_conv_w=jnp.concatenate([_pack_rows(conv_g[d]) for d in range(N_DEV)], axis=0))
    small_names = list(rep) + list(shard_full)
    pieces = [_pack_rows(rep[k]) for k in rep] + list(shard_full.values()) + [loss_piece]
    sizes = [p.shape[0] for p in pieces]
    offs = [sum(sizes[:i]) for i in range(len(sizes))]
    pieces.append(jnp.zeros((-sum(sizes) % 16, LANES), F32))
    gpack = jnp.concatenate(pieces, axis=0).astype(BF16)
    own = place_own([gpack], [gpack.shape[0]], me, scatter=False, name="smallgrad_own")
    s_sems, s_srcs, s_lands, small_tok = exchange_start([gpack], own, scatter=False, name="smallgrad_start")

    slots = exchange_wait(g_srcs[6:], g_lands[6:], g_sems[12:], small_tok, scatter=True, name="scatter_wait_late")
    update("w_in_even", [slots[1]], True)
    update("w_out_even", [slots[0]], False)
    update("w_ffn_up", [early[5], early[1]], True)
    res = adamw(w_ada, m_w_ada, v_w_ada, [g_w_ada[l][None] for l in range(w_ada.shape[0])], tr=D // 4, name="adamw_w_ada")
    ran["w_ada"] = res[0]
    for kind, val in zip(("grad", "delta", "new_m", "new_v"), res):
        out[(kind, "w_ada")] = val

    gpacks = exchange_wait(s_srcs, s_lands, s_sems, updated(("w_ada",)), scatter=False,
                           name="smallgrad_wait")[0]
    per_dev = {k: shard_full[k].shape[0] // N_DEV for k in shard_full}
    params = [(_nat2d(P[k]), _nat2d(M[k]), _nat2d(V[k]), offs[i], per_dev.get(k, 0)) for i, k in enumerate(small_names)]
    res = small_update(gpacks.reshape(N_DEV, -1, LANES), jnp.reshape(me, (1,)).astype(jnp.int32), params, offs[-1], name="adamw_small")
    for i, k in enumerate(small_names):
        for kind, val in zip(("grad", "delta", "new_m", "new_v"), res[4 * i:4 * i + 4]):
            out[(kind, k)] = val.reshape(P[k].shape)
    loss = res[-1][0, 0]

    names = list(P)
    final = [loss, grad_x[None]]
    for kind in ("grad", "delta", "new_m", "new_v"):
        for k in names:
            val = out[(kind, k)]
            final.append(val)
    return tuple(final)
```
